```python
import math
import jax, jax.numpy as jnp
from jax import lax
import numpy as np

D_MODEL = 1024
BATCH = 8
SEQ = 4096
DEPTH = 1

HEAD_DIM = 64
POOL_WIDTH = D_MODEL // 4
POOL_WINDOWS = (2, 4, 8, 16)
POOL_GROUPS = len(POOL_WINDOWS)
POOL_GROUP_DIM = POOL_WIDTH // POOL_GROUPS
ATTN_WIDTH = D_MODEL - POOL_WIDTH
ATTN_HEADS = ATTN_WIDTH // HEAD_DIM
DILATION_CFG = ((128, 1), (512, 4), (2048, 16))
N_DIL = len(DILATION_CFG)
HEADS_PER_DIL = ATTN_HEADS // N_DIL
ATTN_OUT_WIDTH = HEADS_PER_DIL * HEAD_DIM
IN_PROJ_WIDTH = POOL_WIDTH + 3 * ATTN_WIDTH
OUT_PROJ_WIDTH = POOL_WIDTH + ATTN_OUT_WIDTH
ROT_DIM = HEAD_DIM // 4
ROPE_THETA = 500000.0
D_FF = 11 * D_MODEL // 4
CONV_WIDTH = 3
BLOCK = 128
NORM_EPS = 1e-6
N_MOD = 6

kernel_name = "hybrid_pool_dilated_attn_convffn_block"


def rms_norm(x, g):
    x32 = x.astype(jnp.float32)
    y = x32 * lax.rsqrt(jnp.mean(x32 * x32, axis=-1, keepdims=True) + NORM_EPS)
    return (y * g.astype(jnp.float32)).astype(x.dtype)


def partial_rope(t, cos, sin):
    half = ROT_DIM // 2
    t1 = t[..., :half]
    t2 = t[..., half:ROT_DIM]
    return jnp.concatenate(
        [t1 * cos - t2 * sin, t2 * cos + t1 * sin, t[..., ROT_DIM:]], axis=-1)


def causal_multiscale_pool(u, w_pool, b_pool, pool_scale):
    B, S, _ = u.shape
    u32 = u.astype(jnp.float32).reshape(B, S, POOL_GROUPS, POOL_GROUP_DIM)
    cs = jnp.cumsum(u32, axis=1)
    t = jnp.arange(S, dtype=jnp.float32)
    means = []
    for gi, w in enumerate(POOL_WINDOWS):
        csg = cs[:, :, gi]
        lagged = jnp.pad(csg[:, :S - w], ((0, 0), (w, 0), (0, 0)))
        count = jnp.minimum(t + 1.0, float(w))
        means.append((csg - lagged) / count[None, :, None])
    mixed = jnp.stack(means, axis=2) - u32
    y = jnp.einsum('bsgc,gcd->bsgd', mixed, w_pool.astype(jnp.float32)) + b_pool.astype(jnp.float32)
    return (y.reshape(B, S, POOL_WIDTH) * pool_scale.astype(jnp.float32)).astype(u.dtype)


def dilated_window_attention(q, k, v, window, dilation):
    B, S, H, hd = q.shape
    span = window // dilation
    assert span <= BLOCK
    chunk = dilation * BLOCK
    s_pad = -(-S // chunk) * chunk
    padw = ((0, 0), (0, s_pad - S), (0, 0), (0, 0))
    q, k, v = (jnp.pad(a.astype(jnp.float32), padw) for a in (q, k, v))
    nb = s_pad // chunk
    qb, kb, vb = (a.reshape(B, nb, BLOCK, dilation, H, hd) for a in (q, k, v))

    def with_prev(a):
        prev = jnp.pad(a[:, :-1], ((0, 0), (1, 0), (0, 0), (0, 0), (0, 0), (0, 0)))
        return jnp.concatenate([prev, a], axis=2)

    kk, vv = with_prev(kb), with_prev(vb)
    scores = jnp.einsum('bnirhd,bnjrhd->bnrhij', qb, kk)
    i = jnp.arange(BLOCK)[:, None]
    j = jnp.arange(2 * BLOCK)[None, :]
    n = jnp.arange(nb)[:, None, None]
    dist = BLOCK + i - j
    valid = (dist >= 0) & (dist <= span) & (n * BLOCK + j - BLOCK >= 0)
    scores = jnp.where(valid[None, :, None, None], scores, -jnp.inf)
    m = jnp.max(scores, axis=-1, keepdims=True)
    p = jnp.exp(scores - m)
    den = jnp.sum(p, axis=-1)
    o = jnp.einsum('bnrhij,bnjrhd->bnirhd', p, vv)
    den_t = jnp.moveaxis(den, -1, 2)
    lse_t = jnp.moveaxis(m[..., 0] + jnp.log(den), -1, 2)
    o = (o / den_t[..., None]).reshape(B, s_pad, H, hd)[:, :S]
    lse = lse_t.reshape(B, s_pad, H)[:, :S]
    return o, lse


def mixing_sublayer(h, cos, sin, w_in, w_pool, b_pool, pool_scale, w_out):
    B, S, _ = h.shape
    proj = h @ w_in
    u_pool = proj[..., :POOL_WIDTH]
    qkv = proj[..., POOL_WIDTH:].reshape(B, S, 3, ATTN_HEADS, HEAD_DIM)
    q = partial_rope(qkv[:, :, 0], cos, sin) * (HEAD_DIM ** -0.5)
    k = partial_rope(qkv[:, :, 1], cos, sin)
    v = qkv[:, :, 2]
    outs, lses = [], []
    for g, (window, dilation) in enumerate(DILATION_CFG):
        sl = slice(g * HEADS_PER_DIL, (g + 1) * HEADS_PER_DIL)
        o, l = dilated_window_attention(q[:, :, sl], k[:, :, sl], v[:, :, sl], window, dilation)
        outs.append(o)
        lses.append(l)
    alpha = jax.nn.softmax(jnp.stack(lses, axis=0), axis=0)
    attn = jnp.sum(alpha[..., None] * jnp.stack(outs, axis=0), axis=0)
    attn = attn.reshape(B, S, ATTN_OUT_WIDTH).astype(h.dtype)
    pool = causal_multiscale_pool(u_pool, w_pool, b_pool, pool_scale)
    return jnp.concatenate([pool, attn], axis=-1) @ w_out


def conv_ffn(h, w_up, conv_w, conv_b, w_down):
    S = h.shape[1]
    up = h @ w_up
    gate, val = up[..., :D_FF], up[..., D_FF:]
    gp = jnp.pad(gate, ((0, 0), (CONV_WIDTH - 1, 0), (0, 0)))
    gate = gp[:, 0:S] * conv_w[0] + gp[:, 1:S + 1] * conv_w[1] + gp[:, 2:S + 2] * conv_w[2] + conv_b
    return (jax.nn.gelu(gate, approximate=True) * val) @ w_down


def _fwd_setup_inputs(seed: int = 0) -> dict:
    key = jax.random.key(seed)
    ks = jax.random.split(key, 20)
    f32 = jnp.float32
    nrm = lambda k, shape, s: jax.random.normal(k, shape, f32) * s
    L = DEPTH
    return {
        "x": nrm(ks[0], (BATCH, SEQ, D_MODEL), 1.0),
        "c": nrm(ks[1], (BATCH, D_MODEL), 1.0),
        "positions": jnp.broadcast_to(jnp.arange(SEQ, dtype=jnp.int32), (BATCH, SEQ)),
        "w_ada": nrm(ks[2], (L, D_MODEL, N_MOD * D_MODEL), D_MODEL ** -0.5),
        "b_ada": nrm(ks[3], (L, N_MOD * D_MODEL), 0.02),
        "g_pre_mix": 1.0 + nrm(ks[4], (L, D_MODEL), 0.05),
        "g_post_mix": 1.0 + nrm(ks[5], (L, D_MODEL), 0.05),
        "g_pre_ffn": 1.0 + nrm(ks[6], (L, D_MODEL), 0.05),
        "g_post_ffn": 1.0 + nrm(ks[7], (L, D_MODEL), 0.05),
        "w_in": nrm(ks[8], (L, D_MODEL, IN_PROJ_WIDTH), D_MODEL ** -0.5),
        "w_pool": nrm(ks[9], (L, POOL_GROUPS, POOL_GROUP_DIM, POOL_GROUP_DIM), POOL_GROUP_DIM ** -0.5),
        "b_pool": nrm(ks[10], (L, POOL_GROUPS, POOL_GROUP_DIM), 0.02),
        "pool_scale": 1.0 + nrm(ks[11], (L, POOL_WIDTH), 0.05),
        "w_out": nrm(ks[12], (L, OUT_PROJ_WIDTH, D_MODEL), OUT_PROJ_WIDTH ** -0.5),
        "w_up": nrm(ks[13], (L, D_MODEL, 2 * D_FF), D_MODEL ** -0.5),
        "conv_w": nrm(ks[14], (L, CONV_WIDTH, D_FF), CONV_WIDTH ** -0.5),
        "conv_b": nrm(ks[15], (L, D_FF), 0.02),
        "w_down": nrm(ks[16], (L, D_FF, D_MODEL), D_FF ** -0.5),
    }


def _fwd_reference(x, c, positions, w_ada, b_ada, g_pre_mix, g_post_mix, g_pre_ffn, g_post_ffn,
              w_in, w_pool, b_pool, pool_scale, w_out, w_up, conv_w, conv_b, w_down):
    inv_freq = ROPE_THETA ** (-jnp.arange(0, ROT_DIM, 2, dtype=jnp.float32) / ROT_DIM)
    ang = positions.astype(jnp.float32)[..., None] * inv_freq
    cos = jnp.cos(ang)[:, :, None, :].astype(x.dtype)
    sin = jnp.sin(ang)[:, :, None, :].astype(x.dtype)
    c_act = jax.nn.silu(c)
    for l in range(DEPTH):
        mod = c_act @ w_ada[l] + b_ada[l]
        sh_m, sc_m, gt_m, sh_f, sc_f, gt_f = (t[:, None, :] for t in jnp.split(mod, N_MOD, axis=-1))
        h = rms_norm(x, g_pre_mix[l]) * (1.0 + sc_m) + sh_m
        y = mixing_sublayer(h, cos, sin, w_in[l], w_pool[l], b_pool[l], pool_scale[l], w_out[l])
        x = x + gt_m * rms_norm(y, g_post_mix[l])
        h = rms_norm(x, g_pre_ffn[l]) * (1.0 + sc_f) + sh_f
        y = conv_ffn(h, w_up[l], conv_w[l], conv_b[l], w_down[l])
        x = x + gt_f * rms_norm(y, g_post_ffn[l])
    return x


import jax as _jax
import jax.numpy as _jnp

TWIN_FORMAT = 'train_step'
FWD_PARAMS = ['x', 'c', 'positions', 'w_ada', 'b_ada', 'g_pre_mix', 'g_post_mix', 'g_pre_ffn', 'g_post_ffn', 'w_in', 'w_pool', 'b_pool', 'pool_scale', 'w_out', 'w_up', 'conv_w', 'conv_b', 'w_down']
TWIN_WEIGHTS = ['w_ada', 'b_ada', 'g_pre_mix', 'g_post_mix', 'g_pre_ffn', 'g_post_ffn', 'w_in', 'w_pool', 'b_pool', 'pool_scale', 'w_out', 'w_up', 'conv_w', 'conv_b', 'w_down']
TWIN_DIFF_INPUT = 'x'
TWIN_INPUTS = ['x', 'c', 'positions', 'w_ada', 'b_ada', 'g_pre_mix', 'g_post_mix', 'g_pre_ffn', 'g_post_ffn', 'w_in', 'w_pool', 'b_pool', 'pool_scale', 'w_out', 'w_up', 'conv_w', 'conv_b', 'w_down', 'loss_target', 'm_w_ada', 'm_b_ada', 'm_g_pre_mix', 'm_g_post_mix', 'm_g_pre_ffn', 'm_g_post_ffn', 'm_w_in', 'm_w_pool', 'm_b_pool', 'm_pool_scale', 'm_w_out', 'm_w_up', 'm_conv_w', 'm_conv_b', 'm_w_down', 'v_w_ada', 'v_b_ada', 'v_g_pre_mix', 'v_g_post_mix', 'v_g_pre_ffn', 'v_g_post_ffn', 'v_w_in', 'v_w_pool', 'v_b_pool', 'v_pool_scale', 'v_w_out', 'v_w_up', 'v_conv_w', 'v_conv_b', 'v_w_down']
TWIN_OUTPUTS = ['loss', 'grad_x', 'grad_w_ada', 'grad_b_ada', 'grad_g_pre_mix', 'grad_g_post_mix', 'grad_g_pre_ffn', 'grad_g_post_ffn', 'grad_w_in', 'grad_w_pool', 'grad_b_pool', 'grad_pool_scale', 'grad_w_out', 'grad_w_up', 'grad_conv_w', 'grad_conv_b', 'grad_w_down', 'delta_w_ada', 'delta_b_ada', 'delta_g_pre_mix', 'delta_g_post_mix', 'delta_g_pre_ffn', 'delta_g_post_ffn', 'delta_w_in', 'delta_w_pool', 'delta_b_pool', 'delta_pool_scale', 'delta_w_out', 'delta_w_up', 'delta_conv_w', 'delta_conv_b', 'delta_w_down', 'new_m_w_ada', 'new_m_b_ada', 'new_m_g_pre_mix', 'new_m_g_post_mix', 'new_m_g_pre_ffn', 'new_m_g_post_ffn', 'new_m_w_in', 'new_m_w_pool', 'new_m_b_pool', 'new_m_pool_scale', 'new_m_w_out', 'new_m_w_up', 'new_m_conv_w', 'new_m_conv_b', 'new_m_w_down', 'new_v_w_ada', 'new_v_b_ada', 'new_v_g_pre_mix', 'new_v_g_post_mix', 'new_v_g_pre_ffn', 'new_v_g_post_ffn', 'new_v_w_in', 'new_v_w_pool', 'new_v_b_pool', 'new_v_pool_scale', 'new_v_w_out', 'new_v_w_up', 'new_v_conv_w', 'new_v_conv_b', 'new_v_w_down']
TWIN_LEAF_KINDS = {'loss': 'loss', 'grad_x': 'grad_x', 'grad_w_ada': 'grad_w', 'grad_b_ada': 'grad_w', 'grad_g_pre_mix': 'grad_w', 'grad_g_post_mix': 'grad_w', 'grad_g_pre_ffn': 'grad_w', 'grad_g_post_ffn': 'grad_w', 'grad_w_in': 'grad_w', 'grad_w_pool': 'grad_w', 'grad_b_pool': 'grad_w', 'grad_pool_scale': 'grad_w', 'grad_w_out': 'grad_w', 'grad_w_up': 'grad_w', 'grad_conv_w': 'grad_w', 'grad_conv_b': 'grad_w', 'grad_w_down': 'grad_w', 'delta_w_ada': 'delta_w', 'delta_b_ada': 'delta_w', 'delta_g_pre_mix': 'delta_w', 'delta_g_post_mix': 'delta_w', 'delta_g_pre_ffn': 'delta_w', 'delta_g_post_ffn': 'delta_w', 'delta_w_in': 'delta_w', 'delta_w_pool': 'delta_w', 'delta_b_pool': 'delta_w', 'delta_pool_scale': 'delta_w', 'delta_w_out': 'delta_w', 'delta_w_up': 'delta_w', 'delta_conv_w': 'delta_w', 'delta_conv_b': 'delta_w', 'delta_w_down': 'delta_w', 'new_m_w_ada': 'new_m', 'new_m_b_ada': 'new_m', 'new_m_g_pre_mix': 'new_m', 'new_m_g_post_mix': 'new_m', 'new_m_g_pre_ffn': 'new_m', 'new_m_g_post_ffn': 'new_m', 'new_m_w_in': 'new_m', 'new_m_w_pool': 'new_m', 'new_m_b_pool': 'new_m', 'new_m_pool_scale': 'new_m', 'new_m_w_out': 'new_m', 'new_m_w_up': 'new_m', 'new_m_conv_w': 'new_m', 'new_m_conv_b': 'new_m', 'new_m_w_down': 'new_m', 'new_v_w_ada': 'new_v', 'new_v_b_ada': 'new_v', 'new_v_g_pre_mix': 'new_v', 'new_v_g_post_mix': 'new_v', 'new_v_g_pre_ffn': 'new_v', 'new_v_g_post_ffn': 'new_v', 'new_v_w_in': 'new_v', 'new_v_w_pool': 'new_v', 'new_v_b_pool': 'new_v', 'new_v_pool_scale': 'new_v', 'new_v_w_out': 'new_v', 'new_v_w_up': 'new_v', 'new_v_conv_w': 'new_v', 'new_v_conv_b': 'new_v', 'new_v_w_down': 'new_v'}


def _forward(args):
    return _fwd_reference(*[args[k] for k in FWD_PARAMS])


def _output_shape():
    out = _jax.eval_shape(lambda: _forward(_fwd_setup_inputs(0)))
    return out.shape, out.dtype

N_MICROBATCH = 1
ADAM_LR = 0.001
ADAM_B1 = 0.9
ADAM_B2 = 0.999
ADAM_EPS = 1e-08
ADAM_WD = 0.01
ADAM_STEP = 10
PER_EXAMPLE_BATCH_AXIS = {'x': 0, 'c': 0, 'positions': 0, 'loss_target': 0}
SHARED_INPUTS = []
_WEIGHT_DTYPES = {'w_ada': _jnp.float32, 'b_ada': _jnp.float32, 'g_pre_mix': _jnp.float32, 'g_post_mix': _jnp.float32, 'g_pre_ffn': _jnp.float32, 'g_post_ffn': _jnp.float32, 'w_in': _jnp.float32, 'w_pool': _jnp.float32, 'b_pool': _jnp.float32, 'pool_scale': _jnp.float32, 'w_out': _jnp.float32, 'w_up': _jnp.float32, 'conv_w': _jnp.float32, 'conv_b': _jnp.float32, 'w_down': _jnp.float32}
MOMENT_SCALE = {'w_ada': 3.648519e+00, 'b_ada': 6.821166e+00, 'g_pre_mix': 4.125173e-01, 'g_post_mix': 1.548224e+01, 'g_pre_ffn': 3.989306e-01, 'g_post_ffn': 1.511742e+01, 'w_in': 6.192129e-01, 'w_pool': 1.007210e+00, 'b_pool': 5.552886e+00, 'pool_scale': 1.751414e+00, 'w_out': 1.208777e+00, 'w_up': 4.261542e-01, 'conv_w': 4.447154e-01, 'conv_b': 5.217497e-01, 'w_down': 8.913790e-01}


def _to_microbatches(a, axis):
    t = _jnp.moveaxis(a, axis, 0)
    t = t.reshape((N_MICROBATCH, t.shape[0] // N_MICROBATCH) + t.shape[1:])
    return _jnp.moveaxis(t, 1, axis + 1)


def setup_inputs(seed: int = 0) -> dict:
    inp = _fwd_setup_inputs(seed)
    key = _jax.random.fold_in(_jax.random.key(seed), 7919)
    shape, _ = _output_shape()
    out = dict(inp)
    out["loss_target"] = _jax.random.normal(_jax.random.fold_in(key, 0), shape, _jnp.float32)
    for i, name in enumerate(TWIN_WEIGHTS):
        w = inp[name].astype(_jnp.float32)
        if MOMENT_SCALE is None:
            s = _jnp.sqrt(_jnp.mean(_jnp.square(w)) + 1e-30)
        else:
            s = MOMENT_SCALE[name]
        km, kv = _jax.random.split(_jax.random.fold_in(key, i + 1))
        out[name] = w
        out["m_" + name] = s * _jax.random.normal(km, w.shape, _jnp.float32)
        out["v_" + name] = (s * s) * _jax.random.uniform(kv, w.shape, _jnp.float32, 0.5, 1.5)
    if N_MICROBATCH > 1:
        for name, axis in PER_EXAMPLE_BATCH_AXIS.items():
            out[name] = _to_microbatches(out[name], axis)
    return {'x': out['x'], 'c': out['c'], 'positions': out['positions'], 'w_ada': out['w_ada'], 'b_ada': out['b_ada'], 'g_pre_mix': out['g_pre_mix'], 'g_post_mix': out['g_post_mix'], 'g_pre_ffn': out['g_pre_ffn'], 'g_post_ffn': out['g_post_ffn'], 'w_in': out['w_in'], 'w_pool': out['w_pool'], 'b_pool': out['b_pool'], 'pool_scale': out['pool_scale'], 'w_out': out['w_out'], 'w_up': out['w_up'], 'conv_w': out['conv_w'], 'conv_b': out['conv_b'], 'w_down': out['w_down'], 'loss_target': out['loss_target'], 'm_w_ada': out['m_w_ada'], 'm_b_ada': out['m_b_ada'], 'm_g_pre_mix': out['m_g_pre_mix'], 'm_g_post_mix': out['m_g_post_mix'], 'm_g_pre_ffn': out['m_g_pre_ffn'], 'm_g_post_ffn': out['m_g_post_ffn'], 'm_w_in': out['m_w_in'], 'm_w_pool': out['m_w_pool'], 'm_b_pool': out['m_b_pool'], 'm_pool_scale': out['m_pool_scale'], 'm_w_out': out['m_w_out'], 'm_w_up': out['m_w_up'], 'm_conv_w': out['m_conv_w'], 'm_conv_b': out['m_conv_b'], 'm_w_down': out['m_w_down'], 'v_w_ada': out['v_w_ada'], 'v_b_ada': out['v_b_ada'], 'v_g_pre_mix': out['v_g_pre_mix'], 'v_g_post_mix': out['v_g_post_mix'], 'v_g_pre_ffn': out['v_g_pre_ffn'], 'v_g_post_ffn': out['v_g_post_ffn'], 'v_w_in': out['v_w_in'], 'v_w_pool': out['v_w_pool'], 'v_b_pool': out['v_b_pool'], 'v_pool_scale': out['v_pool_scale'], 'v_w_out': out['v_w_out'], 'v_w_up': out['v_w_up'], 'v_conv_w': out['v_conv_w'], 'v_conv_b': out['v_conv_b'], 'v_w_down': out['v_w_down']}


def _loss(weights, diff, rest, loss_target):
    with _jax.named_scope("forward"):
        args = {**rest, TWIN_DIFF_INPUT: diff, **{k: w.astype(_WEIGHT_DTYPES[k]) for k, w in weights.items()}}
        y = _forward(args)
    with _jax.named_scope("loss_head"):
        err = _jnp.square(y.astype(_jnp.float32) - loss_target)
        return 0.5 * _jnp.sum(_jnp.mean(err, axis=-1)) if err.ndim else 0.5 * err


def _adamw(w, g, m, v):
    m = ADAM_B1 * m + (1.0 - ADAM_B1) * g
    v = ADAM_B2 * v + (1.0 - ADAM_B2) * _jnp.square(g)
    m_hat = m / (1.0 - ADAM_B1 ** ADAM_STEP)
    v_hat = v / (1.0 - ADAM_B2 ** ADAM_STEP)
    delta = -ADAM_LR * (m_hat / (_jnp.sqrt(v_hat) + ADAM_EPS) + ADAM_WD * w)
    return delta, m, v


def reference(x, c, positions, w_ada, b_ada, g_pre_mix, g_post_mix, g_pre_ffn, g_post_ffn, w_in, w_pool, b_pool, pool_scale, w_out, w_up, conv_w, conv_b, w_down, loss_target, m_w_ada, m_b_ada, m_g_pre_mix, m_g_post_mix, m_g_pre_ffn, m_g_post_ffn, m_w_in, m_w_pool, m_b_pool, m_pool_scale, m_w_out, m_w_up, m_conv_w, m_conv_b, m_w_down, v_w_ada, v_b_ada, v_g_pre_mix, v_g_post_mix, v_g_pre_ffn, v_g_post_ffn, v_w_in, v_w_pool, v_b_pool, v_pool_scale, v_w_out, v_w_up, v_conv_w, v_conv_b, v_w_down):
    given = dict(x=x, c=c, positions=positions, w_ada=w_ada, b_ada=b_ada, g_pre_mix=g_pre_mix, g_post_mix=g_post_mix, g_pre_ffn=g_pre_ffn, g_post_ffn=g_post_ffn, w_in=w_in, w_pool=w_pool, b_pool=b_pool, pool_scale=pool_scale, w_out=w_out, w_up=w_up, conv_w=conv_w, conv_b=conv_b, w_down=w_down, loss_target=loss_target, m_w_ada=m_w_ada, m_b_ada=m_b_ada, m_g_pre_mix=m_g_pre_mix, m_g_post_mix=m_g_post_mix, m_g_pre_ffn=m_g_pre_ffn, m_g_post_ffn=m_g_post_ffn, m_w_in=m_w_in, m_w_pool=m_w_pool, m_b_pool=m_b_pool, m_pool_scale=m_pool_scale, m_w_out=m_w_out, m_w_up=m_w_up, m_conv_w=m_conv_w, m_conv_b=m_conv_b, m_w_down=m_w_down, v_w_ada=v_w_ada, v_b_ada=v_b_ada, v_g_pre_mix=v_g_pre_mix, v_g_post_mix=v_g_post_mix, v_g_pre_ffn=v_g_pre_ffn, v_g_post_ffn=v_g_post_ffn, v_w_in=v_w_in, v_w_pool=v_w_pool, v_b_pool=v_b_pool, v_pool_scale=v_pool_scale, v_w_out=v_w_out, v_w_up=v_w_up, v_conv_w=v_conv_w, v_conv_b=v_conv_b, v_w_down=v_w_down)
    weights = {n: given[n] for n in TWIN_WEIGHTS}
    shared = {n: given[n] for n in SHARED_INPUTS}
    per_example = {n: given[n] for n in ['x', 'c', 'positions']}
    grad_fn = _jax.value_and_grad(_loss, argnums=(0, 1))

    def one_microbatch(ex, loss_target):
        ex = dict(ex)
        diff = ex.pop(TWIN_DIFF_INPUT)
        return grad_fn(weights, diff, {**shared, **ex}, loss_target)

    if N_MICROBATCH == 1:
        loss, (grad_w, grad_x) = one_microbatch(per_example, given["loss_target"])
    else:
        def body(carry, xs):
            loss_sum, grad_sum = carry
            l_k, (gw_k, gx_k) = one_microbatch(xs[0], xs[1])
            with _jax.named_scope("update"):
                return (loss_sum + l_k, _jax.tree.map(_jnp.add, grad_sum, gw_k)), gx_k

        init = (_jnp.zeros((), _jnp.float32), _jax.tree.map(_jnp.zeros_like, weights))
        (loss, grad_w), grad_x = _jax.lax.scan(body, init, (per_example, given["loss_target"]))
    with _jax.named_scope("update"):
        delta_w, new_m, new_v = {}, {}, {}
        for n in TWIN_WEIGHTS:
            delta_w[n], new_m[n], new_v[n] = _adamw(weights[n], grad_w[n], given["m_" + n], given["v_" + n])
    return (loss, grad_x, *[grad_w[n] for n in TWIN_WEIGHTS], *[delta_w[n] for n in TWIN_WEIGHTS],
            *[new_m[n] for n in TWIN_WEIGHTS], *[new_v[n] for n in TWIN_WEIGHTS])
```

```python
import functools
import math

import jax
import jax.numpy as jnp
from jax import lax
from jax.experimental import pallas as pl
from jax.experimental.pallas import tpu as pltpu

F32 = jnp.float32
BF16 = jnp.bfloat16
MESH = pl.DeviceIdType.MESH

D_MODEL = 1024
HEAD_DIM = 64
POOL_W = 256
GROUP_W = 256
DILATIONS = (1, 4, 16)
ATT_BLOCK = 128
IN_W = 2560
D_FF = 2816
HALF_FF = 1408
ROT_DIM = 16
ROPE_THETA = 500000.0
NORM_EPS = 1e-6
N_CHIPS = 4
N_DEV = 8
NEG = -1e30

ADAM_LR = 0.001
ADAM_B1 = 0.9
ADAM_B2 = 0.999
ADAM_EPS = 1e-08
ADAM_WD = 0.01
ADAM_STEP = 10

VMEM_LIMIT = 56 * 1024 * 1024

NT = (((1,), (1,)), ((), ()))
TN = (((0,), (0,)), ((), ()))


def _params(n_grid=0, **kw):
    sem = ("arbitrary",) * n_grid if n_grid else None
    return pltpu.CompilerParams(dimension_semantics=sem, vmem_limit_bytes=VMEM_LIMIT, **kw)


def _full(shape):
    nd = len(shape)
    return pl.BlockSpec(tuple(shape), lambda *_: (0,) * nd)


def _rows(tm, ncol):
    return pl.BlockSpec((tm, ncol), lambda i: (i, 0))


def _acc(ref, val):
    @pl.when(pl.program_id(0) == 0)
    def _():
        ref[...] = jnp.zeros_like(ref)

    ref[...] += val


def _colsum(v):
    return jnp.sum(v, axis=0, keepdims=True)


def _rope128(t, cs, sa, sb, sign):
    return t * cs + sign * (pltpu.roll(t, 8, 1) * sa + pltpu.roll(t, 120, 1) * sb)


def _gelu(z):
    c = math.sqrt(2.0 / math.pi)
    t = jnp.tanh(c * (z + 0.044715 * (z * z * z)))
    return 0.5 * z * (1.0 + t), t


def _gelu_grad(z, t):
    c = math.sqrt(2.0 / math.pi)
    return 0.5 * (1.0 + t) + 0.5 * z * (1.0 - t * t) * (c * (1.0 + 3.0 * 0.044715 * (z * z)))


def _conv_taps(gate, halo, first):
    row = lax.broadcasted_iota(jnp.int32, gate.shape, 0)
    halo = jnp.where(first, 0.0, halo)
    p1 = halo[15:16, :]
    p2 = halo[14:15, :]
    g1 = jnp.where(row == 0, p1, pltpu.roll(gate, 1, 0))
    g2 = jnp.where(row == 0, p2, jnp.where(row == 1, p1, pltpu.roll(gate, 2, 0)))
    return g1, g2


def inproj_fwd(x, g, mod6, w_in_g, tc, tsa, tsb, tm=512):
    S = x.shape[0]

    def body(x_ref, g_ref, mod_ref, w_ref, tc_ref, tsa_ref, tsb_ref, h_ref, u_ref, qkv_ref):
        xv = x_ref[...]
        rstd = lax.rsqrt(jnp.mean(xv * xv, axis=-1, keepdims=True) + NORM_EPS)
        h = ((xv * rstd) * g_ref[...]) * (1.0 + mod_ref[1:2, :]) + mod_ref[0:1, :]
        hb = h.astype(BF16)
        h_ref[...] = hb
        cs, sa, sb = tc_ref[...], tsa_ref[...], tsb_ref[...]
        for j in range(N_CHIPS):
            res = jnp.dot(hb, w_ref[j], preferred_element_type=F32)
            for t in range(5):
                sp = 5 * j + t
                piece, half = sp // 2, sp % 2
                blk = res[:, t * 128:(t + 1) * 128]
                lanes = slice(half * 128, (half + 1) * 128)
                if piece == 0:
                    u_ref[:, lanes] = blk
                else:
                    i9 = piece - 1
                    if i9 < 3:
                        blk = _rope128(blk, cs, sa, sb, 1.0) * (HEAD_DIM ** -0.5)
                    elif i9 < 6:
                        blk = _rope128(blk, cs, sa, sb, 1.0)
                    qkv_ref[i9, :, lanes] = blk.astype(BF16)

    return pl.pallas_call(
        body,
        name="inproj_fwd",
        grid=(S // tm,),
        in_specs=[_rows(tm, D_MODEL), _full((1, D_MODEL)), _full((6, D_MODEL)), _full(w_in_g.shape),
                  _rows(tm, 128), _rows(tm, 128), _rows(tm, 128)],
        out_specs=[_rows(tm, D_MODEL), _rows(tm, POOL_W), pl.BlockSpec((9, tm, GROUP_W), lambda i: (0, i, 0))],
        out_shape=[jax.ShapeDtypeStruct((S, D_MODEL), BF16), jax.ShapeDtypeStruct((S, POOL_W), F32),
                   jax.ShapeDtypeStruct((9, S, GROUP_W), BF16)],
        compiler_params=_params(1),
    )(x, g, mod6, w_in_g, tc, tsa, tsb)


def _attn_masks():
    row = lax.broadcasted_iota(jnp.int32, (ATT_BLOCK, 2 * ATT_BLOCK), 0)
    col = lax.broadcasted_iota(jnp.int32, (ATT_BLOCK, 2 * ATT_BLOCK), 1)
    band = (col >= row) & (col <= row + ATT_BLOCK)
    lane = lax.broadcasted_iota(jnp.int32, (ATT_BLOCK, 128), 1)
    return band, col, lane < HEAD_DIM


def attn_fwd(qkv9, gi, d):
    S = qkv9.shape[1]
    L = S // d
    nb = L // ATT_BLOCK
    qv = qkv9.reshape(9, L, d * GROUP_W)

    def body(q_ref, k_ref, v_ref, o_ref, l_ref, kpad, vpad):
        kpad[0:ATT_BLOCK, :] = jnp.zeros((ATT_BLOCK, GROUP_W), BF16)
        vpad[0:ATT_BLOCK, :] = jnp.zeros((ATT_BLOCK, GROUP_W), BF16)
        kpad[ATT_BLOCK:, :] = k_ref[...]
        vpad[ATT_BLOCK:, :] = v_ref[...]
        band, col, lo = _attn_masks()

        def step(n, carry):
            r0 = pl.multiple_of(n * ATT_BLOCK, ATT_BLOCK)
            valid = band & ((col >= ATT_BLOCK) | (n > 0))
            qb = q_ref[pl.ds(r0, ATT_BLOCK), :]
            kb = kpad[pl.ds(r0, 2 * ATT_BLOCK), :]
            vb = vpad[pl.ds(r0, 2 * ATT_BLOCK), :]
            for pair in range(2):
                lanes = slice(pair * 128, (pair + 1) * 128)
                qp, kp, vp = qb[:, lanes], kb[:, lanes], vb[:, lanes]
                outs, lses = [], []
                for hh in range(2):
                    sel = lo if hh == 0 else jnp.logical_not(lo)
                    qm = jnp.where(sel, qp, jnp.zeros_like(qp))
                    s = lax.dot_general(qm, kp, NT, preferred_element_type=F32)
                    s = jnp.where(valid, s, NEG)
                    m = jnp.max(s, axis=1, keepdims=True)
                    p = jnp.exp(s - m)
                    den = jnp.sum(p, axis=1, keepdims=True)
                    pv = jnp.dot(p.astype(BF16), vp, preferred_element_type=F32)
                    outs.append(pv / den)
                    lses.append(m + jnp.log(den))
                o_ref[pl.ds(r0, ATT_BLOCK), lanes] = jnp.where(lo, outs[0], outs[1])
                l_ref[pl.ds(r0, ATT_BLOCK), lanes] = jnp.where(lo, lses[0], lses[1])
            return carry

        lax.fori_loop(0, nb, step, 0)

    spec = lambda lead: pl.BlockSpec((None, L, GROUP_W), lambda r: (lead, 0, r))
    o, l = pl.pallas_call(
        body,
        name=f"attn_fwd_d{d}",
        grid=(d,),
        in_specs=[spec(gi), spec(3 + gi), spec(6 + gi)],
        out_specs=[pl.BlockSpec((L, GROUP_W), lambda r: (0, r))] * 2,
        out_shape=[jax.ShapeDtypeStruct((L, d * GROUP_W), F32)] * 2,
        scratch_shapes=[pltpu.VMEM((L + ATT_BLOCK, GROUP_W), BF16)] * 2,
        compiler_params=_params(1),
    )(qv, qv, qv)
    return o.reshape(S, GROUP_W), l.reshape(S, GROUP_W)


def _pool_lane_windows(shape):
    lane = lax.broadcasted_iota(jnp.int32, shape, 1)
    return lane, jnp.where(lane < 64, 2, jnp.where(lane < 128, 4, jnp.where(lane < 192, 8, 16)))


def pool_fwd(u, wbd, b, scale):
    S = u.shape[0]

    def body(u_ref, w_ref, b_ref, s_ref, mixed_ref, out_ref):
        uv = u_ref[...]
        row = lax.broadcasted_iota(jnp.int32, uv.shape, 0)
        lane, win = _pool_lane_windows(uv.shape)

        def shift(a, k):
            return jnp.where(row >= k, pltpu.roll(a, k, 0), 0.0)

        s2 = uv + shift(uv, 1)
        s4 = s2 + shift(s2, 2)
        s8 = s4 + shift(s4, 4)
        s16 = s8 + shift(s8, 8)
        tsum = jnp.where(lane < 64, s2, jnp.where(lane < 128, s4, jnp.where(lane < 192, s8, s16)))
        cnt = jnp.minimum(row + 1, win).astype(F32)
        mb = (tsum / cnt - uv).astype(BF16)
        mixed_ref[...] = mb
        y = jnp.dot(mb, w_ref[...], preferred_element_type=F32) + b_ref[...]
        out_ref[...] = (y * s_ref[...]).astype(BF16)

    vm = pl.BlockSpec(memory_space=pltpu.VMEM)
    return pl.pallas_call(
        body,
        name="pool_fwd",
        in_specs=[vm] * 4,
        out_specs=[vm] * 2,
        out_shape=[jax.ShapeDtypeStruct((S, POOL_W), BF16)] * 2,
        compiler_params=_params(),
    )(u, wbd, b, scale)


def outproj_fwd(o_l, pool, x, w_out_g, g_post, g_pre, mod6, tm=512):
    S = x.shape[0]

    def body(o0, o1, o2, l0, l1, l2, pool_ref, x_ref, w_ref, gpost_ref, gpre_ref, mod_ref,
             cat_ref, lse_ref, y1_ref, x1_ref, h2_ref):
        a, b, c = l0[...], l1[...], l2[...]
        m = jnp.maximum(jnp.maximum(a, b), c)
        e0, e1, e2 = jnp.exp(a - m), jnp.exp(b - m), jnp.exp(c - m)
        z = e0 + e1 + e2
        lse_ref[...] = m + jnp.log(z)
        attn = (e0 * o0[...] + e1 * o1[...] + e2 * o2[...]) / z
        cat = jnp.concatenate([pool_ref[...], attn.astype(BF16)], axis=1)
        cat_ref[...] = cat
        y1 = jnp.concatenate([jnp.dot(cat, w_ref[j], preferred_element_type=F32) for j in range(N_CHIPS)], axis=1)
        y1_ref[...] = y1
        rstd = lax.rsqrt(jnp.mean(y1 * y1, axis=-1, keepdims=True) + NORM_EPS)
        x1 = x_ref[...] + mod_ref[2:3, :] * ((y1 * rstd) * gpost_ref[...])
        x1_ref[...] = x1
        rstd2 = lax.rsqrt(jnp.mean(x1 * x1, axis=-1, keepdims=True) + NORM_EPS)
        h2 = ((x1 * rstd2) * gpre_ref[...]) * (1.0 + mod_ref[4:5, :]) + mod_ref[3:4, :]
        h2_ref[...] = h2.astype(BF16)

    t256 = _rows(tm, GROUP_W)
    return pl.pallas_call(
        body,
        name="outproj_fwd",
        grid=(S // tm,),
        in_specs=[t256] * 7 + [_rows(tm, D_MODEL), _full(w_out_g.shape), _full((1, D_MODEL)), _full((1, D_MODEL)),
                               _full((6, D_MODEL))],
        out_specs=[_rows(tm, 512), t256, _rows(tm, D_MODEL), _rows(tm, D_MODEL), _rows(tm, D_MODEL)],
        out_shape=[jax.ShapeDtypeStruct((S, 512), BF16), jax.ShapeDtypeStruct((S, GROUP_W), F32),
                   jax.ShapeDtypeStruct((S, D_MODEL), F32), jax.ShapeDtypeStruct((S, D_MODEL), F32),
                   jax.ShapeDtypeStruct((S, D_MODEL), BF16)],
        compiler_params=_params(1),
    )(*o_l, pool, x, w_out_g, g_post, g_pre, mod6)


def up_fwd(h2, w_up_g, tm=512):
    S = h2.shape[0]

    def body(h_ref, w_ref, gate_ref, val_ref):
        hb = h_ref[...]
        for j in range(N_CHIPS):
            res = jnp.dot(hb, w_ref[j], preferred_element_type=F32).astype(BF16)
            dst = gate_ref if j < 2 else val_ref
            dst[:, (j % 2) * HALF_FF:(j % 2 + 1) * HALF_FF] = res

    return pl.pallas_call(
        body,
        name="up_fwd",
        grid=(S // tm,),
        in_specs=[_rows(tm, D_MODEL), _full(w_up_g.shape)],
        out_specs=[_rows(tm, D_FF)] * 2,
        out_shape=[jax.ShapeDtypeStruct((S, D_FF), BF16)] * 2,
        compiler_params=_params(1),
    )(h2, w_up_g)


def _halo_prev(tm, ncol):
    return pl.BlockSpec((16, ncol), lambda i: (jnp.maximum(i * (tm // 16) - 1, 0), 0))


def down_fwd(gate, val, conv_w, conv_b, w_down, x1, target, g_post, mod6, tm=256):
    S = x1.shape[0]

    def body(gate_ref, halo_ref, val_ref, cw_ref, cb_ref, w_ref, x1_ref, tgt_ref, g_ref, mod_ref,
             a_ref, dy2_ref, dout_ref, loss_ref, dgt_ref, dg_ref):
        first = pl.program_id(0) == 0
        y2 = jnp.zeros((tm, D_MODEL), F32)
        for ch in range(2):
            cols = slice(ch * HALF_FF, (ch + 1) * HALF_FF)
            gt = gate_ref[:, cols].astype(F32)
            g1, g2 = _conv_taps(gt, halo_ref[:, cols].astype(F32), first)
            gc = g2 * cw_ref[0:1, cols] + g1 * cw_ref[1:2, cols] + gt * cw_ref[2:3, cols] + cb_ref[:, cols]
            ge, _ = _gelu(gc)
            ab = (ge * val_ref[:, cols].astype(F32)).astype(BF16)
            a_ref[:, cols] = ab
            y2 = y2 + jnp.dot(ab, w_ref[cols, :], preferred_element_type=F32)
        rstd = lax.rsqrt(jnp.mean(y2 * y2, axis=-1, keepdims=True) + NORM_EPS)
        y2n = y2 * rstd
        gv = g_ref[...]
        gtf = mod_ref[5:6, :]
        r2 = y2n * gv
        diff = (x1_ref[...] + gtf * r2) - tgt_ref[...]
        _acc(loss_ref, jnp.zeros((1, 128), F32) + 0.5 * jnp.sum(diff * diff) * (1.0 / D_MODEL))
        dout = diff * (1.0 / D_MODEL)
        dout_ref[...] = dout
        _acc(dgt_ref, _colsum(dout * r2))
        dr2 = dout * gtf
        _acc(dg_ref, _colsum(dr2 * y2n))
        dyn = dr2 * gv
        dy2 = rstd * (dyn - y2n * jnp.mean(dyn * y2n, axis=-1, keepdims=True))
        dy2_ref[...] = dy2.astype(BF16)

    vec = _full((1, D_MODEL))
    return pl.pallas_call(
        body,
        name="down_fwd",
        grid=(S // tm,),
        in_specs=[_rows(tm, D_FF), _halo_prev(tm, D_FF), _rows(tm, D_FF), _full((3, D_FF)), _full((1, D_FF)),
                  _full((D_FF, D_MODEL)), _rows(tm, D_MODEL), _rows(tm, D_MODEL), vec, _full((6, D_MODEL))],
        out_specs=[_rows(tm, D_FF), _rows(tm, D_MODEL), _rows(tm, D_MODEL), _full((1, 128)), vec, vec],
        out_shape=[jax.ShapeDtypeStruct((S, D_FF), BF16), jax.ShapeDtypeStruct((S, D_MODEL), BF16),
                   jax.ShapeDtypeStruct((S, D_MODEL), F32), jax.ShapeDtypeStruct((1, 128), F32),
                   jax.ShapeDtypeStruct((1, D_MODEL), F32), jax.ShapeDtypeStruct((1, D_MODEL), F32)],
        compiler_params=_params(1),
    )(gate, gate, val, conv_w, conv_b, w_down, x1, target, g_post, mod6)


def down_bwd(dy2, w_down, gate, val, conv_w, conv_b, tm=256):
    S = dy2.shape[0]

    def body(dy_ref, w_ref, gate_ref, halo_ref, val_ref, cw_ref, cb_ref, dgc_ref, dval_ref, dcw_ref, dcb_ref):
        first = pl.program_id(0) == 0

        @pl.when(first)
        def _():
            dcw_ref[...] = jnp.zeros_like(dcw_ref)
            dcb_ref[...] = jnp.zeros_like(dcb_ref)

        dyb = dy_ref[...]
        for ch in range(2):
            cols = slice(ch * HALF_FF, (ch + 1) * HALF_FF)
            da = lax.dot_general(dyb, w_ref[cols, :], NT, preferred_element_type=F32)
            gt = gate_ref[:, cols].astype(F32)
            g1, g2 = _conv_taps(gt, halo_ref[:, cols].astype(F32), first)
            gc = g2 * cw_ref[0:1, cols] + g1 * cw_ref[1:2, cols] + gt * cw_ref[2:3, cols] + cb_ref[:, cols]
            ge, th = _gelu(gc)
            dgc = da * val_ref[:, cols].astype(F32) * _gelu_grad(gc, th)
            dgc_ref[:, cols] = dgc.astype(BF16)
            dval_ref[:, cols] = (da * ge).astype(BF16)
            dcb_ref[:, cols] += _colsum(dgc)
            dcw_ref[0:1, cols] += _colsum(dgc * g2)
            dcw_ref[1:2, cols] += _colsum(dgc * g1)
            dcw_ref[2:3, cols] += _colsum(dgc * gt)

    return pl.pallas_call(
        body,
        name="down_bwd",
        grid=(S // tm,),
        in_specs=[_rows(tm, D_MODEL), _full((D_FF, D_MODEL)), _rows(tm, D_FF), _halo_prev(tm, D_FF), _rows(tm, D_FF),
                  _full((3, D_FF)), _full((1, D_FF))],
        out_specs=[_rows(tm, D_FF), _rows(tm, D_FF), _full((3, D_FF)), _full((1, D_FF))],
        out_shape=[jax.ShapeDtypeStruct((S, D_FF), BF16), jax.ShapeDtypeStruct((S, D_FF), BF16),
                   jax.ShapeDtypeStruct((3, D_FF), F32), jax.ShapeDtypeStruct((1, D_FF), F32)],
        compiler_params=_params(1),
    )(dy2, w_down, gate, gate, val, conv_w, conv_b)


def dw_matmul(a, b, out_blocks, blk_shape, a_cols, b_cols, a_blocked, name, prev=None, blk_off=0, n_blk=None, tm=512):
    S = a.shape[0]
    n_blk = out_blocks if n_blk is None else n_blk

    def body(*refs):
        a_ref, b_ref, o_ref = refs[0], refs[1], refs[-1]

        @pl.when(pl.program_id(1) == 0)
        def _():
            o_ref[...] = jnp.zeros_like(o_ref)

        o_ref[...] += lax.dot_general(a_ref[...], b_ref[...], TN, preferred_element_type=F32)

    a_spec = pl.BlockSpec((tm, a_cols), (lambda j, i: (i, j)) if a_blocked else (lambda j, i: (i, 0)))
    b_spec = pl.BlockSpec((tm, b_cols), (lambda j, i: (i, 0)) if a_blocked else (lambda j, i: (i, j)))
    in_specs = [a_spec, b_spec]
    args = [a, b]
    aliases = {}
    if prev is not None:
        in_specs.append(pl.BlockSpec(memory_space=pl.ANY))
        args.append(prev)
        aliases = {2: 0}
    return pl.pallas_call(
        body,
        name=name,
        grid=(n_blk, S // tm),
        in_specs=in_specs,
        out_specs=pl.BlockSpec((None,) + tuple(blk_shape), lambda j, i: (j + blk_off, 0, 0)),
        out_shape=jax.ShapeDtypeStruct((out_blocks,) + tuple(blk_shape), F32),
        input_output_aliases=aliases,
        compiler_params=_params(2),
    )(*args)


def up_bwd(dgc, dval, conv_w, w_up_g, x1, dout, y1, g_pre, g_post, mod6, tm=256):
    S = x1.shape[0]
    last_blk = S // 16 - 1

    def body(dgc_ref, nxt_ref, dval_ref, cw_ref, w_ref, x1_ref, dout_ref, y1_ref, gpre_ref, gpost_ref, mod_ref,
             dgate_ref, dx1_ref, dy1_ref, dsh_ref, dsc_ref, dgpre_ref, dgt_ref, dgpost_ref):
        last = pl.program_id(0) == pl.num_programs(0) - 1
        dh = jnp.zeros((tm, D_MODEL), F32)
        for ch in range(2):
            cols = slice(ch * HALF_FF, (ch + 1) * HALF_FF)
            dg = dgc_ref[:, cols].astype(F32)
            nx = jnp.where(last, 0.0, nxt_ref[:, cols].astype(F32))
            row = lax.broadcasted_iota(jnp.int32, dg.shape, 0)
            n0, n1 = nx[0:1, :], nx[1:2, :]
            u1 = jnp.where(row == tm - 1, n0, pltpu.roll(dg, tm - 1, 0))
            u2 = jnp.where(row == tm - 1, n1, jnp.where(row == tm - 2, n0, pltpu.roll(dg, tm - 2, 0)))
            dgate = (dg * cw_ref[2:3, cols] + u1 * cw_ref[1:2, cols] + u2 * cw_ref[0:1, cols]).astype(BF16)
            dgate_ref[:, cols] = dgate
            dh = dh + lax.dot_general(dgate, w_ref[ch], NT, preferred_element_type=F32)
            dh = dh + lax.dot_general(dval_ref[:, cols], w_ref[2 + ch], NT, preferred_element_type=F32)
        x1 = x1_ref[...]
        rstd = lax.rsqrt(jnp.mean(x1 * x1, axis=-1, keepdims=True) + NORM_EPS)
        n2 = x1 * rstd
        gpre = gpre_ref[...]
        one_sc = 1.0 + mod_ref[4:5, :]
        _acc(dsh_ref, _colsum(dh))
        _acc(dsc_ref, _colsum(dh * (n2 * gpre)))
        _acc(dgpre_ref, _colsum(dh * one_sc * n2))
        dn = dh * (gpre * one_sc)
        dx1 = dout_ref[...] + rstd * (dn - n2 * jnp.mean(dn * n2, axis=-1, keepdims=True))
        dx1_ref[...] = dx1
        y1 = y1_ref[...]
        rstd1 = lax.rsqrt(jnp.mean(y1 * y1, axis=-1, keepdims=True) + NORM_EPS)
        y1n = y1 * rstd1
        gpost = gpost_ref[...]
        gtm = mod_ref[2:3, :]
        _acc(dgt_ref, _colsum(dx1 * (y1n * gpost)))
        dr1 = dx1 * gtm
        _acc(dgpost_ref, _colsum(dr1 * y1n))
        dyn = dr1 * gpost
        dy1 = rstd1 * (dyn - y1n * jnp.mean(dyn * y1n, axis=-1, keepdims=True))
        dy1_ref[...] = dy1.astype(BF16)

    vec = _full((1, D_MODEL))
    nxt = pl.BlockSpec((16, D_FF), lambda i: (jnp.minimum((i + 1) * (tm // 16), last_blk), 0))
    return pl.pallas_call(
        body,
        name="up_bwd",
        grid=(S // tm,),
        in_specs=[_rows(tm, D_FF), nxt, _rows(tm, D_FF), _full((3, D_FF)), _full(w_up_g.shape), _rows(tm, D_MODEL),
                  _rows(tm, D_MODEL), _rows(tm, D_MODEL), vec, vec, _full((6, D_MODEL))],
        out_specs=[_rows(tm, D_FF), _rows(tm, D_MODEL), _rows(tm, D_MODEL), vec, vec, vec, vec, vec],
        out_shape=[jax.ShapeDtypeStruct((S, D_FF), BF16), jax.ShapeDtypeStruct((S, D_MODEL), F32),
                   jax.ShapeDtypeStruct((S, D_MODEL), BF16)] + [jax.ShapeDtypeStruct((1, D_MODEL), F32)] * 5,
        compiler_params=_params(1),
    )(dgc, dgc, dval, conv_w, w_up_g, x1, dout, y1, g_pre, g_post, mod6)


def outproj_bwd(dy1, w_out_g, cat, tm=512):
    S = dy1.shape[0]

    def body(dy_ref, w_ref, attn_ref, dpool_ref, dattn_ref, delta_ref):
        dcat = jnp.zeros((tm, 512), F32)
        for j in range(N_CHIPS):
            dcat = dcat + lax.dot_general(dy_ref[:, j * 256:(j + 1) * 256], w_ref[j], NT, preferred_element_type=F32)
        dpool_ref[...] = dcat[:, :POOL_W]
        dattn = dcat[:, POOL_W:]
        dattn_ref[...] = dattn.astype(BF16)
        prod = dattn * attn_ref[...].astype(F32)
        r = lax.broadcasted_iota(jnp.int32, (GROUP_W, GROUP_W), 0) // HEAD_DIM
        c = lax.broadcasted_iota(jnp.int32, (GROUP_W, GROUP_W), 1) // HEAD_DIM
        ones_bd = jnp.where(r == c, 1.0, 0.0).astype(BF16)
        hi = prod.astype(BF16)
        lo = (prod - hi.astype(F32)).astype(BF16)
        delta_ref[...] = (jnp.dot(hi, ones_bd, preferred_element_type=F32)
                          + jnp.dot(lo, ones_bd, preferred_element_type=F32))

    return pl.pallas_call(
        body,
        name="outproj_bwd",
        grid=(S // tm,),
        in_specs=[_rows(tm, D_MODEL), _full(w_out_g.shape), pl.BlockSpec((tm, GROUP_W), lambda i: (i, 1))],
        out_specs=[_rows(tm, POOL_W), _rows(tm, GROUP_W), _rows(tm, GROUP_W)],
        out_shape=[jax.ShapeDtypeStruct((S, POOL_W), F32), jax.ShapeDtypeStruct((S, GROUP_W), BF16),
                   jax.ShapeDtypeStruct((S, GROUP_W), F32)],
        compiler_params=_params(1),
    )(dy1, w_out_g, cat)


def attn_bwd(qkv9, dattn, lse, delta, gi, d):
    S = qkv9.shape[1]
    L = S // d
    nb = L // ATT_BLOCK
    qv = qkv9.reshape(9, L, d * GROUP_W)
    view = lambda t: t.reshape(L, d * GROUP_W)

    def body(q_ref, k_ref, v_ref, do_ref, l_ref, dl_ref, out_ref, kpad, vpad, dkpad, dvpad):
        kpad[0:ATT_BLOCK, :] = jnp.zeros((ATT_BLOCK, GROUP_W), BF16)
        vpad[0:ATT_BLOCK, :] = jnp.zeros((ATT_BLOCK, GROUP_W), BF16)
        kpad[ATT_BLOCK:, :] = k_ref[...]
        vpad[ATT_BLOCK:, :] = v_ref[...]
        dkpad[...] = jnp.zeros_like(dkpad)
        dvpad[...] = jnp.zeros_like(dvpad)
        band, col, lo = _attn_masks()

        def step(n, carry):
            r0 = pl.multiple_of(n * ATT_BLOCK, ATT_BLOCK)
            valid = band & ((col >= ATT_BLOCK) | (n > 0))
            qb = q_ref[pl.ds(r0, ATT_BLOCK), :]
            dob = do_ref[pl.ds(r0, ATT_BLOCK), :]
            lb = l_ref[pl.ds(r0, ATT_BLOCK), :]
            dlb = dl_ref[pl.ds(r0, ATT_BLOCK), :]
            kb = kpad[pl.ds(r0, 2 * ATT_BLOCK), :]
            vb = vpad[pl.ds(r0, 2 * ATT_BLOCK), :]
            for pair in range(2):
                lanes = slice(pair * 128, (pair + 1) * 128)
                qp, dop, kp, vp = qb[:, lanes], dob[:, lanes], kb[:, lanes], vb[:, lanes]
                dqs = []
                dk_sum = jnp.zeros((2 * ATT_BLOCK, 128), F32)
                dv_sum = jnp.zeros((2 * ATT_BLOCK, 128), F32)
                for hh in range(2):
                    sel = lo if hh == 0 else jnp.logical_not(lo)
                    qm = jnp.where(sel, qp, jnp.zeros_like(qp))
                    dom = jnp.where(sel, dop, jnp.zeros_like(dop))
                    cc = pair * 128 + hh * HEAD_DIM
                    s = lax.dot_general(qm, kp, NT, preferred_element_type=F32)
                    s = jnp.where(valid, s, NEG)
                    p = jnp.exp(s - lb[:, cc:cc + 1])
                    dp = lax.dot_general(dom, vp, NT, preferred_element_type=F32)
                    ds = (p * (dp - dlb[:, cc:cc + 1])).astype(BF16)
                    pb = p.astype(BF16)
                    dqs.append(jnp.dot(ds, kp, preferred_element_type=F32))
                    dk_sum = dk_sum + lax.dot_general(ds, qm, TN, preferred_element_type=F32)
                    dv_sum = dv_sum + lax.dot_general(pb, dom, TN, preferred_element_type=F32)
                out_ref[0, pl.ds(r0, ATT_BLOCK), lanes] = jnp.where(lo, dqs[0], dqs[1])
                dkpad[pl.ds(r0, 2 * ATT_BLOCK), lanes] += dk_sum
                dvpad[pl.ds(r0, 2 * ATT_BLOCK), lanes] += dv_sum
            return carry

        lax.fori_loop(0, nb, step, 0)
        out_ref[1] = dkpad[ATT_BLOCK:, :]
        out_ref[2] = dvpad[ATT_BLOCK:, :]

    spec = lambda lead: pl.BlockSpec((None, L, GROUP_W), lambda r: (lead, 0, r))
    cls = pl.BlockSpec((L, GROUP_W), lambda r: (0, r))
    out = pl.pallas_call(
        body,
        name=f"attn_bwd_d{d}",
        grid=(d,),
        in_specs=[spec(gi), spec(3 + gi), spec(6 + gi), cls, cls, cls],
        out_specs=pl.BlockSpec((3, L, GROUP_W), lambda r: (0, 0, r)),
        out_shape=jax.ShapeDtypeStruct((3, L, d * GROUP_W), F32),
        scratch_shapes=[pltpu.VMEM((L + ATT_BLOCK, GROUP_W), BF16)] * 2 + [pltpu.VMEM((L + ATT_BLOCK, GROUP_W), F32)] * 2,
        compiler_params=_params(1),
    )(qv, qv, qv, view(dattn), view(lse), view(delta))
    return out.reshape(3, S, GROUP_W)


def pool_bwd(dpool, mixed, wbd, b, scale):
    S = dpool.shape[0]

    def body(dp_ref, mx_ref, w_ref, b_ref, s_ref, du_ref, dw_ref, db_ref, ds_ref):
        dp = dp_ref[...]
        mb = mx_ref[...]
        wv = w_ref[...]
        ypre = jnp.dot(mb, wv, preferred_element_type=F32) + b_ref[...]
        ds_ref[...] = _colsum(dp * ypre)
        dpre = dp * s_ref[...]
        db_ref[...] = _colsum(dpre)
        dpb = dpre.astype(BF16)
        dw_ref[...] = lax.dot_general(mb, dpb, TN, preferred_element_type=F32)
        dmix = lax.dot_general(dpb, wv, NT, preferred_element_type=F32)
        row = lax.broadcasted_iota(jnp.int32, dmix.shape, 0)
        lane, win = _pool_lane_windows(dmix.shape)
        e = dmix / jnp.minimum(row + 1, win).astype(F32)

        def shift(a, k):
            return jnp.where(row < S - k, pltpu.roll(a, S - k, 0), 0.0)

        f2 = e + shift(e, 1)
        f4 = f2 + shift(f2, 2)
        f8 = f4 + shift(f4, 4)
        f16 = f8 + shift(f8, 8)
        du_ref[...] = jnp.where(lane < 64, f2, jnp.where(lane < 128, f4, jnp.where(lane < 192, f8, f16))) - dmix

    vm = pl.BlockSpec(memory_space=pltpu.VMEM)
    return pl.pallas_call(
        body,
        name="pool_bwd",
        in_specs=[vm] * 5,
        out_specs=[vm] * 4,
        out_shape=[jax.ShapeDtypeStruct((S, POOL_W), F32), jax.ShapeDtypeStruct((POOL_W, POOL_W), F32),
                   jax.ShapeDtypeStruct((1, POOL_W), F32), jax.ShapeDtypeStruct((1, POOL_W), F32)],
        compiler_params=_params(),
    )(dpool, mixed, wbd, b, scale)


def inproj_bwd(dqkv, du, x, dx1, w_in_g, g, mod6, tc, tsa, tsb, tm=256):
    S = x.shape[0]

    def body(d0, d1, d2, du_ref, x_ref, dx1_ref, w_ref, g_ref, mod_ref, tc_ref, tsa_ref, tsb_ref,
             dp_ref, gx_ref, dsh_ref, dsc_ref, dg_ref):
        cs, sa, sb = tc_ref[...], tsa_ref[...], tsb_ref[...]
        dgrp = (d0, d1, d2)
        for sp in range(20):
            piece, half = sp // 2, sp % 2
            lanes = slice(half * 128, (half + 1) * 128)
            if piece == 0:
                blk = du_ref[:, lanes]
            else:
                kind, gi = (piece - 1) // 3, (piece - 1) % 3
                blk = dgrp[gi][kind, :, lanes]
                if kind == 0:
                    blk = _rope128(blk, cs, sa, sb, -1.0) * (HEAD_DIM ** -0.5)
                elif kind == 1:
                    blk = _rope128(blk, cs, sa, sb, -1.0)
            dp_ref[:, sp * 128:(sp + 1) * 128] = blk.astype(BF16)
        dh = jnp.zeros((tm, D_MODEL), F32)
        for j in range(N_CHIPS):
            dh = dh + lax.dot_general(dp_ref[:, j * 640:(j + 1) * 640], w_ref[j], NT, preferred_element_type=F32)
        xv = x_ref[...]
        rstd = lax.rsqrt(jnp.mean(xv * xv, axis=-1, keepdims=True) + NORM_EPS)
        n1 = xv * rstd
        gv = g_ref[...]
        one_sc = 1.0 + mod_ref[1:2, :]
        _acc(dsh_ref, _colsum(dh))
        _acc(dsc_ref, _colsum(dh * (n1 * gv)))
        _acc(dg_ref, _colsum(dh * one_sc * n1))
        dn = dh * (gv * one_sc)
        gx_ref[...] = dx1_ref[...] + rstd * (dn - n1 * jnp.mean(dn * n1, axis=-1, keepdims=True))

    vec = _full((1, D_MODEL))
    dspec = pl.BlockSpec((3, tm, GROUP_W), lambda i: (0, i, 0))
    return pl.pallas_call(
        body,
        name="inproj_bwd",
        grid=(S // tm,),
        in_specs=[dspec] * 3 + [_rows(tm, POOL_W), _rows(tm, D_MODEL), _rows(tm, D_MODEL), _full(w_in_g.shape), vec,
                                _full((6, D_MODEL)), _rows(tm, 128), _rows(tm, 128), _rows(tm, 128)],
        out_specs=[_rows(tm, IN_W), _rows(tm, D_MODEL), vec, vec, vec],
        out_shape=[jax.ShapeDtypeStruct((S, IN_W), BF16), jax.ShapeDtypeStruct((S, D_MODEL), F32)]
        + [jax.ShapeDtypeStruct((1, D_MODEL), F32)] * 3,
        compiler_params=_params(1),
    )(*dqkv, du, x, dx1, w_in_g, g, mod6, tc, tsa, tsb)


def _adamw(w, g, m, v):
    m = ADAM_B1 * m + (1.0 - ADAM_B1) * g
    v = ADAM_B2 * v + (1.0 - ADAM_B2) * (g * g)
    m_hat = m / (1.0 - ADAM_B1 ** ADAM_STEP)
    v_hat = v / (1.0 - ADAM_B2 ** ADAM_STEP)
    delta = -ADAM_LR * (m_hat / (jnp.sqrt(v_hat) + ADAM_EPS) + ADAM_WD * w)
    return delta, m, v


def adamw_rows(w, g, m, v, tr, name):
    R, C = w.shape

    def body(w_ref, g_ref, m_ref, v_ref, d_ref, mo_ref, vo_ref):
        d_ref[...], mo_ref[...], vo_ref[...] = _adamw(w_ref[...], g_ref[...], m_ref[...], v_ref[...])

    spec = pl.BlockSpec((tr, C), lambda i: (i, 0))
    return pl.pallas_call(
        body,
        name=name,
        grid=(R // tr,),
        in_specs=[spec] * 4,
        out_specs=[spec] * 3,
        out_shape=[jax.ShapeDtypeStruct((R, C), F32)] * 3,
        compiler_params=_params(1),
    )(w, g, m, v)


def adamw_ada(c_all_t, dmod_cols, w, m, v, tr=256):
    R, C = w.shape

    def body(ct_ref, dm_ref, w_ref, m_ref, v_ref, g_ref, d_ref, mo_ref, vo_ref):
        ct = ct_ref[...]
        act = ct * jax.nn.sigmoid(ct)
        g = jnp.zeros((tr, C), F32)
        for b in range(N_DEV):
            g = g + act[:, b:b + 1] * dm_ref[b:b + 1, :]
        g_ref[...] = g
        d_ref[...], mo_ref[...], vo_ref[...] = _adamw(w_ref[...], g, m_ref[...], v_ref[...])

    spec = pl.BlockSpec((tr, C), lambda i: (i, 0))
    return pl.pallas_call(
        body,
        name="adamw_ada",
        grid=(R // tr,),
        in_specs=[pl.BlockSpec((tr, N_DEV), lambda i: (i, 0)), _full((N_DEV, C)), spec, spec, spec],
        out_specs=[spec] * 4,
        out_shape=[jax.ShapeDtypeStruct((R, C), F32)] * 4,
        compiler_params=_params(1),
    )(c_all_t, dmod_cols, w, m, v)


def adamw_small(slab_a, slab_b, convw_g, wpool_g, params):
    names = ["b_ada", "g_pre_mix", "g_post_mix", "g_pre_ffn", "g_post_ffn", "b_pool", "pool_scale", "conv_b", "conv_w", "w_pool"]
    flat = []
    for n in names:
        flat += list(params[n])

    def body(a_ref, b_ref, cw_ref, wp_ref, *rest):
        ins, outs = rest[:30], rest[30:]

        def dev_sum(ref):
            t = ref[0]
            for dev in range(1, N_DEV):
                t = t + ref[dev]
            return t

        sa, sb_, scw, swp = dev_sum(a_ref), dev_sum(b_ref), dev_sum(cw_ref), dev_sum(wp_ref)
        grads = [
            jnp.concatenate([sa[k:k + 1, :] for k in range(6)], axis=1),
            sa[6:7, :], sa[7:8, :], sa[8:9, :], sa[9:10, :],
            sa[10:11, 0:256], sa[10:11, 256:512],
            sb_[3:4, :], scw, swp,
        ]
        for i, g in enumerate(grads):
            w_ref, m_ref, v_ref = ins[3 * i:3 * i + 3]
            d, mo, vo = _adamw(w_ref[...], g, m_ref[...], v_ref[...])
            outs[4 * i][...] = g
            outs[4 * i + 1][...] = d
            outs[4 * i + 2][...] = mo
            outs[4 * i + 3][...] = vo

    vm = pl.BlockSpec(memory_space=pltpu.VMEM)
    out_shape = []
    for n in names:
        out_shape += [jax.ShapeDtypeStruct(params[n][0].shape, F32)] * 4
    outs = pl.pallas_call(
        body,
        name="adamw_small",
        in_specs=[vm] * (4 + len(flat)),
        out_specs=[vm] * len(out_shape),
        out_shape=out_shape,
        compiler_params=_params(),
    )(slab_a, slab_b, convw_g, wpool_g, *flat)
    return {n: outs[4 * i:4 * i + 4] for i, n in enumerate(names)}


def _place():
    return lax.axis_index("x"), lax.axis_index("y"), lax.axis_index("c")


def _other_chips(x, y):
    return [(1 - x, y), (x, 1 - y), (1 - x, 1 - y)]


def _chip_id(cx, cy):
    return 2 * cx + cy


def gather_weights(shards):
    n = len(shards)
    halved = [s.shape[0] % 32 == 0 for s in shards]

    def body(*refs):
        ins, outs = refs[:n], refs[n:2 * n]
        send_sems, recv_sems, loc_sems = refs[2 * n:]
        x, y, c = _place()
        me = _chip_id(x, y)
        chips = _other_chips(x, y)
        sib = (x, y, 1 - c)

        def part(w, chip, half):
            if not halved[w]:
                return outs[w].at[chip]
            rh = shards[w].shape[0] // 2
            return outs[w].at[chip, pl.ds(half * rh, rh), :]

        def src_part(w):
            if not halved[w]:
                return ins[w]
            rh = shards[w].shape[0] // 2
            return ins[w].at[pl.ds(c * rh, rh), :]

        def rcopy(w, k, src, dst, to):
            return pltpu.make_async_remote_copy(src_ref=src, dst_ref=dst, send_sem=send_sems.at[6 * w + k],
                                                recv_sem=recv_sems.at[6 * w + k], device_id=to, device_id_type=MESH)

        local = [pltpu.make_async_copy(ins[w], outs[w].at[me], loc_sems.at[w]) for w in range(n)]
        for cp in local:
            cp.start()
        first = []
        for w in range(n):
            for k, (cx, cy) in enumerate(chips):
                cp = rcopy(w, k, src_part(w), part(w, me, c), (cx, cy, c))
                cp.start()
                first.append(cp)
        passed = []
        for w in range(n):
            for k, (cx, cy) in enumerate(chips):
                blk = part(w, _chip_id(cx, cy), c)
                rcopy(w, k, blk, blk, (cx, cy, c)).wait_recv()
                if halved[w]:
                    cp = rcopy(w, 3 + k, blk, blk, sib)
                    cp.start()
                    passed.append(cp)
        for w in range(n):
            if halved[w]:
                for k, (cx, cy) in enumerate(chips):
                    blk = part(w, _chip_id(cx, cy), 1 - c)
                    rcopy(w, 3 + k, blk, blk, sib).wait_recv()
        for cp in first + passed:
            cp.wait_send()
        for cp in local:
            cp.wait()

    hbm = pl.BlockSpec(memory_space=pl.ANY)
    return pl.pallas_call(
        body,
        name="gather_weights",
        in_specs=[hbm] * n,
        out_specs=[hbm] * n,
        out_shape=[jax.ShapeDtypeStruct((N_CHIPS,) + s.shape, s.dtype) for s in shards],
        scratch_shapes=[pltpu.SemaphoreType.DMA((6 * n,)), pltpu.SemaphoreType.DMA((6 * n,)), pltpu.SemaphoreType.DMA((n,))],
        compiler_params=pltpu.CompilerParams(has_side_effects=True, vmem_limit_bytes=VMEM_LIMIT),
    )(*shards)


def _flips():
    return [(fx, fy, fc) for fx in (0, 1) for fy in (0, 1) for fc in (0, 1)][1:]


def _flip(v, f):
    return v if f == 0 else 1 - v


def ada_mod(c3, w_ada, b_cols):
    CB = w_ada.shape[1]

    def body(c_ref, w_ref, b_ref, call_ref, mod_ref, modall, send_sems, recv_sems):
        x, y, c = _place()
        me_dev = 4 * x + 2 * y + c
        me = _chip_id(x, y)
        call_ref[me_dev] = c_ref[0]
        sends = []
        for k, (fx, fy, fc) in enumerate(_flips()):
            cp = pltpu.make_async_remote_copy(src_ref=c_ref.at[0], dst_ref=call_ref.at[me_dev], send_sem=send_sems.at[k],
                                              recv_sem=recv_sems.at[k],
                                              device_id=(_flip(x, fx), _flip(y, fy), _flip(c, fc)), device_id_type=MESH)
            cp.start()
            sends.append(cp)
        for k, (fx, fy, fc) in enumerate(_flips()):
            peer = 4 * _flip(x, fx) + 2 * _flip(y, fy) + _flip(c, fc)
            pltpu.make_async_remote_copy(src_ref=c_ref.at[0], dst_ref=call_ref.at[peer], send_sem=send_sems.at[k],
                                         recv_sem=recv_sems.at[k], device_id=(x, y, c), device_id_type=MESH).wait_recv()
        row = lax.broadcasted_iota(jnp.int32, (N_DEV, D_MODEL), 0)
        call = jnp.zeros((N_DEV, D_MODEL), F32)
        for dev in range(N_DEV):
            call = jnp.where(row == dev, call_ref[dev], call)
        act = call * jax.nn.sigmoid(call)
        modall[me] = jnp.dot(act, w_ref[...], preferred_element_type=F32, precision=lax.Precision.HIGHEST) + b_ref[...]
        for k, (cx, cy) in enumerate(_other_chips(x, y)):
            cp = pltpu.make_async_remote_copy(src_ref=modall.at[me], dst_ref=modall.at[me], send_sem=send_sems.at[7 + k],
                                              recv_sem=recv_sems.at[7 + k], device_id=(cx, cy, c), device_id_type=MESH)
            cp.start()
            sends.append(cp)
        for k, (cx, cy) in enumerate(_other_chips(x, y)):
            blk = modall.at[_chip_id(cx, cy)]
            pltpu.make_async_remote_copy(src_ref=blk, dst_ref=blk, send_sem=send_sems.at[7 + k], recv_sem=recv_sems.at[7 + k],
                                         device_id=(x, y, c), device_id_type=MESH).wait_recv()
        for cp in sends:
            cp.wait_send()
        mine = [modall[j, pl.ds(me_dev, 1), :] for j in range(N_CHIPS)]
        for r in range(6):
            pieces = []
            for h in range(2):
                pos = r * D_MODEL + h * 512
                pieces.append(mine[pos // CB][:, pos % CB:pos % CB + 512])
            mod_ref[r:r + 1, :] = jnp.concatenate(pieces, axis=1)

    vm = pl.BlockSpec(memory_space=pltpu.VMEM)
    return pl.pallas_call(
        body,
        name="ada_mod",
        in_specs=[vm] * 3,
        out_specs=[vm] * 2,
        out_shape=[jax.ShapeDtypeStruct((N_DEV, 1, D_MODEL), F32), jax.ShapeDtypeStruct((6, D_MODEL), F32)],
        scratch_shapes=[pltpu.VMEM((N_CHIPS, N_DEV, CB), F32), pltpu.SemaphoreType.DMA((10,)), pltpu.SemaphoreType.DMA((10,))],
        compiler_params=pltpu.CompilerParams(has_side_effects=True, vmem_limit_bytes=VMEM_LIMIT),
    )(c3, w_ada, b_cols)


def gather_small(blocks):
    n = len(blocks)

    def body(*refs):
        ins, outs = refs[:n], refs[n:2 * n]
        send_sems, recv_sems = refs[2 * n:]
        x, y, c = _place()
        sib = (x, y, 1 - c)
        chips = _other_chips(x, y)

        def dev(px, py, pc):
            return 4 * px + 2 * py + pc

        def cp(w, k, src, block_dev, to):
            return pltpu.make_async_remote_copy(src_ref=src, dst_ref=outs[w].at[block_dev], send_sem=send_sems.at[7 * w + k],
                                                recv_sem=recv_sems.at[7 * w + k], device_id=to, device_id_type=MESH)

        me = dev(x, y, c)
        started = []
        for w in range(n):
            outs[w][me] = ins[w][...]
            t = cp(w, 0, ins[w], me, sib)
            t.start()
            started.append(t)
            for k, (cx, cy) in enumerate(chips):
                t = cp(w, 1 + k, ins[w], me, (cx, cy, c))
                t.start()
                started.append(t)
        for w in range(n):
            for k, (cx, cy) in enumerate(chips):
                b = dev(cx, cy, c)
                cp(w, 1 + k, outs[w].at[b], b, (x, y, c)).wait_recv()
                t = cp(w, 4 + k, outs[w].at[b], b, sib)
                t.start()
                started.append(t)
        for w in range(n):
            b = dev(x, y, 1 - c)
            cp(w, 0, outs[w].at[b], b, (x, y, c)).wait_recv()
            for k, (cx, cy) in enumerate(chips):
                b = dev(cx, cy, 1 - c)
                cp(w, 4 + k, outs[w].at[b], b, (x, y, c)).wait_recv()
        for t in started:
            t.wait_send()

    vm = pl.BlockSpec(memory_space=pltpu.VMEM)
    return pl.pallas_call(
        body,
        name="gather_small",
        in_specs=[vm] * n,
        out_specs=[vm] * n,
        out_shape=[jax.ShapeDtypeStruct((N_DEV,) + b.shape, b.dtype) for b in blocks],
        scratch_shapes=[pltpu.SemaphoreType.DMA((7 * n,)), pltpu.SemaphoreType.DMA((7 * n,))],
        compiler_params=pltpu.CompilerParams(has_side_effects=True, vmem_limit_bytes=VMEM_LIMIT),
    )(*blocks)


def reduce_scatter_grads(grads, chunk_rows):
    n = len(grads)
    shapes = [g.shape[1:] for g in grads]
    halves = [s[0] // 2 for s in shapes]

    def body(*refs):
        gin = refs[:n]
        gout = refs[n:2 * n]
        sibbuf = refs[2 * n:3 * n]
        rest = refs[3 * n:]
        rbuf = rest[:n]
        pown = rest[n:2 * n]
        stage_a, stage_b, stage_o, stage_f = rest[2 * n:2 * n + 4]
        sib_send, sib_recv, ici_send, ici_recv, fin_send, fin_recv, ld_sems, st_sems = rest[2 * n + 4:]
        x, y, c = _place()
        me = _chip_id(x, y)
        chips = _other_chips(x, y)
        sib = (x, y, 1 - c)

        to_sib = []
        for w in range(n):
            rh = halves[w]
            cp = pltpu.make_async_remote_copy(src_ref=gin[w].at[:, pl.ds((1 - c) * rh, rh), :], dst_ref=sibbuf[w],
                                              send_sem=sib_send.at[w], recv_sem=sib_recv.at[w], device_id=sib,
                                              device_id_type=MESH)
            cp.start()
            to_sib.append(cp)

        sent = []
        for w in range(n):
            rh, cw = halves[w], shapes[w][1]
            ch = chunk_rows[w]
            to_sib[w].wait_recv()
            for k in range(4):
                chip = me if k == 3 else _chip_id(*chips[k])
                for r0 in range(0, rh, ch):
                    la = pltpu.make_async_copy(gin[w].at[chip, pl.ds(c * rh + r0, ch), :], stage_a.at[0:ch, 0:cw], ld_sems.at[0])
                    lb = pltpu.make_async_copy(sibbuf[w].at[chip, pl.ds(r0, ch), :], stage_b.at[0:ch, 0:cw], ld_sems.at[1])
                    la.start()
                    lb.start()
                    la.wait()
                    lb.wait()
                    tot = stage_a[0:ch, 0:cw] + stage_b[0:ch, 0:cw]
                    if k == 3:
                        pown[w][r0:r0 + ch, :] = tot
                    else:
                        stage_o[0:ch, 0:cw] = tot.astype(BF16)
                        cx, cy = chips[k]
                        cp = pltpu.make_async_remote_copy(src_ref=stage_o.at[0:ch, 0:cw], dst_ref=rbuf[w].at[k, r0:r0 + ch, :],
                                                          send_sem=ici_send.at[3 * w + k], recv_sem=ici_recv.at[3 * w + k],
                                                          device_id=(cx, cy, c), device_id_type=MESH)
                        cp.start()
                        cp.wait_send()
            sent.append(w)

        fin = []
        for w in range(n):
            rh, cw = halves[w], shapes[w][1]
            for k in range(3):
                whole = rbuf[w].at[k]
                pltpu.make_async_remote_copy(src_ref=whole, dst_ref=whole, send_sem=ici_send.at[3 * w + k],
                                             recv_sem=ici_recv.at[3 * w + k], device_id=(x, y, c),
                                             device_id_type=MESH).wait_recv()
            pown[w][...] = ((pown[w][...] + rbuf[w][0].astype(F32)) + rbuf[w][1].astype(F32)) + rbuf[w][2].astype(F32)
            mine = gout[w].at[pl.ds(c * rh, rh), :]
            st = pltpu.make_async_copy(pown[w], mine, st_sems.at[w])
            st.start()
            cp = pltpu.make_async_remote_copy(src_ref=pown[w], dst_ref=mine, send_sem=fin_send.at[w], recv_sem=fin_recv.at[w],
                                              device_id=sib, device_id_type=MESH)
            cp.start()
            fin.append((st, cp))
        for w in range(n):
            rh = halves[w]
            theirs = gout[w].at[pl.ds((1 - c) * rh, rh), :]
            pltpu.make_async_remote_copy(src_ref=theirs, dst_ref=theirs, send_sem=fin_send.at[w], recv_sem=fin_recv.at[w],
                                         device_id=(x, y, c), device_id_type=MESH).wait_recv()
        for cp in to_sib:
            cp.wait_send()
        for st, cp in fin:
            st.wait()
            cp.wait_send()

    hbm = pl.BlockSpec(memory_space=pl.ANY)
    max_ch = max(chunk_rows)
    max_c = max(s[1] for s in shapes)
    outs = pl.pallas_call(
        body,
        name="reduce_scatter_grads",
        in_specs=[hbm] * n,
        out_specs=[hbm] * (2 * n),
        out_shape=[jax.ShapeDtypeStruct(s, F32) for s in shapes]
        + [jax.ShapeDtypeStruct((N_CHIPS, h, s[1]), F32) for h, s in zip(halves, shapes)],
        scratch_shapes=[pltpu.VMEM((3, h, s[1]), BF16) for h, s in zip(halves, shapes)]
        + [pltpu.VMEM((h, s[1]), F32) for h, s in zip(halves, shapes)]
        + [pltpu.VMEM((max_ch, max_c), F32), pltpu.VMEM((max_ch, max_c), F32), pltpu.VMEM((max_ch, max_c), BF16),
           pltpu.VMEM((8, 128), F32)]
        + [pltpu.SemaphoreType.DMA((n,)), pltpu.SemaphoreType.DMA((n,)), pltpu.SemaphoreType.DMA((3 * n,)),
           pltpu.SemaphoreType.DMA((3 * n,)), pltpu.SemaphoreType.DMA((n,)), pltpu.SemaphoreType.DMA((n,)),
           pltpu.SemaphoreType.DMA((2,)), pltpu.SemaphoreType.DMA((n,))],
        compiler_params=pltpu.CompilerParams(has_side_effects=True, vmem_limit_bytes=VMEM_LIMIT),
    )(*grads)
    return outs[:n]


def _rope_tables(positions):
    inv_freq = ROPE_THETA ** (-jnp.arange(0, ROT_DIM, 2, dtype=F32) / ROT_DIM)
    ang = positions.astype(F32)[:, None] * inv_freq
    cos, sin = jnp.cos(ang), jnp.sin(ang)
    S = positions.shape[0]
    one, zero = jnp.ones((S, 48), F32), jnp.zeros((S, 48), F32)
    z8 = jnp.zeros((S, 8), F32)
    tc = jnp.concatenate([cos, cos, one], axis=1)
    tsa = jnp.concatenate([z8, sin, zero], axis=1)
    tsb = jnp.concatenate([-sin, z8, zero], axis=1)
    return tuple(jnp.tile(t, (1, 2)) for t in (tc, tsa, tsb))


def _block_diag(w_pool):
    wbd = jnp.zeros((POOL_W, POOL_W), F32)
    for gi in range(4):
        wbd = wbd.at[gi * 64:(gi + 1) * 64, gi * 64:(gi + 1) * 64].set(w_pool[gi])
    return wbd


def kernel(x, c, positions, w_ada, b_ada, g_pre_mix, g_post_mix, g_pre_ffn, g_post_ffn, w_in, w_pool, b_pool, pool_scale, w_out, w_up, conv_w, conv_b, w_down, loss_target, m_w_ada, m_b_ada, m_g_pre_mix, m_g_post_mix, m_g_pre_ffn, m_g_post_ffn, m_w_in, m_w_pool, m_b_pool, m_pool_scale, m_w_out, m_w_up, m_conv_w, m_conv_b, m_w_down, v_w_ada, v_b_ada, v_g_pre_mix, v_g_post_mix, v_g_pre_ffn, v_g_post_ffn, v_w_in, v_w_pool, v_b_pool, v_pool_scale, v_w_out, v_w_up, v_conv_w, v_conv_b, v_w_down):
    xi, yi = lax.axis_index("x"), lax.axis_index("y")
    chip = 2 * xi + yi
    x2, tgt = x[0], loss_target[0]
    S = x2.shape[0]

    w_in_g, w_out_g, w_up_g, w_down_g, conv_w_g = gather_weights(
        [w_in[0].astype(BF16), w_out[0].astype(BF16), w_up[0].astype(BF16), w_down[0].astype(BF16), conv_w[0]])
    w_down_f = w_down_g.reshape(D_FF, D_MODEL)
    conv_w_f = jnp.transpose(conv_w_g, (1, 0, 2)).reshape(3, D_FF)
    cb_ada = w_ada.shape[2]
    b_cols = lax.dynamic_slice(b_ada, (0, chip * cb_ada), (1, cb_ada))
    c_all, mod6 = ada_mod(c.reshape(1, 1, D_MODEL), w_ada[0], b_cols)
    tc, tsa, tsb = _rope_tables(positions[0])
    wbd = _block_diag(w_pool[0]).astype(BF16)
    b_pool2, scale2 = b_pool.reshape(1, POOL_W), pool_scale

    h1, u, qkv9 = inproj_fwd(x2, g_pre_mix, mod6, w_in_g, tc, tsa, tsb)
    o_l = [attn_fwd(qkv9, gi, d) for gi, d in enumerate(DILATIONS)]
    mixed, pool = pool_fwd(u, wbd, b_pool2, scale2)
    cat, lse, y1, x1, h2 = outproj_fwd([o for o, _ in o_l] + [l for _, l in o_l], pool, x2, w_out_g, g_post_mix, g_pre_ffn, mod6)
    gate, val = up_fwd(h2, w_up_g)
    a, dy2, dout, loss_v, d_gt_f, d_g_post_ffn = down_fwd(gate, val, conv_w_f, conv_b, w_down_f, x1, tgt, g_post_ffn, mod6)

    dgc, dval, d_conv_w, d_conv_b = down_bwd(dy2, w_down_f, gate, val, conv_w_f, conv_b)
    dw_down = dw_matmul(a, dy2, 2, (HALF_FF, D_MODEL), HALF_FF, D_MODEL, True, "dw_down")
    dgate, dx1, dy1, d_sh_f, d_sc_f, d_g_pre_ffn, d_gt_m, d_g_post_mix = up_bwd(
        dgc, dval, conv_w_f, w_up_g, x1, dout, y1, g_pre_ffn, g_post_mix, mod6)
    dw_up = dw_matmul(h2, dgate, 4, (D_MODEL, HALF_FF), D_MODEL, HALF_FF, False, "dw_up_gate", n_blk=2)
    dw_up = dw_matmul(h2, dval, 4, (D_MODEL, HALF_FF), D_MODEL, HALF_FF, False, "dw_up_val", prev=dw_up, blk_off=2, n_blk=2)
    dpool, dattn, delta = outproj_bwd(dy1, w_out_g, cat)
    dw_out = dw_matmul(cat, dy1, 4, (512, 256), 512, 256, False, "dw_out")
    du, d_wbd, d_b_pool, d_scale = pool_bwd(dpool, mixed, wbd, b_pool2, scale2)
    dqkv = [attn_bwd(qkv9, dattn, lse, delta, gi, d) for gi, d in enumerate(DILATIONS)]
    dproj, grad_x, d_sh_m, d_sc_m, d_g_pre_mix = inproj_bwd(dqkv, du, x2, dx1, w_in_g, g_pre_mix, mod6, tc, tsa, tsb)
    dw_in = dw_matmul(h1, dproj, 4, (D_MODEL, 640), D_MODEL, 640, False, "dw_in")

    g_w_in, g_w_out, g_w_up, g_w_down = reduce_scatter_grads(
        [dw_in, dw_out, dw_up, dw_down.reshape(N_CHIPS, D_FF // N_CHIPS, D_MODEL)], [256, 256, 128, 176])
    z1 = jnp.zeros((1, D_MODEL), F32)
    slab_a = jnp.concatenate(
        [d_sh_m, d_sc_m, d_gt_m, d_sh_f, d_sc_f, d_gt_f, d_g_pre_mix, d_g_post_mix, d_g_pre_ffn, d_g_post_ffn,
         jnp.concatenate([d_b_pool, d_scale, jnp.zeros((1, 512), F32)], axis=1)] + [z1] * 5, axis=0)
    slab_b = jnp.concatenate([d_conv_w, d_conv_b, jnp.zeros((4, D_FF), F32)], axis=0)
    d_wpool = jnp.concatenate([d_wbd[gi * 64:(gi + 1) * 64, gi * 64:(gi + 1) * 64] for gi in range(4)], axis=0)
    slab_a_g, slab_b_g, wpool_g = gather_small([slab_a, slab_b, d_wpool])
    cw_cols = conv_w.shape[2]
    convw_g = lax.dynamic_slice(slab_b_g, (0, 0, chip * cw_cols), (N_DEV, 3, cw_cols))
    dmod_cols = lax.dynamic_slice(slab_a_g[:, :6, :].reshape(N_DEV, 6 * D_MODEL), (0, chip * cb_ada), (N_DEV, cb_ada))

    res = {}
    g_ada, d_ada, m_ada, v_ada = adamw_ada(c_all.reshape(N_DEV, D_MODEL).T, dmod_cols, w_ada[0], m_w_ada[0], v_w_ada[0])
    res["w_ada"] = (g_ada[None], d_ada[None], m_ada[None], v_ada[None])
    for name, w, g, m, v, tr in (("w_in", w_in, g_w_in, m_w_in, v_w_in, 256), ("w_out", w_out, g_w_out, m_w_out, v_w_out, 256),
                                 ("w_up", w_up, g_w_up, m_w_up, v_w_up, 256), ("w_down", w_down, g_w_down, m_w_down, v_w_down, 352)):
        d_, m_, v_ = adamw_rows(w[0], g, m[0], v[0], tr, "adamw_" + name)
        res[name] = (g[None], d_[None], m_[None], v_[None])
    flat = lambda t: t.reshape(1, POOL_W)
    wp = lambda t: t.reshape(POOL_W, 64)
    small = adamw_small(slab_a_g, slab_b_g, convw_g, wpool_g, {
        "b_ada": (b_ada, m_b_ada, v_b_ada), "g_pre_mix": (g_pre_mix, m_g_pre_mix, v_g_pre_mix),
        "g_post_mix": (g_post_mix, m_g_post_mix, v_g_post_mix), "g_pre_ffn": (g_pre_ffn, m_g_pre_ffn, v_g_pre_ffn),
        "g_post_ffn": (g_post_ffn, m_g_post_ffn, v_g_post_ffn), "b_pool": (flat(b_pool), flat(m_b_pool), flat(v_b_pool)),
        "pool_scale": (pool_scale, m_pool_scale, v_pool_scale), "conv_b": (conv_b, m_conv_b, v_conv_b),
        "conv_w": (conv_w[0], m_conv_w[0], v_conv_w[0]), "w_pool": (wp(w_pool), wp(m_w_pool), wp(v_w_pool))})
    for name in ("b_ada", "g_pre_mix", "g_post_mix", "g_pre_ffn", "g_post_ffn", "pool_scale", "conv_b"):
        res[name] = tuple(small[name])
    res["b_pool"] = tuple(t.reshape(1, 4, 64) for t in small["b_pool"])
    res["conv_w"] = tuple(t[None] for t in small["conv_w"])
    res["w_pool"] = tuple(t.reshape(1, 4, 64, 64) for t in small["w_pool"])

    loss = lax.psum(loss_v[0, 0], ("x", "y", "c"))
    order = ["w_ada", "b_ada", "g_pre_mix", "g_post_mix", "g_pre_ffn", "g_post_ffn", "w_in", "w_pool", "b_pool", "pool_scale",
             "w_out", "w_up", "conv_w", "conv_b", "w_down"]
    outs = [loss, grad_x[None]]
    for k in range(4):
        outs += [res[n][k] for n in order]
    return tuple(outs)
```

```python
import functools
import math

import jax
import jax.numpy as jnp
from jax import lax
from jax.experimental import pallas as pl
from jax.experimental.pallas import tpu as pltpu

F32 = jnp.float32
BF16 = jnp.bfloat16
MESH = pl.DeviceIdType.MESH

D_MODEL = 1024
HEAD_DIM = 64
POOL_W = 256
GROUP_W = 256
DILATIONS = (1, 4, 16)
ATT_BLOCK = 128
IN_W = 2560
D_FF = 2816
HALF_FF = 1408
ROT_DIM = 16
ROPE_THETA = 500000.0
NORM_EPS = 1e-6
N_CHIPS = 4
N_DEV = 8
NEG = -1e30

ADAM_LR = 0.001
ADAM_B1 = 0.9
ADAM_B2 = 0.999
ADAM_EPS = 1e-08
ADAM_WD = 0.01
ADAM_STEP = 10

VMEM_LIMIT = 56 * 1024 * 1024

NT = (((1,), (1,)), ((), ()))
TN = (((0,), (0,)), ((), ()))


def _params(n_grid=0, **kw):
    sem = ("arbitrary",) * n_grid if n_grid else None
    return pltpu.CompilerParams(dimension_semantics=sem, vmem_limit_bytes=VMEM_LIMIT, **kw)


def _full(shape):
    nd = len(shape)
    return pl.BlockSpec(tuple(shape), lambda *_: (0,) * nd)


def _rows(tm, ncol):
    return pl.BlockSpec((tm, ncol), lambda i: (i, 0))


def _acc(ref, val):
    @pl.when(pl.program_id(0) == 0)
    def _():
        ref[...] = jnp.zeros_like(ref)

    ref[...] += val


def _colsum(v):
    return jnp.sum(v, axis=0, keepdims=True)


def _rope128(t, cs, sa, sb, sign):
    return t * cs + sign * (pltpu.roll(t, 8, 1) * sa + pltpu.roll(t, 120, 1) * sb)


def _gelu(z):
    c = math.sqrt(2.0 / math.pi)
    t = jnp.tanh(c * (z + 0.044715 * (z * z * z)))
    return 0.5 * z * (1.0 + t), t


def _gelu_grad(z, t):
    c = math.sqrt(2.0 / math.pi)
    return 0.5 * (1.0 + t) + 0.5 * z * (1.0 - t * t) * (c * (1.0 + 3.0 * 0.044715 * (z * z)))


def _conv_taps(gate, halo, first):
    row = lax.broadcasted_iota(jnp.int32, gate.shape, 0)
    halo = jnp.where(first, 0.0, halo)
    p1 = halo[15:16, :]
    p2 = halo[14:15, :]
    g1 = jnp.where(row == 0, p1, pltpu.roll(gate, 1, 0))
    g2 = jnp.where(row == 0, p2, jnp.where(row == 1, p1, pltpu.roll(gate, 2, 0)))
    return g1, g2


def inproj_fwd(x, g, mod6, w_in_g, tc, tsa, tsb, tm=512):
    S = x.shape[0]

    def body(x_ref, g_ref, mod_ref, w_ref, tc_ref, tsa_ref, tsb_ref, h_ref, u_ref, qkv_ref):
        xv = x_ref[...]
        rstd = lax.rsqrt(jnp.mean(xv * xv, axis=-1, keepdims=True) + NORM_EPS)
        h = ((xv * rstd) * g_ref[...]) * (1.0 + mod_ref[1:2, :]) + mod_ref[0:1, :]
        hb = h.astype(BF16)
        h_ref[...] = hb
        cs, sa, sb = tc_ref[...], tsa_ref[...], tsb_ref[...]
        for j in range(N_CHIPS):
            res = jnp.dot(hb, w_ref[j], preferred_element_type=F32)
            for t in range(5):
                sp = 5 * j + t
                piece, half = sp // 2, sp % 2
                blk = res[:, t * 128:(t + 1) * 128]
                lanes = slice(half * 128, (half + 1) * 128)
                if piece == 0:
                    u_ref[:, lanes] = blk
                else:
                    i9 = piece - 1
                    if i9 < 3:
                        blk = _rope128(blk, cs, sa, sb, 1.0) * (HEAD_DIM ** -0.5)
                    elif i9 < 6:
                        blk = _rope128(blk, cs, sa, sb, 1.0)
                    qkv_ref[i9, :, lanes] = blk.astype(BF16)

    return pl.pallas_call(
        body,
        name="inproj_fwd",
        grid=(S // tm,),
        in_specs=[_rows(tm, D_MODEL), _full((1, D_MODEL)), _full((6, D_MODEL)), _full(w_in_g.shape),
                  _rows(tm, 128), _rows(tm, 128), _rows(tm, 128)],
        out_specs=[_rows(tm, D_MODEL), _rows(tm, POOL_W), pl.BlockSpec((9, tm, GROUP_W), lambda i: (0, i, 0))],
        out_shape=[jax.ShapeDtypeStruct((S, D_MODEL), BF16), jax.ShapeDtypeStruct((S, POOL_W), F32),
                   jax.ShapeDtypeStruct((9, S, GROUP_W), BF16)],
        compiler_params=_params(1),
    )(x, g, mod6, w_in_g, tc, tsa, tsb)


def _attn_masks():
    row = lax.broadcasted_iota(jnp.int32, (ATT_BLOCK, 2 * ATT_BLOCK), 0)
    col = lax.broadcasted_iota(jnp.int32, (ATT_BLOCK, 2 * ATT_BLOCK), 1)
    band = (col >= row) & (col <= row + ATT_BLOCK)
    lane = lax.broadcasted_iota(jnp.int32, (ATT_BLOCK, 128), 1)
    return band, col, lane < HEAD_DIM


def attn_fwd(qkv9, gi, d):
    S = qkv9.shape[1]
    L = S // d
    nb = L // ATT_BLOCK
    qv = qkv9.reshape(9, L, d * GROUP_W)

    def body(q_ref, k_ref, v_ref, o_ref, l_ref, kpad, vpad):
        kpad[0:ATT_BLOCK, :] = jnp.zeros((ATT_BLOCK, GROUP_W), BF16)
        vpad[0:ATT_BLOCK, :] = jnp.zeros((ATT_BLOCK, GROUP_W), BF16)
        kpad[ATT_BLOCK:, :] = k_ref[...]
        vpad[ATT_BLOCK:, :] = v_ref[...]
        band, col, lo = _attn_masks()

        def step(n, carry):
            r0 = pl.multiple_of(n * ATT_BLOCK, ATT_BLOCK)
            valid = band & ((col >= ATT_BLOCK) | (n > 0))
            qb = q_ref[pl.ds(r0, ATT_BLOCK), :]
            kb = kpad[pl.ds(r0, 2 * ATT_BLOCK), :]
            vb = vpad[pl.ds(r0, 2 * ATT_BLOCK), :]
            for pair in range(2):
                lanes = slice(pair * 128, (pair + 1) * 128)
                qp, kp, vp = qb[:, lanes], kb[:, lanes], vb[:, lanes]
                outs, lses = [], []
                for hh in range(2):
                    sel = lo if hh == 0 else jnp.logical_not(lo)
                    qm = jnp.where(sel, qp, jnp.zeros_like(qp))
                    s = lax.dot_general(qm, kp, NT, preferred_element_type=F32)
                    s = jnp.where(valid, s, NEG)
                    m = jnp.max(s, axis=1, keepdims=True)
                    p = jnp.exp(s - m)
                    den = jnp.sum(p, axis=1, keepdims=True)
                    pv = jnp.dot(p.astype(BF16), vp, preferred_element_type=F32)
                    outs.append(pv / den)
                    lses.append(m + jnp.log(den))
                o_ref[pl.ds(r0, ATT_BLOCK), lanes] = jnp.where(lo, outs[0], outs[1])
                l_ref[pl.ds(r0, ATT_BLOCK), lanes] = jnp.where(lo, lses[0], lses[1])
            return carry

        lax.fori_loop(0, nb, step, 0)

    spec = lambda lead: pl.BlockSpec((None, L, GROUP_W), lambda r: (lead, 0, r))
    o, l = pl.pallas_call(
        body,
        name=f"attn_fwd_d{d}",
        grid=(d,),
        in_specs=[spec(gi), spec(3 + gi), spec(6 + gi)],
        out_specs=[pl.BlockSpec((L, GROUP_W), lambda r: (0, r))] * 2,
        out_shape=[jax.ShapeDtypeStruct((L, d * GROUP_W), F32)] * 2,
        scratch_shapes=[pltpu.VMEM((L + ATT_BLOCK, GROUP_W), BF16)] * 2,
        compiler_params=_params(1),
    )(qv, qv, qv)
    return o.reshape(S, GROUP_W), l.reshape(S, GROUP_W)


def _pool_lane_windows(shape):
    lane = lax.broadcasted_iota(jnp.int32, shape, 1)
    return lane, jnp.where(lane < 64, 2, jnp.where(lane < 128, 4, jnp.where(lane < 192, 8, 16)))


def pool_fwd(u, wbd, b, scale):
    S = u.shape[0]

    def body(u_ref, w_ref, b_ref, s_ref, mixed_ref, out_ref):
        uv = u_ref[...]
        row = lax.broadcasted_iota(jnp.int32, uv.shape, 0)
        lane, win = _pool_lane_windows(uv.shape)

        def shift(a, k):
            return jnp.where(row >= k, pltpu.roll(a, k, 0), 0.0)

        s2 = uv + shift(uv, 1)
        s4 = s2 + shift(s2, 2)
        s8 = s4 + shift(s4, 4)
        s16 = s8 + shift(s8, 8)
        tsum = jnp.where(lane < 64, s2, jnp.where(lane < 128, s4, jnp.where(lane < 192, s8, s16)))
        cnt = jnp.minimum(row + 1, win).astype(F32)
        mb = (tsum / cnt - uv).astype(BF16)
        mixed_ref[...] = mb
        y = jnp.dot(mb, w_ref[...], preferred_element_type=F32) + b_ref[...]
        out_ref[...] = (y * s_ref[...]).astype(BF16)

    vm = pl.BlockSpec(memory_space=pltpu.VMEM)
    return pl.pallas_call(
        body,
        name="pool_fwd",
        in_specs=[vm] * 4,
        out_specs=[vm] * 2,
        out_shape=[jax.ShapeDtypeStruct((S, POOL_W), BF16)] * 2,
        compiler_params=_params(),
    )(u, wbd, b, scale)


def outproj_fwd(o_l, pool, x, w_out_g, g_post, g_pre, mod6, tm=512):
    S = x.shape[0]

    def body(o0, o1, o2, l0, l1, l2, pool_ref, x_ref, w_ref, gpost_ref, gpre_ref, mod_ref,
             cat_ref, lse_ref, y1_ref, x1_ref, h2_ref):
        a, b, c = l0[...], l1[...], l2[...]
        m = jnp.maximum(jnp.maximum(a, b), c)
        e0, e1, e2 = jnp.exp(a - m), jnp.exp(b - m), jnp.exp(c - m)
        z = e0 + e1 + e2
        lse_ref[...] = m + jnp.log(z)
        attn = (e0 * o0[...] + e1 * o1[...] + e2 * o2[...]) / z
        cat = jnp.concatenate([pool_ref[...], attn.astype(BF16)], axis=1)
        cat_ref[...] = cat
        y1 = jnp.concatenate([jnp.dot(cat, w_ref[j], preferred_element_type=F32) for j in range(N_CHIPS)], axis=1)
        y1_ref[...] = y1
        rstd = lax.rsqrt(jnp.mean(y1 * y1, axis=-1, keepdims=True) + NORM_EPS)
        x1 = x_ref[...] + mod_ref[2:3, :] * ((y1 * rstd) * gpost_ref[...])
        x1_ref[...] = x1
        rstd2 = lax.rsqrt(jnp.mean(x1 * x1, axis=-1, keepdims=True) + NORM_EPS)
        h2 = ((x1 * rstd2) * gpre_ref[...]) * (1.0 + mod_ref[4:5, :]) + mod_ref[3:4, :]
        h2_ref[...] = h2.astype(BF16)

    t256 = _rows(tm, GROUP_W)
    return pl.pallas_call(
        body,
        name="outproj_fwd",
        grid=(S // tm,),
        in_specs=[t256] * 7 + [_rows(tm, D_MODEL), _full(w_out_g.shape), _full((1, D_MODEL)), _full((1, D_MODEL)),
                               _full((6, D_MODEL))],
        out_specs=[_rows(tm, 512), t256, _rows(tm, D_MODEL), _rows(tm, D_MODEL), _rows(tm, D_MODEL)],
        out_shape=[jax.ShapeDtypeStruct((S, 512), BF16), jax.ShapeDtypeStruct((S, GROUP_W), F32),
                   jax.ShapeDtypeStruct((S, D_MODEL), F32), jax.ShapeDtypeStruct((S, D_MODEL), F32),
                   jax.ShapeDtypeStruct((S, D_MODEL), BF16)],
        compiler_params=_params(1),
    )(*o_l, pool, x, w_out_g, g_post, g_pre, mod6)


def up_fwd(h2, w_up_g, tm=512):
    S = h2.shape[0]

    def body(h_ref, w_ref, gate_ref, val_ref):
        hb = h_ref[...]
        for j in range(N_CHIPS):
            res = jnp.dot(hb, w_ref[j], preferred_element_type=F32).astype(BF16)
            dst = gate_ref if j < 2 else val_ref
            dst[:, (j % 2) * HALF_FF:(j % 2 + 1) * HALF_FF] = res

    return pl.pallas_call(
        body,
        name="up_fwd",
        grid=(S // tm,),
        in_specs=[_rows(tm, D_MODEL), _full(w_up_g.shape)],
        out_specs=[_rows(tm, D_FF)] * 2,
        out_shape=[jax.ShapeDtypeStruct((S, D_FF), BF16)] * 2,
        compiler_params=_params(1),
    )(h2, w_up_g)


def _halo_prev(tm, ncol):
    return pl.BlockSpec((16, ncol), lambda i: (jnp.maximum(i * (tm // 16) - 1, 0), 0))


def down_fwd(gate, val, conv_w, conv_b, w_down, x1, target, g_post, mod6, tm=256):
    S = x1.shape[0]

    def body(gate_ref, halo_ref, val_ref, cw_ref, cb_ref, w_ref, x1_ref, tgt_ref, g_ref, mod_ref,
             a_ref, dy2_ref, dout_ref, loss_ref, dgt_ref, dg_ref):
        first = pl.program_id(0) == 0
        y2 = jnp.zeros((tm, D_MODEL), F32)
        for ch in range(2):
            cols = slice(ch * HALF_FF, (ch + 1) * HALF_FF)
            gt = gate_ref[:, cols].astype(F32)
            g1, g2 = _conv_taps(gt, halo_ref[:, cols].astype(F32), first)
            gc = g2 * cw_ref[0:1, cols] + g1 * cw_ref[1:2, cols] + gt * cw_ref[2:3, cols] + cb_ref[:, cols]
            ge, _ = _gelu(gc)
            ab = (ge * val_ref[:, cols].astype(F32)).astype(BF16)
            a_ref[:, cols] = ab
            y2 = y2 + jnp.dot(ab, w_ref[cols, :], preferred_element_type=F32)
        rstd = lax.rsqrt(jnp.mean(y2 * y2, axis=-1, keepdims=True) + NORM_EPS)
        y2n = y2 * rstd
        gv = g_ref[...]
        gtf = mod_ref[5:6, :]
        r2 = y2n * gv
        diff = (x1_ref[...] + gtf * r2) - tgt_ref[...]
        _acc(loss_ref, jnp.zeros((1, 128), F32) + 0.5 * jnp.sum(diff * diff) * (1.0 / D_MODEL))
        dout = diff * (1.0 / D_MODEL)
        dout_ref[...] = dout
        _acc(dgt_ref, _colsum(dout * r2))
        dr2 = dout * gtf
        _acc(dg_ref, _colsum(dr2 * y2n))
        dyn = dr2 * gv
        dy2 = rstd * (dyn - y2n * jnp.mean(dyn * y2n, axis=-1, keepdims=True))
        dy2_ref[...] = dy2.astype(BF16)

    vec = _full((1, D_MODEL))
    return pl.pallas_call(
        body,
        name="down_fwd",
        grid=(S // tm,),
        in_specs=[_rows(tm, D_FF), _halo_prev(tm, D_FF), _rows(tm, D_FF), _full((3, D_FF)), _full((1, D_FF)),
                  _full((D_FF, D_MODEL)), _rows(tm, D_MODEL), _rows(tm, D_MODEL), vec, _full((6, D_MODEL))],
        out_specs=[_rows(tm, D_FF), _rows(tm, D_MODEL), _rows(tm, D_MODEL), _full((1, 128)), vec, vec],
        out_shape=[jax.ShapeDtypeStruct((S, D_FF), BF16), jax.ShapeDtypeStruct((S, D_MODEL), BF16),
                   jax.ShapeDtypeStruct((S, D_MODEL), F32), jax.ShapeDtypeStruct((1, 128), F32),
                   jax.ShapeDtypeStruct((1, D_MODEL), F32), jax.ShapeDtypeStruct((1, D_MODEL), F32)],
        compiler_params=_params(1),
    )(gate, gate, val, conv_w, conv_b, w_down, x1, target, g_post, mod6)


def down_bwd(dy2, w_down, gate, val, conv_w, conv_b, tm=256):
    S = dy2.shape[0]

    def body(dy_ref, w_ref, gate_ref, halo_ref, val_ref, cw_ref, cb_ref, dgc_ref, dval_ref, dcw_ref, dcb_ref):
        first = pl.program_id(0) == 0

        @pl.when(first)
        def _():
            dcw_ref[...] = jnp.zeros_like(dcw_ref)
            dcb_ref[...] = jnp.zeros_like(dcb_ref)

        dyb = dy_ref[...]
        for ch in range(2):
            cols = slice(ch * HALF_FF, (ch + 1) * HALF_FF)
            da = lax.dot_general(dyb, w_ref[cols, :], NT, preferred_element_type=F32)
            gt = gate_ref[:, cols].astype(F32)
            g1, g2 = _conv_taps(gt, halo_ref[:, cols].astype(F32), first)
            gc = g2 * cw_ref[0:1, cols] + g1 * cw_ref[1:2, cols] + gt * cw_ref[2:3, cols] + cb_ref[:, cols]
            ge, th = _gelu(gc)
            dgc = da * val_ref[:, cols].astype(F32) * _gelu_grad(gc, th)
            dgc_ref[:, cols] = dgc.astype(BF16)
            dval_ref[:, cols] = (da * ge).astype(BF16)
            dcb_ref[:, cols] += _colsum(dgc)
            dcw_ref[0:1, cols] += _colsum(dgc * g2)
            dcw_ref[1:2, cols] += _colsum(dgc * g1)
            dcw_ref[2:3, cols] += _colsum(dgc * gt)

    return pl.pallas_call(
        body,
        name="down_bwd",
        grid=(S // tm,),
        in_specs=[_rows(tm, D_MODEL), _full((D_FF, D_MODEL)), _rows(tm, D_FF), _halo_prev(tm, D_FF), _rows(tm, D_FF),
                  _full((3, D_FF)), _full((1, D_FF))],
        out_specs=[_rows(tm, D_FF), _rows(tm, D_FF), _full((3, D_FF)), _full((1, D_FF))],
        out_shape=[jax.ShapeDtypeStruct((S, D_FF), BF16), jax.ShapeDtypeStruct((S, D_FF), BF16),
                   jax.ShapeDtypeStruct((3, D_FF), F32), jax.ShapeDtypeStruct((1, D_FF), F32)],
        compiler_params=_params(1),
    )(dy2, w_down, gate, gate, val, conv_w, conv_b)


def dw_matmul(a, b, out_blocks, blk_shape, a_cols, b_cols, a_blocked, name, prev=None, blk_off=0, n_blk=None, tm=512):
    S = a.shape[0]
    n_blk = out_blocks if n_blk is None else n_blk

    def body(*refs):
        a_ref, b_ref, o_ref = refs[0], refs[1], refs[-1]

        @pl.when(pl.program_id(1) == 0)
        def _():
            o_ref[...] = jnp.zeros_like(o_ref)

        o_ref[...] += lax.dot_general(a_ref[...], b_ref[...], TN, preferred_element_type=F32)

    a_spec = pl.BlockSpec((tm, a_cols), (lambda j, i: (i, j)) if a_blocked else (lambda j, i: (i, 0)))
    b_spec = pl.BlockSpec((tm, b_cols), (lambda j, i: (i, 0)) if a_blocked else (lambda j, i: (i, j)))
    in_specs = [a_spec, b_spec]
    args = [a, b]
    aliases = {}
    if prev is not None:
        in_specs.append(pl.BlockSpec(memory_space=pl.ANY))
        args.append(prev)
        aliases = {2: 0}
    return pl.pallas_call(
        body,
        name=name,
        grid=(n_blk, S // tm),
        in_specs=in_specs,
        out_specs=pl.BlockSpec((None,) + tuple(blk_shape), lambda j, i: (j + blk_off, 0, 0)),
        out_shape=jax.ShapeDtypeStruct((out_blocks,) + tuple(blk_shape), F32),
        input_output_aliases=aliases,
        compiler_params=_params(2),
    )(*args)


def up_bwd(dgc, dval, conv_w, w_up_g, x1, dout, y1, g_pre, g_post, mod6, tm=256):
    S = x1.shape[0]
    last_blk = S // 16 - 1

    def body(dgc_ref, nxt_ref, dval_ref, cw_ref, w_ref, x1_ref, dout_ref, y1_ref, gpre_ref, gpost_ref, mod_ref,
             dgate_ref, dx1_ref, dy1_ref, dsh_ref, dsc_ref, dgpre_ref, dgt_ref, dgpost_ref):
        last = pl.program_id(0) == pl.num_programs(0) - 1
        dh = jnp.zeros((tm, D_MODEL), F32)
        for ch in range(2):
            cols = slice(ch * HALF_FF, (ch + 1) * HALF_FF)
            dg = dgc_ref[:, cols].astype(F32)
            nx = jnp.where(last, 0.0, nxt_ref[:, cols].astype(F32))
            row = lax.broadcasted_iota(jnp.int32, dg.shape, 0)
            n0, n1 = nx[0:1, :], nx[1:2, :]
            u1 = jnp.where(row == tm - 1, n0, pltpu.roll(dg, tm - 1, 0))
            u2 = jnp.where(row == tm - 1, n1, jnp.where(row == tm - 2, n0, pltpu.roll(dg, tm - 2, 0)))
            dgate = (dg * cw_ref[2:3, cols] + u1 * cw_ref[1:2, cols] + u2 * cw_ref[0:1, cols]).astype(BF16)
            dgate_ref[:, cols] = dgate
            dh = dh + lax.dot_general(dgate, w_ref[ch], NT, preferred_element_type=F32)
            dh = dh + lax.dot_general(dval_ref[:, cols], w_ref[2 + ch], NT, preferred_element_type=F32)
        x1 = x1_ref[...]
        rstd = lax.rsqrt(jnp.mean(x1 * x1, axis=-1, keepdims=True) + NORM_EPS)
        n2 = x1 * rstd
        gpre = gpre_ref[...]
        one_sc = 1.0 + mod_ref[4:5, :]
        _acc(dsh_ref, _colsum(dh))
        _acc(dsc_ref, _colsum(dh * (n2 * gpre)))
        _acc(dgpre_ref, _colsum(dh * one_sc * n2))
        dn = dh * (gpre * one_sc)
        dx1 = dout_ref[...] + rstd * (dn - n2 * jnp.mean(dn * n2, axis=-1, keepdims=True))
        dx1_ref[...] = dx1
        y1 = y1_ref[...]
        rstd1 = lax.rsqrt(jnp.mean(y1 * y1, axis=-1, keepdims=True) + NORM_EPS)
        y1n = y1 * rstd1
        gpost = gpost_ref[...]
        gtm = mod_ref[2:3, :]
        _acc(dgt_ref, _colsum(dx1 * (y1n * gpost)))
        dr1 = dx1 * gtm
        _acc(dgpost_ref, _colsum(dr1 * y1n))
        dyn = dr1 * gpost
        dy1 = rstd1 * (dyn - y1n * jnp.mean(dyn * y1n, axis=-1, keepdims=True))
        dy1_ref[...] = dy1.astype(BF16)

    vec = _full((1, D_MODEL))
    nxt = pl.BlockSpec((16, D_FF), lambda i: (jnp.minimum((i + 1) * (tm // 16), last_blk), 0))
    return pl.pallas_call(
        body,
        name="up_bwd",
        grid=(S // tm,),
        in_specs=[_rows(tm, D_FF), nxt, _rows(tm, D_FF), _full((3, D_FF)), _full(w_up_g.shape), _rows(tm, D_MODEL),
                  _rows(tm, D_MODEL), _rows(tm, D_MODEL), vec, vec, _full((6, D_MODEL))],
        out_specs=[_rows(tm, D_FF), _rows(tm, D_MODEL), _rows(tm, D_MODEL), vec, vec, vec, vec, vec],
        out_shape=[jax.ShapeDtypeStruct((S, D_FF), BF16), jax.ShapeDtypeStruct((S, D_MODEL), F32),
                   jax.ShapeDtypeStruct((S, D_MODEL), BF16)] + [jax.ShapeDtypeStruct((1, D_MODEL), F32)] * 5,
        compiler_params=_params(1),
    )(dgc, dgc, dval, conv_w, w_up_g, x1, dout, y1, g_pre, g_post, mod6)


def outproj_bwd(dy1, w_out_g, cat, tm=512):
    S = dy1.shape[0]

    def body(dy_ref, w_ref, attn_ref, dpool_ref, dattn_ref, delta_ref):
        dcat = jnp.zeros((tm, 512), F32)
        for j in range(N_CHIPS):
            dcat = dcat + lax.dot_general(dy_ref[:, j * 256:(j + 1) * 256], w_ref[j], NT, preferred_element_type=F32)
        dpool_ref[...] = dcat[:, :POOL_W]
        dattn = dcat[:, POOL_W:]
        dattn_ref[...] = dattn.astype(BF16)
        prod = dattn * attn_ref[...].astype(F32)
        r = lax.broadcasted_iota(jnp.int32, (GROUP_W, GROUP_W), 0) // HEAD_DIM
        c = lax.broadcasted_iota(jnp.int32, (GROUP_W, GROUP_W), 1) // HEAD_DIM
        ones_bd = jnp.where(r == c, 1.0, 0.0).astype(BF16)
        hi = prod.astype(BF16)
        lo = (prod - hi.astype(F32)).astype(BF16)
        delta_ref[...] = (jnp.dot(hi, ones_bd, preferred_element_type=F32)
                          + jnp.dot(lo, ones_bd, preferred_element_type=F32))

    return pl.pallas_call(
        body,
        name="outproj_bwd",
        grid=(S // tm,),
        in_specs=[_rows(tm, D_MODEL), _full(w_out_g.shape), pl.BlockSpec((tm, GROUP_W), lambda i: (i, 1))],
        out_specs=[_rows(tm, POOL_W), _rows(tm, GROUP_W), _rows(tm, GROUP_W)],
        out_shape=[jax.ShapeDtypeStruct((S, POOL_W), F32), jax.ShapeDtypeStruct((S, GROUP_W), BF16),
                   jax.ShapeDtypeStruct((S, GROUP_W), F32)],
        compiler_params=_params(1),
    )(dy1, w_out_g, cat)


def attn_bwd(qkv9, dattn, lse, delta, gi, d):
    S = qkv9.shape[1]
    L = S // d
    nb = L // ATT_BLOCK
    qv = qkv9.reshape(9, L, d * GROUP_W)
    view = lambda t: t.reshape(L, d * GROUP_W)

    def body(q_ref, k_ref, v_ref, do_ref, l_ref, dl_ref, out_ref, kpad, vpad, dkpad, dvpad):
        kpad[0:ATT_BLOCK, :] = jnp.zeros((ATT_BLOCK, GROUP_W), BF16)
        vpad[0:ATT_BLOCK, :] = jnp.zeros((ATT_BLOCK, GROUP_W), BF16)
        kpad[ATT_BLOCK:, :] = k_ref[...]
        vpad[ATT_BLOCK:, :] = v_ref[...]
        dkpad[...] = jnp.zeros_like(dkpad)
        dvpad[...] = jnp.zeros_like(dvpad)
        band, col, lo = _attn_masks()

        def step(n, carry):
            r0 = pl.multiple_of(n * ATT_BLOCK, ATT_BLOCK)
            valid = band & ((col >= ATT_BLOCK) | (n > 0))
            qb = q_ref[pl.ds(r0, ATT_BLOCK), :]
            dob = do_ref[pl.ds(r0, ATT_BLOCK), :]
            lb = l_ref[pl.ds(r0, ATT_BLOCK), :]
            dlb = dl_ref[pl.ds(r0, ATT_BLOCK), :]
            kb = kpad[pl.ds(r0, 2 * ATT_BLOCK), :]
            vb = vpad[pl.ds(r0, 2 * ATT_BLOCK), :]
            for pair in range(2):
                lanes = slice(pair * 128, (pair + 1) * 128)
                qp, dop, kp, vp = qb[:, lanes], dob[:, lanes], kb[:, lanes], vb[:, lanes]
                dqs = []
                dk_sum = jnp.zeros((2 * ATT_BLOCK, 128), F32)
                dv_sum = jnp.zeros((2 * ATT_BLOCK, 128), F32)
                for hh in range(2):
                    sel = lo if hh == 0 else jnp.logical_not(lo)
                    qm = jnp.where(sel, qp, jnp.zeros_like(qp))
                    dom = jnp.where(sel, dop, jnp.zeros_like(dop))
                    cc = pair * 128 + hh * HEAD_DIM
                    s = lax.dot_general(qm, kp, NT, preferred_element_type=F32)
                    s = jnp.where(valid, s, NEG)
                    p = jnp.exp(s - lb[:, cc:cc + 1])
                    dp = lax.dot_general(dom, vp, NT, preferred_element_type=F32)
                    ds = (p * (dp - dlb[:, cc:cc + 1])).astype(BF16)
                    pb = p.astype(BF16)
                    dqs.append(jnp.dot(ds, kp, preferred_element_type=F32))
                    dk_sum = dk_sum + lax.dot_general(ds, qm, TN, preferred_element_type=F32)
                    dv_sum = dv_sum + lax.dot_general(pb, dom, TN, preferred_element_type=F32)
                out_ref[0, pl.ds(r0, ATT_BLOCK), lanes] = jnp.where(lo, dqs[0], dqs[1])
                dkpad[pl.ds(r0, 2 * ATT_BLOCK), lanes] += dk_sum
                dvpad[pl.ds(r0, 2 * ATT_BLOCK), lanes] += dv_sum
            return carry

        lax.fori_loop(0, nb, step, 0)
        out_ref[1] = dkpad[ATT_BLOCK:, :]
        out_ref[2] = dvpad[ATT_BLOCK:, :]

    spec = lambda lead: pl.BlockSpec((None, L, GROUP_W), lambda r: (lead, 0, r))
    cls = pl.BlockSpec((L, GROUP_W), lambda r: (0, r))
    out = pl.pallas_call(
        body,
        name=f"attn_bwd_d{d}",
        grid=(d,),
        in_specs=[spec(gi), spec(3 + gi), spec(6 + gi), cls, cls, cls],
        out_specs=pl.BlockSpec((3, L, GROUP_W), lambda r: (0, 0, r)),
        out_shape=jax.ShapeDtypeStruct((3, L, d * GROUP_W), F32),
        scratch_shapes=[pltpu.VMEM((L + ATT_BLOCK, GROUP_W), BF16)] * 2 + [pltpu.VMEM((L + ATT_BLOCK, GROUP_W), F32)] * 2,
        compiler_params=_params(1),
    )(qv, qv, qv, view(dattn), view(lse), view(delta))
    return out.reshape(3, S, GROUP_W)


def pool_bwd(dpool, mixed, wbd, b, scale):
    S = dpool.shape[0]

    def body(dp_ref, mx_ref, w_ref, b_ref, s_ref, du_ref, dw_ref, db_ref, ds_ref):
        dp = dp_ref[...]
        mb = mx_ref[...]
        wv = w_ref[...]
        ypre = jnp.dot(mb, wv, preferred_element_type=F32) + b_ref[...]
        ds_ref[...] = _colsum(dp * ypre)
        dpre = dp * s_ref[...]
        db_ref[...] = _colsum(dpre)
        dpb = dpre.astype(BF16)
        dw_ref[...] = lax.dot_general(mb, dpb, TN, preferred_element_type=F32)
        dmix = lax.dot_general(dpb, wv, NT, preferred_element_type=F32)
        row = lax.broadcasted_iota(jnp.int32, dmix.shape, 0)
        lane, win = _pool_lane_windows(dmix.shape)
        e = dmix / jnp.minimum(row + 1, win).astype(F32)

        def shift(a, k):
            return jnp.where(row < S - k, pltpu.roll(a, S - k, 0), 0.0)

        f2 = e + shift(e, 1)
        f4 = f2 + shift(f2, 2)
        f8 = f4 + shift(f4, 4)
        f16 = f8 + shift(f8, 8)
        du_ref[...] = jnp.where(lane < 64, f2, jnp.where(lane < 128, f4, jnp.where(lane < 192, f8, f16))) - dmix

    vm = pl.BlockSpec(memory_space=pltpu.VMEM)
    return pl.pallas_call(
        body,
        name="pool_bwd",
        in_specs=[vm] * 5,
        out_specs=[vm] * 4,
        out_shape=[jax.ShapeDtypeStruct((S, POOL_W), F32), jax.ShapeDtypeStruct((POOL_W, POOL_W), F32),
                   jax.ShapeDtypeStruct((1, POOL_W), F32), jax.ShapeDtypeStruct((1, POOL_W), F32)],
        compiler_params=_params(),
    )(dpool, mixed, wbd, b, scale)


def inproj_bwd(dqkv, du, x, dx1, w_in_g, g, mod6, tc, tsa, tsb, tm=256):
    S = x.shape[0]

    def body(d0, d1, d2, du_ref, x_ref, dx1_ref, w_ref, g_ref, mod_ref, tc_ref, tsa_ref, tsb_ref,
             dp_ref, gx_ref, dsh_ref, dsc_ref, dg_ref):
        cs, sa, sb = tc_ref[...], tsa_ref[...], tsb_ref[...]
        dgrp = (d0, d1, d2)
        for sp in range(20):
            piece, half = sp // 2, sp % 2
            lanes = slice(half * 128, (half + 1) * 128)
            if piece == 0:
                blk = du_ref[:, lanes]
            else:
                kind, gi = (piece - 1) // 3, (piece - 1) % 3
                blk = dgrp[gi][kind, :, lanes]
                if kind == 0:
                    blk = _rope128(blk, cs, sa, sb, -1.0) * (HEAD_DIM ** -0.5)
                elif kind == 1:
                    blk = _rope128(blk, cs, sa, sb, -1.0)
            dp_ref[:, sp * 128:(sp + 1) * 128] = blk.astype(BF16)
        dh = jnp.zeros((tm, D_MODEL), F32)
        for j in range(N_CHIPS):
            dh = dh + lax.dot_general(dp_ref[:, j * 640:(j + 1) * 640], w_ref[j], NT, preferred_element_type=F32)
        xv = x_ref[...]
        rstd = lax.rsqrt(jnp.mean(xv * xv, axis=-1, keepdims=True) + NORM_EPS)
        n1 = xv * rstd
        gv = g_ref[...]
        one_sc = 1.0 + mod_ref[1:2, :]
        _acc(dsh_ref, _colsum(dh))
        _acc(dsc_ref, _colsum(dh * (n1 * gv)))
        _acc(dg_ref, _colsum(dh * one_sc * n1))
        dn = dh * (gv * one_sc)
        gx_ref[...] = dx1_ref[...] + rstd * (dn - n1 * jnp.mean(dn * n1, axis=-1, keepdims=True))

    vec = _full((1, D_MODEL))
    dspec = pl.BlockSpec((3, tm, GROUP_W), lambda i: (0, i, 0))
    return pl.pallas_call(
        body,
        name="inproj_bwd",
        grid=(S // tm,),
        in_specs=[dspec] * 3 + [_rows(tm, POOL_W), _rows(tm, D_MODEL), _rows(tm, D_MODEL), _full(w_in_g.shape), vec,
                                _full((6, D_MODEL)), _rows(tm, 128), _rows(tm, 128), _rows(tm, 128)],
        out_specs=[_rows(tm, IN_W), _rows(tm, D_MODEL), vec, vec, vec],
        out_shape=[jax.ShapeDtypeStruct((S, IN_W), BF16), jax.ShapeDtypeStruct((S, D_MODEL), F32)]
        + [jax.ShapeDtypeStruct((1, D_MODEL), F32)] * 3,
        compiler_params=_params(1),
    )(*dqkv, du, x, dx1, w_in_g, g, mod6, tc, tsa, tsb)


def _adamw(w, g, m, v):
    m = ADAM_B1 * m + (1.0 - ADAM_B1) * g
    v = ADAM_B2 * v + (1.0 - ADAM_B2) * (g * g)
    m_hat = m / (1.0 - ADAM_B1 ** ADAM_STEP)
    v_hat = v / (1.0 - ADAM_B2 ** ADAM_STEP)
    delta = -ADAM_LR * (m_hat / (jnp.sqrt(v_hat) + ADAM_EPS) + ADAM_WD * w)
    return delta, m, v


def adamw_rows(w, g, m, v, tr, name):
    R, C = w.shape

    def body(w_ref, g_ref, m_ref, v_ref, d_ref, mo_ref, vo_ref):
        d_ref[...], mo_ref[...], vo_ref[...] = _adamw(w_ref[...], g_ref[...], m_ref[...], v_ref[...])

    spec = pl.BlockSpec((tr, C), lambda i: (i, 0))
    return pl.pallas_call(
        body,
        name=name,
        grid=(R // tr,),
        in_specs=[spec] * 4,
        out_specs=[spec] * 3,
        out_shape=[jax.ShapeDtypeStruct((R, C), F32)] * 3,
        compiler_params=_params(1),
    )(w, g, m, v)


def adamw_ada(c_all_t, dmod_cols, w, m, v, tr=256):
    R, C = w.shape

    def body(ct_ref, dm_ref, w_ref, m_ref, v_ref, g_ref, d_ref, mo_ref, vo_ref):
        ct = ct_ref[...]
        act = ct * jax.nn.sigmoid(ct)
        g = jnp.zeros((tr, C), F32)
        for b in range(N_DEV):
            g = g + act[:, b:b + 1] * dm_ref[b:b + 1, :]
        g_ref[...] = g
        d_ref[...], mo_ref[...], vo_ref[...] = _adamw(w_ref[...], g, m_ref[...], v_ref[...])

    spec = pl.BlockSpec((tr, C), lambda i: (i, 0))
    return pl.pallas_call(
        body,
        name="adamw_ada",
        grid=(R // tr,),
        in_specs=[pl.BlockSpec((tr, N_DEV), lambda i: (i, 0)), _full((N_DEV, C)), spec, spec, spec],
        out_specs=[spec] * 4,
        out_shape=[jax.ShapeDtypeStruct((R, C), F32)] * 4,
        compiler_params=_params(1),
    )(c_all_t, dmod_cols, w, m, v)


def adamw_small(slab_a, slab_b, convw_g, wpool_g, params):
    names = ["b_ada", "g_pre_mix", "g_post_mix", "g_pre_ffn", "g_post_ffn", "b_pool", "pool_scale", "conv_b", "conv_w", "w_pool"]
    flat = []
    for n in names:
        flat += list(params[n])

    def body(a_ref, b_ref, cw_ref, wp_ref, *rest):
        ins, outs = rest[:30], rest[30:]

        def dev_sum(ref):
            t = ref[0]
            for dev in range(1, N_DEV):
                t = t + ref[dev]
            return t

        sa, sb_, scw, swp = dev_sum(a_ref), dev_sum(b_ref), dev_sum(cw_ref), dev_sum(wp_ref)
        grads = [
            jnp.concatenate([sa[k:k + 1, :] for k in range(6)], axis=1),
            sa[6:7, :], sa[7:8, :], sa[8:9, :], sa[9:10, :],
            sa[10:11, 0:256], sa[10:11, 256:512],
            sb_[3:4, :], scw, swp,
        ]
        for i, g in enumerate(grads):
            w_ref, m_ref, v_ref = ins[3 * i:3 * i + 3]
            d, mo, vo = _adamw(w_ref[...], g, m_ref[...], v_ref[...])
            outs[4 * i][...] = g
            outs[4 * i + 1][...] = d
            outs[4 * i + 2][...] = mo
            outs[4 * i + 3][...] = vo

    vm = pl.BlockSpec(memory_space=pltpu.VMEM)
    out_shape = []
    for n in names:
        out_shape += [jax.ShapeDtypeStruct(params[n][0].shape, F32)] * 4
    outs = pl.pallas_call(
        body,
        name="adamw_small",
        in_specs=[vm] * (4 + len(flat)),
        out_specs=[vm] * len(out_shape),
        out_shape=out_shape,
        compiler_params=_params(),
    )(slab_a, slab_b, convw_g, wpool_g, *flat)
    return {n: outs[4 * i:4 * i + 4] for i, n in enumerate(names)}


def _place():
    return lax.axis_index("x"), lax.axis_index("y"), lax.axis_index("c")


def _other_chips(x, y):
    return [(1 - x, y), (x, 1 - y), (1 - x, 1 - y)]


def _chip_id(cx, cy):
    return 2 * cx + cy


def gather_weights(shards):
    n = len(shards)
    halved = [s.shape[0] % 32 == 0 for s in shards]

    def body(*refs):
        ins, outs = refs[:n], refs[n:2 * n]
        send_sems, recv_sems, loc_sems = refs[2 * n:]
        x, y, c = _place()
        me = _chip_id(x, y)
        chips = _other_chips(x, y)
        sib = (x, y, 1 - c)

        def part(w, chip, half):
            if not halved[w]:
                return outs[w].at[chip]
            rh = shards[w].shape[0] // 2
            return outs[w].at[chip, pl.ds(half * rh, rh), :]

        def src_part(w):
            if not halved[w]:
                return ins[w]
            rh = shards[w].shape[0] // 2
            return ins[w].at[pl.ds(c * rh, rh), :]

        def rcopy(w, k, src, dst, to):
            return pltpu.make_async_remote_copy(src_ref=src, dst_ref=dst, send_sem=send_sems.at[6 * w + k],
                                                recv_sem=recv_sems.at[6 * w + k], device_id=to, device_id_type=MESH)

        local = [pltpu.make_async_copy(ins[w], outs[w].at[me], loc_sems.at[w]) for w in range(n)]
        for cp in local:
            cp.start()
        first = []
        for w in range(n):
            for k, (cx, cy) in enumerate(chips):
                cp = rcopy(w, k, src_part(w), part(w, me, c), (cx, cy, c))
                cp.start()
                first.append(cp)
        passed = []
        for w in range(n):
            for k, (cx, cy) in enumerate(chips):
                blk = part(w, _chip_id(cx, cy), c)
                rcopy(w, k, blk, blk, (cx, cy, c)).wait_recv()
                if halved[w]:
                    cp = rcopy(w, 3 + k, blk, blk, sib)
                    cp.start()
                    passed.append(cp)
        for w in range(n):
            if halved[w]:
                for k, (cx, cy) in enumerate(chips):
                    blk = part(w, _chip_id(cx, cy), 1 - c)
                    rcopy(w, 3 + k, blk, blk, sib).wait_recv()
        for cp in first + passed:
            cp.wait_send()
        for cp in local:
            cp.wait()

    hbm = pl.BlockSpec(memory_space=pl.ANY)
    return pl.pallas_call(
        body,
        name="gather_weights",
        in_specs=[hbm] * n,
        out_specs=[hbm] * n,
        out_shape=[jax.ShapeDtypeStruct((N_CHIPS,) + s.shape, s.dtype) for s in shards],
        scratch_shapes=[pltpu.SemaphoreType.DMA((6 * n,)), pltpu.SemaphoreType.DMA((6 * n,)), pltpu.SemaphoreType.DMA((n,))],
        compiler_params=pltpu.CompilerParams(has_side_effects=True, vmem_limit_bytes=VMEM_LIMIT),
    )(*shards)


HBM_SPEC = pl.BlockSpec(memory_space=pltpu.HBM)
SEM_SPEC = pl.BlockSpec(memory_space=pltpu.SEMAPHORE)
ANY_SPEC = pl.BlockSpec(memory_space=pl.ANY)
EFFECT = pltpu.SideEffectType.DATAFLOW_SIDE_EFFECTING


def _hbm(t):
    return pltpu.with_memory_space_constraint(t, pltpu.HBM)


def _hbm_shapes(ts):
    return [pltpu.HBM(t.shape, t.dtype) for t in ts]


def _half_rows(ref, lead, half, rh):
    return ref.at[lead, pl.ds(half * rh, rh), :]


def gather_split_start(shards):
    n = len(shards)
    lands = [lax.empty((N_CHIPS,) + s.shape, s.dtype) for s in shards]

    def body(*refs):
        ins, land = refs[:n], refs[n:2 * n]
        send_sems, recv_sems = refs[2 * n], refs[2 * n + 1]
        token = refs[4 * n + 2]
        loc_sems = refs[4 * n + 3]
        x, y, c = _place()
        me = _chip_id(x, y)
        local = [pltpu.make_async_copy(ins[w], land[w].at[me], loc_sems.at[w]) for w in range(n)]
        for cp in local:
            cp.start()
        for w in range(n):
            rh = shards[w].shape[0] // 2
            for k, (cx, cy) in enumerate(_other_chips(x, y)):
                pltpu.make_async_remote_copy(src_ref=ins[w].at[pl.ds(c * rh, rh), :], dst_ref=_half_rows(land[w], me, c, rh),
                                             send_sem=send_sems.at[3 * w + k], recv_sem=recv_sems.at[3 * w + k],
                                             device_id=(cx, cy, c), device_id_type=MESH).start()
        for cp in local:
            cp.wait()
        token[...] = jnp.zeros_like(token)

    args = [_hbm(s) for s in shards] + [_hbm(l) for l in lands]
    outs = pl.pallas_call(
        body,
        name="gather_split_start",
        out_shape=[pltpu.SemaphoreType.DMA((3 * n,)), pltpu.SemaphoreType.DMA((3 * n,))] + _hbm_shapes(shards) + _hbm_shapes(lands)
        + [jax.ShapeDtypeStruct((8, 128), F32)],
        in_specs=[HBM_SPEC] * (2 * n),
        out_specs=[SEM_SPEC, SEM_SPEC] + [HBM_SPEC] * (2 * n) + [pl.BlockSpec(memory_space=pltpu.VMEM)],
        input_output_aliases={i: 2 + i for i in range(2 * n)},
        scratch_shapes=[pltpu.SemaphoreType.DMA((n,))],
        compiler_params=pltpu.CompilerParams(has_side_effects=EFFECT),
    )(*args)
    return outs[0], outs[1], list(outs[2:2 + n]), list(outs[2 + n:2 + 2 * n]), outs[-1]


def gather_split_mid(send_sems, recv_sems, shards, lands, after):
    n = len(shards)

    def body(*refs):
        ins, land = refs[:n], refs[n:2 * n]
        ssem, rsem = refs[2 * n], refs[2 * n + 1]
        fsend, frecv = refs[2 * n + 3], refs[2 * n + 4]
        x, y, c = _place()
        me = _chip_id(x, y)
        chips = _other_chips(x, y)
        for w in range(n):
            rh = shards[w].shape[0] // 2
            for k, (cx, cy) in enumerate(chips):
                got = _half_rows(land[w], _chip_id(cx, cy), c, rh)
                cp = pltpu.make_async_remote_copy(src_ref=ins[w].at[pl.ds(c * rh, rh), :], dst_ref=got, send_sem=ssem.at[3 * w + k],
                                                  recv_sem=rsem.at[3 * w + k], device_id=(cx, cy, c), device_id_type=MESH)
                cp.wait_send()
                cp.wait_recv()
        for w in range(n):
            rh = shards[w].shape[0] // 2
            for k, (cx, cy) in enumerate(chips):
                got = _half_rows(land[w], _chip_id(cx, cy), c, rh)
                pltpu.make_async_remote_copy(src_ref=got, dst_ref=got, send_sem=fsend.at[3 * w + k], recv_sem=frecv.at[3 * w + k],
                                             device_id=(x, y, 1 - c), device_id_type=MESH).start()

    outs = pl.pallas_call(
        body,
        name="gather_split_mid",
        out_shape=[pltpu.SemaphoreType.DMA((3 * n,)), pltpu.SemaphoreType.DMA((3 * n,))] + _hbm_shapes(lands),
        in_specs=[HBM_SPEC] * (2 * n) + [SEM_SPEC, SEM_SPEC, ANY_SPEC],
        out_specs=[SEM_SPEC, SEM_SPEC] + [HBM_SPEC] * n,
        input_output_aliases={n + i: 2 + i for i in range(n)},
        compiler_params=pltpu.CompilerParams(has_side_effects=EFFECT),
    )(*shards, *lands, send_sems, recv_sems, after)
    return outs[0], outs[1], list(outs[2:])


def gather_split_done(fsend, frecv, lands, after):
    n = len(lands)

    def body(*refs):
        land = refs[:n]
        ssem, rsem = refs[n], refs[n + 1]
        x, y, c = _place()
        for w in range(n):
            rh = lands[w].shape[1] // 2
            for k, (cx, cy) in enumerate(_other_chips(x, y)):
                sent = _half_rows(land[w], _chip_id(cx, cy), c, rh)
                got = _half_rows(land[w], _chip_id(cx, cy), 1 - c, rh)
                cp = pltpu.make_async_remote_copy(src_ref=sent, dst_ref=got, send_sem=ssem.at[3 * w + k], recv_sem=rsem.at[3 * w + k],
                                                  device_id=(x, y, 1 - c), device_id_type=MESH)
                cp.wait_send()
                cp.wait_recv()

    outs = pl.pallas_call(
        body,
        name="gather_split_done",
        out_shape=_hbm_shapes(lands),
        in_specs=[HBM_SPEC] * n + [SEM_SPEC, SEM_SPEC, ANY_SPEC],
        out_specs=[HBM_SPEC] * n,
        input_output_aliases={i: i for i in range(n)},
        compiler_params=pltpu.CompilerParams(has_side_effects=EFFECT),
    )(*lands, fsend, frecv, after)
    return list(outs)


def _flips():
    return [(fx, fy, fc) for fx in (0, 1) for fy in (0, 1) for fc in (0, 1)][1:]


def _flip(v, f):
    return v if f == 0 else 1 - v


def ada_mod(c3, w_ada, b_cols):
    CB = w_ada.shape[1]

    def body(c_ref, w_ref, b_ref, call_ref, mod_ref, modall, send_sems, recv_sems):
        x, y, c = _place()
        me_dev = 4 * x + 2 * y + c
        me = _chip_id(x, y)
        call_ref[me_dev] = c_ref[0]
        sends = []
        for k, (fx, fy, fc) in enumerate(_flips()):
            cp = pltpu.make_async_remote_copy(src_ref=c_ref.at[0], dst_ref=call_ref.at[me_dev], send_sem=send_sems.at[k],
                                              recv_sem=recv_sems.at[k],
                                              device_id=(_flip(x, fx), _flip(y, fy), _flip(c, fc)), device_id_type=MESH)
            cp.start()
            sends.append(cp)
        for k, (fx, fy, fc) in enumerate(_flips()):
            peer = 4 * _flip(x, fx) + 2 * _flip(y, fy) + _flip(c, fc)
            pltpu.make_async_remote_copy(src_ref=c_ref.at[0], dst_ref=call_ref.at[peer], send_sem=send_sems.at[k],
                                         recv_sem=recv_sems.at[k], device_id=(x, y, c), device_id_type=MESH).wait_recv()
        row = lax.broadcasted_iota(jnp.int32, (N_DEV, D_MODEL), 0)
        call = jnp.zeros((N_DEV, D_MODEL), F32)
        for dev in range(N_DEV):
            call = jnp.where(row == dev, call_ref[dev], call)
        act = call * jax.nn.sigmoid(call)
        modall[me] = jnp.dot(act, w_ref[...], preferred_element_type=F32, precision=lax.Precision.HIGHEST) + b_ref[...]
        for k, (cx, cy) in enumerate(_other_chips(x, y)):
            cp = pltpu.make_async_remote_copy(src_ref=modall.at[me], dst_ref=modall.at[me], send_sem=send_sems.at[7 + k],
                                              recv_sem=recv_sems.at[7 + k], device_id=(cx, cy, c), device_id_type=MESH)
            cp.start()
            sends.append(cp)
        for k, (cx, cy) in enumerate(_other_chips(x, y)):
            blk = modall.at[_chip_id(cx, cy)]
            pltpu.make_async_remote_copy(src_ref=blk, dst_ref=blk, send_sem=send_sems.at[7 + k], recv_sem=recv_sems.at[7 + k],
                                         device_id=(x, y, c), device_id_type=MESH).wait_recv()
        for cp in sends:
            cp.wait_send()
        mine = [modall[j, pl.ds(me_dev, 1), :] for j in range(N_CHIPS)]
        for r in range(6):
            pieces = []
            for h in range(2):
                pos = r * D_MODEL + h * 512
                pieces.append(mine[pos // CB][:, pos % CB:pos % CB + 512])
            mod_ref[r:r + 1, :] = jnp.concatenate(pieces, axis=1)

    vm = pl.BlockSpec(memory_space=pltpu.VMEM)
    return pl.pallas_call(
        body,
        name="ada_mod",
        in_specs=[vm] * 3,
        out_specs=[vm] * 2,
        out_shape=[jax.ShapeDtypeStruct((N_DEV, 1, D_MODEL), F32), jax.ShapeDtypeStruct((6, D_MODEL), F32)],
        scratch_shapes=[pltpu.VMEM((N_CHIPS, N_DEV, CB), F32), pltpu.SemaphoreType.DMA((10,)), pltpu.SemaphoreType.DMA((10,))],
        compiler_params=pltpu.CompilerParams(has_side_effects=True, vmem_limit_bytes=VMEM_LIMIT),
    )(c3, w_ada, b_cols)


def gather_small(blocks):
    n = len(blocks)

    def body(*refs):
        ins, outs = refs[:n], refs[n:2 * n]
        send_sems, recv_sems = refs[2 * n:]
        x, y, c = _place()
        sib = (x, y, 1 - c)
        chips = _other_chips(x, y)

        def dev(px, py, pc):
            return 4 * px + 2 * py + pc

        def cp(w, k, src, block_dev, to):
            return pltpu.make_async_remote_copy(src_ref=src, dst_ref=outs[w].at[block_dev], send_sem=send_sems.at[7 * w + k],
                                                recv_sem=recv_sems.at[7 * w + k], device_id=to, device_id_type=MESH)

        me = dev(x, y, c)
        started = []
        for w in range(n):
            outs[w][me] = ins[w][...]
            t = cp(w, 0, ins[w], me, sib)
            t.start()
            started.append(t)
            for k, (cx, cy) in enumerate(chips):
                t = cp(w, 1 + k, ins[w], me, (cx, cy, c))
                t.start()
                started.append(t)
        for w in range(n):
            for k, (cx, cy) in enumerate(chips):
                b = dev(cx, cy, c)
                cp(w, 1 + k, outs[w].at[b], b, (x, y, c)).wait_recv()
                t = cp(w, 4 + k, outs[w].at[b], b, sib)
                t.start()
                started.append(t)
        for w in range(n):
            b = dev(x, y, 1 - c)
            cp(w, 0, outs[w].at[b], b, (x, y, c)).wait_recv()
            for k, (cx, cy) in enumerate(chips):
                b = dev(cx, cy, 1 - c)
                cp(w, 4 + k, outs[w].at[b], b, (x, y, c)).wait_recv()
        for t in started:
            t.wait_send()

    vm = pl.BlockSpec(memory_space=pltpu.VMEM)
    return pl.pallas_call(
        body,
        name="gather_small",
        in_specs=[vm] * n,
        out_specs=[vm] * n,
        out_shape=[jax.ShapeDtypeStruct((N_DEV,) + b.shape, b.dtype) for b in blocks],
        scratch_shapes=[pltpu.SemaphoreType.DMA((7 * n,)), pltpu.SemaphoreType.DMA((7 * n,))],
        compiler_params=pltpu.CompilerParams(has_side_effects=True, vmem_limit_bytes=VMEM_LIMIT),
    )(*blocks)


def reduce_scatter_grads(grads, chunk_rows):
    n = len(grads)
    shapes = [g.shape[1:] for g in grads]
    halves = [s[0] // 2 for s in shapes]

    def body(*refs):
        gin = refs[:n]
        gout = refs[n:2 * n]
        sibbuf = refs[2 * n:3 * n]
        rest = refs[3 * n:]
        rbuf = rest[:n]
        pown = rest[n:2 * n]
        stage_a, stage_b, stage_o, stage_f = rest[2 * n:2 * n + 4]
        sib_send, sib_recv, ici_send, ici_recv, fin_send, fin_recv, ld_sems, st_sems = rest[2 * n + 4:]
        x, y, c = _place()
        me = _chip_id(x, y)
        chips = _other_chips(x, y)
        sib = (x, y, 1 - c)

        to_sib = []
        for w in range(n):
            rh = halves[w]
            cp = pltpu.make_async_remote_copy(src_ref=gin[w].at[:, pl.ds((1 - c) * rh, rh), :], dst_ref=sibbuf[w],
                                              send_sem=sib_send.at[w], recv_sem=sib_recv.at[w], device_id=sib,
                                              device_id_type=MESH)
            cp.start()
            to_sib.append(cp)

        sent = []
        for w in range(n):
            rh, cw = halves[w], shapes[w][1]
            ch = chunk_rows[w]
            to_sib[w].wait_recv()
            for k in range(4):
                chip = me if k == 3 else _chip_id(*chips[k])
                for r0 in range(0, rh, ch):
                    la = pltpu.make_async_copy(gin[w].at[chip, pl.ds(c * rh + r0, ch), :], stage_a.at[0:ch, 0:cw], ld_sems.at[0])
                    lb = pltpu.make_async_copy(sibbuf[w].at[chip, pl.ds(r0, ch), :], stage_b.at[0:ch, 0:cw], ld_sems.at[1])
                    la.start()
                    lb.start()
                    la.wait()
                    lb.wait()
                    tot = stage_a[0:ch, 0:cw] + stage_b[0:ch, 0:cw]
                    if k == 3:
                        pown[w][r0:r0 + ch, :] = tot
                    else:
                        stage_o[0:ch, 0:cw] = tot.astype(BF16)
                        cx, cy = chips[k]
                        cp = pltpu.make_async_remote_copy(src_ref=stage_o.at[0:ch, 0:cw], dst_ref=rbuf[w].at[k, r0:r0 + ch, :],
                                                          send_sem=ici_send.at[3 * w + k], recv_sem=ici_recv.at[3 * w + k],
                                                          device_id=(cx, cy, c), device_id_type=MESH)
                        cp.start()
                        cp.wait_send()
            sent.append(w)

        fin = []
        for w in range(n):
            rh, cw = halves[w], shapes[w][1]
            for k in range(3):
                whole = rbuf[w].at[k]
                pltpu.make_async_remote_copy(src_ref=whole, dst_ref=whole, send_sem=ici_send.at[3 * w + k],
                                             recv_sem=ici_recv.at[3 * w + k], device_id=(x, y, c),
                                             device_id_type=MESH).wait_recv()
            pown[w][...] = ((pown[w][...] + rbuf[w][0].astype(F32)) + rbuf[w][1].astype(F32)) + rbuf[w][2].astype(F32)
            mine = gout[w].at[pl.ds(c * rh, rh), :]
            st = pltpu.make_async_copy(pown[w], mine, st_sems.at[w])
            st.start()
            cp = pltpu.make_async_remote_copy(src_ref=pown[w], dst_ref=mine, send_sem=fin_send.at[w], recv_sem=fin_recv.at[w],
                                              device_id=sib, device_id_type=MESH)
            cp.start()
            fin.append((st, cp))
        for w in range(n):
            rh = halves[w]
            theirs = gout[w].at[pl.ds((1 - c) * rh, rh), :]
            pltpu.make_async_remote_copy(src_ref=theirs, dst_ref=theirs, send_sem=fin_send.at[w], recv_sem=fin_recv.at[w],
                                         device_id=(x, y, c), device_id_type=MESH).wait_recv()
        for cp in to_sib:
            cp.wait_send()
        for st, cp in fin:
            st.wait()
            cp.wait_send()

    hbm = pl.BlockSpec(memory_space=pl.ANY)
    max_ch = max(chunk_rows)
    max_c = max(s[1] for s in shapes)
    outs = pl.pallas_call(
        body,
        name="reduce_scatter_grads",
        in_specs=[hbm] * n,
        out_specs=[hbm] * (2 * n),
        out_shape=[jax.ShapeDtypeStruct(s, F32) for s in shapes]
        + [jax.ShapeDtypeStruct((N_CHIPS, h, s[1]), F32) for h, s in zip(halves, shapes)],
        scratch_shapes=[pltpu.VMEM((3, h, s[1]), BF16) for h, s in zip(halves, shapes)]
        + [pltpu.VMEM((h, s[1]), F32) for h, s in zip(halves, shapes)]
        + [pltpu.VMEM((max_ch, max_c), F32), pltpu.VMEM((max_ch, max_c), F32), pltpu.VMEM((max_ch, max_c), BF16),
           pltpu.VMEM((8, 128), F32)]
        + [pltpu.SemaphoreType.DMA((n,)), pltpu.SemaphoreType.DMA((n,)), pltpu.SemaphoreType.DMA((3 * n,)),
           pltpu.SemaphoreType.DMA((3 * n,)), pltpu.SemaphoreType.DMA((n,)), pltpu.SemaphoreType.DMA((n,)),
           pltpu.SemaphoreType.DMA((2,)), pltpu.SemaphoreType.DMA((n,))],
        compiler_params=pltpu.CompilerParams(has_side_effects=True, vmem_limit_bytes=VMEM_LIMIT),
    )(*grads)
    return outs[:n]


def _rope_tables(positions):
    inv_freq = ROPE_THETA ** (-jnp.arange(0, ROT_DIM, 2, dtype=F32) / ROT_DIM)
    ang = positions.astype(F32)[:, None] * inv_freq
    cos, sin = jnp.cos(ang), jnp.sin(ang)
    S = positions.shape[0]
    one, zero = jnp.ones((S, 48), F32), jnp.zeros((S, 48), F32)
    z8 = jnp.zeros((S, 8), F32)
    tc = jnp.concatenate([cos, cos, one], axis=1)
    tsa = jnp.concatenate([z8, sin, zero], axis=1)
    tsb = jnp.concatenate([-sin, z8, zero], axis=1)
    return tuple(jnp.tile(t, (1, 2)) for t in (tc, tsa, tsb))


def _block_diag(w_pool):
    wbd = jnp.zeros((POOL_W, POOL_W), F32)
    for gi in range(4):
        wbd = wbd.at[gi * 64:(gi + 1) * 64, gi * 64:(gi + 1) * 64].set(w_pool[gi])
    return wbd


def kernel(x, c, positions, w_ada, b_ada, g_pre_mix, g_post_mix, g_pre_ffn, g_post_ffn, w_in, w_pool, b_pool, pool_scale, w_out, w_up, conv_w, conv_b, w_down, loss_target, m_w_ada, m_b_ada, m_g_pre_mix, m_g_post_mix, m_g_pre_ffn, m_g_post_ffn, m_w_in, m_w_pool, m_b_pool, m_pool_scale, m_w_out, m_w_up, m_conv_w, m_conv_b, m_w_down, v_w_ada, v_b_ada, v_g_pre_mix, v_g_post_mix, v_g_pre_ffn, v_g_post_ffn, v_w_in, v_w_pool, v_b_pool, v_pool_scale, v_w_out, v_w_up, v_conv_w, v_conv_b, v_w_down):
    xi, yi = lax.axis_index("x"), lax.axis_index("y")
    chip = 2 * xi + yi
    x2, tgt = x[0], loss_target[0]
    S = x2.shape[0]

    w_in_g, w_out_g, conv_w_g = gather_weights([w_in[0].astype(BF16), w_out[0].astype(BF16), conv_w[0]])
    gs_send, gs_recv, ffn_shards, ffn_lands, token = gather_split_start([w_up[0].astype(BF16), w_down[0].astype(BF16)])
    conv_w_f = jnp.transpose(conv_w_g, (1, 0, 2)).reshape(3, D_FF)
    cb_ada = w_ada.shape[2]
    b_cols = lax.dynamic_slice(b_ada, (0, chip * cb_ada), (1, cb_ada)) + token[0:1, 0:1]
    c_all, mod6 = ada_mod(c.reshape(1, 1, D_MODEL), w_ada[0], b_cols)
    tc, tsa, tsb = _rope_tables(positions[0])
    wbd = _block_diag(w_pool[0]).astype(BF16)
    b_pool2, scale2 = b_pool.reshape(1, POOL_W), pool_scale

    h1, u, qkv9 = inproj_fwd(x2, g_pre_mix, mod6, w_in_g, tc, tsa, tsb)
    o_l = [attn_fwd(qkv9, gi, d) for gi, d in enumerate(DILATIONS)]
    mixed, pool = pool_fwd(u, wbd, b_pool2, scale2)
    gf_send, gf_recv, ffn_lands = gather_split_mid(gs_send, gs_recv, ffn_shards, ffn_lands, pool)
    cat, lse, y1, x1, h2 = outproj_fwd([o for o, _ in o_l] + [l for _, l in o_l], pool, x2, w_out_g, g_post_mix, g_pre_ffn, mod6)
    w_up_g, w_down_g = gather_split_done(gf_send, gf_recv, ffn_lands, h2)
    w_down_f = w_down_g.reshape(D_FF, D_MODEL)
    gate, val = up_fwd(h2, w_up_g)
    a, dy2, dout, loss_v, d_gt_f, d_g_post_ffn = down_fwd(gate, val, conv_w_f, conv_b, w_down_f, x1, tgt, g_post_ffn, mod6)

    dgc, dval, d_conv_w, d_conv_b = down_bwd(dy2, w_down_f, gate, val, conv_w_f, conv_b)
    dw_down = dw_matmul(a, dy2, 2, (HALF_FF, D_MODEL), HALF_FF, D_MODEL, True, "dw_down")
    dgate, dx1, dy1, d_sh_f, d_sc_f, d_g_pre_ffn, d_gt_m, d_g_post_mix = up_bwd(
        dgc, dval, conv_w_f, w_up_g, x1, dout, y1, g_pre_ffn, g_post_mix, mod6)
    dw_up = dw_matmul(h2, dgate, 4, (D_MODEL, HALF_FF), D_MODEL, HALF_FF, False, "dw_up_gate", n_blk=2)
    dw_up = dw_matmul(h2, dval, 4, (D_MODEL, HALF_FF), D_MODEL, HALF_FF, False, "dw_up_val", prev=dw_up, blk_off=2, n_blk=2)
    dpool, dattn, delta = outproj_bwd(dy1, w_out_g, cat)
    dw_out = dw_matmul(cat, dy1, 4, (512, 256), 512, 256, False, "dw_out")
    du, d_wbd, d_b_pool, d_scale = pool_bwd(dpool, mixed, wbd, b_pool2, scale2)
    dqkv = [attn_bwd(qkv9, dattn, lse, delta, gi, d) for gi, d in enumerate(DILATIONS)]
    dproj, grad_x, d_sh_m, d_sc_m, d_g_pre_mix = inproj_bwd(dqkv, du, x2, dx1, w_in_g, g_pre_mix, mod6, tc, tsa, tsb)
    dw_in = dw_matmul(h1, dproj, 4, (D_MODEL, 640), D_MODEL, 640, False, "dw_in")

    g_w_in, g_w_out, g_w_up, g_w_down = reduce_scatter_grads(
        [dw_in, dw_out, dw_up, dw_down.reshape(N_CHIPS, D_FF // N_CHIPS, D_MODEL)], [256, 256, 128, 176])
    z1 = jnp.zeros((1, D_MODEL), F32)
    slab_a = jnp.concatenate(
        [d_sh_m, d_sc_m, d_gt_m, d_sh_f, d_sc_f, d_gt_f, d_g_pre_mix, d_g_post_mix, d_g_pre_ffn, d_g_post_ffn,
         jnp.concatenate([d_b_pool, d_scale, jnp.zeros((1, 512), F32)], axis=1)] + [z1] * 5, axis=0)
    slab_b = jnp.concatenate([d_conv_w, d_conv_b, jnp.zeros((4, D_FF), F32)], axis=0)
    d_wpool = jnp.concatenate([d_wbd[gi * 64:(gi + 1) * 64, gi * 64:(gi + 1) * 64] for gi in range(4)], axis=0)
    slab_a_g, slab_b_g, wpool_g = gather_small([slab_a, slab_b, d_wpool])
    cw_cols = conv_w.shape[2]
    convw_g = lax.dynamic_slice(slab_b_g, (0, 0, chip * cw_cols), (N_DEV, 3, cw_cols))
    dmod_cols = lax.dynamic_slice(slab_a_g[:, :6, :].reshape(N_DEV, 6 * D_MODEL), (0, chip * cb_ada), (N_DEV, cb_ada))

    res = {}
    g_ada, d_ada, m_ada, v_ada = adamw_ada(c_all.reshape(N_DEV, D_MODEL).T, dmod_cols, w_ada[0], m_w_ada[0], v_w_ada[0])
    res["w_ada"] = (g_ada[None], d_ada[None], m_ada[None], v_ada[None])
    for name, w, g, m, v, tr in (("w_in", w_in, g_w_in, m_w_in, v_w_in, 256), ("w_out", w_out, g_w_out, m_w_out, v_w_out, 256),
                                 ("w_up", w_up, g_w_up, m_w_up, v_w_up, 256), ("w_down", w_down, g_w_down, m_w_down, v_w_down, 352)):
        d_, m_, v_ = adamw_rows(w[0], g, m[0], v[0], tr, "adamw_" + name)
        res[name] = (g[None], d_[None], m_[None], v_[None])
    flat = lambda t: t.reshape(1, POOL_W)
    wp = lambda t: t.reshape(POOL_W, 64)
    small = adamw_small(slab_a_g, slab_b_g, convw_g, wpool_g, {
        "b_ada": (b_ada, m_b_ada, v_b_ada), "g_pre_mix": (g_pre_mix, m_g_pre_mix, v_g_pre_mix),
        "g_post_mix": (g_post_mix, m_g_post_mix, v_g_post_mix), "g_pre_ffn": (g_pre_ffn, m_g_pre_ffn, v_g_pre_ffn),
        "g_post_ffn": (g_post_ffn, m_g_post_ffn, v_g_post_ffn), "b_pool": (flat(b_pool), flat(m_b_pool), flat(v_b_pool)),
        "pool_scale": (pool_scale, m_pool_scale, v_pool_scale), "conv_b": (conv_b, m_conv_b, v_conv_b),
        "conv_w": (conv_w[0], m_conv_w[0], v_conv_w[0]), "w_pool": (wp(w_pool), wp(m_w_pool), wp(v_w_pool))})
    for name in ("b_ada", "g_pre_mix", "g_post_mix", "g_pre_ffn", "g_post_ffn", "pool_scale", "conv_b"):
        res[name] = tuple(small[name])
    res["b_pool"] = tuple(t.reshape(1, 4, 64) for t in small["b_pool"])
    res["conv_w"] = tuple(t[None] for t in small["conv_w"])
    res["w_pool"] = tuple(t.reshape(1, 4, 64, 64) for t in small["w_pool"])

    loss = lax.psum(loss_v[0, 0], ("x", "y", "c"))
    order = ["w_ada", "b_ada", "g_pre_mix", "g_post_mix", "g_pre_ffn", "g_post_ffn", "w_in", "w_pool", "b_pool", "pool_scale",
             "w_out", "w_up", "conv_w", "conv_b", "w_down"]
    outs = [loss, grad_x[None]]
    for k in range(4):
        outs += [res[n][k] for n in order]
    return tuple(outs)
```

```python
import functools
import math

import jax
import jax.numpy as jnp
from jax import lax
from jax.experimental import pallas as pl
from jax.experimental.pallas import tpu as pltpu

F32 = jnp.float32
BF16 = jnp.bfloat16
MESH = pl.DeviceIdType.MESH

D_MODEL = 1024
HEAD_DIM = 64
POOL_W = 256
GROUP_W = 256
DILATIONS = (1, 4, 16)
ATT_BLOCK = 128
IN_W = 2560
D_FF = 2816
HALF_FF = 1408
ROT_DIM = 16
ROPE_THETA = 500000.0
NORM_EPS = 1e-6
N_CHIPS = 4
N_DEV = 8
NEG = -1e30

ADAM_LR = 0.001
ADAM_B1 = 0.9
ADAM_B2 = 0.999
ADAM_EPS = 1e-08
ADAM_WD = 0.01
ADAM_STEP = 10

VMEM_LIMIT = 56 * 1024 * 1024

NT = (((1,), (1,)), ((), ()))
TN = (((0,), (0,)), ((), ()))


def _params(n_grid=0, **kw):
    sem = ("arbitrary",) * n_grid if n_grid else None
    return pltpu.CompilerParams(dimension_semantics=sem, vmem_limit_bytes=VMEM_LIMIT, **kw)


def _full(shape):
    nd = len(shape)
    return pl.BlockSpec(tuple(shape), lambda *_: (0,) * nd)


def _rows(tm, ncol):
    return pl.BlockSpec((tm, ncol), lambda i: (i, 0))


def _acc(ref, val):
    @pl.when(pl.program_id(0) == 0)
    def _():
        ref[...] = jnp.zeros_like(ref)

    ref[...] += val


def _colsum(v):
    return jnp.sum(v, axis=0, keepdims=True)


def _rope128(t, cs, sa, sb, sign):
    return t * cs + sign * (pltpu.roll(t, 8, 1) * sa + pltpu.roll(t, 120, 1) * sb)


def _gelu(z):
    c = math.sqrt(2.0 / math.pi)
    t = jnp.tanh(c * (z + 0.044715 * (z * z * z)))
    return 0.5 * z * (1.0 + t), t


def _gelu_grad(z, t):
    c = math.sqrt(2.0 / math.pi)
    return 0.5 * (1.0 + t) + 0.5 * z * (1.0 - t * t) * (c * (1.0 + 3.0 * 0.044715 * (z * z)))


def _conv_taps(gate, halo, first):
    row = lax.broadcasted_iota(jnp.int32, gate.shape, 0)
    halo = jnp.where(first, 0.0, halo)
    p1 = halo[15:16, :]
    p2 = halo[14:15, :]
    g1 = jnp.where(row == 0, p1, pltpu.roll(gate, 1, 0))
    g2 = jnp.where(row == 0, p2, jnp.where(row == 1, p1, pltpu.roll(gate, 2, 0)))
    return g1, g2


def inproj_fwd(x, g, mod6, w_in_g, tc, tsa, tsb, tm=512):
    S = x.shape[0]

    def body(x_ref, g_ref, mod_ref, w_ref, tc_ref, tsa_ref, tsb_ref, h_ref, u_ref, qkv_ref):
        xv = x_ref[...]
        rstd = lax.rsqrt(jnp.mean(xv * xv, axis=-1, keepdims=True) + NORM_EPS)
        h = ((xv * rstd) * g_ref[...]) * (1.0 + mod_ref[1:2, :]) + mod_ref[0:1, :]
        hb = h.astype(BF16)
        h_ref[...] = hb
        cs, sa, sb = tc_ref[...], tsa_ref[...], tsb_ref[...]
        for j in range(N_CHIPS):
            res = jnp.dot(hb, w_ref[j], preferred_element_type=F32)
            for t in range(5):
                sp = 5 * j + t
                piece, half = sp // 2, sp % 2
                blk = res[:, t * 128:(t + 1) * 128]
                lanes = slice(half * 128, (half + 1) * 128)
                if piece == 0:
                    u_ref[:, lanes] = blk
                else:
                    i9 = piece - 1
                    if i9 < 3:
                        blk = _rope128(blk, cs, sa, sb, 1.0) * (HEAD_DIM ** -0.5)
                    elif i9 < 6:
                        blk = _rope128(blk, cs, sa, sb, 1.0)
                    qkv_ref[i9, :, lanes] = blk.astype(BF16)

    return pl.pallas_call(
        body,
        name="inproj_fwd",
        grid=(S // tm,),
        in_specs=[_rows(tm, D_MODEL), _full((1, D_MODEL)), _full((6, D_MODEL)), _full(w_in_g.shape),
                  _rows(tm, 128), _rows(tm, 128), _rows(tm, 128)],
        out_specs=[_rows(tm, D_MODEL), _rows(tm, POOL_W), pl.BlockSpec((9, tm, GROUP_W), lambda i: (0, i, 0))],
        out_shape=[jax.ShapeDtypeStruct((S, D_MODEL), BF16), jax.ShapeDtypeStruct((S, POOL_W), F32),
                   jax.ShapeDtypeStruct((9, S, GROUP_W), BF16)],
        compiler_params=_params(1),
    )(x, g, mod6, w_in_g, tc, tsa, tsb)


def _attn_masks():
    row = lax.broadcasted_iota(jnp.int32, (ATT_BLOCK, 2 * ATT_BLOCK), 0)
    col = lax.broadcasted_iota(jnp.int32, (ATT_BLOCK, 2 * ATT_BLOCK), 1)
    band = (col >= row) & (col <= row + ATT_BLOCK)
    lane = lax.broadcasted_iota(jnp.int32, (ATT_BLOCK, 128), 1)
    return band, col, lane < HEAD_DIM


def attn_fwd(qkv9, gi, d):
    S = qkv9.shape[1]
    L = S // d
    nb = L // ATT_BLOCK
    qv = qkv9.reshape(9, L, d * GROUP_W)

    def body(q_ref, k_ref, v_ref, o_ref, l_ref, kpad, vpad):
        kpad[0:ATT_BLOCK, :] = jnp.zeros((ATT_BLOCK, GROUP_W), BF16)
        vpad[0:ATT_BLOCK, :] = jnp.zeros((ATT_BLOCK, GROUP_W), BF16)
        kpad[ATT_BLOCK:, :] = k_ref[...]
        vpad[ATT_BLOCK:, :] = v_ref[...]
        band, col, lo = _attn_masks()

        def step(n, carry):
            r0 = pl.multiple_of(n * ATT_BLOCK, ATT_BLOCK)
            valid = band & ((col >= ATT_BLOCK) | (n > 0))
            qb = q_ref[pl.ds(r0, ATT_BLOCK), :]
            kb = kpad[pl.ds(r0, 2 * ATT_BLOCK), :]
            vb = vpad[pl.ds(r0, 2 * ATT_BLOCK), :]
            for pair in range(2):
                lanes = slice(pair * 128, (pair + 1) * 128)
                qp, kp, vp = qb[:, lanes], kb[:, lanes], vb[:, lanes]
                outs, lses = [], []
                for hh in range(2):
                    sel = lo if hh == 0 else jnp.logical_not(lo)
                    qm = jnp.where(sel, qp, jnp.zeros_like(qp))
                    s = lax.dot_general(qm, kp, NT, preferred_element_type=F32)
                    s = jnp.where(valid, s, NEG)
                    m = jnp.max(s, axis=1, keepdims=True)
                    p = jnp.exp(s - m)
                    den = jnp.sum(p, axis=1, keepdims=True)
                    pv = jnp.dot(p.astype(BF16), vp, preferred_element_type=F32)
                    outs.append(pv / den)
                    lses.append(m + jnp.log(den))
                o_ref[pl.ds(r0, ATT_BLOCK), lanes] = jnp.where(lo, outs[0], outs[1])
                l_ref[pl.ds(r0, ATT_BLOCK), lanes] = jnp.where(lo, lses[0], lses[1])
            return carry

        lax.fori_loop(0, nb, step, 0)

    spec = lambda lead: pl.BlockSpec((None, L, GROUP_W), lambda r: (lead, 0, r))
    o, l = pl.pallas_call(
        body,
        name=f"attn_fwd_d{d}",
        grid=(d,),
        in_specs=[spec(gi), spec(3 + gi), spec(6 + gi)],
        out_specs=[pl.BlockSpec((L, GROUP_W), lambda r: (0, r))] * 2,
        out_shape=[jax.ShapeDtypeStruct((L, d * GROUP_W), F32)] * 2,
        scratch_shapes=[pltpu.VMEM((L + ATT_BLOCK, GROUP_W), BF16)] * 2,
        compiler_params=_params(1),
    )(qv, qv, qv)
    return o.reshape(S, GROUP_W), l.reshape(S, GROUP_W)


def _pool_lane_windows(shape):
    lane = lax.broadcasted_iota(jnp.int32, shape, 1)
    return lane, jnp.where(lane < 64, 2, jnp.where(lane < 128, 4, jnp.where(lane < 192, 8, 16)))


def pool_fwd(u, wbd, b, scale):
    S = u.shape[0]

    def body(u_ref, w_ref, b_ref, s_ref, mixed_ref, out_ref):
        uv = u_ref[...]
        row = lax.broadcasted_iota(jnp.int32, uv.shape, 0)
        lane, win = _pool_lane_windows(uv.shape)

        def shift(a, k):
            return jnp.where(row >= k, pltpu.roll(a, k, 0), 0.0)

        s2 = uv + shift(uv, 1)
        s4 = s2 + shift(s2, 2)
        s8 = s4 + shift(s4, 4)
        s16 = s8 + shift(s8, 8)
        tsum = jnp.where(lane < 64, s2, jnp.where(lane < 128, s4, jnp.where(lane < 192, s8, s16)))
        cnt = jnp.minimum(row + 1, win).astype(F32)
        mb = (tsum / cnt - uv).astype(BF16)
        mixed_ref[...] = mb
        y = jnp.dot(mb, w_ref[...], preferred_element_type=F32) + b_ref[...]
        out_ref[...] = (y * s_ref[...]).astype(BF16)

    vm = pl.BlockSpec(memory_space=pltpu.VMEM)
    return pl.pallas_call(
        body,
        name="pool_fwd",
        in_specs=[vm] * 4,
        out_specs=[vm] * 2,
        out_shape=[jax.ShapeDtypeStruct((S, POOL_W), BF16)] * 2,
        compiler_params=_params(),
    )(u, wbd, b, scale)


def outproj_fwd(o_l, pool, x, w_out_g, g_post, g_pre, mod6, tm=512):
    S = x.shape[0]

    def body(o0, o1, o2, l0, l1, l2, pool_ref, x_ref, w_ref, gpost_ref, gpre_ref, mod_ref,
             cat_ref, lse_ref, y1_ref, x1_ref, h2_ref):
        a, b, c = l0[...], l1[...], l2[...]
        m = jnp.maximum(jnp.maximum(a, b), c)
        e0, e1, e2 = jnp.exp(a - m), jnp.exp(b - m), jnp.exp(c - m)
        z = e0 + e1 + e2
        lse_ref[...] = m + jnp.log(z)
        attn = (e0 * o0[...] + e1 * o1[...] + e2 * o2[...]) / z
        cat = jnp.concatenate([pool_ref[...], attn.astype(BF16)], axis=1)
        cat_ref[...] = cat
        y1 = jnp.concatenate([jnp.dot(cat, w_ref[j], preferred_element_type=F32) for j in range(N_CHIPS)], axis=1)
        y1_ref[...] = y1
        rstd = lax.rsqrt(jnp.mean(y1 * y1, axis=-1, keepdims=True) + NORM_EPS)
        x1 = x_ref[...] + mod_ref[2:3, :] * ((y1 * rstd) * gpost_ref[...])
        x1_ref[...] = x1
        rstd2 = lax.rsqrt(jnp.mean(x1 * x1, axis=-1, keepdims=True) + NORM_EPS)
        h2 = ((x1 * rstd2) * gpre_ref[...]) * (1.0 + mod_ref[4:5, :]) + mod_ref[3:4, :]
        h2_ref[...] = h2.astype(BF16)

    t256 = _rows(tm, GROUP_W)
    return pl.pallas_call(
        body,
        name="outproj_fwd",
        grid=(S // tm,),
        in_specs=[t256] * 7 + [_rows(tm, D_MODEL), _full(w_out_g.shape), _full((1, D_MODEL)), _full((1, D_MODEL)),
                               _full((6, D_MODEL))],
        out_specs=[_rows(tm, 512), t256, _rows(tm, D_MODEL), _rows(tm, D_MODEL), _rows(tm, D_MODEL)],
        out_shape=[jax.ShapeDtypeStruct((S, 512), BF16), jax.ShapeDtypeStruct((S, GROUP_W), F32),
                   jax.ShapeDtypeStruct((S, D_MODEL), F32), jax.ShapeDtypeStruct((S, D_MODEL), F32),
                   jax.ShapeDtypeStruct((S, D_MODEL), BF16)],
        compiler_params=_params(1),
    )(*o_l, pool, x, w_out_g, g_post, g_pre, mod6)


def up_fwd(h2, w_up_g, tm=512):
    S = h2.shape[0]

    def body(h_ref, w_ref, gate_ref, val_ref):
        hb = h_ref[...]
        for j in range(N_CHIPS):
            res = jnp.dot(hb, w_ref[j], preferred_element_type=F32).astype(BF16)
            dst = gate_ref if j < 2 else val_ref
            dst[:, (j % 2) * HALF_FF:(j % 2 + 1) * HALF_FF] = res

    return pl.pallas_call(
        body,
        name="up_fwd",
        grid=(S // tm,),
        in_specs=[_rows(tm, D_MODEL), _full(w_up_g.shape)],
        out_specs=[_rows(tm, D_FF)] * 2,
        out_shape=[jax.ShapeDtypeStruct((S, D_FF), BF16)] * 2,
        compiler_params=_params(1),
    )(h2, w_up_g)


def _halo_prev(tm, ncol):
    return pl.BlockSpec((16, ncol), lambda i: (jnp.maximum(i * (tm // 16) - 1, 0), 0))


def down_fwd(gate, val, conv_w, conv_b, w_down, x1, target, g_post, mod6, tm=256):
    S = x1.shape[0]

    def body(gate_ref, halo_ref, val_ref, cw_ref, cb_ref, w_ref, x1_ref, tgt_ref, g_ref, mod_ref,
             a_ref, dy2_ref, dout_ref, loss_ref, dgt_ref, dg_ref):
        first = pl.program_id(0) == 0
        y2 = jnp.zeros((tm, D_MODEL), F32)
        for ch in range(2):
            cols = slice(ch * HALF_FF, (ch + 1) * HALF_FF)
            gt = gate_ref[:, cols].astype(F32)
            g1, g2 = _conv_taps(gt, halo_ref[:, cols].astype(F32), first)
            gc = g2 * cw_ref[0:1, cols] + g1 * cw_ref[1:2, cols] + gt * cw_ref[2:3, cols] + cb_ref[:, cols]
            ge, _ = _gelu(gc)
            ab = (ge * val_ref[:, cols].astype(F32)).astype(BF16)
            a_ref[:, cols] = ab
            y2 = y2 + jnp.dot(ab, w_ref[cols, :], preferred_element_type=F32)
        rstd = lax.rsqrt(jnp.mean(y2 * y2, axis=-1, keepdims=True) + NORM_EPS)
        y2n = y2 * rstd
        gv = g_ref[...]
        gtf = mod_ref[5:6, :]
        r2 = y2n * gv
        diff = (x1_ref[...] + gtf * r2) - tgt_ref[...]
        _acc(loss_ref, jnp.zeros((1, 128), F32) + 0.5 * jnp.sum(diff * diff) * (1.0 / D_MODEL))
        dout = diff * (1.0 / D_MODEL)
        dout_ref[...] = dout
        _acc(dgt_ref, _colsum(dout * r2))
        dr2 = dout * gtf
        _acc(dg_ref, _colsum(dr2 * y2n))
        dyn = dr2 * gv
        dy2 = rstd * (dyn - y2n * jnp.mean(dyn * y2n, axis=-1, keepdims=True))
        dy2_ref[...] = dy2.astype(BF16)

    vec = _full((1, D_MODEL))
    return pl.pallas_call(
        body,
        name="down_fwd",
        grid=(S // tm,),
        in_specs=[_rows(tm, D_FF), _halo_prev(tm, D_FF), _rows(tm, D_FF), _full((3, D_FF)), _full((1, D_FF)),
                  _full((D_FF, D_MODEL)), _rows(tm, D_MODEL), _rows(tm, D_MODEL), vec, _full((6, D_MODEL))],
        out_specs=[_rows(tm, D_FF), _rows(tm, D_MODEL), _rows(tm, D_MODEL), _full((1, 128)), vec, vec],
        out_shape=[jax.ShapeDtypeStruct((S, D_FF), BF16), jax.ShapeDtypeStruct((S, D_MODEL), BF16),
                   jax.ShapeDtypeStruct((S, D_MODEL), F32), jax.ShapeDtypeStruct((1, 128), F32),
                   jax.ShapeDtypeStruct((1, D_MODEL), F32), jax.ShapeDtypeStruct((1, D_MODEL), F32)],
        compiler_params=_params(1),
    )(gate, gate, val, conv_w, conv_b, w_down, x1, target, g_post, mod6)


def down_bwd(dy2, w_down, gate, val, conv_w, conv_b, tm=256):
    S = dy2.shape[0]

    def body(dy_ref, w_ref, gate_ref, halo_ref, val_ref, cw_ref, cb_ref, dgc_ref, dval_ref, dcw_ref, dcb_ref):
        first = pl.program_id(0) == 0

        @pl.when(first)
        def _():
            dcw_ref[...] = jnp.zeros_like(dcw_ref)
            dcb_ref[...] = jnp.zeros_like(dcb_ref)

        dyb = dy_ref[...]
        for ch in range(2):
            cols = slice(ch * HALF_FF, (ch + 1) * HALF_FF)
            da = lax.dot_general(dyb, w_ref[cols, :], NT, preferred_element_type=F32)
            gt = gate_ref[:, cols].astype(F32)
            g1, g2 = _conv_taps(gt, halo_ref[:, cols].astype(F32), first)
            gc = g2 * cw_ref[0:1, cols] + g1 * cw_ref[1:2, cols] + gt * cw_ref[2:3, cols] + cb_ref[:, cols]
            ge, th = _gelu(gc)
            dgc = da * val_ref[:, cols].astype(F32) * _gelu_grad(gc, th)
            dgc_ref[:, cols] = dgc.astype(BF16)
            dval_ref[:, cols] = (da * ge).astype(BF16)
            dcb_ref[:, cols] += _colsum(dgc)
            dcw_ref[0:1, cols] += _colsum(dgc * g2)
            dcw_ref[1:2, cols] += _colsum(dgc * g1)
            dcw_ref[2:3, cols] += _colsum(dgc * gt)

    return pl.pallas_call(
        body,
        name="down_bwd",
        grid=(S // tm,),
        in_specs=[_rows(tm, D_MODEL), _full((D_FF, D_MODEL)), _rows(tm, D_FF), _halo_prev(tm, D_FF), _rows(tm, D_FF),
                  _full((3, D_FF)), _full((1, D_FF))],
        out_specs=[_rows(tm, D_FF), _rows(tm, D_FF), _full((3, D_FF)), _full((1, D_FF))],
        out_shape=[jax.ShapeDtypeStruct((S, D_FF), BF16), jax.ShapeDtypeStruct((S, D_FF), BF16),
                   jax.ShapeDtypeStruct((3, D_FF), F32), jax.ShapeDtypeStruct((1, D_FF), F32)],
        compiler_params=_params(1),
    )(dy2, w_down, gate, gate, val, conv_w, conv_b)


def dw_matmul(a, b, out_blocks, blk_shape, a_cols, b_cols, a_blocked, name, prev=None, blk_off=0, n_blk=None, tm=512):
    S = a.shape[0]
    n_blk = out_blocks if n_blk is None else n_blk

    def body(*refs):
        a_ref, b_ref, o_ref = refs[0], refs[1], refs[-1]

        @pl.when(pl.program_id(1) == 0)
        def _():
            o_ref[...] = jnp.zeros_like(o_ref)

        o_ref[...] += lax.dot_general(a_ref[...], b_ref[...], TN, preferred_element_type=F32)

    a_spec = pl.BlockSpec((tm, a_cols), (lambda j, i: (i, j)) if a_blocked else (lambda j, i: (i, 0)))
    b_spec = pl.BlockSpec((tm, b_cols), (lambda j, i: (i, 0)) if a_blocked else (lambda j, i: (i, j)))
    in_specs = [a_spec, b_spec]
    args = [a, b]
    aliases = {}
    if prev is not None:
        in_specs.append(pl.BlockSpec(memory_space=pl.ANY))
        args.append(prev)
        aliases = {2: 0}
    return pl.pallas_call(
        body,
        name=name,
        grid=(n_blk, S // tm),
        in_specs=in_specs,
        out_specs=pl.BlockSpec((None,) + tuple(blk_shape), lambda j, i: (j + blk_off, 0, 0)),
        out_shape=jax.ShapeDtypeStruct((out_blocks,) + tuple(blk_shape), F32),
        input_output_aliases=aliases,
        compiler_params=_params(2),
    )(*args)


def up_bwd(dgc, dval, conv_w, w_up_g, x1, dout, y1, g_pre, g_post, mod6, tm=256):
    S = x1.shape[0]
    last_blk = S // 16 - 1

    def body(dgc_ref, nxt_ref, dval_ref, cw_ref, w_ref, x1_ref, dout_ref, y1_ref, gpre_ref, gpost_ref, mod_ref,
             dgate_ref, dx1_ref, dy1_ref, dsh_ref, dsc_ref, dgpre_ref, dgt_ref, dgpost_ref):
        last = pl.program_id(0) == pl.num_programs(0) - 1
        dh = jnp.zeros((tm, D_MODEL), F32)
        for ch in range(2):
            cols = slice(ch * HALF_FF, (ch + 1) * HALF_FF)
            dg = dgc_ref[:, cols].astype(F32)
            nx = jnp.where(last, 0.0, nxt_ref[:, cols].astype(F32))
            row = lax.broadcasted_iota(jnp.int32, dg.shape, 0)
            n0, n1 = nx[0:1, :], nx[1:2, :]
            u1 = jnp.where(row == tm - 1, n0, pltpu.roll(dg, tm - 1, 0))
            u2 = jnp.where(row == tm - 1, n1, jnp.where(row == tm - 2, n0, pltpu.roll(dg, tm - 2, 0)))
            dgate = (dg * cw_ref[2:3, cols] + u1 * cw_ref[1:2, cols] + u2 * cw_ref[0:1, cols]).astype(BF16)
            dgate_ref[:, cols] = dgate
            dh = dh + lax.dot_general(dgate, w_ref[ch], NT, preferred_element_type=F32)
            dh = dh + lax.dot_general(dval_ref[:, cols], w_ref[2 + ch], NT, preferred_element_type=F32)
        x1 = x1_ref[...]
        rstd = lax.rsqrt(jnp.mean(x1 * x1, axis=-1, keepdims=True) + NORM_EPS)
        n2 = x1 * rstd
        gpre = gpre_ref[...]
        one_sc = 1.0 + mod_ref[4:5, :]
        _acc(dsh_ref, _colsum(dh))
        _acc(dsc_ref, _colsum(dh * (n2 * gpre)))
        _acc(dgpre_ref, _colsum(dh * one_sc * n2))
        dn = dh * (gpre * one_sc)
        dx1 = dout_ref[...] + rstd * (dn - n2 * jnp.mean(dn * n2, axis=-1, keepdims=True))
        dx1_ref[...] = dx1
        y1 = y1_ref[...]
        rstd1 = lax.rsqrt(jnp.mean(y1 * y1, axis=-1, keepdims=True) + NORM_EPS)
        y1n = y1 * rstd1
        gpost = gpost_ref[...]
        gtm = mod_ref[2:3, :]
        _acc(dgt_ref, _colsum(dx1 * (y1n * gpost)))
        dr1 = dx1 * gtm
        _acc(dgpost_ref, _colsum(dr1 * y1n))
        dyn = dr1 * gpost
        dy1 = rstd1 * (dyn - y1n * jnp.mean(dyn * y1n, axis=-1, keepdims=True))
        dy1_ref[...] = dy1.astype(BF16)

    vec = _full((1, D_MODEL))
    nxt = pl.BlockSpec((16, D_FF), lambda i: (jnp.minimum((i + 1) * (tm // 16), last_blk), 0))
    return pl.pallas_call(
        body,
        name="up_bwd",
        grid=(S // tm,),
        in_specs=[_rows(tm, D_FF), nxt, _rows(tm, D_FF), _full((3, D_FF)), _full(w_up_g.shape), _rows(tm, D_MODEL),
                  _rows(tm, D_MODEL), _rows(tm, D_MODEL), vec, vec, _full((6, D_MODEL))],
        out_specs=[_rows(tm, D_FF), _rows(tm, D_MODEL), _rows(tm, D_MODEL), vec, vec, vec, vec, vec],
        out_shape=[jax.ShapeDtypeStruct((S, D_FF), BF16), jax.ShapeDtypeStruct((S, D_MODEL), F32),
                   jax.ShapeDtypeStruct((S, D_MODEL), BF16)] + [jax.ShapeDtypeStruct((1, D_MODEL), F32)] * 5,
        compiler_params=_params(1),
    )(dgc, dgc, dval, conv_w, w_up_g, x1, dout, y1, g_pre, g_post, mod6)


def outproj_bwd(dy1, w_out_g, cat, tm=512):
    S = dy1.shape[0]

    def body(dy_ref, w_ref, attn_ref, dpool_ref, dattn_ref, delta_ref):
        dcat = jnp.zeros((tm, 512), F32)
        for j in range(N_CHIPS):
            dcat = dcat + lax.dot_general(dy_ref[:, j * 256:(j + 1) * 256], w_ref[j], NT, preferred_element_type=F32)
        dpool_ref[...] = dcat[:, :POOL_W]
        dattn = dcat[:, POOL_W:]
        dattn_ref[...] = dattn.astype(BF16)
        prod = dattn * attn_ref[...].astype(F32)
        r = lax.broadcasted_iota(jnp.int32, (GROUP_W, GROUP_W), 0) // HEAD_DIM
        c = lax.broadcasted_iota(jnp.int32, (GROUP_W, GROUP_W), 1) // HEAD_DIM
        ones_bd = jnp.where(r == c, 1.0, 0.0).astype(BF16)
        hi = prod.astype(BF16)
        lo = (prod - hi.astype(F32)).astype(BF16)
        delta_ref[...] = (jnp.dot(hi, ones_bd, preferred_element_type=F32)
                          + jnp.dot(lo, ones_bd, preferred_element_type=F32))

    return pl.pallas_call(
        body,
        name="outproj_bwd",
        grid=(S // tm,),
        in_specs=[_rows(tm, D_MODEL), _full(w_out_g.shape), pl.BlockSpec((tm, GROUP_W), lambda i: (i, 1))],
        out_specs=[_rows(tm, POOL_W), _rows(tm, GROUP_W), _rows(tm, GROUP_W)],
        out_shape=[jax.ShapeDtypeStruct((S, POOL_W), F32), jax.ShapeDtypeStruct((S, GROUP_W), BF16),
                   jax.ShapeDtypeStruct((S, GROUP_W), F32)],
        compiler_params=_params(1),
    )(dy1, w_out_g, cat)


def attn_bwd(qkv9, dattn, lse, delta, gi, d):
    S = qkv9.shape[1]
    L = S // d
    nb = L // ATT_BLOCK
    qv = qkv9.reshape(9, L, d * GROUP_W)
    view = lambda t: t.reshape(L, d * GROUP_W)

    def body(q_ref, k_ref, v_ref, do_ref, l_ref, dl_ref, out_ref, kpad, vpad, dkpad, dvpad):
        kpad[0:ATT_BLOCK, :] = jnp.zeros((ATT_BLOCK, GROUP_W), BF16)
        vpad[0:ATT_BLOCK, :] = jnp.zeros((ATT_BLOCK, GROUP_W), BF16)
        kpad[ATT_BLOCK:, :] = k_ref[...]
        vpad[ATT_BLOCK:, :] = v_ref[...]
        dkpad[...] = jnp.zeros_like(dkpad)
        dvpad[...] = jnp.zeros_like(dvpad)
        band, col, lo = _attn_masks()

        def step(n, carry):
            r0 = pl.multiple_of(n * ATT_BLOCK, ATT_BLOCK)
            valid = band & ((col >= ATT_BLOCK) | (n > 0))
            qb = q_ref[pl.ds(r0, ATT_BLOCK), :]
            dob = do_ref[pl.ds(r0, ATT_BLOCK), :]
            lb = l_ref[pl.ds(r0, ATT_BLOCK), :]
            dlb = dl_ref[pl.ds(r0, ATT_BLOCK), :]
            kb = kpad[pl.ds(r0, 2 * ATT_BLOCK), :]
            vb = vpad[pl.ds(r0, 2 * ATT_BLOCK), :]
            for pair in range(2):
                lanes = slice(pair * 128, (pair + 1) * 128)
                qp, dop, kp, vp = qb[:, lanes], dob[:, lanes], kb[:, lanes], vb[:, lanes]
                dqs = []
                dk_sum = jnp.zeros((2 * ATT_BLOCK, 128), F32)
                dv_sum = jnp.zeros((2 * ATT_BLOCK, 128), F32)
                for hh in range(2):
                    sel = lo if hh == 0 else jnp.logical_not(lo)
                    qm = jnp.where(sel, qp, jnp.zeros_like(qp))
                    dom = jnp.where(sel, dop, jnp.zeros_like(dop))
                    cc = pair * 128 + hh * HEAD_DIM
                    s = lax.dot_general(qm, kp, NT, preferred_element_type=F32)
                    s = jnp.where(valid, s, NEG)
                    p = jnp.exp(s - lb[:, cc:cc + 1])
                    dp = lax.dot_general(dom, vp, NT, preferred_element_type=F32)
                    ds = (p * (dp - dlb[:, cc:cc + 1])).astype(BF16)
                    pb = p.astype(BF16)
                    dqs.append(jnp.dot(ds, kp, preferred_element_type=F32))
                    dk_sum = dk_sum + lax.dot_general(ds, qm, TN, preferred_element_type=F32)
                    dv_sum = dv_sum + lax.dot_general(pb, dom, TN, preferred_element_type=F32)
                out_ref[0, pl.ds(r0, ATT_BLOCK), lanes] = jnp.where(lo, dqs[0], dqs[1])
                dkpad[pl.ds(r0, 2 * ATT_BLOCK), lanes] += dk_sum
                dvpad[pl.ds(r0, 2 * ATT_BLOCK), lanes] += dv_sum
            return carry

        lax.fori_loop(0, nb, step, 0)
        out_ref[1] = dkpad[ATT_BLOCK:, :]
        out_ref[2] = dvpad[ATT_BLOCK:, :]

    spec = lambda lead: pl.BlockSpec((None, L, GROUP_W), lambda r: (lead, 0, r))
    cls = pl.BlockSpec((L, GROUP_W), lambda r: (0, r))
    out = pl.pallas_call(
        body,
        name=f"attn_bwd_d{d}",
        grid=(d,),
        in_specs=[spec(gi), spec(3 + gi), spec(6 + gi), cls, cls, cls],
        out_specs=pl.BlockSpec((3, L, GROUP_W), lambda r: (0, 0, r)),
        out_shape=jax.ShapeDtypeStruct((3, L, d * GROUP_W), F32),
        scratch_shapes=[pltpu.VMEM((L + ATT_BLOCK, GROUP_W), BF16)] * 2 + [pltpu.VMEM((L + ATT_BLOCK, GROUP_W), F32)] * 2,
        compiler_params=_params(1),
    )(qv, qv, qv, view(dattn), view(lse), view(delta))
    return out.reshape(3, S, GROUP_W)


def pool_bwd(dpool, mixed, wbd, b, scale):
    S = dpool.shape[0]

    def body(dp_ref, mx_ref, w_ref, b_ref, s_ref, du_ref, dw_ref, db_ref, ds_ref):
        dp = dp_ref[...]
        mb = mx_ref[...]
        wv = w_ref[...]
        ypre = jnp.dot(mb, wv, preferred_element_type=F32) + b_ref[...]
        ds_ref[...] = _colsum(dp * ypre)
        dpre = dp * s_ref[...]
        db_ref[...] = _colsum(dpre)
        dpb = dpre.astype(BF16)
        dw_ref[...] = lax.dot_general(mb, dpb, TN, preferred_element_type=F32)
        dmix = lax.dot_general(dpb, wv, NT, preferred_element_type=F32)
        row = lax.broadcasted_iota(jnp.int32, dmix.shape, 0)
        lane, win = _pool_lane_windows(dmix.shape)
        e = dmix / jnp.minimum(row + 1, win).astype(F32)

        def shift(a, k):
            return jnp.where(row < S - k, pltpu.roll(a, S - k, 0), 0.0)

        f2 = e + shift(e, 1)
        f4 = f2 + shift(f2, 2)
        f8 = f4 + shift(f4, 4)
        f16 = f8 + shift(f8, 8)
        du_ref[...] = jnp.where(lane < 64, f2, jnp.where(lane < 128, f4, jnp.where(lane < 192, f8, f16))) - dmix

    vm = pl.BlockSpec(memory_space=pltpu.VMEM)
    return pl.pallas_call(
        body,
        name="pool_bwd",
        in_specs=[vm] * 5,
        out_specs=[vm] * 4,
        out_shape=[jax.ShapeDtypeStruct((S, POOL_W), F32), jax.ShapeDtypeStruct((POOL_W, POOL_W), F32),
                   jax.ShapeDtypeStruct((1, POOL_W), F32), jax.ShapeDtypeStruct((1, POOL_W), F32)],
        compiler_params=_params(),
    )(dpool, mixed, wbd, b, scale)


def inproj_bwd(dqkv, du, x, dx1, w_in_g, g, mod6, tc, tsa, tsb, tm=256):
    S = x.shape[0]

    def body(d0, d1, d2, du_ref, x_ref, dx1_ref, w_ref, g_ref, mod_ref, tc_ref, tsa_ref, tsb_ref,
             dp_ref, gx_ref, dsh_ref, dsc_ref, dg_ref):
        cs, sa, sb = tc_ref[...], tsa_ref[...], tsb_ref[...]
        dgrp = (d0, d1, d2)
        for sp in range(20):
            piece, half = sp // 2, sp % 2
            lanes = slice(half * 128, (half + 1) * 128)
            if piece == 0:
                blk = du_ref[:, lanes]
            else:
                kind, gi = (piece - 1) // 3, (piece - 1) % 3
                blk = dgrp[gi][kind, :, lanes]
                if kind == 0:
                    blk = _rope128(blk, cs, sa, sb, -1.0) * (HEAD_DIM ** -0.5)
                elif kind == 1:
                    blk = _rope128(blk, cs, sa, sb, -1.0)
            dp_ref[:, sp * 128:(sp + 1) * 128] = blk.astype(BF16)
        dh = jnp.zeros((tm, D_MODEL), F32)
        for j in range(N_CHIPS):
            dh = dh + lax.dot_general(dp_ref[:, j * 640:(j + 1) * 640], w_ref[j], NT, preferred_element_type=F32)
        xv = x_ref[...]
        rstd = lax.rsqrt(jnp.mean(xv * xv, axis=-1, keepdims=True) + NORM_EPS)
        n1 = xv * rstd
        gv = g_ref[...]
        one_sc = 1.0 + mod_ref[1:2, :]
        _acc(dsh_ref, _colsum(dh))
        _acc(dsc_ref, _colsum(dh * (n1 * gv)))
        _acc(dg_ref, _colsum(dh * one_sc * n1))
        dn = dh * (gv * one_sc)
        gx_ref[...] = dx1_ref[...] + rstd * (dn - n1 * jnp.mean(dn * n1, axis=-1, keepdims=True))

    vec = _full((1, D_MODEL))
    dspec = pl.BlockSpec((3, tm, GROUP_W), lambda i: (0, i, 0))
    return pl.pallas_call(
        body,
        name="inproj_bwd",
        grid=(S // tm,),
        in_specs=[dspec] * 3 + [_rows(tm, POOL_W), _rows(tm, D_MODEL), _rows(tm, D_MODEL), _full(w_in_g.shape), vec,
                                _full((6, D_MODEL)), _rows(tm, 128), _rows(tm, 128), _rows(tm, 128)],
        out_specs=[_rows(tm, IN_W), _rows(tm, D_MODEL), vec, vec, vec],
        out_shape=[jax.ShapeDtypeStruct((S, IN_W), BF16), jax.ShapeDtypeStruct((S, D_MODEL), F32)]
        + [jax.ShapeDtypeStruct((1, D_MODEL), F32)] * 3,
        compiler_params=_params(1),
    )(*dqkv, du, x, dx1, w_in_g, g, mod6, tc, tsa, tsb)


def _adamw(w, g, m, v):
    m = ADAM_B1 * m + (1.0 - ADAM_B1) * g
    v = ADAM_B2 * v + (1.0 - ADAM_B2) * (g * g)
    m_hat = m / (1.0 - ADAM_B1 ** ADAM_STEP)
    v_hat = v / (1.0 - ADAM_B2 ** ADAM_STEP)
    delta = -ADAM_LR * (m_hat / (jnp.sqrt(v_hat) + ADAM_EPS) + ADAM_WD * w)
    return delta, m, v


def adamw_rows(w, g, m, v, tr, name):
    R, C = w.shape

    def body(w_ref, g_ref, m_ref, v_ref, d_ref, mo_ref, vo_ref):
        d_ref[...], mo_ref[...], vo_ref[...] = _adamw(w_ref[...], g_ref[...], m_ref[...], v_ref[...])

    spec = pl.BlockSpec((tr, C), lambda i: (i, 0))
    return pl.pallas_call(
        body,
        name=name,
        grid=(R // tr,),
        in_specs=[spec] * 4,
        out_specs=[spec] * 3,
        out_shape=[jax.ShapeDtypeStruct((R, C), F32)] * 3,
        compiler_params=_params(1),
    )(w, g, m, v)


def adamw_ada(c_all_t, dmod_cols, w, m, v, tr=256):
    R, C = w.shape

    def body(ct_ref, dm_ref, w_ref, m_ref, v_ref, g_ref, d_ref, mo_ref, vo_ref):
        ct = ct_ref[...]
        act = ct * jax.nn.sigmoid(ct)
        g = jnp.zeros((tr, C), F32)
        for b in range(N_DEV):
            g = g + act[:, b:b + 1] * dm_ref[b:b + 1, :]
        g_ref[...] = g
        d_ref[...], mo_ref[...], vo_ref[...] = _adamw(w_ref[...], g, m_ref[...], v_ref[...])

    spec = pl.BlockSpec((tr, C), lambda i: (i, 0))
    return pl.pallas_call(
        body,
        name="adamw_ada",
        grid=(R // tr,),
        in_specs=[pl.BlockSpec((tr, N_DEV), lambda i: (i, 0)), _full((N_DEV, C)), spec, spec, spec],
        out_specs=[spec] * 4,
        out_shape=[jax.ShapeDtypeStruct((R, C), F32)] * 4,
        compiler_params=_params(1),
    )(c_all_t, dmod_cols, w, m, v)


def adamw_small(slab_a, slab_b, convw_g, wpool_g, params):
    names = ["b_ada", "g_pre_mix", "g_post_mix", "g_pre_ffn", "g_post_ffn", "b_pool", "pool_scale", "conv_b", "conv_w", "w_pool"]
    flat = []
    for n in names:
        flat += list(params[n])

    def body(a_ref, b_ref, cw_ref, wp_ref, *rest):
        ins, outs = rest[:30], rest[30:]

        def dev_sum(ref):
            t = ref[0]
            for dev in range(1, N_DEV):
                t = t + ref[dev]
            return t

        sa, sb_, scw, swp = dev_sum(a_ref), dev_sum(b_ref), dev_sum(cw_ref), dev_sum(wp_ref)
        grads = [
            jnp.concatenate([sa[k:k + 1, :] for k in range(6)], axis=1),
            sa[6:7, :], sa[7:8, :], sa[8:9, :], sa[9:10, :],
            sa[10:11, 0:256], sa[10:11, 256:512],
            sb_[3:4, :], scw, swp,
        ]
        for i, g in enumerate(grads):
            w_ref, m_ref, v_ref = ins[3 * i:3 * i + 3]
            d, mo, vo = _adamw(w_ref[...], g, m_ref[...], v_ref[...])
            outs[4 * i][...] = g
            outs[4 * i + 1][...] = d
            outs[4 * i + 2][...] = mo
            outs[4 * i + 3][...] = vo

    vm = pl.BlockSpec(memory_space=pltpu.VMEM)
    out_shape = []
    for n in names:
        out_shape += [jax.ShapeDtypeStruct(params[n][0].shape, F32)] * 4
    outs = pl.pallas_call(
        body,
        name="adamw_small",
        in_specs=[vm] * (4 + len(flat)),
        out_specs=[vm] * len(out_shape),
        out_shape=out_shape,
        compiler_params=_params(),
    )(slab_a, slab_b, convw_g, wpool_g, *flat)
    return {n: outs[4 * i:4 * i + 4] for i, n in enumerate(names)}


def _place():
    return lax.axis_index("x"), lax.axis_index("y"), lax.axis_index("c")


def _other_chips(x, y):
    return [(1 - x, y), (x, 1 - y), (1 - x, 1 - y)]


def _chip_id(cx, cy):
    return 2 * cx + cy


def gather_weights(shards):
    n = len(shards)
    halved = [s.shape[0] % 32 == 0 for s in shards]

    def body(*refs):
        ins, outs = refs[:n], refs[n:2 * n]
        send_sems, recv_sems, loc_sems = refs[2 * n:]
        x, y, c = _place()
        me = _chip_id(x, y)
        chips = _other_chips(x, y)
        sib = (x, y, 1 - c)

        def part(w, chip, half):
            if not halved[w]:
                return outs[w].at[chip]
            rh = shards[w].shape[0] // 2
            return outs[w].at[chip, pl.ds(half * rh, rh), :]

        def src_part(w):
            if not halved[w]:
                return ins[w]
            rh = shards[w].shape[0] // 2
            return ins[w].at[pl.ds(c * rh, rh), :]

        def rcopy(w, k, src, dst, to):
            return pltpu.make_async_remote_copy(src_ref=src, dst_ref=dst, send_sem=send_sems.at[6 * w + k],
                                                recv_sem=recv_sems.at[6 * w + k], device_id=to, device_id_type=MESH)

        local = [pltpu.make_async_copy(ins[w], outs[w].at[me], loc_sems.at[w]) for w in range(n)]
        for cp in local:
            cp.start()
        first = []
        for w in range(n):
            for k, (cx, cy) in enumerate(chips):
                cp = rcopy(w, k, src_part(w), part(w, me, c), (cx, cy, c))
                cp.start()
                first.append(cp)
        passed = []
        for w in range(n):
            for k, (cx, cy) in enumerate(chips):
                blk = part(w, _chip_id(cx, cy), c)
                rcopy(w, k, blk, blk, (cx, cy, c)).wait_recv()
                if halved[w]:
                    cp = rcopy(w, 3 + k, blk, blk, sib)
                    cp.start()
                    passed.append(cp)
        for w in range(n):
            if halved[w]:
                for k, (cx, cy) in enumerate(chips):
                    blk = part(w, _chip_id(cx, cy), 1 - c)
                    rcopy(w, 3 + k, blk, blk, sib).wait_recv()
        for cp in first + passed:
            cp.wait_send()
        for cp in local:
            cp.wait()

    hbm = pl.BlockSpec(memory_space=pl.ANY)
    return pl.pallas_call(
        body,
        name="gather_weights",
        in_specs=[hbm] * n,
        out_specs=[hbm] * n,
        out_shape=[jax.ShapeDtypeStruct((N_CHIPS,) + s.shape, s.dtype) for s in shards],
        scratch_shapes=[pltpu.SemaphoreType.DMA((6 * n,)), pltpu.SemaphoreType.DMA((6 * n,)), pltpu.SemaphoreType.DMA((n,))],
        compiler_params=pltpu.CompilerParams(has_side_effects=True, vmem_limit_bytes=VMEM_LIMIT),
    )(*shards)


HBM_SPEC = pl.BlockSpec(memory_space=pltpu.HBM)
SEM_SPEC = pl.BlockSpec(memory_space=pltpu.SEMAPHORE)
ANY_SPEC = pl.BlockSpec(memory_space=pl.ANY)
EFFECT = pltpu.SideEffectType.DATAFLOW_SIDE_EFFECTING


def _hbm(t):
    return pltpu.with_memory_space_constraint(t, pltpu.HBM)


def _hbm_shapes(ts):
    return [pltpu.HBM(t.shape, t.dtype) for t in ts]


def _half_rows(ref, lead, half, rh):
    return ref.at[lead, pl.ds(half * rh, rh), :]


def gather_split_start(shards, carry):
    n = len(shards)
    lands = [lax.empty((N_CHIPS,) + s.shape, s.dtype) for s in shards]

    def body(*refs):
        ins, land = refs[:n], refs[n:2 * n]
        send_sems, recv_sems = refs[2 * n + 1], refs[2 * n + 2]
        loc_sems = refs[-1]
        x, y, c = _place()
        me = _chip_id(x, y)
        local = [pltpu.make_async_copy(ins[w], land[w].at[me], loc_sems.at[w]) for w in range(n)]
        for cp in local:
            cp.start()
        for cp in local:
            cp.wait()
        for w in range(n):
            rh = shards[w].shape[0] // 2
            for k, (cx, cy) in enumerate(_other_chips(x, y)):
                pltpu.make_async_remote_copy(src_ref=ins[w].at[pl.ds(c * rh, rh), :], dst_ref=_half_rows(land[w], me, c, rh),
                                             send_sem=send_sems.at[3 * w + k], recv_sem=recv_sems.at[3 * w + k],
                                             device_id=(cx, cy, c), device_id_type=MESH).start()

    args = [_hbm(s) for s in shards] + [_hbm(l) for l in lands] + [_hbm(carry)]
    outs = pl.pallas_call(
        body,
        name="gather_split_start",
        out_shape=[pltpu.SemaphoreType.DMA((3 * n,)), pltpu.SemaphoreType.DMA((3 * n,))] + _hbm_shapes(shards) + _hbm_shapes(lands)
        + _hbm_shapes([carry]),
        in_specs=[HBM_SPEC] * (2 * n + 1),
        out_specs=[SEM_SPEC, SEM_SPEC] + [HBM_SPEC] * (2 * n + 1),
        input_output_aliases={i: 2 + i for i in range(2 * n + 1)},
        scratch_shapes=[pltpu.SemaphoreType.DMA((n,))],
        compiler_params=pltpu.CompilerParams(has_side_effects=EFFECT),
    )(*args)
    return outs[0], outs[1], list(outs[2:2 + n]), list(outs[2 + n:2 + 2 * n]), outs[-1]


def gather_split_mid(send_sems, recv_sems, shards, lands, after):
    n = len(shards)

    def body(*refs):
        ins, land = refs[:n], refs[n:2 * n]
        ssem, rsem = refs[2 * n], refs[2 * n + 1]
        fsend, frecv = refs[2 * n + 3], refs[2 * n + 4]
        x, y, c = _place()
        me = _chip_id(x, y)
        chips = _other_chips(x, y)
        for w in range(n):
            rh = shards[w].shape[0] // 2
            for k, (cx, cy) in enumerate(chips):
                got = _half_rows(land[w], _chip_id(cx, cy), c, rh)
                cp = pltpu.make_async_remote_copy(src_ref=ins[w].at[pl.ds(c * rh, rh), :], dst_ref=got, send_sem=ssem.at[3 * w + k],
                                                  recv_sem=rsem.at[3 * w + k], device_id=(cx, cy, c), device_id_type=MESH)
                cp.wait_send()
                cp.wait_recv()
        for w in range(n):
            rh = shards[w].shape[0] // 2
            for k, (cx, cy) in enumerate(chips):
                got = _half_rows(land[w], _chip_id(cx, cy), c, rh)
                pltpu.make_async_remote_copy(src_ref=got, dst_ref=got, send_sem=fsend.at[3 * w + k], recv_sem=frecv.at[3 * w + k],
                                             device_id=(x, y, 1 - c), device_id_type=MESH).start()

    outs = pl.pallas_call(
        body,
        name="gather_split_mid",
        out_shape=[pltpu.SemaphoreType.DMA((3 * n,)), pltpu.SemaphoreType.DMA((3 * n,))] + _hbm_shapes(lands),
        in_specs=[HBM_SPEC] * (2 * n) + [SEM_SPEC, SEM_SPEC, ANY_SPEC],
        out_specs=[SEM_SPEC, SEM_SPEC] + [HBM_SPEC] * n,
        input_output_aliases={n + i: 2 + i for i in range(n)},
        compiler_params=pltpu.CompilerParams(has_side_effects=EFFECT),
    )(*shards, *lands, send_sems, recv_sems, after)
    return outs[0], outs[1], list(outs[2:])


def gather_split_done(fsend, frecv, lands, after):
    n = len(lands)

    def body(*refs):
        land = refs[:n]
        ssem, rsem = refs[n], refs[n + 1]
        x, y, c = _place()
        for w in range(n):
            rh = lands[w].shape[1] // 2
            for k, (cx, cy) in enumerate(_other_chips(x, y)):
                sent = _half_rows(land[w], _chip_id(cx, cy), c, rh)
                got = _half_rows(land[w], _chip_id(cx, cy), 1 - c, rh)
                cp = pltpu.make_async_remote_copy(src_ref=sent, dst_ref=got, send_sem=ssem.at[3 * w + k], recv_sem=rsem.at[3 * w + k],
                                                  device_id=(x, y, 1 - c), device_id_type=MESH)
                cp.wait_send()
                cp.wait_recv()

    outs = pl.pallas_call(
        body,
        name="gather_split_done",
        out_shape=_hbm_shapes(lands),
        in_specs=[HBM_SPEC] * n + [SEM_SPEC, SEM_SPEC, ANY_SPEC],
        out_specs=[HBM_SPEC] * n,
        input_output_aliases={i: i for i in range(n)},
        compiler_params=pltpu.CompilerParams(has_side_effects=EFFECT),
    )(*lands, fsend, frecv, after)
    return list(outs)


def _flips():
    return [(fx, fy, fc) for fx in (0, 1) for fy in (0, 1) for fc in (0, 1)][1:]


def _flip(v, f):
    return v if f == 0 else 1 - v


def ada_mod(c3, w_ada, b_cols):
    CB = w_ada.shape[1]

    def body(c_ref, w_ref, b_ref, call_ref, mod_ref, modall, send_sems, recv_sems):
        x, y, c = _place()
        me_dev = 4 * x + 2 * y + c
        me = _chip_id(x, y)
        call_ref[me_dev] = c_ref[0]
        sends = []
        for k, (fx, fy, fc) in enumerate(_flips()):
            cp = pltpu.make_async_remote_copy(src_ref=c_ref.at[0], dst_ref=call_ref.at[me_dev], send_sem=send_sems.at[k],
                                              recv_sem=recv_sems.at[k],
                                              device_id=(_flip(x, fx), _flip(y, fy), _flip(c, fc)), device_id_type=MESH)
            cp.start()
            sends.append(cp)
        for k, (fx, fy, fc) in enumerate(_flips()):
            peer = 4 * _flip(x, fx) + 2 * _flip(y, fy) + _flip(c, fc)
            pltpu.make_async_remote_copy(src_ref=c_ref.at[0], dst_ref=call_ref.at[peer], send_sem=send_sems.at[k],
                                         recv_sem=recv_sems.at[k], device_id=(x, y, c), device_id_type=MESH).wait_recv()
        row = lax.broadcasted_iota(jnp.int32, (N_DEV, D_MODEL), 0)
        call = jnp.zeros((N_DEV, D_MODEL), F32)
        for dev in range(N_DEV):
            call = jnp.where(row == dev, call_ref[dev], call)
        act = call * jax.nn.sigmoid(call)
        modall[me] = jnp.dot(act, w_ref[...], preferred_element_type=F32, precision=lax.Precision.HIGHEST) + b_ref[...]
        for k, (cx, cy) in enumerate(_other_chips(x, y)):
            cp = pltpu.make_async_remote_copy(src_ref=modall.at[me], dst_ref=modall.at[me], send_sem=send_sems.at[7 + k],
                                              recv_sem=recv_sems.at[7 + k], device_id=(cx, cy, c), device_id_type=MESH)
            cp.start()
            sends.append(cp)
        for k, (cx, cy) in enumerate(_other_chips(x, y)):
            blk = modall.at[_chip_id(cx, cy)]
            pltpu.make_async_remote_copy(src_ref=blk, dst_ref=blk, send_sem=send_sems.at[7 + k], recv_sem=recv_sems.at[7 + k],
                                         device_id=(x, y, c), device_id_type=MESH).wait_recv()
        for cp in sends:
            cp.wait_send()
        mine = [modall[j, pl.ds(me_dev, 1), :] for j in range(N_CHIPS)]
        for r in range(6):
            pieces = []
            for h in range(2):
                pos = r * D_MODEL + h * 512
                pieces.append(mine[pos // CB][:, pos % CB:pos % CB + 512])
            mod_ref[r:r + 1, :] = jnp.concatenate(pieces, axis=1)

    vm = pl.BlockSpec(memory_space=pltpu.VMEM)
    return pl.pallas_call(
        body,
        name="ada_mod",
        in_specs=[vm] * 3,
        out_specs=[vm] * 2,
        out_shape=[jax.ShapeDtypeStruct((N_DEV, 1, D_MODEL), F32), jax.ShapeDtypeStruct((6, D_MODEL), F32)],
        scratch_shapes=[pltpu.VMEM((N_CHIPS, N_DEV, CB), F32), pltpu.SemaphoreType.DMA((10,)), pltpu.SemaphoreType.DMA((10,))],
        compiler_params=pltpu.CompilerParams(has_side_effects=True, vmem_limit_bytes=VMEM_LIMIT),
    )(c3, w_ada, b_cols)


def gather_small(blocks):
    n = len(blocks)

    def body(*refs):
        ins, outs = refs[:n], refs[n:2 * n]
        send_sems, recv_sems = refs[2 * n:]
        x, y, c = _place()
        sib = (x, y, 1 - c)
        chips = _other_chips(x, y)

        def dev(px, py, pc):
            return 4 * px + 2 * py + pc

        def cp(w, k, src, block_dev, to):
            return pltpu.make_async_remote_copy(src_ref=src, dst_ref=outs[w].at[block_dev], send_sem=send_sems.at[7 * w + k],
                                                recv_sem=recv_sems.at[7 * w + k], device_id=to, device_id_type=MESH)

        me = dev(x, y, c)
        started = []
        for w in range(n):
            outs[w][me] = ins[w][...]
            t = cp(w, 0, ins[w], me, sib)
            t.start()
            started.append(t)
            for k, (cx, cy) in enumerate(chips):
                t = cp(w, 1 + k, ins[w], me, (cx, cy, c))
                t.start()
                started.append(t)
        for w in range(n):
            for k, (cx, cy) in enumerate(chips):
                b = dev(cx, cy, c)
                cp(w, 1 + k, outs[w].at[b], b, (x, y, c)).wait_recv()
                t = cp(w, 4 + k, outs[w].at[b], b, sib)
                t.start()
                started.append(t)
        for w in range(n):
            b = dev(x, y, 1 - c)
            cp(w, 0, outs[w].at[b], b, (x, y, c)).wait_recv()
            for k, (cx, cy) in enumerate(chips):
                b = dev(cx, cy, 1 - c)
                cp(w, 4 + k, outs[w].at[b], b, (x, y, c)).wait_recv()
        for t in started:
            t.wait_send()

    vm = pl.BlockSpec(memory_space=pltpu.VMEM)
    return pl.pallas_call(
        body,
        name="gather_small",
        in_specs=[vm] * n,
        out_specs=[vm] * n,
        out_shape=[jax.ShapeDtypeStruct((N_DEV,) + b.shape, b.dtype) for b in blocks],
        scratch_shapes=[pltpu.SemaphoreType.DMA((7 * n,)), pltpu.SemaphoreType.DMA((7 * n,))],
        compiler_params=pltpu.CompilerParams(has_side_effects=True, vmem_limit_bytes=VMEM_LIMIT),
    )(*blocks)


def reduce_scatter_grads(grads, chunk_rows):
    n = len(grads)
    shapes = [g.shape[1:] for g in grads]
    halves = [s[0] // 2 for s in shapes]

    def body(*refs):
        gin = refs[:n]
        gout = refs[n:2 * n]
        sibbuf = refs[2 * n:3 * n]
        rest = refs[3 * n:]
        rbuf = rest[:n]
        pown = rest[n:2 * n]
        stage_a, stage_b, stage_o, stage_f = rest[2 * n:2 * n + 4]
        sib_send, sib_recv, ici_send, ici_recv, fin_send, fin_recv, ld_sems, st_sems = rest[2 * n + 4:]
        x, y, c = _place()
        me = _chip_id(x, y)
        chips = _other_chips(x, y)
        sib = (x, y, 1 - c)

        to_sib = []
        for w in range(n):
            rh = halves[w]
            cp = pltpu.make_async_remote_copy(src_ref=gin[w].at[:, pl.ds((1 - c) * rh, rh), :], dst_ref=sibbuf[w],
                                              send_sem=sib_send.at[w], recv_sem=sib_recv.at[w], device_id=sib,
                                              device_id_type=MESH)
            cp.start()
            to_sib.append(cp)

        sent = []
        for w in range(n):
            rh, cw = halves[w], shapes[w][1]
            ch = chunk_rows[w]
            to_sib[w].wait_recv()
            for k in range(4):
                chip = me if k == 3 else _chip_id(*chips[k])
                for r0 in range(0, rh, ch):
                    la = pltpu.make_async_copy(gin[w].at[chip, pl.ds(c * rh + r0, ch), :], stage_a.at[0:ch, 0:cw], ld_sems.at[0])
                    lb = pltpu.make_async_copy(sibbuf[w].at[chip, pl.ds(r0, ch), :], stage_b.at[0:ch, 0:cw], ld_sems.at[1])
                    la.start()
                    lb.start()
                    la.wait()
                    lb.wait()
                    tot = stage_a[0:ch, 0:cw] + stage_b[0:ch, 0:cw]
                    if k == 3:
                        pown[w][r0:r0 + ch, :] = tot
                    else:
                        stage_o[0:ch, 0:cw] = tot.astype(BF16)
                        cx, cy = chips[k]
                        cp = pltpu.make_async_remote_copy(src_ref=stage_o.at[0:ch, 0:cw], dst_ref=rbuf[w].at[k, r0:r0 + ch, :],
                                                          send_sem=ici_send.at[3 * w + k], recv_sem=ici_recv.at[3 * w + k],
                                                          device_id=(cx, cy, c), device_id_type=MESH)
                        cp.start()
                        cp.wait_send()
            sent.append(w)

        fin = []
        for w in range(n):
            rh, cw = halves[w], shapes[w][1]
            for k in range(3):
                whole = rbuf[w].at[k]
                pltpu.make_async_remote_copy(src_ref=whole, dst_ref=whole, send_sem=ici_send.at[3 * w + k],
                                             recv_sem=ici_recv.at[3 * w + k], device_id=(x, y, c),
                                             device_id_type=MESH).wait_recv()
            pown[w][...] = ((pown[w][...] + rbuf[w][0].astype(F32)) + rbuf[w][1].astype(F32)) + rbuf[w][2].astype(F32)
            mine = gout[w].at[pl.ds(c * rh, rh), :]
            st = pltpu.make_async_copy(pown[w], mine, st_sems.at[w])
            st.start()
            cp = pltpu.make_async_remote_copy(src_ref=pown[w], dst_ref=mine, send_sem=fin_send.at[w], recv_sem=fin_recv.at[w],
                                              device_id=sib, device_id_type=MESH)
            cp.start()
            fin.append((st, cp))
        for w in range(n):
            rh = halves[w]
            theirs = gout[w].at[pl.ds((1 - c) * rh, rh), :]
            pltpu.make_async_remote_copy(src_ref=theirs, dst_ref=theirs, send_sem=fin_send.at[w], recv_sem=fin_recv.at[w],
                                         device_id=(x, y, c), device_id_type=MESH).wait_recv()
        for cp in to_sib:
            cp.wait_send()
        for st, cp in fin:
            st.wait()
            cp.wait_send()

    hbm = pl.BlockSpec(memory_space=pl.ANY)
    max_ch = max(chunk_rows)
    max_c = max(s[1] for s in shapes)
    outs = pl.pallas_call(
        body,
        name="reduce_scatter_grads",
        in_specs=[hbm] * n,
        out_specs=[hbm] * (2 * n),
        out_shape=[jax.ShapeDtypeStruct(s, F32) for s in shapes]
        + [jax.ShapeDtypeStruct((N_CHIPS, h, s[1]), F32) for h, s in zip(halves, shapes)],
        scratch_shapes=[pltpu.VMEM((3, h, s[1]), BF16) for h, s in zip(halves, shapes)]
        + [pltpu.VMEM((h, s[1]), F32) for h, s in zip(halves, shapes)]
        + [pltpu.VMEM((max_ch, max_c), F32), pltpu.VMEM((max_ch, max_c), F32), pltpu.VMEM((max_ch, max_c), BF16),
           pltpu.VMEM((8, 128), F32)]
        + [pltpu.SemaphoreType.DMA((n,)), pltpu.SemaphoreType.DMA((n,)), pltpu.SemaphoreType.DMA((3 * n,)),
           pltpu.SemaphoreType.DMA((3 * n,)), pltpu.SemaphoreType.DMA((n,)), pltpu.SemaphoreType.DMA((n,)),
           pltpu.SemaphoreType.DMA((2,)), pltpu.SemaphoreType.DMA((n,))],
        compiler_params=pltpu.CompilerParams(has_side_effects=True, vmem_limit_bytes=VMEM_LIMIT),
    )(*grads)
    return outs[:n]


def _rope_tables(positions):
    inv_freq = ROPE_THETA ** (-jnp.arange(0, ROT_DIM, 2, dtype=F32) / ROT_DIM)
    ang = positions.astype(F32)[:, None] * inv_freq
    cos, sin = jnp.cos(ang), jnp.sin(ang)
    S = positions.shape[0]
    one, zero = jnp.ones((S, 48), F32), jnp.zeros((S, 48), F32)
    z8 = jnp.zeros((S, 8), F32)
    tc = jnp.concatenate([cos, cos, one], axis=1)
    tsa = jnp.concatenate([z8, sin, zero], axis=1)
    tsb = jnp.concatenate([-sin, z8, zero], axis=1)
    return tuple(jnp.tile(t, (1, 2)) for t in (tc, tsa, tsb))


def _block_diag(w_pool):
    wbd = jnp.zeros((POOL_W, POOL_W), F32)
    for gi in range(4):
        wbd = wbd.at[gi * 64:(gi + 1) * 64, gi * 64:(gi + 1) * 64].set(w_pool[gi])
    return wbd


def kernel(x, c, positions, w_ada, b_ada, g_pre_mix, g_post_mix, g_pre_ffn, g_post_ffn, w_in, w_pool, b_pool, pool_scale, w_out, w_up, conv_w, conv_b, w_down, loss_target, m_w_ada, m_b_ada, m_g_pre_mix, m_g_post_mix, m_g_pre_ffn, m_g_post_ffn, m_w_in, m_w_pool, m_b_pool, m_pool_scale, m_w_out, m_w_up, m_conv_w, m_conv_b, m_w_down, v_w_ada, v_b_ada, v_g_pre_mix, v_g_post_mix, v_g_pre_ffn, v_g_post_ffn, v_w_in, v_w_pool, v_b_pool, v_pool_scale, v_w_out, v_w_up, v_conv_w, v_conv_b, v_w_down):
    xi, yi = lax.axis_index("x"), lax.axis_index("y")
    chip = 2 * xi + yi
    x2, tgt = x[0], loss_target[0]
    S = x2.shape[0]

    w_in_g, w_out_g, conv_w_g = gather_weights([w_in[0].astype(BF16), w_out[0].astype(BF16), conv_w[0]])
    conv_w_f = jnp.transpose(conv_w_g, (1, 0, 2)).reshape(3, D_FF)
    cb_ada = w_ada.shape[2]
    b_cols = lax.dynamic_slice(b_ada, (0, chip * cb_ada), (1, cb_ada))
    c_all, mod6 = ada_mod(c.reshape(1, 1, D_MODEL), w_ada[0], b_cols)
    gs_send, gs_recv, ffn_shards, ffn_lands, mod6 = gather_split_start([w_up[0].astype(BF16), w_down[0].astype(BF16)], mod6)
    tc, tsa, tsb = _rope_tables(positions[0])
    wbd = _block_diag(w_pool[0]).astype(BF16)
    b_pool2, scale2 = b_pool.reshape(1, POOL_W), pool_scale

    h1, u, qkv9 = inproj_fwd(x2, g_pre_mix, mod6, w_in_g, tc, tsa, tsb)
    o_l = [attn_fwd(qkv9, gi, d) for gi, d in enumerate(DILATIONS)]
    mixed, pool = pool_fwd(u, wbd, b_pool2, scale2)
    gf_send, gf_recv, ffn_lands = gather_split_mid(gs_send, gs_recv, ffn_shards, ffn_lands, pool)
    cat, lse, y1, x1, h2 = outproj_fwd([o for o, _ in o_l] + [l for _, l in o_l], pool, x2, w_out_g, g_post_mix, g_pre_ffn, mod6)
    w_up_g, w_down_g = gather_split_done(gf_send, gf_recv, ffn_lands, h2)
    w_down_f = w_down_g.reshape(D_FF, D_MODEL)
    gate, val = up_fwd(h2, w_up_g)
    a, dy2, dout, loss_v, d_gt_f, d_g_post_ffn = down_fwd(gate, val, conv_w_f, conv_b, w_down_f, x1, tgt, g_post_ffn, mod6)

    dgc, dval, d_conv_w, d_conv_b = down_bwd(dy2, w_down_f, gate, val, conv_w_f, conv_b)
    dw_down = dw_matmul(a, dy2, 2, (HALF_FF, D_MODEL), HALF_FF, D_MODEL, True, "dw_down")
    dgate, dx1, dy1, d_sh_f, d_sc_f, d_g_pre_ffn, d_gt_m, d_g_post_mix = up_bwd(
        dgc, dval, conv_w_f, w_up_g, x1, dout, y1, g_pre_ffn, g_post_mix, mod6)
    dw_up = dw_matmul(h2, dgate, 4, (D_MODEL, HALF_FF), D_MODEL, HALF_FF, False, "dw_up_gate", n_blk=2)
    dw_up = dw_matmul(h2, dval, 4, (D_MODEL, HALF_FF), D_MODEL, HALF_FF, False, "dw_up_val", prev=dw_up, blk_off=2, n_blk=2)
    dpool, dattn, delta = outproj_bwd(dy1, w_out_g, cat)
    dw_out = dw_matmul(cat, dy1, 4, (512, 256), 512, 256, False, "dw_out")
    du, d_wbd, d_b_pool, d_scale = pool_bwd(dpool, mixed, wbd, b_pool2, scale2)
    dqkv = [attn_bwd(qkv9, dattn, lse, delta, gi, d) for gi, d in enumerate(DILATIONS)]
    dproj, grad_x, d_sh_m, d_sc_m, d_g_pre_mix = inproj_bwd(dqkv, du, x2, dx1, w_in_g, g_pre_mix, mod6, tc, tsa, tsb)
    dw_in = dw_matmul(h1, dproj, 4, (D_MODEL, 640), D_MODEL, 640, False, "dw_in")

    g_w_in, g_w_out, g_w_up, g_w_down = reduce_scatter_grads(
        [dw_in, dw_out, dw_up, dw_down.reshape(N_CHIPS, D_FF // N_CHIPS, D_MODEL)], [256, 256, 128, 176])
    z1 = jnp.zeros((1, D_MODEL), F32)
    slab_a = jnp.concatenate(
        [d_sh_m, d_sc_m, d_gt_m, d_sh_f, d_sc_f, d_gt_f, d_g_pre_mix, d_g_post_mix, d_g_pre_ffn, d_g_post_ffn,
         jnp.concatenate([d_b_pool, d_scale, jnp.zeros((1, 512), F32)], axis=1)] + [z1] * 5, axis=0)
    slab_b = jnp.concatenate([d_conv_w, d_conv_b, jnp.zeros((4, D_FF), F32)], axis=0)
    d_wpool = jnp.concatenate([d_wbd[gi * 64:(gi + 1) * 64, gi * 64:(gi + 1) * 64] for gi in range(4)], axis=0)
    slab_a_g, slab_b_g, wpool_g = gather_small([slab_a, slab_b, d_wpool])
    cw_cols = conv_w.shape[2]
    convw_g = lax.dynamic_slice(slab_b_g, (0, 0, chip * cw_cols), (N_DEV, 3, cw_cols))
    dmod_cols = lax.dynamic_slice(slab_a_g[:, :6, :].reshape(N_DEV, 6 * D_MODEL), (0, chip * cb_ada), (N_DEV, cb_ada))

    res = {}
    g_ada, d_ada, m_ada, v_ada = adamw_ada(c_all.reshape(N_DEV, D_MODEL).T, dmod_cols, w_ada[0], m_w_ada[0], v_w_ada[0])
    res["w_ada"] = (g_ada[None], d_ada[None], m_ada[None], v_ada[None])
    for name, w, g, m, v, tr in (("w_in", w_in, g_w_in, m_w_in, v_w_in, 256), ("w_out", w_out, g_w_out, m_w_out, v_w_out, 256),
                                 ("w_up", w_up, g_w_up, m_w_up, v_w_up, 256), ("w_down", w_down, g_w_down, m_w_down, v_w_down, 352)):
        d_, m_, v_ = adamw_rows(w[0], g, m[0], v[0], tr, "adamw_" + name)
        res[name] = (g[None], d_[None], m_[None], v_[None])
    flat = lambda t: t.reshape(1, POOL_W)
    wp = lambda t: t.reshape(POOL_W, 64)
    small = adamw_small(slab_a_g, slab_b_g, convw_g, wpool_g, {
        "b_ada": (b_ada, m_b_ada, v_b_ada), "g_pre_mix": (g_pre_mix, m_g_pre_mix, v_g_pre_mix),
        "g_post_mix": (g_post_mix, m_g_post_mix, v_g_post_mix), "g_pre_ffn": (g_pre_ffn, m_g_pre_ffn, v_g_pre_ffn),
        "g_post_ffn": (g_post_ffn, m_g_post_ffn, v_g_post_ffn), "b_pool": (flat(b_pool), flat(m_b_pool), flat(v_b_pool)),
        "pool_scale": (pool_scale, m_pool_scale, v_pool_scale), "conv_b": (conv_b, m_conv_b, v_conv_b),
        "conv_w": (conv_w[0], m_conv_w[0], v_conv_w[0]), "w_pool": (wp(w_pool), wp(m_w_pool), wp(v_w_pool))})
    for name in ("b_ada", "g_pre_mix", "g_post_mix", "g_pre_ffn", "g_post_ffn", "pool_scale", "conv_b"):
        res[name] = tuple(small[name])
    res["b_pool"] = tuple(t.reshape(1, 4, 64) for t in small["b_pool"])
    res["conv_w"] = tuple(t[None] for t in small["conv_w"])
    res["w_pool"] = tuple(t.reshape(1, 4, 64, 64) for t in small["w_pool"])

    loss = lax.psum(loss_v[0, 0], ("x", "y", "c"))
    order = ["w_ada", "b_ada", "g_pre_mix", "g_post_mix", "g_pre_ffn", "g_post_ffn", "w_in", "w_pool", "b_pool", "pool_scale",
             "w_out", "w_up", "conv_w", "conv_b", "w_down"]
    outs = [loss, grad_x[None]]
    for k in range(4):
        outs += [res[n][k] for n in order]
    return tuple(outs)
```

```python
import functools
import math

import jax
import jax.numpy as jnp
from jax import lax
from jax.experimental import pallas as pl
from jax.experimental.pallas import tpu as pltpu

F32 = jnp.float32
BF16 = jnp.bfloat16
MESH = pl.DeviceIdType.MESH

D_MODEL = 1024
HEAD_DIM = 64
POOL_W = 256
GROUP_W = 256
DILATIONS = (1, 4, 16)
ATT_BLOCK = 128
IN_W = 2560
D_FF = 2816
HALF_FF = 1408
ROT_DIM = 16
ROPE_THETA = 500000.0
NORM_EPS = 1e-6
N_CHIPS = 4
N_DEV = 8
NEG = -1e30

ADAM_LR = 0.001
ADAM_B1 = 0.9
ADAM_B2 = 0.999
ADAM_EPS = 1e-08
ADAM_WD = 0.01
ADAM_STEP = 10

VMEM_LIMIT = 56 * 1024 * 1024

NT = (((1,), (1,)), ((), ()))
TN = (((0,), (0,)), ((), ()))


def _params(n_grid=0, **kw):
    sem = ("arbitrary",) * n_grid if n_grid else None
    return pltpu.CompilerParams(dimension_semantics=sem, vmem_limit_bytes=VMEM_LIMIT, **kw)


def _full(shape):
    nd = len(shape)
    return pl.BlockSpec(tuple(shape), lambda *_: (0,) * nd)


def _rows(tm, ncol):
    return pl.BlockSpec((tm, ncol), lambda i: (i, 0))


def _acc(ref, val):
    @pl.when(pl.program_id(0) == 0)
    def _():
        ref[...] = jnp.zeros_like(ref)

    ref[...] += val


def _colsum(v):
    return jnp.sum(v, axis=0, keepdims=True)


def _rope128(t, cs, sa, sb, sign):
    return t * cs + sign * (pltpu.roll(t, 8, 1) * sa + pltpu.roll(t, 120, 1) * sb)


def _gelu(z):
    c = math.sqrt(2.0 / math.pi)
    t = jnp.tanh(c * (z + 0.044715 * (z * z * z)))
    return 0.5 * z * (1.0 + t), t


def _gelu_grad(z, t):
    c = math.sqrt(2.0 / math.pi)
    return 0.5 * (1.0 + t) + 0.5 * z * (1.0 - t * t) * (c * (1.0 + 3.0 * 0.044715 * (z * z)))


def _conv_taps(gate, halo, first):
    row = lax.broadcasted_iota(jnp.int32, gate.shape, 0)
    halo = jnp.where(first, 0.0, halo)
    p1 = halo[15:16, :]
    p2 = halo[14:15, :]
    g1 = jnp.where(row == 0, p1, pltpu.roll(gate, 1, 0))
    g2 = jnp.where(row == 0, p2, jnp.where(row == 1, p1, pltpu.roll(gate, 2, 0)))
    return g1, g2


def inproj_fwd(x, g, mod6, w_in_g, tc, tsa, tsb, tm=512):
    S = x.shape[0]

    def body(x_ref, g_ref, mod_ref, w_ref, tc_ref, tsa_ref, tsb_ref, h_ref, u_ref, qkv_ref):
        xv = x_ref[...]
        rstd = lax.rsqrt(jnp.mean(xv * xv, axis=-1, keepdims=True) + NORM_EPS)
        h = ((xv * rstd) * g_ref[...]) * (1.0 + mod_ref[1:2, :]) + mod_ref[0:1, :]
        hb = h.astype(BF16)
        h_ref[...] = hb
        cs, sa, sb = tc_ref[...], tsa_ref[...], tsb_ref[...]
        for j in range(N_CHIPS):
            res = jnp.dot(hb, w_ref[j], preferred_element_type=F32)
            for t in range(5):
                sp = 5 * j + t
                piece, half = sp // 2, sp % 2
                blk = res[:, t * 128:(t + 1) * 128]
                lanes = slice(half * 128, (half + 1) * 128)
                if piece == 0:
                    u_ref[:, lanes] = blk
                else:
                    i9 = piece - 1
                    if i9 < 3:
                        blk = _rope128(blk, cs, sa, sb, 1.0) * (HEAD_DIM ** -0.5)
                    elif i9 < 6:
                        blk = _rope128(blk, cs, sa, sb, 1.0)
                    qkv_ref[i9, :, lanes] = blk.astype(BF16)

    return pl.pallas_call(
        body,
        name="inproj_fwd",
        grid=(S // tm,),
        in_specs=[_rows(tm, D_MODEL), _full((1, D_MODEL)), _full((6, D_MODEL)), _full(w_in_g.shape),
                  _rows(tm, 128), _rows(tm, 128), _rows(tm, 128)],
        out_specs=[_rows(tm, D_MODEL), _rows(tm, POOL_W), pl.BlockSpec((9, tm, GROUP_W), lambda i: (0, i, 0))],
        out_shape=[jax.ShapeDtypeStruct((S, D_MODEL), BF16), jax.ShapeDtypeStruct((S, POOL_W), F32),
                   jax.ShapeDtypeStruct((9, S, GROUP_W), BF16)],
        compiler_params=_params(1),
    )(x, g, mod6, w_in_g, tc, tsa, tsb)


def _attn_masks():
    row = lax.broadcasted_iota(jnp.int32, (ATT_BLOCK, 2 * ATT_BLOCK), 0)
    col = lax.broadcasted_iota(jnp.int32, (ATT_BLOCK, 2 * ATT_BLOCK), 1)
    band = (col >= row) & (col <= row + ATT_BLOCK)
    lane = lax.broadcasted_iota(jnp.int32, (ATT_BLOCK, 128), 1)
    return band, col, lane < HEAD_DIM


def attn_fwd(qkv9, gi, d):
    S = qkv9.shape[1]
    L = S // d
    nb = L // ATT_BLOCK
    qv = qkv9.reshape(9, L, d * GROUP_W)

    def body(q_ref, k_ref, v_ref, o_ref, l_ref, kpad, vpad):
        kpad[0:ATT_BLOCK, :] = jnp.zeros((ATT_BLOCK, GROUP_W), BF16)
        vpad[0:ATT_BLOCK, :] = jnp.zeros((ATT_BLOCK, GROUP_W), BF16)
        kpad[ATT_BLOCK:, :] = k_ref[...]
        vpad[ATT_BLOCK:, :] = v_ref[...]
        band, col, lo = _attn_masks()

        def step(n, carry):
            r0 = pl.multiple_of(n * ATT_BLOCK, ATT_BLOCK)
            valid = band & ((col >= ATT_BLOCK) | (n > 0))
            qb = q_ref[pl.ds(r0, ATT_BLOCK), :]
            kb = kpad[pl.ds(r0, 2 * ATT_BLOCK), :]
            vb = vpad[pl.ds(r0, 2 * ATT_BLOCK), :]
            for pair in range(2):
                lanes = slice(pair * 128, (pair + 1) * 128)
                qp, kp, vp = qb[:, lanes], kb[:, lanes], vb[:, lanes]
                outs, lses = [], []
                for hh in range(2):
                    sel = lo if hh == 0 else jnp.logical_not(lo)
                    qm = jnp.where(sel, qp, jnp.zeros_like(qp))
                    s = lax.dot_general(qm, kp, NT, preferred_element_type=F32)
                    s = jnp.where(valid, s, NEG)
                    m = jnp.max(s, axis=1, keepdims=True)
                    p = jnp.exp(s - m)
                    den = jnp.sum(p, axis=1, keepdims=True)
                    pv = jnp.dot(p.astype(BF16), vp, preferred_element_type=F32)
                    outs.append(pv / den)
                    lses.append(m + jnp.log(den))
                o_ref[pl.ds(r0, ATT_BLOCK), lanes] = jnp.where(lo, outs[0], outs[1])
                l_ref[pl.ds(r0, ATT_BLOCK), lanes] = jnp.where(lo, lses[0], lses[1])
            return carry

        lax.fori_loop(0, nb, step, 0)

    spec = lambda lead: pl.BlockSpec((None, L, GROUP_W), lambda r: (lead, 0, r))
    o, l = pl.pallas_call(
        body,
        name=f"attn_fwd_d{d}",
        grid=(d,),
        in_specs=[spec(gi), spec(3 + gi), spec(6 + gi)],
        out_specs=[pl.BlockSpec((L, GROUP_W), lambda r: (0, r))] * 2,
        out_shape=[jax.ShapeDtypeStruct((L, d * GROUP_W), F32)] * 2,
        scratch_shapes=[pltpu.VMEM((L + ATT_BLOCK, GROUP_W), BF16)] * 2,
        compiler_params=_params(1),
    )(qv, qv, qv)
    return o.reshape(S, GROUP_W), l.reshape(S, GROUP_W)


def _pool_lane_windows(shape):
    lane = lax.broadcasted_iota(jnp.int32, shape, 1)
    return lane, jnp.where(lane < 64, 2, jnp.where(lane < 128, 4, jnp.where(lane < 192, 8, 16)))


def pool_fwd(u, wbd, b, scale):
    S = u.shape[0]

    def body(u_ref, w_ref, b_ref, s_ref, mixed_ref, out_ref):
        uv = u_ref[...]
        row = lax.broadcasted_iota(jnp.int32, uv.shape, 0)
        lane, win = _pool_lane_windows(uv.shape)

        def shift(a, k):
            return jnp.where(row >= k, pltpu.roll(a, k, 0), 0.0)

        s2 = uv + shift(uv, 1)
        s4 = s2 + shift(s2, 2)
        s8 = s4 + shift(s4, 4)
        s16 = s8 + shift(s8, 8)
        tsum = jnp.where(lane < 64, s2, jnp.where(lane < 128, s4, jnp.where(lane < 192, s8, s16)))
        cnt = jnp.minimum(row + 1, win).astype(F32)
        mb = (tsum / cnt - uv).astype(BF16)
        mixed_ref[...] = mb
        y = jnp.dot(mb, w_ref[...], preferred_element_type=F32) + b_ref[...]
        out_ref[...] = (y * s_ref[...]).astype(BF16)

    vm = pl.BlockSpec(memory_space=pltpu.VMEM)
    return pl.pallas_call(
        body,
        name="pool_fwd",
        in_specs=[vm] * 4,
        out_specs=[vm] * 2,
        out_shape=[jax.ShapeDtypeStruct((S, POOL_W), BF16)] * 2,
        compiler_params=_params(),
    )(u, wbd, b, scale)


def outproj_fwd(o_l, pool, x, w_out_g, g_post, g_pre, mod6, tm=512):
    S = x.shape[0]

    def body(o0, o1, o2, l0, l1, l2, pool_ref, x_ref, w_ref, gpost_ref, gpre_ref, mod_ref,
             cat_ref, lse_ref, y1_ref, x1_ref, h2_ref):
        a, b, c = l0[...], l1[...], l2[...]
        m = jnp.maximum(jnp.maximum(a, b), c)
        e0, e1, e2 = jnp.exp(a - m), jnp.exp(b - m), jnp.exp(c - m)
        z = e0 + e1 + e2
        lse_ref[...] = m + jnp.log(z)
        attn = (e0 * o0[...] + e1 * o1[...] + e2 * o2[...]) / z
        cat = jnp.concatenate([pool_ref[...], attn.astype(BF16)], axis=1)
        cat_ref[...] = cat
        y1 = jnp.concatenate([jnp.dot(cat, w_ref[j], preferred_element_type=F32) for j in range(N_CHIPS)], axis=1)
        y1_ref[...] = y1
        rstd = lax.rsqrt(jnp.mean(y1 * y1, axis=-1, keepdims=True) + NORM_EPS)
        x1 = x_ref[...] + mod_ref[2:3, :] * ((y1 * rstd) * gpost_ref[...])
        x1_ref[...] = x1
        rstd2 = lax.rsqrt(jnp.mean(x1 * x1, axis=-1, keepdims=True) + NORM_EPS)
        h2 = ((x1 * rstd2) * gpre_ref[...]) * (1.0 + mod_ref[4:5, :]) + mod_ref[3:4, :]
        h2_ref[...] = h2.astype(BF16)

    t256 = _rows(tm, GROUP_W)
    return pl.pallas_call(
        body,
        name="outproj_fwd",
        grid=(S // tm,),
        in_specs=[t256] * 7 + [_rows(tm, D_MODEL), _full(w_out_g.shape), _full((1, D_MODEL)), _full((1, D_MODEL)),
                               _full((6, D_MODEL))],
        out_specs=[_rows(tm, 512), t256, _rows(tm, D_MODEL), _rows(tm, D_MODEL), _rows(tm, D_MODEL)],
        out_shape=[jax.ShapeDtypeStruct((S, 512), BF16), jax.ShapeDtypeStruct((S, GROUP_W), F32),
                   jax.ShapeDtypeStruct((S, D_MODEL), F32), jax.ShapeDtypeStruct((S, D_MODEL), F32),
                   jax.ShapeDtypeStruct((S, D_MODEL), BF16)],
        compiler_params=_params(1),
    )(*o_l, pool, x, w_out_g, g_post, g_pre, mod6)


def up_fwd(h2, w_up_g, tm=512):
    S = h2.shape[0]

    def body(h_ref, w_ref, gate_ref, val_ref):
        hb = h_ref[...]
        for j in range(N_CHIPS):
            res = jnp.dot(hb, w_ref[j], preferred_element_type=F32).astype(BF16)
            dst = gate_ref if j < 2 else val_ref
            dst[:, (j % 2) * HALF_FF:(j % 2 + 1) * HALF_FF] = res

    return pl.pallas_call(
        body,
        name="up_fwd",
        grid=(S // tm,),
        in_specs=[_rows(tm, D_MODEL), _full(w_up_g.shape)],
        out_specs=[_rows(tm, D_FF)] * 2,
        out_shape=[jax.ShapeDtypeStruct((S, D_FF), BF16)] * 2,
        compiler_params=_params(1),
    )(h2, w_up_g)


def _halo_prev(tm, ncol):
    return pl.BlockSpec((16, ncol), lambda i: (jnp.maximum(i * (tm // 16) - 1, 0), 0))


def down_fwd(gate, val, conv_w, conv_b, w_down, x1, target, g_post, mod6, tm=256):
    S = x1.shape[0]

    def body(gate_ref, halo_ref, val_ref, cw_ref, cb_ref, w_ref, x1_ref, tgt_ref, g_ref, mod_ref,
             a_ref, dy2_ref, dout_ref, loss_ref, dgt_ref, dg_ref):
        first = pl.program_id(0) == 0
        y2 = jnp.zeros((tm, D_MODEL), F32)
        for ch in range(2):
            cols = slice(ch * HALF_FF, (ch + 1) * HALF_FF)
            gt = gate_ref[:, cols].astype(F32)
            g1, g2 = _conv_taps(gt, halo_ref[:, cols].astype(F32), first)
            gc = g2 * cw_ref[0:1, cols] + g1 * cw_ref[1:2, cols] + gt * cw_ref[2:3, cols] + cb_ref[:, cols]
            ge, _ = _gelu(gc)
            ab = (ge * val_ref[:, cols].astype(F32)).astype(BF16)
            a_ref[:, cols] = ab
            y2 = y2 + jnp.dot(ab, w_ref[cols, :], preferred_element_type=F32)
        rstd = lax.rsqrt(jnp.mean(y2 * y2, axis=-1, keepdims=True) + NORM_EPS)
        y2n = y2 * rstd
        gv = g_ref[...]
        gtf = mod_ref[5:6, :]
        r2 = y2n * gv
        diff = (x1_ref[...] + gtf * r2) - tgt_ref[...]
        _acc(loss_ref, jnp.zeros((1, 128), F32) + 0.5 * jnp.sum(diff * diff) * (1.0 / D_MODEL))
        dout = diff * (1.0 / D_MODEL)
        dout_ref[...] = dout
        _acc(dgt_ref, _colsum(dout * r2))
        dr2 = dout * gtf
        _acc(dg_ref, _colsum(dr2 * y2n))
        dyn = dr2 * gv
        dy2 = rstd * (dyn - y2n * jnp.mean(dyn * y2n, axis=-1, keepdims=True))
        dy2_ref[...] = dy2.astype(BF16)

    vec = _full((1, D_MODEL))
    return pl.pallas_call(
        body,
        name="down_fwd",
        grid=(S // tm,),
        in_specs=[_rows(tm, D_FF), _halo_prev(tm, D_FF), _rows(tm, D_FF), _full((3, D_FF)), _full((1, D_FF)),
                  _full((D_FF, D_MODEL)), _rows(tm, D_MODEL), _rows(tm, D_MODEL), vec, _full((6, D_MODEL))],
        out_specs=[_rows(tm, D_FF), _rows(tm, D_MODEL), _rows(tm, D_MODEL), _full((1, 128)), vec, vec],
        out_shape=[jax.ShapeDtypeStruct((S, D_FF), BF16), jax.ShapeDtypeStruct((S, D_MODEL), BF16),
                   jax.ShapeDtypeStruct((S, D_MODEL), F32), jax.ShapeDtypeStruct((1, 128), F32),
                   jax.ShapeDtypeStruct((1, D_MODEL), F32), jax.ShapeDtypeStruct((1, D_MODEL), F32)],
        compiler_params=_params(1),
    )(gate, gate, val, conv_w, conv_b, w_down, x1, target, g_post, mod6)


def down_bwd(dy2, w_down, gate, val, conv_w, conv_b, tm=256):
    S = dy2.shape[0]

    def body(dy_ref, w_ref, gate_ref, halo_ref, val_ref, cw_ref, cb_ref, dgc_ref, dval_ref, dcw_ref, dcb_ref):
        first = pl.program_id(0) == 0

        @pl.when(first)
        def _():
            dcw_ref[...] = jnp.zeros_like(dcw_ref)
            dcb_ref[...] = jnp.zeros_like(dcb_ref)

        dyb = dy_ref[...]
        for ch in range(2):
            cols = slice(ch * HALF_FF, (ch + 1) * HALF_FF)
            da = lax.dot_general(dyb, w_ref[cols, :], NT, preferred_element_type=F32)
            gt = gate_ref[:, cols].astype(F32)
            g1, g2 = _conv_taps(gt, halo_ref[:, cols].astype(F32), first)
            gc = g2 * cw_ref[0:1, cols] + g1 * cw_ref[1:2, cols] + gt * cw_ref[2:3, cols] + cb_ref[:, cols]
            ge, th = _gelu(gc)
            dgc = da * val_ref[:, cols].astype(F32) * _gelu_grad(gc, th)
            dgc_ref[:, cols] = dgc.astype(BF16)
            dval_ref[:, cols] = (da * ge).astype(BF16)
            dcb_ref[:, cols] += _colsum(dgc)
            dcw_ref[0:1, cols] += _colsum(dgc * g2)
            dcw_ref[1:2, cols] += _colsum(dgc * g1)
            dcw_ref[2:3, cols] += _colsum(dgc * gt)

    return pl.pallas_call(
        body,
        name="down_bwd",
        grid=(S // tm,),
        in_specs=[_rows(tm, D_MODEL), _full((D_FF, D_MODEL)), _rows(tm, D_FF), _halo_prev(tm, D_FF), _rows(tm, D_FF),
                  _full((3, D_FF)), _full((1, D_FF))],
        out_specs=[_rows(tm, D_FF), _rows(tm, D_FF), _full((3, D_FF)), _full((1, D_FF))],
        out_shape=[jax.ShapeDtypeStruct((S, D_FF), BF16), jax.ShapeDtypeStruct((S, D_FF), BF16),
                   jax.ShapeDtypeStruct((3, D_FF), F32), jax.ShapeDtypeStruct((1, D_FF), F32)],
        compiler_params=_params(1),
    )(dy2, w_down, gate, gate, val, conv_w, conv_b)


def dw_matmul(a, b, out_blocks, blk_shape, a_cols, b_cols, a_blocked, name, prev=None, blk_off=0, n_blk=None, tm=512):
    S = a.shape[0]
    n_blk = out_blocks if n_blk is None else n_blk

    def body(*refs):
        a_ref, b_ref, o_ref = refs[0], refs[1], refs[-1]

        @pl.when(pl.program_id(1) == 0)
        def _():
            o_ref[...] = jnp.zeros_like(o_ref)

        o_ref[...] += lax.dot_general(a_ref[...], b_ref[...], TN, preferred_element_type=F32)

    a_spec = pl.BlockSpec((tm, a_cols), (lambda j, i: (i, j)) if a_blocked else (lambda j, i: (i, 0)))
    b_spec = pl.BlockSpec((tm, b_cols), (lambda j, i: (i, 0)) if a_blocked else (lambda j, i: (i, j)))
    in_specs = [a_spec, b_spec]
    args = [a, b]
    aliases = {}
    if prev is not None:
        in_specs.append(pl.BlockSpec(memory_space=pl.ANY))
        args.append(prev)
        aliases = {2: 0}
    return pl.pallas_call(
        body,
        name=name,
        grid=(n_blk, S // tm),
        in_specs=in_specs,
        out_specs=pl.BlockSpec((None,) + tuple(blk_shape), lambda j, i: (j + blk_off, 0, 0)),
        out_shape=jax.ShapeDtypeStruct((out_blocks,) + tuple(blk_shape), F32),
        input_output_aliases=aliases,
        compiler_params=_params(2),
    )(*args)


def up_bwd(dgc, dval, conv_w, w_up_g, x1, dout, y1, g_pre, g_post, mod6, tm=256):
    S = x1.shape[0]
    last_blk = S // 16 - 1

    def body(dgc_ref, nxt_ref, dval_ref, cw_ref, w_ref, x1_ref, dout_ref, y1_ref, gpre_ref, gpost_ref, mod_ref,
             dgate_ref, dx1_ref, dy1_ref, dsh_ref, dsc_ref, dgpre_ref, dgt_ref, dgpost_ref):
        last = pl.program_id(0) == pl.num_programs(0) - 1
        dh = jnp.zeros((tm, D_MODEL), F32)
        for ch in range(2):
            cols = slice(ch * HALF_FF, (ch + 1) * HALF_FF)
            dg = dgc_ref[:, cols].astype(F32)
            nx = jnp.where(last, 0.0, nxt_ref[:, cols].astype(F32))
            row = lax.broadcasted_iota(jnp.int32, dg.shape, 0)
            n0, n1 = nx[0:1, :], nx[1:2, :]
            u1 = jnp.where(row == tm - 1, n0, pltpu.roll(dg, tm - 1, 0))
            u2 = jnp.where(row == tm - 1, n1, jnp.where(row == tm - 2, n0, pltpu.roll(dg, tm - 2, 0)))
            dgate = (dg * cw_ref[2:3, cols] + u1 * cw_ref[1:2, cols] + u2 * cw_ref[0:1, cols]).astype(BF16)
            dgate_ref[:, cols] = dgate
            dh = dh + lax.dot_general(dgate, w_ref[ch], NT, preferred_element_type=F32)
            dh = dh + lax.dot_general(dval_ref[:, cols], w_ref[2 + ch], NT, preferred_element_type=F32)
        x1 = x1_ref[...]
        rstd = lax.rsqrt(jnp.mean(x1 * x1, axis=-1, keepdims=True) + NORM_EPS)
        n2 = x1 * rstd
        gpre = gpre_ref[...]
        one_sc = 1.0 + mod_ref[4:5, :]
        _acc(dsh_ref, _colsum(dh))
        _acc(dsc_ref, _colsum(dh * (n2 * gpre)))
        _acc(dgpre_ref, _colsum(dh * one_sc * n2))
        dn = dh * (gpre * one_sc)
        dx1 = dout_ref[...] + rstd * (dn - n2 * jnp.mean(dn * n2, axis=-1, keepdims=True))
        dx1_ref[...] = dx1
        y1 = y1_ref[...]
        rstd1 = lax.rsqrt(jnp.mean(y1 * y1, axis=-1, keepdims=True) + NORM_EPS)
        y1n = y1 * rstd1
        gpost = gpost_ref[...]
        gtm = mod_ref[2:3, :]
        _acc(dgt_ref, _colsum(dx1 * (y1n * gpost)))
        dr1 = dx1 * gtm
        _acc(dgpost_ref, _colsum(dr1 * y1n))
        dyn = dr1 * gpost
        dy1 = rstd1 * (dyn - y1n * jnp.mean(dyn * y1n, axis=-1, keepdims=True))
        dy1_ref[...] = dy1.astype(BF16)

    vec = _full((1, D_MODEL))
    nxt = pl.BlockSpec((16, D_FF), lambda i: (jnp.minimum((i + 1) * (tm // 16), last_blk), 0))
    return pl.pallas_call(
        body,
        name="up_bwd",
        grid=(S // tm,),
        in_specs=[_rows(tm, D_FF), nxt, _rows(tm, D_FF), _full((3, D_FF)), _full(w_up_g.shape), _rows(tm, D_MODEL),
                  _rows(tm, D_MODEL), _rows(tm, D_MODEL), vec, vec, _full((6, D_MODEL))],
        out_specs=[_rows(tm, D_FF), _rows(tm, D_MODEL), _rows(tm, D_MODEL), vec, vec, vec, vec, vec],
        out_shape=[jax.ShapeDtypeStruct((S, D_FF), BF16), jax.ShapeDtypeStruct((S, D_MODEL), F32),
                   jax.ShapeDtypeStruct((S, D_MODEL), BF16)] + [jax.ShapeDtypeStruct((1, D_MODEL), F32)] * 5,
        compiler_params=_params(1),
    )(dgc, dgc, dval, conv_w, w_up_g, x1, dout, y1, g_pre, g_post, mod6)


def outproj_bwd(dy1, w_out_g, cat, tm=512):
    S = dy1.shape[0]

    def body(dy_ref, w_ref, attn_ref, dpool_ref, dattn_ref, delta_ref):
        dcat = jnp.zeros((tm, 512), F32)
        for j in range(N_CHIPS):
            dcat = dcat + lax.dot_general(dy_ref[:, j * 256:(j + 1) * 256], w_ref[j], NT, preferred_element_type=F32)
        dpool_ref[...] = dcat[:, :POOL_W]
        dattn = dcat[:, POOL_W:]
        dattn_ref[...] = dattn.astype(BF16)
        prod = dattn * attn_ref[...].astype(F32)
        r = lax.broadcasted_iota(jnp.int32, (GROUP_W, GROUP_W), 0) // HEAD_DIM
        c = lax.broadcasted_iota(jnp.int32, (GROUP_W, GROUP_W), 1) // HEAD_DIM
        ones_bd = jnp.where(r == c, 1.0, 0.0).astype(BF16)
        hi = prod.astype(BF16)
        lo = (prod - hi.astype(F32)).astype(BF16)
        delta_ref[...] = (jnp.dot(hi, ones_bd, preferred_element_type=F32)
                          + jnp.dot(lo, ones_bd, preferred_element_type=F32))

    return pl.pallas_call(
        body,
        name="outproj_bwd",
        grid=(S // tm,),
        in_specs=[_rows(tm, D_MODEL), _full(w_out_g.shape), pl.BlockSpec((tm, GROUP_W), lambda i: (i, 1))],
        out_specs=[_rows(tm, POOL_W), _rows(tm, GROUP_W), _rows(tm, GROUP_W)],
        out_shape=[jax.ShapeDtypeStruct((S, POOL_W), F32), jax.ShapeDtypeStruct((S, GROUP_W), BF16),
                   jax.ShapeDtypeStruct((S, GROUP_W), F32)],
        compiler_params=_params(1),
    )(dy1, w_out_g, cat)


def attn_bwd(qkv9, dattn, lse, delta, gi, d):
    S = qkv9.shape[1]
    L = S // d
    nb = L // ATT_BLOCK
    qv = qkv9.reshape(9, L, d * GROUP_W)
    view = lambda t: t.reshape(L, d * GROUP_W)

    def body(q_ref, k_ref, v_ref, do_ref, l_ref, dl_ref, out_ref, kpad, vpad, dkpad, dvpad):
        kpad[0:ATT_BLOCK, :] = jnp.zeros((ATT_BLOCK, GROUP_W), BF16)
        vpad[0:ATT_BLOCK, :] = jnp.zeros((ATT_BLOCK, GROUP_W), BF16)
        kpad[ATT_BLOCK:, :] = k_ref[...]
        vpad[ATT_BLOCK:, :] = v_ref[...]
        dkpad[...] = jnp.zeros_like(dkpad)
        dvpad[...] = jnp.zeros_like(dvpad)
        band, col, lo = _attn_masks()

        def step(n, carry):
            r0 = pl.multiple_of(n * ATT_BLOCK, ATT_BLOCK)
            valid = band & ((col >= ATT_BLOCK) | (n > 0))
            qb = q_ref[pl.ds(r0, ATT_BLOCK), :]
            dob = do_ref[pl.ds(r0, ATT_BLOCK), :]
            lb = l_ref[pl.ds(r0, ATT_BLOCK), :]
            dlb = dl_ref[pl.ds(r0, ATT_BLOCK), :]
            kb = kpad[pl.ds(r0, 2 * ATT_BLOCK), :]
            vb = vpad[pl.ds(r0, 2 * ATT_BLOCK), :]
            for pair in range(2):
                lanes = slice(pair * 128, (pair + 1) * 128)
                qp, dop, kp, vp = qb[:, lanes], dob[:, lanes], kb[:, lanes], vb[:, lanes]
                dqs = []
                dk_sum = jnp.zeros((2 * ATT_BLOCK, 128), F32)
                dv_sum = jnp.zeros((2 * ATT_BLOCK, 128), F32)
                for hh in range(2):
                    sel = lo if hh == 0 else jnp.logical_not(lo)
                    qm = jnp.where(sel, qp, jnp.zeros_like(qp))
                    dom = jnp.where(sel, dop, jnp.zeros_like(dop))
                    cc = pair * 128 + hh * HEAD_DIM
                    s = lax.dot_general(qm, kp, NT, preferred_element_type=F32)
                    s = jnp.where(valid, s, NEG)
                    p = jnp.exp(s - lb[:, cc:cc + 1])
                    dp = lax.dot_general(dom, vp, NT, preferred_element_type=F32)
                    ds = (p * (dp - dlb[:, cc:cc + 1])).astype(BF16)
                    pb = p.astype(BF16)
                    dqs.append(jnp.dot(ds, kp, preferred_element_type=F32))
                    dk_sum = dk_sum + lax.dot_general(ds, qm, TN, preferred_element_type=F32)
                    dv_sum = dv_sum + lax.dot_general(pb, dom, TN, preferred_element_type=F32)
                out_ref[0, pl.ds(r0, ATT_BLOCK), lanes] = jnp.where(lo, dqs[0], dqs[1])
                dkpad[pl.ds(r0, 2 * ATT_BLOCK), lanes] += dk_sum
                dvpad[pl.ds(r0, 2 * ATT_BLOCK), lanes] += dv_sum
            return carry

        lax.fori_loop(0, nb, step, 0)
        out_ref[1] = dkpad[ATT_BLOCK:, :]
        out_ref[2] = dvpad[ATT_BLOCK:, :]

    spec = lambda lead: pl.BlockSpec((None, L, GROUP_W), lambda r: (lead, 0, r))
    cls = pl.BlockSpec((L, GROUP_W), lambda r: (0, r))
    out = pl.pallas_call(
        body,
        name=f"attn_bwd_d{d}",
        grid=(d,),
        in_specs=[spec(gi), spec(3 + gi), spec(6 + gi), cls, cls, cls],
        out_specs=pl.BlockSpec((3, L, GROUP_W), lambda r: (0, 0, r)),
        out_shape=jax.ShapeDtypeStruct((3, L, d * GROUP_W), F32),
        scratch_shapes=[pltpu.VMEM((L + ATT_BLOCK, GROUP_W), BF16)] * 2 + [pltpu.VMEM((L + ATT_BLOCK, GROUP_W), F32)] * 2,
        compiler_params=_params(1),
    )(qv, qv, qv, view(dattn), view(lse), view(delta))
    return out.reshape(3, S, GROUP_W)


def pool_bwd(dpool, mixed, wbd, b, scale):
    S = dpool.shape[0]

    def body(dp_ref, mx_ref, w_ref, b_ref, s_ref, du_ref, dw_ref, db_ref, ds_ref):
        dp = dp_ref[...]
        mb = mx_ref[...]
        wv = w_ref[...]
        ypre = jnp.dot(mb, wv, preferred_element_type=F32) + b_ref[...]
        ds_ref[...] = _colsum(dp * ypre)
        dpre = dp * s_ref[...]
        db_ref[...] = _colsum(dpre)
        dpb = dpre.astype(BF16)
        dw_ref[...] = lax.dot_general(mb, dpb, TN, preferred_element_type=F32)
        dmix = lax.dot_general(dpb, wv, NT, preferred_element_type=F32)
        row = lax.broadcasted_iota(jnp.int32, dmix.shape, 0)
        lane, win = _pool_lane_windows(dmix.shape)
        e = dmix / jnp.minimum(row + 1, win).astype(F32)

        def shift(a, k):
            return jnp.where(row < S - k, pltpu.roll(a, S - k, 0), 0.0)

        f2 = e + shift(e, 1)
        f4 = f2 + shift(f2, 2)
        f8 = f4 + shift(f4, 4)
        f16 = f8 + shift(f8, 8)
        du_ref[...] = jnp.where(lane < 64, f2, jnp.where(lane < 128, f4, jnp.where(lane < 192, f8, f16))) - dmix

    vm = pl.BlockSpec(memory_space=pltpu.VMEM)
    return pl.pallas_call(
        body,
        name="pool_bwd",
        in_specs=[vm] * 5,
        out_specs=[vm] * 4,
        out_shape=[jax.ShapeDtypeStruct((S, POOL_W), F32), jax.ShapeDtypeStruct((POOL_W, POOL_W), F32),
                   jax.ShapeDtypeStruct((1, POOL_W), F32), jax.ShapeDtypeStruct((1, POOL_W), F32)],
        compiler_params=_params(),
    )(dpool, mixed, wbd, b, scale)


def inproj_bwd(dqkv, du, x, dx1, w_in_g, g, mod6, tc, tsa, tsb, tm=256):
    S = x.shape[0]

    def body(d0, d1, d2, du_ref, x_ref, dx1_ref, w_ref, g_ref, mod_ref, tc_ref, tsa_ref, tsb_ref,
             dp_ref, gx_ref, dsh_ref, dsc_ref, dg_ref):
        cs, sa, sb = tc_ref[...], tsa_ref[...], tsb_ref[...]
        dgrp = (d0, d1, d2)
        for sp in range(20):
            piece, half = sp // 2, sp % 2
            lanes = slice(half * 128, (half + 1) * 128)
            if piece == 0:
                blk = du_ref[:, lanes]
            else:
                kind, gi = (piece - 1) // 3, (piece - 1) % 3
                blk = dgrp[gi][kind, :, lanes]
                if kind == 0:
                    blk = _rope128(blk, cs, sa, sb, -1.0) * (HEAD_DIM ** -0.5)
                elif kind == 1:
                    blk = _rope128(blk, cs, sa, sb, -1.0)
            dp_ref[:, sp * 128:(sp + 1) * 128] = blk.astype(BF16)
        dh = jnp.zeros((tm, D_MODEL), F32)
        for j in range(N_CHIPS):
            dh = dh + lax.dot_general(dp_ref[:, j * 640:(j + 1) * 640], w_ref[j], NT, preferred_element_type=F32)
        xv = x_ref[...]
        rstd = lax.rsqrt(jnp.mean(xv * xv, axis=-1, keepdims=True) + NORM_EPS)
        n1 = xv * rstd
        gv = g_ref[...]
        one_sc = 1.0 + mod_ref[1:2, :]
        _acc(dsh_ref, _colsum(dh))
        _acc(dsc_ref, _colsum(dh * (n1 * gv)))
        _acc(dg_ref, _colsum(dh * one_sc * n1))
        dn = dh * (gv * one_sc)
        gx_ref[...] = dx1_ref[...] + rstd * (dn - n1 * jnp.mean(dn * n1, axis=-1, keepdims=True))

    vec = _full((1, D_MODEL))
    dspec = pl.BlockSpec((3, tm, GROUP_W), lambda i: (0, i, 0))
    return pl.pallas_call(
        body,
        name="inproj_bwd",
        grid=(S // tm,),
        in_specs=[dspec] * 3 + [_rows(tm, POOL_W), _rows(tm, D_MODEL), _rows(tm, D_MODEL), _full(w_in_g.shape), vec,
                                _full((6, D_MODEL)), _rows(tm, 128), _rows(tm, 128), _rows(tm, 128)],
        out_specs=[_rows(tm, IN_W), _rows(tm, D_MODEL), vec, vec, vec],
        out_shape=[jax.ShapeDtypeStruct((S, IN_W), BF16), jax.ShapeDtypeStruct((S, D_MODEL), F32)]
        + [jax.ShapeDtypeStruct((1, D_MODEL), F32)] * 3,
        compiler_params=_params(1),
    )(*dqkv, du, x, dx1, w_in_g, g, mod6, tc, tsa, tsb)


def _adamw(w, g, m, v):
    m = ADAM_B1 * m + (1.0 - ADAM_B1) * g
    v = ADAM_B2 * v + (1.0 - ADAM_B2) * (g * g)
    m_hat = m / (1.0 - ADAM_B1 ** ADAM_STEP)
    v_hat = v / (1.0 - ADAM_B2 ** ADAM_STEP)
    delta = -ADAM_LR * (m_hat / (jnp.sqrt(v_hat) + ADAM_EPS) + ADAM_WD * w)
    return delta, m, v


def adamw_rows(w, g, m, v, tr, name):
    R, C = w.shape

    def body(w_ref, g_ref, m_ref, v_ref, d_ref, mo_ref, vo_ref):
        d_ref[...], mo_ref[...], vo_ref[...] = _adamw(w_ref[...], g_ref[...], m_ref[...], v_ref[...])

    spec = pl.BlockSpec((tr, C), lambda i: (i, 0))
    return pl.pallas_call(
        body,
        name=name,
        grid=(R // tr,),
        in_specs=[spec] * 4,
        out_specs=[spec] * 3,
        out_shape=[jax.ShapeDtypeStruct((R, C), F32)] * 3,
        compiler_params=_params(1),
    )(w, g, m, v)


def adamw_ada(c_all_t, dmod_cols, w, m, v, tr=256):
    R, C = w.shape

    def body(ct_ref, dm_ref, w_ref, m_ref, v_ref, g_ref, d_ref, mo_ref, vo_ref):
        ct = ct_ref[...]
        act = ct * jax.nn.sigmoid(ct)
        g = jnp.zeros((tr, C), F32)
        for b in range(N_DEV):
            g = g + act[:, b:b + 1] * dm_ref[b:b + 1, :]
        g_ref[...] = g
        d_ref[...], mo_ref[...], vo_ref[...] = _adamw(w_ref[...], g, m_ref[...], v_ref[...])

    spec = pl.BlockSpec((tr, C), lambda i: (i, 0))
    return pl.pallas_call(
        body,
        name="adamw_ada",
        grid=(R // tr,),
        in_specs=[pl.BlockSpec((tr, N_DEV), lambda i: (i, 0)), _full((N_DEV, C)), spec, spec, spec],
        out_specs=[spec] * 4,
        out_shape=[jax.ShapeDtypeStruct((R, C), F32)] * 4,
        compiler_params=_params(1),
    )(c_all_t, dmod_cols, w, m, v)


def adamw_small(slab_a, slab_b, convw_g, wpool_g, params):
    names = ["b_ada", "g_pre_mix", "g_post_mix", "g_pre_ffn", "g_post_ffn", "b_pool", "pool_scale", "conv_b", "conv_w", "w_pool"]
    flat = []
    for n in names:
        flat += list(params[n])

    def body(a_ref, b_ref, cw_ref, wp_ref, *rest):
        ins, outs = rest[:30], rest[30:]

        def dev_sum(ref):
            t = ref[0]
            for dev in range(1, N_DEV):
                t = t + ref[dev]
            return t

        sa, sb_, scw, swp = dev_sum(a_ref), dev_sum(b_ref), dev_sum(cw_ref), dev_sum(wp_ref)
        grads = [
            jnp.concatenate([sa[k:k + 1, :] for k in range(6)], axis=1),
            sa[6:7, :], sa[7:8, :], sa[8:9, :], sa[9:10, :],
            sa[10:11, 0:256], sa[10:11, 256:512],
            sb_[3:4, :], scw, swp,
        ]
        for i, g in enumerate(grads):
            w_ref, m_ref, v_ref = ins[3 * i:3 * i + 3]
            d, mo, vo = _adamw(w_ref[...], g, m_ref[...], v_ref[...])
            outs[4 * i][...] = g
            outs[4 * i + 1][...] = d
            outs[4 * i + 2][...] = mo
            outs[4 * i + 3][...] = vo

    vm = pl.BlockSpec(memory_space=pltpu.VMEM)
    out_shape = []
    for n in names:
        out_shape += [jax.ShapeDtypeStruct(params[n][0].shape, F32)] * 4
    outs = pl.pallas_call(
        body,
        name="adamw_small",
        in_specs=[vm] * (4 + len(flat)),
        out_specs=[vm] * len(out_shape),
        out_shape=out_shape,
        compiler_params=_params(),
    )(slab_a, slab_b, convw_g, wpool_g, *flat)
    return {n: outs[4 * i:4 * i + 4] for i, n in enumerate(names)}


def _place():
    return lax.axis_index("x"), lax.axis_index("y"), lax.axis_index("c")


def _other_chips(x, y):
    return [(1 - x, y), (x, 1 - y), (1 - x, 1 - y)]


def _chip_id(cx, cy):
    return 2 * cx + cy


def gather_weights(shards):
    n = len(shards)
    halved = [s.shape[0] % 32 == 0 for s in shards]

    def body(*refs):
        ins, outs = refs[:n], refs[n:2 * n]
        send_sems, recv_sems, loc_sems = refs[2 * n:]
        x, y, c = _place()
        me = _chip_id(x, y)
        chips = _other_chips(x, y)
        sib = (x, y, 1 - c)

        def part(w, chip, half):
            if not halved[w]:
                return outs[w].at[chip]
            rh = shards[w].shape[0] // 2
            return outs[w].at[chip, pl.ds(half * rh, rh), :]

        def src_part(w):
            if not halved[w]:
                return ins[w]
            rh = shards[w].shape[0] // 2
            return ins[w].at[pl.ds(c * rh, rh), :]

        def rcopy(w, k, src, dst, to):
            return pltpu.make_async_remote_copy(src_ref=src, dst_ref=dst, send_sem=send_sems.at[6 * w + k],
                                                recv_sem=recv_sems.at[6 * w + k], device_id=to, device_id_type=MESH)

        local = [pltpu.make_async_copy(ins[w], outs[w].at[me], loc_sems.at[w]) for w in range(n)]
        for cp in local:
            cp.start()
        first = []
        for w in range(n):
            for k, (cx, cy) in enumerate(chips):
                cp = rcopy(w, k, src_part(w), part(w, me, c), (cx, cy, c))
                cp.start()
                first.append(cp)
        passed = []
        for w in range(n):
            for k, (cx, cy) in enumerate(chips):
                blk = part(w, _chip_id(cx, cy), c)
                rcopy(w, k, blk, blk, (cx, cy, c)).wait_recv()
                if halved[w]:
                    cp = rcopy(w, 3 + k, blk, blk, sib)
                    cp.start()
                    passed.append(cp)
        for w in range(n):
            if halved[w]:
                for k, (cx, cy) in enumerate(chips):
                    blk = part(w, _chip_id(cx, cy), 1 - c)
                    rcopy(w, 3 + k, blk, blk, sib).wait_recv()
        for cp in first + passed:
            cp.wait_send()
        for cp in local:
            cp.wait()

    hbm = pl.BlockSpec(memory_space=pl.ANY)
    return pl.pallas_call(
        body,
        name="gather_weights",
        in_specs=[hbm] * n,
        out_specs=[hbm] * n,
        out_shape=[jax.ShapeDtypeStruct((N_CHIPS,) + s.shape, s.dtype) for s in shards],
        scratch_shapes=[pltpu.SemaphoreType.DMA((6 * n,)), pltpu.SemaphoreType.DMA((6 * n,)), pltpu.SemaphoreType.DMA((n,))],
        compiler_params=pltpu.CompilerParams(has_side_effects=True, vmem_limit_bytes=VMEM_LIMIT),
    )(*shards)


HBM_SPEC = pl.BlockSpec(memory_space=pltpu.HBM)
SEM_SPEC = pl.BlockSpec(memory_space=pltpu.SEMAPHORE)
ANY_SPEC = pl.BlockSpec(memory_space=pl.ANY)
EFFECT = pltpu.SideEffectType.DATAFLOW_SIDE_EFFECTING


def _hbm(t):
    return pltpu.with_memory_space_constraint(t, pltpu.HBM)


def _hbm_shapes(ts):
    return [pltpu.HBM(t.shape, t.dtype) for t in ts]


def _half_rows(ref, lead, half, rh):
    return ref.at[lead, pl.ds(half * rh, rh), :]


def gather_split_start(shards, lands, carry, k):
    n = len(shards)

    def body(*refs):
        ins, land = refs[:n], refs[n:2 * n]
        send_sems, recv_sems = refs[2 * n + 1], refs[2 * n + 2]
        loc_sems = refs[-1]
        x, y, c = _place()
        me = _chip_id(x, y)
        if k == 0:
            local = [pltpu.make_async_copy(ins[w], land[w].at[me], loc_sems.at[w]) for w in range(n)]
            for cp in local:
                cp.start()
            for cp in local:
                cp.wait()
        cx, cy = _other_chips(x, y)[k]
        for w in range(n):
            rh = shards[w].shape[0] // 2
            pltpu.make_async_remote_copy(src_ref=ins[w].at[pl.ds(c * rh, rh), :], dst_ref=_half_rows(land[w], me, c, rh),
                                         send_sem=send_sems.at[w], recv_sem=recv_sems.at[w],
                                         device_id=(cx, cy, c), device_id_type=MESH).start()

    args = [_hbm(s) for s in shards] + [_hbm(l) for l in lands] + [_hbm(carry)]
    outs = pl.pallas_call(
        body,
        name="gather_split_start%d" % k,
        out_shape=[pltpu.SemaphoreType.DMA((n,)), pltpu.SemaphoreType.DMA((n,))] + _hbm_shapes(shards) + _hbm_shapes(lands)
        + _hbm_shapes([carry]),
        in_specs=[HBM_SPEC] * (2 * n + 1),
        out_specs=[SEM_SPEC, SEM_SPEC] + [HBM_SPEC] * (2 * n + 1),
        input_output_aliases={i: 2 + i for i in range(2 * n + 1)},
        scratch_shapes=[pltpu.SemaphoreType.DMA((n,))],
        compiler_params=pltpu.CompilerParams(has_side_effects=EFFECT),
    )(*args)
    return outs[0], outs[1], list(outs[2:2 + n]), list(outs[2 + n:2 + 2 * n]), outs[-1]


def gather_split_mid(sems, shards, lands, after):
    n = len(shards)

    def body(*refs):
        ins, land = refs[:n], refs[n:2 * n]
        sem_in = refs[2 * n:2 * n + 6]
        fsend, frecv = refs[2 * n + 7], refs[2 * n + 8]
        x, y, c = _place()
        me = _chip_id(x, y)
        chips = _other_chips(x, y)
        for w in range(n):
            rh = shards[w].shape[0] // 2
            for k, (cx, cy) in enumerate(chips):
                got = _half_rows(land[w], _chip_id(cx, cy), c, rh)
                cp = pltpu.make_async_remote_copy(src_ref=ins[w].at[pl.ds(c * rh, rh), :], dst_ref=got, send_sem=sem_in[2 * k].at[w],
                                                  recv_sem=sem_in[2 * k + 1].at[w], device_id=(cx, cy, c), device_id_type=MESH)
                cp.wait_send()
                cp.wait_recv()
        for w in range(n):
            rh = shards[w].shape[0] // 2
            for k, (cx, cy) in enumerate(chips):
                got = _half_rows(land[w], _chip_id(cx, cy), c, rh)
                pltpu.make_async_remote_copy(src_ref=got, dst_ref=got, send_sem=fsend.at[3 * w + k], recv_sem=frecv.at[3 * w + k],
                                             device_id=(x, y, 1 - c), device_id_type=MESH).start()

    outs = pl.pallas_call(
        body,
        name="gather_split_mid",
        out_shape=[pltpu.SemaphoreType.DMA((3 * n,)), pltpu.SemaphoreType.DMA((3 * n,))] + _hbm_shapes(lands),
        in_specs=[HBM_SPEC] * (2 * n) + [SEM_SPEC] * 6 + [ANY_SPEC],
        out_specs=[SEM_SPEC, SEM_SPEC] + [HBM_SPEC] * n,
        input_output_aliases={n + i: 2 + i for i in range(n)},
        compiler_params=pltpu.CompilerParams(has_side_effects=EFFECT),
    )(*shards, *lands, *sems, after)
    return outs[0], outs[1], list(outs[2:])


def gather_split_done(fsend, frecv, lands, after):
    n = len(lands)

    def body(*refs):
        land = refs[:n]
        ssem, rsem = refs[n], refs[n + 1]
        x, y, c = _place()
        for w in range(n):
            rh = lands[w].shape[1] // 2
            for k, (cx, cy) in enumerate(_other_chips(x, y)):
                sent = _half_rows(land[w], _chip_id(cx, cy), c, rh)
                got = _half_rows(land[w], _chip_id(cx, cy), 1 - c, rh)
                cp = pltpu.make_async_remote_copy(src_ref=sent, dst_ref=got, send_sem=ssem.at[3 * w + k], recv_sem=rsem.at[3 * w + k],
                                                  device_id=(x, y, 1 - c), device_id_type=MESH)
                cp.wait_send()
                cp.wait_recv()

    outs = pl.pallas_call(
        body,
        name="gather_split_done",
        out_shape=_hbm_shapes(lands),
        in_specs=[HBM_SPEC] * n + [SEM_SPEC, SEM_SPEC, ANY_SPEC],
        out_specs=[HBM_SPEC] * n,
        input_output_aliases={i: i for i in range(n)},
        compiler_params=pltpu.CompilerParams(has_side_effects=EFFECT),
    )(*lands, fsend, frecv, after)
    return list(outs)


def _flips():
    return [(fx, fy, fc) for fx in (0, 1) for fy in (0, 1) for fc in (0, 1)][1:]


def _flip(v, f):
    return v if f == 0 else 1 - v


def ada_mod(c3, w_ada, b_cols, conv_w):
    CB = w_ada.shape[1]

    def body(c_ref, w_ref, b_ref, cw_ref, call_ref, mod_ref, cwall_ref, modall, send_sems, recv_sems):
        x, y, c = _place()
        me_dev = 4 * x + 2 * y + c
        me = _chip_id(x, y)
        call_ref[me_dev] = c_ref[0]
        cwall_ref[me] = cw_ref[...]
        sends = []
        for k, (cx, cy) in enumerate(_other_chips(x, y)):
            cp = pltpu.make_async_remote_copy(src_ref=cw_ref, dst_ref=cwall_ref.at[me], send_sem=send_sems.at[10 + k],
                                              recv_sem=recv_sems.at[10 + k], device_id=(cx, cy, c), device_id_type=MESH)
            cp.start()
            sends.append(cp)
        for k, (fx, fy, fc) in enumerate(_flips()):
            cp = pltpu.make_async_remote_copy(src_ref=c_ref.at[0], dst_ref=call_ref.at[me_dev], send_sem=send_sems.at[k],
                                              recv_sem=recv_sems.at[k],
                                              device_id=(_flip(x, fx), _flip(y, fy), _flip(c, fc)), device_id_type=MESH)
            cp.start()
            sends.append(cp)
        for k, (fx, fy, fc) in enumerate(_flips()):
            peer = 4 * _flip(x, fx) + 2 * _flip(y, fy) + _flip(c, fc)
            pltpu.make_async_remote_copy(src_ref=c_ref.at[0], dst_ref=call_ref.at[peer], send_sem=send_sems.at[k],
                                         recv_sem=recv_sems.at[k], device_id=(x, y, c), device_id_type=MESH).wait_recv()
        row = lax.broadcasted_iota(jnp.int32, (N_DEV, D_MODEL), 0)
        call = jnp.zeros((N_DEV, D_MODEL), F32)
        for dev in range(N_DEV):
            call = jnp.where(row == dev, call_ref[dev], call)
        act = call * jax.nn.sigmoid(call)
        modall[me] = jnp.dot(act, w_ref[...], preferred_element_type=F32, precision=lax.Precision.HIGHEST) + b_ref[...]
        for k, (cx, cy) in enumerate(_other_chips(x, y)):
            cp = pltpu.make_async_remote_copy(src_ref=modall.at[me], dst_ref=modall.at[me], send_sem=send_sems.at[7 + k],
                                              recv_sem=recv_sems.at[7 + k], device_id=(cx, cy, c), device_id_type=MESH)
            cp.start()
            sends.append(cp)
        for k, (cx, cy) in enumerate(_other_chips(x, y)):
            blk = modall.at[_chip_id(cx, cy)]
            pltpu.make_async_remote_copy(src_ref=blk, dst_ref=blk, send_sem=send_sems.at[7 + k], recv_sem=recv_sems.at[7 + k],
                                         device_id=(x, y, c), device_id_type=MESH).wait_recv()
        for k, (cx, cy) in enumerate(_other_chips(x, y)):
            blk = cwall_ref.at[_chip_id(cx, cy)]
            pltpu.make_async_remote_copy(src_ref=blk, dst_ref=blk, send_sem=send_sems.at[10 + k], recv_sem=recv_sems.at[10 + k],
                                         device_id=(x, y, c), device_id_type=MESH).wait_recv()
        for cp in sends:
            cp.wait_send()
        mine = [modall[j, pl.ds(me_dev, 1), :] for j in range(N_CHIPS)]
        for r in range(6):
            pieces = []
            for h in range(2):
                pos = r * D_MODEL + h * 512
                pieces.append(mine[pos // CB][:, pos % CB:pos % CB + 512])
            mod_ref[r:r + 1, :] = jnp.concatenate(pieces, axis=1)

    vm = pl.BlockSpec(memory_space=pltpu.VMEM)
    return pl.pallas_call(
        body,
        name="ada_mod",
        in_specs=[vm] * 4,
        out_specs=[vm] * 3,
        out_shape=[jax.ShapeDtypeStruct((N_DEV, 1, D_MODEL), F32), jax.ShapeDtypeStruct((6, D_MODEL), F32),
                   jax.ShapeDtypeStruct((N_CHIPS,) + conv_w.shape, F32)],
        scratch_shapes=[pltpu.VMEM((N_CHIPS, N_DEV, CB), F32), pltpu.SemaphoreType.DMA((13,)), pltpu.SemaphoreType.DMA((13,))],
        compiler_params=pltpu.CompilerParams(has_side_effects=True, vmem_limit_bytes=VMEM_LIMIT),
    )(c3, w_ada, b_cols, conv_w)


def gather_small(blocks):
    n = len(blocks)

    def body(*refs):
        ins, outs = refs[:n], refs[n:2 * n]
        send_sems, recv_sems = refs[2 * n:]
        x, y, c = _place()
        sib = (x, y, 1 - c)
        chips = _other_chips(x, y)

        def dev(px, py, pc):
            return 4 * px + 2 * py + pc

        def cp(w, k, src, block_dev, to):
            return pltpu.make_async_remote_copy(src_ref=src, dst_ref=outs[w].at[block_dev], send_sem=send_sems.at[7 * w + k],
                                                recv_sem=recv_sems.at[7 * w + k], device_id=to, device_id_type=MESH)

        me = dev(x, y, c)
        started = []
        for w in range(n):
            outs[w][me] = ins[w][...]
            t = cp(w, 0, ins[w], me, sib)
            t.start()
            started.append(t)
            for k, (cx, cy) in enumerate(chips):
                t = cp(w, 1 + k, ins[w], me, (cx, cy, c))
                t.start()
                started.append(t)
        for w in range(n):
            for k, (cx, cy) in enumerate(chips):
                b = dev(cx, cy, c)
                cp(w, 1 + k, outs[w].at[b], b, (x, y, c)).wait_recv()
                t = cp(w, 4 + k, outs[w].at[b], b, sib)
                t.start()
                started.append(t)
        for w in range(n):
            b = dev(x, y, 1 - c)
            cp(w, 0, outs[w].at[b], b, (x, y, c)).wait_recv()
            for k, (cx, cy) in enumerate(chips):
                b = dev(cx, cy, 1 - c)
                cp(w, 4 + k, outs[w].at[b], b, (x, y, c)).wait_recv()
        for t in started:
            t.wait_send()

    vm = pl.BlockSpec(memory_space=pltpu.VMEM)
    return pl.pallas_call(
        body,
        name="gather_small",
        in_specs=[vm] * n,
        out_specs=[vm] * n,
        out_shape=[jax.ShapeDtypeStruct((N_DEV,) + b.shape, b.dtype) for b in blocks],
        scratch_shapes=[pltpu.SemaphoreType.DMA((7 * n,)), pltpu.SemaphoreType.DMA((7 * n,))],
        compiler_params=pltpu.CompilerParams(has_side_effects=True, vmem_limit_bytes=VMEM_LIMIT),
    )(*blocks)


def reduce_scatter_grads(grads, chunk_rows):
    n = len(grads)
    shapes = [g.shape[1:] for g in grads]
    halves = [s[0] // 2 for s in shapes]

    def body(*refs):
        gin = refs[:n]
        gout = refs[n:2 * n]
        sibbuf = refs[2 * n:3 * n]
        rest = refs[3 * n:]
        rbuf = rest[:n]
        pown = rest[n:2 * n]
        stage_a, stage_b, stage_o, stage_f = rest[2 * n:2 * n + 4]
        sib_send, sib_recv, ici_send, ici_recv, fin_send, fin_recv, ld_sems, st_sems = rest[2 * n + 4:]
        x, y, c = _place()
        me = _chip_id(x, y)
        chips = _other_chips(x, y)
        sib = (x, y, 1 - c)

        to_sib = []
        for w in range(n):
            rh = halves[w]
            cp = pltpu.make_async_remote_copy(src_ref=gin[w].at[:, pl.ds((1 - c) * rh, rh), :], dst_ref=sibbuf[w],
                                              send_sem=sib_send.at[w], recv_sem=sib_recv.at[w], device_id=sib,
                                              device_id_type=MESH)
            cp.start()
            to_sib.append(cp)

        sent = []
        for w in range(n):
            rh, cw = halves[w], shapes[w][1]
            ch = chunk_rows[w]
            to_sib[w].wait_recv()
            for k in range(4):
                chip = me if k == 3 else _chip_id(*chips[k])
                for r0 in range(0, rh, ch):
                    la = pltpu.make_async_copy(gin[w].at[chip, pl.ds(c * rh + r0, ch), :], stage_a.at[0:ch, 0:cw], ld_sems.at[0])
                    lb = pltpu.make_async_copy(sibbuf[w].at[chip, pl.ds(r0, ch), :], stage_b.at[0:ch, 0:cw], ld_sems.at[1])
                    la.start()
                    lb.start()
                    la.wait()
                    lb.wait()
                    tot = stage_a[0:ch, 0:cw] + stage_b[0:ch, 0:cw]
                    if k == 3:
                        pown[w][r0:r0 + ch, :] = tot
                    else:
                        stage_o[0:ch, 0:cw] = tot.astype(BF16)
                        cx, cy = chips[k]
                        cp = pltpu.make_async_remote_copy(src_ref=stage_o.at[0:ch, 0:cw], dst_ref=rbuf[w].at[k, r0:r0 + ch, :],
                                                          send_sem=ici_send.at[3 * w + k], recv_sem=ici_recv.at[3 * w + k],
                                                          device_id=(cx, cy, c), device_id_type=MESH)
                        cp.start()
                        cp.wait_send()
            sent.append(w)

        fin = []
        for w in range(n):
            rh, cw = halves[w], shapes[w][1]
            for k in range(3):
                whole = rbuf[w].at[k]
                pltpu.make_async_remote_copy(src_ref=whole, dst_ref=whole, send_sem=ici_send.at[3 * w + k],
                                             recv_sem=ici_recv.at[3 * w + k], device_id=(x, y, c),
                                             device_id_type=MESH).wait_recv()
            pown[w][...] = ((pown[w][...] + rbuf[w][0].astype(F32)) + rbuf[w][1].astype(F32)) + rbuf[w][2].astype(F32)
            mine = gout[w].at[pl.ds(c * rh, rh), :]
            st = pltpu.make_async_copy(pown[w], mine, st_sems.at[w])
            st.start()
            cp = pltpu.make_async_remote_copy(src_ref=pown[w], dst_ref=mine, send_sem=fin_send.at[w], recv_sem=fin_recv.at[w],
                                              device_id=sib, device_id_type=MESH)
            cp.start()
            fin.append((st, cp))
        for w in range(n):
            rh = halves[w]
            theirs = gout[w].at[pl.ds((1 - c) * rh, rh), :]
            pltpu.make_async_remote_copy(src_ref=theirs, dst_ref=theirs, send_sem=fin_send.at[w], recv_sem=fin_recv.at[w],
                                         device_id=(x, y, c), device_id_type=MESH).wait_recv()
        for cp in to_sib:
            cp.wait_send()
        for st, cp in fin:
            st.wait()
            cp.wait_send()

    hbm = pl.BlockSpec(memory_space=pl.ANY)
    max_ch = max(chunk_rows)
    max_c = max(s[1] for s in shapes)
    outs = pl.pallas_call(
        body,
        name="reduce_scatter_grads",
        in_specs=[hbm] * n,
        out_specs=[hbm] * (2 * n),
        out_shape=[jax.ShapeDtypeStruct(s, F32) for s in shapes]
        + [jax.ShapeDtypeStruct((N_CHIPS, h, s[1]), F32) for h, s in zip(halves, shapes)],
        scratch_shapes=[pltpu.VMEM((3, h, s[1]), BF16) for h, s in zip(halves, shapes)]
        + [pltpu.VMEM((h, s[1]), F32) for h, s in zip(halves, shapes)]
        + [pltpu.VMEM((max_ch, max_c), F32), pltpu.VMEM((max_ch, max_c), F32), pltpu.VMEM((max_ch, max_c), BF16),
           pltpu.VMEM((8, 128), F32)]
        + [pltpu.SemaphoreType.DMA((n,)), pltpu.SemaphoreType.DMA((n,)), pltpu.SemaphoreType.DMA((3 * n,)),
           pltpu.SemaphoreType.DMA((3 * n,)), pltpu.SemaphoreType.DMA((n,)), pltpu.SemaphoreType.DMA((n,)),
           pltpu.SemaphoreType.DMA((2,)), pltpu.SemaphoreType.DMA((n,))],
        compiler_params=pltpu.CompilerParams(has_side_effects=True, vmem_limit_bytes=VMEM_LIMIT),
    )(*grads)
    return outs[:n]


def split_start(name, bufs, plan, n_sem, carry):
    nb = len(bufs)

    def body(*refs):
        x, y, c = _place()
        ssem, rsem = refs[nb + 1], refs[nb + 2]
        for i, (src, dst, dev) in enumerate(plan(refs[:nb], x, y, c)):
            pltpu.make_async_remote_copy(src_ref=src, dst_ref=dst, send_sem=ssem.at[i], recv_sem=rsem.at[i], device_id=dev,
                                         device_id_type=MESH).start()

    alls = list(bufs) + [carry]
    outs = pl.pallas_call(
        body,
        name=name,
        out_shape=[pltpu.SemaphoreType.DMA((n_sem,)), pltpu.SemaphoreType.DMA((n_sem,))] + _hbm_shapes(alls),
        in_specs=[HBM_SPEC] * (nb + 1),
        out_specs=[SEM_SPEC, SEM_SPEC] + [HBM_SPEC] * (nb + 1),
        input_output_aliases={i: 2 + i for i in range(nb + 1)},
        compiler_params=pltpu.CompilerParams(has_side_effects=EFFECT),
    )(*[_hbm(t) for t in alls])
    return outs[0], outs[1], list(outs[2:2 + nb]), outs[-1]


def split_wait(name, ssem, rsem, bufs, plan, after):
    nb = len(bufs)

    def body(*refs):
        x, y, c = _place()
        s_ref, r_ref = refs[nb], refs[nb + 1]
        for i, (src, dst, dev) in enumerate(plan(refs[:nb], x, y, c)):
            cp = pltpu.make_async_remote_copy(src_ref=src, dst_ref=dst, send_sem=s_ref.at[i], recv_sem=r_ref.at[i], device_id=dev,
                                              device_id_type=MESH)
            cp.wait_send()
            cp.wait_recv()

    outs = pl.pallas_call(
        body,
        name=name,
        out_shape=_hbm_shapes(bufs),
        in_specs=[HBM_SPEC] * nb + [SEM_SPEC, SEM_SPEC, ANY_SPEC],
        out_specs=[HBM_SPEC] * nb,
        input_output_aliases={i: i for i in range(nb)},
        compiler_params=pltpu.CompilerParams(has_side_effects=EFFECT),
    )(*bufs, ssem, rsem, after)
    return list(outs)


def _gather_ici_plan(n):
    def plan(refs, x, y, c):
        out = []
        for w in range(n):
            rh = refs[w].shape[0] // 2
            for cx, cy in _other_chips(x, y):
                out.append((refs[w].at[pl.ds(c * rh, rh), :], _half_rows(refs[n + w], _chip_id(x, y), c, rh), (cx, cy, c)))
        return out

    return plan


def _gather_d2d_plan(n):
    def plan(refs, x, y, c):
        out = []
        for w in range(n):
            rh = refs[w].shape[1] // 2
            for cx, cy in _other_chips(x, y):
                blk = _half_rows(refs[w], _chip_id(cx, cy), c, rh)
                out.append((blk, blk, (x, y, 1 - c)))
        return out

    return plan


def _rs_d2d_plan(n):
    def plan(refs, x, y, c):
        out = []
        for w in range(n):
            rh = refs[w].shape[1] // 2
            out.append((refs[w].at[:, pl.ds((1 - c) * rh, rh), :], refs[n + w], (x, y, 1 - c)))
        return out

    return plan


def _rs_ici_plan(n):
    def plan(refs, x, y, c):
        out = []
        for w in range(n):
            for k, (cx, cy) in enumerate(_other_chips(x, y)):
                out.append((refs[w].at[_chip_id(cx, cy)], refs[n + w].at[k], (cx, cy, c)))
        return out

    return plan


def _rs_share_plan(n):
    def plan(refs, x, y, c):
        out = []
        for w in range(n):
            rh = refs[w].shape[0] // 2
            rows = refs[w].at[pl.ds(c * rh, rh), :]
            out.append((rows, rows, (x, y, 1 - c)))
        return out

    return plan


def rs_add(grad, sibbuf, place, tr, name):
    _, R, C = grad.shape
    nt = (R // 2) // tr

    def body(p_ref, g_ref, s_ref, o_ref):
        o_ref[...] = (g_ref[...] + s_ref[...]).astype(BF16)

    return pl.pallas_call(
        body,
        name=name,
        grid_spec=pltpu.PrefetchScalarGridSpec(
            num_scalar_prefetch=1,
            grid=(N_CHIPS, nt),
            in_specs=[pl.BlockSpec((None, tr, C), lambda j, i, p: (j, p[0] * nt + i, 0)),
                      pl.BlockSpec((None, tr, C), lambda j, i, p: (j, i, 0))],
            out_specs=pl.BlockSpec((None, tr, C), lambda j, i, p: (j, i, 0)),
        ),
        out_shape=jax.ShapeDtypeStruct((N_CHIPS, R // 2, C), BF16),
        compiler_params=_params(2),
    )(place, grad, sibbuf)


def rs_final(grad, sibbuf, rbuf, place, tr, name):
    _, R, C = grad.shape
    nt = (R // 2) // tr

    def body(p_ref, g_ref, s_ref, r_ref, o_ref):
        o_ref[...] = (((g_ref[...] + s_ref[...]) + r_ref[0].astype(F32)) + r_ref[1].astype(F32)) + r_ref[2].astype(F32)

    return pl.pallas_call(
        body,
        name=name,
        grid_spec=pltpu.PrefetchScalarGridSpec(
            num_scalar_prefetch=1,
            grid=(nt,),
            in_specs=[pl.BlockSpec((None, tr, C), lambda i, p: (p[1], p[0] * nt + i, 0)),
                      pl.BlockSpec((None, tr, C), lambda i, p: (p[1], i, 0)),
                      pl.BlockSpec((3, tr, C), lambda i, p: (0, i, 0))],
            out_specs=pl.BlockSpec((tr, C), lambda i, p: (p[0] * nt + i, 0)),
        ),
        out_shape=jax.ShapeDtypeStruct((R, C), F32),
        compiler_params=_params(1),
    )(place, grad, sibbuf, rbuf)


class GradReduce:
    def __init__(self, tag, grads, rows, place):
        self.tag, self.grads, self.rows, self.place = tag, grads, rows, place
        self.n = len(grads)

    def d2d_start(self, carry):
        sib = [lax.empty((N_CHIPS, g.shape[1] // 2, g.shape[2]), F32) for g in self.grads]
        self.s1, self.r1, bufs, carry = split_start(f"rs_{self.tag}_d2d_start", self.grads + sib, _rs_d2d_plan(self.n), self.n, carry)
        self.bufs1 = bufs
        return carry

    def add_and_ici_start(self, after, carry):
        bufs = split_wait(f"rs_{self.tag}_d2d_wait", self.s1, self.r1, self.bufs1, _rs_d2d_plan(self.n), after)
        self.grads, self.sib = bufs[:self.n], bufs[self.n:]
        pb = [rs_add(g, s, self.place, tr, f"rs_{self.tag}_add{w}")
              for w, (g, s, tr) in enumerate(zip(self.grads, self.sib, self.rows))]
        rb = [lax.empty((3,) + p.shape[1:], BF16) for p in pb]
        self.s2, self.r2, self.bufs2, carry = split_start(f"rs_{self.tag}_ici_start", pb + rb, _rs_ici_plan(self.n), 3 * self.n, carry)
        return carry

    def final_and_share_start(self, after, carry):
        bufs = split_wait(f"rs_{self.tag}_ici_wait", self.s2, self.r2, self.bufs2, _rs_ici_plan(self.n), after)
        rb = bufs[self.n:]
        full = [rs_final(g, s, r, self.place, tr, f"rs_{self.tag}_final{w}")
                for w, (g, s, r, tr) in enumerate(zip(self.grads, self.sib, rb, self.rows))]
        self.s3, self.r3, self.bufs3, carry = split_start(f"rs_{self.tag}_share_start", full, _rs_share_plan(self.n), self.n, carry)
        return carry

    def finish(self, after):
        return split_wait(f"rs_{self.tag}_share_wait", self.s3, self.r3, self.bufs3, _rs_share_plan(self.n), after)


def _rope_tables(positions):
    inv_freq = ROPE_THETA ** (-jnp.arange(0, ROT_DIM, 2, dtype=F32) / ROT_DIM)
    ang = positions.astype(F32)[:, None] * inv_freq
    cos, sin = jnp.cos(ang), jnp.sin(ang)
    S = positions.shape[0]
    one, zero = jnp.ones((S, 48), F32), jnp.zeros((S, 48), F32)
    z8 = jnp.zeros((S, 8), F32)
    tc = jnp.concatenate([cos, cos, one], axis=1)
    tsa = jnp.concatenate([z8, sin, zero], axis=1)
    tsb = jnp.concatenate([-sin, z8, zero], axis=1)
    return tuple(jnp.tile(t, (1, 2)) for t in (tc, tsa, tsb))


def _block_diag(w_pool):
    wbd = jnp.zeros((POOL_W, POOL_W), F32)
    for gi in range(4):
        wbd = wbd.at[gi * 64:(gi + 1) * 64, gi * 64:(gi + 1) * 64].set(w_pool[gi])
    return wbd


def kernel(x, c, positions, w_ada, b_ada, g_pre_mix, g_post_mix, g_pre_ffn, g_post_ffn, w_in, w_pool, b_pool, pool_scale, w_out, w_up, conv_w, conv_b, w_down, loss_target, m_w_ada, m_b_ada, m_g_pre_mix, m_g_post_mix, m_g_pre_ffn, m_g_post_ffn, m_w_in, m_w_pool, m_b_pool, m_pool_scale, m_w_out, m_w_up, m_conv_w, m_conv_b, m_w_down, v_w_ada, v_b_ada, v_g_pre_mix, v_g_post_mix, v_g_pre_ffn, v_g_post_ffn, v_w_in, v_w_pool, v_b_pool, v_pool_scale, v_w_out, v_w_up, v_conv_w, v_conv_b, v_w_down):
    xi, yi, ci = lax.axis_index("x"), lax.axis_index("y"), lax.axis_index("c")
    chip = 2 * xi + yi
    place = jnp.stack([ci, chip]).astype(jnp.int32)
    x2, tgt = x[0], loss_target[0]
    S = x2.shape[0]

    cb_ada = w_ada.shape[2]
    b_cols = lax.dynamic_slice(b_ada, (0, chip * cb_ada), (1, cb_ada))
    c_all, mod6, conv_w_g = ada_mod(c.reshape(1, 1, D_MODEL), w_ada[0], b_cols, conv_w[0])
    conv_w_f = jnp.transpose(conv_w_g, (1, 0, 2)).reshape(3, D_FF)

    def landing(s_):
        return lax.dynamic_update_slice(lax.empty((N_CHIPS,) + s_.shape, s_.dtype), s_[None], (chip, 0, 0))

    mix_sh = [w_in[0].astype(BF16), w_out[0].astype(BF16)]
    ffn_sh = [w_up[0].astype(BF16), w_down[0].astype(BF16)]
    ga_s, ga_r, ga_bufs, mod6 = split_start("gather_mix_ici_start", mix_sh + [landing(t) for t in mix_sh], _gather_ici_plan(2), 6, mod6)
    gb_s, gb_r, gb_bufs, mod6 = split_start("gather_ffn_ici_start", ffn_sh + [landing(t) for t in ffn_sh], _gather_ici_plan(2), 6, mod6)
    tc, tsa, tsb = _rope_tables(positions[0])
    wbd = _block_diag(w_pool[0]).astype(BF16)
    b_pool2, scale2 = b_pool.reshape(1, POOL_W), pool_scale
    ga_bufs = split_wait("gather_mix_ici_wait", ga_s, ga_r, ga_bufs, _gather_ici_plan(2), tc)
    gc_s, gc_r, mix_land, mod6 = split_start("gather_mix_d2d_start", ga_bufs[2:], _gather_d2d_plan(2), 6, mod6)
    w_in_g, w_out_g = split_wait("gather_mix_d2d_wait", gc_s, gc_r, mix_land, _gather_d2d_plan(2), mod6)

    h1, u, qkv9 = inproj_fwd(x2, g_pre_mix, mod6, w_in_g, tc, tsa, tsb)
    o_l = [attn_fwd(qkv9, gi, d) for gi, d in enumerate(DILATIONS)]
    mixed, pool = pool_fwd(u, wbd, b_pool2, scale2)
    gb_bufs = split_wait("gather_ffn_ici_wait", gb_s, gb_r, gb_bufs, _gather_ici_plan(2), pool)
    gd_s, gd_r, ffn_land, pool = split_start("gather_ffn_d2d_start", gb_bufs[2:], _gather_d2d_plan(2), 6, pool)
    cat, lse, y1, x1, h2 = outproj_fwd([o for o, _ in o_l] + [l for _, l in o_l], pool, x2, w_out_g, g_post_mix, g_pre_ffn, mod6)
    w_up_g, w_down_g = split_wait("gather_ffn_d2d_wait", gd_s, gd_r, ffn_land, _gather_d2d_plan(2), h2)
    w_down_f = w_down_g.reshape(D_FF, D_MODEL)
    gate, val = up_fwd(h2, w_up_g)
    a, dy2, dout, loss_v, d_gt_f, d_g_post_ffn = down_fwd(gate, val, conv_w_f, conv_b, w_down_f, x1, tgt, g_post_ffn, mod6)

    dgc, dval, d_conv_w, d_conv_b = down_bwd(dy2, w_down_f, gate, val, conv_w_f, conv_b)
    dw_down = dw_matmul(a, dy2, 2, (HALF_FF, D_MODEL), HALF_FF, D_MODEL, True, "dw_down")
    dgate, dx1, dy1, d_sh_f, d_sc_f, d_g_pre_ffn, d_gt_m, d_g_post_mix = up_bwd(
        dgc, dval, conv_w_f, w_up_g, x1, dout, y1, g_pre_ffn, g_post_mix, mod6)
    dw_up = dw_matmul(h2, dgate, 4, (D_MODEL, HALF_FF), D_MODEL, HALF_FF, False, "dw_up_gate", n_blk=2)
    dw_up = dw_matmul(h2, dval, 4, (D_MODEL, HALF_FF), D_MODEL, HALF_FF, False, "dw_up_val", prev=dw_up, blk_off=2, n_blk=2)
    rs_ffn = GradReduce("ffn", [dw_up, dw_down.reshape(N_CHIPS, D_FF // N_CHIPS, D_MODEL)], [256, 176], place)
    dy1 = rs_ffn.d2d_start(dy1)
    dpool, dattn, delta = outproj_bwd(dy1, w_out_g, cat)
    dw_out = dw_matmul(cat, dy1, 4, (512, 256), 512, 256, False, "dw_out")
    dpool = rs_ffn.add_and_ici_start(dw_out, dpool)
    du, d_wbd, d_b_pool, d_scale = pool_bwd(dpool, mixed, wbd, b_pool2, scale2)
    dqkv = [attn_bwd(qkv9, dattn, lse, delta, gi, d) for gi, d in enumerate(DILATIONS)]
    dproj, grad_x, d_sh_m, d_sc_m, d_g_pre_mix = inproj_bwd(dqkv, du, x2, dx1, w_in_g, g_pre_mix, mod6, tc, tsa, tsb)

    z1 = jnp.zeros((1, D_MODEL), F32)
    slab_a = jnp.concatenate(
        [d_sh_m, d_sc_m, d_gt_m, d_sh_f, d_sc_f, d_gt_f, d_g_pre_mix, d_g_post_mix, d_g_pre_ffn, d_g_post_ffn,
         jnp.concatenate([d_b_pool, d_scale, jnp.zeros((1, 512), F32)], axis=1)] + [z1] * 5, axis=0)
    slab_b = jnp.concatenate([d_conv_w, d_conv_b, jnp.zeros((4, D_FF), F32)], axis=0)
    d_wpool = jnp.concatenate([d_wbd[gi * 64:(gi + 1) * 64, gi * 64:(gi + 1) * 64] for gi in range(4)], axis=0)
    slab_a_g, slab_b_g, wpool_g = gather_small([slab_a, slab_b, d_wpool])
    cw_cols = conv_w.shape[2]
    convw_g = lax.dynamic_slice(slab_b_g, (0, 0, chip * cw_cols), (N_DEV, 3, cw_cols))
    dw_in = dw_matmul(h1, dproj, 4, (D_MODEL, 640), D_MODEL, 640, False, "dw_in")
    rs_mix = GradReduce("mix", [dw_in, dw_out], [256, 256], place)
    slab_a_g = rs_mix.d2d_start(slab_a_g)
    slab_a_g = rs_ffn.final_and_share_start(slab_a_g, slab_a_g)
    slab_a_g = rs_mix.add_and_ici_start(slab_a_g, slab_a_g)
    dmod_cols = lax.dynamic_slice(slab_a_g[:, :6, :].reshape(N_DEV, 6 * D_MODEL), (0, chip * cb_ada), (N_DEV, cb_ada))

    res = {}

    def big_adamw(name, w, g, m, v, tr):
        d_, m_, v_ = adamw_rows(w[0], g, m[0], v[0], tr, "adamw_" + name)
        res[name] = (g[None], d_[None], m_[None], v_[None])
        return v_

    g_ada, d_ada, m_ada, v_ada = adamw_ada(c_all.reshape(N_DEV, D_MODEL).T, dmod_cols, w_ada[0], m_w_ada[0], v_w_ada[0])
    res["w_ada"] = (g_ada[None], d_ada[None], m_ada[None], v_ada[None])
    g_w_up, g_w_down = rs_ffn.finish(v_ada)
    big_adamw("w_up", w_up, g_w_up, m_w_up, v_w_up, 256)
    last = big_adamw("w_down", w_down, g_w_down, m_w_down, v_w_down, 352)
    rs_mix.final_and_share_start(last, jnp.zeros((8, 128), F32))
    g_w_in, g_w_out = rs_mix.finish(last)
    big_adamw("w_in", w_in, g_w_in, m_w_in, v_w_in, 256)
    big_adamw("w_out", w_out, g_w_out, m_w_out, v_w_out, 256)
    flat = lambda t: t.reshape(1, POOL_W)
    wp = lambda t: t.reshape(POOL_W, 64)
    small = adamw_small(slab_a_g, slab_b_g, convw_g, wpool_g, {
        "b_ada": (b_ada, m_b_ada, v_b_ada), "g_pre_mix": (g_pre_mix, m_g_pre_mix, v_g_pre_mix),
        "g_post_mix": (g_post_mix, m_g_post_mix, v_g_post_mix), "g_pre_ffn": (g_pre_ffn, m_g_pre_ffn, v_g_pre_ffn),
        "g_post_ffn": (g_post_ffn, m_g_post_ffn, v_g_post_ffn), "b_pool": (flat(b_pool), flat(m_b_pool), flat(v_b_pool)),
        "pool_scale": (pool_scale, m_pool_scale, v_pool_scale), "conv_b": (conv_b, m_conv_b, v_conv_b),
        "conv_w": (conv_w[0], m_conv_w[0], v_conv_w[0]), "w_pool": (wp(w_pool), wp(m_w_pool), wp(v_w_pool))})
    for name in ("b_ada", "g_pre_mix", "g_post_mix", "g_pre_ffn", "g_post_ffn", "pool_scale", "conv_b"):
        res[name] = tuple(small[name])
    res["b_pool"] = tuple(t.reshape(1, 4, 64) for t in small["b_pool"])
    res["conv_w"] = tuple(t[None] for t in small["conv_w"])
    res["w_pool"] = tuple(t.reshape(1, 4, 64, 64) for t in small["w_pool"])

    loss = lax.psum(loss_v[0, 0], ("x", "y", "c"))
    order = ["w_ada", "b_ada", "g_pre_mix", "g_post_mix", "g_pre_ffn", "g_post_ffn", "w_in", "w_pool", "b_pool", "pool_scale",
             "w_out", "w_up", "conv_w", "conv_b", "w_down"]
    outs = [loss, grad_x[None]]
    for k in range(4):
        outs += [res[n][k] for n in order]
    return tuple(outs)
```

```python
import functools
import math

import jax
import jax.numpy as jnp
from jax import lax
from jax.experimental import pallas as pl
from jax.experimental.pallas import tpu as pltpu

F32 = jnp.float32
BF16 = jnp.bfloat16
MESH = pl.DeviceIdType.MESH

D_MODEL = 1024
HEAD_DIM = 64
POOL_W = 256
GROUP_W = 256
DILATIONS = (1, 4, 16)
ATT_BLOCK = 128
IN_W = 2560
D_FF = 2816
HALF_FF = 1408
ROT_DIM = 16
ROPE_THETA = 500000.0
NORM_EPS = 1e-6
N_CHIPS = 4
N_DEV = 8
NEG = -1e30

ADAM_LR = 0.001
ADAM_B1 = 0.9
ADAM_B2 = 0.999
ADAM_EPS = 1e-08
ADAM_WD = 0.01
ADAM_STEP = 10

VMEM_LIMIT = 56 * 1024 * 1024

NT = (((1,), (1,)), ((), ()))
TN = (((0,), (0,)), ((), ()))


def _params(n_grid=0, **kw):
    sem = ("arbitrary",) * n_grid if n_grid else None
    return pltpu.CompilerParams(dimension_semantics=sem, vmem_limit_bytes=VMEM_LIMIT, **kw)


def _full(shape):
    nd = len(shape)
    return pl.BlockSpec(tuple(shape), lambda *_: (0,) * nd)


def _rows(tm, ncol):
    return pl.BlockSpec((tm, ncol), lambda i: (i, 0))


def _acc(ref, val):
    @pl.when(pl.program_id(0) == 0)
    def _():
        ref[...] = jnp.zeros_like(ref)

    ref[...] += val


def _colsum(v):
    return jnp.sum(v, axis=0, keepdims=True)


def _rope128(t, cs, sa, sb, sign):
    return t * cs + sign * (pltpu.roll(t, 8, 1) * sa + pltpu.roll(t, 120, 1) * sb)


def _gelu(z):
    c = math.sqrt(2.0 / math.pi)
    t = jnp.tanh(c * (z + 0.044715 * (z * z * z)))
    return 0.5 * z * (1.0 + t), t


def _gelu_grad(z, t):
    c = math.sqrt(2.0 / math.pi)
    return 0.5 * (1.0 + t) + 0.5 * z * (1.0 - t * t) * (c * (1.0 + 3.0 * 0.044715 * (z * z)))


def _conv_taps(gate, halo, first):
    row = lax.broadcasted_iota(jnp.int32, gate.shape, 0)
    halo = jnp.where(first, 0.0, halo)
    p1 = halo[15:16, :]
    p2 = halo[14:15, :]
    g1 = jnp.where(row == 0, p1, pltpu.roll(gate, 1, 0))
    g2 = jnp.where(row == 0, p2, jnp.where(row == 1, p1, pltpu.roll(gate, 2, 0)))
    return g1, g2


def inproj_fwd(x, g, mod6, w_in_g, tc, tsa, tsb, tm=512):
    S = x.shape[0]

    def body(x_ref, g_ref, mod_ref, w_ref, tc_ref, tsa_ref, tsb_ref, h_ref, u_ref, q1_ref, q4_ref, q16_ref, scr):
        qkv_refs = (q1_ref, q4_ref, q16_ref)
        xv = x_ref[...]
        rstd = lax.rsqrt(jnp.mean(xv * xv, axis=-1, keepdims=True) + NORM_EPS)
        h = ((xv * rstd) * g_ref[...]) * (1.0 + mod_ref[1:2, :]) + mod_ref[0:1, :]
        hb = h.astype(BF16)
        h_ref[...] = hb
        cs, sa, sb = tc_ref[...], tsa_ref[...], tsb_ref[...]
        for j in range(N_CHIPS):
            res = jnp.dot(hb, w_ref[j], preferred_element_type=F32)
            for t in range(5):
                sp = 5 * j + t
                piece, half = sp // 2, sp % 2
                blk = res[:, t * 128:(t + 1) * 128]
                lanes = slice(half * 128, (half + 1) * 128)
                if piece == 0:
                    u_ref[:, lanes] = blk
                else:
                    kind, gi = (piece - 1) // 3, (piece - 1) % 3
                    if kind == 0:
                        blk = _rope128(blk, cs, sa, sb, 1.0) * (HEAD_DIM ** -0.5)
                    elif kind == 1:
                        blk = _rope128(blk, cs, sa, sb, 1.0)
                    d = DILATIONS[gi]
                    if d == 1:
                        q1_ref[kind, 0, :, lanes] = blk.astype(BF16)
                    else:
                        scr[...] = blk
                        for r in range(d):
                            qkv_refs[gi][kind, r, :, lanes] = scr[pl.ds(r, tm // d, stride=d), :].astype(BF16)

    cls = lambda d: pl.BlockSpec((3, d, tm // d, GROUP_W), lambda i: (0, 0, i, 0))
    return pl.pallas_call(
        body,
        name="inproj_fwd",
        grid=(S // tm,),
        in_specs=[_rows(tm, D_MODEL), _full((1, D_MODEL)), _full((6, D_MODEL)), _full(w_in_g.shape),
                  _rows(tm, 128), _rows(tm, 128), _rows(tm, 128)],
        out_specs=[_rows(tm, D_MODEL), _rows(tm, POOL_W)] + [cls(d) for d in DILATIONS],
        out_shape=[jax.ShapeDtypeStruct((S, D_MODEL), BF16), jax.ShapeDtypeStruct((S, POOL_W), F32)]
        + [jax.ShapeDtypeStruct((3, d, S // d, GROUP_W), BF16) for d in DILATIONS],
        scratch_shapes=[pltpu.VMEM((tm, 128), F32)],
        compiler_params=_params(1),
    )(x, g, mod6, w_in_g, tc, tsa, tsb)


def _attn_masks():
    row = lax.broadcasted_iota(jnp.int32, (ATT_BLOCK, 2 * ATT_BLOCK), 0)
    col = lax.broadcasted_iota(jnp.int32, (ATT_BLOCK, 2 * ATT_BLOCK), 1)
    band = (col >= row) & (col <= row + ATT_BLOCK)
    lane = lax.broadcasted_iota(jnp.int32, (ATT_BLOCK, 128), 1)
    return band, col, lane < HEAD_DIM


def attn_fwd(qkv, d):
    L = qkv.shape[2]
    nb = L // ATT_BLOCK

    def body(q_ref, k_ref, v_ref, o_ref, l_ref, kpad, vpad):
        kpad[0:ATT_BLOCK, :] = jnp.zeros((ATT_BLOCK, GROUP_W), BF16)
        vpad[0:ATT_BLOCK, :] = jnp.zeros((ATT_BLOCK, GROUP_W), BF16)
        kpad[ATT_BLOCK:, :] = k_ref[...]
        vpad[ATT_BLOCK:, :] = v_ref[...]
        band, col, lo = _attn_masks()

        def step(n, carry):
            r0 = pl.multiple_of(n * ATT_BLOCK, ATT_BLOCK)
            valid = band & ((col >= ATT_BLOCK) | (n > 0))
            qb = q_ref[pl.ds(r0, ATT_BLOCK), :]
            kb = kpad[pl.ds(r0, 2 * ATT_BLOCK), :]
            vb = vpad[pl.ds(r0, 2 * ATT_BLOCK), :]
            for pair in range(2):
                lanes = slice(pair * 128, (pair + 1) * 128)
                qp, kp, vp = qb[:, lanes], kb[:, lanes], vb[:, lanes]
                outs, lses = [], []
                for hh in range(2):
                    sel = lo if hh == 0 else jnp.logical_not(lo)
                    qm = jnp.where(sel, qp, jnp.zeros_like(qp))
                    s = lax.dot_general(qm, kp, NT, preferred_element_type=F32)
                    s = jnp.where(valid, s, NEG)
                    m = jnp.max(s, axis=1, keepdims=True)
                    p = jnp.exp(s - m)
                    den = jnp.sum(p, axis=1, keepdims=True)
                    pv = jnp.dot(p.astype(BF16), vp, preferred_element_type=F32)
                    outs.append(pv / den)
                    lses.append(m + jnp.log(den))
                o_ref[pl.ds(r0, ATT_BLOCK), lanes] = jnp.where(lo, outs[0], outs[1])
                l_ref[pl.ds(r0, ATT_BLOCK), lanes] = jnp.where(lo, lses[0], lses[1])
            return carry

        lax.fori_loop(0, nb, step, 0)

    spec = lambda kind: pl.BlockSpec((None, None, L, GROUP_W), lambda r: (kind, r, 0, 0))
    return pl.pallas_call(
        body,
        name=f"attn_fwd_d{d}",
        grid=(d,),
        in_specs=[spec(0), spec(1), spec(2)],
        out_specs=[pl.BlockSpec((None, L, GROUP_W), lambda r: (r, 0, 0))] * 2,
        out_shape=[jax.ShapeDtypeStruct((d, L, GROUP_W), F32)] * 2,
        scratch_shapes=[pltpu.VMEM((L + ATT_BLOCK, GROUP_W), BF16)] * 2,
        compiler_params=_params(1),
    )(qkv, qkv, qkv)


def _pool_lane_windows(shape):
    lane = lax.broadcasted_iota(jnp.int32, shape, 1)
    return lane, jnp.where(lane < 64, 2, jnp.where(lane < 128, 4, jnp.where(lane < 192, 8, 16)))


def pool_fwd(u, wbd, b, scale):
    S = u.shape[0]

    def body(u_ref, w_ref, b_ref, s_ref, mixed_ref, out_ref):
        uv = u_ref[...]
        row = lax.broadcasted_iota(jnp.int32, uv.shape, 0)
        lane, win = _pool_lane_windows(uv.shape)

        def shift(a, k):
            return jnp.where(row >= k, pltpu.roll(a, k, 0), 0.0)

        s2 = uv + shift(uv, 1)
        s4 = s2 + shift(s2, 2)
        s8 = s4 + shift(s4, 4)
        s16 = s8 + shift(s8, 8)
        tsum = jnp.where(lane < 64, s2, jnp.where(lane < 128, s4, jnp.where(lane < 192, s8, s16)))
        cnt = jnp.minimum(row + 1, win).astype(F32)
        mb = (tsum / cnt - uv).astype(BF16)
        mixed_ref[...] = mb
        y = jnp.dot(mb, w_ref[...], preferred_element_type=F32) + b_ref[...]
        out_ref[...] = (y * s_ref[...]).astype(BF16)

    vm = pl.BlockSpec(memory_space=pltpu.VMEM)
    return pl.pallas_call(
        body,
        name="pool_fwd",
        in_specs=[vm] * 4,
        out_specs=[vm] * 2,
        out_shape=[jax.ShapeDtypeStruct((S, POOL_W), BF16)] * 2,
        compiler_params=_params(),
    )(u, wbd, b, scale)


def outproj_fwd(o_l, pool, x, w_out_g, g_post, g_pre, mod6, tm=512):
    S = x.shape[0]

    def body(o0, o1, o2, l0, l1, l2, pool_ref, x_ref, w_ref, gpost_ref, gpre_ref, mod_ref,
             cat_ref, lse_ref, lse4_ref, lse16_ref, y1_ref, x1_ref, h2_ref, so4, sl4, so16, sl16):
        for d, src, dst in ((4, o1, so4), (4, l1, sl4), (16, o2, so16), (16, l2, sl16)):
            for r in range(d):
                for h in range(2):
                    dst[h, pl.ds(r, tm // d, stride=d), :] = src[r, :, h * 128:(h + 1) * 128]
        nat = lambda ref: jnp.concatenate([ref[0], ref[1]], axis=1)
        a, b, c = l0[0], nat(sl4), nat(sl16)
        m = jnp.maximum(jnp.maximum(a, b), c)
        e0, e1, e2 = jnp.exp(a - m), jnp.exp(b - m), jnp.exp(c - m)
        z = e0 + e1 + e2
        lse = m + jnp.log(z)
        lse_ref[...] = lse
        for h in range(2):
            sl4[h] = lse[:, h * 128:(h + 1) * 128]
        for d, dst in ((4, lse4_ref), (16, lse16_ref)):
            for r in range(d):
                for h in range(2):
                    dst[r, :, h * 128:(h + 1) * 128] = sl4[h, pl.ds(r, tm // d, stride=d), :]
        attn = (e0 * o0[0] + e1 * nat(so4) + e2 * nat(so16)) / z
        cat = jnp.concatenate([pool_ref[...], attn.astype(BF16)], axis=1)
        cat_ref[...] = cat
        y1 = jnp.concatenate([jnp.dot(cat, w_ref[j], preferred_element_type=F32) for j in range(N_CHIPS)], axis=1)
        y1_ref[...] = y1
        rstd = lax.rsqrt(jnp.mean(y1 * y1, axis=-1, keepdims=True) + NORM_EPS)
        x1 = x_ref[...] + mod_ref[2:3, :] * ((y1 * rstd) * gpost_ref[...])
        x1_ref[...] = x1
        rstd2 = lax.rsqrt(jnp.mean(x1 * x1, axis=-1, keepdims=True) + NORM_EPS)
        h2 = ((x1 * rstd2) * gpre_ref[...]) * (1.0 + mod_ref[4:5, :]) + mod_ref[3:4, :]
        h2_ref[...] = h2.astype(BF16)

    t256 = _rows(tm, GROUP_W)
    cls = lambda d: pl.BlockSpec((d, tm // d, GROUP_W), lambda i: (0, i, 0))
    cls_shape = lambda d: jax.ShapeDtypeStruct((d, S // d, GROUP_W), F32)
    return pl.pallas_call(
        body,
        name="outproj_fwd",
        grid=(S // tm,),
        in_specs=[cls(d) for d in DILATIONS] * 2 + [t256, _rows(tm, D_MODEL), _full(w_out_g.shape), _full((1, D_MODEL)),
                                                    _full((1, D_MODEL)), _full((6, D_MODEL))],
        out_specs=[_rows(tm, 512), t256, cls(4), cls(16), _rows(tm, D_MODEL), _rows(tm, D_MODEL), _rows(tm, D_MODEL)],
        out_shape=[jax.ShapeDtypeStruct((S, 512), BF16), jax.ShapeDtypeStruct((S, GROUP_W), F32), cls_shape(4), cls_shape(16),
                   jax.ShapeDtypeStruct((S, D_MODEL), F32), jax.ShapeDtypeStruct((S, D_MODEL), F32),
                   jax.ShapeDtypeStruct((S, D_MODEL), BF16)],
        scratch_shapes=[pltpu.VMEM((2, tm, 128), F32)] * 4,
        compiler_params=_params(1),
    )(*o_l, pool, x, w_out_g, g_post, g_pre, mod6)


def up_fwd(h2, w_up_g, tm=512):
    S = h2.shape[0]

    def body(h_ref, w_ref, gate_ref, val_ref):
        hb = h_ref[...]
        for j in range(N_CHIPS):
            res = jnp.dot(hb, w_ref[j], preferred_element_type=F32).astype(BF16)
            dst = gate_ref if j < 2 else val_ref
            dst[:, (j % 2) * HALF_FF:(j % 2 + 1) * HALF_FF] = res

    return pl.pallas_call(
        body,
        name="up_fwd",
        grid=(S // tm,),
        in_specs=[_rows(tm, D_MODEL), _full(w_up_g.shape)],
        out_specs=[_rows(tm, D_FF)] * 2,
        out_shape=[jax.ShapeDtypeStruct((S, D_FF), BF16)] * 2,
        compiler_params=_params(1),
    )(h2, w_up_g)


def _halo_prev(tm, ncol):
    return pl.BlockSpec((16, ncol), lambda i: (jnp.maximum(i * (tm // 16) - 1, 0), 0))


def down_fwd(gate, val, conv_w, conv_b, w_down, x1, target, g_post, mod6, tm=256):
    S = x1.shape[0]

    def body(gate_ref, halo_ref, val_ref, cw_ref, cb_ref, w_ref, x1_ref, tgt_ref, g_ref, mod_ref,
             a_ref, dy2_ref, dout_ref, loss_ref, dgt_ref, dg_ref):
        first = pl.program_id(0) == 0
        y2 = jnp.zeros((tm, D_MODEL), F32)
        for ch in range(2):
            cols = slice(ch * HALF_FF, (ch + 1) * HALF_FF)
            gt = gate_ref[:, cols].astype(F32)
            g1, g2 = _conv_taps(gt, halo_ref[:, cols].astype(F32), first)
            gc = g2 * cw_ref[0:1, cols] + g1 * cw_ref[1:2, cols] + gt * cw_ref[2:3, cols] + cb_ref[:, cols]
            ge, _ = _gelu(gc)
            ab = (ge * val_ref[:, cols].astype(F32)).astype(BF16)
            a_ref[:, cols] = ab
            y2 = y2 + jnp.dot(ab, w_ref[cols, :], preferred_element_type=F32)
        rstd = lax.rsqrt(jnp.mean(y2 * y2, axis=-1, keepdims=True) + NORM_EPS)
        y2n = y2 * rstd
        gv = g_ref[...]
        gtf = mod_ref[5:6, :]
        r2 = y2n * gv
        diff = (x1_ref[...] + gtf * r2) - tgt_ref[...]
        _acc(loss_ref, jnp.zeros((1, 128), F32) + 0.5 * jnp.sum(diff * diff) * (1.0 / D_MODEL))
        dout = diff * (1.0 / D_MODEL)
        dout_ref[...] = dout
        _acc(dgt_ref, _colsum(dout * r2))
        dr2 = dout * gtf
        _acc(dg_ref, _colsum(dr2 * y2n))
        dyn = dr2 * gv
        dy2 = rstd * (dyn - y2n * jnp.mean(dyn * y2n, axis=-1, keepdims=True))
        dy2_ref[...] = dy2.astype(BF16)

    vec = _full((1, D_MODEL))
    return pl.pallas_call(
        body,
        name="down_fwd",
        grid=(S // tm,),
        in_specs=[_rows(tm, D_FF), _halo_prev(tm, D_FF), _rows(tm, D_FF), _full((3, D_FF)), _full((1, D_FF)),
                  _full((D_FF, D_MODEL)), _rows(tm, D_MODEL), _rows(tm, D_MODEL), vec, _full((6, D_MODEL))],
        out_specs=[_rows(tm, D_FF), _rows(tm, D_MODEL), _rows(tm, D_MODEL), _full((1, 128)), vec, vec],
        out_shape=[jax.ShapeDtypeStruct((S, D_FF), BF16), jax.ShapeDtypeStruct((S, D_MODEL), BF16),
                   jax.ShapeDtypeStruct((S, D_MODEL), F32), jax.ShapeDtypeStruct((1, 128), F32),
                   jax.ShapeDtypeStruct((1, D_MODEL), F32), jax.ShapeDtypeStruct((1, D_MODEL), F32)],
        compiler_params=_params(1),
    )(gate, gate, val, conv_w, conv_b, w_down, x1, target, g_post, mod6)


def down_bwd(dy2, w_down, gate, val, conv_w, conv_b, tm=256):
    S = dy2.shape[0]

    def body(dy_ref, w_ref, gate_ref, halo_ref, val_ref, cw_ref, cb_ref, dgc_ref, dval_ref, dcw_ref, dcb_ref):
        first = pl.program_id(0) == 0

        @pl.when(first)
        def _():
            dcw_ref[...] = jnp.zeros_like(dcw_ref)
            dcb_ref[...] = jnp.zeros_like(dcb_ref)

        dyb = dy_ref[...]
        for ch in range(2):
            cols = slice(ch * HALF_FF, (ch + 1) * HALF_FF)
            da = lax.dot_general(dyb, w_ref[cols, :], NT, preferred_element_type=F32)
            gt = gate_ref[:, cols].astype(F32)
            g1, g2 = _conv_taps(gt, halo_ref[:, cols].astype(F32), first)
            gc = g2 * cw_ref[0:1, cols] + g1 * cw_ref[1:2, cols] + gt * cw_ref[2:3, cols] + cb_ref[:, cols]
            ge, th = _gelu(gc)
            dgc = da * val_ref[:, cols].astype(F32) * _gelu_grad(gc, th)
            dgc_ref[:, cols] = dgc.astype(BF16)
            dval_ref[:, cols] = (da * ge).astype(BF16)
            dcb_ref[:, cols] += _colsum(dgc)
            dcw_ref[0:1, cols] += _colsum(dgc * g2)
            dcw_ref[1:2, cols] += _colsum(dgc * g1)
            dcw_ref[2:3, cols] += _colsum(dgc * gt)

    return pl.pallas_call(
        body,
        name="down_bwd",
        grid=(S // tm,),
        in_specs=[_rows(tm, D_MODEL), _full((D_FF, D_MODEL)), _rows(tm, D_FF), _halo_prev(tm, D_FF), _rows(tm, D_FF),
                  _full((3, D_FF)), _full((1, D_FF))],
        out_specs=[_rows(tm, D_FF), _rows(tm, D_FF), _full((3, D_FF)), _full((1, D_FF))],
        out_shape=[jax.ShapeDtypeStruct((S, D_FF), BF16), jax.ShapeDtypeStruct((S, D_FF), BF16),
                   jax.ShapeDtypeStruct((3, D_FF), F32), jax.ShapeDtypeStruct((1, D_FF), F32)],
        compiler_params=_params(1),
    )(dy2, w_down, gate, gate, val, conv_w, conv_b)


def dw_matmul(a, b, out_blocks, blk_shape, a_cols, b_cols, a_blocked, name, prev=None, blk_off=0, n_blk=None, tm=512):
    S = a.shape[0]
    n_blk = out_blocks if n_blk is None else n_blk

    def body(*refs):
        a_ref, b_ref, o_ref = refs[0], refs[1], refs[-1]

        @pl.when(pl.program_id(1) == 0)
        def _():
            o_ref[...] = jnp.zeros_like(o_ref)

        o_ref[...] += lax.dot_general(a_ref[...], b_ref[...], TN, preferred_element_type=F32)

    a_spec = pl.BlockSpec((tm, a_cols), (lambda j, i: (i, j)) if a_blocked else (lambda j, i: (i, 0)))
    b_spec = pl.BlockSpec((tm, b_cols), (lambda j, i: (i, 0)) if a_blocked else (lambda j, i: (i, j)))
    in_specs = [a_spec, b_spec]
    args = [a, b]
    aliases = {}
    if prev is not None:
        in_specs.append(pl.BlockSpec(memory_space=pl.ANY))
        args.append(prev)
        aliases = {2: 0}
    return pl.pallas_call(
        body,
        name=name,
        grid=(n_blk, S // tm),
        in_specs=in_specs,
        out_specs=pl.BlockSpec((None,) + tuple(blk_shape), lambda j, i: (j + blk_off, 0, 0)),
        out_shape=jax.ShapeDtypeStruct((out_blocks,) + tuple(blk_shape), F32),
        input_output_aliases=aliases,
        compiler_params=_params(2),
    )(*args)


def up_bwd(dgc, dval, conv_w, w_up_g, x1, dout, y1, g_pre, g_post, mod6, tm=256):
    S = x1.shape[0]
    last_blk = S // 16 - 1

    def body(dgc_ref, nxt_ref, dval_ref, cw_ref, w_ref, x1_ref, dout_ref, y1_ref, gpre_ref, gpost_ref, mod_ref,
             dgate_ref, dx1_ref, dy1_ref, dsh_ref, dsc_ref, dgpre_ref, dgt_ref, dgpost_ref):
        last = pl.program_id(0) == pl.num_programs(0) - 1
        dh = jnp.zeros((tm, D_MODEL), F32)
        for ch in range(2):
            cols = slice(ch * HALF_FF, (ch + 1) * HALF_FF)
            dg = dgc_ref[:, cols].astype(F32)
            nx = jnp.where(last, 0.0, nxt_ref[:, cols].astype(F32))
            row = lax.broadcasted_iota(jnp.int32, dg.shape, 0)
            n0, n1 = nx[0:1, :], nx[1:2, :]
            u1 = jnp.where(row == tm - 1, n0, pltpu.roll(dg, tm - 1, 0))
            u2 = jnp.where(row == tm - 1, n1, jnp.where(row == tm - 2, n0, pltpu.roll(dg, tm - 2, 0)))
            dgate = (dg * cw_ref[2:3, cols] + u1 * cw_ref[1:2, cols] + u2 * cw_ref[0:1, cols]).astype(BF16)
            dgate_ref[:, cols] = dgate
            dh = dh + lax.dot_general(dgate, w_ref[ch], NT, preferred_element_type=F32)
            dh = dh + lax.dot_general(dval_ref[:, cols], w_ref[2 + ch], NT, preferred_element_type=F32)
        x1 = x1_ref[...]
        rstd = lax.rsqrt(jnp.mean(x1 * x1, axis=-1, keepdims=True) + NORM_EPS)
        n2 = x1 * rstd
        gpre = gpre_ref[...]
        one_sc = 1.0 + mod_ref[4:5, :]
        _acc(dsh_ref, _colsum(dh))
        _acc(dsc_ref, _colsum(dh * (n2 * gpre)))
        _acc(dgpre_ref, _colsum(dh * one_sc * n2))
        dn = dh * (gpre * one_sc)
        dx1 = dout_ref[...] + rstd * (dn - n2 * jnp.mean(dn * n2, axis=-1, keepdims=True))
        dx1_ref[...] = dx1
        y1 = y1_ref[...]
        rstd1 = lax.rsqrt(jnp.mean(y1 * y1, axis=-1, keepdims=True) + NORM_EPS)
        y1n = y1 * rstd1
        gpost = gpost_ref[...]
        gtm = mod_ref[2:3, :]
        _acc(dgt_ref, _colsum(dx1 * (y1n * gpost)))
        dr1 = dx1 * gtm
        _acc(dgpost_ref, _colsum(dr1 * y1n))
        dyn = dr1 * gpost
        dy1 = rstd1 * (dyn - y1n * jnp.mean(dyn * y1n, axis=-1, keepdims=True))
        dy1_ref[...] = dy1.astype(BF16)

    vec = _full((1, D_MODEL))
    nxt = pl.BlockSpec((16, D_FF), lambda i: (jnp.minimum((i + 1) * (tm // 16), last_blk), 0))
    return pl.pallas_call(
        body,
        name="up_bwd",
        grid=(S // tm,),
        in_specs=[_rows(tm, D_FF), nxt, _rows(tm, D_FF), _full((3, D_FF)), _full(w_up_g.shape), _rows(tm, D_MODEL),
                  _rows(tm, D_MODEL), _rows(tm, D_MODEL), vec, vec, _full((6, D_MODEL))],
        out_specs=[_rows(tm, D_FF), _rows(tm, D_MODEL), _rows(tm, D_MODEL), vec, vec, vec, vec, vec],
        out_shape=[jax.ShapeDtypeStruct((S, D_FF), BF16), jax.ShapeDtypeStruct((S, D_MODEL), F32),
                   jax.ShapeDtypeStruct((S, D_MODEL), BF16)] + [jax.ShapeDtypeStruct((1, D_MODEL), F32)] * 5,
        compiler_params=_params(1),
    )(dgc, dgc, dval, conv_w, w_up_g, x1, dout, y1, g_pre, g_post, mod6)


def outproj_bwd(dy1, w_out_g, cat, tm=512):
    S = dy1.shape[0]

    def body(dy_ref, w_ref, attn_ref, dpool_ref, dattn_ref, da4_ref, da16_ref, delta_ref, dl4_ref, dl16_ref, scr):
        dcat = jnp.zeros((tm, 512), F32)
        for j in range(N_CHIPS):
            dcat = dcat + lax.dot_general(dy_ref[:, j * 256:(j + 1) * 256], w_ref[j], NT, preferred_element_type=F32)
        dpool_ref[...] = dcat[:, :POOL_W]
        dattn = dcat[:, POOL_W:]
        dattn_ref[...] = dattn.astype(BF16)
        for h in range(2):
            scr[h] = dattn[:, h * 128:(h + 1) * 128]
        for d, dst in ((4, da4_ref), (16, da16_ref)):
            for r in range(d):
                for h in range(2):
                    dst[r, :, h * 128:(h + 1) * 128] = scr[h, pl.ds(r, tm // d, stride=d), :].astype(BF16)
        prod = dattn * attn_ref[...].astype(F32)
        r = lax.broadcasted_iota(jnp.int32, (GROUP_W, GROUP_W), 0) // HEAD_DIM
        c = lax.broadcasted_iota(jnp.int32, (GROUP_W, GROUP_W), 1) // HEAD_DIM
        ones_bd = jnp.where(r == c, 1.0, 0.0).astype(BF16)
        hi = prod.astype(BF16)
        lo = (prod - hi.astype(F32)).astype(BF16)
        delta = jnp.dot(hi, ones_bd, preferred_element_type=F32) + jnp.dot(lo, ones_bd, preferred_element_type=F32)
        delta_ref[...] = delta
        for h in range(2):
            scr[h] = delta[:, h * 128:(h + 1) * 128]
        for d, dst in ((4, dl4_ref), (16, dl16_ref)):
            for r in range(d):
                for h in range(2):
                    dst[r, :, h * 128:(h + 1) * 128] = scr[h, pl.ds(r, tm // d, stride=d), :]

    cls = lambda d: pl.BlockSpec((d, tm // d, GROUP_W), lambda i: (0, i, 0))
    cls_shape = lambda d, dt: jax.ShapeDtypeStruct((d, S // d, GROUP_W), dt)
    return pl.pallas_call(
        body,
        name="outproj_bwd",
        grid=(S // tm,),
        in_specs=[_rows(tm, D_MODEL), _full(w_out_g.shape), pl.BlockSpec((tm, GROUP_W), lambda i: (i, 1))],
        out_specs=[_rows(tm, POOL_W), _rows(tm, GROUP_W), cls(4), cls(16), _rows(tm, GROUP_W), cls(4), cls(16)],
        out_shape=[jax.ShapeDtypeStruct((S, POOL_W), F32), jax.ShapeDtypeStruct((S, GROUP_W), BF16), cls_shape(4, BF16),
                   cls_shape(16, BF16), jax.ShapeDtypeStruct((S, GROUP_W), F32), cls_shape(4, F32), cls_shape(16, F32)],
        scratch_shapes=[pltpu.VMEM((2, tm, 128), F32)],
        compiler_params=_params(1),
    )(dy1, w_out_g, cat)


def attn_bwd(qkv, dattn, lse, delta, d):
    L = qkv.shape[2]
    nb = L // ATT_BLOCK

    def body(q_ref, k_ref, v_ref, do_ref, l_ref, dl_ref, out_ref, kpad, vpad, dkpad, dvpad):
        kpad[0:ATT_BLOCK, :] = jnp.zeros((ATT_BLOCK, GROUP_W), BF16)
        vpad[0:ATT_BLOCK, :] = jnp.zeros((ATT_BLOCK, GROUP_W), BF16)
        kpad[ATT_BLOCK:, :] = k_ref[...]
        vpad[ATT_BLOCK:, :] = v_ref[...]
        dkpad[...] = jnp.zeros_like(dkpad)
        dvpad[...] = jnp.zeros_like(dvpad)
        band, col, lo = _attn_masks()

        def step(n, carry):
            r0 = pl.multiple_of(n * ATT_BLOCK, ATT_BLOCK)
            valid = band & ((col >= ATT_BLOCK) | (n > 0))
            qb = q_ref[pl.ds(r0, ATT_BLOCK), :]
            dob = do_ref[pl.ds(r0, ATT_BLOCK), :]
            lb = l_ref[pl.ds(r0, ATT_BLOCK), :]
            dlb = dl_ref[pl.ds(r0, ATT_BLOCK), :]
            kb = kpad[pl.ds(r0, 2 * ATT_BLOCK), :]
            vb = vpad[pl.ds(r0, 2 * ATT_BLOCK), :]
            for pair in range(2):
                lanes = slice(pair * 128, (pair + 1) * 128)
                qp, dop, kp, vp = qb[:, lanes], dob[:, lanes], kb[:, lanes], vb[:, lanes]
                dqs = []
                dk_sum = jnp.zeros((2 * ATT_BLOCK, 128), F32)
                dv_sum = jnp.zeros((2 * ATT_BLOCK, 128), F32)
                for hh in range(2):
                    sel = lo if hh == 0 else jnp.logical_not(lo)
                    qm = jnp.where(sel, qp, jnp.zeros_like(qp))
                    dom = jnp.where(sel, dop, jnp.zeros_like(dop))
                    cc = pair * 128 + hh * HEAD_DIM
                    s = lax.dot_general(qm, kp, NT, preferred_element_type=F32)
                    s = jnp.where(valid, s, NEG)
                    p = jnp.exp(s - lb[:, cc:cc + 1])
                    dp = lax.dot_general(dom, vp, NT, preferred_element_type=F32)
                    ds = (p * (dp - dlb[:, cc:cc + 1])).astype(BF16)
                    pb = p.astype(BF16)
                    dqs.append(jnp.dot(ds, kp, preferred_element_type=F32))
                    dk_sum = dk_sum + lax.dot_general(ds, qm, TN, preferred_element_type=F32)
                    dv_sum = dv_sum + lax.dot_general(pb, dom, TN, preferred_element_type=F32)
                out_ref[0, pl.ds(r0, ATT_BLOCK), lanes] = jnp.where(lo, dqs[0], dqs[1])
                dkpad[pl.ds(r0, 2 * ATT_BLOCK), lanes] += dk_sum
                dvpad[pl.ds(r0, 2 * ATT_BLOCK), lanes] += dv_sum
            return carry

        lax.fori_loop(0, nb, step, 0)
        out_ref[1] = dkpad[ATT_BLOCK:, :]
        out_ref[2] = dvpad[ATT_BLOCK:, :]

    spec = lambda kind: pl.BlockSpec((None, None, L, GROUP_W), lambda r: (kind, r, 0, 0))
    cls = pl.BlockSpec((None, L, GROUP_W), lambda r: (r, 0, 0))
    return pl.pallas_call(
        body,
        name=f"attn_bwd_d{d}",
        grid=(d,),
        in_specs=[spec(0), spec(1), spec(2), cls, cls, cls],
        out_specs=pl.BlockSpec((3, None, L, GROUP_W), lambda r: (0, r, 0, 0)),
        out_shape=jax.ShapeDtypeStruct((3, d, L, GROUP_W), F32),
        scratch_shapes=[pltpu.VMEM((L + ATT_BLOCK, GROUP_W), BF16)] * 2 + [pltpu.VMEM((L + ATT_BLOCK, GROUP_W), F32)] * 2,
        compiler_params=_params(1),
    )(qkv, qkv, qkv, dattn, lse, delta)


def pool_bwd(dpool, mixed, wbd, b, scale):
    S = dpool.shape[0]

    def body(dp_ref, mx_ref, w_ref, b_ref, s_ref, du_ref, dw_ref, db_ref, ds_ref):
        dp = dp_ref[...]
        mb = mx_ref[...]
        wv = w_ref[...]
        ypre = jnp.dot(mb, wv, preferred_element_type=F32) + b_ref[...]
        ds_ref[...] = _colsum(dp * ypre)
        dpre = dp * s_ref[...]
        db_ref[...] = _colsum(dpre)
        dpb = dpre.astype(BF16)
        dw_ref[...] = lax.dot_general(mb, dpb, TN, preferred_element_type=F32)
        dmix = lax.dot_general(dpb, wv, NT, preferred_element_type=F32)
        row = lax.broadcasted_iota(jnp.int32, dmix.shape, 0)
        lane, win = _pool_lane_windows(dmix.shape)
        e = dmix / jnp.minimum(row + 1, win).astype(F32)

        def shift(a, k):
            return jnp.where(row < S - k, pltpu.roll(a, S - k, 0), 0.0)

        f2 = e + shift(e, 1)
        f4 = f2 + shift(f2, 2)
        f8 = f4 + shift(f4, 4)
        f16 = f8 + shift(f8, 8)
        du_ref[...] = jnp.where(lane < 64, f2, jnp.where(lane < 128, f4, jnp.where(lane < 192, f8, f16))) - dmix

    vm = pl.BlockSpec(memory_space=pltpu.VMEM)
    return pl.pallas_call(
        body,
        name="pool_bwd",
        in_specs=[vm] * 5,
        out_specs=[vm] * 4,
        out_shape=[jax.ShapeDtypeStruct((S, POOL_W), F32), jax.ShapeDtypeStruct((POOL_W, POOL_W), F32),
                   jax.ShapeDtypeStruct((1, POOL_W), F32), jax.ShapeDtypeStruct((1, POOL_W), F32)],
        compiler_params=_params(),
    )(dpool, mixed, wbd, b, scale)


def inproj_bwd(dqkv, du, x, dx1, w_in_g, g, mod6, tc, tsa, tsb, tm=256):
    S = x.shape[0]

    def body(d0, d1, d2, du_ref, x_ref, dx1_ref, w_ref, g_ref, mod_ref, tc_ref, tsa_ref, tsb_ref,
             dp_ref, gx_ref, dsh_ref, dsc_ref, dg_ref, s4, s16):
        cs, sa, sb = tc_ref[...], tsa_ref[...], tsb_ref[...]
        for d, src, dst in ((4, d1, s4), (16, d2, s16)):
            for kind in range(3):
                for r in range(d):
                    for h in range(2):
                        dst[kind, h, pl.ds(r, tm // d, stride=d), :] = src[kind, r, :, h * 128:(h + 1) * 128]
        for sp in range(20):
            piece, half = sp // 2, sp % 2
            lanes = slice(half * 128, (half + 1) * 128)
            if piece == 0:
                blk = du_ref[:, lanes]
            else:
                kind, gi = (piece - 1) // 3, (piece - 1) % 3
                blk = d0[kind, 0, :, lanes] if gi == 0 else (s4, s16)[gi - 1][kind, half]
                if kind == 0:
                    blk = _rope128(blk, cs, sa, sb, -1.0) * (HEAD_DIM ** -0.5)
                elif kind == 1:
                    blk = _rope128(blk, cs, sa, sb, -1.0)
            dp_ref[:, sp * 128:(sp + 1) * 128] = blk.astype(BF16)
        dh = jnp.zeros((tm, D_MODEL), F32)
        for j in range(N_CHIPS):
            dh = dh + lax.dot_general(dp_ref[:, j * 640:(j + 1) * 640], w_ref[j], NT, preferred_element_type=F32)
        xv = x_ref[...]
        rstd = lax.rsqrt(jnp.mean(xv * xv, axis=-1, keepdims=True) + NORM_EPS)
        n1 = xv * rstd
        gv = g_ref[...]
        one_sc = 1.0 + mod_ref[1:2, :]
        _acc(dsh_ref, _colsum(dh))
        _acc(dsc_ref, _colsum(dh * (n1 * gv)))
        _acc(dg_ref, _colsum(dh * one_sc * n1))
        dn = dh * (gv * one_sc)
        gx_ref[...] = dx1_ref[...] + rstd * (dn - n1 * jnp.mean(dn * n1, axis=-1, keepdims=True))

    vec = _full((1, D_MODEL))
    dspec = lambda d: pl.BlockSpec((3, d, tm // d, GROUP_W), lambda i: (0, 0, i, 0))
    return pl.pallas_call(
        body,
        name="inproj_bwd",
        grid=(S // tm,),
        in_specs=[dspec(d) for d in DILATIONS] + [_rows(tm, POOL_W), _rows(tm, D_MODEL), _rows(tm, D_MODEL), _full(w_in_g.shape),
                                                  vec, _full((6, D_MODEL)), _rows(tm, 128), _rows(tm, 128), _rows(tm, 128)],
        out_specs=[_rows(tm, IN_W), _rows(tm, D_MODEL), vec, vec, vec],
        out_shape=[jax.ShapeDtypeStruct((S, IN_W), BF16), jax.ShapeDtypeStruct((S, D_MODEL), F32)]
        + [jax.ShapeDtypeStruct((1, D_MODEL), F32)] * 3,
        scratch_shapes=[pltpu.VMEM((3, 2, tm, 128), F32)] * 2,
        compiler_params=_params(1),
    )(*dqkv, du, x, dx1, w_in_g, g, mod6, tc, tsa, tsb)


def _adamw(w, g, m, v):
    m = ADAM_B1 * m + (1.0 - ADAM_B1) * g
    v = ADAM_B2 * v + (1.0 - ADAM_B2) * (g * g)
    m_hat = m / (1.0 - ADAM_B1 ** ADAM_STEP)
    v_hat = v / (1.0 - ADAM_B2 ** ADAM_STEP)
    delta = -ADAM_LR * (m_hat / (jnp.sqrt(v_hat) + ADAM_EPS) + ADAM_WD * w)
    return delta, m, v


def adamw_rows(w, g, m, v, tr, name):
    R, C = w.shape

    def body(w_ref, g_ref, m_ref, v_ref, d_ref, mo_ref, vo_ref):
        d_ref[...], mo_ref[...], vo_ref[...] = _adamw(w_ref[...], g_ref[...], m_ref[...], v_ref[...])

    spec = pl.BlockSpec((tr, C), lambda i: (i, 0))
    return pl.pallas_call(
        body,
        name=name,
        grid=(R // tr,),
        in_specs=[spec] * 4,
        out_specs=[spec] * 3,
        out_shape=[jax.ShapeDtypeStruct((R, C), F32)] * 3,
        compiler_params=_params(1),
    )(w, g, m, v)


def adamw_ada(c_all_t, dmod_cols, w, m, v, tr=256):
    R, C = w.shape

    def body(ct_ref, dm_ref, w_ref, m_ref, v_ref, g_ref, d_ref, mo_ref, vo_ref):
        ct = ct_ref[...]
        act = ct * jax.nn.sigmoid(ct)
        g = jnp.zeros((tr, C), F32)
        for b in range(N_DEV):
            g = g + act[:, b:b + 1] * dm_ref[b:b + 1, :]
        g_ref[...] = g
        d_ref[...], mo_ref[...], vo_ref[...] = _adamw(w_ref[...], g, m_ref[...], v_ref[...])

    spec = pl.BlockSpec((tr, C), lambda i: (i, 0))
    return pl.pallas_call(
        body,
        name="adamw_ada",
        grid=(R // tr,),
        in_specs=[pl.BlockSpec((tr, N_DEV), lambda i: (i, 0)), _full((N_DEV, C)), spec, spec, spec],
        out_specs=[spec] * 4,
        out_shape=[jax.ShapeDtypeStruct((R, C), F32)] * 4,
        compiler_params=_params(1),
    )(c_all_t, dmod_cols, w, m, v)


def adamw_small(slab_a, slab_b, convw_g, wpool_g, params):
    names = ["b_ada", "g_pre_mix", "g_post_mix", "g_pre_ffn", "g_post_ffn", "b_pool", "pool_scale", "conv_b", "conv_w", "w_pool"]
    flat = []
    for n in names:
        flat += list(params[n])

    def body(a_ref, b_ref, cw_ref, wp_ref, *rest):
        ins, outs = rest[:30], rest[30:]

        def dev_sum(ref):
            t = ref[0]
            for dev in range(1, N_DEV):
                t = t + ref[dev]
            return t

        sa, sb_, scw, swp = dev_sum(a_ref), dev_sum(b_ref), dev_sum(cw_ref), dev_sum(wp_ref)
        grads = [
            jnp.concatenate([sa[k:k + 1, :] for k in range(6)], axis=1),
            sa[6:7, :], sa[7:8, :], sa[8:9, :], sa[9:10, :],
            sa[10:11, 0:256], sa[10:11, 256:512],
            sb_[3:4, :], scw, swp,
        ]
        for i, g in enumerate(grads):
            w_ref, m_ref, v_ref = ins[3 * i:3 * i + 3]
            d, mo, vo = _adamw(w_ref[...], g, m_ref[...], v_ref[...])
            outs[4 * i][...] = g
            outs[4 * i + 1][...] = d
            outs[4 * i + 2][...] = mo
            outs[4 * i + 3][...] = vo

    vm = pl.BlockSpec(memory_space=pltpu.VMEM)
    out_shape = []
    for n in names:
        out_shape += [jax.ShapeDtypeStruct(params[n][0].shape, F32)] * 4
    outs = pl.pallas_call(
        body,
        name="adamw_small",
        in_specs=[vm] * (4 + len(flat)),
        out_specs=[vm] * len(out_shape),
        out_shape=out_shape,
        compiler_params=_params(),
    )(slab_a, slab_b, convw_g, wpool_g, *flat)
    return {n: outs[4 * i:4 * i + 4] for i, n in enumerate(names)}


def _place():
    return lax.axis_index("x"), lax.axis_index("y"), lax.axis_index("c")


def _other_chips(x, y):
    return [(1 - x, y), (x, 1 - y), (1 - x, 1 - y)]


def _chip_id(cx, cy):
    return 2 * cx + cy


def gather_weights(shards):
    n = len(shards)
    halved = [s.shape[0] % 32 == 0 for s in shards]

    def body(*refs):
        ins, outs = refs[:n], refs[n:2 * n]
        send_sems, recv_sems, loc_sems = refs[2 * n:]
        x, y, c = _place()
        me = _chip_id(x, y)
        chips = _other_chips(x, y)
        sib = (x, y, 1 - c)

        def part(w, chip, half):
            if not halved[w]:
                return outs[w].at[chip]
            rh = shards[w].shape[0] // 2
            return outs[w].at[chip, pl.ds(half * rh, rh), :]

        def src_part(w):
            if not halved[w]:
                return ins[w]
            rh = shards[w].shape[0] // 2
            return ins[w].at[pl.ds(c * rh, rh), :]

        def rcopy(w, k, src, dst, to):
            return pltpu.make_async_remote_copy(src_ref=src, dst_ref=dst, send_sem=send_sems.at[6 * w + k],
                                                recv_sem=recv_sems.at[6 * w + k], device_id=to, device_id_type=MESH)

        local = [pltpu.make_async_copy(ins[w], outs[w].at[me], loc_sems.at[w]) for w in range(n)]
        for cp in local:
            cp.start()
        first = []
        for w in range(n):
            for k, (cx, cy) in enumerate(chips):
                cp = rcopy(w, k, src_part(w), part(w, me, c), (cx, cy, c))
                cp.start()
                first.append(cp)
        passed = []
        for w in range(n):
            for k, (cx, cy) in enumerate(chips):
                blk = part(w, _chip_id(cx, cy), c)
                rcopy(w, k, blk, blk, (cx, cy, c)).wait_recv()
                if halved[w]:
                    cp = rcopy(w, 3 + k, blk, blk, sib)
                    cp.start()
                    passed.append(cp)
        for w in range(n):
            if halved[w]:
                for k, (cx, cy) in enumerate(chips):
                    blk = part(w, _chip_id(cx, cy), 1 - c)
                    rcopy(w, 3 + k, blk, blk, sib).wait_recv()
        for cp in first + passed:
            cp.wait_send()
        for cp in local:
            cp.wait()

    hbm = pl.BlockSpec(memory_space=pl.ANY)
    return pl.pallas_call(
        body,
        name="gather_weights",
        in_specs=[hbm] * n,
        out_specs=[hbm] * n,
        out_shape=[jax.ShapeDtypeStruct((N_CHIPS,) + s.shape, s.dtype) for s in shards],
        scratch_shapes=[pltpu.SemaphoreType.DMA((6 * n,)), pltpu.SemaphoreType.DMA((6 * n,)), pltpu.SemaphoreType.DMA((n,))],
        compiler_params=pltpu.CompilerParams(has_side_effects=True, vmem_limit_bytes=VMEM_LIMIT),
    )(*shards)


HBM_SPEC = pl.BlockSpec(memory_space=pltpu.HBM)
SEM_SPEC = pl.BlockSpec(memory_space=pltpu.SEMAPHORE)
ANY_SPEC = pl.BlockSpec(memory_space=pl.ANY)
EFFECT = pltpu.SideEffectType.DATAFLOW_SIDE_EFFECTING


def _hbm(t):
    return pltpu.with_memory_space_constraint(t, pltpu.HBM)


def _hbm_shapes(ts):
    return [pltpu.HBM(t.shape, t.dtype) for t in ts]


def _half_rows(ref, lead, half, rh):
    return ref.at[lead, pl.ds(half * rh, rh), :]


def gather_split_start(shards, lands, carry, k):
    n = len(shards)

    def body(*refs):
        ins, land = refs[:n], refs[n:2 * n]
        send_sems, recv_sems = refs[2 * n + 1], refs[2 * n + 2]
        loc_sems = refs[-1]
        x, y, c = _place()
        me = _chip_id(x, y)
        if k == 0:
            local = [pltpu.make_async_copy(ins[w], land[w].at[me], loc_sems.at[w]) for w in range(n)]
            for cp in local:
                cp.start()
            for cp in local:
                cp.wait()
        cx, cy = _other_chips(x, y)[k]
        for w in range(n):
            rh = shards[w].shape[0] // 2
            pltpu.make_async_remote_copy(src_ref=ins[w].at[pl.ds(c * rh, rh), :], dst_ref=_half_rows(land[w], me, c, rh),
                                         send_sem=send_sems.at[w], recv_sem=recv_sems.at[w],
                                         device_id=(cx, cy, c), device_id_type=MESH).start()

    args = [_hbm(s) for s in shards] + [_hbm(l) for l in lands] + [_hbm(carry)]
    outs = pl.pallas_call(
        body,
        name="gather_split_start%d" % k,
        out_shape=[pltpu.SemaphoreType.DMA((n,)), pltpu.SemaphoreType.DMA((n,))] + _hbm_shapes(shards) + _hbm_shapes(lands)
        + _hbm_shapes([carry]),
        in_specs=[HBM_SPEC] * (2 * n + 1),
        out_specs=[SEM_SPEC, SEM_SPEC] + [HBM_SPEC] * (2 * n + 1),
        input_output_aliases={i: 2 + i for i in range(2 * n + 1)},
        scratch_shapes=[pltpu.SemaphoreType.DMA((n,))],
        compiler_params=pltpu.CompilerParams(has_side_effects=EFFECT),
    )(*args)
    return outs[0], outs[1], list(outs[2:2 + n]), list(outs[2 + n:2 + 2 * n]), outs[-1]


def gather_split_mid(sems, shards, lands, after):
    n = len(shards)

    def body(*refs):
        ins, land = refs[:n], refs[n:2 * n]
        sem_in = refs[2 * n:2 * n + 6]
        fsend, frecv = refs[2 * n + 7], refs[2 * n + 8]
        x, y, c = _place()
        me = _chip_id(x, y)
        chips = _other_chips(x, y)
        for w in range(n):
            rh = shards[w].shape[0] // 2
            for k, (cx, cy) in enumerate(chips):
                got = _half_rows(land[w], _chip_id(cx, cy), c, rh)
                cp = pltpu.make_async_remote_copy(src_ref=ins[w].at[pl.ds(c * rh, rh), :], dst_ref=got, send_sem=sem_in[2 * k].at[w],
                                                  recv_sem=sem_in[2 * k + 1].at[w], device_id=(cx, cy, c), device_id_type=MESH)
                cp.wait_send()
                cp.wait_recv()
        for w in range(n):
            rh = shards[w].shape[0] // 2
            for k, (cx, cy) in enumerate(chips):
                got = _half_rows(land[w], _chip_id(cx, cy), c, rh)
                pltpu.make_async_remote_copy(src_ref=got, dst_ref=got, send_sem=fsend.at[3 * w + k], recv_sem=frecv.at[3 * w + k],
                                             device_id=(x, y, 1 - c), device_id_type=MESH).start()

    outs = pl.pallas_call(
        body,
        name="gather_split_mid",
        out_shape=[pltpu.SemaphoreType.DMA((3 * n,)), pltpu.SemaphoreType.DMA((3 * n,))] + _hbm_shapes(lands),
        in_specs=[HBM_SPEC] * (2 * n) + [SEM_SPEC] * 6 + [ANY_SPEC],
        out_specs=[SEM_SPEC, SEM_SPEC] + [HBM_SPEC] * n,
        input_output_aliases={n + i: 2 + i for i in range(n)},
        compiler_params=pltpu.CompilerParams(has_side_effects=EFFECT),
    )(*shards, *lands, *sems, after)
    return outs[0], outs[1], list(outs[2:])


def gather_split_done(fsend, frecv, lands, after):
    n = len(lands)

    def body(*refs):
        land = refs[:n]
        ssem, rsem = refs[n], refs[n + 1]
        x, y, c = _place()
        for w in range(n):
            rh = lands[w].shape[1] // 2
            for k, (cx, cy) in enumerate(_other_chips(x, y)):
                sent = _half_rows(land[w], _chip_id(cx, cy), c, rh)
                got = _half_rows(land[w], _chip_id(cx, cy), 1 - c, rh)
                cp = pltpu.make_async_remote_copy(src_ref=sent, dst_ref=got, send_sem=ssem.at[3 * w + k], recv_sem=rsem.at[3 * w + k],
                                                  device_id=(x, y, 1 - c), device_id_type=MESH)
                cp.wait_send()
                cp.wait_recv()

    outs = pl.pallas_call(
        body,
        name="gather_split_done",
        out_shape=_hbm_shapes(lands),
        in_specs=[HBM_SPEC] * n + [SEM_SPEC, SEM_SPEC, ANY_SPEC],
        out_specs=[HBM_SPEC] * n,
        input_output_aliases={i: i for i in range(n)},
        compiler_params=pltpu.CompilerParams(has_side_effects=EFFECT),
    )(*lands, fsend, frecv, after)
    return list(outs)


def _flips():
    return [(fx, fy, fc) for fx in (0, 1) for fy in (0, 1) for fc in (0, 1)][1:]


def _flip(v, f):
    return v if f == 0 else 1 - v


def ada_mod(c3, w_ada, b_cols, conv_w):
    CB = w_ada.shape[1]

    def body(c_ref, w_ref, b_ref, cw_ref, call_ref, mod_ref, cwall_ref, modall, send_sems, recv_sems):
        x, y, c = _place()
        me_dev = 4 * x + 2 * y + c
        me = _chip_id(x, y)
        call_ref[me_dev] = c_ref[0]
        cwall_ref[me] = cw_ref[...]
        sends = []
        for k, (cx, cy) in enumerate(_other_chips(x, y)):
            cp = pltpu.make_async_remote_copy(src_ref=cw_ref, dst_ref=cwall_ref.at[me], send_sem=send_sems.at[10 + k],
                                              recv_sem=recv_sems.at[10 + k], device_id=(cx, cy, c), device_id_type=MESH)
            cp.start()
            sends.append(cp)
        for k, (fx, fy, fc) in enumerate(_flips()):
            cp = pltpu.make_async_remote_copy(src_ref=c_ref.at[0], dst_ref=call_ref.at[me_dev], send_sem=send_sems.at[k],
                                              recv_sem=recv_sems.at[k],
                                              device_id=(_flip(x, fx), _flip(y, fy), _flip(c, fc)), device_id_type=MESH)
            cp.start()
            sends.append(cp)
        for k, (fx, fy, fc) in enumerate(_flips()):
            peer = 4 * _flip(x, fx) + 2 * _flip(y, fy) + _flip(c, fc)
            pltpu.make_async_remote_copy(src_ref=c_ref.at[0], dst_ref=call_ref.at[peer], send_sem=send_sems.at[k],
                                         recv_sem=recv_sems.at[k], device_id=(x, y, c), device_id_type=MESH).wait_recv()
        row = lax.broadcasted_iota(jnp.int32, (N_DEV, D_MODEL), 0)
        call = jnp.zeros((N_DEV, D_MODEL), F32)
        for dev in range(N_DEV):
            call = jnp.where(row == dev, call_ref[dev], call)
        act = call * jax.nn.sigmoid(call)
        modall[me] = jnp.dot(act, w_ref[...], preferred_element_type=F32, precision=lax.Precision.HIGHEST) + b_ref[...]
        for k, (cx, cy) in enumerate(_other_chips(x, y)):
            cp = pltpu.make_async_remote_copy(src_ref=modall.at[me], dst_ref=modall.at[me], send_sem=send_sems.at[7 + k],
                                              recv_sem=recv_sems.at[7 + k], device_id=(cx, cy, c), device_id_type=MESH)
            cp.start()
            sends.append(cp)
        for k, (cx, cy) in enumerate(_other_chips(x, y)):
            blk = modall.at[_chip_id(cx, cy)]
            pltpu.make_async_remote_copy(src_ref=blk, dst_ref=blk, send_sem=send_sems.at[7 + k], recv_sem=recv_sems.at[7 + k],
                                         device_id=(x, y, c), device_id_type=MESH).wait_recv()
        for k, (cx, cy) in enumerate(_other_chips(x, y)):
            blk = cwall_ref.at[_chip_id(cx, cy)]
            pltpu.make_async_remote_copy(src_ref=blk, dst_ref=blk, send_sem=send_sems.at[10 + k], recv_sem=recv_sems.at[10 + k],
                                         device_id=(x, y, c), device_id_type=MESH).wait_recv()
        for cp in sends:
            cp.wait_send()
        mine = [modall[j, pl.ds(me_dev, 1), :] for j in range(N_CHIPS)]
        for r in range(6):
            pieces = []
            for h in range(2):
                pos = r * D_MODEL + h * 512
                pieces.append(mine[pos // CB][:, pos % CB:pos % CB + 512])
            mod_ref[r:r + 1, :] = jnp.concatenate(pieces, axis=1)

    vm = pl.BlockSpec(memory_space=pltpu.VMEM)
    return pl.pallas_call(
        body,
        name="ada_mod",
        in_specs=[vm] * 4,
        out_specs=[vm] * 3,
        out_shape=[jax.ShapeDtypeStruct((N_DEV, 1, D_MODEL), F32), jax.ShapeDtypeStruct((6, D_MODEL), F32),
                   jax.ShapeDtypeStruct((N_CHIPS,) + conv_w.shape, F32)],
        scratch_shapes=[pltpu.VMEM((N_CHIPS, N_DEV, CB), F32), pltpu.SemaphoreType.DMA((13,)), pltpu.SemaphoreType.DMA((13,))],
        compiler_params=pltpu.CompilerParams(has_side_effects=True, vmem_limit_bytes=VMEM_LIMIT),
    )(c3, w_ada, b_cols, conv_w)


def gather_small(blocks):
    n = len(blocks)

    def body(*refs):
        ins, outs = refs[:n], refs[n:2 * n]
        send_sems, recv_sems = refs[2 * n:]
        x, y, c = _place()
        sib = (x, y, 1 - c)
        chips = _other_chips(x, y)

        def dev(px, py, pc):
            return 4 * px + 2 * py + pc

        def cp(w, k, src, block_dev, to):
            return pltpu.make_async_remote_copy(src_ref=src, dst_ref=outs[w].at[block_dev], send_sem=send_sems.at[7 * w + k],
                                                recv_sem=recv_sems.at[7 * w + k], device_id=to, device_id_type=MESH)

        me = dev(x, y, c)
        started = []
        for w in range(n):
            outs[w][me] = ins[w][...]
            t = cp(w, 0, ins[w], me, sib)
            t.start()
            started.append(t)
            for k, (cx, cy) in enumerate(chips):
                t = cp(w, 1 + k, ins[w], me, (cx, cy, c))
                t.start()
                started.append(t)
        for w in range(n):
            for k, (cx, cy) in enumerate(chips):
                b = dev(cx, cy, c)
                cp(w, 1 + k, outs[w].at[b], b, (x, y, c)).wait_recv()
                t = cp(w, 4 + k, outs[w].at[b], b, sib)
                t.start()
                started.append(t)
        for w in range(n):
            b = dev(x, y, 1 - c)
            cp(w, 0, outs[w].at[b], b, (x, y, c)).wait_recv()
            for k, (cx, cy) in enumerate(chips):
                b = dev(cx, cy, 1 - c)
                cp(w, 4 + k, outs[w].at[b], b, (x, y, c)).wait_recv()
        for t in started:
            t.wait_send()

    vm = pl.BlockSpec(memory_space=pltpu.VMEM)
    return pl.pallas_call(
        body,
        name="gather_small",
        in_specs=[vm] * n,
        out_specs=[vm] * n,
        out_shape=[jax.ShapeDtypeStruct((N_DEV,) + b.shape, b.dtype) for b in blocks],
        scratch_shapes=[pltpu.SemaphoreType.DMA((7 * n,)), pltpu.SemaphoreType.DMA((7 * n,))],
        compiler_params=pltpu.CompilerParams(has_side_effects=True, vmem_limit_bytes=VMEM_LIMIT),
    )(*blocks)


def reduce_scatter_grads(grads, chunk_rows):
    n = len(grads)
    shapes = [g.shape[1:] for g in grads]
    halves = [s[0] // 2 for s in shapes]

    def body(*refs):
        gin = refs[:n]
        gout = refs[n:2 * n]
        sibbuf = refs[2 * n:3 * n]
        rest = refs[3 * n:]
        rbuf = rest[:n]
        pown = rest[n:2 * n]
        stage_a, stage_b, stage_o, stage_f = rest[2 * n:2 * n + 4]
        sib_send, sib_recv, ici_send, ici_recv, fin_send, fin_recv, ld_sems, st_sems = rest[2 * n + 4:]
        x, y, c = _place()
        me = _chip_id(x, y)
        chips = _other_chips(x, y)
        sib = (x, y, 1 - c)

        to_sib = []
        for w in range(n):
            rh = halves[w]
            cp = pltpu.make_async_remote_copy(src_ref=gin[w].at[:, pl.ds((1 - c) * rh, rh), :], dst_ref=sibbuf[w],
                                              send_sem=sib_send.at[w], recv_sem=sib_recv.at[w], device_id=sib,
                                              device_id_type=MESH)
            cp.start()
            to_sib.append(cp)

        sent = []
        for w in range(n):
            rh, cw = halves[w], shapes[w][1]
            ch = chunk_rows[w]
            to_sib[w].wait_recv()
            for k in range(4):
                chip = me if k == 3 else _chip_id(*chips[k])
                for r0 in range(0, rh, ch):
                    la = pltpu.make_async_copy(gin[w].at[chip, pl.ds(c * rh + r0, ch), :], stage_a.at[0:ch, 0:cw], ld_sems.at[0])
                    lb = pltpu.make_async_copy(sibbuf[w].at[chip, pl.ds(r0, ch), :], stage_b.at[0:ch, 0:cw], ld_sems.at[1])
                    la.start()
                    lb.start()
                    la.wait()
                    lb.wait()
                    tot = stage_a[0:ch, 0:cw] + stage_b[0:ch, 0:cw]
                    if k == 3:
                        pown[w][r0:r0 + ch, :] = tot
                    else:
                        stage_o[0:ch, 0:cw] = tot.astype(BF16)
                        cx, cy = chips[k]
                        cp = pltpu.make_async_remote_copy(src_ref=stage_o.at[0:ch, 0:cw], dst_ref=rbuf[w].at[k, r0:r0 + ch, :],
                                                          send_sem=ici_send.at[3 * w + k], recv_sem=ici_recv.at[3 * w + k],
                                                          device_id=(cx, cy, c), device_id_type=MESH)
                        cp.start()
                        cp.wait_send()
            sent.append(w)

        fin = []
        for w in range(n):
            rh, cw = halves[w], shapes[w][1]
            for k in range(3):
                whole = rbuf[w].at[k]
                pltpu.make_async_remote_copy(src_ref=whole, dst_ref=whole, send_sem=ici_send.at[3 * w + k],
                                             recv_sem=ici_recv.at[3 * w + k], device_id=(x, y, c),
                                             device_id_type=MESH).wait_recv()
            pown[w][...] = ((pown[w][...] + rbuf[w][0].astype(F32)) + rbuf[w][1].astype(F32)) + rbuf[w][2].astype(F32)
            mine = gout[w].at[pl.ds(c * rh, rh), :]
            st = pltpu.make_async_copy(pown[w], mine, st_sems.at[w])
            st.start()
            cp = pltpu.make_async_remote_copy(src_ref=pown[w], dst_ref=mine, send_sem=fin_send.at[w], recv_sem=fin_recv.at[w],
                                              device_id=sib, device_id_type=MESH)
            cp.start()
            fin.append((st, cp))
        for w in range(n):
            rh = halves[w]
            theirs = gout[w].at[pl.ds((1 - c) * rh, rh), :]
            pltpu.make_async_remote_copy(src_ref=theirs, dst_ref=theirs, send_sem=fin_send.at[w], recv_sem=fin_recv.at[w],
                                         device_id=(x, y, c), device_id_type=MESH).wait_recv()
        for cp in to_sib:
            cp.wait_send()
        for st, cp in fin:
            st.wait()
            cp.wait_send()

    hbm = pl.BlockSpec(memory_space=pl.ANY)
    max_ch = max(chunk_rows)
    max_c = max(s[1] for s in shapes)
    outs = pl.pallas_call(
        body,
        name="reduce_scatter_grads",
        in_specs=[hbm] * n,
        out_specs=[hbm] * (2 * n),
        out_shape=[jax.ShapeDtypeStruct(s, F32) for s in shapes]
        + [jax.ShapeDtypeStruct((N_CHIPS, h, s[1]), F32) for h, s in zip(halves, shapes)],
        scratch_shapes=[pltpu.VMEM((3, h, s[1]), BF16) for h, s in zip(halves, shapes)]
        + [pltpu.VMEM((h, s[1]), F32) for h, s in zip(halves, shapes)]
        + [pltpu.VMEM((max_ch, max_c), F32), pltpu.VMEM((max_ch, max_c), F32), pltpu.VMEM((max_ch, max_c), BF16),
           pltpu.VMEM((8, 128), F32)]
        + [pltpu.SemaphoreType.DMA((n,)), pltpu.SemaphoreType.DMA((n,)), pltpu.SemaphoreType.DMA((3 * n,)),
           pltpu.SemaphoreType.DMA((3 * n,)), pltpu.SemaphoreType.DMA((n,)), pltpu.SemaphoreType.DMA((n,)),
           pltpu.SemaphoreType.DMA((2,)), pltpu.SemaphoreType.DMA((n,))],
        compiler_params=pltpu.CompilerParams(has_side_effects=True, vmem_limit_bytes=VMEM_LIMIT),
    )(*grads)
    return outs[:n]


def split_start(name, bufs, plan, n_sem, carry):
    nb = len(bufs)

    def body(*refs):
        x, y, c = _place()
        ssem, rsem = refs[nb + 1], refs[nb + 2]
        for i, (src, dst, dev) in enumerate(plan(refs[:nb], x, y, c)):
            pltpu.make_async_remote_copy(src_ref=src, dst_ref=dst, send_sem=ssem.at[i], recv_sem=rsem.at[i], device_id=dev,
                                         device_id_type=MESH).start()

    alls = list(bufs) + [carry]
    outs = pl.pallas_call(
        body,
        name=name,
        out_shape=[pltpu.SemaphoreType.DMA((n_sem,)), pltpu.SemaphoreType.DMA((n_sem,))] + _hbm_shapes(alls),
        in_specs=[HBM_SPEC] * (nb + 1),
        out_specs=[SEM_SPEC, SEM_SPEC] + [HBM_SPEC] * (nb + 1),
        input_output_aliases={i: 2 + i for i in range(nb + 1)},
        compiler_params=pltpu.CompilerParams(has_side_effects=EFFECT),
    )(*[_hbm(t) for t in alls])
    return outs[0], outs[1], list(outs[2:2 + nb]), outs[-1]


def split_wait(name, ssem, rsem, bufs, plan, after):
    nb = len(bufs)

    def body(*refs):
        x, y, c = _place()
        s_ref, r_ref = refs[nb], refs[nb + 1]
        for i, (src, dst, dev) in enumerate(plan(refs[:nb], x, y, c)):
            cp = pltpu.make_async_remote_copy(src_ref=src, dst_ref=dst, send_sem=s_ref.at[i], recv_sem=r_ref.at[i], device_id=dev,
                                              device_id_type=MESH)
            cp.wait_send()
            cp.wait_recv()

    outs = pl.pallas_call(
        body,
        name=name,
        out_shape=_hbm_shapes(bufs),
        in_specs=[HBM_SPEC] * nb + [SEM_SPEC, SEM_SPEC, ANY_SPEC],
        out_specs=[HBM_SPEC] * nb,
        input_output_aliases={i: i for i in range(nb)},
        compiler_params=pltpu.CompilerParams(has_side_effects=EFFECT),
    )(*bufs, ssem, rsem, after)
    return list(outs)


def _gather_ici_plan(n):
    def plan(refs, x, y, c):
        out = []
        for w in range(n):
            rh = refs[w].shape[0] // 2
            for cx, cy in _other_chips(x, y):
                out.append((refs[w].at[pl.ds(c * rh, rh), :], _half_rows(refs[n + w], _chip_id(x, y), c, rh), (cx, cy, c)))
        return out

    return plan


def _gather_d2d_plan(n):
    def plan(refs, x, y, c):
        out = []
        for w in range(n):
            rh = refs[w].shape[1] // 2
            for cx, cy in _other_chips(x, y):
                blk = _half_rows(refs[w], _chip_id(cx, cy), c, rh)
                out.append((blk, blk, (x, y, 1 - c)))
        return out

    return plan


def _rs_d2d_plan(n):
    def plan(refs, x, y, c):
        out = []
        for w in range(n):
            rh = refs[w].shape[1] // 2
            out.append((refs[w].at[:, pl.ds((1 - c) * rh, rh), :], refs[n + w], (x, y, 1 - c)))
        return out

    return plan


def _rs_ici_plan(n):
    def plan(refs, x, y, c):
        out = []
        for w in range(n):
            for k, (cx, cy) in enumerate(_other_chips(x, y)):
                out.append((refs[w].at[_chip_id(cx, cy)], refs[n + w].at[k], (cx, cy, c)))
        return out

    return plan


def _rs_share_plan(n):
    def plan(refs, x, y, c):
        out = []
        for w in range(n):
            rh = refs[w].shape[0] // 2
            rows = refs[w].at[pl.ds(c * rh, rh), :]
            out.append((rows, rows, (x, y, 1 - c)))
        return out

    return plan


def rs_add(grad, sibbuf, place, tr, name):
    _, R, C = grad.shape
    nt = (R // 2) // tr

    def body(p_ref, g_ref, s_ref, o_ref):
        o_ref[...] = (g_ref[...] + s_ref[...]).astype(BF16)

    return pl.pallas_call(
        body,
        name=name,
        grid_spec=pltpu.PrefetchScalarGridSpec(
            num_scalar_prefetch=1,
            grid=(N_CHIPS, nt),
            in_specs=[pl.BlockSpec((None, tr, C), lambda j, i, p: (j, p[0] * nt + i, 0)),
                      pl.BlockSpec((None, tr, C), lambda j, i, p: (j, i, 0))],
            out_specs=pl.BlockSpec((None, tr, C), lambda j, i, p: (j, i, 0)),
        ),
        out_shape=jax.ShapeDtypeStruct((N_CHIPS, R // 2, C), BF16),
        compiler_params=_params(2),
    )(place, grad, sibbuf)


def rs_final(grad, sibbuf, rbuf, place, tr, name):
    _, R, C = grad.shape
    nt = (R // 2) // tr

    def body(p_ref, g_ref, s_ref, r_ref, o_ref):
        o_ref[...] = (((g_ref[...] + s_ref[...]) + r_ref[0].astype(F32)) + r_ref[1].astype(F32)) + r_ref[2].astype(F32)

    return pl.pallas_call(
        body,
        name=name,
        grid_spec=pltpu.PrefetchScalarGridSpec(
            num_scalar_prefetch=1,
            grid=(nt,),
            in_specs=[pl.BlockSpec((None, tr, C), lambda i, p: (p[1], p[0] * nt + i, 0)),
                      pl.BlockSpec((None, tr, C), lambda i, p: (p[1], i, 0)),
                      pl.BlockSpec((3, tr, C), lambda i, p: (0, i, 0))],
            out_specs=pl.BlockSpec((tr, C), lambda i, p: (p[0] * nt + i, 0)),
        ),
        out_shape=jax.ShapeDtypeStruct((R, C), F32),
        compiler_params=_params(1),
    )(place, grad, sibbuf, rbuf)


class GradReduce:
    def __init__(self, tag, grads, rows, place):
        self.tag, self.grads, self.rows, self.place = tag, grads, rows, place
        self.n = len(grads)

    def d2d_start(self, carry):
        sib = [lax.empty((N_CHIPS, g.shape[1] // 2, g.shape[2]), F32) for g in self.grads]
        self.s1, self.r1, bufs, carry = split_start(f"rs_{self.tag}_d2d_start", self.grads + sib, _rs_d2d_plan(self.n), self.n, carry)
        self.bufs1 = bufs
        return carry

    def add_and_ici_start(self, after, carry):
        bufs = split_wait(f"rs_{self.tag}_d2d_wait", self.s1, self.r1, self.bufs1, _rs_d2d_plan(self.n), after)
        self.grads, self.sib = bufs[:self.n], bufs[self.n:]
        pb = [rs_add(g, s, self.place, tr, f"rs_{self.tag}_add{w}")
              for w, (g, s, tr) in enumerate(zip(self.grads, self.sib, self.rows))]
        rb = [lax.empty((3,) + p.shape[1:], BF16) for p in pb]
        self.s2, self.r2, self.bufs2, carry = split_start(f"rs_{self.tag}_ici_start", pb + rb, _rs_ici_plan(self.n), 3 * self.n, carry)
        return carry

    def final_and_share_start(self, after, carry):
        bufs = split_wait(f"rs_{self.tag}_ici_wait", self.s2, self.r2, self.bufs2, _rs_ici_plan(self.n), after)
        rb = bufs[self.n:]
        full = [rs_final(g, s, r, self.place, tr, f"rs_{self.tag}_final{w}")
                for w, (g, s, r, tr) in enumerate(zip(self.grads, self.sib, rb, self.rows))]
        self.s3, self.r3, self.bufs3, carry = split_start(f"rs_{self.tag}_share_start", full, _rs_share_plan(self.n), self.n, carry)
        return carry

    def finish(self, after):
        return split_wait(f"rs_{self.tag}_share_wait", self.s3, self.r3, self.bufs3, _rs_share_plan(self.n), after)


def _rope_tables(positions):
    inv_freq = ROPE_THETA ** (-jnp.arange(0, ROT_DIM, 2, dtype=F32) / ROT_DIM)
    ang = positions.astype(F32)[:, None] * inv_freq
    cos, sin = jnp.cos(ang), jnp.sin(ang)
    S = positions.shape[0]
    one, zero = jnp.ones((S, 48), F32), jnp.zeros((S, 48), F32)
    z8 = jnp.zeros((S, 8), F32)
    tc = jnp.concatenate([cos, cos, one], axis=1)
    tsa = jnp.concatenate([z8, sin, zero], axis=1)
    tsb = jnp.concatenate([-sin, z8, zero], axis=1)
    return tuple(jnp.tile(t, (1, 2)) for t in (tc, tsa, tsb))


def _block_diag(w_pool):
    wbd = jnp.zeros((POOL_W, POOL_W), F32)
    for gi in range(4):
        wbd = wbd.at[gi * 64:(gi + 1) * 64, gi * 64:(gi + 1) * 64].set(w_pool[gi])
    return wbd


def kernel(x, c, positions, w_ada, b_ada, g_pre_mix, g_post_mix, g_pre_ffn, g_post_ffn, w_in, w_pool, b_pool, pool_scale, w_out, w_up, conv_w, conv_b, w_down, loss_target, m_w_ada, m_b_ada, m_g_pre_mix, m_g_post_mix, m_g_pre_ffn, m_g_post_ffn, m_w_in, m_w_pool, m_b_pool, m_pool_scale, m_w_out, m_w_up, m_conv_w, m_conv_b, m_w_down, v_w_ada, v_b_ada, v_g_pre_mix, v_g_post_mix, v_g_pre_ffn, v_g_post_ffn, v_w_in, v_w_pool, v_b_pool, v_pool_scale, v_w_out, v_w_up, v_conv_w, v_conv_b, v_w_down):
    xi, yi, ci = lax.axis_index("x"), lax.axis_index("y"), lax.axis_index("c")
    chip = 2 * xi + yi
    place = jnp.stack([ci, chip]).astype(jnp.int32)
    x2, tgt = x[0], loss_target[0]
    S = x2.shape[0]

    cb_ada = w_ada.shape[2]
    b_cols = lax.dynamic_slice(b_ada, (0, chip * cb_ada), (1, cb_ada))
    c_all, mod6, conv_w_g = ada_mod(c.reshape(1, 1, D_MODEL), w_ada[0], b_cols, conv_w[0])
    conv_w_f = jnp.transpose(conv_w_g, (1, 0, 2)).reshape(3, D_FF)

    def landing(s_):
        return lax.dynamic_update_slice(lax.empty((N_CHIPS,) + s_.shape, s_.dtype), s_[None], (chip, 0, 0))

    mix_sh = [w_in[0].astype(BF16), w_out[0].astype(BF16)]
    ffn_sh = [w_up[0].astype(BF16), w_down[0].astype(BF16)]
    ga_s, ga_r, ga_bufs, mod6 = split_start("gather_mix_ici_start", mix_sh + [landing(t) for t in mix_sh], _gather_ici_plan(2), 6, mod6)
    gb_s, gb_r, gb_bufs, mod6 = split_start("gather_ffn_ici_start", ffn_sh + [landing(t) for t in ffn_sh], _gather_ici_plan(2), 6, mod6)
    tc, tsa, tsb = _rope_tables(positions[0])
    wbd = _block_diag(w_pool[0]).astype(BF16)
    b_pool2, scale2 = b_pool.reshape(1, POOL_W), pool_scale
    ga_bufs = split_wait("gather_mix_ici_wait", ga_s, ga_r, ga_bufs, _gather_ici_plan(2), tc)
    gc_s, gc_r, mix_land, mod6 = split_start("gather_mix_d2d_start", ga_bufs[2:], _gather_d2d_plan(2), 6, mod6)
    w_in_g, w_out_g = split_wait("gather_mix_d2d_wait", gc_s, gc_r, mix_land, _gather_d2d_plan(2), mod6)

    h1, u, *qkv = inproj_fwd(x2, g_pre_mix, mod6, w_in_g, tc, tsa, tsb)
    mixed, pool = pool_fwd(u, wbd, b_pool2, scale2)
    o_l = [attn_fwd(t, d) for t, d in zip(qkv, DILATIONS)]
    gb_bufs = split_wait("gather_ffn_ici_wait", gb_s, gb_r, gb_bufs, _gather_ici_plan(2), o_l[2][1])
    gd_s, gd_r, ffn_land, pool = split_start("gather_ffn_d2d_start", gb_bufs[2:], _gather_d2d_plan(2), 6, pool)
    cat, lse, lse4, lse16, y1, x1, h2 = outproj_fwd([o for o, _ in o_l] + [l for _, l in o_l], pool, x2, w_out_g, g_post_mix,
                                                    g_pre_ffn, mod6)
    lses = [lse[None], lse4, lse16]
    w_up_g, w_down_g = split_wait("gather_ffn_d2d_wait", gd_s, gd_r, ffn_land, _gather_d2d_plan(2), h2)
    w_down_f = w_down_g.reshape(D_FF, D_MODEL)
    gate, val = up_fwd(h2, w_up_g)
    a, dy2, dout, loss_v, d_gt_f, d_g_post_ffn = down_fwd(gate, val, conv_w_f, conv_b, w_down_f, x1, tgt, g_post_ffn, mod6)

    dgc, dval, d_conv_w, d_conv_b = down_bwd(dy2, w_down_f, gate, val, conv_w_f, conv_b)
    dw_down = dw_matmul(a, dy2, 2, (HALF_FF, D_MODEL), HALF_FF, D_MODEL, True, "dw_down")
    dgate, dx1, dy1, d_sh_f, d_sc_f, d_g_pre_ffn, d_gt_m, d_g_post_mix = up_bwd(
        dgc, dval, conv_w_f, w_up_g, x1, dout, y1, g_pre_ffn, g_post_mix, mod6)
    dw_up = dw_matmul(h2, dgate, 4, (D_MODEL, HALF_FF), D_MODEL, HALF_FF, False, "dw_up_gate", n_blk=2)
    dw_up = dw_matmul(h2, dval, 4, (D_MODEL, HALF_FF), D_MODEL, HALF_FF, False, "dw_up_val", prev=dw_up, blk_off=2, n_blk=2)
    rs_ffn = GradReduce("ffn", [dw_up, dw_down.reshape(N_CHIPS, D_FF // N_CHIPS, D_MODEL)], [256, 176], place)
    dy1 = rs_ffn.d2d_start(dy1)
    dpool, da1, da4, da16, dl1, dl4, dl16 = outproj_bwd(dy1, w_out_g, cat)
    dw_out = dw_matmul(cat, dy1, 4, (512, 256), 512, 256, False, "dw_out")
    dpool = rs_ffn.add_and_ici_start(dw_out, dpool)
    du, d_wbd, d_b_pool, d_scale = pool_bwd(dpool, mixed, wbd, b_pool2, scale2)
    dqkv = [attn_bwd(t, da, ls, dl, d) for t, da, ls, dl, d in zip(qkv, (da1[None], da4, da16), lses, (dl1[None], dl4, dl16), DILATIONS)]
    dproj, grad_x, d_sh_m, d_sc_m, d_g_pre_mix = inproj_bwd(dqkv, du, x2, dx1, w_in_g, g_pre_mix, mod6, tc, tsa, tsb)

    z1 = jnp.zeros((1, D_MODEL), F32)
    slab_a = jnp.concatenate(
        [d_sh_m, d_sc_m, d_gt_m, d_sh_f, d_sc_f, d_gt_f, d_g_pre_mix, d_g_post_mix, d_g_pre_ffn, d_g_post_ffn,
         jnp.concatenate([d_b_pool, d_scale, jnp.zeros((1, 512), F32)], axis=1)] + [z1] * 5, axis=0)
    slab_b = jnp.concatenate([d_conv_w, d_conv_b, jnp.zeros((4, D_FF), F32)], axis=0)
    d_wpool = jnp.concatenate([d_wbd[gi * 64:(gi + 1) * 64, gi * 64:(gi + 1) * 64] for gi in range(4)], axis=0)
    slab_a_g, slab_b_g, wpool_g = gather_small([slab_a, slab_b, d_wpool])
    cw_cols = conv_w.shape[2]
    convw_g = lax.dynamic_slice(slab_b_g, (0, 0, chip * cw_cols), (N_DEV, 3, cw_cols))
    dw_in = dw_matmul(h1, dproj, 4, (D_MODEL, 640), D_MODEL, 640, False, "dw_in")
    rs_mix = GradReduce("mix", [dw_in, dw_out], [256, 256], place)
    slab_a_g = rs_mix.d2d_start(slab_a_g)
    slab_a_g = rs_ffn.final_and_share_start(slab_a_g, slab_a_g)
    slab_a_g = rs_mix.add_and_ici_start(slab_a_g, slab_a_g)
    dmod_cols = lax.dynamic_slice(slab_a_g[:, :6, :].reshape(N_DEV, 6 * D_MODEL), (0, chip * cb_ada), (N_DEV, cb_ada))

    res = {}

    def big_adamw(name, w, g, m, v, tr):
        d_, m_, v_ = adamw_rows(w[0], g, m[0], v[0], tr, "adamw_" + name)
        res[name] = (g[None], d_[None], m_[None], v_[None])
        return v_

    g_ada, d_ada, m_ada, v_ada = adamw_ada(c_all.reshape(N_DEV, D_MODEL).T, dmod_cols, w_ada[0], m_w_ada[0], v_w_ada[0])
    res["w_ada"] = (g_ada[None], d_ada[None], m_ada[None], v_ada[None])
    g_w_up, g_w_down = rs_ffn.finish(v_ada)
    big_adamw("w_up", w_up, g_w_up, m_w_up, v_w_up, 256)
    last = big_adamw("w_down", w_down, g_w_down, m_w_down, v_w_down, 352)
    rs_mix.final_and_share_start(last, jnp.zeros((8, 128), F32))
    g_w_in, g_w_out = rs_mix.finish(last)
    big_adamw("w_in", w_in, g_w_in, m_w_in, v_w_in, 256)
    big_adamw("w_out", w_out, g_w_out, m_w_out, v_w_out, 256)
    flat = lambda t: t.reshape(1, POOL_W)
    wp = lambda t: t.reshape(POOL_W, 64)
    small = adamw_small(slab_a_g, slab_b_g, convw_g, wpool_g, {
        "b_ada": (b_ada, m_b_ada, v_b_ada), "g_pre_mix": (g_pre_mix, m_g_pre_mix, v_g_pre_mix),
        "g_post_mix": (g_post_mix, m_g_post_mix, v_g_post_mix), "g_pre_ffn": (g_pre_ffn, m_g_pre_ffn, v_g_pre_ffn),
        "g_post_ffn": (g_post_ffn, m_g_post_ffn, v_g_post_ffn), "b_pool": (flat(b_pool), flat(m_b_pool), flat(v_b_pool)),
        "pool_scale": (pool_scale, m_pool_scale, v_pool_scale), "conv_b": (conv_b, m_conv_b, v_conv_b),
        "conv_w": (conv_w[0], m_conv_w[0], v_conv_w[0]), "w_pool": (wp(w_pool), wp(m_w_pool), wp(v_w_pool))})
    for name in ("b_ada", "g_pre_mix", "g_post_mix", "g_pre_ffn", "g_post_ffn", "pool_scale", "conv_b"):
        res[name] = tuple(small[name])
    res["b_pool"] = tuple(t.reshape(1, 4, 64) for t in small["b_pool"])
    res["conv_w"] = tuple(t[None] for t in small["conv_w"])
    res["w_pool"] = tuple(t.reshape(1, 4, 64, 64) for t in small["w_pool"])

    loss = lax.psum(loss_v[0, 0], ("x", "y", "c"))
    order = ["w_ada", "b_ada", "g_pre_mix", "g_post_mix", "g_pre_ffn", "g_post_ffn", "w_in", "w_pool", "b_pool", "pool_scale",
             "w_out", "w_up", "conv_w", "conv_b", "w_down"]
    outs = [loss, grad_x[None]]
    for k in range(4):
        outs += [res[n][k] for n in order]
    return tuple(outs)
```

```python
import functools
import math

import jax
import jax.numpy as jnp
from jax import lax
from jax.experimental import pallas as pl
from jax.experimental.pallas import tpu as pltpu

F32 = jnp.float32
BF16 = jnp.bfloat16
MESH = pl.DeviceIdType.MESH

D_MODEL = 1024
HEAD_DIM = 64
POOL_W = 256
GROUP_W = 256
DILATIONS = (1, 4, 16)
ATT_BLOCK = 128
IN_W = 2560
D_FF = 2816
HALF_FF = 1408
ROT_DIM = 16
ROPE_THETA = 500000.0
NORM_EPS = 1e-6
N_CHIPS = 4
N_DEV = 8
NEG = -1e30

ADAM_LR = 0.001
ADAM_B1 = 0.9
ADAM_B2 = 0.999
ADAM_EPS = 1e-08
ADAM_WD = 0.01
ADAM_STEP = 10

VMEM_LIMIT = 56 * 1024 * 1024

NT = (((1,), (1,)), ((), ()))
TN = (((0,), (0,)), ((), ()))


def _params(n_grid=0, **kw):
    sem = ("arbitrary",) * n_grid if n_grid else None
    return pltpu.CompilerParams(dimension_semantics=sem, vmem_limit_bytes=VMEM_LIMIT, **kw)


def _full(shape):
    nd = len(shape)
    return pl.BlockSpec(tuple(shape), lambda *_: (0,) * nd, pipeline_mode=pl.Buffered(1))


def _rows(tm, ncol):
    return pl.BlockSpec((tm, ncol), lambda i: (i, 0))


def _acc(ref, val):
    @pl.when(pl.program_id(0) == 0)
    def _():
        ref[...] = jnp.zeros_like(ref)

    ref[...] += val


def _colsum(v):
    return jnp.sum(v, axis=0, keepdims=True)


def _rope128(t, cs, sa, sb, sign):
    return t * cs + sign * (pltpu.roll(t, 8, 1) * sa + pltpu.roll(t, 120, 1) * sb)


GELU_C0 = math.sqrt(2.0 / math.pi)
GELU_C1 = GELU_C0 * 0.044715


def _gelu(z):
    z2 = z * z
    t = jnp.tanh(z * (GELU_C0 + GELU_C1 * z2))
    u = 0.5 * t + 0.5
    return z * u, u, t, z2


def _gelu_grad(z, u, t, z2):
    return u + (z * (GELU_C0 + (3.0 * GELU_C1) * z2)) * (0.5 - 0.5 * (t * t))


def _conv_taps(gate, halo, first):
    row = lax.broadcasted_iota(jnp.int32, gate.shape, 0)
    halo = jnp.where(first, 0.0, halo)
    p1 = halo[15:16, :]
    p2 = halo[14:15, :]
    g1 = jnp.where(row == 0, p1, pltpu.roll(gate, 1, 0))
    g2 = jnp.where(row == 0, p2, jnp.where(row == 1, p1, pltpu.roll(gate, 2, 0)))
    return g1, g2


def inproj_fwd(x, g, mod6, w_in_g, tc, tsa, tsb, tm=512):
    S = x.shape[0]

    def body(x_ref, g_ref, mod_ref, w_ref, tc_ref, tsa_ref, tsb_ref, h_ref, u_ref, q1_ref, q4_ref, q16_ref, scr):
        qkv_refs = (q1_ref, q4_ref, q16_ref)
        xv = x_ref[...]
        rstd = lax.rsqrt(jnp.mean(xv * xv, axis=-1, keepdims=True) + NORM_EPS)
        h = ((xv * rstd) * g_ref[...]) * (1.0 + mod_ref[1:2, :]) + mod_ref[0:1, :]
        hb = h.astype(BF16)
        h_ref[...] = hb
        cs, sa, sb = tc_ref[...], tsa_ref[...], tsb_ref[...]
        for j in range(N_CHIPS):
            res = jnp.dot(hb, w_ref[j], preferred_element_type=F32)
            for t in range(5):
                sp = 5 * j + t
                piece, half = sp // 2, sp % 2
                blk = res[:, t * 128:(t + 1) * 128]
                lanes = slice(half * 128, (half + 1) * 128)
                if piece == 0:
                    u_ref[:, lanes] = blk
                else:
                    kind, gi = (piece - 1) // 3, (piece - 1) % 3
                    if kind == 0:
                        blk = _rope128(blk, cs, sa, sb, 1.0) * (HEAD_DIM ** -0.5)
                    elif kind == 1:
                        blk = _rope128(blk, cs, sa, sb, 1.0)
                    d = DILATIONS[gi]
                    if d == 1:
                        q1_ref[kind, 0, :, lanes] = blk.astype(BF16)
                    else:
                        scr[...] = blk
                        for r in range(d):
                            qkv_refs[gi][kind, r, :, lanes] = scr[pl.ds(r, tm // d, stride=d), :].astype(BF16)

    cls = lambda d: pl.BlockSpec((3, d, tm // d, GROUP_W), lambda i: (0, 0, i, 0))
    return pl.pallas_call(
        body,
        name="inproj_fwd",
        grid=(S // tm,),
        in_specs=[_rows(tm, D_MODEL), _full((1, D_MODEL)), _full((6, D_MODEL)), _full(w_in_g.shape),
                  _rows(tm, 128), _rows(tm, 128), _rows(tm, 128)],
        out_specs=[_rows(tm, D_MODEL), _rows(tm, POOL_W)] + [cls(d) for d in DILATIONS],
        out_shape=[jax.ShapeDtypeStruct((S, D_MODEL), BF16), jax.ShapeDtypeStruct((S, POOL_W), F32)]
        + [jax.ShapeDtypeStruct((3, d, S // d, GROUP_W), BF16) for d in DILATIONS],
        scratch_shapes=[pltpu.VMEM((tm, 128), F32)],
        compiler_params=_params(1),
    )(x, g, mod6, w_in_g, tc, tsa, tsb)


def _attn_masks():
    row = lax.broadcasted_iota(jnp.int32, (ATT_BLOCK, 2 * ATT_BLOCK), 0)
    col = lax.broadcasted_iota(jnp.int32, (ATT_BLOCK, 2 * ATT_BLOCK), 1)
    band = (col >= row) & (col <= row + ATT_BLOCK)
    lane = lax.broadcasted_iota(jnp.int32, (ATT_BLOCK, 128), 1)
    return band, col, lane < HEAD_DIM


def attn_fwd(qkv, d):
    L = qkv.shape[2]
    nb = L // ATT_BLOCK

    def body(q_ref, k_ref, v_ref, o_ref, l_ref, kpad, vpad):
        kpad[0:ATT_BLOCK, :] = jnp.zeros((ATT_BLOCK, GROUP_W), BF16)
        vpad[0:ATT_BLOCK, :] = jnp.zeros((ATT_BLOCK, GROUP_W), BF16)
        kpad[ATT_BLOCK:, :] = k_ref[...]
        vpad[ATT_BLOCK:, :] = v_ref[...]
        band, col, lo = _attn_masks()

        def step(n, carry):
            r0 = pl.multiple_of(n * ATT_BLOCK, ATT_BLOCK)
            valid = band & ((col >= ATT_BLOCK) | (n > 0))
            qb = q_ref[pl.ds(r0, ATT_BLOCK), :]
            kb = kpad[pl.ds(r0, 2 * ATT_BLOCK), :]
            vb = vpad[pl.ds(r0, 2 * ATT_BLOCK), :]
            for pair in range(2):
                lanes = slice(pair * 128, (pair + 1) * 128)
                qp, kp, vp = qb[:, lanes], kb[:, lanes], vb[:, lanes]
                outs, lses = [], []
                for hh in range(2):
                    sel = lo if hh == 0 else jnp.logical_not(lo)
                    qm = jnp.where(sel, qp, jnp.zeros_like(qp))
                    s = lax.dot_general(qm, kp, NT, preferred_element_type=F32)
                    s = jnp.where(valid, s, NEG)
                    m = jnp.max(s, axis=1, keepdims=True)
                    p = jnp.exp(s - m)
                    den = jnp.sum(p, axis=1, keepdims=True)
                    pv = jnp.dot(p.astype(BF16), vp, preferred_element_type=F32)
                    outs.append(pv / den)
                    lses.append(m + jnp.log(den))
                o_ref[pl.ds(r0, ATT_BLOCK), lanes] = jnp.where(lo, outs[0], outs[1])
                l_ref[pl.ds(r0, ATT_BLOCK), lanes] = jnp.where(lo, lses[0], lses[1])
            return carry

        lax.fori_loop(0, nb, step, 0)

    spec = lambda kind: pl.BlockSpec((None, None, L, GROUP_W), lambda r: (kind, r, 0, 0))
    return pl.pallas_call(
        body,
        name=f"attn_fwd_d{d}",
        grid=(d,),
        in_specs=[spec(0), spec(1), spec(2)],
        out_specs=[pl.BlockSpec((None, L, GROUP_W), lambda r: (r, 0, 0))] * 2,
        out_shape=[jax.ShapeDtypeStruct((d, L, GROUP_W), F32)] * 2,
        scratch_shapes=[pltpu.VMEM((L + ATT_BLOCK, GROUP_W), BF16)] * 2,
        compiler_params=_params(1),
    )(qkv, qkv, qkv)


def _pool_lane_windows(shape):
    lane = lax.broadcasted_iota(jnp.int32, shape, 1)
    return lane, jnp.where(lane < 64, 2, jnp.where(lane < 128, 4, jnp.where(lane < 192, 8, 16)))


def pool_fwd(u, wbd, b, scale):
    S = u.shape[0]

    def body(u_ref, w_ref, b_ref, s_ref, mixed_ref, out_ref):
        uv = u_ref[...]
        row = lax.broadcasted_iota(jnp.int32, uv.shape, 0)
        lane, win = _pool_lane_windows(uv.shape)

        def shift(a, k):
            return jnp.where(row >= k, pltpu.roll(a, k, 0), 0.0)

        s2 = uv + shift(uv, 1)
        s4 = s2 + shift(s2, 2)
        s8 = s4 + shift(s4, 4)
        s16 = s8 + shift(s8, 8)
        tsum = jnp.where(lane < 64, s2, jnp.where(lane < 128, s4, jnp.where(lane < 192, s8, s16)))
        cnt = jnp.minimum(row + 1, win).astype(F32)
        mb = (tsum / cnt - uv).astype(BF16)
        mixed_ref[...] = mb
        y = jnp.dot(mb, w_ref[...], preferred_element_type=F32) + b_ref[...]
        out_ref[...] = (y * s_ref[...]).astype(BF16)

    vm = pl.BlockSpec(memory_space=pltpu.VMEM)
    return pl.pallas_call(
        body,
        name="pool_fwd",
        in_specs=[vm] * 4,
        out_specs=[vm] * 2,
        out_shape=[jax.ShapeDtypeStruct((S, POOL_W), BF16)] * 2,
        compiler_params=_params(),
    )(u, wbd, b, scale)


def outproj_fwd(o_l, pool, x, w_out_g, g_post, g_pre, mod6, tm=512):
    S = x.shape[0]

    def body(o0, o1, o2, l0, l1, l2, pool_ref, x_ref, w_ref, gpost_ref, gpre_ref, mod_ref,
             cat_ref, lse_ref, lse4_ref, lse16_ref, y1_ref, x1_ref, h2_ref, so4, sl4, so16, sl16):
        for d, src, dst in ((4, o1, so4), (4, l1, sl4), (16, o2, so16), (16, l2, sl16)):
            for r in range(d):
                for h in range(2):
                    dst[h, pl.ds(r, tm // d, stride=d), :] = src[r, :, h * 128:(h + 1) * 128]
        nat = lambda ref: jnp.concatenate([ref[0], ref[1]], axis=1)
        a, b, c = l0[0], nat(sl4), nat(sl16)
        m = jnp.maximum(jnp.maximum(a, b), c)
        e0, e1, e2 = jnp.exp(a - m), jnp.exp(b - m), jnp.exp(c - m)
        z = e0 + e1 + e2
        lse = m + jnp.log(z)
        lse_ref[...] = lse
        for h in range(2):
            sl4[h] = lse[:, h * 128:(h + 1) * 128]
        for d, dst in ((4, lse4_ref), (16, lse16_ref)):
            for r in range(d):
                for h in range(2):
                    dst[r, :, h * 128:(h + 1) * 128] = sl4[h, pl.ds(r, tm // d, stride=d), :]
        attn = (e0 * o0[0] + e1 * nat(so4) + e2 * nat(so16)) / z
        cat = jnp.concatenate([pool_ref[...], attn.astype(BF16)], axis=1)
        cat_ref[...] = cat
        y1 = jnp.concatenate([jnp.dot(cat, w_ref[j], preferred_element_type=F32) for j in range(N_CHIPS)], axis=1)
        y1_ref[...] = y1
        rstd = lax.rsqrt(jnp.mean(y1 * y1, axis=-1, keepdims=True) + NORM_EPS)
        x1 = x_ref[...] + mod_ref[2:3, :] * ((y1 * rstd) * gpost_ref[...])
        x1_ref[...] = x1
        rstd2 = lax.rsqrt(jnp.mean(x1 * x1, axis=-1, keepdims=True) + NORM_EPS)
        h2 = ((x1 * rstd2) * gpre_ref[...]) * (1.0 + mod_ref[4:5, :]) + mod_ref[3:4, :]
        h2_ref[...] = h2.astype(BF16)

    t256 = _rows(tm, GROUP_W)
    cls = lambda d: pl.BlockSpec((d, tm // d, GROUP_W), lambda i: (0, i, 0))
    cls_shape = lambda d: jax.ShapeDtypeStruct((d, S // d, GROUP_W), F32)
    return pl.pallas_call(
        body,
        name="outproj_fwd",
        grid=(S // tm,),
        in_specs=[cls(d) for d in DILATIONS] * 2 + [t256, _rows(tm, D_MODEL), _full(w_out_g.shape), _full((1, D_MODEL)),
                                                    _full((1, D_MODEL)), _full((6, D_MODEL))],
        out_specs=[_rows(tm, 512), t256, cls(4), cls(16), _rows(tm, D_MODEL), _rows(tm, D_MODEL), _rows(tm, D_MODEL)],
        out_shape=[jax.ShapeDtypeStruct((S, 512), BF16), jax.ShapeDtypeStruct((S, GROUP_W), F32), cls_shape(4), cls_shape(16),
                   jax.ShapeDtypeStruct((S, D_MODEL), F32), jax.ShapeDtypeStruct((S, D_MODEL), F32),
                   jax.ShapeDtypeStruct((S, D_MODEL), BF16)],
        scratch_shapes=[pltpu.VMEM((2, tm, 128), F32)] * 4,
        compiler_params=_params(1),
    )(*o_l, pool, x, w_out_g, g_post, g_pre, mod6)


def up_fwd(h2, w_up_g, tm=512):
    S = h2.shape[0]

    def body(h_ref, w_ref, gate_ref, val_ref):
        hb = h_ref[...]
        for j in range(N_CHIPS):
            res = jnp.dot(hb, w_ref[j], preferred_element_type=F32).astype(BF16)
            dst = gate_ref if j < 2 else val_ref
            dst[:, (j % 2) * HALF_FF:(j % 2 + 1) * HALF_FF] = res

    return pl.pallas_call(
        body,
        name="up_fwd",
        grid=(S // tm,),
        in_specs=[_rows(tm, D_MODEL), _full(w_up_g.shape)],
        out_specs=[_rows(tm, D_FF)] * 2,
        out_shape=[jax.ShapeDtypeStruct((S, D_FF), BF16)] * 2,
        compiler_params=_params(1),
    )(h2, w_up_g)


def _halo_prev(tm, ncol):
    return pl.BlockSpec((16, ncol), lambda i: (jnp.maximum(i * (tm // 16) - 1, 0), 0))


def down_fwd(gate, val, conv_w, conv_b, w_down, x1, target, g_post, mod6, tm=256):
    S = x1.shape[0]

    def body(gate_ref, halo_ref, val_ref, cw_ref, cb_ref, w_ref, x1_ref, tgt_ref, g_ref, mod_ref,
             a_ref, dy2_ref, dout_ref, loss_ref, dgt_ref, dg_ref):
        first = pl.program_id(0) == 0
        y2 = jnp.zeros((tm, D_MODEL), F32)
        for ch in range(2):
            cols = slice(ch * HALF_FF, (ch + 1) * HALF_FF)
            gt = gate_ref[:, cols].astype(F32)
            g1, g2 = _conv_taps(gt, halo_ref[:, cols].astype(F32), first)
            gc = g2 * cw_ref[0:1, cols] + g1 * cw_ref[1:2, cols] + gt * cw_ref[2:3, cols] + cb_ref[:, cols]
            ge = _gelu(gc)[0]
            ab = (ge * val_ref[:, cols].astype(F32)).astype(BF16)
            a_ref[:, cols] = ab
            y2 = y2 + jnp.dot(ab, w_ref[cols, :], preferred_element_type=F32)
        rstd = lax.rsqrt(jnp.mean(y2 * y2, axis=-1, keepdims=True) + NORM_EPS)
        y2n = y2 * rstd
        gv = g_ref[...]
        gtf = mod_ref[5:6, :]
        r2 = y2n * gv
        diff = (x1_ref[...] + gtf * r2) - tgt_ref[...]
        _acc(loss_ref, jnp.zeros((1, 128), F32) + 0.5 * jnp.sum(diff * diff) * (1.0 / D_MODEL))
        dout = diff * (1.0 / D_MODEL)
        dout_ref[...] = dout
        _acc(dgt_ref, _colsum(dout * r2))
        dr2 = dout * gtf
        _acc(dg_ref, _colsum(dr2 * y2n))
        dyn = dr2 * gv
        dy2 = rstd * (dyn - y2n * jnp.mean(dyn * y2n, axis=-1, keepdims=True))
        dy2_ref[...] = dy2.astype(BF16)

    vec = _full((1, D_MODEL))
    return pl.pallas_call(
        body,
        name="down_fwd",
        grid=(S // tm,),
        in_specs=[_rows(tm, D_FF), _halo_prev(tm, D_FF), _rows(tm, D_FF), _full((3, D_FF)), _full((1, D_FF)),
                  _full((D_FF, D_MODEL)), _rows(tm, D_MODEL), _rows(tm, D_MODEL), vec, _full((6, D_MODEL))],
        out_specs=[_rows(tm, D_FF), _rows(tm, D_MODEL), _rows(tm, D_MODEL), _full((1, 128)), vec, vec],
        out_shape=[jax.ShapeDtypeStruct((S, D_FF), BF16), jax.ShapeDtypeStruct((S, D_MODEL), BF16),
                   jax.ShapeDtypeStruct((S, D_MODEL), F32), jax.ShapeDtypeStruct((1, 128), F32),
                   jax.ShapeDtypeStruct((1, D_MODEL), F32), jax.ShapeDtypeStruct((1, D_MODEL), F32)],
        compiler_params=_params(1),
    )(gate, gate, val, conv_w, conv_b, w_down, x1, target, g_post, mod6)


def down_bwd(dy2, w_down, gate, val, conv_w, conv_b, a, h2, tm=256):
    S = dy2.shape[0]

    def body(dy_ref, w_ref, gate_ref, halo_ref, val_ref, cw_ref, cb_ref, a_ref, h_ref,
             dgc_ref, dval_ref, dcw_ref, dcb_ref, dwd_ref, dwu_ref):
        first = pl.program_id(0) == 0

        @pl.when(first)
        def _():
            dcw_ref[...] = jnp.zeros_like(dcw_ref)
            dcb_ref[...] = jnp.zeros_like(dcb_ref)
            dwd_ref[...] = jnp.zeros_like(dwd_ref)
            dwu_ref[...] = jnp.zeros_like(dwu_ref)

        dyb = dy_ref[...]
        hb = h_ref[...]
        for ch in range(2):
            cols = slice(ch * HALF_FF, (ch + 1) * HALF_FF)
            da = lax.dot_general(dyb, w_ref[cols, :], NT, preferred_element_type=F32)
            gt = gate_ref[:, cols].astype(F32)
            g1, g2 = _conv_taps(gt, halo_ref[:, cols].astype(F32), first)
            gc = g2 * cw_ref[0:1, cols] + g1 * cw_ref[1:2, cols] + gt * cw_ref[2:3, cols] + cb_ref[:, cols]
            ge, u, th, z2 = _gelu(gc)
            dgc = da * val_ref[:, cols].astype(F32) * _gelu_grad(gc, u, th, z2)
            dgc_ref[:, cols] = dgc.astype(BF16)
            dvb = (da * ge).astype(BF16)
            dval_ref[:, cols] = dvb
            dcb_ref[:, cols] += _colsum(dgc)
            dcw_ref[0:1, cols] += _colsum(dgc * g2)
            dcw_ref[1:2, cols] += _colsum(dgc * g1)
            dcw_ref[2:3, cols] += _colsum(dgc * gt)
            dwd_ref[cols, :] += lax.dot_general(a_ref[:, cols], dyb, TN, preferred_element_type=F32)
            dwu_ref[ch] += lax.dot_general(hb, dvb, TN, preferred_element_type=F32)

    return pl.pallas_call(
        body,
        name="down_bwd",
        grid=(S // tm,),
        in_specs=[_rows(tm, D_MODEL), _full((D_FF, D_MODEL)), _rows(tm, D_FF), _halo_prev(tm, D_FF), _rows(tm, D_FF),
                  _full((3, D_FF)), _full((1, D_FF)), _rows(tm, D_FF), _rows(tm, D_MODEL)],
        out_specs=[_rows(tm, D_FF), _rows(tm, D_FF), _full((3, D_FF)), _full((1, D_FF)), _full((D_FF, D_MODEL)),
                   pl.BlockSpec((2, D_MODEL, HALF_FF), lambda i: (1, 0, 0), pipeline_mode=pl.Buffered(1))],
        out_shape=[jax.ShapeDtypeStruct((S, D_FF), BF16), jax.ShapeDtypeStruct((S, D_FF), BF16),
                   jax.ShapeDtypeStruct((3, D_FF), F32), jax.ShapeDtypeStruct((1, D_FF), F32),
                   jax.ShapeDtypeStruct((D_FF, D_MODEL), F32), jax.ShapeDtypeStruct((N_CHIPS, D_MODEL, HALF_FF), F32)],
        compiler_params=_params(1),
    )(dy2, w_down, gate, gate, val, conv_w, conv_b, a, h2)


def dw_matmul(a, b, out_blocks, blk_shape, a_cols, b_cols, a_blocked, name, prev=None, blk_off=0, n_blk=None, tm=512):
    S = a.shape[0]
    n_blk = out_blocks if n_blk is None else n_blk

    def body(*refs):
        a_ref, b_ref, o_ref = refs[0], refs[1], refs[-1]

        @pl.when(pl.program_id(1) == 0)
        def _():
            o_ref[...] = jnp.zeros_like(o_ref)

        o_ref[...] += lax.dot_general(a_ref[...], b_ref[...], TN, preferred_element_type=F32)

    a_spec = pl.BlockSpec((tm, a_cols), (lambda j, i: (i, j)) if a_blocked else (lambda j, i: (i, 0)))
    b_spec = pl.BlockSpec((tm, b_cols), (lambda j, i: (i, 0)) if a_blocked else (lambda j, i: (i, j)))
    in_specs = [a_spec, b_spec]
    args = [a, b]
    aliases = {}
    if prev is not None:
        in_specs.append(pl.BlockSpec(memory_space=pl.ANY))
        args.append(prev)
        aliases = {2: 0}
    return pl.pallas_call(
        body,
        name=name,
        grid=(n_blk, S // tm),
        in_specs=in_specs,
        out_specs=pl.BlockSpec((None,) + tuple(blk_shape), lambda j, i: (j + blk_off, 0, 0)),
        out_shape=jax.ShapeDtypeStruct((out_blocks,) + tuple(blk_shape), F32),
        input_output_aliases=aliases,
        compiler_params=_params(2),
    )(*args)


def up_bwd(dgc, dval, conv_w, w_up_g, x1, dout, y1, g_pre, g_post, mod6, h2, dw_up, tm=256):
    S = x1.shape[0]
    last_blk = S // 16 - 1

    def body(dgc_ref, nxt_ref, dval_ref, cw_ref, w_ref, x1_ref, dout_ref, y1_ref, gpre_ref, gpost_ref, mod_ref, h_ref, dwin_ref,
             dx1_ref, dy1_ref, dsh_ref, dsc_ref, dgpre_ref, dgt_ref, dgpost_ref, dwu_ref):
        last = pl.program_id(0) == pl.num_programs(0) - 1

        @pl.when(pl.program_id(0) == 0)
        def _():
            dwu_ref[...] = jnp.zeros_like(dwu_ref)

        hb = h_ref[...]
        dh = jnp.zeros((tm, D_MODEL), F32)
        for ch in range(2):
            cols = slice(ch * HALF_FF, (ch + 1) * HALF_FF)
            dg = dgc_ref[:, cols].astype(F32)
            nx = jnp.where(last, 0.0, nxt_ref[:, cols].astype(F32))
            row = lax.broadcasted_iota(jnp.int32, dg.shape, 0)
            n0, n1 = nx[0:1, :], nx[1:2, :]
            u1 = jnp.where(row == tm - 1, n0, pltpu.roll(dg, tm - 1, 0))
            u2 = jnp.where(row == tm - 1, n1, jnp.where(row == tm - 2, n0, pltpu.roll(dg, tm - 2, 0)))
            dgate = (dg * cw_ref[2:3, cols] + u1 * cw_ref[1:2, cols] + u2 * cw_ref[0:1, cols]).astype(BF16)
            dwu_ref[ch] += lax.dot_general(hb, dgate, TN, preferred_element_type=F32)
            dh = dh + lax.dot_general(dgate, w_ref[ch], NT, preferred_element_type=F32)
            dh = dh + lax.dot_general(dval_ref[:, cols], w_ref[2 + ch], NT, preferred_element_type=F32)
        x1 = x1_ref[...]
        rstd = lax.rsqrt(jnp.mean(x1 * x1, axis=-1, keepdims=True) + NORM_EPS)
        n2 = x1 * rstd
        gpre = gpre_ref[...]
        one_sc = 1.0 + mod_ref[4:5, :]
        _acc(dsh_ref, _colsum(dh))
        _acc(dsc_ref, _colsum(dh * (n2 * gpre)))
        _acc(dgpre_ref, _colsum(dh * one_sc * n2))
        dn = dh * (gpre * one_sc)
        dx1 = dout_ref[...] + rstd * (dn - n2 * jnp.mean(dn * n2, axis=-1, keepdims=True))
        dx1_ref[...] = dx1
        y1 = y1_ref[...]
        rstd1 = lax.rsqrt(jnp.mean(y1 * y1, axis=-1, keepdims=True) + NORM_EPS)
        y1n = y1 * rstd1
        gpost = gpost_ref[...]
        gtm = mod_ref[2:3, :]
        _acc(dgt_ref, _colsum(dx1 * (y1n * gpost)))
        dr1 = dx1 * gtm
        _acc(dgpost_ref, _colsum(dr1 * y1n))
        dyn = dr1 * gpost
        dy1 = rstd1 * (dyn - y1n * jnp.mean(dyn * y1n, axis=-1, keepdims=True))
        dy1_ref[...] = dy1.astype(BF16)

    vec = _full((1, D_MODEL))
    nxt = pl.BlockSpec((16, D_FF), lambda i: (jnp.minimum((i + 1) * (tm // 16), last_blk), 0))
    return pl.pallas_call(
        body,
        name="up_bwd",
        grid=(S // tm,),
        in_specs=[_rows(tm, D_FF), nxt, _rows(tm, D_FF), _full((3, D_FF)), _full(w_up_g.shape), _rows(tm, D_MODEL),
                  _rows(tm, D_MODEL), _rows(tm, D_MODEL), vec, vec, _full((6, D_MODEL)), _rows(tm, D_MODEL),
                  pl.BlockSpec(memory_space=pl.ANY)],
        out_specs=[_rows(tm, D_MODEL), _rows(tm, D_MODEL), vec, vec, vec, vec, vec,
                   pl.BlockSpec((2, D_MODEL, HALF_FF), lambda i: (0, 0, 0), pipeline_mode=pl.Buffered(1))],
        out_shape=[jax.ShapeDtypeStruct((S, D_MODEL), F32), jax.ShapeDtypeStruct((S, D_MODEL), BF16)]
        + [jax.ShapeDtypeStruct((1, D_MODEL), F32)] * 5 + [jax.ShapeDtypeStruct(dw_up.shape, F32)],
        input_output_aliases={12: 7},
        compiler_params=_params(1),
    )(dgc, dgc, dval, conv_w, w_up_g, x1, dout, y1, g_pre, g_post, mod6, h2, dw_up)


def outproj_bwd(dy1, w_out_g, cat, tm=512):
    S = dy1.shape[0]

    def body(dy_ref, w_ref, cat_ref, dpool_ref, dattn_ref, da4_ref, da16_ref, delta_ref, dl4_ref, dl16_ref, dw_ref, scr):
        @pl.when(pl.program_id(0) == 0)
        def _():
            dw_ref[...] = jnp.zeros_like(dw_ref)

        catb = cat_ref[...]
        dcat = jnp.zeros((tm, 512), F32)
        for j in range(N_CHIPS):
            dyj = dy_ref[:, j * 256:(j + 1) * 256]
            dcat = dcat + lax.dot_general(dyj, w_ref[j], NT, preferred_element_type=F32)
            dw_ref[j] += lax.dot_general(catb, dyj, TN, preferred_element_type=F32)
        dpool_ref[...] = dcat[:, :POOL_W]
        dattn = dcat[:, POOL_W:]
        dattn_ref[...] = dattn.astype(BF16)
        for h in range(2):
            scr[h] = dattn[:, h * 128:(h + 1) * 128]
        for d, dst in ((4, da4_ref), (16, da16_ref)):
            for r in range(d):
                for h in range(2):
                    dst[r, :, h * 128:(h + 1) * 128] = scr[h, pl.ds(r, tm // d, stride=d), :].astype(BF16)
        prod = dattn * catb[:, POOL_W:].astype(F32)
        r = lax.broadcasted_iota(jnp.int32, (GROUP_W, GROUP_W), 0) // HEAD_DIM
        c = lax.broadcasted_iota(jnp.int32, (GROUP_W, GROUP_W), 1) // HEAD_DIM
        ones_bd = jnp.where(r == c, 1.0, 0.0).astype(BF16)
        hi = prod.astype(BF16)
        lo = (prod - hi.astype(F32)).astype(BF16)
        delta = jnp.dot(hi, ones_bd, preferred_element_type=F32) + jnp.dot(lo, ones_bd, preferred_element_type=F32)
        delta_ref[...] = delta
        for h in range(2):
            scr[h] = delta[:, h * 128:(h + 1) * 128]
        for d, dst in ((4, dl4_ref), (16, dl16_ref)):
            for r in range(d):
                for h in range(2):
                    dst[r, :, h * 128:(h + 1) * 128] = scr[h, pl.ds(r, tm // d, stride=d), :]

    cls = lambda d: pl.BlockSpec((d, tm // d, GROUP_W), lambda i: (0, i, 0))
    cls_shape = lambda d, dt: jax.ShapeDtypeStruct((d, S // d, GROUP_W), dt)
    return pl.pallas_call(
        body,
        name="outproj_bwd",
        grid=(S // tm,),
        in_specs=[_rows(tm, D_MODEL), _full(w_out_g.shape), _rows(tm, 512)],
        out_specs=[_rows(tm, POOL_W), _rows(tm, GROUP_W), cls(4), cls(16), _rows(tm, GROUP_W), cls(4), cls(16),
                   _full(w_out_g.shape)],
        out_shape=[jax.ShapeDtypeStruct((S, POOL_W), F32), jax.ShapeDtypeStruct((S, GROUP_W), BF16), cls_shape(4, BF16),
                   cls_shape(16, BF16), jax.ShapeDtypeStruct((S, GROUP_W), F32), cls_shape(4, F32), cls_shape(16, F32),
                   jax.ShapeDtypeStruct(w_out_g.shape, F32)],
        scratch_shapes=[pltpu.VMEM((2, tm, 128), F32)],
        compiler_params=_params(1),
    )(dy1, w_out_g, cat)


def attn_bwd(qkv, dattn, lse, delta, d):
    L = qkv.shape[2]
    nb = L // ATT_BLOCK

    def body(q_ref, k_ref, v_ref, do_ref, l_ref, dl_ref, out_ref, kpad, vpad, dkpad, dvpad):
        kpad[0:ATT_BLOCK, :] = jnp.zeros((ATT_BLOCK, GROUP_W), BF16)
        vpad[0:ATT_BLOCK, :] = jnp.zeros((ATT_BLOCK, GROUP_W), BF16)
        kpad[ATT_BLOCK:, :] = k_ref[...]
        vpad[ATT_BLOCK:, :] = v_ref[...]
        dkpad[...] = jnp.zeros_like(dkpad)
        dvpad[...] = jnp.zeros_like(dvpad)
        band, col, lo = _attn_masks()

        def step(n, carry):
            r0 = pl.multiple_of(n * ATT_BLOCK, ATT_BLOCK)
            valid = band & ((col >= ATT_BLOCK) | (n > 0))
            qb = q_ref[pl.ds(r0, ATT_BLOCK), :]
            dob = do_ref[pl.ds(r0, ATT_BLOCK), :]
            lb = l_ref[pl.ds(r0, ATT_BLOCK), :]
            dlb = dl_ref[pl.ds(r0, ATT_BLOCK), :]
            kb = kpad[pl.ds(r0, 2 * ATT_BLOCK), :]
            vb = vpad[pl.ds(r0, 2 * ATT_BLOCK), :]
            for pair in range(2):
                lanes = slice(pair * 128, (pair + 1) * 128)
                qp, dop, kp, vp = qb[:, lanes], dob[:, lanes], kb[:, lanes], vb[:, lanes]
                dqs = []
                dk_sum = jnp.zeros((2 * ATT_BLOCK, 128), F32)
                dv_sum = jnp.zeros((2 * ATT_BLOCK, 128), F32)
                for hh in range(2):
                    sel = lo if hh == 0 else jnp.logical_not(lo)
                    qm = jnp.where(sel, qp, jnp.zeros_like(qp))
                    dom = jnp.where(sel, dop, jnp.zeros_like(dop))
                    cc = pair * 128 + hh * HEAD_DIM
                    s = lax.dot_general(qm, kp, NT, preferred_element_type=F32)
                    s = jnp.where(valid, s, NEG)
                    p = jnp.exp(s - lb[:, cc:cc + 1])
                    dp = lax.dot_general(dom, vp, NT, preferred_element_type=F32)
                    ds = (p * (dp - dlb[:, cc:cc + 1])).astype(BF16)
                    pb = p.astype(BF16)
                    dqs.append(jnp.dot(ds, kp, preferred_element_type=F32))
                    dk_sum = dk_sum + lax.dot_general(ds, qm, TN, preferred_element_type=F32)
                    dv_sum = dv_sum + lax.dot_general(pb, dom, TN, preferred_element_type=F32)
                out_ref[0, pl.ds(r0, ATT_BLOCK), lanes] = jnp.where(lo, dqs[0], dqs[1])
                dkpad[pl.ds(r0, 2 * ATT_BLOCK), lanes] += dk_sum
                dvpad[pl.ds(r0, 2 * ATT_BLOCK), lanes] += dv_sum
            return carry

        lax.fori_loop(0, nb, step, 0)
        out_ref[1] = dkpad[ATT_BLOCK:, :]
        out_ref[2] = dvpad[ATT_BLOCK:, :]

    spec = lambda kind: pl.BlockSpec((None, None, L, GROUP_W), lambda r: (kind, r, 0, 0))
    cls = pl.BlockSpec((None, L, GROUP_W), lambda r: (r, 0, 0))
    return pl.pallas_call(
        body,
        name=f"attn_bwd_d{d}",
        grid=(d,),
        in_specs=[spec(0), spec(1), spec(2), cls, cls, cls],
        out_specs=pl.BlockSpec((3, None, L, GROUP_W), lambda r: (0, r, 0, 0)),
        out_shape=jax.ShapeDtypeStruct((3, d, L, GROUP_W), F32),
        scratch_shapes=[pltpu.VMEM((L + ATT_BLOCK, GROUP_W), BF16)] * 2 + [pltpu.VMEM((L + ATT_BLOCK, GROUP_W), F32)] * 2,
        compiler_params=_params(1),
    )(qkv, qkv, qkv, dattn, lse, delta)


def pool_bwd(dpool, mixed, wbd, b, scale):
    S = dpool.shape[0]

    def body(dp_ref, mx_ref, w_ref, b_ref, s_ref, du_ref, dw_ref, db_ref, ds_ref):
        dp = dp_ref[...]
        mb = mx_ref[...]
        wv = w_ref[...]
        ypre = jnp.dot(mb, wv, preferred_element_type=F32) + b_ref[...]
        ds_ref[...] = _colsum(dp * ypre)
        dpre = dp * s_ref[...]
        db_ref[...] = _colsum(dpre)
        dpb = dpre.astype(BF16)
        dw_ref[...] = lax.dot_general(mb, dpb, TN, preferred_element_type=F32)
        dmix = lax.dot_general(dpb, wv, NT, preferred_element_type=F32)
        row = lax.broadcasted_iota(jnp.int32, dmix.shape, 0)
        lane, win = _pool_lane_windows(dmix.shape)
        e = dmix / jnp.minimum(row + 1, win).astype(F32)

        def shift(a, k):
            return jnp.where(row < S - k, pltpu.roll(a, S - k, 0), 0.0)

        f2 = e + shift(e, 1)
        f4 = f2 + shift(f2, 2)
        f8 = f4 + shift(f4, 4)
        f16 = f8 + shift(f8, 8)
        du_ref[...] = jnp.where(lane < 64, f2, jnp.where(lane < 128, f4, jnp.where(lane < 192, f8, f16))) - dmix

    vm = pl.BlockSpec(memory_space=pltpu.VMEM)
    return pl.pallas_call(
        body,
        name="pool_bwd",
        in_specs=[vm] * 5,
        out_specs=[vm] * 4,
        out_shape=[jax.ShapeDtypeStruct((S, POOL_W), F32), jax.ShapeDtypeStruct((POOL_W, POOL_W), F32),
                   jax.ShapeDtypeStruct((1, POOL_W), F32), jax.ShapeDtypeStruct((1, POOL_W), F32)],
        compiler_params=_params(),
    )(dpool, mixed, wbd, b, scale)


def inproj_bwd(dqkv, du, x, dx1, w_in_g, g, mod6, tc, tsa, tsb, h1, tm=256):
    S = x.shape[0]

    def body(d0, d1, d2, du_ref, x_ref, dx1_ref, w_ref, g_ref, mod_ref, tc_ref, tsa_ref, tsb_ref, h_ref,
             gx_ref, dsh_ref, dsc_ref, dg_ref, dw_ref, s4, s16, dp_ref):
        @pl.when(pl.program_id(0) == 0)
        def _():
            dw_ref[...] = jnp.zeros_like(dw_ref)

        cs, sa, sb = tc_ref[...], tsa_ref[...], tsb_ref[...]
        for d, src, dst in ((4, d1, s4), (16, d2, s16)):
            for kind in range(3):
                for r in range(d):
                    for h in range(2):
                        dst[kind, h, pl.ds(r, tm // d, stride=d), :] = src[kind, r, :, h * 128:(h + 1) * 128]
        for sp in range(20):
            piece, half = sp // 2, sp % 2
            lanes = slice(half * 128, (half + 1) * 128)
            if piece == 0:
                blk = du_ref[:, lanes]
            else:
                kind, gi = (piece - 1) // 3, (piece - 1) % 3
                blk = d0[kind, 0, :, lanes] if gi == 0 else (s4, s16)[gi - 1][kind, half]
                if kind == 0:
                    blk = _rope128(blk, cs, sa, sb, -1.0) * (HEAD_DIM ** -0.5)
                elif kind == 1:
                    blk = _rope128(blk, cs, sa, sb, -1.0)
            dp_ref[:, sp * 128:(sp + 1) * 128] = blk.astype(BF16)
        dh = jnp.zeros((tm, D_MODEL), F32)
        hb = h_ref[...]
        for j in range(N_CHIPS):
            dpj = dp_ref[:, j * 640:(j + 1) * 640]
            dh = dh + lax.dot_general(dpj, w_ref[j], NT, preferred_element_type=F32)
            dw_ref[j] += lax.dot_general(hb, dpj, TN, preferred_element_type=F32)
        xv = x_ref[...]
        rstd = lax.rsqrt(jnp.mean(xv * xv, axis=-1, keepdims=True) + NORM_EPS)
        n1 = xv * rstd
        gv = g_ref[...]
        one_sc = 1.0 + mod_ref[1:2, :]
        _acc(dsh_ref, _colsum(dh))
        _acc(dsc_ref, _colsum(dh * (n1 * gv)))
        _acc(dg_ref, _colsum(dh * one_sc * n1))
        dn = dh * (gv * one_sc)
        gx_ref[...] = dx1_ref[...] + rstd * (dn - n1 * jnp.mean(dn * n1, axis=-1, keepdims=True))

    vec = _full((1, D_MODEL))
    dspec = lambda d: pl.BlockSpec((3, d, tm // d, GROUP_W), lambda i: (0, 0, i, 0))
    return pl.pallas_call(
        body,
        name="inproj_bwd",
        grid=(S // tm,),
        in_specs=[dspec(d) for d in DILATIONS] + [_rows(tm, POOL_W), _rows(tm, D_MODEL), _rows(tm, D_MODEL), _full(w_in_g.shape),
                                                  vec, _full((6, D_MODEL)), _rows(tm, 128), _rows(tm, 128), _rows(tm, 128),
                                                  _rows(tm, D_MODEL)],
        out_specs=[_rows(tm, D_MODEL), vec, vec, vec, _full(w_in_g.shape)],
        out_shape=[jax.ShapeDtypeStruct((S, D_MODEL), F32)] + [jax.ShapeDtypeStruct((1, D_MODEL), F32)] * 3
        + [jax.ShapeDtypeStruct(w_in_g.shape, F32)],
        scratch_shapes=[pltpu.VMEM((3, 2, tm, 128), F32)] * 2 + [pltpu.VMEM((tm, IN_W), BF16)],
        compiler_params=_params(1),
    )(*dqkv, du, x, dx1, w_in_g, g, mod6, tc, tsa, tsb, h1)


def _adamw(w, g, m, v):
    m = ADAM_B1 * m + (1.0 - ADAM_B1) * g
    v = ADAM_B2 * v + (1.0 - ADAM_B2) * (g * g)
    m_hat = m / (1.0 - ADAM_B1 ** ADAM_STEP)
    v_hat = v / (1.0 - ADAM_B2 ** ADAM_STEP)
    delta = -ADAM_LR * (m_hat / (jnp.sqrt(v_hat) + ADAM_EPS) + ADAM_WD * w)
    return delta, m, v


def adamw_rows(w, g, m, v, tr, name):
    R, C = w.shape

    def body(w_ref, g_ref, m_ref, v_ref, d_ref, mo_ref, vo_ref):
        d_ref[...], mo_ref[...], vo_ref[...] = _adamw(w_ref[...], g_ref[...], m_ref[...], v_ref[...])

    spec = pl.BlockSpec((tr, C), lambda i: (i, 0))
    return pl.pallas_call(
        body,
        name=name,
        grid=(R // tr,),
        in_specs=[spec] * 4,
        out_specs=[spec] * 3,
        out_shape=[jax.ShapeDtypeStruct((R, C), F32)] * 3,
        compiler_params=_params(1),
    )(w, g, m, v)


def adamw_ada(c_all_t, dmod_cols, w, m, v, tr=256):
    R, C = w.shape

    def body(ct_ref, dm_ref, w_ref, m_ref, v_ref, g_ref, d_ref, mo_ref, vo_ref):
        ct = ct_ref[...]
        act = ct * jax.nn.sigmoid(ct)
        g = jnp.zeros((tr, C), F32)
        for b in range(N_DEV):
            g = g + act[:, b:b + 1] * dm_ref[b:b + 1, :]
        g_ref[...] = g
        d_ref[...], mo_ref[...], vo_ref[...] = _adamw(w_ref[...], g, m_ref[...], v_ref[...])

    spec = pl.BlockSpec((tr, C), lambda i: (i, 0))
    return pl.pallas_call(
        body,
        name="adamw_ada",
        grid=(R // tr,),
        in_specs=[pl.BlockSpec((tr, N_DEV), lambda i: (i, 0)), _full((N_DEV, C)), spec, spec, spec],
        out_specs=[spec] * 4,
        out_shape=[jax.ShapeDtypeStruct((R, C), F32)] * 4,
        compiler_params=_params(1),
    )(c_all_t, dmod_cols, w, m, v)


def adamw_small(slab_a, slab_b, convw_g, wpool_g, params):
    names = ["b_ada", "g_pre_mix", "g_post_mix", "g_pre_ffn", "g_post_ffn", "b_pool", "pool_scale", "conv_b", "conv_w", "w_pool"]
    flat = []
    for n in names:
        flat += list(params[n])

    def body(a_ref, b_ref, cw_ref, wp_ref, *rest):
        ins, outs = rest[:30], rest[30:]

        def dev_sum(ref):
            t = ref[0]
            for dev in range(1, N_DEV):
                t = t + ref[dev]
            return t

        sa, sb_, scw, swp = dev_sum(a_ref), dev_sum(b_ref), dev_sum(cw_ref), dev_sum(wp_ref)
        grads = [
            jnp.concatenate([sa[k:k + 1, :] for k in range(6)], axis=1),
            sa[6:7, :], sa[7:8, :], sa[8:9, :], sa[9:10, :],
            sa[10:11, 0:256], sa[10:11, 256:512],
            sb_[3:4, :], scw, swp,
        ]
        for i, g in enumerate(grads):
            w_ref, m_ref, v_ref = ins[3 * i:3 * i + 3]
            d, mo, vo = _adamw(w_ref[...], g, m_ref[...], v_ref[...])
            outs[4 * i][...] = g
            outs[4 * i + 1][...] = d
            outs[4 * i + 2][...] = mo
            outs[4 * i + 3][...] = vo

    vm = pl.BlockSpec(memory_space=pltpu.VMEM)
    out_shape = []
    for n in names:
        out_shape += [jax.ShapeDtypeStruct(params[n][0].shape, F32)] * 4
    outs = pl.pallas_call(
        body,
        name="adamw_small",
        in_specs=[vm] * (4 + len(flat)),
        out_specs=[vm] * len(out_shape),
        out_shape=out_shape,
        compiler_params=_params(),
    )(slab_a, slab_b, convw_g, wpool_g, *flat)
    return {n: outs[4 * i:4 * i + 4] for i, n in enumerate(names)}


def _place():
    return lax.axis_index("x"), lax.axis_index("y"), lax.axis_index("c")


def _other_chips(x, y):
    return [(1 - x, y), (x, 1 - y), (1 - x, 1 - y)]


def _chip_id(cx, cy):
    return 2 * cx + cy


def gather_weights(shards):
    n = len(shards)
    halved = [s.shape[0] % 32 == 0 for s in shards]

    def body(*refs):
        ins, outs = refs[:n], refs[n:2 * n]
        send_sems, recv_sems, loc_sems = refs[2 * n:]
        x, y, c = _place()
        me = _chip_id(x, y)
        chips = _other_chips(x, y)
        sib = (x, y, 1 - c)

        def part(w, chip, half):
            if not halved[w]:
                return outs[w].at[chip]
            rh = shards[w].shape[0] // 2
            return outs[w].at[chip, pl.ds(half * rh, rh), :]

        def src_part(w):
            if not halved[w]:
                return ins[w]
            rh = shards[w].shape[0] // 2
            return ins[w].at[pl.ds(c * rh, rh), :]

        def rcopy(w, k, src, dst, to):
            return pltpu.make_async_remote_copy(src_ref=src, dst_ref=dst, send_sem=send_sems.at[6 * w + k],
                                                recv_sem=recv_sems.at[6 * w + k], device_id=to, device_id_type=MESH)

        local = [pltpu.make_async_copy(ins[w], outs[w].at[me], loc_sems.at[w]) for w in range(n)]
        for cp in local:
            cp.start()
        first = []
        for w in range(n):
            for k, (cx, cy) in enumerate(chips):
                cp = rcopy(w, k, src_part(w), part(w, me, c), (cx, cy, c))
                cp.start()
                first.append(cp)
        passed = []
        for w in range(n):
            for k, (cx, cy) in enumerate(chips):
                blk = part(w, _chip_id(cx, cy), c)
                rcopy(w, k, blk, blk, (cx, cy, c)).wait_recv()
                if halved[w]:
                    cp = rcopy(w, 3 + k, blk, blk, sib)
                    cp.start()
                    passed.append(cp)
        for w in range(n):
            if halved[w]:
                for k, (cx, cy) in enumerate(chips):
                    blk = part(w, _chip_id(cx, cy), 1 - c)
                    rcopy(w, 3 + k, blk, blk, sib).wait_recv()
        for cp in first + passed:
            cp.wait_send()
        for cp in local:
            cp.wait()

    hbm = pl.BlockSpec(memory_space=pl.ANY)
    return pl.pallas_call(
        body,
        name="gather_weights",
        in_specs=[hbm] * n,
        out_specs=[hbm] * n,
        out_shape=[jax.ShapeDtypeStruct((N_CHIPS,) + s.shape, s.dtype) for s in shards],
        scratch_shapes=[pltpu.SemaphoreType.DMA((6 * n,)), pltpu.SemaphoreType.DMA((6 * n,)), pltpu.SemaphoreType.DMA((n,))],
        compiler_params=pltpu.CompilerParams(has_side_effects=True, vmem_limit_bytes=VMEM_LIMIT),
    )(*shards)


HBM_SPEC = pl.BlockSpec(memory_space=pltpu.HBM)
SEM_SPEC = pl.BlockSpec(memory_space=pltpu.SEMAPHORE)
ANY_SPEC = pl.BlockSpec(memory_space=pl.ANY)
EFFECT = pltpu.SideEffectType.DATAFLOW_SIDE_EFFECTING


def _hbm(t):
    return pltpu.with_memory_space_constraint(t, pltpu.HBM)


def _hbm_shapes(ts):
    return [pltpu.HBM(t.shape, t.dtype) for t in ts]


def _half_rows(ref, lead, half, rh):
    return ref.at[lead, pl.ds(half * rh, rh), :]


def gather_split_start(shards, lands, carry, k):
    n = len(shards)

    def body(*refs):
        ins, land = refs[:n], refs[n:2 * n]
        send_sems, recv_sems = refs[2 * n + 1], refs[2 * n + 2]
        loc_sems = refs[-1]
        x, y, c = _place()
        me = _chip_id(x, y)
        if k == 0:
            local = [pltpu.make_async_copy(ins[w], land[w].at[me], loc_sems.at[w]) for w in range(n)]
            for cp in local:
                cp.start()
            for cp in local:
                cp.wait()
        cx, cy = _other_chips(x, y)[k]
        for w in range(n):
            rh = shards[w].shape[0] // 2
            pltpu.make_async_remote_copy(src_ref=ins[w].at[pl.ds(c * rh, rh), :], dst_ref=_half_rows(land[w], me, c, rh),
                                         send_sem=send_sems.at[w], recv_sem=recv_sems.at[w],
                                         device_id=(cx, cy, c), device_id_type=MESH).start()

    args = [_hbm(s) for s in shards] + [_hbm(l) for l in lands] + [_hbm(carry)]
    outs = pl.pallas_call(
        body,
        name="gather_split_start%d" % k,
        out_shape=[pltpu.SemaphoreType.DMA((n,)), pltpu.SemaphoreType.DMA((n,))] + _hbm_shapes(shards) + _hbm_shapes(lands)
        + _hbm_shapes([carry]),
        in_specs=[HBM_SPEC] * (2 * n + 1),
        out_specs=[SEM_SPEC, SEM_SPEC] + [HBM_SPEC] * (2 * n + 1),
        input_output_aliases={i: 2 + i for i in range(2 * n + 1)},
        scratch_shapes=[pltpu.SemaphoreType.DMA((n,))],
        compiler_params=pltpu.CompilerParams(has_side_effects=EFFECT),
    )(*args)
    return outs[0], outs[1], list(outs[2:2 + n]), list(outs[2 + n:2 + 2 * n]), outs[-1]


def gather_split_mid(sems, shards, lands, after):
    n = len(shards)

    def body(*refs):
        ins, land = refs[:n], refs[n:2 * n]
        sem_in = refs[2 * n:2 * n + 6]
        fsend, frecv = refs[2 * n + 7], refs[2 * n + 8]
        x, y, c = _place()
        me = _chip_id(x, y)
        chips = _other_chips(x, y)
        for w in range(n):
            rh = shards[w].shape[0] // 2
            for k, (cx, cy) in enumerate(chips):
                got = _half_rows(land[w], _chip_id(cx, cy), c, rh)
                cp = pltpu.make_async_remote_copy(src_ref=ins[w].at[pl.ds(c * rh, rh), :], dst_ref=got, send_sem=sem_in[2 * k].at[w],
                                                  recv_sem=sem_in[2 * k + 1].at[w], device_id=(cx, cy, c), device_id_type=MESH)
                cp.wait_send()
                cp.wait_recv()
        for w in range(n):
            rh = shards[w].shape[0] // 2
            for k, (cx, cy) in enumerate(chips):
                got = _half_rows(land[w], _chip_id(cx, cy), c, rh)
                pltpu.make_async_remote_copy(src_ref=got, dst_ref=got, send_sem=fsend.at[3 * w + k], recv_sem=frecv.at[3 * w + k],
                                             device_id=(x, y, 1 - c), device_id_type=MESH).start()

    outs = pl.pallas_call(
        body,
        name="gather_split_mid",
        out_shape=[pltpu.SemaphoreType.DMA((3 * n,)), pltpu.SemaphoreType.DMA((3 * n,))] + _hbm_shapes(lands),
        in_specs=[HBM_SPEC] * (2 * n) + [SEM_SPEC] * 6 + [ANY_SPEC],
        out_specs=[SEM_SPEC, SEM_SPEC] + [HBM_SPEC] * n,
        input_output_aliases={n + i: 2 + i for i in range(n)},
        compiler_params=pltpu.CompilerParams(has_side_effects=EFFECT),
    )(*shards, *lands, *sems, after)
    return outs[0], outs[1], list(outs[2:])


def gather_split_done(fsend, frecv, lands, after):
    n = len(lands)

    def body(*refs):
        land = refs[:n]
        ssem, rsem = refs[n], refs[n + 1]
        x, y, c = _place()
        for w in range(n):
            rh = lands[w].shape[1] // 2
            for k, (cx, cy) in enumerate(_other_chips(x, y)):
                sent = _half_rows(land[w], _chip_id(cx, cy), c, rh)
                got = _half_rows(land[w], _chip_id(cx, cy), 1 - c, rh)
                cp = pltpu.make_async_remote_copy(src_ref=sent, dst_ref=got, send_sem=ssem.at[3 * w + k], recv_sem=rsem.at[3 * w + k],
                                                  device_id=(x, y, 1 - c), device_id_type=MESH)
                cp.wait_send()
                cp.wait_recv()

    outs = pl.pallas_call(
        body,
        name="gather_split_done",
        out_shape=_hbm_shapes(lands),
        in_specs=[HBM_SPEC] * n + [SEM_SPEC, SEM_SPEC, ANY_SPEC],
        out_specs=[HBM_SPEC] * n,
        input_output_aliases={i: i for i in range(n)},
        compiler_params=pltpu.CompilerParams(has_side_effects=EFFECT),
    )(*lands, fsend, frecv, after)
    return list(outs)


def _flips():
    return [(fx, fy, fc) for fx in (0, 1) for fy in (0, 1) for fc in (0, 1)][1:]


def _flip(v, f):
    return v if f == 0 else 1 - v


def ada_mod(c3, w_ada, b_cols, conv_w):
    CB = w_ada.shape[1]

    def body(c_ref, w_ref, b_ref, cw_ref, call_ref, mod_ref, cwall_ref, modall, send_sems, recv_sems):
        x, y, c = _place()
        me_dev = 4 * x + 2 * y + c
        me = _chip_id(x, y)
        call_ref[me_dev] = c_ref[0]
        cwall_ref[me] = cw_ref[...]
        sends = []
        for k, (cx, cy) in enumerate(_other_chips(x, y)):
            cp = pltpu.make_async_remote_copy(src_ref=cw_ref, dst_ref=cwall_ref.at[me], send_sem=send_sems.at[10 + k],
                                              recv_sem=recv_sems.at[10 + k], device_id=(cx, cy, c), device_id_type=MESH)
            cp.start()
            sends.append(cp)
        for k, (fx, fy, fc) in enumerate(_flips()):
            cp = pltpu.make_async_remote_copy(src_ref=c_ref.at[0], dst_ref=call_ref.at[me_dev], send_sem=send_sems.at[k],
                                              recv_sem=recv_sems.at[k],
                                              device_id=(_flip(x, fx), _flip(y, fy), _flip(c, fc)), device_id_type=MESH)
            cp.start()
            sends.append(cp)
        for k, (fx, fy, fc) in enumerate(_flips()):
            peer = 4 * _flip(x, fx) + 2 * _flip(y, fy) + _flip(c, fc)
            pltpu.make_async_remote_copy(src_ref=c_ref.at[0], dst_ref=call_ref.at[peer], send_sem=send_sems.at[k],
                                         recv_sem=recv_sems.at[k], device_id=(x, y, c), device_id_type=MESH).wait_recv()
        row = lax.broadcasted_iota(jnp.int32, (N_DEV, D_MODEL), 0)
        call = jnp.zeros((N_DEV, D_MODEL), F32)
        for dev in range(N_DEV):
            call = jnp.where(row == dev, call_ref[dev], call)
        act = call * jax.nn.sigmoid(call)
        modall[me] = jnp.dot(act, w_ref[...], preferred_element_type=F32, precision=lax.Precision.HIGHEST) + b_ref[...]
        for k, (cx, cy) in enumerate(_other_chips(x, y)):
            cp = pltpu.make_async_remote_copy(src_ref=modall.at[me], dst_ref=modall.at[me], send_sem=send_sems.at[7 + k],
                                              recv_sem=recv_sems.at[7 + k], device_id=(cx, cy, c), device_id_type=MESH)
            cp.start()
            sends.append(cp)
        for k, (cx, cy) in enumerate(_other_chips(x, y)):
            blk = modall.at[_chip_id(cx, cy)]
            pltpu.make_async_remote_copy(src_ref=blk, dst_ref=blk, send_sem=send_sems.at[7 + k], recv_sem=recv_sems.at[7 + k],
                                         device_id=(x, y, c), device_id_type=MESH).wait_recv()
        for k, (cx, cy) in enumerate(_other_chips(x, y)):
            blk = cwall_ref.at[_chip_id(cx, cy)]
            pltpu.make_async_remote_copy(src_ref=blk, dst_ref=blk, send_sem=send_sems.at[10 + k], recv_sem=recv_sems.at[10 + k],
                                         device_id=(x, y, c), device_id_type=MESH).wait_recv()
        for cp in sends:
            cp.wait_send()
        mine = [modall[j, pl.ds(me_dev, 1), :] for j in range(N_CHIPS)]
        for r in range(6):
            pieces = []
            for h in range(2):
                pos = r * D_MODEL + h * 512
                pieces.append(mine[pos // CB][:, pos % CB:pos % CB + 512])
            mod_ref[r:r + 1, :] = jnp.concatenate(pieces, axis=1)

    vm = pl.BlockSpec(memory_space=pltpu.VMEM)
    return pl.pallas_call(
        body,
        name="ada_mod",
        in_specs=[vm] * 4,
        out_specs=[vm] * 3,
        out_shape=[jax.ShapeDtypeStruct((N_DEV, 1, D_MODEL), F32), jax.ShapeDtypeStruct((6, D_MODEL), F32),
                   jax.ShapeDtypeStruct((N_CHIPS,) + conv_w.shape, F32)],
        scratch_shapes=[pltpu.VMEM((N_CHIPS, N_DEV, CB), F32), pltpu.SemaphoreType.DMA((13,)), pltpu.SemaphoreType.DMA((13,))],
        compiler_params=pltpu.CompilerParams(has_side_effects=True, vmem_limit_bytes=VMEM_LIMIT),
    )(c3, w_ada, b_cols, conv_w)


def gather_small(blocks):
    n = len(blocks)

    def body(*refs):
        ins, outs = refs[:n], refs[n:2 * n]
        send_sems, recv_sems = refs[2 * n:]
        x, y, c = _place()
        sib = (x, y, 1 - c)
        chips = _other_chips(x, y)

        def dev(px, py, pc):
            return 4 * px + 2 * py + pc

        def cp(w, k, src, block_dev, to):
            return pltpu.make_async_remote_copy(src_ref=src, dst_ref=outs[w].at[block_dev], send_sem=send_sems.at[7 * w + k],
                                                recv_sem=recv_sems.at[7 * w + k], device_id=to, device_id_type=MESH)

        me = dev(x, y, c)
        started = []
        for w in range(n):
            outs[w][me] = ins[w][...]
            t = cp(w, 0, ins[w], me, sib)
            t.start()
            started.append(t)
            for k, (cx, cy) in enumerate(chips):
                t = cp(w, 1 + k, ins[w], me, (cx, cy, c))
                t.start()
                started.append(t)
        for w in range(n):
            for k, (cx, cy) in enumerate(chips):
                b = dev(cx, cy, c)
                cp(w, 1 + k, outs[w].at[b], b, (x, y, c)).wait_recv()
                t = cp(w, 4 + k, outs[w].at[b], b, sib)
                t.start()
                started.append(t)
        for w in range(n):
            b = dev(x, y, 1 - c)
            cp(w, 0, outs[w].at[b], b, (x, y, c)).wait_recv()
            for k, (cx, cy) in enumerate(chips):
                b = dev(cx, cy, 1 - c)
                cp(w, 4 + k, outs[w].at[b], b, (x, y, c)).wait_recv()
        for t in started:
            t.wait_send()

    vm = pl.BlockSpec(memory_space=pltpu.VMEM)
    return pl.pallas_call(
        body,
        name="gather_small",
        in_specs=[vm] * n,
        out_specs=[vm] * n,
        out_shape=[jax.ShapeDtypeStruct((N_DEV,) + b.shape, b.dtype) for b in blocks],
        scratch_shapes=[pltpu.SemaphoreType.DMA((7 * n,)), pltpu.SemaphoreType.DMA((7 * n,))],
        compiler_params=pltpu.CompilerParams(has_side_effects=True, vmem_limit_bytes=VMEM_LIMIT),
    )(*blocks)


def reduce_scatter_grads(grads, chunk_rows):
    n = len(grads)
    shapes = [g.shape[1:] for g in grads]
    halves = [s[0] // 2 for s in shapes]

    def body(*refs):
        gin = refs[:n]
        gout = refs[n:2 * n]
        sibbuf = refs[2 * n:3 * n]
        rest = refs[3 * n:]
        rbuf = rest[:n]
        pown = rest[n:2 * n]
        stage_a, stage_b, stage_o, stage_f = rest[2 * n:2 * n + 4]
        sib_send, sib_recv, ici_send, ici_recv, fin_send, fin_recv, ld_sems, st_sems = rest[2 * n + 4:]
        x, y, c = _place()
        me = _chip_id(x, y)
        chips = _other_chips(x, y)
        sib = (x, y, 1 - c)

        to_sib = []
        for w in range(n):
            rh = halves[w]
            cp = pltpu.make_async_remote_copy(src_ref=gin[w].at[:, pl.ds((1 - c) * rh, rh), :], dst_ref=sibbuf[w],
                                              send_sem=sib_send.at[w], recv_sem=sib_recv.at[w], device_id=sib,
                                              device_id_type=MESH)
            cp.start()
            to_sib.append(cp)

        sent = []
        for w in range(n):
            rh, cw = halves[w], shapes[w][1]
            ch = chunk_rows[w]
            to_sib[w].wait_recv()
            for k in range(4):
                chip = me if k == 3 else _chip_id(*chips[k])
                for r0 in range(0, rh, ch):
                    la = pltpu.make_async_copy(gin[w].at[chip, pl.ds(c * rh + r0, ch), :], stage_a.at[0:ch, 0:cw], ld_sems.at[0])
                    lb = pltpu.make_async_copy(sibbuf[w].at[chip, pl.ds(r0, ch), :], stage_b.at[0:ch, 0:cw], ld_sems.at[1])
                    la.start()
                    lb.start()
                    la.wait()
                    lb.wait()
                    tot = stage_a[0:ch, 0:cw] + stage_b[0:ch, 0:cw]
                    if k == 3:
                        pown[w][r0:r0 + ch, :] = tot
                    else:
                        stage_o[0:ch, 0:cw] = tot.astype(BF16)
                        cx, cy = chips[k]
                        cp = pltpu.make_async_remote_copy(src_ref=stage_o.at[0:ch, 0:cw], dst_ref=rbuf[w].at[k, r0:r0 + ch, :],
                                                          send_sem=ici_send.at[3 * w + k], recv_sem=ici_recv.at[3 * w + k],
                                                          device_id=(cx, cy, c), device_id_type=MESH)
                        cp.start()
                        cp.wait_send()
            sent.append(w)

        fin = []
        for w in range(n):
            rh, cw = halves[w], shapes[w][1]
            for k in range(3):
                whole = rbuf[w].at[k]
                pltpu.make_async_remote_copy(src_ref=whole, dst_ref=whole, send_sem=ici_send.at[3 * w + k],
                                             recv_sem=ici_recv.at[3 * w + k], device_id=(x, y, c),
                                             device_id_type=MESH).wait_recv()
            pown[w][...] = ((pown[w][...] + rbuf[w][0].astype(F32)) + rbuf[w][1].astype(F32)) + rbuf[w][2].astype(F32)
            mine = gout[w].at[pl.ds(c * rh, rh), :]
            st = pltpu.make_async_copy(pown[w], mine, st_sems.at[w])
            st.start()
            cp = pltpu.make_async_remote_copy(src_ref=pown[w], dst_ref=mine, send_sem=fin_send.at[w], recv_sem=fin_recv.at[w],
                                              device_id=sib, device_id_type=MESH)
            cp.start()
            fin.append((st, cp))
        for w in range(n):
            rh = halves[w]
            theirs = gout[w].at[pl.ds((1 - c) * rh, rh), :]
            pltpu.make_async_remote_copy(src_ref=theirs, dst_ref=theirs, send_sem=fin_send.at[w], recv_sem=fin_recv.at[w],
                                         device_id=(x, y, c), device_id_type=MESH).wait_recv()
        for cp in to_sib:
            cp.wait_send()
        for st, cp in fin:
            st.wait()
            cp.wait_send()

    hbm = pl.BlockSpec(memory_space=pl.ANY)
    max_ch = max(chunk_rows)
    max_c = max(s[1] for s in shapes)
    outs = pl.pallas_call(
        body,
        name="reduce_scatter_grads",
        in_specs=[hbm] * n,
        out_specs=[hbm] * (2 * n),
        out_shape=[jax.ShapeDtypeStruct(s, F32) for s in shapes]
        + [jax.ShapeDtypeStruct((N_CHIPS, h, s[1]), F32) for h, s in zip(halves, shapes)],
        scratch_shapes=[pltpu.VMEM((3, h, s[1]), BF16) for h, s in zip(halves, shapes)]
        + [pltpu.VMEM((h, s[1]), F32) for h, s in zip(halves, shapes)]
        + [pltpu.VMEM((max_ch, max_c), F32), pltpu.VMEM((max_ch, max_c), F32), pltpu.VMEM((max_ch, max_c), BF16),
           pltpu.VMEM((8, 128), F32)]
        + [pltpu.SemaphoreType.DMA((n,)), pltpu.SemaphoreType.DMA((n,)), pltpu.SemaphoreType.DMA((3 * n,)),
           pltpu.SemaphoreType.DMA((3 * n,)), pltpu.SemaphoreType.DMA((n,)), pltpu.SemaphoreType.DMA((n,)),
           pltpu.SemaphoreType.DMA((2,)), pltpu.SemaphoreType.DMA((n,))],
        compiler_params=pltpu.CompilerParams(has_side_effects=True, vmem_limit_bytes=VMEM_LIMIT),
    )(*grads)
    return outs[:n]


def split_start(name, bufs, plan, n_sem, carry):
    nb = len(bufs)

    def body(*refs):
        x, y, c = _place()
        ssem, rsem = refs[nb + 1], refs[nb + 2]
        for i, (src, dst, dev) in enumerate(plan(refs[:nb], x, y, c)):
            pltpu.make_async_remote_copy(src_ref=src, dst_ref=dst, send_sem=ssem.at[i], recv_sem=rsem.at[i], device_id=dev,
                                         device_id_type=MESH).start()

    alls = list(bufs) + [carry]
    outs = pl.pallas_call(
        body,
        name=name,
        out_shape=[pltpu.SemaphoreType.DMA((n_sem,)), pltpu.SemaphoreType.DMA((n_sem,))] + _hbm_shapes(alls),
        in_specs=[HBM_SPEC] * (nb + 1),
        out_specs=[SEM_SPEC, SEM_SPEC] + [HBM_SPEC] * (nb + 1),
        input_output_aliases={i: 2 + i for i in range(nb + 1)},
        compiler_params=pltpu.CompilerParams(has_side_effects=EFFECT),
    )(*[_hbm(t) for t in alls])
    return outs[0], outs[1], list(outs[2:2 + nb]), outs[-1]


def split_wait(name, ssem, rsem, bufs, plan, after):
    nb = len(bufs)

    def body(*refs):
        x, y, c = _place()
        s_ref, r_ref = refs[nb], refs[nb + 1]
        for i, (src, dst, dev) in enumerate(plan(refs[:nb], x, y, c)):
            cp = pltpu.make_async_remote_copy(src_ref=src, dst_ref=dst, send_sem=s_ref.at[i], recv_sem=r_ref.at[i], device_id=dev,
                                              device_id_type=MESH)
            cp.wait_send()
            cp.wait_recv()

    outs = pl.pallas_call(
        body,
        name=name,
        out_shape=_hbm_shapes(bufs),
        in_specs=[HBM_SPEC] * nb + [SEM_SPEC, SEM_SPEC, ANY_SPEC],
        out_specs=[HBM_SPEC] * nb,
        input_output_aliases={i: i for i in range(nb)},
        compiler_params=pltpu.CompilerParams(has_side_effects=EFFECT),
    )(*bufs, ssem, rsem, after)
    return list(outs)


def _gather_ici_plan(n):
    def plan(refs, x, y, c):
        out = []
        for w in range(n):
            rh = refs[w].shape[0] // 2
            for cx, cy in _other_chips(x, y):
                out.append((refs[w].at[pl.ds(c * rh, rh), :], _half_rows(refs[n + w], _chip_id(x, y), c, rh), (cx, cy, c)))
        return out

    return plan


def _gather_d2d_plan(n):
    def plan(refs, x, y, c):
        out = []
        for w in range(n):
            rh = refs[w].shape[1] // 2
            for cx, cy in _other_chips(x, y):
                blk = _half_rows(refs[w], _chip_id(cx, cy), c, rh)
                out.append((blk, blk, (x, y, 1 - c)))
        return out

    return plan


def _rs_d2d_plan(n):
    def plan(refs, x, y, c):
        out = []
        for w in range(n):
            rh = refs[w].shape[1] // 2
            out.append((refs[w].at[:, pl.ds((1 - c) * rh, rh), :], refs[n + w], (x, y, 1 - c)))
        return out

    return plan


def _rs_ici_plan(n):
    def plan(refs, x, y, c):
        out = []
        for w in range(n):
            for k, (cx, cy) in enumerate(_other_chips(x, y)):
                out.append((refs[w].at[_chip_id(cx, cy)], refs[n + w].at[k], (cx, cy, c)))
        return out

    return plan


def _rs_share_plan(n):
    def plan(refs, x, y, c):
        out = []
        for w in range(n):
            rh = refs[w].shape[0] // 2
            rows = refs[w].at[pl.ds(c * rh, rh), :]
            out.append((rows, rows, (x, y, 1 - c)))
        return out

    return plan


def rs_add(grad, sibbuf, place, tr, name):
    _, R, C = grad.shape
    nt = (R // 2) // tr

    def body(p_ref, g_ref, s_ref, o_ref):
        o_ref[...] = (g_ref[...] + s_ref[...]).astype(BF16)

    return pl.pallas_call(
        body,
        name=name,
        grid_spec=pltpu.PrefetchScalarGridSpec(
            num_scalar_prefetch=1,
            grid=(N_CHIPS, nt),
            in_specs=[pl.BlockSpec((None, tr, C), lambda j, i, p: (j, p[0] * nt + i, 0)),
                      pl.BlockSpec((None, tr, C), lambda j, i, p: (j, i, 0))],
            out_specs=pl.BlockSpec((None, tr, C), lambda j, i, p: (j, i, 0)),
        ),
        out_shape=jax.ShapeDtypeStruct((N_CHIPS, R // 2, C), BF16),
        compiler_params=_params(2),
    )(place, grad, sibbuf)


def rs_final(grad, sibbuf, rbuf, place, tr, name):
    _, R, C = grad.shape
    nt = (R // 2) // tr

    def body(p_ref, g_ref, s_ref, r_ref, o_ref):
        o_ref[...] = (((g_ref[...] + s_ref[...]) + r_ref[0].astype(F32)) + r_ref[1].astype(F32)) + r_ref[2].astype(F32)

    return pl.pallas_call(
        body,
        name=name,
        grid_spec=pltpu.PrefetchScalarGridSpec(
            num_scalar_prefetch=1,
            grid=(nt,),
            in_specs=[pl.BlockSpec((None, tr, C), lambda i, p: (p[1], p[0] * nt + i, 0)),
                      pl.BlockSpec((None, tr, C), lambda i, p: (p[1], i, 0)),
                      pl.BlockSpec((3, tr, C), lambda i, p: (0, i, 0))],
            out_specs=pl.BlockSpec((tr, C), lambda i, p: (p[0] * nt + i, 0)),
        ),
        out_shape=jax.ShapeDtypeStruct((R, C), F32),
        compiler_params=_params(1),
    )(place, grad, sibbuf, rbuf)


class GradReduce:
    def __init__(self, tag, grads, rows, place):
        self.tag, self.grads, self.rows, self.place = tag, grads, rows, place
        self.n = len(grads)

    def d2d_start(self, carry):
        sib = [lax.empty((N_CHIPS, g.shape[1] // 2, g.shape[2]), F32) for g in self.grads]
        self.s1, self.r1, bufs, carry = split_start(f"rs_{self.tag}_d2d_start", self.grads + sib, _rs_d2d_plan(self.n), self.n, carry)
        self.bufs1 = bufs
        return carry

    def add_and_ici_start(self, after, carry):
        bufs = split_wait(f"rs_{self.tag}_d2d_wait", self.s1, self.r1, self.bufs1, _rs_d2d_plan(self.n), after)
        self.grads, self.sib = bufs[:self.n], bufs[self.n:]
        pb = [rs_add(g, s, self.place, tr, f"rs_{self.tag}_add{w}")
              for w, (g, s, tr) in enumerate(zip(self.grads, self.sib, self.rows))]
        rb = [lax.empty((3,) + p.shape[1:], BF16) for p in pb]
        self.s2, self.r2, self.bufs2, carry = split_start(f"rs_{self.tag}_ici_start", pb + rb, _rs_ici_plan(self.n), 3 * self.n, carry)
        return carry

    def final_and_share_start(self, after, carry):
        bufs = split_wait(f"rs_{self.tag}_ici_wait", self.s2, self.r2, self.bufs2, _rs_ici_plan(self.n), after)
        rb = bufs[self.n:]
        full = [rs_final(g, s, r, self.place, tr, f"rs_{self.tag}_final{w}")
                for w, (g, s, r, tr) in enumerate(zip(self.grads, self.sib, rb, self.rows))]
        self.s3, self.r3, self.bufs3, carry = split_start(f"rs_{self.tag}_share_start", full, _rs_share_plan(self.n), self.n, carry)
        return carry

    def finish(self, after):
        return split_wait(f"rs_{self.tag}_share_wait", self.s3, self.r3, self.bufs3, _rs_share_plan(self.n), after)


def _rope_tables(positions):
    inv_freq = ROPE_THETA ** (-jnp.arange(0, ROT_DIM, 2, dtype=F32) / ROT_DIM)
    ang = positions.astype(F32)[:, None] * inv_freq
    cos, sin = jnp.cos(ang), jnp.sin(ang)
    S = positions.shape[0]
    one, zero = jnp.ones((S, 48), F32), jnp.zeros((S, 48), F32)
    z8 = jnp.zeros((S, 8), F32)
    tc = jnp.concatenate([cos, cos, one], axis=1)
    tsa = jnp.concatenate([z8, sin, zero], axis=1)
    tsb = jnp.concatenate([-sin, z8, zero], axis=1)
    return tuple(jnp.tile(t, (1, 2)) for t in (tc, tsa, tsb))


def _block_diag(w_pool):
    wbd = jnp.zeros((POOL_W, POOL_W), F32)
    for gi in range(4):
        wbd = wbd.at[gi * 64:(gi + 1) * 64, gi * 64:(gi + 1) * 64].set(w_pool[gi])
    return wbd


def kernel(x, c, positions, w_ada, b_ada, g_pre_mix, g_post_mix, g_pre_ffn, g_post_ffn, w_in, w_pool, b_pool, pool_scale, w_out, w_up, conv_w, conv_b, w_down, loss_target, m_w_ada, m_b_ada, m_g_pre_mix, m_g_post_mix, m_g_pre_ffn, m_g_post_ffn, m_w_in, m_w_pool, m_b_pool, m_pool_scale, m_w_out, m_w_up, m_conv_w, m_conv_b, m_w_down, v_w_ada, v_b_ada, v_g_pre_mix, v_g_post_mix, v_g_pre_ffn, v_g_post_ffn, v_w_in, v_w_pool, v_b_pool, v_pool_scale, v_w_out, v_w_up, v_conv_w, v_conv_b, v_w_down):
    xi, yi, ci = lax.axis_index("x"), lax.axis_index("y"), lax.axis_index("c")
    chip = 2 * xi + yi
    place = jnp.stack([ci, chip]).astype(jnp.int32)
    x2, tgt = x[0], loss_target[0]
    S = x2.shape[0]

    cb_ada = w_ada.shape[2]
    b_cols = lax.dynamic_slice(b_ada, (0, chip * cb_ada), (1, cb_ada))
    c_all, mod6, conv_w_g = ada_mod(c.reshape(1, 1, D_MODEL), w_ada[0], b_cols, conv_w[0])
    conv_w_f = jnp.transpose(conv_w_g, (1, 0, 2)).reshape(3, D_FF)

    def landing(s_):
        return lax.dynamic_update_slice(lax.empty((N_CHIPS,) + s_.shape, s_.dtype), s_[None], (chip, 0, 0))

    mix_sh = [w_in[0].astype(BF16), w_out[0].astype(BF16)]
    ffn_sh = [w_up[0].astype(BF16), w_down[0].astype(BF16)]
    ga_s, ga_r, ga_bufs, mod6 = split_start("gather_mix_ici_start", mix_sh + [landing(t) for t in mix_sh], _gather_ici_plan(2), 6, mod6)
    gb_s, gb_r, gb_bufs, mod6 = split_start("gather_ffn_ici_start", ffn_sh + [landing(t) for t in ffn_sh], _gather_ici_plan(2), 6, mod6)
    tc, tsa, tsb = _rope_tables(positions[0])
    wbd = _block_diag(w_pool[0]).astype(BF16)
    b_pool2, scale2 = b_pool.reshape(1, POOL_W), pool_scale
    ga_bufs = split_wait("gather_mix_ici_wait", ga_s, ga_r, ga_bufs, _gather_ici_plan(2), tc)
    gc_s, gc_r, mix_land, mod6 = split_start("gather_mix_d2d_start", ga_bufs[2:], _gather_d2d_plan(2), 6, mod6)
    w_in_g, w_out_g = split_wait("gather_mix_d2d_wait", gc_s, gc_r, mix_land, _gather_d2d_plan(2), mod6)

    h1, u, *qkv = inproj_fwd(x2, g_pre_mix, mod6, w_in_g, tc, tsa, tsb)
    mixed, pool = pool_fwd(u, wbd, b_pool2, scale2)
    o_l = [attn_fwd(t, d) for t, d in zip(qkv, DILATIONS)]
    attn_done = sum(l[0, :8, :128] for _, l in o_l)
    gb_bufs = split_wait("gather_ffn_ici_wait", gb_s, gb_r, gb_bufs, _gather_ici_plan(2), attn_done)
    gd_s, gd_r, ffn_land, pool = split_start("gather_ffn_d2d_start", gb_bufs[2:], _gather_d2d_plan(2), 6, pool)
    cat, lse, lse4, lse16, y1, x1, h2 = outproj_fwd([o for o, _ in o_l] + [l for _, l in o_l], pool, x2, w_out_g, g_post_mix,
                                                    g_pre_ffn, mod6)
    lses = [lse[None], lse4, lse16]
    w_up_g, w_down_g = split_wait("gather_ffn_d2d_wait", gd_s, gd_r, ffn_land, _gather_d2d_plan(2), h2)
    w_down_f = w_down_g.reshape(D_FF, D_MODEL)
    gate, val = up_fwd(h2, w_up_g)
    a, dy2, dout, loss_v, d_gt_f, d_g_post_ffn = down_fwd(gate, val, conv_w_f, conv_b, w_down_f, x1, tgt, g_post_ffn, mod6)

    dgc, dval, d_conv_w, d_conv_b, dw_down, dw_up = down_bwd(dy2, w_down_f, gate, val, conv_w_f, conv_b, a, h2)
    dx1, dy1, d_sh_f, d_sc_f, d_g_pre_ffn, d_gt_m, d_g_post_mix, dw_up = up_bwd(
        dgc, dval, conv_w_f, w_up_g, x1, dout, y1, g_pre_ffn, g_post_mix, mod6, h2, dw_up)
    rs_ffn = GradReduce("ffn", [dw_up, dw_down.reshape(N_CHIPS, D_FF // N_CHIPS, D_MODEL)], [256, 176], place)
    dy1 = rs_ffn.d2d_start(dy1)
    dpool, da1, da4, da16, dl1, dl4, dl16, dw_out = outproj_bwd(dy1, w_out_g, cat)
    dpool = rs_ffn.add_and_ici_start(dw_out, dpool)
    du, d_wbd, d_b_pool, d_scale = pool_bwd(dpool, mixed, wbd, b_pool2, scale2)
    dqkv = [attn_bwd(t, da, ls, dl, d) for t, da, ls, dl, d in zip(qkv, (da1[None], da4, da16), lses, (dl1[None], dl4, dl16), DILATIONS)]
    grad_x, d_sh_m, d_sc_m, d_g_pre_mix, dw_in = inproj_bwd(dqkv, du, x2, dx1, w_in_g, g_pre_mix, mod6, tc, tsa, tsb, h1)

    z1 = jnp.zeros((1, D_MODEL), F32)
    slab_a = jnp.concatenate(
        [d_sh_m, d_sc_m, d_gt_m, d_sh_f, d_sc_f, d_gt_f, d_g_pre_mix, d_g_post_mix, d_g_pre_ffn, d_g_post_ffn,
         jnp.concatenate([d_b_pool, d_scale, jnp.zeros((1, 512), F32)], axis=1)] + [z1] * 5, axis=0)
    slab_b = jnp.concatenate([d_conv_w, d_conv_b, jnp.zeros((4, D_FF), F32)], axis=0)
    d_wpool = jnp.concatenate([d_wbd[gi * 64:(gi + 1) * 64, gi * 64:(gi + 1) * 64] for gi in range(4)], axis=0)
    slab_a_g, slab_b_g, wpool_g = gather_small([slab_a, slab_b, d_wpool])
    cw_cols = conv_w.shape[2]
    convw_g = lax.dynamic_slice(slab_b_g, (0, 0, chip * cw_cols), (N_DEV, 3, cw_cols))
    rs_mix = GradReduce("mix", [dw_in, dw_out], [256, 256], place)
    slab_a_g = rs_mix.d2d_start(slab_a_g)
    slab_a_g = rs_ffn.final_and_share_start(slab_a_g, slab_a_g)
    slab_a_g = rs_mix.add_and_ici_start(slab_a_g, slab_a_g)
    dmod_cols = lax.dynamic_slice(slab_a_g[:, :6, :].reshape(N_DEV, 6 * D_MODEL), (0, chip * cb_ada), (N_DEV, cb_ada))

    res = {}

    def big_adamw(name, w, g, m, v, tr):
        d_, m_, v_ = adamw_rows(w[0], g, m[0], v[0], tr, "adamw_" + name)
        res[name] = (g[None], d_[None], m_[None], v_[None])
        return v_

    g_ada, d_ada, m_ada, v_ada = adamw_ada(c_all.reshape(N_DEV, D_MODEL).T, dmod_cols, w_ada[0], m_w_ada[0], v_w_ada[0])
    res["w_ada"] = (g_ada[None], d_ada[None], m_ada[None], v_ada[None])
    g_w_up, g_w_down = rs_ffn.finish(v_ada)
    big_adamw("w_up", w_up, g_w_up, m_w_up, v_w_up, 256)
    last = big_adamw("w_down", w_down, g_w_down, m_w_down, v_w_down, 352)
    rs_mix.final_and_share_start(last, jnp.zeros((8, 128), F32))
    g_w_in, g_w_out = rs_mix.finish(last)
    big_adamw("w_in", w_in, g_w_in, m_w_in, v_w_in, 256)
    big_adamw("w_out", w_out, g_w_out, m_w_out, v_w_out, 256)
    flat = lambda t: t.reshape(1, POOL_W)
    wp = lambda t: t.reshape(POOL_W, 64)
    small = adamw_small(slab_a_g, slab_b_g, convw_g, wpool_g, {
        "b_ada": (b_ada, m_b_ada, v_b_ada), "g_pre_mix": (g_pre_mix, m_g_pre_mix, v_g_pre_mix),
        "g_post_mix": (g_post_mix, m_g_post_mix, v_g_post_mix), "g_pre_ffn": (g_pre_ffn, m_g_pre_ffn, v_g_pre_ffn),
        "g_post_ffn": (g_post_ffn, m_g_post_ffn, v_g_post_ffn), "b_pool": (flat(b_pool), flat(m_b_pool), flat(v_b_pool)),
        "pool_scale": (pool_scale, m_pool_scale, v_pool_scale), "conv_b": (conv_b, m_conv_b, v_conv_b),
        "conv_w": (conv_w[0], m_conv_w[0], v_conv_w[0]), "w_pool": (wp(w_pool), wp(m_w_pool), wp(v_w_pool))})
    for name in ("b_ada", "g_pre_mix", "g_post_mix", "g_pre_ffn", "g_post_ffn", "pool_scale", "conv_b"):
        res[name] = tuple(small[name])
    res["b_pool"] = tuple(t.reshape(1, 4, 64) for t in small["b_pool"])
    res["conv_w"] = tuple(t[None] for t in small["conv_w"])
    res["w_pool"] = tuple(t.reshape(1, 4, 64, 64) for t in small["w_pool"])

    loss = lax.psum(loss_v[0, 0], ("x", "y", "c"))
    order = ["w_ada", "b_ada", "g_pre_mix", "g_post_mix", "g_pre_ffn", "g_post_ffn", "w_in", "w_pool", "b_pool", "pool_scale",
             "w_out", "w_up", "conv_w", "conv_b", "w_down"]
    outs = [loss, grad_x[None]]
    for k in range(4):
        outs += [res[n][k] for n in order]
    return tuple(outs)
```

```python
import functools
import math

import jax
import jax.numpy as jnp
from jax import lax
from jax.experimental import pallas as pl
from jax.experimental.pallas import tpu as pltpu

F32 = jnp.float32
BF16 = jnp.bfloat16
MESH = pl.DeviceIdType.MESH

D_MODEL = 1024
HEAD_DIM = 64
POOL_W = 256
GROUP_W = 256
DILATIONS = (1, 4, 16)
ATT_BLOCK = 128
IN_W = 2560
D_FF = 2816
HALF_FF = 1408
ROT_DIM = 16
ROPE_THETA = 500000.0
NORM_EPS = 1e-6
N_CHIPS = 4
N_DEV = 8
NEG = -1e30

ADAM_LR = 0.001
ADAM_B1 = 0.9
ADAM_B2 = 0.999
ADAM_EPS = 1e-08
ADAM_WD = 0.01
ADAM_STEP = 10

VMEM_LIMIT = 56 * 1024 * 1024

NT = (((1,), (1,)), ((), ()))
TN = (((0,), (0,)), ((), ()))


def _params(n_grid=0, **kw):
    sem = ("arbitrary",) * n_grid if n_grid else None
    return pltpu.CompilerParams(dimension_semantics=sem, vmem_limit_bytes=VMEM_LIMIT, **kw)


def _full(shape):
    nd = len(shape)
    return pl.BlockSpec(tuple(shape), lambda *_: (0,) * nd, pipeline_mode=pl.Buffered(1))


def _rows(tm, ncol):
    return pl.BlockSpec((tm, ncol), lambda i: (i, 0))


def _acc(ref, val):
    @pl.when(pl.program_id(0) == 0)
    def _():
        ref[...] = jnp.zeros_like(ref)

    ref[...] += val


def _colsum(v):
    return jnp.sum(v, axis=0, keepdims=True)


def _rope128(t, cs, sa, sb, sign):
    return t * cs + sign * (pltpu.roll(t, 8, 1) * sa + pltpu.roll(t, 120, 1) * sb)


GELU_C0 = math.sqrt(2.0 / math.pi)
GELU_C1 = GELU_C0 * 0.044715


def _gelu(z):
    z2 = z * z
    t = jnp.tanh(z * (GELU_C0 + GELU_C1 * z2))
    u = 0.5 * t + 0.5
    return z * u, u, t, z2


def _gelu_grad(z, u, t, z2):
    return u + (z * (GELU_C0 + (3.0 * GELU_C1) * z2)) * (0.5 - 0.5 * (t * t))


def _conv_taps(gate, halo, first):
    row = lax.broadcasted_iota(jnp.int32, gate.shape, 0)
    halo = jnp.where(first, 0.0, halo)
    p1 = halo[15:16, :]
    p2 = halo[14:15, :]
    g1 = jnp.where(row == 0, p1, pltpu.roll(gate, 1, 0))
    g2 = jnp.where(row == 0, p2, jnp.where(row == 1, p1, pltpu.roll(gate, 2, 0)))
    return g1, g2


def inproj_fwd(x, g, mod6, w_in_g, tc, tsa, tsb, tm=512):
    S = x.shape[0]

    def body(x_ref, g_ref, mod_ref, w_ref, tc_ref, tsa_ref, tsb_ref, h_ref, u_ref, q1_ref, q4_ref, q16_ref, scr):
        qkv_refs = (q1_ref, q4_ref, q16_ref)
        xv = x_ref[...]
        rstd = lax.rsqrt(jnp.mean(xv * xv, axis=-1, keepdims=True) + NORM_EPS)
        h = ((xv * rstd) * g_ref[...]) * (1.0 + mod_ref[1:2, :]) + mod_ref[0:1, :]
        hb = h.astype(BF16)
        h_ref[...] = hb
        cs, sa, sb = tc_ref[...], tsa_ref[...], tsb_ref[...]
        for j in range(N_CHIPS):
            res = jnp.dot(hb, w_ref[j], preferred_element_type=F32)
            for t in range(5):
                sp = 5 * j + t
                piece, half = sp // 2, sp % 2
                blk = res[:, t * 128:(t + 1) * 128]
                lanes = slice(half * 128, (half + 1) * 128)
                if piece == 0:
                    u_ref[:, lanes] = blk
                else:
                    kind, gi = (piece - 1) // 3, (piece - 1) % 3
                    if kind == 0:
                        blk = _rope128(blk, cs, sa, sb, 1.0) * (HEAD_DIM ** -0.5)
                    elif kind == 1:
                        blk = _rope128(blk, cs, sa, sb, 1.0)
                    d = DILATIONS[gi]
                    if d == 1:
                        q1_ref[kind, 0, :, lanes] = blk.astype(BF16)
                    else:
                        scr[...] = blk
                        for r in range(d):
                            qkv_refs[gi][kind, r, :, lanes] = scr[pl.ds(r, tm // d, stride=d), :].astype(BF16)

    cls = lambda d: pl.BlockSpec((3, d, tm // d, GROUP_W), lambda i: (0, 0, i, 0))
    return pl.pallas_call(
        body,
        name="inproj_fwd",
        grid=(S // tm,),
        in_specs=[_rows(tm, D_MODEL), _full((1, D_MODEL)), _full((6, D_MODEL)), _full(w_in_g.shape),
                  _rows(tm, 128), _rows(tm, 128), _rows(tm, 128)],
        out_specs=[_rows(tm, D_MODEL), _rows(tm, POOL_W)] + [cls(d) for d in DILATIONS],
        out_shape=[jax.ShapeDtypeStruct((S, D_MODEL), BF16), jax.ShapeDtypeStruct((S, POOL_W), F32)]
        + [jax.ShapeDtypeStruct((3, d, S // d, GROUP_W), BF16) for d in DILATIONS],
        scratch_shapes=[pltpu.VMEM((tm, 128), F32)],
        compiler_params=_params(1),
    )(x, g, mod6, w_in_g, tc, tsa, tsb)


def _attn_masks():
    row = lax.broadcasted_iota(jnp.int32, (ATT_BLOCK, 2 * ATT_BLOCK), 0)
    col = lax.broadcasted_iota(jnp.int32, (ATT_BLOCK, 2 * ATT_BLOCK), 1)
    band = (col >= row) & (col <= row + ATT_BLOCK)
    lane = lax.broadcasted_iota(jnp.int32, (ATT_BLOCK, 128), 1)
    return band, col, lane < HEAD_DIM


def attn_fwd(qkv, d):
    L = qkv.shape[2]
    nb = L // ATT_BLOCK

    def body(q_ref, k_ref, v_ref, o_ref, l_ref, kpad, vpad):
        kpad[0:ATT_BLOCK, :] = jnp.zeros((ATT_BLOCK, GROUP_W), BF16)
        vpad[0:ATT_BLOCK, :] = jnp.zeros((ATT_BLOCK, GROUP_W), BF16)
        kpad[ATT_BLOCK:, :] = k_ref[...]
        vpad[ATT_BLOCK:, :] = v_ref[...]
        band, col, lo = _attn_masks()

        def step(n, carry):
            r0 = pl.multiple_of(n * ATT_BLOCK, ATT_BLOCK)
            valid = band & ((col >= ATT_BLOCK) | (n > 0))
            qb = q_ref[pl.ds(r0, ATT_BLOCK), :]
            kb = kpad[pl.ds(r0, 2 * ATT_BLOCK), :]
            vb = vpad[pl.ds(r0, 2 * ATT_BLOCK), :]
            for pair in range(2):
                lanes = slice(pair * 128, (pair + 1) * 128)
                qp, kp, vp = qb[:, lanes], kb[:, lanes], vb[:, lanes]
                outs, lses = [], []
                for hh in range(2):
                    sel = lo if hh == 0 else jnp.logical_not(lo)
                    qm = jnp.where(sel, qp, jnp.zeros_like(qp))
                    s = lax.dot_general(qm, kp, NT, preferred_element_type=F32)
                    s = jnp.where(valid, s, NEG)
                    m = jnp.max(s, axis=1, keepdims=True)
                    p = jnp.exp(s - m)
                    den = jnp.sum(p, axis=1, keepdims=True)
                    pv = jnp.dot(p.astype(BF16), vp, preferred_element_type=F32)
                    outs.append(pv / den)
                    lses.append(m + jnp.log(den))
                o_ref[pl.ds(r0, ATT_BLOCK), lanes] = jnp.where(lo, outs[0], outs[1])
                l_ref[pl.ds(r0, ATT_BLOCK), lanes] = jnp.where(lo, lses[0], lses[1])
            return carry

        lax.fori_loop(0, nb, step, 0, unroll=min(4, nb))

    spec = lambda kind: pl.BlockSpec((None, None, L, GROUP_W), lambda r: (kind, r, 0, 0))
    return pl.pallas_call(
        body,
        name=f"attn_fwd_d{d}",
        grid=(d,),
        in_specs=[spec(0), spec(1), spec(2)],
        out_specs=[pl.BlockSpec((None, L, GROUP_W), lambda r: (r, 0, 0))] * 2,
        out_shape=[jax.ShapeDtypeStruct((d, L, GROUP_W), F32)] * 2,
        scratch_shapes=[pltpu.VMEM((L + ATT_BLOCK, GROUP_W), BF16)] * 2,
        compiler_params=_params(1),
    )(qkv, qkv, qkv)


def _pool_lane_windows(shape):
    lane = lax.broadcasted_iota(jnp.int32, shape, 1)
    return lane, jnp.where(lane < 64, 2, jnp.where(lane < 128, 4, jnp.where(lane < 192, 8, 16)))


def pool_fwd(u, wbd, b, scale):
    S = u.shape[0]

    def body(u_ref, w_ref, b_ref, s_ref, mixed_ref, out_ref):
        uv = u_ref[...]
        row = lax.broadcasted_iota(jnp.int32, uv.shape, 0)
        lane, win = _pool_lane_windows(uv.shape)

        def shift(a, k):
            return jnp.where(row >= k, pltpu.roll(a, k, 0), 0.0)

        s2 = uv + shift(uv, 1)
        s4 = s2 + shift(s2, 2)
        s8 = s4 + shift(s4, 4)
        s16 = s8 + shift(s8, 8)
        tsum = jnp.where(lane < 64, s2, jnp.where(lane < 128, s4, jnp.where(lane < 192, s8, s16)))
        cnt = jnp.minimum(row + 1, win).astype(F32)
        mb = (tsum / cnt - uv).astype(BF16)
        mixed_ref[...] = mb
        y = jnp.dot(mb, w_ref[...], preferred_element_type=F32) + b_ref[...]
        out_ref[...] = (y * s_ref[...]).astype(BF16)

    vm = pl.BlockSpec(memory_space=pltpu.VMEM)
    return pl.pallas_call(
        body,
        name="pool_fwd",
        in_specs=[vm] * 4,
        out_specs=[vm] * 2,
        out_shape=[jax.ShapeDtypeStruct((S, POOL_W), BF16)] * 2,
        compiler_params=_params(),
    )(u, wbd, b, scale)


def outproj_fwd(o_l, pool, x, w_out_g, g_post, g_pre, mod6, tm=512):
    S = x.shape[0]

    def body(o0, o1, o2, l0, l1, l2, pool_ref, x_ref, w_ref, gpost_ref, gpre_ref, mod_ref,
             cat_ref, lse_ref, lse4_ref, lse16_ref, y1_ref, x1_ref, h2_ref, so4, sl4, so16, sl16):
        for d, src, dst in ((4, o1, so4), (4, l1, sl4), (16, o2, so16), (16, l2, sl16)):
            for r in range(d):
                for h in range(2):
                    dst[h, pl.ds(r, tm // d, stride=d), :] = src[r, :, h * 128:(h + 1) * 128]
        nat = lambda ref: jnp.concatenate([ref[0], ref[1]], axis=1)
        a, b, c = l0[0], nat(sl4), nat(sl16)
        m = jnp.maximum(jnp.maximum(a, b), c)
        e0, e1, e2 = jnp.exp(a - m), jnp.exp(b - m), jnp.exp(c - m)
        z = e0 + e1 + e2
        lse = m + jnp.log(z)
        lse_ref[...] = lse
        for h in range(2):
            sl4[h] = lse[:, h * 128:(h + 1) * 128]
        for d, dst in ((4, lse4_ref), (16, lse16_ref)):
            for r in range(d):
                for h in range(2):
                    dst[r, :, h * 128:(h + 1) * 128] = sl4[h, pl.ds(r, tm // d, stride=d), :]
        attn = (e0 * o0[0] + e1 * nat(so4) + e2 * nat(so16)) / z
        cat = jnp.concatenate([pool_ref[...], attn.astype(BF16)], axis=1)
        cat_ref[...] = cat
        y1 = jnp.concatenate([jnp.dot(cat, w_ref[j], preferred_element_type=F32) for j in range(N_CHIPS)], axis=1)
        y1_ref[...] = y1
        rstd = lax.rsqrt(jnp.mean(y1 * y1, axis=-1, keepdims=True) + NORM_EPS)
        x1 = x_ref[...] + mod_ref[2:3, :] * ((y1 * rstd) * gpost_ref[...])
        x1_ref[...] = x1
        rstd2 = lax.rsqrt(jnp.mean(x1 * x1, axis=-1, keepdims=True) + NORM_EPS)
        h2 = ((x1 * rstd2) * gpre_ref[...]) * (1.0 + mod_ref[4:5, :]) + mod_ref[3:4, :]
        h2_ref[...] = h2.astype(BF16)

    t256 = _rows(tm, GROUP_W)
    cls = lambda d: pl.BlockSpec((d, tm // d, GROUP_W), lambda i: (0, i, 0))
    cls_shape = lambda d: jax.ShapeDtypeStruct((d, S // d, GROUP_W), F32)
    return pl.pallas_call(
        body,
        name="outproj_fwd",
        grid=(S // tm,),
        in_specs=[cls(d) for d in DILATIONS] * 2 + [t256, _rows(tm, D_MODEL), _full(w_out_g.shape), _full((1, D_MODEL)),
                                                    _full((1, D_MODEL)), _full((6, D_MODEL))],
        out_specs=[_rows(tm, 512), t256, cls(4), cls(16), _rows(tm, D_MODEL), _rows(tm, D_MODEL), _rows(tm, D_MODEL)],
        out_shape=[jax.ShapeDtypeStruct((S, 512), BF16), jax.ShapeDtypeStruct((S, GROUP_W), F32), cls_shape(4), cls_shape(16),
                   jax.ShapeDtypeStruct((S, D_MODEL), F32), jax.ShapeDtypeStruct((S, D_MODEL), F32),
                   jax.ShapeDtypeStruct((S, D_MODEL), BF16)],
        scratch_shapes=[pltpu.VMEM((2, tm, 128), F32)] * 4,
        compiler_params=_params(1),
    )(*o_l, pool, x, w_out_g, g_post, g_pre, mod6)


def up_fwd(h2, w_up_g, tm=512):
    S = h2.shape[0]

    def body(h_ref, w_ref, gate_ref, val_ref):
        hb = h_ref[...]
        for j in range(N_CHIPS):
            res = jnp.dot(hb, w_ref[j], preferred_element_type=F32).astype(BF16)
            dst = gate_ref if j < 2 else val_ref
            dst[:, (j % 2) * HALF_FF:(j % 2 + 1) * HALF_FF] = res

    return pl.pallas_call(
        body,
        name="up_fwd",
        grid=(S // tm,),
        in_specs=[_rows(tm, D_MODEL), _full(w_up_g.shape)],
        out_specs=[_rows(tm, D_FF)] * 2,
        out_shape=[jax.ShapeDtypeStruct((S, D_FF), BF16)] * 2,
        compiler_params=_params(1),
    )(h2, w_up_g)


def _halo_prev(tm, ncol):
    return pl.BlockSpec((16, ncol), lambda i: (jnp.maximum(i * (tm // 16) - 1, 0), 0))


def down_fwd(gate, val, conv_w, conv_b, w_down, x1, target, g_post, mod6, tm=256):
    S = x1.shape[0]

    def body(gate_ref, halo_ref, val_ref, cw_ref, cb_ref, w_ref, x1_ref, tgt_ref, g_ref, mod_ref,
             a_ref, dy2_ref, dout_ref, loss_ref, dgt_ref, dg_ref):
        first = pl.program_id(0) == 0
        y2 = jnp.zeros((tm, D_MODEL), F32)
        for ch in range(2):
            cols = slice(ch * HALF_FF, (ch + 1) * HALF_FF)
            gt = gate_ref[:, cols].astype(F32)
            g1, g2 = _conv_taps(gt, halo_ref[:, cols].astype(F32), first)
            gc = g2 * cw_ref[0:1, cols] + g1 * cw_ref[1:2, cols] + gt * cw_ref[2:3, cols] + cb_ref[:, cols]
            ge = _gelu(gc)[0]
            ab = (ge * val_ref[:, cols].astype(F32)).astype(BF16)
            a_ref[:, cols] = ab
            y2 = y2 + jnp.dot(ab, w_ref[cols, :], preferred_element_type=F32)
        rstd = lax.rsqrt(jnp.mean(y2 * y2, axis=-1, keepdims=True) + NORM_EPS)
        y2n = y2 * rstd
        gv = g_ref[...]
        gtf = mod_ref[5:6, :]
        r2 = y2n * gv
        diff = (x1_ref[...] + gtf * r2) - tgt_ref[...]
        _acc(loss_ref, jnp.zeros((1, 128), F32) + 0.5 * jnp.sum(diff * diff) * (1.0 / D_MODEL))
        dout = diff * (1.0 / D_MODEL)
        dout_ref[...] = dout
        _acc(dgt_ref, _colsum(dout * r2))
        dr2 = dout * gtf
        _acc(dg_ref, _colsum(dr2 * y2n))
        dyn = dr2 * gv
        dy2 = rstd * (dyn - y2n * jnp.mean(dyn * y2n, axis=-1, keepdims=True))
        dy2_ref[...] = dy2.astype(BF16)

    vec = _full((1, D_MODEL))
    return pl.pallas_call(
        body,
        name="down_fwd",
        grid=(S // tm,),
        in_specs=[_rows(tm, D_FF), _halo_prev(tm, D_FF), _rows(tm, D_FF), _full((3, D_FF)), _full((1, D_FF)),
                  _full((D_FF, D_MODEL)), _rows(tm, D_MODEL), _rows(tm, D_MODEL), vec, _full((6, D_MODEL))],
        out_specs=[_rows(tm, D_FF), _rows(tm, D_MODEL), _rows(tm, D_MODEL), _full((1, 128)), vec, vec],
        out_shape=[jax.ShapeDtypeStruct((S, D_FF), BF16), jax.ShapeDtypeStruct((S, D_MODEL), BF16),
                   jax.ShapeDtypeStruct((S, D_MODEL), F32), jax.ShapeDtypeStruct((1, 128), F32),
                   jax.ShapeDtypeStruct((1, D_MODEL), F32), jax.ShapeDtypeStruct((1, D_MODEL), F32)],
        compiler_params=_params(1),
    )(gate, gate, val, conv_w, conv_b, w_down, x1, target, g_post, mod6)


def down_bwd(dy2, w_down, gate, val, conv_w, conv_b, a, h2, tm=256):
    S = dy2.shape[0]

    def body(dy_ref, w_ref, gate_ref, halo_ref, val_ref, cw_ref, cb_ref, a_ref, h_ref,
             dgc_ref, dval_ref, dcw_ref, dcb_ref, dwd_ref, dwu_ref):
        first = pl.program_id(0) == 0

        @pl.when(first)
        def _():
            dcw_ref[...] = jnp.zeros_like(dcw_ref)
            dcb_ref[...] = jnp.zeros_like(dcb_ref)
            dwd_ref[...] = jnp.zeros_like(dwd_ref)
            dwu_ref[...] = jnp.zeros_like(dwu_ref)

        dyb = dy_ref[...]
        hb = h_ref[...]
        for ch in range(2):
            cols = slice(ch * HALF_FF, (ch + 1) * HALF_FF)
            da = lax.dot_general(dyb, w_ref[cols, :], NT, preferred_element_type=F32)
            gt = gate_ref[:, cols].astype(F32)
            g1, g2 = _conv_taps(gt, halo_ref[:, cols].astype(F32), first)
            gc = g2 * cw_ref[0:1, cols] + g1 * cw_ref[1:2, cols] + gt * cw_ref[2:3, cols] + cb_ref[:, cols]
            ge, u, th, z2 = _gelu(gc)
            dgc = da * val_ref[:, cols].astype(F32) * _gelu_grad(gc, u, th, z2)
            dgc_ref[:, cols] = dgc.astype(BF16)
            dvb = (da * ge).astype(BF16)
            dval_ref[:, cols] = dvb
            dcb_ref[:, cols] += _colsum(dgc)
            dcw_ref[0:1, cols] += _colsum(dgc * g2)
            dcw_ref[1:2, cols] += _colsum(dgc * g1)
            dcw_ref[2:3, cols] += _colsum(dgc * gt)
            dwd_ref[cols, :] += lax.dot_general(a_ref[:, cols], dyb, TN, preferred_element_type=F32)
            dwu_ref[ch] += lax.dot_general(hb, dvb, TN, preferred_element_type=F32)

    return pl.pallas_call(
        body,
        name="down_bwd",
        grid=(S // tm,),
        in_specs=[_rows(tm, D_MODEL), _full((D_FF, D_MODEL)), _rows(tm, D_FF), _halo_prev(tm, D_FF), _rows(tm, D_FF),
                  _full((3, D_FF)), _full((1, D_FF)), _rows(tm, D_FF), _rows(tm, D_MODEL)],
        out_specs=[_rows(tm, D_FF), _rows(tm, D_FF), _full((3, D_FF)), _full((1, D_FF)), _full((D_FF, D_MODEL)),
                   pl.BlockSpec((2, D_MODEL, HALF_FF), lambda i: (1, 0, 0), pipeline_mode=pl.Buffered(1))],
        out_shape=[jax.ShapeDtypeStruct((S, D_FF), BF16), jax.ShapeDtypeStruct((S, D_FF), BF16),
                   jax.ShapeDtypeStruct((3, D_FF), F32), jax.ShapeDtypeStruct((1, D_FF), F32),
                   jax.ShapeDtypeStruct((D_FF, D_MODEL), F32), jax.ShapeDtypeStruct((N_CHIPS, D_MODEL, HALF_FF), F32)],
        compiler_params=_params(1),
    )(dy2, w_down, gate, gate, val, conv_w, conv_b, a, h2)


def dw_matmul(a, b, out_blocks, blk_shape, a_cols, b_cols, a_blocked, name, prev=None, blk_off=0, n_blk=None, tm=512):
    S = a.shape[0]
    n_blk = out_blocks if n_blk is None else n_blk

    def body(*refs):
        a_ref, b_ref, o_ref = refs[0], refs[1], refs[-1]

        @pl.when(pl.program_id(1) == 0)
        def _():
            o_ref[...] = jnp.zeros_like(o_ref)

        o_ref[...] += lax.dot_general(a_ref[...], b_ref[...], TN, preferred_element_type=F32)

    a_spec = pl.BlockSpec((tm, a_cols), (lambda j, i: (i, j)) if a_blocked else (lambda j, i: (i, 0)))
    b_spec = pl.BlockSpec((tm, b_cols), (lambda j, i: (i, 0)) if a_blocked else (lambda j, i: (i, j)))
    in_specs = [a_spec, b_spec]
    args = [a, b]
    aliases = {}
    if prev is not None:
        in_specs.append(pl.BlockSpec(memory_space=pl.ANY))
        args.append(prev)
        aliases = {2: 0}
    return pl.pallas_call(
        body,
        name=name,
        grid=(n_blk, S // tm),
        in_specs=in_specs,
        out_specs=pl.BlockSpec((None,) + tuple(blk_shape), lambda j, i: (j + blk_off, 0, 0)),
        out_shape=jax.ShapeDtypeStruct((out_blocks,) + tuple(blk_shape), F32),
        input_output_aliases=aliases,
        compiler_params=_params(2),
    )(*args)


def up_bwd(dgc, dval, conv_w, w_up_g, x1, dout, y1, g_pre, g_post, mod6, h2, dw_up, tm=256):
    S = x1.shape[0]
    last_blk = S // 16 - 1

    def body(dgc_ref, nxt_ref, dval_ref, cw_ref, w_ref, x1_ref, dout_ref, y1_ref, gpre_ref, gpost_ref, mod_ref, h_ref, dwin_ref,
             dx1_ref, dy1_ref, dsh_ref, dsc_ref, dgpre_ref, dgt_ref, dgpost_ref, dwu_ref):
        last = pl.program_id(0) == pl.num_programs(0) - 1

        @pl.when(pl.program_id(0) == 0)
        def _():
            dwu_ref[...] = jnp.zeros_like(dwu_ref)

        hb = h_ref[...]
        dh = jnp.zeros((tm, D_MODEL), F32)
        for ch in range(2):
            cols = slice(ch * HALF_FF, (ch + 1) * HALF_FF)
            dg = dgc_ref[:, cols].astype(F32)
            nx = jnp.where(last, 0.0, nxt_ref[:, cols].astype(F32))
            row = lax.broadcasted_iota(jnp.int32, dg.shape, 0)
            n0, n1 = nx[0:1, :], nx[1:2, :]
            u1 = jnp.where(row == tm - 1, n0, pltpu.roll(dg, tm - 1, 0))
            u2 = jnp.where(row == tm - 1, n1, jnp.where(row == tm - 2, n0, pltpu.roll(dg, tm - 2, 0)))
            dgate = (dg * cw_ref[2:3, cols] + u1 * cw_ref[1:2, cols] + u2 * cw_ref[0:1, cols]).astype(BF16)
            dwu_ref[ch] += lax.dot_general(hb, dgate, TN, preferred_element_type=F32)
            dh = dh + lax.dot_general(dgate, w_ref[ch], NT, preferred_element_type=F32)
            dh = dh + lax.dot_general(dval_ref[:, cols], w_ref[2 + ch], NT, preferred_element_type=F32)
        x1 = x1_ref[...]
        rstd = lax.rsqrt(jnp.mean(x1 * x1, axis=-1, keepdims=True) + NORM_EPS)
        n2 = x1 * rstd
        gpre = gpre_ref[...]
        one_sc = 1.0 + mod_ref[4:5, :]
        _acc(dsh_ref, _colsum(dh))
        _acc(dsc_ref, _colsum(dh * (n2 * gpre)))
        _acc(dgpre_ref, _colsum(dh * one_sc * n2))
        dn = dh * (gpre * one_sc)
        dx1 = dout_ref[...] + rstd * (dn - n2 * jnp.mean(dn * n2, axis=-1, keepdims=True))
        dx1_ref[...] = dx1
        y1 = y1_ref[...]
        rstd1 = lax.rsqrt(jnp.mean(y1 * y1, axis=-1, keepdims=True) + NORM_EPS)
        y1n = y1 * rstd1
        gpost = gpost_ref[...]
        gtm = mod_ref[2:3, :]
        _acc(dgt_ref, _colsum(dx1 * (y1n * gpost)))
        dr1 = dx1 * gtm
        _acc(dgpost_ref, _colsum(dr1 * y1n))
        dyn = dr1 * gpost
        dy1 = rstd1 * (dyn - y1n * jnp.mean(dyn * y1n, axis=-1, keepdims=True))
        dy1_ref[...] = dy1.astype(BF16)

    vec = _full((1, D_MODEL))
    nxt = pl.BlockSpec((16, D_FF), lambda i: (jnp.minimum((i + 1) * (tm // 16), last_blk), 0))
    return pl.pallas_call(
        body,
        name="up_bwd",
        grid=(S // tm,),
        in_specs=[_rows(tm, D_FF), nxt, _rows(tm, D_FF), _full((3, D_FF)), _full(w_up_g.shape), _rows(tm, D_MODEL),
                  _rows(tm, D_MODEL), _rows(tm, D_MODEL), vec, vec, _full((6, D_MODEL)), _rows(tm, D_MODEL),
                  pl.BlockSpec(memory_space=pl.ANY)],
        out_specs=[_rows(tm, D_MODEL), _rows(tm, D_MODEL), vec, vec, vec, vec, vec,
                   pl.BlockSpec((2, D_MODEL, HALF_FF), lambda i: (0, 0, 0), pipeline_mode=pl.Buffered(1))],
        out_shape=[jax.ShapeDtypeStruct((S, D_MODEL), F32), jax.ShapeDtypeStruct((S, D_MODEL), BF16)]
        + [jax.ShapeDtypeStruct((1, D_MODEL), F32)] * 5 + [jax.ShapeDtypeStruct(dw_up.shape, F32)],
        input_output_aliases={12: 7},
        compiler_params=_params(1),
    )(dgc, dgc, dval, conv_w, w_up_g, x1, dout, y1, g_pre, g_post, mod6, h2, dw_up)


def outproj_bwd(dy1, w_out_g, cat, tm=512):
    S = dy1.shape[0]

    def body(dy_ref, w_ref, cat_ref, dpool_ref, dattn_ref, da4_ref, da16_ref, delta_ref, dl4_ref, dl16_ref, dw_ref, scr):
        @pl.when(pl.program_id(0) == 0)
        def _():
            dw_ref[...] = jnp.zeros_like(dw_ref)

        catb = cat_ref[...]
        dcat = jnp.zeros((tm, 512), F32)
        for j in range(N_CHIPS):
            dyj = dy_ref[:, j * 256:(j + 1) * 256]
            dcat = dcat + lax.dot_general(dyj, w_ref[j], NT, preferred_element_type=F32)
            dw_ref[j] += lax.dot_general(catb, dyj, TN, preferred_element_type=F32)
        dpool_ref[...] = dcat[:, :POOL_W]
        dattn = dcat[:, POOL_W:]
        dattn_ref[...] = dattn.astype(BF16)
        for h in range(2):
            scr[h] = dattn[:, h * 128:(h + 1) * 128]
        for d, dst in ((4, da4_ref), (16, da16_ref)):
            for r in range(d):
                for h in range(2):
                    dst[r, :, h * 128:(h + 1) * 128] = scr[h, pl.ds(r, tm // d, stride=d), :].astype(BF16)
        prod = dattn * catb[:, POOL_W:].astype(F32)
        r = lax.broadcasted_iota(jnp.int32, (GROUP_W, GROUP_W), 0) // HEAD_DIM
        c = lax.broadcasted_iota(jnp.int32, (GROUP_W, GROUP_W), 1) // HEAD_DIM
        ones_bd = jnp.where(r == c, 1.0, 0.0).astype(BF16)
        hi = prod.astype(BF16)
        lo = (prod - hi.astype(F32)).astype(BF16)
        delta = jnp.dot(hi, ones_bd, preferred_element_type=F32) + jnp.dot(lo, ones_bd, preferred_element_type=F32)
        delta_ref[...] = delta
        for h in range(2):
            scr[h] = delta[:, h * 128:(h + 1) * 128]
        for d, dst in ((4, dl4_ref), (16, dl16_ref)):
            for r in range(d):
                for h in range(2):
                    dst[r, :, h * 128:(h + 1) * 128] = scr[h, pl.ds(r, tm // d, stride=d), :]

    cls = lambda d: pl.BlockSpec((d, tm // d, GROUP_W), lambda i: (0, i, 0))
    cls_shape = lambda d, dt: jax.ShapeDtypeStruct((d, S // d, GROUP_W), dt)
    return pl.pallas_call(
        body,
        name="outproj_bwd",
        grid=(S // tm,),
        in_specs=[_rows(tm, D_MODEL), _full(w_out_g.shape), _rows(tm, 512)],
        out_specs=[_rows(tm, POOL_W), _rows(tm, GROUP_W), cls(4), cls(16), _rows(tm, GROUP_W), cls(4), cls(16),
                   _full(w_out_g.shape)],
        out_shape=[jax.ShapeDtypeStruct((S, POOL_W), F32), jax.ShapeDtypeStruct((S, GROUP_W), BF16), cls_shape(4, BF16),
                   cls_shape(16, BF16), jax.ShapeDtypeStruct((S, GROUP_W), F32), cls_shape(4, F32), cls_shape(16, F32),
                   jax.ShapeDtypeStruct(w_out_g.shape, F32)],
        scratch_shapes=[pltpu.VMEM((2, tm, 128), F32)],
        compiler_params=_params(1),
    )(dy1, w_out_g, cat)


def attn_bwd(qkv, dattn, lse, delta, d):
    L = qkv.shape[2]
    nb = L // ATT_BLOCK

    def body(q_ref, k_ref, v_ref, do_ref, l_ref, dl_ref, out_ref, kpad, vpad, dkpad, dvpad):
        kpad[0:ATT_BLOCK, :] = jnp.zeros((ATT_BLOCK, GROUP_W), BF16)
        vpad[0:ATT_BLOCK, :] = jnp.zeros((ATT_BLOCK, GROUP_W), BF16)
        kpad[ATT_BLOCK:, :] = k_ref[...]
        vpad[ATT_BLOCK:, :] = v_ref[...]
        dkpad[...] = jnp.zeros_like(dkpad)
        dvpad[...] = jnp.zeros_like(dvpad)
        band, col, lo = _attn_masks()

        def step(n, carry):
            r0 = pl.multiple_of(n * ATT_BLOCK, ATT_BLOCK)
            valid = band & ((col >= ATT_BLOCK) | (n > 0))
            qb = q_ref[pl.ds(r0, ATT_BLOCK), :]
            dob = do_ref[pl.ds(r0, ATT_BLOCK), :]
            lb = l_ref[pl.ds(r0, ATT_BLOCK), :]
            dlb = dl_ref[pl.ds(r0, ATT_BLOCK), :]
            kb = kpad[pl.ds(r0, 2 * ATT_BLOCK), :]
            vb = vpad[pl.ds(r0, 2 * ATT_BLOCK), :]
            for pair in range(2):
                lanes = slice(pair * 128, (pair + 1) * 128)
                qp, dop, kp, vp = qb[:, lanes], dob[:, lanes], kb[:, lanes], vb[:, lanes]
                dqs = []
                dk_sum = jnp.zeros((2 * ATT_BLOCK, 128), F32)
                dv_sum = jnp.zeros((2 * ATT_BLOCK, 128), F32)
                for hh in range(2):
                    sel = lo if hh == 0 else jnp.logical_not(lo)
                    qm = jnp.where(sel, qp, jnp.zeros_like(qp))
                    dom = jnp.where(sel, dop, jnp.zeros_like(dop))
                    cc = pair * 128 + hh * HEAD_DIM
                    s = lax.dot_general(qm, kp, NT, preferred_element_type=F32)
                    s = jnp.where(valid, s, NEG)
                    p = jnp.exp(s - lb[:, cc:cc + 1])
                    dp = lax.dot_general(dom, vp, NT, preferred_element_type=F32)
                    ds = (p * (dp - dlb[:, cc:cc + 1])).astype(BF16)
                    pb = p.astype(BF16)
                    dqs.append(jnp.dot(ds, kp, preferred_element_type=F32))
                    dk_sum = dk_sum + lax.dot_general(ds, qm, TN, preferred_element_type=F32)
                    dv_sum = dv_sum + lax.dot_general(pb, dom, TN, preferred_element_type=F32)
                out_ref[0, pl.ds(r0, ATT_BLOCK), lanes] = jnp.where(lo, dqs[0], dqs[1])
                dkpad[pl.ds(r0, 2 * ATT_BLOCK), lanes] += dk_sum
                dvpad[pl.ds(r0, 2 * ATT_BLOCK), lanes] += dv_sum
            return carry

        lax.fori_loop(0, nb, step, 0, unroll=min(4, nb))
        out_ref[1] = dkpad[ATT_BLOCK:, :]
        out_ref[2] = dvpad[ATT_BLOCK:, :]

    spec = lambda kind: pl.BlockSpec((None, None, L, GROUP_W), lambda r: (kind, r, 0, 0))
    cls = pl.BlockSpec((None, L, GROUP_W), lambda r: (r, 0, 0))
    return pl.pallas_call(
        body,
        name=f"attn_bwd_d{d}",
        grid=(d,),
        in_specs=[spec(0), spec(1), spec(2), cls, cls, cls],
        out_specs=pl.BlockSpec((3, None, L, GROUP_W), lambda r: (0, r, 0, 0)),
        out_shape=jax.ShapeDtypeStruct((3, d, L, GROUP_W), F32),
        scratch_shapes=[pltpu.VMEM((L + ATT_BLOCK, GROUP_W), BF16)] * 2 + [pltpu.VMEM((L + ATT_BLOCK, GROUP_W), F32)] * 2,
        compiler_params=_params(1),
    )(qkv, qkv, qkv, dattn, lse, delta)


def pool_bwd(dpool, mixed, wbd, b, scale):
    S = dpool.shape[0]

    def body(dp_ref, mx_ref, w_ref, b_ref, s_ref, du_ref, dw_ref, db_ref, ds_ref):
        dp = dp_ref[...]
        mb = mx_ref[...]
        wv = w_ref[...]
        ypre = jnp.dot(mb, wv, preferred_element_type=F32) + b_ref[...]
        ds_ref[...] = _colsum(dp * ypre)
        dpre = dp * s_ref[...]
        db_ref[...] = _colsum(dpre)
        dpb = dpre.astype(BF16)
        dw_ref[...] = lax.dot_general(mb, dpb, TN, preferred_element_type=F32)
        dmix = lax.dot_general(dpb, wv, NT, preferred_element_type=F32)
        row = lax.broadcasted_iota(jnp.int32, dmix.shape, 0)
        lane, win = _pool_lane_windows(dmix.shape)
        e = dmix / jnp.minimum(row + 1, win).astype(F32)

        def shift(a, k):
            return jnp.where(row < S - k, pltpu.roll(a, S - k, 0), 0.0)

        f2 = e + shift(e, 1)
        f4 = f2 + shift(f2, 2)
        f8 = f4 + shift(f4, 4)
        f16 = f8 + shift(f8, 8)
        du_ref[...] = jnp.where(lane < 64, f2, jnp.where(lane < 128, f4, jnp.where(lane < 192, f8, f16))) - dmix

    vm = pl.BlockSpec(memory_space=pltpu.VMEM)
    return pl.pallas_call(
        body,
        name="pool_bwd",
        in_specs=[vm] * 5,
        out_specs=[vm] * 4,
        out_shape=[jax.ShapeDtypeStruct((S, POOL_W), F32), jax.ShapeDtypeStruct((POOL_W, POOL_W), F32),
                   jax.ShapeDtypeStruct((1, POOL_W), F32), jax.ShapeDtypeStruct((1, POOL_W), F32)],
        compiler_params=_params(),
    )(dpool, mixed, wbd, b, scale)


def inproj_bwd(dqkv, du, x, dx1, w_in_g, g, mod6, tc, tsa, tsb, h1, tm=256):
    S = x.shape[0]

    def body(d0, d1, d2, du_ref, x_ref, dx1_ref, w_ref, g_ref, mod_ref, tc_ref, tsa_ref, tsb_ref, h_ref,
             gx_ref, dsh_ref, dsc_ref, dg_ref, dw_ref, s4, s16, dp_ref):
        @pl.when(pl.program_id(0) == 0)
        def _():
            dw_ref[...] = jnp.zeros_like(dw_ref)

        cs, sa, sb = tc_ref[...], tsa_ref[...], tsb_ref[...]
        for d, src, dst in ((4, d1, s4), (16, d2, s16)):
            for kind in range(3):
                for r in range(d):
                    for h in range(2):
                        dst[kind, h, pl.ds(r, tm // d, stride=d), :] = src[kind, r, :, h * 128:(h + 1) * 128]
        for sp in range(20):
            piece, half = sp // 2, sp % 2
            lanes = slice(half * 128, (half + 1) * 128)
            if piece == 0:
                blk = du_ref[:, lanes]
            else:
                kind, gi = (piece - 1) // 3, (piece - 1) % 3
                blk = d0[kind, 0, :, lanes] if gi == 0 else (s4, s16)[gi - 1][kind, half]
                if kind == 0:
                    blk = _rope128(blk, cs, sa, sb, -1.0) * (HEAD_DIM ** -0.5)
                elif kind == 1:
                    blk = _rope128(blk, cs, sa, sb, -1.0)
            dp_ref[:, sp * 128:(sp + 1) * 128] = blk.astype(BF16)
        dh = jnp.zeros((tm, D_MODEL), F32)
        hb = h_ref[...]
        for j in range(N_CHIPS):
            dpj = dp_ref[:, j * 640:(j + 1) * 640]
            dh = dh + lax.dot_general(dpj, w_ref[j], NT, preferred_element_type=F32)
            dw_ref[j] += lax.dot_general(hb, dpj, TN, preferred_element_type=F32)
        xv = x_ref[...]
        rstd = lax.rsqrt(jnp.mean(xv * xv, axis=-1, keepdims=True) + NORM_EPS)
        n1 = xv * rstd
        gv = g_ref[...]
        one_sc = 1.0 + mod_ref[1:2, :]
        _acc(dsh_ref, _colsum(dh))
        _acc(dsc_ref, _colsum(dh * (n1 * gv)))
        _acc(dg_ref, _colsum(dh * one_sc * n1))
        dn = dh * (gv * one_sc)
        gx_ref[...] = dx1_ref[...] + rstd * (dn - n1 * jnp.mean(dn * n1, axis=-1, keepdims=True))

    vec = _full((1, D_MODEL))
    dspec = lambda d: pl.BlockSpec((3, d, tm // d, GROUP_W), lambda i: (0, 0, i, 0))
    return pl.pallas_call(
        body,
        name="inproj_bwd",
        grid=(S // tm,),
        in_specs=[dspec(d) for d in DILATIONS] + [_rows(tm, POOL_W), _rows(tm, D_MODEL), _rows(tm, D_MODEL), _full(w_in_g.shape),
                                                  vec, _full((6, D_MODEL)), _rows(tm, 128), _rows(tm, 128), _rows(tm, 128),
                                                  _rows(tm, D_MODEL)],
        out_specs=[_rows(tm, D_MODEL), vec, vec, vec, _full(w_in_g.shape)],
        out_shape=[jax.ShapeDtypeStruct((S, D_MODEL), F32)] + [jax.ShapeDtypeStruct((1, D_MODEL), F32)] * 3
        + [jax.ShapeDtypeStruct(w_in_g.shape, F32)],
        scratch_shapes=[pltpu.VMEM((3, 2, tm, 128), F32)] * 2 + [pltpu.VMEM((tm, IN_W), BF16)],
        compiler_params=_params(1),
    )(*dqkv, du, x, dx1, w_in_g, g, mod6, tc, tsa, tsb, h1)


def _adamw(w, g, m, v):
    m = ADAM_B1 * m + (1.0 - ADAM_B1) * g
    v = ADAM_B2 * v + (1.0 - ADAM_B2) * (g * g)
    m_hat = m / (1.0 - ADAM_B1 ** ADAM_STEP)
    v_hat = v / (1.0 - ADAM_B2 ** ADAM_STEP)
    delta = -ADAM_LR * (m_hat / (jnp.sqrt(v_hat) + ADAM_EPS) + ADAM_WD * w)
    return delta, m, v


def adamw_rows(w, g, m, v, tr, name):
    R, C = w.shape

    def body(w_ref, g_ref, m_ref, v_ref, d_ref, mo_ref, vo_ref):
        d_ref[...], mo_ref[...], vo_ref[...] = _adamw(w_ref[...], g_ref[...], m_ref[...], v_ref[...])

    spec = pl.BlockSpec((tr, C), lambda i: (i, 0))
    return pl.pallas_call(
        body,
        name=name,
        grid=(R // tr,),
        in_specs=[spec] * 4,
        out_specs=[spec] * 3,
        out_shape=[jax.ShapeDtypeStruct((R, C), F32)] * 3,
        compiler_params=_params(1),
    )(w, g, m, v)


def adamw_ada(c_all_t, dmod_cols, w, m, v, tr=256):
    R, C = w.shape

    def body(ct_ref, dm_ref, w_ref, m_ref, v_ref, g_ref, d_ref, mo_ref, vo_ref):
        ct = ct_ref[...]
        act = ct * jax.nn.sigmoid(ct)
        g = jnp.zeros((tr, C), F32)
        for b in range(N_DEV):
            g = g + act[:, b:b + 1] * dm_ref[b:b + 1, :]
        g_ref[...] = g
        d_ref[...], mo_ref[...], vo_ref[...] = _adamw(w_ref[...], g, m_ref[...], v_ref[...])

    spec = pl.BlockSpec((tr, C), lambda i: (i, 0))
    return pl.pallas_call(
        body,
        name="adamw_ada",
        grid=(R // tr,),
        in_specs=[pl.BlockSpec((tr, N_DEV), lambda i: (i, 0)), _full((N_DEV, C)), spec, spec, spec],
        out_specs=[spec] * 4,
        out_shape=[jax.ShapeDtypeStruct((R, C), F32)] * 4,
        compiler_params=_params(1),
    )(c_all_t, dmod_cols, w, m, v)


def adamw_small(slab_a, slab_b, convw_g, wpool_g, params):
    names = ["b_ada", "g_pre_mix", "g_post_mix", "g_pre_ffn", "g_post_ffn", "b_pool", "pool_scale", "conv_b", "conv_w", "w_pool"]
    flat = []
    for n in names:
        flat += list(params[n])

    def body(a_ref, b_ref, cw_ref, wp_ref, *rest):
        ins, outs = rest[:30], rest[30:]

        def dev_sum(ref):
            t = ref[0]
            for dev in range(1, N_DEV):
                t = t + ref[dev]
            return t

        sa, sb_, scw, swp = dev_sum(a_ref), dev_sum(b_ref), dev_sum(cw_ref), dev_sum(wp_ref)
        grads = [
            jnp.concatenate([sa[k:k + 1, :] for k in range(6)], axis=1),
            sa[6:7, :], sa[7:8, :], sa[8:9, :], sa[9:10, :],
            sa[10:11, 0:256], sa[10:11, 256:512],
            sb_[3:4, :], scw, swp,
        ]
        for i, g in enumerate(grads):
            w_ref, m_ref, v_ref = ins[3 * i:3 * i + 3]
            d, mo, vo = _adamw(w_ref[...], g, m_ref[...], v_ref[...])
            outs[4 * i][...] = g
            outs[4 * i + 1][...] = d
            outs[4 * i + 2][...] = mo
            outs[4 * i + 3][...] = vo

    vm = pl.BlockSpec(memory_space=pltpu.VMEM)
    out_shape = []
    for n in names:
        out_shape += [jax.ShapeDtypeStruct(params[n][0].shape, F32)] * 4
    outs = pl.pallas_call(
        body,
        name="adamw_small",
        in_specs=[vm] * (4 + len(flat)),
        out_specs=[vm] * len(out_shape),
        out_shape=out_shape,
        compiler_params=_params(),
    )(slab_a, slab_b, convw_g, wpool_g, *flat)
    return {n: outs[4 * i:4 * i + 4] for i, n in enumerate(names)}


def _place():
    return lax.axis_index("x"), lax.axis_index("y"), lax.axis_index("c")


def _other_chips(x, y):
    return [(1 - x, y), (x, 1 - y), (1 - x, 1 - y)]


def _chip_id(cx, cy):
    return 2 * cx + cy


def gather_weights(shards):
    n = len(shards)
    halved = [s.shape[0] % 32 == 0 for s in shards]

    def body(*refs):
        ins, outs = refs[:n], refs[n:2 * n]
        send_sems, recv_sems, loc_sems = refs[2 * n:]
        x, y, c = _place()
        me = _chip_id(x, y)
        chips = _other_chips(x, y)
        sib = (x, y, 1 - c)

        def part(w, chip, half):
            if not halved[w]:
                return outs[w].at[chip]
            rh = shards[w].shape[0] // 2
            return outs[w].at[chip, pl.ds(half * rh, rh), :]

        def src_part(w):
            if not halved[w]:
                return ins[w]
            rh = shards[w].shape[0] // 2
            return ins[w].at[pl.ds(c * rh, rh), :]

        def rcopy(w, k, src, dst, to):
            return pltpu.make_async_remote_copy(src_ref=src, dst_ref=dst, send_sem=send_sems.at[6 * w + k],
                                                recv_sem=recv_sems.at[6 * w + k], device_id=to, device_id_type=MESH)

        local = [pltpu.make_async_copy(ins[w], outs[w].at[me], loc_sems.at[w]) for w in range(n)]
        for cp in local:
            cp.start()
        first = []
        for w in range(n):
            for k, (cx, cy) in enumerate(chips):
                cp = rcopy(w, k, src_part(w), part(w, me, c), (cx, cy, c))
                cp.start()
                first.append(cp)
        passed = []
        for w in range(n):
            for k, (cx, cy) in enumerate(chips):
                blk = part(w, _chip_id(cx, cy), c)
                rcopy(w, k, blk, blk, (cx, cy, c)).wait_recv()
                if halved[w]:
                    cp = rcopy(w, 3 + k, blk, blk, sib)
                    cp.start()
                    passed.append(cp)
        for w in range(n):
            if halved[w]:
                for k, (cx, cy) in enumerate(chips):
                    blk = part(w, _chip_id(cx, cy), 1 - c)
                    rcopy(w, 3 + k, blk, blk, sib).wait_recv()
        for cp in first + passed:
            cp.wait_send()
        for cp in local:
            cp.wait()

    hbm = pl.BlockSpec(memory_space=pl.ANY)
    return pl.pallas_call(
        body,
        name="gather_weights",
        in_specs=[hbm] * n,
        out_specs=[hbm] * n,
        out_shape=[jax.ShapeDtypeStruct((N_CHIPS,) + s.shape, s.dtype) for s in shards],
        scratch_shapes=[pltpu.SemaphoreType.DMA((6 * n,)), pltpu.SemaphoreType.DMA((6 * n,)), pltpu.SemaphoreType.DMA((n,))],
        compiler_params=pltpu.CompilerParams(has_side_effects=True, vmem_limit_bytes=VMEM_LIMIT),
    )(*shards)


HBM_SPEC = pl.BlockSpec(memory_space=pltpu.HBM)
SEM_SPEC = pl.BlockSpec(memory_space=pltpu.SEMAPHORE)
ANY_SPEC = pl.BlockSpec(memory_space=pl.ANY)
EFFECT = pltpu.SideEffectType.DATAFLOW_SIDE_EFFECTING


def _hbm(t):
    return pltpu.with_memory_space_constraint(t, pltpu.HBM)


def _hbm_shapes(ts):
    return [pltpu.HBM(t.shape, t.dtype) for t in ts]


def _half_rows(ref, lead, half, rh):
    return ref.at[lead, pl.ds(half * rh, rh), :]


def gather_split_start(shards, lands, carry, k):
    n = len(shards)

    def body(*refs):
        ins, land = refs[:n], refs[n:2 * n]
        send_sems, recv_sems = refs[2 * n + 1], refs[2 * n + 2]
        loc_sems = refs[-1]
        x, y, c = _place()
        me = _chip_id(x, y)
        if k == 0:
            local = [pltpu.make_async_copy(ins[w], land[w].at[me], loc_sems.at[w]) for w in range(n)]
            for cp in local:
                cp.start()
            for cp in local:
                cp.wait()
        cx, cy = _other_chips(x, y)[k]
        for w in range(n):
            rh = shards[w].shape[0] // 2
            pltpu.make_async_remote_copy(src_ref=ins[w].at[pl.ds(c * rh, rh), :], dst_ref=_half_rows(land[w], me, c, rh),
                                         send_sem=send_sems.at[w], recv_sem=recv_sems.at[w],
                                         device_id=(cx, cy, c), device_id_type=MESH).start()

    args = [_hbm(s) for s in shards] + [_hbm(l) for l in lands] + [_hbm(carry)]
    outs = pl.pallas_call(
        body,
        name="gather_split_start%d" % k,
        out_shape=[pltpu.SemaphoreType.DMA((n,)), pltpu.SemaphoreType.DMA((n,))] + _hbm_shapes(shards) + _hbm_shapes(lands)
        + _hbm_shapes([carry]),
        in_specs=[HBM_SPEC] * (2 * n + 1),
        out_specs=[SEM_SPEC, SEM_SPEC] + [HBM_SPEC] * (2 * n + 1),
        input_output_aliases={i: 2 + i for i in range(2 * n + 1)},
        scratch_shapes=[pltpu.SemaphoreType.DMA((n,))],
        compiler_params=pltpu.CompilerParams(has_side_effects=EFFECT),
    )(*args)
    return outs[0], outs[1], list(outs[2:2 + n]), list(outs[2 + n:2 + 2 * n]), outs[-1]


def gather_split_mid(sems, shards, lands, after):
    n = len(shards)

    def body(*refs):
        ins, land = refs[:n], refs[n:2 * n]
        sem_in = refs[2 * n:2 * n + 6]
        fsend, frecv = refs[2 * n + 7], refs[2 * n + 8]
        x, y, c = _place()
        me = _chip_id(x, y)
        chips = _other_chips(x, y)
        for w in range(n):
            rh = shards[w].shape[0] // 2
            for k, (cx, cy) in enumerate(chips):
                got = _half_rows(land[w], _chip_id(cx, cy), c, rh)
                cp = pltpu.make_async_remote_copy(src_ref=ins[w].at[pl.ds(c * rh, rh), :], dst_ref=got, send_sem=sem_in[2 * k].at[w],
                                                  recv_sem=sem_in[2 * k + 1].at[w], device_id=(cx, cy, c), device_id_type=MESH)
                cp.wait_send()
                cp.wait_recv()
        for w in range(n):
            rh = shards[w].shape[0] // 2
            for k, (cx, cy) in enumerate(chips):
                got = _half_rows(land[w], _chip_id(cx, cy), c, rh)
                pltpu.make_async_remote_copy(src_ref=got, dst_ref=got, send_sem=fsend.at[3 * w + k], recv_sem=frecv.at[3 * w + k],
                                             device_id=(x, y, 1 - c), device_id_type=MESH).start()

    outs = pl.pallas_call(
        body,
        name="gather_split_mid",
        out_shape=[pltpu.SemaphoreType.DMA((3 * n,)), pltpu.SemaphoreType.DMA((3 * n,))] + _hbm_shapes(lands),
        in_specs=[HBM_SPEC] * (2 * n) + [SEM_SPEC] * 6 + [ANY_SPEC],
        out_specs=[SEM_SPEC, SEM_SPEC] + [HBM_SPEC] * n,
        input_output_aliases={n + i: 2 + i for i in range(n)},
        compiler_params=pltpu.CompilerParams(has_side_effects=EFFECT),
    )(*shards, *lands, *sems, after)
    return outs[0], outs[1], list(outs[2:])


def gather_split_done(fsend, frecv, lands, after):
    n = len(lands)

    def body(*refs):
        land = refs[:n]
        ssem, rsem = refs[n], refs[n + 1]
        x, y, c = _place()
        for w in range(n):
            rh = lands[w].shape[1] // 2
            for k, (cx, cy) in enumerate(_other_chips(x, y)):
                sent = _half_rows(land[w], _chip_id(cx, cy), c, rh)
                got = _half_rows(land[w], _chip_id(cx, cy), 1 - c, rh)
                cp = pltpu.make_async_remote_copy(src_ref=sent, dst_ref=got, send_sem=ssem.at[3 * w + k], recv_sem=rsem.at[3 * w + k],
                                                  device_id=(x, y, 1 - c), device_id_type=MESH)
                cp.wait_send()
                cp.wait_recv()

    outs = pl.pallas_call(
        body,
        name="gather_split_done",
        out_shape=_hbm_shapes(lands),
        in_specs=[HBM_SPEC] * n + [SEM_SPEC, SEM_SPEC, ANY_SPEC],
        out_specs=[HBM_SPEC] * n,
        input_output_aliases={i: i for i in range(n)},
        compiler_params=pltpu.CompilerParams(has_side_effects=EFFECT),
    )(*lands, fsend, frecv, after)
    return list(outs)


def _flips():
    return [(fx, fy, fc) for fx in (0, 1) for fy in (0, 1) for fc in (0, 1)][1:]


def _flip(v, f):
    return v if f == 0 else 1 - v


def ada_mod(c3, w_ada, b_cols, conv_w):
    CB = w_ada.shape[1]

    def body(c_ref, w_ref, b_ref, cw_ref, call_ref, mod_ref, cwall_ref, modall, send_sems, recv_sems):
        x, y, c = _place()
        me_dev = 4 * x + 2 * y + c
        me = _chip_id(x, y)
        call_ref[me_dev] = c_ref[0]
        cwall_ref[me] = cw_ref[...]
        sends = []
        for k, (cx, cy) in enumerate(_other_chips(x, y)):
            cp = pltpu.make_async_remote_copy(src_ref=cw_ref, dst_ref=cwall_ref.at[me], send_sem=send_sems.at[10 + k],
                                              recv_sem=recv_sems.at[10 + k], device_id=(cx, cy, c), device_id_type=MESH)
            cp.start()
            sends.append(cp)
        for k, (fx, fy, fc) in enumerate(_flips()):
            cp = pltpu.make_async_remote_copy(src_ref=c_ref.at[0], dst_ref=call_ref.at[me_dev], send_sem=send_sems.at[k],
                                              recv_sem=recv_sems.at[k],
                                              device_id=(_flip(x, fx), _flip(y, fy), _flip(c, fc)), device_id_type=MESH)
            cp.start()
            sends.append(cp)
        for k, (fx, fy, fc) in enumerate(_flips()):
            peer = 4 * _flip(x, fx) + 2 * _flip(y, fy) + _flip(c, fc)
            pltpu.make_async_remote_copy(src_ref=c_ref.at[0], dst_ref=call_ref.at[peer], send_sem=send_sems.at[k],
                                         recv_sem=recv_sems.at[k], device_id=(x, y, c), device_id_type=MESH).wait_recv()
        row = lax.broadcasted_iota(jnp.int32, (N_DEV, D_MODEL), 0)
        call = jnp.zeros((N_DEV, D_MODEL), F32)
        for dev in range(N_DEV):
            call = jnp.where(row == dev, call_ref[dev], call)
        act = call * jax.nn.sigmoid(call)
        modall[me] = jnp.dot(act, w_ref[...], preferred_element_type=F32, precision=lax.Precision.HIGHEST) + b_ref[...]
        for k, (cx, cy) in enumerate(_other_chips(x, y)):
            cp = pltpu.make_async_remote_copy(src_ref=modall.at[me], dst_ref=modall.at[me], send_sem=send_sems.at[7 + k],
                                              recv_sem=recv_sems.at[7 + k], device_id=(cx, cy, c), device_id_type=MESH)
            cp.start()
            sends.append(cp)
        for k, (cx, cy) in enumerate(_other_chips(x, y)):
            blk = modall.at[_chip_id(cx, cy)]
            pltpu.make_async_remote_copy(src_ref=blk, dst_ref=blk, send_sem=send_sems.at[7 + k], recv_sem=recv_sems.at[7 + k],
                                         device_id=(x, y, c), device_id_type=MESH).wait_recv()
        for k, (cx, cy) in enumerate(_other_chips(x, y)):
            blk = cwall_ref.at[_chip_id(cx, cy)]
            pltpu.make_async_remote_copy(src_ref=blk, dst_ref=blk, send_sem=send_sems.at[10 + k], recv_sem=recv_sems.at[10 + k],
                                         device_id=(x, y, c), device_id_type=MESH).wait_recv()
        for cp in sends:
            cp.wait_send()
        mine = [modall[j, pl.ds(me_dev, 1), :] for j in range(N_CHIPS)]
        for r in range(6):
            pieces = []
            for h in range(2):
                pos = r * D_MODEL + h * 512
                pieces.append(mine[pos // CB][:, pos % CB:pos % CB + 512])
            mod_ref[r:r + 1, :] = jnp.concatenate(pieces, axis=1)

    vm = pl.BlockSpec(memory_space=pltpu.VMEM)
    return pl.pallas_call(
        body,
        name="ada_mod",
        in_specs=[vm] * 4,
        out_specs=[vm] * 3,
        out_shape=[jax.ShapeDtypeStruct((N_DEV, 1, D_MODEL), F32), jax.ShapeDtypeStruct((6, D_MODEL), F32),
                   jax.ShapeDtypeStruct((N_CHIPS,) + conv_w.shape, F32)],
        scratch_shapes=[pltpu.VMEM((N_CHIPS, N_DEV, CB), F32), pltpu.SemaphoreType.DMA((13,)), pltpu.SemaphoreType.DMA((13,))],
        compiler_params=pltpu.CompilerParams(has_side_effects=True, vmem_limit_bytes=VMEM_LIMIT),
    )(c3, w_ada, b_cols, conv_w)


def gather_small(blocks):
    n = len(blocks)

    def body(*refs):
        ins, outs = refs[:n], refs[n:2 * n]
        send_sems, recv_sems = refs[2 * n:]
        x, y, c = _place()
        sib = (x, y, 1 - c)
        chips = _other_chips(x, y)

        def dev(px, py, pc):
            return 4 * px + 2 * py + pc

        def cp(w, k, src, block_dev, to):
            return pltpu.make_async_remote_copy(src_ref=src, dst_ref=outs[w].at[block_dev], send_sem=send_sems.at[7 * w + k],
                                                recv_sem=recv_sems.at[7 * w + k], device_id=to, device_id_type=MESH)

        me = dev(x, y, c)
        started = []
        for w in range(n):
            outs[w][me] = ins[w][...]
            t = cp(w, 0, ins[w], me, sib)
            t.start()
            started.append(t)
            for k, (cx, cy) in enumerate(chips):
                t = cp(w, 1 + k, ins[w], me, (cx, cy, c))
                t.start()
                started.append(t)
        for w in range(n):
            for k, (cx, cy) in enumerate(chips):
                b = dev(cx, cy, c)
                cp(w, 1 + k, outs[w].at[b], b, (x, y, c)).wait_recv()
                t = cp(w, 4 + k, outs[w].at[b], b, sib)
                t.start()
                started.append(t)
        for w in range(n):
            b = dev(x, y, 1 - c)
            cp(w, 0, outs[w].at[b], b, (x, y, c)).wait_recv()
            for k, (cx, cy) in enumerate(chips):
                b = dev(cx, cy, 1 - c)
                cp(w, 4 + k, outs[w].at[b], b, (x, y, c)).wait_recv()
        for t in started:
            t.wait_send()

    vm = pl.BlockSpec(memory_space=pltpu.VMEM)
    return pl.pallas_call(
        body,
        name="gather_small",
        in_specs=[vm] * n,
        out_specs=[vm] * n,
        out_shape=[jax.ShapeDtypeStruct((N_DEV,) + b.shape, b.dtype) for b in blocks],
        scratch_shapes=[pltpu.SemaphoreType.DMA((7 * n,)), pltpu.SemaphoreType.DMA((7 * n,))],
        compiler_params=pltpu.CompilerParams(has_side_effects=True, vmem_limit_bytes=VMEM_LIMIT),
    )(*blocks)


def reduce_scatter_grads(grads, chunk_rows):
    n = len(grads)
    shapes = [g.shape[1:] for g in grads]
    halves = [s[0] // 2 for s in shapes]

    def body(*refs):
        gin = refs[:n]
        gout = refs[n:2 * n]
        sibbuf = refs[2 * n:3 * n]
        rest = refs[3 * n:]
        rbuf = rest[:n]
        pown = rest[n:2 * n]
        stage_a, stage_b, stage_o, stage_f = rest[2 * n:2 * n + 4]
        sib_send, sib_recv, ici_send, ici_recv, fin_send, fin_recv, ld_sems, st_sems = rest[2 * n + 4:]
        x, y, c = _place()
        me = _chip_id(x, y)
        chips = _other_chips(x, y)
        sib = (x, y, 1 - c)

        to_sib = []
        for w in range(n):
            rh = halves[w]
            cp = pltpu.make_async_remote_copy(src_ref=gin[w].at[:, pl.ds((1 - c) * rh, rh), :], dst_ref=sibbuf[w],
                                              send_sem=sib_send.at[w], recv_sem=sib_recv.at[w], device_id=sib,
                                              device_id_type=MESH)
            cp.start()
            to_sib.append(cp)

        sent = []
        for w in range(n):
            rh, cw = halves[w], shapes[w][1]
            ch = chunk_rows[w]
            to_sib[w].wait_recv()
            for k in range(4):
                chip = me if k == 3 else _chip_id(*chips[k])
                for r0 in range(0, rh, ch):
                    la = pltpu.make_async_copy(gin[w].at[chip, pl.ds(c * rh + r0, ch), :], stage_a.at[0:ch, 0:cw], ld_sems.at[0])
                    lb = pltpu.make_async_copy(sibbuf[w].at[chip, pl.ds(r0, ch), :], stage_b.at[0:ch, 0:cw], ld_sems.at[1])
                    la.start()
                    lb.start()
                    la.wait()
                    lb.wait()
                    tot = stage_a[0:ch, 0:cw] + stage_b[0:ch, 0:cw]
                    if k == 3:
                        pown[w][r0:r0 + ch, :] = tot
                    else:
                        stage_o[0:ch, 0:cw] = tot.astype(BF16)
                        cx, cy = chips[k]
                        cp = pltpu.make_async_remote_copy(src_ref=stage_o.at[0:ch, 0:cw], dst_ref=rbuf[w].at[k, r0:r0 + ch, :],
                                                          send_sem=ici_send.at[3 * w + k], recv_sem=ici_recv.at[3 * w + k],
                                                          device_id=(cx, cy, c), device_id_type=MESH)
                        cp.start()
                        cp.wait_send()
            sent.append(w)

        fin = []
        for w in range(n):
            rh, cw = halves[w], shapes[w][1]
            for k in range(3):
                whole = rbuf[w].at[k]
                pltpu.make_async_remote_copy(src_ref=whole, dst_ref=whole, send_sem=ici_send.at[3 * w + k],
                                             recv_sem=ici_recv.at[3 * w + k], device_id=(x, y, c),
                                             device_id_type=MESH).wait_recv()
            pown[w][...] = ((pown[w][...] + rbuf[w][0].astype(F32)) + rbuf[w][1].astype(F32)) + rbuf[w][2].astype(F32)
            mine = gout[w].at[pl.ds(c * rh, rh), :]
            st = pltpu.make_async_copy(pown[w], mine, st_sems.at[w])
            st.start()
            cp = pltpu.make_async_remote_copy(src_ref=pown[w], dst_ref=mine, send_sem=fin_send.at[w], recv_sem=fin_recv.at[w],
                                              device_id=sib, device_id_type=MESH)
            cp.start()
            fin.append((st, cp))
        for w in range(n):
            rh = halves[w]
            theirs = gout[w].at[pl.ds((1 - c) * rh, rh), :]
            pltpu.make_async_remote_copy(src_ref=theirs, dst_ref=theirs, send_sem=fin_send.at[w], recv_sem=fin_recv.at[w],
                                         device_id=(x, y, c), device_id_type=MESH).wait_recv()
        for cp in to_sib:
            cp.wait_send()
        for st, cp in fin:
            st.wait()
            cp.wait_send()

    hbm = pl.BlockSpec(memory_space=pl.ANY)
    max_ch = max(chunk_rows)
    max_c = max(s[1] for s in shapes)
    outs = pl.pallas_call(
        body,
        name="reduce_scatter_grads",
        in_specs=[hbm] * n,
        out_specs=[hbm] * (2 * n),
        out_shape=[jax.ShapeDtypeStruct(s, F32) for s in shapes]
        + [jax.ShapeDtypeStruct((N_CHIPS, h, s[1]), F32) for h, s in zip(halves, shapes)],
        scratch_shapes=[pltpu.VMEM((3, h, s[1]), BF16) for h, s in zip(halves, shapes)]
        + [pltpu.VMEM((h, s[1]), F32) for h, s in zip(halves, shapes)]
        + [pltpu.VMEM((max_ch, max_c), F32), pltpu.VMEM((max_ch, max_c), F32), pltpu.VMEM((max_ch, max_c), BF16),
           pltpu.VMEM((8, 128), F32)]
        + [pltpu.SemaphoreType.DMA((n,)), pltpu.SemaphoreType.DMA((n,)), pltpu.SemaphoreType.DMA((3 * n,)),
           pltpu.SemaphoreType.DMA((3 * n,)), pltpu.SemaphoreType.DMA((n,)), pltpu.SemaphoreType.DMA((n,)),
           pltpu.SemaphoreType.DMA((2,)), pltpu.SemaphoreType.DMA((n,))],
        compiler_params=pltpu.CompilerParams(has_side_effects=True, vmem_limit_bytes=VMEM_LIMIT),
    )(*grads)
    return outs[:n]


def split_start(name, bufs, plan, n_sem, carry):
    nb = len(bufs)

    def body(*refs):
        x, y, c = _place()
        ssem, rsem = refs[nb + 1], refs[nb + 2]
        for i, (src, dst, dev) in enumerate(plan(refs[:nb], x, y, c)):
            pltpu.make_async_remote_copy(src_ref=src, dst_ref=dst, send_sem=ssem.at[i], recv_sem=rsem.at[i], device_id=dev,
                                         device_id_type=MESH).start()

    alls = list(bufs) + [carry]
    outs = pl.pallas_call(
        body,
        name=name,
        out_shape=[pltpu.SemaphoreType.DMA((n_sem,)), pltpu.SemaphoreType.DMA((n_sem,))] + _hbm_shapes(alls),
        in_specs=[HBM_SPEC] * (nb + 1),
        out_specs=[SEM_SPEC, SEM_SPEC] + [HBM_SPEC] * (nb + 1),
        input_output_aliases={i: 2 + i for i in range(nb + 1)},
        compiler_params=pltpu.CompilerParams(has_side_effects=EFFECT),
    )(*[_hbm(t) for t in alls])
    return outs[0], outs[1], list(outs[2:2 + nb]), outs[-1]


def split_wait(name, ssem, rsem, bufs, plan, after):
    nb = len(bufs)

    def body(*refs):
        x, y, c = _place()
        s_ref, r_ref = refs[nb], refs[nb + 1]
        for i, (src, dst, dev) in enumerate(plan(refs[:nb], x, y, c)):
            cp = pltpu.make_async_remote_copy(src_ref=src, dst_ref=dst, send_sem=s_ref.at[i], recv_sem=r_ref.at[i], device_id=dev,
                                              device_id_type=MESH)
            cp.wait_send()
            cp.wait_recv()

    outs = pl.pallas_call(
        body,
        name=name,
        out_shape=_hbm_shapes(bufs),
        in_specs=[HBM_SPEC] * nb + [SEM_SPEC, SEM_SPEC, ANY_SPEC],
        out_specs=[HBM_SPEC] * nb,
        input_output_aliases={i: i for i in range(nb)},
        compiler_params=pltpu.CompilerParams(has_side_effects=EFFECT),
    )(*bufs, ssem, rsem, after)
    return list(outs)


def _gather_ici_plan(n):
    def plan(refs, x, y, c):
        out = []
        for w in range(n):
            rh = refs[w].shape[0] // 2
            for cx, cy in _other_chips(x, y):
                out.append((refs[w].at[pl.ds(c * rh, rh), :], _half_rows(refs[n + w], _chip_id(x, y), c, rh), (cx, cy, c)))
        return out

    return plan


def _gather_d2d_plan(n):
    def plan(refs, x, y, c):
        out = []
        for w in range(n):
            rh = refs[w].shape[1] // 2
            for cx, cy in _other_chips(x, y):
                blk = _half_rows(refs[w], _chip_id(cx, cy), c, rh)
                out.append((blk, blk, (x, y, 1 - c)))
        return out

    return plan


def _rs_d2d_plan(n):
    def plan(refs, x, y, c):
        out = []
        for w in range(n):
            rh = refs[w].shape[1] // 2
            out.append((refs[w].at[:, pl.ds((1 - c) * rh, rh), :], refs[n + w], (x, y, 1 - c)))
        return out

    return plan


def _rs_ici_plan(n):
    def plan(refs, x, y, c):
        out = []
        for w in range(n):
            for k, (cx, cy) in enumerate(_other_chips(x, y)):
                out.append((refs[w].at[_chip_id(cx, cy)], refs[n + w].at[k], (cx, cy, c)))
        return out

    return plan


def _rs_share_plan(n):
    def plan(refs, x, y, c):
        out = []
        for w in range(n):
            rh = refs[w].shape[0] // 2
            rows = refs[w].at[pl.ds(c * rh, rh), :]
            out.append((rows, rows, (x, y, 1 - c)))
        return out

    return plan


def rs_add(grad, sibbuf, place, tr, name):
    _, R, C = grad.shape
    nt = (R // 2) // tr

    def body(p_ref, g_ref, s_ref, o_ref):
        o_ref[...] = (g_ref[...] + s_ref[...]).astype(BF16)

    return pl.pallas_call(
        body,
        name=name,
        grid_spec=pltpu.PrefetchScalarGridSpec(
            num_scalar_prefetch=1,
            grid=(N_CHIPS, nt),
            in_specs=[pl.BlockSpec((None, tr, C), lambda j, i, p: (j, p[0] * nt + i, 0)),
                      pl.BlockSpec((None, tr, C), lambda j, i, p: (j, i, 0))],
            out_specs=pl.BlockSpec((None, tr, C), lambda j, i, p: (j, i, 0)),
        ),
        out_shape=jax.ShapeDtypeStruct((N_CHIPS, R // 2, C), BF16),
        compiler_params=_params(2),
    )(place, grad, sibbuf)


def rs_final(grad, sibbuf, rbuf, place, tr, name):
    _, R, C = grad.shape
    nt = (R // 2) // tr

    def body(p_ref, g_ref, s_ref, r_ref, o_ref):
        o_ref[...] = (((g_ref[...] + s_ref[...]) + r_ref[0].astype(F32)) + r_ref[1].astype(F32)) + r_ref[2].astype(F32)

    return pl.pallas_call(
        body,
        name=name,
        grid_spec=pltpu.PrefetchScalarGridSpec(
            num_scalar_prefetch=1,
            grid=(nt,),
            in_specs=[pl.BlockSpec((None, tr, C), lambda i, p: (p[1], p[0] * nt + i, 0)),
                      pl.BlockSpec((None, tr, C), lambda i, p: (p[1], i, 0)),
                      pl.BlockSpec((3, tr, C), lambda i, p: (0, i, 0))],
            out_specs=pl.BlockSpec((tr, C), lambda i, p: (p[0] * nt + i, 0)),
        ),
        out_shape=jax.ShapeDtypeStruct((R, C), F32),
        compiler_params=_params(1),
    )(place, grad, sibbuf, rbuf)


class GradReduce:
    def __init__(self, tag, grads, rows, place):
        self.tag, self.grads, self.rows, self.place = tag, grads, rows, place
        self.n = len(grads)

    def d2d_start(self, carry):
        sib = [lax.empty((N_CHIPS, g.shape[1] // 2, g.shape[2]), F32) for g in self.grads]
        self.s1, self.r1, bufs, carry = split_start(f"rs_{self.tag}_d2d_start", self.grads + sib, _rs_d2d_plan(self.n), self.n, carry)
        self.bufs1 = bufs
        return carry

    def add_and_ici_start(self, after, carry):
        bufs = split_wait(f"rs_{self.tag}_d2d_wait", self.s1, self.r1, self.bufs1, _rs_d2d_plan(self.n), after)
        self.grads, self.sib = bufs[:self.n], bufs[self.n:]
        pb = [rs_add(g, s, self.place, tr, f"rs_{self.tag}_add{w}")
              for w, (g, s, tr) in enumerate(zip(self.grads, self.sib, self.rows))]
        rb = [lax.empty((3,) + p.shape[1:], BF16) for p in pb]
        self.s2, self.r2, self.bufs2, carry = split_start(f"rs_{self.tag}_ici_start", pb + rb, _rs_ici_plan(self.n), 3 * self.n, carry)
        return carry

    def final_and_share_start(self, after, carry):
        bufs = split_wait(f"rs_{self.tag}_ici_wait", self.s2, self.r2, self.bufs2, _rs_ici_plan(self.n), after)
        rb = bufs[self.n:]
        full = [rs_final(g, s, r, self.place, tr, f"rs_{self.tag}_final{w}")
                for w, (g, s, r, tr) in enumerate(zip(self.grads, self.sib, rb, self.rows))]
        self.s3, self.r3, self.bufs3, carry = split_start(f"rs_{self.tag}_share_start", full, _rs_share_plan(self.n), self.n, carry)
        return carry

    def finish(self, after):
        return split_wait(f"rs_{self.tag}_share_wait", self.s3, self.r3, self.bufs3, _rs_share_plan(self.n), after)


def _rope_tables(positions):
    inv_freq = ROPE_THETA ** (-jnp.arange(0, ROT_DIM, 2, dtype=F32) / ROT_DIM)
    ang = positions.astype(F32)[:, None] * inv_freq
    cos, sin = jnp.cos(ang), jnp.sin(ang)
    S = positions.shape[0]
    one, zero = jnp.ones((S, 48), F32), jnp.zeros((S, 48), F32)
    z8 = jnp.zeros((S, 8), F32)
    tc = jnp.concatenate([cos, cos, one], axis=1)
    tsa = jnp.concatenate([z8, sin, zero], axis=1)
    tsb = jnp.concatenate([-sin, z8, zero], axis=1)
    return tuple(jnp.tile(t, (1, 2)) for t in (tc, tsa, tsb))


def _block_diag(w_pool):
    wbd = jnp.zeros((POOL_W, POOL_W), F32)
    for gi in range(4):
        wbd = wbd.at[gi * 64:(gi + 1) * 64, gi * 64:(gi + 1) * 64].set(w_pool[gi])
    return wbd


def kernel(x, c, positions, w_ada, b_ada, g_pre_mix, g_post_mix, g_pre_ffn, g_post_ffn, w_in, w_pool, b_pool, pool_scale, w_out, w_up, conv_w, conv_b, w_down, loss_target, m_w_ada, m_b_ada, m_g_pre_mix, m_g_post_mix, m_g_pre_ffn, m_g_post_ffn, m_w_in, m_w_pool, m_b_pool, m_pool_scale, m_w_out, m_w_up, m_conv_w, m_conv_b, m_w_down, v_w_ada, v_b_ada, v_g_pre_mix, v_g_post_mix, v_g_pre_ffn, v_g_post_ffn, v_w_in, v_w_pool, v_b_pool, v_pool_scale, v_w_out, v_w_up, v_conv_w, v_conv_b, v_w_down):
    xi, yi, ci = lax.axis_index("x"), lax.axis_index("y"), lax.axis_index("c")
    chip = 2 * xi + yi
    place = jnp.stack([ci, chip]).astype(jnp.int32)
    x2, tgt = x[0], loss_target[0]
    S = x2.shape[0]

    cb_ada = w_ada.shape[2]
    b_cols = lax.dynamic_slice(b_ada, (0, chip * cb_ada), (1, cb_ada))
    c_all, mod6, conv_w_g = ada_mod(c.reshape(1, 1, D_MODEL), w_ada[0], b_cols, conv_w[0])
    conv_w_f = jnp.transpose(conv_w_g, (1, 0, 2)).reshape(3, D_FF)

    def landing(s_):
        return lax.dynamic_update_slice(lax.empty((N_CHIPS,) + s_.shape, s_.dtype), s_[None], (chip, 0, 0))

    mix_sh = [w_in[0].astype(BF16), w_out[0].astype(BF16)]
    ffn_sh = [w_up[0].astype(BF16), w_down[0].astype(BF16)]
    ga_s, ga_r, ga_bufs, mod6 = split_start("gather_mix_ici_start", mix_sh + [landing(t) for t in mix_sh], _gather_ici_plan(2), 6, mod6)
    gb_s, gb_r, gb_bufs, mod6 = split_start("gather_ffn_ici_start", ffn_sh + [landing(t) for t in ffn_sh], _gather_ici_plan(2), 6, mod6)
    tc, tsa, tsb = _rope_tables(positions[0])
    wbd = _block_diag(w_pool[0]).astype(BF16)
    b_pool2, scale2 = b_pool.reshape(1, POOL_W), pool_scale
    ga_bufs = split_wait("gather_mix_ici_wait", ga_s, ga_r, ga_bufs, _gather_ici_plan(2), tc)
    gc_s, gc_r, mix_land, mod6 = split_start("gather_mix_d2d_start", ga_bufs[2:], _gather_d2d_plan(2), 6, mod6)
    w_in_g, w_out_g = split_wait("gather_mix_d2d_wait", gc_s, gc_r, mix_land, _gather_d2d_plan(2), mod6)

    h1, u, *qkv = inproj_fwd(x2, g_pre_mix, mod6, w_in_g, tc, tsa, tsb)
    mixed, pool = pool_fwd(u, wbd, b_pool2, scale2)
    o_l = [attn_fwd(t, d) for t, d in zip(qkv, DILATIONS)]
    attn_done = sum(l[0, :8, :128] for _, l in o_l)
    gb_bufs = split_wait("gather_ffn_ici_wait", gb_s, gb_r, gb_bufs, _gather_ici_plan(2), attn_done)
    gd_s, gd_r, ffn_land, pool = split_start("gather_ffn_d2d_start", gb_bufs[2:], _gather_d2d_plan(2), 6, pool)
    cat, lse, lse4, lse16, y1, x1, h2 = outproj_fwd([o for o, _ in o_l] + [l for _, l in o_l], pool, x2, w_out_g, g_post_mix,
                                                    g_pre_ffn, mod6)
    lses = [lse[None], lse4, lse16]
    w_up_g, w_down_g = split_wait("gather_ffn_d2d_wait", gd_s, gd_r, ffn_land, _gather_d2d_plan(2), h2)
    w_down_f = w_down_g.reshape(D_FF, D_MODEL)
    gate, val = up_fwd(h2, w_up_g)
    a, dy2, dout, loss_v, d_gt_f, d_g_post_ffn = down_fwd(gate, val, conv_w_f, conv_b, w_down_f, x1, tgt, g_post_ffn, mod6)

    dgc, dval, d_conv_w, d_conv_b, dw_down, dw_up = down_bwd(dy2, w_down_f, gate, val, conv_w_f, conv_b, a, h2)
    dx1, dy1, d_sh_f, d_sc_f, d_g_pre_ffn, d_gt_m, d_g_post_mix, dw_up = up_bwd(
        dgc, dval, conv_w_f, w_up_g, x1, dout, y1, g_pre_ffn, g_post_mix, mod6, h2, dw_up)
    rs_ffn = GradReduce("ffn", [dw_up, dw_down.reshape(N_CHIPS, D_FF // N_CHIPS, D_MODEL)], [256, 176], place)
    dy1 = rs_ffn.d2d_start(dy1)
    dpool, da1, da4, da16, dl1, dl4, dl16, dw_out = outproj_bwd(dy1, w_out_g, cat)
    dpool = rs_ffn.add_and_ici_start(dw_out, dpool)
    du, d_wbd, d_b_pool, d_scale = pool_bwd(dpool, mixed, wbd, b_pool2, scale2)
    dqkv = [attn_bwd(t, da, ls, dl, d) for t, da, ls, dl, d in zip(qkv, (da1[None], da4, da16), lses, (dl1[None], dl4, dl16), DILATIONS)]
    grad_x, d_sh_m, d_sc_m, d_g_pre_mix, dw_in = inproj_bwd(dqkv, du, x2, dx1, w_in_g, g_pre_mix, mod6, tc, tsa, tsb, h1)

    z1 = jnp.zeros((1, D_MODEL), F32)
    slab_a = jnp.concatenate(
        [d_sh_m, d_sc_m, d_gt_m, d_sh_f, d_sc_f, d_gt_f, d_g_pre_mix, d_g_post_mix, d_g_pre_ffn, d_g_post_ffn,
         jnp.concatenate([d_b_pool, d_scale, jnp.zeros((1, 512), F32)], axis=1)] + [z1] * 5, axis=0)
    slab_b = jnp.concatenate([d_conv_w, d_conv_b, jnp.zeros((4, D_FF), F32)], axis=0)
    d_wpool = jnp.concatenate([d_wbd[gi * 64:(gi + 1) * 64, gi * 64:(gi + 1) * 64] for gi in range(4)], axis=0)
    slab_a_g, slab_b_g, wpool_g = gather_small([slab_a, slab_b, d_wpool])
    cw_cols = conv_w.shape[2]
    convw_g = lax.dynamic_slice(slab_b_g, (0, 0, chip * cw_cols), (N_DEV, 3, cw_cols))
    rs_mix = GradReduce("mix", [dw_in, dw_out], [256, 256], place)
    slab_a_g = rs_mix.d2d_start(slab_a_g)
    slab_a_g = rs_ffn.final_and_share_start(slab_a_g, slab_a_g)
    slab_a_g = rs_mix.add_and_ici_start(slab_a_g, slab_a_g)
    dmod_cols = lax.dynamic_slice(slab_a_g[:, :6, :].reshape(N_DEV, 6 * D_MODEL), (0, chip * cb_ada), (N_DEV, cb_ada))

    res = {}

    def big_adamw(name, w, g, m, v, tr):
        d_, m_, v_ = adamw_rows(w[0], g, m[0], v[0], tr, "adamw_" + name)
        res[name] = (g[None], d_[None], m_[None], v_[None])
        return v_

    g_ada, d_ada, m_ada, v_ada = adamw_ada(c_all.reshape(N_DEV, D_MODEL).T, dmod_cols, w_ada[0], m_w_ada[0], v_w_ada[0])
    res["w_ada"] = (g_ada[None], d_ada[None], m_ada[None], v_ada[None])
    g_w_up, g_w_down = rs_ffn.finish(v_ada)
    big_adamw("w_up", w_up, g_w_up, m_w_up, v_w_up, 256)
    last = big_adamw("w_down", w_down, g_w_down, m_w_down, v_w_down, 352)
    rs_mix.final_and_share_start(last, jnp.zeros((8, 128), F32))
    g_w_in, g_w_out = rs_mix.finish(last)
    big_adamw("w_in", w_in, g_w_in, m_w_in, v_w_in, 256)
    big_adamw("w_out", w_out, g_w_out, m_w_out, v_w_out, 256)
    flat = lambda t: t.reshape(1, POOL_W)
    wp = lambda t: t.reshape(POOL_W, 64)
    small = adamw_small(slab_a_g, slab_b_g, convw_g, wpool_g, {
        "b_ada": (b_ada, m_b_ada, v_b_ada), "g_pre_mix": (g_pre_mix, m_g_pre_mix, v_g_pre_mix),
        "g_post_mix": (g_post_mix, m_g_post_mix, v_g_post_mix), "g_pre_ffn": (g_pre_ffn, m_g_pre_ffn, v_g_pre_ffn),
        "g_post_ffn": (g_post_ffn, m_g_post_ffn, v_g_post_ffn), "b_pool": (flat(b_pool), flat(m_b_pool), flat(v_b_pool)),
        "pool_scale": (pool_scale, m_pool_scale, v_pool_scale), "conv_b": (conv_b, m_conv_b, v_conv_b),
        "conv_w": (conv_w[0], m_conv_w[0], v_conv_w[0]), "w_pool": (wp(w_pool), wp(m_w_pool), wp(v_w_pool))})
    for name in ("b_ada", "g_pre_mix", "g_post_mix", "g_pre_ffn", "g_post_ffn", "pool_scale", "conv_b"):
        res[name] = tuple(small[name])
    res["b_pool"] = tuple(t.reshape(1, 4, 64) for t in small["b_pool"])
    res["conv_w"] = tuple(t[None] for t in small["conv_w"])
    res["w_pool"] = tuple(t.reshape(1, 4, 64, 64) for t in small["w_pool"])

    loss = lax.psum(loss_v[0, 0], ("x", "y", "c"))
    order = ["w_ada", "b_ada", "g_pre_mix", "g_post_mix", "g_pre_ffn", "g_post_ffn", "w_in", "w_pool", "b_pool", "pool_scale",
             "w_out", "w_up", "conv_w", "conv_b", "w_down"]
    outs = [loss, grad_x[None]]
    for k in range(4):
        outs += [res[n][k] for n in order]
    return tuple(outs)
```

```python
import functools
import math

import jax
import jax.numpy as jnp
from jax import lax
from jax.experimental import pallas as pl
from jax.experimental.pallas import tpu as pltpu

F32 = jnp.float32
BF16 = jnp.bfloat16
MESH = pl.DeviceIdType.MESH

D_MODEL = 1024
HEAD_DIM = 64
POOL_W = 256
GROUP_W = 256
DILATIONS = (1, 4, 16)
ATT_BLOCK = 128
IN_W = 2560
D_FF = 2816
HALF_FF = 1408
ROT_DIM = 16
ROPE_THETA = 500000.0
NORM_EPS = 1e-6
N_CHIPS = 4
N_DEV = 8
NEG = -1e30

ADAM_LR = 0.001
ADAM_B1 = 0.9
ADAM_B2 = 0.999
ADAM_EPS = 1e-08
ADAM_WD = 0.01
ADAM_STEP = 10

VMEM_LIMIT = 56 * 1024 * 1024

NT = (((1,), (1,)), ((), ()))
TN = (((0,), (0,)), ((), ()))


def _params(n_grid=0, **kw):
    sem = ("arbitrary",) * n_grid if n_grid else None
    return pltpu.CompilerParams(dimension_semantics=sem, vmem_limit_bytes=VMEM_LIMIT, **kw)


def _full(shape):
    nd = len(shape)
    return pl.BlockSpec(tuple(shape), lambda *_: (0,) * nd, pipeline_mode=pl.Buffered(1))


def _rows(tm, ncol):
    return pl.BlockSpec((tm, ncol), lambda i: (i, 0))


def _acc(ref, val):
    @pl.when(pl.program_id(0) == 0)
    def _():
        ref[...] = jnp.zeros_like(ref)

    ref[...] += val


def _colsum(v):
    return jnp.sum(v, axis=0, keepdims=True)


def _rope128(t, cs, sa, sb, sign):
    return t * cs + sign * (pltpu.roll(t, 8, 1) * sa + pltpu.roll(t, 120, 1) * sb)


GELU_C0 = math.sqrt(2.0 / math.pi)
GELU_C1 = GELU_C0 * 0.044715


def _gelu(z):
    z2 = z * z
    t = jnp.tanh(z * (GELU_C0 + GELU_C1 * z2))
    u = 0.5 * t + 0.5
    return z * u, u, t, z2


def _gelu_grad(z, u, t, z2):
    return u + (z * (GELU_C0 + (3.0 * GELU_C1) * z2)) * (0.5 - 0.5 * (t * t))


def _conv_taps(gate, halo, first):
    row = lax.broadcasted_iota(jnp.int32, gate.shape, 0)
    halo = jnp.where(first, 0.0, halo)
    nh = halo.shape[0]
    p1 = halo[nh - 1:nh, :]
    p2 = halo[nh - 2:nh - 1, :]
    g1 = jnp.where(row == 0, p1, pltpu.roll(gate, 1, 0))
    g2 = jnp.where(row == 0, p2, jnp.where(row == 1, p1, pltpu.roll(gate, 2, 0)))
    return g1, g2


def inproj_fwd(x, g, mod6, w_in_g, tc, tsa, tsb, tm=512):
    S = x.shape[0]

    def body(x_ref, g_ref, mod_ref, w_ref, tc_ref, tsa_ref, tsb_ref, h_ref, u_ref, q1_ref, q4_ref, q16_ref, scr):
        qkv_refs = (q1_ref, q4_ref, q16_ref)
        xv = x_ref[...]
        rstd = lax.rsqrt(jnp.mean(xv * xv, axis=-1, keepdims=True) + NORM_EPS)
        h = ((xv * rstd) * g_ref[...]) * (1.0 + mod_ref[1:2, :]) + mod_ref[0:1, :]
        hb = h.astype(BF16)
        h_ref[...] = hb
        cs, sa, sb = tc_ref[...], tsa_ref[...], tsb_ref[...]
        for j in range(N_CHIPS):
            res = jnp.dot(hb, w_ref[j], preferred_element_type=F32)
            for t in range(5):
                sp = 5 * j + t
                piece, half = sp // 2, sp % 2
                blk = res[:, t * 128:(t + 1) * 128]
                lanes = slice(half * 128, (half + 1) * 128)
                if piece == 0:
                    u_ref[:, lanes] = blk
                else:
                    kind, gi = (piece - 1) // 3, (piece - 1) % 3
                    if kind == 0:
                        blk = _rope128(blk, cs, sa, sb, 1.0) * (HEAD_DIM ** -0.5)
                    elif kind == 1:
                        blk = _rope128(blk, cs, sa, sb, 1.0)
                    d = DILATIONS[gi]
                    if d == 1:
                        q1_ref[kind, 0, :, lanes] = blk.astype(BF16)
                    else:
                        scr[...] = blk
                        for r in range(d):
                            qkv_refs[gi][kind, r, :, lanes] = scr[pl.ds(r, tm // d, stride=d), :].astype(BF16)

    cls = lambda d: pl.BlockSpec((3, d, tm // d, GROUP_W), lambda i: (0, 0, i, 0))
    return pl.pallas_call(
        body,
        name="inproj_fwd",
        grid=(S // tm,),
        in_specs=[_rows(tm, D_MODEL), _full((1, D_MODEL)), _full((6, D_MODEL)), _full(w_in_g.shape),
                  _rows(tm, 128), _rows(tm, 128), _rows(tm, 128)],
        out_specs=[_rows(tm, D_MODEL), _rows(tm, POOL_W)] + [cls(d) for d in DILATIONS],
        out_shape=[jax.ShapeDtypeStruct((S, D_MODEL), BF16), jax.ShapeDtypeStruct((S, POOL_W), F32)]
        + [jax.ShapeDtypeStruct((3, d, S // d, GROUP_W), BF16) for d in DILATIONS],
        scratch_shapes=[pltpu.VMEM((tm, 128), F32)],
        compiler_params=_params(1),
    )(x, g, mod6, w_in_g, tc, tsa, tsb)


def _attn_masks():
    row = lax.broadcasted_iota(jnp.int32, (2 * ATT_BLOCK, 2 * ATT_BLOCK), 0) % ATT_BLOCK
    col = lax.broadcasted_iota(jnp.int32, (2 * ATT_BLOCK, 2 * ATT_BLOCK), 1)
    band = (col >= row) & (col <= row + ATT_BLOCK)
    lane = lax.broadcasted_iota(jnp.int32, (ATT_BLOCK, 128), 1)
    return band, col, lane < HEAD_DIM


def _stack_heads(t, lo):
    z = jnp.zeros_like(t)
    return jnp.concatenate([jnp.where(lo, t, z), jnp.where(lo, z, t)], axis=0)


def _unstack_heads(t2, lo):
    return jnp.where(lo, t2[:ATT_BLOCK], t2[ATT_BLOCK:])


def attn_fwd(qkv, d):
    L = qkv.shape[2]
    nb = L // ATT_BLOCK

    def body(q_ref, k_ref, v_ref, o_ref, l_ref, kpad, vpad):
        kpad[0:ATT_BLOCK, :] = jnp.zeros((ATT_BLOCK, GROUP_W), BF16)
        vpad[0:ATT_BLOCK, :] = jnp.zeros((ATT_BLOCK, GROUP_W), BF16)
        kpad[ATT_BLOCK:, :] = k_ref[...]
        vpad[ATT_BLOCK:, :] = v_ref[...]
        band, col, lo = _attn_masks()

        def step(n, carry):
            r0 = pl.multiple_of(n * ATT_BLOCK, ATT_BLOCK)
            valid = band & ((col >= ATT_BLOCK) | (n > 0))
            qb = q_ref[pl.ds(r0, ATT_BLOCK), :]
            kb = kpad[pl.ds(r0, 2 * ATT_BLOCK), :]
            vb = vpad[pl.ds(r0, 2 * ATT_BLOCK), :]
            for pair in range(2):
                lanes = slice(pair * 128, (pair + 1) * 128)
                qp, kp, vp = qb[:, lanes], kb[:, lanes], vb[:, lanes]
                s = lax.dot_general(_stack_heads(qp, lo), kp, NT, preferred_element_type=F32)
                s = jnp.where(valid, s, NEG)
                m = jnp.max(s, axis=1, keepdims=True)
                p = jnp.exp(s - m)
                den = jnp.sum(p, axis=1, keepdims=True)
                pv = jnp.dot(p.astype(BF16), vp, preferred_element_type=F32)
                o_ref[pl.ds(r0, ATT_BLOCK), lanes] = _unstack_heads(pv / den, lo)
                l_ref[pl.ds(r0, ATT_BLOCK), lanes] = _unstack_heads(jnp.broadcast_to(m + jnp.log(den), pv.shape), lo)
            return carry

        lax.fori_loop(0, nb, step, 0, unroll=min(4, nb))

    spec = lambda kind: pl.BlockSpec((None, None, L, GROUP_W), lambda r: (kind, r, 0, 0))
    return pl.pallas_call(
        body,
        name=f"attn_fwd_d{d}",
        grid=(d,),
        in_specs=[spec(0), spec(1), spec(2)],
        out_specs=[pl.BlockSpec((None, L, GROUP_W), lambda r: (r, 0, 0))] * 2,
        out_shape=[jax.ShapeDtypeStruct((d, L, GROUP_W), F32)] * 2,
        scratch_shapes=[pltpu.VMEM((L + ATT_BLOCK, GROUP_W), BF16)] * 2,
        compiler_params=_params(1),
    )(qkv, qkv, qkv)


def _pool_lane_windows(shape):
    lane = lax.broadcasted_iota(jnp.int32, shape, 1)
    return lane, jnp.where(lane < 64, 2, jnp.where(lane < 128, 4, jnp.where(lane < 192, 8, 16)))


def pool_fwd(u, wbd, b, scale):
    S = u.shape[0]

    def body(u_ref, w_ref, b_ref, s_ref, mixed_ref, out_ref):
        uv = u_ref[...]
        row = lax.broadcasted_iota(jnp.int32, uv.shape, 0)
        lane, win = _pool_lane_windows(uv.shape)

        def shift(a, k):
            return jnp.where(row >= k, pltpu.roll(a, k, 0), 0.0)

        s2 = uv + shift(uv, 1)
        s4 = s2 + shift(s2, 2)
        s8 = s4 + shift(s4, 4)
        s16 = s8 + shift(s8, 8)
        tsum = jnp.where(lane < 64, s2, jnp.where(lane < 128, s4, jnp.where(lane < 192, s8, s16)))
        cnt = jnp.minimum(row + 1, win).astype(F32)
        mb = (tsum / cnt - uv).astype(BF16)
        mixed_ref[...] = mb
        y = jnp.dot(mb, w_ref[...], preferred_element_type=F32) + b_ref[...]
        out_ref[...] = (y * s_ref[...]).astype(BF16)

    vm = pl.BlockSpec(memory_space=pltpu.VMEM)
    return pl.pallas_call(
        body,
        name="pool_fwd",
        in_specs=[vm] * 4,
        out_specs=[vm] * 2,
        out_shape=[jax.ShapeDtypeStruct((S, POOL_W), BF16)] * 2,
        compiler_params=_params(),
    )(u, wbd, b, scale)


def outproj_fwd(o_l, pool, x, w_out_g, g_post, g_pre, mod6, tm=512):
    S = x.shape[0]

    def body(o0, o1, o2, l0, l1, l2, pool_ref, x_ref, w_ref, gpost_ref, gpre_ref, mod_ref,
             cat_ref, lse_ref, lse4_ref, lse16_ref, y1_ref, x1_ref, h2_ref, so4, sl4, so16, sl16):
        for d, src, dst in ((4, o1, so4), (4, l1, sl4), (16, o2, so16), (16, l2, sl16)):
            for r in range(d):
                for h in range(2):
                    dst[h, pl.ds(r, tm // d, stride=d), :] = src[r, :, h * 128:(h + 1) * 128]
        nat = lambda ref: jnp.concatenate([ref[0], ref[1]], axis=1)
        a, b, c = l0[0], nat(sl4), nat(sl16)
        m = jnp.maximum(jnp.maximum(a, b), c)
        e0, e1, e2 = jnp.exp(a - m), jnp.exp(b - m), jnp.exp(c - m)
        z = e0 + e1 + e2
        lse = m + jnp.log(z)
        lse_ref[...] = lse
        for h in range(2):
            sl4[h] = lse[:, h * 128:(h + 1) * 128]
        for d, dst in ((4, lse4_ref), (16, lse16_ref)):
            for r in range(d):
                for h in range(2):
                    dst[r, :, h * 128:(h + 1) * 128] = sl4[h, pl.ds(r, tm // d, stride=d), :]
        attn = (e0 * o0[0] + e1 * nat(so4) + e2 * nat(so16)) / z
        cat = jnp.concatenate([pool_ref[...], attn.astype(BF16)], axis=1)
        cat_ref[...] = cat
        y1 = jnp.concatenate([jnp.dot(cat, w_ref[j], preferred_element_type=F32) for j in range(N_CHIPS)], axis=1)
        y1_ref[...] = y1
        rstd = lax.rsqrt(jnp.mean(y1 * y1, axis=-1, keepdims=True) + NORM_EPS)
        x1 = x_ref[...] + mod_ref[2:3, :] * ((y1 * rstd) * gpost_ref[...])
        x1_ref[...] = x1
        rstd2 = lax.rsqrt(jnp.mean(x1 * x1, axis=-1, keepdims=True) + NORM_EPS)
        h2 = ((x1 * rstd2) * gpre_ref[...]) * (1.0 + mod_ref[4:5, :]) + mod_ref[3:4, :]
        h2_ref[...] = h2.astype(BF16)

    t256 = _rows(tm, GROUP_W)
    cls = lambda d: pl.BlockSpec((d, tm // d, GROUP_W), lambda i: (0, i, 0))
    cls_shape = lambda d: jax.ShapeDtypeStruct((d, S // d, GROUP_W), F32)
    return pl.pallas_call(
        body,
        name="outproj_fwd",
        grid=(S // tm,),
        in_specs=[cls(d) for d in DILATIONS] * 2 + [t256, _rows(tm, D_MODEL), _full(w_out_g.shape), _full((1, D_MODEL)),
                                                    _full((1, D_MODEL)), _full((6, D_MODEL))],
        out_specs=[_rows(tm, 512), t256, cls(4), cls(16), _rows(tm, D_MODEL), _rows(tm, D_MODEL), _rows(tm, D_MODEL)],
        out_shape=[jax.ShapeDtypeStruct((S, 512), BF16), jax.ShapeDtypeStruct((S, GROUP_W), F32), cls_shape(4), cls_shape(16),
                   jax.ShapeDtypeStruct((S, D_MODEL), F32), jax.ShapeDtypeStruct((S, D_MODEL), F32),
                   jax.ShapeDtypeStruct((S, D_MODEL), BF16)],
        scratch_shapes=[pltpu.VMEM((2, tm, 128), F32)] * 4,
        compiler_params=_params(1),
    )(*o_l, pool, x, w_out_g, g_post, g_pre, mod6)


def up_fwd(h2, w_up_g, tm=512):
    S = h2.shape[0]

    def body(h_ref, w_ref, gate_ref, val_ref):
        hb = h_ref[...]
        for j in range(N_CHIPS):
            res = jnp.dot(hb, w_ref[j], preferred_element_type=F32).astype(BF16)
            dst = gate_ref if j < 2 else val_ref
            dst[:, (j % 2) * HALF_FF:(j % 2 + 1) * HALF_FF] = res

    return pl.pallas_call(
        body,
        name="up_fwd",
        grid=(S // tm,),
        in_specs=[_rows(tm, D_MODEL), _full(w_up_g.shape)],
        out_specs=[_rows(tm, D_FF)] * 2,
        out_shape=[jax.ShapeDtypeStruct((S, D_FF), BF16)] * 2,
        compiler_params=_params(1),
    )(h2, w_up_g)


def _halo_prev(tm, ncol):
    return pl.BlockSpec((16, ncol), lambda i: (jnp.maximum(i * (tm // 16) - 1, 0), 0))


def down_fwd(gate, val, conv_w, conv_b, w_down, x1, target, g_post, mod6, tm=256):
    S = x1.shape[0]

    def body(gate_ref, halo_ref, val_ref, cw_ref, cb_ref, w_ref, x1_ref, tgt_ref, g_ref, mod_ref,
             a_ref, dy2_ref, dout_ref, loss_ref, dgt_ref, dg_ref):
        first = pl.program_id(0) == 0
        y2 = jnp.zeros((tm, D_MODEL), F32)
        for ch in range(2):
            cols = slice(ch * HALF_FF, (ch + 1) * HALF_FF)
            gt = gate_ref[:, cols].astype(F32)
            g1, g2 = _conv_taps(gt, halo_ref[:, cols].astype(F32), first)
            gc = g2 * cw_ref[0:1, cols] + g1 * cw_ref[1:2, cols] + gt * cw_ref[2:3, cols] + cb_ref[:, cols]
            ge = _gelu(gc)[0]
            ab = (ge * val_ref[:, cols].astype(F32)).astype(BF16)
            a_ref[:, cols] = ab
            y2 = y2 + jnp.dot(ab, w_ref[cols, :], preferred_element_type=F32)
        rstd = lax.rsqrt(jnp.mean(y2 * y2, axis=-1, keepdims=True) + NORM_EPS)
        y2n = y2 * rstd
        gv = g_ref[...]
        gtf = mod_ref[5:6, :]
        r2 = y2n * gv
        diff = (x1_ref[...] + gtf * r2) - tgt_ref[...]
        _acc(loss_ref, jnp.zeros((1, 128), F32) + 0.5 * jnp.sum(diff * diff) * (1.0 / D_MODEL))
        dout = diff * (1.0 / D_MODEL)
        dout_ref[...] = dout
        _acc(dgt_ref, _colsum(dout * r2))
        dr2 = dout * gtf
        _acc(dg_ref, _colsum(dr2 * y2n))
        dyn = dr2 * gv
        dy2 = rstd * (dyn - y2n * jnp.mean(dyn * y2n, axis=-1, keepdims=True))
        dy2_ref[...] = dy2.astype(BF16)

    vec = _full((1, D_MODEL))
    return pl.pallas_call(
        body,
        name="down_fwd",
        grid=(S // tm,),
        in_specs=[_rows(tm, D_FF), _halo_prev(tm, D_FF), _rows(tm, D_FF), _full((3, D_FF)), _full((1, D_FF)),
                  _full((D_FF, D_MODEL)), _rows(tm, D_MODEL), _rows(tm, D_MODEL), vec, _full((6, D_MODEL))],
        out_specs=[_rows(tm, D_FF), _rows(tm, D_MODEL), _rows(tm, D_MODEL), _full((1, 128)), vec, vec],
        out_shape=[jax.ShapeDtypeStruct((S, D_FF), BF16), jax.ShapeDtypeStruct((S, D_MODEL), BF16),
                   jax.ShapeDtypeStruct((S, D_MODEL), F32), jax.ShapeDtypeStruct((1, 128), F32),
                   jax.ShapeDtypeStruct((1, D_MODEL), F32), jax.ShapeDtypeStruct((1, D_MODEL), F32)],
        compiler_params=_params(1),
    )(gate, gate, val, conv_w, conv_b, w_down, x1, target, g_post, mod6)


def ffn_fwd(h2, w_up_g, conv_w, conv_b, w_down, x1, target, g_post, mod6, tm=256):
    S = x1.shape[0]

    def body(h_ref, wu_ref, cw_ref, cb_ref, wd_ref, x1_ref, tgt_ref, g_ref, mod_ref,
             gate_ref, val_ref, a_ref, dy2_ref, dout_ref, loss_ref, dgt_ref, dg_ref, carry):
        first = pl.program_id(0) == 0

        @pl.when(first)
        def _():
            carry[...] = jnp.zeros_like(carry)

        hb = h_ref[...]
        y2 = jnp.zeros((tm, D_MODEL), F32)
        for ch in range(2):
            cols = slice(ch * HALF_FF, (ch + 1) * HALF_FF)
            gb = jnp.dot(hb, wu_ref[ch], preferred_element_type=F32).astype(BF16)
            vb = jnp.dot(hb, wu_ref[2 + ch], preferred_element_type=F32).astype(BF16)
            gate_ref[:, cols] = gb
            val_ref[:, cols] = vb
            gt = gb.astype(F32)
            g1, g2 = _conv_taps(gt, carry[:, cols], first)
            carry[:, cols] = gt[tm - 8:, :]
            gc = g2 * cw_ref[0:1, cols] + g1 * cw_ref[1:2, cols] + gt * cw_ref[2:3, cols] + cb_ref[:, cols]
            ab = (_gelu(gc)[0] * vb.astype(F32)).astype(BF16)
            a_ref[:, cols] = ab
            y2 = y2 + jnp.dot(ab, wd_ref[cols, :], preferred_element_type=F32)
        rstd = lax.rsqrt(jnp.mean(y2 * y2, axis=-1, keepdims=True) + NORM_EPS)
        y2n = y2 * rstd
        gv = g_ref[...]
        gtf = mod_ref[5:6, :]
        r2 = y2n * gv
        diff = (x1_ref[...] + gtf * r2) - tgt_ref[...]
        _acc(loss_ref, jnp.zeros((1, 128), F32) + 0.5 * jnp.sum(diff * diff) * (1.0 / D_MODEL))
        dout = diff * (1.0 / D_MODEL)
        dout_ref[...] = dout
        _acc(dgt_ref, _colsum(dout * r2))
        dr2 = dout * gtf
        _acc(dg_ref, _colsum(dr2 * y2n))
        dyn = dr2 * gv
        dy2 = rstd * (dyn - y2n * jnp.mean(dyn * y2n, axis=-1, keepdims=True))
        dy2_ref[...] = dy2.astype(BF16)

    vec = _full((1, D_MODEL))
    return pl.pallas_call(
        body,
        name="ffn_fwd",
        grid=(S // tm,),
        in_specs=[_rows(tm, D_MODEL), _full(w_up_g.shape), _full((3, D_FF)), _full((1, D_FF)), _full((D_FF, D_MODEL)),
                  _rows(tm, D_MODEL), _rows(tm, D_MODEL), vec, _full((6, D_MODEL))],
        out_specs=[_rows(tm, D_FF), _rows(tm, D_FF), _rows(tm, D_FF), _rows(tm, D_MODEL), _rows(tm, D_MODEL), _full((1, 128)), vec, vec],
        out_shape=[jax.ShapeDtypeStruct((S, D_FF), BF16)] * 3 + [jax.ShapeDtypeStruct((S, D_MODEL), BF16),
                                                                 jax.ShapeDtypeStruct((S, D_MODEL), F32),
                                                                 jax.ShapeDtypeStruct((1, 128), F32),
                                                                 jax.ShapeDtypeStruct((1, D_MODEL), F32),
                                                                 jax.ShapeDtypeStruct((1, D_MODEL), F32)],
        scratch_shapes=[pltpu.VMEM((8, D_FF), F32)],
        compiler_params=_params(1),
    )(h2, w_up_g, conv_w, conv_b, w_down, x1, target, g_post, mod6)


def down_bwd(dy2, w_down, gate, val, conv_w, conv_b, a, h2, tm=256):
    S = dy2.shape[0]

    def body(dy_ref, w_ref, gate_ref, halo_ref, val_ref, cw_ref, cb_ref, a_ref, h_ref,
             dgc_ref, dval_ref, dcw_ref, dcb_ref, dwd_ref, dwu_ref):
        first = pl.program_id(0) == 0

        @pl.when(first)
        def _():
            dcw_ref[...] = jnp.zeros_like(dcw_ref)
            dcb_ref[...] = jnp.zeros_like(dcb_ref)
            dwd_ref[...] = jnp.zeros_like(dwd_ref)
            dwu_ref[...] = jnp.zeros_like(dwu_ref)

        dyb = dy_ref[...]
        hb = h_ref[...]
        for ch in range(2):
            cols = slice(ch * HALF_FF, (ch + 1) * HALF_FF)
            da = lax.dot_general(dyb, w_ref[cols, :], NT, preferred_element_type=F32)
            gt = gate_ref[:, cols].astype(F32)
            g1, g2 = _conv_taps(gt, halo_ref[:, cols].astype(F32), first)
            gc = g2 * cw_ref[0:1, cols] + g1 * cw_ref[1:2, cols] + gt * cw_ref[2:3, cols] + cb_ref[:, cols]
            ge, u, th, z2 = _gelu(gc)
            dgc = da * val_ref[:, cols].astype(F32) * _gelu_grad(gc, u, th, z2)
            dgc_ref[:, cols] = dgc.astype(BF16)
            dvb = (da * ge).astype(BF16)
            dval_ref[:, cols] = dvb
            dcb_ref[:, cols] += _colsum(dgc)
            dcw_ref[0:1, cols] += _colsum(dgc * g2)
            dcw_ref[1:2, cols] += _colsum(dgc * g1)
            dcw_ref[2:3, cols] += _colsum(dgc * gt)
            dwd_ref[cols, :] += lax.dot_general(a_ref[:, cols], dyb, TN, preferred_element_type=F32)
            dwu_ref[ch] += lax.dot_general(hb, dvb, TN, preferred_element_type=F32)

    return pl.pallas_call(
        body,
        name="down_bwd",
        grid=(S // tm,),
        in_specs=[_rows(tm, D_MODEL), _full((D_FF, D_MODEL)), _rows(tm, D_FF), _halo_prev(tm, D_FF), _rows(tm, D_FF),
                  _full((3, D_FF)), _full((1, D_FF)), _rows(tm, D_FF), _rows(tm, D_MODEL)],
        out_specs=[_rows(tm, D_FF), _rows(tm, D_FF), _full((3, D_FF)), _full((1, D_FF)), _full((D_FF, D_MODEL)),
                   pl.BlockSpec((2, D_MODEL, HALF_FF), lambda i: (1, 0, 0), pipeline_mode=pl.Buffered(1))],
        out_shape=[jax.ShapeDtypeStruct((S, D_FF), BF16), jax.ShapeDtypeStruct((S, D_FF), BF16),
                   jax.ShapeDtypeStruct((3, D_FF), F32), jax.ShapeDtypeStruct((1, D_FF), F32),
                   jax.ShapeDtypeStruct((D_FF, D_MODEL), F32), jax.ShapeDtypeStruct((N_CHIPS, D_MODEL, HALF_FF), F32)],
        compiler_params=_params(1),
    )(dy2, w_down, gate, gate, val, conv_w, conv_b, a, h2)


def dw_matmul(a, b, out_blocks, blk_shape, a_cols, b_cols, a_blocked, name, prev=None, blk_off=0, n_blk=None, tm=512):
    S = a.shape[0]
    n_blk = out_blocks if n_blk is None else n_blk

    def body(*refs):
        a_ref, b_ref, o_ref = refs[0], refs[1], refs[-1]

        @pl.when(pl.program_id(1) == 0)
        def _():
            o_ref[...] = jnp.zeros_like(o_ref)

        o_ref[...] += lax.dot_general(a_ref[...], b_ref[...], TN, preferred_element_type=F32)

    a_spec = pl.BlockSpec((tm, a_cols), (lambda j, i: (i, j)) if a_blocked else (lambda j, i: (i, 0)))
    b_spec = pl.BlockSpec((tm, b_cols), (lambda j, i: (i, 0)) if a_blocked else (lambda j, i: (i, j)))
    in_specs = [a_spec, b_spec]
    args = [a, b]
    aliases = {}
    if prev is not None:
        in_specs.append(pl.BlockSpec(memory_space=pl.ANY))
        args.append(prev)
        aliases = {2: 0}
    return pl.pallas_call(
        body,
        name=name,
        grid=(n_blk, S // tm),
        in_specs=in_specs,
        out_specs=pl.BlockSpec((None,) + tuple(blk_shape), lambda j, i: (j + blk_off, 0, 0)),
        out_shape=jax.ShapeDtypeStruct((out_blocks,) + tuple(blk_shape), F32),
        input_output_aliases=aliases,
        compiler_params=_params(2),
    )(*args)


def up_bwd(dgc, dval, conv_w, w_up_g, x1, dout, y1, g_pre, g_post, mod6, h2, dw_up, tm=256):
    S = x1.shape[0]
    last_blk = S // 16 - 1

    def body(dgc_ref, nxt_ref, dval_ref, cw_ref, w_ref, x1_ref, dout_ref, y1_ref, gpre_ref, gpost_ref, mod_ref, h_ref, dwin_ref,
             dx1_ref, dy1_ref, dsh_ref, dsc_ref, dgpre_ref, dgt_ref, dgpost_ref, dwu_ref):
        last = pl.program_id(0) == pl.num_programs(0) - 1

        @pl.when(pl.program_id(0) == 0)
        def _():
            dwu_ref[...] = jnp.zeros_like(dwu_ref)

        hb = h_ref[...]
        dh = jnp.zeros((tm, D_MODEL), F32)
        for ch in range(2):
            cols = slice(ch * HALF_FF, (ch + 1) * HALF_FF)
            dg = dgc_ref[:, cols].astype(F32)
            nx = jnp.where(last, 0.0, nxt_ref[:, cols].astype(F32))
            row = lax.broadcasted_iota(jnp.int32, dg.shape, 0)
            n0, n1 = nx[0:1, :], nx[1:2, :]
            u1 = jnp.where(row == tm - 1, n0, pltpu.roll(dg, tm - 1, 0))
            u2 = jnp.where(row == tm - 1, n1, jnp.where(row == tm - 2, n0, pltpu.roll(dg, tm - 2, 0)))
            dgate = (dg * cw_ref[2:3, cols] + u1 * cw_ref[1:2, cols] + u2 * cw_ref[0:1, cols]).astype(BF16)
            dwu_ref[ch] += lax.dot_general(hb, dgate, TN, preferred_element_type=F32)
            dh = dh + lax.dot_general(dgate, w_ref[ch], NT, preferred_element_type=F32)
            dh = dh + lax.dot_general(dval_ref[:, cols], w_ref[2 + ch], NT, preferred_element_type=F32)
        x1 = x1_ref[...]
        rstd = lax.rsqrt(jnp.mean(x1 * x1, axis=-1, keepdims=True) + NORM_EPS)
        n2 = x1 * rstd
        gpre = gpre_ref[...]
        one_sc = 1.0 + mod_ref[4:5, :]
        _acc(dsh_ref, _colsum(dh))
        _acc(dsc_ref, _colsum(dh * (n2 * gpre)))
        _acc(dgpre_ref, _colsum(dh * one_sc * n2))
        dn = dh * (gpre * one_sc)
        dx1 = dout_ref[...] + rstd * (dn - n2 * jnp.mean(dn * n2, axis=-1, keepdims=True))
        dx1_ref[...] = dx1
        y1 = y1_ref[...]
        rstd1 = lax.rsqrt(jnp.mean(y1 * y1, axis=-1, keepdims=True) + NORM_EPS)
        y1n = y1 * rstd1
        gpost = gpost_ref[...]
        gtm = mod_ref[2:3, :]
        _acc(dgt_ref, _colsum(dx1 * (y1n * gpost)))
        dr1 = dx1 * gtm
        _acc(dgpost_ref, _colsum(dr1 * y1n))
        dyn = dr1 * gpost
        dy1 = rstd1 * (dyn - y1n * jnp.mean(dyn * y1n, axis=-1, keepdims=True))
        dy1_ref[...] = dy1.astype(BF16)

    vec = _full((1, D_MODEL))
    nxt = pl.BlockSpec((16, D_FF), lambda i: (jnp.minimum((i + 1) * (tm // 16), last_blk), 0))
    return pl.pallas_call(
        body,
        name="up_bwd",
        grid=(S // tm,),
        in_specs=[_rows(tm, D_FF), nxt, _rows(tm, D_FF), _full((3, D_FF)), _full(w_up_g.shape), _rows(tm, D_MODEL),
                  _rows(tm, D_MODEL), _rows(tm, D_MODEL), vec, vec, _full((6, D_MODEL)), _rows(tm, D_MODEL),
                  pl.BlockSpec(memory_space=pl.ANY)],
        out_specs=[_rows(tm, D_MODEL), _rows(tm, D_MODEL), vec, vec, vec, vec, vec,
                   pl.BlockSpec((2, D_MODEL, HALF_FF), lambda i: (0, 0, 0), pipeline_mode=pl.Buffered(1))],
        out_shape=[jax.ShapeDtypeStruct((S, D_MODEL), F32), jax.ShapeDtypeStruct((S, D_MODEL), BF16)]
        + [jax.ShapeDtypeStruct((1, D_MODEL), F32)] * 5 + [jax.ShapeDtypeStruct(dw_up.shape, F32)],
        input_output_aliases={12: 7},
        compiler_params=_params(1),
    )(dgc, dgc, dval, conv_w, w_up_g, x1, dout, y1, g_pre, g_post, mod6, h2, dw_up)


def outproj_bwd(dy1, w_out_g, cat, tm=512):
    S = dy1.shape[0]

    def body(dy_ref, w_ref, cat_ref, dpool_ref, dattn_ref, da4_ref, da16_ref, delta_ref, dl4_ref, dl16_ref, dw_ref, scr):
        @pl.when(pl.program_id(0) == 0)
        def _():
            dw_ref[...] = jnp.zeros_like(dw_ref)

        catb = cat_ref[...]
        dcat = jnp.zeros((tm, 512), F32)
        for j in range(N_CHIPS):
            dyj = dy_ref[:, j * 256:(j + 1) * 256]
            dcat = dcat + lax.dot_general(dyj, w_ref[j], NT, preferred_element_type=F32)
            dw_ref[j] += lax.dot_general(catb, dyj, TN, preferred_element_type=F32)
        dpool_ref[...] = dcat[:, :POOL_W]
        dattn = dcat[:, POOL_W:]
        dattn_ref[...] = dattn.astype(BF16)
        for h in range(2):
            scr[h] = dattn[:, h * 128:(h + 1) * 128]
        for d, dst in ((4, da4_ref), (16, da16_ref)):
            for r in range(d):
                for h in range(2):
                    dst[r, :, h * 128:(h + 1) * 128] = scr[h, pl.ds(r, tm // d, stride=d), :].astype(BF16)
        prod = dattn * catb[:, POOL_W:].astype(F32)
        r = lax.broadcasted_iota(jnp.int32, (GROUP_W, GROUP_W), 0) // HEAD_DIM
        c = lax.broadcasted_iota(jnp.int32, (GROUP_W, GROUP_W), 1) // HEAD_DIM
        ones_bd = jnp.where(r == c, 1.0, 0.0).astype(BF16)
        hi = prod.astype(BF16)
        lo = (prod - hi.astype(F32)).astype(BF16)
        delta = jnp.dot(hi, ones_bd, preferred_element_type=F32) + jnp.dot(lo, ones_bd, preferred_element_type=F32)
        delta_ref[...] = delta
        for h in range(2):
            scr[h] = delta[:, h * 128:(h + 1) * 128]
        for d, dst in ((4, dl4_ref), (16, dl16_ref)):
            for r in range(d):
                for h in range(2):
                    dst[r, :, h * 128:(h + 1) * 128] = scr[h, pl.ds(r, tm // d, stride=d), :]

    cls = lambda d: pl.BlockSpec((d, tm // d, GROUP_W), lambda i: (0, i, 0))
    cls_shape = lambda d, dt: jax.ShapeDtypeStruct((d, S // d, GROUP_W), dt)
    return pl.pallas_call(
        body,
        name="outproj_bwd",
        grid=(S // tm,),
        in_specs=[_rows(tm, D_MODEL), _full(w_out_g.shape), _rows(tm, 512)],
        out_specs=[_rows(tm, POOL_W), _rows(tm, GROUP_W), cls(4), cls(16), _rows(tm, GROUP_W), cls(4), cls(16),
                   _full(w_out_g.shape)],
        out_shape=[jax.ShapeDtypeStruct((S, POOL_W), F32), jax.ShapeDtypeStruct((S, GROUP_W), BF16), cls_shape(4, BF16),
                   cls_shape(16, BF16), jax.ShapeDtypeStruct((S, GROUP_W), F32), cls_shape(4, F32), cls_shape(16, F32),
                   jax.ShapeDtypeStruct(w_out_g.shape, F32)],
        scratch_shapes=[pltpu.VMEM((2, tm, 128), F32)],
        compiler_params=_params(1),
    )(dy1, w_out_g, cat)


def attn_bwd(qkv, dattn, lse, delta, d):
    L = qkv.shape[2]
    nb = L // ATT_BLOCK

    def body(q_ref, k_ref, v_ref, do_ref, l_ref, dl_ref, out_ref, kpad, vpad, dkpad, dvpad):
        kpad[0:ATT_BLOCK, :] = jnp.zeros((ATT_BLOCK, GROUP_W), BF16)
        vpad[0:ATT_BLOCK, :] = jnp.zeros((ATT_BLOCK, GROUP_W), BF16)
        kpad[ATT_BLOCK:, :] = k_ref[...]
        vpad[ATT_BLOCK:, :] = v_ref[...]
        dkpad[...] = jnp.zeros_like(dkpad)
        dvpad[...] = jnp.zeros_like(dvpad)
        band, col, lo = _attn_masks()

        def step(n, carry):
            r0 = pl.multiple_of(n * ATT_BLOCK, ATT_BLOCK)
            valid = band & ((col >= ATT_BLOCK) | (n > 0))
            qb = q_ref[pl.ds(r0, ATT_BLOCK), :]
            dob = do_ref[pl.ds(r0, ATT_BLOCK), :]
            lb = l_ref[pl.ds(r0, ATT_BLOCK), :]
            dlb = dl_ref[pl.ds(r0, ATT_BLOCK), :]
            kb = kpad[pl.ds(r0, 2 * ATT_BLOCK), :]
            vb = vpad[pl.ds(r0, 2 * ATT_BLOCK), :]
            for pair in range(2):
                lanes = slice(pair * 128, (pair + 1) * 128)
                qp, dop, kp, vp = qb[:, lanes], dob[:, lanes], kb[:, lanes], vb[:, lanes]
                c0, c1 = pair * 128, pair * 128 + HEAD_DIM
                q2, do2 = _stack_heads(qp, lo), _stack_heads(dop, lo)
                lse2 = jnp.concatenate([lb[:, c0:c0 + 1], lb[:, c1:c1 + 1]], axis=0)
                dl2 = jnp.concatenate([dlb[:, c0:c0 + 1], dlb[:, c1:c1 + 1]], axis=0)
                s = lax.dot_general(q2, kp, NT, preferred_element_type=F32)
                s = jnp.where(valid, s, NEG)
                p = jnp.exp(s - lse2)
                dp = lax.dot_general(do2, vp, NT, preferred_element_type=F32)
                ds = (p * (dp - dl2)).astype(BF16)
                dq2 = jnp.dot(ds, kp, preferred_element_type=F32)
                out_ref[0, pl.ds(r0, ATT_BLOCK), lanes] = _unstack_heads(dq2, lo)
                dkpad[pl.ds(r0, 2 * ATT_BLOCK), lanes] += lax.dot_general(ds, q2, TN, preferred_element_type=F32)
                dvpad[pl.ds(r0, 2 * ATT_BLOCK), lanes] += lax.dot_general(p.astype(BF16), do2, TN, preferred_element_type=F32)
            return carry

        lax.fori_loop(0, nb, step, 0, unroll=min(4, nb))
        out_ref[1] = dkpad[ATT_BLOCK:, :]
        out_ref[2] = dvpad[ATT_BLOCK:, :]

    spec = lambda kind: pl.BlockSpec((None, None, L, GROUP_W), lambda r: (kind, r, 0, 0))
    cls = pl.BlockSpec((None, L, GROUP_W), lambda r: (r, 0, 0))
    return pl.pallas_call(
        body,
        name=f"attn_bwd_d{d}",
        grid=(d,),
        in_specs=[spec(0), spec(1), spec(2), cls, cls, cls],
        out_specs=pl.BlockSpec((3, None, L, GROUP_W), lambda r: (0, r, 0, 0)),
        out_shape=jax.ShapeDtypeStruct((3, d, L, GROUP_W), F32),
        scratch_shapes=[pltpu.VMEM((L + ATT_BLOCK, GROUP_W), BF16)] * 2 + [pltpu.VMEM((L + ATT_BLOCK, GROUP_W), F32)] * 2,
        compiler_params=_params(1),
    )(qkv, qkv, qkv, dattn, lse, delta)


def pool_bwd(dpool, mixed, wbd, b, scale):
    S = dpool.shape[0]

    def body(dp_ref, mx_ref, w_ref, b_ref, s_ref, du_ref, dw_ref, db_ref, ds_ref):
        dp = dp_ref[...]
        mb = mx_ref[...]
        wv = w_ref[...]
        ypre = jnp.dot(mb, wv, preferred_element_type=F32) + b_ref[...]
        ds_ref[...] = _colsum(dp * ypre)
        dpre = dp * s_ref[...]
        db_ref[...] = _colsum(dpre)
        dpb = dpre.astype(BF16)
        dw_ref[...] = lax.dot_general(mb, dpb, TN, preferred_element_type=F32)
        dmix = lax.dot_general(dpb, wv, NT, preferred_element_type=F32)
        row = lax.broadcasted_iota(jnp.int32, dmix.shape, 0)
        lane, win = _pool_lane_windows(dmix.shape)
        e = dmix / jnp.minimum(row + 1, win).astype(F32)

        def shift(a, k):
            return jnp.where(row < S - k, pltpu.roll(a, S - k, 0), 0.0)

        f2 = e + shift(e, 1)
        f4 = f2 + shift(f2, 2)
        f8 = f4 + shift(f4, 4)
        f16 = f8 + shift(f8, 8)
        du_ref[...] = jnp.where(lane < 64, f2, jnp.where(lane < 128, f4, jnp.where(lane < 192, f8, f16))) - dmix

    vm = pl.BlockSpec(memory_space=pltpu.VMEM)
    return pl.pallas_call(
        body,
        name="pool_bwd",
        in_specs=[vm] * 5,
        out_specs=[vm] * 4,
        out_shape=[jax.ShapeDtypeStruct((S, POOL_W), F32), jax.ShapeDtypeStruct((POOL_W, POOL_W), F32),
                   jax.ShapeDtypeStruct((1, POOL_W), F32), jax.ShapeDtypeStruct((1, POOL_W), F32)],
        compiler_params=_params(),
    )(dpool, mixed, wbd, b, scale)


def inproj_bwd(dqkv, du, x, dx1, w_in_g, g, mod6, tc, tsa, tsb, h1, tm=256):
    S = x.shape[0]

    def body(d0, d1, d2, du_ref, x_ref, dx1_ref, w_ref, g_ref, mod_ref, tc_ref, tsa_ref, tsb_ref, h_ref,
             gx_ref, dsh_ref, dsc_ref, dg_ref, dw_ref, s4, s16, dp_ref):
        @pl.when(pl.program_id(0) == 0)
        def _():
            dw_ref[...] = jnp.zeros_like(dw_ref)

        cs, sa, sb = tc_ref[...], tsa_ref[...], tsb_ref[...]
        for d, src, dst in ((4, d1, s4), (16, d2, s16)):
            for kind in range(3):
                for r in range(d):
                    for h in range(2):
                        dst[kind, h, pl.ds(r, tm // d, stride=d), :] = src[kind, r, :, h * 128:(h + 1) * 128]
        for sp in range(20):
            piece, half = sp // 2, sp % 2
            lanes = slice(half * 128, (half + 1) * 128)
            if piece == 0:
                blk = du_ref[:, lanes]
            else:
                kind, gi = (piece - 1) // 3, (piece - 1) % 3
                blk = d0[kind, 0, :, lanes] if gi == 0 else (s4, s16)[gi - 1][kind, half]
                if kind == 0:
                    blk = _rope128(blk, cs, sa, sb, -1.0) * (HEAD_DIM ** -0.5)
                elif kind == 1:
                    blk = _rope128(blk, cs, sa, sb, -1.0)
            dp_ref[:, sp * 128:(sp + 1) * 128] = blk.astype(BF16)
        dh = jnp.zeros((tm, D_MODEL), F32)
        hb = h_ref[...]
        for j in range(N_CHIPS):
            dpj = dp_ref[:, j * 640:(j + 1) * 640]
            dh = dh + lax.dot_general(dpj, w_ref[j], NT, preferred_element_type=F32)
            dw_ref[j] += lax.dot_general(hb, dpj, TN, preferred_element_type=F32)
        xv = x_ref[...]
        rstd = lax.rsqrt(jnp.mean(xv * xv, axis=-1, keepdims=True) + NORM_EPS)
        n1 = xv * rstd
        gv = g_ref[...]
        one_sc = 1.0 + mod_ref[1:2, :]
        _acc(dsh_ref, _colsum(dh))
        _acc(dsc_ref, _colsum(dh * (n1 * gv)))
        _acc(dg_ref, _colsum(dh * one_sc * n1))
        dn = dh * (gv * one_sc)
        gx_ref[...] = dx1_ref[...] + rstd * (dn - n1 * jnp.mean(dn * n1, axis=-1, keepdims=True))

    vec = _full((1, D_MODEL))
    dspec = lambda d: pl.BlockSpec((3, d, tm // d, GROUP_W), lambda i: (0, 0, i, 0))
    return pl.pallas_call(
        body,
        name="inproj_bwd",
        grid=(S // tm,),
        in_specs=[dspec(d) for d in DILATIONS] + [_rows(tm, POOL_W), _rows(tm, D_MODEL), _rows(tm, D_MODEL), _full(w_in_g.shape),
                                                  vec, _full((6, D_MODEL)), _rows(tm, 128), _rows(tm, 128), _rows(tm, 128),
                                                  _rows(tm, D_MODEL)],
        out_specs=[_rows(tm, D_MODEL), vec, vec, vec, _full(w_in_g.shape)],
        out_shape=[jax.ShapeDtypeStruct((S, D_MODEL), F32)] + [jax.ShapeDtypeStruct((1, D_MODEL), F32)] * 3
        + [jax.ShapeDtypeStruct(w_in_g.shape, F32)],
        scratch_shapes=[pltpu.VMEM((3, 2, tm, 128), F32)] * 2 + [pltpu.VMEM((tm, IN_W), BF16)],
        compiler_params=_params(1),
    )(*dqkv, du, x, dx1, w_in_g, g, mod6, tc, tsa, tsb, h1)


def _adamw(w, g, m, v):
    m = ADAM_B1 * m + (1.0 - ADAM_B1) * g
    v = ADAM_B2 * v + (1.0 - ADAM_B2) * (g * g)
    m_hat = m / (1.0 - ADAM_B1 ** ADAM_STEP)
    v_hat = v / (1.0 - ADAM_B2 ** ADAM_STEP)
    delta = -ADAM_LR * (m_hat / (jnp.sqrt(v_hat) + ADAM_EPS) + ADAM_WD * w)
    return delta, m, v


def adamw_rows(w, g, m, v, tr, name):
    R, C = w.shape

    def body(w_ref, g_ref, m_ref, v_ref, d_ref, mo_ref, vo_ref):
        d_ref[...], mo_ref[...], vo_ref[...] = _adamw(w_ref[...], g_ref[...], m_ref[...], v_ref[...])

    spec = pl.BlockSpec((tr, C), lambda i: (i, 0))
    return pl.pallas_call(
        body,
        name=name,
        grid=(R // tr,),
        in_specs=[spec] * 4,
        out_specs=[spec] * 3,
        out_shape=[jax.ShapeDtypeStruct((R, C), F32)] * 3,
        compiler_params=_params(1),
    )(w, g, m, v)


def adamw_ada(c_all_t, dmod_cols, w, m, v, tr=256):
    R, C = w.shape

    def body(ct_ref, dm_ref, w_ref, m_ref, v_ref, g_ref, d_ref, mo_ref, vo_ref):
        ct = ct_ref[...]
        act = ct * jax.nn.sigmoid(ct)
        g = jnp.zeros((tr, C), F32)
        for b in range(N_DEV):
            g = g + act[:, b:b + 1] * dm_ref[b:b + 1, :]
        g_ref[...] = g
        d_ref[...], mo_ref[...], vo_ref[...] = _adamw(w_ref[...], g, m_ref[...], v_ref[...])

    spec = pl.BlockSpec((tr, C), lambda i: (i, 0))
    return pl.pallas_call(
        body,
        name="adamw_ada",
        grid=(R // tr,),
        in_specs=[pl.BlockSpec((tr, N_DEV), lambda i: (i, 0)), _full((N_DEV, C)), spec, spec, spec],
        out_specs=[spec] * 4,
        out_shape=[jax.ShapeDtypeStruct((R, C), F32)] * 4,
        compiler_params=_params(1),
    )(c_all_t, dmod_cols, w, m, v)


def adamw_small(slab_a, slab_b, convw_g, wpool_g, params):
    names = ["b_ada", "g_pre_mix", "g_post_mix", "g_pre_ffn", "g_post_ffn", "b_pool", "pool_scale", "conv_b", "conv_w", "w_pool"]
    flat = []
    for n in names:
        flat += list(params[n])

    def body(a_ref, b_ref, cw_ref, wp_ref, *rest):
        ins, outs = rest[:30], rest[30:]

        def dev_sum(ref):
            t = ref[0]
            for dev in range(1, N_DEV):
                t = t + ref[dev]
            return t

        sa, sb_, scw, swp = dev_sum(a_ref), dev_sum(b_ref), dev_sum(cw_ref), dev_sum(wp_ref)
        grads = [
            jnp.concatenate([sa[k:k + 1, :] for k in range(6)], axis=1),
            sa[6:7, :], sa[7:8, :], sa[8:9, :], sa[9:10, :],
            sa[10:11, 0:256], sa[10:11, 256:512],
            sb_[3:4, :], scw, swp,
        ]
        for i, g in enumerate(grads):
            w_ref, m_ref, v_ref = ins[3 * i:3 * i + 3]
            d, mo, vo = _adamw(w_ref[...], g, m_ref[...], v_ref[...])
            outs[4 * i][...] = g
            outs[4 * i + 1][...] = d
            outs[4 * i + 2][...] = mo
            outs[4 * i + 3][...] = vo
        outs[-1][...] = sa[10:11, 512:640]

    vm = pl.BlockSpec(memory_space=pltpu.VMEM)
    out_shape = []
    for n in names:
        out_shape += [jax.ShapeDtypeStruct(params[n][0].shape, F32)] * 4
    out_shape.append(jax.ShapeDtypeStruct((1, 128), F32))
    outs = pl.pallas_call(
        body,
        name="adamw_small",
        in_specs=[vm] * (4 + len(flat)),
        out_specs=[vm] * len(out_shape),
        out_shape=out_shape,
        compiler_params=_params(),
    )(slab_a, slab_b, convw_g, wpool_g, *flat)
    return {n: outs[4 * i:4 * i + 4] for i, n in enumerate(names)}, outs[-1]


def _place():
    return lax.axis_index("x"), lax.axis_index("y"), lax.axis_index("c")


def _other_chips(x, y):
    return [(1 - x, y), (x, 1 - y), (1 - x, 1 - y)]


def _chip_id(cx, cy):
    return 2 * cx + cy


def gather_weights(shards):
    n = len(shards)
    halved = [s.shape[0] % 32 == 0 for s in shards]

    def body(*refs):
        ins, outs = refs[:n], refs[n:2 * n]
        send_sems, recv_sems, loc_sems = refs[2 * n:]
        x, y, c = _place()
        me = _chip_id(x, y)
        chips = _other_chips(x, y)
        sib = (x, y, 1 - c)

        def part(w, chip, half):
            if not halved[w]:
                return outs[w].at[chip]
            rh = shards[w].shape[0] // 2
            return outs[w].at[chip, pl.ds(half * rh, rh), :]

        def src_part(w):
            if not halved[w]:
                return ins[w]
            rh = shards[w].shape[0] // 2
            return ins[w].at[pl.ds(c * rh, rh), :]

        def rcopy(w, k, src, dst, to):
            return pltpu.make_async_remote_copy(src_ref=src, dst_ref=dst, send_sem=send_sems.at[6 * w + k],
                                                recv_sem=recv_sems.at[6 * w + k], device_id=to, device_id_type=MESH)

        local = [pltpu.make_async_copy(ins[w], outs[w].at[me], loc_sems.at[w]) for w in range(n)]
        for cp in local:
            cp.start()
        first = []
        for w in range(n):
            for k, (cx, cy) in enumerate(chips):
                cp = rcopy(w, k, src_part(w), part(w, me, c), (cx, cy, c))
                cp.start()
                first.append(cp)
        passed = []
        for w in range(n):
            for k, (cx, cy) in enumerate(chips):
                blk = part(w, _chip_id(cx, cy), c)
                rcopy(w, k, blk, blk, (cx, cy, c)).wait_recv()
                if halved[w]:
                    cp = rcopy(w, 3 + k, blk, blk, sib)
                    cp.start()
                    passed.append(cp)
        for w in range(n):
            if halved[w]:
                for k, (cx, cy) in enumerate(chips):
                    blk = part(w, _chip_id(cx, cy), 1 - c)
                    rcopy(w, 3 + k, blk, blk, sib).wait_recv()
        for cp in first + passed:
            cp.wait_send()
        for cp in local:
            cp.wait()

    hbm = pl.BlockSpec(memory_space=pl.ANY)
    return pl.pallas_call(
        body,
        name="gather_weights",
        in_specs=[hbm] * n,
        out_specs=[hbm] * n,
        out_shape=[jax.ShapeDtypeStruct((N_CHIPS,) + s.shape, s.dtype) for s in shards],
        scratch_shapes=[pltpu.SemaphoreType.DMA((6 * n,)), pltpu.SemaphoreType.DMA((6 * n,)), pltpu.SemaphoreType.DMA((n,))],
        compiler_params=pltpu.CompilerParams(has_side_effects=True, vmem_limit_bytes=VMEM_LIMIT),
    )(*shards)


HBM_SPEC = pl.BlockSpec(memory_space=pltpu.HBM)
SEM_SPEC = pl.BlockSpec(memory_space=pltpu.SEMAPHORE)
ANY_SPEC = pl.BlockSpec(memory_space=pl.ANY)
EFFECT = pltpu.SideEffectType.DATAFLOW_SIDE_EFFECTING


def _hbm(t):
    return pltpu.with_memory_space_constraint(t, pltpu.HBM)


def _hbm_shapes(ts):
    return [pltpu.HBM(t.shape, t.dtype) for t in ts]


def _half_rows(ref, lead, half, rh):
    return ref.at[lead, pl.ds(half * rh, rh), :]


def gather_split_start(shards, lands, carry, k):
    n = len(shards)

    def body(*refs):
        ins, land = refs[:n], refs[n:2 * n]
        send_sems, recv_sems = refs[2 * n + 1], refs[2 * n + 2]
        loc_sems = refs[-1]
        x, y, c = _place()
        me = _chip_id(x, y)
        if k == 0:
            local = [pltpu.make_async_copy(ins[w], land[w].at[me], loc_sems.at[w]) for w in range(n)]
            for cp in local:
                cp.start()
            for cp in local:
                cp.wait()
        cx, cy = _other_chips(x, y)[k]
        for w in range(n):
            rh = shards[w].shape[0] // 2
            pltpu.make_async_remote_copy(src_ref=ins[w].at[pl.ds(c * rh, rh), :], dst_ref=_half_rows(land[w], me, c, rh),
                                         send_sem=send_sems.at[w], recv_sem=recv_sems.at[w],
                                         device_id=(cx, cy, c), device_id_type=MESH).start()

    args = [_hbm(s) for s in shards] + [_hbm(l) for l in lands] + [_hbm(carry)]
    outs = pl.pallas_call(
        body,
        name="gather_split_start%d" % k,
        out_shape=[pltpu.SemaphoreType.DMA((n,)), pltpu.SemaphoreType.DMA((n,))] + _hbm_shapes(shards) + _hbm_shapes(lands)
        + _hbm_shapes([carry]),
        in_specs=[HBM_SPEC] * (2 * n + 1),
        out_specs=[SEM_SPEC, SEM_SPEC] + [HBM_SPEC] * (2 * n + 1),
        input_output_aliases={i: 2 + i for i in range(2 * n + 1)},
        scratch_shapes=[pltpu.SemaphoreType.DMA((n,))],
        compiler_params=pltpu.CompilerParams(has_side_effects=EFFECT),
    )(*args)
    return outs[0], outs[1], list(outs[2:2 + n]), list(outs[2 + n:2 + 2 * n]), outs[-1]


def gather_split_mid(sems, shards, lands, after):
    n = len(shards)

    def body(*refs):
        ins, land = refs[:n], refs[n:2 * n]
        sem_in = refs[2 * n:2 * n + 6]
        fsend, frecv = refs[2 * n + 7], refs[2 * n + 8]
        x, y, c = _place()
        me = _chip_id(x, y)
        chips = _other_chips(x, y)
        for w in range(n):
            rh = shards[w].shape[0] // 2
            for k, (cx, cy) in enumerate(chips):
                got = _half_rows(land[w], _chip_id(cx, cy), c, rh)
                cp = pltpu.make_async_remote_copy(src_ref=ins[w].at[pl.ds(c * rh, rh), :], dst_ref=got, send_sem=sem_in[2 * k].at[w],
                                                  recv_sem=sem_in[2 * k + 1].at[w], device_id=(cx, cy, c), device_id_type=MESH)
                cp.wait_send()
                cp.wait_recv()
        for w in range(n):
            rh = shards[w].shape[0] // 2
            for k, (cx, cy) in enumerate(chips):
                got = _half_rows(land[w], _chip_id(cx, cy), c, rh)
                pltpu.make_async_remote_copy(src_ref=got, dst_ref=got, send_sem=fsend.at[3 * w + k], recv_sem=frecv.at[3 * w + k],
                                             device_id=(x, y, 1 - c), device_id_type=MESH).start()

    outs = pl.pallas_call(
        body,
        name="gather_split_mid",
        out_shape=[pltpu.SemaphoreType.DMA((3 * n,)), pltpu.SemaphoreType.DMA((3 * n,))] + _hbm_shapes(lands),
        in_specs=[HBM_SPEC] * (2 * n) + [SEM_SPEC] * 6 + [ANY_SPEC],
        out_specs=[SEM_SPEC, SEM_SPEC] + [HBM_SPEC] * n,
        input_output_aliases={n + i: 2 + i for i in range(n)},
        compiler_params=pltpu.CompilerParams(has_side_effects=EFFECT),
    )(*shards, *lands, *sems, after)
    return outs[0], outs[1], list(outs[2:])


def gather_split_done(fsend, frecv, lands, after):
    n = len(lands)

    def body(*refs):
        land = refs[:n]
        ssem, rsem = refs[n], refs[n + 1]
        x, y, c = _place()
        for w in range(n):
            rh = lands[w].shape[1] // 2
            for k, (cx, cy) in enumerate(_other_chips(x, y)):
                sent = _half_rows(land[w], _chip_id(cx, cy), c, rh)
                got = _half_rows(land[w], _chip_id(cx, cy), 1 - c, rh)
                cp = pltpu.make_async_remote_copy(src_ref=sent, dst_ref=got, send_sem=ssem.at[3 * w + k], recv_sem=rsem.at[3 * w + k],
                                                  device_id=(x, y, 1 - c), device_id_type=MESH)
                cp.wait_send()
                cp.wait_recv()

    outs = pl.pallas_call(
        body,
        name="gather_split_done",
        out_shape=_hbm_shapes(lands),
        in_specs=[HBM_SPEC] * n + [SEM_SPEC, SEM_SPEC, ANY_SPEC],
        out_specs=[HBM_SPEC] * n,
        input_output_aliases={i: i for i in range(n)},
        compiler_params=pltpu.CompilerParams(has_side_effects=EFFECT),
    )(*lands, fsend, frecv, after)
    return list(outs)


def _flips():
    return [(fx, fy, fc) for fx in (0, 1) for fy in (0, 1) for fc in (0, 1)][1:]


def _flip(v, f):
    return v if f == 0 else 1 - v


def ada_mod(c3, w_ada, b_cols, conv_w):
    CB = w_ada.shape[1]

    def body(c_ref, w_ref, b_ref, cw_ref, call_ref, mod_ref, cwall_ref, modall, send_sems, recv_sems):
        x, y, c = _place()
        me_dev = 4 * x + 2 * y + c
        me = _chip_id(x, y)
        call_ref[me_dev] = c_ref[0]
        cwall_ref[me] = cw_ref[...]
        sends = []
        for k, (cx, cy) in enumerate(_other_chips(x, y)):
            cp = pltpu.make_async_remote_copy(src_ref=cw_ref, dst_ref=cwall_ref.at[me], send_sem=send_sems.at[10 + k],
                                              recv_sem=recv_sems.at[10 + k], device_id=(cx, cy, c), device_id_type=MESH)
            cp.start()
            sends.append(cp)
        for k, (fx, fy, fc) in enumerate(_flips()):
            cp = pltpu.make_async_remote_copy(src_ref=c_ref.at[0], dst_ref=call_ref.at[me_dev], send_sem=send_sems.at[k],
                                              recv_sem=recv_sems.at[k],
                                              device_id=(_flip(x, fx), _flip(y, fy), _flip(c, fc)), device_id_type=MESH)
            cp.start()
            sends.append(cp)
        for k, (fx, fy, fc) in enumerate(_flips()):
            peer = 4 * _flip(x, fx) + 2 * _flip(y, fy) + _flip(c, fc)
            pltpu.make_async_remote_copy(src_ref=c_ref.at[0], dst_ref=call_ref.at[peer], send_sem=send_sems.at[k],
                                         recv_sem=recv_sems.at[k], device_id=(x, y, c), device_id_type=MESH).wait_recv()
        row = lax.broadcasted_iota(jnp.int32, (N_DEV, D_MODEL), 0)
        call = jnp.zeros((N_DEV, D_MODEL), F32)
        for dev in range(N_DEV):
            call = jnp.where(row == dev, call_ref[dev], call)
        act = call * jax.nn.sigmoid(call)
        modall[me] = jnp.dot(act, w_ref[...], preferred_element_type=F32, precision=lax.Precision.HIGHEST) + b_ref[...]
        for k, (cx, cy) in enumerate(_other_chips(x, y)):
            cp = pltpu.make_async_remote_copy(src_ref=modall.at[me], dst_ref=modall.at[me], send_sem=send_sems.at[7 + k],
                                              recv_sem=recv_sems.at[7 + k], device_id=(cx, cy, c), device_id_type=MESH)
            cp.start()
            sends.append(cp)
        for k, (cx, cy) in enumerate(_other_chips(x, y)):
            blk = modall.at[_chip_id(cx, cy)]
            pltpu.make_async_remote_copy(src_ref=blk, dst_ref=blk, send_sem=send_sems.at[7 + k], recv_sem=recv_sems.at[7 + k],
                                         device_id=(x, y, c), device_id_type=MESH).wait_recv()
        for k, (cx, cy) in enumerate(_other_chips(x, y)):
            blk = cwall_ref.at[_chip_id(cx, cy)]
            pltpu.make_async_remote_copy(src_ref=blk, dst_ref=blk, send_sem=send_sems.at[10 + k], recv_sem=recv_sems.at[10 + k],
                                         device_id=(x, y, c), device_id_type=MESH).wait_recv()
        for cp in sends:
            cp.wait_send()
        mine = [modall[j, pl.ds(me_dev, 1), :] for j in range(N_CHIPS)]
        for r in range(6):
            pieces = []
            for h in range(2):
                pos = r * D_MODEL + h * 512
                pieces.append(mine[pos // CB][:, pos % CB:pos % CB + 512])
            mod_ref[r:r + 1, :] = jnp.concatenate(pieces, axis=1)

    vm = pl.BlockSpec(memory_space=pltpu.VMEM)
    return pl.pallas_call(
        body,
        name="ada_mod",
        in_specs=[vm] * 4,
        out_specs=[vm] * 3,
        out_shape=[jax.ShapeDtypeStruct((N_DEV, 1, D_MODEL), F32), jax.ShapeDtypeStruct((6, D_MODEL), F32),
                   jax.ShapeDtypeStruct((N_CHIPS,) + conv_w.shape, F32)],
        scratch_shapes=[pltpu.VMEM((N_CHIPS, N_DEV, CB), F32), pltpu.SemaphoreType.DMA((13,)), pltpu.SemaphoreType.DMA((13,))],
        compiler_params=pltpu.CompilerParams(has_side_effects=True, vmem_limit_bytes=VMEM_LIMIT),
    )(c3, w_ada, b_cols, conv_w)


def gather_small(blocks):
    n = len(blocks)

    def body(*refs):
        ins, outs = refs[:n], refs[n:2 * n]
        send_sems, recv_sems = refs[2 * n:]
        x, y, c = _place()
        sib = (x, y, 1 - c)
        chips = _other_chips(x, y)

        def dev(px, py, pc):
            return 4 * px + 2 * py + pc

        def cp(w, k, src, block_dev, to):
            return pltpu.make_async_remote_copy(src_ref=src, dst_ref=outs[w].at[block_dev], send_sem=send_sems.at[7 * w + k],
                                                recv_sem=recv_sems.at[7 * w + k], device_id=to, device_id_type=MESH)

        me = dev(x, y, c)
        started = []
        for w in range(n):
            outs[w][me] = ins[w][...]
            t = cp(w, 0, ins[w], me, sib)
            t.start()
            started.append(t)
            for k, (cx, cy) in enumerate(chips):
                t = cp(w, 1 + k, ins[w], me, (cx, cy, c))
                t.start()
                started.append(t)
        for w in range(n):
            for k, (cx, cy) in enumerate(chips):
                b = dev(cx, cy, c)
                cp(w, 1 + k, outs[w].at[b], b, (x, y, c)).wait_recv()
                t = cp(w, 4 + k, outs[w].at[b], b, sib)
                t.start()
                started.append(t)
        for w in range(n):
            b = dev(x, y, 1 - c)
            cp(w, 0, outs[w].at[b], b, (x, y, c)).wait_recv()
            for k, (cx, cy) in enumerate(chips):
                b = dev(cx, cy, 1 - c)
                cp(w, 4 + k, outs[w].at[b], b, (x, y, c)).wait_recv()
        for t in started:
            t.wait_send()

    vm = pl.BlockSpec(memory_space=pltpu.VMEM)
    return pl.pallas_call(
        body,
        name="gather_small",
        in_specs=[vm] * n,
        out_specs=[vm] * n,
        out_shape=[jax.ShapeDtypeStruct((N_DEV,) + b.shape, b.dtype) for b in blocks],
        scratch_shapes=[pltpu.SemaphoreType.DMA((7 * n,)), pltpu.SemaphoreType.DMA((7 * n,))],
        compiler_params=pltpu.CompilerParams(has_side_effects=True, vmem_limit_bytes=VMEM_LIMIT),
    )(*blocks)


def reduce_scatter_grads(grads, chunk_rows):
    n = len(grads)
    shapes = [g.shape[1:] for g in grads]
    halves = [s[0] // 2 for s in shapes]

    def body(*refs):
        gin = refs[:n]
        gout = refs[n:2 * n]
        sibbuf = refs[2 * n:3 * n]
        rest = refs[3 * n:]
        rbuf = rest[:n]
        pown = rest[n:2 * n]
        stage_a, stage_b, stage_o, stage_f = rest[2 * n:2 * n + 4]
        sib_send, sib_recv, ici_send, ici_recv, fin_send, fin_recv, ld_sems, st_sems = rest[2 * n + 4:]
        x, y, c = _place()
        me = _chip_id(x, y)
        chips = _other_chips(x, y)
        sib = (x, y, 1 - c)

        to_sib = []
        for w in range(n):
            rh = halves[w]
            cp = pltpu.make_async_remote_copy(src_ref=gin[w].at[:, pl.ds((1 - c) * rh, rh), :], dst_ref=sibbuf[w],
                                              send_sem=sib_send.at[w], recv_sem=sib_recv.at[w], device_id=sib,
                                              device_id_type=MESH)
            cp.start()
            to_sib.append(cp)

        sent = []
        for w in range(n):
            rh, cw = halves[w], shapes[w][1]
            ch = chunk_rows[w]
            to_sib[w].wait_recv()
            for k in range(4):
                chip = me if k == 3 else _chip_id(*chips[k])
                for r0 in range(0, rh, ch):
                    la = pltpu.make_async_copy(gin[w].at[chip, pl.ds(c * rh + r0, ch), :], stage_a.at[0:ch, 0:cw], ld_sems.at[0])
                    lb = pltpu.make_async_copy(sibbuf[w].at[chip, pl.ds(r0, ch), :], stage_b.at[0:ch, 0:cw], ld_sems.at[1])
                    la.start()
                    lb.start()
                    la.wait()
                    lb.wait()
                    tot = stage_a[0:ch, 0:cw] + stage_b[0:ch, 0:cw]
                    if k == 3:
                        pown[w][r0:r0 + ch, :] = tot
                    else:
                        stage_o[0:ch, 0:cw] = tot.astype(BF16)
                        cx, cy = chips[k]
                        cp = pltpu.make_async_remote_copy(src_ref=stage_o.at[0:ch, 0:cw], dst_ref=rbuf[w].at[k, r0:r0 + ch, :],
                                                          send_sem=ici_send.at[3 * w + k], recv_sem=ici_recv.at[3 * w + k],
                                                          device_id=(cx, cy, c), device_id_type=MESH)
                        cp.start()
                        cp.wait_send()
            sent.append(w)

        fin = []
        for w in range(n):
            rh, cw = halves[w], shapes[w][1]
            for k in range(3):
                whole = rbuf[w].at[k]
                pltpu.make_async_remote_copy(src_ref=whole, dst_ref=whole, send_sem=ici_send.at[3 * w + k],
                                             recv_sem=ici_recv.at[3 * w + k], device_id=(x, y, c),
                                             device_id_type=MESH).wait_recv()
            pown[w][...] = ((pown[w][...] + rbuf[w][0].astype(F32)) + rbuf[w][1].astype(F32)) + rbuf[w][2].astype(F32)
            mine = gout[w].at[pl.ds(c * rh, rh), :]
            st = pltpu.make_async_copy(pown[w], mine, st_sems.at[w])
            st.start()
            cp = pltpu.make_async_remote_copy(src_ref=pown[w], dst_ref=mine, send_sem=fin_send.at[w], recv_sem=fin_recv.at[w],
                                              device_id=sib, device_id_type=MESH)
            cp.start()
            fin.append((st, cp))
        for w in range(n):
            rh = halves[w]
            theirs = gout[w].at[pl.ds((1 - c) * rh, rh), :]
            pltpu.make_async_remote_copy(src_ref=theirs, dst_ref=theirs, send_sem=fin_send.at[w], recv_sem=fin_recv.at[w],
                                         device_id=(x, y, c), device_id_type=MESH).wait_recv()
        for cp in to_sib:
            cp.wait_send()
        for st, cp in fin:
            st.wait()
            cp.wait_send()

    hbm = pl.BlockSpec(memory_space=pl.ANY)
    max_ch = max(chunk_rows)
    max_c = max(s[1] for s in shapes)
    outs = pl.pallas_call(
        body,
        name="reduce_scatter_grads",
        in_specs=[hbm] * n,
        out_specs=[hbm] * (2 * n),
        out_shape=[jax.ShapeDtypeStruct(s, F32) for s in shapes]
        + [jax.ShapeDtypeStruct((N_CHIPS, h, s[1]), F32) for h, s in zip(halves, shapes)],
        scratch_shapes=[pltpu.VMEM((3, h, s[1]), BF16) for h, s in zip(halves, shapes)]
        + [pltpu.VMEM((h, s[1]), F32) for h, s in zip(halves, shapes)]
        + [pltpu.VMEM((max_ch, max_c), F32), pltpu.VMEM((max_ch, max_c), F32), pltpu.VMEM((max_ch, max_c), BF16),
           pltpu.VMEM((8, 128), F32)]
        + [pltpu.SemaphoreType.DMA((n,)), pltpu.SemaphoreType.DMA((n,)), pltpu.SemaphoreType.DMA((3 * n,)),
           pltpu.SemaphoreType.DMA((3 * n,)), pltpu.SemaphoreType.DMA((n,)), pltpu.SemaphoreType.DMA((n,)),
           pltpu.SemaphoreType.DMA((2,)), pltpu.SemaphoreType.DMA((n,))],
        compiler_params=pltpu.CompilerParams(has_side_effects=True, vmem_limit_bytes=VMEM_LIMIT),
    )(*grads)
    return outs[:n]


def split_start(name, bufs, plan, n_sem, carry):
    nb = len(bufs)

    def body(*refs):
        x, y, c = _place()
        ssem, rsem = refs[nb + 1], refs[nb + 2]
        for i, (src, dst, dev) in enumerate(plan(refs[:nb], x, y, c)):
            pltpu.make_async_remote_copy(src_ref=src, dst_ref=dst, send_sem=ssem.at[i], recv_sem=rsem.at[i], device_id=dev,
                                         device_id_type=MESH).start()

    alls = list(bufs) + [carry]
    outs = pl.pallas_call(
        body,
        name=name,
        out_shape=[pltpu.SemaphoreType.DMA((n_sem,)), pltpu.SemaphoreType.DMA((n_sem,))] + _hbm_shapes(alls),
        in_specs=[HBM_SPEC] * (nb + 1),
        out_specs=[SEM_SPEC, SEM_SPEC] + [HBM_SPEC] * (nb + 1),
        input_output_aliases={i: 2 + i for i in range(nb + 1)},
        compiler_params=pltpu.CompilerParams(has_side_effects=EFFECT),
    )(*[_hbm(t) for t in alls])
    return outs[0], outs[1], list(outs[2:2 + nb]), outs[-1]


def split_wait(name, ssem, rsem, bufs, plan, after):
    nb = len(bufs)

    def body(*refs):
        x, y, c = _place()
        s_ref, r_ref = refs[nb], refs[nb + 1]
        for i, (src, dst, dev) in enumerate(plan(refs[:nb], x, y, c)):
            cp = pltpu.make_async_remote_copy(src_ref=src, dst_ref=dst, send_sem=s_ref.at[i], recv_sem=r_ref.at[i], device_id=dev,
                                              device_id_type=MESH)
            cp.wait_send()
            cp.wait_recv()

    outs = pl.pallas_call(
        body,
        name=name,
        out_shape=_hbm_shapes(bufs),
        in_specs=[HBM_SPEC] * nb + [SEM_SPEC, SEM_SPEC, ANY_SPEC],
        out_specs=[HBM_SPEC] * nb,
        input_output_aliases={i: i for i in range(nb)},
        compiler_params=pltpu.CompilerParams(has_side_effects=EFFECT),
    )(*bufs, ssem, rsem, after)
    return list(outs)


def _gather_ici_plan(n):
    def plan(refs, x, y, c):
        out = []
        for w in range(n):
            rh = refs[w].shape[0] // 2
            for cx, cy in _other_chips(x, y):
                out.append((refs[w].at[pl.ds(c * rh, rh), :], _half_rows(refs[n + w], _chip_id(x, y), c, rh), (cx, cy, c)))
        return out

    return plan


def _gather_d2d_plan(n):
    def plan(refs, x, y, c):
        out = []
        for w in range(n):
            rh = refs[w].shape[1] // 2
            for cx, cy in _other_chips(x, y):
                blk = _half_rows(refs[w], _chip_id(cx, cy), c, rh)
                out.append((blk, blk, (x, y, 1 - c)))
        return out

    return plan


def _rs_d2d_plan(n):
    def plan(refs, x, y, c):
        out = []
        for w in range(n):
            rh = refs[w].shape[1] // 2
            out.append((refs[w].at[:, pl.ds((1 - c) * rh, rh), :], refs[n + w], (x, y, 1 - c)))
        return out

    return plan


def _rs_ici_plan(n):
    def plan(refs, x, y, c):
        out = []
        for w in range(n):
            for k, (cx, cy) in enumerate(_other_chips(x, y)):
                out.append((refs[w].at[_chip_id(cx, cy)], refs[n + w].at[k], (cx, cy, c)))
        return out

    return plan


def _rs_share_plan(n):
    def plan(refs, x, y, c):
        out = []
        for w in range(n):
            rh = refs[w].shape[0] // 2
            rows = refs[w].at[pl.ds(c * rh, rh), :]
            out.append((rows, rows, (x, y, 1 - c)))
        return out

    return plan


def rs_add(grad, sibbuf, place, tr, name):
    _, R, C = grad.shape
    nt = (R // 2) // tr

    def body(p_ref, g_ref, s_ref, o_ref):
        o_ref[...] = (g_ref[...] + s_ref[...]).astype(BF16)

    return pl.pallas_call(
        body,
        name=name,
        grid_spec=pltpu.PrefetchScalarGridSpec(
            num_scalar_prefetch=1,
            grid=(N_CHIPS, nt),
            in_specs=[pl.BlockSpec((None, tr, C), lambda j, i, p: (j, p[0] * nt + i, 0)),
                      pl.BlockSpec((None, tr, C), lambda j, i, p: (j, i, 0))],
            out_specs=pl.BlockSpec((None, tr, C), lambda j, i, p: (j, i, 0)),
        ),
        out_shape=jax.ShapeDtypeStruct((N_CHIPS, R // 2, C), BF16),
        compiler_params=_params(2),
    )(place, grad, sibbuf)


def rs_final(grad, sibbuf, rbuf, place, tr, name):
    _, R, C = grad.shape
    nt = (R // 2) // tr

    def body(p_ref, g_ref, s_ref, r_ref, o_ref):
        o_ref[...] = (((g_ref[...] + s_ref[...]) + r_ref[0].astype(F32)) + r_ref[1].astype(F32)) + r_ref[2].astype(F32)

    return pl.pallas_call(
        body,
        name=name,
        grid_spec=pltpu.PrefetchScalarGridSpec(
            num_scalar_prefetch=1,
            grid=(nt,),
            in_specs=[pl.BlockSpec((None, tr, C), lambda i, p: (p[1], p[0] * nt + i, 0)),
                      pl.BlockSpec((None, tr, C), lambda i, p: (p[1], i, 0)),
                      pl.BlockSpec((3, tr, C), lambda i, p: (0, i, 0))],
            out_specs=pl.BlockSpec((tr, C), lambda i, p: (p[0] * nt + i, 0)),
        ),
        out_shape=jax.ShapeDtypeStruct((R, C), F32),
        compiler_params=_params(1),
    )(place, grad, sibbuf, rbuf)


class GradReduce:
    def __init__(self, tag, grads, rows, place):
        self.tag, self.grads, self.rows, self.place = tag, grads, rows, place
        self.n = len(grads)

    def d2d_start(self, carry):
        sib = [lax.empty((N_CHIPS, g.shape[1] // 2, g.shape[2]), F32) for g in self.grads]
        self.s1, self.r1, bufs, carry = split_start(f"rs_{self.tag}_d2d_start", self.grads + sib, _rs_d2d_plan(self.n), self.n, carry)
        self.bufs1 = bufs
        return carry

    def add_and_ici_start(self, after, carry):
        bufs = split_wait(f"rs_{self.tag}_d2d_wait", self.s1, self.r1, self.bufs1, _rs_d2d_plan(self.n), after)
        self.grads, self.sib = bufs[:self.n], bufs[self.n:]
        pb = [rs_add(g, s, self.place, tr, f"rs_{self.tag}_add{w}")
              for w, (g, s, tr) in enumerate(zip(self.grads, self.sib, self.rows))]
        rb = [lax.empty((3,) + p.shape[1:], BF16) for p in pb]
        self.s2, self.r2, self.bufs2, carry = split_start(f"rs_{self.tag}_ici_start", pb + rb, _rs_ici_plan(self.n), 3 * self.n, carry)
        return carry

    def final_and_share_start(self, after, carry):
        bufs = split_wait(f"rs_{self.tag}_ici_wait", self.s2, self.r2, self.bufs2, _rs_ici_plan(self.n), after)
        rb = bufs[self.n:]
        full = [rs_final(g, s, r, self.place, tr, f"rs_{self.tag}_final{w}")
                for w, (g, s, r, tr) in enumerate(zip(self.grads, self.sib, rb, self.rows))]
        self.s3, self.r3, self.bufs3, carry = split_start(f"rs_{self.tag}_share_start", full, _rs_share_plan(self.n), self.n, carry)
        return carry

    def finish(self, after):
        return split_wait(f"rs_{self.tag}_share_wait", self.s3, self.r3, self.bufs3, _rs_share_plan(self.n), after)


def _rope_tables(positions):
    inv_freq = ROPE_THETA ** (-jnp.arange(0, ROT_DIM, 2, dtype=F32) / ROT_DIM)
    ang = positions.astype(F32)[:, None] * inv_freq
    cos, sin = jnp.cos(ang), jnp.sin(ang)
    S = positions.shape[0]
    one, zero = jnp.ones((S, 48), F32), jnp.zeros((S, 48), F32)
    z8 = jnp.zeros((S, 8), F32)
    tc = jnp.concatenate([cos, cos, one], axis=1)
    tsa = jnp.concatenate([z8, sin, zero], axis=1)
    tsb = jnp.concatenate([-sin, z8, zero], axis=1)
    return tuple(jnp.tile(t, (1, 2)) for t in (tc, tsa, tsb))


def _block_diag(w_pool):
    wbd = jnp.zeros((POOL_W, POOL_W), F32)
    for gi in range(4):
        wbd = wbd.at[gi * 64:(gi + 1) * 64, gi * 64:(gi + 1) * 64].set(w_pool[gi])
    return wbd


def kernel(x, c, positions, w_ada, b_ada, g_pre_mix, g_post_mix, g_pre_ffn, g_post_ffn, w_in, w_pool, b_pool, pool_scale, w_out, w_up, conv_w, conv_b, w_down, loss_target, m_w_ada, m_b_ada, m_g_pre_mix, m_g_post_mix, m_g_pre_ffn, m_g_post_ffn, m_w_in, m_w_pool, m_b_pool, m_pool_scale, m_w_out, m_w_up, m_conv_w, m_conv_b, m_w_down, v_w_ada, v_b_ada, v_g_pre_mix, v_g_post_mix, v_g_pre_ffn, v_g_post_ffn, v_w_in, v_w_pool, v_b_pool, v_pool_scale, v_w_out, v_w_up, v_conv_w, v_conv_b, v_w_down):
    xi, yi, ci = lax.axis_index("x"), lax.axis_index("y"), lax.axis_index("c")
    chip = 2 * xi + yi
    place = jnp.stack([ci, chip]).astype(jnp.int32)
    x2, tgt = x[0], loss_target[0]
    S = x2.shape[0]

    cb_ada = w_ada.shape[2]
    b_cols = lax.dynamic_slice(b_ada, (0, chip * cb_ada), (1, cb_ada))
    c_all, mod6, conv_w_g = ada_mod(c.reshape(1, 1, D_MODEL), w_ada[0], b_cols, conv_w[0])
    conv_w_f = jnp.transpose(conv_w_g, (1, 0, 2)).reshape(3, D_FF)

    def landing(s_):
        return lax.dynamic_update_slice(lax.empty((N_CHIPS,) + s_.shape, s_.dtype), s_[None], (chip, 0, 0))

    mix_sh = [w_in[0].astype(BF16), w_out[0].astype(BF16)]
    ffn_sh = [w_up[0].astype(BF16), w_down[0].astype(BF16)]
    ga_s, ga_r, ga_bufs, mod6 = split_start("gather_mix_ici_start", mix_sh + [landing(t) for t in mix_sh], _gather_ici_plan(2), 6, mod6)
    gb_s, gb_r, gb_bufs, mod6 = split_start("gather_ffn_ici_start", ffn_sh + [landing(t) for t in ffn_sh], _gather_ici_plan(2), 6, mod6)
    tc, tsa, tsb = _rope_tables(positions[0])
    wbd = _block_diag(w_pool[0]).astype(BF16)
    b_pool2, scale2 = b_pool.reshape(1, POOL_W), pool_scale
    ga_bufs = split_wait("gather_mix_ici_wait", ga_s, ga_r, ga_bufs, _gather_ici_plan(2), tc)
    gc_s, gc_r, mix_land, mod6 = split_start("gather_mix_d2d_start", ga_bufs[2:], _gather_d2d_plan(2), 6, mod6)
    w_in_g, w_out_g = split_wait("gather_mix_d2d_wait", gc_s, gc_r, mix_land, _gather_d2d_plan(2), mod6)

    h1, u, *qkv = inproj_fwd(x2, g_pre_mix, mod6, w_in_g, tc, tsa, tsb)
    mixed, pool = pool_fwd(u, wbd, b_pool2, scale2)
    o_l = [attn_fwd(t, d) for t, d in zip(qkv, DILATIONS)]
    attn_done = sum(l[0, :8, :128] for _, l in o_l)
    gb_bufs = split_wait("gather_ffn_ici_wait", gb_s, gb_r, gb_bufs, _gather_ici_plan(2), attn_done)
    gd_s, gd_r, ffn_land, pool = split_start("gather_ffn_d2d_start", gb_bufs[2:], _gather_d2d_plan(2), 6, pool)
    cat, lse, lse4, lse16, y1, x1, h2 = outproj_fwd([o for o, _ in o_l] + [l for _, l in o_l], pool, x2, w_out_g, g_post_mix,
                                                    g_pre_ffn, mod6)
    lses = [lse[None], lse4, lse16]
    w_up_g, w_down_g = split_wait("gather_ffn_d2d_wait", gd_s, gd_r, ffn_land, _gather_d2d_plan(2), h2)
    w_down_f = w_down_g.reshape(D_FF, D_MODEL)
    gate, val, a, dy2, dout, loss_v, d_gt_f, d_g_post_ffn = ffn_fwd(h2, w_up_g, conv_w_f, conv_b, w_down_f, x1, tgt, g_post_ffn, mod6)

    dgc, dval, d_conv_w, d_conv_b, dw_down, dw_up = down_bwd(dy2, w_down_f, gate, val, conv_w_f, conv_b, a, h2)
    dx1, dy1, d_sh_f, d_sc_f, d_g_pre_ffn, d_gt_m, d_g_post_mix, dw_up = up_bwd(
        dgc, dval, conv_w_f, w_up_g, x1, dout, y1, g_pre_ffn, g_post_mix, mod6, h2, dw_up)
    rs_ffn = GradReduce("ffn", [dw_up, dw_down.reshape(N_CHIPS, D_FF // N_CHIPS, D_MODEL)], [256, 176], place)
    dy1 = rs_ffn.d2d_start(dy1)
    dpool, da1, da4, da16, dl1, dl4, dl16, dw_out = outproj_bwd(dy1, w_out_g, cat)
    dpool = rs_ffn.add_and_ici_start(dw_out, dpool)
    du, d_wbd, d_b_pool, d_scale = pool_bwd(dpool, mixed, wbd, b_pool2, scale2)
    dqkv = [attn_bwd(t, da, ls, dl, d) for t, da, ls, dl, d in zip(qkv, (da1[None], da4, da16), lses, (dl1[None], dl4, dl16), DILATIONS)]
    grad_x, d_sh_m, d_sc_m, d_g_pre_mix, dw_in = inproj_bwd(dqkv, du, x2, dx1, w_in_g, g_pre_mix, mod6, tc, tsa, tsb, h1)

    z1 = jnp.zeros((1, D_MODEL), F32)
    slab_a = jnp.concatenate(
        [d_sh_m, d_sc_m, d_gt_m, d_sh_f, d_sc_f, d_gt_f, d_g_pre_mix, d_g_post_mix, d_g_pre_ffn, d_g_post_ffn,
         jnp.concatenate([d_b_pool, d_scale, loss_v, jnp.zeros((1, 384), F32)], axis=1)] + [z1] * 5, axis=0)
    slab_b = jnp.concatenate([d_conv_w, d_conv_b, jnp.zeros((4, D_FF), F32)], axis=0)
    d_wpool = jnp.concatenate([d_wbd[gi * 64:(gi + 1) * 64, gi * 64:(gi + 1) * 64] for gi in range(4)], axis=0)
    slab_a_g, slab_b_g, wpool_g = gather_small([slab_a, slab_b, d_wpool])
    cw_cols = conv_w.shape[2]
    convw_g = lax.dynamic_slice(slab_b_g, (0, 0, chip * cw_cols), (N_DEV, 3, cw_cols))
    rs_mix = GradReduce("mix", [dw_in, dw_out], [256, 256], place)
    slab_a_g = rs_mix.d2d_start(slab_a_g)
    slab_a_g = rs_ffn.final_and_share_start(slab_a_g, slab_a_g)
    slab_a_g = rs_mix.add_and_ici_start(slab_a_g, slab_a_g)
    dmod_cols = lax.dynamic_slice(slab_a_g[:, :6, :].reshape(N_DEV, 6 * D_MODEL), (0, chip * cb_ada), (N_DEV, cb_ada))

    res = {}

    def big_adamw(name, w, g, m, v, tr):
        d_, m_, v_ = adamw_rows(w[0], g, m[0], v[0], tr, "adamw_" + name)
        res[name] = (g[None], d_[None], m_[None], v_[None])
        return v_

    g_ada, d_ada, m_ada, v_ada = adamw_ada(c_all.reshape(N_DEV, D_MODEL).T, dmod_cols, w_ada[0], m_w_ada[0], v_w_ada[0])
    res["w_ada"] = (g_ada[None], d_ada[None], m_ada[None], v_ada[None])
    g_w_up, g_w_down = rs_ffn.finish(v_ada)
    big_adamw("w_up", w_up, g_w_up, m_w_up, v_w_up, 256)
    last = big_adamw("w_down", w_down, g_w_down, m_w_down, v_w_down, 352)
    rs_mix.final_and_share_start(last, jnp.zeros((8, 128), F32))
    g_w_in, g_w_out = rs_mix.finish(last)
    big_adamw("w_in", w_in, g_w_in, m_w_in, v_w_in, 256)
    big_adamw("w_out", w_out, g_w_out, m_w_out, v_w_out, 256)
    flat = lambda t: t.reshape(1, POOL_W)
    wp = lambda t: t.reshape(POOL_W, 64)
    small, loss_sum = adamw_small(slab_a_g, slab_b_g, convw_g, wpool_g, {
        "b_ada": (b_ada, m_b_ada, v_b_ada), "g_pre_mix": (g_pre_mix, m_g_pre_mix, v_g_pre_mix),
        "g_post_mix": (g_post_mix, m_g_post_mix, v_g_post_mix), "g_pre_ffn": (g_pre_ffn, m_g_pre_ffn, v_g_pre_ffn),
        "g_post_ffn": (g_post_ffn, m_g_post_ffn, v_g_post_ffn), "b_pool": (flat(b_pool), flat(m_b_pool), flat(v_b_pool)),
        "pool_scale": (pool_scale, m_pool_scale, v_pool_scale), "conv_b": (conv_b, m_conv_b, v_conv_b),
        "conv_w": (conv_w[0], m_conv_w[0], v_conv_w[0]), "w_pool": (wp(w_pool), wp(m_w_pool), wp(v_w_pool))})
    for name in ("b_ada", "g_pre_mix", "g_post_mix", "g_pre_ffn", "g_post_ffn", "pool_scale", "conv_b"):
        res[name] = tuple(small[name])
    res["b_pool"] = tuple(t.reshape(1, 4, 64) for t in small["b_pool"])
    res["conv_w"] = tuple(t[None] for t in small["conv_w"])
    res["w_pool"] = tuple(t.reshape(1, 4, 64, 64) for t in small["w_pool"])

    loss = loss_sum[0, 0]
    order = ["w_ada", "b_ada", "g_pre_mix", "g_post_mix", "g_pre_ffn", "g_post_ffn", "w_in", "w_pool", "b_pool", "pool_scale",
             "w_out", "w_up", "conv_w", "conv_b", "w_down"]
    outs = [loss, grad_x[None]]
    for k in range(4):
        outs += [res[n][k] for n in order]
    return tuple(outs)
```

```python
import functools
import math

import jax
import jax.numpy as jnp
from jax import lax
from jax.experimental import pallas as pl
from jax.experimental.pallas import tpu as pltpu

F32 = jnp.float32
BF16 = jnp.bfloat16
MESH = pl.DeviceIdType.MESH

D_MODEL = 1024
HEAD_DIM = 64
POOL_W = 256
GROUP_W = 256
DILATIONS = (1, 4, 16)
ATT_BLOCK = 128
IN_W = 2560
D_FF = 2816
HALF_FF = 1408
ROT_DIM = 16
ROPE_THETA = 500000.0
NORM_EPS = 1e-6
N_CHIPS = 4
N_DEV = 8
NEG = -1e30

ADAM_LR = 0.001
ADAM_B1 = 0.9
ADAM_B2 = 0.999
ADAM_EPS = 1e-08
ADAM_WD = 0.01
ADAM_STEP = 10

VMEM_LIMIT = 56 * 1024 * 1024

NT = (((1,), (1,)), ((), ()))
TN = (((0,), (0,)), ((), ()))


def _params(n_grid=0, **kw):
    sem = ("arbitrary",) * n_grid if n_grid else None
    return pltpu.CompilerParams(dimension_semantics=sem, vmem_limit_bytes=VMEM_LIMIT, **kw)


def _full(shape):
    nd = len(shape)
    return pl.BlockSpec(tuple(shape), lambda *_: (0,) * nd, pipeline_mode=pl.Buffered(1))


def _rows(tm, ncol):
    return pl.BlockSpec((tm, ncol), lambda i: (i, 0))


def _acc(ref, val):
    @pl.when(pl.program_id(0) == 0)
    def _():
        ref[...] = jnp.zeros_like(ref)

    ref[...] += val


def _colsum(v):
    return jnp.sum(v, axis=0, keepdims=True)


def _rope128(t, cs, sa, sb, sign):
    return t * cs + sign * (pltpu.roll(t, 8, 1) * sa + pltpu.roll(t, 120, 1) * sb)


FF_CHUNKS = tuple((ch, off, w) for ch in range(2) for off, w in ((0, 512), (512, 512), (1024, 384)))
GELU_C0 = math.sqrt(2.0 / math.pi)
GELU_C1 = GELU_C0 * 0.044715


def _gelu(z):
    z2 = z * z
    t = jnp.tanh(z * (GELU_C0 + GELU_C1 * z2))
    u = 0.5 * t + 0.5
    return z * u, u, t, z2


def _gelu_grad(z, u, t, z2):
    return u + (z * (GELU_C0 + (3.0 * GELU_C1) * z2)) * (0.5 - 0.5 * (t * t))


def _conv_taps(gate, halo, first):
    row = lax.broadcasted_iota(jnp.int32, gate.shape, 0)
    halo = jnp.where(first, 0.0, halo)
    nh = halo.shape[0]
    p1 = halo[nh - 1:nh, :]
    p2 = halo[nh - 2:nh - 1, :]
    g1 = jnp.where(row == 0, p1, pltpu.roll(gate, 1, 0))
    g2 = jnp.where(row == 0, p2, jnp.where(row == 1, p1, pltpu.roll(gate, 2, 0)))
    return g1, g2


def inproj_fwd(x, g, mod6, w_in_g, tc, tsa, tsb, tm=512):
    S = x.shape[0]

    def body(x_ref, g_ref, mod_ref, w_ref, tc_ref, tsa_ref, tsb_ref, h_ref, u_ref, q1_ref, q4_ref, q16_ref, scr):
        qkv_refs = (q1_ref, q4_ref, q16_ref)
        xv = x_ref[...]
        rstd = lax.rsqrt(jnp.mean(xv * xv, axis=-1, keepdims=True) + NORM_EPS)
        h = ((xv * rstd) * g_ref[...]) * (1.0 + mod_ref[1:2, :]) + mod_ref[0:1, :]
        hb = h.astype(BF16)
        h_ref[...] = hb
        cs, sa, sb = tc_ref[...], tsa_ref[...], tsb_ref[...]
        for j in range(N_CHIPS):
            res = jnp.dot(hb, w_ref[j], preferred_element_type=F32)
            for t in range(5):
                sp = 5 * j + t
                piece, half = sp // 2, sp % 2
                blk = res[:, t * 128:(t + 1) * 128]
                lanes = slice(half * 128, (half + 1) * 128)
                if piece == 0:
                    u_ref[:, lanes] = blk
                else:
                    kind, gi = (piece - 1) // 3, (piece - 1) % 3
                    if kind == 0:
                        blk = _rope128(blk, cs, sa, sb, 1.0) * (HEAD_DIM ** -0.5)
                    elif kind == 1:
                        blk = _rope128(blk, cs, sa, sb, 1.0)
                    d = DILATIONS[gi]
                    if d == 1:
                        q1_ref[kind, 0, :, lanes] = blk.astype(BF16)
                    else:
                        scr[...] = blk
                        for r in range(d):
                            qkv_refs[gi][kind, r, :, lanes] = scr[pl.ds(r, tm // d, stride=d), :].astype(BF16)

    cls = lambda d: pl.BlockSpec((3, d, tm // d, GROUP_W), lambda i: (0, 0, i, 0))
    return pl.pallas_call(
        body,
        name="inproj_fwd",
        grid=(S // tm,),
        in_specs=[_rows(tm, D_MODEL), _full((1, D_MODEL)), _full((6, D_MODEL)), _full(w_in_g.shape),
                  _rows(tm, 128), _rows(tm, 128), _rows(tm, 128)],
        out_specs=[_rows(tm, D_MODEL), _rows(tm, POOL_W)] + [cls(d) for d in DILATIONS],
        out_shape=[jax.ShapeDtypeStruct((S, D_MODEL), BF16), jax.ShapeDtypeStruct((S, POOL_W), F32)]
        + [jax.ShapeDtypeStruct((3, d, S // d, GROUP_W), BF16) for d in DILATIONS],
        scratch_shapes=[pltpu.VMEM((tm, 128), F32)],
        compiler_params=_params(1),
    )(x, g, mod6, w_in_g, tc, tsa, tsb)


def _attn_masks():
    row = lax.broadcasted_iota(jnp.int32, (2 * ATT_BLOCK, 2 * ATT_BLOCK), 0) % ATT_BLOCK
    col = lax.broadcasted_iota(jnp.int32, (2 * ATT_BLOCK, 2 * ATT_BLOCK), 1)
    band = (col >= row) & (col <= row + ATT_BLOCK)
    lane = lax.broadcasted_iota(jnp.int32, (ATT_BLOCK, 128), 1)
    return band, col, lane < HEAD_DIM


def _stack_heads(t, lo):
    z = jnp.zeros_like(t)
    return jnp.concatenate([jnp.where(lo, t, z), jnp.where(lo, z, t)], axis=0)


def _unstack_heads(t2, lo):
    return jnp.where(lo, t2[:ATT_BLOCK], t2[ATT_BLOCK:])


def attn_fwd(qkv, d):
    L = qkv.shape[2]
    nb = L // ATT_BLOCK

    def body(q_ref, k_ref, v_ref, o_ref, l_ref, kpad, vpad):
        kpad[0:ATT_BLOCK, :] = jnp.zeros((ATT_BLOCK, GROUP_W), BF16)
        vpad[0:ATT_BLOCK, :] = jnp.zeros((ATT_BLOCK, GROUP_W), BF16)
        kpad[ATT_BLOCK:, :] = k_ref[...]
        vpad[ATT_BLOCK:, :] = v_ref[...]
        band, col, lo = _attn_masks()

        def step(n, carry):
            r0 = pl.multiple_of(n * ATT_BLOCK, ATT_BLOCK)
            valid = band & ((col >= ATT_BLOCK) | (n > 0))
            qb = q_ref[pl.ds(r0, ATT_BLOCK), :]
            kb = kpad[pl.ds(r0, 2 * ATT_BLOCK), :]
            vb = vpad[pl.ds(r0, 2 * ATT_BLOCK), :]
            for pair in range(2):
                lanes = slice(pair * 128, (pair + 1) * 128)
                qp, kp, vp = qb[:, lanes], kb[:, lanes], vb[:, lanes]
                s = lax.dot_general(_stack_heads(qp, lo), kp, NT, preferred_element_type=F32)
                s = jnp.where(valid, s, NEG)
                m = jnp.max(s, axis=1, keepdims=True)
                p = jnp.exp(s - m)
                den = jnp.sum(p, axis=1, keepdims=True)
                pv = jnp.dot(p.astype(BF16), vp, preferred_element_type=F32)
                o_ref[pl.ds(r0, ATT_BLOCK), lanes] = _unstack_heads(pv / den, lo)
                l_ref[pl.ds(r0, ATT_BLOCK), lanes] = _unstack_heads(jnp.broadcast_to(m + jnp.log(den), pv.shape), lo)
            return carry

        lax.fori_loop(0, nb, step, 0, unroll=min(4, nb))

    spec = lambda kind: pl.BlockSpec((None, None, L, GROUP_W), lambda r: (kind, r, 0, 0))
    return pl.pallas_call(
        body,
        name=f"attn_fwd_d{d}",
        grid=(d,),
        in_specs=[spec(0), spec(1), spec(2)],
        out_specs=[pl.BlockSpec((None, L, GROUP_W), lambda r: (r, 0, 0))] * 2,
        out_shape=[jax.ShapeDtypeStruct((d, L, GROUP_W), F32)] * 2,
        scratch_shapes=[pltpu.VMEM((L + ATT_BLOCK, GROUP_W), BF16)] * 2,
        compiler_params=_params(1),
    )(qkv, qkv, qkv)


def _pool_lane_windows(shape):
    lane = lax.broadcasted_iota(jnp.int32, shape, 1)
    return lane, jnp.where(lane < 64, 2, jnp.where(lane < 128, 4, jnp.where(lane < 192, 8, 16)))


def pool_fwd(u, wbd, b, scale):
    S = u.shape[0]

    def body(u_ref, w_ref, b_ref, s_ref, mixed_ref, out_ref):
        uv = u_ref[...]
        row = lax.broadcasted_iota(jnp.int32, uv.shape, 0)
        lane, win = _pool_lane_windows(uv.shape)

        def shift(a, k):
            return jnp.where(row >= k, pltpu.roll(a, k, 0), 0.0)

        s2 = uv + shift(uv, 1)
        s4 = s2 + shift(s2, 2)
        s8 = s4 + shift(s4, 4)
        s16 = s8 + shift(s8, 8)
        tsum = jnp.where(lane < 64, s2, jnp.where(lane < 128, s4, jnp.where(lane < 192, s8, s16)))
        cnt = jnp.minimum(row + 1, win).astype(F32)
        mb = (tsum / cnt - uv).astype(BF16)
        mixed_ref[...] = mb
        y = jnp.dot(mb, w_ref[...], preferred_element_type=F32) + b_ref[...]
        out_ref[...] = (y * s_ref[...]).astype(BF16)

    vm = pl.BlockSpec(memory_space=pltpu.VMEM)
    return pl.pallas_call(
        body,
        name="pool_fwd",
        in_specs=[vm] * 4,
        out_specs=[vm] * 2,
        out_shape=[jax.ShapeDtypeStruct((S, POOL_W), BF16)] * 2,
        compiler_params=_params(),
    )(u, wbd, b, scale)


def outproj_fwd(o_l, pool, x, w_out_g, g_post, g_pre, mod6, tm=512):
    S = x.shape[0]

    def body(o0, o1, o2, l0, l1, l2, pool_ref, x_ref, w_ref, gpost_ref, gpre_ref, mod_ref,
             cat_ref, lse_ref, lse4_ref, lse16_ref, y1_ref, x1_ref, h2_ref, so4, sl4, so16, sl16):
        for d, src, dst in ((4, o1, so4), (4, l1, sl4), (16, o2, so16), (16, l2, sl16)):
            for r in range(d):
                for h in range(2):
                    dst[h, pl.ds(r, tm // d, stride=d), :] = src[r, :, h * 128:(h + 1) * 128]
        nat = lambda ref: jnp.concatenate([ref[0], ref[1]], axis=1)
        a, b, c = l0[0], nat(sl4), nat(sl16)
        m = jnp.maximum(jnp.maximum(a, b), c)
        e0, e1, e2 = jnp.exp(a - m), jnp.exp(b - m), jnp.exp(c - m)
        z = e0 + e1 + e2
        lse = m + jnp.log(z)
        lse_ref[...] = lse
        for h in range(2):
            sl4[h] = lse[:, h * 128:(h + 1) * 128]
        for d, dst in ((4, lse4_ref), (16, lse16_ref)):
            for r in range(d):
                for h in range(2):
                    dst[r, :, h * 128:(h + 1) * 128] = sl4[h, pl.ds(r, tm // d, stride=d), :]
        attn = (e0 * o0[0] + e1 * nat(so4) + e2 * nat(so16)) / z
        cat = jnp.concatenate([pool_ref[...], attn.astype(BF16)], axis=1)
        cat_ref[...] = cat
        y1 = jnp.concatenate([jnp.dot(cat, w_ref[j], preferred_element_type=F32) for j in range(N_CHIPS)], axis=1)
        y1_ref[...] = y1
        rstd = lax.rsqrt(jnp.mean(y1 * y1, axis=-1, keepdims=True) + NORM_EPS)
        x1 = x_ref[...] + mod_ref[2:3, :] * ((y1 * rstd) * gpost_ref[...])
        x1_ref[...] = x1
        rstd2 = lax.rsqrt(jnp.mean(x1 * x1, axis=-1, keepdims=True) + NORM_EPS)
        h2 = ((x1 * rstd2) * gpre_ref[...]) * (1.0 + mod_ref[4:5, :]) + mod_ref[3:4, :]
        h2_ref[...] = h2.astype(BF16)

    t256 = _rows(tm, GROUP_W)
    cls = lambda d: pl.BlockSpec((d, tm // d, GROUP_W), lambda i: (0, i, 0))
    cls_shape = lambda d: jax.ShapeDtypeStruct((d, S // d, GROUP_W), F32)
    return pl.pallas_call(
        body,
        name="outproj_fwd",
        grid=(S // tm,),
        in_specs=[cls(d) for d in DILATIONS] * 2 + [t256, _rows(tm, D_MODEL), _full(w_out_g.shape), _full((1, D_MODEL)),
                                                    _full((1, D_MODEL)), _full((6, D_MODEL))],
        out_specs=[_rows(tm, 512), t256, cls(4), cls(16), _rows(tm, D_MODEL), _rows(tm, D_MODEL), _rows(tm, D_MODEL)],
        out_shape=[jax.ShapeDtypeStruct((S, 512), BF16), jax.ShapeDtypeStruct((S, GROUP_W), F32), cls_shape(4), cls_shape(16),
                   jax.ShapeDtypeStruct((S, D_MODEL), F32), jax.ShapeDtypeStruct((S, D_MODEL), F32),
                   jax.ShapeDtypeStruct((S, D_MODEL), BF16)],
        scratch_shapes=[pltpu.VMEM((2, tm, 128), F32)] * 4,
        compiler_params=_params(1),
    )(*o_l, pool, x, w_out_g, g_post, g_pre, mod6)


def up_fwd(h2, w_up_g, tm=512):
    S = h2.shape[0]

    def body(h_ref, w_ref, gate_ref, val_ref):
        hb = h_ref[...]
        for j in range(N_CHIPS):
            res = jnp.dot(hb, w_ref[j], preferred_element_type=F32).astype(BF16)
            dst = gate_ref if j < 2 else val_ref
            dst[:, (j % 2) * HALF_FF:(j % 2 + 1) * HALF_FF] = res

    return pl.pallas_call(
        body,
        name="up_fwd",
        grid=(S // tm,),
        in_specs=[_rows(tm, D_MODEL), _full(w_up_g.shape)],
        out_specs=[_rows(tm, D_FF)] * 2,
        out_shape=[jax.ShapeDtypeStruct((S, D_FF), BF16)] * 2,
        compiler_params=_params(1),
    )(h2, w_up_g)


def _halo_prev(tm, ncol):
    return pl.BlockSpec((16, ncol), lambda i: (jnp.maximum(i * (tm // 16) - 1, 0), 0))


def down_fwd(gate, val, conv_w, conv_b, w_down, x1, target, g_post, mod6, tm=256):
    S = x1.shape[0]

    def body(gate_ref, halo_ref, val_ref, cw_ref, cb_ref, w_ref, x1_ref, tgt_ref, g_ref, mod_ref,
             a_ref, dy2_ref, dout_ref, loss_ref, dgt_ref, dg_ref):
        first = pl.program_id(0) == 0
        y2 = jnp.zeros((tm, D_MODEL), F32)
        for ch in range(2):
            cols = slice(ch * HALF_FF, (ch + 1) * HALF_FF)
            gt = gate_ref[:, cols].astype(F32)
            g1, g2 = _conv_taps(gt, halo_ref[:, cols].astype(F32), first)
            gc = g2 * cw_ref[0:1, cols] + g1 * cw_ref[1:2, cols] + gt * cw_ref[2:3, cols] + cb_ref[:, cols]
            ge = _gelu(gc)[0]
            ab = (ge * val_ref[:, cols].astype(F32)).astype(BF16)
            a_ref[:, cols] = ab
            y2 = y2 + jnp.dot(ab, w_ref[cols, :], preferred_element_type=F32)
        rstd = lax.rsqrt(jnp.mean(y2 * y2, axis=-1, keepdims=True) + NORM_EPS)
        y2n = y2 * rstd
        gv = g_ref[...]
        gtf = mod_ref[5:6, :]
        r2 = y2n * gv
        diff = (x1_ref[...] + gtf * r2) - tgt_ref[...]
        _acc(loss_ref, jnp.zeros((1, 128), F32) + 0.5 * jnp.sum(diff * diff) * (1.0 / D_MODEL))
        dout = diff * (1.0 / D_MODEL)
        dout_ref[...] = dout
        _acc(dgt_ref, _colsum(dout * r2))
        dr2 = dout * gtf
        _acc(dg_ref, _colsum(dr2 * y2n))
        dyn = dr2 * gv
        dy2 = rstd * (dyn - y2n * jnp.mean(dyn * y2n, axis=-1, keepdims=True))
        dy2_ref[...] = dy2.astype(BF16)

    vec = _full((1, D_MODEL))
    return pl.pallas_call(
        body,
        name="down_fwd",
        grid=(S // tm,),
        in_specs=[_rows(tm, D_FF), _halo_prev(tm, D_FF), _rows(tm, D_FF), _full((3, D_FF)), _full((1, D_FF)),
                  _full((D_FF, D_MODEL)), _rows(tm, D_MODEL), _rows(tm, D_MODEL), vec, _full((6, D_MODEL))],
        out_specs=[_rows(tm, D_FF), _rows(tm, D_MODEL), _rows(tm, D_MODEL), _full((1, 128)), vec, vec],
        out_shape=[jax.ShapeDtypeStruct((S, D_FF), BF16), jax.ShapeDtypeStruct((S, D_MODEL), BF16),
                   jax.ShapeDtypeStruct((S, D_MODEL), F32), jax.ShapeDtypeStruct((1, 128), F32),
                   jax.ShapeDtypeStruct((1, D_MODEL), F32), jax.ShapeDtypeStruct((1, D_MODEL), F32)],
        compiler_params=_params(1),
    )(gate, gate, val, conv_w, conv_b, w_down, x1, target, g_post, mod6)


def ffn_fwd(h2, w_up_g, conv_w, conv_b, w_down, x1, target, g_post, mod6, tm=256):
    S = x1.shape[0]

    def body(h_ref, wu_ref, cw_ref, cb_ref, wd_ref, x1_ref, tgt_ref, g_ref, mod_ref,
             gate_ref, val_ref, a_ref, dy2_ref, dout_ref, loss_ref, dgt_ref, dg_ref, carry):
        first = pl.program_id(0) == 0

        @pl.when(first)
        def _():
            carry[...] = jnp.zeros_like(carry)

        hb = h_ref[...]
        y2 = jnp.zeros((tm, D_MODEL), F32)
        for ch in range(2):
            cols = slice(ch * HALF_FF, (ch + 1) * HALF_FF)
            gb = jnp.dot(hb, wu_ref[ch], preferred_element_type=F32).astype(BF16)
            vb = jnp.dot(hb, wu_ref[2 + ch], preferred_element_type=F32).astype(BF16)
            gate_ref[:, cols] = gb
            val_ref[:, cols] = vb
            gt = gb.astype(F32)
            g1, g2 = _conv_taps(gt, carry[:, cols], first)
            carry[:, cols] = gt[tm - 8:, :]
            gc = g2 * cw_ref[0:1, cols] + g1 * cw_ref[1:2, cols] + gt * cw_ref[2:3, cols] + cb_ref[:, cols]
            ab = (_gelu(gc)[0] * vb.astype(F32)).astype(BF16)
            a_ref[:, cols] = ab
            y2 = y2 + jnp.dot(ab, wd_ref[cols, :], preferred_element_type=F32)
        rstd = lax.rsqrt(jnp.mean(y2 * y2, axis=-1, keepdims=True) + NORM_EPS)
        y2n = y2 * rstd
        gv = g_ref[...]
        gtf = mod_ref[5:6, :]
        r2 = y2n * gv
        diff = (x1_ref[...] + gtf * r2) - tgt_ref[...]
        _acc(loss_ref, jnp.zeros((1, 128), F32) + 0.5 * jnp.sum(diff * diff) * (1.0 / D_MODEL))
        dout = diff * (1.0 / D_MODEL)
        dout_ref[...] = dout
        _acc(dgt_ref, _colsum(dout * r2))
        dr2 = dout * gtf
        _acc(dg_ref, _colsum(dr2 * y2n))
        dyn = dr2 * gv
        dy2 = rstd * (dyn - y2n * jnp.mean(dyn * y2n, axis=-1, keepdims=True))
        dy2_ref[...] = dy2.astype(BF16)

    vec = _full((1, D_MODEL))
    return pl.pallas_call(
        body,
        name="ffn_fwd",
        grid=(S // tm,),
        in_specs=[_rows(tm, D_MODEL), _full(w_up_g.shape), _full((3, D_FF)), _full((1, D_FF)), _full((D_FF, D_MODEL)),
                  _rows(tm, D_MODEL), _rows(tm, D_MODEL), vec, _full((6, D_MODEL))],
        out_specs=[_rows(tm, D_FF), _rows(tm, D_FF), _rows(tm, D_FF), _rows(tm, D_MODEL), _rows(tm, D_MODEL), _full((1, 128)), vec, vec],
        out_shape=[jax.ShapeDtypeStruct((S, D_FF), BF16)] * 3 + [jax.ShapeDtypeStruct((S, D_MODEL), BF16),
                                                                 jax.ShapeDtypeStruct((S, D_MODEL), F32),
                                                                 jax.ShapeDtypeStruct((1, 128), F32),
                                                                 jax.ShapeDtypeStruct((1, D_MODEL), F32),
                                                                 jax.ShapeDtypeStruct((1, D_MODEL), F32)],
        scratch_shapes=[pltpu.VMEM((8, D_FF), F32)],
        compiler_params=_params(1),
    )(h2, w_up_g, conv_w, conv_b, w_down, x1, target, g_post, mod6)


def down_bwd(dy2, w_down, gate, val, conv_w, conv_b, a, h2, tm=256):
    S = dy2.shape[0]

    def body(dy_ref, w_ref, gate_ref, halo_ref, val_ref, cw_ref, cb_ref, a_ref, h_ref,
             dgc_ref, dval_ref, dcw_ref, dcb_ref, dwd_ref, dwu_ref):
        first = pl.program_id(0) == 0

        @pl.when(first)
        def _():
            dcw_ref[...] = jnp.zeros_like(dcw_ref)
            dcb_ref[...] = jnp.zeros_like(dcb_ref)
            dwd_ref[...] = jnp.zeros_like(dwd_ref)
            dwu_ref[...] = jnp.zeros_like(dwu_ref)

        dyb = dy_ref[...]
        hb = h_ref[...]
        def col(i):
            ch, off, width = FF_CHUNKS[i]
            return slice(ch * HALF_FF + off, ch * HALF_FF + off + width)

        def mm_da(i):
            return lax.dot_general(dyb, w_ref[col(i), :], NT, preferred_element_type=F32)

        def elementwise(i, da):
            cols = col(i)
            gt = gate_ref[:, cols].astype(F32)
            g1, g2 = _conv_taps(gt, halo_ref[:, cols].astype(F32), first)
            gc = g2 * cw_ref[0:1, cols] + g1 * cw_ref[1:2, cols] + gt * cw_ref[2:3, cols] + cb_ref[:, cols]
            ge, u, th, z2 = _gelu(gc)
            dgc = da * val_ref[:, cols].astype(F32) * _gelu_grad(gc, u, th, z2)
            dgc_ref[:, cols] = dgc.astype(BF16)
            dvb = (da * ge).astype(BF16)
            dval_ref[:, cols] = dvb
            dcb_ref[:, cols] += _colsum(dgc)
            dcw_ref[0:1, cols] += _colsum(dgc * g2)
            dcw_ref[1:2, cols] += _colsum(dgc * g1)
            dcw_ref[2:3, cols] += _colsum(dgc * gt)
            return dvb

        def mm_dw(i, dvb):
            ch, off, width = FF_CHUNKS[i]
            dwd_ref[col(i), :] += lax.dot_general(a_ref[:, col(i)], dyb, TN, preferred_element_type=F32)
            dwu_ref[ch, :, off:off + width] += lax.dot_general(hb, dvb, TN, preferred_element_type=F32)

        n = len(FF_CHUNKS)
        da = mm_da(0)
        prev = None
        for i in range(n):
            nxt = mm_da(i + 1) if i + 1 < n else None
            if prev is not None:
                mm_dw(i - 1, prev)
            prev = elementwise(i, da)
            da = nxt
        mm_dw(n - 1, prev)

    return pl.pallas_call(
        body,
        name="down_bwd",
        grid=(S // tm,),
        in_specs=[_rows(tm, D_MODEL), _full((D_FF, D_MODEL)), _rows(tm, D_FF), _halo_prev(tm, D_FF), _rows(tm, D_FF),
                  _full((3, D_FF)), _full((1, D_FF)), _rows(tm, D_FF), _rows(tm, D_MODEL)],
        out_specs=[_rows(tm, D_FF), _rows(tm, D_FF), _full((3, D_FF)), _full((1, D_FF)), _full((D_FF, D_MODEL)),
                   pl.BlockSpec((2, D_MODEL, HALF_FF), lambda i: (1, 0, 0), pipeline_mode=pl.Buffered(1))],
        out_shape=[jax.ShapeDtypeStruct((S, D_FF), BF16), jax.ShapeDtypeStruct((S, D_FF), BF16),
                   jax.ShapeDtypeStruct((3, D_FF), F32), jax.ShapeDtypeStruct((1, D_FF), F32),
                   jax.ShapeDtypeStruct((D_FF, D_MODEL), F32), jax.ShapeDtypeStruct((N_CHIPS, D_MODEL, HALF_FF), F32)],
        compiler_params=_params(1),
    )(dy2, w_down, gate, gate, val, conv_w, conv_b, a, h2)


def dw_matmul(a, b, out_blocks, blk_shape, a_cols, b_cols, a_blocked, name, prev=None, blk_off=0, n_blk=None, tm=512):
    S = a.shape[0]
    n_blk = out_blocks if n_blk is None else n_blk

    def body(*refs):
        a_ref, b_ref, o_ref = refs[0], refs[1], refs[-1]

        @pl.when(pl.program_id(1) == 0)
        def _():
            o_ref[...] = jnp.zeros_like(o_ref)

        o_ref[...] += lax.dot_general(a_ref[...], b_ref[...], TN, preferred_element_type=F32)

    a_spec = pl.BlockSpec((tm, a_cols), (lambda j, i: (i, j)) if a_blocked else (lambda j, i: (i, 0)))
    b_spec = pl.BlockSpec((tm, b_cols), (lambda j, i: (i, 0)) if a_blocked else (lambda j, i: (i, j)))
    in_specs = [a_spec, b_spec]
    args = [a, b]
    aliases = {}
    if prev is not None:
        in_specs.append(pl.BlockSpec(memory_space=pl.ANY))
        args.append(prev)
        aliases = {2: 0}
    return pl.pallas_call(
        body,
        name=name,
        grid=(n_blk, S // tm),
        in_specs=in_specs,
        out_specs=pl.BlockSpec((None,) + tuple(blk_shape), lambda j, i: (j + blk_off, 0, 0)),
        out_shape=jax.ShapeDtypeStruct((out_blocks,) + tuple(blk_shape), F32),
        input_output_aliases=aliases,
        compiler_params=_params(2),
    )(*args)


def up_bwd(dgc, dval, conv_w, w_up_g, x1, dout, y1, g_pre, g_post, mod6, h2, dw_up, tm=256):
    S = x1.shape[0]
    last_blk = S // 16 - 1

    def body(dgc_ref, nxt_ref, dval_ref, cw_ref, w_ref, x1_ref, dout_ref, y1_ref, gpre_ref, gpost_ref, mod_ref, h_ref, dwin_ref,
             dx1_ref, dy1_ref, dsh_ref, dsc_ref, dgpre_ref, dgt_ref, dgpost_ref, dwu_ref):
        last = pl.program_id(0) == pl.num_programs(0) - 1

        @pl.when(pl.program_id(0) == 0)
        def _():
            dwu_ref[...] = jnp.zeros_like(dwu_ref)

        hb = h_ref[...]
        dh = jnp.zeros((tm, D_MODEL), F32)
        for ch in range(2):
            cols = slice(ch * HALF_FF, (ch + 1) * HALF_FF)
            dg = dgc_ref[:, cols].astype(F32)
            nx = jnp.where(last, 0.0, nxt_ref[:, cols].astype(F32))
            row = lax.broadcasted_iota(jnp.int32, dg.shape, 0)
            n0, n1 = nx[0:1, :], nx[1:2, :]
            u1 = jnp.where(row == tm - 1, n0, pltpu.roll(dg, tm - 1, 0))
            u2 = jnp.where(row == tm - 1, n1, jnp.where(row == tm - 2, n0, pltpu.roll(dg, tm - 2, 0)))
            dgate = (dg * cw_ref[2:3, cols] + u1 * cw_ref[1:2, cols] + u2 * cw_ref[0:1, cols]).astype(BF16)
            dwu_ref[ch] += lax.dot_general(hb, dgate, TN, preferred_element_type=F32)
            dh = dh + lax.dot_general(dgate, w_ref[ch], NT, preferred_element_type=F32)
            dh = dh + lax.dot_general(dval_ref[:, cols], w_ref[2 + ch], NT, preferred_element_type=F32)
        x1 = x1_ref[...]
        rstd = lax.rsqrt(jnp.mean(x1 * x1, axis=-1, keepdims=True) + NORM_EPS)
        n2 = x1 * rstd
        gpre = gpre_ref[...]
        one_sc = 1.0 + mod_ref[4:5, :]
        _acc(dsh_ref, _colsum(dh))
        _acc(dsc_ref, _colsum(dh * (n2 * gpre)))
        _acc(dgpre_ref, _colsum(dh * one_sc * n2))
        dn = dh * (gpre * one_sc)
        dx1 = dout_ref[...] + rstd * (dn - n2 * jnp.mean(dn * n2, axis=-1, keepdims=True))
        dx1_ref[...] = dx1
        y1 = y1_ref[...]
        rstd1 = lax.rsqrt(jnp.mean(y1 * y1, axis=-1, keepdims=True) + NORM_EPS)
        y1n = y1 * rstd1
        gpost = gpost_ref[...]
        gtm = mod_ref[2:3, :]
        _acc(dgt_ref, _colsum(dx1 * (y1n * gpost)))
        dr1 = dx1 * gtm
        _acc(dgpost_ref, _colsum(dr1 * y1n))
        dyn = dr1 * gpost
        dy1 = rstd1 * (dyn - y1n * jnp.mean(dyn * y1n, axis=-1, keepdims=True))
        dy1_ref[...] = dy1.astype(BF16)

    vec = _full((1, D_MODEL))
    nxt = pl.BlockSpec((16, D_FF), lambda i: (jnp.minimum((i + 1) * (tm // 16), last_blk), 0))
    return pl.pallas_call(
        body,
        name="up_bwd",
        grid=(S // tm,),
        in_specs=[_rows(tm, D_FF), nxt, _rows(tm, D_FF), _full((3, D_FF)), _full(w_up_g.shape), _rows(tm, D_MODEL),
                  _rows(tm, D_MODEL), _rows(tm, D_MODEL), vec, vec, _full((6, D_MODEL)), _rows(tm, D_MODEL),
                  pl.BlockSpec(memory_space=pl.ANY)],
        out_specs=[_rows(tm, D_MODEL), _rows(tm, D_MODEL), vec, vec, vec, vec, vec,
                   pl.BlockSpec((2, D_MODEL, HALF_FF), lambda i: (0, 0, 0), pipeline_mode=pl.Buffered(1))],
        out_shape=[jax.ShapeDtypeStruct((S, D_MODEL), F32), jax.ShapeDtypeStruct((S, D_MODEL), BF16)]
        + [jax.ShapeDtypeStruct((1, D_MODEL), F32)] * 5 + [jax.ShapeDtypeStruct(dw_up.shape, F32)],
        input_output_aliases={12: 7},
        compiler_params=_params(1),
    )(dgc, dgc, dval, conv_w, w_up_g, x1, dout, y1, g_pre, g_post, mod6, h2, dw_up)


def outproj_bwd(dy1, w_out_g, cat, tm=512):
    S = dy1.shape[0]

    def body(dy_ref, w_ref, cat_ref, dpool_ref, dattn_ref, da4_ref, da16_ref, delta_ref, dl4_ref, dl16_ref, dw_ref, scr):
        @pl.when(pl.program_id(0) == 0)
        def _():
            dw_ref[...] = jnp.zeros_like(dw_ref)

        catb = cat_ref[...]
        dcat = jnp.zeros((tm, 512), F32)
        for j in range(N_CHIPS):
            dyj = dy_ref[:, j * 256:(j + 1) * 256]
            dcat = dcat + lax.dot_general(dyj, w_ref[j], NT, preferred_element_type=F32)
            dw_ref[j] += lax.dot_general(catb, dyj, TN, preferred_element_type=F32)
        dpool_ref[...] = dcat[:, :POOL_W]
        dattn = dcat[:, POOL_W:]
        dattn_ref[...] = dattn.astype(BF16)
        for h in range(2):
            scr[h] = dattn[:, h * 128:(h + 1) * 128]
        for d, dst in ((4, da4_ref), (16, da16_ref)):
            for r in range(d):
                for h in range(2):
                    dst[r, :, h * 128:(h + 1) * 128] = scr[h, pl.ds(r, tm // d, stride=d), :].astype(BF16)
        prod = dattn * catb[:, POOL_W:].astype(F32)
        r = lax.broadcasted_iota(jnp.int32, (GROUP_W, GROUP_W), 0) // HEAD_DIM
        c = lax.broadcasted_iota(jnp.int32, (GROUP_W, GROUP_W), 1) // HEAD_DIM
        ones_bd = jnp.where(r == c, 1.0, 0.0).astype(BF16)
        hi = prod.astype(BF16)
        lo = (prod - hi.astype(F32)).astype(BF16)
        delta = jnp.dot(hi, ones_bd, preferred_element_type=F32) + jnp.dot(lo, ones_bd, preferred_element_type=F32)
        delta_ref[...] = delta
        for h in range(2):
            scr[h] = delta[:, h * 128:(h + 1) * 128]
        for d, dst in ((4, dl4_ref), (16, dl16_ref)):
            for r in range(d):
                for h in range(2):
                    dst[r, :, h * 128:(h + 1) * 128] = scr[h, pl.ds(r, tm // d, stride=d), :]

    cls = lambda d: pl.BlockSpec((d, tm // d, GROUP_W), lambda i: (0, i, 0))
    cls_shape = lambda d, dt: jax.ShapeDtypeStruct((d, S // d, GROUP_W), dt)
    return pl.pallas_call(
        body,
        name="outproj_bwd",
        grid=(S // tm,),
        in_specs=[_rows(tm, D_MODEL), _full(w_out_g.shape), _rows(tm, 512)],
        out_specs=[_rows(tm, POOL_W), _rows(tm, GROUP_W), cls(4), cls(16), _rows(tm, GROUP_W), cls(4), cls(16),
                   _full(w_out_g.shape)],
        out_shape=[jax.ShapeDtypeStruct((S, POOL_W), F32), jax.ShapeDtypeStruct((S, GROUP_W), BF16), cls_shape(4, BF16),
                   cls_shape(16, BF16), jax.ShapeDtypeStruct((S, GROUP_W), F32), cls_shape(4, F32), cls_shape(16, F32),
                   jax.ShapeDtypeStruct(w_out_g.shape, F32)],
        scratch_shapes=[pltpu.VMEM((2, tm, 128), F32)],
        compiler_params=_params(1),
    )(dy1, w_out_g, cat)


def attn_bwd(qkv, dattn, lse, delta, d):
    L = qkv.shape[2]
    nb = L // ATT_BLOCK

    def body(q_ref, k_ref, v_ref, do_ref, l_ref, dl_ref, out_ref, kpad, vpad, dkpad, dvpad):
        kpad[0:ATT_BLOCK, :] = jnp.zeros((ATT_BLOCK, GROUP_W), BF16)
        vpad[0:ATT_BLOCK, :] = jnp.zeros((ATT_BLOCK, GROUP_W), BF16)
        kpad[ATT_BLOCK:, :] = k_ref[...]
        vpad[ATT_BLOCK:, :] = v_ref[...]
        dkpad[...] = jnp.zeros_like(dkpad)
        dvpad[...] = jnp.zeros_like(dvpad)
        band, col, lo = _attn_masks()

        def step(n, carry):
            r0 = pl.multiple_of(n * ATT_BLOCK, ATT_BLOCK)
            valid = band & ((col >= ATT_BLOCK) | (n > 0))
            qb = q_ref[pl.ds(r0, ATT_BLOCK), :]
            dob = do_ref[pl.ds(r0, ATT_BLOCK), :]
            lb = l_ref[pl.ds(r0, ATT_BLOCK), :]
            dlb = dl_ref[pl.ds(r0, ATT_BLOCK), :]
            kb = kpad[pl.ds(r0, 2 * ATT_BLOCK), :]
            vb = vpad[pl.ds(r0, 2 * ATT_BLOCK), :]
            for pair in range(2):
                lanes = slice(pair * 128, (pair + 1) * 128)
                qp, dop, kp, vp = qb[:, lanes], dob[:, lanes], kb[:, lanes], vb[:, lanes]
                c0, c1 = pair * 128, pair * 128 + HEAD_DIM
                q2, do2 = _stack_heads(qp, lo), _stack_heads(dop, lo)
                lse2 = jnp.concatenate([lb[:, c0:c0 + 1], lb[:, c1:c1 + 1]], axis=0)
                dl2 = jnp.concatenate([dlb[:, c0:c0 + 1], dlb[:, c1:c1 + 1]], axis=0)
                s = lax.dot_general(q2, kp, NT, preferred_element_type=F32)
                s = jnp.where(valid, s, NEG)
                p = jnp.exp(s - lse2)
                dp = lax.dot_general(do2, vp, NT, preferred_element_type=F32)
                ds = (p * (dp - dl2)).astype(BF16)
                dq2 = jnp.dot(ds, kp, preferred_element_type=F32)
                out_ref[0, pl.ds(r0, ATT_BLOCK), lanes] = _unstack_heads(dq2, lo)
                dkpad[pl.ds(r0, 2 * ATT_BLOCK), lanes] += lax.dot_general(ds, q2, TN, preferred_element_type=F32)
                dvpad[pl.ds(r0, 2 * ATT_BLOCK), lanes] += lax.dot_general(p.astype(BF16), do2, TN, preferred_element_type=F32)
            return carry

        lax.fori_loop(0, nb, step, 0, unroll=min(4, nb))
        out_ref[1] = dkpad[ATT_BLOCK:, :]
        out_ref[2] = dvpad[ATT_BLOCK:, :]

    spec = lambda kind: pl.BlockSpec((None, None, L, GROUP_W), lambda r: (kind, r, 0, 0))
    cls = pl.BlockSpec((None, L, GROUP_W), lambda r: (r, 0, 0))
    return pl.pallas_call(
        body,
        name=f"attn_bwd_d{d}",
        grid=(d,),
        in_specs=[spec(0), spec(1), spec(2), cls, cls, cls],
        out_specs=pl.BlockSpec((3, None, L, GROUP_W), lambda r: (0, r, 0, 0)),
        out_shape=jax.ShapeDtypeStruct((3, d, L, GROUP_W), F32),
        scratch_shapes=[pltpu.VMEM((L + ATT_BLOCK, GROUP_W), BF16)] * 2 + [pltpu.VMEM((L + ATT_BLOCK, GROUP_W), F32)] * 2,
        compiler_params=_params(1),
    )(qkv, qkv, qkv, dattn, lse, delta)


def pool_bwd(dpool, mixed, wbd, b, scale):
    S = dpool.shape[0]

    def body(dp_ref, mx_ref, w_ref, b_ref, s_ref, du_ref, dw_ref, db_ref, ds_ref):
        dp = dp_ref[...]
        mb = mx_ref[...]
        wv = w_ref[...]
        ypre = jnp.dot(mb, wv, preferred_element_type=F32) + b_ref[...]
        ds_ref[...] = _colsum(dp * ypre)
        dpre = dp * s_ref[...]
        db_ref[...] = _colsum(dpre)
        dpb = dpre.astype(BF16)
        dw_ref[...] = lax.dot_general(mb, dpb, TN, preferred_element_type=F32)
        dmix = lax.dot_general(dpb, wv, NT, preferred_element_type=F32)
        row = lax.broadcasted_iota(jnp.int32, dmix.shape, 0)
        lane, win = _pool_lane_windows(dmix.shape)
        e = dmix / jnp.minimum(row + 1, win).astype(F32)

        def shift(a, k):
            return jnp.where(row < S - k, pltpu.roll(a, S - k, 0), 0.0)

        f2 = e + shift(e, 1)
        f4 = f2 + shift(f2, 2)
        f8 = f4 + shift(f4, 4)
        f16 = f8 + shift(f8, 8)
        du_ref[...] = jnp.where(lane < 64, f2, jnp.where(lane < 128, f4, jnp.where(lane < 192, f8, f16))) - dmix

    vm = pl.BlockSpec(memory_space=pltpu.VMEM)
    return pl.pallas_call(
        body,
        name="pool_bwd",
        in_specs=[vm] * 5,
        out_specs=[vm] * 4,
        out_shape=[jax.ShapeDtypeStruct((S, POOL_W), F32), jax.ShapeDtypeStruct((POOL_W, POOL_W), F32),
                   jax.ShapeDtypeStruct((1, POOL_W), F32), jax.ShapeDtypeStruct((1, POOL_W), F32)],
        compiler_params=_params(),
    )(dpool, mixed, wbd, b, scale)


def inproj_bwd(dqkv, du, x, dx1, w_in_g, g, mod6, tc, tsa, tsb, h1, tm=256):
    S = x.shape[0]

    def body(d0, d1, d2, du_ref, x_ref, dx1_ref, w_ref, g_ref, mod_ref, tc_ref, tsa_ref, tsb_ref, h_ref,
             gx_ref, dsh_ref, dsc_ref, dg_ref, dw_ref, s4, s16, dp_ref):
        @pl.when(pl.program_id(0) == 0)
        def _():
            dw_ref[...] = jnp.zeros_like(dw_ref)

        cs, sa, sb = tc_ref[...], tsa_ref[...], tsb_ref[...]
        for d, src, dst in ((4, d1, s4), (16, d2, s16)):
            for kind in range(3):
                for r in range(d):
                    for h in range(2):
                        dst[kind, h, pl.ds(r, tm // d, stride=d), :] = src[kind, r, :, h * 128:(h + 1) * 128]
        for sp in range(20):
            piece, half = sp // 2, sp % 2
            lanes = slice(half * 128, (half + 1) * 128)
            if piece == 0:
                blk = du_ref[:, lanes]
            else:
                kind, gi = (piece - 1) // 3, (piece - 1) % 3
                blk = d0[kind, 0, :, lanes] if gi == 0 else (s4, s16)[gi - 1][kind, half]
                if kind == 0:
                    blk = _rope128(blk, cs, sa, sb, -1.0) * (HEAD_DIM ** -0.5)
                elif kind == 1:
                    blk = _rope128(blk, cs, sa, sb, -1.0)
            dp_ref[:, sp * 128:(sp + 1) * 128] = blk.astype(BF16)
        dh = jnp.zeros((tm, D_MODEL), F32)
        hb = h_ref[...]
        for j in range(N_CHIPS):
            dpj = dp_ref[:, j * 640:(j + 1) * 640]
            dh = dh + lax.dot_general(dpj, w_ref[j], NT, preferred_element_type=F32)
            dw_ref[j] += lax.dot_general(hb, dpj, TN, preferred_element_type=F32)
        xv = x_ref[...]
        rstd = lax.rsqrt(jnp.mean(xv * xv, axis=-1, keepdims=True) + NORM_EPS)
        n1 = xv * rstd
        gv = g_ref[...]
        one_sc = 1.0 + mod_ref[1:2, :]
        _acc(dsh_ref, _colsum(dh))
        _acc(dsc_ref, _colsum(dh * (n1 * gv)))
        _acc(dg_ref, _colsum(dh * one_sc * n1))
        dn = dh * (gv * one_sc)
        gx_ref[...] = dx1_ref[...] + rstd * (dn - n1 * jnp.mean(dn * n1, axis=-1, keepdims=True))

    vec = _full((1, D_MODEL))
    dspec = lambda d: pl.BlockSpec((3, d, tm // d, GROUP_W), lambda i: (0, 0, i, 0))
    return pl.pallas_call(
        body,
        name="inproj_bwd",
        grid=(S // tm,),
        in_specs=[dspec(d) for d in DILATIONS] + [_rows(tm, POOL_W), _rows(tm, D_MODEL), _rows(tm, D_MODEL), _full(w_in_g.shape),
                                                  vec, _full((6, D_MODEL)), _rows(tm, 128), _rows(tm, 128), _rows(tm, 128),
                                                  _rows(tm, D_MODEL)],
        out_specs=[_rows(tm, D_MODEL), vec, vec, vec, _full(w_in_g.shape)],
        out_shape=[jax.ShapeDtypeStruct((S, D_MODEL), F32)] + [jax.ShapeDtypeStruct((1, D_MODEL), F32)] * 3
        + [jax.ShapeDtypeStruct(w_in_g.shape, F32)],
        scratch_shapes=[pltpu.VMEM((3, 2, tm, 128), F32)] * 2 + [pltpu.VMEM((tm, IN_W), BF16)],
        compiler_params=_params(1),
    )(*dqkv, du, x, dx1, w_in_g, g, mod6, tc, tsa, tsb, h1)


def _adamw(w, g, m, v):
    m = ADAM_B1 * m + (1.0 - ADAM_B1) * g
    v = ADAM_B2 * v + (1.0 - ADAM_B2) * (g * g)
    m_hat = m / (1.0 - ADAM_B1 ** ADAM_STEP)
    v_hat = v / (1.0 - ADAM_B2 ** ADAM_STEP)
    delta = -ADAM_LR * (m_hat / (jnp.sqrt(v_hat) + ADAM_EPS) + ADAM_WD * w)
    return delta, m, v


def adamw_rows(w, g, m, v, tr, name):
    R, C = w.shape

    def body(w_ref, g_ref, m_ref, v_ref, d_ref, mo_ref, vo_ref):
        d_ref[...], mo_ref[...], vo_ref[...] = _adamw(w_ref[...], g_ref[...], m_ref[...], v_ref[...])

    spec = pl.BlockSpec((tr, C), lambda i: (i, 0))
    return pl.pallas_call(
        body,
        name=name,
        grid=(R // tr,),
        in_specs=[spec] * 4,
        out_specs=[spec] * 3,
        out_shape=[jax.ShapeDtypeStruct((R, C), F32)] * 3,
        compiler_params=_params(1),
    )(w, g, m, v)


def adamw_ada(c_all_t, dmod_cols, w, m, v, tr=256):
    R, C = w.shape

    def body(ct_ref, dm_ref, w_ref, m_ref, v_ref, g_ref, d_ref, mo_ref, vo_ref):
        ct = ct_ref[...]
        act = ct * jax.nn.sigmoid(ct)
        g = jnp.zeros((tr, C), F32)
        for b in range(N_DEV):
            g = g + act[:, b:b + 1] * dm_ref[b:b + 1, :]
        g_ref[...] = g
        d_ref[...], mo_ref[...], vo_ref[...] = _adamw(w_ref[...], g, m_ref[...], v_ref[...])

    spec = pl.BlockSpec((tr, C), lambda i: (i, 0))
    return pl.pallas_call(
        body,
        name="adamw_ada",
        grid=(R // tr,),
        in_specs=[pl.BlockSpec((tr, N_DEV), lambda i: (i, 0)), _full((N_DEV, C)), spec, spec, spec],
        out_specs=[spec] * 4,
        out_shape=[jax.ShapeDtypeStruct((R, C), F32)] * 4,
        compiler_params=_params(1),
    )(c_all_t, dmod_cols, w, m, v)


def adamw_small(slab_a, slab_b, convw_g, wpool_g, params):
    names = ["b_ada", "g_pre_mix", "g_post_mix", "g_pre_ffn", "g_post_ffn", "b_pool", "pool_scale", "conv_b", "conv_w", "w_pool"]
    flat = []
    for n in names:
        flat += list(params[n])

    def body(a_ref, b_ref, cw_ref, wp_ref, *rest):
        ins, outs = rest[:30], rest[30:]

        def dev_sum(ref):
            t = ref[0]
            for dev in range(1, N_DEV):
                t = t + ref[dev]
            return t

        sa, sb_, scw, swp = dev_sum(a_ref), dev_sum(b_ref), dev_sum(cw_ref), dev_sum(wp_ref)
        grads = [
            jnp.concatenate([sa[k:k + 1, :] for k in range(6)], axis=1),
            sa[6:7, :], sa[7:8, :], sa[8:9, :], sa[9:10, :],
            sa[10:11, 0:256], sa[10:11, 256:512],
            sb_[3:4, :], scw, swp,
        ]
        for i, g in enumerate(grads):
            w_ref, m_ref, v_ref = ins[3 * i:3 * i + 3]
            d, mo, vo = _adamw(w_ref[...], g, m_ref[...], v_ref[...])
            outs[4 * i][...] = g
            outs[4 * i + 1][...] = d
            outs[4 * i + 2][...] = mo
            outs[4 * i + 3][...] = vo
        outs[-1][...] = sa[10:11, 512:640]

    vm = pl.BlockSpec(memory_space=pltpu.VMEM)
    out_shape = []
    for n in names:
        out_shape += [jax.ShapeDtypeStruct(params[n][0].shape, F32)] * 4
    out_shape.append(jax.ShapeDtypeStruct((1, 128), F32))
    outs = pl.pallas_call(
        body,
        name="adamw_small",
        in_specs=[vm] * (4 + len(flat)),
        out_specs=[vm] * len(out_shape),
        out_shape=out_shape,
        compiler_params=_params(),
    )(slab_a, slab_b, convw_g, wpool_g, *flat)
    return {n: outs[4 * i:4 * i + 4] for i, n in enumerate(names)}, outs[-1]


def _place():
    return lax.axis_index("x"), lax.axis_index("y"), lax.axis_index("c")


def _other_chips(x, y):
    return [(1 - x, y), (x, 1 - y), (1 - x, 1 - y)]


def _chip_id(cx, cy):
    return 2 * cx + cy


def gather_weights(shards):
    n = len(shards)
    halved = [s.shape[0] % 32 == 0 for s in shards]

    def body(*refs):
        ins, outs = refs[:n], refs[n:2 * n]
        send_sems, recv_sems, loc_sems = refs[2 * n:]
        x, y, c = _place()
        me = _chip_id(x, y)
        chips = _other_chips(x, y)
        sib = (x, y, 1 - c)

        def part(w, chip, half):
            if not halved[w]:
                return outs[w].at[chip]
            rh = shards[w].shape[0] // 2
            return outs[w].at[chip, pl.ds(half * rh, rh), :]

        def src_part(w):
            if not halved[w]:
                return ins[w]
            rh = shards[w].shape[0] // 2
            return ins[w].at[pl.ds(c * rh, rh), :]

        def rcopy(w, k, src, dst, to):
            return pltpu.make_async_remote_copy(src_ref=src, dst_ref=dst, send_sem=send_sems.at[6 * w + k],
                                                recv_sem=recv_sems.at[6 * w + k], device_id=to, device_id_type=MESH)

        local = [pltpu.make_async_copy(ins[w], outs[w].at[me], loc_sems.at[w]) for w in range(n)]
        for cp in local:
            cp.start()
        first = []
        for w in range(n):
            for k, (cx, cy) in enumerate(chips):
                cp = rcopy(w, k, src_part(w), part(w, me, c), (cx, cy, c))
                cp.start()
                first.append(cp)
        passed = []
        for w in range(n):
            for k, (cx, cy) in enumerate(chips):
                blk = part(w, _chip_id(cx, cy), c)
                rcopy(w, k, blk, blk, (cx, cy, c)).wait_recv()
                if halved[w]:
                    cp = rcopy(w, 3 + k, blk, blk, sib)
                    cp.start()
                    passed.append(cp)
        for w in range(n):
            if halved[w]:
                for k, (cx, cy) in enumerate(chips):
                    blk = part(w, _chip_id(cx, cy), 1 - c)
                    rcopy(w, 3 + k, blk, blk, sib).wait_recv()
        for cp in first + passed:
            cp.wait_send()
        for cp in local:
            cp.wait()

    hbm = pl.BlockSpec(memory_space=pl.ANY)
    return pl.pallas_call(
        body,
        name="gather_weights",
        in_specs=[hbm] * n,
        out_specs=[hbm] * n,
        out_shape=[jax.ShapeDtypeStruct((N_CHIPS,) + s.shape, s.dtype) for s in shards],
        scratch_shapes=[pltpu.SemaphoreType.DMA((6 * n,)), pltpu.SemaphoreType.DMA((6 * n,)), pltpu.SemaphoreType.DMA((n,))],
        compiler_params=pltpu.CompilerParams(has_side_effects=True, vmem_limit_bytes=VMEM_LIMIT),
    )(*shards)


HBM_SPEC = pl.BlockSpec(memory_space=pltpu.HBM)
SEM_SPEC = pl.BlockSpec(memory_space=pltpu.SEMAPHORE)
ANY_SPEC = pl.BlockSpec(memory_space=pl.ANY)
EFFECT = pltpu.SideEffectType.DATAFLOW_SIDE_EFFECTING


def _hbm(t):
    return pltpu.with_memory_space_constraint(t, pltpu.HBM)


def _hbm_shapes(ts):
    return [pltpu.HBM(t.shape, t.dtype) for t in ts]


def _half_rows(ref, lead, half, rh):
    return ref.at[lead, pl.ds(half * rh, rh), :]


def gather_split_start(shards, lands, carry, k):
    n = len(shards)

    def body(*refs):
        ins, land = refs[:n], refs[n:2 * n]
        send_sems, recv_sems = refs[2 * n + 1], refs[2 * n + 2]
        loc_sems = refs[-1]
        x, y, c = _place()
        me = _chip_id(x, y)
        if k == 0:
            local = [pltpu.make_async_copy(ins[w], land[w].at[me], loc_sems.at[w]) for w in range(n)]
            for cp in local:
                cp.start()
            for cp in local:
                cp.wait()
        cx, cy = _other_chips(x, y)[k]
        for w in range(n):
            rh = shards[w].shape[0] // 2
            pltpu.make_async_remote_copy(src_ref=ins[w].at[pl.ds(c * rh, rh), :], dst_ref=_half_rows(land[w], me, c, rh),
                                         send_sem=send_sems.at[w], recv_sem=recv_sems.at[w],
                                         device_id=(cx, cy, c), device_id_type=MESH).start()

    args = [_hbm(s) for s in shards] + [_hbm(l) for l in lands] + [_hbm(carry)]
    outs = pl.pallas_call(
        body,
        name="gather_split_start%d" % k,
        out_shape=[pltpu.SemaphoreType.DMA((n,)), pltpu.SemaphoreType.DMA((n,))] + _hbm_shapes(shards) + _hbm_shapes(lands)
        + _hbm_shapes([carry]),
        in_specs=[HBM_SPEC] * (2 * n + 1),
        out_specs=[SEM_SPEC, SEM_SPEC] + [HBM_SPEC] * (2 * n + 1),
        input_output_aliases={i: 2 + i for i in range(2 * n + 1)},
        scratch_shapes=[pltpu.SemaphoreType.DMA((n,))],
        compiler_params=pltpu.CompilerParams(has_side_effects=EFFECT),
    )(*args)
    return outs[0], outs[1], list(outs[2:2 + n]), list(outs[2 + n:2 + 2 * n]), outs[-1]


def gather_split_mid(sems, shards, lands, after):
    n = len(shards)

    def body(*refs):
        ins, land = refs[:n], refs[n:2 * n]
        sem_in = refs[2 * n:2 * n + 6]
        fsend, frecv = refs[2 * n + 7], refs[2 * n + 8]
        x, y, c = _place()
        me = _chip_id(x, y)
        chips = _other_chips(x, y)
        for w in range(n):
            rh = shards[w].shape[0] // 2
            for k, (cx, cy) in enumerate(chips):
                got = _half_rows(land[w], _chip_id(cx, cy), c, rh)
                cp = pltpu.make_async_remote_copy(src_ref=ins[w].at[pl.ds(c * rh, rh), :], dst_ref=got, send_sem=sem_in[2 * k].at[w],
                                                  recv_sem=sem_in[2 * k + 1].at[w], device_id=(cx, cy, c), device_id_type=MESH)
                cp.wait_send()
                cp.wait_recv()
        for w in range(n):
            rh = shards[w].shape[0] // 2
            for k, (cx, cy) in enumerate(chips):
                got = _half_rows(land[w], _chip_id(cx, cy), c, rh)
                pltpu.make_async_remote_copy(src_ref=got, dst_ref=got, send_sem=fsend.at[3 * w + k], recv_sem=frecv.at[3 * w + k],
                                             device_id=(x, y, 1 - c), device_id_type=MESH).start()

    outs = pl.pallas_call(
        body,
        name="gather_split_mid",
        out_shape=[pltpu.SemaphoreType.DMA((3 * n,)), pltpu.SemaphoreType.DMA((3 * n,))] + _hbm_shapes(lands),
        in_specs=[HBM_SPEC] * (2 * n) + [SEM_SPEC] * 6 + [ANY_SPEC],
        out_specs=[SEM_SPEC, SEM_SPEC] + [HBM_SPEC] * n,
        input_output_aliases={n + i: 2 + i for i in range(n)},
        compiler_params=pltpu.CompilerParams(has_side_effects=EFFECT),
    )(*shards, *lands, *sems, after)
    return outs[0], outs[1], list(outs[2:])


def gather_split_done(fsend, frecv, lands, after):
    n = len(lands)

    def body(*refs):
        land = refs[:n]
        ssem, rsem = refs[n], refs[n + 1]
        x, y, c = _place()
        for w in range(n):
            rh = lands[w].shape[1] // 2
            for k, (cx, cy) in enumerate(_other_chips(x, y)):
                sent = _half_rows(land[w], _chip_id(cx, cy), c, rh)
                got = _half_rows(land[w], _chip_id(cx, cy), 1 - c, rh)
                cp = pltpu.make_async_remote_copy(src_ref=sent, dst_ref=got, send_sem=ssem.at[3 * w + k], recv_sem=rsem.at[3 * w + k],
                                                  device_id=(x, y, 1 - c), device_id_type=MESH)
                cp.wait_send()
                cp.wait_recv()

    outs = pl.pallas_call(
        body,
        name="gather_split_done",
        out_shape=_hbm_shapes(lands),
        in_specs=[HBM_SPEC] * n + [SEM_SPEC, SEM_SPEC, ANY_SPEC],
        out_specs=[HBM_SPEC] * n,
        input_output_aliases={i: i for i in range(n)},
        compiler_params=pltpu.CompilerParams(has_side_effects=EFFECT),
    )(*lands, fsend, frecv, after)
    return list(outs)


def _flips():
    return [(fx, fy, fc) for fx in (0, 1) for fy in (0, 1) for fc in (0, 1)][1:]


def _flip(v, f):
    return v if f == 0 else 1 - v


def ada_mod(c3, w_ada, b_cols, conv_w):
    CB = w_ada.shape[1]

    def body(c_ref, w_ref, b_ref, cw_ref, call_ref, mod_ref, cwall_ref, modall, send_sems, recv_sems):
        x, y, c = _place()
        me_dev = 4 * x + 2 * y + c
        me = _chip_id(x, y)
        call_ref[me_dev] = c_ref[0]
        cwall_ref[me] = cw_ref[...]
        sends = []
        for k, (cx, cy) in enumerate(_other_chips(x, y)):
            cp = pltpu.make_async_remote_copy(src_ref=cw_ref, dst_ref=cwall_ref.at[me], send_sem=send_sems.at[10 + k],
                                              recv_sem=recv_sems.at[10 + k], device_id=(cx, cy, c), device_id_type=MESH)
            cp.start()
            sends.append(cp)
        for k, (fx, fy, fc) in enumerate(_flips()):
            cp = pltpu.make_async_remote_copy(src_ref=c_ref.at[0], dst_ref=call_ref.at[me_dev], send_sem=send_sems.at[k],
                                              recv_sem=recv_sems.at[k],
                                              device_id=(_flip(x, fx), _flip(y, fy), _flip(c, fc)), device_id_type=MESH)
            cp.start()
            sends.append(cp)
        for k, (fx, fy, fc) in enumerate(_flips()):
            peer = 4 * _flip(x, fx) + 2 * _flip(y, fy) + _flip(c, fc)
            pltpu.make_async_remote_copy(src_ref=c_ref.at[0], dst_ref=call_ref.at[peer], send_sem=send_sems.at[k],
                                         recv_sem=recv_sems.at[k], device_id=(x, y, c), device_id_type=MESH).wait_recv()
        row = lax.broadcasted_iota(jnp.int32, (N_DEV, D_MODEL), 0)
        call = jnp.zeros((N_DEV, D_MODEL), F32)
        for dev in range(N_DEV):
            call = jnp.where(row == dev, call_ref[dev], call)
        act = call * jax.nn.sigmoid(call)
        modall[me] = jnp.dot(act, w_ref[...], preferred_element_type=F32, precision=lax.Precision.HIGHEST) + b_ref[...]
        for k, (cx, cy) in enumerate(_other_chips(x, y)):
            cp = pltpu.make_async_remote_copy(src_ref=modall.at[me], dst_ref=modall.at[me], send_sem=send_sems.at[7 + k],
                                              recv_sem=recv_sems.at[7 + k], device_id=(cx, cy, c), device_id_type=MESH)
            cp.start()
            sends.append(cp)
        for k, (cx, cy) in enumerate(_other_chips(x, y)):
            blk = modall.at[_chip_id(cx, cy)]
            pltpu.make_async_remote_copy(src_ref=blk, dst_ref=blk, send_sem=send_sems.at[7 + k], recv_sem=recv_sems.at[7 + k],
                                         device_id=(x, y, c), device_id_type=MESH).wait_recv()
        for k, (cx, cy) in enumerate(_other_chips(x, y)):
            blk = cwall_ref.at[_chip_id(cx, cy)]
            pltpu.make_async_remote_copy(src_ref=blk, dst_ref=blk, send_sem=send_sems.at[10 + k], recv_sem=recv_sems.at[10 + k],
                                         device_id=(x, y, c), device_id_type=MESH).wait_recv()
        for cp in sends:
            cp.wait_send()
        mine = [modall[j, pl.ds(me_dev, 1), :] for j in range(N_CHIPS)]
        for r in range(6):
            pieces = []
            for h in range(2):
                pos = r * D_MODEL + h * 512
                pieces.append(mine[pos // CB][:, pos % CB:pos % CB + 512])
            mod_ref[r:r + 1, :] = jnp.concatenate(pieces, axis=1)

    vm = pl.BlockSpec(memory_space=pltpu.VMEM)
    return pl.pallas_call(
        body,
        name="ada_mod",
        in_specs=[vm] * 4,
        out_specs=[vm] * 3,
        out_shape=[jax.ShapeDtypeStruct((N_DEV, 1, D_MODEL), F32), jax.ShapeDtypeStruct((6, D_MODEL), F32),
                   jax.ShapeDtypeStruct((N_CHIPS,) + conv_w.shape, F32)],
        scratch_shapes=[pltpu.VMEM((N_CHIPS, N_DEV, CB), F32), pltpu.SemaphoreType.DMA((13,)), pltpu.SemaphoreType.DMA((13,))],
        compiler_params=pltpu.CompilerParams(has_side_effects=True, vmem_limit_bytes=VMEM_LIMIT),
    )(c3, w_ada, b_cols, conv_w)


def gather_small(blocks):
    n = len(blocks)

    def body(*refs):
        ins, outs = refs[:n], refs[n:2 * n]
        send_sems, recv_sems = refs[2 * n:]
        x, y, c = _place()
        sib = (x, y, 1 - c)
        chips = _other_chips(x, y)

        def dev(px, py, pc):
            return 4 * px + 2 * py + pc

        def cp(w, k, src, block_dev, to):
            return pltpu.make_async_remote_copy(src_ref=src, dst_ref=outs[w].at[block_dev], send_sem=send_sems.at[7 * w + k],
                                                recv_sem=recv_sems.at[7 * w + k], device_id=to, device_id_type=MESH)

        me = dev(x, y, c)
        started = []
        for w in range(n):
            outs[w][me] = ins[w][...]
            t = cp(w, 0, ins[w], me, sib)
            t.start()
            started.append(t)
            for k, (cx, cy) in enumerate(chips):
                t = cp(w, 1 + k, ins[w], me, (cx, cy, c))
                t.start()
                started.append(t)
        for w in range(n):
            for k, (cx, cy) in enumerate(chips):
                b = dev(cx, cy, c)
                cp(w, 1 + k, outs[w].at[b], b, (x, y, c)).wait_recv()
                t = cp(w, 4 + k, outs[w].at[b], b, sib)
                t.start()
                started.append(t)
        for w in range(n):
            b = dev(x, y, 1 - c)
            cp(w, 0, outs[w].at[b], b, (x, y, c)).wait_recv()
            for k, (cx, cy) in enumerate(chips):
                b = dev(cx, cy, 1 - c)
                cp(w, 4 + k, outs[w].at[b], b, (x, y, c)).wait_recv()
        for t in started:
            t.wait_send()

    vm = pl.BlockSpec(memory_space=pltpu.VMEM)
    return pl.pallas_call(
        body,
        name="gather_small",
        in_specs=[vm] * n,
        out_specs=[vm] * n,
        out_shape=[jax.ShapeDtypeStruct((N_DEV,) + b.shape, b.dtype) for b in blocks],
        scratch_shapes=[pltpu.SemaphoreType.DMA((7 * n,)), pltpu.SemaphoreType.DMA((7 * n,))],
        compiler_params=pltpu.CompilerParams(has_side_effects=True, vmem_limit_bytes=VMEM_LIMIT),
    )(*blocks)


def reduce_scatter_grads(grads, chunk_rows):
    n = len(grads)
    shapes = [g.shape[1:] for g in grads]
    halves = [s[0] // 2 for s in shapes]

    def body(*refs):
        gin = refs[:n]
        gout = refs[n:2 * n]
        sibbuf = refs[2 * n:3 * n]
        rest = refs[3 * n:]
        rbuf = rest[:n]
        pown = rest[n:2 * n]
        stage_a, stage_b, stage_o, stage_f = rest[2 * n:2 * n + 4]
        sib_send, sib_recv, ici_send, ici_recv, fin_send, fin_recv, ld_sems, st_sems = rest[2 * n + 4:]
        x, y, c = _place()
        me = _chip_id(x, y)
        chips = _other_chips(x, y)
        sib = (x, y, 1 - c)

        to_sib = []
        for w in range(n):
            rh = halves[w]
            cp = pltpu.make_async_remote_copy(src_ref=gin[w].at[:, pl.ds((1 - c) * rh, rh), :], dst_ref=sibbuf[w],
                                              send_sem=sib_send.at[w], recv_sem=sib_recv.at[w], device_id=sib,
                                              device_id_type=MESH)
            cp.start()
            to_sib.append(cp)

        sent = []
        for w in range(n):
            rh, cw = halves[w], shapes[w][1]
            ch = chunk_rows[w]
            to_sib[w].wait_recv()
            for k in range(4):
                chip = me if k == 3 else _chip_id(*chips[k])
                for r0 in range(0, rh, ch):
                    la = pltpu.make_async_copy(gin[w].at[chip, pl.ds(c * rh + r0, ch), :], stage_a.at[0:ch, 0:cw], ld_sems.at[0])
                    lb = pltpu.make_async_copy(sibbuf[w].at[chip, pl.ds(r0, ch), :], stage_b.at[0:ch, 0:cw], ld_sems.at[1])
                    la.start()
                    lb.start()
                    la.wait()
                    lb.wait()
                    tot = stage_a[0:ch, 0:cw] + stage_b[0:ch, 0:cw]
                    if k == 3:
                        pown[w][r0:r0 + ch, :] = tot
                    else:
                        stage_o[0:ch, 0:cw] = tot.astype(BF16)
                        cx, cy = chips[k]
                        cp = pltpu.make_async_remote_copy(src_ref=stage_o.at[0:ch, 0:cw], dst_ref=rbuf[w].at[k, r0:r0 + ch, :],
                                                          send_sem=ici_send.at[3 * w + k], recv_sem=ici_recv.at[3 * w + k],
                                                          device_id=(cx, cy, c), device_id_type=MESH)
                        cp.start()
                        cp.wait_send()
            sent.append(w)

        fin = []
        for w in range(n):
            rh, cw = halves[w], shapes[w][1]
            for k in range(3):
                whole = rbuf[w].at[k]
                pltpu.make_async_remote_copy(src_ref=whole, dst_ref=whole, send_sem=ici_send.at[3 * w + k],
                                             recv_sem=ici_recv.at[3 * w + k], device_id=(x, y, c),
                                             device_id_type=MESH).wait_recv()
            pown[w][...] = ((pown[w][...] + rbuf[w][0].astype(F32)) + rbuf[w][1].astype(F32)) + rbuf[w][2].astype(F32)
            mine = gout[w].at[pl.ds(c * rh, rh), :]
            st = pltpu.make_async_copy(pown[w], mine, st_sems.at[w])
            st.start()
            cp = pltpu.make_async_remote_copy(src_ref=pown[w], dst_ref=mine, send_sem=fin_send.at[w], recv_sem=fin_recv.at[w],
                                              device_id=sib, device_id_type=MESH)
            cp.start()
            fin.append((st, cp))
        for w in range(n):
            rh = halves[w]
            theirs = gout[w].at[pl.ds((1 - c) * rh, rh), :]
            pltpu.make_async_remote_copy(src_ref=theirs, dst_ref=theirs, send_sem=fin_send.at[w], recv_sem=fin_recv.at[w],
                                         device_id=(x, y, c), device_id_type=MESH).wait_recv()
        for cp in to_sib:
            cp.wait_send()
        for st, cp in fin:
            st.wait()
            cp.wait_send()

    hbm = pl.BlockSpec(memory_space=pl.ANY)
    max_ch = max(chunk_rows)
    max_c = max(s[1] for s in shapes)
    outs = pl.pallas_call(
        body,
        name="reduce_scatter_grads",
        in_specs=[hbm] * n,
        out_specs=[hbm] * (2 * n),
        out_shape=[jax.ShapeDtypeStruct(s, F32) for s in shapes]
        + [jax.ShapeDtypeStruct((N_CHIPS, h, s[1]), F32) for h, s in zip(halves, shapes)],
        scratch_shapes=[pltpu.VMEM((3, h, s[1]), BF16) for h, s in zip(halves, shapes)]
        + [pltpu.VMEM((h, s[1]), F32) for h, s in zip(halves, shapes)]
        + [pltpu.VMEM((max_ch, max_c), F32), pltpu.VMEM((max_ch, max_c), F32), pltpu.VMEM((max_ch, max_c), BF16),
           pltpu.VMEM((8, 128), F32)]
        + [pltpu.SemaphoreType.DMA((n,)), pltpu.SemaphoreType.DMA((n,)), pltpu.SemaphoreType.DMA((3 * n,)),
           pltpu.SemaphoreType.DMA((3 * n,)), pltpu.SemaphoreType.DMA((n,)), pltpu.SemaphoreType.DMA((n,)),
           pltpu.SemaphoreType.DMA((2,)), pltpu.SemaphoreType.DMA((n,))],
        compiler_params=pltpu.CompilerParams(has_side_effects=True, vmem_limit_bytes=VMEM_LIMIT),
    )(*grads)
    return outs[:n]


def split_start(name, bufs, plan, n_sem, carry):
    nb = len(bufs)

    def body(*refs):
        x, y, c = _place()
        ssem, rsem = refs[nb + 1], refs[nb + 2]
        for i, (src, dst, dev) in enumerate(plan(refs[:nb], x, y, c)):
            pltpu.make_async_remote_copy(src_ref=src, dst_ref=dst, send_sem=ssem.at[i], recv_sem=rsem.at[i], device_id=dev,
                                         device_id_type=MESH).start()

    alls = list(bufs) + [carry]
    outs = pl.pallas_call(
        body,
        name=name,
        out_shape=[pltpu.SemaphoreType.DMA((n_sem,)), pltpu.SemaphoreType.DMA((n_sem,))] + _hbm_shapes(alls),
        in_specs=[HBM_SPEC] * (nb + 1),
        out_specs=[SEM_SPEC, SEM_SPEC] + [HBM_SPEC] * (nb + 1),
        input_output_aliases={i: 2 + i for i in range(nb + 1)},
        compiler_params=pltpu.CompilerParams(has_side_effects=EFFECT),
    )(*[_hbm(t) for t in alls])
    return outs[0], outs[1], list(outs[2:2 + nb]), outs[-1]


def split_wait(name, ssem, rsem, bufs, plan, after):
    nb = len(bufs)

    def body(*refs):
        x, y, c = _place()
        s_ref, r_ref = refs[nb], refs[nb + 1]
        for i, (src, dst, dev) in enumerate(plan(refs[:nb], x, y, c)):
            cp = pltpu.make_async_remote_copy(src_ref=src, dst_ref=dst, send_sem=s_ref.at[i], recv_sem=r_ref.at[i], device_id=dev,
                                              device_id_type=MESH)
            cp.wait_send()
            cp.wait_recv()

    outs = pl.pallas_call(
        body,
        name=name,
        out_shape=_hbm_shapes(bufs),
        in_specs=[HBM_SPEC] * nb + [SEM_SPEC, SEM_SPEC, ANY_SPEC],
        out_specs=[HBM_SPEC] * nb,
        input_output_aliases={i: i for i in range(nb)},
        compiler_params=pltpu.CompilerParams(has_side_effects=EFFECT),
    )(*bufs, ssem, rsem, after)
    return list(outs)


def _gather_ici_plan(n):
    def plan(refs, x, y, c):
        out = []
        for w in range(n):
            rh = refs[w].shape[0] // 2
            for cx, cy in _other_chips(x, y):
                out.append((refs[w].at[pl.ds(c * rh, rh), :], _half_rows(refs[n + w], _chip_id(x, y), c, rh), (cx, cy, c)))
        return out

    return plan


def _gather_d2d_plan(n):
    def plan(refs, x, y, c):
        out = []
        for w in range(n):
            rh = refs[w].shape[1] // 2
            for cx, cy in _other_chips(x, y):
                blk = _half_rows(refs[w], _chip_id(cx, cy), c, rh)
                out.append((blk, blk, (x, y, 1 - c)))
        return out

    return plan


def _dev_id(x, y, c):
    return 4 * x + 2 * y + c


def _small_ici_plan(n):
    def plan(refs, x, y, c):
        out = []
        for w in range(n):
            dst = refs[n + w].at[_dev_id(x, y, c)]
            out.append((refs[w], dst, (x, y, 1 - c)))
            for cx, cy in _other_chips(x, y):
                out.append((refs[w], dst, (cx, cy, c)))
        return out

    return plan


def _small_d2d_plan(n):
    def plan(refs, x, y, c):
        out = []
        for w in range(n):
            for cx, cy in _other_chips(x, y):
                blk = refs[w].at[_dev_id(cx, cy, c)]
                out.append((blk, blk, (x, y, 1 - c)))
        return out

    return plan


def _rs_d2d_plan(n):
    def plan(refs, x, y, c):
        out = []
        for w in range(n):
            rh = refs[w].shape[1] // 2
            out.append((refs[w].at[:, pl.ds((1 - c) * rh, rh), :], refs[n + w], (x, y, 1 - c)))
        return out

    return plan


def _rs_ici_plan(n):
    def plan(refs, x, y, c):
        out = []
        for w in range(n):
            for k, (cx, cy) in enumerate(_other_chips(x, y)):
                out.append((refs[w].at[_chip_id(cx, cy)], refs[n + w].at[k], (cx, cy, c)))
        return out

    return plan


def _rs_share_plan(n):
    def plan(refs, x, y, c):
        out = []
        for w in range(n):
            rh = refs[w].shape[0] // 2
            rows = refs[w].at[pl.ds(c * rh, rh), :]
            out.append((rows, rows, (x, y, 1 - c)))
        return out

    return plan


def rs_add(grad, sibbuf, place, tr, name):
    _, R, C = grad.shape
    nt = (R // 2) // tr

    def body(p_ref, g_ref, s_ref, o_ref):
        o_ref[...] = (g_ref[...] + s_ref[...]).astype(BF16)

    return pl.pallas_call(
        body,
        name=name,
        grid_spec=pltpu.PrefetchScalarGridSpec(
            num_scalar_prefetch=1,
            grid=(N_CHIPS, nt),
            in_specs=[pl.BlockSpec((None, tr, C), lambda j, i, p: (j, p[0] * nt + i, 0)),
                      pl.BlockSpec((None, tr, C), lambda j, i, p: (j, i, 0))],
            out_specs=pl.BlockSpec((None, tr, C), lambda j, i, p: (j, i, 0)),
        ),
        out_shape=jax.ShapeDtypeStruct((N_CHIPS, R // 2, C), BF16),
        compiler_params=_params(2),
    )(place, grad, sibbuf)


def rs_final(grad, sibbuf, rbuf, place, tr, name):
    _, R, C = grad.shape
    nt = (R // 2) // tr

    def body(p_ref, g_ref, s_ref, r_ref, o_ref):
        o_ref[...] = (((g_ref[...] + s_ref[...]) + r_ref[0].astype(F32)) + r_ref[1].astype(F32)) + r_ref[2].astype(F32)

    return pl.pallas_call(
        body,
        name=name,
        grid_spec=pltpu.PrefetchScalarGridSpec(
            num_scalar_prefetch=1,
            grid=(nt,),
            in_specs=[pl.BlockSpec((None, tr, C), lambda i, p: (p[1], p[0] * nt + i, 0)),
                      pl.BlockSpec((None, tr, C), lambda i, p: (p[1], i, 0)),
                      pl.BlockSpec((3, tr, C), lambda i, p: (0, i, 0))],
            out_specs=pl.BlockSpec((tr, C), lambda i, p: (p[0] * nt + i, 0)),
        ),
        out_shape=jax.ShapeDtypeStruct((R, C), F32),
        compiler_params=_params(1),
    )(place, grad, sibbuf, rbuf)


class GradReduce:
    def __init__(self, tag, grads, rows, place):
        self.tag, self.grads, self.rows, self.place = tag, grads, rows, place
        self.n = len(grads)

    def d2d_start(self, carry):
        sib = [lax.empty((N_CHIPS, g.shape[1] // 2, g.shape[2]), F32) for g in self.grads]
        self.s1, self.r1, bufs, carry = split_start(f"rs_{self.tag}_d2d_start", self.grads + sib, _rs_d2d_plan(self.n), self.n, carry)
        self.bufs1 = bufs
        return carry

    def add_and_ici_start(self, after, carry):
        bufs = split_wait(f"rs_{self.tag}_d2d_wait", self.s1, self.r1, self.bufs1, _rs_d2d_plan(self.n), after)
        self.grads, self.sib = bufs[:self.n], bufs[self.n:]
        pb = [rs_add(g, s, self.place, tr, f"rs_{self.tag}_add{w}")
              for w, (g, s, tr) in enumerate(zip(self.grads, self.sib, self.rows))]
        rb = [lax.empty((3,) + p.shape[1:], BF16) for p in pb]
        self.s2, self.r2, self.bufs2, carry = split_start(f"rs_{self.tag}_ici_start", pb + rb, _rs_ici_plan(self.n), 3 * self.n, carry)
        return carry

    def final_and_share_start(self, after, carry):
        bufs = split_wait(f"rs_{self.tag}_ici_wait", self.s2, self.r2, self.bufs2, _rs_ici_plan(self.n), after)
        rb = bufs[self.n:]
        full = [rs_final(g, s, r, self.place, tr, f"rs_{self.tag}_final{w}")
                for w, (g, s, r, tr) in enumerate(zip(self.grads, self.sib, rb, self.rows))]
        self.s3, self.r3, self.bufs3, carry = split_start(f"rs_{self.tag}_share_start", full, _rs_share_plan(self.n), self.n, carry)
        return carry

    def finish(self, after):
        return split_wait(f"rs_{self.tag}_share_wait", self.s3, self.r3, self.bufs3, _rs_share_plan(self.n), after)


def _rope_tables(positions):
    inv_freq = ROPE_THETA ** (-jnp.arange(0, ROT_DIM, 2, dtype=F32) / ROT_DIM)
    ang = positions.astype(F32)[:, None] * inv_freq
    cos, sin = jnp.cos(ang), jnp.sin(ang)
    S = positions.shape[0]
    one, zero = jnp.ones((S, 48), F32), jnp.zeros((S, 48), F32)
    z8 = jnp.zeros((S, 8), F32)
    tc = jnp.concatenate([cos, cos, one], axis=1)
    tsa = jnp.concatenate([z8, sin, zero], axis=1)
    tsb = jnp.concatenate([-sin, z8, zero], axis=1)
    return tuple(jnp.tile(t, (1, 2)) for t in (tc, tsa, tsb))


def _block_diag(w_pool):
    wbd = jnp.zeros((POOL_W, POOL_W), F32)
    for gi in range(4):
        wbd = wbd.at[gi * 64:(gi + 1) * 64, gi * 64:(gi + 1) * 64].set(w_pool[gi])
    return wbd


def kernel(x, c, positions, w_ada, b_ada, g_pre_mix, g_post_mix, g_pre_ffn, g_post_ffn, w_in, w_pool, b_pool, pool_scale, w_out, w_up, conv_w, conv_b, w_down, loss_target, m_w_ada, m_b_ada, m_g_pre_mix, m_g_post_mix, m_g_pre_ffn, m_g_post_ffn, m_w_in, m_w_pool, m_b_pool, m_pool_scale, m_w_out, m_w_up, m_conv_w, m_conv_b, m_w_down, v_w_ada, v_b_ada, v_g_pre_mix, v_g_post_mix, v_g_pre_ffn, v_g_post_ffn, v_w_in, v_w_pool, v_b_pool, v_pool_scale, v_w_out, v_w_up, v_conv_w, v_conv_b, v_w_down):
    xi, yi, ci = lax.axis_index("x"), lax.axis_index("y"), lax.axis_index("c")
    chip = 2 * xi + yi
    place = jnp.stack([ci, chip]).astype(jnp.int32)
    x2, tgt = x[0], loss_target[0]
    S = x2.shape[0]

    def landing(s_):
        return lax.dynamic_update_slice(lax.empty((N_CHIPS,) + s_.shape, s_.dtype), s_[None], (chip, 0, 0))

    cb_ada = w_ada.shape[2]
    b_cols = lax.dynamic_slice(b_ada, (0, chip * cb_ada), (1, cb_ada))
    mix_sh = [w_in[0].astype(BF16), w_out[0].astype(BF16)]
    ffn_sh = [w_up[0].astype(BF16), w_down[0].astype(BF16)]
    ga_s, ga_r, ga_bufs, b_cols = split_start("gather_mix_ici_start", mix_sh + [landing(t) for t in mix_sh], _gather_ici_plan(2), 6, b_cols)
    gb_s, gb_r, gb_bufs, b_cols = split_start("gather_ffn_ici_start", ffn_sh + [landing(t) for t in ffn_sh], _gather_ici_plan(2), 6, b_cols)

    c_all, mod6, conv_w_g = ada_mod(c.reshape(1, 1, D_MODEL), w_ada[0], b_cols, conv_w[0])
    conv_w_f = jnp.transpose(conv_w_g, (1, 0, 2)).reshape(3, D_FF)
    tc, tsa, tsb = _rope_tables(positions[0])
    wbd = _block_diag(w_pool[0]).astype(BF16)
    b_pool2, scale2 = b_pool.reshape(1, POOL_W), pool_scale
    ga_bufs = split_wait("gather_mix_ici_wait", ga_s, ga_r, ga_bufs, _gather_ici_plan(2), mod6)
    gc_s, gc_r, mix_land, mod6 = split_start("gather_mix_d2d_start", ga_bufs[2:], _gather_d2d_plan(2), 6, mod6)
    w_in_g, w_out_g = split_wait("gather_mix_d2d_wait", gc_s, gc_r, mix_land, _gather_d2d_plan(2), mod6)

    h1, u, *qkv = inproj_fwd(x2, g_pre_mix, mod6, w_in_g, tc, tsa, tsb)
    mixed, pool = pool_fwd(u, wbd, b_pool2, scale2)
    o_l = [attn_fwd(t, d) for t, d in zip(qkv, DILATIONS)]
    attn_done = sum(l[0, :8, :128] for _, l in o_l)
    gb_bufs = split_wait("gather_ffn_ici_wait", gb_s, gb_r, gb_bufs, _gather_ici_plan(2), attn_done)
    gd_s, gd_r, ffn_land, pool = split_start("gather_ffn_d2d_start", gb_bufs[2:], _gather_d2d_plan(2), 6, pool)
    cat, lse, lse4, lse16, y1, x1, h2 = outproj_fwd([o for o, _ in o_l] + [l for _, l in o_l], pool, x2, w_out_g, g_post_mix,
                                                    g_pre_ffn, mod6)
    lses = [lse[None], lse4, lse16]
    w_up_g, w_down_g = split_wait("gather_ffn_d2d_wait", gd_s, gd_r, ffn_land, _gather_d2d_plan(2), h2)
    w_down_f = w_down_g.reshape(D_FF, D_MODEL)
    gate, val, a, dy2, dout, loss_v, d_gt_f, d_g_post_ffn = ffn_fwd(h2, w_up_g, conv_w_f, conv_b, w_down_f, x1, tgt, g_post_ffn, mod6)

    dgc, dval, d_conv_w, d_conv_b, dw_down, dw_up = down_bwd(dy2, w_down_f, gate, val, conv_w_f, conv_b, a, h2)
    dx1, dy1, d_sh_f, d_sc_f, d_g_pre_ffn, d_gt_m, d_g_post_mix, dw_up = up_bwd(
        dgc, dval, conv_w_f, w_up_g, x1, dout, y1, g_pre_ffn, g_post_mix, mod6, h2, dw_up)
    rs_ffn = GradReduce("ffn", [dw_up, dw_down.reshape(N_CHIPS, D_FF // N_CHIPS, D_MODEL)], [256, 176], place)
    dy1 = rs_ffn.d2d_start(dy1)
    dpool, da1, da4, da16, dl1, dl4, dl16, dw_out = outproj_bwd(dy1, w_out_g, cat)
    dpool = rs_ffn.add_and_ici_start(dw_out, dpool)
    du, d_wbd, d_b_pool, d_scale = pool_bwd(dpool, mixed, wbd, b_pool2, scale2)
    dqkv = [attn_bwd(t, da, ls, dl, d) for t, da, ls, dl, d in zip(qkv, (da1[None], da4, da16), lses, (dl1[None], dl4, dl16), DILATIONS)]
    grad_x, d_sh_m, d_sc_m, d_g_pre_mix, dw_in = inproj_bwd(dqkv, du, x2, dx1, w_in_g, g_pre_mix, mod6, tc, tsa, tsb, h1)

    z1 = jnp.zeros((1, D_MODEL), F32)
    slab_a = jnp.concatenate(
        [d_sh_m, d_sc_m, d_gt_m, d_sh_f, d_sc_f, d_gt_f, d_g_pre_mix, d_g_post_mix, d_g_pre_ffn, d_g_post_ffn,
         jnp.concatenate([d_b_pool, d_scale, loss_v, jnp.zeros((1, 384), F32)], axis=1)] + [z1] * 5, axis=0)
    slab_b = jnp.concatenate([d_conv_w, d_conv_b, jnp.zeros((4, D_FF), F32)], axis=0)
    d_wpool = jnp.concatenate([d_wbd[gi * 64:(gi + 1) * 64, gi * 64:(gi + 1) * 64] for gi in range(4)], axis=0)
    dev = _dev_id(xi, yi, ci)
    small_src = [slab_a, slab_b, d_wpool]
    small_land = [lax.dynamic_update_slice(lax.empty((N_DEV,) + t.shape, F32), t[None], (dev, 0, 0)) for t in small_src]
    tok = jnp.zeros((8, 128), F32)
    gs_s, gs_r, gs_bufs, tok = split_start("small_ici_start", small_src + small_land, _small_ici_plan(3), 12, tok)
    rs_mix = GradReduce("mix", [dw_in, dw_out], [256, 256], place)
    tok = rs_mix.d2d_start(tok)
    tok = rs_ffn.final_and_share_start(tok, tok)
    gs_bufs = split_wait("small_ici_wait", gs_s, gs_r, gs_bufs, _small_ici_plan(3), tok)
    gt_s, gt_r, small_land, tok = split_start("small_d2d_start", gs_bufs[3:], _small_d2d_plan(3), 9, tok)
    tok = rs_mix.add_and_ici_start(tok, tok)
    slab_a_g, slab_b_g, wpool_g = split_wait("small_d2d_wait", gt_s, gt_r, small_land, _small_d2d_plan(3), tok)
    cw_cols = conv_w.shape[2]
    convw_g = lax.dynamic_slice(slab_b_g, (0, 0, chip * cw_cols), (N_DEV, 3, cw_cols))
    dmod_cols = lax.dynamic_slice(slab_a_g[:, :6, :].reshape(N_DEV, 6 * D_MODEL), (0, chip * cb_ada), (N_DEV, cb_ada))

    res = {}

    def big_adamw(name, w, g, m, v, tr):
        d_, m_, v_ = adamw_rows(w[0], g, m[0], v[0], tr, "adamw_" + name)
        res[name] = (g[None], d_[None], m_[None], v_[None])
        return v_

    g_ada, d_ada, m_ada, v_ada = adamw_ada(c_all.reshape(N_DEV, D_MODEL).T, dmod_cols, w_ada[0], m_w_ada[0], v_w_ada[0])
    res["w_ada"] = (g_ada[None], d_ada[None], m_ada[None], v_ada[None])
    g_w_up, g_w_down = rs_ffn.finish(v_ada)
    big_adamw("w_up", w_up, g_w_up, m_w_up, v_w_up, 256)
    last = big_adamw("w_down", w_down, g_w_down, m_w_down, v_w_down, 352)
    rs_mix.final_and_share_start(last, jnp.zeros((8, 128), F32))
    g_w_in, g_w_out = rs_mix.finish(last)
    big_adamw("w_in", w_in, g_w_in, m_w_in, v_w_in, 256)
    big_adamw("w_out", w_out, g_w_out, m_w_out, v_w_out, 256)
    flat = lambda t: t.reshape(1, POOL_W)
    wp = lambda t: t.reshape(POOL_W, 64)
    small, loss_sum = adamw_small(slab_a_g, slab_b_g, convw_g, wpool_g, {
        "b_ada": (b_ada, m_b_ada, v_b_ada), "g_pre_mix": (g_pre_mix, m_g_pre_mix, v_g_pre_mix),
        "g_post_mix": (g_post_mix, m_g_post_mix, v_g_post_mix), "g_pre_ffn": (g_pre_ffn, m_g_pre_ffn, v_g_pre_ffn),
        "g_post_ffn": (g_post_ffn, m_g_post_ffn, v_g_post_ffn), "b_pool": (flat(b_pool), flat(m_b_pool), flat(v_b_pool)),
        "pool_scale": (pool_scale, m_pool_scale, v_pool_scale), "conv_b": (conv_b, m_conv_b, v_conv_b),
        "conv_w": (conv_w[0], m_conv_w[0], v_conv_w[0]), "w_pool": (wp(w_pool), wp(m_w_pool), wp(v_w_pool))})
    for name in ("b_ada", "g_pre_mix", "g_post_mix", "g_pre_ffn", "g_post_ffn", "pool_scale", "conv_b"):
        res[name] = tuple(small[name])
    res["b_pool"] = tuple(t.reshape(1, 4, 64) for t in small["b_pool"])
    res["conv_w"] = tuple(t[None] for t in small["conv_w"])
    res["w_pool"] = tuple(t.reshape(1, 4, 64, 64) for t in small["w_pool"])

    loss = loss_sum[0, 0]
    order = ["w_ada", "b_ada", "g_pre_mix", "g_post_mix", "g_pre_ffn", "g_post_ffn", "w_in", "w_pool", "b_pool", "pool_scale",
             "w_out", "w_up", "conv_w", "conv_b", "w_down"]
    outs = [loss, grad_x[None]]
    for k in range(4):
        outs += [res[n][k] for n in order]
    return tuple(outs)
```

```python
import functools
import math

import jax
import jax.numpy as jnp
from jax import lax
from jax.experimental import pallas as pl
from jax.experimental.pallas import tpu as pltpu

F32 = jnp.float32
BF16 = jnp.bfloat16
MESH = pl.DeviceIdType.MESH

D_MODEL = 1024
HEAD_DIM = 64
POOL_W = 256
GROUP_W = 256
DILATIONS = (1, 4, 16)
ATT_BLOCK = 128
IN_W = 2560
D_FF = 2816
HALF_FF = 1408
ROT_DIM = 16
ROPE_THETA = 500000.0
NORM_EPS = 1e-6
N_CHIPS = 4
N_DEV = 8
NEG = -1e30

ADAM_LR = 0.001
ADAM_B1 = 0.9
ADAM_B2 = 0.999
ADAM_EPS = 1e-08
ADAM_WD = 0.01
ADAM_STEP = 10

VMEM_LIMIT = 56 * 1024 * 1024

NT = (((1,), (1,)), ((), ()))
TN = (((0,), (0,)), ((), ()))


def _params(n_grid=0, **kw):
    sem = ("arbitrary",) * n_grid if n_grid else None
    return pltpu.CompilerParams(dimension_semantics=sem, vmem_limit_bytes=VMEM_LIMIT, **kw)


def _full(shape):
    nd = len(shape)
    return pl.BlockSpec(tuple(shape), lambda *_: (0,) * nd, pipeline_mode=pl.Buffered(1))


def _rows(tm, ncol):
    return pl.BlockSpec((tm, ncol), lambda i: (i, 0))


def _acc(ref, val):
    @pl.when(pl.program_id(0) == 0)
    def _():
        ref[...] = jnp.zeros_like(ref)

    ref[...] += val


def _colsum(v):
    return jnp.sum(v, axis=0, keepdims=True)


def _rope128(t, cs, sa, sb, sign):
    return t * cs + sign * (pltpu.roll(t, 8, 1) * sa + pltpu.roll(t, 120, 1) * sb)


FF_CHUNKS = tuple((ch, off, w) for ch in range(2) for off, w in ((0, 512), (512, 512), (1024, 384)))
GELU_C0 = math.sqrt(2.0 / math.pi)
GELU_C1 = GELU_C0 * 0.044715


def _gelu(z):
    z2 = z * z
    t = jnp.tanh(z * (GELU_C0 + GELU_C1 * z2))
    u = 0.5 * t + 0.5
    return z * u, u, t, z2


def _gelu_grad(z, u, t, z2):
    return u + (z * (GELU_C0 + (3.0 * GELU_C1) * z2)) * (0.5 - 0.5 * (t * t))


def _conv_taps(gate, halo, first):
    row = lax.broadcasted_iota(jnp.int32, gate.shape, 0)
    halo = jnp.where(first, 0.0, halo)
    nh = halo.shape[0]
    p1 = halo[nh - 1:nh, :]
    p2 = halo[nh - 2:nh - 1, :]
    g1 = jnp.where(row == 0, p1, pltpu.roll(gate, 1, 0))
    g2 = jnp.where(row == 0, p2, jnp.where(row == 1, p1, pltpu.roll(gate, 2, 0)))
    return g1, g2


def inproj_fwd(x, g, mod6, w_in_g, tc, tsa, tsb, tm=512):
    S = x.shape[0]

    def body(x_ref, g_ref, mod_ref, w_ref, tc_ref, tsa_ref, tsb_ref, h_ref, u_ref, q1_ref, q4_ref, q16_ref, scr):
        qkv_refs = (q1_ref, q4_ref, q16_ref)
        xv = x_ref[...]
        rstd = lax.rsqrt(jnp.mean(xv * xv, axis=-1, keepdims=True) + NORM_EPS)
        h = ((xv * rstd) * g_ref[...]) * (1.0 + mod_ref[1:2, :]) + mod_ref[0:1, :]
        hb = h.astype(BF16)
        h_ref[...] = hb
        cs, sa, sb = tc_ref[...], tsa_ref[...], tsb_ref[...]
        for j in range(N_CHIPS):
            res = jnp.dot(hb, w_ref[j], preferred_element_type=F32)
            for t in range(5):
                sp = 5 * j + t
                piece, half = sp // 2, sp % 2
                blk = res[:, t * 128:(t + 1) * 128]
                lanes = slice(half * 128, (half + 1) * 128)
                if piece == 0:
                    u_ref[:, lanes] = blk
                else:
                    kind, gi = (piece - 1) // 3, (piece - 1) % 3
                    if kind == 0:
                        blk = _rope128(blk, cs, sa, sb, 1.0) * (HEAD_DIM ** -0.5)
                    elif kind == 1:
                        blk = _rope128(blk, cs, sa, sb, 1.0)
                    d = DILATIONS[gi]
                    if d == 1:
                        q1_ref[kind, 0, :, lanes] = blk.astype(BF16)
                    else:
                        scr[...] = blk
                        for r in range(d):
                            qkv_refs[gi][kind, r, :, lanes] = scr[pl.ds(r, tm // d, stride=d), :].astype(BF16)

    cls = lambda d: pl.BlockSpec((3, d, tm // d, GROUP_W), lambda i: (0, 0, i, 0))
    return pl.pallas_call(
        body,
        name="inproj_fwd",
        grid=(S // tm,),
        in_specs=[_rows(tm, D_MODEL), _full((1, D_MODEL)), _full((6, D_MODEL)), _full(w_in_g.shape),
                  _rows(tm, 128), _rows(tm, 128), _rows(tm, 128)],
        out_specs=[_rows(tm, D_MODEL), _rows(tm, POOL_W)] + [cls(d) for d in DILATIONS],
        out_shape=[jax.ShapeDtypeStruct((S, D_MODEL), BF16), jax.ShapeDtypeStruct((S, POOL_W), F32)]
        + [jax.ShapeDtypeStruct((3, d, S // d, GROUP_W), BF16) for d in DILATIONS],
        scratch_shapes=[pltpu.VMEM((tm, 128), F32)],
        compiler_params=_params(1),
    )(x, g, mod6, w_in_g, tc, tsa, tsb)


def _attn_masks():
    row = lax.broadcasted_iota(jnp.int32, (2 * ATT_BLOCK, 2 * ATT_BLOCK), 0) % ATT_BLOCK
    col = lax.broadcasted_iota(jnp.int32, (2 * ATT_BLOCK, 2 * ATT_BLOCK), 1)
    band = (col >= row) & (col <= row + ATT_BLOCK)
    lane = lax.broadcasted_iota(jnp.int32, (ATT_BLOCK, 128), 1)
    return band, col, lane < HEAD_DIM


def _stack_heads(t, lo):
    z = jnp.zeros_like(t)
    return jnp.concatenate([jnp.where(lo, t, z), jnp.where(lo, z, t)], axis=0)


def _unstack_heads(t2, lo):
    return jnp.where(lo, t2[:ATT_BLOCK], t2[ATT_BLOCK:])


def attn_fwd(qkv, d):
    L = qkv.shape[2]
    nb = L // ATT_BLOCK

    def body(q_ref, k_ref, v_ref, o_ref, l_ref, kpad, vpad):
        kpad[0:ATT_BLOCK, :] = jnp.zeros((ATT_BLOCK, GROUP_W), BF16)
        vpad[0:ATT_BLOCK, :] = jnp.zeros((ATT_BLOCK, GROUP_W), BF16)
        kpad[ATT_BLOCK:, :] = k_ref[...]
        vpad[ATT_BLOCK:, :] = v_ref[...]
        band, col, lo = _attn_masks()

        def step(n, carry):
            r0 = pl.multiple_of(n * ATT_BLOCK, ATT_BLOCK)
            valid = band & ((col >= ATT_BLOCK) | (n > 0))
            qb = q_ref[pl.ds(r0, ATT_BLOCK), :]
            kb = kpad[pl.ds(r0, 2 * ATT_BLOCK), :]
            vb = vpad[pl.ds(r0, 2 * ATT_BLOCK), :]
            for pair in range(2):
                lanes = slice(pair * 128, (pair + 1) * 128)
                qp, kp, vp = qb[:, lanes], kb[:, lanes], vb[:, lanes]
                s = lax.dot_general(_stack_heads(qp, lo), kp, NT, preferred_element_type=F32)
                s = jnp.where(valid, s, NEG)
                m = jnp.max(s, axis=1, keepdims=True)
                p = jnp.exp(s - m)
                den = jnp.sum(p, axis=1, keepdims=True)
                pv = jnp.dot(p.astype(BF16), vp, preferred_element_type=F32)
                o_ref[pl.ds(r0, ATT_BLOCK), lanes] = _unstack_heads(pv / den, lo)
                l_ref[pl.ds(r0, ATT_BLOCK), lanes] = _unstack_heads(jnp.broadcast_to(m + jnp.log(den), pv.shape), lo)
            return carry

        lax.fori_loop(0, nb, step, 0, unroll=min(4, nb))

    spec = lambda kind: pl.BlockSpec((None, None, L, GROUP_W), lambda r: (kind, r, 0, 0))
    return pl.pallas_call(
        body,
        name=f"attn_fwd_d{d}",
        grid=(d,),
        in_specs=[spec(0), spec(1), spec(2)],
        out_specs=[pl.BlockSpec((None, L, GROUP_W), lambda r: (r, 0, 0))] * 2,
        out_shape=[jax.ShapeDtypeStruct((d, L, GROUP_W), F32)] * 2,
        scratch_shapes=[pltpu.VMEM((L + ATT_BLOCK, GROUP_W), BF16)] * 2,
        compiler_params=_params(1),
    )(qkv, qkv, qkv)


def _pool_lane_windows(shape):
    lane = lax.broadcasted_iota(jnp.int32, shape, 1)
    return lane, jnp.where(lane < 64, 2, jnp.where(lane < 128, 4, jnp.where(lane < 192, 8, 16)))


def pool_fwd(u, wbd, b, scale):
    S = u.shape[0]

    def body(u_ref, w_ref, b_ref, s_ref, mixed_ref, out_ref):
        uv = u_ref[...]
        row = lax.broadcasted_iota(jnp.int32, uv.shape, 0)
        lane, win = _pool_lane_windows(uv.shape)

        def shift(a, k):
            return jnp.where(row >= k, pltpu.roll(a, k, 0), 0.0)

        s2 = uv + shift(uv, 1)
        s4 = s2 + shift(s2, 2)
        s8 = s4 + shift(s4, 4)
        s16 = s8 + shift(s8, 8)
        tsum = jnp.where(lane < 64, s2, jnp.where(lane < 128, s4, jnp.where(lane < 192, s8, s16)))
        cnt = jnp.minimum(row + 1, win).astype(F32)
        mb = (tsum / cnt - uv).astype(BF16)
        mixed_ref[...] = mb
        y = jnp.dot(mb, w_ref[...], preferred_element_type=F32) + b_ref[...]
        out_ref[...] = (y * s_ref[...]).astype(BF16)

    vm = pl.BlockSpec(memory_space=pltpu.VMEM)
    return pl.pallas_call(
        body,
        name="pool_fwd",
        in_specs=[vm] * 4,
        out_specs=[vm] * 2,
        out_shape=[jax.ShapeDtypeStruct((S, POOL_W), BF16)] * 2,
        compiler_params=_params(),
    )(u, wbd, b, scale)


def outproj_fwd(o_l, pool, x, w_out_g, g_post, g_pre, mod6, tm=512):
    S = x.shape[0]

    def body(o0, o1, o2, l0, l1, l2, pool_ref, x_ref, w_ref, gpost_ref, gpre_ref, mod_ref,
             cat_ref, lse_ref, lse4_ref, lse16_ref, y1_ref, x1_ref, h2_ref, so4, sl4, so16, sl16):
        for d, src, dst in ((4, o1, so4), (4, l1, sl4), (16, o2, so16), (16, l2, sl16)):
            for r in range(d):
                for h in range(2):
                    dst[h, pl.ds(r, tm // d, stride=d), :] = src[r, :, h * 128:(h + 1) * 128]
        nat = lambda ref: jnp.concatenate([ref[0], ref[1]], axis=1)
        a, b, c = l0[0], nat(sl4), nat(sl16)
        m = jnp.maximum(jnp.maximum(a, b), c)
        e0, e1, e2 = jnp.exp(a - m), jnp.exp(b - m), jnp.exp(c - m)
        z = e0 + e1 + e2
        lse = m + jnp.log(z)
        lse_ref[...] = lse
        for h in range(2):
            sl4[h] = lse[:, h * 128:(h + 1) * 128]
        for d, dst in ((4, lse4_ref), (16, lse16_ref)):
            for r in range(d):
                for h in range(2):
                    dst[r, :, h * 128:(h + 1) * 128] = sl4[h, pl.ds(r, tm // d, stride=d), :]
        attn = (e0 * o0[0] + e1 * nat(so4) + e2 * nat(so16)) / z
        cat = jnp.concatenate([pool_ref[...], attn.astype(BF16)], axis=1)
        cat_ref[...] = cat
        y1 = jnp.concatenate([jnp.dot(cat, w_ref[j], preferred_element_type=F32) for j in range(N_CHIPS)], axis=1)
        y1_ref[...] = y1
        rstd = lax.rsqrt(jnp.mean(y1 * y1, axis=-1, keepdims=True) + NORM_EPS)
        x1 = x_ref[...] + mod_ref[2:3, :] * ((y1 * rstd) * gpost_ref[...])
        x1_ref[...] = x1
        rstd2 = lax.rsqrt(jnp.mean(x1 * x1, axis=-1, keepdims=True) + NORM_EPS)
        h2 = ((x1 * rstd2) * gpre_ref[...]) * (1.0 + mod_ref[4:5, :]) + mod_ref[3:4, :]
        h2_ref[...] = h2.astype(BF16)

    t256 = _rows(tm, GROUP_W)
    cls = lambda d: pl.BlockSpec((d, tm // d, GROUP_W), lambda i: (0, i, 0))
    cls_shape = lambda d: jax.ShapeDtypeStruct((d, S // d, GROUP_W), F32)
    return pl.pallas_call(
        body,
        name="outproj_fwd",
        grid=(S // tm,),
        in_specs=[cls(d) for d in DILATIONS] * 2 + [t256, _rows(tm, D_MODEL), _full(w_out_g.shape), _full((1, D_MODEL)),
                                                    _full((1, D_MODEL)), _full((6, D_MODEL))],
        out_specs=[_rows(tm, 512), t256, cls(4), cls(16), _rows(tm, D_MODEL), _rows(tm, D_MODEL), _rows(tm, D_MODEL)],
        out_shape=[jax.ShapeDtypeStruct((S, 512), BF16), jax.ShapeDtypeStruct((S, GROUP_W), F32), cls_shape(4), cls_shape(16),
                   jax.ShapeDtypeStruct((S, D_MODEL), F32), jax.ShapeDtypeStruct((S, D_MODEL), F32),
                   jax.ShapeDtypeStruct((S, D_MODEL), BF16)],
        scratch_shapes=[pltpu.VMEM((2, tm, 128), F32)] * 4,
        compiler_params=_params(1),
    )(*o_l, pool, x, w_out_g, g_post, g_pre, mod6)


def up_fwd(h2, w_up_g, tm=512):
    S = h2.shape[0]

    def body(h_ref, w_ref, gate_ref, val_ref):
        hb = h_ref[...]
        for j in range(N_CHIPS):
            res = jnp.dot(hb, w_ref[j], preferred_element_type=F32).astype(BF16)
            dst = gate_ref if j < 2 else val_ref
            dst[:, (j % 2) * HALF_FF:(j % 2 + 1) * HALF_FF] = res

    return pl.pallas_call(
        body,
        name="up_fwd",
        grid=(S // tm,),
        in_specs=[_rows(tm, D_MODEL), _full(w_up_g.shape)],
        out_specs=[_rows(tm, D_FF)] * 2,
        out_shape=[jax.ShapeDtypeStruct((S, D_FF), BF16)] * 2,
        compiler_params=_params(1),
    )(h2, w_up_g)


def _halo_prev(tm, ncol):
    return pl.BlockSpec((16, ncol), lambda i: (jnp.maximum(i * (tm // 16) - 1, 0), 0))


def down_fwd(gate, val, conv_w, conv_b, w_down, x1, target, g_post, mod6, tm=256):
    S = x1.shape[0]

    def body(gate_ref, halo_ref, val_ref, cw_ref, cb_ref, w_ref, x1_ref, tgt_ref, g_ref, mod_ref,
             a_ref, dy2_ref, dout_ref, loss_ref, dgt_ref, dg_ref):
        first = pl.program_id(0) == 0
        y2 = jnp.zeros((tm, D_MODEL), F32)
        for ch in range(2):
            cols = slice(ch * HALF_FF, (ch + 1) * HALF_FF)
            gt = gate_ref[:, cols].astype(F32)
            g1, g2 = _conv_taps(gt, halo_ref[:, cols].astype(F32), first)
            gc = g2 * cw_ref[0:1, cols] + g1 * cw_ref[1:2, cols] + gt * cw_ref[2:3, cols] + cb_ref[:, cols]
            ge = _gelu(gc)[0]
            ab = (ge * val_ref[:, cols].astype(F32)).astype(BF16)
            a_ref[:, cols] = ab
            y2 = y2 + jnp.dot(ab, w_ref[cols, :], preferred_element_type=F32)
        rstd = lax.rsqrt(jnp.mean(y2 * y2, axis=-1, keepdims=True) + NORM_EPS)
        y2n = y2 * rstd
        gv = g_ref[...]
        gtf = mod_ref[5:6, :]
        r2 = y2n * gv
        diff = (x1_ref[...] + gtf * r2) - tgt_ref[...]
        _acc(loss_ref, jnp.zeros((1, 128), F32) + 0.5 * jnp.sum(diff * diff) * (1.0 / D_MODEL))
        dout = diff * (1.0 / D_MODEL)
        dout_ref[...] = dout
        _acc(dgt_ref, _colsum(dout * r2))
        dr2 = dout * gtf
        _acc(dg_ref, _colsum(dr2 * y2n))
        dyn = dr2 * gv
        dy2 = rstd * (dyn - y2n * jnp.mean(dyn * y2n, axis=-1, keepdims=True))
        dy2_ref[...] = dy2.astype(BF16)

    vec = _full((1, D_MODEL))
    return pl.pallas_call(
        body,
        name="down_fwd",
        grid=(S // tm,),
        in_specs=[_rows(tm, D_FF), _halo_prev(tm, D_FF), _rows(tm, D_FF), _full((3, D_FF)), _full((1, D_FF)),
                  _full((D_FF, D_MODEL)), _rows(tm, D_MODEL), _rows(tm, D_MODEL), vec, _full((6, D_MODEL))],
        out_specs=[_rows(tm, D_FF), _rows(tm, D_MODEL), _rows(tm, D_MODEL), _full((1, 128)), vec, vec],
        out_shape=[jax.ShapeDtypeStruct((S, D_FF), BF16), jax.ShapeDtypeStruct((S, D_MODEL), BF16),
                   jax.ShapeDtypeStruct((S, D_MODEL), F32), jax.ShapeDtypeStruct((1, 128), F32),
                   jax.ShapeDtypeStruct((1, D_MODEL), F32), jax.ShapeDtypeStruct((1, D_MODEL), F32)],
        compiler_params=_params(1),
    )(gate, gate, val, conv_w, conv_b, w_down, x1, target, g_post, mod6)


def ffn_fwd(h2, w_up_g, conv_w, conv_b, w_down, x1, target, g_post, mod6, tm=256):
    S = x1.shape[0]

    def body(h_ref, wu_ref, cw_ref, cb_ref, wd_ref, x1_ref, tgt_ref, g_ref, mod_ref,
             gate_ref, val_ref, a_ref, dy2_ref, dout_ref, loss_ref, dgt_ref, dg_ref, carry):
        first = pl.program_id(0) == 0

        @pl.when(first)
        def _():
            carry[...] = jnp.zeros_like(carry)

        hb = h_ref[...]
        y2 = jnp.zeros((tm, D_MODEL), F32)
        for ch in range(2):
            cols = slice(ch * HALF_FF, (ch + 1) * HALF_FF)
            gb = jnp.dot(hb, wu_ref[ch], preferred_element_type=F32).astype(BF16)
            vb = jnp.dot(hb, wu_ref[2 + ch], preferred_element_type=F32).astype(BF16)
            gate_ref[:, cols] = gb
            val_ref[:, cols] = vb
            gt = gb.astype(F32)
            g1, g2 = _conv_taps(gt, carry[:, cols], first)
            carry[:, cols] = gt[tm - 8:, :]
            gc = g2 * cw_ref[0:1, cols] + g1 * cw_ref[1:2, cols] + gt * cw_ref[2:3, cols] + cb_ref[:, cols]
            ab = (_gelu(gc)[0] * vb.astype(F32)).astype(BF16)
            a_ref[:, cols] = ab
            y2 = y2 + jnp.dot(ab, wd_ref[cols, :], preferred_element_type=F32)
        rstd = lax.rsqrt(jnp.mean(y2 * y2, axis=-1, keepdims=True) + NORM_EPS)
        y2n = y2 * rstd
        gv = g_ref[...]
        gtf = mod_ref[5:6, :]
        r2 = y2n * gv
        diff = (x1_ref[...] + gtf * r2) - tgt_ref[...]
        _acc(loss_ref, jnp.zeros((1, 128), F32) + 0.5 * jnp.sum(diff * diff) * (1.0 / D_MODEL))
        dout = diff * (1.0 / D_MODEL)
        dout_ref[...] = dout
        _acc(dgt_ref, _colsum(dout * r2))
        dr2 = dout * gtf
        _acc(dg_ref, _colsum(dr2 * y2n))
        dyn = dr2 * gv
        dy2 = rstd * (dyn - y2n * jnp.mean(dyn * y2n, axis=-1, keepdims=True))
        dy2_ref[...] = dy2.astype(BF16)

    vec = _full((1, D_MODEL))
    return pl.pallas_call(
        body,
        name="ffn_fwd",
        grid=(S // tm,),
        in_specs=[_rows(tm, D_MODEL), _full(w_up_g.shape), _full((3, D_FF)), _full((1, D_FF)), _full((D_FF, D_MODEL)),
                  _rows(tm, D_MODEL), _rows(tm, D_MODEL), vec, _full((6, D_MODEL))],
        out_specs=[_rows(tm, D_FF), _rows(tm, D_FF), _rows(tm, D_FF), _rows(tm, D_MODEL), _rows(tm, D_MODEL), _full((1, 128)), vec, vec],
        out_shape=[jax.ShapeDtypeStruct((S, D_FF), BF16)] * 3 + [jax.ShapeDtypeStruct((S, D_MODEL), BF16),
                                                                 jax.ShapeDtypeStruct((S, D_MODEL), F32),
                                                                 jax.ShapeDtypeStruct((1, 128), F32),
                                                                 jax.ShapeDtypeStruct((1, D_MODEL), F32),
                                                                 jax.ShapeDtypeStruct((1, D_MODEL), F32)],
        scratch_shapes=[pltpu.VMEM((8, D_FF), F32)],
        compiler_params=_params(1),
    )(h2, w_up_g, conv_w, conv_b, w_down, x1, target, g_post, mod6)


def down_bwd(dy2, w_down, gate, val, conv_w, conv_b, a, h2, tm=256):
    S = dy2.shape[0]

    def body(dy_ref, w_ref, gate_ref, halo_ref, val_ref, cw_ref, cb_ref, a_ref, h_ref,
             dgc_ref, dval_ref, dcw_ref, dcb_ref, dwd_ref, dwu_ref):
        first = pl.program_id(0) == 0

        @pl.when(first)
        def _():
            dcw_ref[...] = jnp.zeros_like(dcw_ref)
            dcb_ref[...] = jnp.zeros_like(dcb_ref)
            dwd_ref[...] = jnp.zeros_like(dwd_ref)
            dwu_ref[...] = jnp.zeros_like(dwu_ref)

        dyb = dy_ref[...]
        hb = h_ref[...]
        def col(i):
            ch, off, width = FF_CHUNKS[i]
            return slice(ch * HALF_FF + off, ch * HALF_FF + off + width)

        def mm_da(i):
            return lax.dot_general(dyb, w_ref[col(i), :], NT, preferred_element_type=F32)

        def elementwise(i, da):
            cols = col(i)
            gt = gate_ref[:, cols].astype(F32)
            g1, g2 = _conv_taps(gt, halo_ref[:, cols].astype(F32), first)
            gc = g2 * cw_ref[0:1, cols] + g1 * cw_ref[1:2, cols] + gt * cw_ref[2:3, cols] + cb_ref[:, cols]
            ge, u, th, z2 = _gelu(gc)
            dgc = da * val_ref[:, cols].astype(F32) * _gelu_grad(gc, u, th, z2)
            dgc_ref[:, cols] = dgc.astype(BF16)
            dvb = (da * ge).astype(BF16)
            dval_ref[:, cols] = dvb
            dcb_ref[:, cols] += _colsum(dgc)
            dcw_ref[0:1, cols] += _colsum(dgc * g2)
            dcw_ref[1:2, cols] += _colsum(dgc * g1)
            dcw_ref[2:3, cols] += _colsum(dgc * gt)
            return dvb

        def mm_dw(i, dvb):
            ch, off, width = FF_CHUNKS[i]
            dwd_ref[col(i), :] += lax.dot_general(a_ref[:, col(i)], dyb, TN, preferred_element_type=F32)
            dwu_ref[ch, :, off:off + width] += lax.dot_general(hb, dvb, TN, preferred_element_type=F32)

        n = len(FF_CHUNKS)
        da = mm_da(0)
        prev = None
        for i in range(n):
            nxt = mm_da(i + 1) if i + 1 < n else None
            if prev is not None:
                mm_dw(i - 1, prev)
            prev = elementwise(i, da)
            da = nxt
        mm_dw(n - 1, prev)

    return pl.pallas_call(
        body,
        name="down_bwd",
        grid=(S // tm,),
        in_specs=[_rows(tm, D_MODEL), _full((D_FF, D_MODEL)), _rows(tm, D_FF), _halo_prev(tm, D_FF), _rows(tm, D_FF),
                  _full((3, D_FF)), _full((1, D_FF)), _rows(tm, D_FF), _rows(tm, D_MODEL)],
        out_specs=[_rows(tm, D_FF), _rows(tm, D_FF), _full((3, D_FF)), _full((1, D_FF)), _full((D_FF, D_MODEL)),
                   pl.BlockSpec((2, D_MODEL, HALF_FF), lambda i: (1, 0, 0), pipeline_mode=pl.Buffered(1))],
        out_shape=[jax.ShapeDtypeStruct((S, D_FF), BF16), jax.ShapeDtypeStruct((S, D_FF), BF16),
                   jax.ShapeDtypeStruct((3, D_FF), F32), jax.ShapeDtypeStruct((1, D_FF), F32),
                   jax.ShapeDtypeStruct((D_FF, D_MODEL), F32), jax.ShapeDtypeStruct((N_CHIPS, D_MODEL, HALF_FF), F32)],
        compiler_params=_params(1),
    )(dy2, w_down, gate, gate, val, conv_w, conv_b, a, h2)


def dw_matmul(a, b, out_blocks, blk_shape, a_cols, b_cols, a_blocked, name, prev=None, blk_off=0, n_blk=None, tm=512):
    S = a.shape[0]
    n_blk = out_blocks if n_blk is None else n_blk

    def body(*refs):
        a_ref, b_ref, o_ref = refs[0], refs[1], refs[-1]

        @pl.when(pl.program_id(1) == 0)
        def _():
            o_ref[...] = jnp.zeros_like(o_ref)

        o_ref[...] += lax.dot_general(a_ref[...], b_ref[...], TN, preferred_element_type=F32)

    a_spec = pl.BlockSpec((tm, a_cols), (lambda j, i: (i, j)) if a_blocked else (lambda j, i: (i, 0)))
    b_spec = pl.BlockSpec((tm, b_cols), (lambda j, i: (i, 0)) if a_blocked else (lambda j, i: (i, j)))
    in_specs = [a_spec, b_spec]
    args = [a, b]
    aliases = {}
    if prev is not None:
        in_specs.append(pl.BlockSpec(memory_space=pl.ANY))
        args.append(prev)
        aliases = {2: 0}
    return pl.pallas_call(
        body,
        name=name,
        grid=(n_blk, S // tm),
        in_specs=in_specs,
        out_specs=pl.BlockSpec((None,) + tuple(blk_shape), lambda j, i: (j + blk_off, 0, 0)),
        out_shape=jax.ShapeDtypeStruct((out_blocks,) + tuple(blk_shape), F32),
        input_output_aliases=aliases,
        compiler_params=_params(2),
    )(*args)


def up_bwd(dgc, dval, conv_w, w_up_g, x1, dout, y1, g_pre, g_post, mod6, h2, dw_up, tm=256):
    S = x1.shape[0]
    last_blk = S // 16 - 1

    def body(dgc_ref, nxt_ref, dval_ref, cw_ref, w_ref, x1_ref, dout_ref, y1_ref, gpre_ref, gpost_ref, mod_ref, h_ref, dwin_ref,
             dx1_ref, dy1_ref, dsh_ref, dsc_ref, dgpre_ref, dgt_ref, dgpost_ref, dwu_ref):
        last = pl.program_id(0) == pl.num_programs(0) - 1

        @pl.when(pl.program_id(0) == 0)
        def _():
            dwu_ref[...] = jnp.zeros_like(dwu_ref)

        hb = h_ref[...]
        dh = jnp.zeros((tm, D_MODEL), F32)
        for ch in range(2):
            cols = slice(ch * HALF_FF, (ch + 1) * HALF_FF)
            dg = dgc_ref[:, cols].astype(F32)
            nx = jnp.where(last, 0.0, nxt_ref[:, cols].astype(F32))
            row = lax.broadcasted_iota(jnp.int32, dg.shape, 0)
            n0, n1 = nx[0:1, :], nx[1:2, :]
            u1 = jnp.where(row == tm - 1, n0, pltpu.roll(dg, tm - 1, 0))
            u2 = jnp.where(row == tm - 1, n1, jnp.where(row == tm - 2, n0, pltpu.roll(dg, tm - 2, 0)))
            dgate = (dg * cw_ref[2:3, cols] + u1 * cw_ref[1:2, cols] + u2 * cw_ref[0:1, cols]).astype(BF16)
            dwu_ref[ch] += lax.dot_general(hb, dgate, TN, preferred_element_type=F32)
            dh = dh + lax.dot_general(dgate, w_ref[ch], NT, preferred_element_type=F32)
            dh = dh + lax.dot_general(dval_ref[:, cols], w_ref[2 + ch], NT, preferred_element_type=F32)
        x1 = x1_ref[...]
        rstd = lax.rsqrt(jnp.mean(x1 * x1, axis=-1, keepdims=True) + NORM_EPS)
        n2 = x1 * rstd
        gpre = gpre_ref[...]
        one_sc = 1.0 + mod_ref[4:5, :]
        _acc(dsh_ref, _colsum(dh))
        _acc(dsc_ref, _colsum(dh * (n2 * gpre)))
        _acc(dgpre_ref, _colsum(dh * one_sc * n2))
        dn = dh * (gpre * one_sc)
        dx1 = dout_ref[...] + rstd * (dn - n2 * jnp.mean(dn * n2, axis=-1, keepdims=True))
        dx1_ref[...] = dx1
        y1 = y1_ref[...]
        rstd1 = lax.rsqrt(jnp.mean(y1 * y1, axis=-1, keepdims=True) + NORM_EPS)
        y1n = y1 * rstd1
        gpost = gpost_ref[...]
        gtm = mod_ref[2:3, :]
        _acc(dgt_ref, _colsum(dx1 * (y1n * gpost)))
        dr1 = dx1 * gtm
        _acc(dgpost_ref, _colsum(dr1 * y1n))
        dyn = dr1 * gpost
        dy1 = rstd1 * (dyn - y1n * jnp.mean(dyn * y1n, axis=-1, keepdims=True))
        dy1_ref[...] = dy1.astype(BF16)

    vec = _full((1, D_MODEL))
    nxt = pl.BlockSpec((16, D_FF), lambda i: (jnp.minimum((i + 1) * (tm // 16), last_blk), 0))
    return pl.pallas_call(
        body,
        name="up_bwd",
        grid=(S // tm,),
        in_specs=[_rows(tm, D_FF), nxt, _rows(tm, D_FF), _full((3, D_FF)), _full(w_up_g.shape), _rows(tm, D_MODEL),
                  _rows(tm, D_MODEL), _rows(tm, D_MODEL), vec, vec, _full((6, D_MODEL)), _rows(tm, D_MODEL),
                  pl.BlockSpec(memory_space=pl.ANY)],
        out_specs=[_rows(tm, D_MODEL), _rows(tm, D_MODEL), vec, vec, vec, vec, vec,
                   pl.BlockSpec((2, D_MODEL, HALF_FF), lambda i: (0, 0, 0), pipeline_mode=pl.Buffered(1))],
        out_shape=[jax.ShapeDtypeStruct((S, D_MODEL), F32), jax.ShapeDtypeStruct((S, D_MODEL), BF16)]
        + [jax.ShapeDtypeStruct((1, D_MODEL), F32)] * 5 + [jax.ShapeDtypeStruct(dw_up.shape, F32)],
        input_output_aliases={12: 7},
        compiler_params=_params(1),
    )(dgc, dgc, dval, conv_w, w_up_g, x1, dout, y1, g_pre, g_post, mod6, h2, dw_up)


def outproj_bwd(dy1, w_out_g, cat, tm=512):
    S = dy1.shape[0]

    def body(dy_ref, w_ref, cat_ref, dpool_ref, dattn_ref, da4_ref, da16_ref, delta_ref, dl4_ref, dl16_ref, dw_ref, scr):
        @pl.when(pl.program_id(0) == 0)
        def _():
            dw_ref[...] = jnp.zeros_like(dw_ref)

        catb = cat_ref[...]
        dcat = jnp.zeros((tm, 512), F32)
        for j in range(N_CHIPS):
            dyj = dy_ref[:, j * 256:(j + 1) * 256]
            dcat = dcat + lax.dot_general(dyj, w_ref[j], NT, preferred_element_type=F32)
            dw_ref[j] += lax.dot_general(catb, dyj, TN, preferred_element_type=F32)
        dpool_ref[...] = dcat[:, :POOL_W]
        dattn = dcat[:, POOL_W:]
        dattn_ref[...] = dattn.astype(BF16)
        for h in range(2):
            scr[h] = dattn[:, h * 128:(h + 1) * 128]
        for d, dst in ((4, da4_ref), (16, da16_ref)):
            for r in range(d):
                for h in range(2):
                    dst[r, :, h * 128:(h + 1) * 128] = scr[h, pl.ds(r, tm // d, stride=d), :].astype(BF16)
        prod = dattn * catb[:, POOL_W:].astype(F32)
        r = lax.broadcasted_iota(jnp.int32, (GROUP_W, GROUP_W), 0) // HEAD_DIM
        c = lax.broadcasted_iota(jnp.int32, (GROUP_W, GROUP_W), 1) // HEAD_DIM
        ones_bd = jnp.where(r == c, 1.0, 0.0).astype(BF16)
        hi = prod.astype(BF16)
        lo = (prod - hi.astype(F32)).astype(BF16)
        delta = jnp.dot(hi, ones_bd, preferred_element_type=F32) + jnp.dot(lo, ones_bd, preferred_element_type=F32)
        delta_ref[...] = delta
        for h in range(2):
            scr[h] = delta[:, h * 128:(h + 1) * 128]
        for d, dst in ((4, dl4_ref), (16, dl16_ref)):
            for r in range(d):
                for h in range(2):
                    dst[r, :, h * 128:(h + 1) * 128] = scr[h, pl.ds(r, tm // d, stride=d), :]

    cls = lambda d: pl.BlockSpec((d, tm // d, GROUP_W), lambda i: (0, i, 0))
    cls_shape = lambda d, dt: jax.ShapeDtypeStruct((d, S // d, GROUP_W), dt)
    return pl.pallas_call(
        body,
        name="outproj_bwd",
        grid=(S // tm,),
        in_specs=[_rows(tm, D_MODEL), _full(w_out_g.shape), _rows(tm, 512)],
        out_specs=[_rows(tm, POOL_W), _rows(tm, GROUP_W), cls(4), cls(16), _rows(tm, GROUP_W), cls(4), cls(16),
                   _full(w_out_g.shape)],
        out_shape=[jax.ShapeDtypeStruct((S, POOL_W), F32), jax.ShapeDtypeStruct((S, GROUP_W), BF16), cls_shape(4, BF16),
                   cls_shape(16, BF16), jax.ShapeDtypeStruct((S, GROUP_W), F32), cls_shape(4, F32), cls_shape(16, F32),
                   jax.ShapeDtypeStruct(w_out_g.shape, F32)],
        scratch_shapes=[pltpu.VMEM((2, tm, 128), F32)],
        compiler_params=_params(1),
    )(dy1, w_out_g, cat)


def attn_bwd(qkv, dattn, lse, delta, d):
    L = qkv.shape[2]
    nb = L // ATT_BLOCK

    def body(q_ref, k_ref, v_ref, do_ref, l_ref, dl_ref, out_ref, kpad, vpad, dkpad, dvpad):
        kpad[0:ATT_BLOCK, :] = jnp.zeros((ATT_BLOCK, GROUP_W), BF16)
        vpad[0:ATT_BLOCK, :] = jnp.zeros((ATT_BLOCK, GROUP_W), BF16)
        kpad[ATT_BLOCK:, :] = k_ref[...]
        vpad[ATT_BLOCK:, :] = v_ref[...]
        dkpad[...] = jnp.zeros_like(dkpad)
        dvpad[...] = jnp.zeros_like(dvpad)
        band, col, lo = _attn_masks()

        def step(n, carry):
            r0 = pl.multiple_of(n * ATT_BLOCK, ATT_BLOCK)
            valid = band & ((col >= ATT_BLOCK) | (n > 0))
            qb = q_ref[pl.ds(r0, ATT_BLOCK), :]
            dob = do_ref[pl.ds(r0, ATT_BLOCK), :]
            lb = l_ref[pl.ds(r0, ATT_BLOCK), :]
            dlb = dl_ref[pl.ds(r0, ATT_BLOCK), :]
            kb = kpad[pl.ds(r0, 2 * ATT_BLOCK), :]
            vb = vpad[pl.ds(r0, 2 * ATT_BLOCK), :]
            for pair in range(2):
                lanes = slice(pair * 128, (pair + 1) * 128)
                qp, dop, kp, vp = qb[:, lanes], dob[:, lanes], kb[:, lanes], vb[:, lanes]
                c0, c1 = pair * 128, pair * 128 + HEAD_DIM
                q2, do2 = _stack_heads(qp, lo), _stack_heads(dop, lo)
                lse2 = jnp.concatenate([lb[:, c0:c0 + 1], lb[:, c1:c1 + 1]], axis=0)
                dl2 = jnp.concatenate([dlb[:, c0:c0 + 1], dlb[:, c1:c1 + 1]], axis=0)
                s = lax.dot_general(q2, kp, NT, preferred_element_type=F32)
                s = jnp.where(valid, s, NEG)
                p = jnp.exp(s - lse2)
                dp = lax.dot_general(do2, vp, NT, preferred_element_type=F32)
                ds = (p * (dp - dl2)).astype(BF16)
                dq2 = jnp.dot(ds, kp, preferred_element_type=F32)
                out_ref[0, pl.ds(r0, ATT_BLOCK), lanes] = _unstack_heads(dq2, lo)
                dkpad[pl.ds(r0, 2 * ATT_BLOCK), lanes] += lax.dot_general(ds, q2, TN, preferred_element_type=F32)
                dvpad[pl.ds(r0, 2 * ATT_BLOCK), lanes] += lax.dot_general(p.astype(BF16), do2, TN, preferred_element_type=F32)
            return carry

        lax.fori_loop(0, nb, step, 0, unroll=min(4, nb))
        out_ref[1] = dkpad[ATT_BLOCK:, :]
        out_ref[2] = dvpad[ATT_BLOCK:, :]

    spec = lambda kind: pl.BlockSpec((None, None, L, GROUP_W), lambda r: (kind, r, 0, 0))
    cls = pl.BlockSpec((None, L, GROUP_W), lambda r: (r, 0, 0))
    return pl.pallas_call(
        body,
        name=f"attn_bwd_d{d}",
        grid=(d,),
        in_specs=[spec(0), spec(1), spec(2), cls, cls, cls],
        out_specs=pl.BlockSpec((3, None, L, GROUP_W), lambda r: (0, r, 0, 0)),
        out_shape=jax.ShapeDtypeStruct((3, d, L, GROUP_W), F32),
        scratch_shapes=[pltpu.VMEM((L + ATT_BLOCK, GROUP_W), BF16)] * 2 + [pltpu.VMEM((L + ATT_BLOCK, GROUP_W), F32)] * 2,
        compiler_params=_params(1),
    )(qkv, qkv, qkv, dattn, lse, delta)


def pool_bwd(dpool, mixed, wbd, b, scale):
    S = dpool.shape[0]

    def body(dp_ref, mx_ref, w_ref, b_ref, s_ref, du_ref, dw_ref, db_ref, ds_ref):
        dp = dp_ref[...]
        mb = mx_ref[...]
        wv = w_ref[...]
        ypre = jnp.dot(mb, wv, preferred_element_type=F32) + b_ref[...]
        ds_ref[...] = _colsum(dp * ypre)
        dpre = dp * s_ref[...]
        db_ref[...] = _colsum(dpre)
        dpb = dpre.astype(BF16)
        dw_ref[...] = lax.dot_general(mb, dpb, TN, preferred_element_type=F32)
        dmix = lax.dot_general(dpb, wv, NT, preferred_element_type=F32)
        row = lax.broadcasted_iota(jnp.int32, dmix.shape, 0)
        lane, win = _pool_lane_windows(dmix.shape)
        e = dmix / jnp.minimum(row + 1, win).astype(F32)

        def shift(a, k):
            return jnp.where(row < S - k, pltpu.roll(a, S - k, 0), 0.0)

        f2 = e + shift(e, 1)
        f4 = f2 + shift(f2, 2)
        f8 = f4 + shift(f4, 4)
        f16 = f8 + shift(f8, 8)
        du_ref[...] = jnp.where(lane < 64, f2, jnp.where(lane < 128, f4, jnp.where(lane < 192, f8, f16))) - dmix

    vm = pl.BlockSpec(memory_space=pltpu.VMEM)
    return pl.pallas_call(
        body,
        name="pool_bwd",
        in_specs=[vm] * 5,
        out_specs=[vm] * 4,
        out_shape=[jax.ShapeDtypeStruct((S, POOL_W), F32), jax.ShapeDtypeStruct((POOL_W, POOL_W), F32),
                   jax.ShapeDtypeStruct((1, POOL_W), F32), jax.ShapeDtypeStruct((1, POOL_W), F32)],
        compiler_params=_params(),
    )(dpool, mixed, wbd, b, scale)


def inproj_bwd(dqkv, du, x, dx1, w_in_g, g, mod6, tc, tsa, tsb, h1, tm=256):
    S = x.shape[0]

    def body(d0, d1, d2, du_ref, x_ref, dx1_ref, w_ref, g_ref, mod_ref, tc_ref, tsa_ref, tsb_ref, h_ref,
             gx_ref, dsh_ref, dsc_ref, dg_ref, dw_ref, s4, s16, dp_ref):
        @pl.when(pl.program_id(0) == 0)
        def _():
            dw_ref[...] = jnp.zeros_like(dw_ref)

        cs, sa, sb = tc_ref[...], tsa_ref[...], tsb_ref[...]
        for d, src, dst in ((4, d1, s4), (16, d2, s16)):
            for kind in range(3):
                for r in range(d):
                    for h in range(2):
                        dst[kind, h, pl.ds(r, tm // d, stride=d), :] = src[kind, r, :, h * 128:(h + 1) * 128]
        for sp in range(20):
            piece, half = sp // 2, sp % 2
            lanes = slice(half * 128, (half + 1) * 128)
            if piece == 0:
                blk = du_ref[:, lanes]
            else:
                kind, gi = (piece - 1) // 3, (piece - 1) % 3
                blk = d0[kind, 0, :, lanes] if gi == 0 else (s4, s16)[gi - 1][kind, half]
                if kind == 0:
                    blk = _rope128(blk, cs, sa, sb, -1.0) * (HEAD_DIM ** -0.5)
                elif kind == 1:
                    blk = _rope128(blk, cs, sa, sb, -1.0)
            dp_ref[:, sp * 128:(sp + 1) * 128] = blk.astype(BF16)
        dh = jnp.zeros((tm, D_MODEL), F32)
        hb = h_ref[...]
        for j in range(N_CHIPS):
            dpj = dp_ref[:, j * 640:(j + 1) * 640]
            dh = dh + lax.dot_general(dpj, w_ref[j], NT, preferred_element_type=F32)
            dw_ref[j] += lax.dot_general(hb, dpj, TN, preferred_element_type=F32)
        xv = x_ref[...]
        rstd = lax.rsqrt(jnp.mean(xv * xv, axis=-1, keepdims=True) + NORM_EPS)
        n1 = xv * rstd
        gv = g_ref[...]
        one_sc = 1.0 + mod_ref[1:2, :]
        _acc(dsh_ref, _colsum(dh))
        _acc(dsc_ref, _colsum(dh * (n1 * gv)))
        _acc(dg_ref, _colsum(dh * one_sc * n1))
        dn = dh * (gv * one_sc)
        gx_ref[...] = dx1_ref[...] + rstd * (dn - n1 * jnp.mean(dn * n1, axis=-1, keepdims=True))

    vec = _full((1, D_MODEL))
    dspec = lambda d: pl.BlockSpec((3, d, tm // d, GROUP_W), lambda i: (0, 0, i, 0))
    return pl.pallas_call(
        body,
        name="inproj_bwd",
        grid=(S // tm,),
        in_specs=[dspec(d) for d in DILATIONS] + [_rows(tm, POOL_W), _rows(tm, D_MODEL), _rows(tm, D_MODEL), _full(w_in_g.shape),
                                                  vec, _full((6, D_MODEL)), _rows(tm, 128), _rows(tm, 128), _rows(tm, 128),
                                                  _rows(tm, D_MODEL)],
        out_specs=[_rows(tm, D_MODEL), vec, vec, vec, _full(w_in_g.shape)],
        out_shape=[jax.ShapeDtypeStruct((S, D_MODEL), F32)] + [jax.ShapeDtypeStruct((1, D_MODEL), F32)] * 3
        + [jax.ShapeDtypeStruct(w_in_g.shape, F32)],
        scratch_shapes=[pltpu.VMEM((3, 2, tm, 128), F32)] * 2 + [pltpu.VMEM((tm, IN_W), BF16)],
        compiler_params=_params(1),
    )(*dqkv, du, x, dx1, w_in_g, g, mod6, tc, tsa, tsb, h1)


def _adamw(w, g, m, v):
    m = ADAM_B1 * m + (1.0 - ADAM_B1) * g
    v = ADAM_B2 * v + (1.0 - ADAM_B2) * (g * g)
    m_hat = m / (1.0 - ADAM_B1 ** ADAM_STEP)
    v_hat = v / (1.0 - ADAM_B2 ** ADAM_STEP)
    delta = -ADAM_LR * (m_hat / (jnp.sqrt(v_hat) + ADAM_EPS) + ADAM_WD * w)
    return delta, m, v


def adamw_rows(w, g, m, v, tr, name):
    R, C = w.shape

    def body(w_ref, g_ref, m_ref, v_ref, go_ref, d_ref, mo_ref, vo_ref):
        g = g_ref[...]
        go_ref[...] = g
        d_ref[...], mo_ref[...], vo_ref[...] = _adamw(w_ref[...], g, m_ref[...], v_ref[...])

    spec = pl.BlockSpec((tr, C), lambda i: (i, 0))
    return pl.pallas_call(
        body,
        name=name,
        grid=(R // tr,),
        in_specs=[spec] * 4,
        out_specs=[spec] * 4,
        out_shape=[jax.ShapeDtypeStruct((R, C), F32)] * 4,
        compiler_params=_params(1),
    )(w, g, m, v)


def adamw_ada(c_all_t, dmod_cols, w, m, v, tr=256):
    R, C = w.shape

    def body(ct_ref, dm_ref, w_ref, m_ref, v_ref, g_ref, d_ref, mo_ref, vo_ref):
        ct = ct_ref[...]
        act = ct * jax.nn.sigmoid(ct)
        g = jnp.zeros((tr, C), F32)
        for b in range(N_DEV):
            g = g + act[:, b:b + 1] * dm_ref[b:b + 1, :]
        g_ref[...] = g
        d_ref[...], mo_ref[...], vo_ref[...] = _adamw(w_ref[...], g, m_ref[...], v_ref[...])

    spec = pl.BlockSpec((tr, C), lambda i: (i, 0))
    return pl.pallas_call(
        body,
        name="adamw_ada",
        grid=(R // tr,),
        in_specs=[pl.BlockSpec((tr, N_DEV), lambda i: (i, 0)), _full((N_DEV, C)), spec, spec, spec],
        out_specs=[spec] * 4,
        out_shape=[jax.ShapeDtypeStruct((R, C), F32)] * 4,
        compiler_params=_params(1),
    )(c_all_t, dmod_cols, w, m, v)


def adamw_small(slab_a, slab_b, convw_g, wpool_g, params):
    names = ["b_ada", "g_pre_mix", "g_post_mix", "g_pre_ffn", "g_post_ffn", "b_pool", "pool_scale", "conv_b", "conv_w", "w_pool"]
    flat = []
    for n in names:
        flat += list(params[n])

    def body(a_ref, b_ref, cw_ref, wp_ref, *rest):
        ins, outs = rest[:30], rest[30:]

        def dev_sum(ref):
            t = ref[0]
            for dev in range(1, N_DEV):
                t = t + ref[dev]
            return t

        sa, sb_, scw, swp = dev_sum(a_ref), dev_sum(b_ref), dev_sum(cw_ref), dev_sum(wp_ref)
        grads = [
            jnp.concatenate([sa[k:k + 1, :] for k in range(6)], axis=1),
            sa[6:7, :], sa[7:8, :], sa[8:9, :], sa[9:10, :],
            sa[10:11, 0:256], sa[10:11, 256:512],
            sb_[3:4, :], scw, swp,
        ]
        for i, g in enumerate(grads):
            w_ref, m_ref, v_ref = ins[3 * i:3 * i + 3]
            if names[i] in ("b_pool", "w_pool"):
                for grp in range(4):
                    if names[i] == "b_pool":
                        gp, at = g[:, grp * 64:(grp + 1) * 64], (0, slice(grp, grp + 1))
                    else:
                        gp, at = g[grp * 64:(grp + 1) * 64, :], (0, grp)
                    d, mo, vo = _adamw(w_ref[at], gp, m_ref[at], v_ref[at])
                    for k, val in enumerate((gp, d, mo, vo)):
                        outs[4 * i + k][at] = val
                continue
            d, mo, vo = _adamw(w_ref[...], g, m_ref[...], v_ref[...])
            outs[4 * i][...] = g
            outs[4 * i + 1][...] = d
            outs[4 * i + 2][...] = mo
            outs[4 * i + 3][...] = vo
        outs[-1][...] = sa[10:11, 512:640]

    vm = pl.BlockSpec(memory_space=pltpu.VMEM)
    out_shape = []
    for n in names:
        out_shape += [jax.ShapeDtypeStruct(params[n][0].shape, F32)] * 4
    out_shape.append(jax.ShapeDtypeStruct((1, 128), F32))
    outs = pl.pallas_call(
        body,
        name="adamw_small",
        in_specs=[vm] * (4 + len(flat)),
        out_specs=[vm] * len(out_shape),
        out_shape=out_shape,
        compiler_params=_params(),
    )(slab_a, slab_b, convw_g, wpool_g, *flat)
    return {n: outs[4 * i:4 * i + 4] for i, n in enumerate(names)}, outs[-1]


def _place():
    return lax.axis_index("x"), lax.axis_index("y"), lax.axis_index("c")


def _other_chips(x, y):
    return [(1 - x, y), (x, 1 - y), (1 - x, 1 - y)]


def _chip_id(cx, cy):
    return 2 * cx + cy


def gather_weights(shards):
    n = len(shards)
    halved = [s.shape[0] % 32 == 0 for s in shards]

    def body(*refs):
        ins, outs = refs[:n], refs[n:2 * n]
        send_sems, recv_sems, loc_sems = refs[2 * n:]
        x, y, c = _place()
        me = _chip_id(x, y)
        chips = _other_chips(x, y)
        sib = (x, y, 1 - c)

        def part(w, chip, half):
            if not halved[w]:
                return outs[w].at[chip]
            rh = shards[w].shape[0] // 2
            return outs[w].at[chip, pl.ds(half * rh, rh), :]

        def src_part(w):
            if not halved[w]:
                return ins[w]
            rh = shards[w].shape[0] // 2
            return ins[w].at[pl.ds(c * rh, rh), :]

        def rcopy(w, k, src, dst, to):
            return pltpu.make_async_remote_copy(src_ref=src, dst_ref=dst, send_sem=send_sems.at[6 * w + k],
                                                recv_sem=recv_sems.at[6 * w + k], device_id=to, device_id_type=MESH)

        local = [pltpu.make_async_copy(ins[w], outs[w].at[me], loc_sems.at[w]) for w in range(n)]
        for cp in local:
            cp.start()
        first = []
        for w in range(n):
            for k, (cx, cy) in enumerate(chips):
                cp = rcopy(w, k, src_part(w), part(w, me, c), (cx, cy, c))
                cp.start()
                first.append(cp)
        passed = []
        for w in range(n):
            for k, (cx, cy) in enumerate(chips):
                blk = part(w, _chip_id(cx, cy), c)
                rcopy(w, k, blk, blk, (cx, cy, c)).wait_recv()
                if halved[w]:
                    cp = rcopy(w, 3 + k, blk, blk, sib)
                    cp.start()
                    passed.append(cp)
        for w in range(n):
            if halved[w]:
                for k, (cx, cy) in enumerate(chips):
                    blk = part(w, _chip_id(cx, cy), 1 - c)
                    rcopy(w, 3 + k, blk, blk, sib).wait_recv()
        for cp in first + passed:
            cp.wait_send()
        for cp in local:
            cp.wait()

    hbm = pl.BlockSpec(memory_space=pl.ANY)
    return pl.pallas_call(
        body,
        name="gather_weights",
        in_specs=[hbm] * n,
        out_specs=[hbm] * n,
        out_shape=[jax.ShapeDtypeStruct((N_CHIPS,) + s.shape, s.dtype) for s in shards],
        scratch_shapes=[pltpu.SemaphoreType.DMA((6 * n,)), pltpu.SemaphoreType.DMA((6 * n,)), pltpu.SemaphoreType.DMA((n,))],
        compiler_params=pltpu.CompilerParams(has_side_effects=True, vmem_limit_bytes=VMEM_LIMIT),
    )(*shards)


HBM_SPEC = pl.BlockSpec(memory_space=pltpu.HBM)
SEM_SPEC = pl.BlockSpec(memory_space=pltpu.SEMAPHORE)
ANY_SPEC = pl.BlockSpec(memory_space=pl.ANY)
EFFECT = pltpu.SideEffectType.DATAFLOW_SIDE_EFFECTING


def _hbm(t):
    return pltpu.with_memory_space_constraint(t, pltpu.HBM)


def _hbm_shapes(ts):
    return [pltpu.HBM(t.shape, t.dtype) for t in ts]


def _half_rows(ref, lead, half, rh):
    return ref.at[lead, pl.ds(half * rh, rh), :]


def gather_split_start(shards, lands, carry, k):
    n = len(shards)

    def body(*refs):
        ins, land = refs[:n], refs[n:2 * n]
        send_sems, recv_sems = refs[2 * n + 1], refs[2 * n + 2]
        loc_sems = refs[-1]
        x, y, c = _place()
        me = _chip_id(x, y)
        if k == 0:
            local = [pltpu.make_async_copy(ins[w], land[w].at[me], loc_sems.at[w]) for w in range(n)]
            for cp in local:
                cp.start()
            for cp in local:
                cp.wait()
        cx, cy = _other_chips(x, y)[k]
        for w in range(n):
            rh = shards[w].shape[0] // 2
            pltpu.make_async_remote_copy(src_ref=ins[w].at[pl.ds(c * rh, rh), :], dst_ref=_half_rows(land[w], me, c, rh),
                                         send_sem=send_sems.at[w], recv_sem=recv_sems.at[w],
                                         device_id=(cx, cy, c), device_id_type=MESH).start()

    args = [_hbm(s) for s in shards] + [_hbm(l) for l in lands] + [_hbm(carry)]
    outs = pl.pallas_call(
        body,
        name="gather_split_start%d" % k,
        out_shape=[pltpu.SemaphoreType.DMA((n,)), pltpu.SemaphoreType.DMA((n,))] + _hbm_shapes(shards) + _hbm_shapes(lands)
        + _hbm_shapes([carry]),
        in_specs=[HBM_SPEC] * (2 * n + 1),
        out_specs=[SEM_SPEC, SEM_SPEC] + [HBM_SPEC] * (2 * n + 1),
        input_output_aliases={i: 2 + i for i in range(2 * n + 1)},
        scratch_shapes=[pltpu.SemaphoreType.DMA((n,))],
        compiler_params=pltpu.CompilerParams(has_side_effects=EFFECT),
    )(*args)
    return outs[0], outs[1], list(outs[2:2 + n]), list(outs[2 + n:2 + 2 * n]), outs[-1]


def gather_split_mid(sems, shards, lands, after):
    n = len(shards)

    def body(*refs):
        ins, land = refs[:n], refs[n:2 * n]
        sem_in = refs[2 * n:2 * n + 6]
        fsend, frecv = refs[2 * n + 7], refs[2 * n + 8]
        x, y, c = _place()
        me = _chip_id(x, y)
        chips = _other_chips(x, y)
        for w in range(n):
            rh = shards[w].shape[0] // 2
            for k, (cx, cy) in enumerate(chips):
                got = _half_rows(land[w], _chip_id(cx, cy), c, rh)
                cp = pltpu.make_async_remote_copy(src_ref=ins[w].at[pl.ds(c * rh, rh), :], dst_ref=got, send_sem=sem_in[2 * k].at[w],
                                                  recv_sem=sem_in[2 * k + 1].at[w], device_id=(cx, cy, c), device_id_type=MESH)
                cp.wait_send()
                cp.wait_recv()
        for w in range(n):
            rh = shards[w].shape[0] // 2
            for k, (cx, cy) in enumerate(chips):
                got = _half_rows(land[w], _chip_id(cx, cy), c, rh)
                pltpu.make_async_remote_copy(src_ref=got, dst_ref=got, send_sem=fsend.at[3 * w + k], recv_sem=frecv.at[3 * w + k],
                                             device_id=(x, y, 1 - c), device_id_type=MESH).start()

    outs = pl.pallas_call(
        body,
        name="gather_split_mid",
        out_shape=[pltpu.SemaphoreType.DMA((3 * n,)), pltpu.SemaphoreType.DMA((3 * n,))] + _hbm_shapes(lands),
        in_specs=[HBM_SPEC] * (2 * n) + [SEM_SPEC] * 6 + [ANY_SPEC],
        out_specs=[SEM_SPEC, SEM_SPEC] + [HBM_SPEC] * n,
        input_output_aliases={n + i: 2 + i for i in range(n)},
        compiler_params=pltpu.CompilerParams(has_side_effects=EFFECT),
    )(*shards, *lands, *sems, after)
    return outs[0], outs[1], list(outs[2:])


def gather_split_done(fsend, frecv, lands, after):
    n = len(lands)

    def body(*refs):
        land = refs[:n]
        ssem, rsem = refs[n], refs[n + 1]
        x, y, c = _place()
        for w in range(n):
            rh = lands[w].shape[1] // 2
            for k, (cx, cy) in enumerate(_other_chips(x, y)):
                sent = _half_rows(land[w], _chip_id(cx, cy), c, rh)
                got = _half_rows(land[w], _chip_id(cx, cy), 1 - c, rh)
                cp = pltpu.make_async_remote_copy(src_ref=sent, dst_ref=got, send_sem=ssem.at[3 * w + k], recv_sem=rsem.at[3 * w + k],
                                                  device_id=(x, y, 1 - c), device_id_type=MESH)
                cp.wait_send()
                cp.wait_recv()

    outs = pl.pallas_call(
        body,
        name="gather_split_done",
        out_shape=_hbm_shapes(lands),
        in_specs=[HBM_SPEC] * n + [SEM_SPEC, SEM_SPEC, ANY_SPEC],
        out_specs=[HBM_SPEC] * n,
        input_output_aliases={i: i for i in range(n)},
        compiler_params=pltpu.CompilerParams(has_side_effects=EFFECT),
    )(*lands, fsend, frecv, after)
    return list(outs)


def _flips():
    return [(fx, fy, fc) for fx in (0, 1) for fy in (0, 1) for fc in (0, 1)][1:]


def _flip(v, f):
    return v if f == 0 else 1 - v


def ada_mod(c3, w_ada, b_cols, conv_w):
    CB = w_ada.shape[1]

    def body(c_ref, w_ref, b_ref, cw_ref, call_ref, mod_ref, cwall_ref, modall, send_sems, recv_sems):
        x, y, c = _place()
        me_dev = 4 * x + 2 * y + c
        me = _chip_id(x, y)
        call_ref[me_dev] = c_ref[0]
        cwall_ref[me] = cw_ref[...]
        sends = []
        for k, (cx, cy) in enumerate(_other_chips(x, y)):
            cp = pltpu.make_async_remote_copy(src_ref=cw_ref, dst_ref=cwall_ref.at[me], send_sem=send_sems.at[10 + k],
                                              recv_sem=recv_sems.at[10 + k], device_id=(cx, cy, c), device_id_type=MESH)
            cp.start()
            sends.append(cp)
        for k, (fx, fy, fc) in enumerate(_flips()):
            cp = pltpu.make_async_remote_copy(src_ref=c_ref.at[0], dst_ref=call_ref.at[me_dev], send_sem=send_sems.at[k],
                                              recv_sem=recv_sems.at[k],
                                              device_id=(_flip(x, fx), _flip(y, fy), _flip(c, fc)), device_id_type=MESH)
            cp.start()
            sends.append(cp)
        for k, (fx, fy, fc) in enumerate(_flips()):
            peer = 4 * _flip(x, fx) + 2 * _flip(y, fy) + _flip(c, fc)
            pltpu.make_async_remote_copy(src_ref=c_ref.at[0], dst_ref=call_ref.at[peer], send_sem=send_sems.at[k],
                                         recv_sem=recv_sems.at[k], device_id=(x, y, c), device_id_type=MESH).wait_recv()
        row = lax.broadcasted_iota(jnp.int32, (N_DEV, D_MODEL), 0)
        call = jnp.zeros((N_DEV, D_MODEL), F32)
        for dev in range(N_DEV):
            call = jnp.where(row == dev, call_ref[dev], call)
        act = call * jax.nn.sigmoid(call)
        modall[me] = jnp.dot(act, w_ref[...], preferred_element_type=F32, precision=lax.Precision.HIGHEST) + b_ref[...]
        for k, (cx, cy) in enumerate(_other_chips(x, y)):
            cp = pltpu.make_async_remote_copy(src_ref=modall.at[me], dst_ref=modall.at[me], send_sem=send_sems.at[7 + k],
                                              recv_sem=recv_sems.at[7 + k], device_id=(cx, cy, c), device_id_type=MESH)
            cp.start()
            sends.append(cp)
        for k, (cx, cy) in enumerate(_other_chips(x, y)):
            blk = modall.at[_chip_id(cx, cy)]
            pltpu.make_async_remote_copy(src_ref=blk, dst_ref=blk, send_sem=send_sems.at[7 + k], recv_sem=recv_sems.at[7 + k],
                                         device_id=(x, y, c), device_id_type=MESH).wait_recv()
        for k, (cx, cy) in enumerate(_other_chips(x, y)):
            blk = cwall_ref.at[_chip_id(cx, cy)]
            pltpu.make_async_remote_copy(src_ref=blk, dst_ref=blk, send_sem=send_sems.at[10 + k], recv_sem=recv_sems.at[10 + k],
                                         device_id=(x, y, c), device_id_type=MESH).wait_recv()
        for cp in sends:
            cp.wait_send()
        mine = [modall[j, pl.ds(me_dev, 1), :] for j in range(N_CHIPS)]
        for r in range(6):
            pieces = []
            for h in range(2):
                pos = r * D_MODEL + h * 512
                pieces.append(mine[pos // CB][:, pos % CB:pos % CB + 512])
            mod_ref[r:r + 1, :] = jnp.concatenate(pieces, axis=1)

    vm = pl.BlockSpec(memory_space=pltpu.VMEM)
    return pl.pallas_call(
        body,
        name="ada_mod",
        in_specs=[vm] * 4,
        out_specs=[vm] * 3,
        out_shape=[jax.ShapeDtypeStruct((N_DEV, 1, D_MODEL), F32), jax.ShapeDtypeStruct((6, D_MODEL), F32),
                   jax.ShapeDtypeStruct((N_CHIPS,) + conv_w.shape, F32)],
        scratch_shapes=[pltpu.VMEM((N_CHIPS, N_DEV, CB), F32), pltpu.SemaphoreType.DMA((13,)), pltpu.SemaphoreType.DMA((13,))],
        compiler_params=pltpu.CompilerParams(has_side_effects=True, vmem_limit_bytes=VMEM_LIMIT),
    )(c3, w_ada, b_cols, conv_w)


def gather_small(blocks):
    n = len(blocks)

    def body(*refs):
        ins, outs = refs[:n], refs[n:2 * n]
        send_sems, recv_sems = refs[2 * n:]
        x, y, c = _place()
        sib = (x, y, 1 - c)
        chips = _other_chips(x, y)

        def dev(px, py, pc):
            return 4 * px + 2 * py + pc

        def cp(w, k, src, block_dev, to):
            return pltpu.make_async_remote_copy(src_ref=src, dst_ref=outs[w].at[block_dev], send_sem=send_sems.at[7 * w + k],
                                                recv_sem=recv_sems.at[7 * w + k], device_id=to, device_id_type=MESH)

        me = dev(x, y, c)
        started = []
        for w in range(n):
            outs[w][me] = ins[w][...]
            t = cp(w, 0, ins[w], me, sib)
            t.start()
            started.append(t)
            for k, (cx, cy) in enumerate(chips):
                t = cp(w, 1 + k, ins[w], me, (cx, cy, c))
                t.start()
                started.append(t)
        for w in range(n):
            for k, (cx, cy) in enumerate(chips):
                b = dev(cx, cy, c)
                cp(w, 1 + k, outs[w].at[b], b, (x, y, c)).wait_recv()
                t = cp(w, 4 + k, outs[w].at[b], b, sib)
                t.start()
                started.append(t)
        for w in range(n):
            b = dev(x, y, 1 - c)
            cp(w, 0, outs[w].at[b], b, (x, y, c)).wait_recv()
            for k, (cx, cy) in enumerate(chips):
                b = dev(cx, cy, 1 - c)
                cp(w, 4 + k, outs[w].at[b], b, (x, y, c)).wait_recv()
        for t in started:
            t.wait_send()

    vm = pl.BlockSpec(memory_space=pltpu.VMEM)
    return pl.pallas_call(
        body,
        name="gather_small",
        in_specs=[vm] * n,
        out_specs=[vm] * n,
        out_shape=[jax.ShapeDtypeStruct((N_DEV,) + b.shape, b.dtype) for b in blocks],
        scratch_shapes=[pltpu.SemaphoreType.DMA((7 * n,)), pltpu.SemaphoreType.DMA((7 * n,))],
        compiler_params=pltpu.CompilerParams(has_side_effects=True, vmem_limit_bytes=VMEM_LIMIT),
    )(*blocks)


def reduce_scatter_grads(grads, chunk_rows):
    n = len(grads)
    shapes = [g.shape[1:] for g in grads]
    halves = [s[0] // 2 for s in shapes]

    def body(*refs):
        gin = refs[:n]
        gout = refs[n:2 * n]
        sibbuf = refs[2 * n:3 * n]
        rest = refs[3 * n:]
        rbuf = rest[:n]
        pown = rest[n:2 * n]
        stage_a, stage_b, stage_o, stage_f = rest[2 * n:2 * n + 4]
        sib_send, sib_recv, ici_send, ici_recv, fin_send, fin_recv, ld_sems, st_sems = rest[2 * n + 4:]
        x, y, c = _place()
        me = _chip_id(x, y)
        chips = _other_chips(x, y)
        sib = (x, y, 1 - c)

        to_sib = []
        for w in range(n):
            rh = halves[w]
            cp = pltpu.make_async_remote_copy(src_ref=gin[w].at[:, pl.ds((1 - c) * rh, rh), :], dst_ref=sibbuf[w],
                                              send_sem=sib_send.at[w], recv_sem=sib_recv.at[w], device_id=sib,
                                              device_id_type=MESH)
            cp.start()
            to_sib.append(cp)

        sent = []
        for w in range(n):
            rh, cw = halves[w], shapes[w][1]
            ch = chunk_rows[w]
            to_sib[w].wait_recv()
            for k in range(4):
                chip = me if k == 3 else _chip_id(*chips[k])
                for r0 in range(0, rh, ch):
                    la = pltpu.make_async_copy(gin[w].at[chip, pl.ds(c * rh + r0, ch), :], stage_a.at[0:ch, 0:cw], ld_sems.at[0])
                    lb = pltpu.make_async_copy(sibbuf[w].at[chip, pl.ds(r0, ch), :], stage_b.at[0:ch, 0:cw], ld_sems.at[1])
                    la.start()
                    lb.start()
                    la.wait()
                    lb.wait()
                    tot = stage_a[0:ch, 0:cw] + stage_b[0:ch, 0:cw]
                    if k == 3:
                        pown[w][r0:r0 + ch, :] = tot
                    else:
                        stage_o[0:ch, 0:cw] = tot.astype(BF16)
                        cx, cy = chips[k]
                        cp = pltpu.make_async_remote_copy(src_ref=stage_o.at[0:ch, 0:cw], dst_ref=rbuf[w].at[k, r0:r0 + ch, :],
                                                          send_sem=ici_send.at[3 * w + k], recv_sem=ici_recv.at[3 * w + k],
                                                          device_id=(cx, cy, c), device_id_type=MESH)
                        cp.start()
                        cp.wait_send()
            sent.append(w)

        fin = []
        for w in range(n):
            rh, cw = halves[w], shapes[w][1]
            for k in range(3):
                whole = rbuf[w].at[k]
                pltpu.make_async_remote_copy(src_ref=whole, dst_ref=whole, send_sem=ici_send.at[3 * w + k],
                                             recv_sem=ici_recv.at[3 * w + k], device_id=(x, y, c),
                                             device_id_type=MESH).wait_recv()
            pown[w][...] = ((pown[w][...] + rbuf[w][0].astype(F32)) + rbuf[w][1].astype(F32)) + rbuf[w][2].astype(F32)
            mine = gout[w].at[pl.ds(c * rh, rh), :]
            st = pltpu.make_async_copy(pown[w], mine, st_sems.at[w])
            st.start()
            cp = pltpu.make_async_remote_copy(src_ref=pown[w], dst_ref=mine, send_sem=fin_send.at[w], recv_sem=fin_recv.at[w],
                                              device_id=sib, device_id_type=MESH)
            cp.start()
            fin.append((st, cp))
        for w in range(n):
            rh = halves[w]
            theirs = gout[w].at[pl.ds((1 - c) * rh, rh), :]
            pltpu.make_async_remote_copy(src_ref=theirs, dst_ref=theirs, send_sem=fin_send.at[w], recv_sem=fin_recv.at[w],
                                         device_id=(x, y, c), device_id_type=MESH).wait_recv()
        for cp in to_sib:
            cp.wait_send()
        for st, cp in fin:
            st.wait()
            cp.wait_send()

    hbm = pl.BlockSpec(memory_space=pl.ANY)
    max_ch = max(chunk_rows)
    max_c = max(s[1] for s in shapes)
    outs = pl.pallas_call(
        body,
        name="reduce_scatter_grads",
        in_specs=[hbm] * n,
        out_specs=[hbm] * (2 * n),
        out_shape=[jax.ShapeDtypeStruct(s, F32) for s in shapes]
        + [jax.ShapeDtypeStruct((N_CHIPS, h, s[1]), F32) for h, s in zip(halves, shapes)],
        scratch_shapes=[pltpu.VMEM((3, h, s[1]), BF16) for h, s in zip(halves, shapes)]
        + [pltpu.VMEM((h, s[1]), F32) for h, s in zip(halves, shapes)]
        + [pltpu.VMEM((max_ch, max_c), F32), pltpu.VMEM((max_ch, max_c), F32), pltpu.VMEM((max_ch, max_c), BF16),
           pltpu.VMEM((8, 128), F32)]
        + [pltpu.SemaphoreType.DMA((n,)), pltpu.SemaphoreType.DMA((n,)), pltpu.SemaphoreType.DMA((3 * n,)),
           pltpu.SemaphoreType.DMA((3 * n,)), pltpu.SemaphoreType.DMA((n,)), pltpu.SemaphoreType.DMA((n,)),
           pltpu.SemaphoreType.DMA((2,)), pltpu.SemaphoreType.DMA((n,))],
        compiler_params=pltpu.CompilerParams(has_side_effects=True, vmem_limit_bytes=VMEM_LIMIT),
    )(*grads)
    return outs[:n]


def split_start(name, bufs, plan, n_sem, carry):
    nb = len(bufs)

    def body(*refs):
        x, y, c = _place()
        ssem, rsem = refs[nb + 1], refs[nb + 2]
        for i, (src, dst, dev) in enumerate(plan(refs[:nb], x, y, c)):
            pltpu.make_async_remote_copy(src_ref=src, dst_ref=dst, send_sem=ssem.at[i], recv_sem=rsem.at[i], device_id=dev,
                                         device_id_type=MESH).start()

    alls = list(bufs) + [carry]
    outs = pl.pallas_call(
        body,
        name=name,
        out_shape=[pltpu.SemaphoreType.DMA((n_sem,)), pltpu.SemaphoreType.DMA((n_sem,))] + _hbm_shapes(alls),
        in_specs=[HBM_SPEC] * (nb + 1),
        out_specs=[SEM_SPEC, SEM_SPEC] + [HBM_SPEC] * (nb + 1),
        input_output_aliases={i: 2 + i for i in range(nb + 1)},
        compiler_params=pltpu.CompilerParams(has_side_effects=EFFECT),
    )(*[_hbm(t) for t in alls])
    return outs[0], outs[1], list(outs[2:2 + nb]), outs[-1]


def split_wait(name, ssem, rsem, bufs, plan, after):
    nb = len(bufs)

    def body(*refs):
        x, y, c = _place()
        s_ref, r_ref = refs[nb], refs[nb + 1]
        for i, (src, dst, dev) in enumerate(plan(refs[:nb], x, y, c)):
            cp = pltpu.make_async_remote_copy(src_ref=src, dst_ref=dst, send_sem=s_ref.at[i], recv_sem=r_ref.at[i], device_id=dev,
                                              device_id_type=MESH)
            cp.wait_send()
            cp.wait_recv()

    outs = pl.pallas_call(
        body,
        name=name,
        out_shape=_hbm_shapes(bufs),
        in_specs=[HBM_SPEC] * nb + [SEM_SPEC, SEM_SPEC, ANY_SPEC],
        out_specs=[HBM_SPEC] * nb,
        input_output_aliases={i: i for i in range(nb)},
        compiler_params=pltpu.CompilerParams(has_side_effects=EFFECT),
    )(*bufs, ssem, rsem, after)
    return list(outs)


def _gather_ici_plan(n):
    def plan(refs, x, y, c):
        out = []
        for w in range(n):
            rh = refs[w].shape[0] // 2
            for cx, cy in _other_chips(x, y):
                out.append((refs[w].at[pl.ds(c * rh, rh), :], _half_rows(refs[n + w], _chip_id(x, y), c, rh), (cx, cy, c)))
        return out

    return plan


def _gather_d2d_plan(n):
    def plan(refs, x, y, c):
        out = []
        for w in range(n):
            rh = refs[w].shape[1] // 2
            for cx, cy in _other_chips(x, y):
                blk = _half_rows(refs[w], _chip_id(cx, cy), c, rh)
                out.append((blk, blk, (x, y, 1 - c)))
        return out

    return plan


def _dev_id(x, y, c):
    return 4 * x + 2 * y + c


def _small_ici_plan(n):
    def plan(refs, x, y, c):
        out = []
        for w in range(n):
            dst = refs[n + w].at[_dev_id(x, y, c)]
            out.append((refs[w], dst, (x, y, 1 - c)))
            for cx, cy in _other_chips(x, y):
                out.append((refs[w], dst, (cx, cy, c)))
        return out

    return plan


def _small_d2d_plan(n):
    def plan(refs, x, y, c):
        out = []
        for w in range(n):
            for cx, cy in _other_chips(x, y):
                blk = refs[w].at[_dev_id(cx, cy, c)]
                out.append((blk, blk, (x, y, 1 - c)))
        return out

    return plan


def _rs_d2d_plan(n):
    def plan(refs, x, y, c):
        out = []
        for w in range(n):
            rh = refs[w].shape[1] // 2
            out.append((refs[w].at[:, pl.ds((1 - c) * rh, rh), :], refs[n + w], (x, y, 1 - c)))
        return out

    return plan


def _rs_ici_plan(n):
    def plan(refs, x, y, c):
        out = []
        for w in range(n):
            for k, (cx, cy) in enumerate(_other_chips(x, y)):
                out.append((refs[w].at[_chip_id(cx, cy)], refs[n + w].at[k], (cx, cy, c)))
        return out

    return plan


def _rs_share_plan(n):
    def plan(refs, x, y, c):
        out = []
        for w in range(n):
            rh = refs[w].shape[0] // 2
            rows = refs[w].at[pl.ds(c * rh, rh), :]
            out.append((rows, rows, (x, y, 1 - c)))
        return out

    return plan


def rs_add(grad, sibbuf, place, tr, name):
    _, R, C = grad.shape
    nt = (R // 2) // tr

    def body(p_ref, g_ref, s_ref, o_ref):
        o_ref[...] = (g_ref[...] + s_ref[...]).astype(BF16)

    return pl.pallas_call(
        body,
        name=name,
        grid_spec=pltpu.PrefetchScalarGridSpec(
            num_scalar_prefetch=1,
            grid=(N_CHIPS, nt),
            in_specs=[pl.BlockSpec((None, tr, C), lambda j, i, p: (j, p[0] * nt + i, 0)),
                      pl.BlockSpec((None, tr, C), lambda j, i, p: (j, i, 0))],
            out_specs=pl.BlockSpec((None, tr, C), lambda j, i, p: (j, i, 0)),
        ),
        out_shape=jax.ShapeDtypeStruct((N_CHIPS, R // 2, C), BF16),
        compiler_params=_params(2),
    )(place, grad, sibbuf)


def rs_final(grad, sibbuf, rbuf, place, tr, name):
    _, R, C = grad.shape
    nt = (R // 2) // tr

    def body(p_ref, g_ref, s_ref, r_ref, o_ref):
        o_ref[...] = (((g_ref[...] + s_ref[...]) + r_ref[0].astype(F32)) + r_ref[1].astype(F32)) + r_ref[2].astype(F32)

    return pl.pallas_call(
        body,
        name=name,
        grid_spec=pltpu.PrefetchScalarGridSpec(
            num_scalar_prefetch=1,
            grid=(nt,),
            in_specs=[pl.BlockSpec((None, tr, C), lambda i, p: (p[1], p[0] * nt + i, 0)),
                      pl.BlockSpec((None, tr, C), lambda i, p: (p[1], i, 0)),
                      pl.BlockSpec((3, tr, C), lambda i, p: (0, i, 0))],
            out_specs=pl.BlockSpec((tr, C), lambda i, p: (p[0] * nt + i, 0)),
        ),
        out_shape=jax.ShapeDtypeStruct((R, C), F32),
        compiler_params=_params(1),
    )(place, grad, sibbuf, rbuf)


class GradReduce:
    def __init__(self, tag, grads, rows, place):
        self.tag, self.grads, self.rows, self.place = tag, grads, rows, place
        self.n = len(grads)

    def d2d_start(self, carry):
        sib = [lax.empty((N_CHIPS, g.shape[1] // 2, g.shape[2]), F32) for g in self.grads]
        self.s1, self.r1, bufs, carry = split_start(f"rs_{self.tag}_d2d_start", self.grads + sib, _rs_d2d_plan(self.n), self.n, carry)
        self.bufs1 = bufs
        return carry

    def add_and_ici_start(self, after, carry):
        bufs = split_wait(f"rs_{self.tag}_d2d_wait", self.s1, self.r1, self.bufs1, _rs_d2d_plan(self.n), after)
        self.grads, self.sib = bufs[:self.n], bufs[self.n:]
        pb = [rs_add(g, s, self.place, tr, f"rs_{self.tag}_add{w}")
              for w, (g, s, tr) in enumerate(zip(self.grads, self.sib, self.rows))]
        rb = [lax.empty((3,) + p.shape[1:], BF16) for p in pb]
        self.s2, self.r2, self.bufs2, carry = split_start(f"rs_{self.tag}_ici_start", pb + rb, _rs_ici_plan(self.n), 3 * self.n, carry)
        return carry

    def final_and_share_start(self, after, carry):
        bufs = split_wait(f"rs_{self.tag}_ici_wait", self.s2, self.r2, self.bufs2, _rs_ici_plan(self.n), after)
        rb = bufs[self.n:]
        full = [rs_final(g, s, r, self.place, tr, f"rs_{self.tag}_final{w}")
                for w, (g, s, r, tr) in enumerate(zip(self.grads, self.sib, rb, self.rows))]
        self.s3, self.r3, self.bufs3, carry = split_start(f"rs_{self.tag}_share_start", full, _rs_share_plan(self.n), self.n, carry)
        return carry

    def finish(self, after):
        return split_wait(f"rs_{self.tag}_share_wait", self.s3, self.r3, self.bufs3, _rs_share_plan(self.n), after)


def _rope_tables(positions):
    inv_freq = ROPE_THETA ** (-jnp.arange(0, ROT_DIM, 2, dtype=F32) / ROT_DIM)
    ang = positions.astype(F32)[:, None] * inv_freq
    cos, sin = jnp.cos(ang), jnp.sin(ang)
    S = positions.shape[0]
    one, zero = jnp.ones((S, 48), F32), jnp.zeros((S, 48), F32)
    z8 = jnp.zeros((S, 8), F32)
    tc = jnp.concatenate([cos, cos, one], axis=1)
    tsa = jnp.concatenate([z8, sin, zero], axis=1)
    tsb = jnp.concatenate([-sin, z8, zero], axis=1)
    return tuple(jnp.tile(t, (1, 2)) for t in (tc, tsa, tsb))


def _block_diag(w_pool):
    wbd = jnp.zeros((POOL_W, POOL_W), F32)
    for gi in range(4):
        wbd = wbd.at[gi * 64:(gi + 1) * 64, gi * 64:(gi + 1) * 64].set(w_pool[gi])
    return wbd


def kernel(x, c, positions, w_ada, b_ada, g_pre_mix, g_post_mix, g_pre_ffn, g_post_ffn, w_in, w_pool, b_pool, pool_scale, w_out, w_up, conv_w, conv_b, w_down, loss_target, m_w_ada, m_b_ada, m_g_pre_mix, m_g_post_mix, m_g_pre_ffn, m_g_post_ffn, m_w_in, m_w_pool, m_b_pool, m_pool_scale, m_w_out, m_w_up, m_conv_w, m_conv_b, m_w_down, v_w_ada, v_b_ada, v_g_pre_mix, v_g_post_mix, v_g_pre_ffn, v_g_post_ffn, v_w_in, v_w_pool, v_b_pool, v_pool_scale, v_w_out, v_w_up, v_conv_w, v_conv_b, v_w_down):
    xi, yi, ci = lax.axis_index("x"), lax.axis_index("y"), lax.axis_index("c")
    chip = 2 * xi + yi
    place = jnp.stack([ci, chip]).astype(jnp.int32)
    x2, tgt = x[0], loss_target[0]
    S = x2.shape[0]

    def landing(s_):
        return lax.dynamic_update_slice(lax.empty((N_CHIPS,) + s_.shape, s_.dtype), s_[None], (chip, 0, 0))

    cb_ada = w_ada.shape[2]
    b_cols = lax.dynamic_slice(b_ada, (0, chip * cb_ada), (1, cb_ada))
    c_all, mod6, conv_w_g = ada_mod(c.reshape(1, 1, D_MODEL), w_ada[0], b_cols, conv_w[0])
    conv_w_f = jnp.transpose(conv_w_g, (1, 0, 2)).reshape(3, D_FF)
    mix_sh = [w_in[0].astype(BF16), w_out[0].astype(BF16)]
    ffn_sh = [w_up[0].astype(BF16), w_down[0].astype(BF16)]
    ga_s, ga_r, ga_bufs, mod6 = split_start("gather_mix_ici_start", mix_sh + [landing(t) for t in mix_sh], _gather_ici_plan(2), 6, mod6)
    gb_s, gb_r, gb_bufs, mod6 = split_start("gather_ffn_ici_start", ffn_sh + [landing(t) for t in ffn_sh], _gather_ici_plan(2), 6, mod6)
    tc, tsa, tsb = _rope_tables(positions[0])
    wbd = _block_diag(w_pool[0]).astype(BF16)
    b_pool2, scale2 = b_pool.reshape(1, POOL_W), pool_scale
    ga_bufs = split_wait("gather_mix_ici_wait", ga_s, ga_r, ga_bufs, _gather_ici_plan(2), mod6)
    gc_s, gc_r, mix_land, mod6 = split_start("gather_mix_d2d_start", ga_bufs[2:], _gather_d2d_plan(2), 6, mod6)
    w_in_g, w_out_g = split_wait("gather_mix_d2d_wait", gc_s, gc_r, mix_land, _gather_d2d_plan(2), mod6)

    h1, u, *qkv = inproj_fwd(x2, g_pre_mix, mod6, w_in_g, tc, tsa, tsb)
    mixed, pool = pool_fwd(u, wbd, b_pool2, scale2)
    o_l = [attn_fwd(t, d) for t, d in zip(qkv, DILATIONS)]
    attn_done = sum(l[0, :8, :128] for _, l in o_l)
    gb_bufs = split_wait("gather_ffn_ici_wait", gb_s, gb_r, gb_bufs, _gather_ici_plan(2), attn_done)
    gd_s, gd_r, ffn_land, pool = split_start("gather_ffn_d2d_start", gb_bufs[2:], _gather_d2d_plan(2), 6, pool)
    cat, lse, lse4, lse16, y1, x1, h2 = outproj_fwd([o for o, _ in o_l] + [l for _, l in o_l], pool, x2, w_out_g, g_post_mix,
                                                    g_pre_ffn, mod6)
    lses = [lse[None], lse4, lse16]
    w_up_g, w_down_g = split_wait("gather_ffn_d2d_wait", gd_s, gd_r, ffn_land, _gather_d2d_plan(2), h2)
    w_down_f = w_down_g.reshape(D_FF, D_MODEL)
    gate, val, a, dy2, dout, loss_v, d_gt_f, d_g_post_ffn = ffn_fwd(h2, w_up_g, conv_w_f, conv_b, w_down_f, x1, tgt, g_post_ffn, mod6)

    dgc, dval, d_conv_w, d_conv_b, dw_down, dw_up = down_bwd(dy2, w_down_f, gate, val, conv_w_f, conv_b, a, h2)
    dx1, dy1, d_sh_f, d_sc_f, d_g_pre_ffn, d_gt_m, d_g_post_mix, dw_up = up_bwd(
        dgc, dval, conv_w_f, w_up_g, x1, dout, y1, g_pre_ffn, g_post_mix, mod6, h2, dw_up)
    rs_ffn = GradReduce("ffn", [dw_up, dw_down.reshape(N_CHIPS, D_FF // N_CHIPS, D_MODEL)], [256, 176], place)
    dy1 = rs_ffn.d2d_start(dy1)
    dpool, da1, da4, da16, dl1, dl4, dl16, dw_out = outproj_bwd(dy1, w_out_g, cat)
    dpool = rs_ffn.add_and_ici_start(dw_out, dpool)
    du, d_wbd, d_b_pool, d_scale = pool_bwd(dpool, mixed, wbd, b_pool2, scale2)
    dqkv = [attn_bwd(t, da, ls, dl, d) for t, da, ls, dl, d in zip(qkv, (da1[None], da4, da16), lses, (dl1[None], dl4, dl16), DILATIONS)]
    grad_x, d_sh_m, d_sc_m, d_g_pre_mix, dw_in = inproj_bwd(dqkv, du, x2, dx1, w_in_g, g_pre_mix, mod6, tc, tsa, tsb, h1)

    z1 = jnp.zeros((1, D_MODEL), F32)
    slab_a = jnp.concatenate(
        [d_sh_m, d_sc_m, d_gt_m, d_sh_f, d_sc_f, d_gt_f, d_g_pre_mix, d_g_post_mix, d_g_pre_ffn, d_g_post_ffn,
         jnp.concatenate([d_b_pool, d_scale, loss_v, jnp.zeros((1, 384), F32)], axis=1)] + [z1] * 5, axis=0)
    slab_b = jnp.concatenate([d_conv_w, d_conv_b, jnp.zeros((4, D_FF), F32)], axis=0)
    d_wpool = jnp.concatenate([d_wbd[gi * 64:(gi + 1) * 64, gi * 64:(gi + 1) * 64] for gi in range(4)], axis=0)
    dev = _dev_id(xi, yi, ci)
    small_src = [slab_a, slab_b, d_wpool]
    small_land = [lax.dynamic_update_slice(lax.empty((N_DEV,) + t.shape, F32), t[None], (dev, 0, 0)) for t in small_src]
    tok = jnp.zeros((8, 128), F32)
    gs_s, gs_r, gs_bufs, tok = split_start("small_ici_start", small_src + small_land, _small_ici_plan(3), 12, tok)
    rs_mix = GradReduce("mix", [dw_in, dw_out], [256, 256], place)
    tok = rs_mix.d2d_start(tok)
    tok = rs_ffn.final_and_share_start(tok, tok)
    gs_bufs = split_wait("small_ici_wait", gs_s, gs_r, gs_bufs, _small_ici_plan(3), tok)
    gt_s, gt_r, small_land, tok = split_start("small_d2d_start", gs_bufs[3:], _small_d2d_plan(3), 9, tok)
    tok = rs_mix.add_and_ici_start(tok, tok)
    slab_a_g, slab_b_g, wpool_g = split_wait("small_d2d_wait", gt_s, gt_r, small_land, _small_d2d_plan(3), tok)
    cw_cols = conv_w.shape[2]
    convw_g = lax.dynamic_slice(slab_b_g, (0, 0, chip * cw_cols), (N_DEV, 3, cw_cols))
    dmod_cols = lax.dynamic_slice(slab_a_g[:, :6, :].reshape(N_DEV, 6 * D_MODEL), (0, chip * cb_ada), (N_DEV, cb_ada))

    res = {}

    def big_adamw(name, w, g, m, v, tr):
        g_, d_, m_, v_ = adamw_rows(w[0], g, m[0], v[0], tr, "adamw_" + name)
        res[name] = (g_[None], d_[None], m_[None], v_[None])
        return v_

    g_ada, d_ada, m_ada, v_ada = adamw_ada(c_all.reshape(N_DEV, D_MODEL).T, dmod_cols, w_ada[0], m_w_ada[0], v_w_ada[0])
    res["w_ada"] = (g_ada[None], d_ada[None], m_ada[None], v_ada[None])
    g_w_up, g_w_down = rs_ffn.finish(v_ada)
    big_adamw("w_up", w_up, g_w_up, m_w_up, v_w_up, 256)
    last = big_adamw("w_down", w_down, g_w_down, m_w_down, v_w_down, 352)
    rs_mix.final_and_share_start(last, jnp.zeros((8, 128), F32))
    g_w_in, g_w_out = rs_mix.finish(last)
    big_adamw("w_in", w_in, g_w_in, m_w_in, v_w_in, 256)
    big_adamw("w_out", w_out, g_w_out, m_w_out, v_w_out, 256)
    small, loss_sum = adamw_small(slab_a_g, slab_b_g, convw_g, wpool_g, {
        "b_ada": (b_ada, m_b_ada, v_b_ada), "g_pre_mix": (g_pre_mix, m_g_pre_mix, v_g_pre_mix),
        "g_post_mix": (g_post_mix, m_g_post_mix, v_g_post_mix), "g_pre_ffn": (g_pre_ffn, m_g_pre_ffn, v_g_pre_ffn),
        "g_post_ffn": (g_post_ffn, m_g_post_ffn, v_g_post_ffn), "b_pool": (b_pool, m_b_pool, v_b_pool),
        "pool_scale": (pool_scale, m_pool_scale, v_pool_scale), "conv_b": (conv_b, m_conv_b, v_conv_b),
        "conv_w": (conv_w[0], m_conv_w[0], v_conv_w[0]), "w_pool": (w_pool, m_w_pool, v_w_pool)})
    for name in ("b_ada", "g_pre_mix", "g_post_mix", "g_pre_ffn", "g_post_ffn", "pool_scale", "conv_b", "b_pool", "w_pool"):
        res[name] = tuple(small[name])
    res["conv_w"] = tuple(t[None] for t in small["conv_w"])

    loss = loss_sum[0, 0]
    order = ["w_ada", "b_ada", "g_pre_mix", "g_post_mix", "g_pre_ffn", "g_post_ffn", "w_in", "w_pool", "b_pool", "pool_scale",
             "w_out", "w_up", "conv_w", "conv_b", "w_down"]
    outs = [loss, grad_x[None]]
    for k in range(4):
        outs += [res[n][k] for n in order]
    return tuple(outs)
```

```python
import functools
import math

import jax
import jax.numpy as jnp
from jax import lax
from jax.experimental import pallas as pl
from jax.experimental.pallas import tpu as pltpu

F32 = jnp.float32
BF16 = jnp.bfloat16
MESH = pl.DeviceIdType.MESH

D_MODEL = 1024
HEAD_DIM = 64
POOL_W = 256
GROUP_W = 256
DILATIONS = (1, 4, 16)
ATT_BLOCK = 128
IN_W = 2560
D_FF = 2816
HALF_FF = 1408
ROT_DIM = 16
ROPE_THETA = 500000.0
NORM_EPS = 1e-6
N_CHIPS = 4
N_DEV = 8
NEG = -1e30

ADAM_LR = 0.001
ADAM_B1 = 0.9
ADAM_B2 = 0.999
ADAM_EPS = 1e-08
ADAM_WD = 0.01
ADAM_STEP = 10

VMEM_LIMIT = 56 * 1024 * 1024

NT = (((1,), (1,)), ((), ()))
TN = (((0,), (0,)), ((), ()))


def _params(n_grid=0, **kw):
    sem = ("arbitrary",) * n_grid if n_grid else None
    return pltpu.CompilerParams(dimension_semantics=sem, vmem_limit_bytes=VMEM_LIMIT, **kw)


def _full(shape):
    nd = len(shape)
    return pl.BlockSpec(tuple(shape), lambda *_: (0,) * nd, pipeline_mode=pl.Buffered(1))


def _rows(tm, ncol):
    return pl.BlockSpec((tm, ncol), lambda i: (i, 0))


def _acc(ref, val):
    @pl.when(pl.program_id(0) == 0)
    def _():
        ref[...] = jnp.zeros_like(ref)

    ref[...] += val


def _colsum(v):
    return jnp.sum(v, axis=0, keepdims=True)


def _rope128(t, cs, sa, sb, sign):
    return t * cs + sign * (pltpu.roll(t, 8, 1) * sa + pltpu.roll(t, 120, 1) * sb)


FF_CHUNKS = tuple((ch, off, w) for ch in range(2) for off, w in ((0, 512), (512, 512), (1024, 384)))
GELU_C0 = math.sqrt(2.0 / math.pi)
GELU_C1 = GELU_C0 * 0.044715


def _gelu(z):
    z2 = z * z
    t = jnp.tanh(z * (GELU_C0 + GELU_C1 * z2))
    u = 0.5 * t + 0.5
    return z * u, u, t, z2


def _gelu_grad(z, u, t, z2):
    return u + (z * (GELU_C0 + (3.0 * GELU_C1) * z2)) * (0.5 - 0.5 * (t * t))


def _conv_taps(gate, halo, first):
    row = lax.broadcasted_iota(jnp.int32, gate.shape, 0)
    halo = jnp.where(first, 0.0, halo)
    nh = halo.shape[0]
    p1 = halo[nh - 1:nh, :]
    p2 = halo[nh - 2:nh - 1, :]
    g1 = jnp.where(row == 0, p1, pltpu.roll(gate, 1, 0))
    g2 = jnp.where(row == 0, p2, jnp.where(row == 1, p1, pltpu.roll(gate, 2, 0)))
    return g1, g2


def inproj_fwd(x, g, mod6, w_in_g, tc, tsa, tsb, tm=512):
    S = x.shape[0]

    def body(x_ref, g_ref, mod_ref, w_ref, tc_ref, tsa_ref, tsb_ref, h_ref, u_ref, q1_ref, q4_ref, q16_ref, scr):
        qkv_refs = (q1_ref, q4_ref, q16_ref)
        xv = x_ref[...]
        rstd = lax.rsqrt(jnp.mean(xv * xv, axis=-1, keepdims=True) + NORM_EPS)
        h = ((xv * rstd) * g_ref[...]) * (1.0 + mod_ref[1:2, :]) + mod_ref[0:1, :]
        hb = h.astype(BF16)
        h_ref[...] = hb
        cs, sa, sb = tc_ref[...], tsa_ref[...], tsb_ref[...]
        for j in range(N_CHIPS):
            res = jnp.dot(hb, w_ref[j], preferred_element_type=F32)
            for t in range(5):
                sp = 5 * j + t
                piece, half = sp // 2, sp % 2
                blk = res[:, t * 128:(t + 1) * 128]
                lanes = slice(half * 128, (half + 1) * 128)
                if piece == 0:
                    u_ref[:, lanes] = blk
                else:
                    kind, gi = (piece - 1) // 3, (piece - 1) % 3
                    if kind == 0:
                        blk = _rope128(blk, cs, sa, sb, 1.0) * (HEAD_DIM ** -0.5)
                    elif kind == 1:
                        blk = _rope128(blk, cs, sa, sb, 1.0)
                    d = DILATIONS[gi]
                    if d == 1:
                        q1_ref[kind, 0, :, lanes] = blk.astype(BF16)
                    else:
                        scr[...] = blk
                        for r in range(d):
                            qkv_refs[gi][kind, r, :, lanes] = scr[pl.ds(r, tm // d, stride=d), :].astype(BF16)

    cls = lambda d: pl.BlockSpec((3, d, tm // d, GROUP_W), lambda i: (0, 0, i, 0))
    return pl.pallas_call(
        body,
        name="inproj_fwd",
        grid=(S // tm,),
        in_specs=[_rows(tm, D_MODEL), _full((1, D_MODEL)), _full((6, D_MODEL)), _full(w_in_g.shape),
                  _rows(tm, 128), _rows(tm, 128), _rows(tm, 128)],
        out_specs=[_rows(tm, D_MODEL), _rows(tm, POOL_W)] + [cls(d) for d in DILATIONS],
        out_shape=[jax.ShapeDtypeStruct((S, D_MODEL), BF16), jax.ShapeDtypeStruct((S, POOL_W), F32)]
        + [jax.ShapeDtypeStruct((3, d, S // d, GROUP_W), BF16) for d in DILATIONS],
        scratch_shapes=[pltpu.VMEM((tm, 128), F32)],
        compiler_params=_params(1),
    )(x, g, mod6, w_in_g, tc, tsa, tsb)


def _attn_masks():
    row = lax.broadcasted_iota(jnp.int32, (2 * ATT_BLOCK, 2 * ATT_BLOCK), 0) % ATT_BLOCK
    col = lax.broadcasted_iota(jnp.int32, (2 * ATT_BLOCK, 2 * ATT_BLOCK), 1)
    band = (col >= row) & (col <= row + ATT_BLOCK)
    lane = lax.broadcasted_iota(jnp.int32, (ATT_BLOCK, 128), 1)
    return band, col, lane < HEAD_DIM


def _stack_heads(t, lo):
    z = jnp.zeros_like(t)
    return jnp.concatenate([jnp.where(lo, t, z), jnp.where(lo, z, t)], axis=0)


def _unstack_heads(t2, lo):
    return jnp.where(lo, t2[:ATT_BLOCK], t2[ATT_BLOCK:])


def attn_fwd(qkv, d):
    L = qkv.shape[2]
    nb = L // ATT_BLOCK

    def body(q_ref, k_ref, v_ref, o_ref, l_ref, kpad, vpad):
        kpad[0:ATT_BLOCK, :] = jnp.zeros((ATT_BLOCK, GROUP_W), BF16)
        vpad[0:ATT_BLOCK, :] = jnp.zeros((ATT_BLOCK, GROUP_W), BF16)
        kpad[ATT_BLOCK:, :] = k_ref[...]
        vpad[ATT_BLOCK:, :] = v_ref[...]
        band, col, lo = _attn_masks()

        def step(n, carry):
            r0 = pl.multiple_of(n * ATT_BLOCK, ATT_BLOCK)
            valid = band & ((col >= ATT_BLOCK) | (n > 0))
            qb = q_ref[pl.ds(r0, ATT_BLOCK), :]
            kb = kpad[pl.ds(r0, 2 * ATT_BLOCK), :]
            vb = vpad[pl.ds(r0, 2 * ATT_BLOCK), :]
            for pair in range(2):
                lanes = slice(pair * 128, (pair + 1) * 128)
                qp, kp, vp = qb[:, lanes], kb[:, lanes], vb[:, lanes]
                s = lax.dot_general(_stack_heads(qp, lo), kp, NT, preferred_element_type=F32)
                s = jnp.where(valid, s, NEG)
                m = jnp.max(s, axis=1, keepdims=True)
                p = jnp.exp(s - m)
                den = jnp.sum(p, axis=1, keepdims=True)
                pv = jnp.dot(p.astype(BF16), vp, preferred_element_type=F32)
                o_ref[pl.ds(r0, ATT_BLOCK), lanes] = _unstack_heads(pv / den, lo)
                l_ref[pl.ds(r0, ATT_BLOCK), lanes] = _unstack_heads(jnp.broadcast_to(m + jnp.log(den), pv.shape), lo)
            return carry

        lax.fori_loop(0, nb, step, 0, unroll=min(4, nb))

    spec = lambda kind: pl.BlockSpec((None, None, L, GROUP_W), lambda r: (kind, r, 0, 0))
    return pl.pallas_call(
        body,
        name=f"attn_fwd_d{d}",
        grid=(d,),
        in_specs=[spec(0), spec(1), spec(2)],
        out_specs=[pl.BlockSpec((None, L, GROUP_W), lambda r: (r, 0, 0))] * 2,
        out_shape=[jax.ShapeDtypeStruct((d, L, GROUP_W), F32)] * 2,
        scratch_shapes=[pltpu.VMEM((L + ATT_BLOCK, GROUP_W), BF16)] * 2,
        compiler_params=_params(1),
    )(qkv, qkv, qkv)


def _pool_lane_windows(shape):
    lane = lax.broadcasted_iota(jnp.int32, shape, 1)
    return lane, jnp.where(lane < 64, 2, jnp.where(lane < 128, 4, jnp.where(lane < 192, 8, 16)))


def pool_fwd(u, wbd, b, scale):
    S = u.shape[0]

    def body(u_ref, w_ref, b_ref, s_ref, mixed_ref, out_ref):
        uv = u_ref[...]
        row = lax.broadcasted_iota(jnp.int32, uv.shape, 0)
        lane, win = _pool_lane_windows(uv.shape)

        def shift(a, k):
            return jnp.where(row >= k, pltpu.roll(a, k, 0), 0.0)

        s2 = uv + shift(uv, 1)
        s4 = s2 + shift(s2, 2)
        s8 = s4 + shift(s4, 4)
        s16 = s8 + shift(s8, 8)
        tsum = jnp.where(lane < 64, s2, jnp.where(lane < 128, s4, jnp.where(lane < 192, s8, s16)))
        cnt = jnp.minimum(row + 1, win).astype(F32)
        mb = (tsum / cnt - uv).astype(BF16)
        mixed_ref[...] = mb
        y = jnp.dot(mb, w_ref[...], preferred_element_type=F32) + b_ref[...]
        out_ref[...] = (y * s_ref[...]).astype(BF16)

    vm = pl.BlockSpec(memory_space=pltpu.VMEM)
    return pl.pallas_call(
        body,
        name="pool_fwd",
        in_specs=[vm] * 4,
        out_specs=[vm] * 2,
        out_shape=[jax.ShapeDtypeStruct((S, POOL_W), BF16)] * 2,
        compiler_params=_params(),
    )(u, wbd, b, scale)


def outproj_fwd(o_l, pool, x, w_out_g, g_post, g_pre, mod6, tm=512):
    S = x.shape[0]

    def body(o0, o1, o2, l0, l1, l2, pool_ref, x_ref, w_ref, gpost_ref, gpre_ref, mod_ref,
             cat_ref, lse_ref, lse4_ref, lse16_ref, y1_ref, x1_ref, h2_ref, so4, sl4, so16, sl16):
        for d, src, dst in ((4, o1, so4), (4, l1, sl4), (16, o2, so16), (16, l2, sl16)):
            for r in range(d):
                for h in range(2):
                    dst[h, pl.ds(r, tm // d, stride=d), :] = src[r, :, h * 128:(h + 1) * 128]
        nat = lambda ref: jnp.concatenate([ref[0], ref[1]], axis=1)
        a, b, c = l0[0], nat(sl4), nat(sl16)
        m = jnp.maximum(jnp.maximum(a, b), c)
        e0, e1, e2 = jnp.exp(a - m), jnp.exp(b - m), jnp.exp(c - m)
        z = e0 + e1 + e2
        lse = m + jnp.log(z)
        lse_ref[...] = lse
        for h in range(2):
            sl4[h] = lse[:, h * 128:(h + 1) * 128]
        for d, dst in ((4, lse4_ref), (16, lse16_ref)):
            for r in range(d):
                for h in range(2):
                    dst[r, :, h * 128:(h + 1) * 128] = sl4[h, pl.ds(r, tm // d, stride=d), :]
        attn = (e0 * o0[0] + e1 * nat(so4) + e2 * nat(so16)) / z
        cat = jnp.concatenate([pool_ref[...], attn.astype(BF16)], axis=1)
        cat_ref[...] = cat
        y1 = jnp.concatenate([jnp.dot(cat, w_ref[j], preferred_element_type=F32) for j in range(N_CHIPS)], axis=1)
        y1_ref[...] = y1
        rstd = lax.rsqrt(jnp.mean(y1 * y1, axis=-1, keepdims=True) + NORM_EPS)
        x1 = x_ref[...] + mod_ref[2:3, :] * ((y1 * rstd) * gpost_ref[...])
        x1_ref[...] = x1
        rstd2 = lax.rsqrt(jnp.mean(x1 * x1, axis=-1, keepdims=True) + NORM_EPS)
        h2 = ((x1 * rstd2) * gpre_ref[...]) * (1.0 + mod_ref[4:5, :]) + mod_ref[3:4, :]
        h2_ref[...] = h2.astype(BF16)

    t256 = _rows(tm, GROUP_W)
    cls = lambda d: pl.BlockSpec((d, tm // d, GROUP_W), lambda i: (0, i, 0))
    cls_shape = lambda d: jax.ShapeDtypeStruct((d, S // d, GROUP_W), F32)
    return pl.pallas_call(
        body,
        name="outproj_fwd",
        grid=(S // tm,),
        in_specs=[cls(d) for d in DILATIONS] * 2 + [t256, _rows(tm, D_MODEL), _full(w_out_g.shape), _full((1, D_MODEL)),
                                                    _full((1, D_MODEL)), _full((6, D_MODEL))],
        out_specs=[_rows(tm, 512), t256, cls(4), cls(16), _rows(tm, D_MODEL), _rows(tm, D_MODEL), _rows(tm, D_MODEL)],
        out_shape=[jax.ShapeDtypeStruct((S, 512), BF16), jax.ShapeDtypeStruct((S, GROUP_W), F32), cls_shape(4), cls_shape(16),
                   jax.ShapeDtypeStruct((S, D_MODEL), F32), jax.ShapeDtypeStruct((S, D_MODEL), F32),
                   jax.ShapeDtypeStruct((S, D_MODEL), BF16)],
        scratch_shapes=[pltpu.VMEM((2, tm, 128), F32)] * 4,
        compiler_params=_params(1),
    )(*o_l, pool, x, w_out_g, g_post, g_pre, mod6)


def up_fwd(h2, w_up_g, tm=512):
    S = h2.shape[0]

    def body(h_ref, w_ref, gate_ref, val_ref):
        hb = h_ref[...]
        for j in range(N_CHIPS):
            res = jnp.dot(hb, w_ref[j], preferred_element_type=F32).astype(BF16)
            dst = gate_ref if j < 2 else val_ref
            dst[:, (j % 2) * HALF_FF:(j % 2 + 1) * HALF_FF] = res

    return pl.pallas_call(
        body,
        name="up_fwd",
        grid=(S // tm,),
        in_specs=[_rows(tm, D_MODEL), _full(w_up_g.shape)],
        out_specs=[_rows(tm, D_FF)] * 2,
        out_shape=[jax.ShapeDtypeStruct((S, D_FF), BF16)] * 2,
        compiler_params=_params(1),
    )(h2, w_up_g)


def _halo_prev(tm, ncol):
    return pl.BlockSpec((16, ncol), lambda i: (jnp.maximum(i * (tm // 16) - 1, 0), 0))


def down_fwd(gate, val, conv_w, conv_b, w_down, x1, target, g_post, mod6, tm=256):
    S = x1.shape[0]

    def body(gate_ref, halo_ref, val_ref, cw_ref, cb_ref, w_ref, x1_ref, tgt_ref, g_ref, mod_ref,
             a_ref, dy2_ref, dout_ref, loss_ref, dgt_ref, dg_ref):
        first = pl.program_id(0) == 0
        y2 = jnp.zeros((tm, D_MODEL), F32)
        for ch in range(2):
            cols = slice(ch * HALF_FF, (ch + 1) * HALF_FF)
            gt = gate_ref[:, cols].astype(F32)
            g1, g2 = _conv_taps(gt, halo_ref[:, cols].astype(F32), first)
            gc = g2 * cw_ref[0:1, cols] + g1 * cw_ref[1:2, cols] + gt * cw_ref[2:3, cols] + cb_ref[:, cols]
            ge = _gelu(gc)[0]
            ab = (ge * val_ref[:, cols].astype(F32)).astype(BF16)
            a_ref[:, cols] = ab
            y2 = y2 + jnp.dot(ab, w_ref[cols, :], preferred_element_type=F32)
        rstd = lax.rsqrt(jnp.mean(y2 * y2, axis=-1, keepdims=True) + NORM_EPS)
        y2n = y2 * rstd
        gv = g_ref[...]
        gtf = mod_ref[5:6, :]
        r2 = y2n * gv
        diff = (x1_ref[...] + gtf * r2) - tgt_ref[...]
        _acc(loss_ref, jnp.zeros((1, 128), F32) + 0.5 * jnp.sum(diff * diff) * (1.0 / D_MODEL))
        dout = diff * (1.0 / D_MODEL)
        dout_ref[...] = dout
        _acc(dgt_ref, _colsum(dout * r2))
        dr2 = dout * gtf
        _acc(dg_ref, _colsum(dr2 * y2n))
        dyn = dr2 * gv
        dy2 = rstd * (dyn - y2n * jnp.mean(dyn * y2n, axis=-1, keepdims=True))
        dy2_ref[...] = dy2.astype(BF16)

    vec = _full((1, D_MODEL))
    return pl.pallas_call(
        body,
        name="down_fwd",
        grid=(S // tm,),
        in_specs=[_rows(tm, D_FF), _halo_prev(tm, D_FF), _rows(tm, D_FF), _full((3, D_FF)), _full((1, D_FF)),
                  _full((D_FF, D_MODEL)), _rows(tm, D_MODEL), _rows(tm, D_MODEL), vec, _full((6, D_MODEL))],
        out_specs=[_rows(tm, D_FF), _rows(tm, D_MODEL), _rows(tm, D_MODEL), _full((1, 128)), vec, vec],
        out_shape=[jax.ShapeDtypeStruct((S, D_FF), BF16), jax.ShapeDtypeStruct((S, D_MODEL), BF16),
                   jax.ShapeDtypeStruct((S, D_MODEL), F32), jax.ShapeDtypeStruct((1, 128), F32),
                   jax.ShapeDtypeStruct((1, D_MODEL), F32), jax.ShapeDtypeStruct((1, D_MODEL), F32)],
        compiler_params=_params(1),
    )(gate, gate, val, conv_w, conv_b, w_down, x1, target, g_post, mod6)


def ffn_fwd(h2, w_up_g, conv_w, conv_b, w_down, x1, target, g_post, mod6, tm=256):
    S = x1.shape[0]

    def body(h_ref, wu_ref, cw_ref, cb_ref, wd_ref, x1_ref, tgt_ref, g_ref, mod_ref,
             gate_ref, val_ref, a_ref, dy2_ref, dout_ref, loss_ref, dgt_ref, dg_ref, carry):
        first = pl.program_id(0) == 0

        @pl.when(first)
        def _():
            carry[...] = jnp.zeros_like(carry)

        hb = h_ref[...]
        y2 = jnp.zeros((tm, D_MODEL), F32)
        for ch in range(2):
            cols = slice(ch * HALF_FF, (ch + 1) * HALF_FF)
            gb = jnp.dot(hb, wu_ref[ch], preferred_element_type=F32).astype(BF16)
            vb = jnp.dot(hb, wu_ref[2 + ch], preferred_element_type=F32).astype(BF16)
            gate_ref[:, cols] = gb
            val_ref[:, cols] = vb
            gt = gb.astype(F32)
            g1, g2 = _conv_taps(gt, carry[:, cols], first)
            carry[:, cols] = gt[tm - 8:, :]
            gc = g2 * cw_ref[0:1, cols] + g1 * cw_ref[1:2, cols] + gt * cw_ref[2:3, cols] + cb_ref[:, cols]
            ab = (_gelu(gc)[0] * vb.astype(F32)).astype(BF16)
            a_ref[:, cols] = ab
            y2 = y2 + jnp.dot(ab, wd_ref[cols, :], preferred_element_type=F32)
        rstd = lax.rsqrt(jnp.mean(y2 * y2, axis=-1, keepdims=True) + NORM_EPS)
        y2n = y2 * rstd
        gv = g_ref[...]
        gtf = mod_ref[5:6, :]
        r2 = y2n * gv
        diff = (x1_ref[...] + gtf * r2) - tgt_ref[...]
        _acc(loss_ref, jnp.zeros((1, 128), F32) + 0.5 * jnp.sum(diff * diff) * (1.0 / D_MODEL))
        dout = diff * (1.0 / D_MODEL)
        dout_ref[...] = dout
        _acc(dgt_ref, _colsum(dout * r2))
        dr2 = dout * gtf
        _acc(dg_ref, _colsum(dr2 * y2n))
        dyn = dr2 * gv
        dy2 = rstd * (dyn - y2n * jnp.mean(dyn * y2n, axis=-1, keepdims=True))
        dy2_ref[...] = dy2.astype(BF16)

    vec = _full((1, D_MODEL))
    return pl.pallas_call(
        body,
        name="ffn_fwd",
        grid=(S // tm,),
        in_specs=[_rows(tm, D_MODEL), _full(w_up_g.shape), _full((3, D_FF)), _full((1, D_FF)), _full((D_FF, D_MODEL)),
                  _rows(tm, D_MODEL), _rows(tm, D_MODEL), vec, _full((6, D_MODEL))],
        out_specs=[_rows(tm, D_FF), _rows(tm, D_FF), _rows(tm, D_FF), _rows(tm, D_MODEL), _rows(tm, D_MODEL), _full((1, 128)), vec, vec],
        out_shape=[jax.ShapeDtypeStruct((S, D_FF), BF16)] * 3 + [jax.ShapeDtypeStruct((S, D_MODEL), BF16),
                                                                 jax.ShapeDtypeStruct((S, D_MODEL), F32),
                                                                 jax.ShapeDtypeStruct((1, 128), F32),
                                                                 jax.ShapeDtypeStruct((1, D_MODEL), F32),
                                                                 jax.ShapeDtypeStruct((1, D_MODEL), F32)],
        scratch_shapes=[pltpu.VMEM((8, D_FF), F32)],
        compiler_params=_params(1),
    )(h2, w_up_g, conv_w, conv_b, w_down, x1, target, g_post, mod6)


def down_bwd(dy2, w_down, gate, val, conv_w, conv_b, a, h2, tm=256):
    S = dy2.shape[0]

    def body(dy_ref, w_ref, gate_ref, halo_ref, val_ref, cw_ref, cb_ref, a_ref, h_ref,
             dgc_ref, dval_ref, dcw_ref, dcb_ref, dwd_ref, dwu_ref):
        first = pl.program_id(0) == 0

        @pl.when(first)
        def _():
            dcw_ref[...] = jnp.zeros_like(dcw_ref)
            dcb_ref[...] = jnp.zeros_like(dcb_ref)
            dwd_ref[...] = jnp.zeros_like(dwd_ref)
            dwu_ref[...] = jnp.zeros_like(dwu_ref)

        dyb = dy_ref[...]
        hb = h_ref[...]
        def col(i):
            ch, off, width = FF_CHUNKS[i]
            return slice(ch * HALF_FF + off, ch * HALF_FF + off + width)

        def mm_da(i):
            return lax.dot_general(dyb, w_ref[col(i), :], NT, preferred_element_type=F32)

        def elementwise(i, da):
            cols = col(i)
            gt = gate_ref[:, cols].astype(F32)
            g1, g2 = _conv_taps(gt, halo_ref[:, cols].astype(F32), first)
            gc = g2 * cw_ref[0:1, cols] + g1 * cw_ref[1:2, cols] + gt * cw_ref[2:3, cols] + cb_ref[:, cols]
            ge, u, th, z2 = _gelu(gc)
            dgc = da * val_ref[:, cols].astype(F32) * _gelu_grad(gc, u, th, z2)
            dgc_ref[:, cols] = dgc.astype(BF16)
            dvb = (da * ge).astype(BF16)
            dval_ref[:, cols] = dvb
            dcb_ref[:, cols] += _colsum(dgc)
            dcw_ref[0:1, cols] += _colsum(dgc * g2)
            dcw_ref[1:2, cols] += _colsum(dgc * g1)
            dcw_ref[2:3, cols] += _colsum(dgc * gt)
            return dvb

        def mm_dw(i, dvb):
            ch, off, width = FF_CHUNKS[i]
            dwd_ref[col(i), :] += lax.dot_general(a_ref[:, col(i)], dyb, TN, preferred_element_type=F32)
            dwu_ref[ch, :, off:off + width] += lax.dot_general(hb, dvb, TN, preferred_element_type=F32)

        n = len(FF_CHUNKS)
        da = mm_da(0)
        prev = None
        for i in range(n):
            nxt = mm_da(i + 1) if i + 1 < n else None
            if prev is not None:
                mm_dw(i - 1, prev)
            prev = elementwise(i, da)
            da = nxt
        mm_dw(n - 1, prev)

    return pl.pallas_call(
        body,
        name="down_bwd",
        grid=(S // tm,),
        in_specs=[_rows(tm, D_MODEL), _full((D_FF, D_MODEL)), _rows(tm, D_FF), _halo_prev(tm, D_FF), _rows(tm, D_FF),
                  _full((3, D_FF)), _full((1, D_FF)), _rows(tm, D_FF), _rows(tm, D_MODEL)],
        out_specs=[_rows(tm, D_FF), _rows(tm, D_FF), _full((3, D_FF)), _full((1, D_FF)), _full((D_FF, D_MODEL)),
                   pl.BlockSpec((2, D_MODEL, HALF_FF), lambda i: (1, 0, 0), pipeline_mode=pl.Buffered(1))],
        out_shape=[jax.ShapeDtypeStruct((S, D_FF), BF16), jax.ShapeDtypeStruct((S, D_FF), BF16),
                   jax.ShapeDtypeStruct((3, D_FF), F32), jax.ShapeDtypeStruct((1, D_FF), F32),
                   jax.ShapeDtypeStruct((D_FF, D_MODEL), F32), jax.ShapeDtypeStruct((N_CHIPS, D_MODEL, HALF_FF), F32)],
        compiler_params=_params(1),
    )(dy2, w_down, gate, gate, val, conv_w, conv_b, a, h2)


def dw_matmul(a, b, out_blocks, blk_shape, a_cols, b_cols, a_blocked, name, prev=None, blk_off=0, n_blk=None, tm=512):
    S = a.shape[0]
    n_blk = out_blocks if n_blk is None else n_blk

    def body(*refs):
        a_ref, b_ref, o_ref = refs[0], refs[1], refs[-1]

        @pl.when(pl.program_id(1) == 0)
        def _():
            o_ref[...] = jnp.zeros_like(o_ref)

        o_ref[...] += lax.dot_general(a_ref[...], b_ref[...], TN, preferred_element_type=F32)

    a_spec = pl.BlockSpec((tm, a_cols), (lambda j, i: (i, j)) if a_blocked else (lambda j, i: (i, 0)))
    b_spec = pl.BlockSpec((tm, b_cols), (lambda j, i: (i, 0)) if a_blocked else (lambda j, i: (i, j)))
    in_specs = [a_spec, b_spec]
    args = [a, b]
    aliases = {}
    if prev is not None:
        in_specs.append(pl.BlockSpec(memory_space=pl.ANY))
        args.append(prev)
        aliases = {2: 0}
    return pl.pallas_call(
        body,
        name=name,
        grid=(n_blk, S // tm),
        in_specs=in_specs,
        out_specs=pl.BlockSpec((None,) + tuple(blk_shape), lambda j, i: (j + blk_off, 0, 0)),
        out_shape=jax.ShapeDtypeStruct((out_blocks,) + tuple(blk_shape), F32),
        input_output_aliases=aliases,
        compiler_params=_params(2),
    )(*args)


def up_bwd(dgc, dval, conv_w, w_up_g, x1, dout, y1, g_pre, g_post, mod6, h2, dw_up, tm=256):
    S = x1.shape[0]
    last_blk = S // 16 - 1

    def body(dgc_ref, nxt_ref, dval_ref, cw_ref, w_ref, x1_ref, dout_ref, y1_ref, gpre_ref, gpost_ref, mod_ref, h_ref, dwin_ref,
             dx1_ref, dy1_ref, dsh_ref, dsc_ref, dgpre_ref, dgt_ref, dgpost_ref, dwu_ref):
        last = pl.program_id(0) == pl.num_programs(0) - 1

        @pl.when(pl.program_id(0) == 0)
        def _():
            dwu_ref[...] = jnp.zeros_like(dwu_ref)

        hb = h_ref[...]
        dh = jnp.zeros((tm, D_MODEL), F32)
        for ch in range(2):
            cols = slice(ch * HALF_FF, (ch + 1) * HALF_FF)
            dg = dgc_ref[:, cols].astype(F32)
            nx = jnp.where(last, 0.0, nxt_ref[:, cols].astype(F32))
            row = lax.broadcasted_iota(jnp.int32, dg.shape, 0)
            n0, n1 = nx[0:1, :], nx[1:2, :]
            u1 = jnp.where(row == tm - 1, n0, pltpu.roll(dg, tm - 1, 0))
            u2 = jnp.where(row == tm - 1, n1, jnp.where(row == tm - 2, n0, pltpu.roll(dg, tm - 2, 0)))
            dgate = (dg * cw_ref[2:3, cols] + u1 * cw_ref[1:2, cols] + u2 * cw_ref[0:1, cols]).astype(BF16)
            dwu_ref[ch] += lax.dot_general(hb, dgate, TN, preferred_element_type=F32)
            dh = dh + lax.dot_general(dgate, w_ref[ch], NT, preferred_element_type=F32)
            dh = dh + lax.dot_general(dval_ref[:, cols], w_ref[2 + ch], NT, preferred_element_type=F32)
        x1 = x1_ref[...]
        rstd = lax.rsqrt(jnp.mean(x1 * x1, axis=-1, keepdims=True) + NORM_EPS)
        n2 = x1 * rstd
        gpre = gpre_ref[...]
        one_sc = 1.0 + mod_ref[4:5, :]
        _acc(dsh_ref, _colsum(dh))
        _acc(dsc_ref, _colsum(dh * (n2 * gpre)))
        _acc(dgpre_ref, _colsum(dh * one_sc * n2))
        dn = dh * (gpre * one_sc)
        dx1 = dout_ref[...] + rstd * (dn - n2 * jnp.mean(dn * n2, axis=-1, keepdims=True))
        dx1_ref[...] = dx1
        y1 = y1_ref[...]
        rstd1 = lax.rsqrt(jnp.mean(y1 * y1, axis=-1, keepdims=True) + NORM_EPS)
        y1n = y1 * rstd1
        gpost = gpost_ref[...]
        gtm = mod_ref[2:3, :]
        _acc(dgt_ref, _colsum(dx1 * (y1n * gpost)))
        dr1 = dx1 * gtm
        _acc(dgpost_ref, _colsum(dr1 * y1n))
        dyn = dr1 * gpost
        dy1 = rstd1 * (dyn - y1n * jnp.mean(dyn * y1n, axis=-1, keepdims=True))
        dy1_ref[...] = dy1.astype(BF16)

    vec = _full((1, D_MODEL))
    nxt = pl.BlockSpec((16, D_FF), lambda i: (jnp.minimum((i + 1) * (tm // 16), last_blk), 0))
    return pl.pallas_call(
        body,
        name="up_bwd",
        grid=(S // tm,),
        in_specs=[_rows(tm, D_FF), nxt, _rows(tm, D_FF), _full((3, D_FF)), _full(w_up_g.shape), _rows(tm, D_MODEL),
                  _rows(tm, D_MODEL), _rows(tm, D_MODEL), vec, vec, _full((6, D_MODEL)), _rows(tm, D_MODEL),
                  pl.BlockSpec(memory_space=pl.ANY)],
        out_specs=[_rows(tm, D_MODEL), _rows(tm, D_MODEL), vec, vec, vec, vec, vec,
                   pl.BlockSpec((2, D_MODEL, HALF_FF), lambda i: (0, 0, 0), pipeline_mode=pl.Buffered(1))],
        out_shape=[jax.ShapeDtypeStruct((S, D_MODEL), F32), jax.ShapeDtypeStruct((S, D_MODEL), BF16)]
        + [jax.ShapeDtypeStruct((1, D_MODEL), F32)] * 5 + [jax.ShapeDtypeStruct(dw_up.shape, F32)],
        input_output_aliases={12: 7},
        compiler_params=_params(1),
    )(dgc, dgc, dval, conv_w, w_up_g, x1, dout, y1, g_pre, g_post, mod6, h2, dw_up)


def outproj_bwd(dy1, w_out_g, cat, tm=512):
    S = dy1.shape[0]

    def body(dy_ref, w_ref, cat_ref, dpool_ref, dattn_ref, da4_ref, da16_ref, delta_ref, dl4_ref, dl16_ref, dw_ref, scr):
        @pl.when(pl.program_id(0) == 0)
        def _():
            dw_ref[...] = jnp.zeros_like(dw_ref)

        catb = cat_ref[...]
        dcat = jnp.zeros((tm, 512), F32)
        for j in range(N_CHIPS):
            dyj = dy_ref[:, j * 256:(j + 1) * 256]
            dcat = dcat + lax.dot_general(dyj, w_ref[j], NT, preferred_element_type=F32)
            dw_ref[j] += lax.dot_general(catb, dyj, TN, preferred_element_type=F32)
        dpool_ref[...] = dcat[:, :POOL_W]
        dattn = dcat[:, POOL_W:]
        dattn_ref[...] = dattn.astype(BF16)
        for h in range(2):
            scr[h] = dattn[:, h * 128:(h + 1) * 128]
        for d, dst in ((4, da4_ref), (16, da16_ref)):
            for r in range(d):
                for h in range(2):
                    dst[r, :, h * 128:(h + 1) * 128] = scr[h, pl.ds(r, tm // d, stride=d), :].astype(BF16)
        prod = dattn * catb[:, POOL_W:].astype(F32)
        r = lax.broadcasted_iota(jnp.int32, (GROUP_W, GROUP_W), 0) // HEAD_DIM
        c = lax.broadcasted_iota(jnp.int32, (GROUP_W, GROUP_W), 1) // HEAD_DIM
        ones_bd = jnp.where(r == c, 1.0, 0.0).astype(BF16)
        hi = prod.astype(BF16)
        lo = (prod - hi.astype(F32)).astype(BF16)
        delta = jnp.dot(hi, ones_bd, preferred_element_type=F32) + jnp.dot(lo, ones_bd, preferred_element_type=F32)
        delta_ref[...] = delta
        for h in range(2):
            scr[h] = delta[:, h * 128:(h + 1) * 128]
        for d, dst in ((4, dl4_ref), (16, dl16_ref)):
            for r in range(d):
                for h in range(2):
                    dst[r, :, h * 128:(h + 1) * 128] = scr[h, pl.ds(r, tm // d, stride=d), :]

    cls = lambda d: pl.BlockSpec((d, tm // d, GROUP_W), lambda i: (0, i, 0))
    cls_shape = lambda d, dt: jax.ShapeDtypeStruct((d, S // d, GROUP_W), dt)
    return pl.pallas_call(
        body,
        name="outproj_bwd",
        grid=(S // tm,),
        in_specs=[_rows(tm, D_MODEL), _full(w_out_g.shape), _rows(tm, 512)],
        out_specs=[_rows(tm, POOL_W), _rows(tm, GROUP_W), cls(4), cls(16), _rows(tm, GROUP_W), cls(4), cls(16),
                   _full(w_out_g.shape)],
        out_shape=[jax.ShapeDtypeStruct((S, POOL_W), F32), jax.ShapeDtypeStruct((S, GROUP_W), BF16), cls_shape(4, BF16),
                   cls_shape(16, BF16), jax.ShapeDtypeStruct((S, GROUP_W), F32), cls_shape(4, F32), cls_shape(16, F32),
                   jax.ShapeDtypeStruct(w_out_g.shape, F32)],
        scratch_shapes=[pltpu.VMEM((2, tm, 128), F32)],
        compiler_params=_params(1),
    )(dy1, w_out_g, cat)


def attn_bwd(qkv, dattn, lse, delta, d):
    L = qkv.shape[2]
    nb = L // ATT_BLOCK

    def body(q_ref, k_ref, v_ref, do_ref, l_ref, dl_ref, out_ref, kpad, vpad, dkpad, dvpad):
        kpad[0:ATT_BLOCK, :] = jnp.zeros((ATT_BLOCK, GROUP_W), BF16)
        vpad[0:ATT_BLOCK, :] = jnp.zeros((ATT_BLOCK, GROUP_W), BF16)
        kpad[ATT_BLOCK:, :] = k_ref[...]
        vpad[ATT_BLOCK:, :] = v_ref[...]
        dkpad[...] = jnp.zeros_like(dkpad)
        dvpad[...] = jnp.zeros_like(dvpad)
        band, col, lo = _attn_masks()

        def step(n, carry):
            r0 = pl.multiple_of(n * ATT_BLOCK, ATT_BLOCK)
            valid = band & ((col >= ATT_BLOCK) | (n > 0))
            qb = q_ref[pl.ds(r0, ATT_BLOCK), :]
            dob = do_ref[pl.ds(r0, ATT_BLOCK), :]
            lb = l_ref[pl.ds(r0, ATT_BLOCK), :]
            dlb = dl_ref[pl.ds(r0, ATT_BLOCK), :]
            kb = kpad[pl.ds(r0, 2 * ATT_BLOCK), :]
            vb = vpad[pl.ds(r0, 2 * ATT_BLOCK), :]
            for pair in range(2):
                lanes = slice(pair * 128, (pair + 1) * 128)
                qp, dop, kp, vp = qb[:, lanes], dob[:, lanes], kb[:, lanes], vb[:, lanes]
                c0, c1 = pair * 128, pair * 128 + HEAD_DIM
                q2, do2 = _stack_heads(qp, lo), _stack_heads(dop, lo)
                lse2 = jnp.concatenate([lb[:, c0:c0 + 1], lb[:, c1:c1 + 1]], axis=0)
                dl2 = jnp.concatenate([dlb[:, c0:c0 + 1], dlb[:, c1:c1 + 1]], axis=0)
                s = lax.dot_general(q2, kp, NT, preferred_element_type=F32)
                s = jnp.where(valid, s, NEG)
                p = jnp.exp(s - lse2)
                dp = lax.dot_general(do2, vp, NT, preferred_element_type=F32)
                ds = (p * (dp - dl2)).astype(BF16)
                dq2 = jnp.dot(ds, kp, preferred_element_type=F32)
                out_ref[0, pl.ds(r0, ATT_BLOCK), lanes] = _unstack_heads(dq2, lo)
                dkpad[pl.ds(r0, 2 * ATT_BLOCK), lanes] += lax.dot_general(ds, q2, TN, preferred_element_type=F32)
                dvpad[pl.ds(r0, 2 * ATT_BLOCK), lanes] += lax.dot_general(p.astype(BF16), do2, TN, preferred_element_type=F32)
            return carry

        lax.fori_loop(0, nb, step, 0, unroll=min(4, nb))
        out_ref[1] = dkpad[ATT_BLOCK:, :]
        out_ref[2] = dvpad[ATT_BLOCK:, :]

    spec = lambda kind: pl.BlockSpec((None, None, L, GROUP_W), lambda r: (kind, r, 0, 0))
    cls = pl.BlockSpec((None, L, GROUP_W), lambda r: (r, 0, 0))
    return pl.pallas_call(
        body,
        name=f"attn_bwd_d{d}",
        grid=(d,),
        in_specs=[spec(0), spec(1), spec(2), cls, cls, cls],
        out_specs=pl.BlockSpec((3, None, L, GROUP_W), lambda r: (0, r, 0, 0)),
        out_shape=jax.ShapeDtypeStruct((3, d, L, GROUP_W), F32),
        scratch_shapes=[pltpu.VMEM((L + ATT_BLOCK, GROUP_W), BF16)] * 2 + [pltpu.VMEM((L + ATT_BLOCK, GROUP_W), F32)] * 2,
        compiler_params=_params(1),
    )(qkv, qkv, qkv, dattn, lse, delta)


def pool_bwd(dpool, mixed, wbd, b, scale):
    S = dpool.shape[0]

    def body(dp_ref, mx_ref, w_ref, b_ref, s_ref, du_ref, dw_ref, db_ref, ds_ref):
        dp = dp_ref[...]
        mb = mx_ref[...]
        wv = w_ref[...]
        ypre = jnp.dot(mb, wv, preferred_element_type=F32) + b_ref[...]
        ds_ref[...] = _colsum(dp * ypre)
        dpre = dp * s_ref[...]
        db_ref[...] = _colsum(dpre)
        dpb = dpre.astype(BF16)
        dw_ref[...] = lax.dot_general(mb, dpb, TN, preferred_element_type=F32)
        dmix = lax.dot_general(dpb, wv, NT, preferred_element_type=F32)
        row = lax.broadcasted_iota(jnp.int32, dmix.shape, 0)
        lane, win = _pool_lane_windows(dmix.shape)
        e = dmix / jnp.minimum(row + 1, win).astype(F32)

        def shift(a, k):
            return jnp.where(row < S - k, pltpu.roll(a, S - k, 0), 0.0)

        f2 = e + shift(e, 1)
        f4 = f2 + shift(f2, 2)
        f8 = f4 + shift(f4, 4)
        f16 = f8 + shift(f8, 8)
        du_ref[...] = jnp.where(lane < 64, f2, jnp.where(lane < 128, f4, jnp.where(lane < 192, f8, f16))) - dmix

    vm = pl.BlockSpec(memory_space=pltpu.VMEM)
    return pl.pallas_call(
        body,
        name="pool_bwd",
        in_specs=[vm] * 5,
        out_specs=[vm] * 4,
        out_shape=[jax.ShapeDtypeStruct((S, POOL_W), F32), jax.ShapeDtypeStruct((POOL_W, POOL_W), F32),
                   jax.ShapeDtypeStruct((1, POOL_W), F32), jax.ShapeDtypeStruct((1, POOL_W), F32)],
        compiler_params=_params(),
    )(dpool, mixed, wbd, b, scale)


def inproj_bwd(dqkv, du, x, dx1, w_in_g, g, mod6, tc, tsa, tsb, h1, tm=512):
    S = x.shape[0]

    def body(d0, d1, d2, du_ref, x_ref, dx1_ref, w_ref, g_ref, mod_ref, tc_ref, tsa_ref, tsb_ref, h_ref,
             gx_ref, dsh_ref, dsc_ref, dg_ref, dw_ref, s4, s16, dp_ref):
        @pl.when(pl.program_id(0) == 0)
        def _():
            dw_ref[...] = jnp.zeros_like(dw_ref)

        cs, sa, sb = tc_ref[...], tsa_ref[...], tsb_ref[...]
        for d, src, dst in ((4, d1, s4), (16, d2, s16)):
            for kind in range(3):
                for r in range(d):
                    for h in range(2):
                        dst[kind, h, pl.ds(r, tm // d, stride=d), :] = src[kind, r, :, h * 128:(h + 1) * 128]
        for sp in range(20):
            piece, half = sp // 2, sp % 2
            lanes = slice(half * 128, (half + 1) * 128)
            if piece == 0:
                blk = du_ref[:, lanes]
            else:
                kind, gi = (piece - 1) // 3, (piece - 1) % 3
                blk = d0[kind, 0, :, lanes] if gi == 0 else (s4, s16)[gi - 1][kind, half]
                if kind == 0:
                    blk = _rope128(blk, cs, sa, sb, -1.0) * (HEAD_DIM ** -0.5)
                elif kind == 1:
                    blk = _rope128(blk, cs, sa, sb, -1.0)
            dp_ref[:, sp * 128:(sp + 1) * 128] = blk.astype(BF16)
        dh = jnp.zeros((tm, D_MODEL), F32)
        hbt = h_ref[...].T
        for j in range(N_CHIPS):
            dpj = dp_ref[:, j * 640:(j + 1) * 640]
            dh = dh + lax.dot_general(dpj, w_ref[j], NT, preferred_element_type=F32)
            dw_ref[j] += jnp.dot(hbt, dpj, preferred_element_type=F32)
        xv = x_ref[...]
        rstd = lax.rsqrt(jnp.mean(xv * xv, axis=-1, keepdims=True) + NORM_EPS)
        n1 = xv * rstd
        gv = g_ref[...]
        one_sc = 1.0 + mod_ref[1:2, :]
        _acc(dsh_ref, _colsum(dh))
        _acc(dsc_ref, _colsum(dh * (n1 * gv)))
        _acc(dg_ref, _colsum(dh * one_sc * n1))
        dn = dh * (gv * one_sc)
        gx_ref[...] = dx1_ref[...] + rstd * (dn - n1 * jnp.mean(dn * n1, axis=-1, keepdims=True))

    vec = _full((1, D_MODEL))
    dspec = lambda d: pl.BlockSpec((3, d, tm // d, GROUP_W), lambda i: (0, 0, i, 0))
    return pl.pallas_call(
        body,
        name="inproj_bwd",
        grid=(S // tm,),
        in_specs=[dspec(d) for d in DILATIONS] + [_rows(tm, POOL_W), _rows(tm, D_MODEL), _rows(tm, D_MODEL), _full(w_in_g.shape),
                                                  vec, _full((6, D_MODEL)), _rows(tm, 128), _rows(tm, 128), _rows(tm, 128),
                                                  _rows(tm, D_MODEL)],
        out_specs=[_rows(tm, D_MODEL), vec, vec, vec, _full(w_in_g.shape)],
        out_shape=[jax.ShapeDtypeStruct((S, D_MODEL), F32)] + [jax.ShapeDtypeStruct((1, D_MODEL), F32)] * 3
        + [jax.ShapeDtypeStruct(w_in_g.shape, F32)],
        scratch_shapes=[pltpu.VMEM((3, 2, tm, 128), F32)] * 2 + [pltpu.VMEM((tm, IN_W), BF16)],
        compiler_params=_params(1),
    )(*dqkv, du, x, dx1, w_in_g, g, mod6, tc, tsa, tsb, h1)


def _adamw(w, g, m, v):
    m = ADAM_B1 * m + (1.0 - ADAM_B1) * g
    v = ADAM_B2 * v + (1.0 - ADAM_B2) * (g * g)
    m_hat = m / (1.0 - ADAM_B1 ** ADAM_STEP)
    v_hat = v / (1.0 - ADAM_B2 ** ADAM_STEP)
    delta = -ADAM_LR * (m_hat / (jnp.sqrt(v_hat) + ADAM_EPS) + ADAM_WD * w)
    return delta, m, v


def adamw_rows(w, g, m, v, tr, name):
    R, C = w.shape

    def body(w_ref, g_ref, m_ref, v_ref, go_ref, d_ref, mo_ref, vo_ref):
        g = g_ref[...]
        go_ref[...] = g
        d_ref[...], mo_ref[...], vo_ref[...] = _adamw(w_ref[...], g, m_ref[...], v_ref[...])

    spec = pl.BlockSpec((tr, C), lambda i: (i, 0))
    return pl.pallas_call(
        body,
        name=name,
        grid=(R // tr,),
        in_specs=[spec] * 4,
        out_specs=[spec] * 4,
        out_shape=[jax.ShapeDtypeStruct((R, C), F32)] * 4,
        compiler_params=_params(1),
    )(w, g, m, v)


def adamw_ada(c_all_t, dmod_cols, w, m, v, tr=256):
    R, C = w.shape

    def body(ct_ref, dm_ref, w_ref, m_ref, v_ref, g_ref, d_ref, mo_ref, vo_ref):
        ct = ct_ref[...]
        act = ct * jax.nn.sigmoid(ct)
        g = jnp.zeros((tr, C), F32)
        for b in range(N_DEV):
            g = g + act[:, b:b + 1] * dm_ref[b:b + 1, :]
        g_ref[...] = g
        d_ref[...], mo_ref[...], vo_ref[...] = _adamw(w_ref[...], g, m_ref[...], v_ref[...])

    spec = pl.BlockSpec((tr, C), lambda i: (i, 0))
    return pl.pallas_call(
        body,
        name="adamw_ada",
        grid=(R // tr,),
        in_specs=[pl.BlockSpec((tr, N_DEV), lambda i: (i, 0)), _full((N_DEV, C)), spec, spec, spec],
        out_specs=[spec] * 4,
        out_shape=[jax.ShapeDtypeStruct((R, C), F32)] * 4,
        compiler_params=_params(1),
    )(c_all_t, dmod_cols, w, m, v)


def adamw_small(slab_a, slab_b, convw_g, wpool_g, params):
    names = ["b_ada", "g_pre_mix", "g_post_mix", "g_pre_ffn", "g_post_ffn", "b_pool", "pool_scale", "conv_b", "conv_w", "w_pool"]
    flat = []
    for n in names:
        flat += list(params[n])

    def body(a_ref, b_ref, cw_ref, wp_ref, *rest):
        ins, outs = rest[:30], rest[30:]

        def dev_sum(ref):
            t = ref[0]
            for dev in range(1, N_DEV):
                t = t + ref[dev]
            return t

        sa, sb_, scw, swp = dev_sum(a_ref), dev_sum(b_ref), dev_sum(cw_ref), dev_sum(wp_ref)
        grads = [
            jnp.concatenate([sa[k:k + 1, :] for k in range(6)], axis=1),
            sa[6:7, :], sa[7:8, :], sa[8:9, :], sa[9:10, :],
            sa[10:11, 0:256], sa[10:11, 256:512],
            sb_[3:4, :], scw, swp,
        ]
        for i, g in enumerate(grads):
            w_ref, m_ref, v_ref = ins[3 * i:3 * i + 3]
            if names[i] in ("b_pool", "w_pool"):
                for grp in range(4):
                    if names[i] == "b_pool":
                        gp, at = g[:, grp * 64:(grp + 1) * 64], (0, slice(grp, grp + 1))
                    else:
                        gp, at = g[grp * 64:(grp + 1) * 64, :], (0, grp)
                    d, mo, vo = _adamw(w_ref[at], gp, m_ref[at], v_ref[at])
                    for k, val in enumerate((gp, d, mo, vo)):
                        outs[4 * i + k][at] = val
                continue
            d, mo, vo = _adamw(w_ref[...], g, m_ref[...], v_ref[...])
            outs[4 * i][...] = g
            outs[4 * i + 1][...] = d
            outs[4 * i + 2][...] = mo
            outs[4 * i + 3][...] = vo
        outs[-1][...] = sa[10:11, 512:640]

    vm = pl.BlockSpec(memory_space=pltpu.VMEM)
    out_shape = []
    for n in names:
        out_shape += [jax.ShapeDtypeStruct(params[n][0].shape, F32)] * 4
    out_shape.append(jax.ShapeDtypeStruct((1, 128), F32))
    outs = pl.pallas_call(
        body,
        name="adamw_small",
        in_specs=[vm] * (4 + len(flat)),
        out_specs=[vm] * len(out_shape),
        out_shape=out_shape,
        compiler_params=_params(),
    )(slab_a, slab_b, convw_g, wpool_g, *flat)
    return {n: outs[4 * i:4 * i + 4] for i, n in enumerate(names)}, outs[-1]


def _place():
    return lax.axis_index("x"), lax.axis_index("y"), lax.axis_index("c")


def _other_chips(x, y):
    return [(1 - x, y), (x, 1 - y), (1 - x, 1 - y)]


def _chip_id(cx, cy):
    return 2 * cx + cy


def gather_weights(shards):
    n = len(shards)
    halved = [s.shape[0] % 32 == 0 for s in shards]

    def body(*refs):
        ins, outs = refs[:n], refs[n:2 * n]
        send_sems, recv_sems, loc_sems = refs[2 * n:]
        x, y, c = _place()
        me = _chip_id(x, y)
        chips = _other_chips(x, y)
        sib = (x, y, 1 - c)

        def part(w, chip, half):
            if not halved[w]:
                return outs[w].at[chip]
            rh = shards[w].shape[0] // 2
            return outs[w].at[chip, pl.ds(half * rh, rh), :]

        def src_part(w):
            if not halved[w]:
                return ins[w]
            rh = shards[w].shape[0] // 2
            return ins[w].at[pl.ds(c * rh, rh), :]

        def rcopy(w, k, src, dst, to):
            return pltpu.make_async_remote_copy(src_ref=src, dst_ref=dst, send_sem=send_sems.at[6 * w + k],
                                                recv_sem=recv_sems.at[6 * w + k], device_id=to, device_id_type=MESH)

        local = [pltpu.make_async_copy(ins[w], outs[w].at[me], loc_sems.at[w]) for w in range(n)]
        for cp in local:
            cp.start()
        first = []
        for w in range(n):
            for k, (cx, cy) in enumerate(chips):
                cp = rcopy(w, k, src_part(w), part(w, me, c), (cx, cy, c))
                cp.start()
                first.append(cp)
        passed = []
        for w in range(n):
            for k, (cx, cy) in enumerate(chips):
                blk = part(w, _chip_id(cx, cy), c)
                rcopy(w, k, blk, blk, (cx, cy, c)).wait_recv()
                if halved[w]:
                    cp = rcopy(w, 3 + k, blk, blk, sib)
                    cp.start()
                    passed.append(cp)
        for w in range(n):
            if halved[w]:
                for k, (cx, cy) in enumerate(chips):
                    blk = part(w, _chip_id(cx, cy), 1 - c)
                    rcopy(w, 3 + k, blk, blk, sib).wait_recv()
        for cp in first + passed:
            cp.wait_send()
        for cp in local:
            cp.wait()

    hbm = pl.BlockSpec(memory_space=pl.ANY)
    return pl.pallas_call(
        body,
        name="gather_weights",
        in_specs=[hbm] * n,
        out_specs=[hbm] * n,
        out_shape=[jax.ShapeDtypeStruct((N_CHIPS,) + s.shape, s.dtype) for s in shards],
        scratch_shapes=[pltpu.SemaphoreType.DMA((6 * n,)), pltpu.SemaphoreType.DMA((6 * n,)), pltpu.SemaphoreType.DMA((n,))],
        compiler_params=pltpu.CompilerParams(has_side_effects=True, vmem_limit_bytes=VMEM_LIMIT),
    )(*shards)


HBM_SPEC = pl.BlockSpec(memory_space=pltpu.HBM)
SEM_SPEC = pl.BlockSpec(memory_space=pltpu.SEMAPHORE)
ANY_SPEC = pl.BlockSpec(memory_space=pl.ANY)
EFFECT = pltpu.SideEffectType.DATAFLOW_SIDE_EFFECTING


def _hbm(t):
    return pltpu.with_memory_space_constraint(t, pltpu.HBM)


def _hbm_shapes(ts):
    return [pltpu.HBM(t.shape, t.dtype) for t in ts]


def _half_rows(ref, lead, half, rh):
    return ref.at[lead, pl.ds(half * rh, rh), :]


def gather_split_start(shards, lands, carry, k):
    n = len(shards)

    def body(*refs):
        ins, land = refs[:n], refs[n:2 * n]
        send_sems, recv_sems = refs[2 * n + 1], refs[2 * n + 2]
        loc_sems = refs[-1]
        x, y, c = _place()
        me = _chip_id(x, y)
        if k == 0:
            local = [pltpu.make_async_copy(ins[w], land[w].at[me], loc_sems.at[w]) for w in range(n)]
            for cp in local:
                cp.start()
            for cp in local:
                cp.wait()
        cx, cy = _other_chips(x, y)[k]
        for w in range(n):
            rh = shards[w].shape[0] // 2
            pltpu.make_async_remote_copy(src_ref=ins[w].at[pl.ds(c * rh, rh), :], dst_ref=_half_rows(land[w], me, c, rh),
                                         send_sem=send_sems.at[w], recv_sem=recv_sems.at[w],
                                         device_id=(cx, cy, c), device_id_type=MESH).start()

    args = [_hbm(s) for s in shards] + [_hbm(l) for l in lands] + [_hbm(carry)]
    outs = pl.pallas_call(
        body,
        name="gather_split_start%d" % k,
        out_shape=[pltpu.SemaphoreType.DMA((n,)), pltpu.SemaphoreType.DMA((n,))] + _hbm_shapes(shards) + _hbm_shapes(lands)
        + _hbm_shapes([carry]),
        in_specs=[HBM_SPEC] * (2 * n + 1),
        out_specs=[SEM_SPEC, SEM_SPEC] + [HBM_SPEC] * (2 * n + 1),
        input_output_aliases={i: 2 + i for i in range(2 * n + 1)},
        scratch_shapes=[pltpu.SemaphoreType.DMA((n,))],
        compiler_params=pltpu.CompilerParams(has_side_effects=EFFECT),
    )(*args)
    return outs[0], outs[1], list(outs[2:2 + n]), list(outs[2 + n:2 + 2 * n]), outs[-1]


def gather_split_mid(sems, shards, lands, after):
    n = len(shards)

    def body(*refs):
        ins, land = refs[:n], refs[n:2 * n]
        sem_in = refs[2 * n:2 * n + 6]
        fsend, frecv = refs[2 * n + 7], refs[2 * n + 8]
        x, y, c = _place()
        me = _chip_id(x, y)
        chips = _other_chips(x, y)
        for w in range(n):
            rh = shards[w].shape[0] // 2
            for k, (cx, cy) in enumerate(chips):
                got = _half_rows(land[w], _chip_id(cx, cy), c, rh)
                cp = pltpu.make_async_remote_copy(src_ref=ins[w].at[pl.ds(c * rh, rh), :], dst_ref=got, send_sem=sem_in[2 * k].at[w],
                                                  recv_sem=sem_in[2 * k + 1].at[w], device_id=(cx, cy, c), device_id_type=MESH)
                cp.wait_send()
                cp.wait_recv()
        for w in range(n):
            rh = shards[w].shape[0] // 2
            for k, (cx, cy) in enumerate(chips):
                got = _half_rows(land[w], _chip_id(cx, cy), c, rh)
                pltpu.make_async_remote_copy(src_ref=got, dst_ref=got, send_sem=fsend.at[3 * w + k], recv_sem=frecv.at[3 * w + k],
                                             device_id=(x, y, 1 - c), device_id_type=MESH).start()

    outs = pl.pallas_call(
        body,
        name="gather_split_mid",
        out_shape=[pltpu.SemaphoreType.DMA((3 * n,)), pltpu.SemaphoreType.DMA((3 * n,))] + _hbm_shapes(lands),
        in_specs=[HBM_SPEC] * (2 * n) + [SEM_SPEC] * 6 + [ANY_SPEC],
        out_specs=[SEM_SPEC, SEM_SPEC] + [HBM_SPEC] * n,
        input_output_aliases={n + i: 2 + i for i in range(n)},
        compiler_params=pltpu.CompilerParams(has_side_effects=EFFECT),
    )(*shards, *lands, *sems, after)
    return outs[0], outs[1], list(outs[2:])


def gather_split_done(fsend, frecv, lands, after):
    n = len(lands)

    def body(*refs):
        land = refs[:n]
        ssem, rsem = refs[n], refs[n + 1]
        x, y, c = _place()
        for w in range(n):
            rh = lands[w].shape[1] // 2
            for k, (cx, cy) in enumerate(_other_chips(x, y)):
                sent = _half_rows(land[w], _chip_id(cx, cy), c, rh)
                got = _half_rows(land[w], _chip_id(cx, cy), 1 - c, rh)
                cp = pltpu.make_async_remote_copy(src_ref=sent, dst_ref=got, send_sem=ssem.at[3 * w + k], recv_sem=rsem.at[3 * w + k],
                                                  device_id=(x, y, 1 - c), device_id_type=MESH)
                cp.wait_send()
                cp.wait_recv()

    outs = pl.pallas_call(
        body,
        name="gather_split_done",
        out_shape=_hbm_shapes(lands),
        in_specs=[HBM_SPEC] * n + [SEM_SPEC, SEM_SPEC, ANY_SPEC],
        out_specs=[HBM_SPEC] * n,
        input_output_aliases={i: i for i in range(n)},
        compiler_params=pltpu.CompilerParams(has_side_effects=EFFECT),
    )(*lands, fsend, frecv, after)
    return list(outs)


def _flips():
    return [(fx, fy, fc) for fx in (0, 1) for fy in (0, 1) for fc in (0, 1)][1:]


def _flip(v, f):
    return v if f == 0 else 1 - v


def ada_mod(c3, w_ada, b_cols, conv_w):
    CB = w_ada.shape[1]

    def body(c_ref, w_ref, b_ref, cw_ref, call_ref, mod_ref, cwall_ref, modall, send_sems, recv_sems):
        x, y, c = _place()
        me_dev = 4 * x + 2 * y + c
        me = _chip_id(x, y)
        call_ref[me_dev] = c_ref[0]
        cwall_ref[me] = cw_ref[...]
        sends = []
        for k, (cx, cy) in enumerate(_other_chips(x, y)):
            cp = pltpu.make_async_remote_copy(src_ref=cw_ref, dst_ref=cwall_ref.at[me], send_sem=send_sems.at[10 + k],
                                              recv_sem=recv_sems.at[10 + k], device_id=(cx, cy, c), device_id_type=MESH)
            cp.start()
            sends.append(cp)
        for k, (fx, fy, fc) in enumerate(_flips()):
            cp = pltpu.make_async_remote_copy(src_ref=c_ref.at[0], dst_ref=call_ref.at[me_dev], send_sem=send_sems.at[k],
                                              recv_sem=recv_sems.at[k],
                                              device_id=(_flip(x, fx), _flip(y, fy), _flip(c, fc)), device_id_type=MESH)
            cp.start()
            sends.append(cp)
        for k, (fx, fy, fc) in enumerate(_flips()):
            peer = 4 * _flip(x, fx) + 2 * _flip(y, fy) + _flip(c, fc)
            pltpu.make_async_remote_copy(src_ref=c_ref.at[0], dst_ref=call_ref.at[peer], send_sem=send_sems.at[k],
                                         recv_sem=recv_sems.at[k], device_id=(x, y, c), device_id_type=MESH).wait_recv()
        row = lax.broadcasted_iota(jnp.int32, (N_DEV, D_MODEL), 0)
        call = jnp.zeros((N_DEV, D_MODEL), F32)
        for dev in range(N_DEV):
            call = jnp.where(row == dev, call_ref[dev], call)
        act = call * jax.nn.sigmoid(call)
        modall[me] = jnp.dot(act, w_ref[...], preferred_element_type=F32, precision=lax.Precision.HIGHEST) + b_ref[...]
        for k, (cx, cy) in enumerate(_other_chips(x, y)):
            cp = pltpu.make_async_remote_copy(src_ref=modall.at[me], dst_ref=modall.at[me], send_sem=send_sems.at[7 + k],
                                              recv_sem=recv_sems.at[7 + k], device_id=(cx, cy, c), device_id_type=MESH)
            cp.start()
            sends.append(cp)
        for k, (cx, cy) in enumerate(_other_chips(x, y)):
            blk = modall.at[_chip_id(cx, cy)]
            pltpu.make_async_remote_copy(src_ref=blk, dst_ref=blk, send_sem=send_sems.at[7 + k], recv_sem=recv_sems.at[7 + k],
                                         device_id=(x, y, c), device_id_type=MESH).wait_recv()
        for k, (cx, cy) in enumerate(_other_chips(x, y)):
            blk = cwall_ref.at[_chip_id(cx, cy)]
            pltpu.make_async_remote_copy(src_ref=blk, dst_ref=blk, send_sem=send_sems.at[10 + k], recv_sem=recv_sems.at[10 + k],
                                         device_id=(x, y, c), device_id_type=MESH).wait_recv()
        for cp in sends:
            cp.wait_send()
        mine = [modall[j, pl.ds(me_dev, 1), :] for j in range(N_CHIPS)]
        for r in range(6):
            pieces = []
            for h in range(2):
                pos = r * D_MODEL + h * 512
                pieces.append(mine[pos // CB][:, pos % CB:pos % CB + 512])
            mod_ref[r:r + 1, :] = jnp.concatenate(pieces, axis=1)

    vm = pl.BlockSpec(memory_space=pltpu.VMEM)
    return pl.pallas_call(
        body,
        name="ada_mod",
        in_specs=[vm] * 4,
        out_specs=[vm] * 3,
        out_shape=[jax.ShapeDtypeStruct((N_DEV, 1, D_MODEL), F32), jax.ShapeDtypeStruct((6, D_MODEL), F32),
                   jax.ShapeDtypeStruct((N_CHIPS,) + conv_w.shape, F32)],
        scratch_shapes=[pltpu.VMEM((N_CHIPS, N_DEV, CB), F32), pltpu.SemaphoreType.DMA((13,)), pltpu.SemaphoreType.DMA((13,))],
        compiler_params=pltpu.CompilerParams(has_side_effects=True, vmem_limit_bytes=VMEM_LIMIT),
    )(c3, w_ada, b_cols, conv_w)


def gather_small(blocks):
    n = len(blocks)

    def body(*refs):
        ins, outs = refs[:n], refs[n:2 * n]
        send_sems, recv_sems = refs[2 * n:]
        x, y, c = _place()
        sib = (x, y, 1 - c)
        chips = _other_chips(x, y)

        def dev(px, py, pc):
            return 4 * px + 2 * py + pc

        def cp(w, k, src, block_dev, to):
            return pltpu.make_async_remote_copy(src_ref=src, dst_ref=outs[w].at[block_dev], send_sem=send_sems.at[7 * w + k],
                                                recv_sem=recv_sems.at[7 * w + k], device_id=to, device_id_type=MESH)

        me = dev(x, y, c)
        started = []
        for w in range(n):
            outs[w][me] = ins[w][...]
            t = cp(w, 0, ins[w], me, sib)
            t.start()
            started.append(t)
            for k, (cx, cy) in enumerate(chips):
                t = cp(w, 1 + k, ins[w], me, (cx, cy, c))
                t.start()
                started.append(t)
        for w in range(n):
            for k, (cx, cy) in enumerate(chips):
                b = dev(cx, cy, c)
                cp(w, 1 + k, outs[w].at[b], b, (x, y, c)).wait_recv()
                t = cp(w, 4 + k, outs[w].at[b], b, sib)
                t.start()
                started.append(t)
        for w in range(n):
            b = dev(x, y, 1 - c)
            cp(w, 0, outs[w].at[b], b, (x, y, c)).wait_recv()
            for k, (cx, cy) in enumerate(chips):
                b = dev(cx, cy, 1 - c)
                cp(w, 4 + k, outs[w].at[b], b, (x, y, c)).wait_recv()
        for t in started:
            t.wait_send()

    vm = pl.BlockSpec(memory_space=pltpu.VMEM)
    return pl.pallas_call(
        body,
        name="gather_small",
        in_specs=[vm] * n,
        out_specs=[vm] * n,
        out_shape=[jax.ShapeDtypeStruct((N_DEV,) + b.shape, b.dtype) for b in blocks],
        scratch_shapes=[pltpu.SemaphoreType.DMA((7 * n,)), pltpu.SemaphoreType.DMA((7 * n,))],
        compiler_params=pltpu.CompilerParams(has_side_effects=True, vmem_limit_bytes=VMEM_LIMIT),
    )(*blocks)


def reduce_scatter_grads(grads, chunk_rows):
    n = len(grads)
    shapes = [g.shape[1:] for g in grads]
    halves = [s[0] // 2 for s in shapes]

    def body(*refs):
        gin = refs[:n]
        gout = refs[n:2 * n]
        sibbuf = refs[2 * n:3 * n]
        rest = refs[3 * n:]
        rbuf = rest[:n]
        pown = rest[n:2 * n]
        stage_a, stage_b, stage_o, stage_f = rest[2 * n:2 * n + 4]
        sib_send, sib_recv, ici_send, ici_recv, fin_send, fin_recv, ld_sems, st_sems = rest[2 * n + 4:]
        x, y, c = _place()
        me = _chip_id(x, y)
        chips = _other_chips(x, y)
        sib = (x, y, 1 - c)

        to_sib = []
        for w in range(n):
            rh = halves[w]
            cp = pltpu.make_async_remote_copy(src_ref=gin[w].at[:, pl.ds((1 - c) * rh, rh), :], dst_ref=sibbuf[w],
                                              send_sem=sib_send.at[w], recv_sem=sib_recv.at[w], device_id=sib,
                                              device_id_type=MESH)
            cp.start()
            to_sib.append(cp)

        sent = []
        for w in range(n):
            rh, cw = halves[w], shapes[w][1]
            ch = chunk_rows[w]
            to_sib[w].wait_recv()
            for k in range(4):
                chip = me if k == 3 else _chip_id(*chips[k])
                for r0 in range(0, rh, ch):
                    la = pltpu.make_async_copy(gin[w].at[chip, pl.ds(c * rh + r0, ch), :], stage_a.at[0:ch, 0:cw], ld_sems.at[0])
                    lb = pltpu.make_async_copy(sibbuf[w].at[chip, pl.ds(r0, ch), :], stage_b.at[0:ch, 0:cw], ld_sems.at[1])
                    la.start()
                    lb.start()
                    la.wait()
                    lb.wait()
                    tot = stage_a[0:ch, 0:cw] + stage_b[0:ch, 0:cw]
                    if k == 3:
                        pown[w][r0:r0 + ch, :] = tot
                    else:
                        stage_o[0:ch, 0:cw] = tot.astype(BF16)
                        cx, cy = chips[k]
                        cp = pltpu.make_async_remote_copy(src_ref=stage_o.at[0:ch, 0:cw], dst_ref=rbuf[w].at[k, r0:r0 + ch, :],
                                                          send_sem=ici_send.at[3 * w + k], recv_sem=ici_recv.at[3 * w + k],
                                                          device_id=(cx, cy, c), device_id_type=MESH)
                        cp.start()
                        cp.wait_send()
            sent.append(w)

        fin = []
        for w in range(n):
            rh, cw = halves[w], shapes[w][1]
            for k in range(3):
                whole = rbuf[w].at[k]
                pltpu.make_async_remote_copy(src_ref=whole, dst_ref=whole, send_sem=ici_send.at[3 * w + k],
                                             recv_sem=ici_recv.at[3 * w + k], device_id=(x, y, c),
                                             device_id_type=MESH).wait_recv()
            pown[w][...] = ((pown[w][...] + rbuf[w][0].astype(F32)) + rbuf[w][1].astype(F32)) + rbuf[w][2].astype(F32)
            mine = gout[w].at[pl.ds(c * rh, rh), :]
            st = pltpu.make_async_copy(pown[w], mine, st_sems.at[w])
            st.start()
            cp = pltpu.make_async_remote_copy(src_ref=pown[w], dst_ref=mine, send_sem=fin_send.at[w], recv_sem=fin_recv.at[w],
                                              device_id=sib, device_id_type=MESH)
            cp.start()
            fin.append((st, cp))
        for w in range(n):
            rh = halves[w]
            theirs = gout[w].at[pl.ds((1 - c) * rh, rh), :]
            pltpu.make_async_remote_copy(src_ref=theirs, dst_ref=theirs, send_sem=fin_send.at[w], recv_sem=fin_recv.at[w],
                                         device_id=(x, y, c), device_id_type=MESH).wait_recv()
        for cp in to_sib:
            cp.wait_send()
        for st, cp in fin:
            st.wait()
            cp.wait_send()

    hbm = pl.BlockSpec(memory_space=pl.ANY)
    max_ch = max(chunk_rows)
    max_c = max(s[1] for s in shapes)
    outs = pl.pallas_call(
        body,
        name="reduce_scatter_grads",
        in_specs=[hbm] * n,
        out_specs=[hbm] * (2 * n),
        out_shape=[jax.ShapeDtypeStruct(s, F32) for s in shapes]
        + [jax.ShapeDtypeStruct((N_CHIPS, h, s[1]), F32) for h, s in zip(halves, shapes)],
        scratch_shapes=[pltpu.VMEM((3, h, s[1]), BF16) for h, s in zip(halves, shapes)]
        + [pltpu.VMEM((h, s[1]), F32) for h, s in zip(halves, shapes)]
        + [pltpu.VMEM((max_ch, max_c), F32), pltpu.VMEM((max_ch, max_c), F32), pltpu.VMEM((max_ch, max_c), BF16),
           pltpu.VMEM((8, 128), F32)]
        + [pltpu.SemaphoreType.DMA((n,)), pltpu.SemaphoreType.DMA((n,)), pltpu.SemaphoreType.DMA((3 * n,)),
           pltpu.SemaphoreType.DMA((3 * n,)), pltpu.SemaphoreType.DMA((n,)), pltpu.SemaphoreType.DMA((n,)),
           pltpu.SemaphoreType.DMA((2,)), pltpu.SemaphoreType.DMA((n,))],
        compiler_params=pltpu.CompilerParams(has_side_effects=True, vmem_limit_bytes=VMEM_LIMIT),
    )(*grads)
    return outs[:n]


def split_start(name, bufs, plan, n_sem, carry):
    nb = len(bufs)

    def body(*refs):
        x, y, c = _place()
        ssem, rsem = refs[nb + 1], refs[nb + 2]
        for i, (src, dst, dev) in enumerate(plan(refs[:nb], x, y, c)):
            pltpu.make_async_remote_copy(src_ref=src, dst_ref=dst, send_sem=ssem.at[i], recv_sem=rsem.at[i], device_id=dev,
                                         device_id_type=MESH).start()

    alls = list(bufs) + [carry]
    outs = pl.pallas_call(
        body,
        name=name,
        out_shape=[pltpu.SemaphoreType.DMA((n_sem,)), pltpu.SemaphoreType.DMA((n_sem,))] + _hbm_shapes(alls),
        in_specs=[HBM_SPEC] * (nb + 1),
        out_specs=[SEM_SPEC, SEM_SPEC] + [HBM_SPEC] * (nb + 1),
        input_output_aliases={i: 2 + i for i in range(nb + 1)},
        compiler_params=pltpu.CompilerParams(has_side_effects=EFFECT),
    )(*[_hbm(t) for t in alls])
    return outs[0], outs[1], list(outs[2:2 + nb]), outs[-1]


def split_wait(name, ssem, rsem, bufs, plan, after):
    nb = len(bufs)

    def body(*refs):
        x, y, c = _place()
        s_ref, r_ref = refs[nb], refs[nb + 1]
        for i, (src, dst, dev) in enumerate(plan(refs[:nb], x, y, c)):
            cp = pltpu.make_async_remote_copy(src_ref=src, dst_ref=dst, send_sem=s_ref.at[i], recv_sem=r_ref.at[i], device_id=dev,
                                              device_id_type=MESH)
            cp.wait_send()
            cp.wait_recv()

    outs = pl.pallas_call(
        body,
        name=name,
        out_shape=_hbm_shapes(bufs),
        in_specs=[HBM_SPEC] * nb + [SEM_SPEC, SEM_SPEC, ANY_SPEC],
        out_specs=[HBM_SPEC] * nb,
        input_output_aliases={i: i for i in range(nb)},
        compiler_params=pltpu.CompilerParams(has_side_effects=EFFECT),
    )(*bufs, ssem, rsem, after)
    return list(outs)


def _gather_ici_plan(n):
    def plan(refs, x, y, c):
        out = []
        for w in range(n):
            rh = refs[w].shape[0] // 2
            for cx, cy in _other_chips(x, y):
                out.append((refs[w].at[pl.ds(c * rh, rh), :], _half_rows(refs[n + w], _chip_id(x, y), c, rh), (cx, cy, c)))
        return out

    return plan


def _gather_d2d_plan(n):
    def plan(refs, x, y, c):
        out = []
        for w in range(n):
            rh = refs[w].shape[1] // 2
            for cx, cy in _other_chips(x, y):
                blk = _half_rows(refs[w], _chip_id(cx, cy), c, rh)
                out.append((blk, blk, (x, y, 1 - c)))
        return out

    return plan


def _dev_id(x, y, c):
    return 4 * x + 2 * y + c


def _small_ici_plan(n):
    def plan(refs, x, y, c):
        out = []
        for w in range(n):
            dst = refs[n + w].at[_dev_id(x, y, c)]
            out.append((refs[w], dst, (x, y, 1 - c)))
            for cx, cy in _other_chips(x, y):
                out.append((refs[w], dst, (cx, cy, c)))
        return out

    return plan


def _small_d2d_plan(n):
    def plan(refs, x, y, c):
        out = []
        for w in range(n):
            for cx, cy in _other_chips(x, y):
                blk = refs[w].at[_dev_id(cx, cy, c)]
                out.append((blk, blk, (x, y, 1 - c)))
        return out

    return plan


def _rs_d2d_plan(n):
    def plan(refs, x, y, c):
        out = []
        for w in range(n):
            rh = refs[w].shape[1] // 2
            out.append((refs[w].at[:, pl.ds((1 - c) * rh, rh), :], refs[n + w], (x, y, 1 - c)))
        return out

    return plan


def _rs_ici_plan(n):
    def plan(refs, x, y, c):
        out = []
        for w in range(n):
            for k, (cx, cy) in enumerate(_other_chips(x, y)):
                out.append((refs[w].at[_chip_id(cx, cy)], refs[n + w].at[k], (cx, cy, c)))
        return out

    return plan


def _rs_share_plan(n):
    def plan(refs, x, y, c):
        out = []
        for w in range(n):
            rh = refs[w].shape[0] // 2
            rows = refs[w].at[pl.ds(c * rh, rh), :]
            out.append((rows, rows, (x, y, 1 - c)))
        return out

    return plan


def rs_add(grad, sibbuf, place, tr, name):
    _, R, C = grad.shape
    nt = (R // 2) // tr

    def body(p_ref, g_ref, s_ref, o_ref):
        o_ref[...] = (g_ref[...] + s_ref[...]).astype(BF16)

    return pl.pallas_call(
        body,
        name=name,
        grid_spec=pltpu.PrefetchScalarGridSpec(
            num_scalar_prefetch=1,
            grid=(N_CHIPS, nt),
            in_specs=[pl.BlockSpec((None, tr, C), lambda j, i, p: (j, p[0] * nt + i, 0)),
                      pl.BlockSpec((None, tr, C), lambda j, i, p: (j, i, 0))],
            out_specs=pl.BlockSpec((None, tr, C), lambda j, i, p: (j, i, 0)),
        ),
        out_shape=jax.ShapeDtypeStruct((N_CHIPS, R // 2, C), BF16),
        compiler_params=_params(2),
    )(place, grad, sibbuf)


def rs_final(grad, sibbuf, rbuf, place, tr, name):
    _, R, C = grad.shape
    nt = (R // 2) // tr

    def body(p_ref, g_ref, s_ref, r_ref, o_ref):
        o_ref[...] = (((g_ref[...] + s_ref[...]) + r_ref[0].astype(F32)) + r_ref[1].astype(F32)) + r_ref[2].astype(F32)

    return pl.pallas_call(
        body,
        name=name,
        grid_spec=pltpu.PrefetchScalarGridSpec(
            num_scalar_prefetch=1,
            grid=(nt,),
            in_specs=[pl.BlockSpec((None, tr, C), lambda i, p: (p[1], p[0] * nt + i, 0)),
                      pl.BlockSpec((None, tr, C), lambda i, p: (p[1], i, 0)),
                      pl.BlockSpec((3, tr, C), lambda i, p: (0, i, 0))],
            out_specs=pl.BlockSpec((tr, C), lambda i, p: (p[0] * nt + i, 0)),
        ),
        out_shape=jax.ShapeDtypeStruct((R, C), F32),
        compiler_params=_params(1),
    )(place, grad, sibbuf, rbuf)


class GradReduce:
    def __init__(self, tag, grads, rows, place):
        self.tag, self.grads, self.rows, self.place = tag, grads, rows, place
        self.n = len(grads)

    def d2d_start(self, carry):
        sib = [lax.empty((N_CHIPS, g.shape[1] // 2, g.shape[2]), F32) for g in self.grads]
        self.s1, self.r1, bufs, carry = split_start(f"rs_{self.tag}_d2d_start", self.grads + sib, _rs_d2d_plan(self.n), self.n, carry)
        self.bufs1 = bufs
        return carry

    def add_and_ici_start(self, after, carry):
        bufs = split_wait(f"rs_{self.tag}_d2d_wait", self.s1, self.r1, self.bufs1, _rs_d2d_plan(self.n), after)
        self.grads, self.sib = bufs[:self.n], bufs[self.n:]
        pb = [rs_add(g, s, self.place, tr, f"rs_{self.tag}_add{w}")
              for w, (g, s, tr) in enumerate(zip(self.grads, self.sib, self.rows))]
        rb = [lax.empty((3,) + p.shape[1:], BF16) for p in pb]
        self.s2, self.r2, self.bufs2, carry = split_start(f"rs_{self.tag}_ici_start", pb + rb, _rs_ici_plan(self.n), 3 * self.n, carry)
        return carry

    def final_and_share_start(self, after, carry):
        bufs = split_wait(f"rs_{self.tag}_ici_wait", self.s2, self.r2, self.bufs2, _rs_ici_plan(self.n), after)
        rb = bufs[self.n:]
        full = [rs_final(g, s, r, self.place, tr, f"rs_{self.tag}_final{w}")
                for w, (g, s, r, tr) in enumerate(zip(self.grads, self.sib, rb, self.rows))]
        self.s3, self.r3, self.bufs3, carry = split_start(f"rs_{self.tag}_share_start", full, _rs_share_plan(self.n), self.n, carry)
        return carry

    def finish(self, after):
        return split_wait(f"rs_{self.tag}_share_wait", self.s3, self.r3, self.bufs3, _rs_share_plan(self.n), after)


def _rope_tables(positions):
    inv_freq = ROPE_THETA ** (-jnp.arange(0, ROT_DIM, 2, dtype=F32) / ROT_DIM)
    ang = positions.astype(F32)[:, None] * inv_freq
    cos, sin = jnp.cos(ang), jnp.sin(ang)
    S = positions.shape[0]
    one, zero = jnp.ones((S, 48), F32), jnp.zeros((S, 48), F32)
    z8 = jnp.zeros((S, 8), F32)
    tc = jnp.concatenate([cos, cos, one], axis=1)
    tsa = jnp.concatenate([z8, sin, zero], axis=1)
    tsb = jnp.concatenate([-sin, z8, zero], axis=1)
    return tuple(jnp.tile(t, (1, 2)) for t in (tc, tsa, tsb))


def _block_diag(w_pool):
    wbd = jnp.zeros((POOL_W, POOL_W), F32)
    for gi in range(4):
        wbd = wbd.at[gi * 64:(gi + 1) * 64, gi * 64:(gi + 1) * 64].set(w_pool[gi])
    return wbd


def kernel(x, c, positions, w_ada, b_ada, g_pre_mix, g_post_mix, g_pre_ffn, g_post_ffn, w_in, w_pool, b_pool, pool_scale, w_out, w_up, conv_w, conv_b, w_down, loss_target, m_w_ada, m_b_ada, m_g_pre_mix, m_g_post_mix, m_g_pre_ffn, m_g_post_ffn, m_w_in, m_w_pool, m_b_pool, m_pool_scale, m_w_out, m_w_up, m_conv_w, m_conv_b, m_w_down, v_w_ada, v_b_ada, v_g_pre_mix, v_g_post_mix, v_g_pre_ffn, v_g_post_ffn, v_w_in, v_w_pool, v_b_pool, v_pool_scale, v_w_out, v_w_up, v_conv_w, v_conv_b, v_w_down):
    xi, yi, ci = lax.axis_index("x"), lax.axis_index("y"), lax.axis_index("c")
    chip = 2 * xi + yi
    place = jnp.stack([ci, chip]).astype(jnp.int32)
    x2, tgt = x[0], loss_target[0]
    S = x2.shape[0]

    def landing(s_):
        return lax.dynamic_update_slice(lax.empty((N_CHIPS,) + s_.shape, s_.dtype), s_[None], (chip, 0, 0))

    cb_ada = w_ada.shape[2]
    b_cols = lax.dynamic_slice(b_ada, (0, chip * cb_ada), (1, cb_ada))
    c_all, mod6, conv_w_g = ada_mod(c.reshape(1, 1, D_MODEL), w_ada[0], b_cols, conv_w[0])
    conv_w_f = jnp.transpose(conv_w_g, (1, 0, 2)).reshape(3, D_FF)
    mix_sh = [w_in[0].astype(BF16), w_out[0].astype(BF16)]
    ffn_sh = [w_up[0].astype(BF16), w_down[0].astype(BF16)]
    ga_s, ga_r, ga_bufs, mod6 = split_start("gather_mix_ici_start", mix_sh + [landing(t) for t in mix_sh], _gather_ici_plan(2), 6, mod6)
    gb_s, gb_r, gb_bufs, mod6 = split_start("gather_ffn_ici_start", ffn_sh + [landing(t) for t in ffn_sh], _gather_ici_plan(2), 6, mod6)
    tc, tsa, tsb = _rope_tables(positions[0])
    wbd = _block_diag(w_pool[0]).astype(BF16)
    b_pool2, scale2 = b_pool.reshape(1, POOL_W), pool_scale
    glue_done = tc[:8] + tsa[:8] + tsb[:8] + wbd[:8, :128].astype(F32)
    ga_bufs = split_wait("gather_mix_ici_wait", ga_s, ga_r, ga_bufs, _gather_ici_plan(2), glue_done)
    gc_s, gc_r, mix_land, mod6 = split_start("gather_mix_d2d_start", ga_bufs[2:], _gather_d2d_plan(2), 6, mod6)
    w_in_g, w_out_g = split_wait("gather_mix_d2d_wait", gc_s, gc_r, mix_land, _gather_d2d_plan(2), mod6)

    h1, u, *qkv = inproj_fwd(x2, g_pre_mix, mod6, w_in_g, tc, tsa, tsb)
    mixed, pool = pool_fwd(u, wbd, b_pool2, scale2)
    o_l = [attn_fwd(t, d) for t, d in zip(qkv, DILATIONS)]
    attn_done = sum(l[0, :8, :128] for _, l in o_l)
    gb_bufs = split_wait("gather_ffn_ici_wait", gb_s, gb_r, gb_bufs, _gather_ici_plan(2), attn_done)
    gd_s, gd_r, ffn_land, pool = split_start("gather_ffn_d2d_start", gb_bufs[2:], _gather_d2d_plan(2), 6, pool)
    cat, lse, lse4, lse16, y1, x1, h2 = outproj_fwd([o for o, _ in o_l] + [l for _, l in o_l], pool, x2, w_out_g, g_post_mix,
                                                    g_pre_ffn, mod6)
    lses = [lse[None], lse4, lse16]
    w_up_g, w_down_g = split_wait("gather_ffn_d2d_wait", gd_s, gd_r, ffn_land, _gather_d2d_plan(2), h2)
    w_down_f = w_down_g.reshape(D_FF, D_MODEL)
    gate, val, a, dy2, dout, loss_v, d_gt_f, d_g_post_ffn = ffn_fwd(h2, w_up_g, conv_w_f, conv_b, w_down_f, x1, tgt, g_post_ffn, mod6)

    dgc, dval, d_conv_w, d_conv_b, dw_down, dw_up = down_bwd(dy2, w_down_f, gate, val, conv_w_f, conv_b, a, h2)
    dx1, dy1, d_sh_f, d_sc_f, d_g_pre_ffn, d_gt_m, d_g_post_mix, dw_up = up_bwd(
        dgc, dval, conv_w_f, w_up_g, x1, dout, y1, g_pre_ffn, g_post_mix, mod6, h2, dw_up)
    rs_ffn = GradReduce("ffn", [dw_up, dw_down.reshape(N_CHIPS, D_FF // N_CHIPS, D_MODEL)], [256, 176], place)
    dy1 = rs_ffn.d2d_start(dy1)
    dpool, da1, da4, da16, dl1, dl4, dl16, dw_out = outproj_bwd(dy1, w_out_g, cat)
    dpool = rs_ffn.add_and_ici_start(dw_out, dpool)
    du, d_wbd, d_b_pool, d_scale = pool_bwd(dpool, mixed, wbd, b_pool2, scale2)
    dqkv = [attn_bwd(t, da, ls, dl, d) for t, da, ls, dl, d in zip(qkv, (da1[None], da4, da16), lses, (dl1[None], dl4, dl16), DILATIONS)]
    grad_x, d_sh_m, d_sc_m, d_g_pre_mix, dw_in = inproj_bwd(dqkv, du, x2, dx1, w_in_g, g_pre_mix, mod6, tc, tsa, tsb, h1)

    z1 = jnp.zeros((1, D_MODEL), F32)
    slab_a = jnp.concatenate(
        [d_sh_m, d_sc_m, d_gt_m, d_sh_f, d_sc_f, d_gt_f, d_g_pre_mix, d_g_post_mix, d_g_pre_ffn, d_g_post_ffn,
         jnp.concatenate([d_b_pool, d_scale, loss_v, jnp.zeros((1, 384), F32)], axis=1)] + [z1] * 5, axis=0)
    slab_b = jnp.concatenate([d_conv_w, d_conv_b, jnp.zeros((4, D_FF), F32)], axis=0)
    d_wpool = jnp.concatenate([d_wbd[gi * 64:(gi + 1) * 64, gi * 64:(gi + 1) * 64] for gi in range(4)], axis=0)
    dev = _dev_id(xi, yi, ci)
    small_src = [slab_a, slab_b, d_wpool]
    small_land = [lax.dynamic_update_slice(lax.empty((N_DEV,) + t.shape, F32), t[None], (dev, 0, 0)) for t in small_src]
    tok = jnp.zeros((8, 128), F32)
    gs_s, gs_r, gs_bufs, tok = split_start("small_ici_start", small_src + small_land, _small_ici_plan(3), 12, tok)
    rs_mix = GradReduce("mix", [dw_in, dw_out], [256, 256], place)
    tok = rs_mix.d2d_start(tok)
    tok = rs_ffn.final_and_share_start(tok, tok)
    gs_bufs = split_wait("small_ici_wait", gs_s, gs_r, gs_bufs, _small_ici_plan(3), tok)
    gt_s, gt_r, small_land, tok = split_start("small_d2d_start", gs_bufs[3:], _small_d2d_plan(3), 9, tok)
    tok = rs_mix.add_and_ici_start(tok, tok)
    slab_a_g, slab_b_g, wpool_g = split_wait("small_d2d_wait", gt_s, gt_r, small_land, _small_d2d_plan(3), tok)
    cw_cols = conv_w.shape[2]
    convw_g = lax.dynamic_slice(slab_b_g, (0, 0, chip * cw_cols), (N_DEV, 3, cw_cols))
    dmod_cols = lax.dynamic_slice(slab_a_g[:, :6, :].reshape(N_DEV, 6 * D_MODEL), (0, chip * cb_ada), (N_DEV, cb_ada))

    res = {}

    def big_adamw(name, w, g, m, v, tr):
        g_, d_, m_, v_ = adamw_rows(w[0], g, m[0], v[0], tr, "adamw_" + name)
        res[name] = (g_[None], d_[None], m_[None], v_[None])
        return v_

    g_ada, d_ada, m_ada, v_ada = adamw_ada(c_all.reshape(N_DEV, D_MODEL).T, dmod_cols, w_ada[0], m_w_ada[0], v_w_ada[0])
    res["w_ada"] = (g_ada[None], d_ada[None], m_ada[None], v_ada[None])
    g_w_up, g_w_down = rs_ffn.finish(v_ada)
    big_adamw("w_up", w_up, g_w_up, m_w_up, v_w_up, 256)
    last = big_adamw("w_down", w_down, g_w_down, m_w_down, v_w_down, 352)
    rs_mix.final_and_share_start(last, jnp.zeros((8, 128), F32))
    g_w_in, g_w_out = rs_mix.finish(last)
    big_adamw("w_in", w_in, g_w_in, m_w_in, v_w_in, 256)
    big_adamw("w_out", w_out, g_w_out, m_w_out, v_w_out, 256)
    small, loss_sum = adamw_small(slab_a_g, slab_b_g, convw_g, wpool_g, {
        "b_ada": (b_ada, m_b_ada, v_b_ada), "g_pre_mix": (g_pre_mix, m_g_pre_mix, v_g_pre_mix),
        "g_post_mix": (g_post_mix, m_g_post_mix, v_g_post_mix), "g_pre_ffn": (g_pre_ffn, m_g_pre_ffn, v_g_pre_ffn),
        "g_post_ffn": (g_post_ffn, m_g_post_ffn, v_g_post_ffn), "b_pool": (b_pool, m_b_pool, v_b_pool),
        "pool_scale": (pool_scale, m_pool_scale, v_pool_scale), "conv_b": (conv_b, m_conv_b, v_conv_b),
        "conv_w": (conv_w[0], m_conv_w[0], v_conv_w[0]), "w_pool": (w_pool, m_w_pool, v_w_pool)})
    for name in ("b_ada", "g_pre_mix", "g_post_mix", "g_pre_ffn", "g_post_ffn", "pool_scale", "conv_b", "b_pool", "w_pool"):
        res[name] = tuple(small[name])
    res["conv_w"] = tuple(t[None] for t in small["conv_w"])

    loss = loss_sum[0, 0]
    order = ["w_ada", "b_ada", "g_pre_mix", "g_post_mix", "g_pre_ffn", "g_post_ffn", "w_in", "w_pool", "b_pool", "pool_scale",
             "w_out", "w_up", "conv_w", "conv_b", "w_down"]
    outs = [loss, grad_x[None]]
    for k in range(4):
        outs += [res[n][k] for n in order]
    return tuple(outs)
```

```python
import math

import jax
import jax.numpy as jnp
from jax import lax
from jax.experimental import pallas as pl
from jax.experimental.pallas import tpu as pltpu

F32 = jnp.float32
BF16 = jnp.bfloat16
MESH = pl.DeviceIdType.MESH

D_MODEL = 1024
HEAD_DIM = 64
POOL_W = 256
GROUP_W = 256
DILATIONS = (1, 4, 16)
ATT_BLOCK = 128
IN_W = 2560
D_FF = 2816
HALF_FF = 1408
ROT_DIM = 16
ROPE_THETA = 500000.0
NORM_EPS = 1e-6
N_CHIPS = 4
N_DEV = 8
NEG = -1e30

ADAM_LR = 0.001
ADAM_B1 = 0.9
ADAM_B2 = 0.999
ADAM_EPS = 1e-08
ADAM_WD = 0.01
ADAM_STEP = 10

VMEM_LIMIT = 56 * 1024 * 1024

NT = (((1,), (1,)), ((), ()))
TN = (((0,), (0,)), ((), ()))


def _params(n_grid=0, **kw):
    sem = ("arbitrary",) * n_grid if n_grid else None
    return pltpu.CompilerParams(dimension_semantics=sem, vmem_limit_bytes=VMEM_LIMIT, **kw)


def _full(shape):
    nd = len(shape)
    return pl.BlockSpec(tuple(shape), lambda *_: (0,) * nd, pipeline_mode=pl.Buffered(1))


def _rows(tm, ncol):
    return pl.BlockSpec((tm, ncol), lambda i: (i, 0))


def _acc(ref, val):
    @pl.when(pl.program_id(0) == 0)
    def _():
        ref[...] = jnp.zeros_like(ref)

    ref[...] += val


def _colsum(v):
    return jnp.sum(v, axis=0, keepdims=True)


def _rope128(t, cs, sa, sb, sign):
    return t * cs + sign * (pltpu.roll(t, 8, 1) * sa + pltpu.roll(t, 120, 1) * sb)


FF_CHUNKS = tuple((ch, off, w) for ch in range(2) for off, w in ((0, 512), (512, 512), (1024, 384)))
GELU_C0 = math.sqrt(2.0 / math.pi)
GELU_C1 = GELU_C0 * 0.044715


def _gelu(z):
    z2 = z * z
    t = jnp.tanh(z * (GELU_C0 + GELU_C1 * z2))
    u = 0.5 * t + 0.5
    return z * u, u, t, z2


def _gelu_grad(z, u, t, z2):
    return u + (z * (GELU_C0 + (3.0 * GELU_C1) * z2)) * (0.5 - 0.5 * (t * t))


def _conv_taps(gate, halo, first):
    row = lax.broadcasted_iota(jnp.int32, gate.shape, 0)
    halo = jnp.where(first, 0.0, halo)
    nh = halo.shape[0]
    p1 = halo[nh - 1:nh, :]
    p2 = halo[nh - 2:nh - 1, :]
    g1 = jnp.where(row == 0, p1, pltpu.roll(gate, 1, 0))
    g2 = jnp.where(row == 0, p2, jnp.where(row == 1, p1, pltpu.roll(gate, 2, 0)))
    return g1, g2


def inproj_fwd(x, g, mod6, w_in_g, tc, tsa, tsb, tm=512):
    S = x.shape[0]

    def body(x_ref, g_ref, mod_ref, w_ref, tc_ref, tsa_ref, tsb_ref, h_ref, u_ref, q1_ref, q4_ref, q16_ref, scr):
        qkv_refs = (q1_ref, q4_ref, q16_ref)
        xv = x_ref[...]
        rstd = lax.rsqrt(jnp.mean(xv * xv, axis=-1, keepdims=True) + NORM_EPS)
        h = ((xv * rstd) * g_ref[...]) * (1.0 + mod_ref[1:2, :]) + mod_ref[0:1, :]
        hb = h.astype(BF16)
        h_ref[...] = hb
        cs, sa, sb = tc_ref[...], tsa_ref[...], tsb_ref[...]
        for j in range(N_CHIPS):
            res = jnp.dot(hb, w_ref[j], preferred_element_type=F32)
            for t in range(5):
                sp = 5 * j + t
                piece, half = sp // 2, sp % 2
                blk = res[:, t * 128:(t + 1) * 128]
                lanes = slice(half * 128, (half + 1) * 128)
                if piece == 0:
                    u_ref[:, lanes] = blk
                else:
                    kind, gi = (piece - 1) // 3, (piece - 1) % 3
                    if kind == 0:
                        blk = _rope128(blk, cs, sa, sb, 1.0) * (HEAD_DIM ** -0.5)
                    elif kind == 1:
                        blk = _rope128(blk, cs, sa, sb, 1.0)
                    d = DILATIONS[gi]
                    if d == 1:
                        q1_ref[kind, 0, :, lanes] = blk.astype(BF16)
                    else:
                        scr[...] = blk
                        for r in range(d):
                            qkv_refs[gi][kind, r, :, lanes] = scr[pl.ds(r, tm // d, stride=d), :].astype(BF16)

    cls = lambda d: pl.BlockSpec((3, d, tm // d, GROUP_W), lambda i: (0, 0, i, 0))
    return pl.pallas_call(
        body,
        name="inproj_fwd",
        grid=(S // tm,),
        in_specs=[_rows(tm, D_MODEL), _full((1, D_MODEL)), _full((6, D_MODEL)), _full(w_in_g.shape),
                  _rows(tm, 128), _rows(tm, 128), _rows(tm, 128)],
        out_specs=[_rows(tm, D_MODEL), _rows(tm, POOL_W)] + [cls(d) for d in DILATIONS],
        out_shape=[jax.ShapeDtypeStruct((S, D_MODEL), BF16), jax.ShapeDtypeStruct((S, POOL_W), F32)]
        + [jax.ShapeDtypeStruct((3, d, S // d, GROUP_W), BF16) for d in DILATIONS],
        scratch_shapes=[pltpu.VMEM((tm, 128), F32)],
        compiler_params=_params(1),
    )(x, g, mod6, w_in_g, tc, tsa, tsb)


def _attn_masks():
    row = lax.broadcasted_iota(jnp.int32, (2 * ATT_BLOCK, 2 * ATT_BLOCK), 0) % ATT_BLOCK
    col = lax.broadcasted_iota(jnp.int32, (2 * ATT_BLOCK, 2 * ATT_BLOCK), 1)
    band = (col >= row) & (col <= row + ATT_BLOCK)
    lane = lax.broadcasted_iota(jnp.int32, (ATT_BLOCK, 128), 1)
    return band, col, lane < HEAD_DIM


def _stack_heads(t, lo):
    z = jnp.zeros_like(t)
    return jnp.concatenate([jnp.where(lo, t, z), jnp.where(lo, z, t)], axis=0)


def _unstack_heads(t2, lo):
    return jnp.where(lo, t2[:ATT_BLOCK], t2[ATT_BLOCK:])


def attn_fwd(qkv, d):
    L = qkv.shape[2]
    nb = L // ATT_BLOCK

    def body(q_ref, k_ref, v_ref, o_ref, l_ref, kpad, vpad):
        kpad[0:ATT_BLOCK, :] = jnp.zeros((ATT_BLOCK, GROUP_W), BF16)
        vpad[0:ATT_BLOCK, :] = jnp.zeros((ATT_BLOCK, GROUP_W), BF16)
        kpad[ATT_BLOCK:, :] = k_ref[...]
        vpad[ATT_BLOCK:, :] = v_ref[...]
        band, col, lo = _attn_masks()

        def step(n, carry):
            r0 = pl.multiple_of(n * ATT_BLOCK, ATT_BLOCK)
            valid = band & ((col >= ATT_BLOCK) | (n > 0))
            qb = q_ref[pl.ds(r0, ATT_BLOCK), :]
            kb = kpad[pl.ds(r0, 2 * ATT_BLOCK), :]
            vb = vpad[pl.ds(r0, 2 * ATT_BLOCK), :]
            for pair in range(2):
                lanes = slice(pair * 128, (pair + 1) * 128)
                qp, kp, vp = qb[:, lanes], kb[:, lanes], vb[:, lanes]
                s = lax.dot_general(_stack_heads(qp, lo), kp, NT, preferred_element_type=F32)
                s = jnp.where(valid, s, NEG)
                m = jnp.max(s, axis=1, keepdims=True)
                p = jnp.exp(s - m)
                den = jnp.sum(p, axis=1, keepdims=True)
                pv = jnp.dot(p.astype(BF16), vp, preferred_element_type=F32)
                o_ref[pl.ds(r0, ATT_BLOCK), lanes] = _unstack_heads(pv / den, lo)
                l_ref[pl.ds(r0, ATT_BLOCK), lanes] = _unstack_heads(jnp.broadcast_to(m + jnp.log(den), pv.shape), lo)
            return carry

        lax.fori_loop(0, nb, step, 0, unroll=min(4, nb))

    spec = lambda kind: pl.BlockSpec((None, None, L, GROUP_W), lambda r: (kind, r, 0, 0))
    return pl.pallas_call(
        body,
        name=f"attn_fwd_d{d}",
        grid=(d,),
        in_specs=[spec(0), spec(1), spec(2)],
        out_specs=[pl.BlockSpec((None, L, GROUP_W), lambda r: (r, 0, 0))] * 2,
        out_shape=[jax.ShapeDtypeStruct((d, L, GROUP_W), F32)] * 2,
        scratch_shapes=[pltpu.VMEM((L + ATT_BLOCK, GROUP_W), BF16)] * 2,
        compiler_params=_params(1),
    )(qkv, qkv, qkv)


def _pool_lane_windows(shape):
    lane = lax.broadcasted_iota(jnp.int32, shape, 1)
    return lane, jnp.where(lane < 64, 2, jnp.where(lane < 128, 4, jnp.where(lane < 192, 8, 16)))


def pool_fwd(u, wbd, b, scale):
    S = u.shape[0]

    def body(u_ref, w_ref, b_ref, s_ref, mixed_ref, out_ref):
        uv = u_ref[...]
        row = lax.broadcasted_iota(jnp.int32, uv.shape, 0)
        lane, win = _pool_lane_windows(uv.shape)

        def shift(a, k):
            return jnp.where(row >= k, pltpu.roll(a, k, 0), 0.0)

        s2 = uv + shift(uv, 1)
        s4 = s2 + shift(s2, 2)
        s8 = s4 + shift(s4, 4)
        s16 = s8 + shift(s8, 8)
        tsum = jnp.where(lane < 64, s2, jnp.where(lane < 128, s4, jnp.where(lane < 192, s8, s16)))
        cnt = jnp.minimum(row + 1, win).astype(F32)
        mb = (tsum / cnt - uv).astype(BF16)
        mixed_ref[...] = mb
        y = jnp.dot(mb, w_ref[...], preferred_element_type=F32) + b_ref[...]
        out_ref[...] = (y * s_ref[...]).astype(BF16)

    vm = pl.BlockSpec(memory_space=pltpu.VMEM)
    return pl.pallas_call(
        body,
        name="pool_fwd",
        in_specs=[vm] * 4,
        out_specs=[vm] * 2,
        out_shape=[jax.ShapeDtypeStruct((S, POOL_W), BF16)] * 2,
        compiler_params=_params(),
    )(u, wbd, b, scale)


def outproj_fwd(o_l, pool, x, w_out_g, g_post, g_pre, mod6, tm=512):
    S = x.shape[0]

    def body(o0, o1, o2, l0, l1, l2, pool_ref, x_ref, w_ref, gpost_ref, gpre_ref, mod_ref,
             cat_ref, lse_ref, lse4_ref, lse16_ref, y1_ref, x1_ref, h2_ref, so4, sl4, so16, sl16):
        for d, src, dst in ((4, o1, so4), (4, l1, sl4), (16, o2, so16), (16, l2, sl16)):
            for r in range(d):
                for h in range(2):
                    dst[h, pl.ds(r, tm // d, stride=d), :] = src[r, :, h * 128:(h + 1) * 128]
        nat = lambda ref: jnp.concatenate([ref[0], ref[1]], axis=1)
        a, b, c = l0[0], nat(sl4), nat(sl16)
        m = jnp.maximum(jnp.maximum(a, b), c)
        e0, e1, e2 = jnp.exp(a - m), jnp.exp(b - m), jnp.exp(c - m)
        z = e0 + e1 + e2
        lse = m + jnp.log(z)
        lse_ref[...] = lse
        for h in range(2):
            sl4[h] = lse[:, h * 128:(h + 1) * 128]
        for d, dst in ((4, lse4_ref), (16, lse16_ref)):
            for r in range(d):
                for h in range(2):
                    dst[r, :, h * 128:(h + 1) * 128] = sl4[h, pl.ds(r, tm // d, stride=d), :]
        attn = (e0 * o0[0] + e1 * nat(so4) + e2 * nat(so16)) / z
        cat = jnp.concatenate([pool_ref[...], attn.astype(BF16)], axis=1)
        cat_ref[...] = cat
        y1 = jnp.concatenate([jnp.dot(cat, w_ref[j], preferred_element_type=F32) for j in range(N_CHIPS)], axis=1)
        y1_ref[...] = y1
        rstd = lax.rsqrt(jnp.mean(y1 * y1, axis=-1, keepdims=True) + NORM_EPS)
        x1 = x_ref[...] + mod_ref[2:3, :] * ((y1 * rstd) * gpost_ref[...])
        x1_ref[...] = x1
        rstd2 = lax.rsqrt(jnp.mean(x1 * x1, axis=-1, keepdims=True) + NORM_EPS)
        h2 = ((x1 * rstd2) * gpre_ref[...]) * (1.0 + mod_ref[4:5, :]) + mod_ref[3:4, :]
        h2_ref[...] = h2.astype(BF16)

    t256 = _rows(tm, GROUP_W)
    cls = lambda d: pl.BlockSpec((d, tm // d, GROUP_W), lambda i: (0, i, 0))
    cls_shape = lambda d: jax.ShapeDtypeStruct((d, S // d, GROUP_W), F32)
    return pl.pallas_call(
        body,
        name="outproj_fwd",
        grid=(S // tm,),
        in_specs=[cls(d) for d in DILATIONS] * 2 + [t256, _rows(tm, D_MODEL), _full(w_out_g.shape), _full((1, D_MODEL)),
                                                    _full((1, D_MODEL)), _full((6, D_MODEL))],
        out_specs=[_rows(tm, 512), t256, cls(4), cls(16), _rows(tm, D_MODEL), _rows(tm, D_MODEL), _rows(tm, D_MODEL)],
        out_shape=[jax.ShapeDtypeStruct((S, 512), BF16), jax.ShapeDtypeStruct((S, GROUP_W), F32), cls_shape(4), cls_shape(16),
                   jax.ShapeDtypeStruct((S, D_MODEL), F32), jax.ShapeDtypeStruct((S, D_MODEL), F32),
                   jax.ShapeDtypeStruct((S, D_MODEL), BF16)],
        scratch_shapes=[pltpu.VMEM((2, tm, 128), F32)] * 4,
        compiler_params=_params(1),
    )(*o_l, pool, x, w_out_g, g_post, g_pre, mod6)


def _halo_prev(tm, ncol):
    return pl.BlockSpec((16, ncol), lambda i: (jnp.maximum(i * (tm // 16) - 1, 0), 0))


def ffn_fwd(h2, w_up_g, conv_w, conv_b, w_down, x1, target, g_post, mod6, tm=256):
    S = x1.shape[0]

    def body(h_ref, wu_ref, cw_ref, cb_ref, wd_ref, x1_ref, tgt_ref, g_ref, mod_ref,
             gate_ref, val_ref, a_ref, dy2_ref, dout_ref, loss_ref, dgt_ref, dg_ref, carry):
        first = pl.program_id(0) == 0

        @pl.when(first)
        def _():
            carry[...] = jnp.zeros_like(carry)

        hb = h_ref[...]
        y2 = jnp.zeros((tm, D_MODEL), F32)
        for ch in range(2):
            cols = slice(ch * HALF_FF, (ch + 1) * HALF_FF)
            gb = jnp.dot(hb, wu_ref[ch], preferred_element_type=F32).astype(BF16)
            vb = jnp.dot(hb, wu_ref[2 + ch], preferred_element_type=F32).astype(BF16)
            gate_ref[:, cols] = gb
            val_ref[:, cols] = vb
            gt = gb.astype(F32)
            g1, g2 = _conv_taps(gt, carry[:, cols], first)
            carry[:, cols] = gt[tm - 8:, :]
            gc = g2 * cw_ref[0:1, cols] + g1 * cw_ref[1:2, cols] + gt * cw_ref[2:3, cols] + cb_ref[:, cols]
            ab = (_gelu(gc)[0] * vb.astype(F32)).astype(BF16)
            a_ref[:, cols] = ab
            y2 = y2 + jnp.dot(ab, wd_ref[cols, :], preferred_element_type=F32)
        rstd = lax.rsqrt(jnp.mean(y2 * y2, axis=-1, keepdims=True) + NORM_EPS)
        y2n = y2 * rstd
        gv = g_ref[...]
        gtf = mod_ref[5:6, :]
        r2 = y2n * gv
        diff = (x1_ref[...] + gtf * r2) - tgt_ref[...]
        _acc(loss_ref, jnp.zeros((1, 128), F32) + 0.5 * jnp.sum(diff * diff) * (1.0 / D_MODEL))
        dout = diff * (1.0 / D_MODEL)
        dout_ref[...] = dout
        _acc(dgt_ref, _colsum(dout * r2))
        dr2 = dout * gtf
        _acc(dg_ref, _colsum(dr2 * y2n))
        dyn = dr2 * gv
        dy2 = rstd * (dyn - y2n * jnp.mean(dyn * y2n, axis=-1, keepdims=True))
        dy2_ref[...] = dy2.astype(BF16)

    vec = _full((1, D_MODEL))
    return pl.pallas_call(
        body,
        name="ffn_fwd",
        grid=(S // tm,),
        in_specs=[_rows(tm, D_MODEL), _full(w_up_g.shape), _full((3, D_FF)), _full((1, D_FF)), _full((D_FF, D_MODEL)),
                  _rows(tm, D_MODEL), _rows(tm, D_MODEL), vec, _full((6, D_MODEL))],
        out_specs=[_rows(tm, D_FF), _rows(tm, D_FF), _rows(tm, D_FF), _rows(tm, D_MODEL), _rows(tm, D_MODEL), _full((1, 128)), vec, vec],
        out_shape=[jax.ShapeDtypeStruct((S, D_FF), BF16)] * 3 + [jax.ShapeDtypeStruct((S, D_MODEL), BF16),
                                                                 jax.ShapeDtypeStruct((S, D_MODEL), F32),
                                                                 jax.ShapeDtypeStruct((1, 128), F32),
                                                                 jax.ShapeDtypeStruct((1, D_MODEL), F32),
                                                                 jax.ShapeDtypeStruct((1, D_MODEL), F32)],
        scratch_shapes=[pltpu.VMEM((8, D_FF), F32)],
        compiler_params=_params(1),
    )(h2, w_up_g, conv_w, conv_b, w_down, x1, target, g_post, mod6)


def down_bwd(dy2, w_down, gate, val, conv_w, conv_b, a, h2, tm=256):
    S = dy2.shape[0]

    def body(dy_ref, w_ref, gate_ref, halo_ref, val_ref, cw_ref, cb_ref, a_ref, h_ref,
             dgc_ref, dval_ref, dcw_ref, dcb_ref, dwd_ref, dwu_ref):
        first = pl.program_id(0) == 0

        @pl.when(first)
        def _():
            dcw_ref[...] = jnp.zeros_like(dcw_ref)
            dcb_ref[...] = jnp.zeros_like(dcb_ref)
            dwd_ref[...] = jnp.zeros_like(dwd_ref)
            dwu_ref[...] = jnp.zeros_like(dwu_ref)

        dyb = dy_ref[...]
        hb = h_ref[...]
        def col(i):
            ch, off, width = FF_CHUNKS[i]
            return slice(ch * HALF_FF + off, ch * HALF_FF + off + width)

        def mm_da(i):
            return lax.dot_general(dyb, w_ref[col(i), :], NT, preferred_element_type=F32)

        def elementwise(i, da):
            cols = col(i)
            gt = gate_ref[:, cols].astype(F32)
            g1, g2 = _conv_taps(gt, halo_ref[:, cols].astype(F32), first)
            gc = g2 * cw_ref[0:1, cols] + g1 * cw_ref[1:2, cols] + gt * cw_ref[2:3, cols] + cb_ref[:, cols]
            ge, u, th, z2 = _gelu(gc)
            dgc = da * val_ref[:, cols].astype(F32) * _gelu_grad(gc, u, th, z2)
            dgc_ref[:, cols] = dgc.astype(BF16)
            dvb = (da * ge).astype(BF16)
            dval_ref[:, cols] = dvb
            dcb_ref[:, cols] += _colsum(dgc)
            dcw_ref[0:1, cols] += _colsum(dgc * g2)
            dcw_ref[1:2, cols] += _colsum(dgc * g1)
            dcw_ref[2:3, cols] += _colsum(dgc * gt)
            return dvb

        def mm_dw(i, dvb):
            ch, off, width = FF_CHUNKS[i]
            dwd_ref[col(i), :] += lax.dot_general(a_ref[:, col(i)], dyb, TN, preferred_element_type=F32)
            dwu_ref[ch, :, off:off + width] += lax.dot_general(hb, dvb, TN, preferred_element_type=F32)

        n = len(FF_CHUNKS)
        da = mm_da(0)
        prev = None
        for i in range(n):
            nxt = mm_da(i + 1) if i + 1 < n else None
            if prev is not None:
                mm_dw(i - 1, prev)
            prev = elementwise(i, da)
            da = nxt
        mm_dw(n - 1, prev)

    return pl.pallas_call(
        body,
        name="down_bwd",
        grid=(S // tm,),
        in_specs=[_rows(tm, D_MODEL), _full((D_FF, D_MODEL)), _rows(tm, D_FF), _halo_prev(tm, D_FF), _rows(tm, D_FF),
                  _full((3, D_FF)), _full((1, D_FF)), _rows(tm, D_FF), _rows(tm, D_MODEL)],
        out_specs=[_rows(tm, D_FF), _rows(tm, D_FF), _full((3, D_FF)), _full((1, D_FF)), _full((D_FF, D_MODEL)),
                   pl.BlockSpec((2, D_MODEL, HALF_FF), lambda i: (1, 0, 0), pipeline_mode=pl.Buffered(1))],
        out_shape=[jax.ShapeDtypeStruct((S, D_FF), BF16), jax.ShapeDtypeStruct((S, D_FF), BF16),
                   jax.ShapeDtypeStruct((3, D_FF), F32), jax.ShapeDtypeStruct((1, D_FF), F32),
                   jax.ShapeDtypeStruct((D_FF, D_MODEL), F32), jax.ShapeDtypeStruct((N_CHIPS, D_MODEL, HALF_FF), F32)],
        compiler_params=_params(1),
    )(dy2, w_down, gate, gate, val, conv_w, conv_b, a, h2)


def up_bwd(dgc, dval, conv_w, w_up_g, x1, dout, y1, g_pre, g_post, mod6, h2, dw_up, tm=256):
    S = x1.shape[0]
    last_blk = S // 16 - 1

    def body(dgc_ref, nxt_ref, dval_ref, cw_ref, w_ref, x1_ref, dout_ref, y1_ref, gpre_ref, gpost_ref, mod_ref, h_ref, dwin_ref,
             dx1_ref, dy1_ref, dsh_ref, dsc_ref, dgpre_ref, dgt_ref, dgpost_ref, dwu_ref):
        last = pl.program_id(0) == pl.num_programs(0) - 1

        @pl.when(pl.program_id(0) == 0)
        def _():
            dwu_ref[...] = jnp.zeros_like(dwu_ref)

        hb = h_ref[...]
        dh = jnp.zeros((tm, D_MODEL), F32)
        for ch in range(2):
            cols = slice(ch * HALF_FF, (ch + 1) * HALF_FF)
            dg = dgc_ref[:, cols].astype(F32)
            nx = jnp.where(last, 0.0, nxt_ref[:, cols].astype(F32))
            row = lax.broadcasted_iota(jnp.int32, dg.shape, 0)
            n0, n1 = nx[0:1, :], nx[1:2, :]
            u1 = jnp.where(row == tm - 1, n0, pltpu.roll(dg, tm - 1, 0))
            u2 = jnp.where(row == tm - 1, n1, jnp.where(row == tm - 2, n0, pltpu.roll(dg, tm - 2, 0)))
            dgate = (dg * cw_ref[2:3, cols] + u1 * cw_ref[1:2, cols] + u2 * cw_ref[0:1, cols]).astype(BF16)
            dwu_ref[ch] += lax.dot_general(hb, dgate, TN, preferred_element_type=F32)
            dh = dh + lax.dot_general(dgate, w_ref[ch], NT, preferred_element_type=F32)
            dh = dh + lax.dot_general(dval_ref[:, cols], w_ref[2 + ch], NT, preferred_element_type=F32)
        x1 = x1_ref[...]
        rstd = lax.rsqrt(jnp.mean(x1 * x1, axis=-1, keepdims=True) + NORM_EPS)
        n2 = x1 * rstd
        gpre = gpre_ref[...]
        one_sc = 1.0 + mod_ref[4:5, :]
        _acc(dsh_ref, _colsum(dh))
        _acc(dsc_ref, _colsum(dh * (n2 * gpre)))
        _acc(dgpre_ref, _colsum(dh * one_sc * n2))
        dn = dh * (gpre * one_sc)
        dx1 = dout_ref[...] + rstd * (dn - n2 * jnp.mean(dn * n2, axis=-1, keepdims=True))
        dx1_ref[...] = dx1
        y1 = y1_ref[...]
        rstd1 = lax.rsqrt(jnp.mean(y1 * y1, axis=-1, keepdims=True) + NORM_EPS)
        y1n = y1 * rstd1
        gpost = gpost_ref[...]
        gtm = mod_ref[2:3, :]
        _acc(dgt_ref, _colsum(dx1 * (y1n * gpost)))
        dr1 = dx1 * gtm
        _acc(dgpost_ref, _colsum(dr1 * y1n))
        dyn = dr1 * gpost
        dy1 = rstd1 * (dyn - y1n * jnp.mean(dyn * y1n, axis=-1, keepdims=True))
        dy1_ref[...] = dy1.astype(BF16)

    vec = _full((1, D_MODEL))
    nxt = pl.BlockSpec((16, D_FF), lambda i: (jnp.minimum((i + 1) * (tm // 16), last_blk), 0))
    return pl.pallas_call(
        body,
        name="up_bwd",
        grid=(S // tm,),
        in_specs=[_rows(tm, D_FF), nxt, _rows(tm, D_FF), _full((3, D_FF)), _full(w_up_g.shape), _rows(tm, D_MODEL),
                  _rows(tm, D_MODEL), _rows(tm, D_MODEL), vec, vec, _full((6, D_MODEL)), _rows(tm, D_MODEL),
                  pl.BlockSpec(memory_space=pl.ANY)],
        out_specs=[_rows(tm, D_MODEL), _rows(tm, D_MODEL), vec, vec, vec, vec, vec,
                   pl.BlockSpec((2, D_MODEL, HALF_FF), lambda i: (0, 0, 0), pipeline_mode=pl.Buffered(1))],
        out_shape=[jax.ShapeDtypeStruct((S, D_MODEL), F32), jax.ShapeDtypeStruct((S, D_MODEL), BF16)]
        + [jax.ShapeDtypeStruct((1, D_MODEL), F32)] * 5 + [jax.ShapeDtypeStruct(dw_up.shape, F32)],
        input_output_aliases={12: 7},
        compiler_params=_params(1),
    )(dgc, dgc, dval, conv_w, w_up_g, x1, dout, y1, g_pre, g_post, mod6, h2, dw_up)


def outproj_bwd(dy1, w_out_g, cat, tm=512):
    S = dy1.shape[0]

    def body(dy_ref, w_ref, cat_ref, dpool_ref, dattn_ref, da4_ref, da16_ref, delta_ref, dl4_ref, dl16_ref, dw_ref, scr):
        @pl.when(pl.program_id(0) == 0)
        def _():
            dw_ref[...] = jnp.zeros_like(dw_ref)

        catb = cat_ref[...]
        dcat = jnp.zeros((tm, 512), F32)
        for j in range(N_CHIPS):
            dyj = dy_ref[:, j * 256:(j + 1) * 256]
            dcat = dcat + lax.dot_general(dyj, w_ref[j], NT, preferred_element_type=F32)
            dw_ref[j] += lax.dot_general(catb, dyj, TN, preferred_element_type=F32)
        dpool_ref[...] = dcat[:, :POOL_W]
        dattn = dcat[:, POOL_W:]
        dattn_ref[...] = dattn.astype(BF16)
        for h in range(2):
            scr[h] = dattn[:, h * 128:(h + 1) * 128]
        for d, dst in ((4, da4_ref), (16, da16_ref)):
            for r in range(d):
                for h in range(2):
                    dst[r, :, h * 128:(h + 1) * 128] = scr[h, pl.ds(r, tm // d, stride=d), :].astype(BF16)
        prod = dattn * catb[:, POOL_W:].astype(F32)
        r = lax.broadcasted_iota(jnp.int32, (GROUP_W, GROUP_W), 0) // HEAD_DIM
        c = lax.broadcasted_iota(jnp.int32, (GROUP_W, GROUP_W), 1) // HEAD_DIM
        ones_bd = jnp.where(r == c, 1.0, 0.0).astype(BF16)
        hi = prod.astype(BF16)
        lo = (prod - hi.astype(F32)).astype(BF16)
        delta = jnp.dot(hi, ones_bd, preferred_element_type=F32) + jnp.dot(lo, ones_bd, preferred_element_type=F32)
        delta_ref[...] = delta
        for h in range(2):
            scr[h] = delta[:, h * 128:(h + 1) * 128]
        for d, dst in ((4, dl4_ref), (16, dl16_ref)):
            for r in range(d):
                for h in range(2):
                    dst[r, :, h * 128:(h + 1) * 128] = scr[h, pl.ds(r, tm // d, stride=d), :]

    cls = lambda d: pl.BlockSpec((d, tm // d, GROUP_W), lambda i: (0, i, 0))
    cls_shape = lambda d, dt: jax.ShapeDtypeStruct((d, S // d, GROUP_W), dt)
    return pl.pallas_call(
        body,
        name="outproj_bwd",
        grid=(S // tm,),
        in_specs=[_rows(tm, D_MODEL), _full(w_out_g.shape), _rows(tm, 512)],
        out_specs=[_rows(tm, POOL_W), _rows(tm, GROUP_W), cls(4), cls(16), _rows(tm, GROUP_W), cls(4), cls(16),
                   _full(w_out_g.shape)],
        out_shape=[jax.ShapeDtypeStruct((S, POOL_W), F32), jax.ShapeDtypeStruct((S, GROUP_W), BF16), cls_shape(4, BF16),
                   cls_shape(16, BF16), jax.ShapeDtypeStruct((S, GROUP_W), F32), cls_shape(4, F32), cls_shape(16, F32),
                   jax.ShapeDtypeStruct(w_out_g.shape, F32)],
        scratch_shapes=[pltpu.VMEM((2, tm, 128), F32)],
        compiler_params=_params(1),
    )(dy1, w_out_g, cat)


def attn_bwd(qkv, dattn, lse, delta, d):
    L = qkv.shape[2]
    nb = L // ATT_BLOCK

    def body(q_ref, k_ref, v_ref, do_ref, l_ref, dl_ref, out_ref, kpad, vpad, dkpad, dvpad):
        kpad[0:ATT_BLOCK, :] = jnp.zeros((ATT_BLOCK, GROUP_W), BF16)
        vpad[0:ATT_BLOCK, :] = jnp.zeros((ATT_BLOCK, GROUP_W), BF16)
        kpad[ATT_BLOCK:, :] = k_ref[...]
        vpad[ATT_BLOCK:, :] = v_ref[...]
        dkpad[...] = jnp.zeros_like(dkpad)
        dvpad[...] = jnp.zeros_like(dvpad)
        band, col, lo = _attn_masks()

        def step(n, carry):
            r0 = pl.multiple_of(n * ATT_BLOCK, ATT_BLOCK)
            valid = band & ((col >= ATT_BLOCK) | (n > 0))
            qb = q_ref[pl.ds(r0, ATT_BLOCK), :]
            dob = do_ref[pl.ds(r0, ATT_BLOCK), :]
            lb = l_ref[pl.ds(r0, ATT_BLOCK), :]
            dlb = dl_ref[pl.ds(r0, ATT_BLOCK), :]
            kb = kpad[pl.ds(r0, 2 * ATT_BLOCK), :]
            vb = vpad[pl.ds(r0, 2 * ATT_BLOCK), :]
            for pair in range(2):
                lanes = slice(pair * 128, (pair + 1) * 128)
                qp, dop, kp, vp = qb[:, lanes], dob[:, lanes], kb[:, lanes], vb[:, lanes]
                c0, c1 = pair * 128, pair * 128 + HEAD_DIM
                q2, do2 = _stack_heads(qp, lo), _stack_heads(dop, lo)
                lse2 = jnp.concatenate([lb[:, c0:c0 + 1], lb[:, c1:c1 + 1]], axis=0)
                dl2 = jnp.concatenate([dlb[:, c0:c0 + 1], dlb[:, c1:c1 + 1]], axis=0)
                s = lax.dot_general(q2, kp, NT, preferred_element_type=F32)
                s = jnp.where(valid, s, NEG)
                p = jnp.exp(s - lse2)
                dp = lax.dot_general(do2, vp, NT, preferred_element_type=F32)
                ds = (p * (dp - dl2)).astype(BF16)
                dq2 = jnp.dot(ds, kp, preferred_element_type=F32)
                out_ref[0, pl.ds(r0, ATT_BLOCK), lanes] = _unstack_heads(dq2, lo)
                dkpad[pl.ds(r0, 2 * ATT_BLOCK), lanes] += lax.dot_general(ds, q2, TN, preferred_element_type=F32)
                dvpad[pl.ds(r0, 2 * ATT_BLOCK), lanes] += lax.dot_general(p.astype(BF16), do2, TN, preferred_element_type=F32)
            return carry

        lax.fori_loop(0, nb, step, 0, unroll=min(4, nb))
        out_ref[1] = dkpad[ATT_BLOCK:, :]
        out_ref[2] = dvpad[ATT_BLOCK:, :]

    spec = lambda kind: pl.BlockSpec((None, None, L, GROUP_W), lambda r: (kind, r, 0, 0))
    cls = pl.BlockSpec((None, L, GROUP_W), lambda r: (r, 0, 0))
    return pl.pallas_call(
        body,
        name=f"attn_bwd_d{d}",
        grid=(d,),
        in_specs=[spec(0), spec(1), spec(2), cls, cls, cls],
        out_specs=pl.BlockSpec((3, None, L, GROUP_W), lambda r: (0, r, 0, 0)),
        out_shape=jax.ShapeDtypeStruct((3, d, L, GROUP_W), F32),
        scratch_shapes=[pltpu.VMEM((L + ATT_BLOCK, GROUP_W), BF16)] * 2 + [pltpu.VMEM((L + ATT_BLOCK, GROUP_W), F32)] * 2,
        compiler_params=_params(1),
    )(qkv, qkv, qkv, dattn, lse, delta)


def pool_bwd(dpool, mixed, wbd, b, scale):
    S = dpool.shape[0]

    def body(dp_ref, mx_ref, w_ref, b_ref, s_ref, du_ref, dw_ref, db_ref, ds_ref):
        dp = dp_ref[...]
        mb = mx_ref[...]
        wv = w_ref[...]
        ypre = jnp.dot(mb, wv, preferred_element_type=F32) + b_ref[...]
        ds_ref[...] = _colsum(dp * ypre)
        dpre = dp * s_ref[...]
        db_ref[...] = _colsum(dpre)
        dpb = dpre.astype(BF16)
        dw_ref[...] = lax.dot_general(mb, dpb, TN, preferred_element_type=F32)
        dmix = lax.dot_general(dpb, wv, NT, preferred_element_type=F32)
        row = lax.broadcasted_iota(jnp.int32, dmix.shape, 0)
        lane, win = _pool_lane_windows(dmix.shape)
        e = dmix / jnp.minimum(row + 1, win).astype(F32)

        def shift(a, k):
            return jnp.where(row < S - k, pltpu.roll(a, S - k, 0), 0.0)

        f2 = e + shift(e, 1)
        f4 = f2 + shift(f2, 2)
        f8 = f4 + shift(f4, 4)
        f16 = f8 + shift(f8, 8)
        du_ref[...] = jnp.where(lane < 64, f2, jnp.where(lane < 128, f4, jnp.where(lane < 192, f8, f16))) - dmix

    vm = pl.BlockSpec(memory_space=pltpu.VMEM)
    return pl.pallas_call(
        body,
        name="pool_bwd",
        in_specs=[vm] * 5,
        out_specs=[vm] * 4,
        out_shape=[jax.ShapeDtypeStruct((S, POOL_W), F32), jax.ShapeDtypeStruct((POOL_W, POOL_W), F32),
                   jax.ShapeDtypeStruct((1, POOL_W), F32), jax.ShapeDtypeStruct((1, POOL_W), F32)],
        compiler_params=_params(),
    )(dpool, mixed, wbd, b, scale)


def inproj_bwd(dqkv, du, x, dx1, w_in_g, g, mod6, tc, tsa, tsb, h1, tm=512):
    S = x.shape[0]

    def body(d0, d1, d2, du_ref, x_ref, dx1_ref, w_ref, g_ref, mod_ref, tc_ref, tsa_ref, tsb_ref, h_ref,
             gx_ref, dsh_ref, dsc_ref, dg_ref, dw_ref, s4, s16, dp_ref):
        @pl.when(pl.program_id(0) == 0)
        def _():
            dw_ref[...] = jnp.zeros_like(dw_ref)

        cs, sa, sb = tc_ref[...], tsa_ref[...], tsb_ref[...]
        for d, src, dst in ((4, d1, s4), (16, d2, s16)):
            for kind in range(3):
                for r in range(d):
                    for h in range(2):
                        dst[kind, h, pl.ds(r, tm // d, stride=d), :] = src[kind, r, :, h * 128:(h + 1) * 128]
        for sp in range(20):
            piece, half = sp // 2, sp % 2
            lanes = slice(half * 128, (half + 1) * 128)
            if piece == 0:
                blk = du_ref[:, lanes]
            else:
                kind, gi = (piece - 1) // 3, (piece - 1) % 3
                blk = d0[kind, 0, :, lanes] if gi == 0 else (s4, s16)[gi - 1][kind, half]
                if kind == 0:
                    blk = _rope128(blk, cs, sa, sb, -1.0) * (HEAD_DIM ** -0.5)
                elif kind == 1:
                    blk = _rope128(blk, cs, sa, sb, -1.0)
            dp_ref[:, sp * 128:(sp + 1) * 128] = blk.astype(BF16)
        dh = jnp.zeros((tm, D_MODEL), F32)
        hbt = h_ref[...].T
        for j in range(N_CHIPS):
            dpj = dp_ref[:, j * 640:(j + 1) * 640]
            dh = dh + lax.dot_general(dpj, w_ref[j], NT, preferred_element_type=F32)
            dw_ref[j] += jnp.dot(hbt, dpj, preferred_element_type=F32)
        xv = x_ref[...]
        rstd = lax.rsqrt(jnp.mean(xv * xv, axis=-1, keepdims=True) + NORM_EPS)
        n1 = xv * rstd
        gv = g_ref[...]
        one_sc = 1.0 + mod_ref[1:2, :]
        _acc(dsh_ref, _colsum(dh))
        _acc(dsc_ref, _colsum(dh * (n1 * gv)))
        _acc(dg_ref, _colsum(dh * one_sc * n1))
        dn = dh * (gv * one_sc)
        gx_ref[...] = dx1_ref[...] + rstd * (dn - n1 * jnp.mean(dn * n1, axis=-1, keepdims=True))

    vec = _full((1, D_MODEL))
    dspec = lambda d: pl.BlockSpec((3, d, tm // d, GROUP_W), lambda i: (0, 0, i, 0))
    return pl.pallas_call(
        body,
        name="inproj_bwd",
        grid=(S // tm,),
        in_specs=[dspec(d) for d in DILATIONS] + [_rows(tm, POOL_W), _rows(tm, D_MODEL), _rows(tm, D_MODEL), _full(w_in_g.shape),
                                                  vec, _full((6, D_MODEL)), _rows(tm, 128), _rows(tm, 128), _rows(tm, 128),
                                                  _rows(tm, D_MODEL)],
        out_specs=[_rows(tm, D_MODEL), vec, vec, vec, _full(w_in_g.shape)],
        out_shape=[jax.ShapeDtypeStruct((S, D_MODEL), F32)] + [jax.ShapeDtypeStruct((1, D_MODEL), F32)] * 3
        + [jax.ShapeDtypeStruct(w_in_g.shape, F32)],
        scratch_shapes=[pltpu.VMEM((3, 2, tm, 128), F32)] * 2 + [pltpu.VMEM((tm, IN_W), BF16)],
        compiler_params=_params(1),
    )(*dqkv, du, x, dx1, w_in_g, g, mod6, tc, tsa, tsb, h1)


def _adamw(w, g, m, v):
    m = ADAM_B1 * m + (1.0 - ADAM_B1) * g
    v = ADAM_B2 * v + (1.0 - ADAM_B2) * (g * g)
    m_hat = m / (1.0 - ADAM_B1 ** ADAM_STEP)
    v_hat = v / (1.0 - ADAM_B2 ** ADAM_STEP)
    delta = -ADAM_LR * (m_hat / (jnp.sqrt(v_hat) + ADAM_EPS) + ADAM_WD * w)
    return delta, m, v


def adamw_rows(w, g, m, v, tr, name):
    R, C = w.shape

    def body(w_ref, g_ref, m_ref, v_ref, go_ref, d_ref, mo_ref, vo_ref):
        g = g_ref[...]
        go_ref[...] = g
        d_ref[...], mo_ref[...], vo_ref[...] = _adamw(w_ref[...], g, m_ref[...], v_ref[...])

    spec = pl.BlockSpec((tr, C), lambda i: (i, 0))
    return pl.pallas_call(
        body,
        name=name,
        grid=(R // tr,),
        in_specs=[spec] * 4,
        out_specs=[spec] * 4,
        out_shape=[jax.ShapeDtypeStruct((R, C), F32)] * 4,
        compiler_params=_params(1),
    )(w, g, m, v)


def adamw_ada(c_all_t, dmod_cols, w, m, v, tr=256):
    R, C = w.shape

    def body(ct_ref, dm_ref, w_ref, m_ref, v_ref, g_ref, d_ref, mo_ref, vo_ref):
        ct = ct_ref[...]
        act = ct * jax.nn.sigmoid(ct)
        g = jnp.zeros((tr, C), F32)
        for b in range(N_DEV):
            g = g + act[:, b:b + 1] * dm_ref[b:b + 1, :]
        g_ref[...] = g
        d_ref[...], mo_ref[...], vo_ref[...] = _adamw(w_ref[...], g, m_ref[...], v_ref[...])

    spec = pl.BlockSpec((tr, C), lambda i: (i, 0))
    return pl.pallas_call(
        body,
        name="adamw_ada",
        grid=(R // tr,),
        in_specs=[pl.BlockSpec((tr, N_DEV), lambda i: (i, 0)), _full((N_DEV, C)), spec, spec, spec],
        out_specs=[spec] * 4,
        out_shape=[jax.ShapeDtypeStruct((R, C), F32)] * 4,
        compiler_params=_params(1),
    )(c_all_t, dmod_cols, w, m, v)


def adamw_small(slab_a, slab_b, convw_g, wpool_g, params):
    names = ["b_ada", "g_pre_mix", "g_post_mix", "g_pre_ffn", "g_post_ffn", "b_pool", "pool_scale", "conv_b", "conv_w", "w_pool"]
    flat = []
    for n in names:
        flat += list(params[n])

    def body(a_ref, b_ref, cw_ref, wp_ref, *rest):
        ins, outs = rest[:30], rest[30:]

        def dev_sum(ref):
            t = ref[0]
            for dev in range(1, N_DEV):
                t = t + ref[dev]
            return t

        sa, sb_, scw, swp = dev_sum(a_ref), dev_sum(b_ref), dev_sum(cw_ref), dev_sum(wp_ref)
        grads = [
            jnp.concatenate([sa[k:k + 1, :] for k in range(6)], axis=1),
            sa[6:7, :], sa[7:8, :], sa[8:9, :], sa[9:10, :],
            sa[10:11, 0:256], sa[10:11, 256:512],
            sb_[3:4, :], scw, swp,
        ]
        for i, g in enumerate(grads):
            w_ref, m_ref, v_ref = ins[3 * i:3 * i + 3]
            if names[i] in ("b_pool", "w_pool"):
                for grp in range(4):
                    if names[i] == "b_pool":
                        gp, at = g[:, grp * 64:(grp + 1) * 64], (0, slice(grp, grp + 1))
                    else:
                        gp, at = g[grp * 64:(grp + 1) * 64, :], (0, grp)
                    d, mo, vo = _adamw(w_ref[at], gp, m_ref[at], v_ref[at])
                    for k, val in enumerate((gp, d, mo, vo)):
                        outs[4 * i + k][at] = val
                continue
            d, mo, vo = _adamw(w_ref[...], g, m_ref[...], v_ref[...])
            outs[4 * i][...] = g
            outs[4 * i + 1][...] = d
            outs[4 * i + 2][...] = mo
            outs[4 * i + 3][...] = vo
        outs[-1][...] = sa[10:11, 512:640]

    vm = pl.BlockSpec(memory_space=pltpu.VMEM)
    out_shape = []
    for n in names:
        out_shape += [jax.ShapeDtypeStruct(params[n][0].shape, F32)] * 4
    out_shape.append(jax.ShapeDtypeStruct((1, 128), F32))
    outs = pl.pallas_call(
        body,
        name="adamw_small",
        in_specs=[vm] * (4 + len(flat)),
        out_specs=[vm] * len(out_shape),
        out_shape=out_shape,
        compiler_params=_params(),
    )(slab_a, slab_b, convw_g, wpool_g, *flat)
    return {n: outs[4 * i:4 * i + 4] for i, n in enumerate(names)}, outs[-1]


def _place():
    return lax.axis_index("x"), lax.axis_index("y"), lax.axis_index("c")


def _other_chips(x, y):
    return [(1 - x, y), (x, 1 - y), (1 - x, 1 - y)]


def _chip_id(cx, cy):
    return 2 * cx + cy


HBM_SPEC = pl.BlockSpec(memory_space=pltpu.HBM)
SEM_SPEC = pl.BlockSpec(memory_space=pltpu.SEMAPHORE)
ANY_SPEC = pl.BlockSpec(memory_space=pl.ANY)
EFFECT = pltpu.SideEffectType.DATAFLOW_SIDE_EFFECTING


def _hbm(t):
    return pltpu.with_memory_space_constraint(t, pltpu.HBM)


def _hbm_shapes(ts):
    return [pltpu.HBM(t.shape, t.dtype) for t in ts]


def _half_rows(ref, lead, half, rh):
    return ref.at[lead, pl.ds(half * rh, rh), :]


def _flips():
    return [(fx, fy, fc) for fx in (0, 1) for fy in (0, 1) for fc in (0, 1)][1:]


def _flip(v, f):
    return v if f == 0 else 1 - v


def ada_mod(c3, w_ada, b_cols, conv_w):
    CB = w_ada.shape[1]

    def body(c_ref, w_ref, b_ref, cw_ref, call_ref, mod_ref, cwall_ref, modall, send_sems, recv_sems):
        x, y, c = _place()
        me_dev = 4 * x + 2 * y + c
        me = _chip_id(x, y)
        call_ref[me_dev] = c_ref[0]
        cwall_ref[me] = cw_ref[...]
        sends = []
        for k, (cx, cy) in enumerate(_other_chips(x, y)):
            cp = pltpu.make_async_remote_copy(src_ref=cw_ref, dst_ref=cwall_ref.at[me], send_sem=send_sems.at[10 + k],
                                              recv_sem=recv_sems.at[10 + k], device_id=(cx, cy, c), device_id_type=MESH)
            cp.start()
            sends.append(cp)
        for k, (fx, fy, fc) in enumerate(_flips()):
            cp = pltpu.make_async_remote_copy(src_ref=c_ref.at[0], dst_ref=call_ref.at[me_dev], send_sem=send_sems.at[k],
                                              recv_sem=recv_sems.at[k],
                                              device_id=(_flip(x, fx), _flip(y, fy), _flip(c, fc)), device_id_type=MESH)
            cp.start()
            sends.append(cp)
        for k, (fx, fy, fc) in enumerate(_flips()):
            peer = 4 * _flip(x, fx) + 2 * _flip(y, fy) + _flip(c, fc)
            pltpu.make_async_remote_copy(src_ref=c_ref.at[0], dst_ref=call_ref.at[peer], send_sem=send_sems.at[k],
                                         recv_sem=recv_sems.at[k], device_id=(x, y, c), device_id_type=MESH).wait_recv()
        row = lax.broadcasted_iota(jnp.int32, (N_DEV, D_MODEL), 0)
        call = jnp.zeros((N_DEV, D_MODEL), F32)
        for dev in range(N_DEV):
            call = jnp.where(row == dev, call_ref[dev], call)
        act = call * jax.nn.sigmoid(call)
        modall[me] = jnp.dot(act, w_ref[...], preferred_element_type=F32, precision=lax.Precision.HIGHEST) + b_ref[...]
        for k, (cx, cy) in enumerate(_other_chips(x, y)):
            cp = pltpu.make_async_remote_copy(src_ref=modall.at[me], dst_ref=modall.at[me], send_sem=send_sems.at[7 + k],
                                              recv_sem=recv_sems.at[7 + k], device_id=(cx, cy, c), device_id_type=MESH)
            cp.start()
            sends.append(cp)
        for k, (cx, cy) in enumerate(_other_chips(x, y)):
            blk = modall.at[_chip_id(cx, cy)]
            pltpu.make_async_remote_copy(src_ref=blk, dst_ref=blk, send_sem=send_sems.at[7 + k], recv_sem=recv_sems.at[7 + k],
                                         device_id=(x, y, c), device_id_type=MESH).wait_recv()
        for k, (cx, cy) in enumerate(_other_chips(x, y)):
            blk = cwall_ref.at[_chip_id(cx, cy)]
            pltpu.make_async_remote_copy(src_ref=blk, dst_ref=blk, send_sem=send_sems.at[10 + k], recv_sem=recv_sems.at[10 + k],
                                         device_id=(x, y, c), device_id_type=MESH).wait_recv()
        for cp in sends:
            cp.wait_send()
        mine = [modall[j, pl.ds(me_dev, 1), :] for j in range(N_CHIPS)]
        for r in range(6):
            pieces = []
            for h in range(2):
                pos = r * D_MODEL + h * 512
                pieces.append(mine[pos // CB][:, pos % CB:pos % CB + 512])
            mod_ref[r:r + 1, :] = jnp.concatenate(pieces, axis=1)

    vm = pl.BlockSpec(memory_space=pltpu.VMEM)
    return pl.pallas_call(
        body,
        name="ada_mod",
        in_specs=[vm] * 4,
        out_specs=[vm] * 3,
        out_shape=[jax.ShapeDtypeStruct((N_DEV, 1, D_MODEL), F32), jax.ShapeDtypeStruct((6, D_MODEL), F32),
                   jax.ShapeDtypeStruct((N_CHIPS,) + conv_w.shape, F32)],
        scratch_shapes=[pltpu.VMEM((N_CHIPS, N_DEV, CB), F32), pltpu.SemaphoreType.DMA((13,)), pltpu.SemaphoreType.DMA((13,))],
        compiler_params=pltpu.CompilerParams(has_side_effects=True, vmem_limit_bytes=VMEM_LIMIT),
    )(c3, w_ada, b_cols, conv_w)


def split_start(name, bufs, plan, n_sem, carry):
    nb = len(bufs)
    many = isinstance(carry, (list, tuple))
    alls = list(bufs) + (list(carry) if many else [carry])
    na = len(alls)

    def body(*refs):
        x, y, c = _place()
        ssem, rsem = refs[na], refs[na + 1]
        for i, (src, dst, dev) in enumerate(plan(refs[:nb], x, y, c)):
            pltpu.make_async_remote_copy(src_ref=src, dst_ref=dst, send_sem=ssem.at[i], recv_sem=rsem.at[i], device_id=dev,
                                         device_id_type=MESH).start()

    outs = pl.pallas_call(
        body,
        name=name,
        out_shape=[pltpu.SemaphoreType.DMA((n_sem,)), pltpu.SemaphoreType.DMA((n_sem,))] + _hbm_shapes(alls),
        in_specs=[HBM_SPEC] * na,
        out_specs=[SEM_SPEC, SEM_SPEC] + [HBM_SPEC] * na,
        input_output_aliases={i: 2 + i for i in range(na)},
        compiler_params=pltpu.CompilerParams(has_side_effects=EFFECT),
    )(*[_hbm(t) for t in alls])
    return outs[0], outs[1], list(outs[2:2 + nb]), (list(outs[2 + nb:]) if many else outs[-1])


def split_wait(name, ssem, rsem, bufs, plan, after):
    nb = len(bufs)

    def body(*refs):
        x, y, c = _place()
        s_ref, r_ref = refs[nb], refs[nb + 1]
        for i, (src, dst, dev) in enumerate(plan(refs[:nb], x, y, c)):
            cp = pltpu.make_async_remote_copy(src_ref=src, dst_ref=dst, send_sem=s_ref.at[i], recv_sem=r_ref.at[i], device_id=dev,
                                              device_id_type=MESH)
            cp.wait_send()
            cp.wait_recv()

    outs = pl.pallas_call(
        body,
        name=name,
        out_shape=_hbm_shapes(bufs),
        in_specs=[HBM_SPEC] * nb + [SEM_SPEC, SEM_SPEC, ANY_SPEC],
        out_specs=[HBM_SPEC] * nb,
        input_output_aliases={i: i for i in range(nb)},
        compiler_params=pltpu.CompilerParams(has_side_effects=EFFECT),
    )(*bufs, ssem, rsem, after)
    return list(outs)


def _gather_ici_plan(n):
    def plan(refs, x, y, c):
        out = []
        for w in range(n):
            rh = refs[w].shape[0] // 2
            for cx, cy in _other_chips(x, y):
                out.append((refs[w].at[pl.ds(c * rh, rh), :], _half_rows(refs[n + w], _chip_id(x, y), c, rh), (cx, cy, c)))
        return out

    return plan


def _gather_d2d_plan(n):
    def plan(refs, x, y, c):
        out = []
        for w in range(n):
            rh = refs[w].shape[1] // 2
            for cx, cy in _other_chips(x, y):
                blk = _half_rows(refs[w], _chip_id(cx, cy), c, rh)
                out.append((blk, blk, (x, y, 1 - c)))
        return out

    return plan


def _dev_id(x, y, c):
    return 4 * x + 2 * y + c


def _small_ici_plan(n):
    def plan(refs, x, y, c):
        out = []
        for w in range(n):
            dst = refs[n + w].at[_dev_id(x, y, c)]
            out.append((refs[w], dst, (x, y, 1 - c)))
            for cx, cy in _other_chips(x, y):
                out.append((refs[w], dst, (cx, cy, c)))
        return out

    return plan


def _small_d2d_plan(n):
    def plan(refs, x, y, c):
        out = []
        for w in range(n):
            for cx, cy in _other_chips(x, y):
                blk = refs[w].at[_dev_id(cx, cy, c)]
                out.append((blk, blk, (x, y, 1 - c)))
        return out

    return plan


def _rs_d2d_plan(n):
    def plan(refs, x, y, c):
        out = []
        for w in range(n):
            rh = refs[w].shape[1] // 2
            out.append((refs[w].at[:, pl.ds((1 - c) * rh, rh), :], refs[n + w], (x, y, 1 - c)))
        return out

    return plan


def _rs_ici_plan(n):
    def plan(refs, x, y, c):
        out = []
        for w in range(n):
            for k, (cx, cy) in enumerate(_other_chips(x, y)):
                out.append((refs[w].at[_chip_id(cx, cy)], refs[n + w].at[k], (cx, cy, c)))
        return out

    return plan


def _rs_share_plan(n):
    def plan(refs, x, y, c):
        out = []
        for w in range(n):
            rh = refs[w].shape[0] // 2
            rows = refs[w].at[pl.ds(c * rh, rh), :]
            out.append((rows, rows, (x, y, 1 - c)))
        return out

    return plan


def rs_add(grad, sibbuf, place, tr, name):
    _, R, C = grad.shape
    nt = (R // 2) // tr

    def body(p_ref, g_ref, s_ref, o_ref):
        o_ref[...] = (g_ref[...] + s_ref[...]).astype(BF16)

    return pl.pallas_call(
        body,
        name=name,
        grid_spec=pltpu.PrefetchScalarGridSpec(
            num_scalar_prefetch=1,
            grid=(N_CHIPS, nt),
            in_specs=[pl.BlockSpec((None, tr, C), lambda j, i, p: (j, p[0] * nt + i, 0)),
                      pl.BlockSpec((None, tr, C), lambda j, i, p: (j, i, 0))],
            out_specs=pl.BlockSpec((None, tr, C), lambda j, i, p: (j, i, 0)),
        ),
        out_shape=jax.ShapeDtypeStruct((N_CHIPS, R // 2, C), BF16),
        compiler_params=_params(2),
    )(place, grad, sibbuf)


def rs_final(grad, sibbuf, rbuf, place, tr, name):
    _, R, C = grad.shape
    nt = (R // 2) // tr

    def body(p_ref, g_ref, s_ref, r_ref, o_ref):
        o_ref[...] = (((g_ref[...] + s_ref[...]) + r_ref[0].astype(F32)) + r_ref[1].astype(F32)) + r_ref[2].astype(F32)

    return pl.pallas_call(
        body,
        name=name,
        grid_spec=pltpu.PrefetchScalarGridSpec(
            num_scalar_prefetch=1,
            grid=(nt,),
            in_specs=[pl.BlockSpec((None, tr, C), lambda i, p: (p[1], p[0] * nt + i, 0)),
                      pl.BlockSpec((None, tr, C), lambda i, p: (p[1], i, 0)),
                      pl.BlockSpec((3, tr, C), lambda i, p: (0, i, 0))],
            out_specs=pl.BlockSpec((tr, C), lambda i, p: (p[0] * nt + i, 0)),
        ),
        out_shape=jax.ShapeDtypeStruct((R, C), F32),
        compiler_params=_params(1),
    )(place, grad, sibbuf, rbuf)


class GradReduce:
    def __init__(self, tag, grads, rows, place):
        self.tag, self.grads, self.rows, self.place = tag, grads, rows, place
        self.n = len(grads)

    def d2d_start(self, carry):
        sib = [lax.empty((N_CHIPS, g.shape[1] // 2, g.shape[2]), F32) for g in self.grads]
        self.s1, self.r1, bufs, carry = split_start(f"rs_{self.tag}_d2d_start", self.grads + sib, _rs_d2d_plan(self.n), self.n, carry)
        self.bufs1 = bufs
        return carry

    def add_and_ici_start(self, after, carry):
        bufs = split_wait(f"rs_{self.tag}_d2d_wait", self.s1, self.r1, self.bufs1, _rs_d2d_plan(self.n), after)
        self.grads, self.sib = bufs[:self.n], bufs[self.n:]
        pb = [rs_add(g, s, self.place, tr, f"rs_{self.tag}_add{w}")
              for w, (g, s, tr) in enumerate(zip(self.grads, self.sib, self.rows))]
        rb = [lax.empty((3,) + p.shape[1:], BF16) for p in pb]
        self.s2, self.r2, self.bufs2, carry = split_start(f"rs_{self.tag}_ici_start", pb + rb, _rs_ici_plan(self.n), 3 * self.n, carry)
        return carry

    def final_and_share_start(self, after, carry):
        bufs = split_wait(f"rs_{self.tag}_ici_wait", self.s2, self.r2, self.bufs2, _rs_ici_plan(self.n), after)
        rb = bufs[self.n:]
        full = [rs_final(g, s, r, self.place, tr, f"rs_{self.tag}_final{w}")
                for w, (g, s, r, tr) in enumerate(zip(self.grads, self.sib, rb, self.rows))]
        self.s3, self.r3, self.bufs3, carry = split_start(f"rs_{self.tag}_share_start", full, _rs_share_plan(self.n), self.n, carry)
        return carry

    def finish(self, after):
        return split_wait(f"rs_{self.tag}_share_wait", self.s3, self.r3, self.bufs3, _rs_share_plan(self.n), after)


def _rope_tables(positions):
    inv_freq = ROPE_THETA ** (-jnp.arange(0, ROT_DIM, 2, dtype=F32) / ROT_DIM)
    ang = positions.astype(F32)[:, None] * inv_freq
    cos, sin = jnp.cos(ang), jnp.sin(ang)
    S = positions.shape[0]
    one, zero = jnp.ones((S, 48), F32), jnp.zeros((S, 48), F32)
    z8 = jnp.zeros((S, 8), F32)
    tc = jnp.concatenate([cos, cos, one], axis=1)
    tsa = jnp.concatenate([z8, sin, zero], axis=1)
    tsb = jnp.concatenate([-sin, z8, zero], axis=1)
    return tuple(jnp.tile(t, (1, 2)) for t in (tc, tsa, tsb))


def _block_diag(w_pool):
    wbd = jnp.zeros((POOL_W, POOL_W), F32)
    for gi in range(4):
        wbd = wbd.at[gi * 64:(gi + 1) * 64, gi * 64:(gi + 1) * 64].set(w_pool[gi])
    return wbd


def kernel(x, c, positions, w_ada, b_ada, g_pre_mix, g_post_mix, g_pre_ffn, g_post_ffn, w_in, w_pool, b_pool, pool_scale, w_out, w_up, conv_w, conv_b, w_down, loss_target, m_w_ada, m_b_ada, m_g_pre_mix, m_g_post_mix, m_g_pre_ffn, m_g_post_ffn, m_w_in, m_w_pool, m_b_pool, m_pool_scale, m_w_out, m_w_up, m_conv_w, m_conv_b, m_w_down, v_w_ada, v_b_ada, v_g_pre_mix, v_g_post_mix, v_g_pre_ffn, v_g_post_ffn, v_w_in, v_w_pool, v_b_pool, v_pool_scale, v_w_out, v_w_up, v_conv_w, v_conv_b, v_w_down):
    xi, yi, ci = lax.axis_index("x"), lax.axis_index("y"), lax.axis_index("c")
    chip = 2 * xi + yi
    place = jnp.stack([ci, chip]).astype(jnp.int32)
    x2, tgt = x[0], loss_target[0]
    S = x2.shape[0]

    def landing(s_):
        return lax.dynamic_update_slice(lax.empty((N_CHIPS,) + s_.shape, s_.dtype), s_[None], (chip, 0, 0))

    cb_ada = w_ada.shape[2]
    b_cols = lax.dynamic_slice(b_ada, (0, chip * cb_ada), (1, cb_ada))
    c_all, mod6, conv_w_g = ada_mod(c.reshape(1, 1, D_MODEL), w_ada[0], b_cols, conv_w[0])
    conv_w_f = jnp.transpose(conv_w_g, (1, 0, 2)).reshape(3, D_FF)
    mix_sh = [w_in[0].astype(BF16), w_out[0].astype(BF16)]
    ffn_sh = [w_up[0].astype(BF16), w_down[0].astype(BF16)]
    ga_s, ga_r, ga_bufs, mod6 = split_start("gather_mix_ici_start", mix_sh + [landing(t) for t in mix_sh], _gather_ici_plan(2), 6, mod6)
    gb_s, gb_r, gb_bufs, (mod6, tc, tsa, tsb) = split_start("gather_ffn_ici_start", ffn_sh + [landing(t) for t in ffn_sh],
                                                            _gather_ici_plan(2), 6, [mod6, *_rope_tables(positions[0])])
    wbd = _block_diag(w_pool[0]).astype(BF16)
    b_pool2, scale2 = b_pool.reshape(1, POOL_W), pool_scale
    ga_bufs = split_wait("gather_mix_ici_wait", ga_s, ga_r, ga_bufs, _gather_ici_plan(2), mod6)
    gc_s, gc_r, mix_land, mod6 = split_start("gather_mix_d2d_start", ga_bufs[2:], _gather_d2d_plan(2), 6, mod6)
    w_in_g, w_out_g = split_wait("gather_mix_d2d_wait", gc_s, gc_r, mix_land, _gather_d2d_plan(2), mod6)

    h1, u, *qkv = inproj_fwd(x2, g_pre_mix, mod6, w_in_g, tc, tsa, tsb)
    mixed, pool = pool_fwd(u, wbd, b_pool2, scale2)
    o_l = [attn_fwd(t, d) for t, d in zip(qkv, DILATIONS)]
    attn_done = sum(l[0, :8, :128] for _, l in o_l)
    gb_bufs = split_wait("gather_ffn_ici_wait", gb_s, gb_r, gb_bufs, _gather_ici_plan(2), attn_done)
    gd_s, gd_r, ffn_land, pool = split_start("gather_ffn_d2d_start", gb_bufs[2:], _gather_d2d_plan(2), 6, pool)
    cat, lse, lse4, lse16, y1, x1, h2 = outproj_fwd([o for o, _ in o_l] + [l for _, l in o_l], pool, x2, w_out_g, g_post_mix,
                                                    g_pre_ffn, mod6)
    lses = [lse[None], lse4, lse16]
    w_up_g, w_down_g = split_wait("gather_ffn_d2d_wait", gd_s, gd_r, ffn_land, _gather_d2d_plan(2), h2)
    w_down_f = w_down_g.reshape(D_FF, D_MODEL)
    gate, val, a, dy2, dout, loss_v, d_gt_f, d_g_post_ffn = ffn_fwd(h2, w_up_g, conv_w_f, conv_b, w_down_f, x1, tgt, g_post_ffn, mod6)

    dgc, dval, d_conv_w, d_conv_b, dw_down, dw_up = down_bwd(dy2, w_down_f, gate, val, conv_w_f, conv_b, a, h2)
    dx1, dy1, d_sh_f, d_sc_f, d_g_pre_ffn, d_gt_m, d_g_post_mix, dw_up = up_bwd(
        dgc, dval, conv_w_f, w_up_g, x1, dout, y1, g_pre_ffn, g_post_mix, mod6, h2, dw_up)
    rs_ffn = GradReduce("ffn", [dw_up, dw_down.reshape(N_CHIPS, D_FF // N_CHIPS, D_MODEL)], [256, 176], place)
    dy1 = rs_ffn.d2d_start(dy1)
    dpool, da1, da4, da16, dl1, dl4, dl16, dw_out = outproj_bwd(dy1, w_out_g, cat)
    dpool = rs_ffn.add_and_ici_start(dw_out, dpool)
    du, d_wbd, d_b_pool, d_scale = pool_bwd(dpool, mixed, wbd, b_pool2, scale2)
    dqkv = [attn_bwd(t, da, ls, dl, d) for t, da, ls, dl, d in zip(qkv, (da1[None], da4, da16), lses, (dl1[None], dl4, dl16), DILATIONS)]
    grad_x, d_sh_m, d_sc_m, d_g_pre_mix, dw_in = inproj_bwd(dqkv, du, x2, dx1, w_in_g, g_pre_mix, mod6, tc, tsa, tsb, h1)

    z1 = jnp.zeros((1, D_MODEL), F32)
    slab_a = jnp.concatenate(
        [d_sh_m, d_sc_m, d_gt_m, d_sh_f, d_sc_f, d_gt_f, d_g_pre_mix, d_g_post_mix, d_g_pre_ffn, d_g_post_ffn,
         jnp.concatenate([d_b_pool, d_scale, loss_v, jnp.zeros((1, 384), F32)], axis=1)] + [z1] * 5, axis=0)
    slab_b = jnp.concatenate([d_conv_w, d_conv_b, jnp.zeros((4, D_FF), F32)], axis=0)
    d_wpool = jnp.concatenate([d_wbd[gi * 64:(gi + 1) * 64, gi * 64:(gi + 1) * 64] for gi in range(4)], axis=0)
    dev = _dev_id(xi, yi, ci)
    small_src = [slab_a, slab_b, d_wpool]
    small_land = [lax.dynamic_update_slice(lax.empty((N_DEV,) + t.shape, F32), t[None], (dev, 0, 0)) for t in small_src]
    tok = jnp.zeros((8, 128), F32)
    gs_s, gs_r, gs_bufs, tok = split_start("small_ici_start", small_src + small_land, _small_ici_plan(3), 12, tok)
    rs_mix = GradReduce("mix", [dw_in, dw_out], [256, 256], place)
    tok = rs_mix.d2d_start(tok)
    tok = rs_ffn.final_and_share_start(tok, tok)
    gs_bufs = split_wait("small_ici_wait", gs_s, gs_r, gs_bufs, _small_ici_plan(3), tok)
    gt_s, gt_r, small_land, tok = split_start("small_d2d_start", gs_bufs[3:], _small_d2d_plan(3), 9, tok)
    tok = rs_mix.add_and_ici_start(tok, tok)
    slab_a_g, slab_b_g, wpool_g = split_wait("small_d2d_wait", gt_s, gt_r, small_land, _small_d2d_plan(3), tok)
    cw_cols = conv_w.shape[2]
    convw_g = lax.dynamic_slice(slab_b_g, (0, 0, chip * cw_cols), (N_DEV, 3, cw_cols))
    dmod_cols = lax.dynamic_slice(slab_a_g[:, :6, :].reshape(N_DEV, 6 * D_MODEL), (0, chip * cb_ada), (N_DEV, cb_ada))

    res = {}

    def big_adamw(name, w, g, m, v, tr):
        g_, d_, m_, v_ = adamw_rows(w[0], g, m[0], v[0], tr, "adamw_" + name)
        res[name] = (g_[None], d_[None], m_[None], v_[None])
        return v_

    g_ada, d_ada, m_ada, v_ada = adamw_ada(c_all.reshape(N_DEV, D_MODEL).T, dmod_cols, w_ada[0], m_w_ada[0], v_w_ada[0])
    res["w_ada"] = (g_ada[None], d_ada[None], m_ada[None], v_ada[None])
    g_w_up, g_w_down = rs_ffn.finish(v_ada)
    big_adamw("w_up", w_up, g_w_up, m_w_up, v_w_up, 256)
    last = big_adamw("w_down", w_down, g_w_down, m_w_down, v_w_down, 352)
    rs_mix.final_and_share_start(last, jnp.zeros((8, 128), F32))
    g_w_in, g_w_out = rs_mix.finish(last)
    big_adamw("w_in", w_in, g_w_in, m_w_in, v_w_in, 256)
    big_adamw("w_out", w_out, g_w_out, m_w_out, v_w_out, 256)
    small, loss_sum = adamw_small(slab_a_g, slab_b_g, convw_g, wpool_g, {
        "b_ada": (b_ada, m_b_ada, v_b_ada), "g_pre_mix": (g_pre_mix, m_g_pre_mix, v_g_pre_mix),
        "g_post_mix": (g_post_mix, m_g_post_mix, v_g_post_mix), "g_pre_ffn": (g_pre_ffn, m_g_pre_ffn, v_g_pre_ffn),
        "g_post_ffn": (g_post_ffn, m_g_post_ffn, v_g_post_ffn), "b_pool": (b_pool, m_b_pool, v_b_pool),
        "pool_scale": (pool_scale, m_pool_scale, v_pool_scale), "conv_b": (conv_b, m_conv_b, v_conv_b),
        "conv_w": (conv_w[0], m_conv_w[0], v_conv_w[0]), "w_pool": (w_pool, m_w_pool, v_w_pool)})
    for name in ("b_ada", "g_pre_mix", "g_post_mix", "g_pre_ffn", "g_post_ffn", "pool_scale", "conv_b", "b_pool", "w_pool"):
        res[name] = tuple(small[name])
    res["conv_w"] = tuple(t[None] for t in small["conv_w"])

    loss = loss_sum[0, 0]
    order = ["w_ada", "b_ada", "g_pre_mix", "g_post_mix", "g_pre_ffn", "g_post_ffn", "w_in", "w_pool", "b_pool", "pool_scale",
             "w_out", "w_up", "conv_w", "conv_b", "w_down"]
    outs = [loss, grad_x[None]]
    for k in range(4):
        outs += [res[n][k] for n in order]
    return tuple(outs)
```

```python
import math

import jax
import jax.numpy as jnp
from jax import lax
from jax.experimental import pallas as pl
from jax.experimental.pallas import tpu as pltpu

F32 = jnp.float32
BF16 = jnp.bfloat16
MESH = pl.DeviceIdType.MESH

D_MODEL = 1024
HEAD_DIM = 64
POOL_W = 256
GROUP_W = 256
DILATIONS = (1, 4, 16)
ATT_BLOCK = 128
IN_W = 2560
D_FF = 2816
HALF_FF = 1408
ROT_DIM = 16
ROPE_THETA = 500000.0
NORM_EPS = 1e-6
N_CHIPS = 4
N_DEV = 8
NEG = -1e30

ADAM_LR = 0.001
ADAM_B1 = 0.9
ADAM_B2 = 0.999
ADAM_EPS = 1e-08
ADAM_WD = 0.01
ADAM_STEP = 10

VMEM_LIMIT = 56 * 1024 * 1024

NT = (((1,), (1,)), ((), ()))
TN = (((0,), (0,)), ((), ()))


def _params(n_grid=0, **kw):
    sem = ("arbitrary",) * n_grid if n_grid else None
    return pltpu.CompilerParams(dimension_semantics=sem, vmem_limit_bytes=VMEM_LIMIT, **kw)


def _full(shape):
    nd = len(shape)
    return pl.BlockSpec(tuple(shape), lambda *_: (0,) * nd, pipeline_mode=pl.Buffered(1))


def _rows(tm, ncol):
    return pl.BlockSpec((tm, ncol), lambda i: (i, 0))


def _acc(ref, val):
    @pl.when(pl.program_id(0) == 0)
    def _():
        ref[...] = jnp.zeros_like(ref)

    ref[...] += val


def _colsum(v):
    return jnp.sum(v, axis=0, keepdims=True)


def _rope128(t, cs, sa, sb, sign):
    return t * cs + sign * (pltpu.roll(t, 8, 1) * sa + pltpu.roll(t, 120, 1) * sb)


FF_CHUNKS = tuple((ch, off, w) for ch in range(2) for off, w in ((0, 512), (512, 512), (1024, 384)))
GELU_C0 = math.sqrt(2.0 / math.pi)
GELU_C1 = GELU_C0 * 0.044715


def _gelu(z):
    z2 = z * z
    t = jnp.tanh(z * (GELU_C0 + GELU_C1 * z2))
    u = 0.5 * t + 0.5
    return z * u, u, t, z2


def _gelu_grad(z, u, t, z2):
    return u + (z * (GELU_C0 + (3.0 * GELU_C1) * z2)) * (0.5 - 0.5 * (t * t))


def _conv_taps(gate, halo, first):
    row = lax.broadcasted_iota(jnp.int32, gate.shape, 0)
    halo = jnp.where(first, 0.0, halo)
    nh = halo.shape[0]
    p1 = halo[nh - 1:nh, :]
    p2 = halo[nh - 2:nh - 1, :]
    g1 = jnp.where(row == 0, p1, pltpu.roll(gate, 1, 0))
    g2 = jnp.where(row == 0, p2, jnp.where(row == 1, p1, pltpu.roll(gate, 2, 0)))
    return g1, g2


def inproj_fwd(x, g, mod6, w_in_g, tc, tsa, tsb, tm=512):
    S = x.shape[0]

    def body(x_ref, g_ref, mod_ref, w_ref, tc_ref, tsa_ref, tsb_ref, h_ref, u_ref, q1_ref, q4_ref, q16_ref, scr):
        qkv_refs = (q1_ref, q4_ref, q16_ref)
        xv = x_ref[...]
        rstd = lax.rsqrt(jnp.mean(xv * xv, axis=-1, keepdims=True) + NORM_EPS)
        h = ((xv * rstd) * g_ref[...]) * (1.0 + mod_ref[1:2, :]) + mod_ref[0:1, :]
        hb = h.astype(BF16)
        h_ref[...] = hb
        cs, sa, sb = tc_ref[...], tsa_ref[...], tsb_ref[...]
        for j in range(N_CHIPS):
            res = jnp.dot(hb, w_ref[j], preferred_element_type=F32)
            for t in range(5):
                sp = 5 * j + t
                piece, half = sp // 2, sp % 2
                blk = res[:, t * 128:(t + 1) * 128]
                lanes = slice(half * 128, (half + 1) * 128)
                if piece == 0:
                    u_ref[:, lanes] = blk
                else:
                    kind, gi = (piece - 1) // 3, (piece - 1) % 3
                    if kind == 0:
                        blk = _rope128(blk, cs, sa, sb, 1.0) * (HEAD_DIM ** -0.5)
                    elif kind == 1:
                        blk = _rope128(blk, cs, sa, sb, 1.0)
                    d = DILATIONS[gi]
                    if d == 1:
                        q1_ref[kind, 0, :, lanes] = blk.astype(BF16)
                    else:
                        scr[...] = blk
                        for r in range(d):
                            qkv_refs[gi][kind, r, :, lanes] = scr[pl.ds(r, tm // d, stride=d), :].astype(BF16)

    cls = lambda d: pl.BlockSpec((3, d, tm // d, GROUP_W), lambda i: (0, 0, i, 0))
    return pl.pallas_call(
        body,
        name="inproj_fwd",
        grid=(S // tm,),
        in_specs=[_rows(tm, D_MODEL), _full((1, D_MODEL)), _full((6, D_MODEL)), _full(w_in_g.shape),
                  _rows(tm, 128), _rows(tm, 128), _rows(tm, 128)],
        out_specs=[_rows(tm, D_MODEL), _rows(tm, POOL_W)] + [cls(d) for d in DILATIONS],
        out_shape=[jax.ShapeDtypeStruct((S, D_MODEL), BF16), jax.ShapeDtypeStruct((S, POOL_W), F32)]
        + [jax.ShapeDtypeStruct((3, d, S // d, GROUP_W), BF16) for d in DILATIONS],
        scratch_shapes=[pltpu.VMEM((tm, 128), F32)],
        compiler_params=_params(1),
    )(x, g, mod6, w_in_g, tc, tsa, tsb)


def _attn_masks():
    row = lax.broadcasted_iota(jnp.int32, (2 * ATT_BLOCK, 2 * ATT_BLOCK), 0) % ATT_BLOCK
    col = lax.broadcasted_iota(jnp.int32, (2 * ATT_BLOCK, 2 * ATT_BLOCK), 1)
    band = (col >= row) & (col <= row + ATT_BLOCK)
    lane = lax.broadcasted_iota(jnp.int32, (ATT_BLOCK, 128), 1)
    return band, col, lane < HEAD_DIM


def _classes_per_step(d, nb):
    return min(d, max(1, 8 // nb))


def _stack_heads(t, lo):
    z = jnp.zeros_like(t)
    return jnp.concatenate([jnp.where(lo, t, z), jnp.where(lo, z, t)], axis=0)


def _unstack_heads(t2, lo):
    return jnp.where(lo, t2[:ATT_BLOCK], t2[ATT_BLOCK:])


def attn_fwd(qkv, d):
    L = qkv.shape[2]
    nb = L // ATT_BLOCK
    cpb = _classes_per_step(d, nb)

    def body(q_ref, k_ref, v_ref, o_ref, l_ref, kpad, vpad):
        for cls in range(cpb):
            kpad[cls, 0:ATT_BLOCK, :] = jnp.zeros((ATT_BLOCK, GROUP_W), BF16)
            vpad[cls, 0:ATT_BLOCK, :] = jnp.zeros((ATT_BLOCK, GROUP_W), BF16)
            kpad[cls, ATT_BLOCK:, :] = k_ref[cls]
            vpad[cls, ATT_BLOCK:, :] = v_ref[cls]
        band, col, lo = _attn_masks()

        def step(t, carry):
            cls, n = t // nb, t % nb
            r0 = pl.multiple_of(n * ATT_BLOCK, ATT_BLOCK)
            valid = band & ((col >= ATT_BLOCK) | (n > 0))
            qb = q_ref[cls, pl.ds(r0, ATT_BLOCK), :]
            kb = kpad[cls, pl.ds(r0, 2 * ATT_BLOCK), :]
            vb = vpad[cls, pl.ds(r0, 2 * ATT_BLOCK), :]
            for pair in range(2):
                lanes = slice(pair * 128, (pair + 1) * 128)
                qp, kp, vp = qb[:, lanes], kb[:, lanes], vb[:, lanes]
                s = lax.dot_general(_stack_heads(qp, lo), kp, NT, preferred_element_type=F32)
                s = jnp.where(valid, s, NEG)
                m = jnp.max(s, axis=1, keepdims=True)
                p = jnp.exp(s - m)
                den = jnp.sum(p, axis=1, keepdims=True)
                pv = jnp.dot(p.astype(BF16), vp, preferred_element_type=F32)
                o_ref[cls, pl.ds(r0, ATT_BLOCK), lanes] = _unstack_heads(pv / den, lo)
                l_ref[cls, pl.ds(r0, ATT_BLOCK), lanes] = _unstack_heads(jnp.broadcast_to(m + jnp.log(den), pv.shape), lo)
            return carry

        lax.fori_loop(0, cpb * nb, step, 0, unroll=4)

    spec = lambda kind: pl.BlockSpec((None, cpb, L, GROUP_W), lambda r: (kind, r, 0, 0))
    return pl.pallas_call(
        body,
        name=f"attn_fwd_d{d}",
        grid=(d // cpb,),
        in_specs=[spec(0), spec(1), spec(2)],
        out_specs=[pl.BlockSpec((cpb, L, GROUP_W), lambda r: (r, 0, 0))] * 2,
        out_shape=[jax.ShapeDtypeStruct((d, L, GROUP_W), F32)] * 2,
        scratch_shapes=[pltpu.VMEM((cpb, L + ATT_BLOCK, GROUP_W), BF16)] * 2,
        compiler_params=_params(1),
    )(qkv, qkv, qkv)


def _pool_lane_windows(shape):
    lane = lax.broadcasted_iota(jnp.int32, shape, 1)
    return lane, jnp.where(lane < 64, 2, jnp.where(lane < 128, 4, jnp.where(lane < 192, 8, 16)))


def pool_fwd(u, wbd, b, scale):
    S = u.shape[0]

    def body(u_ref, w_ref, b_ref, s_ref, mixed_ref, out_ref):
        uv = u_ref[...]
        row = lax.broadcasted_iota(jnp.int32, uv.shape, 0)
        lane, win = _pool_lane_windows(uv.shape)

        def shift(a, k):
            return jnp.where(row >= k, pltpu.roll(a, k, 0), 0.0)

        s2 = uv + shift(uv, 1)
        s4 = s2 + shift(s2, 2)
        s8 = s4 + shift(s4, 4)
        s16 = s8 + shift(s8, 8)
        tsum = jnp.where(lane < 64, s2, jnp.where(lane < 128, s4, jnp.where(lane < 192, s8, s16)))
        cnt = jnp.minimum(row + 1, win).astype(F32)
        mb = (tsum / cnt - uv).astype(BF16)
        mixed_ref[...] = mb
        y = jnp.dot(mb, w_ref[...], preferred_element_type=F32) + b_ref[...]
        out_ref[...] = (y * s_ref[...]).astype(BF16)

    vm = pl.BlockSpec(memory_space=pltpu.VMEM)
    return pl.pallas_call(
        body,
        name="pool_fwd",
        in_specs=[vm] * 4,
        out_specs=[vm] * 2,
        out_shape=[jax.ShapeDtypeStruct((S, POOL_W), BF16)] * 2,
        compiler_params=_params(),
    )(u, wbd, b, scale)


def outproj_fwd(o_l, pool, x, w_out_g, g_post, g_pre, mod6, tm=512):
    S = x.shape[0]

    def body(o0, o1, o2, l0, l1, l2, pool_ref, x_ref, w_ref, gpost_ref, gpre_ref, mod_ref,
             cat_ref, lse_ref, lse4_ref, lse16_ref, y1_ref, x1_ref, h2_ref, so4, sl4, so16, sl16):
        for d, src, dst in ((4, o1, so4), (4, l1, sl4), (16, o2, so16), (16, l2, sl16)):
            for r in range(d):
                for h in range(2):
                    dst[h, pl.ds(r, tm // d, stride=d), :] = src[r, :, h * 128:(h + 1) * 128]
        nat = lambda ref: jnp.concatenate([ref[0], ref[1]], axis=1)
        a, b, c = l0[0], nat(sl4), nat(sl16)
        m = jnp.maximum(jnp.maximum(a, b), c)
        e0, e1, e2 = jnp.exp(a - m), jnp.exp(b - m), jnp.exp(c - m)
        z = e0 + e1 + e2
        lse = m + jnp.log(z)
        lse_ref[...] = lse
        for h in range(2):
            sl4[h] = lse[:, h * 128:(h + 1) * 128]
        for d, dst in ((4, lse4_ref), (16, lse16_ref)):
            for r in range(d):
                for h in range(2):
                    dst[r, :, h * 128:(h + 1) * 128] = sl4[h, pl.ds(r, tm // d, stride=d), :]
        attn = (e0 * o0[0] + e1 * nat(so4) + e2 * nat(so16)) / z
        cat = jnp.concatenate([pool_ref[...], attn.astype(BF16)], axis=1)
        cat_ref[...] = cat
        y1 = jnp.concatenate([jnp.dot(cat, w_ref[j], preferred_element_type=F32) for j in range(N_CHIPS)], axis=1)
        y1_ref[...] = y1
        rstd = lax.rsqrt(jnp.mean(y1 * y1, axis=-1, keepdims=True) + NORM_EPS)
        x1 = x_ref[...] + mod_ref[2:3, :] * ((y1 * rstd) * gpost_ref[...])
        x1_ref[...] = x1
        rstd2 = lax.rsqrt(jnp.mean(x1 * x1, axis=-1, keepdims=True) + NORM_EPS)
        h2 = ((x1 * rstd2) * gpre_ref[...]) * (1.0 + mod_ref[4:5, :]) + mod_ref[3:4, :]
        h2_ref[...] = h2.astype(BF16)

    t256 = _rows(tm, GROUP_W)
    cls = lambda d: pl.BlockSpec((d, tm // d, GROUP_W), lambda i: (0, i, 0))
    cls_shape = lambda d: jax.ShapeDtypeStruct((d, S // d, GROUP_W), F32)
    return pl.pallas_call(
        body,
        name="outproj_fwd",
        grid=(S // tm,),
        in_specs=[cls(d) for d in DILATIONS] * 2 + [t256, _rows(tm, D_MODEL), _full(w_out_g.shape), _full((1, D_MODEL)),
                                                    _full((1, D_MODEL)), _full((6, D_MODEL))],
        out_specs=[_rows(tm, 512), t256, cls(4), cls(16), _rows(tm, D_MODEL), _rows(tm, D_MODEL), _rows(tm, D_MODEL)],
        out_shape=[jax.ShapeDtypeStruct((S, 512), BF16), jax.ShapeDtypeStruct((S, GROUP_W), F32), cls_shape(4), cls_shape(16),
                   jax.ShapeDtypeStruct((S, D_MODEL), F32), jax.ShapeDtypeStruct((S, D_MODEL), F32),
                   jax.ShapeDtypeStruct((S, D_MODEL), BF16)],
        scratch_shapes=[pltpu.VMEM((2, tm, 128), F32)] * 4,
        compiler_params=_params(1),
    )(*o_l, pool, x, w_out_g, g_post, g_pre, mod6)


def _halo_prev(tm, ncol):
    return pl.BlockSpec((16, ncol), lambda i: (jnp.maximum(i * (tm // 16) - 1, 0), 0))


def ffn_fwd(h2, w_up_g, conv_w, conv_b, w_down, x1, target, g_post, mod6, tm=256):
    S = x1.shape[0]

    def body(h_ref, wu_ref, cw_ref, cb_ref, wd_ref, x1_ref, tgt_ref, g_ref, mod_ref,
             gate_ref, val_ref, a_ref, dy2_ref, dout_ref, loss_ref, dgt_ref, dg_ref, carry):
        first = pl.program_id(0) == 0

        @pl.when(first)
        def _():
            carry[...] = jnp.zeros_like(carry)

        hb = h_ref[...]
        y2 = jnp.zeros((tm, D_MODEL), F32)
        for ch in range(2):
            cols = slice(ch * HALF_FF, (ch + 1) * HALF_FF)
            gb = jnp.dot(hb, wu_ref[ch], preferred_element_type=F32).astype(BF16)
            vb = jnp.dot(hb, wu_ref[2 + ch], preferred_element_type=F32).astype(BF16)
            gate_ref[:, cols] = gb
            val_ref[:, cols] = vb
            gt = gb.astype(F32)
            g1, g2 = _conv_taps(gt, carry[:, cols], first)
            carry[:, cols] = gt[tm - 8:, :]
            gc = g2 * cw_ref[0:1, cols] + g1 * cw_ref[1:2, cols] + gt * cw_ref[2:3, cols] + cb_ref[:, cols]
            ab = (_gelu(gc)[0] * vb.astype(F32)).astype(BF16)
            a_ref[:, cols] = ab
            y2 = y2 + jnp.dot(ab, wd_ref[cols, :], preferred_element_type=F32)
        rstd = lax.rsqrt(jnp.mean(y2 * y2, axis=-1, keepdims=True) + NORM_EPS)
        y2n = y2 * rstd
        gv = g_ref[...]
        gtf = mod_ref[5:6, :]
        r2 = y2n * gv
        diff = (x1_ref[...] + gtf * r2) - tgt_ref[...]
        _acc(loss_ref, jnp.zeros((1, 128), F32) + 0.5 * jnp.sum(diff * diff) * (1.0 / D_MODEL))
        dout = diff * (1.0 / D_MODEL)
        dout_ref[...] = dout
        _acc(dgt_ref, _colsum(dout * r2))
        dr2 = dout * gtf
        _acc(dg_ref, _colsum(dr2 * y2n))
        dyn = dr2 * gv
        dy2 = rstd * (dyn - y2n * jnp.mean(dyn * y2n, axis=-1, keepdims=True))
        dy2_ref[...] = dy2.astype(BF16)

    vec = _full((1, D_MODEL))
    return pl.pallas_call(
        body,
        name="ffn_fwd",
        grid=(S // tm,),
        in_specs=[_rows(tm, D_MODEL), _full(w_up_g.shape), _full((3, D_FF)), _full((1, D_FF)), _full((D_FF, D_MODEL)),
                  _rows(tm, D_MODEL), _rows(tm, D_MODEL), vec, _full((6, D_MODEL))],
        out_specs=[_rows(tm, D_FF), _rows(tm, D_FF), _rows(tm, D_FF), _rows(tm, D_MODEL), _rows(tm, D_MODEL), _full((1, 128)), vec, vec],
        out_shape=[jax.ShapeDtypeStruct((S, D_FF), BF16)] * 3 + [jax.ShapeDtypeStruct((S, D_MODEL), BF16),
                                                                 jax.ShapeDtypeStruct((S, D_MODEL), F32),
                                                                 jax.ShapeDtypeStruct((1, 128), F32),
                                                                 jax.ShapeDtypeStruct((1, D_MODEL), F32),
                                                                 jax.ShapeDtypeStruct((1, D_MODEL), F32)],
        scratch_shapes=[pltpu.VMEM((8, D_FF), F32)],
        compiler_params=_params(1),
    )(h2, w_up_g, conv_w, conv_b, w_down, x1, target, g_post, mod6)


def down_bwd(dy2, w_down, gate, val, conv_w, conv_b, a, h2, tm=256):
    S = dy2.shape[0]

    def body(dy_ref, w_ref, gate_ref, halo_ref, val_ref, cw_ref, cb_ref, a_ref, h_ref,
             dgc_ref, dval_ref, dcw_ref, dcb_ref, dwd_ref, dwu_ref):
        first = pl.program_id(0) == 0

        @pl.when(first)
        def _():
            dcw_ref[...] = jnp.zeros_like(dcw_ref)
            dcb_ref[...] = jnp.zeros_like(dcb_ref)
            dwd_ref[...] = jnp.zeros_like(dwd_ref)
            dwu_ref[...] = jnp.zeros_like(dwu_ref)

        dyb = dy_ref[...]
        hb = h_ref[...]
        def col(i):
            ch, off, width = FF_CHUNKS[i]
            return slice(ch * HALF_FF + off, ch * HALF_FF + off + width)

        def mm_da(i):
            return lax.dot_general(dyb, w_ref[col(i), :], NT, preferred_element_type=F32)

        def elementwise(i, da):
            cols = col(i)
            gt = gate_ref[:, cols].astype(F32)
            g1, g2 = _conv_taps(gt, halo_ref[:, cols].astype(F32), first)
            gc = g2 * cw_ref[0:1, cols] + g1 * cw_ref[1:2, cols] + gt * cw_ref[2:3, cols] + cb_ref[:, cols]
            ge, u, th, z2 = _gelu(gc)
            dgc = da * val_ref[:, cols].astype(F32) * _gelu_grad(gc, u, th, z2)
            dgc_ref[:, cols] = dgc.astype(BF16)
            dvb = (da * ge).astype(BF16)
            dval_ref[:, cols] = dvb
            dcb_ref[:, cols] += _colsum(dgc)
            dcw_ref[0:1, cols] += _colsum(dgc * g2)
            dcw_ref[1:2, cols] += _colsum(dgc * g1)
            dcw_ref[2:3, cols] += _colsum(dgc * gt)
            return dvb

        def mm_dw(i, dvb):
            ch, off, width = FF_CHUNKS[i]
            dwd_ref[col(i), :] += lax.dot_general(a_ref[:, col(i)], dyb, TN, preferred_element_type=F32)
            dwu_ref[ch, :, off:off + width] += lax.dot_general(hb, dvb, TN, preferred_element_type=F32)

        n = len(FF_CHUNKS)
        da = mm_da(0)
        prev = None
        for i in range(n):
            nxt = mm_da(i + 1) if i + 1 < n else None
            if prev is not None:
                mm_dw(i - 1, prev)
            prev = elementwise(i, da)
            da = nxt
        mm_dw(n - 1, prev)

    return pl.pallas_call(
        body,
        name="down_bwd",
        grid=(S // tm,),
        in_specs=[_rows(tm, D_MODEL), _full((D_FF, D_MODEL)), _rows(tm, D_FF), _halo_prev(tm, D_FF), _rows(tm, D_FF),
                  _full((3, D_FF)), _full((1, D_FF)), _rows(tm, D_FF), _rows(tm, D_MODEL)],
        out_specs=[_rows(tm, D_FF), _rows(tm, D_FF), _full((3, D_FF)), _full((1, D_FF)), _full((D_FF, D_MODEL)),
                   pl.BlockSpec((2, D_MODEL, HALF_FF), lambda i: (1, 0, 0), pipeline_mode=pl.Buffered(1))],
        out_shape=[jax.ShapeDtypeStruct((S, D_FF), BF16), jax.ShapeDtypeStruct((S, D_FF), BF16),
                   jax.ShapeDtypeStruct((3, D_FF), F32), jax.ShapeDtypeStruct((1, D_FF), F32),
                   jax.ShapeDtypeStruct((D_FF, D_MODEL), F32), jax.ShapeDtypeStruct((N_CHIPS, D_MODEL, HALF_FF), F32)],
        compiler_params=_params(1),
    )(dy2, w_down, gate, gate, val, conv_w, conv_b, a, h2)


def up_bwd(dgc, dval, conv_w, w_up_g, x1, dout, y1, g_pre, g_post, mod6, h2, dw_up, tm=256):
    S = x1.shape[0]
    last_blk = S // 16 - 1

    def body(dgc_ref, nxt_ref, dval_ref, cw_ref, w_ref, x1_ref, dout_ref, y1_ref, gpre_ref, gpost_ref, mod_ref, h_ref, dwin_ref,
             dx1_ref, dy1_ref, dsh_ref, dsc_ref, dgpre_ref, dgt_ref, dgpost_ref, dwu_ref):
        last = pl.program_id(0) == pl.num_programs(0) - 1

        @pl.when(pl.program_id(0) == 0)
        def _():
            dwu_ref[...] = jnp.zeros_like(dwu_ref)

        hb = h_ref[...]
        dh = jnp.zeros((tm, D_MODEL), F32)
        for ch in range(2):
            cols = slice(ch * HALF_FF, (ch + 1) * HALF_FF)
            dg = dgc_ref[:, cols].astype(F32)
            nx = jnp.where(last, 0.0, nxt_ref[:, cols].astype(F32))
            row = lax.broadcasted_iota(jnp.int32, dg.shape, 0)
            n0, n1 = nx[0:1, :], nx[1:2, :]
            u1 = jnp.where(row == tm - 1, n0, pltpu.roll(dg, tm - 1, 0))
            u2 = jnp.where(row == tm - 1, n1, jnp.where(row == tm - 2, n0, pltpu.roll(dg, tm - 2, 0)))
            dgate = (dg * cw_ref[2:3, cols] + u1 * cw_ref[1:2, cols] + u2 * cw_ref[0:1, cols]).astype(BF16)
            dwu_ref[ch] += lax.dot_general(hb, dgate, TN, preferred_element_type=F32)
            dh = dh + lax.dot_general(dgate, w_ref[ch], NT, preferred_element_type=F32)
            dh = dh + lax.dot_general(dval_ref[:, cols], w_ref[2 + ch], NT, preferred_element_type=F32)
        x1 = x1_ref[...]
        rstd = lax.rsqrt(jnp.mean(x1 * x1, axis=-1, keepdims=True) + NORM_EPS)
        n2 = x1 * rstd
        gpre = gpre_ref[...]
        one_sc = 1.0 + mod_ref[4:5, :]
        _acc(dsh_ref, _colsum(dh))
        _acc(dsc_ref, _colsum(dh * (n2 * gpre)))
        _acc(dgpre_ref, _colsum(dh * one_sc * n2))
        dn = dh * (gpre * one_sc)
        dx1 = dout_ref[...] + rstd * (dn - n2 * jnp.mean(dn * n2, axis=-1, keepdims=True))
        dx1_ref[...] = dx1
        y1 = y1_ref[...]
        rstd1 = lax.rsqrt(jnp.mean(y1 * y1, axis=-1, keepdims=True) + NORM_EPS)
        y1n = y1 * rstd1
        gpost = gpost_ref[...]
        gtm = mod_ref[2:3, :]
        _acc(dgt_ref, _colsum(dx1 * (y1n * gpost)))
        dr1 = dx1 * gtm
        _acc(dgpost_ref, _colsum(dr1 * y1n))
        dyn = dr1 * gpost
        dy1 = rstd1 * (dyn - y1n * jnp.mean(dyn * y1n, axis=-1, keepdims=True))
        dy1_ref[...] = dy1.astype(BF16)

    vec = _full((1, D_MODEL))
    nxt = pl.BlockSpec((16, D_FF), lambda i: (jnp.minimum((i + 1) * (tm // 16), last_blk), 0))
    return pl.pallas_call(
        body,
        name="up_bwd",
        grid=(S // tm,),
        in_specs=[_rows(tm, D_FF), nxt, _rows(tm, D_FF), _full((3, D_FF)), _full(w_up_g.shape), _rows(tm, D_MODEL),
                  _rows(tm, D_MODEL), _rows(tm, D_MODEL), vec, vec, _full((6, D_MODEL)), _rows(tm, D_MODEL),
                  pl.BlockSpec(memory_space=pl.ANY)],
        out_specs=[_rows(tm, D_MODEL), _rows(tm, D_MODEL), vec, vec, vec, vec, vec,
                   pl.BlockSpec((2, D_MODEL, HALF_FF), lambda i: (0, 0, 0), pipeline_mode=pl.Buffered(1))],
        out_shape=[jax.ShapeDtypeStruct((S, D_MODEL), F32), jax.ShapeDtypeStruct((S, D_MODEL), BF16)]
        + [jax.ShapeDtypeStruct((1, D_MODEL), F32)] * 5 + [jax.ShapeDtypeStruct(dw_up.shape, F32)],
        input_output_aliases={12: 7},
        compiler_params=_params(1),
    )(dgc, dgc, dval, conv_w, w_up_g, x1, dout, y1, g_pre, g_post, mod6, h2, dw_up)


def outproj_bwd(dy1, w_out_g, cat, tm=512):
    S = dy1.shape[0]

    def body(dy_ref, w_ref, cat_ref, dpool_ref, dattn_ref, da4_ref, da16_ref, delta_ref, dl4_ref, dl16_ref, dw_ref, scr):
        @pl.when(pl.program_id(0) == 0)
        def _():
            dw_ref[...] = jnp.zeros_like(dw_ref)

        catb = cat_ref[...]
        dcat = jnp.zeros((tm, 512), F32)
        for j in range(N_CHIPS):
            dyj = dy_ref[:, j * 256:(j + 1) * 256]
            dcat = dcat + lax.dot_general(dyj, w_ref[j], NT, preferred_element_type=F32)
            dw_ref[j] += lax.dot_general(catb, dyj, TN, preferred_element_type=F32)
        dpool_ref[...] = dcat[:, :POOL_W]
        dattn = dcat[:, POOL_W:]
        dattn_ref[...] = dattn.astype(BF16)
        for h in range(2):
            scr[h] = dattn[:, h * 128:(h + 1) * 128]
        for d, dst in ((4, da4_ref), (16, da16_ref)):
            for r in range(d):
                for h in range(2):
                    dst[r, :, h * 128:(h + 1) * 128] = scr[h, pl.ds(r, tm // d, stride=d), :].astype(BF16)
        prod = dattn * catb[:, POOL_W:].astype(F32)
        r = lax.broadcasted_iota(jnp.int32, (GROUP_W, GROUP_W), 0) // HEAD_DIM
        c = lax.broadcasted_iota(jnp.int32, (GROUP_W, GROUP_W), 1) // HEAD_DIM
        ones_bd = jnp.where(r == c, 1.0, 0.0).astype(BF16)
        hi = prod.astype(BF16)
        lo = (prod - hi.astype(F32)).astype(BF16)
        delta = jnp.dot(hi, ones_bd, preferred_element_type=F32) + jnp.dot(lo, ones_bd, preferred_element_type=F32)
        delta_ref[...] = delta
        for h in range(2):
            scr[h] = delta[:, h * 128:(h + 1) * 128]
        for d, dst in ((4, dl4_ref), (16, dl16_ref)):
            for r in range(d):
                for h in range(2):
                    dst[r, :, h * 128:(h + 1) * 128] = scr[h, pl.ds(r, tm // d, stride=d), :]

    cls = lambda d: pl.BlockSpec((d, tm // d, GROUP_W), lambda i: (0, i, 0))
    cls_shape = lambda d, dt: jax.ShapeDtypeStruct((d, S // d, GROUP_W), dt)
    return pl.pallas_call(
        body,
        name="outproj_bwd",
        grid=(S // tm,),
        in_specs=[_rows(tm, D_MODEL), _full(w_out_g.shape), _rows(tm, 512)],
        out_specs=[_rows(tm, POOL_W), _rows(tm, GROUP_W), cls(4), cls(16), _rows(tm, GROUP_W), cls(4), cls(16),
                   _full(w_out_g.shape)],
        out_shape=[jax.ShapeDtypeStruct((S, POOL_W), F32), jax.ShapeDtypeStruct((S, GROUP_W), BF16), cls_shape(4, BF16),
                   cls_shape(16, BF16), jax.ShapeDtypeStruct((S, GROUP_W), F32), cls_shape(4, F32), cls_shape(16, F32),
                   jax.ShapeDtypeStruct(w_out_g.shape, F32)],
        scratch_shapes=[pltpu.VMEM((2, tm, 128), F32)],
        compiler_params=_params(1),
    )(dy1, w_out_g, cat)


def attn_bwd(qkv, dattn, lse, delta, d):
    L = qkv.shape[2]
    nb = L // ATT_BLOCK
    cpb = _classes_per_step(d, nb)

    def body(q_ref, k_ref, v_ref, do_ref, l_ref, dl_ref, out_ref, kpad, vpad, dkpad, dvpad):
        for cls in range(cpb):
            kpad[cls, 0:ATT_BLOCK, :] = jnp.zeros((ATT_BLOCK, GROUP_W), BF16)
            vpad[cls, 0:ATT_BLOCK, :] = jnp.zeros((ATT_BLOCK, GROUP_W), BF16)
            kpad[cls, ATT_BLOCK:, :] = k_ref[cls]
            vpad[cls, ATT_BLOCK:, :] = v_ref[cls]
        dkpad[...] = jnp.zeros_like(dkpad)
        dvpad[...] = jnp.zeros_like(dvpad)
        band, col, lo = _attn_masks()

        def step(t, carry):
            cls, n = t // nb, t % nb
            r0 = pl.multiple_of(n * ATT_BLOCK, ATT_BLOCK)
            valid = band & ((col >= ATT_BLOCK) | (n > 0))
            qb = q_ref[cls, pl.ds(r0, ATT_BLOCK), :]
            dob = do_ref[cls, pl.ds(r0, ATT_BLOCK), :]
            lb = l_ref[cls, pl.ds(r0, ATT_BLOCK), :]
            dlb = dl_ref[cls, pl.ds(r0, ATT_BLOCK), :]
            kb = kpad[cls, pl.ds(r0, 2 * ATT_BLOCK), :]
            vb = vpad[cls, pl.ds(r0, 2 * ATT_BLOCK), :]
            for pair in range(2):
                lanes = slice(pair * 128, (pair + 1) * 128)
                qp, dop, kp, vp = qb[:, lanes], dob[:, lanes], kb[:, lanes], vb[:, lanes]
                c0, c1 = pair * 128, pair * 128 + HEAD_DIM
                q2, do2 = _stack_heads(qp, lo), _stack_heads(dop, lo)
                lse2 = jnp.concatenate([lb[:, c0:c0 + 1], lb[:, c1:c1 + 1]], axis=0)
                dl2 = jnp.concatenate([dlb[:, c0:c0 + 1], dlb[:, c1:c1 + 1]], axis=0)
                s = lax.dot_general(q2, kp, NT, preferred_element_type=F32)
                s = jnp.where(valid, s, NEG)
                p = jnp.exp(s - lse2)
                dp = lax.dot_general(do2, vp, NT, preferred_element_type=F32)
                ds = (p * (dp - dl2)).astype(BF16)
                dq2 = jnp.dot(ds, kp, preferred_element_type=F32)
                out_ref[0, cls, pl.ds(r0, ATT_BLOCK), lanes] = _unstack_heads(dq2, lo)
                dkpad[cls, pl.ds(r0, 2 * ATT_BLOCK), lanes] += lax.dot_general(ds, q2, TN, preferred_element_type=F32)
                dvpad[cls, pl.ds(r0, 2 * ATT_BLOCK), lanes] += lax.dot_general(p.astype(BF16), do2, TN, preferred_element_type=F32)
            return carry

        lax.fori_loop(0, cpb * nb, step, 0, unroll=4)
        for cls in range(cpb):
            out_ref[1, cls] = dkpad[cls, ATT_BLOCK:, :]
            out_ref[2, cls] = dvpad[cls, ATT_BLOCK:, :]

    spec = lambda kind: pl.BlockSpec((None, cpb, L, GROUP_W), lambda r: (kind, r, 0, 0))
    per_cls = pl.BlockSpec((cpb, L, GROUP_W), lambda r: (r, 0, 0))
    return pl.pallas_call(
        body,
        name=f"attn_bwd_d{d}",
        grid=(d // cpb,),
        in_specs=[spec(0), spec(1), spec(2), per_cls, per_cls, per_cls],
        out_specs=pl.BlockSpec((3, cpb, L, GROUP_W), lambda r: (0, r, 0, 0)),
        out_shape=jax.ShapeDtypeStruct((3, d, L, GROUP_W), F32),
        scratch_shapes=[pltpu.VMEM((cpb, L + ATT_BLOCK, GROUP_W), BF16)] * 2 + [pltpu.VMEM((cpb, L + ATT_BLOCK, GROUP_W), F32)] * 2,
        compiler_params=_params(1),
    )(qkv, qkv, qkv, dattn, lse, delta)


def pool_bwd(dpool, mixed, wbd, b, scale):
    S = dpool.shape[0]

    def body(dp_ref, mx_ref, w_ref, b_ref, s_ref, du_ref, dw_ref, db_ref, ds_ref):
        dp = dp_ref[...]
        mb = mx_ref[...]
        wv = w_ref[...]
        ypre = jnp.dot(mb, wv, preferred_element_type=F32) + b_ref[...]
        ds_ref[...] = _colsum(dp * ypre)
        dpre = dp * s_ref[...]
        db_ref[...] = _colsum(dpre)
        dpb = dpre.astype(BF16)
        dw_ref[...] = lax.dot_general(mb, dpb, TN, preferred_element_type=F32)
        dmix = lax.dot_general(dpb, wv, NT, preferred_element_type=F32)
        row = lax.broadcasted_iota(jnp.int32, dmix.shape, 0)
        lane, win = _pool_lane_windows(dmix.shape)
        e = dmix / jnp.minimum(row + 1, win).astype(F32)

        def shift(a, k):
            return jnp.where(row < S - k, pltpu.roll(a, S - k, 0), 0.0)

        f2 = e + shift(e, 1)
        f4 = f2 + shift(f2, 2)
        f8 = f4 + shift(f4, 4)
        f16 = f8 + shift(f8, 8)
        du_ref[...] = jnp.where(lane < 64, f2, jnp.where(lane < 128, f4, jnp.where(lane < 192, f8, f16))) - dmix

    vm = pl.BlockSpec(memory_space=pltpu.VMEM)
    return pl.pallas_call(
        body,
        name="pool_bwd",
        in_specs=[vm] * 5,
        out_specs=[vm] * 4,
        out_shape=[jax.ShapeDtypeStruct((S, POOL_W), F32), jax.ShapeDtypeStruct((POOL_W, POOL_W), F32),
                   jax.ShapeDtypeStruct((1, POOL_W), F32), jax.ShapeDtypeStruct((1, POOL_W), F32)],
        compiler_params=_params(),
    )(dpool, mixed, wbd, b, scale)


def inproj_bwd(dqkv, du, x, dx1, w_in_g, g, mod6, tc, tsa, tsb, h1, tm=512):
    S = x.shape[0]

    def body(d0, d1, d2, du_ref, x_ref, dx1_ref, w_ref, g_ref, mod_ref, tc_ref, tsa_ref, tsb_ref, h_ref,
             gx_ref, dsh_ref, dsc_ref, dg_ref, dw_ref, s4, s16, dp_ref):
        @pl.when(pl.program_id(0) == 0)
        def _():
            dw_ref[...] = jnp.zeros_like(dw_ref)

        cs, sa, sb = tc_ref[...], tsa_ref[...], tsb_ref[...]
        for d, src, dst in ((4, d1, s4), (16, d2, s16)):
            for kind in range(3):
                for r in range(d):
                    for h in range(2):
                        dst[kind, h, pl.ds(r, tm // d, stride=d), :] = src[kind, r, :, h * 128:(h + 1) * 128]
        for sp in range(20):
            piece, half = sp // 2, sp % 2
            lanes = slice(half * 128, (half + 1) * 128)
            if piece == 0:
                blk = du_ref[:, lanes]
            else:
                kind, gi = (piece - 1) // 3, (piece - 1) % 3
                blk = d0[kind, 0, :, lanes] if gi == 0 else (s4, s16)[gi - 1][kind, half]
                if kind == 0:
                    blk = _rope128(blk, cs, sa, sb, -1.0) * (HEAD_DIM ** -0.5)
                elif kind == 1:
                    blk = _rope128(blk, cs, sa, sb, -1.0)
            dp_ref[:, sp * 128:(sp + 1) * 128] = blk.astype(BF16)
        dh = jnp.zeros((tm, D_MODEL), F32)
        hbt = h_ref[...].T
        for j in range(N_CHIPS):
            dpj = dp_ref[:, j * 640:(j + 1) * 640]
            dh = dh + lax.dot_general(dpj, w_ref[j], NT, preferred_element_type=F32)
            dw_ref[j] += jnp.dot(hbt, dpj, preferred_element_type=F32)
        xv = x_ref[...]
        rstd = lax.rsqrt(jnp.mean(xv * xv, axis=-1, keepdims=True) + NORM_EPS)
        n1 = xv * rstd
        gv = g_ref[...]
        one_sc = 1.0 + mod_ref[1:2, :]
        _acc(dsh_ref, _colsum(dh))
        _acc(dsc_ref, _colsum(dh * (n1 * gv)))
        _acc(dg_ref, _colsum(dh * one_sc * n1))
        dn = dh * (gv * one_sc)
        gx_ref[...] = dx1_ref[...] + rstd * (dn - n1 * jnp.mean(dn * n1, axis=-1, keepdims=True))

    vec = _full((1, D_MODEL))
    dspec = lambda d: pl.BlockSpec((3, d, tm // d, GROUP_W), lambda i: (0, 0, i, 0))
    return pl.pallas_call(
        body,
        name="inproj_bwd",
        grid=(S // tm,),
        in_specs=[dspec(d) for d in DILATIONS] + [_rows(tm, POOL_W), _rows(tm, D_MODEL), _rows(tm, D_MODEL), _full(w_in_g.shape),
                                                  vec, _full((6, D_MODEL)), _rows(tm, 128), _rows(tm, 128), _rows(tm, 128),
                                                  _rows(tm, D_MODEL)],
        out_specs=[_rows(tm, D_MODEL), vec, vec, vec, _full(w_in_g.shape)],
        out_shape=[jax.ShapeDtypeStruct((S, D_MODEL), F32)] + [jax.ShapeDtypeStruct((1, D_MODEL), F32)] * 3
        + [jax.ShapeDtypeStruct(w_in_g.shape, F32)],
        scratch_shapes=[pltpu.VMEM((3, 2, tm, 128), F32)] * 2 + [pltpu.VMEM((tm, IN_W), BF16)],
        compiler_params=_params(1),
    )(*dqkv, du, x, dx1, w_in_g, g, mod6, tc, tsa, tsb, h1)


def _adamw(w, g, m, v):
    m = ADAM_B1 * m + (1.0 - ADAM_B1) * g
    v = ADAM_B2 * v + (1.0 - ADAM_B2) * (g * g)
    m_hat = m / (1.0 - ADAM_B1 ** ADAM_STEP)
    v_hat = v / (1.0 - ADAM_B2 ** ADAM_STEP)
    delta = -ADAM_LR * (m_hat / (jnp.sqrt(v_hat) + ADAM_EPS) + ADAM_WD * w)
    return delta, m, v


def adamw_rows(w, g, m, v, tr, name):
    R, C = w.shape

    def body(w_ref, g_ref, m_ref, v_ref, go_ref, d_ref, mo_ref, vo_ref):
        g = g_ref[...]
        go_ref[...] = g
        d_ref[...], mo_ref[...], vo_ref[...] = _adamw(w_ref[...], g, m_ref[...], v_ref[...])

    spec = pl.BlockSpec((tr, C), lambda i: (i, 0))
    return pl.pallas_call(
        body,
        name=name,
        grid=(R // tr,),
        in_specs=[spec] * 4,
        out_specs=[spec] * 4,
        out_shape=[jax.ShapeDtypeStruct((R, C), F32)] * 4,
        compiler_params=_params(1),
    )(w, g, m, v)


def adamw_ada(c_all_t, dmod_cols, w, m, v, tr=256):
    R, C = w.shape

    def body(ct_ref, dm_ref, w_ref, m_ref, v_ref, g_ref, d_ref, mo_ref, vo_ref):
        ct = ct_ref[...]
        act = ct * jax.nn.sigmoid(ct)
        g = jnp.zeros((tr, C), F32)
        for b in range(N_DEV):
            g = g + act[:, b:b + 1] * dm_ref[b:b + 1, :]
        g_ref[...] = g
        d_ref[...], mo_ref[...], vo_ref[...] = _adamw(w_ref[...], g, m_ref[...], v_ref[...])

    spec = pl.BlockSpec((tr, C), lambda i: (i, 0))
    return pl.pallas_call(
        body,
        name="adamw_ada",
        grid=(R // tr,),
        in_specs=[pl.BlockSpec((tr, N_DEV), lambda i: (i, 0)), _full((N_DEV, C)), spec, spec, spec],
        out_specs=[spec] * 4,
        out_shape=[jax.ShapeDtypeStruct((R, C), F32)] * 4,
        compiler_params=_params(1),
    )(c_all_t, dmod_cols, w, m, v)


def adamw_small(slab_a, slab_b, convw_g, wpool_g, params):
    names = ["b_ada", "g_pre_mix", "g_post_mix", "g_pre_ffn", "g_post_ffn", "b_pool", "pool_scale", "conv_b", "conv_w", "w_pool"]
    flat = []
    for n in names:
        flat += list(params[n])

    def body(a_ref, b_ref, cw_ref, wp_ref, *rest):
        ins, outs = rest[:30], rest[30:]

        def dev_sum(ref):
            t = ref[0]
            for dev in range(1, N_DEV):
                t = t + ref[dev]
            return t

        sa, sb_, scw, swp = dev_sum(a_ref), dev_sum(b_ref), dev_sum(cw_ref), dev_sum(wp_ref)
        grads = [
            jnp.concatenate([sa[k:k + 1, :] for k in range(6)], axis=1),
            sa[6:7, :], sa[7:8, :], sa[8:9, :], sa[9:10, :],
            sa[10:11, 0:256], sa[10:11, 256:512],
            sb_[3:4, :], scw, swp,
        ]
        for i, g in enumerate(grads):
            w_ref, m_ref, v_ref = ins[3 * i:3 * i + 3]
            if names[i] == "b_pool":
                parts = [((0, slice(grp, grp + 1)), g[:, grp * 64:(grp + 1) * 64]) for grp in range(4)]
            elif names[i] == "w_pool":
                parts = [((0, grp), g[grp * 64:(grp + 1) * 64, :]) for grp in range(4)]
            elif names[i] == "conv_w":
                parts = [((0,), g)]
            else:
                parts = [((Ellipsis,), g)]
            for at, gp in parts:
                d, mo, vo = _adamw(w_ref[at], gp, m_ref[at], v_ref[at])
                for k, val in enumerate((gp, d, mo, vo)):
                    outs[4 * i + k][at] = val
        outs[-1][...] = sa[10:11, 512:640]

    vm = pl.BlockSpec(memory_space=pltpu.VMEM)
    out_shape = []
    for n in names:
        out_shape += [jax.ShapeDtypeStruct(params[n][0].shape, F32)] * 4
    out_shape.append(jax.ShapeDtypeStruct((1, 128), F32))
    outs = pl.pallas_call(
        body,
        name="adamw_small",
        in_specs=[vm] * (4 + len(flat)),
        out_specs=[vm] * len(out_shape),
        out_shape=out_shape,
        compiler_params=_params(),
    )(slab_a, slab_b, convw_g, wpool_g, *flat)
    return {n: outs[4 * i:4 * i + 4] for i, n in enumerate(names)}, outs[-1]


def _place():
    return lax.axis_index("x"), lax.axis_index("y"), lax.axis_index("c")


def _other_chips(x, y):
    return [(1 - x, y), (x, 1 - y), (1 - x, 1 - y)]


def _chip_id(cx, cy):
    return 2 * cx + cy


HBM_SPEC = pl.BlockSpec(memory_space=pltpu.HBM)
SEM_SPEC = pl.BlockSpec(memory_space=pltpu.SEMAPHORE)
ANY_SPEC = pl.BlockSpec(memory_space=pl.ANY)
EFFECT = pltpu.SideEffectType.DATAFLOW_SIDE_EFFECTING


def _hbm(t):
    return pltpu.with_memory_space_constraint(t, pltpu.HBM)


def _hbm_shapes(ts):
    return [pltpu.HBM(t.shape, t.dtype) for t in ts]


def _half_rows(ref, lead, half, rh):
    return ref.at[lead, pl.ds(half * rh, rh), :]


def _flips():
    return [(fx, fy, fc) for fx in (0, 1) for fy in (0, 1) for fc in (0, 1)][1:]


def _flip(v, f):
    return v if f == 0 else 1 - v


def ada_mod(c3, w_ada, b_cols, conv_w):
    CB = w_ada.shape[1]

    def body(c_ref, w_ref, b_ref, cw_ref, call_ref, mod_ref, cwall_ref, modall, send_sems, recv_sems):
        x, y, c = _place()
        me_dev = 4 * x + 2 * y + c
        me = _chip_id(x, y)
        call_ref[me_dev] = c_ref[0]
        cwall_ref[me] = cw_ref[...]
        sends = []
        for k, (cx, cy) in enumerate(_other_chips(x, y)):
            cp = pltpu.make_async_remote_copy(src_ref=cw_ref, dst_ref=cwall_ref.at[me], send_sem=send_sems.at[10 + k],
                                              recv_sem=recv_sems.at[10 + k], device_id=(cx, cy, c), device_id_type=MESH)
            cp.start()
            sends.append(cp)
        for k, (fx, fy, fc) in enumerate(_flips()):
            cp = pltpu.make_async_remote_copy(src_ref=c_ref.at[0], dst_ref=call_ref.at[me_dev], send_sem=send_sems.at[k],
                                              recv_sem=recv_sems.at[k],
                                              device_id=(_flip(x, fx), _flip(y, fy), _flip(c, fc)), device_id_type=MESH)
            cp.start()
            sends.append(cp)
        for k, (fx, fy, fc) in enumerate(_flips()):
            peer = 4 * _flip(x, fx) + 2 * _flip(y, fy) + _flip(c, fc)
            pltpu.make_async_remote_copy(src_ref=c_ref.at[0], dst_ref=call_ref.at[peer], send_sem=send_sems.at[k],
                                         recv_sem=recv_sems.at[k], device_id=(x, y, c), device_id_type=MESH).wait_recv()
        row = lax.broadcasted_iota(jnp.int32, (N_DEV, D_MODEL), 0)
        call = jnp.zeros((N_DEV, D_MODEL), F32)
        for dev in range(N_DEV):
            call = jnp.where(row == dev, call_ref[dev], call)
        act = call * jax.nn.sigmoid(call)
        wv = w_ref[...]
        w_hi = wv.astype(BF16)
        w_lo = (wv - w_hi.astype(F32)).astype(BF16)
        a_hi = act.astype(BF16)
        a_lo = (act - a_hi.astype(F32)).astype(BF16)
        prod = (jnp.dot(a_hi, w_hi, preferred_element_type=F32) + jnp.dot(a_lo, w_hi, preferred_element_type=F32)
                + jnp.dot(a_hi, w_lo, preferred_element_type=F32))
        modall[me] = prod + b_ref[...]
        for k, (cx, cy) in enumerate(_other_chips(x, y)):
            cp = pltpu.make_async_remote_copy(src_ref=modall.at[me], dst_ref=modall.at[me], send_sem=send_sems.at[7 + k],
                                              recv_sem=recv_sems.at[7 + k], device_id=(cx, cy, c), device_id_type=MESH)
            cp.start()
            sends.append(cp)
        for k, (cx, cy) in enumerate(_other_chips(x, y)):
            blk = modall.at[_chip_id(cx, cy)]
            pltpu.make_async_remote_copy(src_ref=blk, dst_ref=blk, send_sem=send_sems.at[7 + k], recv_sem=recv_sems.at[7 + k],
                                         device_id=(x, y, c), device_id_type=MESH).wait_recv()
        for k, (cx, cy) in enumerate(_other_chips(x, y)):
            blk = cwall_ref.at[_chip_id(cx, cy)]
            pltpu.make_async_remote_copy(src_ref=blk, dst_ref=blk, send_sem=send_sems.at[10 + k], recv_sem=recv_sems.at[10 + k],
                                         device_id=(x, y, c), device_id_type=MESH).wait_recv()
        for cp in sends:
            cp.wait_send()
        mine = [modall[j, pl.ds(me_dev, 1), :] for j in range(N_CHIPS)]
        for r in range(6):
            pieces = []
            for h in range(2):
                pos = r * D_MODEL + h * 512
                pieces.append(mine[pos // CB][:, pos % CB:pos % CB + 512])
            mod_ref[r:r + 1, :] = jnp.concatenate(pieces, axis=1)

    vm = pl.BlockSpec(memory_space=pltpu.VMEM)
    return pl.pallas_call(
        body,
        name="ada_mod",
        in_specs=[vm] * 4,
        out_specs=[vm] * 3,
        out_shape=[jax.ShapeDtypeStruct((N_DEV, 1, D_MODEL), F32), jax.ShapeDtypeStruct((6, D_MODEL), F32),
                   jax.ShapeDtypeStruct((N_CHIPS,) + conv_w.shape, F32)],
        scratch_shapes=[pltpu.VMEM((N_CHIPS, N_DEV, CB), F32), pltpu.SemaphoreType.DMA((13,)), pltpu.SemaphoreType.DMA((13,))],
        compiler_params=pltpu.CompilerParams(has_side_effects=True, vmem_limit_bytes=VMEM_LIMIT),
    )(c3, w_ada, b_cols, conv_w)


def split_start(name, bufs, plan, n_sem, carry):
    nb = len(bufs)
    many = isinstance(carry, (list, tuple))
    alls = list(bufs) + (list(carry) if many else [carry])
    na = len(alls)

    def body(*refs):
        x, y, c = _place()
        ssem, rsem = refs[na], refs[na + 1]
        for i, (src, dst, dev) in enumerate(plan(refs[:nb], x, y, c)):
            pltpu.make_async_remote_copy(src_ref=src, dst_ref=dst, send_sem=ssem.at[i], recv_sem=rsem.at[i], device_id=dev,
                                         device_id_type=MESH).start()

    outs = pl.pallas_call(
        body,
        name=name,
        out_shape=[pltpu.SemaphoreType.DMA((n_sem,)), pltpu.SemaphoreType.DMA((n_sem,))] + _hbm_shapes(alls),
        in_specs=[HBM_SPEC] * na,
        out_specs=[SEM_SPEC, SEM_SPEC] + [HBM_SPEC] * na,
        input_output_aliases={i: 2 + i for i in range(na)},
        compiler_params=pltpu.CompilerParams(has_side_effects=EFFECT),
    )(*[_hbm(t) for t in alls])
    return outs[0], outs[1], list(outs[2:2 + nb]), (list(outs[2 + nb:]) if many else outs[-1])


def split_wait(name, ssem, rsem, bufs, plan, after):
    nb = len(bufs)

    def body(*refs):
        x, y, c = _place()
        s_ref, r_ref = refs[nb], refs[nb + 1]
        for i, (src, dst, dev) in enumerate(plan(refs[:nb], x, y, c)):
            cp = pltpu.make_async_remote_copy(src_ref=src, dst_ref=dst, send_sem=s_ref.at[i], recv_sem=r_ref.at[i], device_id=dev,
                                              device_id_type=MESH)
            cp.wait_send()
            cp.wait_recv()

    outs = pl.pallas_call(
        body,
        name=name,
        out_shape=_hbm_shapes(bufs),
        in_specs=[HBM_SPEC] * nb + [SEM_SPEC, SEM_SPEC, ANY_SPEC],
        out_specs=[HBM_SPEC] * nb,
        input_output_aliases={i: i for i in range(nb)},
        compiler_params=pltpu.CompilerParams(has_side_effects=EFFECT),
    )(*bufs, ssem, rsem, after)
    return list(outs)


def _gather_ici_plan(n):
    def plan(refs, x, y, c):
        out = []
        for w in range(n):
            rh = refs[w].shape[0] // 2
            for cx, cy in _other_chips(x, y):
                out.append((refs[w].at[pl.ds(c * rh, rh), :], _half_rows(refs[n + w], _chip_id(x, y), c, rh), (cx, cy, c)))
        return out

    return plan


def _gather_d2d_plan(n):
    def plan(refs, x, y, c):
        out = []
        for w in range(n):
            rh = refs[w].shape[1] // 2
            for cx, cy in _other_chips(x, y):
                blk = _half_rows(refs[w], _chip_id(cx, cy), c, rh)
                out.append((blk, blk, (x, y, 1 - c)))
        return out

    return plan


def _dev_id(x, y, c):
    return 4 * x + 2 * y + c


def _small_ici_plan(n):
    def plan(refs, x, y, c):
        out = []
        for w in range(n):
            dst = refs[n + w].at[_dev_id(x, y, c)]
            out.append((refs[w], dst, (x, y, 1 - c)))
            for cx, cy in _other_chips(x, y):
                out.append((refs[w], dst, (cx, cy, c)))
        return out

    return plan


def _small_d2d_plan(n):
    def plan(refs, x, y, c):
        out = []
        for w in range(n):
            for cx, cy in _other_chips(x, y):
                blk = refs[w].at[_dev_id(cx, cy, c)]
                out.append((blk, blk, (x, y, 1 - c)))
        return out

    return plan


def _rs_d2d_plan(n):
    def plan(refs, x, y, c):
        out = []
        for w in range(n):
            rh = refs[w].shape[1] // 2
            out.append((refs[w].at[:, pl.ds((1 - c) * rh, rh), :], refs[n + w], (x, y, 1 - c)))
        return out

    return plan


def _rs_ici_plan(n):
    def plan(refs, x, y, c):
        out = []
        for w in range(n):
            for k, (cx, cy) in enumerate(_other_chips(x, y)):
                out.append((refs[w].at[_chip_id(cx, cy)], refs[n + w].at[k], (cx, cy, c)))
        return out

    return plan


def _rs_share_plan(n):
    def plan(refs, x, y, c):
        out = []
        for w in range(n):
            rh = refs[w].shape[0] // 2
            rows = refs[w].at[pl.ds(c * rh, rh), :]
            out.append((rows, rows, (x, y, 1 - c)))
        return out

    return plan


def rs_add(grad, sibbuf, place, tr, name):
    _, R, C = grad.shape
    nt = (R // 2) // tr

    def body(p_ref, g_ref, s_ref, o_ref):
        o_ref[...] = (g_ref[...] + s_ref[...]).astype(BF16)

    return pl.pallas_call(
        body,
        name=name,
        grid_spec=pltpu.PrefetchScalarGridSpec(
            num_scalar_prefetch=1,
            grid=(N_CHIPS, nt),
            in_specs=[pl.BlockSpec((None, tr, C), lambda j, i, p: (j, p[0] * nt + i, 0)),
                      pl.BlockSpec((None, tr, C), lambda j, i, p: (j, i, 0))],
            out_specs=pl.BlockSpec((None, tr, C), lambda j, i, p: (j, i, 0)),
        ),
        out_shape=jax.ShapeDtypeStruct((N_CHIPS, R // 2, C), BF16),
        compiler_params=_params(2),
    )(place, grad, sibbuf)


def rs_final(grad, sibbuf, rbuf, place, tr, name):
    _, R, C = grad.shape
    nt = (R // 2) // tr

    def body(p_ref, g_ref, s_ref, r_ref, o_ref):
        o_ref[...] = (((g_ref[...] + s_ref[...]) + r_ref[0].astype(F32)) + r_ref[1].astype(F32)) + r_ref[2].astype(F32)

    return pl.pallas_call(
        body,
        name=name,
        grid_spec=pltpu.PrefetchScalarGridSpec(
            num_scalar_prefetch=1,
            grid=(nt,),
            in_specs=[pl.BlockSpec((None, tr, C), lambda i, p: (p[1], p[0] * nt + i, 0)),
                      pl.BlockSpec((None, tr, C), lambda i, p: (p[1], i, 0)),
                      pl.BlockSpec((3, tr, C), lambda i, p: (0, i, 0))],
            out_specs=pl.BlockSpec((tr, C), lambda i, p: (p[0] * nt + i, 0)),
        ),
        out_shape=jax.ShapeDtypeStruct((R, C), F32),
        compiler_params=_params(1),
    )(place, grad, sibbuf, rbuf)


class GradReduce:
    def __init__(self, tag, grads, rows, place):
        self.tag, self.grads, self.rows, self.place = tag, grads, rows, place
        self.n = len(grads)

    def d2d_start(self, carry):
        sib = [lax.empty((N_CHIPS, g.shape[1] // 2, g.shape[2]), F32) for g in self.grads]
        self.s1, self.r1, bufs, carry = split_start(f"rs_{self.tag}_d2d_start", self.grads + sib, _rs_d2d_plan(self.n), self.n, carry)
        self.bufs1 = bufs
        return carry

    def add_and_ici_start(self, after, carry):
        bufs = split_wait(f"rs_{self.tag}_d2d_wait", self.s1, self.r1, self.bufs1, _rs_d2d_plan(self.n), after)
        self.grads, self.sib = bufs[:self.n], bufs[self.n:]
        pb = [rs_add(g, s, self.place, tr, f"rs_{self.tag}_add{w}")
              for w, (g, s, tr) in enumerate(zip(self.grads, self.sib, self.rows))]
        rb = [lax.empty((3,) + p.shape[1:], BF16) for p in pb]
        self.s2, self.r2, self.bufs2, carry = split_start(f"rs_{self.tag}_ici_start", pb + rb, _rs_ici_plan(self.n), 3 * self.n, carry)
        return carry

    def final_and_share_start(self, after, carry):
        bufs = split_wait(f"rs_{self.tag}_ici_wait", self.s2, self.r2, self.bufs2, _rs_ici_plan(self.n), after)
        rb = bufs[self.n:]
        full = [rs_final(g, s, r, self.place, tr, f"rs_{self.tag}_final{w}")
                for w, (g, s, r, tr) in enumerate(zip(self.grads, self.sib, rb, self.rows))]
        self.s3, self.r3, self.bufs3, carry = split_start(f"rs_{self.tag}_share_start", full, _rs_share_plan(self.n), self.n, carry)
        return carry

    def finish(self, after):
        return split_wait(f"rs_{self.tag}_share_wait", self.s3, self.r3, self.bufs3, _rs_share_plan(self.n), after)


def _rope_tables(positions):
    inv_freq = ROPE_THETA ** (-jnp.arange(0, ROT_DIM, 2, dtype=F32) / ROT_DIM)
    ang = positions.astype(F32)[:, None] * inv_freq
    cos, sin = jnp.cos(ang), jnp.sin(ang)
    S = positions.shape[0]
    one, zero = jnp.ones((S, 48), F32), jnp.zeros((S, 48), F32)
    z8 = jnp.zeros((S, 8), F32)
    tc = jnp.concatenate([cos, cos, one], axis=1)
    tsa = jnp.concatenate([z8, sin, zero], axis=1)
    tsb = jnp.concatenate([-sin, z8, zero], axis=1)
    return tuple(jnp.tile(t, (1, 2)) for t in (tc, tsa, tsb))


def _block_diag(w_pool):
    wbd = jnp.zeros((POOL_W, POOL_W), F32)
    for gi in range(4):
        wbd = wbd.at[gi * 64:(gi + 1) * 64, gi * 64:(gi + 1) * 64].set(w_pool[gi])
    return wbd


def kernel(x, c, positions, w_ada, b_ada, g_pre_mix, g_post_mix, g_pre_ffn, g_post_ffn, w_in, w_pool, b_pool, pool_scale, w_out, w_up, conv_w, conv_b, w_down, loss_target, m_w_ada, m_b_ada, m_g_pre_mix, m_g_post_mix, m_g_pre_ffn, m_g_post_ffn, m_w_in, m_w_pool, m_b_pool, m_pool_scale, m_w_out, m_w_up, m_conv_w, m_conv_b, m_w_down, v_w_ada, v_b_ada, v_g_pre_mix, v_g_post_mix, v_g_pre_ffn, v_g_post_ffn, v_w_in, v_w_pool, v_b_pool, v_pool_scale, v_w_out, v_w_up, v_conv_w, v_conv_b, v_w_down):
    xi, yi, ci = lax.axis_index("x"), lax.axis_index("y"), lax.axis_index("c")
    chip = 2 * xi + yi
    place = jnp.stack([ci, chip]).astype(jnp.int32)
    x2, tgt = x[0], loss_target[0]
    S = x2.shape[0]

    def landing(s_):
        return lax.dynamic_update_slice(lax.empty((N_CHIPS,) + s_.shape, s_.dtype), s_[None], (chip, 0, 0))

    cb_ada = w_ada.shape[2]
    b_cols = lax.dynamic_slice(b_ada, (0, chip * cb_ada), (1, cb_ada))
    c_all, mod6, conv_w_g = ada_mod(c.reshape(1, 1, D_MODEL), w_ada[0], b_cols, conv_w[0])
    conv_w_f = jnp.transpose(conv_w_g, (1, 0, 2)).reshape(3, D_FF)
    mix_sh = [w_in[0].astype(BF16), w_out[0].astype(BF16)]
    ffn_sh = [w_up[0].astype(BF16), w_down[0].astype(BF16)]
    ga_s, ga_r, ga_bufs, mod6 = split_start("gather_mix_ici_start", mix_sh + [landing(t) for t in mix_sh], _gather_ici_plan(2), 6, mod6)
    gb_s, gb_r, gb_bufs, (mod6, tc, tsa, tsb) = split_start("gather_ffn_ici_start", ffn_sh + [landing(t) for t in ffn_sh],
                                                            _gather_ici_plan(2), 6, [mod6, *_rope_tables(positions[0])])
    wbd = _block_diag(w_pool[0]).astype(BF16)
    b_pool2, scale2 = b_pool.reshape(1, POOL_W), pool_scale
    ga_bufs = split_wait("gather_mix_ici_wait", ga_s, ga_r, ga_bufs, _gather_ici_plan(2), mod6)
    gc_s, gc_r, mix_land, mod6 = split_start("gather_mix_d2d_start", ga_bufs[2:], _gather_d2d_plan(2), 6, mod6)
    w_in_g, w_out_g = split_wait("gather_mix_d2d_wait", gc_s, gc_r, mix_land, _gather_d2d_plan(2), mod6)

    h1, u, *qkv = inproj_fwd(x2, g_pre_mix, mod6, w_in_g, tc, tsa, tsb)
    mixed, pool = pool_fwd(u, wbd, b_pool2, scale2)
    o_l = [attn_fwd(t, d) for t, d in zip(qkv, DILATIONS)]
    attn_done = sum(l[0, :8, :128] for _, l in o_l)
    gb_bufs = split_wait("gather_ffn_ici_wait", gb_s, gb_r, gb_bufs, _gather_ici_plan(2), attn_done)
    gd_s, gd_r, ffn_land, pool = split_start("gather_ffn_d2d_start", gb_bufs[2:], _gather_d2d_plan(2), 6, pool)
    cat, lse, lse4, lse16, y1, x1, h2 = outproj_fwd([o for o, _ in o_l] + [l for _, l in o_l], pool, x2, w_out_g, g_post_mix,
                                                    g_pre_ffn, mod6)
    lses = [lse[None], lse4, lse16]
    w_up_g, w_down_g = split_wait("gather_ffn_d2d_wait", gd_s, gd_r, ffn_land, _gather_d2d_plan(2), h2)
    w_down_f = w_down_g.reshape(D_FF, D_MODEL)
    gate, val, a, dy2, dout, loss_v, d_gt_f, d_g_post_ffn = ffn_fwd(h2, w_up_g, conv_w_f, conv_b, w_down_f, x1, tgt, g_post_ffn, mod6)

    dgc, dval, d_conv_w, d_conv_b, dw_down, dw_up = down_bwd(dy2, w_down_f, gate, val, conv_w_f, conv_b, a, h2)
    dx1, dy1, d_sh_f, d_sc_f, d_g_pre_ffn, d_gt_m, d_g_post_mix, dw_up = up_bwd(
        dgc, dval, conv_w_f, w_up_g, x1, dout, y1, g_pre_ffn, g_post_mix, mod6, h2, dw_up)
    rs_ffn = GradReduce("ffn", [dw_up, dw_down.reshape(N_CHIPS, D_FF // N_CHIPS, D_MODEL)], [256, 176], place)
    dy1 = rs_ffn.d2d_start(dy1)
    dpool, da1, da4, da16, dl1, dl4, dl16, dw_out = outproj_bwd(dy1, w_out_g, cat)
    dpool = rs_ffn.add_and_ici_start(dw_out, dpool)
    du, d_wbd, d_b_pool, d_scale = pool_bwd(dpool, mixed, wbd, b_pool2, scale2)
    dqkv = [attn_bwd(t, da, ls, dl, d) for t, da, ls, dl, d in zip(qkv, (da1[None], da4, da16), lses, (dl1[None], dl4, dl16), DILATIONS)]
    grad_x, d_sh_m, d_sc_m, d_g_pre_mix, dw_in = inproj_bwd(dqkv, du, x2, dx1, w_in_g, g_pre_mix, mod6, tc, tsa, tsb, h1)

    z1 = jnp.zeros((1, D_MODEL), F32)
    slab_a = jnp.concatenate(
        [d_sh_m, d_sc_m, d_gt_m, d_sh_f, d_sc_f, d_gt_f, d_g_pre_mix, d_g_post_mix, d_g_pre_ffn, d_g_post_ffn,
         jnp.concatenate([d_b_pool, d_scale, loss_v, jnp.zeros((1, 384), F32)], axis=1)] + [z1] * 5, axis=0)
    slab_b = jnp.concatenate([d_conv_w, d_conv_b, jnp.zeros((4, D_FF), F32)], axis=0)
    d_wpool = jnp.concatenate([d_wbd[gi * 64:(gi + 1) * 64, gi * 64:(gi + 1) * 64] for gi in range(4)], axis=0)
    dev = _dev_id(xi, yi, ci)
    small_src = [slab_a, slab_b, d_wpool]
    small_land = [lax.dynamic_update_slice(lax.empty((N_DEV,) + t.shape, F32), t[None], (dev, 0, 0)) for t in small_src]
    tok = jnp.zeros((8, 128), F32)
    gs_s, gs_r, gs_bufs, tok = split_start("small_ici_start", small_src + small_land, _small_ici_plan(3), 12, tok)
    rs_mix = GradReduce("mix", [dw_in, dw_out], [256, 256], place)
    tok = rs_mix.d2d_start(tok)
    tok = rs_ffn.final_and_share_start(tok, tok)
    gs_bufs = split_wait("small_ici_wait", gs_s, gs_r, gs_bufs, _small_ici_plan(3), tok)
    gt_s, gt_r, small_land, tok = split_start("small_d2d_start", gs_bufs[3:], _small_d2d_plan(3), 9, tok)
    tok = rs_mix.add_and_ici_start(tok, tok)
    slab_a_g, slab_b_g, wpool_g = split_wait("small_d2d_wait", gt_s, gt_r, small_land, _small_d2d_plan(3), tok)
    cw_cols = conv_w.shape[2]
    convw_g = lax.dynamic_slice(slab_b_g, (0, 0, chip * cw_cols), (N_DEV, 3, cw_cols))
    dmod_cols = lax.dynamic_slice(slab_a_g[:, :6, :].reshape(N_DEV, 6 * D_MODEL), (0, chip * cb_ada), (N_DEV, cb_ada))

    res = {}

    def big_adamw(name, w, g, m, v, tr):
        g_, d_, m_, v_ = adamw_rows(w[0], g, m[0], v[0], tr, "adamw_" + name)
        res[name] = (g_[None], d_[None], m_[None], v_[None])
        return v_

    g_ada, d_ada, m_ada, v_ada = adamw_ada(c_all.reshape(N_DEV, D_MODEL).T, dmod_cols, w_ada[0], m_w_ada[0], v_w_ada[0])
    res["w_ada"] = (g_ada[None], d_ada[None], m_ada[None], v_ada[None])
    g_w_up, g_w_down = rs_ffn.finish(v_ada)
    big_adamw("w_up", w_up, g_w_up, m_w_up, v_w_up, 256)
    last = big_adamw("w_down", w_down, g_w_down, m_w_down, v_w_down, 352)
    rs_mix.final_and_share_start(last, jnp.zeros((8, 128), F32))
    g_w_in, g_w_out = rs_mix.finish(last)
    big_adamw("w_in", w_in, g_w_in, m_w_in, v_w_in, 256)
    big_adamw("w_out", w_out, g_w_out, m_w_out, v_w_out, 256)
    small, loss_sum = adamw_small(slab_a_g, slab_b_g, convw_g, wpool_g, {
        "b_ada": (b_ada, m_b_ada, v_b_ada), "g_pre_mix": (g_pre_mix, m_g_pre_mix, v_g_pre_mix),
        "g_post_mix": (g_post_mix, m_g_post_mix, v_g_post_mix), "g_pre_ffn": (g_pre_ffn, m_g_pre_ffn, v_g_pre_ffn),
        "g_post_ffn": (g_post_ffn, m_g_post_ffn, v_g_post_ffn), "b_pool": (b_pool, m_b_pool, v_b_pool),
        "pool_scale": (pool_scale, m_pool_scale, v_pool_scale), "conv_b": (conv_b, m_conv_b, v_conv_b),
        "conv_w": (conv_w, m_conv_w, v_conv_w), "w_pool": (w_pool, m_w_pool, v_w_pool)})
    for name in ("b_ada", "g_pre_mix", "g_post_mix", "g_pre_ffn", "g_post_ffn", "pool_scale", "conv_b", "b_pool", "w_pool", "conv_w"):
        res[name] = tuple(small[name])

    loss = loss_sum[0, 0]
    order = ["w_ada", "b_ada", "g_pre_mix", "g_post_mix", "g_pre_ffn", "g_post_ffn", "w_in", "w_pool", "b_pool", "pool_scale",
             "w_out", "w_up", "conv_w", "conv_b", "w_down"]
    outs = [loss, grad_x[None]]
    for k in range(4):
        outs += [res[n][k] for n in order]
    return tuple(outs)
```

```python
import math

import jax
import jax.numpy as jnp
from jax import lax
from jax.experimental import pallas as pl
from jax.experimental.pallas import tpu as pltpu

F32 = jnp.float32
BF16 = jnp.bfloat16
MESH = pl.DeviceIdType.MESH

D_MODEL = 1024
HEAD_DIM = 64
POOL_W = 256
GROUP_W = 256
DILATIONS = (1, 4, 16)
ATT_BLOCK = 128
IN_W = 2560
D_FF = 2816
HALF_FF = 1408
ROT_DIM = 16
ROPE_THETA = 500000.0
NORM_EPS = 1e-6
N_CHIPS = 4
N_DEV = 8
NEG = -1e30

ADAM_LR = 0.001
ADAM_B1 = 0.9
ADAM_B2 = 0.999
ADAM_EPS = 1e-08
ADAM_WD = 0.01
ADAM_STEP = 10

VMEM_LIMIT = 56 * 1024 * 1024

NT = (((1,), (1,)), ((), ()))
TN = (((0,), (0,)), ((), ()))


def _params(n_grid=0, **kw):
    sem = ("arbitrary",) * n_grid if n_grid else None
    return pltpu.CompilerParams(dimension_semantics=sem, vmem_limit_bytes=VMEM_LIMIT, **kw)


def _full(shape):
    nd = len(shape)
    return pl.BlockSpec(tuple(shape), lambda *_: (0,) * nd, pipeline_mode=pl.Buffered(1))


def _rows(tm, ncol):
    return pl.BlockSpec((tm, ncol), lambda i: (i, 0))


def _acc(ref, val):
    @pl.when(pl.program_id(0) == 0)
    def _():
        ref[...] = jnp.zeros_like(ref)

    ref[...] += val


def _colsum(v):
    return jnp.sum(v, axis=0, keepdims=True)


def _rope128(t, cs, sa, sb, sign):
    return t * cs + sign * (pltpu.roll(t, 8, 1) * sa + pltpu.roll(t, 120, 1) * sb)


FF_CHUNKS = tuple((ch, off, w) for ch in range(2) for off, w in ((0, 512), (512, 512), (1024, 384)))
GELU_C0 = math.sqrt(2.0 / math.pi)
GELU_C1 = GELU_C0 * 0.044715


def _gelu(z):
    z2 = z * z
    t = jnp.tanh(z * (GELU_C0 + GELU_C1 * z2))
    u = 0.5 * t + 0.5
    return z * u, u, t, z2


def _gelu_grad(z, u, t, z2):
    return u + (z * (GELU_C0 + (3.0 * GELU_C1) * z2)) * (0.5 - 0.5 * (t * t))


def _conv_taps(gate, halo, first):
    row = lax.broadcasted_iota(jnp.int32, gate.shape, 0)
    halo = jnp.where(first, 0.0, halo)
    nh = halo.shape[0]
    p1 = halo[nh - 1:nh, :]
    p2 = halo[nh - 2:nh - 1, :]
    g1 = jnp.where(row == 0, p1, pltpu.roll(gate, 1, 0))
    g2 = jnp.where(row == 0, p2, jnp.where(row == 1, p1, pltpu.roll(gate, 2, 0)))
    return g1, g2


def inproj_fwd(x, g, mod6, w_in_g, tc, tsa, tsb, tm=512):
    S = x.shape[0]

    def body(x_ref, g_ref, mod_ref, w_ref, tc_ref, tsa_ref, tsb_ref, h_ref, u_ref, q1_ref, q4_ref, q16_ref, scr):
        qkv_refs = (q1_ref, q4_ref, q16_ref)
        xv = x_ref[...]
        rstd = lax.rsqrt(jnp.mean(xv * xv, axis=-1, keepdims=True) + NORM_EPS)
        h = ((xv * rstd) * g_ref[...]) * (1.0 + mod_ref[1:2, :]) + mod_ref[0:1, :]
        hb = h.astype(BF16)
        h_ref[...] = hb
        cs, sa, sb = tc_ref[...], tsa_ref[...], tsb_ref[...]
        for j in range(N_CHIPS):
            res = jnp.dot(hb, w_ref[j], preferred_element_type=F32)
            for t in range(5):
                sp = 5 * j + t
                piece, half = sp // 2, sp % 2
                blk = res[:, t * 128:(t + 1) * 128]
                lanes = slice(half * 128, (half + 1) * 128)
                if piece == 0:
                    u_ref[:, lanes] = blk
                else:
                    kind, gi = (piece - 1) // 3, (piece - 1) % 3
                    if kind == 0:
                        blk = _rope128(blk, cs, sa, sb, 1.0) * (HEAD_DIM ** -0.5)
                    elif kind == 1:
                        blk = _rope128(blk, cs, sa, sb, 1.0)
                    d = DILATIONS[gi]
                    if d == 1:
                        q1_ref[kind, 0, :, lanes] = blk.astype(BF16)
                    else:
                        scr[...] = blk
                        for r in range(d):
                            qkv_refs[gi][kind, r, :, lanes] = scr[pl.ds(r, tm // d, stride=d), :].astype(BF16)

    cls = lambda d: pl.BlockSpec((3, d, tm // d, GROUP_W), lambda i: (0, 0, i, 0))
    return pl.pallas_call(
        body,
        name="inproj_fwd",
        grid=(S // tm,),
        in_specs=[_rows(tm, D_MODEL), _full((1, D_MODEL)), _full((6, D_MODEL)), _full(w_in_g.shape),
                  _rows(tm, 128), _rows(tm, 128), _rows(tm, 128)],
        out_specs=[_rows(tm, D_MODEL), _rows(tm, POOL_W)] + [cls(d) for d in DILATIONS],
        out_shape=[jax.ShapeDtypeStruct((S, D_MODEL), BF16), jax.ShapeDtypeStruct((S, POOL_W), F32)]
        + [jax.ShapeDtypeStruct((3, d, S // d, GROUP_W), BF16) for d in DILATIONS],
        scratch_shapes=[pltpu.VMEM((tm, 128), F32)],
        compiler_params=_params(1),
    )(x, g, mod6, w_in_g, tc, tsa, tsb)


def _attn_masks():
    row = lax.broadcasted_iota(jnp.int32, (2 * ATT_BLOCK, 2 * ATT_BLOCK), 0) % ATT_BLOCK
    col = lax.broadcasted_iota(jnp.int32, (2 * ATT_BLOCK, 2 * ATT_BLOCK), 1)
    band = (col >= row) & (col <= row + ATT_BLOCK)
    lane = lax.broadcasted_iota(jnp.int32, (ATT_BLOCK, 128), 1)
    return band, col, lane < HEAD_DIM


def _classes_per_step(d, nb):
    return min(d, max(1, 8 // nb))


def _stack_heads(t, lo):
    z = jnp.zeros_like(t)
    return jnp.concatenate([jnp.where(lo, t, z), jnp.where(lo, z, t)], axis=0)


def _unstack_heads(t2, lo):
    return jnp.where(lo, t2[:ATT_BLOCK], t2[ATT_BLOCK:])


def attn_fwd(qkv, d):
    L = qkv.shape[2]
    nb = L // ATT_BLOCK
    cpb = _classes_per_step(d, nb)

    def body(q_ref, k_ref, v_ref, o_ref, l_ref, kpad, vpad):
        for cls in range(cpb):
            kpad[cls, 0:ATT_BLOCK, :] = jnp.zeros((ATT_BLOCK, GROUP_W), BF16)
            vpad[cls, 0:ATT_BLOCK, :] = jnp.zeros((ATT_BLOCK, GROUP_W), BF16)
            kpad[cls, ATT_BLOCK:, :] = k_ref[cls]
            vpad[cls, ATT_BLOCK:, :] = v_ref[cls]
        band, col, lo = _attn_masks()

        def step(t, carry):
            cls, n = t // nb, t % nb
            r0 = pl.multiple_of(n * ATT_BLOCK, ATT_BLOCK)
            valid = band & ((col >= ATT_BLOCK) | (n > 0))
            qb = q_ref[cls, pl.ds(r0, ATT_BLOCK), :]
            kb = kpad[cls, pl.ds(r0, 2 * ATT_BLOCK), :]
            vb = vpad[cls, pl.ds(r0, 2 * ATT_BLOCK), :]
            for pair in range(2):
                lanes = slice(pair * 128, (pair + 1) * 128)
                qp, kp, vp = qb[:, lanes], kb[:, lanes], vb[:, lanes]
                s = lax.dot_general(_stack_heads(qp, lo), kp, NT, preferred_element_type=F32)
                s = jnp.where(valid, s, NEG)
                m = jnp.max(s, axis=1, keepdims=True)
                p = jnp.exp(s - m)
                den = jnp.sum(p, axis=1, keepdims=True)
                pv = jnp.dot(p.astype(BF16), vp, preferred_element_type=F32)
                o_ref[cls, pl.ds(r0, ATT_BLOCK), lanes] = _unstack_heads(pv / den, lo)
                l_ref[cls, pl.ds(r0, ATT_BLOCK), lanes] = _unstack_heads(jnp.broadcast_to(m + jnp.log(den), pv.shape), lo)
            return carry

        lax.fori_loop(0, cpb * nb, step, 0, unroll=4)

    spec = lambda kind: pl.BlockSpec((None, cpb, L, GROUP_W), lambda r: (kind, r, 0, 0))
    return pl.pallas_call(
        body,
        name=f"attn_fwd_d{d}",
        grid=(d // cpb,),
        in_specs=[spec(0), spec(1), spec(2)],
        out_specs=[pl.BlockSpec((cpb, L, GROUP_W), lambda r: (r, 0, 0))] * 2,
        out_shape=[jax.ShapeDtypeStruct((d, L, GROUP_W), F32)] * 2,
        scratch_shapes=[pltpu.VMEM((cpb, L + ATT_BLOCK, GROUP_W), BF16)] * 2,
        compiler_params=_params(1),
    )(qkv, qkv, qkv)


def _pool_lane_windows(shape):
    lane = lax.broadcasted_iota(jnp.int32, shape, 1)
    return lane, jnp.where(lane < 64, 2, jnp.where(lane < 128, 4, jnp.where(lane < 192, 8, 16)))


def pool_fwd(u, wbd, b, scale):
    S = u.shape[0]

    def body(u_ref, w_ref, b_ref, s_ref, mixed_ref, out_ref):
        uv = u_ref[...]
        row = lax.broadcasted_iota(jnp.int32, uv.shape, 0)
        lane, win = _pool_lane_windows(uv.shape)

        def shift(a, k):
            return jnp.where(row >= k, pltpu.roll(a, k, 0), 0.0)

        s2 = uv + shift(uv, 1)
        s4 = s2 + shift(s2, 2)
        s8 = s4 + shift(s4, 4)
        s16 = s8 + shift(s8, 8)
        tsum = jnp.where(lane < 64, s2, jnp.where(lane < 128, s4, jnp.where(lane < 192, s8, s16)))
        cnt = jnp.minimum(row + 1, win).astype(F32)
        mb = (tsum / cnt - uv).astype(BF16)
        mixed_ref[...] = mb
        y = jnp.dot(mb, w_ref[...], preferred_element_type=F32) + b_ref[...]
        out_ref[...] = (y * s_ref[...]).astype(BF16)

    vm = pl.BlockSpec(memory_space=pltpu.VMEM)
    return pl.pallas_call(
        body,
        name="pool_fwd",
        in_specs=[vm] * 4,
        out_specs=[vm] * 2,
        out_shape=[jax.ShapeDtypeStruct((S, POOL_W), BF16)] * 2,
        compiler_params=_params(),
    )(u, wbd, b, scale)


def outproj_fwd(o_l, pool, x, w_out_g, g_post, g_pre, mod6, tm=512):
    S = x.shape[0]

    def body(o0, o1, o2, l0, l1, l2, pool_ref, x_ref, w_ref, gpost_ref, gpre_ref, mod_ref,
             cat_ref, lse_ref, lse4_ref, lse16_ref, y1_ref, x1_ref, h2_ref, so4, sl4, so16, sl16):
        for d, src, dst in ((4, o1, so4), (4, l1, sl4), (16, o2, so16), (16, l2, sl16)):
            for r in range(d):
                for h in range(2):
                    dst[h, pl.ds(r, tm // d, stride=d), :] = src[r, :, h * 128:(h + 1) * 128]
        nat = lambda ref: jnp.concatenate([ref[0], ref[1]], axis=1)
        a, b, c = l0[0], nat(sl4), nat(sl16)
        m = jnp.maximum(jnp.maximum(a, b), c)
        e0, e1, e2 = jnp.exp(a - m), jnp.exp(b - m), jnp.exp(c - m)
        z = e0 + e1 + e2
        lse = m + jnp.log(z)
        lse_ref[...] = lse
        for h in range(2):
            sl4[h] = lse[:, h * 128:(h + 1) * 128]
        for d, dst in ((4, lse4_ref), (16, lse16_ref)):
            for r in range(d):
                for h in range(2):
                    dst[r, :, h * 128:(h + 1) * 128] = sl4[h, pl.ds(r, tm // d, stride=d), :]
        attn = (e0 * o0[0] + e1 * nat(so4) + e2 * nat(so16)) / z
        cat = jnp.concatenate([pool_ref[...], attn.astype(BF16)], axis=1)
        cat_ref[...] = cat
        y1 = jnp.concatenate([jnp.dot(cat, w_ref[j], preferred_element_type=F32) for j in range(N_CHIPS)], axis=1)
        y1_ref[...] = y1
        rstd = lax.rsqrt(jnp.mean(y1 * y1, axis=-1, keepdims=True) + NORM_EPS)
        x1 = x_ref[...] + mod_ref[2:3, :] * ((y1 * rstd) * gpost_ref[...])
        x1_ref[...] = x1
        rstd2 = lax.rsqrt(jnp.mean(x1 * x1, axis=-1, keepdims=True) + NORM_EPS)
        h2 = ((x1 * rstd2) * gpre_ref[...]) * (1.0 + mod_ref[4:5, :]) + mod_ref[3:4, :]
        h2_ref[...] = h2.astype(BF16)

    t256 = _rows(tm, GROUP_W)
    cls = lambda d: pl.BlockSpec((d, tm // d, GROUP_W), lambda i: (0, i, 0))
    cls_shape = lambda d: jax.ShapeDtypeStruct((d, S // d, GROUP_W), F32)
    return pl.pallas_call(
        body,
        name="outproj_fwd",
        grid=(S // tm,),
        in_specs=[cls(d) for d in DILATIONS] * 2 + [t256, _rows(tm, D_MODEL), _full(w_out_g.shape), _full((1, D_MODEL)),
                                                    _full((1, D_MODEL)), _full((6, D_MODEL))],
        out_specs=[_rows(tm, 512), t256, cls(4), cls(16), _rows(tm, D_MODEL), _rows(tm, D_MODEL), _rows(tm, D_MODEL)],
        out_shape=[jax.ShapeDtypeStruct((S, 512), BF16), jax.ShapeDtypeStruct((S, GROUP_W), F32), cls_shape(4), cls_shape(16),
                   jax.ShapeDtypeStruct((S, D_MODEL), F32), jax.ShapeDtypeStruct((S, D_MODEL), F32),
                   jax.ShapeDtypeStruct((S, D_MODEL), BF16)],
        scratch_shapes=[pltpu.VMEM((2, tm, 128), F32)] * 4,
        compiler_params=_params(1),
    )(*o_l, pool, x, w_out_g, g_post, g_pre, mod6)


def _halo_prev(tm, ncol):
    return pl.BlockSpec((16, ncol), lambda i: (jnp.maximum(i * (tm // 16) - 1, 0), 0))


def ffn_fwd(h2, w_up_g, conv_w, conv_b, w_down, x1, target, g_post, mod6, tm=256):
    S = x1.shape[0]

    def body(h_ref, wu_ref, cw_ref, cb_ref, wd_ref, x1_ref, tgt_ref, g_ref, mod_ref,
             gate_ref, val_ref, a_ref, dy2_ref, dout_ref, loss_ref, dgt_ref, dg_ref, carry):
        first = pl.program_id(0) == 0

        @pl.when(first)
        def _():
            carry[...] = jnp.zeros_like(carry)

        hb = h_ref[...]
        y2 = jnp.zeros((tm, D_MODEL), F32)
        for ch in range(2):
            cols = slice(ch * HALF_FF, (ch + 1) * HALF_FF)
            gb = jnp.dot(hb, wu_ref[ch], preferred_element_type=F32).astype(BF16)
            vb = jnp.dot(hb, wu_ref[2 + ch], preferred_element_type=F32).astype(BF16)
            gate_ref[:, cols] = gb
            val_ref[:, cols] = vb
            gt = gb.astype(F32)
            g1, g2 = _conv_taps(gt, carry[:, cols], first)
            carry[:, cols] = gt[tm - 8:, :]
            gc = g2 * cw_ref[0:1, cols] + g1 * cw_ref[1:2, cols] + gt * cw_ref[2:3, cols] + cb_ref[:, cols]
            ab = (_gelu(gc)[0] * vb.astype(F32)).astype(BF16)
            a_ref[:, cols] = ab
            y2 = y2 + jnp.dot(ab, wd_ref[cols, :], preferred_element_type=F32)
        rstd = lax.rsqrt(jnp.mean(y2 * y2, axis=-1, keepdims=True) + NORM_EPS)
        y2n = y2 * rstd
        gv = g_ref[...]
        gtf = mod_ref[5:6, :]
        r2 = y2n * gv
        diff = (x1_ref[...] + gtf * r2) - tgt_ref[...]
        _acc(loss_ref, jnp.zeros((1, 128), F32) + 0.5 * jnp.sum(diff * diff) * (1.0 / D_MODEL))
        dout = diff * (1.0 / D_MODEL)
        dout_ref[...] = dout
        _acc(dgt_ref, _colsum(dout * r2))
        dr2 = dout * gtf
        _acc(dg_ref, _colsum(dr2 * y2n))
        dyn = dr2 * gv
        dy2 = rstd * (dyn - y2n * jnp.mean(dyn * y2n, axis=-1, keepdims=True))
        dy2_ref[...] = dy2.astype(BF16)

    vec = _full((1, D_MODEL))
    return pl.pallas_call(
        body,
        name="ffn_fwd",
        grid=(S // tm,),
        in_specs=[_rows(tm, D_MODEL), _full(w_up_g.shape), _full((3, D_FF)), _full((1, D_FF)), _full((D_FF, D_MODEL)),
                  _rows(tm, D_MODEL), _rows(tm, D_MODEL), vec, _full((6, D_MODEL))],
        out_specs=[_rows(tm, D_FF), _rows(tm, D_FF), _rows(tm, D_FF), _rows(tm, D_MODEL), _rows(tm, D_MODEL), _full((1, 128)), vec, vec],
        out_shape=[jax.ShapeDtypeStruct((S, D_FF), BF16)] * 3 + [jax.ShapeDtypeStruct((S, D_MODEL), BF16),
                                                                 jax.ShapeDtypeStruct((S, D_MODEL), F32),
                                                                 jax.ShapeDtypeStruct((1, 128), F32),
                                                                 jax.ShapeDtypeStruct((1, D_MODEL), F32),
                                                                 jax.ShapeDtypeStruct((1, D_MODEL), F32)],
        scratch_shapes=[pltpu.VMEM((8, D_FF), F32)],
        compiler_params=_params(1),
    )(h2, w_up_g, conv_w, conv_b, w_down, x1, target, g_post, mod6)


def down_bwd(dy2, w_down, gate, val, conv_w, conv_b, a, h2, tm=256):
    S = dy2.shape[0]

    def body(dy_ref, w_ref, gate_ref, halo_ref, val_ref, cw_ref, cb_ref, a_ref, h_ref,
             dgc_ref, dval_ref, dcw_ref, dcb_ref, dwd_ref, dwu_ref):
        first = pl.program_id(0) == 0

        @pl.when(first)
        def _():
            dcw_ref[...] = jnp.zeros_like(dcw_ref)
            dcb_ref[...] = jnp.zeros_like(dcb_ref)
            dwd_ref[...] = jnp.zeros_like(dwd_ref)
            dwu_ref[...] = jnp.zeros_like(dwu_ref)

        dyb = dy_ref[...]
        hb = h_ref[...]
        def col(i):
            ch, off, width = FF_CHUNKS[i]
            return slice(ch * HALF_FF + off, ch * HALF_FF + off + width)

        def mm_da(i):
            return lax.dot_general(dyb, w_ref[col(i), :], NT, preferred_element_type=F32)

        def elementwise(i, da):
            cols = col(i)
            gt = gate_ref[:, cols].astype(F32)
            g1, g2 = _conv_taps(gt, halo_ref[:, cols].astype(F32), first)
            gc = g2 * cw_ref[0:1, cols] + g1 * cw_ref[1:2, cols] + gt * cw_ref[2:3, cols] + cb_ref[:, cols]
            ge, u, th, z2 = _gelu(gc)
            dgc = da * val_ref[:, cols].astype(F32) * _gelu_grad(gc, u, th, z2)
            dgc_ref[:, cols] = dgc.astype(BF16)
            dvb = (da * ge).astype(BF16)
            dval_ref[:, cols] = dvb
            dcb_ref[:, cols] += _colsum(dgc)
            dcw_ref[0:1, cols] += _colsum(dgc * g2)
            dcw_ref[1:2, cols] += _colsum(dgc * g1)
            dcw_ref[2:3, cols] += _colsum(dgc * gt)
            return dvb

        def mm_dw(i, dvb):
            ch, off, width = FF_CHUNKS[i]
            dwd_ref[col(i), :] += lax.dot_general(a_ref[:, col(i)], dyb, TN, preferred_element_type=F32)
            dwu_ref[ch, :, off:off + width] += lax.dot_general(hb, dvb, TN, preferred_element_type=F32)

        n = len(FF_CHUNKS)
        da = mm_da(0)
        prev = None
        for i in range(n):
            nxt = mm_da(i + 1) if i + 1 < n else None
            if prev is not None:
                mm_dw(i - 1, prev)
            prev = elementwise(i, da)
            da = nxt
        mm_dw(n - 1, prev)

    return pl.pallas_call(
        body,
        name="down_bwd",
        grid=(S // tm,),
        in_specs=[_rows(tm, D_MODEL), _full((D_FF, D_MODEL)), _rows(tm, D_FF), _halo_prev(tm, D_FF), _rows(tm, D_FF),
                  _full((3, D_FF)), _full((1, D_FF)), _rows(tm, D_FF), _rows(tm, D_MODEL)],
        out_specs=[_rows(tm, D_FF), _rows(tm, D_FF), _full((3, D_FF)), _full((1, D_FF)), _full((D_FF, D_MODEL)),
                   pl.BlockSpec((2, D_MODEL, HALF_FF), lambda i: (1, 0, 0), pipeline_mode=pl.Buffered(1))],
        out_shape=[jax.ShapeDtypeStruct((S, D_FF), BF16), jax.ShapeDtypeStruct((S, D_FF), BF16),
                   jax.ShapeDtypeStruct((3, D_FF), F32), jax.ShapeDtypeStruct((1, D_FF), F32),
                   jax.ShapeDtypeStruct((D_FF, D_MODEL), F32), jax.ShapeDtypeStruct((N_CHIPS, D_MODEL, HALF_FF), F32)],
        compiler_params=_params(1),
    )(dy2, w_down, gate, gate, val, conv_w, conv_b, a, h2)


def up_bwd(dgc, dval, conv_w, w_up_g, x1, dout, y1, g_pre, g_post, mod6, h2, dw_up, tm=256):
    S = x1.shape[0]
    last_blk = S // 16 - 1

    def body(dgc_ref, nxt_ref, dval_ref, cw_ref, w_ref, x1_ref, dout_ref, y1_ref, gpre_ref, gpost_ref, mod_ref, h_ref, dwin_ref,
             dx1_ref, dy1_ref, dsh_ref, dsc_ref, dgpre_ref, dgt_ref, dgpost_ref, dwu_ref):
        last = pl.program_id(0) == pl.num_programs(0) - 1

        @pl.when(pl.program_id(0) == 0)
        def _():
            dwu_ref[...] = jnp.zeros_like(dwu_ref)

        hb = h_ref[...]
        dh = jnp.zeros((tm, D_MODEL), F32)
        for ch in range(2):
            cols = slice(ch * HALF_FF, (ch + 1) * HALF_FF)
            dg = dgc_ref[:, cols].astype(F32)
            nx = jnp.where(last, 0.0, nxt_ref[:, cols].astype(F32))
            row = lax.broadcasted_iota(jnp.int32, dg.shape, 0)
            n0, n1 = nx[0:1, :], nx[1:2, :]
            u1 = jnp.where(row == tm - 1, n0, pltpu.roll(dg, tm - 1, 0))
            u2 = jnp.where(row == tm - 1, n1, jnp.where(row == tm - 2, n0, pltpu.roll(dg, tm - 2, 0)))
            dgate = (dg * cw_ref[2:3, cols] + u1 * cw_ref[1:2, cols] + u2 * cw_ref[0:1, cols]).astype(BF16)
            dwu_ref[ch] += lax.dot_general(hb, dgate, TN, preferred_element_type=F32)
            dh = dh + lax.dot_general(dgate, w_ref[ch], NT, preferred_element_type=F32)
            dh = dh + lax.dot_general(dval_ref[:, cols], w_ref[2 + ch], NT, preferred_element_type=F32)
        x1 = x1_ref[...]
        rstd = lax.rsqrt(jnp.mean(x1 * x1, axis=-1, keepdims=True) + NORM_EPS)
        n2 = x1 * rstd
        gpre = gpre_ref[...]
        one_sc = 1.0 + mod_ref[4:5, :]
        _acc(dsh_ref, _colsum(dh))
        _acc(dsc_ref, _colsum(dh * (n2 * gpre)))
        _acc(dgpre_ref, _colsum(dh * one_sc * n2))
        dn = dh * (gpre * one_sc)
        dx1 = dout_ref[...] + rstd * (dn - n2 * jnp.mean(dn * n2, axis=-1, keepdims=True))
        dx1_ref[...] = dx1
        y1 = y1_ref[...]
        rstd1 = lax.rsqrt(jnp.mean(y1 * y1, axis=-1, keepdims=True) + NORM_EPS)
        y1n = y1 * rstd1
        gpost = gpost_ref[...]
        gtm = mod_ref[2:3, :]
        _acc(dgt_ref, _colsum(dx1 * (y1n * gpost)))
        dr1 = dx1 * gtm
        _acc(dgpost_ref, _colsum(dr1 * y1n))
        dyn = dr1 * gpost
        dy1 = rstd1 * (dyn - y1n * jnp.mean(dyn * y1n, axis=-1, keepdims=True))
        dy1_ref[...] = dy1.astype(BF16)

    vec = _full((1, D_MODEL))
    nxt = pl.BlockSpec((16, D_FF), lambda i: (jnp.minimum((i + 1) * (tm // 16), last_blk), 0))
    return pl.pallas_call(
        body,
        name="up_bwd",
        grid=(S // tm,),
        in_specs=[_rows(tm, D_FF), nxt, _rows(tm, D_FF), _full((3, D_FF)), _full(w_up_g.shape), _rows(tm, D_MODEL),
                  _rows(tm, D_MODEL), _rows(tm, D_MODEL), vec, vec, _full((6, D_MODEL)), _rows(tm, D_MODEL),
                  pl.BlockSpec(memory_space=pl.ANY)],
        out_specs=[_rows(tm, D_MODEL), _rows(tm, D_MODEL), vec, vec, vec, vec, vec,
                   pl.BlockSpec((2, D_MODEL, HALF_FF), lambda i: (0, 0, 0), pipeline_mode=pl.Buffered(1))],
        out_shape=[jax.ShapeDtypeStruct((S, D_MODEL), F32), jax.ShapeDtypeStruct((S, D_MODEL), BF16)]
        + [jax.ShapeDtypeStruct((1, D_MODEL), F32)] * 5 + [jax.ShapeDtypeStruct(dw_up.shape, F32)],
        input_output_aliases={12: 7},
        compiler_params=_params(1),
    )(dgc, dgc, dval, conv_w, w_up_g, x1, dout, y1, g_pre, g_post, mod6, h2, dw_up)


def outproj_bwd(dy1, w_out_g, cat, tm=512):
    S = dy1.shape[0]

    def body(dy_ref, w_ref, cat_ref, dpool_ref, dattn_ref, da4_ref, da16_ref, delta_ref, dl4_ref, dl16_ref, dw_ref, scr):
        @pl.when(pl.program_id(0) == 0)
        def _():
            dw_ref[...] = jnp.zeros_like(dw_ref)

        catb = cat_ref[...]
        dcat = jnp.zeros((tm, 512), F32)
        for j in range(N_CHIPS):
            dyj = dy_ref[:, j * 256:(j + 1) * 256]
            dcat = dcat + lax.dot_general(dyj, w_ref[j], NT, preferred_element_type=F32)
            dw_ref[j] += lax.dot_general(catb, dyj, TN, preferred_element_type=F32)
        dpool_ref[...] = dcat[:, :POOL_W]
        dattn = dcat[:, POOL_W:]
        dattn_ref[...] = dattn.astype(BF16)
        for h in range(2):
            scr[h] = dattn[:, h * 128:(h + 1) * 128]
        for d, dst in ((4, da4_ref), (16, da16_ref)):
            for r in range(d):
                for h in range(2):
                    dst[r, :, h * 128:(h + 1) * 128] = scr[h, pl.ds(r, tm // d, stride=d), :].astype(BF16)
        prod = dattn * catb[:, POOL_W:].astype(F32)
        r = lax.broadcasted_iota(jnp.int32, (GROUP_W, GROUP_W), 0) // HEAD_DIM
        c = lax.broadcasted_iota(jnp.int32, (GROUP_W, GROUP_W), 1) // HEAD_DIM
        ones_bd = jnp.where(r == c, 1.0, 0.0).astype(BF16)
        hi = prod.astype(BF16)
        lo = (prod - hi.astype(F32)).astype(BF16)
        delta = jnp.dot(hi, ones_bd, preferred_element_type=F32) + jnp.dot(lo, ones_bd, preferred_element_type=F32)
        delta_ref[...] = delta
        for h in range(2):
            scr[h] = delta[:, h * 128:(h + 1) * 128]
        for d, dst in ((4, dl4_ref), (16, dl16_ref)):
            for r in range(d):
                for h in range(2):
                    dst[r, :, h * 128:(h + 1) * 128] = scr[h, pl.ds(r, tm // d, stride=d), :]

    cls = lambda d: pl.BlockSpec((d, tm // d, GROUP_W), lambda i: (0, i, 0))
    cls_shape = lambda d, dt: jax.ShapeDtypeStruct((d, S // d, GROUP_W), dt)
    return pl.pallas_call(
        body,
        name="outproj_bwd",
        grid=(S // tm,),
        in_specs=[_rows(tm, D_MODEL), _full(w_out_g.shape), _rows(tm, 512)],
        out_specs=[_rows(tm, POOL_W), _rows(tm, GROUP_W), cls(4), cls(16), _rows(tm, GROUP_W), cls(4), cls(16),
                   _full(w_out_g.shape)],
        out_shape=[jax.ShapeDtypeStruct((S, POOL_W), F32), jax.ShapeDtypeStruct((S, GROUP_W), BF16), cls_shape(4, BF16),
                   cls_shape(16, BF16), jax.ShapeDtypeStruct((S, GROUP_W), F32), cls_shape(4, F32), cls_shape(16, F32),
                   jax.ShapeDtypeStruct(w_out_g.shape, F32)],
        scratch_shapes=[pltpu.VMEM((2, tm, 128), F32)],
        compiler_params=_params(1),
    )(dy1, w_out_g, cat)


def attn_bwd(qkv, dattn, lse, delta, d):
    L = qkv.shape[2]
    nb = L // ATT_BLOCK
    cpb = _classes_per_step(d, nb)

    def body(q_ref, k_ref, v_ref, do_ref, l_ref, dl_ref, out_ref, kpad, vpad, dkpad, dvpad):
        for cls in range(cpb):
            kpad[cls, 0:ATT_BLOCK, :] = jnp.zeros((ATT_BLOCK, GROUP_W), BF16)
            vpad[cls, 0:ATT_BLOCK, :] = jnp.zeros((ATT_BLOCK, GROUP_W), BF16)
            kpad[cls, ATT_BLOCK:, :] = k_ref[cls]
            vpad[cls, ATT_BLOCK:, :] = v_ref[cls]
        dkpad[...] = jnp.zeros_like(dkpad)
        dvpad[...] = jnp.zeros_like(dvpad)
        band, col, lo = _attn_masks()

        def step(t, carry):
            cls, n = t // nb, t % nb
            r0 = pl.multiple_of(n * ATT_BLOCK, ATT_BLOCK)
            valid = band & ((col >= ATT_BLOCK) | (n > 0))
            qb = q_ref[cls, pl.ds(r0, ATT_BLOCK), :]
            dob = do_ref[cls, pl.ds(r0, ATT_BLOCK), :]
            lb = l_ref[cls, pl.ds(r0, ATT_BLOCK), :]
            dlb = dl_ref[cls, pl.ds(r0, ATT_BLOCK), :]
            kb = kpad[cls, pl.ds(r0, 2 * ATT_BLOCK), :]
            vb = vpad[cls, pl.ds(r0, 2 * ATT_BLOCK), :]
            for pair in range(2):
                lanes = slice(pair * 128, (pair + 1) * 128)
                qp, dop, kp, vp = qb[:, lanes], dob[:, lanes], kb[:, lanes], vb[:, lanes]
                c0, c1 = pair * 128, pair * 128 + HEAD_DIM
                q2, do2 = _stack_heads(qp, lo), _stack_heads(dop, lo)
                lse2 = jnp.concatenate([lb[:, c0:c0 + 1], lb[:, c1:c1 + 1]], axis=0)
                dl2 = jnp.concatenate([dlb[:, c0:c0 + 1], dlb[:, c1:c1 + 1]], axis=0)
                s = lax.dot_general(q2, kp, NT, preferred_element_type=F32)
                s = jnp.where(valid, s, NEG)
                p = jnp.exp(s - lse2)
                dp = lax.dot_general(do2, vp, NT, preferred_element_type=F32)
                ds = (p * (dp - dl2)).astype(BF16)
                dq2 = jnp.dot(ds, kp, preferred_element_type=F32)
                out_ref[0, cls, pl.ds(r0, ATT_BLOCK), lanes] = _unstack_heads(dq2, lo)
                dkpad[cls, pl.ds(r0, 2 * ATT_BLOCK), lanes] += lax.dot_general(ds, q2, TN, preferred_element_type=F32)
                dvpad[cls, pl.ds(r0, 2 * ATT_BLOCK), lanes] += lax.dot_general(p.astype(BF16), do2, TN, preferred_element_type=F32)
            return carry

        lax.fori_loop(0, cpb * nb, step, 0, unroll=4)
        for cls in range(cpb):
            out_ref[1, cls] = dkpad[cls, ATT_BLOCK:, :]
            out_ref[2, cls] = dvpad[cls, ATT_BLOCK:, :]

    spec = lambda kind: pl.BlockSpec((None, cpb, L, GROUP_W), lambda r: (kind, r, 0, 0))
    per_cls = pl.BlockSpec((cpb, L, GROUP_W), lambda r: (r, 0, 0))
    return pl.pallas_call(
        body,
        name=f"attn_bwd_d{d}",
        grid=(d // cpb,),
        in_specs=[spec(0), spec(1), spec(2), per_cls, per_cls, per_cls],
        out_specs=pl.BlockSpec((3, cpb, L, GROUP_W), lambda r: (0, r, 0, 0)),
        out_shape=jax.ShapeDtypeStruct((3, d, L, GROUP_W), F32),
        scratch_shapes=[pltpu.VMEM((cpb, L + ATT_BLOCK, GROUP_W), BF16)] * 2 + [pltpu.VMEM((cpb, L + ATT_BLOCK, GROUP_W), F32)] * 2,
        compiler_params=_params(1),
    )(qkv, qkv, qkv, dattn, lse, delta)


def pool_bwd(dpool, mixed, wbd, b, scale):
    S = dpool.shape[0]

    def body(dp_ref, mx_ref, w_ref, b_ref, s_ref, du_ref, dw_ref, db_ref, ds_ref):
        dp = dp_ref[...]
        mb = mx_ref[...]
        wv = w_ref[...]
        ypre = jnp.dot(mb, wv, preferred_element_type=F32) + b_ref[...]
        ds_ref[...] = _colsum(dp * ypre)
        dpre = dp * s_ref[...]
        db_ref[...] = _colsum(dpre)
        dpb = dpre.astype(BF16)
        dw_ref[...] = lax.dot_general(mb, dpb, TN, preferred_element_type=F32)
        dmix = lax.dot_general(dpb, wv, NT, preferred_element_type=F32)
        row = lax.broadcasted_iota(jnp.int32, dmix.shape, 0)
        lane, win = _pool_lane_windows(dmix.shape)
        e = dmix / jnp.minimum(row + 1, win).astype(F32)

        def shift(a, k):
            return jnp.where(row < S - k, pltpu.roll(a, S - k, 0), 0.0)

        f2 = e + shift(e, 1)
        f4 = f2 + shift(f2, 2)
        f8 = f4 + shift(f4, 4)
        f16 = f8 + shift(f8, 8)
        du_ref[...] = jnp.where(lane < 64, f2, jnp.where(lane < 128, f4, jnp.where(lane < 192, f8, f16))) - dmix

    vm = pl.BlockSpec(memory_space=pltpu.VMEM)
    return pl.pallas_call(
        body,
        name="pool_bwd",
        in_specs=[vm] * 5,
        out_specs=[vm] * 4,
        out_shape=[jax.ShapeDtypeStruct((S, POOL_W), F32), jax.ShapeDtypeStruct((POOL_W, POOL_W), F32),
                   jax.ShapeDtypeStruct((1, POOL_W), F32), jax.ShapeDtypeStruct((1, POOL_W), F32)],
        compiler_params=_params(),
    )(dpool, mixed, wbd, b, scale)


def inproj_bwd(dqkv, du, x, dx1, w_in_g, g, mod6, tc, tsa, tsb, h1, tm=512):
    S = x.shape[0]

    def body(d0, d1, d2, du_ref, x_ref, dx1_ref, w_ref, g_ref, mod_ref, tc_ref, tsa_ref, tsb_ref, h_ref,
             gx_ref, dsh_ref, dsc_ref, dg_ref, dw_ref, s4, s16, dp_ref):
        @pl.when(pl.program_id(0) == 0)
        def _():
            dw_ref[...] = jnp.zeros_like(dw_ref)

        cs, sa, sb = tc_ref[...], tsa_ref[...], tsb_ref[...]
        for d, src, dst in ((4, d1, s4), (16, d2, s16)):
            for kind in range(3):
                for r in range(d):
                    for h in range(2):
                        dst[kind, h, pl.ds(r, tm // d, stride=d), :] = src[kind, r, :, h * 128:(h + 1) * 128]
        for sp in range(20):
            piece, half = sp // 2, sp % 2
            lanes = slice(half * 128, (half + 1) * 128)
            if piece == 0:
                blk = du_ref[:, lanes]
            else:
                kind, gi = (piece - 1) // 3, (piece - 1) % 3
                blk = d0[kind, 0, :, lanes] if gi == 0 else (s4, s16)[gi - 1][kind, half]
                if kind == 0:
                    blk = _rope128(blk, cs, sa, sb, -1.0) * (HEAD_DIM ** -0.5)
                elif kind == 1:
                    blk = _rope128(blk, cs, sa, sb, -1.0)
            dp_ref[:, sp * 128:(sp + 1) * 128] = blk.astype(BF16)
        dh = jnp.zeros((tm, D_MODEL), F32)
        hbt = h_ref[...].T
        for j in range(N_CHIPS):
            dpj = dp_ref[:, j * 640:(j + 1) * 640]
            dh = dh + lax.dot_general(dpj, w_ref[j], NT, preferred_element_type=F32)
            dw_ref[j] += jnp.dot(hbt, dpj, preferred_element_type=F32)
        xv = x_ref[...]
        rstd = lax.rsqrt(jnp.mean(xv * xv, axis=-1, keepdims=True) + NORM_EPS)
        n1 = xv * rstd
        gv = g_ref[...]
        one_sc = 1.0 + mod_ref[1:2, :]
        _acc(dsh_ref, _colsum(dh))
        _acc(dsc_ref, _colsum(dh * (n1 * gv)))
        _acc(dg_ref, _colsum(dh * one_sc * n1))
        dn = dh * (gv * one_sc)
        gx_ref[...] = dx1_ref[...] + rstd * (dn - n1 * jnp.mean(dn * n1, axis=-1, keepdims=True))

    vec = _full((1, D_MODEL))
    dspec = lambda d: pl.BlockSpec((3, d, tm // d, GROUP_W), lambda i: (0, 0, i, 0))
    return pl.pallas_call(
        body,
        name="inproj_bwd",
        grid=(S // tm,),
        in_specs=[dspec(d) for d in DILATIONS] + [_rows(tm, POOL_W), _rows(tm, D_MODEL), _rows(tm, D_MODEL), _full(w_in_g.shape),
                                                  vec, _full((6, D_MODEL)), _rows(tm, 128), _rows(tm, 128), _rows(tm, 128),
                                                  _rows(tm, D_MODEL)],
        out_specs=[_rows(tm, D_MODEL), vec, vec, vec, _full(w_in_g.shape)],
        out_shape=[jax.ShapeDtypeStruct((S, D_MODEL), F32)] + [jax.ShapeDtypeStruct((1, D_MODEL), F32)] * 3
        + [jax.ShapeDtypeStruct(w_in_g.shape, F32)],
        scratch_shapes=[pltpu.VMEM((3, 2, tm, 128), F32)] * 2 + [pltpu.VMEM((tm, IN_W), BF16)],
        compiler_params=_params(1),
    )(*dqkv, du, x, dx1, w_in_g, g, mod6, tc, tsa, tsb, h1)


def _adamw(w, g, m, v):
    m = ADAM_B1 * m + (1.0 - ADAM_B1) * g
    v = ADAM_B2 * v + (1.0 - ADAM_B2) * (g * g)
    m_hat = m / (1.0 - ADAM_B1 ** ADAM_STEP)
    v_hat = v / (1.0 - ADAM_B2 ** ADAM_STEP)
    delta = -ADAM_LR * (m_hat / (jnp.sqrt(v_hat) + ADAM_EPS) + ADAM_WD * w)
    return delta, m, v


def adamw_rows(w, g, m, v, tr, name):
    R, C = w.shape

    def body(w_ref, g_ref, m_ref, v_ref, go_ref, d_ref, mo_ref, vo_ref):
        g = g_ref[...]
        go_ref[...] = g
        d_ref[...], mo_ref[...], vo_ref[...] = _adamw(w_ref[...], g, m_ref[...], v_ref[...])

    spec = pl.BlockSpec((tr, C), lambda i: (i, 0))
    return pl.pallas_call(
        body,
        name=name,
        grid=(R // tr,),
        in_specs=[spec] * 4,
        out_specs=[spec] * 4,
        out_shape=[jax.ShapeDtypeStruct((R, C), F32)] * 4,
        compiler_params=_params(1),
    )(w, g, m, v)


def adamw_ada(c_all_t, dmod_cols, w, m, v, tr=256):
    R, C = w.shape

    def body(ct_ref, dm_ref, w_ref, m_ref, v_ref, g_ref, d_ref, mo_ref, vo_ref):
        ct = ct_ref[...]
        act = ct * jax.nn.sigmoid(ct)
        dm = dm_ref[...]
        a_hi, d_hi = act.astype(BF16), dm.astype(BF16)
        a_lo, d_lo = (act - a_hi.astype(F32)).astype(BF16), (dm - d_hi.astype(F32)).astype(BF16)
        g = (jnp.dot(a_hi, d_hi, preferred_element_type=F32) + jnp.dot(a_lo, d_hi, preferred_element_type=F32)
             + jnp.dot(a_hi, d_lo, preferred_element_type=F32))
        g_ref[...] = g
        d_ref[...], mo_ref[...], vo_ref[...] = _adamw(w_ref[...], g, m_ref[...], v_ref[...])

    spec = pl.BlockSpec((tr, C), lambda i: (i, 0))
    return pl.pallas_call(
        body,
        name="adamw_ada",
        grid=(R // tr,),
        in_specs=[pl.BlockSpec((tr, N_DEV), lambda i: (i, 0)), _full((N_DEV, C)), spec, spec, spec],
        out_specs=[spec] * 4,
        out_shape=[jax.ShapeDtypeStruct((R, C), F32)] * 4,
        compiler_params=_params(1),
    )(c_all_t, dmod_cols, w, m, v)


def adamw_small(slab_a, slab_b, convw_g, wpool_g, params):
    names = ["b_ada", "g_pre_mix", "g_post_mix", "g_pre_ffn", "g_post_ffn", "b_pool", "pool_scale", "conv_b", "conv_w", "w_pool"]
    flat = []
    for n in names:
        flat += list(params[n])

    def body(a_ref, b_ref, cw_ref, wp_ref, *rest):
        ins, outs = rest[:30], rest[30:]

        def dev_sum(ref):
            t = ref[0]
            for dev in range(1, N_DEV):
                t = t + ref[dev]
            return t

        sa, sb_, scw, swp = dev_sum(a_ref), dev_sum(b_ref), dev_sum(cw_ref), dev_sum(wp_ref)
        grads = [
            jnp.concatenate([sa[k:k + 1, :] for k in range(6)], axis=1),
            sa[6:7, :], sa[7:8, :], sa[8:9, :], sa[9:10, :],
            sa[10:11, 0:256], sa[10:11, 256:512],
            sb_[3:4, :], scw, swp,
        ]
        for i, g in enumerate(grads):
            w_ref, m_ref, v_ref = ins[3 * i:3 * i + 3]
            if names[i] == "b_pool":
                parts = [((0, slice(grp, grp + 1)), g[:, grp * 64:(grp + 1) * 64]) for grp in range(4)]
            elif names[i] == "w_pool":
                parts = [((0, grp), g[grp * 64:(grp + 1) * 64, :]) for grp in range(4)]
            elif names[i] == "conv_w":
                parts = [((0,), g)]
            else:
                parts = [((Ellipsis,), g)]
            for at, gp in parts:
                d, mo, vo = _adamw(w_ref[at], gp, m_ref[at], v_ref[at])
                for k, val in enumerate((gp, d, mo, vo)):
                    outs[4 * i + k][at] = val
        outs[-1][...] = sa[10:11, 512:640]

    vm = pl.BlockSpec(memory_space=pltpu.VMEM)
    out_shape = []
    for n in names:
        out_shape += [jax.ShapeDtypeStruct(params[n][0].shape, F32)] * 4
    out_shape.append(jax.ShapeDtypeStruct((1, 128), F32))
    outs = pl.pallas_call(
        body,
        name="adamw_small",
        in_specs=[vm] * (4 + len(flat)),
        out_specs=[vm] * len(out_shape),
        out_shape=out_shape,
        compiler_params=_params(),
    )(slab_a, slab_b, convw_g, wpool_g, *flat)
    return {n: outs[4 * i:4 * i + 4] for i, n in enumerate(names)}, outs[-1]


def _place():
    return lax.axis_index("x"), lax.axis_index("y"), lax.axis_index("c")


def _other_chips(x, y):
    return [(1 - x, y), (x, 1 - y), (1 - x, 1 - y)]


def _chip_id(cx, cy):
    return 2 * cx + cy


HBM_SPEC = pl.BlockSpec(memory_space=pltpu.HBM)
SEM_SPEC = pl.BlockSpec(memory_space=pltpu.SEMAPHORE)
ANY_SPEC = pl.BlockSpec(memory_space=pl.ANY)
EFFECT = pltpu.SideEffectType.DATAFLOW_SIDE_EFFECTING


def _hbm(t):
    return pltpu.with_memory_space_constraint(t, pltpu.HBM)


def _hbm_shapes(ts):
    return [pltpu.HBM(t.shape, t.dtype) for t in ts]


def _half_rows(ref, lead, half, rh):
    return ref.at[lead, pl.ds(half * rh, rh), :]


def _flips():
    return [(fx, fy, fc) for fx in (0, 1) for fy in (0, 1) for fc in (0, 1)][1:]


def _flip(v, f):
    return v if f == 0 else 1 - v


def ada_mod(c3, w_ada, b_cols, conv_w):
    CB = w_ada.shape[1]

    def body(c_ref, w_ref, b_ref, cw_ref, call_ref, mod_ref, cwall_ref, modall, send_sems, recv_sems):
        x, y, c = _place()
        me_dev = 4 * x + 2 * y + c
        me = _chip_id(x, y)
        call_ref[me_dev] = c_ref[0]
        cwall_ref[me] = cw_ref[...]
        sends = []
        for k, (cx, cy) in enumerate(_other_chips(x, y)):
            cp = pltpu.make_async_remote_copy(src_ref=cw_ref, dst_ref=cwall_ref.at[me], send_sem=send_sems.at[10 + k],
                                              recv_sem=recv_sems.at[10 + k], device_id=(cx, cy, c), device_id_type=MESH)
            cp.start()
            sends.append(cp)
        for k, (fx, fy, fc) in enumerate(_flips()):
            cp = pltpu.make_async_remote_copy(src_ref=c_ref.at[0], dst_ref=call_ref.at[me_dev], send_sem=send_sems.at[k],
                                              recv_sem=recv_sems.at[k],
                                              device_id=(_flip(x, fx), _flip(y, fy), _flip(c, fc)), device_id_type=MESH)
            cp.start()
            sends.append(cp)
        for k, (fx, fy, fc) in enumerate(_flips()):
            peer = 4 * _flip(x, fx) + 2 * _flip(y, fy) + _flip(c, fc)
            pltpu.make_async_remote_copy(src_ref=c_ref.at[0], dst_ref=call_ref.at[peer], send_sem=send_sems.at[k],
                                         recv_sem=recv_sems.at[k], device_id=(x, y, c), device_id_type=MESH).wait_recv()
        row = lax.broadcasted_iota(jnp.int32, (N_DEV, D_MODEL), 0)
        call = jnp.zeros((N_DEV, D_MODEL), F32)
        for dev in range(N_DEV):
            call = jnp.where(row == dev, call_ref[dev], call)
        act = call * jax.nn.sigmoid(call)
        wv = w_ref[...]
        w_hi = wv.astype(BF16)
        w_lo = (wv - w_hi.astype(F32)).astype(BF16)
        a_hi = act.astype(BF16)
        a_lo = (act - a_hi.astype(F32)).astype(BF16)
        prod = (jnp.dot(a_hi, w_hi, preferred_element_type=F32) + jnp.dot(a_lo, w_hi, preferred_element_type=F32)
                + jnp.dot(a_hi, w_lo, preferred_element_type=F32))
        modall[me] = prod + b_ref[...]
        for k, (cx, cy) in enumerate(_other_chips(x, y)):
            cp = pltpu.make_async_remote_copy(src_ref=modall.at[me], dst_ref=modall.at[me], send_sem=send_sems.at[7 + k],
                                              recv_sem=recv_sems.at[7 + k], device_id=(cx, cy, c), device_id_type=MESH)
            cp.start()
            sends.append(cp)
        for k, (cx, cy) in enumerate(_other_chips(x, y)):
            blk = modall.at[_chip_id(cx, cy)]
            pltpu.make_async_remote_copy(src_ref=blk, dst_ref=blk, send_sem=send_sems.at[7 + k], recv_sem=recv_sems.at[7 + k],
                                         device_id=(x, y, c), device_id_type=MESH).wait_recv()
        for k, (cx, cy) in enumerate(_other_chips(x, y)):
            blk = cwall_ref.at[_chip_id(cx, cy)]
            pltpu.make_async_remote_copy(src_ref=blk, dst_ref=blk, send_sem=send_sems.at[10 + k], recv_sem=recv_sems.at[10 + k],
                                         device_id=(x, y, c), device_id_type=MESH).wait_recv()
        for cp in sends:
            cp.wait_send()
        mine = [modall[j, pl.ds(me_dev, 1), :] for j in range(N_CHIPS)]
        for r in range(6):
            pieces = []
            for h in range(2):
                pos = r * D_MODEL + h * 512
                pieces.append(mine[pos // CB][:, pos % CB:pos % CB + 512])
            mod_ref[r:r + 1, :] = jnp.concatenate(pieces, axis=1)

    vm = pl.BlockSpec(memory_space=pltpu.VMEM)
    return pl.pallas_call(
        body,
        name="ada_mod",
        in_specs=[vm] * 4,
        out_specs=[vm] * 3,
        out_shape=[jax.ShapeDtypeStruct((N_DEV, 1, D_MODEL), F32), jax.ShapeDtypeStruct((6, D_MODEL), F32),
                   jax.ShapeDtypeStruct((N_CHIPS,) + conv_w.shape, F32)],
        scratch_shapes=[pltpu.VMEM((N_CHIPS, N_DEV, CB), F32), pltpu.SemaphoreType.DMA((13,)), pltpu.SemaphoreType.DMA((13,))],
        compiler_params=pltpu.CompilerParams(has_side_effects=True, vmem_limit_bytes=VMEM_LIMIT),
    )(c3, w_ada, b_cols, conv_w)


def split_start(name, bufs, plan, n_sem, carry):
    nb = len(bufs)
    many = isinstance(carry, (list, tuple))
    alls = list(bufs) + (list(carry) if many else [carry])
    na = len(alls)

    def body(*refs):
        x, y, c = _place()
        ssem, rsem = refs[na], refs[na + 1]
        for i, (src, dst, dev) in enumerate(plan(refs[:nb], x, y, c)):
            pltpu.make_async_remote_copy(src_ref=src, dst_ref=dst, send_sem=ssem.at[i], recv_sem=rsem.at[i], device_id=dev,
                                         device_id_type=MESH).start()

    outs = pl.pallas_call(
        body,
        name=name,
        out_shape=[pltpu.SemaphoreType.DMA((n_sem,)), pltpu.SemaphoreType.DMA((n_sem,))] + _hbm_shapes(alls),
        in_specs=[HBM_SPEC] * na,
        out_specs=[SEM_SPEC, SEM_SPEC] + [HBM_SPEC] * na,
        input_output_aliases={i: 2 + i for i in range(na)},
        compiler_params=pltpu.CompilerParams(has_side_effects=EFFECT),
    )(*[_hbm(t) for t in alls])
    return outs[0], outs[1], list(outs[2:2 + nb]), (list(outs[2 + nb:]) if many else outs[-1])


def split_wait(name, ssem, rsem, bufs, plan, after):
    nb = len(bufs)

    def body(*refs):
        x, y, c = _place()
        s_ref, r_ref = refs[nb], refs[nb + 1]
        for i, (src, dst, dev) in enumerate(plan(refs[:nb], x, y, c)):
            cp = pltpu.make_async_remote_copy(src_ref=src, dst_ref=dst, send_sem=s_ref.at[i], recv_sem=r_ref.at[i], device_id=dev,
                                              device_id_type=MESH)
            cp.wait_send()
            cp.wait_recv()

    outs = pl.pallas_call(
        body,
        name=name,
        out_shape=_hbm_shapes(bufs),
        in_specs=[HBM_SPEC] * nb + [SEM_SPEC, SEM_SPEC, ANY_SPEC],
        out_specs=[HBM_SPEC] * nb,
        input_output_aliases={i: i for i in range(nb)},
        compiler_params=pltpu.CompilerParams(has_side_effects=EFFECT),
    )(*bufs, ssem, rsem, after)
    return list(outs)


def _gather_ici_plan(n):
    def plan(refs, x, y, c):
        out = []
        for w in range(n):
            rh = refs[w].shape[0] // 2
            for cx, cy in _other_chips(x, y):
                out.append((refs[w].at[pl.ds(c * rh, rh), :], _half_rows(refs[n + w], _chip_id(x, y), c, rh), (cx, cy, c)))
        return out

    return plan


def _gather_d2d_plan(n):
    def plan(refs, x, y, c):
        out = []
        for w in range(n):
            rh = refs[w].shape[1] // 2
            for cx, cy in _other_chips(x, y):
                blk = _half_rows(refs[w], _chip_id(cx, cy), c, rh)
                out.append((blk, blk, (x, y, 1 - c)))
        return out

    return plan


def _dev_id(x, y, c):
    return 4 * x + 2 * y + c


def _small_ici_plan(n):
    def plan(refs, x, y, c):
        out = []
        for w in range(n):
            dst = refs[n + w].at[_dev_id(x, y, c)]
            out.append((refs[w], dst, (x, y, 1 - c)))
            for cx, cy in _other_chips(x, y):
                out.append((refs[w], dst, (cx, cy, c)))
        return out

    return plan


def _small_d2d_plan(n):
    def plan(refs, x, y, c):
        out = []
        for w in range(n):
            for cx, cy in _other_chips(x, y):
                blk = refs[w].at[_dev_id(cx, cy, c)]
                out.append((blk, blk, (x, y, 1 - c)))
        return out

    return plan


def _rs_d2d_plan(n):
    def plan(refs, x, y, c):
        out = []
        for w in range(n):
            rh = refs[w].shape[1] // 2
            out.append((refs[w].at[:, pl.ds((1 - c) * rh, rh), :], refs[n + w], (x, y, 1 - c)))
        return out

    return plan


def _rs_ici_plan(n):
    def plan(refs, x, y, c):
        out = []
        for w in range(n):
            for k, (cx, cy) in enumerate(_other_chips(x, y)):
                out.append((refs[w].at[_chip_id(cx, cy)], refs[n + w].at[k], (cx, cy, c)))
        return out

    return plan


def _rs_share_plan(n):
    def plan(refs, x, y, c):
        out = []
        for w in range(n):
            rh = refs[w].shape[0] // 2
            rows = refs[w].at[pl.ds(c * rh, rh), :]
            out.append((rows, rows, (x, y, 1 - c)))
        return out

    return plan


def rs_add(grad, sibbuf, place, tr, name):
    _, R, C = grad.shape
    nt = (R // 2) // tr

    def body(p_ref, g_ref, s_ref, o_ref):
        o_ref[...] = (g_ref[...] + s_ref[...]).astype(BF16)

    return pl.pallas_call(
        body,
        name=name,
        grid_spec=pltpu.PrefetchScalarGridSpec(
            num_scalar_prefetch=1,
            grid=(N_CHIPS, nt),
            in_specs=[pl.BlockSpec((None, tr, C), lambda j, i, p: (j, p[0] * nt + i, 0)),
                      pl.BlockSpec((None, tr, C), lambda j, i, p: (j, i, 0))],
            out_specs=pl.BlockSpec((None, tr, C), lambda j, i, p: (j, i, 0)),
        ),
        out_shape=jax.ShapeDtypeStruct((N_CHIPS, R // 2, C), BF16),
        compiler_params=_params(2),
    )(place, grad, sibbuf)


def rs_final(grad, sibbuf, rbuf, place, tr, name):
    _, R, C = grad.shape
    nt = (R // 2) // tr

    def body(p_ref, g_ref, s_ref, r_ref, o_ref):
        o_ref[...] = (((g_ref[...] + s_ref[...]) + r_ref[0].astype(F32)) + r_ref[1].astype(F32)) + r_ref[2].astype(F32)

    return pl.pallas_call(
        body,
        name=name,
        grid_spec=pltpu.PrefetchScalarGridSpec(
            num_scalar_prefetch=1,
            grid=(nt,),
            in_specs=[pl.BlockSpec((None, tr, C), lambda i, p: (p[1], p[0] * nt + i, 0)),
                      pl.BlockSpec((None, tr, C), lambda i, p: (p[1], i, 0)),
                      pl.BlockSpec((3, tr, C), lambda i, p: (0, i, 0))],
            out_specs=pl.BlockSpec((tr, C), lambda i, p: (p[0] * nt + i, 0)),
        ),
        out_shape=jax.ShapeDtypeStruct((R, C), F32),
        compiler_params=_params(1),
    )(place, grad, sibbuf, rbuf)


class GradReduce:
    def __init__(self, tag, grads, rows, place):
        self.tag, self.grads, self.rows, self.place = tag, grads, rows, place
        self.n = len(grads)

    def d2d_start(self, carry):
        sib = [lax.empty((N_CHIPS, g.shape[1] // 2, g.shape[2]), F32) for g in self.grads]
        self.s1, self.r1, bufs, carry = split_start(f"rs_{self.tag}_d2d_start", self.grads + sib, _rs_d2d_plan(self.n), self.n, carry)
        self.bufs1 = bufs
        return carry

    def add_and_ici_start(self, after, carry):
        bufs = split_wait(f"rs_{self.tag}_d2d_wait", self.s1, self.r1, self.bufs1, _rs_d2d_plan(self.n), after)
        self.grads, self.sib = bufs[:self.n], bufs[self.n:]
        pb = [rs_add(g, s, self.place, tr, f"rs_{self.tag}_add{w}")
              for w, (g, s, tr) in enumerate(zip(self.grads, self.sib, self.rows))]
        rb = [lax.empty((3,) + p.shape[1:], BF16) for p in pb]
        self.s2, self.r2, self.bufs2, carry = split_start(f"rs_{self.tag}_ici_start", pb + rb, _rs_ici_plan(self.n), 3 * self.n, carry)
        return carry

    def final_and_share_start(self, after, carry):
        bufs = split_wait(f"rs_{self.tag}_ici_wait", self.s2, self.r2, self.bufs2, _rs_ici_plan(self.n), after)
        rb = bufs[self.n:]
        full = [rs_final(g, s, r, self.place, tr, f"rs_{self.tag}_final{w}")
                for w, (g, s, r, tr) in enumerate(zip(self.grads, self.sib, rb, self.rows))]
        self.s3, self.r3, self.bufs3, carry = split_start(f"rs_{self.tag}_share_start", full, _rs_share_plan(self.n), self.n, carry)
        return carry

    def finish(self, after):
        return split_wait(f"rs_{self.tag}_share_wait", self.s3, self.r3, self.bufs3, _rs_share_plan(self.n), after)


def _rope_tables(positions):
    inv_freq = ROPE_THETA ** (-jnp.arange(0, ROT_DIM, 2, dtype=F32) / ROT_DIM)
    ang = positions.astype(F32)[:, None] * inv_freq
    cos, sin = jnp.cos(ang), jnp.sin(ang)
    S = positions.shape[0]
    one, zero = jnp.ones((S, 48), F32), jnp.zeros((S, 48), F32)
    z8 = jnp.zeros((S, 8), F32)
    tc = jnp.concatenate([cos, cos, one], axis=1)
    tsa = jnp.concatenate([z8, sin, zero], axis=1)
    tsb = jnp.concatenate([-sin, z8, zero], axis=1)
    return tuple(jnp.tile(t, (1, 2)) for t in (tc, tsa, tsb))


def _block_diag(w_pool):
    wbd = jnp.zeros((POOL_W, POOL_W), F32)
    for gi in range(4):
        wbd = wbd.at[gi * 64:(gi + 1) * 64, gi * 64:(gi + 1) * 64].set(w_pool[gi])
    return wbd


def kernel(x, c, positions, w_ada, b_ada, g_pre_mix, g_post_mix, g_pre_ffn, g_post_ffn, w_in, w_pool, b_pool, pool_scale, w_out, w_up, conv_w, conv_b, w_down, loss_target, m_w_ada, m_b_ada, m_g_pre_mix, m_g_post_mix, m_g_pre_ffn, m_g_post_ffn, m_w_in, m_w_pool, m_b_pool, m_pool_scale, m_w_out, m_w_up, m_conv_w, m_conv_b, m_w_down, v_w_ada, v_b_ada, v_g_pre_mix, v_g_post_mix, v_g_pre_ffn, v_g_post_ffn, v_w_in, v_w_pool, v_b_pool, v_pool_scale, v_w_out, v_w_up, v_conv_w, v_conv_b, v_w_down):
    xi, yi, ci = lax.axis_index("x"), lax.axis_index("y"), lax.axis_index("c")
    chip = 2 * xi + yi
    place = jnp.stack([ci, chip]).astype(jnp.int32)
    x2, tgt = x[0], loss_target[0]
    S = x2.shape[0]

    def landing(s_):
        return lax.dynamic_update_slice(lax.empty((N_CHIPS,) + s_.shape, s_.dtype), s_[None], (chip, 0, 0))

    cb_ada = w_ada.shape[2]
    b_cols = lax.dynamic_slice(b_ada, (0, chip * cb_ada), (1, cb_ada))
    c_all, mod6, conv_w_g = ada_mod(c.reshape(1, 1, D_MODEL), w_ada[0], b_cols, conv_w[0])
    conv_w_f = jnp.transpose(conv_w_g, (1, 0, 2)).reshape(3, D_FF)
    mix_sh = [w_in[0].astype(BF16), w_out[0].astype(BF16)]
    ffn_sh = [w_up[0].astype(BF16), w_down[0].astype(BF16)]
    ga_s, ga_r, ga_bufs, mod6 = split_start("gather_mix_ici_start", mix_sh + [landing(t) for t in mix_sh], _gather_ici_plan(2), 6, mod6)
    gb_s, gb_r, gb_bufs, (mod6, tc, tsa, tsb) = split_start("gather_ffn_ici_start", ffn_sh + [landing(t) for t in ffn_sh],
                                                            _gather_ici_plan(2), 6, [mod6, *_rope_tables(positions[0])])
    wbd = _block_diag(w_pool[0]).astype(BF16)
    b_pool2, scale2 = b_pool.reshape(1, POOL_W), pool_scale
    ga_bufs = split_wait("gather_mix_ici_wait", ga_s, ga_r, ga_bufs, _gather_ici_plan(2), mod6)
    gc_s, gc_r, mix_land, mod6 = split_start("gather_mix_d2d_start", ga_bufs[2:], _gather_d2d_plan(2), 6, mod6)
    w_in_g, w_out_g = split_wait("gather_mix_d2d_wait", gc_s, gc_r, mix_land, _gather_d2d_plan(2), mod6)

    h1, u, *qkv = inproj_fwd(x2, g_pre_mix, mod6, w_in_g, tc, tsa, tsb)
    mixed, pool = pool_fwd(u, wbd, b_pool2, scale2)
    o_l = [attn_fwd(t, d) for t, d in zip(qkv, DILATIONS)]
    attn_done = sum(l[0, :8, :128] for _, l in o_l)
    gb_bufs = split_wait("gather_ffn_ici_wait", gb_s, gb_r, gb_bufs, _gather_ici_plan(2), attn_done)
    gd_s, gd_r, ffn_land, pool = split_start("gather_ffn_d2d_start", gb_bufs[2:], _gather_d2d_plan(2), 6, pool)
    cat, lse, lse4, lse16, y1, x1, h2 = outproj_fwd([o for o, _ in o_l] + [l for _, l in o_l], pool, x2, w_out_g, g_post_mix,
                                                    g_pre_ffn, mod6)
    lses = [lse[None], lse4, lse16]
    w_up_g, w_down_g = split_wait("gather_ffn_d2d_wait", gd_s, gd_r, ffn_land, _gather_d2d_plan(2), h2)
    w_down_f = w_down_g.reshape(D_FF, D_MODEL)
    gate, val, a, dy2, dout, loss_v, d_gt_f, d_g_post_ffn = ffn_fwd(h2, w_up_g, conv_w_f, conv_b, w_down_f, x1, tgt, g_post_ffn, mod6)

    dgc, dval, d_conv_w, d_conv_b, dw_down, dw_up = down_bwd(dy2, w_down_f, gate, val, conv_w_f, conv_b, a, h2)
    dx1, dy1, d_sh_f, d_sc_f, d_g_pre_ffn, d_gt_m, d_g_post_mix, dw_up = up_bwd(
        dgc, dval, conv_w_f, w_up_g, x1, dout, y1, g_pre_ffn, g_post_mix, mod6, h2, dw_up)
    rs_ffn = GradReduce("ffn", [dw_up, dw_down.reshape(N_CHIPS, D_FF // N_CHIPS, D_MODEL)], [256, 176], place)
    dy1 = rs_ffn.d2d_start(dy1)
    dpool, da1, da4, da16, dl1, dl4, dl16, dw_out = outproj_bwd(dy1, w_out_g, cat)
    dpool = rs_ffn.add_and_ici_start(dw_out, dpool)
    du, d_wbd, d_b_pool, d_scale = pool_bwd(dpool, mixed, wbd, b_pool2, scale2)
    dqkv = [attn_bwd(t, da, ls, dl, d) for t, da, ls, dl, d in zip(qkv, (da1[None], da4, da16), lses, (dl1[None], dl4, dl16), DILATIONS)]
    grad_x, d_sh_m, d_sc_m, d_g_pre_mix, dw_in = inproj_bwd(dqkv, du, x2, dx1, w_in_g, g_pre_mix, mod6, tc, tsa, tsb, h1)

    z1 = jnp.zeros((1, D_MODEL), F32)
    slab_a = jnp.concatenate(
        [d_sh_m, d_sc_m, d_gt_m, d_sh_f, d_sc_f, d_gt_f, d_g_pre_mix, d_g_post_mix, d_g_pre_ffn, d_g_post_ffn,
         jnp.concatenate([d_b_pool, d_scale, loss_v, jnp.zeros((1, 384), F32)], axis=1)] + [z1] * 5, axis=0)
    slab_b = jnp.concatenate([d_conv_w, d_conv_b, jnp.zeros((4, D_FF), F32)], axis=0)
    d_wpool = jnp.concatenate([d_wbd[gi * 64:(gi + 1) * 64, gi * 64:(gi + 1) * 64] for gi in range(4)], axis=0)
    dev = _dev_id(xi, yi, ci)
    small_src = [slab_a, slab_b, d_wpool]
    small_land = [lax.dynamic_update_slice(lax.empty((N_DEV,) + t.shape, F32), t[None], (dev, 0, 0)) for t in small_src]
    tok = jnp.zeros((8, 128), F32)
    gs_s, gs_r, gs_bufs, tok = split_start("small_ici_start", small_src + small_land, _small_ici_plan(3), 12, tok)
    rs_mix = GradReduce("mix", [dw_in, dw_out], [256, 256], place)
    tok = rs_mix.d2d_start(tok)
    tok = rs_ffn.final_and_share_start(tok, tok)
    gs_bufs = split_wait("small_ici_wait", gs_s, gs_r, gs_bufs, _small_ici_plan(3), tok)
    gt_s, gt_r, small_land, tok = split_start("small_d2d_start", gs_bufs[3:], _small_d2d_plan(3), 9, tok)
    tok = rs_mix.add_and_ici_start(tok, tok)
    slab_a_g, slab_b_g, wpool_g = split_wait("small_d2d_wait", gt_s, gt_r, small_land, _small_d2d_plan(3), tok)
    cw_cols = conv_w.shape[2]
    convw_g = lax.dynamic_slice(slab_b_g, (0, 0, chip * cw_cols), (N_DEV, 3, cw_cols))
    dmod_cols = lax.dynamic_slice(slab_a_g[:, :6, :].reshape(N_DEV, 6 * D_MODEL), (0, chip * cb_ada), (N_DEV, cb_ada))

    res = {}

    def big_adamw(name, w, g, m, v, tr):
        g_, d_, m_, v_ = adamw_rows(w[0], g, m[0], v[0], tr, "adamw_" + name)
        res[name] = (g_[None], d_[None], m_[None], v_[None])
        return v_

    g_ada, d_ada, m_ada, v_ada = adamw_ada(c_all.reshape(N_DEV, D_MODEL).T, dmod_cols, w_ada[0], m_w_ada[0], v_w_ada[0])
    res["w_ada"] = (g_ada[None], d_ada[None], m_ada[None], v_ada[None])
    g_w_up, g_w_down = rs_ffn.finish(v_ada)
    big_adamw("w_up", w_up, g_w_up, m_w_up, v_w_up, 256)
    last = big_adamw("w_down", w_down, g_w_down, m_w_down, v_w_down, 352)
    rs_mix.final_and_share_start(last, jnp.zeros((8, 128), F32))
    g_w_in, g_w_out = rs_mix.finish(last)
    big_adamw("w_in", w_in, g_w_in, m_w_in, v_w_in, 256)
    big_adamw("w_out", w_out, g_w_out, m_w_out, v_w_out, 256)
    small, loss_sum = adamw_small(slab_a_g, slab_b_g, convw_g, wpool_g, {
        "b_ada": (b_ada, m_b_ada, v_b_ada), "g_pre_mix": (g_pre_mix, m_g_pre_mix, v_g_pre_mix),
        "g_post_mix": (g_post_mix, m_g_post_mix, v_g_post_mix), "g_pre_ffn": (g_pre_ffn, m_g_pre_ffn, v_g_pre_ffn),
        "g_post_ffn": (g_post_ffn, m_g_post_ffn, v_g_post_ffn), "b_pool": (b_pool, m_b_pool, v_b_pool),
        "pool_scale": (pool_scale, m_pool_scale, v_pool_scale), "conv_b": (conv_b, m_conv_b, v_conv_b),
        "conv_w": (conv_w, m_conv_w, v_conv_w), "w_pool": (w_pool, m_w_pool, v_w_pool)})
    for name in ("b_ada", "g_pre_mix", "g_post_mix", "g_pre_ffn", "g_post_ffn", "pool_scale", "conv_b", "b_pool", "w_pool", "conv_w"):
        res[name] = tuple(small[name])

    loss = loss_sum[0, 0]
    order = ["w_ada", "b_ada", "g_pre_mix", "g_post_mix", "g_pre_ffn", "g_post_ffn", "w_in", "w_pool", "b_pool", "pool_scale",
             "w_out", "w_up", "conv_w", "conv_b", "w_down"]
    outs = [loss, grad_x[None]]
    for k in range(4):
        outs += [res[n][k] for n in order]
    return tuple(outs)
```

```python
import math

import jax
import jax.numpy as jnp
from jax import lax
from jax.experimental import pallas as pl
from jax.experimental.pallas import tpu as pltpu

F32 = jnp.float32
BF16 = jnp.bfloat16
MESH = pl.DeviceIdType.MESH

D_MODEL = 1024
HEAD_DIM = 64
POOL_W = 256
GROUP_W = 256
DILATIONS = (1, 4, 16)
ATT_BLOCK = 128
IN_W = 2560
D_FF = 2816
HALF_FF = 1408
ROT_DIM = 16
ROPE_THETA = 500000.0
NORM_EPS = 1e-6
N_CHIPS = 4
N_DEV = 8
NEG = -1e30

ADAM_LR = 0.001
ADAM_B1 = 0.9
ADAM_B2 = 0.999
ADAM_EPS = 1e-08
ADAM_WD = 0.01
ADAM_STEP = 10

VMEM_LIMIT = 56 * 1024 * 1024

NT = (((1,), (1,)), ((), ()))
TN = (((0,), (0,)), ((), ()))


def _params(n_grid=0, **kw):
    sem = ("arbitrary",) * n_grid if n_grid else None
    return pltpu.CompilerParams(dimension_semantics=sem, vmem_limit_bytes=VMEM_LIMIT, **kw)


def _full(shape):
    nd = len(shape)
    return pl.BlockSpec(tuple(shape), lambda *_: (0,) * nd, pipeline_mode=pl.Buffered(1))


def _rows(tm, ncol):
    return pl.BlockSpec((tm, ncol), lambda i: (i, 0))


def _acc(ref, val):
    @pl.when(pl.program_id(0) == 0)
    def _():
        ref[...] = jnp.zeros_like(ref)

    ref[...] += val


def _colsum(v):
    return jnp.sum(v, axis=0, keepdims=True)


def _rope128(t, cs, sa, sb, sign):
    return t * cs + sign * (pltpu.roll(t, 8, 1) * sa + pltpu.roll(t, 120, 1) * sb)


FF_CHUNKS = tuple((ch, off, w) for ch in range(2) for off, w in ((0, 512), (512, 512), (1024, 384)))
GELU_C0 = math.sqrt(2.0 / math.pi)
GELU_C1 = GELU_C0 * 0.044715


def _gelu(z):
    z2 = z * z
    t = jnp.tanh(z * (GELU_C0 + GELU_C1 * z2))
    u = 0.5 * t + 0.5
    return z * u, u, t, z2


def _gelu_grad(z, u, t, z2):
    return u + (z * (GELU_C0 + (3.0 * GELU_C1) * z2)) * (0.5 - 0.5 * (t * t))


def _conv_taps(gate, halo, first):
    row = lax.broadcasted_iota(jnp.int32, gate.shape, 0)
    halo = jnp.where(first, 0.0, halo)
    nh = halo.shape[0]
    p1 = halo[nh - 1:nh, :]
    p2 = halo[nh - 2:nh - 1, :]
    g1 = jnp.where(row == 0, p1, pltpu.roll(gate, 1, 0))
    g2 = jnp.where(row == 0, p2, jnp.where(row == 1, p1, pltpu.roll(gate, 2, 0)))
    return g1, g2


def prenorm_fwd(x, g, mod6, tm=512):
    S = x.shape[0]

    def body(x_ref, g_ref, mod_ref, h_ref):
        xv = x_ref[...]
        rstd = lax.rsqrt(jnp.mean(xv * xv, axis=-1, keepdims=True) + NORM_EPS)
        h = ((xv * rstd) * g_ref[...]) * (1.0 + mod_ref[1:2, :]) + mod_ref[0:1, :]
        h_ref[...] = h.astype(BF16)

    return pl.pallas_call(
        body,
        name="prenorm_fwd",
        grid=(S // tm,),
        in_specs=[_rows(tm, D_MODEL), _full((1, D_MODEL)), _full((6, D_MODEL))],
        out_specs=_rows(tm, D_MODEL),
        out_shape=jax.ShapeDtypeStruct((S, D_MODEL), BF16),
        compiler_params=_params(1),
    )(x, g, mod6)


def inproj_fwd(h1, w_in_g, tc, tsa, tsb, tm=512):
    S = h1.shape[0]

    def body(h_ref, w_ref, tc_ref, tsa_ref, tsb_ref, u_ref, q1_ref, q4_ref, q16_ref, scr):
        qkv_refs = (q1_ref, q4_ref, q16_ref)
        hb = h_ref[...]
        cs, sa, sb = tc_ref[...], tsa_ref[...], tsb_ref[...]
        for j in range(N_CHIPS):
            res = jnp.dot(hb, w_ref[j], preferred_element_type=F32)
            for t in range(5):
                sp = 5 * j + t
                piece, half = sp // 2, sp % 2
                blk = res[:, t * 128:(t + 1) * 128]
                lanes = slice(half * 128, (half + 1) * 128)
                if piece == 0:
                    u_ref[:, lanes] = blk
                else:
                    kind, gi = (piece - 1) // 3, (piece - 1) % 3
                    if kind == 0:
                        blk = _rope128(blk, cs, sa, sb, 1.0) * (HEAD_DIM ** -0.5)
                    elif kind == 1:
                        blk = _rope128(blk, cs, sa, sb, 1.0)
                    d = DILATIONS[gi]
                    if d == 1:
                        q1_ref[kind, 0, :, lanes] = blk.astype(BF16)
                    else:
                        scr[...] = blk
                        for r in range(d):
                            qkv_refs[gi][kind, r, :, lanes] = scr[pl.ds(r, tm // d, stride=d), :].astype(BF16)

    cls = lambda d: pl.BlockSpec((3, d, tm // d, GROUP_W), lambda i: (0, 0, i, 0))
    return pl.pallas_call(
        body,
        name="inproj_fwd",
        grid=(S // tm,),
        in_specs=[_rows(tm, D_MODEL), _full(w_in_g.shape), _rows(tm, 128), _rows(tm, 128), _rows(tm, 128)],
        out_specs=[_rows(tm, POOL_W)] + [cls(d) for d in DILATIONS],
        out_shape=[jax.ShapeDtypeStruct((S, POOL_W), F32)]
        + [jax.ShapeDtypeStruct((3, d, S // d, GROUP_W), BF16) for d in DILATIONS],
        scratch_shapes=[pltpu.VMEM((tm, 128), F32)],
        compiler_params=_params(1),
    )(h1, w_in_g, tc, tsa, tsb)


def _attn_masks():
    row = lax.broadcasted_iota(jnp.int32, (2 * ATT_BLOCK, 2 * ATT_BLOCK), 0) % ATT_BLOCK
    col = lax.broadcasted_iota(jnp.int32, (2 * ATT_BLOCK, 2 * ATT_BLOCK), 1)
    band = (col >= row) & (col <= row + ATT_BLOCK)
    lane = lax.broadcasted_iota(jnp.int32, (ATT_BLOCK, 128), 1)
    return band, col, lane < HEAD_DIM


def _classes_per_step(d, nb):
    return min(d, max(1, 8 // nb))


def _stack_heads(t, lo):
    z = jnp.zeros_like(t)
    return jnp.concatenate([jnp.where(lo, t, z), jnp.where(lo, z, t)], axis=0)


def _unstack_heads(t2, lo):
    return jnp.where(lo, t2[:ATT_BLOCK], t2[ATT_BLOCK:])


def attn_fwd(qkv, d):
    L = qkv.shape[2]
    nb = L // ATT_BLOCK
    cpb = _classes_per_step(d, nb)

    def body(q_ref, k_ref, v_ref, o_ref, l_ref, kpad, vpad):
        for cls in range(cpb):
            kpad[cls, 0:ATT_BLOCK, :] = jnp.zeros((ATT_BLOCK, GROUP_W), BF16)
            vpad[cls, 0:ATT_BLOCK, :] = jnp.zeros((ATT_BLOCK, GROUP_W), BF16)
            kpad[cls, ATT_BLOCK:, :] = k_ref[cls]
            vpad[cls, ATT_BLOCK:, :] = v_ref[cls]
        band, col, lo = _attn_masks()

        def step(t, carry):
            cls, n = t // nb, t % nb
            r0 = pl.multiple_of(n * ATT_BLOCK, ATT_BLOCK)
            valid = band & ((col >= ATT_BLOCK) | (n > 0))
            qb = q_ref[cls, pl.ds(r0, ATT_BLOCK), :]
            kb = kpad[cls, pl.ds(r0, 2 * ATT_BLOCK), :]
            vb = vpad[cls, pl.ds(r0, 2 * ATT_BLOCK), :]
            for pair in range(2):
                lanes = slice(pair * 128, (pair + 1) * 128)
                qp, kp, vp = qb[:, lanes], kb[:, lanes], vb[:, lanes]
                s = lax.dot_general(_stack_heads(qp, lo), kp, NT, preferred_element_type=F32)
                s = jnp.where(valid, s, NEG)
                m = jnp.max(s, axis=1, keepdims=True)
                p = jnp.exp(s - m)
                den = jnp.sum(p, axis=1, keepdims=True)
                pv = jnp.dot(p.astype(BF16), vp, preferred_element_type=F32)
                o_ref[cls, pl.ds(r0, ATT_BLOCK), lanes] = _unstack_heads(pv / den, lo)
                l_ref[cls, pl.ds(r0, ATT_BLOCK), lanes] = _unstack_heads(jnp.broadcast_to(m + jnp.log(den), pv.shape), lo)
            return carry

        lax.fori_loop(0, cpb * nb, step, 0, unroll=4)

    spec = lambda kind: pl.BlockSpec((None, cpb, L, GROUP_W), lambda r: (kind, r, 0, 0))
    return pl.pallas_call(
        body,
        name=f"attn_fwd_d{d}",
        grid=(d // cpb,),
        in_specs=[spec(0), spec(1), spec(2)],
        out_specs=[pl.BlockSpec((cpb, L, GROUP_W), lambda r: (r, 0, 0))] * 2,
        out_shape=[jax.ShapeDtypeStruct((d, L, GROUP_W), F32)] * 2,
        scratch_shapes=[pltpu.VMEM((cpb, L + ATT_BLOCK, GROUP_W), BF16)] * 2,
        compiler_params=_params(1),
    )(qkv, qkv, qkv)


def _pool_lane_windows(shape):
    lane = lax.broadcasted_iota(jnp.int32, shape, 1)
    return lane, jnp.where(lane < 64, 2, jnp.where(lane < 128, 4, jnp.where(lane < 192, 8, 16)))


def pool_fwd(u, wbd, b, scale):
    S = u.shape[0]

    def body(u_ref, w_ref, b_ref, s_ref, mixed_ref, out_ref):
        uv = u_ref[...]
        row = lax.broadcasted_iota(jnp.int32, uv.shape, 0)
        lane, win = _pool_lane_windows(uv.shape)

        def shift(a, k):
            return jnp.where(row >= k, pltpu.roll(a, k, 0), 0.0)

        s2 = uv + shift(uv, 1)
        s4 = s2 + shift(s2, 2)
        s8 = s4 + shift(s4, 4)
        s16 = s8 + shift(s8, 8)
        tsum = jnp.where(lane < 64, s2, jnp.where(lane < 128, s4, jnp.where(lane < 192, s8, s16)))
        cnt = jnp.minimum(row + 1, win).astype(F32)
        mb = (tsum / cnt - uv).astype(BF16)
        mixed_ref[...] = mb
        y = jnp.dot(mb, w_ref[...], preferred_element_type=F32) + b_ref[...]
        out_ref[...] = (y * s_ref[...]).astype(BF16)

    vm = pl.BlockSpec(memory_space=pltpu.VMEM)
    return pl.pallas_call(
        body,
        name="pool_fwd",
        in_specs=[vm] * 4,
        out_specs=[vm] * 2,
        out_shape=[jax.ShapeDtypeStruct((S, POOL_W), BF16)] * 2,
        compiler_params=_params(),
    )(u, wbd, b, scale)


def outproj_fwd(o_l, pool, x, w_out_g, g_post, g_pre, mod6, tm=512):
    S = x.shape[0]

    def body(o0, o1, o2, l0, l1, l2, pool_ref, x_ref, w_ref, gpost_ref, gpre_ref, mod_ref,
             cat_ref, lse_ref, lse4_ref, lse16_ref, y1_ref, x1_ref, h2_ref, so4, sl4, so16, sl16):
        for d, src, dst in ((4, o1, so4), (4, l1, sl4), (16, o2, so16), (16, l2, sl16)):
            for r in range(d):
                for h in range(2):
                    dst[h, pl.ds(r, tm // d, stride=d), :] = src[r, :, h * 128:(h + 1) * 128]
        nat = lambda ref: jnp.concatenate([ref[0], ref[1]], axis=1)
        a, b, c = l0[0], nat(sl4), nat(sl16)
        m = jnp.maximum(jnp.maximum(a, b), c)
        e0, e1, e2 = jnp.exp(a - m), jnp.exp(b - m), jnp.exp(c - m)
        z = e0 + e1 + e2
        lse = m + jnp.log(z)
        lse_ref[...] = lse
        for h in range(2):
            sl4[h] = lse[:, h * 128:(h + 1) * 128]
        for d, dst in ((4, lse4_ref), (16, lse16_ref)):
            for r in range(d):
                for h in range(2):
                    dst[r, :, h * 128:(h + 1) * 128] = sl4[h, pl.ds(r, tm // d, stride=d), :]
        attn = (e0 * o0[0] + e1 * nat(so4) + e2 * nat(so16)) / z
        cat = jnp.concatenate([pool_ref[...], attn.astype(BF16)], axis=1)
        cat_ref[...] = cat
        y1 = jnp.concatenate([jnp.dot(cat, w_ref[j], preferred_element_type=F32) for j in range(N_CHIPS)], axis=1)
        y1_ref[...] = y1
        rstd = lax.rsqrt(jnp.mean(y1 * y1, axis=-1, keepdims=True) + NORM_EPS)
        x1 = x_ref[...] + mod_ref[2:3, :] * ((y1 * rstd) * gpost_ref[...])
        x1_ref[...] = x1
        rstd2 = lax.rsqrt(jnp.mean(x1 * x1, axis=-1, keepdims=True) + NORM_EPS)
        h2 = ((x1 * rstd2) * gpre_ref[...]) * (1.0 + mod_ref[4:5, :]) + mod_ref[3:4, :]
        h2_ref[...] = h2.astype(BF16)

    t256 = _rows(tm, GROUP_W)
    cls = lambda d: pl.BlockSpec((d, tm // d, GROUP_W), lambda i: (0, i, 0))
    cls_shape = lambda d: jax.ShapeDtypeStruct((d, S // d, GROUP_W), F32)
    return pl.pallas_call(
        body,
        name="outproj_fwd",
        grid=(S // tm,),
        in_specs=[cls(d) for d in DILATIONS] * 2 + [t256, _rows(tm, D_MODEL), _full(w_out_g.shape), _full((1, D_MODEL)),
                                                    _full((1, D_MODEL)), _full((6, D_MODEL))],
        out_specs=[_rows(tm, 512), t256, cls(4), cls(16), _rows(tm, D_MODEL), _rows(tm, D_MODEL), _rows(tm, D_MODEL)],
        out_shape=[jax.ShapeDtypeStruct((S, 512), BF16), jax.ShapeDtypeStruct((S, GROUP_W), F32), cls_shape(4), cls_shape(16),
                   jax.ShapeDtypeStruct((S, D_MODEL), F32), jax.ShapeDtypeStruct((S, D_MODEL), F32),
                   jax.ShapeDtypeStruct((S, D_MODEL), BF16)],
        scratch_shapes=[pltpu.VMEM((2, tm, 128), F32)] * 4,
        compiler_params=_params(1),
    )(*o_l, pool, x, w_out_g, g_post, g_pre, mod6)


def _halo_prev(tm, ncol):
    return pl.BlockSpec((16, ncol), lambda i: (jnp.maximum(i * (tm // 16) - 1, 0), 0))


def ffn_fwd(h2, w_up_g, conv_w, conv_b, w_down, x1, target, g_post, mod6, tm=256):
    S = x1.shape[0]

    def body(h_ref, wu_ref, cw_ref, cb_ref, wd_ref, x1_ref, tgt_ref, g_ref, mod_ref,
             gate_ref, val_ref, a_ref, dy2_ref, dout_ref, loss_ref, dgt_ref, dg_ref, carry):
        first = pl.program_id(0) == 0

        @pl.when(first)
        def _():
            carry[...] = jnp.zeros_like(carry)

        hb = h_ref[...]
        y2 = jnp.zeros((tm, D_MODEL), F32)
        for ch in range(2):
            cols = slice(ch * HALF_FF, (ch + 1) * HALF_FF)
            gb = jnp.dot(hb, wu_ref[ch], preferred_element_type=F32).astype(BF16)
            vb = jnp.dot(hb, wu_ref[2 + ch], preferred_element_type=F32).astype(BF16)
            gate_ref[:, cols] = gb
            val_ref[:, cols] = vb
            gt = gb.astype(F32)
            g1, g2 = _conv_taps(gt, carry[:, cols], first)
            carry[:, cols] = gt[tm - 8:, :]
            gc = g2 * cw_ref[0:1, cols] + g1 * cw_ref[1:2, cols] + gt * cw_ref[2:3, cols] + cb_ref[:, cols]
            ab = (_gelu(gc)[0] * vb.astype(F32)).astype(BF16)
            a_ref[:, cols] = ab
            y2 = y2 + jnp.dot(ab, wd_ref[cols, :], preferred_element_type=F32)
        rstd = lax.rsqrt(jnp.mean(y2 * y2, axis=-1, keepdims=True) + NORM_EPS)
        y2n = y2 * rstd
        gv = g_ref[...]
        gtf = mod_ref[5:6, :]
        r2 = y2n * gv
        diff = (x1_ref[...] + gtf * r2) - tgt_ref[...]
        _acc(loss_ref, jnp.zeros((1, 128), F32) + 0.5 * jnp.sum(diff * diff) * (1.0 / D_MODEL))
        dout = diff * (1.0 / D_MODEL)
        dout_ref[...] = dout
        _acc(dgt_ref, _colsum(dout * r2))
        dr2 = dout * gtf
        _acc(dg_ref, _colsum(dr2 * y2n))
        dyn = dr2 * gv
        dy2 = rstd * (dyn - y2n * jnp.mean(dyn * y2n, axis=-1, keepdims=True))
        dy2_ref[...] = dy2.astype(BF16)

    vec = _full((1, D_MODEL))
    return pl.pallas_call(
        body,
        name="ffn_fwd",
        grid=(S // tm,),
        in_specs=[_rows(tm, D_MODEL), _full(w_up_g.shape), _full((3, D_FF)), _full((1, D_FF)), _full((D_FF, D_MODEL)),
                  _rows(tm, D_MODEL), _rows(tm, D_MODEL), vec, _full((6, D_MODEL))],
        out_specs=[_rows(tm, D_FF), _rows(tm, D_FF), _rows(tm, D_FF), _rows(tm, D_MODEL), _rows(tm, D_MODEL), _full((1, 128)), vec, vec],
        out_shape=[jax.ShapeDtypeStruct((S, D_FF), BF16)] * 3 + [jax.ShapeDtypeStruct((S, D_MODEL), BF16),
                                                                 jax.ShapeDtypeStruct((S, D_MODEL), F32),
                                                                 jax.ShapeDtypeStruct((1, 128), F32),
                                                                 jax.ShapeDtypeStruct((1, D_MODEL), F32),
                                                                 jax.ShapeDtypeStruct((1, D_MODEL), F32)],
        scratch_shapes=[pltpu.VMEM((8, D_FF), F32)],
        compiler_params=_params(1),
    )(h2, w_up_g, conv_w, conv_b, w_down, x1, target, g_post, mod6)


def down_bwd(dy2, w_down, gate, val, conv_w, conv_b, a, h2, tm=256):
    S = dy2.shape[0]

    def body(dy_ref, w_ref, gate_ref, halo_ref, val_ref, cw_ref, cb_ref, a_ref, h_ref,
             dgc_ref, dval_ref, dcw_ref, dcb_ref, dwd_ref, dwu_ref):
        first = pl.program_id(0) == 0

        @pl.when(first)
        def _():
            dcw_ref[...] = jnp.zeros_like(dcw_ref)
            dcb_ref[...] = jnp.zeros_like(dcb_ref)
            dwd_ref[...] = jnp.zeros_like(dwd_ref)
            dwu_ref[...] = jnp.zeros_like(dwu_ref)

        dyb = dy_ref[...]
        hb = h_ref[...]
        def col(i):
            ch, off, width = FF_CHUNKS[i]
            return slice(ch * HALF_FF + off, ch * HALF_FF + off + width)

        def mm_da(i):
            return lax.dot_general(dyb, w_ref[col(i), :], NT, preferred_element_type=F32)

        def elementwise(i, da):
            cols = col(i)
            gt = gate_ref[:, cols].astype(F32)
            g1, g2 = _conv_taps(gt, halo_ref[:, cols].astype(F32), first)
            gc = g2 * cw_ref[0:1, cols] + g1 * cw_ref[1:2, cols] + gt * cw_ref[2:3, cols] + cb_ref[:, cols]
            ge, u, th, z2 = _gelu(gc)
            dgc = da * val_ref[:, cols].astype(F32) * _gelu_grad(gc, u, th, z2)
            dgc_ref[:, cols] = dgc.astype(BF16)
            dvb = (da * ge).astype(BF16)
            dval_ref[:, cols] = dvb
            dcb_ref[:, cols] += _colsum(dgc)
            dcw_ref[0:1, cols] += _colsum(dgc * g2)
            dcw_ref[1:2, cols] += _colsum(dgc * g1)
            dcw_ref[2:3, cols] += _colsum(dgc * gt)
            return dvb

        def mm_dw(i, dvb):
            ch, off, width = FF_CHUNKS[i]
            dwd_ref[col(i), :] += lax.dot_general(a_ref[:, col(i)], dyb, TN, preferred_element_type=F32)
            dwu_ref[ch, :, off:off + width] += lax.dot_general(hb, dvb, TN, preferred_element_type=F32)

        n = len(FF_CHUNKS)
        da = mm_da(0)
        prev = None
        for i in range(n):
            nxt = mm_da(i + 1) if i + 1 < n else None
            if prev is not None:
                mm_dw(i - 1, prev)
            prev = elementwise(i, da)
            da = nxt
        mm_dw(n - 1, prev)

    return pl.pallas_call(
        body,
        name="down_bwd",
        grid=(S // tm,),
        in_specs=[_rows(tm, D_MODEL), _full((D_FF, D_MODEL)), _rows(tm, D_FF), _halo_prev(tm, D_FF), _rows(tm, D_FF),
                  _full((3, D_FF)), _full((1, D_FF)), _rows(tm, D_FF), _rows(tm, D_MODEL)],
        out_specs=[_rows(tm, D_FF), _rows(tm, D_FF), _full((3, D_FF)), _full((1, D_FF)), _full((D_FF, D_MODEL)),
                   pl.BlockSpec((2, D_MODEL, HALF_FF), lambda i: (1, 0, 0), pipeline_mode=pl.Buffered(1))],
        out_shape=[jax.ShapeDtypeStruct((S, D_FF), BF16), jax.ShapeDtypeStruct((S, D_FF), BF16),
                   jax.ShapeDtypeStruct((3, D_FF), F32), jax.ShapeDtypeStruct((1, D_FF), F32),
                   jax.ShapeDtypeStruct((D_FF, D_MODEL), F32), jax.ShapeDtypeStruct((N_CHIPS, D_MODEL, HALF_FF), F32)],
        compiler_params=_params(1),
    )(dy2, w_down, gate, gate, val, conv_w, conv_b, a, h2)


def up_bwd(dgc, dval, conv_w, w_up_g, x1, dout, y1, g_pre, g_post, mod6, h2, dw_up, tm=256):
    S = x1.shape[0]
    last_blk = S // 16 - 1

    def body(dgc_ref, nxt_ref, dval_ref, cw_ref, w_ref, x1_ref, dout_ref, y1_ref, gpre_ref, gpost_ref, mod_ref, h_ref, dwin_ref,
             dx1_ref, dy1_ref, dsh_ref, dsc_ref, dgpre_ref, dgt_ref, dgpost_ref, dwu_ref):
        last = pl.program_id(0) == pl.num_programs(0) - 1

        @pl.when(pl.program_id(0) == 0)
        def _():
            dwu_ref[...] = jnp.zeros_like(dwu_ref)

        hb = h_ref[...]
        dh = jnp.zeros((tm, D_MODEL), F32)
        for ch in range(2):
            cols = slice(ch * HALF_FF, (ch + 1) * HALF_FF)
            dg = dgc_ref[:, cols].astype(F32)
            nx = jnp.where(last, 0.0, nxt_ref[:, cols].astype(F32))
            row = lax.broadcasted_iota(jnp.int32, dg.shape, 0)
            n0, n1 = nx[0:1, :], nx[1:2, :]
            u1 = jnp.where(row == tm - 1, n0, pltpu.roll(dg, tm - 1, 0))
            u2 = jnp.where(row == tm - 1, n1, jnp.where(row == tm - 2, n0, pltpu.roll(dg, tm - 2, 0)))
            dgate = (dg * cw_ref[2:3, cols] + u1 * cw_ref[1:2, cols] + u2 * cw_ref[0:1, cols]).astype(BF16)
            dwu_ref[ch] += lax.dot_general(hb, dgate, TN, preferred_element_type=F32)
            dh = dh + lax.dot_general(dgate, w_ref[ch], NT, preferred_element_type=F32)
            dh = dh + lax.dot_general(dval_ref[:, cols], w_ref[2 + ch], NT, preferred_element_type=F32)
        x1 = x1_ref[...]
        rstd = lax.rsqrt(jnp.mean(x1 * x1, axis=-1, keepdims=True) + NORM_EPS)
        n2 = x1 * rstd
        gpre = gpre_ref[...]
        one_sc = 1.0 + mod_ref[4:5, :]
        _acc(dsh_ref, _colsum(dh))
        _acc(dsc_ref, _colsum(dh * (n2 * gpre)))
        _acc(dgpre_ref, _colsum(dh * one_sc * n2))
        dn = dh * (gpre * one_sc)
        dx1 = dout_ref[...] + rstd * (dn - n2 * jnp.mean(dn * n2, axis=-1, keepdims=True))
        dx1_ref[...] = dx1
        y1 = y1_ref[...]
        rstd1 = lax.rsqrt(jnp.mean(y1 * y1, axis=-1, keepdims=True) + NORM_EPS)
        y1n = y1 * rstd1
        gpost = gpost_ref[...]
        gtm = mod_ref[2:3, :]
        _acc(dgt_ref, _colsum(dx1 * (y1n * gpost)))
        dr1 = dx1 * gtm
        _acc(dgpost_ref, _colsum(dr1 * y1n))
        dyn = dr1 * gpost
        dy1 = rstd1 * (dyn - y1n * jnp.mean(dyn * y1n, axis=-1, keepdims=True))
        dy1_ref[...] = dy1.astype(BF16)

    vec = _full((1, D_MODEL))
    nxt = pl.BlockSpec((16, D_FF), lambda i: (jnp.minimum((i + 1) * (tm // 16), last_blk), 0))
    return pl.pallas_call(
        body,
        name="up_bwd",
        grid=(S // tm,),
        in_specs=[_rows(tm, D_FF), nxt, _rows(tm, D_FF), _full((3, D_FF)), _full(w_up_g.shape), _rows(tm, D_MODEL),
                  _rows(tm, D_MODEL), _rows(tm, D_MODEL), vec, vec, _full((6, D_MODEL)), _rows(tm, D_MODEL),
                  pl.BlockSpec(memory_space=pl.ANY)],
        out_specs=[_rows(tm, D_MODEL), _rows(tm, D_MODEL), vec, vec, vec, vec, vec,
                   pl.BlockSpec((2, D_MODEL, HALF_FF), lambda i: (0, 0, 0), pipeline_mode=pl.Buffered(1))],
        out_shape=[jax.ShapeDtypeStruct((S, D_MODEL), F32), jax.ShapeDtypeStruct((S, D_MODEL), BF16)]
        + [jax.ShapeDtypeStruct((1, D_MODEL), F32)] * 5 + [jax.ShapeDtypeStruct(dw_up.shape, F32)],
        input_output_aliases={12: 7},
        compiler_params=_params(1),
    )(dgc, dgc, dval, conv_w, w_up_g, x1, dout, y1, g_pre, g_post, mod6, h2, dw_up)


def outproj_bwd(dy1, w_out_g, cat, tm=512):
    S = dy1.shape[0]

    def body(dy_ref, w_ref, cat_ref, dpool_ref, dattn_ref, da4_ref, da16_ref, delta_ref, dl4_ref, dl16_ref, dw_ref, scr):
        @pl.when(pl.program_id(0) == 0)
        def _():
            dw_ref[...] = jnp.zeros_like(dw_ref)

        catb = cat_ref[...]
        dcat = jnp.zeros((tm, 512), F32)
        for j in range(N_CHIPS):
            dyj = dy_ref[:, j * 256:(j + 1) * 256]
            dcat = dcat + lax.dot_general(dyj, w_ref[j], NT, preferred_element_type=F32)
            dw_ref[j] += lax.dot_general(catb, dyj, TN, preferred_element_type=F32)
        dpool_ref[...] = dcat[:, :POOL_W]
        dattn = dcat[:, POOL_W:]
        dattn_ref[...] = dattn.astype(BF16)
        for h in range(2):
            scr[h] = dattn[:, h * 128:(h + 1) * 128]
        for d, dst in ((4, da4_ref), (16, da16_ref)):
            for r in range(d):
                for h in range(2):
                    dst[r, :, h * 128:(h + 1) * 128] = scr[h, pl.ds(r, tm // d, stride=d), :].astype(BF16)
        prod = dattn * catb[:, POOL_W:].astype(F32)
        r = lax.broadcasted_iota(jnp.int32, (GROUP_W, GROUP_W), 0) // HEAD_DIM
        c = lax.broadcasted_iota(jnp.int32, (GROUP_W, GROUP_W), 1) // HEAD_DIM
        ones_bd = jnp.where(r == c, 1.0, 0.0).astype(BF16)
        hi = prod.astype(BF16)
        lo = (prod - hi.astype(F32)).astype(BF16)
        delta = jnp.dot(hi, ones_bd, preferred_element_type=F32) + jnp.dot(lo, ones_bd, preferred_element_type=F32)
        delta_ref[...] = delta
        for h in range(2):
            scr[h] = delta[:, h * 128:(h + 1) * 128]
        for d, dst in ((4, dl4_ref), (16, dl16_ref)):
            for r in range(d):
                for h in range(2):
                    dst[r, :, h * 128:(h + 1) * 128] = scr[h, pl.ds(r, tm // d, stride=d), :]

    cls = lambda d: pl.BlockSpec((d, tm // d, GROUP_W), lambda i: (0, i, 0))
    cls_shape = lambda d, dt: jax.ShapeDtypeStruct((d, S // d, GROUP_W), dt)
    return pl.pallas_call(
        body,
        name="outproj_bwd",
        grid=(S // tm,),
        in_specs=[_rows(tm, D_MODEL), _full(w_out_g.shape), _rows(tm, 512)],
        out_specs=[_rows(tm, POOL_W), _rows(tm, GROUP_W), cls(4), cls(16), _rows(tm, GROUP_W), cls(4), cls(16),
                   _full(w_out_g.shape)],
        out_shape=[jax.ShapeDtypeStruct((S, POOL_W), F32), jax.ShapeDtypeStruct((S, GROUP_W), BF16), cls_shape(4, BF16),
                   cls_shape(16, BF16), jax.ShapeDtypeStruct((S, GROUP_W), F32), cls_shape(4, F32), cls_shape(16, F32),
                   jax.ShapeDtypeStruct(w_out_g.shape, F32)],
        scratch_shapes=[pltpu.VMEM((2, tm, 128), F32)],
        compiler_params=_params(1),
    )(dy1, w_out_g, cat)


def attn_bwd(qkv, dattn, lse, delta, d):
    L = qkv.shape[2]
    nb = L // ATT_BLOCK
    cpb = _classes_per_step(d, nb)

    def body(q_ref, k_ref, v_ref, do_ref, l_ref, dl_ref, out_ref, kpad, vpad, dkpad, dvpad):
        for cls in range(cpb):
            kpad[cls, 0:ATT_BLOCK, :] = jnp.zeros((ATT_BLOCK, GROUP_W), BF16)
            vpad[cls, 0:ATT_BLOCK, :] = jnp.zeros((ATT_BLOCK, GROUP_W), BF16)
            kpad[cls, ATT_BLOCK:, :] = k_ref[cls]
            vpad[cls, ATT_BLOCK:, :] = v_ref[cls]
        dkpad[...] = jnp.zeros_like(dkpad)
        dvpad[...] = jnp.zeros_like(dvpad)
        band, col, lo = _attn_masks()

        def step(t, carry):
            cls, n = t // nb, t % nb
            r0 = pl.multiple_of(n * ATT_BLOCK, ATT_BLOCK)
            valid = band & ((col >= ATT_BLOCK) | (n > 0))
            qb = q_ref[cls, pl.ds(r0, ATT_BLOCK), :]
            dob = do_ref[cls, pl.ds(r0, ATT_BLOCK), :]
            lb = l_ref[cls, pl.ds(r0, ATT_BLOCK), :]
            dlb = dl_ref[cls, pl.ds(r0, ATT_BLOCK), :]
            kb = kpad[cls, pl.ds(r0, 2 * ATT_BLOCK), :]
            vb = vpad[cls, pl.ds(r0, 2 * ATT_BLOCK), :]
            for pair in range(2):
                lanes = slice(pair * 128, (pair + 1) * 128)
                qp, dop, kp, vp = qb[:, lanes], dob[:, lanes], kb[:, lanes], vb[:, lanes]
                c0, c1 = pair * 128, pair * 128 + HEAD_DIM
                q2, do2 = _stack_heads(qp, lo), _stack_heads(dop, lo)
                lse2 = jnp.concatenate([lb[:, c0:c0 + 1], lb[:, c1:c1 + 1]], axis=0)
                dl2 = jnp.concatenate([dlb[:, c0:c0 + 1], dlb[:, c1:c1 + 1]], axis=0)
                s = lax.dot_general(q2, kp, NT, preferred_element_type=F32)
                s = jnp.where(valid, s, NEG)
                p = jnp.exp(s - lse2)
                dp = lax.dot_general(do2, vp, NT, preferred_element_type=F32)
                ds = (p * (dp - dl2)).astype(BF16)
                dq2 = jnp.dot(ds, kp, preferred_element_type=F32)
                out_ref[0, cls, pl.ds(r0, ATT_BLOCK), lanes] = _unstack_heads(dq2, lo)
                dkpad[cls, pl.ds(r0, 2 * ATT_BLOCK), lanes] += lax.dot_general(ds, q2, TN, preferred_element_type=F32)
                dvpad[cls, pl.ds(r0, 2 * ATT_BLOCK), lanes] += lax.dot_general(p.astype(BF16), do2, TN, preferred_element_type=F32)
            return carry

        lax.fori_loop(0, cpb * nb, step, 0, unroll=4)
        for cls in range(cpb):
            out_ref[1, cls] = dkpad[cls, ATT_BLOCK:, :]
            out_ref[2, cls] = dvpad[cls, ATT_BLOCK:, :]

    spec = lambda kind: pl.BlockSpec((None, cpb, L, GROUP_W), lambda r: (kind, r, 0, 0))
    per_cls = pl.BlockSpec((cpb, L, GROUP_W), lambda r: (r, 0, 0))
    return pl.pallas_call(
        body,
        name=f"attn_bwd_d{d}",
        grid=(d // cpb,),
        in_specs=[spec(0), spec(1), spec(2), per_cls, per_cls, per_cls],
        out_specs=pl.BlockSpec((3, cpb, L, GROUP_W), lambda r: (0, r, 0, 0)),
        out_shape=jax.ShapeDtypeStruct((3, d, L, GROUP_W), F32),
        scratch_shapes=[pltpu.VMEM((cpb, L + ATT_BLOCK, GROUP_W), BF16)] * 2 + [pltpu.VMEM((cpb, L + ATT_BLOCK, GROUP_W), F32)] * 2,
        compiler_params=_params(1),
    )(qkv, qkv, qkv, dattn, lse, delta)


def pool_bwd(dpool, mixed, wbd, b, scale):
    S = dpool.shape[0]

    def body(dp_ref, mx_ref, w_ref, b_ref, s_ref, du_ref, dw_ref, db_ref, ds_ref):
        dp = dp_ref[...]
        mb = mx_ref[...]
        wv = w_ref[...]
        ypre = jnp.dot(mb, wv, preferred_element_type=F32) + b_ref[...]
        ds_ref[...] = _colsum(dp * ypre)
        dpre = dp * s_ref[...]
        db_ref[...] = _colsum(dpre)
        dpb = dpre.astype(BF16)
        dw_ref[...] = lax.dot_general(mb, dpb, TN, preferred_element_type=F32)
        dmix = lax.dot_general(dpb, wv, NT, preferred_element_type=F32)
        row = lax.broadcasted_iota(jnp.int32, dmix.shape, 0)
        lane, win = _pool_lane_windows(dmix.shape)
        e = dmix / jnp.minimum(row + 1, win).astype(F32)

        def shift(a, k):
            return jnp.where(row < S - k, pltpu.roll(a, S - k, 0), 0.0)

        f2 = e + shift(e, 1)
        f4 = f2 + shift(f2, 2)
        f8 = f4 + shift(f4, 4)
        f16 = f8 + shift(f8, 8)
        du_ref[...] = jnp.where(lane < 64, f2, jnp.where(lane < 128, f4, jnp.where(lane < 192, f8, f16))) - dmix

    vm = pl.BlockSpec(memory_space=pltpu.VMEM)
    return pl.pallas_call(
        body,
        name="pool_bwd",
        in_specs=[vm] * 5,
        out_specs=[vm] * 4,
        out_shape=[jax.ShapeDtypeStruct((S, POOL_W), F32), jax.ShapeDtypeStruct((POOL_W, POOL_W), F32),
                   jax.ShapeDtypeStruct((1, POOL_W), F32), jax.ShapeDtypeStruct((1, POOL_W), F32)],
        compiler_params=_params(),
    )(dpool, mixed, wbd, b, scale)


def inproj_bwd(dqkv, du, x, dx1, w_in_g, g, mod6, tc, tsa, tsb, h1, tm=512):
    S = x.shape[0]

    def body(d0, d1, d2, du_ref, x_ref, dx1_ref, w_ref, g_ref, mod_ref, tc_ref, tsa_ref, tsb_ref, h_ref,
             gx_ref, dsh_ref, dsc_ref, dg_ref, dw_ref, s4, s16, dp_ref):
        @pl.when(pl.program_id(0) == 0)
        def _():
            dw_ref[...] = jnp.zeros_like(dw_ref)

        cs, sa, sb = tc_ref[...], tsa_ref[...], tsb_ref[...]
        for d, src, dst in ((4, d1, s4), (16, d2, s16)):
            for kind in range(3):
                for r in range(d):
                    for h in range(2):
                        dst[kind, h, pl.ds(r, tm // d, stride=d), :] = src[kind, r, :, h * 128:(h + 1) * 128]
        for sp in range(20):
            piece, half = sp // 2, sp % 2
            lanes = slice(half * 128, (half + 1) * 128)
            if piece == 0:
                blk = du_ref[:, lanes]
            else:
                kind, gi = (piece - 1) // 3, (piece - 1) % 3
                blk = d0[kind, 0, :, lanes] if gi == 0 else (s4, s16)[gi - 1][kind, half]
                if kind == 0:
                    blk = _rope128(blk, cs, sa, sb, -1.0) * (HEAD_DIM ** -0.5)
                elif kind == 1:
                    blk = _rope128(blk, cs, sa, sb, -1.0)
            dp_ref[:, sp * 128:(sp + 1) * 128] = blk.astype(BF16)
        dh = jnp.zeros((tm, D_MODEL), F32)
        hbt = h_ref[...].T
        for j in range(N_CHIPS):
            dpj = dp_ref[:, j * 640:(j + 1) * 640]
            dh = dh + lax.dot_general(dpj, w_ref[j], NT, preferred_element_type=F32)
            dw_ref[j] += jnp.dot(hbt, dpj, preferred_element_type=F32)
        xv = x_ref[...]
        rstd = lax.rsqrt(jnp.mean(xv * xv, axis=-1, keepdims=True) + NORM_EPS)
        n1 = xv * rstd
        gv = g_ref[...]
        one_sc = 1.0 + mod_ref[1:2, :]
        _acc(dsh_ref, _colsum(dh))
        _acc(dsc_ref, _colsum(dh * (n1 * gv)))
        _acc(dg_ref, _colsum(dh * one_sc * n1))
        dn = dh * (gv * one_sc)
        gx_ref[...] = dx1_ref[...] + rstd * (dn - n1 * jnp.mean(dn * n1, axis=-1, keepdims=True))

    vec = _full((1, D_MODEL))
    dspec = lambda d: pl.BlockSpec((3, d, tm // d, GROUP_W), lambda i: (0, 0, i, 0))
    return pl.pallas_call(
        body,
        name="inproj_bwd",
        grid=(S // tm,),
        in_specs=[dspec(d) for d in DILATIONS] + [_rows(tm, POOL_W), _rows(tm, D_MODEL), _rows(tm, D_MODEL), _full(w_in_g.shape),
                                                  vec, _full((6, D_MODEL)), _rows(tm, 128), _rows(tm, 128), _rows(tm, 128),
                                                  _rows(tm, D_MODEL)],
        out_specs=[_rows(tm, D_MODEL), vec, vec, vec, _full(w_in_g.shape)],
        out_shape=[jax.ShapeDtypeStruct((S, D_MODEL), F32)] + [jax.ShapeDtypeStruct((1, D_MODEL), F32)] * 3
        + [jax.ShapeDtypeStruct(w_in_g.shape, F32)],
        scratch_shapes=[pltpu.VMEM((3, 2, tm, 128), F32)] * 2 + [pltpu.VMEM((tm, IN_W), BF16)],
        compiler_params=_params(1),
    )(*dqkv, du, x, dx1, w_in_g, g, mod6, tc, tsa, tsb, h1)


def _adamw(w, g, m, v):
    m = ADAM_B1 * m + (1.0 - ADAM_B1) * g
    v = ADAM_B2 * v + (1.0 - ADAM_B2) * (g * g)
    m_hat = m / (1.0 - ADAM_B1 ** ADAM_STEP)
    v_hat = v / (1.0 - ADAM_B2 ** ADAM_STEP)
    delta = -ADAM_LR * (m_hat / (jnp.sqrt(v_hat) + ADAM_EPS) + ADAM_WD * w)
    return delta, m, v


def adamw_rows(w, g, m, v, tr, name):
    R, C = w.shape

    def body(w_ref, g_ref, m_ref, v_ref, go_ref, d_ref, mo_ref, vo_ref):
        g = g_ref[...]
        go_ref[...] = g
        d_ref[...], mo_ref[...], vo_ref[...] = _adamw(w_ref[...], g, m_ref[...], v_ref[...])

    spec = pl.BlockSpec((tr, C), lambda i: (i, 0))
    return pl.pallas_call(
        body,
        name=name,
        grid=(R // tr,),
        in_specs=[spec] * 4,
        out_specs=[spec] * 4,
        out_shape=[jax.ShapeDtypeStruct((R, C), F32)] * 4,
        compiler_params=_params(1),
    )(w, g, m, v)


def adamw_ada(c_all_t, dmod_cols, w, m, v, tr=256):
    R, C = w.shape

    def body(ct_ref, dm_ref, w_ref, m_ref, v_ref, g_ref, d_ref, mo_ref, vo_ref):
        ct = ct_ref[...]
        act = ct * jax.nn.sigmoid(ct)
        dm = dm_ref[...]
        a_hi, d_hi = act.astype(BF16), dm.astype(BF16)
        a_lo, d_lo = (act - a_hi.astype(F32)).astype(BF16), (dm - d_hi.astype(F32)).astype(BF16)
        g = (jnp.dot(a_hi, d_hi, preferred_element_type=F32) + jnp.dot(a_lo, d_hi, preferred_element_type=F32)
             + jnp.dot(a_hi, d_lo, preferred_element_type=F32))
        g_ref[...] = g
        d_ref[...], mo_ref[...], vo_ref[...] = _adamw(w_ref[...], g, m_ref[...], v_ref[...])

    spec = pl.BlockSpec((tr, C), lambda i: (i, 0))
    return pl.pallas_call(
        body,
        name="adamw_ada",
        grid=(R // tr,),
        in_specs=[pl.BlockSpec((tr, N_DEV), lambda i: (i, 0)), _full((N_DEV, C)), spec, spec, spec],
        out_specs=[spec] * 4,
        out_shape=[jax.ShapeDtypeStruct((R, C), F32)] * 4,
        compiler_params=_params(1),
    )(c_all_t, dmod_cols, w, m, v)


def adamw_small(slab_a, slab_b, convw_g, wpool_g, params):
    names = ["b_ada", "g_pre_mix", "g_post_mix", "g_pre_ffn", "g_post_ffn", "b_pool", "pool_scale", "conv_b", "conv_w", "w_pool"]
    flat = []
    for n in names:
        flat += list(params[n])

    def body(a_ref, b_ref, cw_ref, wp_ref, *rest):
        ins, outs = rest[:30], rest[30:]

        def dev_sum(ref):
            t = ref[0]
            for dev in range(1, N_DEV):
                t = t + ref[dev]
            return t

        sa, sb_, scw, swp = dev_sum(a_ref), dev_sum(b_ref), dev_sum(cw_ref), dev_sum(wp_ref)
        grads = [
            jnp.concatenate([sa[k:k + 1, :] for k in range(6)], axis=1),
            sa[6:7, :], sa[7:8, :], sa[8:9, :], sa[9:10, :],
            sa[10:11, 0:256], sa[10:11, 256:512],
            sb_[3:4, :], scw, swp,
        ]
        for i, g in enumerate(grads):
            w_ref, m_ref, v_ref = ins[3 * i:3 * i + 3]
            if names[i] == "b_pool":
                parts = [((0, slice(grp, grp + 1)), g[:, grp * 64:(grp + 1) * 64]) for grp in range(4)]
            elif names[i] == "w_pool":
                parts = [((0, grp), g[grp * 64:(grp + 1) * 64, :]) for grp in range(4)]
            elif names[i] == "conv_w":
                parts = [((0,), g)]
            else:
                parts = [((Ellipsis,), g)]
            for at, gp in parts:
                d, mo, vo = _adamw(w_ref[at], gp, m_ref[at], v_ref[at])
                for k, val in enumerate((gp, d, mo, vo)):
                    outs[4 * i + k][at] = val
        outs[-1][...] = sa[10:11, 512:640]

    vm = pl.BlockSpec(memory_space=pltpu.VMEM)
    out_shape = []
    for n in names:
        out_shape += [jax.ShapeDtypeStruct(params[n][0].shape, F32)] * 4
    out_shape.append(jax.ShapeDtypeStruct((1, 128), F32))
    outs = pl.pallas_call(
        body,
        name="adamw_small",
        in_specs=[vm] * (4 + len(flat)),
        out_specs=[vm] * len(out_shape),
        out_shape=out_shape,
        compiler_params=_params(),
    )(slab_a, slab_b, convw_g, wpool_g, *flat)
    return {n: outs[4 * i:4 * i + 4] for i, n in enumerate(names)}, outs[-1]


def _place():
    return lax.axis_index("x"), lax.axis_index("y"), lax.axis_index("c")


def _other_chips(x, y):
    return [(1 - x, y), (x, 1 - y), (1 - x, 1 - y)]


def _chip_id(cx, cy):
    return 2 * cx + cy


HBM_SPEC = pl.BlockSpec(memory_space=pltpu.HBM)
SEM_SPEC = pl.BlockSpec(memory_space=pltpu.SEMAPHORE)
ANY_SPEC = pl.BlockSpec(memory_space=pl.ANY)
EFFECT = pltpu.SideEffectType.DATAFLOW_SIDE_EFFECTING


def _hbm(t):
    return pltpu.with_memory_space_constraint(t, pltpu.HBM)


def _hbm_shapes(ts):
    return [pltpu.HBM(t.shape, t.dtype) for t in ts]


def _half_rows(ref, lead, half, rh):
    return ref.at[lead, pl.ds(half * rh, rh), :]


def _flips():
    return [(fx, fy, fc) for fx in (0, 1) for fy in (0, 1) for fc in (0, 1)][1:]


def _flip(v, f):
    return v if f == 0 else 1 - v


def ada_mod(c3, w_ada, b_cols, conv_w):
    CB = w_ada.shape[1]

    def body(c_ref, w_ref, b_ref, cw_ref, call_ref, mod_ref, cwall_ref, modall, send_sems, recv_sems):
        x, y, c = _place()
        me_dev = 4 * x + 2 * y + c
        me = _chip_id(x, y)
        call_ref[me_dev] = c_ref[0]
        cwall_ref[me] = cw_ref[...]
        sends = []
        for k, (cx, cy) in enumerate(_other_chips(x, y)):
            cp = pltpu.make_async_remote_copy(src_ref=cw_ref, dst_ref=cwall_ref.at[me], send_sem=send_sems.at[10 + k],
                                              recv_sem=recv_sems.at[10 + k], device_id=(cx, cy, c), device_id_type=MESH)
            cp.start()
            sends.append(cp)
        for k, (fx, fy, fc) in enumerate(_flips()):
            cp = pltpu.make_async_remote_copy(src_ref=c_ref.at[0], dst_ref=call_ref.at[me_dev], send_sem=send_sems.at[k],
                                              recv_sem=recv_sems.at[k],
                                              device_id=(_flip(x, fx), _flip(y, fy), _flip(c, fc)), device_id_type=MESH)
            cp.start()
            sends.append(cp)
        for k, (fx, fy, fc) in enumerate(_flips()):
            peer = 4 * _flip(x, fx) + 2 * _flip(y, fy) + _flip(c, fc)
            pltpu.make_async_remote_copy(src_ref=c_ref.at[0], dst_ref=call_ref.at[peer], send_sem=send_sems.at[k],
                                         recv_sem=recv_sems.at[k], device_id=(x, y, c), device_id_type=MESH).wait_recv()
        row = lax.broadcasted_iota(jnp.int32, (N_DEV, D_MODEL), 0)
        call = jnp.zeros((N_DEV, D_MODEL), F32)
        for dev in range(N_DEV):
            call = jnp.where(row == dev, call_ref[dev], call)
        act = call * jax.nn.sigmoid(call)
        wv = w_ref[...]
        w_hi = wv.astype(BF16)
        w_lo = (wv - w_hi.astype(F32)).astype(BF16)
        a_hi = act.astype(BF16)
        a_lo = (act - a_hi.astype(F32)).astype(BF16)
        prod = (jnp.dot(a_hi, w_hi, preferred_element_type=F32) + jnp.dot(a_lo, w_hi, preferred_element_type=F32)
                + jnp.dot(a_hi, w_lo, preferred_element_type=F32))
        modall[me] = prod + b_ref[...]
        for k, (cx, cy) in enumerate(_other_chips(x, y)):
            cp = pltpu.make_async_remote_copy(src_ref=modall.at[me], dst_ref=modall.at[me], send_sem=send_sems.at[7 + k],
                                              recv_sem=recv_sems.at[7 + k], device_id=(cx, cy, c), device_id_type=MESH)
            cp.start()
            sends.append(cp)
        for k, (cx, cy) in enumerate(_other_chips(x, y)):
            blk = modall.at[_chip_id(cx, cy)]
            pltpu.make_async_remote_copy(src_ref=blk, dst_ref=blk, send_sem=send_sems.at[7 + k], recv_sem=recv_sems.at[7 + k],
                                         device_id=(x, y, c), device_id_type=MESH).wait_recv()
        for k, (cx, cy) in enumerate(_other_chips(x, y)):
            blk = cwall_ref.at[_chip_id(cx, cy)]
            pltpu.make_async_remote_copy(src_ref=blk, dst_ref=blk, send_sem=send_sems.at[10 + k], recv_sem=recv_sems.at[10 + k],
                                         device_id=(x, y, c), device_id_type=MESH).wait_recv()
        for cp in sends:
            cp.wait_send()
        mine = [modall[j, pl.ds(me_dev, 1), :] for j in range(N_CHIPS)]
        for r in range(6):
            pieces = []
            for h in range(2):
                pos = r * D_MODEL + h * 512
                pieces.append(mine[pos // CB][:, pos % CB:pos % CB + 512])
            mod_ref[r:r + 1, :] = jnp.concatenate(pieces, axis=1)

    vm = pl.BlockSpec(memory_space=pltpu.VMEM)
    return pl.pallas_call(
        body,
        name="ada_mod",
        in_specs=[vm] * 4,
        out_specs=[vm] * 3,
        out_shape=[jax.ShapeDtypeStruct((N_DEV, 1, D_MODEL), F32), jax.ShapeDtypeStruct((6, D_MODEL), F32),
                   jax.ShapeDtypeStruct((N_CHIPS,) + conv_w.shape, F32)],
        scratch_shapes=[pltpu.VMEM((N_CHIPS, N_DEV, CB), F32), pltpu.SemaphoreType.DMA((13,)), pltpu.SemaphoreType.DMA((13,))],
        compiler_params=pltpu.CompilerParams(has_side_effects=True, vmem_limit_bytes=VMEM_LIMIT),
    )(c3, w_ada, b_cols, conv_w)


def split_start(name, bufs, plan, n_sem, carry):
    nb = len(bufs)
    many = isinstance(carry, (list, tuple))
    alls = list(bufs) + (list(carry) if many else [carry])
    na = len(alls)

    def body(*refs):
        x, y, c = _place()
        ssem, rsem = refs[na], refs[na + 1]
        for i, (src, dst, dev) in enumerate(plan(refs[:nb], x, y, c)):
            pltpu.make_async_remote_copy(src_ref=src, dst_ref=dst, send_sem=ssem.at[i], recv_sem=rsem.at[i], device_id=dev,
                                         device_id_type=MESH).start()

    outs = pl.pallas_call(
        body,
        name=name,
        out_shape=[pltpu.SemaphoreType.DMA((n_sem,)), pltpu.SemaphoreType.DMA((n_sem,))] + _hbm_shapes(alls),
        in_specs=[HBM_SPEC] * na,
        out_specs=[SEM_SPEC, SEM_SPEC] + [HBM_SPEC] * na,
        input_output_aliases={i: 2 + i for i in range(na)},
        compiler_params=pltpu.CompilerParams(has_side_effects=EFFECT),
    )(*[_hbm(t) for t in alls])
    return outs[0], outs[1], list(outs[2:2 + nb]), (list(outs[2 + nb:]) if many else outs[-1])


def split_wait(name, ssem, rsem, bufs, plan, after):
    nb = len(bufs)

    def body(*refs):
        x, y, c = _place()
        s_ref, r_ref = refs[nb], refs[nb + 1]
        for i, (src, dst, dev) in enumerate(plan(refs[:nb], x, y, c)):
            cp = pltpu.make_async_remote_copy(src_ref=src, dst_ref=dst, send_sem=s_ref.at[i], recv_sem=r_ref.at[i], device_id=dev,
                                              device_id_type=MESH)
            cp.wait_send()
            cp.wait_recv()

    outs = pl.pallas_call(
        body,
        name=name,
        out_shape=_hbm_shapes(bufs),
        in_specs=[HBM_SPEC] * nb + [SEM_SPEC, SEM_SPEC, ANY_SPEC],
        out_specs=[HBM_SPEC] * nb,
        input_output_aliases={i: i for i in range(nb)},
        compiler_params=pltpu.CompilerParams(has_side_effects=EFFECT),
    )(*bufs, ssem, rsem, after)
    return list(outs)


def _gather_ici_plan(n):
    def plan(refs, x, y, c):
        out = []
        for w in range(n):
            rh = refs[w].shape[0] // 2
            for cx, cy in _other_chips(x, y):
                out.append((refs[w].at[pl.ds(c * rh, rh), :], _half_rows(refs[n + w], _chip_id(x, y), c, rh), (cx, cy, c)))
        return out

    return plan


def _gather_d2d_plan(n):
    def plan(refs, x, y, c):
        out = []
        for w in range(n):
            rh = refs[w].shape[1] // 2
            for cx, cy in _other_chips(x, y):
                blk = _half_rows(refs[w], _chip_id(cx, cy), c, rh)
                out.append((blk, blk, (x, y, 1 - c)))
        return out

    return plan


def _dev_id(x, y, c):
    return 4 * x + 2 * y + c


def _small_ici_plan(n):
    def plan(refs, x, y, c):
        out = []
        for w in range(n):
            dst = refs[n + w].at[_dev_id(x, y, c)]
            out.append((refs[w], dst, (x, y, 1 - c)))
            for cx, cy in _other_chips(x, y):
                out.append((refs[w], dst, (cx, cy, c)))
        return out

    return plan


def _small_d2d_plan(n):
    def plan(refs, x, y, c):
        out = []
        for w in range(n):
            for cx, cy in _other_chips(x, y):
                blk = refs[w].at[_dev_id(cx, cy, c)]
                out.append((blk, blk, (x, y, 1 - c)))
        return out

    return plan


def _rs_d2d_plan(n):
    def plan(refs, x, y, c):
        out = []
        for w in range(n):
            rh = refs[w].shape[1] // 2
            out.append((refs[w].at[:, pl.ds((1 - c) * rh, rh), :], refs[n + w], (x, y, 1 - c)))
        return out

    return plan


def _rs_ici_plan(n):
    def plan(refs, x, y, c):
        out = []
        for w in range(n):
            for k, (cx, cy) in enumerate(_other_chips(x, y)):
                out.append((refs[w].at[_chip_id(cx, cy)], refs[n + w].at[k], (cx, cy, c)))
        return out

    return plan


def _rs_share_plan(n):
    def plan(refs, x, y, c):
        out = []
        for w in range(n):
            rh = refs[w].shape[0] // 2
            rows = refs[w].at[pl.ds(c * rh, rh), :]
            out.append((rows, rows, (x, y, 1 - c)))
        return out

    return plan


def rs_add(grad, sibbuf, place, tr, name):
    _, R, C = grad.shape
    nt = (R // 2) // tr

    def body(p_ref, g_ref, s_ref, o_ref):
        o_ref[...] = (g_ref[...] + s_ref[...]).astype(BF16)

    return pl.pallas_call(
        body,
        name=name,
        grid_spec=pltpu.PrefetchScalarGridSpec(
            num_scalar_prefetch=1,
            grid=(N_CHIPS, nt),
            in_specs=[pl.BlockSpec((None, tr, C), lambda j, i, p: (j, p[0] * nt + i, 0)),
                      pl.BlockSpec((None, tr, C), lambda j, i, p: (j, i, 0))],
            out_specs=pl.BlockSpec((None, tr, C), lambda j, i, p: (j, i, 0)),
        ),
        out_shape=jax.ShapeDtypeStruct((N_CHIPS, R // 2, C), BF16),
        compiler_params=_params(2),
    )(place, grad, sibbuf)


def rs_final(grad, sibbuf, rbuf, place, tr, name):
    _, R, C = grad.shape
    nt = (R // 2) // tr

    def body(p_ref, g_ref, s_ref, r_ref, o_ref):
        o_ref[...] = (((g_ref[...] + s_ref[...]) + r_ref[0].astype(F32)) + r_ref[1].astype(F32)) + r_ref[2].astype(F32)

    return pl.pallas_call(
        body,
        name=name,
        grid_spec=pltpu.PrefetchScalarGridSpec(
            num_scalar_prefetch=1,
            grid=(nt,),
            in_specs=[pl.BlockSpec((None, tr, C), lambda i, p: (p[1], p[0] * nt + i, 0)),
                      pl.BlockSpec((None, tr, C), lambda i, p: (p[1], i, 0)),
                      pl.BlockSpec((3, tr, C), lambda i, p: (0, i, 0))],
            out_specs=pl.BlockSpec((tr, C), lambda i, p: (p[0] * nt + i, 0)),
        ),
        out_shape=jax.ShapeDtypeStruct((R, C), F32),
        compiler_params=_params(1),
    )(place, grad, sibbuf, rbuf)


class GradReduce:
    def __init__(self, tag, grads, rows, place):
        self.tag, self.grads, self.rows, self.place = tag, grads, rows, place
        self.n = len(grads)

    def d2d_start(self, carry):
        sib = [lax.empty((N_CHIPS, g.shape[1] // 2, g.shape[2]), F32) for g in self.grads]
        self.s1, self.r1, bufs, carry = split_start(f"rs_{self.tag}_d2d_start", self.grads + sib, _rs_d2d_plan(self.n), self.n, carry)
        self.bufs1 = bufs
        return carry

    def add_and_ici_start(self, after, carry):
        bufs = split_wait(f"rs_{self.tag}_d2d_wait", self.s1, self.r1, self.bufs1, _rs_d2d_plan(self.n), after)
        self.grads, self.sib = bufs[:self.n], bufs[self.n:]
        pb = [rs_add(g, s, self.place, tr, f"rs_{self.tag}_add{w}")
              for w, (g, s, tr) in enumerate(zip(self.grads, self.sib, self.rows))]
        rb = [lax.empty((3,) + p.shape[1:], BF16) for p in pb]
        self.s2, self.r2, self.bufs2, carry = split_start(f"rs_{self.tag}_ici_start", pb + rb, _rs_ici_plan(self.n), 3 * self.n, carry)
        return carry

    def final_and_share_start(self, after, carry):
        bufs = split_wait(f"rs_{self.tag}_ici_wait", self.s2, self.r2, self.bufs2, _rs_ici_plan(self.n), after)
        rb = bufs[self.n:]
        full = [rs_final(g, s, r, self.place, tr, f"rs_{self.tag}_final{w}")
                for w, (g, s, r, tr) in enumerate(zip(self.grads, self.sib, rb, self.rows))]
        self.s3, self.r3, self.bufs3, carry = split_start(f"rs_{self.tag}_share_start", full, _rs_share_plan(self.n), self.n, carry)
        return carry

    def finish(self, after):
        return split_wait(f"rs_{self.tag}_share_wait", self.s3, self.r3, self.bufs3, _rs_share_plan(self.n), after)


def _rope_tables(positions):
    inv_freq = ROPE_THETA ** (-jnp.arange(0, ROT_DIM, 2, dtype=F32) / ROT_DIM)
    ang = positions.astype(F32)[:, None] * inv_freq
    cos, sin = jnp.cos(ang), jnp.sin(ang)
    S = positions.shape[0]
    one, zero = jnp.ones((S, 48), F32), jnp.zeros((S, 48), F32)
    z8 = jnp.zeros((S, 8), F32)
    tc = jnp.concatenate([cos, cos, one], axis=1)
    tsa = jnp.concatenate([z8, sin, zero], axis=1)
    tsb = jnp.concatenate([-sin, z8, zero], axis=1)
    return tuple(jnp.tile(t, (1, 2)) for t in (tc, tsa, tsb))


def _block_diag(w_pool):
    wbd = jnp.zeros((POOL_W, POOL_W), F32)
    for gi in range(4):
        wbd = wbd.at[gi * 64:(gi + 1) * 64, gi * 64:(gi + 1) * 64].set(w_pool[gi])
    return wbd


def kernel(x, c, positions, w_ada, b_ada, g_pre_mix, g_post_mix, g_pre_ffn, g_post_ffn, w_in, w_pool, b_pool, pool_scale, w_out, w_up, conv_w, conv_b, w_down, loss_target, m_w_ada, m_b_ada, m_g_pre_mix, m_g_post_mix, m_g_pre_ffn, m_g_post_ffn, m_w_in, m_w_pool, m_b_pool, m_pool_scale, m_w_out, m_w_up, m_conv_w, m_conv_b, m_w_down, v_w_ada, v_b_ada, v_g_pre_mix, v_g_post_mix, v_g_pre_ffn, v_g_post_ffn, v_w_in, v_w_pool, v_b_pool, v_pool_scale, v_w_out, v_w_up, v_conv_w, v_conv_b, v_w_down):
    xi, yi, ci = lax.axis_index("x"), lax.axis_index("y"), lax.axis_index("c")
    chip = 2 * xi + yi
    place = jnp.stack([ci, chip]).astype(jnp.int32)
    x2, tgt = x[0], loss_target[0]
    S = x2.shape[0]

    def landing(s_):
        return lax.dynamic_update_slice(lax.empty((N_CHIPS,) + s_.shape, s_.dtype), s_[None], (chip, 0, 0))

    cb_ada = w_ada.shape[2]
    b_cols = lax.dynamic_slice(b_ada, (0, chip * cb_ada), (1, cb_ada))
    c_all, mod6, conv_w_g = ada_mod(c.reshape(1, 1, D_MODEL), w_ada[0], b_cols, conv_w[0])
    conv_w_f = jnp.transpose(conv_w_g, (1, 0, 2)).reshape(3, D_FF)
    mix_sh = [w_in[0].astype(BF16), w_out[0].astype(BF16)]
    ffn_sh = [w_up[0].astype(BF16), w_down[0].astype(BF16)]
    ga_s, ga_r, ga_bufs, mod6 = split_start("gather_mix_ici_start", mix_sh + [landing(t) for t in mix_sh], _gather_ici_plan(2), 6, mod6)
    gb_s, gb_r, gb_bufs, (mod6, tc, tsa, tsb) = split_start("gather_ffn_ici_start", ffn_sh + [landing(t) for t in ffn_sh],
                                                            _gather_ici_plan(2), 6, [mod6, *_rope_tables(positions[0])])
    wbd = _block_diag(w_pool[0]).astype(BF16)
    b_pool2, scale2 = b_pool.reshape(1, POOL_W), pool_scale
    h1 = prenorm_fwd(x2, g_pre_mix, mod6)
    ga_bufs = split_wait("gather_mix_ici_wait", ga_s, ga_r, ga_bufs, _gather_ici_plan(2), h1)
    gc_s, gc_r, mix_land, mod6 = split_start("gather_mix_d2d_start", ga_bufs[2:], _gather_d2d_plan(2), 6, mod6)
    w_in_g, w_out_g = split_wait("gather_mix_d2d_wait", gc_s, gc_r, mix_land, _gather_d2d_plan(2), mod6)

    u, *qkv = inproj_fwd(h1, w_in_g, tc, tsa, tsb)
    mixed, pool = pool_fwd(u, wbd, b_pool2, scale2)
    o_l = [attn_fwd(t, d) for t, d in zip(qkv, DILATIONS)]
    attn_done = sum(l[0, :8, :128] for _, l in o_l)
    gb_bufs = split_wait("gather_ffn_ici_wait", gb_s, gb_r, gb_bufs, _gather_ici_plan(2), attn_done)
    gd_s, gd_r, ffn_land, pool = split_start("gather_ffn_d2d_start", gb_bufs[2:], _gather_d2d_plan(2), 6, pool)
    cat, lse, lse4, lse16, y1, x1, h2 = outproj_fwd([o for o, _ in o_l] + [l for _, l in o_l], pool, x2, w_out_g, g_post_mix,
                                                    g_pre_ffn, mod6)
    lses = [lse[None], lse4, lse16]
    w_up_g, w_down_g = split_wait("gather_ffn_d2d_wait", gd_s, gd_r, ffn_land, _gather_d2d_plan(2), h2)
    w_down_f = w_down_g.reshape(D_FF, D_MODEL)
    gate, val, a, dy2, dout, loss_v, d_gt_f, d_g_post_ffn = ffn_fwd(h2, w_up_g, conv_w_f, conv_b, w_down_f, x1, tgt, g_post_ffn, mod6)

    dgc, dval, d_conv_w, d_conv_b, dw_down, dw_up = down_bwd(dy2, w_down_f, gate, val, conv_w_f, conv_b, a, h2)
    dx1, dy1, d_sh_f, d_sc_f, d_g_pre_ffn, d_gt_m, d_g_post_mix, dw_up = up_bwd(
        dgc, dval, conv_w_f, w_up_g, x1, dout, y1, g_pre_ffn, g_post_mix, mod6, h2, dw_up)
    rs_ffn = GradReduce("ffn", [dw_up, dw_down.reshape(N_CHIPS, D_FF // N_CHIPS, D_MODEL)], [256, 176], place)
    dy1 = rs_ffn.d2d_start(dy1)
    dpool, da1, da4, da16, dl1, dl4, dl16, dw_out = outproj_bwd(dy1, w_out_g, cat)
    dpool = rs_ffn.add_and_ici_start(dw_out, dpool)
    du, d_wbd, d_b_pool, d_scale = pool_bwd(dpool, mixed, wbd, b_pool2, scale2)
    dqkv = [attn_bwd(t, da, ls, dl, d) for t, da, ls, dl, d in zip(qkv, (da1[None], da4, da16), lses, (dl1[None], dl4, dl16), DILATIONS)]
    grad_x, d_sh_m, d_sc_m, d_g_pre_mix, dw_in = inproj_bwd(dqkv, du, x2, dx1, w_in_g, g_pre_mix, mod6, tc, tsa, tsb, h1)

    z1 = jnp.zeros((1, D_MODEL), F32)
    slab_a = jnp.concatenate(
        [d_sh_m, d_sc_m, d_gt_m, d_sh_f, d_sc_f, d_gt_f, d_g_pre_mix, d_g_post_mix, d_g_pre_ffn, d_g_post_ffn,
         jnp.concatenate([d_b_pool, d_scale, loss_v, jnp.zeros((1, 384), F32)], axis=1)] + [z1] * 5, axis=0)
    slab_b = jnp.concatenate([d_conv_w, d_conv_b, jnp.zeros((4, D_FF), F32)], axis=0)
    d_wpool = jnp.concatenate([d_wbd[gi * 64:(gi + 1) * 64, gi * 64:(gi + 1) * 64] for gi in range(4)], axis=0)
    dev = _dev_id(xi, yi, ci)
    small_src = [slab_a, slab_b, d_wpool]
    small_land = [lax.dynamic_update_slice(lax.empty((N_DEV,) + t.shape, F32), t[None], (dev, 0, 0)) for t in small_src]
    tok = jnp.zeros((8, 128), F32)
    gs_s, gs_r, gs_bufs, tok = split_start("small_ici_start", small_src + small_land, _small_ici_plan(3), 12, tok)
    rs_mix = GradReduce("mix", [dw_in, dw_out], [256, 256], place)
    tok = rs_mix.d2d_start(tok)
    tok = rs_ffn.final_and_share_start(tok, tok)
    gs_bufs = split_wait("small_ici_wait", gs_s, gs_r, gs_bufs, _small_ici_plan(3), tok)
    gt_s, gt_r, small_land, tok = split_start("small_d2d_start", gs_bufs[3:], _small_d2d_plan(3), 9, tok)
    tok = rs_mix.add_and_ici_start(tok, tok)
    slab_a_g, slab_b_g, wpool_g = split_wait("small_d2d_wait", gt_s, gt_r, small_land, _small_d2d_plan(3), tok)
    cw_cols = conv_w.shape[2]
    convw_g = lax.dynamic_slice(slab_b_g, (0, 0, chip * cw_cols), (N_DEV, 3, cw_cols))
    dmod_cols = lax.dynamic_slice(slab_a_g[:, :6, :].reshape(N_DEV, 6 * D_MODEL), (0, chip * cb_ada), (N_DEV, cb_ada))

    res = {}

    def big_adamw(name, w, g, m, v, tr):
        g_, d_, m_, v_ = adamw_rows(w[0], g, m[0], v[0], tr, "adamw_" + name)
        res[name] = (g_[None], d_[None], m_[None], v_[None])
        return v_

    g_ada, d_ada, m_ada, v_ada = adamw_ada(c_all.reshape(N_DEV, D_MODEL).T, dmod_cols, w_ada[0], m_w_ada[0], v_w_ada[0])
    res["w_ada"] = (g_ada[None], d_ada[None], m_ada[None], v_ada[None])
    g_w_up, g_w_down = rs_ffn.finish(v_ada)
    big_adamw("w_up", w_up, g_w_up, m_w_up, v_w_up, 256)
    last = big_adamw("w_down", w_down, g_w_down, m_w_down, v_w_down, 352)
    rs_mix.final_and_share_start(last, jnp.zeros((8, 128), F32))
    g_w_in, g_w_out = rs_mix.finish(last)
    big_adamw("w_in", w_in, g_w_in, m_w_in, v_w_in, 256)
    big_adamw("w_out", w_out, g_w_out, m_w_out, v_w_out, 256)
    small, loss_sum = adamw_small(slab_a_g, slab_b_g, convw_g, wpool_g, {
        "b_ada": (b_ada, m_b_ada, v_b_ada), "g_pre_mix": (g_pre_mix, m_g_pre_mix, v_g_pre_mix),
        "g_post_mix": (g_post_mix, m_g_post_mix, v_g_post_mix), "g_pre_ffn": (g_pre_ffn, m_g_pre_ffn, v_g_pre_ffn),
        "g_post_ffn": (g_post_ffn, m_g_post_ffn, v_g_post_ffn), "b_pool": (b_pool, m_b_pool, v_b_pool),
        "pool_scale": (pool_scale, m_pool_scale, v_pool_scale), "conv_b": (conv_b, m_conv_b, v_conv_b),
        "conv_w": (conv_w, m_conv_w, v_conv_w), "w_pool": (w_pool, m_w_pool, v_w_pool)})
    for name in ("b_ada", "g_pre_mix", "g_post_mix", "g_pre_ffn", "g_post_ffn", "pool_scale", "conv_b", "b_pool", "w_pool", "conv_w"):
        res[name] = tuple(small[name])

    loss = loss_sum[0, 0]
    order = ["w_ada", "b_ada", "g_pre_mix", "g_post_mix", "g_pre_ffn", "g_post_ffn", "w_in", "w_pool", "b_pool", "pool_scale",
             "w_out", "w_up", "conv_w", "conv_b", "w_down"]
    outs = [loss, grad_x[None]]
    for k in range(4):
        outs += [res[n][k] for n in order]
    return tuple(outs)
```

```python
import math

import jax
import jax.numpy as jnp
from jax import lax
from jax.experimental import pallas as pl
from jax.experimental.pallas import tpu as pltpu

F32 = jnp.float32
BF16 = jnp.bfloat16
MESH = pl.DeviceIdType.MESH

D_MODEL = 1024
HEAD_DIM = 64
POOL_W = 256
GROUP_W = 256
DILATIONS = (1, 4, 16)
ATT_BLOCK = 128
IN_W = 2560
D_FF = 2816
HALF_FF = 1408
ROT_DIM = 16
ROPE_THETA = 500000.0
NORM_EPS = 1e-6
N_CHIPS = 4
N_DEV = 8
NEG = -1e30

ADAM_LR = 0.001
ADAM_B1 = 0.9
ADAM_B2 = 0.999
ADAM_EPS = 1e-08
ADAM_WD = 0.01
ADAM_STEP = 10

VMEM_LIMIT = 56 * 1024 * 1024

NT = (((1,), (1,)), ((), ()))
TN = (((0,), (0,)), ((), ()))


def _params(n_grid=0, **kw):
    sem = ("arbitrary",) * n_grid if n_grid else None
    return pltpu.CompilerParams(dimension_semantics=sem, vmem_limit_bytes=VMEM_LIMIT, **kw)


def _full(shape):
    nd = len(shape)
    return pl.BlockSpec(tuple(shape), lambda *_: (0,) * nd, pipeline_mode=pl.Buffered(1))


def _rows(tm, ncol):
    return pl.BlockSpec((tm, ncol), lambda i: (i, 0))


def _acc(ref, val):
    @pl.when(pl.program_id(0) == 0)
    def _():
        ref[...] = jnp.zeros_like(ref)

    ref[...] += val


def _colsum(v):
    return jnp.sum(v, axis=0, keepdims=True)


def _rope128(t, cs, sa, sb, sign):
    return t * cs + sign * (pltpu.roll(t, 8, 1) * sa + pltpu.roll(t, 120, 1) * sb)


FF_CHUNKS = tuple((ch, off, w) for ch in range(2) for off, w in ((0, 512), (512, 512), (1024, 384)))
GELU_C0 = math.sqrt(2.0 / math.pi)
GELU_C1 = GELU_C0 * 0.044715


def _gelu(z):
    z2 = z * z
    t = jnp.tanh(z * (GELU_C0 + GELU_C1 * z2))
    u = 0.5 * t + 0.5
    return z * u, u, t, z2


def _gelu_grad(z, u, t, z2):
    return u + (z * (GELU_C0 + (3.0 * GELU_C1) * z2)) * (0.5 - 0.5 * (t * t))


def _conv_taps(gate, halo, first):
    row = lax.broadcasted_iota(jnp.int32, gate.shape, 0)
    halo = jnp.where(first, 0.0, halo)
    nh = halo.shape[0]
    p1 = halo[nh - 1:nh, :]
    p2 = halo[nh - 2:nh - 1, :]
    g1 = jnp.where(row == 0, p1, pltpu.roll(gate, 1, 0))
    g2 = jnp.where(row == 0, p2, jnp.where(row == 1, p1, pltpu.roll(gate, 2, 0)))
    return g1, g2


def inproj_fwd(x, g, mod6, w_in_g, tc, tsa, tsb, tm=512):
    S = x.shape[0]

    def body(x_ref, g_ref, mod_ref, w_ref, tc_ref, tsa_ref, tsb_ref, h_ref, u_ref, q1_ref, q4_ref, q16_ref, scr):
        qkv_refs = (q1_ref, q4_ref, q16_ref)
        xv = x_ref[...]
        rstd = lax.rsqrt(jnp.mean(xv * xv, axis=-1, keepdims=True) + NORM_EPS)
        h = ((xv * rstd) * g_ref[...]) * (1.0 + mod_ref[1:2, :]) + mod_ref[0:1, :]
        hb = h.astype(BF16)
        h_ref[...] = hb
        cs, sa, sb = tc_ref[...], tsa_ref[...], tsb_ref[...]
        for j in range(N_CHIPS):
            res = jnp.dot(hb, w_ref[j], preferred_element_type=F32)
            for t in range(5):
                sp = 5 * j + t
                piece, half = sp // 2, sp % 2
                blk = res[:, t * 128:(t + 1) * 128]
                lanes = slice(half * 128, (half + 1) * 128)
                if piece == 0:
                    u_ref[:, lanes] = blk
                else:
                    kind, gi = (piece - 1) // 3, (piece - 1) % 3
                    if kind == 0:
                        blk = _rope128(blk, cs, sa, sb, 1.0) * (HEAD_DIM ** -0.5)
                    elif kind == 1:
                        blk = _rope128(blk, cs, sa, sb, 1.0)
                    d = DILATIONS[gi]
                    if d == 1:
                        q1_ref[kind, 0, :, lanes] = blk.astype(BF16)
                    else:
                        scr[...] = blk
                        for r in range(d):
                            qkv_refs[gi][kind, r, :, lanes] = scr[pl.ds(r, tm // d, stride=d), :].astype(BF16)

    cls = lambda d: pl.BlockSpec((3, d, tm // d, GROUP_W), lambda i: (0, 0, i, 0))
    return pl.pallas_call(
        body,
        name="inproj_fwd",
        grid=(S // tm,),
        in_specs=[_rows(tm, D_MODEL), _full((1, D_MODEL)), _full((6, D_MODEL)), _full(w_in_g.shape),
                  _rows(tm, 128), _rows(tm, 128), _rows(tm, 128)],
        out_specs=[_rows(tm, D_MODEL), _rows(tm, POOL_W)] + [cls(d) for d in DILATIONS],
        out_shape=[jax.ShapeDtypeStruct((S, D_MODEL), BF16), jax.ShapeDtypeStruct((S, POOL_W), F32)]
        + [jax.ShapeDtypeStruct((3, d, S // d, GROUP_W), BF16) for d in DILATIONS],
        scratch_shapes=[pltpu.VMEM((tm, 128), F32)],
        compiler_params=_params(1),
    )(x, g, mod6, w_in_g, tc, tsa, tsb)


def _attn_masks():
    row = lax.broadcasted_iota(jnp.int32, (2 * ATT_BLOCK, 2 * ATT_BLOCK), 0) % ATT_BLOCK
    col = lax.broadcasted_iota(jnp.int32, (2 * ATT_BLOCK, 2 * ATT_BLOCK), 1)
    band = (col >= row) & (col <= row + ATT_BLOCK)
    lane = lax.broadcasted_iota(jnp.int32, (ATT_BLOCK, 128), 1)
    return band, col, lane < HEAD_DIM


def _classes_per_step(d, nb):
    return min(d, max(1, 8 // nb))


def _stack_heads(t, lo):
    z = jnp.zeros_like(t)
    return jnp.concatenate([jnp.where(lo, t, z), jnp.where(lo, z, t)], axis=0)


def _unstack_heads(t2, lo):
    return jnp.where(lo, t2[:ATT_BLOCK], t2[ATT_BLOCK:])


def attn_fwd(qkv, d):
    L = qkv.shape[2]
    nb = L // ATT_BLOCK
    cpb = _classes_per_step(d, nb)

    def body(q_ref, k_ref, v_ref, o_ref, l_ref, kpad, vpad):
        for cls in range(cpb):
            kpad[cls, 0:ATT_BLOCK, :] = jnp.zeros((ATT_BLOCK, GROUP_W), BF16)
            vpad[cls, 0:ATT_BLOCK, :] = jnp.zeros((ATT_BLOCK, GROUP_W), BF16)
            kpad[cls, ATT_BLOCK:, :] = k_ref[cls]
            vpad[cls, ATT_BLOCK:, :] = v_ref[cls]
        band, col, lo = _attn_masks()

        def step(t, carry):
            cls, n = t // nb, t % nb
            r0 = pl.multiple_of(n * ATT_BLOCK, ATT_BLOCK)
            valid = band & ((col >= ATT_BLOCK) | (n > 0))
            qb = q_ref[cls, pl.ds(r0, ATT_BLOCK), :]
            kb = kpad[cls, pl.ds(r0, 2 * ATT_BLOCK), :]
            vb = vpad[cls, pl.ds(r0, 2 * ATT_BLOCK), :]
            for pair in range(2):
                lanes = slice(pair * 128, (pair + 1) * 128)
                qp, kp, vp = qb[:, lanes], kb[:, lanes], vb[:, lanes]
                s = lax.dot_general(_stack_heads(qp, lo), kp, NT, preferred_element_type=F32)
                s = jnp.where(valid, s, NEG)
                m = jnp.max(s, axis=1, keepdims=True)
                p = jnp.exp(s - m)
                den = jnp.sum(p, axis=1, keepdims=True)
                pv = jnp.dot(p.astype(BF16), vp, preferred_element_type=F32)
                o_ref[cls, pl.ds(r0, ATT_BLOCK), lanes] = _unstack_heads(pv / den, lo)
                l_ref[cls, pl.ds(r0, ATT_BLOCK), lanes] = _unstack_heads(jnp.broadcast_to(m + jnp.log(den), pv.shape), lo)
            return carry

        lax.fori_loop(0, cpb * nb, step, 0, unroll=4)

    spec = lambda kind: pl.BlockSpec((None, cpb, L, GROUP_W), lambda r: (kind, r, 0, 0))
    return pl.pallas_call(
        body,
        name=f"attn_fwd_d{d}",
        grid=(d // cpb,),
        in_specs=[spec(0), spec(1), spec(2)],
        out_specs=[pl.BlockSpec((cpb, L, GROUP_W), lambda r: (r, 0, 0))] * 2,
        out_shape=[jax.ShapeDtypeStruct((d, L, GROUP_W), F32)] * 2,
        scratch_shapes=[pltpu.VMEM((cpb, L + ATT_BLOCK, GROUP_W), BF16)] * 2,
        compiler_params=_params(1),
    )(qkv, qkv, qkv)


def _pool_lane_windows(shape):
    lane = lax.broadcasted_iota(jnp.int32, shape, 1)
    return lane, jnp.where(lane < 64, 2, jnp.where(lane < 128, 4, jnp.where(lane < 192, 8, 16)))


def pool_fwd(u, wbd, b, scale):
    S = u.shape[0]

    def body(u_ref, w_ref, b_ref, s_ref, mixed_ref, out_ref):
        uv = u_ref[...]
        row = lax.broadcasted_iota(jnp.int32, uv.shape, 0)
        lane, win = _pool_lane_windows(uv.shape)

        def shift(a, k):
            return jnp.where(row >= k, pltpu.roll(a, k, 0), 0.0)

        s2 = uv + shift(uv, 1)
        s4 = s2 + shift(s2, 2)
        s8 = s4 + shift(s4, 4)
        s16 = s8 + shift(s8, 8)
        tsum = jnp.where(lane < 64, s2, jnp.where(lane < 128, s4, jnp.where(lane < 192, s8, s16)))
        cnt = jnp.minimum(row + 1, win).astype(F32)
        mb = (tsum / cnt - uv).astype(BF16)
        mixed_ref[...] = mb
        y = jnp.dot(mb, w_ref[...], preferred_element_type=F32) + b_ref[...]
        out_ref[...] = (y * s_ref[...]).astype(BF16)

    vm = pl.BlockSpec(memory_space=pltpu.VMEM)
    return pl.pallas_call(
        body,
        name="pool_fwd",
        in_specs=[vm] * 4,
        out_specs=[vm] * 2,
        out_shape=[jax.ShapeDtypeStruct((S, POOL_W), BF16)] * 2,
        compiler_params=_params(),
    )(u, wbd, b, scale)


def outproj_fwd(o_l, pool, x, w_out_g, g_post, g_pre, mod6, tm=512):
    S = x.shape[0]

    def body(o0, o1, o2, l0, l1, l2, pool_ref, x_ref, w_ref, gpost_ref, gpre_ref, mod_ref,
             cat_ref, lse_ref, lse4_ref, lse16_ref, y1_ref, x1_ref, h2_ref, so4, sl4, so16, sl16):
        for d, src, dst in ((4, o1, so4), (4, l1, sl4), (16, o2, so16), (16, l2, sl16)):
            for r in range(d):
                for h in range(2):
                    dst[h, pl.ds(r, tm // d, stride=d), :] = src[r, :, h * 128:(h + 1) * 128]
        nat = lambda ref: jnp.concatenate([ref[0], ref[1]], axis=1)
        a, b, c = l0[0], nat(sl4), nat(sl16)
        m = jnp.maximum(jnp.maximum(a, b), c)
        e0, e1, e2 = jnp.exp(a - m), jnp.exp(b - m), jnp.exp(c - m)
        z = e0 + e1 + e2
        lse = m + jnp.log(z)
        lse_ref[...] = lse
        for h in range(2):
            sl4[h] = lse[:, h * 128:(h + 1) * 128]
        for d, dst in ((4, lse4_ref), (16, lse16_ref)):
            for r in range(d):
                for h in range(2):
                    dst[r, :, h * 128:(h + 1) * 128] = sl4[h, pl.ds(r, tm // d, stride=d), :]
        attn = (e0 * o0[0] + e1 * nat(so4) + e2 * nat(so16)) / z
        cat = jnp.concatenate([pool_ref[...], attn.astype(BF16)], axis=1)
        cat_ref[...] = cat
        y1 = jnp.concatenate([jnp.dot(cat, w_ref[j], preferred_element_type=F32) for j in range(N_CHIPS)], axis=1)
        y1_ref[...] = y1
        rstd = lax.rsqrt(jnp.mean(y1 * y1, axis=-1, keepdims=True) + NORM_EPS)
        x1 = x_ref[...] + mod_ref[2:3, :] * ((y1 * rstd) * gpost_ref[...])
        x1_ref[...] = x1
        rstd2 = lax.rsqrt(jnp.mean(x1 * x1, axis=-1, keepdims=True) + NORM_EPS)
        h2 = ((x1 * rstd2) * gpre_ref[...]) * (1.0 + mod_ref[4:5, :]) + mod_ref[3:4, :]
        h2_ref[...] = h2.astype(BF16)

    t256 = _rows(tm, GROUP_W)
    cls = lambda d: pl.BlockSpec((d, tm // d, GROUP_W), lambda i: (0, i, 0))
    cls_shape = lambda d: jax.ShapeDtypeStruct((d, S // d, GROUP_W), F32)
    return pl.pallas_call(
        body,
        name="outproj_fwd",
        grid=(S // tm,),
        in_specs=[cls(d) for d in DILATIONS] * 2 + [t256, _rows(tm, D_MODEL), _full(w_out_g.shape), _full((1, D_MODEL)),
                                                    _full((1, D_MODEL)), _full((6, D_MODEL))],
        out_specs=[_rows(tm, 512), t256, cls(4), cls(16), _rows(tm, D_MODEL), _rows(tm, D_MODEL), _rows(tm, D_MODEL)],
        out_shape=[jax.ShapeDtypeStruct((S, 512), BF16), jax.ShapeDtypeStruct((S, GROUP_W), F32), cls_shape(4), cls_shape(16),
                   jax.ShapeDtypeStruct((S, D_MODEL), F32), jax.ShapeDtypeStruct((S, D_MODEL), F32),
                   jax.ShapeDtypeStruct((S, D_MODEL), BF16)],
        scratch_shapes=[pltpu.VMEM((2, tm, 128), F32)] * 4,
        compiler_params=_params(1),
    )(*o_l, pool, x, w_out_g, g_post, g_pre, mod6)


def _halo_prev(tm, ncol):
    return pl.BlockSpec((16, ncol), lambda i: (jnp.maximum(i * (tm // 16) - 1, 0), 0))


def ffn_fwd(h2, w_up_g, conv_w, conv_b, w_down, x1, target, g_post, mod6, tm=256):
    S = x1.shape[0]

    def body(h_ref, wu_ref, cw_ref, cb_ref, wd_ref, x1_ref, tgt_ref, g_ref, mod_ref,
             gate_ref, val_ref, a_ref, dy2_ref, dout_ref, loss_ref, dgt_ref, dg_ref, carry):
        first = pl.program_id(0) == 0

        @pl.when(first)
        def _():
            carry[...] = jnp.zeros_like(carry)

        hb = h_ref[...]
        y2 = jnp.zeros((tm, D_MODEL), F32)
        for ch in range(2):
            cols = slice(ch * HALF_FF, (ch + 1) * HALF_FF)
            gb = jnp.dot(hb, wu_ref[ch], preferred_element_type=F32).astype(BF16)
            vb = jnp.dot(hb, wu_ref[2 + ch], preferred_element_type=F32).astype(BF16)
            gate_ref[:, cols] = gb
            val_ref[:, cols] = vb
            gt = gb.astype(F32)
            g1, g2 = _conv_taps(gt, carry[:, cols], first)
            carry[:, cols] = gt[tm - 8:, :]
            gc = g2 * cw_ref[0:1, cols] + g1 * cw_ref[1:2, cols] + gt * cw_ref[2:3, cols] + cb_ref[:, cols]
            ab = _gelu(gc.astype(BF16))[0] * vb
            a_ref[:, cols] = ab
            y2 = y2 + jnp.dot(ab, wd_ref[cols, :], preferred_element_type=F32)
        rstd = lax.rsqrt(jnp.mean(y2 * y2, axis=-1, keepdims=True) + NORM_EPS)
        y2n = y2 * rstd
        gv = g_ref[...]
        gtf = mod_ref[5:6, :]
        r2 = y2n * gv
        diff = (x1_ref[...] + gtf * r2) - tgt_ref[...]
        _acc(loss_ref, jnp.zeros((1, 128), F32) + 0.5 * jnp.sum(diff * diff) * (1.0 / D_MODEL))
        dout = diff * (1.0 / D_MODEL)
        dout_ref[...] = dout
        _acc(dgt_ref, _colsum(dout * r2))
        dr2 = dout * gtf
        _acc(dg_ref, _colsum(dr2 * y2n))
        dyn = dr2 * gv
        dy2 = rstd * (dyn - y2n * jnp.mean(dyn * y2n, axis=-1, keepdims=True))
        dy2_ref[...] = dy2.astype(BF16)

    vec = _full((1, D_MODEL))
    return pl.pallas_call(
        body,
        name="ffn_fwd",
        grid=(S // tm,),
        in_specs=[_rows(tm, D_MODEL), _full(w_up_g.shape), _full((3, D_FF)), _full((1, D_FF)), _full((D_FF, D_MODEL)),
                  _rows(tm, D_MODEL), _rows(tm, D_MODEL), vec, _full((6, D_MODEL))],
        out_specs=[_rows(tm, D_FF), _rows(tm, D_FF), _rows(tm, D_FF), _rows(tm, D_MODEL), _rows(tm, D_MODEL), _full((1, 128)), vec, vec],
        out_shape=[jax.ShapeDtypeStruct((S, D_FF), BF16)] * 3 + [jax.ShapeDtypeStruct((S, D_MODEL), BF16),
                                                                 jax.ShapeDtypeStruct((S, D_MODEL), F32),
                                                                 jax.ShapeDtypeStruct((1, 128), F32),
                                                                 jax.ShapeDtypeStruct((1, D_MODEL), F32),
                                                                 jax.ShapeDtypeStruct((1, D_MODEL), F32)],
        scratch_shapes=[pltpu.VMEM((8, D_FF), F32)],
        compiler_params=_params(1),
    )(h2, w_up_g, conv_w, conv_b, w_down, x1, target, g_post, mod6)


def down_bwd(dy2, w_down, gate, val, conv_w, conv_b, a, h2, tm=256):
    S = dy2.shape[0]

    def body(dy_ref, w_ref, gate_ref, halo_ref, val_ref, cw_ref, cb_ref, a_ref, h_ref,
             dgc_ref, dval_ref, dcw_ref, dcb_ref, dwd_ref, dwu_ref):
        first = pl.program_id(0) == 0

        @pl.when(first)
        def _():
            dcw_ref[...] = jnp.zeros_like(dcw_ref)
            dcb_ref[...] = jnp.zeros_like(dcb_ref)
            dwd_ref[...] = jnp.zeros_like(dwd_ref)
            dwu_ref[...] = jnp.zeros_like(dwu_ref)

        dyb = dy_ref[...]
        hb = h_ref[...]
        def col(i):
            ch, off, width = FF_CHUNKS[i]
            return slice(ch * HALF_FF + off, ch * HALF_FF + off + width)

        def mm_da(i):
            return lax.dot_general(dyb, w_ref[col(i), :], NT, preferred_element_type=F32)

        def elementwise(i, da):
            cols = col(i)
            gt = gate_ref[:, cols].astype(F32)
            g1, g2 = _conv_taps(gt, halo_ref[:, cols].astype(F32), first)
            gc = g2 * cw_ref[0:1, cols] + g1 * cw_ref[1:2, cols] + gt * cw_ref[2:3, cols] + cb_ref[:, cols]
            zb, dab = gc.astype(BF16), da.astype(BF16)
            ge, u, th, z2 = _gelu(zb)
            dgb = dab * val_ref[:, cols] * _gelu_grad(zb, u, th, z2)
            dgc_ref[:, cols] = dgb
            dgc = dgb.astype(F32)
            dvb = dab * ge
            dval_ref[:, cols] = dvb
            dcb_ref[:, cols] += _colsum(dgc)
            dcw_ref[0:1, cols] += _colsum(dgc * g2)
            dcw_ref[1:2, cols] += _colsum(dgc * g1)
            dcw_ref[2:3, cols] += _colsum(dgc * gt)
            return dvb

        def mm_dw(i, dvb):
            ch, off, width = FF_CHUNKS[i]
            dwd_ref[col(i), :] += lax.dot_general(a_ref[:, col(i)], dyb, TN, preferred_element_type=F32)
            dwu_ref[ch, :, off:off + width] += lax.dot_general(hb, dvb, TN, preferred_element_type=F32)

        n = len(FF_CHUNKS)
        da = mm_da(0)
        prev = None
        for i in range(n):
            nxt = mm_da(i + 1) if i + 1 < n else None
            if prev is not None:
                mm_dw(i - 1, prev)
            prev = elementwise(i, da)
            da = nxt
        mm_dw(n - 1, prev)

    return pl.pallas_call(
        body,
        name="down_bwd",
        grid=(S // tm,),
        in_specs=[_rows(tm, D_MODEL), _full((D_FF, D_MODEL)), _rows(tm, D_FF), _halo_prev(tm, D_FF), _rows(tm, D_FF),
                  _full((3, D_FF)), _full((1, D_FF)), _rows(tm, D_FF), _rows(tm, D_MODEL)],
        out_specs=[_rows(tm, D_FF), _rows(tm, D_FF), _full((3, D_FF)), _full((1, D_FF)), _full((D_FF, D_MODEL)),
                   pl.BlockSpec((2, D_MODEL, HALF_FF), lambda i: (1, 0, 0), pipeline_mode=pl.Buffered(1))],
        out_shape=[jax.ShapeDtypeStruct((S, D_FF), BF16), jax.ShapeDtypeStruct((S, D_FF), BF16),
                   jax.ShapeDtypeStruct((3, D_FF), F32), jax.ShapeDtypeStruct((1, D_FF), F32),
                   jax.ShapeDtypeStruct((D_FF, D_MODEL), F32), jax.ShapeDtypeStruct((N_CHIPS, D_MODEL, HALF_FF), F32)],
        compiler_params=_params(1),
    )(dy2, w_down, gate, gate, val, conv_w, conv_b, a, h2)


def up_bwd(dgc, dval, conv_w, w_up_g, x1, dout, y1, g_pre, g_post, mod6, h2, dw_up, tm=256):
    S = x1.shape[0]
    last_blk = S // 16 - 1

    def body(dgc_ref, nxt_ref, dval_ref, cw_ref, w_ref, x1_ref, dout_ref, y1_ref, gpre_ref, gpost_ref, mod_ref, h_ref, dwin_ref,
             dx1_ref, dy1_ref, dsh_ref, dsc_ref, dgpre_ref, dgt_ref, dgpost_ref, dwu_ref):
        last = pl.program_id(0) == pl.num_programs(0) - 1

        @pl.when(pl.program_id(0) == 0)
        def _():
            dwu_ref[...] = jnp.zeros_like(dwu_ref)

        hb = h_ref[...]
        dh = jnp.zeros((tm, D_MODEL), F32)
        for ch in range(2):
            cols = slice(ch * HALF_FF, (ch + 1) * HALF_FF)
            dg = dgc_ref[:, cols].astype(F32)
            nx = jnp.where(last, 0.0, nxt_ref[:, cols].astype(F32))
            row = lax.broadcasted_iota(jnp.int32, dg.shape, 0)
            n0, n1 = nx[0:1, :], nx[1:2, :]
            u1 = jnp.where(row == tm - 1, n0, pltpu.roll(dg, tm - 1, 0))
            u2 = jnp.where(row == tm - 1, n1, jnp.where(row == tm - 2, n0, pltpu.roll(dg, tm - 2, 0)))
            dgate = (dg * cw_ref[2:3, cols] + u1 * cw_ref[1:2, cols] + u2 * cw_ref[0:1, cols]).astype(BF16)
            dwu_ref[ch] += lax.dot_general(hb, dgate, TN, preferred_element_type=F32)
            dh = dh + lax.dot_general(dgate, w_ref[ch], NT, preferred_element_type=F32)
            dh = dh + lax.dot_general(dval_ref[:, cols], w_ref[2 + ch], NT, preferred_element_type=F32)
        x1 = x1_ref[...]
        rstd = lax.rsqrt(jnp.mean(x1 * x1, axis=-1, keepdims=True) + NORM_EPS)
        n2 = x1 * rstd
        gpre = gpre_ref[...]
        one_sc = 1.0 + mod_ref[4:5, :]
        _acc(dsh_ref, _colsum(dh))
        _acc(dsc_ref, _colsum(dh * (n2 * gpre)))
        _acc(dgpre_ref, _colsum(dh * one_sc * n2))
        dn = dh * (gpre * one_sc)
        dx1 = dout_ref[...] + rstd * (dn - n2 * jnp.mean(dn * n2, axis=-1, keepdims=True))
        dx1_ref[...] = dx1
        y1 = y1_ref[...]
        rstd1 = lax.rsqrt(jnp.mean(y1 * y1, axis=-1, keepdims=True) + NORM_EPS)
        y1n = y1 * rstd1
        gpost = gpost_ref[...]
        gtm = mod_ref[2:3, :]
        _acc(dgt_ref, _colsum(dx1 * (y1n * gpost)))
        dr1 = dx1 * gtm
        _acc(dgpost_ref, _colsum(dr1 * y1n))
        dyn = dr1 * gpost
        dy1 = rstd1 * (dyn - y1n * jnp.mean(dyn * y1n, axis=-1, keepdims=True))
        dy1_ref[...] = dy1.astype(BF16)

    vec = _full((1, D_MODEL))
    nxt = pl.BlockSpec((16, D_FF), lambda i: (jnp.minimum((i + 1) * (tm // 16), last_blk), 0))
    return pl.pallas_call(
        body,
        name="up_bwd",
        grid=(S // tm,),
        in_specs=[_rows(tm, D_FF), nxt, _rows(tm, D_FF), _full((3, D_FF)), _full(w_up_g.shape), _rows(tm, D_MODEL),
                  _rows(tm, D_MODEL), _rows(tm, D_MODEL), vec, vec, _full((6, D_MODEL)), _rows(tm, D_MODEL),
                  pl.BlockSpec(memory_space=pl.ANY)],
        out_specs=[_rows(tm, D_MODEL), _rows(tm, D_MODEL), vec, vec, vec, vec, vec,
                   pl.BlockSpec((2, D_MODEL, HALF_FF), lambda i: (0, 0, 0), pipeline_mode=pl.Buffered(1))],
        out_shape=[jax.ShapeDtypeStruct((S, D_MODEL), F32), jax.ShapeDtypeStruct((S, D_MODEL), BF16)]
        + [jax.ShapeDtypeStruct((1, D_MODEL), F32)] * 5 + [jax.ShapeDtypeStruct(dw_up.shape, F32)],
        input_output_aliases={12: 7},
        compiler_params=_params(1),
    )(dgc, dgc, dval, conv_w, w_up_g, x1, dout, y1, g_pre, g_post, mod6, h2, dw_up)


def outproj_bwd(dy1, w_out_g, cat, tm=512):
    S = dy1.shape[0]

    def body(dy_ref, w_ref, cat_ref, dpool_ref, dattn_ref, da4_ref, da16_ref, delta_ref, dl4_ref, dl16_ref, dw_ref, scr):
        @pl.when(pl.program_id(0) == 0)
        def _():
            dw_ref[...] = jnp.zeros_like(dw_ref)

        catb = cat_ref[...]
        dcat = jnp.zeros((tm, 512), F32)
        for j in range(N_CHIPS):
            dyj = dy_ref[:, j * 256:(j + 1) * 256]
            dcat = dcat + lax.dot_general(dyj, w_ref[j], NT, preferred_element_type=F32)
            dw_ref[j] += lax.dot_general(catb, dyj, TN, preferred_element_type=F32)
        dpool_ref[...] = dcat[:, :POOL_W]
        dattn = dcat[:, POOL_W:]
        dattn_ref[...] = dattn.astype(BF16)
        for h in range(2):
            scr[h] = dattn[:, h * 128:(h + 1) * 128]
        for d, dst in ((4, da4_ref), (16, da16_ref)):
            for r in range(d):
                for h in range(2):
                    dst[r, :, h * 128:(h + 1) * 128] = scr[h, pl.ds(r, tm // d, stride=d), :].astype(BF16)
        prod = dattn * catb[:, POOL_W:].astype(F32)
        r = lax.broadcasted_iota(jnp.int32, (GROUP_W, GROUP_W), 0) // HEAD_DIM
        c = lax.broadcasted_iota(jnp.int32, (GROUP_W, GROUP_W), 1) // HEAD_DIM
        ones_bd = jnp.where(r == c, 1.0, 0.0).astype(BF16)
        hi = prod.astype(BF16)
        lo = (prod - hi.astype(F32)).astype(BF16)
        delta = jnp.dot(hi, ones_bd, preferred_element_type=F32) + jnp.dot(lo, ones_bd, preferred_element_type=F32)
        delta_ref[...] = delta
        for h in range(2):
            scr[h] = delta[:, h * 128:(h + 1) * 128]
        for d, dst in ((4, dl4_ref), (16, dl16_ref)):
            for r in range(d):
                for h in range(2):
                    dst[r, :, h * 128:(h + 1) * 128] = scr[h, pl.ds(r, tm // d, stride=d), :]

    cls = lambda d: pl.BlockSpec((d, tm // d, GROUP_W), lambda i: (0, i, 0))
    cls_shape = lambda d, dt: jax.ShapeDtypeStruct((d, S // d, GROUP_W), dt)
    return pl.pallas_call(
        body,
        name="outproj_bwd",
        grid=(S // tm,),
        in_specs=[_rows(tm, D_MODEL), _full(w_out_g.shape), _rows(tm, 512)],
        out_specs=[_rows(tm, POOL_W), _rows(tm, GROUP_W), cls(4), cls(16), _rows(tm, GROUP_W), cls(4), cls(16),
                   _full(w_out_g.shape)],
        out_shape=[jax.ShapeDtypeStruct((S, POOL_W), F32), jax.ShapeDtypeStruct((S, GROUP_W), BF16), cls_shape(4, BF16),
                   cls_shape(16, BF16), jax.ShapeDtypeStruct((S, GROUP_W), F32), cls_shape(4, F32), cls_shape(16, F32),
                   jax.ShapeDtypeStruct(w_out_g.shape, F32)],
        scratch_shapes=[pltpu.VMEM((2, tm, 128), F32)],
        compiler_params=_params(1),
    )(dy1, w_out_g, cat)


def attn_bwd(qkv, dattn, lse, delta, d):
    L = qkv.shape[2]
    nb = L // ATT_BLOCK
    cpb = _classes_per_step(d, nb)

    def body(q_ref, k_ref, v_ref, do_ref, l_ref, dl_ref, out_ref, kpad, vpad, dkpad, dvpad):
        for cls in range(cpb):
            kpad[cls, 0:ATT_BLOCK, :] = jnp.zeros((ATT_BLOCK, GROUP_W), BF16)
            vpad[cls, 0:ATT_BLOCK, :] = jnp.zeros((ATT_BLOCK, GROUP_W), BF16)
            kpad[cls, ATT_BLOCK:, :] = k_ref[cls]
            vpad[cls, ATT_BLOCK:, :] = v_ref[cls]
        dkpad[...] = jnp.zeros_like(dkpad)
        dvpad[...] = jnp.zeros_like(dvpad)
        band, col, lo = _attn_masks()

        def step(t, carry):
            cls, n = t // nb, t % nb
            r0 = pl.multiple_of(n * ATT_BLOCK, ATT_BLOCK)
            valid = band & ((col >= ATT_BLOCK) | (n > 0))
            qb = q_ref[cls, pl.ds(r0, ATT_BLOCK), :]
            dob = do_ref[cls, pl.ds(r0, ATT_BLOCK), :]
            lb = l_ref[cls, pl.ds(r0, ATT_BLOCK), :]
            dlb = dl_ref[cls, pl.ds(r0, ATT_BLOCK), :]
            kb = kpad[cls, pl.ds(r0, 2 * ATT_BLOCK), :]
            vb = vpad[cls, pl.ds(r0, 2 * ATT_BLOCK), :]
            for pair in range(2):
                lanes = slice(pair * 128, (pair + 1) * 128)
                qp, dop, kp, vp = qb[:, lanes], dob[:, lanes], kb[:, lanes], vb[:, lanes]
                c0, c1 = pair * 128, pair * 128 + HEAD_DIM
                q2, do2 = _stack_heads(qp, lo), _stack_heads(dop, lo)
                lse2 = jnp.concatenate([lb[:, c0:c0 + 1], lb[:, c1:c1 + 1]], axis=0)
                dl2 = jnp.concatenate([dlb[:, c0:c0 + 1], dlb[:, c1:c1 + 1]], axis=0)
                s = lax.dot_general(q2, kp, NT, preferred_element_type=F32)
                s = jnp.where(valid, s, NEG)
                p = jnp.exp(s - lse2)
                dp = lax.dot_general(do2, vp, NT, preferred_element_type=F32)
                ds = (p * (dp - dl2)).astype(BF16)
                dq2 = jnp.dot(ds, kp, preferred_element_type=F32)
                out_ref[0, cls, pl.ds(r0, ATT_BLOCK), lanes] = _unstack_heads(dq2, lo)
                dkpad[cls, pl.ds(r0, 2 * ATT_BLOCK), lanes] += lax.dot_general(ds, q2, TN, preferred_element_type=F32)
                dvpad[cls, pl.ds(r0, 2 * ATT_BLOCK), lanes] += lax.dot_general(p.astype(BF16), do2, TN, preferred_element_type=F32)
            return carry

        lax.fori_loop(0, cpb * nb, step, 0, unroll=4)
        for cls in range(cpb):
            out_ref[1, cls] = dkpad[cls, ATT_BLOCK:, :]
            out_ref[2, cls] = dvpad[cls, ATT_BLOCK:, :]

    spec = lambda kind: pl.BlockSpec((None, cpb, L, GROUP_W), lambda r: (kind, r, 0, 0))
    per_cls = pl.BlockSpec((cpb, L, GROUP_W), lambda r: (r, 0, 0))
    return pl.pallas_call(
        body,
        name=f"attn_bwd_d{d}",
        grid=(d // cpb,),
        in_specs=[spec(0), spec(1), spec(2), per_cls, per_cls, per_cls],
        out_specs=pl.BlockSpec((3, cpb, L, GROUP_W), lambda r: (0, r, 0, 0)),
        out_shape=jax.ShapeDtypeStruct((3, d, L, GROUP_W), F32),
        scratch_shapes=[pltpu.VMEM((cpb, L + ATT_BLOCK, GROUP_W), BF16)] * 2 + [pltpu.VMEM((cpb, L + ATT_BLOCK, GROUP_W), F32)] * 2,
        compiler_params=_params(1),
    )(qkv, qkv, qkv, dattn, lse, delta)


def pool_bwd(dpool, mixed, wbd, b, scale):
    S = dpool.shape[0]

    def body(dp_ref, mx_ref, w_ref, b_ref, s_ref, du_ref, dw_ref, db_ref, ds_ref):
        dp = dp_ref[...]
        mb = mx_ref[...]
        wv = w_ref[...]
        ypre = jnp.dot(mb, wv, preferred_element_type=F32) + b_ref[...]
        ds_ref[...] = _colsum(dp * ypre)
        dpre = dp * s_ref[...]
        db_ref[...] = _colsum(dpre)
        dpb = dpre.astype(BF16)
        dw_ref[...] = lax.dot_general(mb, dpb, TN, preferred_element_type=F32)
        dmix = lax.dot_general(dpb, wv, NT, preferred_element_type=F32)
        row = lax.broadcasted_iota(jnp.int32, dmix.shape, 0)
        lane, win = _pool_lane_windows(dmix.shape)
        e = dmix / jnp.minimum(row + 1, win).astype(F32)

        def shift(a, k):
            return jnp.where(row < S - k, pltpu.roll(a, S - k, 0), 0.0)

        f2 = e + shift(e, 1)
        f4 = f2 + shift(f2, 2)
        f8 = f4 + shift(f4, 4)
        f16 = f8 + shift(f8, 8)
        du_ref[...] = jnp.where(lane < 64, f2, jnp.where(lane < 128, f4, jnp.where(lane < 192, f8, f16))) - dmix

    vm = pl.BlockSpec(memory_space=pltpu.VMEM)
    return pl.pallas_call(
        body,
        name="pool_bwd",
        in_specs=[vm] * 5,
        out_specs=[vm] * 4,
        out_shape=[jax.ShapeDtypeStruct((S, POOL_W), F32), jax.ShapeDtypeStruct((POOL_W, POOL_W), F32),
                   jax.ShapeDtypeStruct((1, POOL_W), F32), jax.ShapeDtypeStruct((1, POOL_W), F32)],
        compiler_params=_params(),
    )(dpool, mixed, wbd, b, scale)


def inproj_bwd(dqkv, du, x, dx1, w_in_g, g, mod6, tc, tsa, tsb, h1, tm=512):
    S = x.shape[0]

    def body(d0, d1, d2, du_ref, x_ref, dx1_ref, w_ref, g_ref, mod_ref, tc_ref, tsa_ref, tsb_ref, h_ref,
             gx_ref, dsh_ref, dsc_ref, dg_ref, dw_ref, s4, s16, dp_ref):
        @pl.when(pl.program_id(0) == 0)
        def _():
            dw_ref[...] = jnp.zeros_like(dw_ref)

        cs, sa, sb = tc_ref[...], tsa_ref[...], tsb_ref[...]
        for d, src, dst in ((4, d1, s4), (16, d2, s16)):
            for kind in range(3):
                for r in range(d):
                    for h in range(2):
                        dst[kind, h, pl.ds(r, tm // d, stride=d), :] = src[kind, r, :, h * 128:(h + 1) * 128]
        for sp in range(20):
            piece, half = sp // 2, sp % 2
            lanes = slice(half * 128, (half + 1) * 128)
            if piece == 0:
                blk = du_ref[:, lanes]
            else:
                kind, gi = (piece - 1) // 3, (piece - 1) % 3
                blk = d0[kind, 0, :, lanes] if gi == 0 else (s4, s16)[gi - 1][kind, half]
                if kind == 0:
                    blk = _rope128(blk, cs, sa, sb, -1.0) * (HEAD_DIM ** -0.5)
                elif kind == 1:
                    blk = _rope128(blk, cs, sa, sb, -1.0)
            dp_ref[:, sp * 128:(sp + 1) * 128] = blk.astype(BF16)
        dh = jnp.zeros((tm, D_MODEL), F32)
        hbt = h_ref[...].T
        for j in range(N_CHIPS):
            dpj = dp_ref[:, j * 640:(j + 1) * 640]
            dh = dh + lax.dot_general(dpj, w_ref[j], NT, preferred_element_type=F32)
            dw_ref[j] += jnp.dot(hbt, dpj, preferred_element_type=F32)
        xv = x_ref[...]
        rstd = lax.rsqrt(jnp.mean(xv * xv, axis=-1, keepdims=True) + NORM_EPS)
        n1 = xv * rstd
        gv = g_ref[...]
        one_sc = 1.0 + mod_ref[1:2, :]
        _acc(dsh_ref, _colsum(dh))
        _acc(dsc_ref, _colsum(dh * (n1 * gv)))
        _acc(dg_ref, _colsum(dh * one_sc * n1))
        dn = dh * (gv * one_sc)
        gx_ref[...] = dx1_ref[...] + rstd * (dn - n1 * jnp.mean(dn * n1, axis=-1, keepdims=True))

    vec = _full((1, D_MODEL))
    dspec = lambda d: pl.BlockSpec((3, d, tm // d, GROUP_W), lambda i: (0, 0, i, 0))
    return pl.pallas_call(
        body,
        name="inproj_bwd",
        grid=(S // tm,),
        in_specs=[dspec(d) for d in DILATIONS] + [_rows(tm, POOL_W), _rows(tm, D_MODEL), _rows(tm, D_MODEL), _full(w_in_g.shape),
                                                  vec, _full((6, D_MODEL)), _rows(tm, 128), _rows(tm, 128), _rows(tm, 128),
                                                  _rows(tm, D_MODEL)],
        out_specs=[_rows(tm, D_MODEL), vec, vec, vec, _full(w_in_g.shape)],
        out_shape=[jax.ShapeDtypeStruct((S, D_MODEL), F32)] + [jax.ShapeDtypeStruct((1, D_MODEL), F32)] * 3
        + [jax.ShapeDtypeStruct(w_in_g.shape, F32)],
        scratch_shapes=[pltpu.VMEM((3, 2, tm, 128), F32)] * 2 + [pltpu.VMEM((tm, IN_W), BF16)],
        compiler_params=_params(1),
    )(*dqkv, du, x, dx1, w_in_g, g, mod6, tc, tsa, tsb, h1)


def _adamw(w, g, m, v):
    m = ADAM_B1 * m + (1.0 - ADAM_B1) * g
    v = ADAM_B2 * v + (1.0 - ADAM_B2) * (g * g)
    m_hat = m / (1.0 - ADAM_B1 ** ADAM_STEP)
    v_hat = v / (1.0 - ADAM_B2 ** ADAM_STEP)
    delta = -ADAM_LR * (m_hat / (jnp.sqrt(v_hat) + ADAM_EPS) + ADAM_WD * w)
    return delta, m, v


def adamw_rows(w, g, m, v, tr, name):
    R, C = w.shape

    def body(w_ref, g_ref, m_ref, v_ref, go_ref, d_ref, mo_ref, vo_ref):
        g = g_ref[...]
        go_ref[...] = g
        d_ref[...], mo_ref[...], vo_ref[...] = _adamw(w_ref[...], g, m_ref[...], v_ref[...])

    spec = pl.BlockSpec((tr, C), lambda i: (i, 0))
    return pl.pallas_call(
        body,
        name=name,
        grid=(R // tr,),
        in_specs=[spec] * 4,
        out_specs=[spec] * 4,
        out_shape=[jax.ShapeDtypeStruct((R, C), F32)] * 4,
        compiler_params=_params(1),
    )(w, g, m, v)


def adamw_ada(c_all_t, dmod_cols, w, m, v, tr=256):
    R, C = w.shape

    def body(ct_ref, dm_ref, w_ref, m_ref, v_ref, g_ref, d_ref, mo_ref, vo_ref):
        ct = ct_ref[...]
        act = ct * jax.nn.sigmoid(ct)
        dm = dm_ref[...]
        a_hi, d_hi = act.astype(BF16), dm.astype(BF16)
        a_lo, d_lo = (act - a_hi.astype(F32)).astype(BF16), (dm - d_hi.astype(F32)).astype(BF16)
        g = (jnp.dot(a_hi, d_hi, preferred_element_type=F32) + jnp.dot(a_lo, d_hi, preferred_element_type=F32)
             + jnp.dot(a_hi, d_lo, preferred_element_type=F32))
        g_ref[...] = g
        d_ref[...], mo_ref[...], vo_ref[...] = _adamw(w_ref[...], g, m_ref[...], v_ref[...])

    spec = pl.BlockSpec((tr, C), lambda i: (i, 0))
    return pl.pallas_call(
        body,
        name="adamw_ada",
        grid=(R // tr,),
        in_specs=[pl.BlockSpec((tr, N_DEV), lambda i: (i, 0)), _full((N_DEV, C)), spec, spec, spec],
        out_specs=[spec] * 4,
        out_shape=[jax.ShapeDtypeStruct((R, C), F32)] * 4,
        compiler_params=_params(1),
    )(c_all_t, dmod_cols, w, m, v)


def adamw_small(slab_a, slab_b, convw_g, wpool_g, params):
    names = ["b_ada", "g_pre_mix", "g_post_mix", "g_pre_ffn", "g_post_ffn", "b_pool", "pool_scale", "conv_b", "conv_w", "w_pool"]
    flat = []
    for n in names:
        flat += list(params[n])

    def body(a_ref, b_ref, cw_ref, wp_ref, *rest):
        ins, outs = rest[:30], rest[30:]

        def dev_sum(ref):
            t = ref[0]
            for dev in range(1, N_DEV):
                t = t + ref[dev]
            return t

        sa, sb_, scw, swp = dev_sum(a_ref), dev_sum(b_ref), dev_sum(cw_ref), dev_sum(wp_ref)
        grads = [
            jnp.concatenate([sa[k:k + 1, :] for k in range(6)], axis=1),
            sa[6:7, :], sa[7:8, :], sa[8:9, :], sa[9:10, :],
            sa[10:11, 0:256], sa[10:11, 256:512],
            sb_[3:4, :], scw, swp,
        ]
        for i, g in enumerate(grads):
            w_ref, m_ref, v_ref = ins[3 * i:3 * i + 3]
            if names[i] == "b_pool":
                parts = [((0, slice(grp, grp + 1)), g[:, grp * 64:(grp + 1) * 64]) for grp in range(4)]
            elif names[i] == "w_pool":
                parts = [((0, grp), g[grp * 64:(grp + 1) * 64, :]) for grp in range(4)]
            elif names[i] == "conv_w":
                parts = [((0,), g)]
            else:
                parts = [((Ellipsis,), g)]
            for at, gp in parts:
                d, mo, vo = _adamw(w_ref[at], gp, m_ref[at], v_ref[at])
                for k, val in enumerate((gp, d, mo, vo)):
                    outs[4 * i + k][at] = val
        outs[-1][...] = sa[10:11, 512:640]

    vm = pl.BlockSpec(memory_space=pltpu.VMEM)
    out_shape = []
    for n in names:
        out_shape += [jax.ShapeDtypeStruct(params[n][0].shape, F32)] * 4
    out_shape.append(jax.ShapeDtypeStruct((1, 128), F32))
    outs = pl.pallas_call(
        body,
        name="adamw_small",
        in_specs=[vm] * (4 + len(flat)),
        out_specs=[vm] * len(out_shape),
        out_shape=out_shape,
        compiler_params=_params(),
    )(slab_a, slab_b, convw_g, wpool_g, *flat)
    return {n: outs[4 * i:4 * i + 4] for i, n in enumerate(names)}, outs[-1]


def _place():
    return lax.axis_index("x"), lax.axis_index("y"), lax.axis_index("c")


def _other_chips(x, y):
    return [(1 - x, y), (x, 1 - y), (1 - x, 1 - y)]


def _chip_id(cx, cy):
    return 2 * cx + cy


HBM_SPEC = pl.BlockSpec(memory_space=pltpu.HBM)
SEM_SPEC = pl.BlockSpec(memory_space=pltpu.SEMAPHORE)
ANY_SPEC = pl.BlockSpec(memory_space=pl.ANY)
EFFECT = pltpu.SideEffectType.DATAFLOW_SIDE_EFFECTING


def _hbm(t):
    return pltpu.with_memory_space_constraint(t, pltpu.HBM)


def _hbm_shapes(ts):
    return [pltpu.HBM(t.shape, t.dtype) for t in ts]


def _half_rows(ref, lead, half, rh):
    return ref.at[lead, pl.ds(half * rh, rh), :]


def _flips():
    return [(fx, fy, fc) for fx in (0, 1) for fy in (0, 1) for fc in (0, 1)][1:]


def _flip(v, f):
    return v if f == 0 else 1 - v


def ada_mod(c3, w_ada, b_cols, conv_w):
    CB = w_ada.shape[1]

    def body(c_ref, w_ref, b_ref, cw_ref, call_ref, mod_ref, cwall_ref, modall, send_sems, recv_sems):
        x, y, c = _place()
        me_dev = 4 * x + 2 * y + c
        me = _chip_id(x, y)
        call_ref[me_dev] = c_ref[0]
        cwall_ref[me] = cw_ref[...]
        sends = []
        for k, (cx, cy) in enumerate(_other_chips(x, y)):
            cp = pltpu.make_async_remote_copy(src_ref=cw_ref, dst_ref=cwall_ref.at[me], send_sem=send_sems.at[10 + k],
                                              recv_sem=recv_sems.at[10 + k], device_id=(cx, cy, c), device_id_type=MESH)
            cp.start()
            sends.append(cp)
        for k, (fx, fy, fc) in enumerate(_flips()):
            cp = pltpu.make_async_remote_copy(src_ref=c_ref.at[0], dst_ref=call_ref.at[me_dev], send_sem=send_sems.at[k],
                                              recv_sem=recv_sems.at[k],
                                              device_id=(_flip(x, fx), _flip(y, fy), _flip(c, fc)), device_id_type=MESH)
            cp.start()
            sends.append(cp)
        for k, (fx, fy, fc) in enumerate(_flips()):
            peer = 4 * _flip(x, fx) + 2 * _flip(y, fy) + _flip(c, fc)
            pltpu.make_async_remote_copy(src_ref=c_ref.at[0], dst_ref=call_ref.at[peer], send_sem=send_sems.at[k],
                                         recv_sem=recv_sems.at[k], device_id=(x, y, c), device_id_type=MESH).wait_recv()
        row = lax.broadcasted_iota(jnp.int32, (N_DEV, D_MODEL), 0)
        call = jnp.zeros((N_DEV, D_MODEL), F32)
        for dev in range(N_DEV):
            call = jnp.where(row == dev, call_ref[dev], call)
        act = call * jax.nn.sigmoid(call)
        wv = w_ref[...]
        w_hi = wv.astype(BF16)
        w_lo = (wv - w_hi.astype(F32)).astype(BF16)
        a_hi = act.astype(BF16)
        a_lo = (act - a_hi.astype(F32)).astype(BF16)
        prod = (jnp.dot(a_hi, w_hi, preferred_element_type=F32) + jnp.dot(a_lo, w_hi, preferred_element_type=F32)
                + jnp.dot(a_hi, w_lo, preferred_element_type=F32))
        modall[me] = prod + b_ref[...]
        for k, (cx, cy) in enumerate(_other_chips(x, y)):
            cp = pltpu.make_async_remote_copy(src_ref=modall.at[me], dst_ref=modall.at[me], send_sem=send_sems.at[7 + k],
                                              recv_sem=recv_sems.at[7 + k], device_id=(cx, cy, c), device_id_type=MESH)
            cp.start()
            sends.append(cp)
        for k, (cx, cy) in enumerate(_other_chips(x, y)):
            blk = modall.at[_chip_id(cx, cy)]
            pltpu.make_async_remote_copy(src_ref=blk, dst_ref=blk, send_sem=send_sems.at[7 + k], recv_sem=recv_sems.at[7 + k],
                                         device_id=(x, y, c), device_id_type=MESH).wait_recv()
        for k, (cx, cy) in enumerate(_other_chips(x, y)):
            blk = cwall_ref.at[_chip_id(cx, cy)]
            pltpu.make_async_remote_copy(src_ref=blk, dst_ref=blk, send_sem=send_sems.at[10 + k], recv_sem=recv_sems.at[10 + k],
                                         device_id=(x, y, c), device_id_type=MESH).wait_recv()
        for cp in sends:
            cp.wait_send()
        mine = [modall[j, pl.ds(me_dev, 1), :] for j in range(N_CHIPS)]
        for r in range(6):
            pieces = []
            for h in range(2):
                pos = r * D_MODEL + h * 512
                pieces.append(mine[pos // CB][:, pos % CB:pos % CB + 512])
            mod_ref[r:r + 1, :] = jnp.concatenate(pieces, axis=1)

    vm = pl.BlockSpec(memory_space=pltpu.VMEM)
    return pl.pallas_call(
        body,
        name="ada_mod",
        in_specs=[vm] * 4,
        out_specs=[vm] * 3,
        out_shape=[jax.ShapeDtypeStruct((N_DEV, 1, D_MODEL), F32), jax.ShapeDtypeStruct((6, D_MODEL), F32),
                   jax.ShapeDtypeStruct((N_CHIPS,) + conv_w.shape, F32)],
        scratch_shapes=[pltpu.VMEM((N_CHIPS, N_DEV, CB), F32), pltpu.SemaphoreType.DMA((13,)), pltpu.SemaphoreType.DMA((13,))],
        compiler_params=pltpu.CompilerParams(has_side_effects=True, vmem_limit_bytes=VMEM_LIMIT),
    )(c3, w_ada, b_cols, conv_w)


def split_start(name, bufs, plan, n_sem, carry):
    nb = len(bufs)
    many = isinstance(carry, (list, tuple))
    alls = list(bufs) + (list(carry) if many else [carry])
    na = len(alls)

    def body(*refs):
        x, y, c = _place()
        ssem, rsem = refs[na], refs[na + 1]
        for i, (src, dst, dev) in enumerate(plan(refs[:nb], x, y, c)):
            pltpu.make_async_remote_copy(src_ref=src, dst_ref=dst, send_sem=ssem.at[i], recv_sem=rsem.at[i], device_id=dev,
                                         device_id_type=MESH).start()

    outs = pl.pallas_call(
        body,
        name=name,
        out_shape=[pltpu.SemaphoreType.DMA((n_sem,)), pltpu.SemaphoreType.DMA((n_sem,))] + _hbm_shapes(alls),
        in_specs=[HBM_SPEC] * na,
        out_specs=[SEM_SPEC, SEM_SPEC] + [HBM_SPEC] * na,
        input_output_aliases={i: 2 + i for i in range(na)},
        compiler_params=pltpu.CompilerParams(has_side_effects=EFFECT),
    )(*[_hbm(t) for t in alls])
    return outs[0], outs[1], list(outs[2:2 + nb]), (list(outs[2 + nb:]) if many else outs[-1])


def split_wait(name, ssem, rsem, bufs, plan, after):
    nb = len(bufs)

    def body(*refs):
        x, y, c = _place()
        s_ref, r_ref = refs[nb], refs[nb + 1]
        for i, (src, dst, dev) in enumerate(plan(refs[:nb], x, y, c)):
            cp = pltpu.make_async_remote_copy(src_ref=src, dst_ref=dst, send_sem=s_ref.at[i], recv_sem=r_ref.at[i], device_id=dev,
                                              device_id_type=MESH)
            cp.wait_send()
            cp.wait_recv()

    outs = pl.pallas_call(
        body,
        name=name,
        out_shape=_hbm_shapes(bufs),
        in_specs=[HBM_SPEC] * nb + [SEM_SPEC, SEM_SPEC, ANY_SPEC],
        out_specs=[HBM_SPEC] * nb,
        input_output_aliases={i: i for i in range(nb)},
        compiler_params=pltpu.CompilerParams(has_side_effects=EFFECT),
    )(*bufs, ssem, rsem, after)
    return list(outs)


def _gather_ici_plan(n):
    def plan(refs, x, y, c):
        out = []
        for w in range(n):
            rh = refs[w].shape[0] // 2
            for cx, cy in _other_chips(x, y):
                out.append((refs[w].at[pl.ds(c * rh, rh), :], _half_rows(refs[n + w], _chip_id(x, y), c, rh), (cx, cy, c)))
        return out

    return plan


def _gather_d2d_plan(n):
    def plan(refs, x, y, c):
        out = []
        for w in range(n):
            rh = refs[w].shape[1] // 2
            for cx, cy in _other_chips(x, y):
                blk = _half_rows(refs[w], _chip_id(cx, cy), c, rh)
                out.append((blk, blk, (x, y, 1 - c)))
        return out

    return plan


def _dev_id(x, y, c):
    return 4 * x + 2 * y + c


def _small_ici_plan(n):
    def plan(refs, x, y, c):
        out = []
        for w in range(n):
            dst = refs[n + w].at[_dev_id(x, y, c)]
            out.append((refs[w], dst, (x, y, 1 - c)))
            for cx, cy in _other_chips(x, y):
                out.append((refs[w], dst, (cx, cy, c)))
        return out

    return plan


def _small_d2d_plan(n):
    def plan(refs, x, y, c):
        out = []
        for w in range(n):
            for cx, cy in _other_chips(x, y):
                blk = refs[w].at[_dev_id(cx, cy, c)]
                out.append((blk, blk, (x, y, 1 - c)))
        return out

    return plan


def _rs_d2d_plan(n):
    def plan(refs, x, y, c):
        out = []
        for w in range(n):
            rh = refs[w].shape[1] // 2
            out.append((refs[w].at[:, pl.ds((1 - c) * rh, rh), :], refs[n + w], (x, y, 1 - c)))
        return out

    return plan


def _rs_ici_plan(n):
    def plan(refs, x, y, c):
        out = []
        for w in range(n):
            for k, (cx, cy) in enumerate(_other_chips(x, y)):
                out.append((refs[w].at[_chip_id(cx, cy)], refs[n + w].at[k], (cx, cy, c)))
        return out

    return plan


def _rs_share_plan(n):
    def plan(refs, x, y, c):
        out = []
        for w in range(n):
            rh = refs[w].shape[0] // 2
            rows = refs[w].at[pl.ds(c * rh, rh), :]
            out.append((rows, rows, (x, y, 1 - c)))
        return out

    return plan


def rs_add(grad, sibbuf, place, tr, name):
    _, R, C = grad.shape
    nt = (R // 2) // tr

    def body(p_ref, g_ref, s_ref, o_ref):
        o_ref[...] = (g_ref[...] + s_ref[...]).astype(BF16)

    return pl.pallas_call(
        body,
        name=name,
        grid_spec=pltpu.PrefetchScalarGridSpec(
            num_scalar_prefetch=1,
            grid=(N_CHIPS, nt),
            in_specs=[pl.BlockSpec((None, tr, C), lambda j, i, p: (j, p[0] * nt + i, 0)),
                      pl.BlockSpec((None, tr, C), lambda j, i, p: (j, i, 0))],
            out_specs=pl.BlockSpec((None, tr, C), lambda j, i, p: (j, i, 0)),
        ),
        out_shape=jax.ShapeDtypeStruct((N_CHIPS, R // 2, C), BF16),
        compiler_params=_params(2),
    )(place, grad, sibbuf)


def rs_final(grad, sibbuf, rbuf, place, tr, name):
    _, R, C = grad.shape
    nt = (R // 2) // tr

    def body(p_ref, g_ref, s_ref, r_ref, o_ref):
        o_ref[...] = (((g_ref[...] + s_ref[...]) + r_ref[0].astype(F32)) + r_ref[1].astype(F32)) + r_ref[2].astype(F32)

    return pl.pallas_call(
        body,
        name=name,
        grid_spec=pltpu.PrefetchScalarGridSpec(
            num_scalar_prefetch=1,
            grid=(nt,),
            in_specs=[pl.BlockSpec((None, tr, C), lambda i, p: (p[1], p[0] * nt + i, 0)),
                      pl.BlockSpec((None, tr, C), lambda i, p: (p[1], i, 0)),
                      pl.BlockSpec((3, tr, C), lambda i, p: (0, i, 0))],
            out_specs=pl.BlockSpec((tr, C), lambda i, p: (p[0] * nt + i, 0)),
        ),
        out_shape=jax.ShapeDtypeStruct((R, C), F32),
        compiler_params=_params(1),
    )(place, grad, sibbuf, rbuf)


class GradReduce:
    def __init__(self, tag, grads, rows, place):
        self.tag, self.grads, self.rows, self.place = tag, grads, rows, place
        self.n = len(grads)

    def d2d_start(self, carry):
        sib = [lax.empty((N_CHIPS, g.shape[1] // 2, g.shape[2]), F32) for g in self.grads]
        self.s1, self.r1, bufs, carry = split_start(f"rs_{self.tag}_d2d_start", self.grads + sib, _rs_d2d_plan(self.n), self.n, carry)
        self.bufs1 = bufs
        return carry

    def add_and_ici_start(self, after, carry):
        bufs = split_wait(f"rs_{self.tag}_d2d_wait", self.s1, self.r1, self.bufs1, _rs_d2d_plan(self.n), after)
        self.grads, self.sib = bufs[:self.n], bufs[self.n:]
        pb = [rs_add(g, s, self.place, tr, f"rs_{self.tag}_add{w}")
              for w, (g, s, tr) in enumerate(zip(self.grads, self.sib, self.rows))]
        rb = [lax.empty((3,) + p.shape[1:], BF16) for p in pb]
        self.s2, self.r2, self.bufs2, carry = split_start(f"rs_{self.tag}_ici_start", pb + rb, _rs_ici_plan(self.n), 3 * self.n, carry)
        return carry

    def final_and_share_start(self, after, carry):
        bufs = split_wait(f"rs_{self.tag}_ici_wait", self.s2, self.r2, self.bufs2, _rs_ici_plan(self.n), after)
        rb = bufs[self.n:]
        full = [rs_final(g, s, r, self.place, tr, f"rs_{self.tag}_final{w}")
                for w, (g, s, r, tr) in enumerate(zip(self.grads, self.sib, rb, self.rows))]
        self.s3, self.r3, self.bufs3, carry = split_start(f"rs_{self.tag}_share_start", full, _rs_share_plan(self.n), self.n, carry)
        return carry

    def finish(self, after):
        return split_wait(f"rs_{self.tag}_share_wait", self.s3, self.r3, self.bufs3, _rs_share_plan(self.n), after)


def _rope_tables(positions):
    inv_freq = ROPE_THETA ** (-jnp.arange(0, ROT_DIM, 2, dtype=F32) / ROT_DIM)
    ang = positions.astype(F32)[:, None] * inv_freq
    cos, sin = jnp.cos(ang), jnp.sin(ang)
    S = positions.shape[0]
    one, zero = jnp.ones((S, 48), F32), jnp.zeros((S, 48), F32)
    z8 = jnp.zeros((S, 8), F32)
    tc = jnp.concatenate([cos, cos, one], axis=1)
    tsa = jnp.concatenate([z8, sin, zero], axis=1)
    tsb = jnp.concatenate([-sin, z8, zero], axis=1)
    return tuple(jnp.tile(t, (1, 2)) for t in (tc, tsa, tsb))


def _block_diag(w_pool):
    wbd = jnp.zeros((POOL_W, POOL_W), F32)
    for gi in range(4):
        wbd = wbd.at[gi * 64:(gi + 1) * 64, gi * 64:(gi + 1) * 64].set(w_pool[gi])
    return wbd


def kernel(x, c, positions, w_ada, b_ada, g_pre_mix, g_post_mix, g_pre_ffn, g_post_ffn, w_in, w_pool, b_pool, pool_scale, w_out, w_up, conv_w, conv_b, w_down, loss_target, m_w_ada, m_b_ada, m_g_pre_mix, m_g_post_mix, m_g_pre_ffn, m_g_post_ffn, m_w_in, m_w_pool, m_b_pool, m_pool_scale, m_w_out, m_w_up, m_conv_w, m_conv_b, m_w_down, v_w_ada, v_b_ada, v_g_pre_mix, v_g_post_mix, v_g_pre_ffn, v_g_post_ffn, v_w_in, v_w_pool, v_b_pool, v_pool_scale, v_w_out, v_w_up, v_conv_w, v_conv_b, v_w_down):
    xi, yi, ci = lax.axis_index("x"), lax.axis_index("y"), lax.axis_index("c")
    chip = 2 * xi + yi
    place = jnp.stack([ci, chip]).astype(jnp.int32)
    x2, tgt = x[0], loss_target[0]
    S = x2.shape[0]

    def landing(s_):
        return lax.dynamic_update_slice(lax.empty((N_CHIPS,) + s_.shape, s_.dtype), s_[None], (chip, 0, 0))

    cb_ada = w_ada.shape[2]
    b_cols = lax.dynamic_slice(b_ada, (0, chip * cb_ada), (1, cb_ada))
    c_all, mod6, conv_w_g = ada_mod(c.reshape(1, 1, D_MODEL), w_ada[0], b_cols, conv_w[0])
    conv_w_f = jnp.transpose(conv_w_g, (1, 0, 2)).reshape(3, D_FF)
    mix_sh = [w_in[0].astype(BF16), w_out[0].astype(BF16)]
    ffn_sh = [w_up[0].astype(BF16), w_down[0].astype(BF16)]
    ga_s, ga_r, ga_bufs, mod6 = split_start("gather_mix_ici_start", mix_sh + [landing(t) for t in mix_sh], _gather_ici_plan(2), 6, mod6)
    gb_s, gb_r, gb_bufs, (mod6, tc, tsa, tsb) = split_start("gather_ffn_ici_start", ffn_sh + [landing(t) for t in ffn_sh],
                                                            _gather_ici_plan(2), 6, [mod6, *_rope_tables(positions[0])])
    wbd = _block_diag(w_pool[0]).astype(BF16)
    b_pool2, scale2 = b_pool.reshape(1, POOL_W), pool_scale
    ga_bufs = split_wait("gather_mix_ici_wait", ga_s, ga_r, ga_bufs, _gather_ici_plan(2), mod6)
    gc_s, gc_r, mix_land, mod6 = split_start("gather_mix_d2d_start", ga_bufs[2:], _gather_d2d_plan(2), 6, mod6)
    w_in_g, w_out_g = split_wait("gather_mix_d2d_wait", gc_s, gc_r, mix_land, _gather_d2d_plan(2), mod6)

    h1, u, *qkv = inproj_fwd(x2, g_pre_mix, mod6, w_in_g, tc, tsa, tsb)
    mixed, pool = pool_fwd(u, wbd, b_pool2, scale2)
    o_l = [attn_fwd(t, d) for t, d in zip(qkv, DILATIONS)]
    attn_done = sum(l[0, :8, :128] for _, l in o_l)
    gb_bufs = split_wait("gather_ffn_ici_wait", gb_s, gb_r, gb_bufs, _gather_ici_plan(2), attn_done)
    gd_s, gd_r, ffn_land, pool = split_start("gather_ffn_d2d_start", gb_bufs[2:], _gather_d2d_plan(2), 6, pool)
    cat, lse, lse4, lse16, y1, x1, h2 = outproj_fwd([o for o, _ in o_l] + [l for _, l in o_l], pool, x2, w_out_g, g_post_mix,
                                                    g_pre_ffn, mod6)
    lses = [lse[None], lse4, lse16]
    w_up_g, w_down_g = split_wait("gather_ffn_d2d_wait", gd_s, gd_r, ffn_land, _gather_d2d_plan(2), h2)
    w_down_f = w_down_g.reshape(D_FF, D_MODEL)
    gate, val, a, dy2, dout, loss_v, d_gt_f, d_g_post_ffn = ffn_fwd(h2, w_up_g, conv_w_f, conv_b, w_down_f, x1, tgt, g_post_ffn, mod6)

    dgc, dval, d_conv_w, d_conv_b, dw_down, dw_up = down_bwd(dy2, w_down_f, gate, val, conv_w_f, conv_b, a, h2)
    dx1, dy1, d_sh_f, d_sc_f, d_g_pre_ffn, d_gt_m, d_g_post_mix, dw_up = up_bwd(
        dgc, dval, conv_w_f, w_up_g, x1, dout, y1, g_pre_ffn, g_post_mix, mod6, h2, dw_up)
    rs_ffn = GradReduce("ffn", [dw_up, dw_down.reshape(N_CHIPS, D_FF // N_CHIPS, D_MODEL)], [256, 176], place)
    dy1 = rs_ffn.d2d_start(dy1)
    dpool, da1, da4, da16, dl1, dl4, dl16, dw_out = outproj_bwd(dy1, w_out_g, cat)
    dpool = rs_ffn.add_and_ici_start(dw_out, dpool)
    du, d_wbd, d_b_pool, d_scale = pool_bwd(dpool, mixed, wbd, b_pool2, scale2)
    dqkv = [attn_bwd(t, da, ls, dl, d) for t, da, ls, dl, d in zip(qkv, (da1[None], da4, da16), lses, (dl1[None], dl4, dl16), DILATIONS)]
    grad_x, d_sh_m, d_sc_m, d_g_pre_mix, dw_in = inproj_bwd(dqkv, du, x2, dx1, w_in_g, g_pre_mix, mod6, tc, tsa, tsb, h1)

    z1 = jnp.zeros((1, D_MODEL), F32)
    slab_a = jnp.concatenate(
        [d_sh_m, d_sc_m, d_gt_m, d_sh_f, d_sc_f, d_gt_f, d_g_pre_mix, d_g_post_mix, d_g_pre_ffn, d_g_post_ffn,
         jnp.concatenate([d_b_pool, d_scale, loss_v, jnp.zeros((1, 384), F32)], axis=1)] + [z1] * 5, axis=0)
    slab_b = jnp.concatenate([d_conv_w, d_conv_b, jnp.zeros((4, D_FF), F32)], axis=0)
    d_wpool = jnp.concatenate([d_wbd[gi * 64:(gi + 1) * 64, gi * 64:(gi + 1) * 64] for gi in range(4)], axis=0)
    dev = _dev_id(xi, yi, ci)
    small_src = [slab_a, slab_b, d_wpool]
    small_land = [lax.dynamic_update_slice(lax.empty((N_DEV,) + t.shape, F32), t[None], (dev, 0, 0)) for t in small_src]
    tok = jnp.zeros((8, 128), F32)
    gs_s, gs_r, gs_bufs, tok = split_start("small_ici_start", small_src + small_land, _small_ici_plan(3), 12, tok)
    rs_mix = GradReduce("mix", [dw_in, dw_out], [256, 256], place)
    tok = rs_mix.d2d_start(tok)
    tok = rs_ffn.final_and_share_start(tok, tok)
    gs_bufs = split_wait("small_ici_wait", gs_s, gs_r, gs_bufs, _small_ici_plan(3), tok)
    gt_s, gt_r, small_land, tok = split_start("small_d2d_start", gs_bufs[3:], _small_d2d_plan(3), 9, tok)
    tok = rs_mix.add_and_ici_start(tok, tok)
    slab_a_g, slab_b_g, wpool_g = split_wait("small_d2d_wait", gt_s, gt_r, small_land, _small_d2d_plan(3), tok)
    cw_cols = conv_w.shape[2]
    convw_g = lax.dynamic_slice(slab_b_g, (0, 0, chip * cw_cols), (N_DEV, 3, cw_cols))
    dmod_cols = lax.dynamic_slice(slab_a_g[:, :6, :].reshape(N_DEV, 6 * D_MODEL), (0, chip * cb_ada), (N_DEV, cb_ada))

    res = {}

    def big_adamw(name, w, g, m, v, tr):
        g_, d_, m_, v_ = adamw_rows(w[0], g, m[0], v[0], tr, "adamw_" + name)
        res[name] = (g_[None], d_[None], m_[None], v_[None])
        return v_

    g_ada, d_ada, m_ada, v_ada = adamw_ada(c_all.reshape(N_DEV, D_MODEL).T, dmod_cols, w_ada[0], m_w_ada[0], v_w_ada[0])
    res["w_ada"] = (g_ada[None], d_ada[None], m_ada[None], v_ada[None])
    g_w_up, g_w_down = rs_ffn.finish(v_ada)
    big_adamw("w_up", w_up, g_w_up, m_w_up, v_w_up, 256)
    last = big_adamw("w_down", w_down, g_w_down, m_w_down, v_w_down, 352)
    rs_mix.final_and_share_start(last, jnp.zeros((8, 128), F32))
    g_w_in, g_w_out = rs_mix.finish(last)
    big_adamw("w_in", w_in, g_w_in, m_w_in, v_w_in, 256)
    big_adamw("w_out", w_out, g_w_out, m_w_out, v_w_out, 256)
    small, loss_sum = adamw_small(slab_a_g, slab_b_g, convw_g, wpool_g, {
        "b_ada": (b_ada, m_b_ada, v_b_ada), "g_pre_mix": (g_pre_mix, m_g_pre_mix, v_g_pre_mix),
        "g_post_mix": (g_post_mix, m_g_post_mix, v_g_post_mix), "g_pre_ffn": (g_pre_ffn, m_g_pre_ffn, v_g_pre_ffn),
        "g_post_ffn": (g_post_ffn, m_g_post_ffn, v_g_post_ffn), "b_pool": (b_pool, m_b_pool, v_b_pool),
        "pool_scale": (pool_scale, m_pool_scale, v_pool_scale), "conv_b": (conv_b, m_conv_b, v_conv_b),
        "conv_w": (conv_w, m_conv_w, v_conv_w), "w_pool": (w_pool, m_w_pool, v_w_pool)})
    for name in ("b_ada", "g_pre_mix", "g_post_mix", "g_pre_ffn", "g_post_ffn", "pool_scale", "conv_b", "b_pool", "w_pool", "conv_w"):
        res[name] = tuple(small[name])

    loss = loss_sum[0, 0]
    order = ["w_ada", "b_ada", "g_pre_mix", "g_post_mix", "g_pre_ffn", "g_post_ffn", "w_in", "w_pool", "b_pool", "pool_scale",
             "w_out", "w_up", "conv_w", "conv_b", "w_down"]
    outs = [loss, grad_x[None]]
    for k in range(4):
        outs += [res[n][k] for n in order]
    return tuple(outs)
```

```python
import math

import jax
import jax.numpy as jnp
from jax import lax
from jax.experimental import pallas as pl
from jax.experimental.pallas import tpu as pltpu

F32 = jnp.float32
BF16 = jnp.bfloat16
MESH = pl.DeviceIdType.MESH

D_MODEL = 1024
HEAD_DIM = 64
POOL_W = 256
GROUP_W = 256
DILATIONS = (1, 4, 16)
ATT_BLOCK = 128
IN_W = 2560
D_FF = 2816
HALF_FF = 1408
ROT_DIM = 16
ROPE_THETA = 500000.0
NORM_EPS = 1e-6
N_CHIPS = 4
N_DEV = 8
NEG = -1e30

ADAM_LR = 0.001
ADAM_B1 = 0.9
ADAM_B2 = 0.999
ADAM_EPS = 1e-08
ADAM_WD = 0.01
ADAM_STEP = 10

VMEM_LIMIT = 56 * 1024 * 1024

NT = (((1,), (1,)), ((), ()))
TN = (((0,), (0,)), ((), ()))


def _params(n_grid=0, **kw):
    sem = ("arbitrary",) * n_grid if n_grid else None
    return pltpu.CompilerParams(dimension_semantics=sem, vmem_limit_bytes=VMEM_LIMIT, **kw)


def _full(shape):
    nd = len(shape)
    return pl.BlockSpec(tuple(shape), lambda *_: (0,) * nd, pipeline_mode=pl.Buffered(1))


def _rows(tm, ncol):
    return pl.BlockSpec((tm, ncol), lambda i: (i, 0))


def _acc(ref, val):
    @pl.when(pl.program_id(0) == 0)
    def _():
        ref[...] = jnp.zeros_like(ref)

    ref[...] += val


def _colsum(v):
    return jnp.sum(v, axis=0, keepdims=True)


def _rope128(t, cs, sa, sb, sign):
    return t * cs + sign * (pltpu.roll(t, 8, 1) * sa + pltpu.roll(t, 120, 1) * sb)


FF_CHUNKS = tuple((ch, off, w) for ch in range(2) for off, w in ((0, 512), (512, 512), (1024, 384)))
GELU_C0 = math.sqrt(2.0 / math.pi)
GELU_C1 = GELU_C0 * 0.044715


def _gelu(z):
    z2 = z * z
    t = jnp.tanh(z * (GELU_C0 + GELU_C1 * z2))
    u = 0.5 * t + 0.5
    return z * u, u, t, z2


def _gelu_grad(z, u, t, z2):
    return u + (z * (GELU_C0 + (3.0 * GELU_C1) * z2)) * (0.5 - 0.5 * (t * t))


def _conv_taps(gate, halo, first):
    row = lax.broadcasted_iota(jnp.int32, gate.shape, 0)
    halo = jnp.where(first, 0.0, halo)
    nh = halo.shape[0]
    p1 = halo[nh - 1:nh, :]
    p2 = halo[nh - 2:nh - 1, :]
    g1 = jnp.where(row == 0, p1, pltpu.roll(gate, 1, 0))
    g2 = jnp.where(row == 0, p2, jnp.where(row == 1, p1, pltpu.roll(gate, 2, 0)))
    return g1, g2


def inproj_fwd(x, g, mod6, w_in_g, tc, tsa, tsb, tm=512):
    S = x.shape[0]

    def body(x_ref, g_ref, mod_ref, w_ref, tc_ref, tsa_ref, tsb_ref, h_ref, u_ref, q1_ref, q4_ref, q16_ref, scr):
        qkv_refs = (q1_ref, q4_ref, q16_ref)
        xv = x_ref[...]
        rstd = lax.rsqrt(jnp.mean(xv * xv, axis=-1, keepdims=True) + NORM_EPS)
        h = ((xv * rstd) * g_ref[...]) * (1.0 + mod_ref[1:2, :]) + mod_ref[0:1, :]
        hb = h.astype(BF16)
        h_ref[...] = hb
        cs, sa, sb = tc_ref[...], tsa_ref[...], tsb_ref[...]
        for j in range(N_CHIPS):
            res = jnp.dot(hb, w_ref[j], preferred_element_type=F32)
            for t in range(5):
                sp = 5 * j + t
                piece, half = sp // 2, sp % 2
                blk = res[:, t * 128:(t + 1) * 128]
                lanes = slice(half * 128, (half + 1) * 128)
                if piece == 0:
                    u_ref[:, lanes] = blk
                else:
                    kind, gi = (piece - 1) // 3, (piece - 1) % 3
                    if kind == 0:
                        blk = _rope128(blk, cs, sa, sb, 1.0) * (HEAD_DIM ** -0.5)
                    elif kind == 1:
                        blk = _rope128(blk, cs, sa, sb, 1.0)
                    d = DILATIONS[gi]
                    if d == 1:
                        q1_ref[kind, 0, :, lanes] = blk.astype(BF16)
                    else:
                        scr[...] = blk
                        for r in range(d):
                            qkv_refs[gi][kind, r, :, lanes] = scr[pl.ds(r, tm // d, stride=d), :].astype(BF16)

    cls = lambda d: pl.BlockSpec((3, d, tm // d, GROUP_W), lambda i: (0, 0, i, 0))
    return pl.pallas_call(
        body,
        name="inproj_fwd",
        grid=(S // tm,),
        in_specs=[_rows(tm, D_MODEL), _full((1, D_MODEL)), _full((6, D_MODEL)), _full(w_in_g.shape),
                  _rows(tm, 128), _rows(tm, 128), _rows(tm, 128)],
        out_specs=[_rows(tm, D_MODEL), _rows(tm, POOL_W)] + [cls(d) for d in DILATIONS],
        out_shape=[jax.ShapeDtypeStruct((S, D_MODEL), BF16), jax.ShapeDtypeStruct((S, POOL_W), F32)]
        + [jax.ShapeDtypeStruct((3, d, S // d, GROUP_W), BF16) for d in DILATIONS],
        scratch_shapes=[pltpu.VMEM((tm, 128), F32)],
        compiler_params=_params(1),
    )(x, g, mod6, w_in_g, tc, tsa, tsb)


def _attn_masks():
    row = lax.broadcasted_iota(jnp.int32, (2 * ATT_BLOCK, 2 * ATT_BLOCK), 0) % ATT_BLOCK
    col = lax.broadcasted_iota(jnp.int32, (2 * ATT_BLOCK, 2 * ATT_BLOCK), 1)
    band = (col >= row) & (col <= row + ATT_BLOCK)
    lane = lax.broadcasted_iota(jnp.int32, (ATT_BLOCK, 128), 1)
    return band, col, lane < HEAD_DIM


def _classes_per_step(d, nb):
    return min(d, max(1, 8 // nb))


def _stack_heads(t, lo):
    z = jnp.zeros_like(t)
    return jnp.concatenate([jnp.where(lo, t, z), jnp.where(lo, z, t)], axis=0)


def _unstack_heads(t2, lo):
    return jnp.where(lo, t2[:ATT_BLOCK], t2[ATT_BLOCK:])


def attn_fwd(qkv, d):
    L = qkv.shape[2]
    nb = L // ATT_BLOCK
    cpb = _classes_per_step(d, nb)

    def body(q_ref, k_ref, v_ref, o_ref, l_ref, kpad, vpad):
        for cls in range(cpb):
            kpad[cls, 0:ATT_BLOCK, :] = jnp.zeros((ATT_BLOCK, GROUP_W), BF16)
            vpad[cls, 0:ATT_BLOCK, :] = jnp.zeros((ATT_BLOCK, GROUP_W), BF16)
            kpad[cls, ATT_BLOCK:, :] = k_ref[cls]
            vpad[cls, ATT_BLOCK:, :] = v_ref[cls]
        band, col, lo = _attn_masks()

        def step(t, carry):
            cls, n = t // nb, t % nb
            r0 = pl.multiple_of(n * ATT_BLOCK, ATT_BLOCK)
            valid = band & ((col >= ATT_BLOCK) | (n > 0))
            qb = q_ref[cls, pl.ds(r0, ATT_BLOCK), :]
            kb = kpad[cls, pl.ds(r0, 2 * ATT_BLOCK), :]
            vb = vpad[cls, pl.ds(r0, 2 * ATT_BLOCK), :]
            for pair in range(2):
                lanes = slice(pair * 128, (pair + 1) * 128)
                qp, kp, vp = qb[:, lanes], kb[:, lanes], vb[:, lanes]
                s = lax.dot_general(_stack_heads(qp, lo), kp, NT, preferred_element_type=F32)
                s = jnp.where(valid, s, NEG)
                m = jnp.max(s, axis=1, keepdims=True)
                p = jnp.exp(s - m)
                den = jnp.sum(p, axis=1, keepdims=True)
                pv = jnp.dot(p.astype(BF16), vp, preferred_element_type=F32)
                o_ref[cls, pl.ds(r0, ATT_BLOCK), lanes] = _unstack_heads(pv / den, lo)
                l_ref[cls, pl.ds(r0, ATT_BLOCK), lanes] = _unstack_heads(jnp.broadcast_to(m + jnp.log(den), pv.shape), lo)
            return carry

        lax.fori_loop(0, cpb * nb, step, 0, unroll=4)

    spec = lambda kind: pl.BlockSpec((None, cpb, L, GROUP_W), lambda r: (kind, r, 0, 0))
    return pl.pallas_call(
        body,
        name=f"attn_fwd_d{d}",
        grid=(d // cpb,),
        in_specs=[spec(0), spec(1), spec(2)],
        out_specs=[pl.BlockSpec((cpb, L, GROUP_W), lambda r: (r, 0, 0))] * 2,
        out_shape=[jax.ShapeDtypeStruct((d, L, GROUP_W), F32)] * 2,
        scratch_shapes=[pltpu.VMEM((cpb, L + ATT_BLOCK, GROUP_W), BF16)] * 2,
        compiler_params=_params(1),
    )(qkv, qkv, qkv)


def _pool_lane_windows(shape):
    lane = lax.broadcasted_iota(jnp.int32, shape, 1)
    return lane, jnp.where(lane < 64, 2, jnp.where(lane < 128, 4, jnp.where(lane < 192, 8, 16)))


def pool_fwd(u, wbd, b, scale):
    S = u.shape[0]

    def body(u_ref, w_ref, b_ref, s_ref, mixed_ref, out_ref):
        uv = u_ref[...]
        row = lax.broadcasted_iota(jnp.int32, uv.shape, 0)
        lane, win = _pool_lane_windows(uv.shape)

        def shift(a, k):
            return jnp.where(row >= k, pltpu.roll(a, k, 0), 0.0)

        s2 = uv + shift(uv, 1)
        s4 = s2 + shift(s2, 2)
        s8 = s4 + shift(s4, 4)
        s16 = s8 + shift(s8, 8)
        tsum = jnp.where(lane < 64, s2, jnp.where(lane < 128, s4, jnp.where(lane < 192, s8, s16)))
        cnt = jnp.minimum(row + 1, win).astype(F32)
        mb = (tsum / cnt - uv).astype(BF16)
        mixed_ref[...] = mb
        y = jnp.dot(mb, w_ref[...], preferred_element_type=F32) + b_ref[...]
        out_ref[...] = (y * s_ref[...]).astype(BF16)

    vm = pl.BlockSpec(memory_space=pltpu.VMEM)
    return pl.pallas_call(
        body,
        name="pool_fwd",
        in_specs=[vm] * 4,
        out_specs=[vm] * 2,
        out_shape=[jax.ShapeDtypeStruct((S, POOL_W), BF16)] * 2,
        compiler_params=_params(),
    )(u, wbd, b, scale)


def outproj_fwd(o_l, pool, x, w_out_g, g_post, g_pre, mod6, tm=512):
    S = x.shape[0]

    def body(o0, o1, o2, l0, l1, l2, pool_ref, x_ref, w_ref, gpost_ref, gpre_ref, mod_ref,
             cat_ref, lse_ref, lse4_ref, lse16_ref, y1_ref, x1_ref, h2_ref, so4, sl4, so16, sl16):
        for d, src, dst in ((4, o1, so4), (4, l1, sl4), (16, o2, so16), (16, l2, sl16)):
            for r in range(d):
                for h in range(2):
                    dst[h, pl.ds(r, tm // d, stride=d), :] = src[r, :, h * 128:(h + 1) * 128]
        nat = lambda ref: jnp.concatenate([ref[0], ref[1]], axis=1)
        a, b, c = l0[0], nat(sl4), nat(sl16)
        m = jnp.maximum(jnp.maximum(a, b), c)
        e0, e1, e2 = jnp.exp(a - m), jnp.exp(b - m), jnp.exp(c - m)
        z = e0 + e1 + e2
        lse = m + jnp.log(z)
        lse_ref[...] = lse
        for h in range(2):
            sl4[h] = lse[:, h * 128:(h + 1) * 128]
        for d, dst in ((4, lse4_ref), (16, lse16_ref)):
            for r in range(d):
                for h in range(2):
                    dst[r, :, h * 128:(h + 1) * 128] = sl4[h, pl.ds(r, tm // d, stride=d), :]
        attn = (e0 * o0[0] + e1 * nat(so4) + e2 * nat(so16)) / z
        cat = jnp.concatenate([pool_ref[...], attn.astype(BF16)], axis=1)
        cat_ref[...] = cat
        y1 = jnp.concatenate([jnp.dot(cat, w_ref[j], preferred_element_type=F32) for j in range(N_CHIPS)], axis=1)
        y1_ref[...] = y1
        rstd = lax.rsqrt(jnp.mean(y1 * y1, axis=-1, keepdims=True) + NORM_EPS)
        x1 = x_ref[...] + mod_ref[2:3, :] * ((y1 * rstd) * gpost_ref[...])
        x1_ref[...] = x1
        rstd2 = lax.rsqrt(jnp.mean(x1 * x1, axis=-1, keepdims=True) + NORM_EPS)
        h2 = ((x1 * rstd2) * gpre_ref[...]) * (1.0 + mod_ref[4:5, :]) + mod_ref[3:4, :]
        h2_ref[...] = h2.astype(BF16)

    t256 = _rows(tm, GROUP_W)
    cls = lambda d: pl.BlockSpec((d, tm // d, GROUP_W), lambda i: (0, i, 0))
    cls_shape = lambda d: jax.ShapeDtypeStruct((d, S // d, GROUP_W), F32)
    return pl.pallas_call(
        body,
        name="outproj_fwd",
        grid=(S // tm,),
        in_specs=[cls(d) for d in DILATIONS] * 2 + [t256, _rows(tm, D_MODEL), _full(w_out_g.shape), _full((1, D_MODEL)),
                                                    _full((1, D_MODEL)), _full((6, D_MODEL))],
        out_specs=[_rows(tm, 512), t256, cls(4), cls(16), _rows(tm, D_MODEL), _rows(tm, D_MODEL), _rows(tm, D_MODEL)],
        out_shape=[jax.ShapeDtypeStruct((S, 512), BF16), jax.ShapeDtypeStruct((S, GROUP_W), F32), cls_shape(4), cls_shape(16),
                   jax.ShapeDtypeStruct((S, D_MODEL), F32), jax.ShapeDtypeStruct((S, D_MODEL), F32),
                   jax.ShapeDtypeStruct((S, D_MODEL), BF16)],
        scratch_shapes=[pltpu.VMEM((2, tm, 128), F32)] * 4,
        compiler_params=_params(1),
    )(*o_l, pool, x, w_out_g, g_post, g_pre, mod6)


def _halo_prev(tm, ncol):
    return pl.BlockSpec((16, ncol), lambda i: (jnp.maximum(i * (tm // 16) - 1, 0), 0))


def ffn_fwd(h2, w_up_g, conv_w, conv_b, w_down, x1, target, g_post, mod6, tm=512):
    S = x1.shape[0]

    def body(h_ref, wu_ref, cw_ref, cb_ref, wd_ref, x1_ref, tgt_ref, g_ref, mod_ref,
             gate_ref, val_ref, dy2_ref, dout_ref, loss_ref, dgt_ref, dg_ref, carry):
        first = pl.program_id(0) == 0

        @pl.when(first)
        def _():
            carry[...] = jnp.zeros_like(carry)

        hb = h_ref[...]
        y2 = jnp.zeros((tm, D_MODEL), F32)
        for ch in range(2):
            cols = slice(ch * HALF_FF, (ch + 1) * HALF_FF)
            gb = jnp.dot(hb, wu_ref[ch], preferred_element_type=F32).astype(BF16)
            vb = jnp.dot(hb, wu_ref[2 + ch], preferred_element_type=F32).astype(BF16)
            gate_ref[:, cols] = gb
            val_ref[:, cols] = vb
            gt = gb.astype(F32)
            g1, g2 = _conv_taps(gt, carry[:, cols], first)
            carry[:, cols] = gt[tm - 8:, :]
            gc = g2 * cw_ref[0:1, cols] + g1 * cw_ref[1:2, cols] + gt * cw_ref[2:3, cols] + cb_ref[:, cols]
            ab = _gelu(gc.astype(BF16))[0] * vb
            y2 = y2 + jnp.dot(ab, wd_ref[cols, :], preferred_element_type=F32)
        rstd = lax.rsqrt(jnp.mean(y2 * y2, axis=-1, keepdims=True) + NORM_EPS)
        y2n = y2 * rstd
        gv = g_ref[...]
        gtf = mod_ref[5:6, :]
        r2 = y2n * gv
        diff = (x1_ref[...] + gtf * r2) - tgt_ref[...]
        _acc(loss_ref, jnp.zeros((1, 128), F32) + 0.5 * jnp.sum(diff * diff) * (1.0 / D_MODEL))
        dout = diff * (1.0 / D_MODEL)
        dout_ref[...] = dout
        _acc(dgt_ref, _colsum(dout * r2))
        dr2 = dout * gtf
        _acc(dg_ref, _colsum(dr2 * y2n))
        dyn = dr2 * gv
        dy2 = rstd * (dyn - y2n * jnp.mean(dyn * y2n, axis=-1, keepdims=True))
        dy2_ref[...] = dy2.astype(BF16)

    vec = _full((1, D_MODEL))
    return pl.pallas_call(
        body,
        name="ffn_fwd",
        grid=(S // tm,),
        in_specs=[_rows(tm, D_MODEL), _full(w_up_g.shape), _full((3, D_FF)), _full((1, D_FF)), _full((D_FF, D_MODEL)),
                  _rows(tm, D_MODEL), _rows(tm, D_MODEL), vec, _full((6, D_MODEL))],
        out_specs=[_rows(tm, D_FF), _rows(tm, D_FF), _rows(tm, D_MODEL), _rows(tm, D_MODEL), _full((1, 128)), vec, vec],
        out_shape=[jax.ShapeDtypeStruct((S, D_FF), BF16)] * 2 + [jax.ShapeDtypeStruct((S, D_MODEL), BF16),
                                                                 jax.ShapeDtypeStruct((S, D_MODEL), F32),
                                                                 jax.ShapeDtypeStruct((1, 128), F32),
                                                                 jax.ShapeDtypeStruct((1, D_MODEL), F32),
                                                                 jax.ShapeDtypeStruct((1, D_MODEL), F32)],
        scratch_shapes=[pltpu.VMEM((8, D_FF), F32)],
        compiler_params=_params(1),
    )(h2, w_up_g, conv_w, conv_b, w_down, x1, target, g_post, mod6)


def down_bwd(dy2, w_down, gate, val, conv_w, conv_b, h2, tm=256):
    S = dy2.shape[0]

    def body(dy_ref, w_ref, gate_ref, halo_ref, val_ref, cw_ref, cb_ref, h_ref,
             dgc_ref, dval_ref, dcw_ref, dcb_ref, dwd_ref, dwu_ref):
        first = pl.program_id(0) == 0

        @pl.when(first)
        def _():
            dcw_ref[...] = jnp.zeros_like(dcw_ref)
            dcb_ref[...] = jnp.zeros_like(dcb_ref)
            dwd_ref[...] = jnp.zeros_like(dwd_ref)
            dwu_ref[...] = jnp.zeros_like(dwu_ref)

        dyb = dy_ref[...]
        hb = h_ref[...]
        def col(i):
            ch, off, width = FF_CHUNKS[i]
            return slice(ch * HALF_FF + off, ch * HALF_FF + off + width)

        def mm_da(i):
            return lax.dot_general(dyb, w_ref[col(i), :], NT, preferred_element_type=F32)

        def elementwise(i, da):
            cols = col(i)
            gt = gate_ref[:, cols].astype(F32)
            g1, g2 = _conv_taps(gt, halo_ref[:, cols].astype(F32), first)
            gc = g2 * cw_ref[0:1, cols] + g1 * cw_ref[1:2, cols] + gt * cw_ref[2:3, cols] + cb_ref[:, cols]
            zb, dab, vb = gc.astype(BF16), da.astype(BF16), val_ref[:, cols]
            ge, u, th, z2 = _gelu(zb)
            dgb = dab * vb * _gelu_grad(zb, u, th, z2)
            dgc_ref[:, cols] = dgb
            dgc = dgb.astype(F32)
            dvb = dab * ge
            dval_ref[:, cols] = dvb
            dcb_ref[:, cols] += _colsum(dgc)
            dcw_ref[0:1, cols] += _colsum(dgc * g2)
            dcw_ref[1:2, cols] += _colsum(dgc * g1)
            dcw_ref[2:3, cols] += _colsum(dgc * gt)
            return dvb, ge * vb

        def mm_dw(i, dvb_ab):
            dvb, ab = dvb_ab
            ch, off, width = FF_CHUNKS[i]
            dwd_ref[col(i), :] += lax.dot_general(ab, dyb, TN, preferred_element_type=F32)
            dwu_ref[ch, :, off:off + width] += lax.dot_general(hb, dvb, TN, preferred_element_type=F32)

        n = len(FF_CHUNKS)
        da = mm_da(0)
        prev = None
        for i in range(n):
            nxt = mm_da(i + 1) if i + 1 < n else None
            if prev is not None:
                mm_dw(i - 1, prev)
            prev = elementwise(i, da)
            da = nxt
        mm_dw(n - 1, prev)

    return pl.pallas_call(
        body,
        name="down_bwd",
        grid=(S // tm,),
        in_specs=[_rows(tm, D_MODEL), _full((D_FF, D_MODEL)), _rows(tm, D_FF), _halo_prev(tm, D_FF), _rows(tm, D_FF),
                  _full((3, D_FF)), _full((1, D_FF)), _rows(tm, D_MODEL)],
        out_specs=[_rows(tm, D_FF), _rows(tm, D_FF), _full((3, D_FF)), _full((1, D_FF)), _full((D_FF, D_MODEL)),
                   pl.BlockSpec((2, D_MODEL, HALF_FF), lambda i: (1, 0, 0), pipeline_mode=pl.Buffered(1))],
        out_shape=[jax.ShapeDtypeStruct((S, D_FF), BF16), jax.ShapeDtypeStruct((S, D_FF), BF16),
                   jax.ShapeDtypeStruct((3, D_FF), F32), jax.ShapeDtypeStruct((1, D_FF), F32),
                   jax.ShapeDtypeStruct((D_FF, D_MODEL), F32), jax.ShapeDtypeStruct((N_CHIPS, D_MODEL, HALF_FF), F32)],
        compiler_params=_params(1),
    )(dy2, w_down, gate, gate, val, conv_w, conv_b, h2)


def up_bwd(dgc, dval, conv_w, w_up_g, x1, dout, y1, g_pre, g_post, mod6, h2, dw_up, tm=256):
    S = x1.shape[0]
    last_blk = S // 16 - 1

    def body(dgc_ref, nxt_ref, dval_ref, cw_ref, w_ref, x1_ref, dout_ref, y1_ref, gpre_ref, gpost_ref, mod_ref, h_ref, dwin_ref,
             dx1_ref, dy1_ref, dsh_ref, dsc_ref, dgpre_ref, dgt_ref, dgpost_ref, dwu_ref):
        last = pl.program_id(0) == pl.num_programs(0) - 1

        @pl.when(pl.program_id(0) == 0)
        def _():
            dwu_ref[...] = jnp.zeros_like(dwu_ref)

        hb = h_ref[...]
        dh = jnp.zeros((tm, D_MODEL), F32)
        for ch in range(2):
            cols = slice(ch * HALF_FF, (ch + 1) * HALF_FF)
            dg = dgc_ref[:, cols].astype(F32)
            nx = jnp.where(last, 0.0, nxt_ref[:, cols].astype(F32))
            row = lax.broadcasted_iota(jnp.int32, dg.shape, 0)
            n0, n1 = nx[0:1, :], nx[1:2, :]
            u1 = jnp.where(row == tm - 1, n0, pltpu.roll(dg, tm - 1, 0))
            u2 = jnp.where(row == tm - 1, n1, jnp.where(row == tm - 2, n0, pltpu.roll(dg, tm - 2, 0)))
            dgate = (dg * cw_ref[2:3, cols] + u1 * cw_ref[1:2, cols] + u2 * cw_ref[0:1, cols]).astype(BF16)
            dwu_ref[ch] += lax.dot_general(hb, dgate, TN, preferred_element_type=F32)
            dh = dh + lax.dot_general(dgate, w_ref[ch], NT, preferred_element_type=F32)
            dh = dh + lax.dot_general(dval_ref[:, cols], w_ref[2 + ch], NT, preferred_element_type=F32)
        x1 = x1_ref[...]
        rstd = lax.rsqrt(jnp.mean(x1 * x1, axis=-1, keepdims=True) + NORM_EPS)
        n2 = x1 * rstd
        gpre = gpre_ref[...]
        one_sc = 1.0 + mod_ref[4:5, :]
        _acc(dsh_ref, _colsum(dh))
        _acc(dsc_ref, _colsum(dh * (n2 * gpre)))
        _acc(dgpre_ref, _colsum(dh * one_sc * n2))
        dn = dh * (gpre * one_sc)
        dx1 = dout_ref[...] + rstd * (dn - n2 * jnp.mean(dn * n2, axis=-1, keepdims=True))
        dx1_ref[...] = dx1
        y1 = y1_ref[...]
        rstd1 = lax.rsqrt(jnp.mean(y1 * y1, axis=-1, keepdims=True) + NORM_EPS)
        y1n = y1 * rstd1
        gpost = gpost_ref[...]
        gtm = mod_ref[2:3, :]
        _acc(dgt_ref, _colsum(dx1 * (y1n * gpost)))
        dr1 = dx1 * gtm
        _acc(dgpost_ref, _colsum(dr1 * y1n))
        dyn = dr1 * gpost
        dy1 = rstd1 * (dyn - y1n * jnp.mean(dyn * y1n, axis=-1, keepdims=True))
        dy1_ref[...] = dy1.astype(BF16)

    vec = _full((1, D_MODEL))
    nxt = pl.BlockSpec((16, D_FF), lambda i: (jnp.minimum((i + 1) * (tm // 16), last_blk), 0))
    return pl.pallas_call(
        body,
        name="up_bwd",
        grid=(S // tm,),
        in_specs=[_rows(tm, D_FF), nxt, _rows(tm, D_FF), _full((3, D_FF)), _full(w_up_g.shape), _rows(tm, D_MODEL),
                  _rows(tm, D_MODEL), _rows(tm, D_MODEL), vec, vec, _full((6, D_MODEL)), _rows(tm, D_MODEL),
                  pl.BlockSpec(memory_space=pl.ANY)],
        out_specs=[_rows(tm, D_MODEL), _rows(tm, D_MODEL), vec, vec, vec, vec, vec,
                   pl.BlockSpec((2, D_MODEL, HALF_FF), lambda i: (0, 0, 0), pipeline_mode=pl.Buffered(1))],
        out_shape=[jax.ShapeDtypeStruct((S, D_MODEL), F32), jax.ShapeDtypeStruct((S, D_MODEL), BF16)]
        + [jax.ShapeDtypeStruct((1, D_MODEL), F32)] * 5 + [jax.ShapeDtypeStruct(dw_up.shape, F32)],
        input_output_aliases={12: 7},
        compiler_params=_params(1),
    )(dgc, dgc, dval, conv_w, w_up_g, x1, dout, y1, g_pre, g_post, mod6, h2, dw_up)


def outproj_bwd(dy1, w_out_g, cat, tm=512):
    S = dy1.shape[0]

    def body(dy_ref, w_ref, cat_ref, dpool_ref, dattn_ref, da4_ref, da16_ref, delta_ref, dl4_ref, dl16_ref, dw_ref, scr):
        @pl.when(pl.program_id(0) == 0)
        def _():
            dw_ref[...] = jnp.zeros_like(dw_ref)

        catb = cat_ref[...]
        dcat = jnp.zeros((tm, 512), F32)
        for j in range(N_CHIPS):
            dyj = dy_ref[:, j * 256:(j + 1) * 256]
            dcat = dcat + lax.dot_general(dyj, w_ref[j], NT, preferred_element_type=F32)
            dw_ref[j] += lax.dot_general(catb, dyj, TN, preferred_element_type=F32)
        dpool_ref[...] = dcat[:, :POOL_W]
        dattn = dcat[:, POOL_W:]
        dattn_ref[...] = dattn.astype(BF16)
        for h in range(2):
            scr[h] = dattn[:, h * 128:(h + 1) * 128]
        for d, dst in ((4, da4_ref), (16, da16_ref)):
            for r in range(d):
                for h in range(2):
                    dst[r, :, h * 128:(h + 1) * 128] = scr[h, pl.ds(r, tm // d, stride=d), :].astype(BF16)
        prod = dattn * catb[:, POOL_W:].astype(F32)
        r = lax.broadcasted_iota(jnp.int32, (GROUP_W, GROUP_W), 0) // HEAD_DIM
        c = lax.broadcasted_iota(jnp.int32, (GROUP_W, GROUP_W), 1) // HEAD_DIM
        ones_bd = jnp.where(r == c, 1.0, 0.0).astype(BF16)
        hi = prod.astype(BF16)
        lo = (prod - hi.astype(F32)).astype(BF16)
        delta = jnp.dot(hi, ones_bd, preferred_element_type=F32) + jnp.dot(lo, ones_bd, preferred_element_type=F32)
        delta_ref[...] = delta
        for h in range(2):
            scr[h] = delta[:, h * 128:(h + 1) * 128]
        for d, dst in ((4, dl4_ref), (16, dl16_ref)):
            for r in range(d):
                for h in range(2):
                    dst[r, :, h * 128:(h + 1) * 128] = scr[h, pl.ds(r, tm // d, stride=d), :]

    cls = lambda d: pl.BlockSpec((d, tm // d, GROUP_W), lambda i: (0, i, 0))
    cls_shape = lambda d, dt: jax.ShapeDtypeStruct((d, S // d, GROUP_W), dt)
    return pl.pallas_call(
        body,
        name="outproj_bwd",
        grid=(S // tm,),
        in_specs=[_rows(tm, D_MODEL), _full(w_out_g.shape), _rows(tm, 512)],
        out_specs=[_rows(tm, POOL_W), _rows(tm, GROUP_W), cls(4), cls(16), _rows(tm, GROUP_W), cls(4), cls(16),
                   _full(w_out_g.shape)],
        out_shape=[jax.ShapeDtypeStruct((S, POOL_W), F32), jax.ShapeDtypeStruct((S, GROUP_W), BF16), cls_shape(4, BF16),
                   cls_shape(16, BF16), jax.ShapeDtypeStruct((S, GROUP_W), F32), cls_shape(4, F32), cls_shape(16, F32),
                   jax.ShapeDtypeStruct(w_out_g.shape, F32)],
        scratch_shapes=[pltpu.VMEM((2, tm, 128), F32)],
        compiler_params=_params(1),
    )(dy1, w_out_g, cat)


def attn_bwd(qkv, dattn, lse, delta, d):
    L = qkv.shape[2]
    nb = L // ATT_BLOCK
    cpb = _classes_per_step(d, nb)

    def body(q_ref, k_ref, v_ref, do_ref, l_ref, dl_ref, out_ref, kpad, vpad, dkpad, dvpad):
        for cls in range(cpb):
            kpad[cls, 0:ATT_BLOCK, :] = jnp.zeros((ATT_BLOCK, GROUP_W), BF16)
            vpad[cls, 0:ATT_BLOCK, :] = jnp.zeros((ATT_BLOCK, GROUP_W), BF16)
            kpad[cls, ATT_BLOCK:, :] = k_ref[cls]
            vpad[cls, ATT_BLOCK:, :] = v_ref[cls]
        dkpad[...] = jnp.zeros_like(dkpad)
        dvpad[...] = jnp.zeros_like(dvpad)
        band, col, lo = _attn_masks()

        def step(t, carry):
            cls, n = t // nb, t % nb
            r0 = pl.multiple_of(n * ATT_BLOCK, ATT_BLOCK)
            valid = band & ((col >= ATT_BLOCK) | (n > 0))
            qb = q_ref[cls, pl.ds(r0, ATT_BLOCK), :]
            dob = do_ref[cls, pl.ds(r0, ATT_BLOCK), :]
            lb = l_ref[cls, pl.ds(r0, ATT_BLOCK), :]
            dlb = dl_ref[cls, pl.ds(r0, ATT_BLOCK), :]
            kb = kpad[cls, pl.ds(r0, 2 * ATT_BLOCK), :]
            vb = vpad[cls, pl.ds(r0, 2 * ATT_BLOCK), :]
            for pair in range(2):
                lanes = slice(pair * 128, (pair + 1) * 128)
                qp, dop, kp, vp = qb[:, lanes], dob[:, lanes], kb[:, lanes], vb[:, lanes]
                c0, c1 = pair * 128, pair * 128 + HEAD_DIM
                q2, do2 = _stack_heads(qp, lo), _stack_heads(dop, lo)
                lse2 = jnp.concatenate([lb[:, c0:c0 + 1], lb[:, c1:c1 + 1]], axis=0)
                dl2 = jnp.concatenate([dlb[:, c0:c0 + 1], dlb[:, c1:c1 + 1]], axis=0)
                s = lax.dot_general(q2, kp, NT, preferred_element_type=F32)
                s = jnp.where(valid, s, NEG)
                p = jnp.exp(s - lse2)
                dp = lax.dot_general(do2, vp, NT, preferred_element_type=F32)
                ds = (p * (dp - dl2)).astype(BF16)
                dq2 = jnp.dot(ds, kp, preferred_element_type=F32)
                out_ref[0, cls, pl.ds(r0, ATT_BLOCK), lanes] = _unstack_heads(dq2, lo)
                dkpad[cls, pl.ds(r0, 2 * ATT_BLOCK), lanes] += lax.dot_general(ds, q2, TN, preferred_element_type=F32)
                dvpad[cls, pl.ds(r0, 2 * ATT_BLOCK), lanes] += lax.dot_general(p.astype(BF16), do2, TN, preferred_element_type=F32)
            return carry

        lax.fori_loop(0, cpb * nb, step, 0, unroll=4)
        for cls in range(cpb):
            out_ref[1, cls] = dkpad[cls, ATT_BLOCK:, :]
            out_ref[2, cls] = dvpad[cls, ATT_BLOCK:, :]

    spec = lambda kind: pl.BlockSpec((None, cpb, L, GROUP_W), lambda r: (kind, r, 0, 0))
    per_cls = pl.BlockSpec((cpb, L, GROUP_W), lambda r: (r, 0, 0))
    return pl.pallas_call(
        body,
        name=f"attn_bwd_d{d}",
        grid=(d // cpb,),
        in_specs=[spec(0), spec(1), spec(2), per_cls, per_cls, per_cls],
        out_specs=pl.BlockSpec((3, cpb, L, GROUP_W), lambda r: (0, r, 0, 0)),
        out_shape=jax.ShapeDtypeStruct((3, d, L, GROUP_W), F32),
        scratch_shapes=[pltpu.VMEM((cpb, L + ATT_BLOCK, GROUP_W), BF16)] * 2 + [pltpu.VMEM((cpb, L + ATT_BLOCK, GROUP_W), F32)] * 2,
        compiler_params=_params(1),
    )(qkv, qkv, qkv, dattn, lse, delta)


def pool_bwd(dpool, mixed, wbd, b, scale):
    S = dpool.shape[0]

    def body(dp_ref, mx_ref, w_ref, b_ref, s_ref, du_ref, dw_ref, db_ref, ds_ref):
        dp = dp_ref[...]
        mb = mx_ref[...]
        wv = w_ref[...]
        ypre = jnp.dot(mb, wv, preferred_element_type=F32) + b_ref[...]
        ds_ref[...] = _colsum(dp * ypre)
        dpre = dp * s_ref[...]
        db_ref[...] = _colsum(dpre)
        dpb = dpre.astype(BF16)
        dw_ref[...] = lax.dot_general(mb, dpb, TN, preferred_element_type=F32)
        dmix = lax.dot_general(dpb, wv, NT, preferred_element_type=F32)
        row = lax.broadcasted_iota(jnp.int32, dmix.shape, 0)
        lane, win = _pool_lane_windows(dmix.shape)
        e = dmix / jnp.minimum(row + 1, win).astype(F32)

        def shift(a, k):
            return jnp.where(row < S - k, pltpu.roll(a, S - k, 0), 0.0)

        f2 = e + shift(e, 1)
        f4 = f2 + shift(f2, 2)
        f8 = f4 + shift(f4, 4)
        f16 = f8 + shift(f8, 8)
        du_ref[...] = jnp.where(lane < 64, f2, jnp.where(lane < 128, f4, jnp.where(lane < 192, f8, f16))) - dmix

    vm = pl.BlockSpec(memory_space=pltpu.VMEM)
    return pl.pallas_call(
        body,
        name="pool_bwd",
        in_specs=[vm] * 5,
        out_specs=[vm] * 4,
        out_shape=[jax.ShapeDtypeStruct((S, POOL_W), F32), jax.ShapeDtypeStruct((POOL_W, POOL_W), F32),
                   jax.ShapeDtypeStruct((1, POOL_W), F32), jax.ShapeDtypeStruct((1, POOL_W), F32)],
        compiler_params=_params(),
    )(dpool, mixed, wbd, b, scale)


def inproj_bwd(dqkv, du, x, dx1, w_in_g, g, mod6, tc, tsa, tsb, h1, tm=512):
    S = x.shape[0]

    def body(d0, d1, d2, du_ref, x_ref, dx1_ref, w_ref, g_ref, mod_ref, tc_ref, tsa_ref, tsb_ref, h_ref,
             gx_ref, dsh_ref, dsc_ref, dg_ref, dw_ref, s4, s16, dp_ref):
        @pl.when(pl.program_id(0) == 0)
        def _():
            dw_ref[...] = jnp.zeros_like(dw_ref)

        cs, sa, sb = tc_ref[...], tsa_ref[...], tsb_ref[...]
        for d, src, dst in ((4, d1, s4), (16, d2, s16)):
            for kind in range(3):
                for r in range(d):
                    for h in range(2):
                        dst[kind, h, pl.ds(r, tm // d, stride=d), :] = src[kind, r, :, h * 128:(h + 1) * 128]
        for sp in range(20):
            piece, half = sp // 2, sp % 2
            lanes = slice(half * 128, (half + 1) * 128)
            if piece == 0:
                blk = du_ref[:, lanes]
            else:
                kind, gi = (piece - 1) // 3, (piece - 1) % 3
                blk = d0[kind, 0, :, lanes] if gi == 0 else (s4, s16)[gi - 1][kind, half]
                if kind == 0:
                    blk = _rope128(blk, cs, sa, sb, -1.0) * (HEAD_DIM ** -0.5)
                elif kind == 1:
                    blk = _rope128(blk, cs, sa, sb, -1.0)
            dp_ref[:, sp * 128:(sp + 1) * 128] = blk.astype(BF16)
        dh = jnp.zeros((tm, D_MODEL), F32)
        hbt = h_ref[...].T
        for j in range(N_CHIPS):
            dpj = dp_ref[:, j * 640:(j + 1) * 640]
            dh = dh + lax.dot_general(dpj, w_ref[j], NT, preferred_element_type=F32)
            dw_ref[j] += jnp.dot(hbt, dpj, preferred_element_type=F32)
        xv = x_ref[...]
        rstd = lax.rsqrt(jnp.mean(xv * xv, axis=-1, keepdims=True) + NORM_EPS)
        n1 = xv * rstd
        gv = g_ref[...]
        one_sc = 1.0 + mod_ref[1:2, :]
        _acc(dsh_ref, _colsum(dh))
        _acc(dsc_ref, _colsum(dh * (n1 * gv)))
        _acc(dg_ref, _colsum(dh * one_sc * n1))
        dn = dh * (gv * one_sc)
        gx_ref[...] = dx1_ref[...] + rstd * (dn - n1 * jnp.mean(dn * n1, axis=-1, keepdims=True))

    vec = _full((1, D_MODEL))
    dspec = lambda d: pl.BlockSpec((3, d, tm // d, GROUP_W), lambda i: (0, 0, i, 0))
    return pl.pallas_call(
        body,
        name="inproj_bwd",
        grid=(S // tm,),
        in_specs=[dspec(d) for d in DILATIONS] + [_rows(tm, POOL_W), _rows(tm, D_MODEL), _rows(tm, D_MODEL), _full(w_in_g.shape),
                                                  vec, _full((6, D_MODEL)), _rows(tm, 128), _rows(tm, 128), _rows(tm, 128),
                                                  _rows(tm, D_MODEL)],
        out_specs=[_rows(tm, D_MODEL), vec, vec, vec, _full(w_in_g.shape)],
        out_shape=[jax.ShapeDtypeStruct((S, D_MODEL), F32)] + [jax.ShapeDtypeStruct((1, D_MODEL), F32)] * 3
        + [jax.ShapeDtypeStruct(w_in_g.shape, F32)],
        scratch_shapes=[pltpu.VMEM((3, 2, tm, 128), F32)] * 2 + [pltpu.VMEM((tm, IN_W), BF16)],
        compiler_params=_params(1),
    )(*dqkv, du, x, dx1, w_in_g, g, mod6, tc, tsa, tsb, h1)


def _adamw(w, g, m, v):
    m = ADAM_B1 * m + (1.0 - ADAM_B1) * g
    v = ADAM_B2 * v + (1.0 - ADAM_B2) * (g * g)
    m_hat = m / (1.0 - ADAM_B1 ** ADAM_STEP)
    v_hat = v / (1.0 - ADAM_B2 ** ADAM_STEP)
    delta = -ADAM_LR * (m_hat / (jnp.sqrt(v_hat) + ADAM_EPS) + ADAM_WD * w)
    return delta, m, v


def adamw_rows(w, g, m, v, tr, name):
    R, C = w.shape

    def body(w_ref, g_ref, m_ref, v_ref, go_ref, d_ref, mo_ref, vo_ref):
        g = g_ref[...]
        go_ref[...] = g
        d_ref[...], mo_ref[...], vo_ref[...] = _adamw(w_ref[...], g, m_ref[...], v_ref[...])

    spec = pl.BlockSpec((tr, C), lambda i: (i, 0))
    return pl.pallas_call(
        body,
        name=name,
        grid=(R // tr,),
        in_specs=[spec] * 4,
        out_specs=[spec] * 4,
        out_shape=[jax.ShapeDtypeStruct((R, C), F32)] * 4,
        compiler_params=_params(1),
    )(w, g, m, v)


def adamw_ada(c_all_t, dmod_cols, w, m, v, tr=256):
    R, C = w.shape

    def body(ct_ref, dm_ref, w_ref, m_ref, v_ref, g_ref, d_ref, mo_ref, vo_ref):
        ct = ct_ref[...]
        act = ct * jax.nn.sigmoid(ct)
        dm = dm_ref[...]
        a_hi, d_hi = act.astype(BF16), dm.astype(BF16)
        a_lo, d_lo = (act - a_hi.astype(F32)).astype(BF16), (dm - d_hi.astype(F32)).astype(BF16)
        g = (jnp.dot(a_hi, d_hi, preferred_element_type=F32) + jnp.dot(a_lo, d_hi, preferred_element_type=F32)
             + jnp.dot(a_hi, d_lo, preferred_element_type=F32))
        g_ref[...] = g
        d_ref[...], mo_ref[...], vo_ref[...] = _adamw(w_ref[...], g, m_ref[...], v_ref[...])

    spec = pl.BlockSpec((tr, C), lambda i: (i, 0))
    return pl.pallas_call(
        body,
        name="adamw_ada",
        grid=(R // tr,),
        in_specs=[pl.BlockSpec((tr, N_DEV), lambda i: (i, 0)), _full((N_DEV, C)), spec, spec, spec],
        out_specs=[spec] * 4,
        out_shape=[jax.ShapeDtypeStruct((R, C), F32)] * 4,
        compiler_params=_params(1),
    )(c_all_t, dmod_cols, w, m, v)


def adamw_small(slab_a, slab_b, convw_g, wpool_g, params):
    names = ["b_ada", "g_pre_mix", "g_post_mix", "g_pre_ffn", "g_post_ffn", "b_pool", "pool_scale", "conv_b", "conv_w", "w_pool"]
    flat = []
    for n in names:
        flat += list(params[n])

    def body(a_ref, b_ref, cw_ref, wp_ref, *rest):
        ins, outs = rest[:30], rest[30:]

        def dev_sum(ref):
            t = ref[0]
            for dev in range(1, N_DEV):
                t = t + ref[dev]
            return t

        sa, sb_, scw, swp = dev_sum(a_ref), dev_sum(b_ref), dev_sum(cw_ref), dev_sum(wp_ref)
        grads = [
            jnp.concatenate([sa[k:k + 1, :] for k in range(6)], axis=1),
            sa[6:7, :], sa[7:8, :], sa[8:9, :], sa[9:10, :],
            sa[10:11, 0:256], sa[10:11, 256:512],
            sb_[3:4, :], scw, swp,
        ]
        for i, g in enumerate(grads):
            w_ref, m_ref, v_ref = ins[3 * i:3 * i + 3]
            if names[i] == "b_pool":
                parts = [((0, slice(grp, grp + 1)), g[:, grp * 64:(grp + 1) * 64]) for grp in range(4)]
            elif names[i] == "w_pool":
                parts = [((0, grp), g[grp * 64:(grp + 1) * 64, :]) for grp in range(4)]
            elif names[i] == "conv_w":
                parts = [((0,), g)]
            else:
                parts = [((Ellipsis,), g)]
            for at, gp in parts:
                d, mo, vo = _adamw(w_ref[at], gp, m_ref[at], v_ref[at])
                for k, val in enumerate((gp, d, mo, vo)):
                    outs[4 * i + k][at] = val
        outs[-1][...] = sa[10:11, 512:640]

    vm = pl.BlockSpec(memory_space=pltpu.VMEM)
    out_shape = []
    for n in names:
        out_shape += [jax.ShapeDtypeStruct(params[n][0].shape, F32)] * 4
    out_shape.append(jax.ShapeDtypeStruct((1, 128), F32))
    outs = pl.pallas_call(
        body,
        name="adamw_small",
        in_specs=[vm] * (4 + len(flat)),
        out_specs=[vm] * len(out_shape),
        out_shape=out_shape,
        compiler_params=_params(),
    )(slab_a, slab_b, convw_g, wpool_g, *flat)
    return {n: outs[4 * i:4 * i + 4] for i, n in enumerate(names)}, outs[-1]


def _place():
    return lax.axis_index("x"), lax.axis_index("y"), lax.axis_index("c")


def _other_chips(x, y):
    return [(1 - x, y), (x, 1 - y), (1 - x, 1 - y)]


def _chip_id(cx, cy):
    return 2 * cx + cy


HBM_SPEC = pl.BlockSpec(memory_space=pltpu.HBM)
SEM_SPEC = pl.BlockSpec(memory_space=pltpu.SEMAPHORE)
ANY_SPEC = pl.BlockSpec(memory_space=pl.ANY)
EFFECT = pltpu.SideEffectType.DATAFLOW_SIDE_EFFECTING


def _hbm(t):
    return pltpu.with_memory_space_constraint(t, pltpu.HBM)


def _hbm_shapes(ts):
    return [pltpu.HBM(t.shape, t.dtype) for t in ts]


def _half_rows(ref, lead, half, rh):
    return ref.at[lead, pl.ds(half * rh, rh), :]


def _flips():
    return [(fx, fy, fc) for fx in (0, 1) for fy in (0, 1) for fc in (0, 1)][1:]


def _flip(v, f):
    return v if f == 0 else 1 - v


def ada_mod(c3, w_ada, b_cols, conv_w):
    CB = w_ada.shape[1]

    def body(c_ref, w_ref, b_ref, cw_ref, call_ref, mod_ref, cwall_ref, modall, send_sems, recv_sems):
        x, y, c = _place()
        me_dev = 4 * x + 2 * y + c
        me = _chip_id(x, y)
        call_ref[me_dev] = c_ref[0]
        cwall_ref[me] = cw_ref[...]
        sends = []
        for k, (cx, cy) in enumerate(_other_chips(x, y)):
            cp = pltpu.make_async_remote_copy(src_ref=cw_ref, dst_ref=cwall_ref.at[me], send_sem=send_sems.at[10 + k],
                                              recv_sem=recv_sems.at[10 + k], device_id=(cx, cy, c), device_id_type=MESH)
            cp.start()
            sends.append(cp)
        for k, (fx, fy, fc) in enumerate(_flips()):
            cp = pltpu.make_async_remote_copy(src_ref=c_ref.at[0], dst_ref=call_ref.at[me_dev], send_sem=send_sems.at[k],
                                              recv_sem=recv_sems.at[k],
                                              device_id=(_flip(x, fx), _flip(y, fy), _flip(c, fc)), device_id_type=MESH)
            cp.start()
            sends.append(cp)
        for k, (fx, fy, fc) in enumerate(_flips()):
            peer = 4 * _flip(x, fx) + 2 * _flip(y, fy) + _flip(c, fc)
            pltpu.make_async_remote_copy(src_ref=c_ref.at[0], dst_ref=call_ref.at[peer], send_sem=send_sems.at[k],
                                         recv_sem=recv_sems.at[k], device_id=(x, y, c), device_id_type=MESH).wait_recv()
        row = lax.broadcasted_iota(jnp.int32, (N_DEV, D_MODEL), 0)
        call = jnp.zeros((N_DEV, D_MODEL), F32)
        for dev in range(N_DEV):
            call = jnp.where(row == dev, call_ref[dev], call)
        act = call * jax.nn.sigmoid(call)
        wv = w_ref[...]
        w_hi = wv.astype(BF16)
        w_lo = (wv - w_hi.astype(F32)).astype(BF16)
        a_hi = act.astype(BF16)
        a_lo = (act - a_hi.astype(F32)).astype(BF16)
        prod = (jnp.dot(a_hi, w_hi, preferred_element_type=F32) + jnp.dot(a_lo, w_hi, preferred_element_type=F32)
                + jnp.dot(a_hi, w_lo, preferred_element_type=F32))
        modall[me] = prod + b_ref[...]
        for k, (cx, cy) in enumerate(_other_chips(x, y)):
            cp = pltpu.make_async_remote_copy(src_ref=modall.at[me], dst_ref=modall.at[me], send_sem=send_sems.at[7 + k],
                                              recv_sem=recv_sems.at[7 + k], device_id=(cx, cy, c), device_id_type=MESH)
            cp.start()
            sends.append(cp)
        for k, (cx, cy) in enumerate(_other_chips(x, y)):
            blk = modall.at[_chip_id(cx, cy)]
            pltpu.make_async_remote_copy(src_ref=blk, dst_ref=blk, send_sem=send_sems.at[7 + k], recv_sem=recv_sems.at[7 + k],
                                         device_id=(x, y, c), device_id_type=MESH).wait_recv()
        for k, (cx, cy) in enumerate(_other_chips(x, y)):
            blk = cwall_ref.at[_chip_id(cx, cy)]
            pltpu.make_async_remote_copy(src_ref=blk, dst_ref=blk, send_sem=send_sems.at[10 + k], recv_sem=recv_sems.at[10 + k],
                                         device_id=(x, y, c), device_id_type=MESH).wait_recv()
        for cp in sends:
            cp.wait_send()
        mine = [modall[j, pl.ds(me_dev, 1), :] for j in range(N_CHIPS)]
        for r in range(6):
            pieces = []
            for h in range(2):
                pos = r * D_MODEL + h * 512
                pieces.append(mine[pos // CB][:, pos % CB:pos % CB + 512])
            mod_ref[r:r + 1, :] = jnp.concatenate(pieces, axis=1)

    vm = pl.BlockSpec(memory_space=pltpu.VMEM)
    return pl.pallas_call(
        body,
        name="ada_mod",
        in_specs=[vm] * 4,
        out_specs=[vm] * 3,
        out_shape=[jax.ShapeDtypeStruct((N_DEV, 1, D_MODEL), F32), jax.ShapeDtypeStruct((6, D_MODEL), F32),
                   jax.ShapeDtypeStruct((N_CHIPS,) + conv_w.shape, F32)],
        scratch_shapes=[pltpu.VMEM((N_CHIPS, N_DEV, CB), F32), pltpu.SemaphoreType.DMA((13,)), pltpu.SemaphoreType.DMA((13,))],
        compiler_params=pltpu.CompilerParams(has_side_effects=True, vmem_limit_bytes=VMEM_LIMIT),
    )(c3, w_ada, b_cols, conv_w)


def split_start(name, bufs, plan, n_sem, carry):
    nb = len(bufs)
    many = isinstance(carry, (list, tuple))
    alls = list(bufs) + (list(carry) if many else [carry])
    na = len(alls)

    def body(*refs):
        x, y, c = _place()
        ssem, rsem = refs[na], refs[na + 1]
        for i, (src, dst, dev) in enumerate(plan(refs[:nb], x, y, c)):
            pltpu.make_async_remote_copy(src_ref=src, dst_ref=dst, send_sem=ssem.at[i], recv_sem=rsem.at[i], device_id=dev,
                                         device_id_type=MESH).start()

    outs = pl.pallas_call(
        body,
        name=name,
        out_shape=[pltpu.SemaphoreType.DMA((n_sem,)), pltpu.SemaphoreType.DMA((n_sem,))] + _hbm_shapes(alls),
        in_specs=[HBM_SPEC] * na,
        out_specs=[SEM_SPEC, SEM_SPEC] + [HBM_SPEC] * na,
        input_output_aliases={i: 2 + i for i in range(na)},
        compiler_params=pltpu.CompilerParams(has_side_effects=EFFECT),
    )(*[_hbm(t) for t in alls])
    return outs[0], outs[1], list(outs[2:2 + nb]), (list(outs[2 + nb:]) if many else outs[-1])


def split_wait(name, ssem, rsem, bufs, plan, after):
    nb = len(bufs)

    def body(*refs):
        x, y, c = _place()
        s_ref, r_ref = refs[nb], refs[nb + 1]
        for i, (src, dst, dev) in enumerate(plan(refs[:nb], x, y, c)):
            cp = pltpu.make_async_remote_copy(src_ref=src, dst_ref=dst, send_sem=s_ref.at[i], recv_sem=r_ref.at[i], device_id=dev,
                                              device_id_type=MESH)
            cp.wait_send()
            cp.wait_recv()

    outs = pl.pallas_call(
        body,
        name=name,
        out_shape=_hbm_shapes(bufs),
        in_specs=[HBM_SPEC] * nb + [SEM_SPEC, SEM_SPEC, ANY_SPEC],
        out_specs=[HBM_SPEC] * nb,
        input_output_aliases={i: i for i in range(nb)},
        compiler_params=pltpu.CompilerParams(has_side_effects=EFFECT),
    )(*bufs, ssem, rsem, after)
    return list(outs)


def _gather_ici_plan(n):
    def plan(refs, x, y, c):
        out = []
        for w in range(n):
            rh = refs[w].shape[0] // 2
            for cx, cy in _other_chips(x, y):
                out.append((refs[w].at[pl.ds(c * rh, rh), :], _half_rows(refs[n + w], _chip_id(x, y), c, rh), (cx, cy, c)))
        return out

    return plan


def _gather_d2d_plan(n):
    def plan(refs, x, y, c):
        out = []
        for w in range(n):
            rh = refs[w].shape[1] // 2
            for cx, cy in _other_chips(x, y):
                blk = _half_rows(refs[w], _chip_id(cx, cy), c, rh)
                out.append((blk, blk, (x, y, 1 - c)))
        return out

    return plan


def _dev_id(x, y, c):
    return 4 * x + 2 * y + c


def _small_ici_plan(n):
    def plan(refs, x, y, c):
        out = []
        for w in range(n):
            dst = refs[n + w].at[_dev_id(x, y, c)]
            out.append((refs[w], dst, (x, y, 1 - c)))
            for cx, cy in _other_chips(x, y):
                out.append((refs[w], dst, (cx, cy, c)))
        return out

    return plan


def _small_d2d_plan(n):
    def plan(refs, x, y, c):
        out = []
        for w in range(n):
            for cx, cy in _other_chips(x, y):
                blk = refs[w].at[_dev_id(cx, cy, c)]
                out.append((blk, blk, (x, y, 1 - c)))
        return out

    return plan


def _rs_d2d_plan(n):
    def plan(refs, x, y, c):
        out = []
        for w in range(n):
            rh = refs[w].shape[1] // 2
            out.append((refs[w].at[:, pl.ds((1 - c) * rh, rh), :], refs[n + w], (x, y, 1 - c)))
        return out

    return plan


def _rs_ici_plan(n):
    def plan(refs, x, y, c):
        out = []
        for w in range(n):
            for k, (cx, cy) in enumerate(_other_chips(x, y)):
                out.append((refs[w].at[_chip_id(cx, cy)], refs[n + w].at[k], (cx, cy, c)))
        return out

    return plan


def _rs_share_plan(n):
    def plan(refs, x, y, c):
        out = []
        for w in range(n):
            rh = refs[w].shape[0] // 2
            rows = refs[w].at[pl.ds(c * rh, rh), :]
            out.append((rows, rows, (x, y, 1 - c)))
        return out

    return plan


def rs_add(grad, sibbuf, place, tr, name):
    _, R, C = grad.shape
    nt = (R // 2) // tr

    def body(p_ref, g_ref, s_ref, o_ref):
        o_ref[...] = (g_ref[...] + s_ref[...]).astype(BF16)

    return pl.pallas_call(
        body,
        name=name,
        grid_spec=pltpu.PrefetchScalarGridSpec(
            num_scalar_prefetch=1,
            grid=(N_CHIPS, nt),
            in_specs=[pl.BlockSpec((None, tr, C), lambda j, i, p: (j, p[0] * nt + i, 0)),
                      pl.BlockSpec((None, tr, C), lambda j, i, p: (j, i, 0))],
            out_specs=pl.BlockSpec((None, tr, C), lambda j, i, p: (j, i, 0)),
        ),
        out_shape=jax.ShapeDtypeStruct((N_CHIPS, R // 2, C), BF16),
        compiler_params=_params(2),
    )(place, grad, sibbuf)


def rs_final(grad, sibbuf, rbuf, place, tr, name):
    _, R, C = grad.shape
    nt = (R // 2) // tr

    def body(p_ref, g_ref, s_ref, r_ref, o_ref):
        o_ref[...] = (((g_ref[...] + s_ref[...]) + r_ref[0].astype(F32)) + r_ref[1].astype(F32)) + r_ref[2].astype(F32)

    return pl.pallas_call(
        body,
        name=name,
        grid_spec=pltpu.PrefetchScalarGridSpec(
            num_scalar_prefetch=1,
            grid=(nt,),
            in_specs=[pl.BlockSpec((None, tr, C), lambda i, p: (p[1], p[0] * nt + i, 0)),
                      pl.BlockSpec((None, tr, C), lambda i, p: (p[1], i, 0)),
                      pl.BlockSpec((3, tr, C), lambda i, p: (0, i, 0))],
            out_specs=pl.BlockSpec((tr, C), lambda i, p: (p[0] * nt + i, 0)),
        ),
        out_shape=jax.ShapeDtypeStruct((R, C), F32),
        compiler_params=_params(1),
    )(place, grad, sibbuf, rbuf)


class GradReduce:
    def __init__(self, tag, grads, rows, place):
        self.tag, self.grads, self.rows, self.place = tag, grads, rows, place
        self.n = len(grads)

    def d2d_start(self, carry):
        sib = [lax.empty((N_CHIPS, g.shape[1] // 2, g.shape[2]), F32) for g in self.grads]
        self.s1, self.r1, bufs, carry = split_start(f"rs_{self.tag}_d2d_start", self.grads + sib, _rs_d2d_plan(self.n), self.n, carry)
        self.bufs1 = bufs
        return carry

    def add_and_ici_start(self, after, carry):
        bufs = split_wait(f"rs_{self.tag}_d2d_wait", self.s1, self.r1, self.bufs1, _rs_d2d_plan(self.n), after)
        self.grads, self.sib = bufs[:self.n], bufs[self.n:]
        pb = [rs_add(g, s, self.place, tr, f"rs_{self.tag}_add{w}")
              for w, (g, s, tr) in enumerate(zip(self.grads, self.sib, self.rows))]
        rb = [lax.empty((3,) + p.shape[1:], BF16) for p in pb]
        self.s2, self.r2, self.bufs2, carry = split_start(f"rs_{self.tag}_ici_start", pb + rb, _rs_ici_plan(self.n), 3 * self.n, carry)
        return carry

    def final_and_share_start(self, after, carry):
        bufs = split_wait(f"rs_{self.tag}_ici_wait", self.s2, self.r2, self.bufs2, _rs_ici_plan(self.n), after)
        rb = bufs[self.n:]
        full = [rs_final(g, s, r, self.place, tr, f"rs_{self.tag}_final{w}")
                for w, (g, s, r, tr) in enumerate(zip(self.grads, self.sib, rb, self.rows))]
        self.s3, self.r3, self.bufs3, carry = split_start(f"rs_{self.tag}_share_start", full, _rs_share_plan(self.n), self.n, carry)
        return carry

    def finish(self, after):
        return split_wait(f"rs_{self.tag}_share_wait", self.s3, self.r3, self.bufs3, _rs_share_plan(self.n), after)


def _rope_tables(positions):
    inv_freq = ROPE_THETA ** (-jnp.arange(0, ROT_DIM, 2, dtype=F32) / ROT_DIM)
    ang = positions.astype(F32)[:, None] * inv_freq
    cos, sin = jnp.cos(ang), jnp.sin(ang)
    S = positions.shape[0]
    one, zero = jnp.ones((S, 48), F32), jnp.zeros((S, 48), F32)
    z8 = jnp.zeros((S, 8), F32)
    tc = jnp.concatenate([cos, cos, one], axis=1)
    tsa = jnp.concatenate([z8, sin, zero], axis=1)
    tsb = jnp.concatenate([-sin, z8, zero], axis=1)
    return tuple(jnp.tile(t, (1, 2)) for t in (tc, tsa, tsb))


def _block_diag(w_pool):
    wbd = jnp.zeros((POOL_W, POOL_W), F32)
    for gi in range(4):
        wbd = wbd.at[gi * 64:(gi + 1) * 64, gi * 64:(gi + 1) * 64].set(w_pool[gi])
    return wbd


def kernel(x, c, positions, w_ada, b_ada, g_pre_mix, g_post_mix, g_pre_ffn, g_post_ffn, w_in, w_pool, b_pool, pool_scale, w_out, w_up, conv_w, conv_b, w_down, loss_target, m_w_ada, m_b_ada, m_g_pre_mix, m_g_post_mix, m_g_pre_ffn, m_g_post_ffn, m_w_in, m_w_pool, m_b_pool, m_pool_scale, m_w_out, m_w_up, m_conv_w, m_conv_b, m_w_down, v_w_ada, v_b_ada, v_g_pre_mix, v_g_post_mix, v_g_pre_ffn, v_g_post_ffn, v_w_in, v_w_pool, v_b_pool, v_pool_scale, v_w_out, v_w_up, v_conv_w, v_conv_b, v_w_down):
    xi, yi, ci = lax.axis_index("x"), lax.axis_index("y"), lax.axis_index("c")
    chip = 2 * xi + yi
    place = jnp.stack([ci, chip]).astype(jnp.int32)
    x2, tgt = x[0], loss_target[0]
    S = x2.shape[0]

    def landing(s_):
        return lax.dynamic_update_slice(lax.empty((N_CHIPS,) + s_.shape, s_.dtype), s_[None], (chip, 0, 0))

    cb_ada = w_ada.shape[2]
    b_cols = lax.dynamic_slice(b_ada, (0, chip * cb_ada), (1, cb_ada))
    c_all, mod6, conv_w_g = ada_mod(c.reshape(1, 1, D_MODEL), w_ada[0], b_cols, conv_w[0])
    conv_w_f = jnp.transpose(conv_w_g, (1, 0, 2)).reshape(3, D_FF)
    mix_sh = [w_in[0].astype(BF16), w_out[0].astype(BF16)]
    ffn_sh = [w_up[0].astype(BF16), w_down[0].astype(BF16)]
    ga_s, ga_r, ga_bufs, mod6 = split_start("gather_mix_ici_start", mix_sh + [landing(t) for t in mix_sh], _gather_ici_plan(2), 6, mod6)
    gb_s, gb_r, gb_bufs, (mod6, tc, tsa, tsb) = split_start("gather_ffn_ici_start", ffn_sh + [landing(t) for t in ffn_sh],
                                                            _gather_ici_plan(2), 6, [mod6, *_rope_tables(positions[0])])
    wbd = _block_diag(w_pool[0]).astype(BF16)
    b_pool2, scale2 = b_pool.reshape(1, POOL_W), pool_scale
    ga_bufs = split_wait("gather_mix_ici_wait", ga_s, ga_r, ga_bufs, _gather_ici_plan(2), mod6)
    gc_s, gc_r, mix_land, mod6 = split_start("gather_mix_d2d_start", ga_bufs[2:], _gather_d2d_plan(2), 6, mod6)
    w_in_g, w_out_g = split_wait("gather_mix_d2d_wait", gc_s, gc_r, mix_land, _gather_d2d_plan(2), mod6)

    h1, u, *qkv = inproj_fwd(x2, g_pre_mix, mod6, w_in_g, tc, tsa, tsb)
    mixed, pool = pool_fwd(u, wbd, b_pool2, scale2)
    o_l = [attn_fwd(t, d) for t, d in zip(qkv, DILATIONS)]
    attn_done = sum(l[0, :8, :128] for _, l in o_l)
    gb_bufs = split_wait("gather_ffn_ici_wait", gb_s, gb_r, gb_bufs, _gather_ici_plan(2), attn_done)
    gd_s, gd_r, ffn_land, pool = split_start("gather_ffn_d2d_start", gb_bufs[2:], _gather_d2d_plan(2), 6, pool)
    cat, lse, lse4, lse16, y1, x1, h2 = outproj_fwd([o for o, _ in o_l] + [l for _, l in o_l], pool, x2, w_out_g, g_post_mix,
                                                    g_pre_ffn, mod6)
    lses = [lse[None], lse4, lse16]
    w_up_g, w_down_g = split_wait("gather_ffn_d2d_wait", gd_s, gd_r, ffn_land, _gather_d2d_plan(2), h2)
    w_down_f = w_down_g.reshape(D_FF, D_MODEL)
    gate, val, dy2, dout, loss_v, d_gt_f, d_g_post_ffn = ffn_fwd(h2, w_up_g, conv_w_f, conv_b, w_down_f, x1, tgt, g_post_ffn, mod6)

    dgc, dval, d_conv_w, d_conv_b, dw_down, dw_up = down_bwd(dy2, w_down_f, gate, val, conv_w_f, conv_b, h2)
    dx1, dy1, d_sh_f, d_sc_f, d_g_pre_ffn, d_gt_m, d_g_post_mix, dw_up = up_bwd(
        dgc, dval, conv_w_f, w_up_g, x1, dout, y1, g_pre_ffn, g_post_mix, mod6, h2, dw_up)
    rs_ffn = GradReduce("ffn", [dw_up, dw_down.reshape(N_CHIPS, D_FF // N_CHIPS, D_MODEL)], [256, 176], place)
    dy1 = rs_ffn.d2d_start(dy1)
    dpool, da1, da4, da16, dl1, dl4, dl16, dw_out = outproj_bwd(dy1, w_out_g, cat)
    dpool = rs_ffn.add_and_ici_start(dw_out, dpool)
    du, d_wbd, d_b_pool, d_scale = pool_bwd(dpool, mixed, wbd, b_pool2, scale2)
    dqkv = [attn_bwd(t, da, ls, dl, d) for t, da, ls, dl, d in zip(qkv, (da1[None], da4, da16), lses, (dl1[None], dl4, dl16), DILATIONS)]
    grad_x, d_sh_m, d_sc_m, d_g_pre_mix, dw_in = inproj_bwd(dqkv, du, x2, dx1, w_in_g, g_pre_mix, mod6, tc, tsa, tsb, h1)

    z1 = jnp.zeros((1, D_MODEL), F32)
    slab_a = jnp.concatenate(
        [d_sh_m, d_sc_m, d_gt_m, d_sh_f, d_sc_f, d_gt_f, d_g_pre_mix, d_g_post_mix, d_g_pre_ffn, d_g_post_ffn,
         jnp.concatenate([d_b_pool, d_scale, loss_v, jnp.zeros((1, 384), F32)], axis=1)] + [z1] * 5, axis=0)
    slab_b = jnp.concatenate([d_conv_w, d_conv_b, jnp.zeros((4, D_FF), F32)], axis=0)
    d_wpool = jnp.concatenate([d_wbd[gi * 64:(gi + 1) * 64, gi * 64:(gi + 1) * 64] for gi in range(4)], axis=0)
    dev = _dev_id(xi, yi, ci)
    small_src = [slab_a, slab_b, d_wpool]
    small_land = [lax.dynamic_update_slice(lax.empty((N_DEV,) + t.shape, F32), t[None], (dev, 0, 0)) for t in small_src]
    tok = jnp.zeros((8, 128), F32)
    gs_s, gs_r, gs_bufs, tok = split_start("small_ici_start", small_src + small_land, _small_ici_plan(3), 12, tok)
    rs_mix = GradReduce("mix", [dw_in, dw_out], [256, 256], place)
    tok = rs_mix.d2d_start(tok)
    tok = rs_ffn.final_and_share_start(tok, tok)
    gs_bufs = split_wait("small_ici_wait", gs_s, gs_r, gs_bufs, _small_ici_plan(3), tok)
    gt_s, gt_r, small_land, tok = split_start("small_d2d_start", gs_bufs[3:], _small_d2d_plan(3), 9, tok)
    tok = rs_mix.add_and_ici_start(tok, tok)
    slab_a_g, slab_b_g, wpool_g = split_wait("small_d2d_wait", gt_s, gt_r, small_land, _small_d2d_plan(3), tok)
    cw_cols = conv_w.shape[2]
    convw_g = lax.dynamic_slice(slab_b_g, (0, 0, chip * cw_cols), (N_DEV, 3, cw_cols))
    dmod_cols = lax.dynamic_slice(slab_a_g[:, :6, :].reshape(N_DEV, 6 * D_MODEL), (0, chip * cb_ada), (N_DEV, cb_ada))

    res = {}

    def big_adamw(name, w, g, m, v, tr):
        g_, d_, m_, v_ = adamw_rows(w[0], g, m[0], v[0], tr, "adamw_" + name)
        res[name] = (g_[None], d_[None], m_[None], v_[None])
        return v_

    g_ada, d_ada, m_ada, v_ada = adamw_ada(c_all.reshape(N_DEV, D_MODEL).T, dmod_cols, w_ada[0], m_w_ada[0], v_w_ada[0])
    res["w_ada"] = (g_ada[None], d_ada[None], m_ada[None], v_ada[None])
    g_w_up, g_w_down = rs_ffn.finish(v_ada)
    big_adamw("w_up", w_up, g_w_up, m_w_up, v_w_up, 256)
    last = big_adamw("w_down", w_down, g_w_down, m_w_down, v_w_down, 352)
    rs_mix.final_and_share_start(last, jnp.zeros((8, 128), F32))
    g_w_in, g_w_out = rs_mix.finish(last)
    big_adamw("w_in", w_in, g_w_in, m_w_in, v_w_in, 256)
    big_adamw("w_out", w_out, g_w_out, m_w_out, v_w_out, 256)
    small, loss_sum = adamw_small(slab_a_g, slab_b_g, convw_g, wpool_g, {
        "b_ada": (b_ada, m_b_ada, v_b_ada), "g_pre_mix": (g_pre_mix, m_g_pre_mix, v_g_pre_mix),
        "g_post_mix": (g_post_mix, m_g_post_mix, v_g_post_mix), "g_pre_ffn": (g_pre_ffn, m_g_pre_ffn, v_g_pre_ffn),
        "g_post_ffn": (g_post_ffn, m_g_post_ffn, v_g_post_ffn), "b_pool": (b_pool, m_b_pool, v_b_pool),
        "pool_scale": (pool_scale, m_pool_scale, v_pool_scale), "conv_b": (conv_b, m_conv_b, v_conv_b),
        "conv_w": (conv_w, m_conv_w, v_conv_w), "w_pool": (w_pool, m_w_pool, v_w_pool)})
    for name in ("b_ada", "g_pre_mix", "g_post_mix", "g_pre_ffn", "g_post_ffn", "pool_scale", "conv_b", "b_pool", "w_pool", "conv_w"):
        res[name] = tuple(small[name])

    loss = loss_sum[0, 0]
    order = ["w_ada", "b_ada", "g_pre_mix", "g_post_mix", "g_pre_ffn", "g_post_ffn", "w_in", "w_pool", "b_pool", "pool_scale",
             "w_out", "w_up", "conv_w", "conv_b", "w_down"]
    outs = [loss, grad_x[None]]
    for k in range(4):
        outs += [res[n][k] for n in order]
    return tuple(outs)
```

```python
import math

import jax
import jax.numpy as jnp
from jax import lax
from jax.experimental import pallas as pl
from jax.experimental.pallas import tpu as pltpu

F32 = jnp.float32
BF16 = jnp.bfloat16
MESH = pl.DeviceIdType.MESH

D_MODEL = 1024
HEAD_DIM = 64
POOL_W = 256
GROUP_W = 256
DILATIONS = (1, 4, 16)
ATT_BLOCK = 128
IN_W = 2560
D_FF = 2816
HALF_FF = 1408
ROT_DIM = 16
ROPE_THETA = 500000.0
NORM_EPS = 1e-6
N_CHIPS = 4
N_DEV = 8
NEG = -1e30

ADAM_LR = 0.001
ADAM_B1 = 0.9
ADAM_B2 = 0.999
ADAM_EPS = 1e-08
ADAM_WD = 0.01
ADAM_STEP = 10

VMEM_LIMIT = 56 * 1024 * 1024

NT = (((1,), (1,)), ((), ()))
TN = (((0,), (0,)), ((), ()))


def _params(n_grid=0, **kw):
    sem = ("arbitrary",) * n_grid if n_grid else None
    return pltpu.CompilerParams(dimension_semantics=sem, vmem_limit_bytes=VMEM_LIMIT, **kw)


def _full(shape):
    nd = len(shape)
    return pl.BlockSpec(tuple(shape), lambda *_: (0,) * nd, pipeline_mode=pl.Buffered(1))


def _rows(tm, ncol):
    return pl.BlockSpec((tm, ncol), lambda i: (i, 0))


def _acc(ref, val):
    @pl.when(pl.program_id(0) == 0)
    def _():
        ref[...] = jnp.zeros_like(ref)

    ref[...] += val


def _colsum(v):
    return jnp.sum(v, axis=0, keepdims=True)


def _rope128(t, cs, sa, sb, sign):
    return t * cs + sign * (pltpu.roll(t, 8, 1) * sa + pltpu.roll(t, 120, 1) * sb)


FF_CHUNKS = tuple((ch, off, w) for ch in range(2) for off, w in ((0, 512), (512, 512), (1024, 384)))
GELU_C0 = math.sqrt(2.0 / math.pi)
GELU_C1 = GELU_C0 * 0.044715


def _gelu(z):
    z2 = z * z
    t = jnp.tanh(z * (GELU_C0 + GELU_C1 * z2))
    u = 0.5 * t + 0.5
    return z * u, u, t, z2


def _gelu_grad(z, u, t, z2):
    return u + (z * (GELU_C0 + (3.0 * GELU_C1) * z2)) * (0.5 - 0.5 * (t * t))


def _conv_taps(gate, halo, first):
    row = lax.broadcasted_iota(jnp.int32, gate.shape, 0)
    halo = jnp.where(first, 0.0, halo)
    nh = halo.shape[0]
    p1 = halo[nh - 1:nh, :]
    p2 = halo[nh - 2:nh - 1, :]
    g1 = jnp.where(row == 0, p1, pltpu.roll(gate, 1, 0))
    g2 = jnp.where(row == 0, p2, jnp.where(row == 1, p1, pltpu.roll(gate, 2, 0)))
    return g1, g2


def inproj_fwd(x, g, mod6, w_in_g, tc, tsa, tsb, tm=512):
    S = x.shape[0]

    def body(x_ref, g_ref, mod_ref, w_ref, tc_ref, tsa_ref, tsb_ref, h_ref, u_ref, q1_ref, q4_ref, q16_ref, scr):
        qkv_refs = (q1_ref, q4_ref, q16_ref)
        xv = x_ref[...]
        rstd = lax.rsqrt(jnp.mean(xv * xv, axis=-1, keepdims=True) + NORM_EPS)
        h = ((xv * rstd) * g_ref[...]) * (1.0 + mod_ref[1:2, :]) + mod_ref[0:1, :]
        hb = h.astype(BF16)
        h_ref[...] = hb
        cs, sa, sb = tc_ref[...], tsa_ref[...], tsb_ref[...]
        for j in range(N_CHIPS):
            res = jnp.dot(hb, w_ref[j], preferred_element_type=F32)
            for t in range(5):
                sp = 5 * j + t
                piece, half = sp // 2, sp % 2
                blk = res[:, t * 128:(t + 1) * 128]
                lanes = slice(half * 128, (half + 1) * 128)
                if piece == 0:
                    u_ref[:, lanes] = blk
                else:
                    kind, gi = (piece - 1) // 3, (piece - 1) % 3
                    if kind == 0:
                        blk = _rope128(blk, cs, sa, sb, 1.0) * (HEAD_DIM ** -0.5)
                    elif kind == 1:
                        blk = _rope128(blk, cs, sa, sb, 1.0)
                    d = DILATIONS[gi]
                    if d == 1:
                        q1_ref[kind, 0, :, lanes] = blk.astype(BF16)
                    else:
                        scr[...] = blk
                        for r in range(d):
                            qkv_refs[gi][kind, r, :, lanes] = scr[pl.ds(r, tm // d, stride=d), :].astype(BF16)

    cls = lambda d: pl.BlockSpec((3, d, tm // d, GROUP_W), lambda i: (0, 0, i, 0))
    return pl.pallas_call(
        body,
        name="inproj_fwd",
        grid=(S // tm,),
        in_specs=[_rows(tm, D_MODEL), _full((1, D_MODEL)), _full((6, D_MODEL)), _full(w_in_g.shape),
                  _rows(tm, 128), _rows(tm, 128), _rows(tm, 128)],
        out_specs=[_rows(tm, D_MODEL), _rows(tm, POOL_W)] + [cls(d) for d in DILATIONS],
        out_shape=[jax.ShapeDtypeStruct((S, D_MODEL), BF16), jax.ShapeDtypeStruct((S, POOL_W), F32)]
        + [jax.ShapeDtypeStruct((3, d, S // d, GROUP_W), BF16) for d in DILATIONS],
        scratch_shapes=[pltpu.VMEM((tm, 128), F32)],
        compiler_params=_params(1),
    )(x, g, mod6, w_in_g, tc, tsa, tsb)


def _attn_masks():
    row = lax.broadcasted_iota(jnp.int32, (2 * ATT_BLOCK, 2 * ATT_BLOCK), 0) % ATT_BLOCK
    col = lax.broadcasted_iota(jnp.int32, (2 * ATT_BLOCK, 2 * ATT_BLOCK), 1)
    band = (col >= row) & (col <= row + ATT_BLOCK)
    lane = lax.broadcasted_iota(jnp.int32, (ATT_BLOCK, 128), 1)
    return band, col, lane < HEAD_DIM


def _classes_per_step(d, nb):
    return min(d, max(1, 8 // nb))


def _stack_heads(t, lo):
    z = jnp.zeros_like(t)
    return jnp.concatenate([jnp.where(lo, t, z), jnp.where(lo, z, t)], axis=0)


def _unstack_heads(t2, lo):
    return jnp.where(lo, t2[:ATT_BLOCK], t2[ATT_BLOCK:])


def attn_fwd(qkv, d):
    L = qkv.shape[2]
    nb = L // ATT_BLOCK
    cpb = _classes_per_step(d, nb)

    def body(q_ref, k_ref, v_ref, o_ref, l_ref, kpad, vpad):
        for cls in range(cpb):
            kpad[cls, 0:ATT_BLOCK, :] = jnp.zeros((ATT_BLOCK, GROUP_W), BF16)
            vpad[cls, 0:ATT_BLOCK, :] = jnp.zeros((ATT_BLOCK, GROUP_W), BF16)
            kpad[cls, ATT_BLOCK:, :] = k_ref[cls]
            vpad[cls, ATT_BLOCK:, :] = v_ref[cls]
        band, col, lo = _attn_masks()

        def step(t, carry):
            cls, n = t // nb, t % nb
            r0 = pl.multiple_of(n * ATT_BLOCK, ATT_BLOCK)
            valid = band & ((col >= ATT_BLOCK) | (n > 0))
            qb = q_ref[cls, pl.ds(r0, ATT_BLOCK), :]
            kb = kpad[cls, pl.ds(r0, 2 * ATT_BLOCK), :]
            vb = vpad[cls, pl.ds(r0, 2 * ATT_BLOCK), :]
            for pair in range(2):
                lanes = slice(pair * 128, (pair + 1) * 128)
                qp, kp, vp = qb[:, lanes], kb[:, lanes], vb[:, lanes]
                s = lax.dot_general(_stack_heads(qp, lo), kp, NT, preferred_element_type=F32)
                s = jnp.where(valid, s, NEG)
                m = jnp.max(s, axis=1, keepdims=True)
                p = jnp.exp(s - m)
                den = jnp.sum(p, axis=1, keepdims=True)
                pv = jnp.dot(p.astype(BF16), vp, preferred_element_type=F32)
                o_ref[cls, pl.ds(r0, ATT_BLOCK), lanes] = _unstack_heads(pv / den, lo)
                l_ref[cls, pl.ds(r0, ATT_BLOCK), lanes] = _unstack_heads(jnp.broadcast_to(m + jnp.log(den), pv.shape), lo)
            return carry

        lax.fori_loop(0, cpb * nb, step, 0, unroll=4)

    spec = lambda kind: pl.BlockSpec((None, cpb, L, GROUP_W), lambda r: (kind, r, 0, 0))
    return pl.pallas_call(
        body,
        name=f"attn_fwd_d{d}",
        grid=(d // cpb,),
        in_specs=[spec(0), spec(1), spec(2)],
        out_specs=[pl.BlockSpec((cpb, L, GROUP_W), lambda r: (r, 0, 0))] * 2,
        out_shape=[jax.ShapeDtypeStruct((d, L, GROUP_W), F32)] * 2,
        scratch_shapes=[pltpu.VMEM((cpb, L + ATT_BLOCK, GROUP_W), BF16)] * 2,
        compiler_params=_params(1),
    )(qkv, qkv, qkv)


def _pool_lane_windows(shape):
    lane = lax.broadcasted_iota(jnp.int32, shape, 1)
    return lane, jnp.where(lane < 64, 2, jnp.where(lane < 128, 4, jnp.where(lane < 192, 8, 16)))


def pool_fwd(u, wbd, b, scale):
    S = u.shape[0]

    def body(u_ref, w_ref, b_ref, s_ref, mixed_ref, out_ref):
        uv = u_ref[...]
        row = lax.broadcasted_iota(jnp.int32, uv.shape, 0)
        lane, win = _pool_lane_windows(uv.shape)

        def shift(a, k):
            return jnp.where(row >= k, pltpu.roll(a, k, 0), 0.0)

        s2 = uv + shift(uv, 1)
        s4 = s2 + shift(s2, 2)
        s8 = s4 + shift(s4, 4)
        s16 = s8 + shift(s8, 8)
        tsum = jnp.where(lane < 64, s2, jnp.where(lane < 128, s4, jnp.where(lane < 192, s8, s16)))
        cnt = jnp.minimum(row + 1, win).astype(F32)
        mb = (tsum / cnt - uv).astype(BF16)
        mixed_ref[...] = mb
        y = jnp.dot(mb, w_ref[...], preferred_element_type=F32) + b_ref[...]
        out_ref[...] = (y * s_ref[...]).astype(BF16)

    vm = pl.BlockSpec(memory_space=pltpu.VMEM)
    return pl.pallas_call(
        body,
        name="pool_fwd",
        in_specs=[vm] * 4,
        out_specs=[vm] * 2,
        out_shape=[jax.ShapeDtypeStruct((S, POOL_W), BF16)] * 2,
        compiler_params=_params(),
    )(u, wbd, b, scale)


def outproj_fwd(o_l, pool, x, w_out_g, g_post, g_pre, mod6, tm=512):
    S = x.shape[0]

    def body(o0, o1, o2, l0, l1, l2, pool_ref, x_ref, w_ref, gpost_ref, gpre_ref, mod_ref,
             cat_ref, lse_ref, lse4_ref, lse16_ref, y1_ref, x1_ref, h2_ref, so4, sl4, so16, sl16):
        for d, src, dst in ((4, o1, so4), (4, l1, sl4), (16, o2, so16), (16, l2, sl16)):
            for r in range(d):
                for h in range(2):
                    dst[h, pl.ds(r, tm // d, stride=d), :] = src[r, :, h * 128:(h + 1) * 128]
        nat = lambda ref: jnp.concatenate([ref[0], ref[1]], axis=1)
        a, b, c = l0[0], nat(sl4), nat(sl16)
        m = jnp.maximum(jnp.maximum(a, b), c)
        e0, e1, e2 = jnp.exp(a - m), jnp.exp(b - m), jnp.exp(c - m)
        z = e0 + e1 + e2
        lse = m + jnp.log(z)
        lse_ref[...] = lse
        for h in range(2):
            sl4[h] = lse[:, h * 128:(h + 1) * 128]
        for d, dst in ((4, lse4_ref), (16, lse16_ref)):
            for r in range(d):
                for h in range(2):
                    dst[r, :, h * 128:(h + 1) * 128] = sl4[h, pl.ds(r, tm // d, stride=d), :]
        attn = (e0 * o0[0] + e1 * nat(so4) + e2 * nat(so16)) / z
        cat = jnp.concatenate([pool_ref[...], attn.astype(BF16)], axis=1)
        cat_ref[...] = cat
        y1 = jnp.concatenate([jnp.dot(cat, w_ref[j], preferred_element_type=F32) for j in range(N_CHIPS)], axis=1)
        y1_ref[...] = y1
        rstd = lax.rsqrt(jnp.mean(y1 * y1, axis=-1, keepdims=True) + NORM_EPS)
        x1 = x_ref[...] + mod_ref[2:3, :] * ((y1 * rstd) * gpost_ref[...])
        x1_ref[...] = x1
        rstd2 = lax.rsqrt(jnp.mean(x1 * x1, axis=-1, keepdims=True) + NORM_EPS)
        h2 = ((x1 * rstd2) * gpre_ref[...]) * (1.0 + mod_ref[4:5, :]) + mod_ref[3:4, :]
        h2_ref[...] = h2.astype(BF16)

    t256 = _rows(tm, GROUP_W)
    cls = lambda d: pl.BlockSpec((d, tm // d, GROUP_W), lambda i: (0, i, 0))
    cls_shape = lambda d: jax.ShapeDtypeStruct((d, S // d, GROUP_W), F32)
    return pl.pallas_call(
        body,
        name="outproj_fwd",
        grid=(S // tm,),
        in_specs=[cls(d) for d in DILATIONS] * 2 + [t256, _rows(tm, D_MODEL), _full(w_out_g.shape), _full((1, D_MODEL)),
                                                    _full((1, D_MODEL)), _full((6, D_MODEL))],
        out_specs=[_rows(tm, 512), t256, cls(4), cls(16), _rows(tm, D_MODEL), _rows(tm, D_MODEL), _rows(tm, D_MODEL)],
        out_shape=[jax.ShapeDtypeStruct((S, 512), BF16), jax.ShapeDtypeStruct((S, GROUP_W), F32), cls_shape(4), cls_shape(16),
                   jax.ShapeDtypeStruct((S, D_MODEL), F32), jax.ShapeDtypeStruct((S, D_MODEL), F32),
                   jax.ShapeDtypeStruct((S, D_MODEL), BF16)],
        scratch_shapes=[pltpu.VMEM((2, tm, 128), F32)] * 4,
        compiler_params=_params(1),
    )(*o_l, pool, x, w_out_g, g_post, g_pre, mod6)


def _halo_prev(tm, ncol):
    return pl.BlockSpec((16, ncol), lambda i: (jnp.maximum(i * (tm // 16) - 1, 0), 0))


def ffn_fwd(h2, w_up_g, conv_w, conv_b, w_down, x1, target, g_post, mod6, tm=512):
    S = x1.shape[0]

    def body(h_ref, wu_ref, cw_ref, cb_ref, wd_ref, x1_ref, tgt_ref, g_ref, mod_ref,
             gate_ref, val_ref, dy2_ref, dout_ref, loss_ref, dgt_ref, dg_ref, carry):
        first = pl.program_id(0) == 0

        @pl.when(first)
        def _():
            carry[...] = jnp.zeros_like(carry)

        hb = h_ref[...]
        y2 = jnp.zeros((tm, D_MODEL), F32)
        for ch in range(2):
            cols = slice(ch * HALF_FF, (ch + 1) * HALF_FF)
            gb = jnp.dot(hb, wu_ref[ch], preferred_element_type=F32).astype(BF16)
            vb = jnp.dot(hb, wu_ref[2 + ch], preferred_element_type=F32).astype(BF16)
            gate_ref[:, cols] = gb
            val_ref[:, cols] = vb
            gt = gb.astype(F32)
            g1, g2 = _conv_taps(gt, carry[:, cols], first)
            carry[:, cols] = gt[tm - 8:, :]
            gc = g2 * cw_ref[0:1, cols] + g1 * cw_ref[1:2, cols] + gt * cw_ref[2:3, cols] + cb_ref[:, cols]
            ab = _gelu(gc.astype(BF16))[0] * vb
            y2 = y2 + jnp.dot(ab, wd_ref[cols, :], preferred_element_type=F32)
        rstd = lax.rsqrt(jnp.mean(y2 * y2, axis=-1, keepdims=True) + NORM_EPS)
        y2n = y2 * rstd
        gv = g_ref[...]
        gtf = mod_ref[5:6, :]
        r2 = y2n * gv
        diff = (x1_ref[...] + gtf * r2) - tgt_ref[...]
        _acc(loss_ref, jnp.zeros((1, 128), F32) + 0.5 * jnp.sum(diff * diff) * (1.0 / D_MODEL))
        dout = diff * (1.0 / D_MODEL)
        dout_ref[...] = dout
        _acc(dgt_ref, _colsum(dout * r2))
        dr2 = dout * gtf
        _acc(dg_ref, _colsum(dr2 * y2n))
        dyn = dr2 * gv
        dy2 = rstd * (dyn - y2n * jnp.mean(dyn * y2n, axis=-1, keepdims=True))
        dy2_ref[...] = dy2.astype(BF16)

    vec = _full((1, D_MODEL))
    return pl.pallas_call(
        body,
        name="ffn_fwd",
        grid=(S // tm,),
        in_specs=[_rows(tm, D_MODEL), _full(w_up_g.shape), _full((3, D_FF)), _full((1, D_FF)), _full((D_FF, D_MODEL)),
                  _rows(tm, D_MODEL), _rows(tm, D_MODEL), vec, _full((6, D_MODEL))],
        out_specs=[_rows(tm, D_FF), _rows(tm, D_FF), _rows(tm, D_MODEL), _rows(tm, D_MODEL), _full((1, 128)), vec, vec],
        out_shape=[jax.ShapeDtypeStruct((S, D_FF), BF16)] * 2 + [jax.ShapeDtypeStruct((S, D_MODEL), BF16),
                                                                 jax.ShapeDtypeStruct((S, D_MODEL), F32),
                                                                 jax.ShapeDtypeStruct((1, 128), F32),
                                                                 jax.ShapeDtypeStruct((1, D_MODEL), F32),
                                                                 jax.ShapeDtypeStruct((1, D_MODEL), F32)],
        scratch_shapes=[pltpu.VMEM((8, D_FF), F32)],
        compiler_params=_params(1),
    )(h2, w_up_g, conv_w, conv_b, w_down, x1, target, g_post, mod6)


def down_bwd(dy2, w_down, gate, val, conv_w, conv_b, h2, tm=512):
    S = dy2.shape[0]

    def body(dy_ref, w_ref, gate_ref, halo_ref, val_ref, cw_ref, cb_ref, h_ref,
             dgc_ref, dval_ref, dcw_ref, dcb_ref, dwd_ref, dwu_ref):
        first = pl.program_id(1) == 0

        @pl.when(first)
        def _():
            dcw_ref[...] = jnp.zeros_like(dcw_ref)
            dcb_ref[...] = jnp.zeros_like(dcb_ref)
            dwd_ref[...] = jnp.zeros_like(dwd_ref)
            dwu_ref[...] = jnp.zeros_like(dwu_ref)

        dyb = dy_ref[...]
        hb = h_ref[...]
        pieces = [(off, w) for ch, off, w in FF_CHUNKS if ch == 0]

        def col(i):
            return slice(pieces[i][0], pieces[i][0] + pieces[i][1])

        def mm_da(i):
            return lax.dot_general(dyb, w_ref[col(i), :], NT, preferred_element_type=F32)

        def elementwise(i, da):
            cols = col(i)
            gt = gate_ref[:, cols].astype(F32)
            g1, g2 = _conv_taps(gt, halo_ref[:, cols].astype(F32), first)
            gc = g2 * cw_ref[0:1, cols] + g1 * cw_ref[1:2, cols] + gt * cw_ref[2:3, cols] + cb_ref[:, cols]
            zb, dab, vb = gc.astype(BF16), da.astype(BF16), val_ref[:, cols]
            ge, u, th, z2 = _gelu(zb)
            dgb = dab * vb * _gelu_grad(zb, u, th, z2)
            dgc_ref[:, cols] = dgb
            dgc = dgb.astype(F32)
            dvb = dab * ge
            dval_ref[:, cols] = dvb
            dcb_ref[:, cols] += _colsum(dgc)
            dcw_ref[0:1, cols] += _colsum(dgc * g2)
            dcw_ref[1:2, cols] += _colsum(dgc * g1)
            dcw_ref[2:3, cols] += _colsum(dgc * gt)
            return dvb, ge * vb

        def mm_dw(i, dvb_ab):
            dvb, ab = dvb_ab
            dwd_ref[col(i), :] += lax.dot_general(ab, dyb, TN, preferred_element_type=F32)
            dwu_ref[:, col(i)] += lax.dot_general(hb, dvb, TN, preferred_element_type=F32)

        n = len(pieces)
        da = mm_da(0)
        prev = None
        for i in range(n):
            nxt = mm_da(i + 1) if i + 1 < n else None
            if prev is not None:
                mm_dw(i - 1, prev)
            prev = elementwise(i, da)
            da = nxt
        mm_dw(n - 1, prev)

    one = pl.Buffered(1)
    tok = pl.BlockSpec((tm, D_MODEL), lambda c, i: (i, 0))
    ff = pl.BlockSpec((tm, HALF_FF), lambda c, i: (i, c))
    halo = pl.BlockSpec((16, HALF_FF), lambda c, i: (jnp.maximum(i * (tm // 16) - 1, 0), c))
    per_half = lambda rows: pl.BlockSpec((rows, HALF_FF), lambda c, i: (0, c), pipeline_mode=one)
    return pl.pallas_call(
        body,
        name="down_bwd",
        grid=(2, S // tm),
        in_specs=[tok, pl.BlockSpec((HALF_FF, D_MODEL), lambda c, i: (c, 0), pipeline_mode=one), ff, halo, ff,
                  per_half(3), per_half(1), tok],
        out_specs=[ff, ff, per_half(3), per_half(1), pl.BlockSpec((HALF_FF, D_MODEL), lambda c, i: (c, 0), pipeline_mode=one),
                   pl.BlockSpec((None, D_MODEL, HALF_FF), lambda c, i: (2 + c, 0, 0), pipeline_mode=one)],
        out_shape=[jax.ShapeDtypeStruct((S, D_FF), BF16), jax.ShapeDtypeStruct((S, D_FF), BF16),
                   jax.ShapeDtypeStruct((3, D_FF), F32), jax.ShapeDtypeStruct((1, D_FF), F32),
                   jax.ShapeDtypeStruct((D_FF, D_MODEL), F32), jax.ShapeDtypeStruct((N_CHIPS, D_MODEL, HALF_FF), F32)],
        compiler_params=_params(2),
    )(dy2, w_down, gate, gate, val, conv_w, conv_b, h2)


def up_bwd(dgc, dval, conv_w, w_up_g, x1, dout, y1, g_pre, g_post, mod6, h2, dw_up, tm=256):
    S = x1.shape[0]
    last_blk = S // 16 - 1

    def body(dgc_ref, nxt_ref, dval_ref, cw_ref, w_ref, x1_ref, dout_ref, y1_ref, gpre_ref, gpost_ref, mod_ref, h_ref, dwin_ref,
             dx1_ref, dy1_ref, dsh_ref, dsc_ref, dgpre_ref, dgt_ref, dgpost_ref, dwu_ref):
        last = pl.program_id(0) == pl.num_programs(0) - 1

        @pl.when(pl.program_id(0) == 0)
        def _():
            dwu_ref[...] = jnp.zeros_like(dwu_ref)

        hb = h_ref[...]
        dh = jnp.zeros((tm, D_MODEL), F32)
        for ch in range(2):
            cols = slice(ch * HALF_FF, (ch + 1) * HALF_FF)
            dg = dgc_ref[:, cols].astype(F32)
            nx = jnp.where(last, 0.0, nxt_ref[:, cols].astype(F32))
            row = lax.broadcasted_iota(jnp.int32, dg.shape, 0)
            n0, n1 = nx[0:1, :], nx[1:2, :]
            u1 = jnp.where(row == tm - 1, n0, pltpu.roll(dg, tm - 1, 0))
            u2 = jnp.where(row == tm - 1, n1, jnp.where(row == tm - 2, n0, pltpu.roll(dg, tm - 2, 0)))
            dgate = (dg * cw_ref[2:3, cols] + u1 * cw_ref[1:2, cols] + u2 * cw_ref[0:1, cols]).astype(BF16)
            dwu_ref[ch] += lax.dot_general(hb, dgate, TN, preferred_element_type=F32)
            dh = dh + lax.dot_general(dgate, w_ref[ch], NT, preferred_element_type=F32)
            dh = dh + lax.dot_general(dval_ref[:, cols], w_ref[2 + ch], NT, preferred_element_type=F32)
        x1 = x1_ref[...]
        rstd = lax.rsqrt(jnp.mean(x1 * x1, axis=-1, keepdims=True) + NORM_EPS)
        n2 = x1 * rstd
        gpre = gpre_ref[...]
        one_sc = 1.0 + mod_ref[4:5, :]
        _acc(dsh_ref, _colsum(dh))
        _acc(dsc_ref, _colsum(dh * (n2 * gpre)))
        _acc(dgpre_ref, _colsum(dh * one_sc * n2))
        dn = dh * (gpre * one_sc)
        dx1 = dout_ref[...] + rstd * (dn - n2 * jnp.mean(dn * n2, axis=-1, keepdims=True))
        dx1_ref[...] = dx1
        y1 = y1_ref[...]
        rstd1 = lax.rsqrt(jnp.mean(y1 * y1, axis=-1, keepdims=True) + NORM_EPS)
        y1n = y1 * rstd1
        gpost = gpost_ref[...]
        gtm = mod_ref[2:3, :]
        _acc(dgt_ref, _colsum(dx1 * (y1n * gpost)))
        dr1 = dx1 * gtm
        _acc(dgpost_ref, _colsum(dr1 * y1n))
        dyn = dr1 * gpost
        dy1 = rstd1 * (dyn - y1n * jnp.mean(dyn * y1n, axis=-1, keepdims=True))
        dy1_ref[...] = dy1.astype(BF16)

    vec = _full((1, D_MODEL))
    nxt = pl.BlockSpec((16, D_FF), lambda i: (jnp.minimum((i + 1) * (tm // 16), last_blk), 0))
    return pl.pallas_call(
        body,
        name="up_bwd",
        grid=(S // tm,),
        in_specs=[_rows(tm, D_FF), nxt, _rows(tm, D_FF), _full((3, D_FF)), _full(w_up_g.shape), _rows(tm, D_MODEL),
                  _rows(tm, D_MODEL), _rows(tm, D_MODEL), vec, vec, _full((6, D_MODEL)), _rows(tm, D_MODEL),
                  pl.BlockSpec(memory_space=pl.ANY)],
        out_specs=[_rows(tm, D_MODEL), _rows(tm, D_MODEL), vec, vec, vec, vec, vec,
                   pl.BlockSpec((2, D_MODEL, HALF_FF), lambda i: (0, 0, 0), pipeline_mode=pl.Buffered(1))],
        out_shape=[jax.ShapeDtypeStruct((S, D_MODEL), F32), jax.ShapeDtypeStruct((S, D_MODEL), BF16)]
        + [jax.ShapeDtypeStruct((1, D_MODEL), F32)] * 5 + [jax.ShapeDtypeStruct(dw_up.shape, F32)],
        input_output_aliases={12: 7},
        compiler_params=_params(1),
    )(dgc, dgc, dval, conv_w, w_up_g, x1, dout, y1, g_pre, g_post, mod6, h2, dw_up)


def outproj_bwd(dy1, w_out_g, cat, tm=512):
    S = dy1.shape[0]

    def body(dy_ref, w_ref, cat_ref, dpool_ref, dattn_ref, da4_ref, da16_ref, delta_ref, dl4_ref, dl16_ref, dw_ref, scr):
        @pl.when(pl.program_id(0) == 0)
        def _():
            dw_ref[...] = jnp.zeros_like(dw_ref)

        catb = cat_ref[...]
        dcat = jnp.zeros((tm, 512), F32)
        for j in range(N_CHIPS):
            dyj = dy_ref[:, j * 256:(j + 1) * 256]
            dcat = dcat + lax.dot_general(dyj, w_ref[j], NT, preferred_element_type=F32)
            dw_ref[j] += lax.dot_general(catb, dyj, TN, preferred_element_type=F32)
        dpool_ref[...] = dcat[:, :POOL_W]
        dattn = dcat[:, POOL_W:]
        dattn_ref[...] = dattn.astype(BF16)
        for h in range(2):
            scr[h] = dattn[:, h * 128:(h + 1) * 128]
        for d, dst in ((4, da4_ref), (16, da16_ref)):
            for r in range(d):
                for h in range(2):
                    dst[r, :, h * 128:(h + 1) * 128] = scr[h, pl.ds(r, tm // d, stride=d), :].astype(BF16)
        prod = dattn * catb[:, POOL_W:].astype(F32)
        r = lax.broadcasted_iota(jnp.int32, (GROUP_W, GROUP_W), 0) // HEAD_DIM
        c = lax.broadcasted_iota(jnp.int32, (GROUP_W, GROUP_W), 1) // HEAD_DIM
        ones_bd = jnp.where(r == c, 1.0, 0.0).astype(BF16)
        hi = prod.astype(BF16)
        lo = (prod - hi.astype(F32)).astype(BF16)
        delta = jnp.dot(hi, ones_bd, preferred_element_type=F32) + jnp.dot(lo, ones_bd, preferred_element_type=F32)
        delta_ref[...] = delta
        for h in range(2):
            scr[h] = delta[:, h * 128:(h + 1) * 128]
        for d, dst in ((4, dl4_ref), (16, dl16_ref)):
            for r in range(d):
                for h in range(2):
                    dst[r, :, h * 128:(h + 1) * 128] = scr[h, pl.ds(r, tm // d, stride=d), :]

    cls = lambda d: pl.BlockSpec((d, tm // d, GROUP_W), lambda i: (0, i, 0))
    cls_shape = lambda d, dt: jax.ShapeDtypeStruct((d, S // d, GROUP_W), dt)
    return pl.pallas_call(
        body,
        name="outproj_bwd",
        grid=(S // tm,),
        in_specs=[_rows(tm, D_MODEL), _full(w_out_g.shape), _rows(tm, 512)],
        out_specs=[_rows(tm, POOL_W), _rows(tm, GROUP_W), cls(4), cls(16), _rows(tm, GROUP_W), cls(4), cls(16),
                   _full(w_out_g.shape)],
        out_shape=[jax.ShapeDtypeStruct((S, POOL_W), F32), jax.ShapeDtypeStruct((S, GROUP_W), BF16), cls_shape(4, BF16),
                   cls_shape(16, BF16), jax.ShapeDtypeStruct((S, GROUP_W), F32), cls_shape(4, F32), cls_shape(16, F32),
                   jax.ShapeDtypeStruct(w_out_g.shape, F32)],
        scratch_shapes=[pltpu.VMEM((2, tm, 128), F32)],
        compiler_params=_params(1),
    )(dy1, w_out_g, cat)


def attn_bwd(qkv, dattn, lse, delta, d):
    L = qkv.shape[2]
    nb = L // ATT_BLOCK
    cpb = _classes_per_step(d, nb)

    def body(q_ref, k_ref, v_ref, do_ref, l_ref, dl_ref, out_ref, kpad, vpad, dkpad, dvpad):
        for cls in range(cpb):
            kpad[cls, 0:ATT_BLOCK, :] = jnp.zeros((ATT_BLOCK, GROUP_W), BF16)
            vpad[cls, 0:ATT_BLOCK, :] = jnp.zeros((ATT_BLOCK, GROUP_W), BF16)
            kpad[cls, ATT_BLOCK:, :] = k_ref[cls]
            vpad[cls, ATT_BLOCK:, :] = v_ref[cls]
        dkpad[...] = jnp.zeros_like(dkpad)
        dvpad[...] = jnp.zeros_like(dvpad)
        band, col, lo = _attn_masks()

        def step(t, carry):
            cls, n = t // nb, t % nb
            r0 = pl.multiple_of(n * ATT_BLOCK, ATT_BLOCK)
            valid = band & ((col >= ATT_BLOCK) | (n > 0))
            qb = q_ref[cls, pl.ds(r0, ATT_BLOCK), :]
            dob = do_ref[cls, pl.ds(r0, ATT_BLOCK), :]
            lb = l_ref[cls, pl.ds(r0, ATT_BLOCK), :]
            dlb = dl_ref[cls, pl.ds(r0, ATT_BLOCK), :]
            kb = kpad[cls, pl.ds(r0, 2 * ATT_BLOCK), :]
            vb = vpad[cls, pl.ds(r0, 2 * ATT_BLOCK), :]
            for pair in range(2):
                lanes = slice(pair * 128, (pair + 1) * 128)
                qp, dop, kp, vp = qb[:, lanes], dob[:, lanes], kb[:, lanes], vb[:, lanes]
                c0, c1 = pair * 128, pair * 128 + HEAD_DIM
                q2, do2 = _stack_heads(qp, lo), _stack_heads(dop, lo)
                lse2 = jnp.concatenate([lb[:, c0:c0 + 1], lb[:, c1:c1 + 1]], axis=0)
                dl2 = jnp.concatenate([dlb[:, c0:c0 + 1], dlb[:, c1:c1 + 1]], axis=0)
                s = lax.dot_general(q2, kp, NT, preferred_element_type=F32)
                s = jnp.where(valid, s, NEG)
                p = jnp.exp(s - lse2)
                dp = lax.dot_general(do2, vp, NT, preferred_element_type=F32)
                ds = (p * (dp - dl2)).astype(BF16)
                dq2 = jnp.dot(ds, kp, preferred_element_type=F32)
                out_ref[0, cls, pl.ds(r0, ATT_BLOCK), lanes] = _unstack_heads(dq2, lo)
                dkpad[cls, pl.ds(r0, 2 * ATT_BLOCK), lanes] += lax.dot_general(ds, q2, TN, preferred_element_type=F32)
                dvpad[cls, pl.ds(r0, 2 * ATT_BLOCK), lanes] += lax.dot_general(p.astype(BF16), do2, TN, preferred_element_type=F32)
            return carry

        lax.fori_loop(0, cpb * nb, step, 0, unroll=4)
        for cls in range(cpb):
            out_ref[1, cls] = dkpad[cls, ATT_BLOCK:, :]
            out_ref[2, cls] = dvpad[cls, ATT_BLOCK:, :]

    spec = lambda kind: pl.BlockSpec((None, cpb, L, GROUP_W), lambda r: (kind, r, 0, 0))
    per_cls = pl.BlockSpec((cpb, L, GROUP_W), lambda r: (r, 0, 0))
    return pl.pallas_call(
        body,
        name=f"attn_bwd_d{d}",
        grid=(d // cpb,),
        in_specs=[spec(0), spec(1), spec(2), per_cls, per_cls, per_cls],
        out_specs=pl.BlockSpec((3, cpb, L, GROUP_W), lambda r: (0, r, 0, 0)),
        out_shape=jax.ShapeDtypeStruct((3, d, L, GROUP_W), F32),
        scratch_shapes=[pltpu.VMEM((cpb, L + ATT_BLOCK, GROUP_W), BF16)] * 2 + [pltpu.VMEM((cpb, L + ATT_BLOCK, GROUP_W), F32)] * 2,
        compiler_params=_params(1),
    )(qkv, qkv, qkv, dattn, lse, delta)


def pool_bwd(dpool, mixed, wbd, b, scale):
    S = dpool.shape[0]

    def body(dp_ref, mx_ref, w_ref, b_ref, s_ref, du_ref, dw_ref, db_ref, ds_ref):
        dp = dp_ref[...]
        mb = mx_ref[...]
        wv = w_ref[...]
        ypre = jnp.dot(mb, wv, preferred_element_type=F32) + b_ref[...]
        ds_ref[...] = _colsum(dp * ypre)
        dpre = dp * s_ref[...]
        db_ref[...] = _colsum(dpre)
        dpb = dpre.astype(BF16)
        dw_ref[...] = lax.dot_general(mb, dpb, TN, preferred_element_type=F32)
        dmix = lax.dot_general(dpb, wv, NT, preferred_element_type=F32)
        row = lax.broadcasted_iota(jnp.int32, dmix.shape, 0)
        lane, win = _pool_lane_windows(dmix.shape)
        e = dmix / jnp.minimum(row + 1, win).astype(F32)

        def shift(a, k):
            return jnp.where(row < S - k, pltpu.roll(a, S - k, 0), 0.0)

        f2 = e + shift(e, 1)
        f4 = f2 + shift(f2, 2)
        f8 = f4 + shift(f4, 4)
        f16 = f8 + shift(f8, 8)
        du_ref[...] = jnp.where(lane < 64, f2, jnp.where(lane < 128, f4, jnp.where(lane < 192, f8, f16))) - dmix

    vm = pl.BlockSpec(memory_space=pltpu.VMEM)
    return pl.pallas_call(
        body,
        name="pool_bwd",
        in_specs=[vm] * 5,
        out_specs=[vm] * 4,
        out_shape=[jax.ShapeDtypeStruct((S, POOL_W), F32), jax.ShapeDtypeStruct((POOL_W, POOL_W), F32),
                   jax.ShapeDtypeStruct((1, POOL_W), F32), jax.ShapeDtypeStruct((1, POOL_W), F32)],
        compiler_params=_params(),
    )(dpool, mixed, wbd, b, scale)


def inproj_bwd(dqkv, du, x, dx1, w_in_g, g, mod6, tc, tsa, tsb, h1, tm=512):
    S = x.shape[0]

    def body(d0, d1, d2, du_ref, x_ref, dx1_ref, w_ref, g_ref, mod_ref, tc_ref, tsa_ref, tsb_ref, h_ref,
             gx_ref, dsh_ref, dsc_ref, dg_ref, dw_ref, s4, s16, dp_ref):
        @pl.when(pl.program_id(0) == 0)
        def _():
            dw_ref[...] = jnp.zeros_like(dw_ref)

        cs, sa, sb = tc_ref[...], tsa_ref[...], tsb_ref[...]
        for d, src, dst in ((4, d1, s4), (16, d2, s16)):
            for kind in range(3):
                for r in range(d):
                    for h in range(2):
                        dst[kind, h, pl.ds(r, tm // d, stride=d), :] = src[kind, r, :, h * 128:(h + 1) * 128]
        for sp in range(20):
            piece, half = sp // 2, sp % 2
            lanes = slice(half * 128, (half + 1) * 128)
            if piece == 0:
                blk = du_ref[:, lanes]
            else:
                kind, gi = (piece - 1) // 3, (piece - 1) % 3
                blk = d0[kind, 0, :, lanes] if gi == 0 else (s4, s16)[gi - 1][kind, half]
                if kind == 0:
                    blk = _rope128(blk, cs, sa, sb, -1.0) * (HEAD_DIM ** -0.5)
                elif kind == 1:
                    blk = _rope128(blk, cs, sa, sb, -1.0)
            dp_ref[:, sp * 128:(sp + 1) * 128] = blk.astype(BF16)
        dh = jnp.zeros((tm, D_MODEL), F32)
        hbt = h_ref[...].T
        for j in range(N_CHIPS):
            dpj = dp_ref[:, j * 640:(j + 1) * 640]
            dh = dh + lax.dot_general(dpj, w_ref[j], NT, preferred_element_type=F32)
            dw_ref[j] += jnp.dot(hbt, dpj, preferred_element_type=F32)
        xv = x_ref[...]
        rstd = lax.rsqrt(jnp.mean(xv * xv, axis=-1, keepdims=True) + NORM_EPS)
        n1 = xv * rstd
        gv = g_ref[...]
        one_sc = 1.0 + mod_ref[1:2, :]
        _acc(dsh_ref, _colsum(dh))
        _acc(dsc_ref, _colsum(dh * (n1 * gv)))
        _acc(dg_ref, _colsum(dh * one_sc * n1))
        dn = dh * (gv * one_sc)
        gx_ref[...] = dx1_ref[...] + rstd * (dn - n1 * jnp.mean(dn * n1, axis=-1, keepdims=True))

    vec = _full((1, D_MODEL))
    dspec = lambda d: pl.BlockSpec((3, d, tm // d, GROUP_W), lambda i: (0, 0, i, 0))
    return pl.pallas_call(
        body,
        name="inproj_bwd",
        grid=(S // tm,),
        in_specs=[dspec(d) for d in DILATIONS] + [_rows(tm, POOL_W), _rows(tm, D_MODEL), _rows(tm, D_MODEL), _full(w_in_g.shape),
                                                  vec, _full((6, D_MODEL)), _rows(tm, 128), _rows(tm, 128), _rows(tm, 128),
                                                  _rows(tm, D_MODEL)],
        out_specs=[_rows(tm, D_MODEL), vec, vec, vec, _full(w_in_g.shape)],
        out_shape=[jax.ShapeDtypeStruct((S, D_MODEL), F32)] + [jax.ShapeDtypeStruct((1, D_MODEL), F32)] * 3
        + [jax.ShapeDtypeStruct(w_in_g.shape, F32)],
        scratch_shapes=[pltpu.VMEM((3, 2, tm, 128), F32)] * 2 + [pltpu.VMEM((tm, IN_W), BF16)],
        compiler_params=_params(1),
    )(*dqkv, du, x, dx1, w_in_g, g, mod6, tc, tsa, tsb, h1)


def _adamw(w, g, m, v):
    m = ADAM_B1 * m + (1.0 - ADAM_B1) * g
    v = ADAM_B2 * v + (1.0 - ADAM_B2) * (g * g)
    m_hat = m / (1.0 - ADAM_B1 ** ADAM_STEP)
    v_hat = v / (1.0 - ADAM_B2 ** ADAM_STEP)
    delta = -ADAM_LR * (m_hat / (jnp.sqrt(v_hat) + ADAM_EPS) + ADAM_WD * w)
    return delta, m, v


def adamw_rows(w, g, m, v, tr, name):
    R, C = w.shape

    def body(w_ref, g_ref, m_ref, v_ref, go_ref, d_ref, mo_ref, vo_ref):
        g = g_ref[...]
        go_ref[...] = g
        d_ref[...], mo_ref[...], vo_ref[...] = _adamw(w_ref[...], g, m_ref[...], v_ref[...])

    spec = pl.BlockSpec((tr, C), lambda i: (i, 0))
    return pl.pallas_call(
        body,
        name=name,
        grid=(R // tr,),
        in_specs=[spec] * 4,
        out_specs=[spec] * 4,
        out_shape=[jax.ShapeDtypeStruct((R, C), F32)] * 4,
        compiler_params=_params(1),
    )(w, g, m, v)


def adamw_ada(c_all_t, dmod_cols, w, m, v, tr=256):
    R, C = w.shape

    def body(ct_ref, dm_ref, w_ref, m_ref, v_ref, g_ref, d_ref, mo_ref, vo_ref):
        ct = ct_ref[...]
        act = ct * jax.nn.sigmoid(ct)
        dm = dm_ref[...]
        a_hi, d_hi = act.astype(BF16), dm.astype(BF16)
        a_lo, d_lo = (act - a_hi.astype(F32)).astype(BF16), (dm - d_hi.astype(F32)).astype(BF16)
        g = (jnp.dot(a_hi, d_hi, preferred_element_type=F32) + jnp.dot(a_lo, d_hi, preferred_element_type=F32)
             + jnp.dot(a_hi, d_lo, preferred_element_type=F32))
        g_ref[...] = g
        d_ref[...], mo_ref[...], vo_ref[...] = _adamw(w_ref[...], g, m_ref[...], v_ref[...])

    spec = pl.BlockSpec((tr, C), lambda i: (i, 0))
    return pl.pallas_call(
        body,
        name="adamw_ada",
        grid=(R // tr,),
        in_specs=[pl.BlockSpec((tr, N_DEV), lambda i: (i, 0)), _full((N_DEV, C)), spec, spec, spec],
        out_specs=[spec] * 4,
        out_shape=[jax.ShapeDtypeStruct((R, C), F32)] * 4,
        compiler_params=_params(1),
    )(c_all_t, dmod_cols, w, m, v)


def adamw_small(slab_a, slab_b, convw_g, wpool_g, params):
    names = ["b_ada", "g_pre_mix", "g_post_mix", "g_pre_ffn", "g_post_ffn", "b_pool", "pool_scale", "conv_b", "conv_w", "w_pool"]
    flat = []
    for n in names:
        flat += list(params[n])

    def body(a_ref, b_ref, cw_ref, wp_ref, *rest):
        ins, outs = rest[:30], rest[30:]

        def dev_sum(ref):
            t = ref[0]
            for dev in range(1, N_DEV):
                t = t + ref[dev]
            return t

        sa, sb_, scw, swp = dev_sum(a_ref), dev_sum(b_ref), dev_sum(cw_ref), dev_sum(wp_ref)
        grads = [
            jnp.concatenate([sa[k:k + 1, :] for k in range(6)], axis=1),
            sa[6:7, :], sa[7:8, :], sa[8:9, :], sa[9:10, :],
            sa[10:11, 0:256], sa[10:11, 256:512],
            sb_[3:4, :], scw, swp,
        ]
        for i, g in enumerate(grads):
            w_ref, m_ref, v_ref = ins[3 * i:3 * i + 3]
            if names[i] == "b_pool":
                parts = [((0, slice(grp, grp + 1)), g[:, grp * 64:(grp + 1) * 64]) for grp in range(4)]
            elif names[i] == "w_pool":
                parts = [((0, grp), g[grp * 64:(grp + 1) * 64, :]) for grp in range(4)]
            elif names[i] == "conv_w":
                parts = [((0,), g)]
            else:
                parts = [((Ellipsis,), g)]
            for at, gp in parts:
                d, mo, vo = _adamw(w_ref[at], gp, m_ref[at], v_ref[at])
                for k, val in enumerate((gp, d, mo, vo)):
                    outs[4 * i + k][at] = val
        outs[-1][...] = sa[10:11, 512:640]

    vm = pl.BlockSpec(memory_space=pltpu.VMEM)
    out_shape = []
    for n in names:
        out_shape += [jax.ShapeDtypeStruct(params[n][0].shape, F32)] * 4
    out_shape.append(jax.ShapeDtypeStruct((1, 128), F32))
    outs = pl.pallas_call(
        body,
        name="adamw_small",
        in_specs=[vm] * (4 + len(flat)),
        out_specs=[vm] * len(out_shape),
        out_shape=out_shape,
        compiler_params=_params(),
    )(slab_a, slab_b, convw_g, wpool_g, *flat)
    return {n: outs[4 * i:4 * i + 4] for i, n in enumerate(names)}, outs[-1]


def _place():
    return lax.axis_index("x"), lax.axis_index("y"), lax.axis_index("c")


def _other_chips(x, y):
    return [(1 - x, y), (x, 1 - y), (1 - x, 1 - y)]


def _chip_id(cx, cy):
    return 2 * cx + cy


HBM_SPEC = pl.BlockSpec(memory_space=pltpu.HBM)
SEM_SPEC = pl.BlockSpec(memory_space=pltpu.SEMAPHORE)
ANY_SPEC = pl.BlockSpec(memory_space=pl.ANY)
EFFECT = pltpu.SideEffectType.DATAFLOW_SIDE_EFFECTING


def _hbm(t):
    return pltpu.with_memory_space_constraint(t, pltpu.HBM)


def _hbm_shapes(ts):
    return [pltpu.HBM(t.shape, t.dtype) for t in ts]


def _half_rows(ref, lead, half, rh):
    return ref.at[lead, pl.ds(half * rh, rh), :]


def _flips():
    return [(fx, fy, fc) for fx in (0, 1) for fy in (0, 1) for fc in (0, 1)][1:]


def _flip(v, f):
    return v if f == 0 else 1 - v


def ada_mod(c3, w_ada, b_cols, conv_w):
    CB = w_ada.shape[1]

    def body(c_ref, w_ref, b_ref, cw_ref, call_ref, mod_ref, cwall_ref, modall, send_sems, recv_sems):
        x, y, c = _place()
        me_dev = 4 * x + 2 * y + c
        me = _chip_id(x, y)
        call_ref[me_dev] = c_ref[0]
        cwall_ref[me] = cw_ref[...]
        sends = []
        for k, (cx, cy) in enumerate(_other_chips(x, y)):
            cp = pltpu.make_async_remote_copy(src_ref=cw_ref, dst_ref=cwall_ref.at[me], send_sem=send_sems.at[10 + k],
                                              recv_sem=recv_sems.at[10 + k], device_id=(cx, cy, c), device_id_type=MESH)
            cp.start()
            sends.append(cp)
        for k, (fx, fy, fc) in enumerate(_flips()):
            cp = pltpu.make_async_remote_copy(src_ref=c_ref.at[0], dst_ref=call_ref.at[me_dev], send_sem=send_sems.at[k],
                                              recv_sem=recv_sems.at[k],
                                              device_id=(_flip(x, fx), _flip(y, fy), _flip(c, fc)), device_id_type=MESH)
            cp.start()
            sends.append(cp)
        for k, (fx, fy, fc) in enumerate(_flips()):
            peer = 4 * _flip(x, fx) + 2 * _flip(y, fy) + _flip(c, fc)
            pltpu.make_async_remote_copy(src_ref=c_ref.at[0], dst_ref=call_ref.at[peer], send_sem=send_sems.at[k],
                                         recv_sem=recv_sems.at[k], device_id=(x, y, c), device_id_type=MESH).wait_recv()
        row = lax.broadcasted_iota(jnp.int32, (N_DEV, D_MODEL), 0)
        call = jnp.zeros((N_DEV, D_MODEL), F32)
        for dev in range(N_DEV):
            call = jnp.where(row == dev, call_ref[dev], call)
        act = call * jax.nn.sigmoid(call)
        wv = w_ref[...]
        w_hi = wv.astype(BF16)
        w_lo = (wv - w_hi.astype(F32)).astype(BF16)
        a_hi = act.astype(BF16)
        a_lo = (act - a_hi.astype(F32)).astype(BF16)
        prod = (jnp.dot(a_hi, w_hi, preferred_element_type=F32) + jnp.dot(a_lo, w_hi, preferred_element_type=F32)
                + jnp.dot(a_hi, w_lo, preferred_element_type=F32))
        modall[me] = prod + b_ref[...]
        for k, (cx, cy) in enumerate(_other_chips(x, y)):
            cp = pltpu.make_async_remote_copy(src_ref=modall.at[me], dst_ref=modall.at[me], send_sem=send_sems.at[7 + k],
                                              recv_sem=recv_sems.at[7 + k], device_id=(cx, cy, c), device_id_type=MESH)
            cp.start()
            sends.append(cp)
        for k, (cx, cy) in enumerate(_other_chips(x, y)):
            blk = modall.at[_chip_id(cx, cy)]
            pltpu.make_async_remote_copy(src_ref=blk, dst_ref=blk, send_sem=send_sems.at[7 + k], recv_sem=recv_sems.at[7 + k],
                                         device_id=(x, y, c), device_id_type=MESH).wait_recv()
        for k, (cx, cy) in enumerate(_other_chips(x, y)):
            blk = cwall_ref.at[_chip_id(cx, cy)]
            pltpu.make_async_remote_copy(src_ref=blk, dst_ref=blk, send_sem=send_sems.at[10 + k], recv_sem=recv_sems.at[10 + k],
                                         device_id=(x, y, c), device_id_type=MESH).wait_recv()
        for cp in sends:
            cp.wait_send()
        mine = [modall[j, pl.ds(me_dev, 1), :] for j in range(N_CHIPS)]
        for r in range(6):
            pieces = []
            for h in range(2):
                pos = r * D_MODEL + h * 512
                pieces.append(mine[pos // CB][:, pos % CB:pos % CB + 512])
            mod_ref[r:r + 1, :] = jnp.concatenate(pieces, axis=1)

    vm = pl.BlockSpec(memory_space=pltpu.VMEM)
    return pl.pallas_call(
        body,
        name="ada_mod",
        in_specs=[vm] * 4,
        out_specs=[vm] * 3,
        out_shape=[jax.ShapeDtypeStruct((N_DEV, 1, D_MODEL), F32), jax.ShapeDtypeStruct((6, D_MODEL), F32),
                   jax.ShapeDtypeStruct((N_CHIPS,) + conv_w.shape, F32)],
        scratch_shapes=[pltpu.VMEM((N_CHIPS, N_DEV, CB), F32), pltpu.SemaphoreType.DMA((13,)), pltpu.SemaphoreType.DMA((13,))],
        compiler_params=pltpu.CompilerParams(has_side_effects=True, vmem_limit_bytes=VMEM_LIMIT),
    )(c3, w_ada, b_cols, conv_w)


def split_start(name, bufs, plan, n_sem, carry):
    nb = len(bufs)
    many = isinstance(carry, (list, tuple))
    alls = list(bufs) + (list(carry) if many else [carry])
    na = len(alls)

    def body(*refs):
        x, y, c = _place()
        ssem, rsem = refs[na], refs[na + 1]
        for i, (src, dst, dev) in enumerate(plan(refs[:nb], x, y, c)):
            pltpu.make_async_remote_copy(src_ref=src, dst_ref=dst, send_sem=ssem.at[i], recv_sem=rsem.at[i], device_id=dev,
                                         device_id_type=MESH).start()

    outs = pl.pallas_call(
        body,
        name=name,
        out_shape=[pltpu.SemaphoreType.DMA((n_sem,)), pltpu.SemaphoreType.DMA((n_sem,))] + _hbm_shapes(alls),
        in_specs=[HBM_SPEC] * na,
        out_specs=[SEM_SPEC, SEM_SPEC] + [HBM_SPEC] * na,
        input_output_aliases={i: 2 + i for i in range(na)},
        compiler_params=pltpu.CompilerParams(has_side_effects=EFFECT),
    )(*[_hbm(t) for t in alls])
    return outs[0], outs[1], list(outs[2:2 + nb]), (list(outs[2 + nb:]) if many else outs[-1])


def split_wait(name, ssem, rsem, bufs, plan, after):
    nb = len(bufs)

    def body(*refs):
        x, y, c = _place()
        s_ref, r_ref = refs[nb], refs[nb + 1]
        for i, (src, dst, dev) in enumerate(plan(refs[:nb], x, y, c)):
            cp = pltpu.make_async_remote_copy(src_ref=src, dst_ref=dst, send_sem=s_ref.at[i], recv_sem=r_ref.at[i], device_id=dev,
                                              device_id_type=MESH)
            cp.wait_send()
            cp.wait_recv()

    outs = pl.pallas_call(
        body,
        name=name,
        out_shape=_hbm_shapes(bufs),
        in_specs=[HBM_SPEC] * nb + [SEM_SPEC, SEM_SPEC, ANY_SPEC],
        out_specs=[HBM_SPEC] * nb,
        input_output_aliases={i: i for i in range(nb)},
        compiler_params=pltpu.CompilerParams(has_side_effects=EFFECT),
    )(*bufs, ssem, rsem, after)
    return list(outs)


def _gather_ici_plan(n):
    def plan(refs, x, y, c):
        out = []
        for w in range(n):
            rh = refs[w].shape[0] // 2
            for cx, cy in _other_chips(x, y):
                out.append((refs[w].at[pl.ds(c * rh, rh), :], _half_rows(refs[n + w], _chip_id(x, y), c, rh), (cx, cy, c)))
        return out

    return plan


def _gather_d2d_plan(n):
    def plan(refs, x, y, c):
        out = []
        for w in range(n):
            rh = refs[w].shape[1] // 2
            for cx, cy in _other_chips(x, y):
                blk = _half_rows(refs[w], _chip_id(cx, cy), c, rh)
                out.append((blk, blk, (x, y, 1 - c)))
        return out

    return plan


def _dev_id(x, y, c):
    return 4 * x + 2 * y + c


def _small_ici_plan(n):
    def plan(refs, x, y, c):
        out = []
        for w in range(n):
            dst = refs[n + w].at[_dev_id(x, y, c)]
            out.append((refs[w], dst, (x, y, 1 - c)))
            for cx, cy in _other_chips(x, y):
                out.append((refs[w], dst, (cx, cy, c)))
        return out

    return plan


def _small_d2d_plan(n):
    def plan(refs, x, y, c):
        out = []
        for w in range(n):
            for cx, cy in _other_chips(x, y):
                blk = refs[w].at[_dev_id(cx, cy, c)]
                out.append((blk, blk, (x, y, 1 - c)))
        return out

    return plan


def _rs_d2d_plan(n):
    def plan(refs, x, y, c):
        out = []
        for w in range(n):
            rh = refs[w].shape[1] // 2
            out.append((refs[w].at[:, pl.ds((1 - c) * rh, rh), :], refs[n + w], (x, y, 1 - c)))
        return out

    return plan


def _rs_ici_plan(n):
    def plan(refs, x, y, c):
        out = []
        for w in range(n):
            for k, (cx, cy) in enumerate(_other_chips(x, y)):
                out.append((refs[w].at[_chip_id(cx, cy)], refs[n + w].at[k], (cx, cy, c)))
        return out

    return plan


def _rs_share_plan(n):
    def plan(refs, x, y, c):
        out = []
        for w in range(n):
            rh = refs[w].shape[0] // 2
            rows = refs[w].at[pl.ds(c * rh, rh), :]
            out.append((rows, rows, (x, y, 1 - c)))
        return out

    return plan


def rs_add(grad, sibbuf, place, tr, name):
    _, R, C = grad.shape
    nt = (R // 2) // tr

    def body(p_ref, g_ref, s_ref, o_ref):
        o_ref[...] = (g_ref[...] + s_ref[...]).astype(BF16)

    return pl.pallas_call(
        body,
        name=name,
        grid_spec=pltpu.PrefetchScalarGridSpec(
            num_scalar_prefetch=1,
            grid=(N_CHIPS, nt),
            in_specs=[pl.BlockSpec((None, tr, C), lambda j, i, p: (j, p[0] * nt + i, 0)),
                      pl.BlockSpec((None, tr, C), lambda j, i, p: (j, i, 0))],
            out_specs=pl.BlockSpec((None, tr, C), lambda j, i, p: (j, i, 0)),
        ),
        out_shape=jax.ShapeDtypeStruct((N_CHIPS, R // 2, C), BF16),
        compiler_params=_params(2),
    )(place, grad, sibbuf)


def rs_final(grad, sibbuf, rbuf, place, tr, name):
    _, R, C = grad.shape
    nt = (R // 2) // tr

    def body(p_ref, g_ref, s_ref, r_ref, o_ref):
        o_ref[...] = (((g_ref[...] + s_ref[...]) + r_ref[0].astype(F32)) + r_ref[1].astype(F32)) + r_ref[2].astype(F32)

    return pl.pallas_call(
        body,
        name=name,
        grid_spec=pltpu.PrefetchScalarGridSpec(
            num_scalar_prefetch=1,
            grid=(nt,),
            in_specs=[pl.BlockSpec((None, tr, C), lambda i, p: (p[1], p[0] * nt + i, 0)),
                      pl.BlockSpec((None, tr, C), lambda i, p: (p[1], i, 0)),
                      pl.BlockSpec((3, tr, C), lambda i, p: (0, i, 0))],
            out_specs=pl.BlockSpec((tr, C), lambda i, p: (p[0] * nt + i, 0)),
        ),
        out_shape=jax.ShapeDtypeStruct((R, C), F32),
        compiler_params=_params(1),
    )(place, grad, sibbuf, rbuf)


class GradReduce:
    def __init__(self, tag, grads, rows, place):
        self.tag, self.grads, self.rows, self.place = tag, grads, rows, place
        self.n = len(grads)

    def d2d_start(self, carry):
        sib = [lax.empty((N_CHIPS, g.shape[1] // 2, g.shape[2]), F32) for g in self.grads]
        self.s1, self.r1, bufs, carry = split_start(f"rs_{self.tag}_d2d_start", self.grads + sib, _rs_d2d_plan(self.n), self.n, carry)
        self.bufs1 = bufs
        return carry

    def add_and_ici_start(self, after, carry):
        bufs = split_wait(f"rs_{self.tag}_d2d_wait", self.s1, self.r1, self.bufs1, _rs_d2d_plan(self.n), after)
        self.grads, self.sib = bufs[:self.n], bufs[self.n:]
        pb = [rs_add(g, s, self.place, tr, f"rs_{self.tag}_add{w}")
              for w, (g, s, tr) in enumerate(zip(self.grads, self.sib, self.rows))]
        rb = [lax.empty((3,) + p.shape[1:], BF16) for p in pb]
        self.s2, self.r2, self.bufs2, carry = split_start(f"rs_{self.tag}_ici_start", pb + rb, _rs_ici_plan(self.n), 3 * self.n, carry)
        return carry

    def final_and_share_start(self, after, carry):
        bufs = split_wait(f"rs_{self.tag}_ici_wait", self.s2, self.r2, self.bufs2, _rs_ici_plan(self.n), after)
        rb = bufs[self.n:]
        full = [rs_final(g, s, r, self.place, tr, f"rs_{self.tag}_final{w}")
                for w, (g, s, r, tr) in enumerate(zip(self.grads, self.sib, rb, self.rows))]
        self.s3, self.r3, self.bufs3, carry = split_start(f"rs_{self.tag}_share_start", full, _rs_share_plan(self.n), self.n, carry)
        return carry

    def finish(self, after):
        return split_wait(f"rs_{self.tag}_share_wait", self.s3, self.r3, self.bufs3, _rs_share_plan(self.n), after)


def _rope_tables(positions):
    inv_freq = ROPE_THETA ** (-jnp.arange(0, ROT_DIM, 2, dtype=F32) / ROT_DIM)
    ang = positions.astype(F32)[:, None] * inv_freq
    cos, sin = jnp.cos(ang), jnp.sin(ang)
    S = positions.shape[0]
    one, zero = jnp.ones((S, 48), F32), jnp.zeros((S, 48), F32)
    z8 = jnp.zeros((S, 8), F32)
    tc = jnp.concatenate([cos, cos, one], axis=1)
    tsa = jnp.concatenate([z8, sin, zero], axis=1)
    tsb = jnp.concatenate([-sin, z8, zero], axis=1)
    return tuple(jnp.tile(t, (1, 2)) for t in (tc, tsa, tsb))


def _block_diag(w_pool):
    wbd = jnp.zeros((POOL_W, POOL_W), F32)
    for gi in range(4):
        wbd = wbd.at[gi * 64:(gi + 1) * 64, gi * 64:(gi + 1) * 64].set(w_pool[gi])
    return wbd


def kernel(x, c, positions, w_ada, b_ada, g_pre_mix, g_post_mix, g_pre_ffn, g_post_ffn, w_in, w_pool, b_pool, pool_scale, w_out, w_up, conv_w, conv_b, w_down, loss_target, m_w_ada, m_b_ada, m_g_pre_mix, m_g_post_mix, m_g_pre_ffn, m_g_post_ffn, m_w_in, m_w_pool, m_b_pool, m_pool_scale, m_w_out, m_w_up, m_conv_w, m_conv_b, m_w_down, v_w_ada, v_b_ada, v_g_pre_mix, v_g_post_mix, v_g_pre_ffn, v_g_post_ffn, v_w_in, v_w_pool, v_b_pool, v_pool_scale, v_w_out, v_w_up, v_conv_w, v_conv_b, v_w_down):
    xi, yi, ci = lax.axis_index("x"), lax.axis_index("y"), lax.axis_index("c")
    chip = 2 * xi + yi
    place = jnp.stack([ci, chip]).astype(jnp.int32)
    x2, tgt = x[0], loss_target[0]
    S = x2.shape[0]

    def landing(s_):
        return lax.dynamic_update_slice(lax.empty((N_CHIPS,) + s_.shape, s_.dtype), s_[None], (chip, 0, 0))

    cb_ada = w_ada.shape[2]
    b_cols = lax.dynamic_slice(b_ada, (0, chip * cb_ada), (1, cb_ada))
    c_all, mod6, conv_w_g = ada_mod(c.reshape(1, 1, D_MODEL), w_ada[0], b_cols, conv_w[0])
    conv_w_f = jnp.transpose(conv_w_g, (1, 0, 2)).reshape(3, D_FF)
    mix_sh = [w_in[0].astype(BF16), w_out[0].astype(BF16)]
    ffn_sh = [w_up[0].astype(BF16), w_down[0].astype(BF16)]
    ga_s, ga_r, ga_bufs, mod6 = split_start("gather_mix_ici_start", mix_sh + [landing(t) for t in mix_sh], _gather_ici_plan(2), 6, mod6)
    gb_s, gb_r, gb_bufs, (mod6, tc, tsa, tsb) = split_start("gather_ffn_ici_start", ffn_sh + [landing(t) for t in ffn_sh],
                                                            _gather_ici_plan(2), 6, [mod6, *_rope_tables(positions[0])])
    wbd = _block_diag(w_pool[0]).astype(BF16)
    b_pool2, scale2 = b_pool.reshape(1, POOL_W), pool_scale
    ga_bufs = split_wait("gather_mix_ici_wait", ga_s, ga_r, ga_bufs, _gather_ici_plan(2), mod6)
    gc_s, gc_r, mix_land, mod6 = split_start("gather_mix_d2d_start", ga_bufs[2:], _gather_d2d_plan(2), 6, mod6)
    w_in_g, w_out_g = split_wait("gather_mix_d2d_wait", gc_s, gc_r, mix_land, _gather_d2d_plan(2), mod6)

    h1, u, *qkv = inproj_fwd(x2, g_pre_mix, mod6, w_in_g, tc, tsa, tsb)
    mixed, pool = pool_fwd(u, wbd, b_pool2, scale2)
    o_l = [attn_fwd(t, d) for t, d in zip(qkv, DILATIONS)]
    attn_done = sum(l[0, :8, :128] for _, l in o_l)
    gb_bufs = split_wait("gather_ffn_ici_wait", gb_s, gb_r, gb_bufs, _gather_ici_plan(2), attn_done)
    gd_s, gd_r, ffn_land, pool = split_start("gather_ffn_d2d_start", gb_bufs[2:], _gather_d2d_plan(2), 6, pool)
    cat, lse, lse4, lse16, y1, x1, h2 = outproj_fwd([o for o, _ in o_l] + [l for _, l in o_l], pool, x2, w_out_g, g_post_mix,
                                                    g_pre_ffn, mod6)
    lses = [lse[None], lse4, lse16]
    w_up_g, w_down_g = split_wait("gather_ffn_d2d_wait", gd_s, gd_r, ffn_land, _gather_d2d_plan(2), h2)
    w_down_f = w_down_g.reshape(D_FF, D_MODEL)
    gate, val, dy2, dout, loss_v, d_gt_f, d_g_post_ffn = ffn_fwd(h2, w_up_g, conv_w_f, conv_b, w_down_f, x1, tgt, g_post_ffn, mod6)

    dgc, dval, d_conv_w, d_conv_b, dw_down, dw_up = down_bwd(dy2, w_down_f, gate, val, conv_w_f, conv_b, h2)
    dx1, dy1, d_sh_f, d_sc_f, d_g_pre_ffn, d_gt_m, d_g_post_mix, dw_up = up_bwd(
        dgc, dval, conv_w_f, w_up_g, x1, dout, y1, g_pre_ffn, g_post_mix, mod6, h2, dw_up)
    rs_ffn = GradReduce("ffn", [dw_up, dw_down.reshape(N_CHIPS, D_FF // N_CHIPS, D_MODEL)], [256, 176], place)
    dy1 = rs_ffn.d2d_start(dy1)
    dpool, da1, da4, da16, dl1, dl4, dl16, dw_out = outproj_bwd(dy1, w_out_g, cat)
    dpool = rs_ffn.add_and_ici_start(dw_out, dpool)
    du, d_wbd, d_b_pool, d_scale = pool_bwd(dpool, mixed, wbd, b_pool2, scale2)
    dqkv = [attn_bwd(t, da, ls, dl, d) for t, da, ls, dl, d in zip(qkv, (da1[None], da4, da16), lses, (dl1[None], dl4, dl16), DILATIONS)]
    grad_x, d_sh_m, d_sc_m, d_g_pre_mix, dw_in = inproj_bwd(dqkv, du, x2, dx1, w_in_g, g_pre_mix, mod6, tc, tsa, tsb, h1)

    z1 = jnp.zeros((1, D_MODEL), F32)
    slab_a = jnp.concatenate(
        [d_sh_m, d_sc_m, d_gt_m, d_sh_f, d_sc_f, d_gt_f, d_g_pre_mix, d_g_post_mix, d_g_pre_ffn, d_g_post_ffn,
         jnp.concatenate([d_b_pool, d_scale, loss_v, jnp.zeros((1, 384), F32)], axis=1)] + [z1] * 5, axis=0)
    slab_b = jnp.concatenate([d_conv_w, d_conv_b, jnp.zeros((4, D_FF), F32)], axis=0)
    d_wpool = jnp.concatenate([d_wbd[gi * 64:(gi + 1) * 64, gi * 64:(gi + 1) * 64] for gi in range(4)], axis=0)
    dev = _dev_id(xi, yi, ci)
    small_src = [slab_a, slab_b, d_wpool]
    small_land = [lax.dynamic_update_slice(lax.empty((N_DEV,) + t.shape, F32), t[None], (dev, 0, 0)) for t in small_src]
    tok = jnp.zeros((8, 128), F32)
    gs_s, gs_r, gs_bufs, tok = split_start("small_ici_start", small_src + small_land, _small_ici_plan(3), 12, tok)
    rs_mix = GradReduce("mix", [dw_in, dw_out], [256, 256], place)
    tok = rs_mix.d2d_start(tok)
    tok = rs_ffn.final_and_share_start(tok, tok)
    gs_bufs = split_wait("small_ici_wait", gs_s, gs_r, gs_bufs, _small_ici_plan(3), tok)
    gt_s, gt_r, small_land, tok = split_start("small_d2d_start", gs_bufs[3:], _small_d2d_plan(3), 9, tok)
    tok = rs_mix.add_and_ici_start(tok, tok)
    slab_a_g, slab_b_g, wpool_g = split_wait("small_d2d_wait", gt_s, gt_r, small_land, _small_d2d_plan(3), tok)
    cw_cols = conv_w.shape[2]
    convw_g = lax.dynamic_slice(slab_b_g, (0, 0, chip * cw_cols), (N_DEV, 3, cw_cols))
    dmod_cols = lax.dynamic_slice(slab_a_g[:, :6, :].reshape(N_DEV, 6 * D_MODEL), (0, chip * cb_ada), (N_DEV, cb_ada))

    res = {}

    def big_adamw(name, w, g, m, v, tr):
        g_, d_, m_, v_ = adamw_rows(w[0], g, m[0], v[0], tr, "adamw_" + name)
        res[name] = (g_[None], d_[None], m_[None], v_[None])
        return v_

    g_ada, d_ada, m_ada, v_ada = adamw_ada(c_all.reshape(N_DEV, D_MODEL).T, dmod_cols, w_ada[0], m_w_ada[0], v_w_ada[0])
    res["w_ada"] = (g_ada[None], d_ada[None], m_ada[None], v_ada[None])
    g_w_up, g_w_down = rs_ffn.finish(v_ada)
    big_adamw("w_up", w_up, g_w_up, m_w_up, v_w_up, 256)
    last = big_adamw("w_down", w_down, g_w_down, m_w_down, v_w_down, 352)
    rs_mix.final_and_share_start(last, jnp.zeros((8, 128), F32))
    g_w_in, g_w_out = rs_mix.finish(last)
    big_adamw("w_in", w_in, g_w_in, m_w_in, v_w_in, 256)
    big_adamw("w_out", w_out, g_w_out, m_w_out, v_w_out, 256)
    small, loss_sum = adamw_small(slab_a_g, slab_b_g, convw_g, wpool_g, {
        "b_ada": (b_ada, m_b_ada, v_b_ada), "g_pre_mix": (g_pre_mix, m_g_pre_mix, v_g_pre_mix),
        "g_post_mix": (g_post_mix, m_g_post_mix, v_g_post_mix), "g_pre_ffn": (g_pre_ffn, m_g_pre_ffn, v_g_pre_ffn),
        "g_post_ffn": (g_post_ffn, m_g_post_ffn, v_g_post_ffn), "b_pool": (b_pool, m_b_pool, v_b_pool),
        "pool_scale": (pool_scale, m_pool_scale, v_pool_scale), "conv_b": (conv_b, m_conv_b, v_conv_b),
        "conv_w": (conv_w, m_conv_w, v_conv_w), "w_pool": (w_pool, m_w_pool, v_w_pool)})
    for name in ("b_ada", "g_pre_mix", "g_post_mix", "g_pre_ffn", "g_post_ffn", "pool_scale", "conv_b", "b_pool", "w_pool", "conv_w"):
        res[name] = tuple(small[name])

    loss = loss_sum[0, 0]
    order = ["w_ada", "b_ada", "g_pre_mix", "g_post_mix", "g_pre_ffn", "g_post_ffn", "w_in", "w_pool", "b_pool", "pool_scale",
             "w_out", "w_up", "conv_w", "conv_b", "w_down"]
    outs = [loss, grad_x[None]]
    for k in range(4):
        outs += [res[n][k] for n in order]
    return tuple(outs)
```

```python
import math

import jax
import jax.numpy as jnp
from jax import lax
from jax.experimental import pallas as pl
from jax.experimental.pallas import tpu as pltpu

F32 = jnp.float32
BF16 = jnp.bfloat16
MESH = pl.DeviceIdType.MESH

D_MODEL = 1024
HEAD_DIM = 64
POOL_W = 256
GROUP_W = 256
DILATIONS = (1, 4, 16)
ATT_BLOCK = 128
IN_W = 2560
D_FF = 2816
HALF_FF = 1408
ROT_DIM = 16
ROPE_THETA = 500000.0
NORM_EPS = 1e-6
N_CHIPS = 4
N_DEV = 8
NEG = -1e30

ADAM_LR = 0.001
ADAM_B1 = 0.9
ADAM_B2 = 0.999
ADAM_EPS = 1e-08
ADAM_WD = 0.01
ADAM_STEP = 10

VMEM_LIMIT = 56 * 1024 * 1024

NT = (((1,), (1,)), ((), ()))
TN = (((0,), (0,)), ((), ()))


def _params(n_grid=0, **kw):
    sem = ("arbitrary",) * n_grid if n_grid else None
    return pltpu.CompilerParams(dimension_semantics=sem, vmem_limit_bytes=VMEM_LIMIT, **kw)


def _full(shape):
    nd = len(shape)
    return pl.BlockSpec(tuple(shape), lambda *_: (0,) * nd, pipeline_mode=pl.Buffered(1))


def _rows(tm, ncol):
    return pl.BlockSpec((tm, ncol), lambda i: (i, 0))


def _acc(ref, val):
    @pl.when(pl.program_id(0) == 0)
    def _():
        ref[...] = jnp.zeros_like(ref)

    ref[...] += val


def _colsum(v):
    return jnp.sum(v, axis=0, keepdims=True)


def _rope128(t, cs, sa, sb, sign):
    return t * cs + sign * (pltpu.roll(t, 8, 1) * sa + pltpu.roll(t, 120, 1) * sb)


FF_CHUNKS = tuple((ch, off, w) for ch in range(2) for off, w in ((0, 512), (512, 512), (1024, 384)))
GELU_C0 = math.sqrt(2.0 / math.pi)
GELU_C1 = GELU_C0 * 0.044715


def _gelu(z):
    z2 = z * z
    t = jnp.tanh(z * (GELU_C0 + GELU_C1 * z2))
    u = 0.5 * t + 0.5
    return z * u, u, t, z2


def _gelu_grad(z, u, t, z2):
    return u + (z * (GELU_C0 + (3.0 * GELU_C1) * z2)) * (0.5 - 0.5 * (t * t))


def _conv_taps(gate, halo, first):
    row = lax.broadcasted_iota(jnp.int32, gate.shape, 0)
    halo = jnp.where(first, 0.0, halo)
    nh = halo.shape[0]
    p1 = halo[nh - 1:nh, :]
    p2 = halo[nh - 2:nh - 1, :]
    g1 = jnp.where(row == 0, p1, pltpu.roll(gate, 1, 0))
    g2 = jnp.where(row == 0, p2, jnp.where(row == 1, p1, pltpu.roll(gate, 2, 0)))
    return g1, g2


def inproj_fwd(x, g, mod6, w_in_g, tc, tsa, tsb, tm=512):
    S = x.shape[0]

    def body(x_ref, g_ref, mod_ref, w_ref, tc_ref, tsa_ref, tsb_ref, h_ref, u_ref, q1_ref, q4_ref, q16_ref, scr):
        qkv_refs = (q1_ref, q4_ref, q16_ref)
        xv = x_ref[...]
        rstd = lax.rsqrt(jnp.mean(xv * xv, axis=-1, keepdims=True) + NORM_EPS)
        h = ((xv * rstd) * g_ref[...]) * (1.0 + mod_ref[1:2, :]) + mod_ref[0:1, :]
        hb = h.astype(BF16)
        h_ref[...] = hb
        cs, sa, sb = tc_ref[...], tsa_ref[...], tsb_ref[...]
        for j in range(N_CHIPS):
            res = jnp.dot(hb, w_ref[j], preferred_element_type=F32)
            for t in range(5):
                sp = 5 * j + t
                piece, half = sp // 2, sp % 2
                blk = res[:, t * 128:(t + 1) * 128]
                lanes = slice(half * 128, (half + 1) * 128)
                if piece == 0:
                    u_ref[:, lanes] = blk
                else:
                    kind, gi = (piece - 1) // 3, (piece - 1) % 3
                    if kind == 0:
                        blk = _rope128(blk, cs, sa, sb, 1.0) * (HEAD_DIM ** -0.5)
                    elif kind == 1:
                        blk = _rope128(blk, cs, sa, sb, 1.0)
                    d = DILATIONS[gi]
                    if d == 1:
                        q1_ref[kind, 0, :, lanes] = blk.astype(BF16)
                    else:
                        scr[...] = blk
                        for r in range(d):
                            qkv_refs[gi][kind, r, :, lanes] = scr[pl.ds(r, tm // d, stride=d), :].astype(BF16)

    cls = lambda d: pl.BlockSpec((3, d, tm // d, GROUP_W), lambda i: (0, 0, i, 0))
    return pl.pallas_call(
        body,
        name="inproj_fwd",
        grid=(S // tm,),
        in_specs=[_rows(tm, D_MODEL), _full((1, D_MODEL)), _full((6, D_MODEL)), _full(w_in_g.shape),
                  _rows(tm, 128), _rows(tm, 128), _rows(tm, 128)],
        out_specs=[_rows(tm, D_MODEL), _rows(tm, POOL_W)] + [cls(d) for d in DILATIONS],
        out_shape=[jax.ShapeDtypeStruct((S, D_MODEL), BF16), jax.ShapeDtypeStruct((S, POOL_W), F32)]
        + [jax.ShapeDtypeStruct((3, d, S // d, GROUP_W), BF16) for d in DILATIONS],
        scratch_shapes=[pltpu.VMEM((tm, 128), F32)],
        compiler_params=_params(1),
    )(x, g, mod6, w_in_g, tc, tsa, tsb)


def _attn_masks():
    row = lax.broadcasted_iota(jnp.int32, (2 * ATT_BLOCK, 2 * ATT_BLOCK), 0) % ATT_BLOCK
    col = lax.broadcasted_iota(jnp.int32, (2 * ATT_BLOCK, 2 * ATT_BLOCK), 1)
    band = (col >= row) & (col <= row + ATT_BLOCK)
    lane = lax.broadcasted_iota(jnp.int32, (ATT_BLOCK, 128), 1)
    return band, col, lane < HEAD_DIM


def _classes_per_step(d, nb):
    return min(d, max(1, 8 // nb))


def _stack_heads(t, lo):
    z = jnp.zeros_like(t)
    return jnp.concatenate([jnp.where(lo, t, z), jnp.where(lo, z, t)], axis=0)


def _unstack_heads(t2, lo):
    return jnp.where(lo, t2[:ATT_BLOCK], t2[ATT_BLOCK:])


def attn_fwd(qkv, d):
    L = qkv.shape[2]
    nb = L // ATT_BLOCK
    cpb = _classes_per_step(d, nb)

    def body(q_ref, k_ref, v_ref, o_ref, l_ref, kpad, vpad):
        for cls in range(cpb):
            kpad[cls, 0:ATT_BLOCK, :] = jnp.zeros((ATT_BLOCK, GROUP_W), BF16)
            vpad[cls, 0:ATT_BLOCK, :] = jnp.zeros((ATT_BLOCK, GROUP_W), BF16)
            kpad[cls, ATT_BLOCK:, :] = k_ref[cls]
            vpad[cls, ATT_BLOCK:, :] = v_ref[cls]
        band, col, lo = _attn_masks()

        def step(t, carry):
            cls, n = t // nb, t % nb
            r0 = pl.multiple_of(n * ATT_BLOCK, ATT_BLOCK)
            valid = band & ((col >= ATT_BLOCK) | (n > 0))
            qb = q_ref[cls, pl.ds(r0, ATT_BLOCK), :]
            kb = kpad[cls, pl.ds(r0, 2 * ATT_BLOCK), :]
            vb = vpad[cls, pl.ds(r0, 2 * ATT_BLOCK), :]
            for pair in range(2):
                lanes = slice(pair * 128, (pair + 1) * 128)
                qp, kp, vp = qb[:, lanes], kb[:, lanes], vb[:, lanes]
                s = lax.dot_general(_stack_heads(qp, lo), kp, NT, preferred_element_type=F32)
                s = jnp.where(valid, s, NEG)
                m = jnp.max(s, axis=1, keepdims=True)
                p = jnp.exp(s - m)
                den = jnp.sum(p, axis=1, keepdims=True)
                pv = jnp.dot(p.astype(BF16), vp, preferred_element_type=F32)
                o_ref[cls, pl.ds(r0, ATT_BLOCK), lanes] = _unstack_heads(pv / den, lo).astype(BF16)
                l_ref[cls, pl.ds(r0, ATT_BLOCK), lanes] = _unstack_heads(jnp.broadcast_to(m + jnp.log(den), pv.shape), lo)
            return carry

        lax.fori_loop(0, cpb * nb, step, 0, unroll=4)

    spec = lambda kind: pl.BlockSpec((None, cpb, L, GROUP_W), lambda r: (kind, r, 0, 0))
    return pl.pallas_call(
        body,
        name=f"attn_fwd_d{d}",
        grid=(d // cpb,),
        in_specs=[spec(0), spec(1), spec(2)],
        out_specs=[pl.BlockSpec((cpb, L, GROUP_W), lambda r: (r, 0, 0))] * 2,
        out_shape=[jax.ShapeDtypeStruct((d, L, GROUP_W), BF16), jax.ShapeDtypeStruct((d, L, GROUP_W), F32)],
        scratch_shapes=[pltpu.VMEM((cpb, L + ATT_BLOCK, GROUP_W), BF16)] * 2,
        compiler_params=_params(1),
    )(qkv, qkv, qkv)


def _pool_lane_windows(shape):
    lane = lax.broadcasted_iota(jnp.int32, shape, 1)
    return lane, jnp.where(lane < 64, 2, jnp.where(lane < 128, 4, jnp.where(lane < 192, 8, 16)))


def pool_fwd(u, wbd, b, scale):
    S = u.shape[0]

    def body(u_ref, w_ref, b_ref, s_ref, mixed_ref, out_ref):
        uv = u_ref[...]
        row = lax.broadcasted_iota(jnp.int32, uv.shape, 0)
        lane, win = _pool_lane_windows(uv.shape)

        def shift(a, k):
            return jnp.where(row >= k, pltpu.roll(a, k, 0), 0.0)

        s2 = uv + shift(uv, 1)
        s4 = s2 + shift(s2, 2)
        s8 = s4 + shift(s4, 4)
        s16 = s8 + shift(s8, 8)
        tsum = jnp.where(lane < 64, s2, jnp.where(lane < 128, s4, jnp.where(lane < 192, s8, s16)))
        cnt = jnp.minimum(row + 1, win).astype(F32)
        mb = (tsum / cnt - uv).astype(BF16)
        mixed_ref[...] = mb
        y = jnp.dot(mb, w_ref[...], preferred_element_type=F32) + b_ref[...]
        out_ref[...] = (y * s_ref[...]).astype(BF16)

    vm = pl.BlockSpec(memory_space=pltpu.VMEM)
    return pl.pallas_call(
        body,
        name="pool_fwd",
        in_specs=[vm] * 4,
        out_specs=[vm] * 2,
        out_shape=[jax.ShapeDtypeStruct((S, POOL_W), BF16)] * 2,
        compiler_params=_params(),
    )(u, wbd, b, scale)


def outproj_fwd(o_l, pool, x, w_out_g, g_post, g_pre, mod6, tm=512):
    S = x.shape[0]

    def body(o0, o1, o2, l0, l1, l2, pool_ref, x_ref, w_ref, gpost_ref, gpre_ref, mod_ref,
             cat_ref, lse_ref, lse4_ref, lse16_ref, y1_ref, x1_ref, h2_ref, so4, sl4, so16, sl16):
        for d, src, dst in ((4, o1, so4), (4, l1, sl4), (16, o2, so16), (16, l2, sl16)):
            for r in range(d):
                for h in range(2):
                    dst[h, pl.ds(r, tm // d, stride=d), :] = src[r, :, h * 128:(h + 1) * 128].astype(F32)
        nat = lambda ref: jnp.concatenate([ref[0], ref[1]], axis=1)
        a, b, c = l0[0], nat(sl4), nat(sl16)
        m = jnp.maximum(jnp.maximum(a, b), c)
        e0, e1, e2 = jnp.exp(a - m), jnp.exp(b - m), jnp.exp(c - m)
        z = e0 + e1 + e2
        lse = m + jnp.log(z)
        lse_ref[...] = lse
        for h in range(2):
            sl4[h] = lse[:, h * 128:(h + 1) * 128]
        for d, dst in ((4, lse4_ref), (16, lse16_ref)):
            for r in range(d):
                for h in range(2):
                    dst[r, :, h * 128:(h + 1) * 128] = sl4[h, pl.ds(r, tm // d, stride=d), :]
        attn = (e0 * o0[0].astype(F32) + e1 * nat(so4) + e2 * nat(so16)) / z
        cat = jnp.concatenate([pool_ref[...], attn.astype(BF16)], axis=1)
        cat_ref[...] = cat
        y1 = jnp.concatenate([jnp.dot(cat, w_ref[j], preferred_element_type=F32) for j in range(N_CHIPS)], axis=1)
        y1_ref[...] = y1.astype(BF16)
        rstd = lax.rsqrt(jnp.mean(y1 * y1, axis=-1, keepdims=True) + NORM_EPS)
        x1 = x_ref[...] + mod_ref[2:3, :] * ((y1 * rstd) * gpost_ref[...])
        x1_ref[...] = x1
        rstd2 = lax.rsqrt(jnp.mean(x1 * x1, axis=-1, keepdims=True) + NORM_EPS)
        h2 = ((x1 * rstd2) * gpre_ref[...]) * (1.0 + mod_ref[4:5, :]) + mod_ref[3:4, :]
        h2_ref[...] = h2.astype(BF16)

    t256 = _rows(tm, GROUP_W)
    cls = lambda d: pl.BlockSpec((d, tm // d, GROUP_W), lambda i: (0, i, 0))
    cls_shape = lambda d: jax.ShapeDtypeStruct((d, S // d, GROUP_W), F32)
    return pl.pallas_call(
        body,
        name="outproj_fwd",
        grid=(S // tm,),
        in_specs=[cls(d) for d in DILATIONS] * 2 + [t256, _rows(tm, D_MODEL), _full(w_out_g.shape), _full((1, D_MODEL)),
                                                    _full((1, D_MODEL)), _full((6, D_MODEL))],
        out_specs=[_rows(tm, 512), t256, cls(4), cls(16), _rows(tm, D_MODEL), _rows(tm, D_MODEL), _rows(tm, D_MODEL)],
        out_shape=[jax.ShapeDtypeStruct((S, 512), BF16), jax.ShapeDtypeStruct((S, GROUP_W), F32), cls_shape(4), cls_shape(16),
                   jax.ShapeDtypeStruct((S, D_MODEL), BF16), jax.ShapeDtypeStruct((S, D_MODEL), F32),
                   jax.ShapeDtypeStruct((S, D_MODEL), BF16)],
        scratch_shapes=[pltpu.VMEM((2, tm, 128), F32)] * 4,
        compiler_params=_params(1),
    )(*o_l, pool, x, w_out_g, g_post, g_pre, mod6)


def _halo_prev(tm, ncol):
    return pl.BlockSpec((16, ncol), lambda i: (jnp.maximum(i * (tm // 16) - 1, 0), 0))


def ffn_fwd(h2, w_up_g, conv_w, conv_b, w_down, x1, target, g_post, mod6, tm=512):
    S = x1.shape[0]

    def body(h_ref, wu_ref, cw_ref, cb_ref, wd_ref, x1_ref, tgt_ref, g_ref, mod_ref,
             gate_ref, val_ref, dy2_ref, dout_ref, loss_ref, dgt_ref, dg_ref, carry):
        first = pl.program_id(0) == 0

        @pl.when(first)
        def _():
            carry[...] = jnp.zeros_like(carry)

        hb = h_ref[...]
        y2 = jnp.zeros((tm, D_MODEL), F32)
        for ch in range(2):
            cols = slice(ch * HALF_FF, (ch + 1) * HALF_FF)
            gb = jnp.dot(hb, wu_ref[ch], preferred_element_type=F32).astype(BF16)
            vb = jnp.dot(hb, wu_ref[2 + ch], preferred_element_type=F32).astype(BF16)
            gate_ref[:, cols] = gb
            val_ref[:, cols] = vb
            gt = gb.astype(F32)
            g1, g2 = _conv_taps(gt, carry[:, cols], first)
            carry[:, cols] = gt[tm - 8:, :]
            gc = g2 * cw_ref[0:1, cols] + g1 * cw_ref[1:2, cols] + gt * cw_ref[2:3, cols] + cb_ref[:, cols]
            ab = _gelu(gc.astype(BF16))[0] * vb
            y2 = y2 + jnp.dot(ab, wd_ref[cols, :], preferred_element_type=F32)
        rstd = lax.rsqrt(jnp.mean(y2 * y2, axis=-1, keepdims=True) + NORM_EPS)
        y2n = y2 * rstd
        gv = g_ref[...]
        gtf = mod_ref[5:6, :]
        r2 = y2n * gv
        diff = (x1_ref[...] + gtf * r2) - tgt_ref[...]
        _acc(loss_ref, jnp.zeros((1, 128), F32) + 0.5 * jnp.sum(diff * diff) * (1.0 / D_MODEL))
        dout = diff * (1.0 / D_MODEL)
        dout_ref[...] = dout
        _acc(dgt_ref, _colsum(dout * r2))
        dr2 = dout * gtf
        _acc(dg_ref, _colsum(dr2 * y2n))
        dyn = dr2 * gv
        dy2 = rstd * (dyn - y2n * jnp.mean(dyn * y2n, axis=-1, keepdims=True))
        dy2_ref[...] = dy2.astype(BF16)

    vec = _full((1, D_MODEL))
    return pl.pallas_call(
        body,
        name="ffn_fwd",
        grid=(S // tm,),
        in_specs=[_rows(tm, D_MODEL), _full(w_up_g.shape), _full((3, D_FF)), _full((1, D_FF)), _full((D_FF, D_MODEL)),
                  _rows(tm, D_MODEL), _rows(tm, D_MODEL), vec, _full((6, D_MODEL))],
        out_specs=[_rows(tm, D_FF), _rows(tm, D_FF), _rows(tm, D_MODEL), _rows(tm, D_MODEL), _full((1, 128)), vec, vec],
        out_shape=[jax.ShapeDtypeStruct((S, D_FF), BF16)] * 2 + [jax.ShapeDtypeStruct((S, D_MODEL), BF16),
                                                                 jax.ShapeDtypeStruct((S, D_MODEL), F32),
                                                                 jax.ShapeDtypeStruct((1, 128), F32),
                                                                 jax.ShapeDtypeStruct((1, D_MODEL), F32),
                                                                 jax.ShapeDtypeStruct((1, D_MODEL), F32)],
        scratch_shapes=[pltpu.VMEM((8, D_FF), F32)],
        compiler_params=_params(1),
    )(h2, w_up_g, conv_w, conv_b, w_down, x1, target, g_post, mod6)


def down_bwd(dy2, w_down, gate, val, conv_w, conv_b, h2, tm=512):
    S = dy2.shape[0]

    def body(dy_ref, w_ref, gate_ref, halo_ref, val_ref, cw_ref, cb_ref, h_ref,
             dgc_ref, dval_ref, dcw_ref, dcb_ref, dwd_ref, dwu_ref):
        first = pl.program_id(1) == 0

        @pl.when(first)
        def _():
            dcw_ref[...] = jnp.zeros_like(dcw_ref)
            dcb_ref[...] = jnp.zeros_like(dcb_ref)
            dwd_ref[...] = jnp.zeros_like(dwd_ref)
            dwu_ref[...] = jnp.zeros_like(dwu_ref)

        dyb = dy_ref[...]
        hb = h_ref[...]
        pieces = [(off, w) for ch, off, w in FF_CHUNKS if ch == 0]

        def col(i):
            return slice(pieces[i][0], pieces[i][0] + pieces[i][1])

        def mm_da(i):
            return lax.dot_general(dyb, w_ref[col(i), :], NT, preferred_element_type=F32)

        def elementwise(i, da):
            cols = col(i)
            gt = gate_ref[:, cols].astype(F32)
            g1, g2 = _conv_taps(gt, halo_ref[:, cols].astype(F32), first)
            gc = g2 * cw_ref[0:1, cols] + g1 * cw_ref[1:2, cols] + gt * cw_ref[2:3, cols] + cb_ref[:, cols]
            zb, dab, vb = gc.astype(BF16), da.astype(BF16), val_ref[:, cols]
            ge, u, th, z2 = _gelu(zb)
            dgb = dab * vb * _gelu_grad(zb, u, th, z2)
            dgc_ref[:, cols] = dgb
            dgc = dgb.astype(F32)
            dvb = dab * ge
            dval_ref[:, cols] = dvb
            dcb_ref[:, cols] += _colsum(dgc)
            dcw_ref[0:1, cols] += _colsum(dgc * g2)
            dcw_ref[1:2, cols] += _colsum(dgc * g1)
            dcw_ref[2:3, cols] += _colsum(dgc * gt)
            return dvb, ge * vb

        def mm_dw(i, dvb_ab):
            dvb, ab = dvb_ab
            dwd_ref[col(i), :] += lax.dot_general(ab, dyb, TN, preferred_element_type=F32)
            dwu_ref[:, col(i)] += lax.dot_general(hb, dvb, TN, preferred_element_type=F32)

        n = len(pieces)
        da = mm_da(0)
        prev = None
        for i in range(n):
            nxt = mm_da(i + 1) if i + 1 < n else None
            if prev is not None:
                mm_dw(i - 1, prev)
            prev = elementwise(i, da)
            da = nxt
        mm_dw(n - 1, prev)

    one = pl.Buffered(1)
    tok = pl.BlockSpec((tm, D_MODEL), lambda c, i: (i, 0))
    ff = pl.BlockSpec((tm, HALF_FF), lambda c, i: (i, c))
    halo = pl.BlockSpec((16, HALF_FF), lambda c, i: (jnp.maximum(i * (tm // 16) - 1, 0), c))
    per_half = lambda rows: pl.BlockSpec((rows, HALF_FF), lambda c, i: (0, c), pipeline_mode=one)
    return pl.pallas_call(
        body,
        name="down_bwd",
        grid=(2, S // tm),
        in_specs=[tok, pl.BlockSpec((HALF_FF, D_MODEL), lambda c, i: (c, 0), pipeline_mode=one), ff, halo, ff,
                  per_half(3), per_half(1), tok],
        out_specs=[ff, ff, per_half(3), per_half(1), pl.BlockSpec((HALF_FF, D_MODEL), lambda c, i: (c, 0), pipeline_mode=one),
                   pl.BlockSpec((None, D_MODEL, HALF_FF), lambda c, i: (2 + c, 0, 0), pipeline_mode=one)],
        out_shape=[jax.ShapeDtypeStruct((S, D_FF), BF16), jax.ShapeDtypeStruct((S, D_FF), BF16),
                   jax.ShapeDtypeStruct((3, D_FF), F32), jax.ShapeDtypeStruct((1, D_FF), F32),
                   jax.ShapeDtypeStruct((D_FF, D_MODEL), F32), jax.ShapeDtypeStruct((N_CHIPS, D_MODEL, HALF_FF), F32)],
        compiler_params=_params(2),
    )(dy2, w_down, gate, gate, val, conv_w, conv_b, h2)


def up_bwd(dgc, dval, conv_w, w_up_g, x1, dout, y1, g_pre, g_post, mod6, h2, dw_up, tm=256):
    S = x1.shape[0]
    last_blk = S // 16 - 1

    def body(dgc_ref, nxt_ref, dval_ref, cw_ref, w_ref, x1_ref, dout_ref, y1_ref, gpre_ref, gpost_ref, mod_ref, h_ref, dwin_ref,
             dx1_ref, dy1_ref, dsh_ref, dsc_ref, dgpre_ref, dgt_ref, dgpost_ref, dwu_ref):
        last = pl.program_id(0) == pl.num_programs(0) - 1

        @pl.when(pl.program_id(0) == 0)
        def _():
            dwu_ref[...] = jnp.zeros_like(dwu_ref)

        hb = h_ref[...]
        dh = jnp.zeros((tm, D_MODEL), F32)
        for ch in range(2):
            cols = slice(ch * HALF_FF, (ch + 1) * HALF_FF)
            dg = dgc_ref[:, cols].astype(F32)
            nx = jnp.where(last, 0.0, nxt_ref[:, cols].astype(F32))
            row = lax.broadcasted_iota(jnp.int32, dg.shape, 0)
            n0, n1 = nx[0:1, :], nx[1:2, :]
            u1 = jnp.where(row == tm - 1, n0, pltpu.roll(dg, tm - 1, 0))
            u2 = jnp.where(row == tm - 1, n1, jnp.where(row == tm - 2, n0, pltpu.roll(dg, tm - 2, 0)))
            dgate = (dg * cw_ref[2:3, cols] + u1 * cw_ref[1:2, cols] + u2 * cw_ref[0:1, cols]).astype(BF16)
            dwu_ref[ch] += lax.dot_general(hb, dgate, TN, preferred_element_type=F32)
            dh = dh + lax.dot_general(dgate, w_ref[ch], NT, preferred_element_type=F32)
            dh = dh + lax.dot_general(dval_ref[:, cols], w_ref[2 + ch], NT, preferred_element_type=F32)
        x1 = x1_ref[...]
        rstd = lax.rsqrt(jnp.mean(x1 * x1, axis=-1, keepdims=True) + NORM_EPS)
        n2 = x1 * rstd
        gpre = gpre_ref[...]
        one_sc = 1.0 + mod_ref[4:5, :]
        _acc(dsh_ref, _colsum(dh))
        _acc(dsc_ref, _colsum(dh * (n2 * gpre)))
        _acc(dgpre_ref, _colsum(dh * one_sc * n2))
        dn = dh * (gpre * one_sc)
        dx1 = dout_ref[...] + rstd * (dn - n2 * jnp.mean(dn * n2, axis=-1, keepdims=True))
        dx1_ref[...] = dx1
        y1 = y1_ref[...].astype(F32)
        rstd1 = lax.rsqrt(jnp.mean(y1 * y1, axis=-1, keepdims=True) + NORM_EPS)
        y1n = y1 * rstd1
        gpost = gpost_ref[...]
        gtm = mod_ref[2:3, :]
        _acc(dgt_ref, _colsum(dx1 * (y1n * gpost)))
        dr1 = dx1 * gtm
        _acc(dgpost_ref, _colsum(dr1 * y1n))
        dyn = dr1 * gpost
        dy1 = rstd1 * (dyn - y1n * jnp.mean(dyn * y1n, axis=-1, keepdims=True))
        dy1_ref[...] = dy1.astype(BF16)

    vec = _full((1, D_MODEL))
    nxt = pl.BlockSpec((16, D_FF), lambda i: (jnp.minimum((i + 1) * (tm // 16), last_blk), 0))
    return pl.pallas_call(
        body,
        name="up_bwd",
        grid=(S // tm,),
        in_specs=[_rows(tm, D_FF), nxt, _rows(tm, D_FF), _full((3, D_FF)), _full(w_up_g.shape), _rows(tm, D_MODEL),
                  _rows(tm, D_MODEL), _rows(tm, D_MODEL), vec, vec, _full((6, D_MODEL)), _rows(tm, D_MODEL),
                  pl.BlockSpec(memory_space=pl.ANY)],
        out_specs=[_rows(tm, D_MODEL), _rows(tm, D_MODEL), vec, vec, vec, vec, vec,
                   pl.BlockSpec((2, D_MODEL, HALF_FF), lambda i: (0, 0, 0), pipeline_mode=pl.Buffered(1))],
        out_shape=[jax.ShapeDtypeStruct((S, D_MODEL), F32), jax.ShapeDtypeStruct((S, D_MODEL), BF16)]
        + [jax.ShapeDtypeStruct((1, D_MODEL), F32)] * 5 + [jax.ShapeDtypeStruct(dw_up.shape, F32)],
        input_output_aliases={12: 7},
        compiler_params=_params(1),
    )(dgc, dgc, dval, conv_w, w_up_g, x1, dout, y1, g_pre, g_post, mod6, h2, dw_up)


def outproj_bwd(dy1, w_out_g, cat, tm=512):
    S = dy1.shape[0]

    def body(dy_ref, w_ref, cat_ref, dpool_ref, dattn_ref, da4_ref, da16_ref, delta_ref, dl4_ref, dl16_ref, dw_ref, scr):
        @pl.when(pl.program_id(0) == 0)
        def _():
            dw_ref[...] = jnp.zeros_like(dw_ref)

        catb = cat_ref[...]
        dcat = jnp.zeros((tm, 512), F32)
        for j in range(N_CHIPS):
            dyj = dy_ref[:, j * 256:(j + 1) * 256]
            dcat = dcat + lax.dot_general(dyj, w_ref[j], NT, preferred_element_type=F32)
            dw_ref[j] += lax.dot_general(catb, dyj, TN, preferred_element_type=F32)
        dpool_ref[...] = dcat[:, :POOL_W]
        dattn = dcat[:, POOL_W:]
        dattn_ref[...] = dattn.astype(BF16)
        for h in range(2):
            scr[h] = dattn[:, h * 128:(h + 1) * 128]
        for d, dst in ((4, da4_ref), (16, da16_ref)):
            for r in range(d):
                for h in range(2):
                    dst[r, :, h * 128:(h + 1) * 128] = scr[h, pl.ds(r, tm // d, stride=d), :].astype(BF16)
        prod = dattn * catb[:, POOL_W:].astype(F32)
        r = lax.broadcasted_iota(jnp.int32, (GROUP_W, GROUP_W), 0) // HEAD_DIM
        c = lax.broadcasted_iota(jnp.int32, (GROUP_W, GROUP_W), 1) // HEAD_DIM
        ones_bd = jnp.where(r == c, 1.0, 0.0).astype(BF16)
        hi = prod.astype(BF16)
        lo = (prod - hi.astype(F32)).astype(BF16)
        delta = jnp.dot(hi, ones_bd, preferred_element_type=F32) + jnp.dot(lo, ones_bd, preferred_element_type=F32)
        delta_ref[...] = delta
        for h in range(2):
            scr[h] = delta[:, h * 128:(h + 1) * 128]
        for d, dst in ((4, dl4_ref), (16, dl16_ref)):
            for r in range(d):
                for h in range(2):
                    dst[r, :, h * 128:(h + 1) * 128] = scr[h, pl.ds(r, tm // d, stride=d), :]

    cls = lambda d: pl.BlockSpec((d, tm // d, GROUP_W), lambda i: (0, i, 0))
    cls_shape = lambda d, dt: jax.ShapeDtypeStruct((d, S // d, GROUP_W), dt)
    return pl.pallas_call(
        body,
        name="outproj_bwd",
        grid=(S // tm,),
        in_specs=[_rows(tm, D_MODEL), _full(w_out_g.shape), _rows(tm, 512)],
        out_specs=[_rows(tm, POOL_W), _rows(tm, GROUP_W), cls(4), cls(16), _rows(tm, GROUP_W), cls(4), cls(16),
                   _full(w_out_g.shape)],
        out_shape=[jax.ShapeDtypeStruct((S, POOL_W), F32), jax.ShapeDtypeStruct((S, GROUP_W), BF16), cls_shape(4, BF16),
                   cls_shape(16, BF16), jax.ShapeDtypeStruct((S, GROUP_W), F32), cls_shape(4, F32), cls_shape(16, F32),
                   jax.ShapeDtypeStruct(w_out_g.shape, F32)],
        scratch_shapes=[pltpu.VMEM((2, tm, 128), F32)],
        compiler_params=_params(1),
    )(dy1, w_out_g, cat)


def attn_bwd(qkv, dattn, lse, delta, d):
    L = qkv.shape[2]
    nb = L // ATT_BLOCK
    cpb = _classes_per_step(d, nb)

    def body(q_ref, k_ref, v_ref, do_ref, l_ref, dl_ref, out_ref, kpad, vpad, dkpad, dvpad):
        for cls in range(cpb):
            kpad[cls, 0:ATT_BLOCK, :] = jnp.zeros((ATT_BLOCK, GROUP_W), BF16)
            vpad[cls, 0:ATT_BLOCK, :] = jnp.zeros((ATT_BLOCK, GROUP_W), BF16)
            kpad[cls, ATT_BLOCK:, :] = k_ref[cls]
            vpad[cls, ATT_BLOCK:, :] = v_ref[cls]
        dkpad[...] = jnp.zeros_like(dkpad)
        dvpad[...] = jnp.zeros_like(dvpad)
        band, col, lo = _attn_masks()

        def step(t, carry):
            cls, n = t // nb, t % nb
            r0 = pl.multiple_of(n * ATT_BLOCK, ATT_BLOCK)
            valid = band & ((col >= ATT_BLOCK) | (n > 0))
            qb = q_ref[cls, pl.ds(r0, ATT_BLOCK), :]
            dob = do_ref[cls, pl.ds(r0, ATT_BLOCK), :]
            lb = l_ref[cls, pl.ds(r0, ATT_BLOCK), :]
            dlb = dl_ref[cls, pl.ds(r0, ATT_BLOCK), :]
            kb = kpad[cls, pl.ds(r0, 2 * ATT_BLOCK), :]
            vb = vpad[cls, pl.ds(r0, 2 * ATT_BLOCK), :]
            for pair in range(2):
                lanes = slice(pair * 128, (pair + 1) * 128)
                qp, dop, kp, vp = qb[:, lanes], dob[:, lanes], kb[:, lanes], vb[:, lanes]
                c0, c1 = pair * 128, pair * 128 + HEAD_DIM
                q2, do2 = _stack_heads(qp, lo), _stack_heads(dop, lo)
                lse2 = jnp.concatenate([lb[:, c0:c0 + 1], lb[:, c1:c1 + 1]], axis=0)
                dl2 = jnp.concatenate([dlb[:, c0:c0 + 1], dlb[:, c1:c1 + 1]], axis=0)
                s = lax.dot_general(q2, kp, NT, preferred_element_type=F32)
                s = jnp.where(valid, s, NEG)
                p = jnp.exp(s - lse2)
                dp = lax.dot_general(do2, vp, NT, preferred_element_type=F32)
                ds = (p * (dp - dl2)).astype(BF16)
                dq2 = jnp.dot(ds, kp, preferred_element_type=F32)
                out_ref[0, cls, pl.ds(r0, ATT_BLOCK), lanes] = _unstack_heads(dq2, lo)
                dkpad[cls, pl.ds(r0, 2 * ATT_BLOCK), lanes] += lax.dot_general(ds, q2, TN, preferred_element_type=F32)
                dvpad[cls, pl.ds(r0, 2 * ATT_BLOCK), lanes] += lax.dot_general(p.astype(BF16), do2, TN, preferred_element_type=F32)
            return carry

        lax.fori_loop(0, cpb * nb, step, 0, unroll=4)
        for cls in range(cpb):
            out_ref[1, cls] = dkpad[cls, ATT_BLOCK:, :]
            out_ref[2, cls] = dvpad[cls, ATT_BLOCK:, :]

    spec = lambda kind: pl.BlockSpec((None, cpb, L, GROUP_W), lambda r: (kind, r, 0, 0))
    per_cls = pl.BlockSpec((cpb, L, GROUP_W), lambda r: (r, 0, 0))
    return pl.pallas_call(
        body,
        name=f"attn_bwd_d{d}",
        grid=(d // cpb,),
        in_specs=[spec(0), spec(1), spec(2), per_cls, per_cls, per_cls],
        out_specs=pl.BlockSpec((3, cpb, L, GROUP_W), lambda r: (0, r, 0, 0)),
        out_shape=jax.ShapeDtypeStruct((3, d, L, GROUP_W), F32),
        scratch_shapes=[pltpu.VMEM((cpb, L + ATT_BLOCK, GROUP_W), BF16)] * 2 + [pltpu.VMEM((cpb, L + ATT_BLOCK, GROUP_W), F32)] * 2,
        compiler_params=_params(1),
    )(qkv, qkv, qkv, dattn, lse, delta)


def pool_bwd(dpool, mixed, wbd, b, scale):
    S = dpool.shape[0]

    def body(dp_ref, mx_ref, w_ref, b_ref, s_ref, du_ref, dw_ref, db_ref, ds_ref):
        dp = dp_ref[...]
        mb = mx_ref[...]
        wv = w_ref[...]
        ypre = jnp.dot(mb, wv, preferred_element_type=F32) + b_ref[...]
        ds_ref[...] = _colsum(dp * ypre)
        dpre = dp * s_ref[...]
        db_ref[...] = _colsum(dpre)
        dpb = dpre.astype(BF16)
        dw_ref[...] = lax.dot_general(mb, dpb, TN, preferred_element_type=F32)
        dmix = lax.dot_general(dpb, wv, NT, preferred_element_type=F32)
        row = lax.broadcasted_iota(jnp.int32, dmix.shape, 0)
        lane, win = _pool_lane_windows(dmix.shape)
        e = dmix / jnp.minimum(row + 1, win).astype(F32)

        def shift(a, k):
            return jnp.where(row < S - k, pltpu.roll(a, S - k, 0), 0.0)

        f2 = e + shift(e, 1)
        f4 = f2 + shift(f2, 2)
        f8 = f4 + shift(f4, 4)
        f16 = f8 + shift(f8, 8)
        du_ref[...] = jnp.where(lane < 64, f2, jnp.where(lane < 128, f4, jnp.where(lane < 192, f8, f16))) - dmix

    vm = pl.BlockSpec(memory_space=pltpu.VMEM)
    return pl.pallas_call(
        body,
        name="pool_bwd",
        in_specs=[vm] * 5,
        out_specs=[vm] * 4,
        out_shape=[jax.ShapeDtypeStruct((S, POOL_W), F32), jax.ShapeDtypeStruct((POOL_W, POOL_W), F32),
                   jax.ShapeDtypeStruct((1, POOL_W), F32), jax.ShapeDtypeStruct((1, POOL_W), F32)],
        compiler_params=_params(),
    )(dpool, mixed, wbd, b, scale)


def inproj_bwd(dqkv, du, x, dx1, w_in_g, g, mod6, tc, tsa, tsb, h1, tm=512):
    S = x.shape[0]

    def body(d0, d1, d2, du_ref, x_ref, dx1_ref, w_ref, g_ref, mod_ref, tc_ref, tsa_ref, tsb_ref, h_ref,
             gx_ref, dsh_ref, dsc_ref, dg_ref, dw_ref, s4, s16, dp_ref):
        @pl.when(pl.program_id(0) == 0)
        def _():
            dw_ref[...] = jnp.zeros_like(dw_ref)

        cs, sa, sb = tc_ref[...], tsa_ref[...], tsb_ref[...]
        for d, src, dst in ((4, d1, s4), (16, d2, s16)):
            for kind in range(3):
                for r in range(d):
                    for h in range(2):
                        dst[kind, h, pl.ds(r, tm // d, stride=d), :] = src[kind, r, :, h * 128:(h + 1) * 128]
        for sp in range(20):
            piece, half = sp // 2, sp % 2
            lanes = slice(half * 128, (half + 1) * 128)
            if piece == 0:
                blk = du_ref[:, lanes]
            else:
                kind, gi = (piece - 1) // 3, (piece - 1) % 3
                blk = d0[kind, 0, :, lanes] if gi == 0 else (s4, s16)[gi - 1][kind, half]
                if kind == 0:
                    blk = _rope128(blk, cs, sa, sb, -1.0) * (HEAD_DIM ** -0.5)
                elif kind == 1:
                    blk = _rope128(blk, cs, sa, sb, -1.0)
            dp_ref[:, sp * 128:(sp + 1) * 128] = blk.astype(BF16)
        dh = jnp.zeros((tm, D_MODEL), F32)
        hbt = h_ref[...].T
        for j in range(N_CHIPS):
            dpj = dp_ref[:, j * 640:(j + 1) * 640]
            dh = dh + lax.dot_general(dpj, w_ref[j], NT, preferred_element_type=F32)
            dw_ref[j] += jnp.dot(hbt, dpj, preferred_element_type=F32)
        xv = x_ref[...]
        rstd = lax.rsqrt(jnp.mean(xv * xv, axis=-1, keepdims=True) + NORM_EPS)
        n1 = xv * rstd
        gv = g_ref[...]
        one_sc = 1.0 + mod_ref[1:2, :]
        _acc(dsh_ref, _colsum(dh))
        _acc(dsc_ref, _colsum(dh * (n1 * gv)))
        _acc(dg_ref, _colsum(dh * one_sc * n1))
        dn = dh * (gv * one_sc)
        gx_ref[...] = dx1_ref[...] + rstd * (dn - n1 * jnp.mean(dn * n1, axis=-1, keepdims=True))

    vec = _full((1, D_MODEL))
    dspec = lambda d: pl.BlockSpec((3, d, tm // d, GROUP_W), lambda i: (0, 0, i, 0))
    return pl.pallas_call(
        body,
        name="inproj_bwd",
        grid=(S // tm,),
        in_specs=[dspec(d) for d in DILATIONS] + [_rows(tm, POOL_W), _rows(tm, D_MODEL), _rows(tm, D_MODEL), _full(w_in_g.shape),
                                                  vec, _full((6, D_MODEL)), _rows(tm, 128), _rows(tm, 128), _rows(tm, 128),
                                                  _rows(tm, D_MODEL)],
        out_specs=[_rows(tm, D_MODEL), vec, vec, vec, _full(w_in_g.shape)],
        out_shape=[jax.ShapeDtypeStruct((S, D_MODEL), F32)] + [jax.ShapeDtypeStruct((1, D_MODEL), F32)] * 3
        + [jax.ShapeDtypeStruct(w_in_g.shape, F32)],
        scratch_shapes=[pltpu.VMEM((3, 2, tm, 128), F32)] * 2 + [pltpu.VMEM((tm, IN_W), BF16)],
        compiler_params=_params(1),
    )(*dqkv, du, x, dx1, w_in_g, g, mod6, tc, tsa, tsb, h1)


def _adamw(w, g, m, v):
    m = ADAM_B1 * m + (1.0 - ADAM_B1) * g
    v = ADAM_B2 * v + (1.0 - ADAM_B2) * (g * g)
    m_hat = m / (1.0 - ADAM_B1 ** ADAM_STEP)
    v_hat = v / (1.0 - ADAM_B2 ** ADAM_STEP)
    delta = -ADAM_LR * (m_hat / (jnp.sqrt(v_hat) + ADAM_EPS) + ADAM_WD * w)
    return delta, m, v


def adamw_rows(w, g, m, v, tr, name):
    R, C = w.shape

    def body(w_ref, g_ref, m_ref, v_ref, go_ref, d_ref, mo_ref, vo_ref):
        g = g_ref[...]
        go_ref[...] = g
        d_ref[...], mo_ref[...], vo_ref[...] = _adamw(w_ref[...], g, m_ref[...], v_ref[...])

    spec = pl.BlockSpec((tr, C), lambda i: (i, 0))
    return pl.pallas_call(
        body,
        name=name,
        grid=(R // tr,),
        in_specs=[spec] * 4,
        out_specs=[spec] * 4,
        out_shape=[jax.ShapeDtypeStruct((R, C), F32)] * 4,
        compiler_params=_params(1),
    )(w, g, m, v)


def adamw_ada(c_all_t, dmod_cols, w, m, v, tr=256):
    R, C = w.shape

    def body(ct_ref, dm_ref, w_ref, m_ref, v_ref, g_ref, d_ref, mo_ref, vo_ref):
        ct = ct_ref[...]
        act = ct * jax.nn.sigmoid(ct)
        dm = dm_ref[...]
        a_hi, d_hi = act.astype(BF16), dm.astype(BF16)
        a_lo, d_lo = (act - a_hi.astype(F32)).astype(BF16), (dm - d_hi.astype(F32)).astype(BF16)
        g = (jnp.dot(a_hi, d_hi, preferred_element_type=F32) + jnp.dot(a_lo, d_hi, preferred_element_type=F32)
             + jnp.dot(a_hi, d_lo, preferred_element_type=F32))
        g_ref[...] = g
        d_ref[...], mo_ref[...], vo_ref[...] = _adamw(w_ref[...], g, m_ref[...], v_ref[...])

    spec = pl.BlockSpec((tr, C), lambda i: (i, 0))
    return pl.pallas_call(
        body,
        name="adamw_ada",
        grid=(R // tr,),
        in_specs=[pl.BlockSpec((tr, N_DEV), lambda i: (i, 0)), _full((N_DEV, C)), spec, spec, spec],
        out_specs=[spec] * 4,
        out_shape=[jax.ShapeDtypeStruct((R, C), F32)] * 4,
        compiler_params=_params(1),
    )(c_all_t, dmod_cols, w, m, v)


def adamw_small(slab_a, slab_b, convw_g, wpool_g, params):
    names = ["b_ada", "g_pre_mix", "g_post_mix", "g_pre_ffn", "g_post_ffn", "b_pool", "pool_scale", "conv_b", "conv_w", "w_pool"]
    flat = []
    for n in names:
        flat += list(params[n])

    def body(a_ref, b_ref, cw_ref, wp_ref, *rest):
        ins, outs = rest[:30], rest[30:]

        def dev_sum(ref):
            t = ref[0]
            for dev in range(1, N_DEV):
                t = t + ref[dev]
            return t

        sa, sb_, scw, swp = dev_sum(a_ref), dev_sum(b_ref), dev_sum(cw_ref), dev_sum(wp_ref)
        grads = [
            jnp.concatenate([sa[k:k + 1, :] for k in range(6)], axis=1),
            sa[6:7, :], sa[7:8, :], sa[8:9, :], sa[9:10, :],
            sa[10:11, 0:256], sa[10:11, 256:512],
            sb_[3:4, :], scw, swp,
        ]
        for i, g in enumerate(grads):
            w_ref, m_ref, v_ref = ins[3 * i:3 * i + 3]
            if names[i] == "b_pool":
                parts = [((0, slice(grp, grp + 1)), g[:, grp * 64:(grp + 1) * 64]) for grp in range(4)]
            elif names[i] == "w_pool":
                parts = [((0, grp), g[grp * 64:(grp + 1) * 64, :]) for grp in range(4)]
            elif names[i] == "conv_w":
                parts = [((0,), g)]
            else:
                parts = [((Ellipsis,), g)]
            for at, gp in parts:
                d, mo, vo = _adamw(w_ref[at], gp, m_ref[at], v_ref[at])
                for k, val in enumerate((gp, d, mo, vo)):
                    outs[4 * i + k][at] = val
        outs[-1][...] = sa[10:11, 512:640]

    vm = pl.BlockSpec(memory_space=pltpu.VMEM)
    out_shape = []
    for n in names:
        out_shape += [jax.ShapeDtypeStruct(params[n][0].shape, F32)] * 4
    out_shape.append(jax.ShapeDtypeStruct((1, 128), F32))
    outs = pl.pallas_call(
        body,
        name="adamw_small",
        in_specs=[vm] * (4 + len(flat)),
        out_specs=[vm] * len(out_shape),
        out_shape=out_shape,
        compiler_params=_params(),
    )(slab_a, slab_b, convw_g, wpool_g, *flat)
    return {n: outs[4 * i:4 * i + 4] for i, n in enumerate(names)}, outs[-1]


def _place():
    return lax.axis_index("x"), lax.axis_index("y"), lax.axis_index("c")


def _other_chips(x, y):
    return [(1 - x, y), (x, 1 - y), (1 - x, 1 - y)]


def _chip_id(cx, cy):
    return 2 * cx + cy


HBM_SPEC = pl.BlockSpec(memory_space=pltpu.HBM)
SEM_SPEC = pl.BlockSpec(memory_space=pltpu.SEMAPHORE)
ANY_SPEC = pl.BlockSpec(memory_space=pl.ANY)
EFFECT = pltpu.SideEffectType.DATAFLOW_SIDE_EFFECTING


def _hbm(t):
    return pltpu.with_memory_space_constraint(t, pltpu.HBM)


def _hbm_shapes(ts):
    return [pltpu.HBM(t.shape, t.dtype) for t in ts]


def _half_rows(ref, lead, half, rh):
    return ref.at[lead, pl.ds(half * rh, rh), :]


def _flips():
    return [(fx, fy, fc) for fx in (0, 1) for fy in (0, 1) for fc in (0, 1)][1:]


def _flip(v, f):
    return v if f == 0 else 1 - v


def ada_mod(c3, w_ada, b_cols, conv_w):
    CB = w_ada.shape[1]

    def body(c_ref, w_ref, b_ref, cw_ref, call_ref, mod_ref, cwall_ref, modall, send_sems, recv_sems):
        x, y, c = _place()
        me_dev = 4 * x + 2 * y + c
        me = _chip_id(x, y)
        call_ref[me_dev] = c_ref[0]
        cwall_ref[me] = cw_ref[...]
        sends = []
        for k, (cx, cy) in enumerate(_other_chips(x, y)):
            cp = pltpu.make_async_remote_copy(src_ref=cw_ref, dst_ref=cwall_ref.at[me], send_sem=send_sems.at[10 + k],
                                              recv_sem=recv_sems.at[10 + k], device_id=(cx, cy, c), device_id_type=MESH)
            cp.start()
            sends.append(cp)
        for k, (fx, fy, fc) in enumerate(_flips()):
            cp = pltpu.make_async_remote_copy(src_ref=c_ref.at[0], dst_ref=call_ref.at[me_dev], send_sem=send_sems.at[k],
                                              recv_sem=recv_sems.at[k],
                                              device_id=(_flip(x, fx), _flip(y, fy), _flip(c, fc)), device_id_type=MESH)
            cp.start()
            sends.append(cp)
        for k, (fx, fy, fc) in enumerate(_flips()):
            peer = 4 * _flip(x, fx) + 2 * _flip(y, fy) + _flip(c, fc)
            pltpu.make_async_remote_copy(src_ref=c_ref.at[0], dst_ref=call_ref.at[peer], send_sem=send_sems.at[k],
                                         recv_sem=recv_sems.at[k], device_id=(x, y, c), device_id_type=MESH).wait_recv()
        row = lax.broadcasted_iota(jnp.int32, (N_DEV, D_MODEL), 0)
        call = jnp.zeros((N_DEV, D_MODEL), F32)
        for dev in range(N_DEV):
            call = jnp.where(row == dev, call_ref[dev], call)
        act = call * jax.nn.sigmoid(call)
        wv = w_ref[...]
        w_hi = wv.astype(BF16)
        w_lo = (wv - w_hi.astype(F32)).astype(BF16)
        a_hi = act.astype(BF16)
        a_lo = (act - a_hi.astype(F32)).astype(BF16)
        prod = (jnp.dot(a_hi, w_hi, preferred_element_type=F32) + jnp.dot(a_lo, w_hi, preferred_element_type=F32)
                + jnp.dot(a_hi, w_lo, preferred_element_type=F32))
        modall[me] = prod + b_ref[...]
        for k, (cx, cy) in enumerate(_other_chips(x, y)):
            cp = pltpu.make_async_remote_copy(src_ref=modall.at[me], dst_ref=modall.at[me], send_sem=send_sems.at[7 + k],
                                              recv_sem=recv_sems.at[7 + k], device_id=(cx, cy, c), device_id_type=MESH)
            cp.start()
            sends.append(cp)
        for k, (cx, cy) in enumerate(_other_chips(x, y)):
            blk = modall.at[_chip_id(cx, cy)]
            pltpu.make_async_remote_copy(src_ref=blk, dst_ref=blk, send_sem=send_sems.at[7 + k], recv_sem=recv_sems.at[7 + k],
                                         device_id=(x, y, c), device_id_type=MESH).wait_recv()
        for k, (cx, cy) in enumerate(_other_chips(x, y)):
            blk = cwall_ref.at[_chip_id(cx, cy)]
            pltpu.make_async_remote_copy(src_ref=blk, dst_ref=blk, send_sem=send_sems.at[10 + k], recv_sem=recv_sems.at[10 + k],
                                         device_id=(x, y, c), device_id_type=MESH).wait_recv()
        for cp in sends:
            cp.wait_send()
        mine = [modall[j, pl.ds(me_dev, 1), :] for j in range(N_CHIPS)]
        for r in range(6):
            pieces = []
            for h in range(2):
                pos = r * D_MODEL + h * 512
                pieces.append(mine[pos // CB][:, pos % CB:pos % CB + 512])
            mod_ref[r:r + 1, :] = jnp.concatenate(pieces, axis=1)

    vm = pl.BlockSpec(memory_space=pltpu.VMEM)
    return pl.pallas_call(
        body,
        name="ada_mod",
        in_specs=[vm] * 4,
        out_specs=[vm] * 3,
        out_shape=[jax.ShapeDtypeStruct((N_DEV, 1, D_MODEL), F32), jax.ShapeDtypeStruct((6, D_MODEL), F32),
                   jax.ShapeDtypeStruct((N_CHIPS,) + conv_w.shape, F32)],
        scratch_shapes=[pltpu.VMEM((N_CHIPS, N_DEV, CB), F32), pltpu.SemaphoreType.DMA((13,)), pltpu.SemaphoreType.DMA((13,))],
        compiler_params=pltpu.CompilerParams(has_side_effects=True, vmem_limit_bytes=VMEM_LIMIT),
    )(c3, w_ada, b_cols, conv_w)


def split_start(name, bufs, plan, n_sem, carry):
    nb = len(bufs)
    many = isinstance(carry, (list, tuple))
    alls = list(bufs) + (list(carry) if many else [carry])
    na = len(alls)

    def body(*refs):
        x, y, c = _place()
        ssem, rsem = refs[na], refs[na + 1]
        for i, (src, dst, dev) in enumerate(plan(refs[:nb], x, y, c)):
            pltpu.make_async_remote_copy(src_ref=src, dst_ref=dst, send_sem=ssem.at[i], recv_sem=rsem.at[i], device_id=dev,
                                         device_id_type=MESH).start()

    outs = pl.pallas_call(
        body,
        name=name,
        out_shape=[pltpu.SemaphoreType.DMA((n_sem,)), pltpu.SemaphoreType.DMA((n_sem,))] + _hbm_shapes(alls),
        in_specs=[HBM_SPEC] * na,
        out_specs=[SEM_SPEC, SEM_SPEC] + [HBM_SPEC] * na,
        input_output_aliases={i: 2 + i for i in range(na)},
        compiler_params=pltpu.CompilerParams(has_side_effects=EFFECT),
    )(*[_hbm(t) for t in alls])
    return outs[0], outs[1], list(outs[2:2 + nb]), (list(outs[2 + nb:]) if many else outs[-1])


def split_wait(name, ssem, rsem, bufs, plan, after):
    nb = len(bufs)

    def body(*refs):
        x, y, c = _place()
        s_ref, r_ref = refs[nb], refs[nb + 1]
        for i, (src, dst, dev) in enumerate(plan(refs[:nb], x, y, c)):
            cp = pltpu.make_async_remote_copy(src_ref=src, dst_ref=dst, send_sem=s_ref.at[i], recv_sem=r_ref.at[i], device_id=dev,
                                              device_id_type=MESH)
            cp.wait_send()
            cp.wait_recv()

    outs = pl.pallas_call(
        body,
        name=name,
        out_shape=_hbm_shapes(bufs),
        in_specs=[HBM_SPEC] * nb + [SEM_SPEC, SEM_SPEC, ANY_SPEC],
        out_specs=[HBM_SPEC] * nb,
        input_output_aliases={i: i for i in range(nb)},
        compiler_params=pltpu.CompilerParams(has_side_effects=EFFECT),
    )(*bufs, ssem, rsem, after)
    return list(outs)


def _gather_ici_plan(n):
    def plan(refs, x, y, c):
        out = []
        for w in range(n):
            rh = refs[w].shape[0] // 2
            for cx, cy in _other_chips(x, y):
                out.append((refs[w].at[pl.ds(c * rh, rh), :], _half_rows(refs[n + w], _chip_id(x, y), c, rh), (cx, cy, c)))
        return out

    return plan


def _gather_d2d_plan(n):
    def plan(refs, x, y, c):
        out = []
        for w in range(n):
            rh = refs[w].shape[1] // 2
            for cx, cy in _other_chips(x, y):
                blk = _half_rows(refs[w], _chip_id(cx, cy), c, rh)
                out.append((blk, blk, (x, y, 1 - c)))
        return out

    return plan


def _dev_id(x, y, c):
    return 4 * x + 2 * y + c


def _small_ici_plan(n):
    def plan(refs, x, y, c):
        out = []
        for w in range(n):
            dst = refs[n + w].at[_dev_id(x, y, c)]
            out.append((refs[w], dst, (x, y, 1 - c)))
            for cx, cy in _other_chips(x, y):
                out.append((refs[w], dst, (cx, cy, c)))
        return out

    return plan


def _small_d2d_plan(n):
    def plan(refs, x, y, c):
        out = []
        for w in range(n):
            for cx, cy in _other_chips(x, y):
                blk = refs[w].at[_dev_id(cx, cy, c)]
                out.append((blk, blk, (x, y, 1 - c)))
        return out

    return plan


def _rs_d2d_plan(n):
    def plan(refs, x, y, c):
        out = []
        for w in range(n):
            rh = refs[w].shape[1] // 2
            out.append((refs[w].at[:, pl.ds((1 - c) * rh, rh), :], refs[n + w], (x, y, 1 - c)))
        return out

    return plan


def _rs_ici_plan(n):
    def plan(refs, x, y, c):
        out = []
        for w in range(n):
            for k, (cx, cy) in enumerate(_other_chips(x, y)):
                out.append((refs[w].at[_chip_id(cx, cy)], refs[n + w].at[k], (cx, cy, c)))
        return out

    return plan


def _rs_share_plan(n):
    def plan(refs, x, y, c):
        out = []
        for w in range(n):
            rh = refs[w].shape[0] // 2
            rows = refs[w].at[pl.ds(c * rh, rh), :]
            out.append((rows, rows, (x, y, 1 - c)))
        return out

    return plan


def rs_add(grad, sibbuf, place, tr, name):
    _, R, C = grad.shape
    nt = (R // 2) // tr

    def body(p_ref, g_ref, s_ref, o_ref):
        o_ref[...] = (g_ref[...] + s_ref[...]).astype(BF16)

    return pl.pallas_call(
        body,
        name=name,
        grid_spec=pltpu.PrefetchScalarGridSpec(
            num_scalar_prefetch=1,
            grid=(N_CHIPS, nt),
            in_specs=[pl.BlockSpec((None, tr, C), lambda j, i, p: (j, p[0] * nt + i, 0)),
                      pl.BlockSpec((None, tr, C), lambda j, i, p: (j, i, 0))],
            out_specs=pl.BlockSpec((None, tr, C), lambda j, i, p: (j, i, 0)),
        ),
        out_shape=jax.ShapeDtypeStruct((N_CHIPS, R // 2, C), BF16),
        compiler_params=_params(2),
    )(place, grad, sibbuf)


def rs_final(grad, sibbuf, rbuf, place, tr, name):
    _, R, C = grad.shape
    nt = (R // 2) // tr

    def body(p_ref, g_ref, s_ref, r_ref, o_ref):
        o_ref[...] = (((g_ref[...] + s_ref[...]) + r_ref[0].astype(F32)) + r_ref[1].astype(F32)) + r_ref[2].astype(F32)

    return pl.pallas_call(
        body,
        name=name,
        grid_spec=pltpu.PrefetchScalarGridSpec(
            num_scalar_prefetch=1,
            grid=(nt,),
            in_specs=[pl.BlockSpec((None, tr, C), lambda i, p: (p[1], p[0] * nt + i, 0)),
                      pl.BlockSpec((None, tr, C), lambda i, p: (p[1], i, 0)),
                      pl.BlockSpec((3, tr, C), lambda i, p: (0, i, 0))],
            out_specs=pl.BlockSpec((tr, C), lambda i, p: (p[0] * nt + i, 0)),
        ),
        out_shape=jax.ShapeDtypeStruct((R, C), F32),
        compiler_params=_params(1),
    )(place, grad, sibbuf, rbuf)


class GradReduce:
    def __init__(self, tag, grads, rows, place):
        self.tag, self.grads, self.rows, self.place = tag, grads, rows, place
        self.n = len(grads)

    def d2d_start(self, carry):
        sib = [lax.empty((N_CHIPS, g.shape[1] // 2, g.shape[2]), F32) for g in self.grads]
        self.s1, self.r1, bufs, carry = split_start(f"rs_{self.tag}_d2d_start", self.grads + sib, _rs_d2d_plan(self.n), self.n, carry)
        self.bufs1 = bufs
        return carry

    def add_and_ici_start(self, after, carry):
        bufs = split_wait(f"rs_{self.tag}_d2d_wait", self.s1, self.r1, self.bufs1, _rs_d2d_plan(self.n), after)
        self.grads, self.sib = bufs[:self.n], bufs[self.n:]
        pb = [rs_add(g, s, self.place, tr, f"rs_{self.tag}_add{w}")
              for w, (g, s, tr) in enumerate(zip(self.grads, self.sib, self.rows))]
        rb = [lax.empty((3,) + p.shape[1:], BF16) for p in pb]
        self.s2, self.r2, self.bufs2, carry = split_start(f"rs_{self.tag}_ici_start", pb + rb, _rs_ici_plan(self.n), 3 * self.n, carry)
        return carry

    def final_and_share_start(self, after, carry):
        bufs = split_wait(f"rs_{self.tag}_ici_wait", self.s2, self.r2, self.bufs2, _rs_ici_plan(self.n), after)
        rb = bufs[self.n:]
        full = [rs_final(g, s, r, self.place, tr, f"rs_{self.tag}_final{w}")
                for w, (g, s, r, tr) in enumerate(zip(self.grads, self.sib, rb, self.rows))]
        self.s3, self.r3, self.bufs3, carry = split_start(f"rs_{self.tag}_share_start", full, _rs_share_plan(self.n), self.n, carry)
        return carry

    def finish(self, after):
        return split_wait(f"rs_{self.tag}_share_wait", self.s3, self.r3, self.bufs3, _rs_share_plan(self.n), after)


def _rope_tables(positions):
    inv_freq = ROPE_THETA ** (-jnp.arange(0, ROT_DIM, 2, dtype=F32) / ROT_DIM)
    ang = positions.astype(F32)[:, None] * inv_freq
    cos, sin = jnp.cos(ang), jnp.sin(ang)
    S = positions.shape[0]
    one, zero = jnp.ones((S, 48), F32), jnp.zeros((S, 48), F32)
    z8 = jnp.zeros((S, 8), F32)
    tc = jnp.concatenate([cos, cos, one], axis=1)
    tsa = jnp.concatenate([z8, sin, zero], axis=1)
    tsb = jnp.concatenate([-sin, z8, zero], axis=1)
    return tuple(jnp.tile(t, (1, 2)) for t in (tc, tsa, tsb))


def _block_diag(w_pool):
    wbd = jnp.zeros((POOL_W, POOL_W), F32)
    for gi in range(4):
        wbd = wbd.at[gi * 64:(gi + 1) * 64, gi * 64:(gi + 1) * 64].set(w_pool[gi])
    return wbd


def kernel(x, c, positions, w_ada, b_ada, g_pre_mix, g_post_mix, g_pre_ffn, g_post_ffn, w_in, w_pool, b_pool, pool_scale, w_out, w_up, conv_w, conv_b, w_down, loss_target, m_w_ada, m_b_ada, m_g_pre_mix, m_g_post_mix, m_g_pre_ffn, m_g_post_ffn, m_w_in, m_w_pool, m_b_pool, m_pool_scale, m_w_out, m_w_up, m_conv_w, m_conv_b, m_w_down, v_w_ada, v_b_ada, v_g_pre_mix, v_g_post_mix, v_g_pre_ffn, v_g_post_ffn, v_w_in, v_w_pool, v_b_pool, v_pool_scale, v_w_out, v_w_up, v_conv_w, v_conv_b, v_w_down):
    xi, yi, ci = lax.axis_index("x"), lax.axis_index("y"), lax.axis_index("c")
    chip = 2 * xi + yi
    place = jnp.stack([ci, chip]).astype(jnp.int32)
    x2, tgt = x[0], loss_target[0]
    S = x2.shape[0]

    def landing(s_):
        return lax.dynamic_update_slice(lax.empty((N_CHIPS,) + s_.shape, s_.dtype), s_[None], (chip, 0, 0))

    cb_ada = w_ada.shape[2]
    b_cols = lax.dynamic_slice(b_ada, (0, chip * cb_ada), (1, cb_ada))
    c_all, mod6, conv_w_g = ada_mod(c.reshape(1, 1, D_MODEL), w_ada[0], b_cols, conv_w[0])
    conv_w_f = jnp.transpose(conv_w_g, (1, 0, 2)).reshape(3, D_FF)
    mix_sh = [w_in[0].astype(BF16), w_out[0].astype(BF16)]
    ffn_sh = [w_up[0].astype(BF16), w_down[0].astype(BF16)]
    ga_s, ga_r, ga_bufs, mod6 = split_start("gather_mix_ici_start", mix_sh + [landing(t) for t in mix_sh], _gather_ici_plan(2), 6, mod6)
    gb_s, gb_r, gb_bufs, (mod6, tc, tsa, tsb) = split_start("gather_ffn_ici_start", ffn_sh + [landing(t) for t in ffn_sh],
                                                            _gather_ici_plan(2), 6, [mod6, *_rope_tables(positions[0])])
    wbd = _block_diag(w_pool[0]).astype(BF16)
    b_pool2, scale2 = b_pool.reshape(1, POOL_W), pool_scale
    ga_bufs = split_wait("gather_mix_ici_wait", ga_s, ga_r, ga_bufs, _gather_ici_plan(2), mod6)
    gc_s, gc_r, mix_land, mod6 = split_start("gather_mix_d2d_start", ga_bufs[2:], _gather_d2d_plan(2), 6, mod6)
    w_in_g, w_out_g = split_wait("gather_mix_d2d_wait", gc_s, gc_r, mix_land, _gather_d2d_plan(2), mod6)

    h1, u, *qkv = inproj_fwd(x2, g_pre_mix, mod6, w_in_g, tc, tsa, tsb)
    mixed, pool = pool_fwd(u, wbd, b_pool2, scale2)
    o_l = [attn_fwd(t, d) for t, d in zip(qkv, DILATIONS)]
    attn_done = sum(l[0, :8, :128] for _, l in o_l)
    gb_bufs = split_wait("gather_ffn_ici_wait", gb_s, gb_r, gb_bufs, _gather_ici_plan(2), attn_done)
    gd_s, gd_r, ffn_land, pool = split_start("gather_ffn_d2d_start", gb_bufs[2:], _gather_d2d_plan(2), 6, pool)
    cat, lse, lse4, lse16, y1, x1, h2 = outproj_fwd([o for o, _ in o_l] + [l for _, l in o_l], pool, x2, w_out_g, g_post_mix,
                                                    g_pre_ffn, mod6)
    lses = [lse[None], lse4, lse16]
    w_up_g, w_down_g = split_wait("gather_ffn_d2d_wait", gd_s, gd_r, ffn_land, _gather_d2d_plan(2), h2)
    w_down_f = w_down_g.reshape(D_FF, D_MODEL)
    gate, val, dy2, dout, loss_v, d_gt_f, d_g_post_ffn = ffn_fwd(h2, w_up_g, conv_w_f, conv_b, w_down_f, x1, tgt, g_post_ffn, mod6)

    dgc, dval, d_conv_w, d_conv_b, dw_down, dw_up = down_bwd(dy2, w_down_f, gate, val, conv_w_f, conv_b, h2)
    dx1, dy1, d_sh_f, d_sc_f, d_g_pre_ffn, d_gt_m, d_g_post_mix, dw_up = up_bwd(
        dgc, dval, conv_w_f, w_up_g, x1, dout, y1, g_pre_ffn, g_post_mix, mod6, h2, dw_up)
    rs_ffn = GradReduce("ffn", [dw_up, dw_down.reshape(N_CHIPS, D_FF // N_CHIPS, D_MODEL)], [256, 176], place)
    dy1 = rs_ffn.d2d_start(dy1)
    dpool, da1, da4, da16, dl1, dl4, dl16, dw_out = outproj_bwd(dy1, w_out_g, cat)
    dpool = rs_ffn.add_and_ici_start(dw_out, dpool)
    du, d_wbd, d_b_pool, d_scale = pool_bwd(dpool, mixed, wbd, b_pool2, scale2)
    dqkv = [attn_bwd(t, da, ls, dl, d) for t, da, ls, dl, d in zip(qkv, (da1[None], da4, da16), lses, (dl1[None], dl4, dl16), DILATIONS)]
    grad_x, d_sh_m, d_sc_m, d_g_pre_mix, dw_in = inproj_bwd(dqkv, du, x2, dx1, w_in_g, g_pre_mix, mod6, tc, tsa, tsb, h1)

    z1 = jnp.zeros((1, D_MODEL), F32)
    slab_a = jnp.concatenate(
        [d_sh_m, d_sc_m, d_gt_m, d_sh_f, d_sc_f, d_gt_f, d_g_pre_mix, d_g_post_mix, d_g_pre_ffn, d_g_post_ffn,
         jnp.concatenate([d_b_pool, d_scale, loss_v, jnp.zeros((1, 384), F32)], axis=1)] + [z1] * 5, axis=0)
    slab_b = jnp.concatenate([d_conv_w, d_conv_b, jnp.zeros((4, D_FF), F32)], axis=0)
    d_wpool = jnp.concatenate([d_wbd[gi * 64:(gi + 1) * 64, gi * 64:(gi + 1) * 64] for gi in range(4)], axis=0)
    dev = _dev_id(xi, yi, ci)
    small_src = [slab_a, slab_b, d_wpool]
    small_land = [lax.dynamic_update_slice(lax.empty((N_DEV,) + t.shape, F32), t[None], (dev, 0, 0)) for t in small_src]
    tok = jnp.zeros((8, 128), F32)
    gs_s, gs_r, gs_bufs, tok = split_start("small_ici_start", small_src + small_land, _small_ici_plan(3), 12, tok)
    rs_mix = GradReduce("mix", [dw_in, dw_out], [256, 256], place)
    tok = rs_mix.d2d_start(tok)
    tok = rs_ffn.final_and_share_start(tok, tok)
    gs_bufs = split_wait("small_ici_wait", gs_s, gs_r, gs_bufs, _small_ici_plan(3), tok)
    gt_s, gt_r, small_land, tok = split_start("small_d2d_start", gs_bufs[3:], _small_d2d_plan(3), 9, tok)
    tok = rs_mix.add_and_ici_start(tok, tok)
    slab_a_g, slab_b_g, wpool_g = split_wait("small_d2d_wait", gt_s, gt_r, small_land, _small_d2d_plan(3), tok)
    cw_cols = conv_w.shape[2]
    convw_g = lax.dynamic_slice(slab_b_g, (0, 0, chip * cw_cols), (N_DEV, 3, cw_cols))
    dmod_cols = lax.dynamic_slice(slab_a_g[:, :6, :].reshape(N_DEV, 6 * D_MODEL), (0, chip * cb_ada), (N_DEV, cb_ada))

    res = {}

    def big_adamw(name, w, g, m, v, tr):
        g_, d_, m_, v_ = adamw_rows(w[0], g, m[0], v[0], tr, "adamw_" + name)
        res[name] = (g_[None], d_[None], m_[None], v_[None])
        return v_

    g_ada, d_ada, m_ada, v_ada = adamw_ada(c_all.reshape(N_DEV, D_MODEL).T, dmod_cols, w_ada[0], m_w_ada[0], v_w_ada[0])
    res["w_ada"] = (g_ada[None], d_ada[None], m_ada[None], v_ada[None])
    g_w_up, g_w_down = rs_ffn.finish(v_ada)
    big_adamw("w_up", w_up, g_w_up, m_w_up, v_w_up, 256)
    last = big_adamw("w_down", w_down, g_w_down, m_w_down, v_w_down, 352)
    rs_mix.final_and_share_start(last, jnp.zeros((8, 128), F32))
    g_w_in, g_w_out = rs_mix.finish(last)
    big_adamw("w_in", w_in, g_w_in, m_w_in, v_w_in, 256)
    big_adamw("w_out", w_out, g_w_out, m_w_out, v_w_out, 256)
    small, loss_sum = adamw_small(slab_a_g, slab_b_g, convw_g, wpool_g, {
        "b_ada": (b_ada, m_b_ada, v_b_ada), "g_pre_mix": (g_pre_mix, m_g_pre_mix, v_g_pre_mix),
        "g_post_mix": (g_post_mix, m_g_post_mix, v_g_post_mix), "g_pre_ffn": (g_pre_ffn, m_g_pre_ffn, v_g_pre_ffn),
        "g_post_ffn": (g_post_ffn, m_g_post_ffn, v_g_post_ffn), "b_pool": (b_pool, m_b_pool, v_b_pool),
        "pool_scale": (pool_scale, m_pool_scale, v_pool_scale), "conv_b": (conv_b, m_conv_b, v_conv_b),
        "conv_w": (conv_w, m_conv_w, v_conv_w), "w_pool": (w_pool, m_w_pool, v_w_pool)})
    for name in ("b_ada", "g_pre_mix", "g_post_mix", "g_pre_ffn", "g_post_ffn", "pool_scale", "conv_b", "b_pool", "w_pool", "conv_w"):
        res[name] = tuple(small[name])

    loss = loss_sum[0, 0]
    order = ["w_ada", "b_ada", "g_pre_mix", "g_post_mix", "g_pre_ffn", "g_post_ffn", "w_in", "w_pool", "b_pool", "pool_scale",
             "w_out", "w_up", "conv_w", "conv_b", "w_down"]
    outs = [loss, grad_x[None]]
    for k in range(4):
        outs += [res[n][k] for n in order]
    return tuple(outs)
```

```python
import math

import jax
import jax.numpy as jnp
from jax import lax
from jax.experimental import pallas as pl
from jax.experimental.pallas import tpu as pltpu

F32 = jnp.float32
BF16 = jnp.bfloat16
MESH = pl.DeviceIdType.MESH

D_MODEL = 1024
HEAD_DIM = 64
POOL_W = 256
GROUP_W = 256
DILATIONS = (1, 4, 16)
ATT_BLOCK = 128
IN_W = 2560
D_FF = 2816
HALF_FF = 1408
ROT_DIM = 16
ROPE_THETA = 500000.0
NORM_EPS = 1e-6
N_CHIPS = 4
N_DEV = 8
NEG = -1e30

ADAM_LR = 0.001
ADAM_B1 = 0.9
ADAM_B2 = 0.999
ADAM_EPS = 1e-08
ADAM_WD = 0.01
ADAM_STEP = 10

VMEM_LIMIT = 56 * 1024 * 1024

NT = (((1,), (1,)), ((), ()))
TN = (((0,), (0,)), ((), ()))


def _params(n_grid=0, **kw):
    sem = ("arbitrary",) * n_grid if n_grid else None
    return pltpu.CompilerParams(dimension_semantics=sem, vmem_limit_bytes=VMEM_LIMIT, **kw)


def _full(shape):
    nd = len(shape)
    return pl.BlockSpec(tuple(shape), lambda *_: (0,) * nd, pipeline_mode=pl.Buffered(1))


def _rows(tm, ncol):
    return pl.BlockSpec((tm, ncol), lambda i: (i, 0))


def _acc(ref, val):
    @pl.when(pl.program_id(0) == 0)
    def _():
        ref[...] = jnp.zeros_like(ref)

    ref[...] += val


def _colsum(v):
    return jnp.sum(v, axis=0, keepdims=True)


def _rope128(t, cs, sa, sb, sign):
    return t * cs + sign * (pltpu.roll(t, 8, 1) * sa + pltpu.roll(t, 120, 1) * sb)


FF_CHUNKS = tuple((ch, off, w) for ch in range(2) for off, w in ((0, 512), (512, 512), (1024, 384)))
GELU_C0 = math.sqrt(2.0 / math.pi)
GELU_C1 = GELU_C0 * 0.044715


def _gelu(z):
    z2 = z * z
    t = jnp.tanh(z * (GELU_C0 + GELU_C1 * z2))
    u = 0.5 * t + 0.5
    return z * u, u, t, z2


def _gelu_grad(z, u, t, z2):
    return u + (z * (GELU_C0 + (3.0 * GELU_C1) * z2)) * (0.5 - 0.5 * (t * t))


def _conv_taps(gate, halo, first):
    row = lax.broadcasted_iota(jnp.int32, gate.shape, 0)
    halo = jnp.where(first, 0.0, halo)
    nh = halo.shape[0]
    p1 = halo[nh - 1:nh, :]
    p2 = halo[nh - 2:nh - 1, :]
    g1 = jnp.where(row == 0, p1, pltpu.roll(gate, 1, 0))
    g2 = jnp.where(row == 0, p2, jnp.where(row == 1, p1, pltpu.roll(gate, 2, 0)))
    return g1, g2


def inproj_fwd(x, g, mod6, w_in_g, tc, tsa, tsb, tm=512):
    S = x.shape[0]

    def body(x_ref, g_ref, mod_ref, w_ref, tc_ref, tsa_ref, tsb_ref, h_ref, u_ref, q1_ref, q4_ref, q16_ref, scr):
        qkv_refs = (q1_ref, q4_ref, q16_ref)
        xv = x_ref[...]
        rstd = lax.rsqrt(jnp.mean(xv * xv, axis=-1, keepdims=True) + NORM_EPS)
        h = ((xv * rstd) * g_ref[...]) * (1.0 + mod_ref[1:2, :]) + mod_ref[0:1, :]
        hb = h.astype(BF16)
        h_ref[...] = hb
        cs, sa, sb = tc_ref[...], tsa_ref[...], tsb_ref[...]
        for j in range(N_CHIPS):
            res = jnp.dot(hb, w_ref[j], preferred_element_type=F32)
            for t in range(5):
                sp = 5 * j + t
                piece, half = sp // 2, sp % 2
                blk = res[:, t * 128:(t + 1) * 128]
                lanes = slice(half * 128, (half + 1) * 128)
                if piece == 0:
                    u_ref[:, lanes] = blk
                else:
                    kind, gi = (piece - 1) // 3, (piece - 1) % 3
                    if kind == 0:
                        blk = _rope128(blk, cs, sa, sb, 1.0) * (HEAD_DIM ** -0.5)
                    elif kind == 1:
                        blk = _rope128(blk, cs, sa, sb, 1.0)
                    d = DILATIONS[gi]
                    if d == 1:
                        q1_ref[kind, 0, :, lanes] = blk.astype(BF16)
                    else:
                        scr[...] = blk
                        for r in range(d):
                            qkv_refs[gi][kind, r, :, lanes] = scr[pl.ds(r, tm // d, stride=d), :].astype(BF16)

    cls = lambda d: pl.BlockSpec((3, d, tm // d, GROUP_W), lambda i: (0, 0, i, 0))
    return pl.pallas_call(
        body,
        name="inproj_fwd",
        grid=(S // tm,),
        in_specs=[_rows(tm, D_MODEL), _full((1, D_MODEL)), _full((6, D_MODEL)), _full(w_in_g.shape),
                  _rows(tm, 128), _rows(tm, 128), _rows(tm, 128)],
        out_specs=[_rows(tm, D_MODEL), _rows(tm, POOL_W)] + [cls(d) for d in DILATIONS],
        out_shape=[jax.ShapeDtypeStruct((S, D_MODEL), BF16), jax.ShapeDtypeStruct((S, POOL_W), F32)]
        + [jax.ShapeDtypeStruct((3, d, S // d, GROUP_W), BF16) for d in DILATIONS],
        scratch_shapes=[pltpu.VMEM((tm, 128), F32)],
        compiler_params=_params(1),
    )(x, g, mod6, w_in_g, tc, tsa, tsb)


def _attn_masks():
    row = lax.broadcasted_iota(jnp.int32, (2 * ATT_BLOCK, 2 * ATT_BLOCK), 0) % ATT_BLOCK
    col = lax.broadcasted_iota(jnp.int32, (2 * ATT_BLOCK, 2 * ATT_BLOCK), 1)
    band = (col >= row) & (col <= row + ATT_BLOCK)
    lane = lax.broadcasted_iota(jnp.int32, (ATT_BLOCK, 128), 1)
    return band, col, lane < HEAD_DIM


def _classes_per_step(d, nb):
    return min(d, max(1, 8 // nb))


def _stack_heads(t, lo):
    z = jnp.zeros_like(t)
    return jnp.concatenate([jnp.where(lo, t, z), jnp.where(lo, z, t)], axis=0)


def _unstack_heads(t2, lo):
    return jnp.where(lo, t2[:ATT_BLOCK], t2[ATT_BLOCK:])


def attn_fwd(qkv, d):
    L = qkv.shape[2]
    nb = L // ATT_BLOCK
    cpb = _classes_per_step(d, nb)

    def body(q_ref, k_ref, v_ref, o_ref, l_ref, kpad, vpad):
        for cls in range(cpb):
            kpad[cls, 0:ATT_BLOCK, :] = jnp.zeros((ATT_BLOCK, GROUP_W), BF16)
            vpad[cls, 0:ATT_BLOCK, :] = jnp.zeros((ATT_BLOCK, GROUP_W), BF16)
            kpad[cls, ATT_BLOCK:, :] = k_ref[cls]
            vpad[cls, ATT_BLOCK:, :] = v_ref[cls]
        band, col, lo = _attn_masks()

        def step(t, carry):
            cls, n = t // nb, t % nb
            r0 = pl.multiple_of(n * ATT_BLOCK, ATT_BLOCK)
            valid = band & ((col >= ATT_BLOCK) | (n > 0))
            qb = q_ref[cls, pl.ds(r0, ATT_BLOCK), :]
            kb = kpad[cls, pl.ds(r0, 2 * ATT_BLOCK), :]
            vb = vpad[cls, pl.ds(r0, 2 * ATT_BLOCK), :]
            for pair in range(2):
                lanes = slice(pair * 128, (pair + 1) * 128)
                qp, kp, vp = qb[:, lanes], kb[:, lanes], vb[:, lanes]
                s = lax.dot_general(_stack_heads(qp, lo), kp, NT, preferred_element_type=F32)
                s = jnp.where(valid, s, NEG)
                m = jnp.max(s, axis=1, keepdims=True)
                p = jnp.exp(s - m)
                den = jnp.sum(p, axis=1, keepdims=True)
                pv = jnp.dot(p.astype(BF16), vp, preferred_element_type=F32)
                o_ref[cls, pl.ds(r0, ATT_BLOCK), lanes] = _unstack_heads(pv / den, lo).astype(BF16)
                l_ref[cls, pl.ds(r0, ATT_BLOCK), lanes] = _unstack_heads(jnp.broadcast_to(m + jnp.log(den), pv.shape), lo)
            return carry

        lax.fori_loop(0, cpb * nb, step, 0, unroll=4)

    spec = lambda kind: pl.BlockSpec((None, cpb, L, GROUP_W), lambda r: (kind, r, 0, 0))
    return pl.pallas_call(
        body,
        name=f"attn_fwd_d{d}",
        grid=(d // cpb,),
        in_specs=[spec(0), spec(1), spec(2)],
        out_specs=[pl.BlockSpec((cpb, L, GROUP_W), lambda r: (r, 0, 0))] * 2,
        out_shape=[jax.ShapeDtypeStruct((d, L, GROUP_W), BF16), jax.ShapeDtypeStruct((d, L, GROUP_W), F32)],
        scratch_shapes=[pltpu.VMEM((cpb, L + ATT_BLOCK, GROUP_W), BF16)] * 2,
        compiler_params=_params(1),
    )(qkv, qkv, qkv)


def _pool_lane_windows(shape):
    lane = lax.broadcasted_iota(jnp.int32, shape, 1)
    return lane, jnp.where(lane < 64, 2, jnp.where(lane < 128, 4, jnp.where(lane < 192, 8, 16)))


def pool_fwd(u, wbd, b, scale):
    S = u.shape[0]

    def body(u_ref, w_ref, b_ref, s_ref, mixed_ref, out_ref):
        uv = u_ref[...]
        row = lax.broadcasted_iota(jnp.int32, uv.shape, 0)
        lane, win = _pool_lane_windows(uv.shape)

        def shift(a, k):
            return jnp.where(row >= k, pltpu.roll(a, k, 0), 0.0)

        s2 = uv + shift(uv, 1)
        s4 = s2 + shift(s2, 2)
        s8 = s4 + shift(s4, 4)
        s16 = s8 + shift(s8, 8)
        tsum = jnp.where(lane < 64, s2, jnp.where(lane < 128, s4, jnp.where(lane < 192, s8, s16)))
        cnt = jnp.minimum(row + 1, win).astype(F32)
        mb = (tsum / cnt - uv).astype(BF16)
        mixed_ref[...] = mb
        y = jnp.dot(mb, w_ref[...], preferred_element_type=F32) + b_ref[...]
        out_ref[...] = (y * s_ref[...]).astype(BF16)

    vm = pl.BlockSpec(memory_space=pltpu.VMEM)
    return pl.pallas_call(
        body,
        name="pool_fwd",
        in_specs=[vm] * 4,
        out_specs=[vm] * 2,
        out_shape=[jax.ShapeDtypeStruct((S, POOL_W), BF16)] * 2,
        compiler_params=_params(),
    )(u, wbd, b, scale)


def outproj_fwd(o_l, pool, x, w_out_g, g_post, g_pre, mod6, tm=512):
    S = x.shape[0]

    def body(o0, o1, o2, l0, l1, l2, pool_ref, x_ref, w_ref, gpost_ref, gpre_ref, mod_ref,
             cat_ref, lse_ref, lse4_ref, lse16_ref, y1_ref, x1_ref, h2_ref, so4, sl4, so16, sl16):
        for d, src, dst in ((4, o1, so4), (4, l1, sl4), (16, o2, so16), (16, l2, sl16)):
            for r in range(d):
                for h in range(2):
                    dst[h, pl.ds(r, tm // d, stride=d), :] = src[r, :, h * 128:(h + 1) * 128].astype(F32)
        nat = lambda ref: jnp.concatenate([ref[0], ref[1]], axis=1)
        a, b, c = l0[0], nat(sl4), nat(sl16)
        m = jnp.maximum(jnp.maximum(a, b), c)
        e0, e1, e2 = jnp.exp(a - m), jnp.exp(b - m), jnp.exp(c - m)
        z = e0 + e1 + e2
        lse = m + jnp.log(z)
        lse_ref[...] = lse
        for h in range(2):
            sl4[h] = lse[:, h * 128:(h + 1) * 128]
        for d, dst in ((4, lse4_ref), (16, lse16_ref)):
            for r in range(d):
                for h in range(2):
                    dst[r, :, h * 128:(h + 1) * 128] = sl4[h, pl.ds(r, tm // d, stride=d), :]
        attn = (e0 * o0[0].astype(F32) + e1 * nat(so4) + e2 * nat(so16)) / z
        cat = jnp.concatenate([pool_ref[...], attn.astype(BF16)], axis=1)
        cat_ref[...] = cat
        y1 = jnp.concatenate([jnp.dot(cat, w_ref[j], preferred_element_type=F32) for j in range(N_CHIPS)], axis=1)
        y1_ref[...] = y1.astype(BF16)
        rstd = lax.rsqrt(jnp.mean(y1 * y1, axis=-1, keepdims=True) + NORM_EPS)
        x1 = x_ref[...] + mod_ref[2:3, :] * ((y1 * rstd) * gpost_ref[...])
        x1_ref[...] = x1
        rstd2 = lax.rsqrt(jnp.mean(x1 * x1, axis=-1, keepdims=True) + NORM_EPS)
        h2 = ((x1 * rstd2) * gpre_ref[...]) * (1.0 + mod_ref[4:5, :]) + mod_ref[3:4, :]
        h2_ref[...] = h2.astype(BF16)

    t256 = _rows(tm, GROUP_W)
    cls = lambda d: pl.BlockSpec((d, tm // d, GROUP_W), lambda i: (0, i, 0))
    cls_shape = lambda d: jax.ShapeDtypeStruct((d, S // d, GROUP_W), F32)
    return pl.pallas_call(
        body,
        name="outproj_fwd",
        grid=(S // tm,),
        in_specs=[cls(d) for d in DILATIONS] * 2 + [t256, _rows(tm, D_MODEL), _full(w_out_g.shape), _full((1, D_MODEL)),
                                                    _full((1, D_MODEL)), _full((6, D_MODEL))],
        out_specs=[_rows(tm, 512), t256, cls(4), cls(16), _rows(tm, D_MODEL), _rows(tm, D_MODEL), _rows(tm, D_MODEL)],
        out_shape=[jax.ShapeDtypeStruct((S, 512), BF16), jax.ShapeDtypeStruct((S, GROUP_W), F32), cls_shape(4), cls_shape(16),
                   jax.ShapeDtypeStruct((S, D_MODEL), BF16), jax.ShapeDtypeStruct((S, D_MODEL), F32),
                   jax.ShapeDtypeStruct((S, D_MODEL), BF16)],
        scratch_shapes=[pltpu.VMEM((2, tm, 128), F32)] * 4,
        compiler_params=_params(1),
    )(*o_l, pool, x, w_out_g, g_post, g_pre, mod6)


def _halo_prev(tm, ncol):
    return pl.BlockSpec((16, ncol), lambda i: (jnp.maximum(i * (tm // 16) - 1, 0), 0))


def ffn_fwd(h2, w_up_g, conv_w, conv_b, w_down, x1, target, g_post, mod6, tm=512):
    S = x1.shape[0]

    def body(h_ref, wu_ref, cw_ref, cb_ref, wd_ref, x1_ref, tgt_ref, g_ref, mod_ref,
             gate_ref, val_ref, dy2_ref, dout_ref, loss_ref, dgt_ref, dg_ref, carry):
        first = pl.program_id(0) == 0

        @pl.when(first)
        def _():
            carry[...] = jnp.zeros_like(carry)

        hb = h_ref[...]
        y2 = jnp.zeros((tm, D_MODEL), F32)
        for ch in range(2):
            cols = slice(ch * HALF_FF, (ch + 1) * HALF_FF)
            gb = jnp.dot(hb, wu_ref[ch], preferred_element_type=F32).astype(BF16)
            vb = jnp.dot(hb, wu_ref[2 + ch], preferred_element_type=F32).astype(BF16)
            gate_ref[:, cols] = gb
            val_ref[:, cols] = vb
            gt = gb.astype(F32)
            g1, g2 = _conv_taps(gt, carry[:, cols], first)
            carry[:, cols] = gt[tm - 8:, :]
            gc = g2 * cw_ref[0:1, cols] + g1 * cw_ref[1:2, cols] + gt * cw_ref[2:3, cols] + cb_ref[:, cols]
            ab = _gelu(gc.astype(BF16))[0] * vb
            y2 = y2 + jnp.dot(ab, wd_ref[cols, :], preferred_element_type=F32)
        rstd = lax.rsqrt(jnp.mean(y2 * y2, axis=-1, keepdims=True) + NORM_EPS)
        y2n = y2 * rstd
        gv = g_ref[...]
        gtf = mod_ref[5:6, :]
        r2 = y2n * gv
        diff = (x1_ref[...] + gtf * r2) - tgt_ref[...]
        _acc(loss_ref, jnp.zeros((1, 128), F32) + 0.5 * jnp.sum(diff * diff) * (1.0 / D_MODEL))
        dout = diff * (1.0 / D_MODEL)
        dout_ref[...] = dout
        _acc(dgt_ref, _colsum(dout * r2))
        dr2 = dout * gtf
        _acc(dg_ref, _colsum(dr2 * y2n))
        dyn = dr2 * gv
        dy2 = rstd * (dyn - y2n * jnp.mean(dyn * y2n, axis=-1, keepdims=True))
        dy2_ref[...] = dy2.astype(BF16)

    vec = _full((1, D_MODEL))
    return pl.pallas_call(
        body,
        name="ffn_fwd",
        grid=(S // tm,),
        in_specs=[_rows(tm, D_MODEL), _full(w_up_g.shape), _full((3, D_FF)), _full((1, D_FF)), _full((D_FF, D_MODEL)),
                  _rows(tm, D_MODEL), _rows(tm, D_MODEL), vec, _full((6, D_MODEL))],
        out_specs=[_rows(tm, D_FF), _rows(tm, D_FF), _rows(tm, D_MODEL), _rows(tm, D_MODEL), _full((1, 128)), vec, vec],
        out_shape=[jax.ShapeDtypeStruct((S, D_FF), BF16)] * 2 + [jax.ShapeDtypeStruct((S, D_MODEL), BF16),
                                                                 jax.ShapeDtypeStruct((S, D_MODEL), F32),
                                                                 jax.ShapeDtypeStruct((1, 128), F32),
                                                                 jax.ShapeDtypeStruct((1, D_MODEL), F32),
                                                                 jax.ShapeDtypeStruct((1, D_MODEL), F32)],
        scratch_shapes=[pltpu.VMEM((8, D_FF), F32)],
        compiler_params=_params(1),
    )(h2, w_up_g, conv_w, conv_b, w_down, x1, target, g_post, mod6)


def down_bwd(dy2, w_down, gate, val, conv_w, conv_b, h2, tm=512):
    S = dy2.shape[0]

    def body(dy_ref, w_ref, gate_ref, halo_ref, val_ref, cw_ref, cb_ref, h_ref,
             dgc_ref, dval_ref, dcw_ref, dcb_ref, dwd_ref, dwu_ref):
        first = pl.program_id(1) == 0

        @pl.when(first)
        def _():
            dcw_ref[...] = jnp.zeros_like(dcw_ref)
            dcb_ref[...] = jnp.zeros_like(dcb_ref)
            dwd_ref[...] = jnp.zeros_like(dwd_ref)
            dwu_ref[...] = jnp.zeros_like(dwu_ref)

        dyb = dy_ref[...]
        hb = h_ref[...]
        pieces = [(off, w) for ch, off, w in FF_CHUNKS if ch == 0]

        def col(i):
            return slice(pieces[i][0], pieces[i][0] + pieces[i][1])

        def mm_da(i):
            return lax.dot_general(dyb, w_ref[col(i), :], NT, preferred_element_type=F32)

        def elementwise(i, da):
            cols = col(i)
            gt = gate_ref[:, cols].astype(F32)
            g1, g2 = _conv_taps(gt, halo_ref[:, cols].astype(F32), first)
            gc = g2 * cw_ref[0:1, cols] + g1 * cw_ref[1:2, cols] + gt * cw_ref[2:3, cols] + cb_ref[:, cols]
            zb, dab, vb = gc.astype(BF16), da.astype(BF16), val_ref[:, cols]
            ge, u, th, z2 = _gelu(zb)
            dgb = dab * vb * _gelu_grad(zb, u, th, z2)
            dgc_ref[:, cols] = dgb
            dgc = dgb.astype(F32)
            dvb = dab * ge
            dval_ref[:, cols] = dvb
            dcb_ref[:, cols] += _colsum(dgc)
            dcw_ref[0:1, cols] += _colsum(dgc * g2)
            dcw_ref[1:2, cols] += _colsum(dgc * g1)
            dcw_ref[2:3, cols] += _colsum(dgc * gt)
            return dvb, ge * vb

        def mm_dw(i, dvb_ab):
            dvb, ab = dvb_ab
            dwd_ref[col(i), :] += lax.dot_general(ab, dyb, TN, preferred_element_type=F32)
            dwu_ref[:, col(i)] += lax.dot_general(hb, dvb, TN, preferred_element_type=F32)

        n = len(pieces)
        da = mm_da(0)
        prev = None
        for i in range(n):
            nxt = mm_da(i + 1) if i + 1 < n else None
            if prev is not None:
                mm_dw(i - 1, prev)
            prev = elementwise(i, da)
            da = nxt
        mm_dw(n - 1, prev)

    one = pl.Buffered(1)
    tok = pl.BlockSpec((tm, D_MODEL), lambda c, i: (i, 0))
    ff = pl.BlockSpec((tm, HALF_FF), lambda c, i: (i, c))
    halo = pl.BlockSpec((16, HALF_FF), lambda c, i: (jnp.maximum(i * (tm // 16) - 1, 0), c))
    per_half = lambda rows: pl.BlockSpec((rows, HALF_FF), lambda c, i: (0, c), pipeline_mode=one)
    return pl.pallas_call(
        body,
        name="down_bwd",
        grid=(2, S // tm),
        in_specs=[tok, pl.BlockSpec((HALF_FF, D_MODEL), lambda c, i: (c, 0), pipeline_mode=one), ff, halo, ff,
                  per_half(3), per_half(1), tok],
        out_specs=[ff, ff, per_half(3), per_half(1), pl.BlockSpec((HALF_FF, D_MODEL), lambda c, i: (c, 0), pipeline_mode=one),
                   pl.BlockSpec((None, D_MODEL, HALF_FF), lambda c, i: (2 + c, 0, 0), pipeline_mode=one)],
        out_shape=[jax.ShapeDtypeStruct((S, D_FF), BF16), jax.ShapeDtypeStruct((S, D_FF), BF16),
                   jax.ShapeDtypeStruct((3, D_FF), F32), jax.ShapeDtypeStruct((1, D_FF), F32),
                   jax.ShapeDtypeStruct((D_FF, D_MODEL), F32), jax.ShapeDtypeStruct((N_CHIPS, D_MODEL, HALF_FF), F32)],
        compiler_params=_params(2),
    )(dy2, w_down, gate, gate, val, conv_w, conv_b, h2)


def up_bwd(dgc, dval, conv_w, w_up_g, x1, dout, y1, g_pre, g_post, mod6, h2, dw_up, tm=256):
    S = x1.shape[0]
    last_blk = S // 16 - 1

    def body(dgc_ref, nxt_ref, dval_ref, cw_ref, w_ref, x1_ref, dout_ref, y1_ref, gpre_ref, gpost_ref, mod_ref, h_ref, dwin_ref,
             dx1_ref, dy1_ref, dsh_ref, dsc_ref, dgpre_ref, dgt_ref, dgpost_ref, dwu_ref):
        last = pl.program_id(0) == pl.num_programs(0) - 1

        @pl.when(pl.program_id(0) == 0)
        def _():
            dwu_ref[...] = jnp.zeros_like(dwu_ref)

        hb = h_ref[...]
        dh = jnp.zeros((tm, D_MODEL), F32)
        for ch in range(2):
            cols = slice(ch * HALF_FF, (ch + 1) * HALF_FF)
            dg = dgc_ref[:, cols].astype(F32)
            nx = jnp.where(last, 0.0, nxt_ref[:, cols].astype(F32))
            row = lax.broadcasted_iota(jnp.int32, dg.shape, 0)
            n0, n1 = nx[0:1, :], nx[1:2, :]
            u1 = jnp.where(row == tm - 1, n0, pltpu.roll(dg, tm - 1, 0))
            u2 = jnp.where(row == tm - 1, n1, jnp.where(row == tm - 2, n0, pltpu.roll(dg, tm - 2, 0)))
            dgate = (dg * cw_ref[2:3, cols] + u1 * cw_ref[1:2, cols] + u2 * cw_ref[0:1, cols]).astype(BF16)
            dwu_ref[ch] += lax.dot_general(hb, dgate, TN, preferred_element_type=F32)
            dh = dh + lax.dot_general(dgate, w_ref[ch], NT, preferred_element_type=F32)
            dh = dh + lax.dot_general(dval_ref[:, cols], w_ref[2 + ch], NT, preferred_element_type=F32)
        x1 = x1_ref[...]
        rstd = lax.rsqrt(jnp.mean(x1 * x1, axis=-1, keepdims=True) + NORM_EPS)
        n2 = x1 * rstd
        gpre = gpre_ref[...]
        one_sc = 1.0 + mod_ref[4:5, :]
        _acc(dsh_ref, _colsum(dh))
        _acc(dsc_ref, _colsum(dh * (n2 * gpre)))
        _acc(dgpre_ref, _colsum(dh * one_sc * n2))
        dn = dh * (gpre * one_sc)
        dx1 = dout_ref[...] + rstd * (dn - n2 * jnp.mean(dn * n2, axis=-1, keepdims=True))
        dx1_ref[...] = dx1
        y1 = y1_ref[...].astype(F32)
        rstd1 = lax.rsqrt(jnp.mean(y1 * y1, axis=-1, keepdims=True) + NORM_EPS)
        y1n = y1 * rstd1
        gpost = gpost_ref[...]
        gtm = mod_ref[2:3, :]
        _acc(dgt_ref, _colsum(dx1 * (y1n * gpost)))
        dr1 = dx1 * gtm
        _acc(dgpost_ref, _colsum(dr1 * y1n))
        dyn = dr1 * gpost
        dy1 = rstd1 * (dyn - y1n * jnp.mean(dyn * y1n, axis=-1, keepdims=True))
        dy1_ref[...] = dy1.astype(BF16)

    vec = _full((1, D_MODEL))
    nxt = pl.BlockSpec((16, D_FF), lambda i: (jnp.minimum((i + 1) * (tm // 16), last_blk), 0))
    return pl.pallas_call(
        body,
        name="up_bwd",
        grid=(S // tm,),
        in_specs=[_rows(tm, D_FF), nxt, _rows(tm, D_FF), _full((3, D_FF)), _full(w_up_g.shape), _rows(tm, D_MODEL),
                  _rows(tm, D_MODEL), _rows(tm, D_MODEL), vec, vec, _full((6, D_MODEL)), _rows(tm, D_MODEL),
                  pl.BlockSpec(memory_space=pl.ANY)],
        out_specs=[_rows(tm, D_MODEL), _rows(tm, D_MODEL), vec, vec, vec, vec, vec,
                   pl.BlockSpec((2, D_MODEL, HALF_FF), lambda i: (0, 0, 0), pipeline_mode=pl.Buffered(1))],
        out_shape=[jax.ShapeDtypeStruct((S, D_MODEL), F32), jax.ShapeDtypeStruct((S, D_MODEL), BF16)]
        + [jax.ShapeDtypeStruct((1, D_MODEL), F32)] * 5 + [jax.ShapeDtypeStruct(dw_up.shape, F32)],
        input_output_aliases={12: 7},
        compiler_params=_params(1),
    )(dgc, dgc, dval, conv_w, w_up_g, x1, dout, y1, g_pre, g_post, mod6, h2, dw_up)


def outproj_bwd(dy1, w_out_g, cat, tm=512):
    S = dy1.shape[0]

    def body(dy_ref, w_ref, cat_ref, dpool_ref, dattn_ref, da4_ref, da16_ref, delta_ref, dl4_ref, dl16_ref, dw_ref, scr):
        @pl.when(pl.program_id(0) == 0)
        def _():
            dw_ref[...] = jnp.zeros_like(dw_ref)

        catb = cat_ref[...]
        dcat = jnp.zeros((tm, 512), F32)
        for j in range(N_CHIPS):
            dyj = dy_ref[:, j * 256:(j + 1) * 256]
            dcat = dcat + lax.dot_general(dyj, w_ref[j], NT, preferred_element_type=F32)
            dw_ref[j] += lax.dot_general(catb, dyj, TN, preferred_element_type=F32)
        dpool_ref[...] = dcat[:, :POOL_W]
        dattn = dcat[:, POOL_W:]
        dattn_ref[...] = dattn.astype(BF16)
        for h in range(2):
            scr[h] = dattn[:, h * 128:(h + 1) * 128]
        for d, dst in ((4, da4_ref), (16, da16_ref)):
            for r in range(d):
                for h in range(2):
                    dst[r, :, h * 128:(h + 1) * 128] = scr[h, pl.ds(r, tm // d, stride=d), :].astype(BF16)
        prod = dattn * catb[:, POOL_W:].astype(F32)
        r = lax.broadcasted_iota(jnp.int32, (GROUP_W, GROUP_W), 0) // HEAD_DIM
        c = lax.broadcasted_iota(jnp.int32, (GROUP_W, GROUP_W), 1) // HEAD_DIM
        ones_bd = jnp.where(r == c, 1.0, 0.0).astype(BF16)
        hi = prod.astype(BF16)
        lo = (prod - hi.astype(F32)).astype(BF16)
        delta = jnp.dot(hi, ones_bd, preferred_element_type=F32) + jnp.dot(lo, ones_bd, preferred_element_type=F32)
        delta_ref[...] = delta
        for h in range(2):
            scr[h] = delta[:, h * 128:(h + 1) * 128]
        for d, dst in ((4, dl4_ref), (16, dl16_ref)):
            for r in range(d):
                for h in range(2):
                    dst[r, :, h * 128:(h + 1) * 128] = scr[h, pl.ds(r, tm // d, stride=d), :]

    cls = lambda d: pl.BlockSpec((d, tm // d, GROUP_W), lambda i: (0, i, 0))
    cls_shape = lambda d, dt: jax.ShapeDtypeStruct((d, S // d, GROUP_W), dt)
    return pl.pallas_call(
        body,
        name="outproj_bwd",
        grid=(S // tm,),
        in_specs=[_rows(tm, D_MODEL), _full(w_out_g.shape), _rows(tm, 512)],
        out_specs=[_rows(tm, POOL_W), _rows(tm, GROUP_W), cls(4), cls(16), _rows(tm, GROUP_W), cls(4), cls(16),
                   _full(w_out_g.shape)],
        out_shape=[jax.ShapeDtypeStruct((S, POOL_W), F32), jax.ShapeDtypeStruct((S, GROUP_W), BF16), cls_shape(4, BF16),
                   cls_shape(16, BF16), jax.ShapeDtypeStruct((S, GROUP_W), F32), cls_shape(4, F32), cls_shape(16, F32),
                   jax.ShapeDtypeStruct(w_out_g.shape, F32)],
        scratch_shapes=[pltpu.VMEM((2, tm, 128), F32)],
        compiler_params=_params(1),
    )(dy1, w_out_g, cat)


def attn_bwd(qkv, dattn, lse, delta, d):
    L = qkv.shape[2]
    nb = L // ATT_BLOCK
    cpb = _classes_per_step(d, nb)

    def body(q_ref, k_ref, v_ref, do_ref, l_ref, dl_ref, out_ref, kpad, vpad, dkpad, dvpad):
        for cls in range(cpb):
            kpad[cls, 0:ATT_BLOCK, :] = jnp.zeros((ATT_BLOCK, GROUP_W), BF16)
            vpad[cls, 0:ATT_BLOCK, :] = jnp.zeros((ATT_BLOCK, GROUP_W), BF16)
            kpad[cls, ATT_BLOCK:, :] = k_ref[cls]
            vpad[cls, ATT_BLOCK:, :] = v_ref[cls]
        dkpad[...] = jnp.zeros_like(dkpad)
        dvpad[...] = jnp.zeros_like(dvpad)
        band, col, lo = _attn_masks()

        def step(t, carry):
            cls, n = t // nb, t % nb
            r0 = pl.multiple_of(n * ATT_BLOCK, ATT_BLOCK)
            valid = band & ((col >= ATT_BLOCK) | (n > 0))
            qb = q_ref[cls, pl.ds(r0, ATT_BLOCK), :]
            dob = do_ref[cls, pl.ds(r0, ATT_BLOCK), :]
            lb = l_ref[cls, pl.ds(r0, ATT_BLOCK), :]
            dlb = dl_ref[cls, pl.ds(r0, ATT_BLOCK), :]
            kb = kpad[cls, pl.ds(r0, 2 * ATT_BLOCK), :]
            vb = vpad[cls, pl.ds(r0, 2 * ATT_BLOCK), :]
            for pair in range(2):
                lanes = slice(pair * 128, (pair + 1) * 128)
                qp, dop, kp, vp = qb[:, lanes], dob[:, lanes], kb[:, lanes], vb[:, lanes]
                c0, c1 = pair * 128, pair * 128 + HEAD_DIM
                q2, do2 = _stack_heads(qp, lo), _stack_heads(dop, lo)
                lse2 = jnp.concatenate([lb[:, c0:c0 + 1], lb[:, c1:c1 + 1]], axis=0)
                dl2 = jnp.concatenate([dlb[:, c0:c0 + 1], dlb[:, c1:c1 + 1]], axis=0)
                s = lax.dot_general(q2, kp, NT, preferred_element_type=F32)
                s = jnp.where(valid, s, NEG)
                p = jnp.exp(s - lse2)
                dp = lax.dot_general(do2, vp, NT, preferred_element_type=F32)
                ds = (p * (dp - dl2)).astype(BF16)
                dq2 = jnp.dot(ds, kp, preferred_element_type=F32)
                out_ref[0, cls, pl.ds(r0, ATT_BLOCK), lanes] = _unstack_heads(dq2, lo)
                dkpad[cls, pl.ds(r0, 2 * ATT_BLOCK), lanes] += lax.dot_general(ds, q2, TN, preferred_element_type=F32)
                dvpad[cls, pl.ds(r0, 2 * ATT_BLOCK), lanes] += lax.dot_general(p.astype(BF16), do2, TN, preferred_element_type=F32)
            return carry

        lax.fori_loop(0, cpb * nb, step, 0, unroll=4)
        for cls in range(cpb):
            out_ref[1, cls] = dkpad[cls, ATT_BLOCK:, :]
            out_ref[2, cls] = dvpad[cls, ATT_BLOCK:, :]

    spec = lambda kind: pl.BlockSpec((None, cpb, L, GROUP_W), lambda r: (kind, r, 0, 0))
    per_cls = pl.BlockSpec((cpb, L, GROUP_W), lambda r: (r, 0, 0))
    return pl.pallas_call(
        body,
        name=f"attn_bwd_d{d}",
        grid=(d // cpb,),
        in_specs=[spec(0), spec(1), spec(2), per_cls, per_cls, per_cls],
        out_specs=pl.BlockSpec((3, cpb, L, GROUP_W), lambda r: (0, r, 0, 0)),
        out_shape=jax.ShapeDtypeStruct((3, d, L, GROUP_W), F32),
        scratch_shapes=[pltpu.VMEM((cpb, L + ATT_BLOCK, GROUP_W), BF16)] * 2 + [pltpu.VMEM((cpb, L + ATT_BLOCK, GROUP_W), F32)] * 2,
        compiler_params=_params(1),
    )(qkv, qkv, qkv, dattn, lse, delta)


def pool_bwd(dpool, mixed, wbd, b, scale):
    S = dpool.shape[0]

    def body(dp_ref, mx_ref, w_ref, b_ref, s_ref, du_ref, dw_ref, db_ref, ds_ref):
        dp = dp_ref[...]
        mb = mx_ref[...]
        wv = w_ref[...]
        ypre = jnp.dot(mb, wv, preferred_element_type=F32) + b_ref[...]
        ds_ref[...] = _colsum(dp * ypre)
        dpre = dp * s_ref[...]
        db_ref[...] = _colsum(dpre)
        dpb = dpre.astype(BF16)
        dw_ref[...] = lax.dot_general(mb, dpb, TN, preferred_element_type=F32)
        dmix = lax.dot_general(dpb, wv, NT, preferred_element_type=F32)
        row = lax.broadcasted_iota(jnp.int32, dmix.shape, 0)
        lane, win = _pool_lane_windows(dmix.shape)
        e = dmix / jnp.minimum(row + 1, win).astype(F32)

        def shift(a, k):
            return jnp.where(row < S - k, pltpu.roll(a, S - k, 0), 0.0)

        f2 = e + shift(e, 1)
        f4 = f2 + shift(f2, 2)
        f8 = f4 + shift(f4, 4)
        f16 = f8 + shift(f8, 8)
        du_ref[...] = jnp.where(lane < 64, f2, jnp.where(lane < 128, f4, jnp.where(lane < 192, f8, f16))) - dmix

    vm = pl.BlockSpec(memory_space=pltpu.VMEM)
    return pl.pallas_call(
        body,
        name="pool_bwd",
        in_specs=[vm] * 5,
        out_specs=[vm] * 4,
        out_shape=[jax.ShapeDtypeStruct((S, POOL_W), F32), jax.ShapeDtypeStruct((POOL_W, POOL_W), F32),
                   jax.ShapeDtypeStruct((1, POOL_W), F32), jax.ShapeDtypeStruct((1, POOL_W), F32)],
        compiler_params=_params(),
    )(dpool, mixed, wbd, b, scale)


def inproj_bwd(dqkv, du, x, dx1, w_in_g, g, mod6, tc, tsa, tsb, h1, tm=512):
    S = x.shape[0]

    def body(d0, d1, d2, du_ref, x_ref, dx1_ref, w_ref, g_ref, mod_ref, tc_ref, tsa_ref, tsb_ref, h_ref,
             gx_ref, dsh_ref, dsc_ref, dg_ref, dw_ref, s4, s16, dp_ref):
        @pl.when(pl.program_id(0) == 0)
        def _():
            dw_ref[...] = jnp.zeros_like(dw_ref)

        cs, sa, sb = tc_ref[...], tsa_ref[...], tsb_ref[...]
        for d, src, dst in ((4, d1, s4), (16, d2, s16)):
            for kind in range(3):
                for r in range(d):
                    for h in range(2):
                        dst[kind, h, pl.ds(r, tm // d, stride=d), :] = src[kind, r, :, h * 128:(h + 1) * 128]
        for sp in range(20):
            piece, half = sp // 2, sp % 2
            lanes = slice(half * 128, (half + 1) * 128)
            if piece == 0:
                blk = du_ref[:, lanes]
            else:
                kind, gi = (piece - 1) // 3, (piece - 1) % 3
                blk = d0[kind, 0, :, lanes] if gi == 0 else (s4, s16)[gi - 1][kind, half]
                if kind == 0:
                    blk = _rope128(blk, cs, sa, sb, -1.0) * (HEAD_DIM ** -0.5)
                elif kind == 1:
                    blk = _rope128(blk, cs, sa, sb, -1.0)
            dp_ref[:, sp * 128:(sp + 1) * 128] = blk.astype(BF16)
        dh = jnp.zeros((tm, D_MODEL), F32)
        hbt = h_ref[...].T
        for j in range(N_CHIPS):
            dpj = dp_ref[:, j * 640:(j + 1) * 640]
            dh = dh + lax.dot_general(dpj, w_ref[j], NT, preferred_element_type=F32)
            dw_ref[j] += jnp.dot(hbt, dpj, preferred_element_type=F32)
        xv = x_ref[...]
        rstd = lax.rsqrt(jnp.mean(xv * xv, axis=-1, keepdims=True) + NORM_EPS)
        n1 = xv * rstd
        gv = g_ref[...]
        one_sc = 1.0 + mod_ref[1:2, :]
        _acc(dsh_ref, _colsum(dh))
        _acc(dsc_ref, _colsum(dh * (n1 * gv)))
        _acc(dg_ref, _colsum(dh * one_sc * n1))
        dn = dh * (gv * one_sc)
        gx_ref[...] = dx1_ref[...] + rstd * (dn - n1 * jnp.mean(dn * n1, axis=-1, keepdims=True))

    vec = _full((1, D_MODEL))
    dspec = lambda d: pl.BlockSpec((3, d, tm // d, GROUP_W), lambda i: (0, 0, i, 0))
    return pl.pallas_call(
        body,
        name="inproj_bwd",
        grid=(S // tm,),
        in_specs=[dspec(d) for d in DILATIONS] + [_rows(tm, POOL_W), _rows(tm, D_MODEL), _rows(tm, D_MODEL), _full(w_in_g.shape),
                                                  vec, _full((6, D_MODEL)), _rows(tm, 128), _rows(tm, 128), _rows(tm, 128),
                                                  _rows(tm, D_MODEL)],
        out_specs=[_rows(tm, D_MODEL), vec, vec, vec, _full(w_in_g.shape)],
        out_shape=[jax.ShapeDtypeStruct((S, D_MODEL), F32)] + [jax.ShapeDtypeStruct((1, D_MODEL), F32)] * 3
        + [jax.ShapeDtypeStruct(w_in_g.shape, F32)],
        scratch_shapes=[pltpu.VMEM((3, 2, tm, 128), F32)] * 2 + [pltpu.VMEM((tm, IN_W), BF16)],
        compiler_params=_params(1),
    )(*dqkv, du, x, dx1, w_in_g, g, mod6, tc, tsa, tsb, h1)


def _adamw(w, g, m, v):
    m = ADAM_B1 * m + (1.0 - ADAM_B1) * g
    v = ADAM_B2 * v + (1.0 - ADAM_B2) * (g * g)
    m_hat = m / (1.0 - ADAM_B1 ** ADAM_STEP)
    v_hat = v / (1.0 - ADAM_B2 ** ADAM_STEP)
    delta = -ADAM_LR * (m_hat / (jnp.sqrt(v_hat) + ADAM_EPS) + ADAM_WD * w)
    return delta, m, v


def adamw_rows(w, g, m, v, tr, name):
    R, C = w.shape

    def body(w_ref, g_ref, m_ref, v_ref, go_ref, d_ref, mo_ref, vo_ref):
        g = g_ref[...]
        go_ref[...] = g
        d_ref[...], mo_ref[...], vo_ref[...] = _adamw(w_ref[...], g, m_ref[...], v_ref[...])

    spec = pl.BlockSpec((tr, C), lambda i: (i, 0))
    return pl.pallas_call(
        body,
        name=name,
        grid=(R // tr,),
        in_specs=[spec] * 4,
        out_specs=[spec] * 4,
        out_shape=[jax.ShapeDtypeStruct((R, C), F32)] * 4,
        compiler_params=_params(1),
    )(w, g, m, v)


def adamw_ada(c_all_t, dmod_cols, w, m, v, tr=256):
    R, C = w.shape

    def body(ct_ref, dm_ref, w_ref, m_ref, v_ref, g_ref, d_ref, mo_ref, vo_ref):
        ct = ct_ref[...]
        act = ct * jax.nn.sigmoid(ct)
        dm = dm_ref[...]
        a_hi, d_hi = act.astype(BF16), dm.astype(BF16)
        a_lo, d_lo = (act - a_hi.astype(F32)).astype(BF16), (dm - d_hi.astype(F32)).astype(BF16)
        g = (jnp.dot(a_hi, d_hi, preferred_element_type=F32) + jnp.dot(a_lo, d_hi, preferred_element_type=F32)
             + jnp.dot(a_hi, d_lo, preferred_element_type=F32))
        g_ref[...] = g
        d_ref[...], mo_ref[...], vo_ref[...] = _adamw(w_ref[...], g, m_ref[...], v_ref[...])

    spec = pl.BlockSpec((tr, C), lambda i: (i, 0))
    return pl.pallas_call(
        body,
        name="adamw_ada",
        grid=(R // tr,),
        in_specs=[pl.BlockSpec((tr, N_DEV), lambda i: (i, 0)), _full((N_DEV, C)), spec, spec, spec],
        out_specs=[spec] * 4,
        out_shape=[jax.ShapeDtypeStruct((R, C), F32)] * 4,
        compiler_params=_params(1),
    )(c_all_t, dmod_cols, w, m, v)


def adamw_small(slab_a, slab_b, convw_g, wpool_g, params):
    names = ["b_ada", "g_pre_mix", "g_post_mix", "g_pre_ffn", "g_post_ffn", "b_pool", "pool_scale", "conv_b", "conv_w", "w_pool"]
    flat = []
    for n in names:
        flat += list(params[n])

    def body(a_ref, b_ref, cw_ref, wp_ref, *rest):
        ins, outs = rest[:30], rest[30:]

        def dev_sum(ref):
            t = ref[0]
            for dev in range(1, N_DEV):
                t = t + ref[dev]
            return t

        sa, sb_, scw, swp = dev_sum(a_ref), dev_sum(b_ref), dev_sum(cw_ref), dev_sum(wp_ref)
        grads = [
            jnp.concatenate([sa[k:k + 1, :] for k in range(6)], axis=1),
            sa[6:7, :], sa[7:8, :], sa[8:9, :], sa[9:10, :],
            sa[10:11, 0:256], sa[10:11, 256:512],
            sb_[3:4, :], scw, swp,
        ]
        for i, g in enumerate(grads):
            w_ref, m_ref, v_ref = ins[3 * i:3 * i + 3]
            if names[i] == "b_pool":
                parts = [((0, slice(grp, grp + 1)), g[:, grp * 64:(grp + 1) * 64]) for grp in range(4)]
            elif names[i] == "w_pool":
                parts = [((0, grp), g[grp * 64:(grp + 1) * 64, :]) for grp in range(4)]
            elif names[i] == "conv_w":
                parts = [((0,), g)]
            else:
                parts = [((Ellipsis,), g)]
            for at, gp in parts:
                d, mo, vo = _adamw(w_ref[at], gp, m_ref[at], v_ref[at])
                for k, val in enumerate((gp, d, mo, vo)):
                    outs[4 * i + k][at] = val
        outs[-1][...] = sa[10:11, 512:640]

    vm = pl.BlockSpec(memory_space=pltpu.VMEM)
    out_shape = []
    for n in names:
        out_shape += [jax.ShapeDtypeStruct(params[n][0].shape, F32)] * 4
    out_shape.append(jax.ShapeDtypeStruct((1, 128), F32))
    outs = pl.pallas_call(
        body,
        name="adamw_small",
        in_specs=[vm] * (4 + len(flat)),
        out_specs=[vm] * len(out_shape),
        out_shape=out_shape,
        compiler_params=_params(),
    )(slab_a, slab_b, convw_g, wpool_g, *flat)
    return {n: outs[4 * i:4 * i + 4] for i, n in enumerate(names)}, outs[-1]


def _place():
    return lax.axis_index("x"), lax.axis_index("y"), lax.axis_index("c")


def _other_chips(x, y):
    return [(1 - x, y), (x, 1 - y), (1 - x, 1 - y)]


def _chip_id(cx, cy):
    return 2 * cx + cy


HBM_SPEC = pl.BlockSpec(memory_space=pltpu.HBM)
SEM_SPEC = pl.BlockSpec(memory_space=pltpu.SEMAPHORE)
ANY_SPEC = pl.BlockSpec(memory_space=pl.ANY)
EFFECT = pltpu.SideEffectType.DATAFLOW_SIDE_EFFECTING


def _hbm(t):
    return pltpu.with_memory_space_constraint(t, pltpu.HBM)


def _hbm_shapes(ts):
    return [pltpu.HBM(t.shape, t.dtype) for t in ts]


def _half_rows(ref, lead, half, rh):
    return ref.at[lead, pl.ds(half * rh, rh), :]


def _flips():
    return [(fx, fy, fc) for fx in (0, 1) for fy in (0, 1) for fc in (0, 1)][1:]


def _flip(v, f):
    return v if f == 0 else 1 - v


def ada_mod(c3, w_ada, b_cols, conv_w):
    CB = w_ada.shape[1]

    def body(c_ref, w_hbm, b_ref, cw_ref, call_ref, mod_ref, cwall_ref, modall, send_sems, recv_sems, w_ref, w_sem):
        x, y, c = _place()
        me_dev = 4 * x + 2 * y + c
        me = _chip_id(x, y)
        w_load = pltpu.make_async_copy(w_hbm, w_ref, w_sem)
        w_load.start()
        call_ref[me_dev] = c_ref[0]
        cwall_ref[me] = cw_ref[...]
        sends = []
        for k, (cx, cy) in enumerate(_other_chips(x, y)):
            cp = pltpu.make_async_remote_copy(src_ref=cw_ref, dst_ref=cwall_ref.at[me], send_sem=send_sems.at[10 + k],
                                              recv_sem=recv_sems.at[10 + k], device_id=(cx, cy, c), device_id_type=MESH)
            cp.start()
            sends.append(cp)
        for k, (fx, fy, fc) in enumerate(_flips()):
            cp = pltpu.make_async_remote_copy(src_ref=c_ref.at[0], dst_ref=call_ref.at[me_dev], send_sem=send_sems.at[k],
                                              recv_sem=recv_sems.at[k],
                                              device_id=(_flip(x, fx), _flip(y, fy), _flip(c, fc)), device_id_type=MESH)
            cp.start()
            sends.append(cp)
        for k, (fx, fy, fc) in enumerate(_flips()):
            peer = 4 * _flip(x, fx) + 2 * _flip(y, fy) + _flip(c, fc)
            pltpu.make_async_remote_copy(src_ref=c_ref.at[0], dst_ref=call_ref.at[peer], send_sem=send_sems.at[k],
                                         recv_sem=recv_sems.at[k], device_id=(x, y, c), device_id_type=MESH).wait_recv()
        row = lax.broadcasted_iota(jnp.int32, (N_DEV, D_MODEL), 0)
        call = jnp.zeros((N_DEV, D_MODEL), F32)
        for dev in range(N_DEV):
            call = jnp.where(row == dev, call_ref[dev], call)
        act = call * jax.nn.sigmoid(call)
        w_load.wait()
        wv = w_ref[...]
        w_hi = wv.astype(BF16)
        w_lo = (wv - w_hi.astype(F32)).astype(BF16)
        a_hi = act.astype(BF16)
        a_lo = (act - a_hi.astype(F32)).astype(BF16)
        prod = (jnp.dot(a_hi, w_hi, preferred_element_type=F32) + jnp.dot(a_lo, w_hi, preferred_element_type=F32)
                + jnp.dot(a_hi, w_lo, preferred_element_type=F32))
        modall[me] = prod + b_ref[...]
        for k, (cx, cy) in enumerate(_other_chips(x, y)):
            cp = pltpu.make_async_remote_copy(src_ref=modall.at[me], dst_ref=modall.at[me], send_sem=send_sems.at[7 + k],
                                              recv_sem=recv_sems.at[7 + k], device_id=(cx, cy, c), device_id_type=MESH)
            cp.start()
            sends.append(cp)
        for k, (cx, cy) in enumerate(_other_chips(x, y)):
            blk = modall.at[_chip_id(cx, cy)]
            pltpu.make_async_remote_copy(src_ref=blk, dst_ref=blk, send_sem=send_sems.at[7 + k], recv_sem=recv_sems.at[7 + k],
                                         device_id=(x, y, c), device_id_type=MESH).wait_recv()
        for k, (cx, cy) in enumerate(_other_chips(x, y)):
            blk = cwall_ref.at[_chip_id(cx, cy)]
            pltpu.make_async_remote_copy(src_ref=blk, dst_ref=blk, send_sem=send_sems.at[10 + k], recv_sem=recv_sems.at[10 + k],
                                         device_id=(x, y, c), device_id_type=MESH).wait_recv()
        for cp in sends:
            cp.wait_send()
        mine = [modall[j, pl.ds(me_dev, 1), :] for j in range(N_CHIPS)]
        for r in range(6):
            pieces = []
            for h in range(2):
                pos = r * D_MODEL + h * 512
                pieces.append(mine[pos // CB][:, pos % CB:pos % CB + 512])
            mod_ref[r:r + 1, :] = jnp.concatenate(pieces, axis=1)

    vm = pl.BlockSpec(memory_space=pltpu.VMEM)
    return pl.pallas_call(
        body,
        name="ada_mod",
        in_specs=[vm, ANY_SPEC, vm, vm],
        out_specs=[vm] * 3,
        out_shape=[jax.ShapeDtypeStruct((N_DEV, 1, D_MODEL), F32), jax.ShapeDtypeStruct((6, D_MODEL), F32),
                   jax.ShapeDtypeStruct((N_CHIPS,) + conv_w.shape, F32)],
        scratch_shapes=[pltpu.VMEM((N_CHIPS, N_DEV, CB), F32), pltpu.SemaphoreType.DMA((13,)), pltpu.SemaphoreType.DMA((13,)),
                        pltpu.VMEM(w_ada.shape, F32), pltpu.SemaphoreType.DMA],
        compiler_params=pltpu.CompilerParams(has_side_effects=True, vmem_limit_bytes=VMEM_LIMIT),
    )(c3, w_ada, b_cols, conv_w)


def split_start(name, bufs, plan, n_sem, carry):
    nb = len(bufs)
    many = isinstance(carry, (list, tuple))
    alls = list(bufs) + (list(carry) if many else [carry])
    na = len(alls)

    def body(*refs):
        x, y, c = _place()
        ssem, rsem = refs[na], refs[na + 1]
        for i, (src, dst, dev) in enumerate(plan(refs[:nb], x, y, c)):
            pltpu.make_async_remote_copy(src_ref=src, dst_ref=dst, send_sem=ssem.at[i], recv_sem=rsem.at[i], device_id=dev,
                                         device_id_type=MESH).start()

    outs = pl.pallas_call(
        body,
        name=name,
        out_shape=[pltpu.SemaphoreType.DMA((n_sem,)), pltpu.SemaphoreType.DMA((n_sem,))] + _hbm_shapes(alls),
        in_specs=[HBM_SPEC] * na,
        out_specs=[SEM_SPEC, SEM_SPEC] + [HBM_SPEC] * na,
        input_output_aliases={i: 2 + i for i in range(na)},
        compiler_params=pltpu.CompilerParams(has_side_effects=EFFECT),
    )(*[_hbm(t) for t in alls])
    return outs[0], outs[1], list(outs[2:2 + nb]), (list(outs[2 + nb:]) if many else outs[-1])


def split_wait(name, ssem, rsem, bufs, plan, after):
    nb = len(bufs)

    def body(*refs):
        x, y, c = _place()
        s_ref, r_ref = refs[nb], refs[nb + 1]
        for i, (src, dst, dev) in enumerate(plan(refs[:nb], x, y, c)):
            cp = pltpu.make_async_remote_copy(src_ref=src, dst_ref=dst, send_sem=s_ref.at[i], recv_sem=r_ref.at[i], device_id=dev,
                                              device_id_type=MESH)
            cp.wait_send()
            cp.wait_recv()

    outs = pl.pallas_call(
        body,
        name=name,
        out_shape=_hbm_shapes(bufs),
        in_specs=[HBM_SPEC] * nb + [SEM_SPEC, SEM_SPEC, ANY_SPEC],
        out_specs=[HBM_SPEC] * nb,
        input_output_aliases={i: i for i in range(nb)},
        compiler_params=pltpu.CompilerParams(has_side_effects=EFFECT),
    )(*bufs, ssem, rsem, after)
    return list(outs)


def _gather_ici_plan(n):
    def plan(refs, x, y, c):
        out = []
        for w in range(n):
            rh = refs[w].shape[0] // 2
            for cx, cy in _other_chips(x, y):
                out.append((refs[w].at[pl.ds(c * rh, rh), :], _half_rows(refs[n + w], _chip_id(x, y), c, rh), (cx, cy, c)))
        return out

    return plan


def _gather_d2d_plan(n):
    def plan(refs, x, y, c):
        out = []
        for w in range(n):
            rh = refs[w].shape[1] // 2
            for cx, cy in _other_chips(x, y):
                blk = _half_rows(refs[w], _chip_id(cx, cy), c, rh)
                out.append((blk, blk, (x, y, 1 - c)))
        return out

    return plan


def _dev_id(x, y, c):
    return 4 * x + 2 * y + c


def _small_ici_plan(n):
    def plan(refs, x, y, c):
        out = []
        for w in range(n):
            dst = refs[n + w].at[_dev_id(x, y, c)]
            out.append((refs[w], dst, (x, y, 1 - c)))
            for cx, cy in _other_chips(x, y):
                out.append((refs[w], dst, (cx, cy, c)))
        return out

    return plan


def _small_d2d_plan(n):
    def plan(refs, x, y, c):
        out = []
        for w in range(n):
            for cx, cy in _other_chips(x, y):
                blk = refs[w].at[_dev_id(cx, cy, c)]
                out.append((blk, blk, (x, y, 1 - c)))
        return out

    return plan


def _rs_d2d_plan(n):
    def plan(refs, x, y, c):
        out = []
        for w in range(n):
            rh = refs[w].shape[1] // 2
            out.append((refs[w].at[:, pl.ds((1 - c) * rh, rh), :], refs[n + w], (x, y, 1 - c)))
        return out

    return plan


def _rs_ici_plan(n):
    def plan(refs, x, y, c):
        out = []
        for w in range(n):
            for k, (cx, cy) in enumerate(_other_chips(x, y)):
                out.append((refs[w].at[_chip_id(cx, cy)], refs[n + w].at[k], (cx, cy, c)))
        return out

    return plan


def _rs_share_plan(n):
    def plan(refs, x, y, c):
        out = []
        for w in range(n):
            rh = refs[w].shape[0] // 2
            rows = refs[w].at[pl.ds(c * rh, rh), :]
            out.append((rows, rows, (x, y, 1 - c)))
        return out

    return plan


def rs_add(grad, sibbuf, place, tr, name):
    _, R, C = grad.shape
    nt = (R // 2) // tr

    def body(p_ref, g_ref, s_ref, o_ref):
        o_ref[...] = (g_ref[...] + s_ref[...]).astype(BF16)

    return pl.pallas_call(
        body,
        name=name,
        grid_spec=pltpu.PrefetchScalarGridSpec(
            num_scalar_prefetch=1,
            grid=(N_CHIPS, nt),
            in_specs=[pl.BlockSpec((None, tr, C), lambda j, i, p: (j, p[0] * nt + i, 0)),
                      pl.BlockSpec((None, tr, C), lambda j, i, p: (j, i, 0))],
            out_specs=pl.BlockSpec((None, tr, C), lambda j, i, p: (j, i, 0)),
        ),
        out_shape=jax.ShapeDtypeStruct((N_CHIPS, R // 2, C), BF16),
        compiler_params=_params(2),
    )(place, grad, sibbuf)


def rs_final(grad, sibbuf, rbuf, place, tr, name):
    _, R, C = grad.shape
    nt = (R // 2) // tr

    def body(p_ref, g_ref, s_ref, r_ref, o_ref):
        o_ref[...] = (((g_ref[...] + s_ref[...]) + r_ref[0].astype(F32)) + r_ref[1].astype(F32)) + r_ref[2].astype(F32)

    return pl.pallas_call(
        body,
        name=name,
        grid_spec=pltpu.PrefetchScalarGridSpec(
            num_scalar_prefetch=1,
            grid=(nt,),
            in_specs=[pl.BlockSpec((None, tr, C), lambda i, p: (p[1], p[0] * nt + i, 0)),
                      pl.BlockSpec((None, tr, C), lambda i, p: (p[1], i, 0)),
                      pl.BlockSpec((3, tr, C), lambda i, p: (0, i, 0))],
            out_specs=pl.BlockSpec((tr, C), lambda i, p: (p[0] * nt + i, 0)),
        ),
        out_shape=jax.ShapeDtypeStruct((R, C), F32),
        compiler_params=_params(1),
    )(place, grad, sibbuf, rbuf)


class GradReduce:
    def __init__(self, tag, grads, rows, place):
        self.tag, self.grads, self.rows, self.place = tag, grads, rows, place
        self.n = len(grads)

    def d2d_start(self, carry):
        sib = [lax.empty((N_CHIPS, g.shape[1] // 2, g.shape[2]), F32) for g in self.grads]
        self.s1, self.r1, bufs, carry = split_start(f"rs_{self.tag}_d2d_start", self.grads + sib, _rs_d2d_plan(self.n), self.n, carry)
        self.bufs1 = bufs
        return carry

    def add_and_ici_start(self, after, carry):
        bufs = split_wait(f"rs_{self.tag}_d2d_wait", self.s1, self.r1, self.bufs1, _rs_d2d_plan(self.n), after)
        self.grads, self.sib = bufs[:self.n], bufs[self.n:]
        pb = [rs_add(g, s, self.place, tr, f"rs_{self.tag}_add{w}")
              for w, (g, s, tr) in enumerate(zip(self.grads, self.sib, self.rows))]
        rb = [lax.empty((3,) + p.shape[1:], BF16) for p in pb]
        self.s2, self.r2, self.bufs2, carry = split_start(f"rs_{self.tag}_ici_start", pb + rb, _rs_ici_plan(self.n), 3 * self.n, carry)
        return carry

    def final_and_share_start(self, after, carry):
        bufs = split_wait(f"rs_{self.tag}_ici_wait", self.s2, self.r2, self.bufs2, _rs_ici_plan(self.n), after)
        rb = bufs[self.n:]
        full = [rs_final(g, s, r, self.place, tr, f"rs_{self.tag}_final{w}")
                for w, (g, s, r, tr) in enumerate(zip(self.grads, self.sib, rb, self.rows))]
        self.s3, self.r3, self.bufs3, carry = split_start(f"rs_{self.tag}_share_start", full, _rs_share_plan(self.n), self.n, carry)
        return carry

    def finish(self, after):
        return split_wait(f"rs_{self.tag}_share_wait", self.s3, self.r3, self.bufs3, _rs_share_plan(self.n), after)


def _rope_tables(positions):
    inv_freq = ROPE_THETA ** (-jnp.arange(0, ROT_DIM, 2, dtype=F32) / ROT_DIM)
    ang = positions.astype(F32)[:, None] * inv_freq
    cos, sin = jnp.cos(ang), jnp.sin(ang)
    S = positions.shape[0]
    one, zero = jnp.ones((S, 48), F32), jnp.zeros((S, 48), F32)
    z8 = jnp.zeros((S, 8), F32)
    tc = jnp.concatenate([cos, cos, one], axis=1)
    tsa = jnp.concatenate([z8, sin, zero], axis=1)
    tsb = jnp.concatenate([-sin, z8, zero], axis=1)
    return tuple(jnp.tile(t, (1, 2)) for t in (tc, tsa, tsb))


def _block_diag(w_pool):
    wbd = jnp.zeros((POOL_W, POOL_W), F32)
    for gi in range(4):
        wbd = wbd.at[gi * 64:(gi + 1) * 64, gi * 64:(gi + 1) * 64].set(w_pool[gi])
    return wbd


def kernel(x, c, positions, w_ada, b_ada, g_pre_mix, g_post_mix, g_pre_ffn, g_post_ffn, w_in, w_pool, b_pool, pool_scale, w_out, w_up, conv_w, conv_b, w_down, loss_target, m_w_ada, m_b_ada, m_g_pre_mix, m_g_post_mix, m_g_pre_ffn, m_g_post_ffn, m_w_in, m_w_pool, m_b_pool, m_pool_scale, m_w_out, m_w_up, m_conv_w, m_conv_b, m_w_down, v_w_ada, v_b_ada, v_g_pre_mix, v_g_post_mix, v_g_pre_ffn, v_g_post_ffn, v_w_in, v_w_pool, v_b_pool, v_pool_scale, v_w_out, v_w_up, v_conv_w, v_conv_b, v_w_down):
    xi, yi, ci = lax.axis_index("x"), lax.axis_index("y"), lax.axis_index("c")
    chip = 2 * xi + yi
    place = jnp.stack([ci, chip]).astype(jnp.int32)
    x2, tgt = x[0], loss_target[0]
    S = x2.shape[0]

    def landing(s_):
        return lax.dynamic_update_slice(lax.empty((N_CHIPS,) + s_.shape, s_.dtype), s_[None], (chip, 0, 0))

    cb_ada = w_ada.shape[2]
    b_cols = lax.dynamic_slice(b_ada, (0, chip * cb_ada), (1, cb_ada))
    c_all, mod6, conv_w_g = ada_mod(c.reshape(1, 1, D_MODEL), w_ada[0], b_cols, conv_w[0])
    conv_w_f = jnp.transpose(conv_w_g, (1, 0, 2)).reshape(3, D_FF)
    mix_sh = [w_in[0].astype(BF16), w_out[0].astype(BF16)]
    ffn_sh = [w_up[0].astype(BF16), w_down[0].astype(BF16)]
    ga_s, ga_r, ga_bufs, mod6 = split_start("gather_mix_ici_start", mix_sh + [landing(t) for t in mix_sh], _gather_ici_plan(2), 6, mod6)
    gb_s, gb_r, gb_bufs, (mod6, tc, tsa, tsb) = split_start("gather_ffn_ici_start", ffn_sh + [landing(t) for t in ffn_sh],
                                                            _gather_ici_plan(2), 6, [mod6, *_rope_tables(positions[0])])
    wbd = _block_diag(w_pool[0]).astype(BF16)
    b_pool2, scale2 = b_pool.reshape(1, POOL_W), pool_scale
    ga_bufs = split_wait("gather_mix_ici_wait", ga_s, ga_r, ga_bufs, _gather_ici_plan(2), mod6)
    gc_s, gc_r, mix_land, mod6 = split_start("gather_mix_d2d_start", ga_bufs[2:], _gather_d2d_plan(2), 6, mod6)
    w_in_g, w_out_g = split_wait("gather_mix_d2d_wait", gc_s, gc_r, mix_land, _gather_d2d_plan(2), mod6)

    h1, u, *qkv = inproj_fwd(x2, g_pre_mix, mod6, w_in_g, tc, tsa, tsb)
    mixed, pool = pool_fwd(u, wbd, b_pool2, scale2)
    o_l = [attn_fwd(t, d) for t, d in zip(qkv, DILATIONS)]
    attn_done = sum(l[0, :8, :128] for _, l in o_l)
    gb_bufs = split_wait("gather_ffn_ici_wait", gb_s, gb_r, gb_bufs, _gather_ici_plan(2), attn_done)
    gd_s, gd_r, ffn_land, pool = split_start("gather_ffn_d2d_start", gb_bufs[2:], _gather_d2d_plan(2), 6, pool)
    cat, lse, lse4, lse16, y1, x1, h2 = outproj_fwd([o for o, _ in o_l] + [l for _, l in o_l], pool, x2, w_out_g, g_post_mix,
                                                    g_pre_ffn, mod6)
    lses = [lse[None], lse4, lse16]
    w_up_g, w_down_g = split_wait("gather_ffn_d2d_wait", gd_s, gd_r, ffn_land, _gather_d2d_plan(2), h2)
    w_down_f = w_down_g.reshape(D_FF, D_MODEL)
    gate, val, dy2, dout, loss_v, d_gt_f, d_g_post_ffn = ffn_fwd(h2, w_up_g, conv_w_f, conv_b, w_down_f, x1, tgt, g_post_ffn, mod6)

    dgc, dval, d_conv_w, d_conv_b, dw_down, dw_up = down_bwd(dy2, w_down_f, gate, val, conv_w_f, conv_b, h2)
    dx1, dy1, d_sh_f, d_sc_f, d_g_pre_ffn, d_gt_m, d_g_post_mix, dw_up = up_bwd(
        dgc, dval, conv_w_f, w_up_g, x1, dout, y1, g_pre_ffn, g_post_mix, mod6, h2, dw_up)
    rs_ffn = GradReduce("ffn", [dw_up, dw_down.reshape(N_CHIPS, D_FF // N_CHIPS, D_MODEL)], [256, 176], place)
    dy1 = rs_ffn.d2d_start(dy1)
    dpool, da1, da4, da16, dl1, dl4, dl16, dw_out = outproj_bwd(dy1, w_out_g, cat)
    dpool = rs_ffn.add_and_ici_start(dw_out, dpool)
    du, d_wbd, d_b_pool, d_scale = pool_bwd(dpool, mixed, wbd, b_pool2, scale2)
    dqkv = [attn_bwd(t, da, ls, dl, d) for t, da, ls, dl, d in zip(qkv, (da1[None], da4, da16), lses, (dl1[None], dl4, dl16), DILATIONS)]
    grad_x, d_sh_m, d_sc_m, d_g_pre_mix, dw_in = inproj_bwd(dqkv, du, x2, dx1, w_in_g, g_pre_mix, mod6, tc, tsa, tsb, h1)

    z1 = jnp.zeros((1, D_MODEL), F32)
    slab_a = jnp.concatenate(
        [d_sh_m, d_sc_m, d_gt_m, d_sh_f, d_sc_f, d_gt_f, d_g_pre_mix, d_g_post_mix, d_g_pre_ffn, d_g_post_ffn,
         jnp.concatenate([d_b_pool, d_scale, loss_v, jnp.zeros((1, 384), F32)], axis=1)] + [z1] * 5, axis=0)
    slab_b = jnp.concatenate([d_conv_w, d_conv_b, jnp.zeros((4, D_FF), F32)], axis=0)
    d_wpool = jnp.concatenate([d_wbd[gi * 64:(gi + 1) * 64, gi * 64:(gi + 1) * 64] for gi in range(4)], axis=0)
    dev = _dev_id(xi, yi, ci)
    small_src = [slab_a, slab_b, d_wpool]
    small_land = [lax.dynamic_update_slice(lax.empty((N_DEV,) + t.shape, F32), t[None], (dev, 0, 0)) for t in small_src]
    tok = jnp.zeros((8, 128), F32)
    gs_s, gs_r, gs_bufs, tok = split_start("small_ici_start", small_src + small_land, _small_ici_plan(3), 12, tok)
    rs_mix = GradReduce("mix", [dw_in, dw_out], [256, 256], place)
    tok = rs_mix.d2d_start(tok)
    tok = rs_ffn.final_and_share_start(tok, tok)
    gs_bufs = split_wait("small_ici_wait", gs_s, gs_r, gs_bufs, _small_ici_plan(3), tok)
    gt_s, gt_r, small_land, tok = split_start("small_d2d_start", gs_bufs[3:], _small_d2d_plan(3), 9, tok)
    tok = rs_mix.add_and_ici_start(tok, tok)
    slab_a_g, slab_b_g, wpool_g = split_wait("small_d2d_wait", gt_s, gt_r, small_land, _small_d2d_plan(3), tok)
    cw_cols = conv_w.shape[2]
    convw_g = lax.dynamic_slice(slab_b_g, (0, 0, chip * cw_cols), (N_DEV, 3, cw_cols))
    dmod_cols = lax.dynamic_slice(slab_a_g[:, :6, :].reshape(N_DEV, 6 * D_MODEL), (0, chip * cb_ada), (N_DEV, cb_ada))

    res = {}

    def big_adamw(name, w, g, m, v, tr):
        g_, d_, m_, v_ = adamw_rows(w[0], g, m[0], v[0], tr, "adamw_" + name)
        res[name] = (g_[None], d_[None], m_[None], v_[None])
        return v_

    g_ada, d_ada, m_ada, v_ada = adamw_ada(c_all.reshape(N_DEV, D_MODEL).T, dmod_cols, w_ada[0], m_w_ada[0], v_w_ada[0])
    res["w_ada"] = (g_ada[None], d_ada[None], m_ada[None], v_ada[None])
    g_w_up, g_w_down = rs_ffn.finish(v_ada)
    big_adamw("w_up", w_up, g_w_up, m_w_up, v_w_up, 256)
    last = big_adamw("w_down", w_down, g_w_down, m_w_down, v_w_down, 352)
    rs_mix.final_and_share_start(last, jnp.zeros((8, 128), F32))
    g_w_in, g_w_out = rs_mix.finish(last)
    big_adamw("w_in", w_in, g_w_in, m_w_in, v_w_in, 256)
    big_adamw("w_out", w_out, g_w_out, m_w_out, v_w_out, 256)
    small, loss_sum = adamw_small(slab_a_g, slab_b_g, convw_g, wpool_g, {
        "b_ada": (b_ada, m_b_ada, v_b_ada), "g_pre_mix": (g_pre_mix, m_g_pre_mix, v_g_pre_mix),
        "g_post_mix": (g_post_mix, m_g_post_mix, v_g_post_mix), "g_pre_ffn": (g_pre_ffn, m_g_pre_ffn, v_g_pre_ffn),
        "g_post_ffn": (g_post_ffn, m_g_post_ffn, v_g_post_ffn), "b_pool": (b_pool, m_b_pool, v_b_pool),
        "pool_scale": (pool_scale, m_pool_scale, v_pool_scale), "conv_b": (conv_b, m_conv_b, v_conv_b),
        "conv_w": (conv_w, m_conv_w, v_conv_w), "w_pool": (w_pool, m_w_pool, v_w_pool)})
    for name in ("b_ada", "g_pre_mix", "g_post_mix", "g_pre_ffn", "g_post_ffn", "pool_scale", "conv_b", "b_pool", "w_pool", "conv_w"):
        res[name] = tuple(small[name])

    loss = loss_sum[0, 0]
    order = ["w_ada", "b_ada", "g_pre_mix", "g_post_mix", "g_pre_ffn", "g_post_ffn", "w_in", "w_pool", "b_pool", "pool_scale",
             "w_out", "w_up", "conv_w", "conv_b", "w_down"]
    outs = [loss, grad_x[None]]
    for k in range(4):
        outs += [res[n][k] for n in order]
    return tuple(outs)
```

```python
import math

import jax
import jax.numpy as jnp
from jax import lax
from jax.experimental import pallas as pl
from jax.experimental.pallas import tpu as pltpu

F32 = jnp.float32
BF16 = jnp.bfloat16
MESH = pl.DeviceIdType.MESH

D_MODEL = 1024
HEAD_DIM = 64
POOL_W = 256
GROUP_W = 256
DILATIONS = (1, 4, 16)
ATT_BLOCK = 128
IN_W = 2560
D_FF = 2816
HALF_FF = 1408
ROT_DIM = 16
ROPE_THETA = 500000.0
NORM_EPS = 1e-6
N_CHIPS = 4
N_DEV = 8
NEG = -1e30

ADAM_LR = 0.001
ADAM_B1 = 0.9
ADAM_B2 = 0.999
ADAM_EPS = 1e-08
ADAM_WD = 0.01
ADAM_STEP = 10

VMEM_LIMIT = 56 * 1024 * 1024

NT = (((1,), (1,)), ((), ()))
TN = (((0,), (0,)), ((), ()))


def _params(n_grid=0, **kw):
    sem = ("arbitrary",) * n_grid if n_grid else None
    return pltpu.CompilerParams(dimension_semantics=sem, vmem_limit_bytes=VMEM_LIMIT, **kw)


def _full(shape):
    nd = len(shape)
    return pl.BlockSpec(tuple(shape), lambda *_: (0,) * nd, pipeline_mode=pl.Buffered(1))


def _rows(tm, ncol):
    return pl.BlockSpec((tm, ncol), lambda i: (i, 0))


def _zero(*refs):
    for ref in refs:
        ref[...] = jnp.zeros_like(ref)


def _colsum(v):
    return jnp.sum(v, axis=0, keepdims=True)


def _rope128(t, cs, sa, sb, sign):
    return t * cs + sign * (pltpu.roll(t, 8, 1) * sa + pltpu.roll(t, 120, 1) * sb)


FF_CHUNKS = tuple((ch, off, w) for ch in range(2) for off, w in ((0, 512), (512, 512), (1024, 384)))
GELU_C0 = math.sqrt(2.0 / math.pi)
GELU_C1 = GELU_C0 * 0.044715


def _gelu(z):
    z2 = z * z
    t = jnp.tanh(z * (GELU_C0 + GELU_C1 * z2))
    u = 0.5 * t + 0.5
    return z * u, u, t, z2


def _gelu_grad(z, u, t, z2):
    return u + (z * (GELU_C0 + (3.0 * GELU_C1) * z2)) * (0.5 - 0.5 * (t * t))


def _conv_taps(gate, halo, first):
    row = lax.broadcasted_iota(jnp.int32, gate.shape, 0)
    halo = jnp.where(first, 0.0, halo)
    nh = halo.shape[0]
    p1 = halo[nh - 1:nh, :]
    p2 = halo[nh - 2:nh - 1, :]
    g1 = jnp.where(row == 0, p1, pltpu.roll(gate, 1, 0))
    g2 = jnp.where(row == 0, p2, jnp.where(row == 1, p1, pltpu.roll(gate, 2, 0)))
    return g1, g2


def inproj_fwd(x, g, mod6, w_in_g, tc, tsa, tsb, tm=512):
    S = x.shape[0]

    def body(x_ref, g_ref, mod_ref, w_ref, tc_ref, tsa_ref, tsb_ref, h_ref, u_ref, q1_ref, q4_ref, q16_ref, scr):
        qkv_refs = (q1_ref, q4_ref, q16_ref)
        xv = x_ref[...]
        rstd = lax.rsqrt(jnp.mean(xv * xv, axis=-1, keepdims=True) + NORM_EPS)
        h = ((xv * rstd) * g_ref[...]) * (1.0 + mod_ref[1:2, :]) + mod_ref[0:1, :]
        hb = h.astype(BF16)
        h_ref[...] = hb
        cs, sa, sb = tc_ref[...], tsa_ref[...], tsb_ref[...]
        for j in range(N_CHIPS):
            res = jnp.dot(hb, w_ref[j], preferred_element_type=F32)
            for t in range(5):
                sp = 5 * j + t
                piece, half = sp // 2, sp % 2
                blk = res[:, t * 128:(t + 1) * 128]
                lanes = slice(half * 128, (half + 1) * 128)
                if piece == 0:
                    u_ref[:, lanes] = blk
                else:
                    kind, gi = (piece - 1) // 3, (piece - 1) % 3
                    if kind == 0:
                        blk = _rope128(blk, cs, sa, sb, 1.0) * (HEAD_DIM ** -0.5)
                    elif kind == 1:
                        blk = _rope128(blk, cs, sa, sb, 1.0)
                    d = DILATIONS[gi]
                    if d == 1:
                        q1_ref[kind, 0, :, lanes] = blk.astype(BF16)
                    else:
                        scr[...] = blk
                        for r in range(d):
                            qkv_refs[gi][kind, r, :, lanes] = scr[pl.ds(r, tm // d, stride=d), :].astype(BF16)

    cls = lambda d: pl.BlockSpec((3, d, tm // d, GROUP_W), lambda i: (0, 0, i, 0))
    return pl.pallas_call(
        body,
        name="inproj_fwd",
        grid=(S // tm,),
        in_specs=[_rows(tm, D_MODEL), _full((1, D_MODEL)), _full((6, D_MODEL)), _full(w_in_g.shape),
                  _rows(tm, 128), _rows(tm, 128), _rows(tm, 128)],
        out_specs=[_rows(tm, D_MODEL), _rows(tm, POOL_W)] + [cls(d) for d in DILATIONS],
        out_shape=[jax.ShapeDtypeStruct((S, D_MODEL), BF16), jax.ShapeDtypeStruct((S, POOL_W), F32)]
        + [jax.ShapeDtypeStruct((3, d, S // d, GROUP_W), BF16) for d in DILATIONS],
        scratch_shapes=[pltpu.VMEM((tm, 128), F32)],
        compiler_params=_params(1),
    )(x, g, mod6, w_in_g, tc, tsa, tsb)


def _attn_masks():
    row = lax.broadcasted_iota(jnp.int32, (2 * ATT_BLOCK, 2 * ATT_BLOCK), 0) % ATT_BLOCK
    col = lax.broadcasted_iota(jnp.int32, (2 * ATT_BLOCK, 2 * ATT_BLOCK), 1)
    band = (col >= row) & (col <= row + ATT_BLOCK)
    lane = lax.broadcasted_iota(jnp.int32, (ATT_BLOCK, 128), 1)
    return band, col, lane < HEAD_DIM


def _classes_per_step(d, nb):
    return min(d, max(1, 8 // nb))


def _stack_heads(t, lo):
    z = jnp.zeros_like(t)
    return jnp.concatenate([jnp.where(lo, t, z), jnp.where(lo, z, t)], axis=0)


def _unstack_heads(t2, lo):
    return jnp.where(lo, t2[:ATT_BLOCK], t2[ATT_BLOCK:])


def attn_fwd(qkv, d):
    L = qkv.shape[2]
    nb = L // ATT_BLOCK
    cpb = _classes_per_step(d, nb)

    def body(q_ref, k_ref, v_ref, o_ref, l_ref, kpad, vpad):
        for cls in range(cpb):
            kpad[cls, 0:ATT_BLOCK, :] = jnp.zeros((ATT_BLOCK, GROUP_W), BF16)
            vpad[cls, 0:ATT_BLOCK, :] = jnp.zeros((ATT_BLOCK, GROUP_W), BF16)
            kpad[cls, ATT_BLOCK:, :] = k_ref[cls]
            vpad[cls, ATT_BLOCK:, :] = v_ref[cls]
        band, col, lo = _attn_masks()

        def step(t, carry):
            cls, n = t // nb, t % nb
            r0 = pl.multiple_of(n * ATT_BLOCK, ATT_BLOCK)
            valid = band & ((col >= ATT_BLOCK) | (n > 0))
            qb = q_ref[cls, pl.ds(r0, ATT_BLOCK), :]
            kb = kpad[cls, pl.ds(r0, 2 * ATT_BLOCK), :]
            vb = vpad[cls, pl.ds(r0, 2 * ATT_BLOCK), :]
            for pair in range(2):
                lanes = slice(pair * 128, (pair + 1) * 128)
                qp, kp, vp = qb[:, lanes], kb[:, lanes], vb[:, lanes]
                s = lax.dot_general(_stack_heads(qp, lo), kp, NT, preferred_element_type=F32)
                s = jnp.where(valid, s, NEG)
                m = jnp.max(s, axis=1, keepdims=True)
                p = jnp.exp(s - m)
                den = jnp.sum(p, axis=1, keepdims=True)
                pv = jnp.dot(p.astype(BF16), vp, preferred_element_type=F32)
                o_ref[cls, pl.ds(r0, ATT_BLOCK), lanes] = _unstack_heads(pv / den, lo).astype(BF16)
                l_ref[cls, pl.ds(r0, ATT_BLOCK), lanes] = _unstack_heads(jnp.broadcast_to(m + jnp.log(den), pv.shape), lo)
            return carry

        lax.fori_loop(0, cpb * nb, step, 0, unroll=4)

    spec = lambda kind: pl.BlockSpec((None, cpb, L, GROUP_W), lambda r: (kind, r, 0, 0))
    return pl.pallas_call(
        body,
        name=f"attn_fwd_d{d}",
        grid=(d // cpb,),
        in_specs=[spec(0), spec(1), spec(2)],
        out_specs=[pl.BlockSpec((cpb, L, GROUP_W), lambda r: (r, 0, 0))] * 2,
        out_shape=[jax.ShapeDtypeStruct((d, L, GROUP_W), BF16), jax.ShapeDtypeStruct((d, L, GROUP_W), F32)],
        scratch_shapes=[pltpu.VMEM((cpb, L + ATT_BLOCK, GROUP_W), BF16)] * 2,
        compiler_params=_params(1),
    )(qkv, qkv, qkv)


def _pool_lane_windows(shape):
    lane = lax.broadcasted_iota(jnp.int32, shape, 1)
    return lane, jnp.where(lane < 64, 2, jnp.where(lane < 128, 4, jnp.where(lane < 192, 8, 16)))


def pool_fwd(u, wbd, b, scale):
    S = u.shape[0]

    def body(u_ref, w_ref, b_ref, s_ref, mixed_ref, out_ref):
        uv = u_ref[...]
        row = lax.broadcasted_iota(jnp.int32, uv.shape, 0)
        lane, win = _pool_lane_windows(uv.shape)

        def shift(a, k):
            return jnp.where(row >= k, pltpu.roll(a, k, 0), 0.0)

        s2 = uv + shift(uv, 1)
        s4 = s2 + shift(s2, 2)
        s8 = s4 + shift(s4, 4)
        s16 = s8 + shift(s8, 8)
        tsum = jnp.where(lane < 64, s2, jnp.where(lane < 128, s4, jnp.where(lane < 192, s8, s16)))
        cnt = jnp.minimum(row + 1, win).astype(F32)
        mb = (tsum / cnt - uv).astype(BF16)
        mixed_ref[...] = mb
        y = jnp.dot(mb, w_ref[...], preferred_element_type=F32) + b_ref[...]
        out_ref[...] = (y * s_ref[...]).astype(BF16)

    vm = pl.BlockSpec(memory_space=pltpu.VMEM)
    return pl.pallas_call(
        body,
        name="pool_fwd",
        in_specs=[vm] * 4,
        out_specs=[vm] * 2,
        out_shape=[jax.ShapeDtypeStruct((S, POOL_W), BF16)] * 2,
        compiler_params=_params(),
    )(u, wbd, b, scale)


def outproj_fwd(o_l, pool, x, w_out_g, g_post, g_pre, mod6, tm=512):
    S = x.shape[0]

    def body(o0, o1, o2, l0, l1, l2, pool_ref, x_ref, w_ref, gpost_ref, gpre_ref, mod_ref,
             cat_ref, lse_ref, lse4_ref, lse16_ref, y1_ref, x1_ref, h2_ref, so4, sl4, so16, sl16):
        for d, src, dst in ((4, o1, so4), (4, l1, sl4), (16, o2, so16), (16, l2, sl16)):
            for r in range(d):
                for h in range(2):
                    dst[h, pl.ds(r, tm // d, stride=d), :] = src[r, :, h * 128:(h + 1) * 128].astype(F32)
        nat = lambda ref: jnp.concatenate([ref[0], ref[1]], axis=1)
        a, b, c = l0[0], nat(sl4), nat(sl16)
        m = jnp.maximum(jnp.maximum(a, b), c)
        e0, e1, e2 = jnp.exp(a - m), jnp.exp(b - m), jnp.exp(c - m)
        z = e0 + e1 + e2
        lse = m + jnp.log(z)
        lse_ref[...] = lse
        for h in range(2):
            sl4[h] = lse[:, h * 128:(h + 1) * 128]
        for d, dst in ((4, lse4_ref), (16, lse16_ref)):
            for r in range(d):
                for h in range(2):
                    dst[r, :, h * 128:(h + 1) * 128] = sl4[h, pl.ds(r, tm // d, stride=d), :]
        attn = (e0 * o0[0].astype(F32) + e1 * nat(so4) + e2 * nat(so16)) / z
        cat = jnp.concatenate([pool_ref[...], attn.astype(BF16)], axis=1)
        cat_ref[...] = cat
        y1 = jnp.concatenate([jnp.dot(cat, w_ref[j], preferred_element_type=F32) for j in range(N_CHIPS)], axis=1)
        y1_ref[...] = y1.astype(BF16)
        rstd = lax.rsqrt(jnp.mean(y1 * y1, axis=-1, keepdims=True) + NORM_EPS)
        x1 = x_ref[...] + mod_ref[2:3, :] * ((y1 * rstd) * gpost_ref[...])
        x1_ref[...] = x1
        rstd2 = lax.rsqrt(jnp.mean(x1 * x1, axis=-1, keepdims=True) + NORM_EPS)
        h2 = ((x1 * rstd2) * gpre_ref[...]) * (1.0 + mod_ref[4:5, :]) + mod_ref[3:4, :]
        h2_ref[...] = h2.astype(BF16)

    t256 = _rows(tm, GROUP_W)
    cls = lambda d: pl.BlockSpec((d, tm // d, GROUP_W), lambda i: (0, i, 0))
    cls_shape = lambda d: jax.ShapeDtypeStruct((d, S // d, GROUP_W), F32)
    return pl.pallas_call(
        body,
        name="outproj_fwd",
        grid=(S // tm,),
        in_specs=[cls(d) for d in DILATIONS] * 2 + [t256, _rows(tm, D_MODEL), _full(w_out_g.shape), _full((1, D_MODEL)),
                                                    _full((1, D_MODEL)), _full((6, D_MODEL))],
        out_specs=[_rows(tm, 512), t256, cls(4), cls(16), _rows(tm, D_MODEL), _rows(tm, D_MODEL), _rows(tm, D_MODEL)],
        out_shape=[jax.ShapeDtypeStruct((S, 512), BF16), jax.ShapeDtypeStruct((S, GROUP_W), F32), cls_shape(4), cls_shape(16),
                   jax.ShapeDtypeStruct((S, D_MODEL), BF16), jax.ShapeDtypeStruct((S, D_MODEL), F32),
                   jax.ShapeDtypeStruct((S, D_MODEL), BF16)],
        scratch_shapes=[pltpu.VMEM((2, tm, 128), F32)] * 4,
        compiler_params=_params(1),
    )(*o_l, pool, x, w_out_g, g_post, g_pre, mod6)


def _halo_prev(tm, ncol):
    return pl.BlockSpec((16, ncol), lambda i: (jnp.maximum(i * (tm // 16) - 1, 0), 0))


def ffn_fwd(h2, w_up_g, conv_w, conv_b, w_down, x1, target, g_post, mod6, tm=512):
    S = x1.shape[0]

    def body(h_ref, wu_ref, cw_ref, cb_ref, wd_ref, x1_ref, tgt_ref, g_ref, mod_ref,
             gate_ref, val_ref, dy2_ref, dout_ref, loss_ref, dgt_ref, dg_ref, carry):
        first = pl.program_id(0) == 0

        @pl.when(first)
        def _():
            _zero(carry, loss_ref, dgt_ref, dg_ref)

        hb = h_ref[...]
        y2 = jnp.zeros((tm, D_MODEL), F32)
        for ch in range(2):
            cols = slice(ch * HALF_FF, (ch + 1) * HALF_FF)
            gb = jnp.dot(hb, wu_ref[ch], preferred_element_type=F32).astype(BF16)
            vb = jnp.dot(hb, wu_ref[2 + ch], preferred_element_type=F32).astype(BF16)
            gate_ref[:, cols] = gb
            val_ref[:, cols] = vb
            gt = gb.astype(F32)
            g1, g2 = _conv_taps(gt, carry[:, cols], first)
            carry[:, cols] = gt[tm - 8:, :]
            gc = g2 * cw_ref[0:1, cols] + g1 * cw_ref[1:2, cols] + gt * cw_ref[2:3, cols] + cb_ref[:, cols]
            ab = _gelu(gc.astype(BF16))[0] * vb
            y2 = y2 + jnp.dot(ab, wd_ref[cols, :], preferred_element_type=F32)
        rstd = lax.rsqrt(jnp.mean(y2 * y2, axis=-1, keepdims=True) + NORM_EPS)
        y2n = y2 * rstd
        gv = g_ref[...]
        gtf = mod_ref[5:6, :]
        r2 = y2n * gv
        diff = (x1_ref[...] + gtf * r2) - tgt_ref[...]
        loss_ref[...] += jnp.zeros((1, 128), F32) + 0.5 * jnp.sum(diff * diff) * (1.0 / D_MODEL)
        dout = diff * (1.0 / D_MODEL)
        dout_ref[...] = dout
        dgt_ref[...] += _colsum(dout * r2)
        dr2 = dout * gtf
        dg_ref[...] += _colsum(dr2 * y2n)
        dyn = dr2 * gv
        dy2 = rstd * (dyn - y2n * jnp.mean(dyn * y2n, axis=-1, keepdims=True))
        dy2_ref[...] = dy2.astype(BF16)

    vec = _full((1, D_MODEL))
    return pl.pallas_call(
        body,
        name="ffn_fwd",
        grid=(S // tm,),
        in_specs=[_rows(tm, D_MODEL), _full(w_up_g.shape), _full((3, D_FF)), _full((1, D_FF)), _full((D_FF, D_MODEL)),
                  _rows(tm, D_MODEL), _rows(tm, D_MODEL), vec, _full((6, D_MODEL))],
        out_specs=[_rows(tm, D_FF), _rows(tm, D_FF), _rows(tm, D_MODEL), _rows(tm, D_MODEL), _full((1, 128)), vec, vec],
        out_shape=[jax.ShapeDtypeStruct((S, D_FF), BF16)] * 2 + [jax.ShapeDtypeStruct((S, D_MODEL), BF16),
                                                                 jax.ShapeDtypeStruct((S, D_MODEL), F32),
                                                                 jax.ShapeDtypeStruct((1, 128), F32),
                                                                 jax.ShapeDtypeStruct((1, D_MODEL), F32),
                                                                 jax.ShapeDtypeStruct((1, D_MODEL), F32)],
        scratch_shapes=[pltpu.VMEM((8, D_FF), F32)],
        compiler_params=_params(1),
    )(h2, w_up_g, conv_w, conv_b, w_down, x1, target, g_post, mod6)


def down_bwd(dy2, w_down, gate, val, conv_w, conv_b, h2, tm=512):
    S = dy2.shape[0]

    def body(dy_ref, w_ref, gate_ref, halo_ref, val_ref, cw_ref, cb_ref, h_ref,
             dgc_ref, dval_ref, dcw_ref, dcb_ref, dwd_ref, dwu_ref):
        first = pl.program_id(1) == 0

        @pl.when(first)
        def _():
            dcw_ref[...] = jnp.zeros_like(dcw_ref)
            dcb_ref[...] = jnp.zeros_like(dcb_ref)
            dwd_ref[...] = jnp.zeros_like(dwd_ref)
            dwu_ref[...] = jnp.zeros_like(dwu_ref)

        dyb = dy_ref[...]
        hb = h_ref[...]
        pieces = [(off, w) for ch, off, w in FF_CHUNKS if ch == 0]

        def col(i):
            return slice(pieces[i][0], pieces[i][0] + pieces[i][1])

        def mm_da(i):
            return lax.dot_general(dyb, w_ref[col(i), :], NT, preferred_element_type=F32)

        def elementwise(i, da):
            cols = col(i)
            gt = gate_ref[:, cols].astype(F32)
            g1, g2 = _conv_taps(gt, halo_ref[:, cols].astype(F32), first)
            gc = g2 * cw_ref[0:1, cols] + g1 * cw_ref[1:2, cols] + gt * cw_ref[2:3, cols] + cb_ref[:, cols]
            zb, dab, vb = gc.astype(BF16), da.astype(BF16), val_ref[:, cols]
            ge, u, th, z2 = _gelu(zb)
            dgb = dab * vb * _gelu_grad(zb, u, th, z2)
            dgc_ref[:, cols] = dgb
            dgc = dgb.astype(F32)
            dvb = dab * ge
            dval_ref[:, cols] = dvb
            dcb_ref[:, cols] += _colsum(dgc)
            dcw_ref[0:1, cols] += _colsum(dgc * g2)
            dcw_ref[1:2, cols] += _colsum(dgc * g1)
            dcw_ref[2:3, cols] += _colsum(dgc * gt)
            return dvb, ge * vb

        def mm_dw(i, dvb_ab):
            dvb, ab = dvb_ab
            dwd_ref[col(i), :] += lax.dot_general(ab, dyb, TN, preferred_element_type=F32)
            dwu_ref[:, col(i)] += lax.dot_general(hb, dvb, TN, preferred_element_type=F32)

        n = len(pieces)
        da = mm_da(0)
        prev = None
        for i in range(n):
            nxt = mm_da(i + 1) if i + 1 < n else None
            if prev is not None:
                mm_dw(i - 1, prev)
            prev = elementwise(i, da)
            da = nxt
        mm_dw(n - 1, prev)

    one = pl.Buffered(1)
    tok = pl.BlockSpec((tm, D_MODEL), lambda c, i: (i, 0))
    ff = pl.BlockSpec((tm, HALF_FF), lambda c, i: (i, c))
    halo = pl.BlockSpec((16, HALF_FF), lambda c, i: (jnp.maximum(i * (tm // 16) - 1, 0), c))
    per_half = lambda rows: pl.BlockSpec((rows, HALF_FF), lambda c, i: (0, c), pipeline_mode=one)
    return pl.pallas_call(
        body,
        name="down_bwd",
        grid=(2, S // tm),
        in_specs=[tok, pl.BlockSpec((HALF_FF, D_MODEL), lambda c, i: (c, 0), pipeline_mode=one), ff, halo, ff,
                  per_half(3), per_half(1), tok],
        out_specs=[ff, ff, per_half(3), per_half(1), pl.BlockSpec((HALF_FF, D_MODEL), lambda c, i: (c, 0), pipeline_mode=one),
                   pl.BlockSpec((None, D_MODEL, HALF_FF), lambda c, i: (2 + c, 0, 0), pipeline_mode=one)],
        out_shape=[jax.ShapeDtypeStruct((S, D_FF), BF16), jax.ShapeDtypeStruct((S, D_FF), BF16),
                   jax.ShapeDtypeStruct((3, D_FF), F32), jax.ShapeDtypeStruct((1, D_FF), F32),
                   jax.ShapeDtypeStruct((D_FF, D_MODEL), F32), jax.ShapeDtypeStruct((N_CHIPS, D_MODEL, HALF_FF), F32)],
        compiler_params=_params(2),
    )(dy2, w_down, gate, gate, val, conv_w, conv_b, h2)


def up_bwd(dgc, dval, conv_w, w_up_g, x1, dout, y1, g_pre, g_post, mod6, h2, dw_up, tm=256):
    S = x1.shape[0]
    last_blk = S // 16 - 1

    def body(dgc_ref, nxt_ref, dval_ref, cw_ref, w_ref, x1_ref, dout_ref, y1_ref, gpre_ref, gpost_ref, mod_ref, h_ref, dwin_ref,
             dx1_ref, dy1_ref, dsh_ref, dsc_ref, dgpre_ref, dgt_ref, dgpost_ref, dwu_ref):
        last = pl.program_id(0) == pl.num_programs(0) - 1

        @pl.when(pl.program_id(0) == 0)
        def _():
            _zero(dwu_ref, dsh_ref, dsc_ref, dgpre_ref, dgt_ref, dgpost_ref)

        hb = h_ref[...]
        dh = jnp.zeros((tm, D_MODEL), F32)
        for ch in range(2):
            cols = slice(ch * HALF_FF, (ch + 1) * HALF_FF)
            dg = dgc_ref[:, cols].astype(F32)
            nx = jnp.where(last, 0.0, nxt_ref[:, cols].astype(F32))
            row = lax.broadcasted_iota(jnp.int32, dg.shape, 0)
            n0, n1 = nx[0:1, :], nx[1:2, :]
            u1 = jnp.where(row == tm - 1, n0, pltpu.roll(dg, tm - 1, 0))
            u2 = jnp.where(row == tm - 1, n1, jnp.where(row == tm - 2, n0, pltpu.roll(dg, tm - 2, 0)))
            dgate = (dg * cw_ref[2:3, cols] + u1 * cw_ref[1:2, cols] + u2 * cw_ref[0:1, cols]).astype(BF16)
            dwu_ref[ch] += lax.dot_general(hb, dgate, TN, preferred_element_type=F32)
            dh = dh + lax.dot_general(dgate, w_ref[ch], NT, preferred_element_type=F32)
            dh = dh + lax.dot_general(dval_ref[:, cols], w_ref[2 + ch], NT, preferred_element_type=F32)
        x1 = x1_ref[...]
        rstd = lax.rsqrt(jnp.mean(x1 * x1, axis=-1, keepdims=True) + NORM_EPS)
        n2 = x1 * rstd
        gpre = gpre_ref[...]
        one_sc = 1.0 + mod_ref[4:5, :]
        dsh_ref[...] += _colsum(dh)
        dsc_ref[...] += _colsum(dh * (n2 * gpre))
        dgpre_ref[...] += _colsum(dh * one_sc * n2)
        dn = dh * (gpre * one_sc)
        dx1 = dout_ref[...] + rstd * (dn - n2 * jnp.mean(dn * n2, axis=-1, keepdims=True))
        dx1_ref[...] = dx1
        y1 = y1_ref[...].astype(F32)
        rstd1 = lax.rsqrt(jnp.mean(y1 * y1, axis=-1, keepdims=True) + NORM_EPS)
        y1n = y1 * rstd1
        gpost = gpost_ref[...]
        gtm = mod_ref[2:3, :]
        dgt_ref[...] += _colsum(dx1 * (y1n * gpost))
        dr1 = dx1 * gtm
        dgpost_ref[...] += _colsum(dr1 * y1n)
        dyn = dr1 * gpost
        dy1 = rstd1 * (dyn - y1n * jnp.mean(dyn * y1n, axis=-1, keepdims=True))
        dy1_ref[...] = dy1.astype(BF16)

    vec = _full((1, D_MODEL))
    nxt = pl.BlockSpec((16, D_FF), lambda i: (jnp.minimum((i + 1) * (tm // 16), last_blk), 0))
    return pl.pallas_call(
        body,
        name="up_bwd",
        grid=(S // tm,),
        in_specs=[_rows(tm, D_FF), nxt, _rows(tm, D_FF), _full((3, D_FF)), _full(w_up_g.shape), _rows(tm, D_MODEL),
                  _rows(tm, D_MODEL), _rows(tm, D_MODEL), vec, vec, _full((6, D_MODEL)), _rows(tm, D_MODEL),
                  pl.BlockSpec(memory_space=pl.ANY)],
        out_specs=[_rows(tm, D_MODEL), _rows(tm, D_MODEL), vec, vec, vec, vec, vec,
                   pl.BlockSpec((2, D_MODEL, HALF_FF), lambda i: (0, 0, 0), pipeline_mode=pl.Buffered(1))],
        out_shape=[jax.ShapeDtypeStruct((S, D_MODEL), F32), jax.ShapeDtypeStruct((S, D_MODEL), BF16)]
        + [jax.ShapeDtypeStruct((1, D_MODEL), F32)] * 5 + [jax.ShapeDtypeStruct(dw_up.shape, F32)],
        input_output_aliases={12: 7},
        compiler_params=_params(1),
    )(dgc, dgc, dval, conv_w, w_up_g, x1, dout, y1, g_pre, g_post, mod6, h2, dw_up)


def outproj_bwd(dy1, w_out_g, cat, tm=512):
    S = dy1.shape[0]

    def body(dy_ref, w_ref, cat_ref, dpool_ref, dattn_ref, da4_ref, da16_ref, delta_ref, dl4_ref, dl16_ref, dw_ref, scr):
        @pl.when(pl.program_id(0) == 0)
        def _():
            dw_ref[...] = jnp.zeros_like(dw_ref)

        catb = cat_ref[...]
        dcat = jnp.zeros((tm, 512), F32)
        for j in range(N_CHIPS):
            dyj = dy_ref[:, j * 256:(j + 1) * 256]
            dcat = dcat + lax.dot_general(dyj, w_ref[j], NT, preferred_element_type=F32)
            dw_ref[j] += lax.dot_general(catb, dyj, TN, preferred_element_type=F32)
        dpool_ref[...] = dcat[:, :POOL_W]
        dattn = dcat[:, POOL_W:]
        dattn_ref[...] = dattn.astype(BF16)
        for h in range(2):
            scr[h] = dattn[:, h * 128:(h + 1) * 128]
        for d, dst in ((4, da4_ref), (16, da16_ref)):
            for r in range(d):
                for h in range(2):
                    dst[r, :, h * 128:(h + 1) * 128] = scr[h, pl.ds(r, tm // d, stride=d), :].astype(BF16)
        prod = dattn * catb[:, POOL_W:].astype(F32)
        r = lax.broadcasted_iota(jnp.int32, (GROUP_W, GROUP_W), 0) // HEAD_DIM
        c = lax.broadcasted_iota(jnp.int32, (GROUP_W, GROUP_W), 1) // HEAD_DIM
        ones_bd = jnp.where(r == c, 1.0, 0.0).astype(BF16)
        hi = prod.astype(BF16)
        lo = (prod - hi.astype(F32)).astype(BF16)
        delta = jnp.dot(hi, ones_bd, preferred_element_type=F32) + jnp.dot(lo, ones_bd, preferred_element_type=F32)
        delta_ref[...] = delta
        for h in range(2):
            scr[h] = delta[:, h * 128:(h + 1) * 128]
        for d, dst in ((4, dl4_ref), (16, dl16_ref)):
            for r in range(d):
                for h in range(2):
                    dst[r, :, h * 128:(h + 1) * 128] = scr[h, pl.ds(r, tm // d, stride=d), :]

    cls = lambda d: pl.BlockSpec((d, tm // d, GROUP_W), lambda i: (0, i, 0))
    cls_shape = lambda d, dt: jax.ShapeDtypeStruct((d, S // d, GROUP_W), dt)
    return pl.pallas_call(
        body,
        name="outproj_bwd",
        grid=(S // tm,),
        in_specs=[_rows(tm, D_MODEL), _full(w_out_g.shape), _rows(tm, 512)],
        out_specs=[_rows(tm, POOL_W), _rows(tm, GROUP_W), cls(4), cls(16), _rows(tm, GROUP_W), cls(4), cls(16),
                   _full(w_out_g.shape)],
        out_shape=[jax.ShapeDtypeStruct((S, POOL_W), F32), jax.ShapeDtypeStruct((S, GROUP_W), BF16), cls_shape(4, BF16),
                   cls_shape(16, BF16), jax.ShapeDtypeStruct((S, GROUP_W), F32), cls_shape(4, F32), cls_shape(16, F32),
                   jax.ShapeDtypeStruct(w_out_g.shape, F32)],
        scratch_shapes=[pltpu.VMEM((2, tm, 128), F32)],
        compiler_params=_params(1),
    )(dy1, w_out_g, cat)


def attn_bwd(qkv, dattn, lse, delta, d):
    L = qkv.shape[2]
    nb = L // ATT_BLOCK
    cpb = _classes_per_step(d, nb)

    def body(q_ref, k_ref, v_ref, do_ref, l_ref, dl_ref, out_ref, kpad, vpad, dkpad, dvpad):
        for cls in range(cpb):
            kpad[cls, 0:ATT_BLOCK, :] = jnp.zeros((ATT_BLOCK, GROUP_W), BF16)
            vpad[cls, 0:ATT_BLOCK, :] = jnp.zeros((ATT_BLOCK, GROUP_W), BF16)
            kpad[cls, ATT_BLOCK:, :] = k_ref[cls]
            vpad[cls, ATT_BLOCK:, :] = v_ref[cls]
        dkpad[...] = jnp.zeros_like(dkpad)
        dvpad[...] = jnp.zeros_like(dvpad)
        band, col, lo = _attn_masks()

        def step(t, carry):
            cls, n = t // nb, t % nb
            r0 = pl.multiple_of(n * ATT_BLOCK, ATT_BLOCK)
            valid = band & ((col >= ATT_BLOCK) | (n > 0))
            qb = q_ref[cls, pl.ds(r0, ATT_BLOCK), :]
            dob = do_ref[cls, pl.ds(r0, ATT_BLOCK), :]
            lb = l_ref[cls, pl.ds(r0, ATT_BLOCK), :]
            dlb = dl_ref[cls, pl.ds(r0, ATT_BLOCK), :]
            kb = kpad[cls, pl.ds(r0, 2 * ATT_BLOCK), :]
            vb = vpad[cls, pl.ds(r0, 2 * ATT_BLOCK), :]
            for pair in range(2):
                lanes = slice(pair * 128, (pair + 1) * 128)
                qp, dop, kp, vp = qb[:, lanes], dob[:, lanes], kb[:, lanes], vb[:, lanes]
                c0, c1 = pair * 128, pair * 128 + HEAD_DIM
                q2, do2 = _stack_heads(qp, lo), _stack_heads(dop, lo)
                lse2 = jnp.concatenate([lb[:, c0:c0 + 1], lb[:, c1:c1 + 1]], axis=0)
                dl2 = jnp.concatenate([dlb[:, c0:c0 + 1], dlb[:, c1:c1 + 1]], axis=0)
                s = lax.dot_general(q2, kp, NT, preferred_element_type=F32)
                s = jnp.where(valid, s, NEG)
                p = jnp.exp(s - lse2)
                dp = lax.dot_general(do2, vp, NT, preferred_element_type=F32)
                ds = (p * (dp - dl2)).astype(BF16)
                dq2 = jnp.dot(ds, kp, preferred_element_type=F32)
                out_ref[0, cls, pl.ds(r0, ATT_BLOCK), lanes] = _unstack_heads(dq2, lo)
                dkpad[cls, pl.ds(r0, 2 * ATT_BLOCK), lanes] += lax.dot_general(ds, q2, TN, preferred_element_type=F32)
                dvpad[cls, pl.ds(r0, 2 * ATT_BLOCK), lanes] += lax.dot_general(p.astype(BF16), do2, TN, preferred_element_type=F32)
            return carry

        lax.fori_loop(0, cpb * nb, step, 0, unroll=4)
        for cls in range(cpb):
            out_ref[1, cls] = dkpad[cls, ATT_BLOCK:, :]
            out_ref[2, cls] = dvpad[cls, ATT_BLOCK:, :]

    spec = lambda kind: pl.BlockSpec((None, cpb, L, GROUP_W), lambda r: (kind, r, 0, 0))
    per_cls = pl.BlockSpec((cpb, L, GROUP_W), lambda r: (r, 0, 0))
    return pl.pallas_call(
        body,
        name=f"attn_bwd_d{d}",
        grid=(d // cpb,),
        in_specs=[spec(0), spec(1), spec(2), per_cls, per_cls, per_cls],
        out_specs=pl.BlockSpec((3, cpb, L, GROUP_W), lambda r: (0, r, 0, 0)),
        out_shape=jax.ShapeDtypeStruct((3, d, L, GROUP_W), F32),
        scratch_shapes=[pltpu.VMEM((cpb, L + ATT_BLOCK, GROUP_W), BF16)] * 2 + [pltpu.VMEM((cpb, L + ATT_BLOCK, GROUP_W), F32)] * 2,
        compiler_params=_params(1),
    )(qkv, qkv, qkv, dattn, lse, delta)


def pool_bwd(dpool, mixed, wbd, b, scale):
    S = dpool.shape[0]

    def body(dp_ref, mx_ref, w_ref, b_ref, s_ref, du_ref, dw_ref, db_ref, ds_ref):
        dp = dp_ref[...]
        mb = mx_ref[...]
        wv = w_ref[...]
        ypre = jnp.dot(mb, wv, preferred_element_type=F32) + b_ref[...]
        ds_ref[...] = _colsum(dp * ypre)
        dpre = dp * s_ref[...]
        db_ref[...] = _colsum(dpre)
        dpb = dpre.astype(BF16)
        dw_ref[...] = lax.dot_general(mb, dpb, TN, preferred_element_type=F32)
        dmix = lax.dot_general(dpb, wv, NT, preferred_element_type=F32)
        row = lax.broadcasted_iota(jnp.int32, dmix.shape, 0)
        lane, win = _pool_lane_windows(dmix.shape)
        e = dmix / jnp.minimum(row + 1, win).astype(F32)

        def shift(a, k):
            return jnp.where(row < S - k, pltpu.roll(a, S - k, 0), 0.0)

        f2 = e + shift(e, 1)
        f4 = f2 + shift(f2, 2)
        f8 = f4 + shift(f4, 4)
        f16 = f8 + shift(f8, 8)
        du_ref[...] = jnp.where(lane < 64, f2, jnp.where(lane < 128, f4, jnp.where(lane < 192, f8, f16))) - dmix

    vm = pl.BlockSpec(memory_space=pltpu.VMEM)
    return pl.pallas_call(
        body,
        name="pool_bwd",
        in_specs=[vm] * 5,
        out_specs=[vm] * 4,
        out_shape=[jax.ShapeDtypeStruct((S, POOL_W), F32), jax.ShapeDtypeStruct((POOL_W, POOL_W), F32),
                   jax.ShapeDtypeStruct((1, POOL_W), F32), jax.ShapeDtypeStruct((1, POOL_W), F32)],
        compiler_params=_params(),
    )(dpool, mixed, wbd, b, scale)


def inproj_bwd(dqkv, du, x, dx1, w_in_g, g, mod6, tc, tsa, tsb, h1, tm=512):
    S = x.shape[0]

    def body(d0, d1, d2, du_ref, x_ref, dx1_ref, w_ref, g_ref, mod_ref, tc_ref, tsa_ref, tsb_ref, h_ref,
             gx_ref, dsh_ref, dsc_ref, dg_ref, dw_ref, s4, s16, dp_ref):
        @pl.when(pl.program_id(0) == 0)
        def _():
            _zero(dw_ref, dsh_ref, dsc_ref, dg_ref)

        cs, sa, sb = tc_ref[...], tsa_ref[...], tsb_ref[...]
        for d, src, dst in ((4, d1, s4), (16, d2, s16)):
            for kind in range(3):
                for r in range(d):
                    for h in range(2):
                        dst[kind, h, pl.ds(r, tm // d, stride=d), :] = src[kind, r, :, h * 128:(h + 1) * 128]
        for sp in range(20):
            piece, half = sp // 2, sp % 2
            lanes = slice(half * 128, (half + 1) * 128)
            if piece == 0:
                blk = du_ref[:, lanes]
            else:
                kind, gi = (piece - 1) // 3, (piece - 1) % 3
                blk = d0[kind, 0, :, lanes] if gi == 0 else (s4, s16)[gi - 1][kind, half]
                if kind == 0:
                    blk = _rope128(blk, cs, sa, sb, -1.0) * (HEAD_DIM ** -0.5)
                elif kind == 1:
                    blk = _rope128(blk, cs, sa, sb, -1.0)
            dp_ref[:, sp * 128:(sp + 1) * 128] = blk.astype(BF16)
        dh = jnp.zeros((tm, D_MODEL), F32)
        hbt = h_ref[...].T
        for j in range(N_CHIPS):
            dpj = dp_ref[:, j * 640:(j + 1) * 640]
            dh = dh + lax.dot_general(dpj, w_ref[j], NT, preferred_element_type=F32)
            dw_ref[j] += jnp.dot(hbt, dpj, preferred_element_type=F32)
        xv = x_ref[...]
        rstd = lax.rsqrt(jnp.mean(xv * xv, axis=-1, keepdims=True) + NORM_EPS)
        n1 = xv * rstd
        gv = g_ref[...]
        one_sc = 1.0 + mod_ref[1:2, :]
        dsh_ref[...] += _colsum(dh)
        dsc_ref[...] += _colsum(dh * (n1 * gv))
        dg_ref[...] += _colsum(dh * one_sc * n1)
        dn = dh * (gv * one_sc)
        gx_ref[...] = dx1_ref[...] + rstd * (dn - n1 * jnp.mean(dn * n1, axis=-1, keepdims=True))

    vec = _full((1, D_MODEL))
    dspec = lambda d: pl.BlockSpec((3, d, tm // d, GROUP_W), lambda i: (0, 0, i, 0))
    return pl.pallas_call(
        body,
        name="inproj_bwd",
        grid=(S // tm,),
        in_specs=[dspec(d) for d in DILATIONS] + [_rows(tm, POOL_W), _rows(tm, D_MODEL), _rows(tm, D_MODEL), _full(w_in_g.shape),
                                                  vec, _full((6, D_MODEL)), _rows(tm, 128), _rows(tm, 128), _rows(tm, 128),
                                                  _rows(tm, D_MODEL)],
        out_specs=[_rows(tm, D_MODEL), vec, vec, vec, _full(w_in_g.shape)],
        out_shape=[jax.ShapeDtypeStruct((S, D_MODEL), F32)] + [jax.ShapeDtypeStruct((1, D_MODEL), F32)] * 3
        + [jax.ShapeDtypeStruct(w_in_g.shape, F32)],
        scratch_shapes=[pltpu.VMEM((3, 2, tm, 128), F32)] * 2 + [pltpu.VMEM((tm, IN_W), BF16)],
        compiler_params=_params(1),
    )(*dqkv, du, x, dx1, w_in_g, g, mod6, tc, tsa, tsb, h1)


def _adamw(w, g, m, v):
    m = ADAM_B1 * m + (1.0 - ADAM_B1) * g
    v = ADAM_B2 * v + (1.0 - ADAM_B2) * (g * g)
    m_hat = m / (1.0 - ADAM_B1 ** ADAM_STEP)
    v_hat = v / (1.0 - ADAM_B2 ** ADAM_STEP)
    delta = -ADAM_LR * (m_hat / (jnp.sqrt(v_hat) + ADAM_EPS) + ADAM_WD * w)
    return delta, m, v


def adamw_rows(w, g, m, v, tr, name):
    R, C = w.shape

    def body(w_ref, g_ref, m_ref, v_ref, go_ref, d_ref, mo_ref, vo_ref):
        g = g_ref[...]
        go_ref[...] = g
        d_ref[...], mo_ref[...], vo_ref[...] = _adamw(w_ref[...], g, m_ref[...], v_ref[...])

    spec = pl.BlockSpec((tr, C), lambda i: (i, 0))
    return pl.pallas_call(
        body,
        name=name,
        grid=(R // tr,),
        in_specs=[spec] * 4,
        out_specs=[spec] * 4,
        out_shape=[jax.ShapeDtypeStruct((R, C), F32)] * 4,
        compiler_params=_params(1),
    )(w, g, m, v)


def adamw_ada(c_all_t, dmod_cols, w, m, v, tr=256):
    R, C = w.shape

    def body(ct_ref, dm_ref, w_ref, m_ref, v_ref, g_ref, d_ref, mo_ref, vo_ref):
        ct = ct_ref[...]
        act = ct * jax.nn.sigmoid(ct)
        dm = dm_ref[...]
        a_hi, d_hi = act.astype(BF16), dm.astype(BF16)
        a_lo, d_lo = (act - a_hi.astype(F32)).astype(BF16), (dm - d_hi.astype(F32)).astype(BF16)
        g = (jnp.dot(a_hi, d_hi, preferred_element_type=F32) + jnp.dot(a_lo, d_hi, preferred_element_type=F32)
             + jnp.dot(a_hi, d_lo, preferred_element_type=F32))
        g_ref[...] = g
        d_ref[...], mo_ref[...], vo_ref[...] = _adamw(w_ref[...], g, m_ref[...], v_ref[...])

    spec = pl.BlockSpec((tr, C), lambda i: (i, 0))
    return pl.pallas_call(
        body,
        name="adamw_ada",
        grid=(R // tr,),
        in_specs=[pl.BlockSpec((tr, N_DEV), lambda i: (i, 0)), _full((N_DEV, C)), spec, spec, spec],
        out_specs=[spec] * 4,
        out_shape=[jax.ShapeDtypeStruct((R, C), F32)] * 4,
        compiler_params=_params(1),
    )(c_all_t, dmod_cols, w, m, v)


def adamw_small(slab_a, slab_b, convw_g, wpool_g, params):
    names = ["b_ada", "g_pre_mix", "g_post_mix", "g_pre_ffn", "g_post_ffn", "b_pool", "pool_scale", "conv_b", "conv_w", "w_pool"]
    flat = []
    for n in names:
        flat += list(params[n])

    def body(a_ref, b_ref, cw_ref, wp_ref, *rest):
        ins, outs = rest[:30], rest[30:]

        def dev_sum(ref):
            t = ref[0]
            for dev in range(1, N_DEV):
                t = t + ref[dev]
            return t

        sa, sb_, scw, swp = dev_sum(a_ref), dev_sum(b_ref), dev_sum(cw_ref), dev_sum(wp_ref)
        grads = [
            jnp.concatenate([sa[k:k + 1, :] for k in range(6)], axis=1),
            sa[6:7, :], sa[7:8, :], sa[8:9, :], sa[9:10, :],
            sa[10:11, 0:256], sa[10:11, 256:512],
            sb_[3:4, :], scw, swp,
        ]
        for i, g in enumerate(grads):
            w_ref, m_ref, v_ref = ins[3 * i:3 * i + 3]
            if names[i] == "b_pool":
                parts = [((0, slice(grp, grp + 1)), g[:, grp * 64:(grp + 1) * 64]) for grp in range(4)]
            elif names[i] == "w_pool":
                parts = [((0, grp), g[grp * 64:(grp + 1) * 64, :]) for grp in range(4)]
            elif names[i] == "conv_w":
                parts = [((0,), g)]
            else:
                parts = [((Ellipsis,), g)]
            for at, gp in parts:
                d, mo, vo = _adamw(w_ref[at], gp, m_ref[at], v_ref[at])
                for k, val in enumerate((gp, d, mo, vo)):
                    outs[4 * i + k][at] = val
        outs[-1][...] = sa[10:11, 512:640]

    vm = pl.BlockSpec(memory_space=pltpu.VMEM)
    out_shape = []
    for n in names:
        out_shape += [jax.ShapeDtypeStruct(params[n][0].shape, F32)] * 4
    out_shape.append(jax.ShapeDtypeStruct((1, 128), F32))
    outs = pl.pallas_call(
        body,
        name="adamw_small",
        in_specs=[vm] * (4 + len(flat)),
        out_specs=[vm] * len(out_shape),
        out_shape=out_shape,
        compiler_params=_params(),
    )(slab_a, slab_b, convw_g, wpool_g, *flat)
    return {n: outs[4 * i:4 * i + 4] for i, n in enumerate(names)}, outs[-1]


def _place():
    return lax.axis_index("x"), lax.axis_index("y"), lax.axis_index("c")


def _other_chips(x, y):
    return [(1 - x, y), (x, 1 - y), (1 - x, 1 - y)]


def _chip_id(cx, cy):
    return 2 * cx + cy


HBM_SPEC = pl.BlockSpec(memory_space=pltpu.HBM)
SEM_SPEC = pl.BlockSpec(memory_space=pltpu.SEMAPHORE)
ANY_SPEC = pl.BlockSpec(memory_space=pl.ANY)
EFFECT = pltpu.SideEffectType.DATAFLOW_SIDE_EFFECTING


def _hbm(t):
    return pltpu.with_memory_space_constraint(t, pltpu.HBM)


def _hbm_shapes(ts):
    return [pltpu.HBM(t.shape, t.dtype) for t in ts]


def _half_rows(ref, lead, half, rh):
    return ref.at[lead, pl.ds(half * rh, rh), :]


def _flips():
    return [(fx, fy, fc) for fx in (0, 1) for fy in (0, 1) for fc in (0, 1)][1:]


def _flip(v, f):
    return v if f == 0 else 1 - v


def ada_mod(c3, w_ada, b_cols, conv_w):
    CB = w_ada.shape[1]

    def body(c_ref, w_hbm, b_ref, cw_ref, call_ref, mod_ref, cwall_ref, modall, send_sems, recv_sems, w_ref, w_sem):
        x, y, c = _place()
        me_dev = 4 * x + 2 * y + c
        me = _chip_id(x, y)
        w_load = pltpu.make_async_copy(w_hbm, w_ref, w_sem)
        w_load.start()
        call_ref[me_dev] = c_ref[0]
        cwall_ref[me] = cw_ref[...]
        sends = []
        for k, (cx, cy) in enumerate(_other_chips(x, y)):
            cp = pltpu.make_async_remote_copy(src_ref=cw_ref, dst_ref=cwall_ref.at[me], send_sem=send_sems.at[10 + k],
                                              recv_sem=recv_sems.at[10 + k], device_id=(cx, cy, c), device_id_type=MESH)
            cp.start()
            sends.append(cp)
        for k, (fx, fy, fc) in enumerate(_flips()):
            cp = pltpu.make_async_remote_copy(src_ref=c_ref.at[0], dst_ref=call_ref.at[me_dev], send_sem=send_sems.at[k],
                                              recv_sem=recv_sems.at[k],
                                              device_id=(_flip(x, fx), _flip(y, fy), _flip(c, fc)), device_id_type=MESH)
            cp.start()
            sends.append(cp)
        for k, (fx, fy, fc) in enumerate(_flips()):
            peer = 4 * _flip(x, fx) + 2 * _flip(y, fy) + _flip(c, fc)
            pltpu.make_async_remote_copy(src_ref=c_ref.at[0], dst_ref=call_ref.at[peer], send_sem=send_sems.at[k],
                                         recv_sem=recv_sems.at[k], device_id=(x, y, c), device_id_type=MESH).wait_recv()
        row = lax.broadcasted_iota(jnp.int32, (N_DEV, D_MODEL), 0)
        call = jnp.zeros((N_DEV, D_MODEL), F32)
        for dev in range(N_DEV):
            call = jnp.where(row == dev, call_ref[dev], call)
        act = call * jax.nn.sigmoid(call)
        w_load.wait()
        wv = w_ref[...]
        w_hi = wv.astype(BF16)
        w_lo = (wv - w_hi.astype(F32)).astype(BF16)
        a_hi = act.astype(BF16)
        a_lo = (act - a_hi.astype(F32)).astype(BF16)
        prod = (jnp.dot(a_hi, w_hi, preferred_element_type=F32) + jnp.dot(a_lo, w_hi, preferred_element_type=F32)
                + jnp.dot(a_hi, w_lo, preferred_element_type=F32))
        modall[me] = prod + b_ref[...]
        for k, (cx, cy) in enumerate(_other_chips(x, y)):
            cp = pltpu.make_async_remote_copy(src_ref=modall.at[me], dst_ref=modall.at[me], send_sem=send_sems.at[7 + k],
                                              recv_sem=recv_sems.at[7 + k], device_id=(cx, cy, c), device_id_type=MESH)
            cp.start()
            sends.append(cp)
        for k, (cx, cy) in enumerate(_other_chips(x, y)):
            blk = modall.at[_chip_id(cx, cy)]
            pltpu.make_async_remote_copy(src_ref=blk, dst_ref=blk, send_sem=send_sems.at[7 + k], recv_sem=recv_sems.at[7 + k],
                                         device_id=(x, y, c), device_id_type=MESH).wait_recv()
        for k, (cx, cy) in enumerate(_other_chips(x, y)):
            blk = cwall_ref.at[_chip_id(cx, cy)]
            pltpu.make_async_remote_copy(src_ref=blk, dst_ref=blk, send_sem=send_sems.at[10 + k], recv_sem=recv_sems.at[10 + k],
                                         device_id=(x, y, c), device_id_type=MESH).wait_recv()
        for cp in sends:
            cp.wait_send()
        mine = [modall[j, pl.ds(me_dev, 1), :] for j in range(N_CHIPS)]
        for r in range(6):
            pieces = []
            for h in range(2):
                pos = r * D_MODEL + h * 512
                pieces.append(mine[pos // CB][:, pos % CB:pos % CB + 512])
            mod_ref[r:r + 1, :] = jnp.concatenate(pieces, axis=1)

    vm = pl.BlockSpec(memory_space=pltpu.VMEM)
    return pl.pallas_call(
        body,
        name="ada_mod",
        in_specs=[vm, ANY_SPEC, vm, vm],
        out_specs=[vm] * 3,
        out_shape=[jax.ShapeDtypeStruct((N_DEV, 1, D_MODEL), F32), jax.ShapeDtypeStruct((6, D_MODEL), F32),
                   jax.ShapeDtypeStruct((N_CHIPS,) + conv_w.shape, F32)],
        scratch_shapes=[pltpu.VMEM((N_CHIPS, N_DEV, CB), F32), pltpu.SemaphoreType.DMA((13,)), pltpu.SemaphoreType.DMA((13,)),
                        pltpu.VMEM(w_ada.shape, F32), pltpu.SemaphoreType.DMA],
        compiler_params=pltpu.CompilerParams(has_side_effects=True, vmem_limit_bytes=VMEM_LIMIT),
    )(c3, w_ada, b_cols, conv_w)


def split_start(name, bufs, plan, n_sem, carry):
    nb = len(bufs)
    many = isinstance(carry, (list, tuple))
    alls = list(bufs) + (list(carry) if many else [carry])
    na = len(alls)

    def body(*refs):
        x, y, c = _place()
        ssem, rsem = refs[na], refs[na + 1]
        for i, (src, dst, dev) in enumerate(plan(refs[:nb], x, y, c)):
            pltpu.make_async_remote_copy(src_ref=src, dst_ref=dst, send_sem=ssem.at[i], recv_sem=rsem.at[i], device_id=dev,
                                         device_id_type=MESH).start()

    outs = pl.pallas_call(
        body,
        name=name,
        out_shape=[pltpu.SemaphoreType.DMA((n_sem,)), pltpu.SemaphoreType.DMA((n_sem,))] + _hbm_shapes(alls),
        in_specs=[HBM_SPEC] * na,
        out_specs=[SEM_SPEC, SEM_SPEC] + [HBM_SPEC] * na,
        input_output_aliases={i: 2 + i for i in range(na)},
        compiler_params=pltpu.CompilerParams(has_side_effects=EFFECT),
    )(*[_hbm(t) for t in alls])
    return outs[0], outs[1], list(outs[2:2 + nb]), (list(outs[2 + nb:]) if many else outs[-1])


def split_wait(name, ssem, rsem, bufs, plan, after):
    nb = len(bufs)

    def body(*refs):
        x, y, c = _place()
        s_ref, r_ref = refs[nb], refs[nb + 1]
        for i, (src, dst, dev) in enumerate(plan(refs[:nb], x, y, c)):
            cp = pltpu.make_async_remote_copy(src_ref=src, dst_ref=dst, send_sem=s_ref.at[i], recv_sem=r_ref.at[i], device_id=dev,
                                              device_id_type=MESH)
            cp.wait_send()
            cp.wait_recv()

    outs = pl.pallas_call(
        body,
        name=name,
        out_shape=_hbm_shapes(bufs),
        in_specs=[HBM_SPEC] * nb + [SEM_SPEC, SEM_SPEC, ANY_SPEC],
        out_specs=[HBM_SPEC] * nb,
        input_output_aliases={i: i for i in range(nb)},
        compiler_params=pltpu.CompilerParams(has_side_effects=EFFECT),
    )(*bufs, ssem, rsem, after)
    return list(outs)


def _gather_ici_plan(n):
    def plan(refs, x, y, c):
        out = []
        for w in range(n):
            rh = refs[w].shape[0] // 2
            for cx, cy in _other_chips(x, y):
                out.append((refs[w].at[pl.ds(c * rh, rh), :], _half_rows(refs[n + w], _chip_id(x, y), c, rh), (cx, cy, c)))
        return out

    return plan


def _gather_d2d_plan(n):
    def plan(refs, x, y, c):
        out = []
        for w in range(n):
            rh = refs[w].shape[1] // 2
            for cx, cy in _other_chips(x, y):
                blk = _half_rows(refs[w], _chip_id(cx, cy), c, rh)
                out.append((blk, blk, (x, y, 1 - c)))
        return out

    return plan


def _dev_id(x, y, c):
    return 4 * x + 2 * y + c


def _small_ici_plan(n):
    def plan(refs, x, y, c):
        out = []
        for w in range(n):
            dst = refs[n + w].at[_dev_id(x, y, c)]
            out.append((refs[w], dst, (x, y, 1 - c)))
            for cx, cy in _other_chips(x, y):
                out.append((refs[w], dst, (cx, cy, c)))
        return out

    return plan


def _small_d2d_plan(n):
    def plan(refs, x, y, c):
        out = []
        for w in range(n):
            for cx, cy in _other_chips(x, y):
                blk = refs[w].at[_dev_id(cx, cy, c)]
                out.append((blk, blk, (x, y, 1 - c)))
        return out

    return plan


def _rs_d2d_plan(n):
    def plan(refs, x, y, c):
        out = []
        for w in range(n):
            rh = refs[w].shape[1] // 2
            out.append((refs[w].at[:, pl.ds((1 - c) * rh, rh), :], refs[n + w], (x, y, 1 - c)))
        return out

    return plan


def _rs_ici_plan(n):
    def plan(refs, x, y, c):
        out = []
        for w in range(n):
            for k, (cx, cy) in enumerate(_other_chips(x, y)):
                out.append((refs[w].at[_chip_id(cx, cy)], refs[n + w].at[k], (cx, cy, c)))
        return out

    return plan


def _rs_share_plan(n):
    def plan(refs, x, y, c):
        out = []
        for w in range(n):
            rh = refs[w].shape[0] // 2
            rows = refs[w].at[pl.ds(c * rh, rh), :]
            out.append((rows, rows, (x, y, 1 - c)))
        return out

    return plan


def rs_add(grad, sibbuf, place, tr, name):
    _, R, C = grad.shape
    nt = (R // 2) // tr

    def body(p_ref, g_ref, s_ref, o_ref):
        o_ref[...] = (g_ref[...] + s_ref[...]).astype(BF16)

    return pl.pallas_call(
        body,
        name=name,
        grid_spec=pltpu.PrefetchScalarGridSpec(
            num_scalar_prefetch=1,
            grid=(N_CHIPS, nt),
            in_specs=[pl.BlockSpec((None, tr, C), lambda j, i, p: (j, p[0] * nt + i, 0)),
                      pl.BlockSpec((None, tr, C), lambda j, i, p: (j, i, 0))],
            out_specs=pl.BlockSpec((None, tr, C), lambda j, i, p: (j, i, 0)),
        ),
        out_shape=jax.ShapeDtypeStruct((N_CHIPS, R // 2, C), BF16),
        compiler_params=_params(2),
    )(place, grad, sibbuf)


def rs_final(grad, sibbuf, rbuf, place, tr, name):
    _, R, C = grad.shape
    nt = (R // 2) // tr

    def body(p_ref, g_ref, s_ref, r_ref, o_ref):
        o_ref[...] = (((g_ref[...] + s_ref[...]) + r_ref[0].astype(F32)) + r_ref[1].astype(F32)) + r_ref[2].astype(F32)

    return pl.pallas_call(
        body,
        name=name,
        grid_spec=pltpu.PrefetchScalarGridSpec(
            num_scalar_prefetch=1,
            grid=(nt,),
            in_specs=[pl.BlockSpec((None, tr, C), lambda i, p: (p[1], p[0] * nt + i, 0)),
                      pl.BlockSpec((None, tr, C), lambda i, p: (p[1], i, 0)),
                      pl.BlockSpec((3, tr, C), lambda i, p: (0, i, 0))],
            out_specs=pl.BlockSpec((tr, C), lambda i, p: (p[0] * nt + i, 0)),
        ),
        out_shape=jax.ShapeDtypeStruct((R, C), F32),
        compiler_params=_params(1),
    )(place, grad, sibbuf, rbuf)


class GradReduce:
    def __init__(self, tag, grads, rows, place):
        self.tag, self.grads, self.rows, self.place = tag, grads, rows, place
        self.n = len(grads)

    def d2d_start(self, carry):
        sib = [lax.empty((N_CHIPS, g.shape[1] // 2, g.shape[2]), F32) for g in self.grads]
        self.s1, self.r1, bufs, carry = split_start(f"rs_{self.tag}_d2d_start", self.grads + sib, _rs_d2d_plan(self.n), self.n, carry)
        self.bufs1 = bufs
        return carry

    def add_and_ici_start(self, after, carry):
        bufs = split_wait(f"rs_{self.tag}_d2d_wait", self.s1, self.r1, self.bufs1, _rs_d2d_plan(self.n), after)
        self.grads, self.sib = bufs[:self.n], bufs[self.n:]
        pb = [rs_add(g, s, self.place, tr, f"rs_{self.tag}_add{w}")
              for w, (g, s, tr) in enumerate(zip(self.grads, self.sib, self.rows))]
        rb = [lax.empty((3,) + p.shape[1:], BF16) for p in pb]
        self.s2, self.r2, self.bufs2, carry = split_start(f"rs_{self.tag}_ici_start", pb + rb, _rs_ici_plan(self.n), 3 * self.n, carry)
        return carry

    def final_and_share_start(self, after, carry):
        bufs = split_wait(f"rs_{self.tag}_ici_wait", self.s2, self.r2, self.bufs2, _rs_ici_plan(self.n), after)
        rb = bufs[self.n:]
        full = [rs_final(g, s, r, self.place, tr, f"rs_{self.tag}_final{w}")
                for w, (g, s, r, tr) in enumerate(zip(self.grads, self.sib, rb, self.rows))]
        self.s3, self.r3, self.bufs3, carry = split_start(f"rs_{self.tag}_share_start", full, _rs_share_plan(self.n), self.n, carry)
        return carry

    def finish(self, after):
        return split_wait(f"rs_{self.tag}_share_wait", self.s3, self.r3, self.bufs3, _rs_share_plan(self.n), after)


def _rope_tables(positions):
    inv_freq = ROPE_THETA ** (-jnp.arange(0, ROT_DIM, 2, dtype=F32) / ROT_DIM)
    ang = positions.astype(F32)[:, None] * inv_freq
    cos, sin = jnp.cos(ang), jnp.sin(ang)
    S = positions.shape[0]
    one, zero = jnp.ones((S, 48), F32), jnp.zeros((S, 48), F32)
    z8 = jnp.zeros((S, 8), F32)
    tc = jnp.concatenate([cos, cos, one], axis=1)
    tsa = jnp.concatenate([z8, sin, zero], axis=1)
    tsb = jnp.concatenate([-sin, z8, zero], axis=1)
    return tuple(jnp.tile(t, (1, 2)) for t in (tc, tsa, tsb))


def _block_diag(w_pool):
    wbd = jnp.zeros((POOL_W, POOL_W), F32)
    for gi in range(4):
        wbd = wbd.at[gi * 64:(gi + 1) * 64, gi * 64:(gi + 1) * 64].set(w_pool[gi])
    return wbd


def kernel(x, c, positions, w_ada, b_ada, g_pre_mix, g_post_mix, g_pre_ffn, g_post_ffn, w_in, w_pool, b_pool, pool_scale, w_out, w_up, conv_w, conv_b, w_down, loss_target, m_w_ada, m_b_ada, m_g_pre_mix, m_g_post_mix, m_g_pre_ffn, m_g_post_ffn, m_w_in, m_w_pool, m_b_pool, m_pool_scale, m_w_out, m_w_up, m_conv_w, m_conv_b, m_w_down, v_w_ada, v_b_ada, v_g_pre_mix, v_g_post_mix, v_g_pre_ffn, v_g_post_ffn, v_w_in, v_w_pool, v_b_pool, v_pool_scale, v_w_out, v_w_up, v_conv_w, v_conv_b, v_w_down):
    xi, yi, ci = lax.axis_index("x"), lax.axis_index("y"), lax.axis_index("c")
    chip = 2 * xi + yi
    place = jnp.stack([ci, chip]).astype(jnp.int32)
    x2, tgt = x[0], loss_target[0]
    S = x2.shape[0]

    def landing(s_):
        return lax.dynamic_update_slice(lax.empty((N_CHIPS,) + s_.shape, s_.dtype), s_[None], (chip, 0, 0))

    cb_ada = w_ada.shape[2]
    b_cols = lax.dynamic_slice(b_ada, (0, chip * cb_ada), (1, cb_ada))
    c_all, mod6, conv_w_g = ada_mod(c.reshape(1, 1, D_MODEL), w_ada[0], b_cols, conv_w[0])
    conv_w_f = jnp.transpose(conv_w_g, (1, 0, 2)).reshape(3, D_FF)
    mix_sh = [w_in[0].astype(BF16), w_out[0].astype(BF16)]
    ffn_sh = [w_up[0].astype(BF16), w_down[0].astype(BF16)]
    ga_s, ga_r, ga_bufs, mod6 = split_start("gather_mix_ici_start", mix_sh + [landing(t) for t in mix_sh], _gather_ici_plan(2), 6, mod6)
    gb_s, gb_r, gb_bufs, (mod6, tc, tsa, tsb) = split_start("gather_ffn_ici_start", ffn_sh + [landing(t) for t in ffn_sh],
                                                            _gather_ici_plan(2), 6, [mod6, *_rope_tables(positions[0])])
    wbd = _block_diag(w_pool[0]).astype(BF16)
    b_pool2, scale2 = b_pool.reshape(1, POOL_W), pool_scale
    ga_bufs = split_wait("gather_mix_ici_wait", ga_s, ga_r, ga_bufs, _gather_ici_plan(2), mod6)
    gc_s, gc_r, mix_land, mod6 = split_start("gather_mix_d2d_start", ga_bufs[2:], _gather_d2d_plan(2), 6, mod6)
    w_in_g, w_out_g = split_wait("gather_mix_d2d_wait", gc_s, gc_r, mix_land, _gather_d2d_plan(2), mod6)

    h1, u, *qkv = inproj_fwd(x2, g_pre_mix, mod6, w_in_g, tc, tsa, tsb)
    mixed, pool = pool_fwd(u, wbd, b_pool2, scale2)
    o_l = [attn_fwd(t, d) for t, d in zip(qkv, DILATIONS)]
    attn_done = sum(l[0, :8, :128] for _, l in o_l)
    gb_bufs = split_wait("gather_ffn_ici_wait", gb_s, gb_r, gb_bufs, _gather_ici_plan(2), attn_done)
    gd_s, gd_r, ffn_land, pool = split_start("gather_ffn_d2d_start", gb_bufs[2:], _gather_d2d_plan(2), 6, pool)
    cat, lse, lse4, lse16, y1, x1, h2 = outproj_fwd([o for o, _ in o_l] + [l for _, l in o_l], pool, x2, w_out_g, g_post_mix,
                                                    g_pre_ffn, mod6)
    lses = [lse[None], lse4, lse16]
    w_up_g, w_down_g = split_wait("gather_ffn_d2d_wait", gd_s, gd_r, ffn_land, _gather_d2d_plan(2), h2)
    w_down_f = w_down_g.reshape(D_FF, D_MODEL)
    gate, val, dy2, dout, loss_v, d_gt_f, d_g_post_ffn = ffn_fwd(h2, w_up_g, conv_w_f, conv_b, w_down_f, x1, tgt, g_post_ffn, mod6)

    dgc, dval, d_conv_w, d_conv_b, dw_down, dw_up = down_bwd(dy2, w_down_f, gate, val, conv_w_f, conv_b, h2)
    dx1, dy1, d_sh_f, d_sc_f, d_g_pre_ffn, d_gt_m, d_g_post_mix, dw_up = up_bwd(
        dgc, dval, conv_w_f, w_up_g, x1, dout, y1, g_pre_ffn, g_post_mix, mod6, h2, dw_up)
    rs_ffn = GradReduce("ffn", [dw_up, dw_down.reshape(N_CHIPS, D_FF // N_CHIPS, D_MODEL)], [256, 176], place)
    dy1 = rs_ffn.d2d_start(dy1)
    dpool, da1, da4, da16, dl1, dl4, dl16, dw_out = outproj_bwd(dy1, w_out_g, cat)
    dpool = rs_ffn.add_and_ici_start(dw_out, dpool)
    du, d_wbd, d_b_pool, d_scale = pool_bwd(dpool, mixed, wbd, b_pool2, scale2)
    dqkv = [attn_bwd(t, da, ls, dl, d) for t, da, ls, dl, d in zip(qkv, (da1[None], da4, da16), lses, (dl1[None], dl4, dl16), DILATIONS)]
    grad_x, d_sh_m, d_sc_m, d_g_pre_mix, dw_in = inproj_bwd(dqkv, du, x2, dx1, w_in_g, g_pre_mix, mod6, tc, tsa, tsb, h1)

    z1 = jnp.zeros((1, D_MODEL), F32)
    slab_a = jnp.concatenate(
        [d_sh_m, d_sc_m, d_gt_m, d_sh_f, d_sc_f, d_gt_f, d_g_pre_mix, d_g_post_mix, d_g_pre_ffn, d_g_post_ffn,
         jnp.concatenate([d_b_pool, d_scale, loss_v, jnp.zeros((1, 384), F32)], axis=1)] + [z1] * 5, axis=0)
    slab_b = jnp.concatenate([d_conv_w, d_conv_b, jnp.zeros((4, D_FF), F32)], axis=0)
    d_wpool = jnp.concatenate([d_wbd[gi * 64:(gi + 1) * 64, gi * 64:(gi + 1) * 64] for gi in range(4)], axis=0)
    dev = _dev_id(xi, yi, ci)
    small_src = [slab_a, slab_b, d_wpool]
    small_land = [lax.dynamic_update_slice(lax.empty((N_DEV,) + t.shape, F32), t[None], (dev, 0, 0)) for t in small_src]
    tok = jnp.zeros((8, 128), F32)
    gs_s, gs_r, gs_bufs, tok = split_start("small_ici_start", small_src + small_land, _small_ici_plan(3), 12, tok)
    rs_mix = GradReduce("mix", [dw_in, dw_out], [256, 256], place)
    tok = rs_mix.d2d_start(tok)
    tok = rs_ffn.final_and_share_start(tok, tok)
    gs_bufs = split_wait("small_ici_wait", gs_s, gs_r, gs_bufs, _small_ici_plan(3), tok)
    gt_s, gt_r, small_land, tok = split_start("small_d2d_start", gs_bufs[3:], _small_d2d_plan(3), 9, tok)
    tok = rs_mix.add_and_ici_start(tok, tok)
    slab_a_g, slab_b_g, wpool_g = split_wait("small_d2d_wait", gt_s, gt_r, small_land, _small_d2d_plan(3), tok)
    cw_cols = conv_w.shape[2]
    convw_g = lax.dynamic_slice(slab_b_g, (0, 0, chip * cw_cols), (N_DEV, 3, cw_cols))
    dmod_cols = lax.dynamic_slice(slab_a_g[:, :6, :].reshape(N_DEV, 6 * D_MODEL), (0, chip * cb_ada), (N_DEV, cb_ada))

    res = {}

    def big_adamw(name, w, g, m, v, tr):
        g_, d_, m_, v_ = adamw_rows(w[0], g, m[0], v[0], tr, "adamw_" + name)
        res[name] = (g_[None], d_[None], m_[None], v_[None])
        return v_

    g_ada, d_ada, m_ada, v_ada = adamw_ada(c_all.reshape(N_DEV, D_MODEL).T, dmod_cols, w_ada[0], m_w_ada[0], v_w_ada[0])
    res["w_ada"] = (g_ada[None], d_ada[None], m_ada[None], v_ada[None])
    g_w_up, g_w_down = rs_ffn.finish(v_ada)
    big_adamw("w_up", w_up, g_w_up, m_w_up, v_w_up, 256)
    last = big_adamw("w_down", w_down, g_w_down, m_w_down, v_w_down, 352)
    rs_mix.final_and_share_start(last, jnp.zeros((8, 128), F32))
    g_w_in, g_w_out = rs_mix.finish(last)
    big_adamw("w_in", w_in, g_w_in, m_w_in, v_w_in, 256)
    big_adamw("w_out", w_out, g_w_out, m_w_out, v_w_out, 256)
    small, loss_sum = adamw_small(slab_a_g, slab_b_g, convw_g, wpool_g, {
        "b_ada": (b_ada, m_b_ada, v_b_ada), "g_pre_mix": (g_pre_mix, m_g_pre_mix, v_g_pre_mix),
        "g_post_mix": (g_post_mix, m_g_post_mix, v_g_post_mix), "g_pre_ffn": (g_pre_ffn, m_g_pre_ffn, v_g_pre_ffn),
        "g_post_ffn": (g_post_ffn, m_g_post_ffn, v_g_post_ffn), "b_pool": (b_pool, m_b_pool, v_b_pool),
        "pool_scale": (pool_scale, m_pool_scale, v_pool_scale), "conv_b": (conv_b, m_conv_b, v_conv_b),
        "conv_w": (conv_w, m_conv_w, v_conv_w), "w_pool": (w_pool, m_w_pool, v_w_pool)})
    for name in ("b_ada", "g_pre_mix", "g_post_mix", "g_pre_ffn", "g_post_ffn", "pool_scale", "conv_b", "b_pool", "w_pool", "conv_w"):
        res[name] = tuple(small[name])

    loss = loss_sum[0, 0]
    order = ["w_ada", "b_ada", "g_pre_mix", "g_post_mix", "g_pre_ffn", "g_post_ffn", "w_in", "w_pool", "b_pool", "pool_scale",
             "w_out", "w_up", "conv_w", "conv_b", "w_down"]
    outs = [loss, grad_x[None]]
    for k in range(4):
        outs += [res[n][k] for n in order]
    return tuple(outs)
```

```python
import math

import jax
import jax.numpy as jnp
from jax import lax
from jax.experimental import pallas as pl
from jax.experimental.pallas import tpu as pltpu

F32 = jnp.float32
BF16 = jnp.bfloat16
MESH = pl.DeviceIdType.MESH

D_MODEL = 1024
HEAD_DIM = 64
POOL_W = 256
GROUP_W = 256
DILATIONS = (1, 4, 16)
ATT_BLOCK = 128
IN_W = 2560
D_FF = 2816
HALF_FF = 1408
ROT_DIM = 16
ROPE_THETA = 500000.0
NORM_EPS = 1e-6
N_CHIPS = 4
N_DEV = 8
NEG = -1e30

ADAM_LR = 0.001
ADAM_B1 = 0.9
ADAM_B2 = 0.999
ADAM_EPS = 1e-08
ADAM_WD = 0.01
ADAM_STEP = 10

VMEM_LIMIT = 56 * 1024 * 1024

NT = (((1,), (1,)), ((), ()))
TN = (((0,), (0,)), ((), ()))


def _params(n_grid=0, **kw):
    sem = ("arbitrary",) * n_grid if n_grid else None
    return pltpu.CompilerParams(dimension_semantics=sem, vmem_limit_bytes=VMEM_LIMIT, **kw)


def _full(shape):
    nd = len(shape)
    return pl.BlockSpec(tuple(shape), lambda *_: (0,) * nd, pipeline_mode=pl.Buffered(1))


def _rows(tm, ncol):
    return pl.BlockSpec((tm, ncol), lambda i: (i, 0))


def _zero(*refs):
    for ref in refs:
        ref[...] = jnp.zeros_like(ref)


def _colsum(v):
    return jnp.sum(v, axis=0, keepdims=True)


def _rope128(t, cs, sa, sb, sign):
    return t * cs + sign * (pltpu.roll(t, 8, 1) * sa + pltpu.roll(t, 120, 1) * sb)


FF_CHUNKS = tuple((ch, off, w) for ch in range(2) for off, w in ((0, 512), (512, 512), (1024, 384)))
GELU_C0 = math.sqrt(2.0 / math.pi)
GELU_C1 = GELU_C0 * 0.044715


def _gelu(z):
    z2 = z * z
    t = jnp.tanh(z * (GELU_C0 + GELU_C1 * z2))
    u = 0.5 * t + 0.5
    return z * u, u, t, z2


def _gelu_grad(z, u, t, z2):
    return u + (z * (GELU_C0 + (3.0 * GELU_C1) * z2)) * (0.5 - 0.5 * (t * t))


def _conv_taps(gate, halo, first):
    row = lax.broadcasted_iota(jnp.int32, gate.shape, 0)
    halo = jnp.where(first, 0.0, halo)
    nh = halo.shape[0]
    p1 = halo[nh - 1:nh, :]
    p2 = halo[nh - 2:nh - 1, :]
    g1 = jnp.where(row == 0, p1, pltpu.roll(gate, 1, 0))
    g2 = jnp.where(row == 0, p2, jnp.where(row == 1, p1, pltpu.roll(gate, 2, 0)))
    return g1, g2


def inproj_fwd(x, g, mod6, w_in_g, tc, tsa, tsb, tm=512):
    S = x.shape[0]

    def body(x_ref, g_ref, mod_ref, w_ref, tc_ref, tsa_ref, tsb_ref, h_ref, u_ref, q1_ref, q4_ref, q16_ref, scr):
        qkv_refs = (q1_ref, q4_ref, q16_ref)
        xv = x_ref[...]
        rstd = lax.rsqrt(jnp.mean(xv * xv, axis=-1, keepdims=True) + NORM_EPS)
        h = ((xv * rstd) * g_ref[...]) * (1.0 + mod_ref[1:2, :]) + mod_ref[0:1, :]
        hb = h.astype(BF16)
        h_ref[...] = hb
        cs, sa, sb = tc_ref[...], tsa_ref[...], tsb_ref[...]
        for j in range(N_CHIPS):
            res = jnp.dot(hb, w_ref[j], preferred_element_type=F32)
            for t in range(5):
                sp = 5 * j + t
                piece, half = sp // 2, sp % 2
                blk = res[:, t * 128:(t + 1) * 128]
                lanes = slice(half * 128, (half + 1) * 128)
                if piece == 0:
                    u_ref[:, lanes] = blk
                else:
                    kind, gi = (piece - 1) // 3, (piece - 1) % 3
                    if kind == 0:
                        blk = _rope128(blk, cs, sa, sb, 1.0) * (HEAD_DIM ** -0.5)
                    elif kind == 1:
                        blk = _rope128(blk, cs, sa, sb, 1.0)
                    d = DILATIONS[gi]
                    if d == 1:
                        q1_ref[kind, 0, :, lanes] = blk.astype(BF16)
                    else:
                        scr[...] = blk
                        for r in range(d):
                            qkv_refs[gi][kind, r, :, lanes] = scr[pl.ds(r, tm // d, stride=d), :].astype(BF16)

    cls = lambda d: pl.BlockSpec((3, d, tm // d, GROUP_W), lambda i: (0, 0, i, 0))
    return pl.pallas_call(
        body,
        name="inproj_fwd",
        grid=(S // tm,),
        in_specs=[_rows(tm, D_MODEL), _full((1, D_MODEL)), _full((6, D_MODEL)), _full(w_in_g.shape),
                  _rows(tm, 128), _rows(tm, 128), _rows(tm, 128)],
        out_specs=[_rows(tm, D_MODEL), _rows(tm, POOL_W)] + [cls(d) for d in DILATIONS],
        out_shape=[jax.ShapeDtypeStruct((S, D_MODEL), BF16), jax.ShapeDtypeStruct((S, POOL_W), F32)]
        + [jax.ShapeDtypeStruct((3, d, S // d, GROUP_W), BF16) for d in DILATIONS],
        scratch_shapes=[pltpu.VMEM((tm, 128), F32)],
        compiler_params=_params(1),
    )(x, g, mod6, w_in_g, tc, tsa, tsb)


def _attn_masks():
    row = lax.broadcasted_iota(jnp.int32, (2 * ATT_BLOCK, 2 * ATT_BLOCK), 0) % ATT_BLOCK
    col = lax.broadcasted_iota(jnp.int32, (2 * ATT_BLOCK, 2 * ATT_BLOCK), 1)
    band = (col >= row) & (col <= row + ATT_BLOCK)
    lane = lax.broadcasted_iota(jnp.int32, (ATT_BLOCK, 128), 1)
    return band, col, lane < HEAD_DIM


def _classes_per_step(d, nb):
    return min(d, max(1, 8 // nb))


def _stack_heads(t, lo):
    z = jnp.zeros_like(t)
    return jnp.concatenate([jnp.where(lo, t, z), jnp.where(lo, z, t)], axis=0)


def _unstack_heads(t2, lo):
    return jnp.where(lo, t2[:ATT_BLOCK], t2[ATT_BLOCK:])


def attn_fwd(qkv, d):
    L = qkv.shape[2]
    nb = L // ATT_BLOCK
    cpb = _classes_per_step(d, nb)

    def body(q_ref, k_ref, v_ref, o_ref, l_ref, kpad, vpad):
        for cls in range(cpb):
            kpad[cls, 0:ATT_BLOCK, :] = jnp.zeros((ATT_BLOCK, GROUP_W), BF16)
            vpad[cls, 0:ATT_BLOCK, :] = jnp.zeros((ATT_BLOCK, GROUP_W), BF16)
            kpad[cls, ATT_BLOCK:, :] = k_ref[cls]
            vpad[cls, ATT_BLOCK:, :] = v_ref[cls]
        band, col, lo = _attn_masks()

        def step(t, carry):
            cls, n = t // nb, t % nb
            r0 = pl.multiple_of(n * ATT_BLOCK, ATT_BLOCK)
            valid = band & ((col >= ATT_BLOCK) | (n > 0))
            qb = q_ref[cls, pl.ds(r0, ATT_BLOCK), :]
            kb = kpad[cls, pl.ds(r0, 2 * ATT_BLOCK), :]
            vb = vpad[cls, pl.ds(r0, 2 * ATT_BLOCK), :]
            for pair in range(2):
                lanes = slice(pair * 128, (pair + 1) * 128)
                qp, kp, vp = qb[:, lanes], kb[:, lanes], vb[:, lanes]
                s = lax.dot_general(_stack_heads(qp, lo), kp, NT, preferred_element_type=F32)
                s = jnp.where(valid, s, NEG)
                m = jnp.max(s, axis=1, keepdims=True)
                p = jnp.exp(s - m)
                den = jnp.sum(p, axis=1, keepdims=True)
                pv = jnp.dot(p.astype(BF16), vp, preferred_element_type=F32)
                o_ref[cls, pl.ds(r0, ATT_BLOCK), lanes] = _unstack_heads(pv / den, lo).astype(BF16)
                l_ref[cls, pl.ds(r0, ATT_BLOCK), lanes] = _unstack_heads(jnp.broadcast_to(m + jnp.log(den), pv.shape), lo)
            return carry

        lax.fori_loop(0, cpb * nb, step, 0, unroll=8)

    spec = lambda kind: pl.BlockSpec((None, cpb, L, GROUP_W), lambda r: (kind, r, 0, 0))
    return pl.pallas_call(
        body,
        name=f"attn_fwd_d{d}",
        grid=(d // cpb,),
        in_specs=[spec(0), spec(1), spec(2)],
        out_specs=[pl.BlockSpec((cpb, L, GROUP_W), lambda r: (r, 0, 0))] * 2,
        out_shape=[jax.ShapeDtypeStruct((d, L, GROUP_W), BF16), jax.ShapeDtypeStruct((d, L, GROUP_W), F32)],
        scratch_shapes=[pltpu.VMEM((cpb, L + ATT_BLOCK, GROUP_W), BF16)] * 2,
        compiler_params=_params(1),
    )(qkv, qkv, qkv)


def _pool_lane_windows(shape):
    lane = lax.broadcasted_iota(jnp.int32, shape, 1)
    return lane, jnp.where(lane < 64, 2, jnp.where(lane < 128, 4, jnp.where(lane < 192, 8, 16)))


def pool_fwd(u, wbd, b, scale):
    S = u.shape[0]

    def body(u_ref, w_ref, b_ref, s_ref, mixed_ref, out_ref):
        uv = u_ref[...]
        row = lax.broadcasted_iota(jnp.int32, uv.shape, 0)
        lane, win = _pool_lane_windows(uv.shape)

        def shift(a, k):
            return jnp.where(row >= k, pltpu.roll(a, k, 0), 0.0)

        s2 = uv + shift(uv, 1)
        s4 = s2 + shift(s2, 2)
        s8 = s4 + shift(s4, 4)
        s16 = s8 + shift(s8, 8)
        tsum = jnp.where(lane < 64, s2, jnp.where(lane < 128, s4, jnp.where(lane < 192, s8, s16)))
        cnt = jnp.minimum(row + 1, win).astype(F32)
        mb = (tsum / cnt - uv).astype(BF16)
        mixed_ref[...] = mb
        y = jnp.dot(mb, w_ref[...], preferred_element_type=F32) + b_ref[...]
        out_ref[...] = (y * s_ref[...]).astype(BF16)

    vm = pl.BlockSpec(memory_space=pltpu.VMEM)
    return pl.pallas_call(
        body,
        name="pool_fwd",
        in_specs=[vm] * 4,
        out_specs=[vm] * 2,
        out_shape=[jax.ShapeDtypeStruct((S, POOL_W), BF16)] * 2,
        compiler_params=_params(),
    )(u, wbd, b, scale)


def outproj_fwd(o_l, pool, x, w_out_g, g_post, g_pre, mod6, tm=512):
    S = x.shape[0]

    def body(o0, o1, o2, l0, l1, l2, pool_ref, x_ref, w_ref, gpost_ref, gpre_ref, mod_ref,
             cat_ref, lse_ref, lse4_ref, lse16_ref, y1_ref, x1_ref, h2_ref, so4, sl4, so16, sl16):
        for d, src, dst in ((4, o1, so4), (4, l1, sl4), (16, o2, so16), (16, l2, sl16)):
            for r in range(d):
                for h in range(2):
                    dst[h, pl.ds(r, tm // d, stride=d), :] = src[r, :, h * 128:(h + 1) * 128].astype(F32)
        nat = lambda ref: jnp.concatenate([ref[0], ref[1]], axis=1)
        a, b, c = l0[0], nat(sl4), nat(sl16)
        m = jnp.maximum(jnp.maximum(a, b), c)
        e0, e1, e2 = jnp.exp(a - m), jnp.exp(b - m), jnp.exp(c - m)
        z = e0 + e1 + e2
        lse = m + jnp.log(z)
        lse_ref[...] = lse
        for h in range(2):
            sl4[h] = lse[:, h * 128:(h + 1) * 128]
        for d, dst in ((4, lse4_ref), (16, lse16_ref)):
            for r in range(d):
                for h in range(2):
                    dst[r, :, h * 128:(h + 1) * 128] = sl4[h, pl.ds(r, tm // d, stride=d), :]
        attn = (e0 * o0[0].astype(F32) + e1 * nat(so4) + e2 * nat(so16)) / z
        cat = jnp.concatenate([pool_ref[...], attn.astype(BF16)], axis=1)
        cat_ref[...] = cat
        y1 = jnp.concatenate([jnp.dot(cat, w_ref[j], preferred_element_type=F32) for j in range(N_CHIPS)], axis=1)
        y1_ref[...] = y1.astype(BF16)
        rstd = lax.rsqrt(jnp.mean(y1 * y1, axis=-1, keepdims=True) + NORM_EPS)
        x1 = x_ref[...] + mod_ref[2:3, :] * ((y1 * rstd) * gpost_ref[...])
        x1_ref[...] = x1
        rstd2 = lax.rsqrt(jnp.mean(x1 * x1, axis=-1, keepdims=True) + NORM_EPS)
        h2 = ((x1 * rstd2) * gpre_ref[...]) * (1.0 + mod_ref[4:5, :]) + mod_ref[3:4, :]
        h2_ref[...] = h2.astype(BF16)

    t256 = _rows(tm, GROUP_W)
    cls = lambda d: pl.BlockSpec((d, tm // d, GROUP_W), lambda i: (0, i, 0))
    cls_shape = lambda d: jax.ShapeDtypeStruct((d, S // d, GROUP_W), F32)
    return pl.pallas_call(
        body,
        name="outproj_fwd",
        grid=(S // tm,),
        in_specs=[cls(d) for d in DILATIONS] * 2 + [t256, _rows(tm, D_MODEL), _full(w_out_g.shape), _full((1, D_MODEL)),
                                                    _full((1, D_MODEL)), _full((6, D_MODEL))],
        out_specs=[_rows(tm, 512), t256, cls(4), cls(16), _rows(tm, D_MODEL), _rows(tm, D_MODEL), _rows(tm, D_MODEL)],
        out_shape=[jax.ShapeDtypeStruct((S, 512), BF16), jax.ShapeDtypeStruct((S, GROUP_W), F32), cls_shape(4), cls_shape(16),
                   jax.ShapeDtypeStruct((S, D_MODEL), BF16), jax.ShapeDtypeStruct((S, D_MODEL), F32),
                   jax.ShapeDtypeStruct((S, D_MODEL), BF16)],
        scratch_shapes=[pltpu.VMEM((2, tm, 128), F32)] * 4,
        compiler_params=_params(1),
    )(*o_l, pool, x, w_out_g, g_post, g_pre, mod6)


def _halo_prev(tm, ncol):
    return pl.BlockSpec((16, ncol), lambda i: (jnp.maximum(i * (tm // 16) - 1, 0), 0))


def ffn_fwd(h2, w_up_g, conv_w, conv_b, w_down, x1, target, g_post, mod6, tm=512):
    S = x1.shape[0]

    def body(h_ref, wu_ref, cw_ref, cb_ref, wd_ref, x1_ref, tgt_ref, g_ref, mod_ref,
             gate_ref, val_ref, dy2_ref, dout_ref, loss_ref, dgt_ref, dg_ref, carry):
        first = pl.program_id(0) == 0

        @pl.when(first)
        def _():
            _zero(carry, loss_ref, dgt_ref, dg_ref)

        hb = h_ref[...]
        y2 = jnp.zeros((tm, D_MODEL), F32)
        for ch in range(2):
            cols = slice(ch * HALF_FF, (ch + 1) * HALF_FF)
            gb = jnp.dot(hb, wu_ref[ch], preferred_element_type=F32).astype(BF16)
            vb = jnp.dot(hb, wu_ref[2 + ch], preferred_element_type=F32).astype(BF16)
            gate_ref[:, cols] = gb
            val_ref[:, cols] = vb
            gt = gb.astype(F32)
            g1, g2 = _conv_taps(gt, carry[:, cols], first)
            carry[:, cols] = gt[tm - 8:, :]
            gc = g2 * cw_ref[0:1, cols] + g1 * cw_ref[1:2, cols] + gt * cw_ref[2:3, cols] + cb_ref[:, cols]
            ab = _gelu(gc.astype(BF16))[0] * vb
            y2 = y2 + jnp.dot(ab, wd_ref[cols, :], preferred_element_type=F32)
        rstd = lax.rsqrt(jnp.mean(y2 * y2, axis=-1, keepdims=True) + NORM_EPS)
        y2n = y2 * rstd
        gv = g_ref[...]
        gtf = mod_ref[5:6, :]
        r2 = y2n * gv
        diff = (x1_ref[...] + gtf * r2) - tgt_ref[...]
        loss_ref[...] += jnp.zeros((1, 128), F32) + 0.5 * jnp.sum(diff * diff) * (1.0 / D_MODEL)
        dout = diff * (1.0 / D_MODEL)
        dout_ref[...] = dout
        dgt_ref[...] += _colsum(dout * r2)
        dr2 = dout * gtf
        dg_ref[...] += _colsum(dr2 * y2n)
        dyn = dr2 * gv
        dy2 = rstd * (dyn - y2n * jnp.mean(dyn * y2n, axis=-1, keepdims=True))
        dy2_ref[...] = dy2.astype(BF16)

    vec = _full((1, D_MODEL))
    return pl.pallas_call(
        body,
        name="ffn_fwd",
        grid=(S // tm,),
        in_specs=[_rows(tm, D_MODEL), _full(w_up_g.shape), _full((3, D_FF)), _full((1, D_FF)), _full((D_FF, D_MODEL)),
                  _rows(tm, D_MODEL), _rows(tm, D_MODEL), vec, _full((6, D_MODEL))],
        out_specs=[_rows(tm, D_FF), _rows(tm, D_FF), _rows(tm, D_MODEL), _rows(tm, D_MODEL), _full((1, 128)), vec, vec],
        out_shape=[jax.ShapeDtypeStruct((S, D_FF), BF16)] * 2 + [jax.ShapeDtypeStruct((S, D_MODEL), BF16),
                                                                 jax.ShapeDtypeStruct((S, D_MODEL), F32),
                                                                 jax.ShapeDtypeStruct((1, 128), F32),
                                                                 jax.ShapeDtypeStruct((1, D_MODEL), F32),
                                                                 jax.ShapeDtypeStruct((1, D_MODEL), F32)],
        scratch_shapes=[pltpu.VMEM((8, D_FF), F32)],
        compiler_params=_params(1),
    )(h2, w_up_g, conv_w, conv_b, w_down, x1, target, g_post, mod6)


def down_bwd(dy2, w_down, gate, val, conv_w, conv_b, h2, tm=512):
    S = dy2.shape[0]

    def body(dy_ref, w_ref, gate_ref, halo_ref, val_ref, cw_ref, cb_ref, h_ref,
             dgc_ref, dval_ref, dcw_ref, dcb_ref, dwd_ref, dwu_ref):
        first = pl.program_id(1) == 0

        @pl.when(first)
        def _():
            dcw_ref[...] = jnp.zeros_like(dcw_ref)
            dcb_ref[...] = jnp.zeros_like(dcb_ref)
            dwd_ref[...] = jnp.zeros_like(dwd_ref)
            dwu_ref[...] = jnp.zeros_like(dwu_ref)

        dyb = dy_ref[...]
        hb = h_ref[...]
        pieces = [(off, w) for ch, off, w in FF_CHUNKS if ch == 0]

        def col(i):
            return slice(pieces[i][0], pieces[i][0] + pieces[i][1])

        def mm_da(i):
            return lax.dot_general(dyb, w_ref[col(i), :], NT, preferred_element_type=F32)

        def elementwise(i, da):
            cols = col(i)
            gt = gate_ref[:, cols].astype(F32)
            g1, g2 = _conv_taps(gt, halo_ref[:, cols].astype(F32), first)
            gc = g2 * cw_ref[0:1, cols] + g1 * cw_ref[1:2, cols] + gt * cw_ref[2:3, cols] + cb_ref[:, cols]
            zb, dab, vb = gc.astype(BF16), da.astype(BF16), val_ref[:, cols]
            ge, u, th, z2 = _gelu(zb)
            dgb = dab * vb * _gelu_grad(zb, u, th, z2)
            dgc_ref[:, cols] = dgb
            dgc = dgb.astype(F32)
            dvb = dab * ge
            dval_ref[:, cols] = dvb
            dcb_ref[:, cols] += _colsum(dgc)
            dcw_ref[0:1, cols] += _colsum(dgc * g2)
            dcw_ref[1:2, cols] += _colsum(dgc * g1)
            dcw_ref[2:3, cols] += _colsum(dgc * gt)
            return dvb, ge * vb

        def mm_dw(i, dvb_ab):
            dvb, ab = dvb_ab
            dwd_ref[col(i), :] += lax.dot_general(ab, dyb, TN, preferred_element_type=F32)
            dwu_ref[:, col(i)] += lax.dot_general(hb, dvb, TN, preferred_element_type=F32)

        n = len(pieces)
        da = mm_da(0)
        prev = None
        for i in range(n):
            nxt = mm_da(i + 1) if i + 1 < n else None
            if prev is not None:
                mm_dw(i - 1, prev)
            prev = elementwise(i, da)
            da = nxt
        mm_dw(n - 1, prev)

    one = pl.Buffered(1)
    tok = pl.BlockSpec((tm, D_MODEL), lambda c, i: (i, 0))
    ff = pl.BlockSpec((tm, HALF_FF), lambda c, i: (i, c))
    halo = pl.BlockSpec((16, HALF_FF), lambda c, i: (jnp.maximum(i * (tm // 16) - 1, 0), c))
    per_half = lambda rows: pl.BlockSpec((rows, HALF_FF), lambda c, i: (0, c), pipeline_mode=one)
    return pl.pallas_call(
        body,
        name="down_bwd",
        grid=(2, S // tm),
        in_specs=[tok, pl.BlockSpec((HALF_FF, D_MODEL), lambda c, i: (c, 0), pipeline_mode=one), ff, halo, ff,
                  per_half(3), per_half(1), tok],
        out_specs=[ff, ff, per_half(3), per_half(1), pl.BlockSpec((HALF_FF, D_MODEL), lambda c, i: (c, 0), pipeline_mode=one),
                   pl.BlockSpec((None, D_MODEL, HALF_FF), lambda c, i: (2 + c, 0, 0), pipeline_mode=one)],
        out_shape=[jax.ShapeDtypeStruct((S, D_FF), BF16), jax.ShapeDtypeStruct((S, D_FF), BF16),
                   jax.ShapeDtypeStruct((3, D_FF), F32), jax.ShapeDtypeStruct((1, D_FF), F32),
                   jax.ShapeDtypeStruct((D_FF, D_MODEL), F32), jax.ShapeDtypeStruct((N_CHIPS, D_MODEL, HALF_FF), F32)],
        compiler_params=_params(2),
    )(dy2, w_down, gate, gate, val, conv_w, conv_b, h2)


def up_bwd(dgc, dval, conv_w, w_up_g, x1, dout, y1, g_pre, g_post, mod6, h2, dw_up, tm=256):
    S = x1.shape[0]
    last_blk = S // 16 - 1

    def body(dgc_ref, nxt_ref, dval_ref, cw_ref, w_ref, x1_ref, dout_ref, y1_ref, gpre_ref, gpost_ref, mod_ref, h_ref, dwin_ref,
             dx1_ref, dy1_ref, dsh_ref, dsc_ref, dgpre_ref, dgt_ref, dgpost_ref, dwu_ref):
        last = pl.program_id(0) == pl.num_programs(0) - 1

        @pl.when(pl.program_id(0) == 0)
        def _():
            _zero(dwu_ref, dsh_ref, dsc_ref, dgpre_ref, dgt_ref, dgpost_ref)

        hb = h_ref[...]
        dh = jnp.zeros((tm, D_MODEL), F32)
        for ch in range(2):
            cols = slice(ch * HALF_FF, (ch + 1) * HALF_FF)
            dg = dgc_ref[:, cols].astype(F32)
            nx = jnp.where(last, 0.0, nxt_ref[:, cols].astype(F32))
            row = lax.broadcasted_iota(jnp.int32, dg.shape, 0)
            n0, n1 = nx[0:1, :], nx[1:2, :]
            u1 = jnp.where(row == tm - 1, n0, pltpu.roll(dg, tm - 1, 0))
            u2 = jnp.where(row == tm - 1, n1, jnp.where(row == tm - 2, n0, pltpu.roll(dg, tm - 2, 0)))
            dgate = (dg * cw_ref[2:3, cols] + u1 * cw_ref[1:2, cols] + u2 * cw_ref[0:1, cols]).astype(BF16)
            dwu_ref[ch] += lax.dot_general(hb, dgate, TN, preferred_element_type=F32)
            dh = dh + lax.dot_general(dgate, w_ref[ch], NT, preferred_element_type=F32)
            dh = dh + lax.dot_general(dval_ref[:, cols], w_ref[2 + ch], NT, preferred_element_type=F32)
        x1 = x1_ref[...]
        rstd = lax.rsqrt(jnp.mean(x1 * x1, axis=-1, keepdims=True) + NORM_EPS)
        n2 = x1 * rstd
        gpre = gpre_ref[...]
        one_sc = 1.0 + mod_ref[4:5, :]
        dsh_ref[...] += _colsum(dh)
        dsc_ref[...] += _colsum(dh * (n2 * gpre))
        dgpre_ref[...] += _colsum(dh * one_sc * n2)
        dn = dh * (gpre * one_sc)
        dx1 = dout_ref[...] + rstd * (dn - n2 * jnp.mean(dn * n2, axis=-1, keepdims=True))
        dx1_ref[...] = dx1
        y1 = y1_ref[...].astype(F32)
        rstd1 = lax.rsqrt(jnp.mean(y1 * y1, axis=-1, keepdims=True) + NORM_EPS)
        y1n = y1 * rstd1
        gpost = gpost_ref[...]
        gtm = mod_ref[2:3, :]
        dgt_ref[...] += _colsum(dx1 * (y1n * gpost))
        dr1 = dx1 * gtm
        dgpost_ref[...] += _colsum(dr1 * y1n)
        dyn = dr1 * gpost
        dy1 = rstd1 * (dyn - y1n * jnp.mean(dyn * y1n, axis=-1, keepdims=True))
        dy1_ref[...] = dy1.astype(BF16)

    vec = _full((1, D_MODEL))
    nxt = pl.BlockSpec((16, D_FF), lambda i: (jnp.minimum((i + 1) * (tm // 16), last_blk), 0))
    return pl.pallas_call(
        body,
        name="up_bwd",
        grid=(S // tm,),
        in_specs=[_rows(tm, D_FF), nxt, _rows(tm, D_FF), _full((3, D_FF)), _full(w_up_g.shape), _rows(tm, D_MODEL),
                  _rows(tm, D_MODEL), _rows(tm, D_MODEL), vec, vec, _full((6, D_MODEL)), _rows(tm, D_MODEL),
                  pl.BlockSpec(memory_space=pl.ANY)],
        out_specs=[_rows(tm, D_MODEL), _rows(tm, D_MODEL), vec, vec, vec, vec, vec,
                   pl.BlockSpec((2, D_MODEL, HALF_FF), lambda i: (0, 0, 0), pipeline_mode=pl.Buffered(1))],
        out_shape=[jax.ShapeDtypeStruct((S, D_MODEL), F32), jax.ShapeDtypeStruct((S, D_MODEL), BF16)]
        + [jax.ShapeDtypeStruct((1, D_MODEL), F32)] * 5 + [jax.ShapeDtypeStruct(dw_up.shape, F32)],
        input_output_aliases={12: 7},
        compiler_params=_params(1),
    )(dgc, dgc, dval, conv_w, w_up_g, x1, dout, y1, g_pre, g_post, mod6, h2, dw_up)


def outproj_bwd(dy1, w_out_g, cat, tm=512):
    S = dy1.shape[0]

    def body(dy_ref, w_ref, cat_ref, dpool_ref, dattn_ref, da4_ref, da16_ref, delta_ref, dl4_ref, dl16_ref, dw_ref, scr):
        @pl.when(pl.program_id(0) == 0)
        def _():
            dw_ref[...] = jnp.zeros_like(dw_ref)

        catb = cat_ref[...]
        dcat = jnp.zeros((tm, 512), F32)
        for j in range(N_CHIPS):
            dyj = dy_ref[:, j * 256:(j + 1) * 256]
            dcat = dcat + lax.dot_general(dyj, w_ref[j], NT, preferred_element_type=F32)
            dw_ref[j] += lax.dot_general(catb, dyj, TN, preferred_element_type=F32)
        dpool_ref[...] = dcat[:, :POOL_W]
        dattn = dcat[:, POOL_W:]
        dattn_ref[...] = dattn.astype(BF16)
        for h in range(2):
            scr[h] = dattn[:, h * 128:(h + 1) * 128]
        for d, dst in ((4, da4_ref), (16, da16_ref)):
            for r in range(d):
                for h in range(2):
                    dst[r, :, h * 128:(h + 1) * 128] = scr[h, pl.ds(r, tm // d, stride=d), :].astype(BF16)
        prod = dattn * catb[:, POOL_W:].astype(F32)
        r = lax.broadcasted_iota(jnp.int32, (GROUP_W, GROUP_W), 0) // HEAD_DIM
        c = lax.broadcasted_iota(jnp.int32, (GROUP_W, GROUP_W), 1) // HEAD_DIM
        ones_bd = jnp.where(r == c, 1.0, 0.0).astype(BF16)
        hi = prod.astype(BF16)
        lo = (prod - hi.astype(F32)).astype(BF16)
        delta = jnp.dot(hi, ones_bd, preferred_element_type=F32) + jnp.dot(lo, ones_bd, preferred_element_type=F32)
        delta_ref[...] = delta
        for h in range(2):
            scr[h] = delta[:, h * 128:(h + 1) * 128]
        for d, dst in ((4, dl4_ref), (16, dl16_ref)):
            for r in range(d):
                for h in range(2):
                    dst[r, :, h * 128:(h + 1) * 128] = scr[h, pl.ds(r, tm // d, stride=d), :]

    cls = lambda d: pl.BlockSpec((d, tm // d, GROUP_W), lambda i: (0, i, 0))
    cls_shape = lambda d, dt: jax.ShapeDtypeStruct((d, S // d, GROUP_W), dt)
    return pl.pallas_call(
        body,
        name="outproj_bwd",
        grid=(S // tm,),
        in_specs=[_rows(tm, D_MODEL), _full(w_out_g.shape), _rows(tm, 512)],
        out_specs=[_rows(tm, POOL_W), _rows(tm, GROUP_W), cls(4), cls(16), _rows(tm, GROUP_W), cls(4), cls(16),
                   _full(w_out_g.shape)],
        out_shape=[jax.ShapeDtypeStruct((S, POOL_W), F32), jax.ShapeDtypeStruct((S, GROUP_W), BF16), cls_shape(4, BF16),
                   cls_shape(16, BF16), jax.ShapeDtypeStruct((S, GROUP_W), F32), cls_shape(4, F32), cls_shape(16, F32),
                   jax.ShapeDtypeStruct(w_out_g.shape, F32)],
        scratch_shapes=[pltpu.VMEM((2, tm, 128), F32)],
        compiler_params=_params(1),
    )(dy1, w_out_g, cat)


def attn_bwd(qkv, dattn, lse, delta, d):
    L = qkv.shape[2]
    nb = L // ATT_BLOCK
    cpb = _classes_per_step(d, nb)

    def body(q_ref, k_ref, v_ref, do_ref, l_ref, dl_ref, out_ref, kpad, vpad, dkpad, dvpad):
        for cls in range(cpb):
            kpad[cls, 0:ATT_BLOCK, :] = jnp.zeros((ATT_BLOCK, GROUP_W), BF16)
            vpad[cls, 0:ATT_BLOCK, :] = jnp.zeros((ATT_BLOCK, GROUP_W), BF16)
            kpad[cls, ATT_BLOCK:, :] = k_ref[cls]
            vpad[cls, ATT_BLOCK:, :] = v_ref[cls]
        dkpad[...] = jnp.zeros_like(dkpad)
        dvpad[...] = jnp.zeros_like(dvpad)
        band, col, lo = _attn_masks()

        def step(t, carry):
            cls, n = t // nb, t % nb
            r0 = pl.multiple_of(n * ATT_BLOCK, ATT_BLOCK)
            valid = band & ((col >= ATT_BLOCK) | (n > 0))
            qb = q_ref[cls, pl.ds(r0, ATT_BLOCK), :]
            dob = do_ref[cls, pl.ds(r0, ATT_BLOCK), :]
            lb = l_ref[cls, pl.ds(r0, ATT_BLOCK), :]
            dlb = dl_ref[cls, pl.ds(r0, ATT_BLOCK), :]
            kb = kpad[cls, pl.ds(r0, 2 * ATT_BLOCK), :]
            vb = vpad[cls, pl.ds(r0, 2 * ATT_BLOCK), :]
            for pair in range(2):
                lanes = slice(pair * 128, (pair + 1) * 128)
                qp, dop, kp, vp = qb[:, lanes], dob[:, lanes], kb[:, lanes], vb[:, lanes]
                c0, c1 = pair * 128, pair * 128 + HEAD_DIM
                q2, do2 = _stack_heads(qp, lo), _stack_heads(dop, lo)
                lse2 = jnp.concatenate([lb[:, c0:c0 + 1], lb[:, c1:c1 + 1]], axis=0)
                dl2 = jnp.concatenate([dlb[:, c0:c0 + 1], dlb[:, c1:c1 + 1]], axis=0)
                s = lax.dot_general(q2, kp, NT, preferred_element_type=F32)
                s = jnp.where(valid, s, NEG)
                p = jnp.exp(s - lse2)
                dp = lax.dot_general(do2, vp, NT, preferred_element_type=F32)
                ds = (p * (dp - dl2)).astype(BF16)
                dq2 = jnp.dot(ds, kp, preferred_element_type=F32)
                out_ref[0, cls, pl.ds(r0, ATT_BLOCK), lanes] = _unstack_heads(dq2, lo)
                dkpad[cls, pl.ds(r0, 2 * ATT_BLOCK), lanes] += lax.dot_general(ds, q2, TN, preferred_element_type=F32)
                dvpad[cls, pl.ds(r0, 2 * ATT_BLOCK), lanes] += lax.dot_general(p.astype(BF16), do2, TN, preferred_element_type=F32)
            return carry

        lax.fori_loop(0, cpb * nb, step, 0, unroll=8)
        for cls in range(cpb):
            out_ref[1, cls] = dkpad[cls, ATT_BLOCK:, :]
            out_ref[2, cls] = dvpad[cls, ATT_BLOCK:, :]

    spec = lambda kind: pl.BlockSpec((None, cpb, L, GROUP_W), lambda r: (kind, r, 0, 0))
    per_cls = pl.BlockSpec((cpb, L, GROUP_W), lambda r: (r, 0, 0))
    return pl.pallas_call(
        body,
        name=f"attn_bwd_d{d}",
        grid=(d // cpb,),
        in_specs=[spec(0), spec(1), spec(2), per_cls, per_cls, per_cls],
        out_specs=pl.BlockSpec((3, cpb, L, GROUP_W), lambda r: (0, r, 0, 0)),
        out_shape=jax.ShapeDtypeStruct((3, d, L, GROUP_W), F32),
        scratch_shapes=[pltpu.VMEM((cpb, L + ATT_BLOCK, GROUP_W), BF16)] * 2 + [pltpu.VMEM((cpb, L + ATT_BLOCK, GROUP_W), F32)] * 2,
        compiler_params=_params(1),
    )(qkv, qkv, qkv, dattn, lse, delta)


def pool_bwd(dpool, mixed, wbd, b, scale):
    S = dpool.shape[0]

    def body(dp_ref, mx_ref, w_ref, b_ref, s_ref, du_ref, dw_ref, db_ref, ds_ref):
        dp = dp_ref[...]
        mb = mx_ref[...]
        wv = w_ref[...]
        ypre = jnp.dot(mb, wv, preferred_element_type=F32) + b_ref[...]
        ds_ref[...] = _colsum(dp * ypre)
        dpre = dp * s_ref[...]
        db_ref[...] = _colsum(dpre)
        dpb = dpre.astype(BF16)
        dw_ref[...] = lax.dot_general(mb, dpb, TN, preferred_element_type=F32)
        dmix = lax.dot_general(dpb, wv, NT, preferred_element_type=F32)
        row = lax.broadcasted_iota(jnp.int32, dmix.shape, 0)
        lane, win = _pool_lane_windows(dmix.shape)
        e = dmix / jnp.minimum(row + 1, win).astype(F32)

        def shift(a, k):
            return jnp.where(row < S - k, pltpu.roll(a, S - k, 0), 0.0)

        f2 = e + shift(e, 1)
        f4 = f2 + shift(f2, 2)
        f8 = f4 + shift(f4, 4)
        f16 = f8 + shift(f8, 8)
        du_ref[...] = jnp.where(lane < 64, f2, jnp.where(lane < 128, f4, jnp.where(lane < 192, f8, f16))) - dmix

    vm = pl.BlockSpec(memory_space=pltpu.VMEM)
    return pl.pallas_call(
        body,
        name="pool_bwd",
        in_specs=[vm] * 5,
        out_specs=[vm] * 4,
        out_shape=[jax.ShapeDtypeStruct((S, POOL_W), F32), jax.ShapeDtypeStruct((POOL_W, POOL_W), F32),
                   jax.ShapeDtypeStruct((1, POOL_W), F32), jax.ShapeDtypeStruct((1, POOL_W), F32)],
        compiler_params=_params(),
    )(dpool, mixed, wbd, b, scale)


def inproj_bwd(dqkv, du, x, dx1, w_in_g, g, mod6, tc, tsa, tsb, h1, tm=512):
    S = x.shape[0]

    def body(d0, d1, d2, du_ref, x_ref, dx1_ref, w_ref, g_ref, mod_ref, tc_ref, tsa_ref, tsb_ref, h_ref,
             gx_ref, dsh_ref, dsc_ref, dg_ref, dw_ref, s4, s16, dp_ref):
        @pl.when(pl.program_id(0) == 0)
        def _():
            _zero(dw_ref, dsh_ref, dsc_ref, dg_ref)

        cs, sa, sb = tc_ref[...], tsa_ref[...], tsb_ref[...]
        for d, src, dst in ((4, d1, s4), (16, d2, s16)):
            for kind in range(3):
                for r in range(d):
                    for h in range(2):
                        dst[kind, h, pl.ds(r, tm // d, stride=d), :] = src[kind, r, :, h * 128:(h + 1) * 128]
        for sp in range(20):
            piece, half = sp // 2, sp % 2
            lanes = slice(half * 128, (half + 1) * 128)
            if piece == 0:
                blk = du_ref[:, lanes]
            else:
                kind, gi = (piece - 1) // 3, (piece - 1) % 3
                blk = d0[kind, 0, :, lanes] if gi == 0 else (s4, s16)[gi - 1][kind, half]
                if kind == 0:
                    blk = _rope128(blk, cs, sa, sb, -1.0) * (HEAD_DIM ** -0.5)
                elif kind == 1:
                    blk = _rope128(blk, cs, sa, sb, -1.0)
            dp_ref[:, sp * 128:(sp + 1) * 128] = blk.astype(BF16)
        dh = jnp.zeros((tm, D_MODEL), F32)
        hbt = h_ref[...].T
        for j in range(N_CHIPS):
            dpj = dp_ref[:, j * 640:(j + 1) * 640]
            dh = dh + lax.dot_general(dpj, w_ref[j], NT, preferred_element_type=F32)
            dw_ref[j] += jnp.dot(hbt, dpj, preferred_element_type=F32)
        xv = x_ref[...]
        rstd = lax.rsqrt(jnp.mean(xv * xv, axis=-1, keepdims=True) + NORM_EPS)
        n1 = xv * rstd
        gv = g_ref[...]
        one_sc = 1.0 + mod_ref[1:2, :]
        dsh_ref[...] += _colsum(dh)
        dsc_ref[...] += _colsum(dh * (n1 * gv))
        dg_ref[...] += _colsum(dh * one_sc * n1)
        dn = dh * (gv * one_sc)
        gx_ref[...] = dx1_ref[...] + rstd * (dn - n1 * jnp.mean(dn * n1, axis=-1, keepdims=True))

    vec = _full((1, D_MODEL))
    dspec = lambda d: pl.BlockSpec((3, d, tm // d, GROUP_W), lambda i: (0, 0, i, 0))
    return pl.pallas_call(
        body,
        name="inproj_bwd",
        grid=(S // tm,),
        in_specs=[dspec(d) for d in DILATIONS] + [_rows(tm, POOL_W), _rows(tm, D_MODEL), _rows(tm, D_MODEL), _full(w_in_g.shape),
                                                  vec, _full((6, D_MODEL)), _rows(tm, 128), _rows(tm, 128), _rows(tm, 128),
                                                  _rows(tm, D_MODEL)],
        out_specs=[_rows(tm, D_MODEL), vec, vec, vec, _full(w_in_g.shape)],
        out_shape=[jax.ShapeDtypeStruct((S, D_MODEL), F32)] + [jax.ShapeDtypeStruct((1, D_MODEL), F32)] * 3
        + [jax.ShapeDtypeStruct(w_in_g.shape, F32)],
        scratch_shapes=[pltpu.VMEM((3, 2, tm, 128), F32)] * 2 + [pltpu.VMEM((tm, IN_W), BF16)],
        compiler_params=_params(1),
    )(*dqkv, du, x, dx1, w_in_g, g, mod6, tc, tsa, tsb, h1)


def _adamw(w, g, m, v):
    m = ADAM_B1 * m + (1.0 - ADAM_B1) * g
    v = ADAM_B2 * v + (1.0 - ADAM_B2) * (g * g)
    m_hat = m / (1.0 - ADAM_B1 ** ADAM_STEP)
    v_hat = v / (1.0 - ADAM_B2 ** ADAM_STEP)
    delta = -ADAM_LR * (m_hat / (jnp.sqrt(v_hat) + ADAM_EPS) + ADAM_WD * w)
    return delta, m, v


def adamw_rows(w, g, m, v, tr, name):
    R, C = w.shape

    def body(w_ref, g_ref, m_ref, v_ref, go_ref, d_ref, mo_ref, vo_ref):
        g = g_ref[...]
        go_ref[...] = g
        d_ref[...], mo_ref[...], vo_ref[...] = _adamw(w_ref[...], g, m_ref[...], v_ref[...])

    spec = pl.BlockSpec((tr, C), lambda i: (i, 0))
    return pl.pallas_call(
        body,
        name=name,
        grid=(R // tr,),
        in_specs=[spec] * 4,
        out_specs=[spec] * 4,
        out_shape=[jax.ShapeDtypeStruct((R, C), F32)] * 4,
        compiler_params=_params(1),
    )(w, g, m, v)


def adamw_ada(c_all_t, dmod_cols, w, m, v, tr=256):
    R, C = w.shape

    def body(ct_ref, dm_ref, w_ref, m_ref, v_ref, g_ref, d_ref, mo_ref, vo_ref):
        ct = ct_ref[...]
        act = ct * jax.nn.sigmoid(ct)
        dm = dm_ref[...]
        a_hi, d_hi = act.astype(BF16), dm.astype(BF16)
        a_lo, d_lo = (act - a_hi.astype(F32)).astype(BF16), (dm - d_hi.astype(F32)).astype(BF16)
        g = (jnp.dot(a_hi, d_hi, preferred_element_type=F32) + jnp.dot(a_lo, d_hi, preferred_element_type=F32)
             + jnp.dot(a_hi, d_lo, preferred_element_type=F32))
        g_ref[...] = g
        d_ref[...], mo_ref[...], vo_ref[...] = _adamw(w_ref[...], g, m_ref[...], v_ref[...])

    spec = pl.BlockSpec((tr, C), lambda i: (i, 0))
    return pl.pallas_call(
        body,
        name="adamw_ada",
        grid=(R // tr,),
        in_specs=[pl.BlockSpec((tr, N_DEV), lambda i: (i, 0)), _full((N_DEV, C)), spec, spec, spec],
        out_specs=[spec] * 4,
        out_shape=[jax.ShapeDtypeStruct((R, C), F32)] * 4,
        compiler_params=_params(1),
    )(c_all_t, dmod_cols, w, m, v)


def adamw_small(slab_a, slab_b, convw_g, wpool_g, params):
    names = ["b_ada", "g_pre_mix", "g_post_mix", "g_pre_ffn", "g_post_ffn", "b_pool", "pool_scale", "conv_b", "conv_w", "w_pool"]
    flat = []
    for n in names:
        flat += list(params[n])

    def body(a_ref, b_ref, cw_ref, wp_ref, *rest):
        ins, outs = rest[:30], rest[30:]

        def dev_sum(ref):
            t = ref[0]
            for dev in range(1, N_DEV):
                t = t + ref[dev]
            return t

        sa, sb_, scw, swp = dev_sum(a_ref), dev_sum(b_ref), dev_sum(cw_ref), dev_sum(wp_ref)
        grads = [
            jnp.concatenate([sa[k:k + 1, :] for k in range(6)], axis=1),
            sa[6:7, :], sa[7:8, :], sa[8:9, :], sa[9:10, :],
            sa[10:11, 0:256], sa[10:11, 256:512],
            sb_[3:4, :], scw, swp,
        ]
        for i, g in enumerate(grads):
            w_ref, m_ref, v_ref = ins[3 * i:3 * i + 3]
            if names[i] == "b_pool":
                parts = [((0, slice(grp, grp + 1)), g[:, grp * 64:(grp + 1) * 64]) for grp in range(4)]
            elif names[i] == "w_pool":
                parts = [((0, grp), g[grp * 64:(grp + 1) * 64, :]) for grp in range(4)]
            elif names[i] == "conv_w":
                parts = [((0,), g)]
            else:
                parts = [((Ellipsis,), g)]
            for at, gp in parts:
                d, mo, vo = _adamw(w_ref[at], gp, m_ref[at], v_ref[at])
                for k, val in enumerate((gp, d, mo, vo)):
                    outs[4 * i + k][at] = val
        outs[-1][...] = sa[10:11, 512:640]

    vm = pl.BlockSpec(memory_space=pltpu.VMEM)
    out_shape = []
    for n in names:
        out_shape += [jax.ShapeDtypeStruct(params[n][0].shape, F32)] * 4
    out_shape.append(jax.ShapeDtypeStruct((1, 128), F32))
    outs = pl.pallas_call(
        body,
        name="adamw_small",
        in_specs=[vm] * (4 + len(flat)),
        out_specs=[vm] * len(out_shape),
        out_shape=out_shape,
        compiler_params=_params(),
    )(slab_a, slab_b, convw_g, wpool_g, *flat)
    return {n: outs[4 * i:4 * i + 4] for i, n in enumerate(names)}, outs[-1]


def _place():
    return lax.axis_index("x"), lax.axis_index("y"), lax.axis_index("c")


def _other_chips(x, y):
    return [(1 - x, y), (x, 1 - y), (1 - x, 1 - y)]


def _chip_id(cx, cy):
    return 2 * cx + cy


HBM_SPEC = pl.BlockSpec(memory_space=pltpu.HBM)
SEM_SPEC = pl.BlockSpec(memory_space=pltpu.SEMAPHORE)
ANY_SPEC = pl.BlockSpec(memory_space=pl.ANY)
EFFECT = pltpu.SideEffectType.DATAFLOW_SIDE_EFFECTING


def _hbm(t):
    return pltpu.with_memory_space_constraint(t, pltpu.HBM)


def _hbm_shapes(ts):
    return [pltpu.HBM(t.shape, t.dtype) for t in ts]


def _half_rows(ref, lead, half, rh):
    return ref.at[lead, pl.ds(half * rh, rh), :]


def _flips():
    return [(fx, fy, fc) for fx in (0, 1) for fy in (0, 1) for fc in (0, 1)][1:]


def _flip(v, f):
    return v if f == 0 else 1 - v


def ada_mod(c3, w_ada, b_cols, conv_w):
    CB = w_ada.shape[1]

    def body(c_ref, w_hbm, b_ref, cw_ref, call_ref, mod_ref, cwall_ref, modall, send_sems, recv_sems, w_ref, w_sem):
        x, y, c = _place()
        me_dev = 4 * x + 2 * y + c
        me = _chip_id(x, y)
        w_load = pltpu.make_async_copy(w_hbm, w_ref, w_sem)
        w_load.start()
        call_ref[me_dev] = c_ref[0]
        cwall_ref[me] = cw_ref[...]
        sends = []
        for k, (cx, cy) in enumerate(_other_chips(x, y)):
            cp = pltpu.make_async_remote_copy(src_ref=cw_ref, dst_ref=cwall_ref.at[me], send_sem=send_sems.at[10 + k],
                                              recv_sem=recv_sems.at[10 + k], device_id=(cx, cy, c), device_id_type=MESH)
            cp.start()
            sends.append(cp)
        for k, (fx, fy, fc) in enumerate(_flips()):
            cp = pltpu.make_async_remote_copy(src_ref=c_ref.at[0], dst_ref=call_ref.at[me_dev], send_sem=send_sems.at[k],
                                              recv_sem=recv_sems.at[k],
                                              device_id=(_flip(x, fx), _flip(y, fy), _flip(c, fc)), device_id_type=MESH)
            cp.start()
            sends.append(cp)
        for k, (fx, fy, fc) in enumerate(_flips()):
            peer = 4 * _flip(x, fx) + 2 * _flip(y, fy) + _flip(c, fc)
            pltpu.make_async_remote_copy(src_ref=c_ref.at[0], dst_ref=call_ref.at[peer], send_sem=send_sems.at[k],
                                         recv_sem=recv_sems.at[k], device_id=(x, y, c), device_id_type=MESH).wait_recv()
        row = lax.broadcasted_iota(jnp.int32, (N_DEV, D_MODEL), 0)
        call = jnp.zeros((N_DEV, D_MODEL), F32)
        for dev in range(N_DEV):
            call = jnp.where(row == dev, call_ref[dev], call)
        act = call * jax.nn.sigmoid(call)
        w_load.wait()
        wv = w_ref[...]
        w_hi = wv.astype(BF16)
        w_lo = (wv - w_hi.astype(F32)).astype(BF16)
        a_hi = act.astype(BF16)
        a_lo = (act - a_hi.astype(F32)).astype(BF16)
        prod = (jnp.dot(a_hi, w_hi, preferred_element_type=F32) + jnp.dot(a_lo, w_hi, preferred_element_type=F32)
                + jnp.dot(a_hi, w_lo, preferred_element_type=F32))
        modall[me] = prod + b_ref[...]
        for k, (cx, cy) in enumerate(_other_chips(x, y)):
            cp = pltpu.make_async_remote_copy(src_ref=modall.at[me], dst_ref=modall.at[me], send_sem=send_sems.at[7 + k],
                                              recv_sem=recv_sems.at[7 + k], device_id=(cx, cy, c), device_id_type=MESH)
            cp.start()
            sends.append(cp)
        for k, (cx, cy) in enumerate(_other_chips(x, y)):
            blk = modall.at[_chip_id(cx, cy)]
            pltpu.make_async_remote_copy(src_ref=blk, dst_ref=blk, send_sem=send_sems.at[7 + k], recv_sem=recv_sems.at[7 + k],
                                         device_id=(x, y, c), device_id_type=MESH).wait_recv()
        for k, (cx, cy) in enumerate(_other_chips(x, y)):
            blk = cwall_ref.at[_chip_id(cx, cy)]
            pltpu.make_async_remote_copy(src_ref=blk, dst_ref=blk, send_sem=send_sems.at[10 + k], recv_sem=recv_sems.at[10 + k],
                                         device_id=(x, y, c), device_id_type=MESH).wait_recv()
        for cp in sends:
            cp.wait_send()
        mine = [modall[j, pl.ds(me_dev, 1), :] for j in range(N_CHIPS)]
        for r in range(6):
            pieces = []
            for h in range(2):
                pos = r * D_MODEL + h * 512
                pieces.append(mine[pos // CB][:, pos % CB:pos % CB + 512])
            mod_ref[r:r + 1, :] = jnp.concatenate(pieces, axis=1)

    vm = pl.BlockSpec(memory_space=pltpu.VMEM)
    return pl.pallas_call(
        body,
        name="ada_mod",
        in_specs=[vm, ANY_SPEC, vm, vm],
        out_specs=[vm] * 3,
        out_shape=[jax.ShapeDtypeStruct((N_DEV, 1, D_MODEL), F32), jax.ShapeDtypeStruct((6, D_MODEL), F32),
                   jax.ShapeDtypeStruct((N_CHIPS,) + conv_w.shape, F32)],
        scratch_shapes=[pltpu.VMEM((N_CHIPS, N_DEV, CB), F32), pltpu.SemaphoreType.DMA((13,)), pltpu.SemaphoreType.DMA((13,)),
                        pltpu.VMEM(w_ada.shape, F32), pltpu.SemaphoreType.DMA],
        compiler_params=pltpu.CompilerParams(has_side_effects=True, vmem_limit_bytes=VMEM_LIMIT),
    )(c3, w_ada, b_cols, conv_w)


def split_start(name, bufs, plan, n_sem, carry):
    nb = len(bufs)
    many = isinstance(carry, (list, tuple))
    alls = list(bufs) + (list(carry) if many else [carry])
    na = len(alls)

    def body(*refs):
        x, y, c = _place()
        ssem, rsem = refs[na], refs[na + 1]
        for i, (src, dst, dev) in enumerate(plan(refs[:nb], x, y, c)):
            pltpu.make_async_remote_copy(src_ref=src, dst_ref=dst, send_sem=ssem.at[i], recv_sem=rsem.at[i], device_id=dev,
                                         device_id_type=MESH).start()

    outs = pl.pallas_call(
        body,
        name=name,
        out_shape=[pltpu.SemaphoreType.DMA((n_sem,)), pltpu.SemaphoreType.DMA((n_sem,))] + _hbm_shapes(alls),
        in_specs=[HBM_SPEC] * na,
        out_specs=[SEM_SPEC, SEM_SPEC] + [HBM_SPEC] * na,
        input_output_aliases={i: 2 + i for i in range(na)},
        compiler_params=pltpu.CompilerParams(has_side_effects=EFFECT),
    )(*[_hbm(t) for t in alls])
    return outs[0], outs[1], list(outs[2:2 + nb]), (list(outs[2 + nb:]) if many else outs[-1])


def split_wait(name, ssem, rsem, bufs, plan, after):
    nb = len(bufs)

    def body(*refs):
        x, y, c = _place()
        s_ref, r_ref = refs[nb], refs[nb + 1]
        for i, (src, dst, dev) in enumerate(plan(refs[:nb], x, y, c)):
            cp = pltpu.make_async_remote_copy(src_ref=src, dst_ref=dst, send_sem=s_ref.at[i], recv_sem=r_ref.at[i], device_id=dev,
                                              device_id_type=MESH)
            cp.wait_send()
            cp.wait_recv()

    outs = pl.pallas_call(
        body,
        name=name,
        out_shape=_hbm_shapes(bufs),
        in_specs=[HBM_SPEC] * nb + [SEM_SPEC, SEM_SPEC, ANY_SPEC],
        out_specs=[HBM_SPEC] * nb,
        input_output_aliases={i: i for i in range(nb)},
        compiler_params=pltpu.CompilerParams(has_side_effects=EFFECT),
    )(*bufs, ssem, rsem, after)
    return list(outs)


def _gather_ici_plan(n):
    def plan(refs, x, y, c):
        out = []
        for w in range(n):
            rh = refs[w].shape[0] // 2
            for cx, cy in _other_chips(x, y):
                out.append((refs[w].at[pl.ds(c * rh, rh), :], _half_rows(refs[n + w], _chip_id(x, y), c, rh), (cx, cy, c)))
        return out

    return plan


def _gather_d2d_plan(n):
    def plan(refs, x, y, c):
        out = []
        for w in range(n):
            rh = refs[w].shape[1] // 2
            for cx, cy in _other_chips(x, y):
                blk = _half_rows(refs[w], _chip_id(cx, cy), c, rh)
                out.append((blk, blk, (x, y, 1 - c)))
        return out

    return plan


def _dev_id(x, y, c):
    return 4 * x + 2 * y + c


def _small_ici_plan(n):
    def plan(refs, x, y, c):
        out = []
        for w in range(n):
            dst = refs[n + w].at[_dev_id(x, y, c)]
            out.append((refs[w], dst, (x, y, 1 - c)))
            for cx, cy in _other_chips(x, y):
                out.append((refs[w], dst, (cx, cy, c)))
        return out

    return plan


def _small_d2d_plan(n):
    def plan(refs, x, y, c):
        out = []
        for w in range(n):
            for cx, cy in _other_chips(x, y):
                blk = refs[w].at[_dev_id(cx, cy, c)]
                out.append((blk, blk, (x, y, 1 - c)))
        return out

    return plan


def _rs_d2d_plan(n):
    def plan(refs, x, y, c):
        out = []
        for w in range(n):
            rh = refs[w].shape[1] // 2
            out.append((refs[w].at[:, pl.ds((1 - c) * rh, rh), :], refs[n + w], (x, y, 1 - c)))
        return out

    return plan


def _rs_ici_plan(n):
    def plan(refs, x, y, c):
        out = []
        for w in range(n):
            for k, (cx, cy) in enumerate(_other_chips(x, y)):
                out.append((refs[w].at[_chip_id(cx, cy)], refs[n + w].at[k], (cx, cy, c)))
        return out

    return plan


def _rs_share_plan(n):
    def plan(refs, x, y, c):
        out = []
        for w in range(n):
            rh = refs[w].shape[0] // 2
            rows = refs[w].at[pl.ds(c * rh, rh), :]
            out.append((rows, rows, (x, y, 1 - c)))
        return out

    return plan


def rs_add(grad, sibbuf, place, tr, name):
    _, R, C = grad.shape
    nt = (R // 2) // tr

    def body(p_ref, g_ref, s_ref, o_ref):
        o_ref[...] = (g_ref[...] + s_ref[...]).astype(BF16)

    return pl.pallas_call(
        body,
        name=name,
        grid_spec=pltpu.PrefetchScalarGridSpec(
            num_scalar_prefetch=1,
            grid=(N_CHIPS, nt),
            in_specs=[pl.BlockSpec((None, tr, C), lambda j, i, p: (j, p[0] * nt + i, 0)),
                      pl.BlockSpec((None, tr, C), lambda j, i, p: (j, i, 0))],
            out_specs=pl.BlockSpec((None, tr, C), lambda j, i, p: (j, i, 0)),
        ),
        out_shape=jax.ShapeDtypeStruct((N_CHIPS, R // 2, C), BF16),
        compiler_params=_params(2),
    )(place, grad, sibbuf)


def rs_final(grad, sibbuf, rbuf, place, tr, name):
    _, R, C = grad.shape
    nt = (R // 2) // tr

    def body(p_ref, g_ref, s_ref, r_ref, o_ref):
        o_ref[...] = (((g_ref[...] + s_ref[...]) + r_ref[0].astype(F32)) + r_ref[1].astype(F32)) + r_ref[2].astype(F32)

    return pl.pallas_call(
        body,
        name=name,
        grid_spec=pltpu.PrefetchScalarGridSpec(
            num_scalar_prefetch=1,
            grid=(nt,),
            in_specs=[pl.BlockSpec((None, tr, C), lambda i, p: (p[1], p[0] * nt + i, 0)),
                      pl.BlockSpec((None, tr, C), lambda i, p: (p[1], i, 0)),
                      pl.BlockSpec((3, tr, C), lambda i, p: (0, i, 0))],
            out_specs=pl.BlockSpec((tr, C), lambda i, p: (p[0] * nt + i, 0)),
        ),
        out_shape=jax.ShapeDtypeStruct((R, C), F32),
        compiler_params=_params(1),
    )(place, grad, sibbuf, rbuf)


class GradReduce:
    def __init__(self, tag, grads, rows, place):
        self.tag, self.grads, self.rows, self.place = tag, grads, rows, place
        self.n = len(grads)

    def d2d_start(self, carry):
        sib = [lax.empty((N_CHIPS, g.shape[1] // 2, g.shape[2]), F32) for g in self.grads]
        self.s1, self.r1, bufs, carry = split_start(f"rs_{self.tag}_d2d_start", self.grads + sib, _rs_d2d_plan(self.n), self.n, carry)
        self.bufs1 = bufs
        return carry

    def add_and_ici_start(self, after, carry):
        bufs = split_wait(f"rs_{self.tag}_d2d_wait", self.s1, self.r1, self.bufs1, _rs_d2d_plan(self.n), after)
        self.grads, self.sib = bufs[:self.n], bufs[self.n:]
        pb = [rs_add(g, s, self.place, tr, f"rs_{self.tag}_add{w}")
              for w, (g, s, tr) in enumerate(zip(self.grads, self.sib, self.rows))]
        rb = [lax.empty((3,) + p.shape[1:], BF16) for p in pb]
        self.s2, self.r2, self.bufs2, carry = split_start(f"rs_{self.tag}_ici_start", pb + rb, _rs_ici_plan(self.n), 3 * self.n, carry)
        return carry

    def final_and_share_start(self, after, carry):
        bufs = split_wait(f"rs_{self.tag}_ici_wait", self.s2, self.r2, self.bufs2, _rs_ici_plan(self.n), after)
        rb = bufs[self.n:]
        full = [rs_final(g, s, r, self.place, tr, f"rs_{self.tag}_final{w}")
                for w, (g, s, r, tr) in enumerate(zip(self.grads, self.sib, rb, self.rows))]
        self.s3, self.r3, self.bufs3, carry = split_start(f"rs_{self.tag}_share_start", full, _rs_share_plan(self.n), self.n, carry)
        return carry

    def finish(self, after):
        return split_wait(f"rs_{self.tag}_share_wait", self.s3, self.r3, self.bufs3, _rs_share_plan(self.n), after)


def _rope_tables(positions):
    inv_freq = ROPE_THETA ** (-jnp.arange(0, ROT_DIM, 2, dtype=F32) / ROT_DIM)
    ang = positions.astype(F32)[:, None] * inv_freq
    cos, sin = jnp.cos(ang), jnp.sin(ang)
    S = positions.shape[0]
    one, zero = jnp.ones((S, 48), F32), jnp.zeros((S, 48), F32)
    z8 = jnp.zeros((S, 8), F32)
    tc = jnp.concatenate([cos, cos, one], axis=1)
    tsa = jnp.concatenate([z8, sin, zero], axis=1)
    tsb = jnp.concatenate([-sin, z8, zero], axis=1)
    return tuple(jnp.tile(t, (1, 2)) for t in (tc, tsa, tsb))


def _block_diag(w_pool):
    wbd = jnp.zeros((POOL_W, POOL_W), F32)
    for gi in range(4):
        wbd = wbd.at[gi * 64:(gi + 1) * 64, gi * 64:(gi + 1) * 64].set(w_pool[gi])
    return wbd


def kernel(x, c, positions, w_ada, b_ada, g_pre_mix, g_post_mix, g_pre_ffn, g_post_ffn, w_in, w_pool, b_pool, pool_scale, w_out, w_up, conv_w, conv_b, w_down, loss_target, m_w_ada, m_b_ada, m_g_pre_mix, m_g_post_mix, m_g_pre_ffn, m_g_post_ffn, m_w_in, m_w_pool, m_b_pool, m_pool_scale, m_w_out, m_w_up, m_conv_w, m_conv_b, m_w_down, v_w_ada, v_b_ada, v_g_pre_mix, v_g_post_mix, v_g_pre_ffn, v_g_post_ffn, v_w_in, v_w_pool, v_b_pool, v_pool_scale, v_w_out, v_w_up, v_conv_w, v_conv_b, v_w_down):
    xi, yi, ci = lax.axis_index("x"), lax.axis_index("y"), lax.axis_index("c")
    chip = 2 * xi + yi
    place = jnp.stack([ci, chip]).astype(jnp.int32)
    x2, tgt = x[0], loss_target[0]
    S = x2.shape[0]

    def landing(s_):
        return lax.dynamic_update_slice(lax.empty((N_CHIPS,) + s_.shape, s_.dtype), s_[None], (chip, 0, 0))

    cb_ada = w_ada.shape[2]
    b_cols = lax.dynamic_slice(b_ada, (0, chip * cb_ada), (1, cb_ada))
    c_all, mod6, conv_w_g = ada_mod(c.reshape(1, 1, D_MODEL), w_ada[0], b_cols, conv_w[0])
    conv_w_f = jnp.transpose(conv_w_g, (1, 0, 2)).reshape(3, D_FF)
    mix_sh = [w_in[0].astype(BF16), w_out[0].astype(BF16)]
    ffn_sh = [w_up[0].astype(BF16), w_down[0].astype(BF16)]
    ga_s, ga_r, ga_bufs, mod6 = split_start("gather_mix_ici_start", mix_sh + [landing(t) for t in mix_sh], _gather_ici_plan(2), 6, mod6)
    gb_s, gb_r, gb_bufs, (mod6, tc, tsa, tsb) = split_start("gather_ffn_ici_start", ffn_sh + [landing(t) for t in ffn_sh],
                                                            _gather_ici_plan(2), 6, [mod6, *_rope_tables(positions[0])])
    wbd = _block_diag(w_pool[0]).astype(BF16)
    b_pool2, scale2 = b_pool.reshape(1, POOL_W), pool_scale
    ga_bufs = split_wait("gather_mix_ici_wait", ga_s, ga_r, ga_bufs, _gather_ici_plan(2), mod6)
    gc_s, gc_r, mix_land, mod6 = split_start("gather_mix_d2d_start", ga_bufs[2:], _gather_d2d_plan(2), 6, mod6)
    w_in_g, w_out_g = split_wait("gather_mix_d2d_wait", gc_s, gc_r, mix_land, _gather_d2d_plan(2), mod6)

    h1, u, *qkv = inproj_fwd(x2, g_pre_mix, mod6, w_in_g, tc, tsa, tsb)
    mixed, pool = pool_fwd(u, wbd, b_pool2, scale2)
    o_l = [attn_fwd(t, d) for t, d in zip(qkv, DILATIONS)]
    attn_done = sum(l[0, :8, :128] for _, l in o_l)
    gb_bufs = split_wait("gather_ffn_ici_wait", gb_s, gb_r, gb_bufs, _gather_ici_plan(2), attn_done)
    gd_s, gd_r, ffn_land, pool = split_start("gather_ffn_d2d_start", gb_bufs[2:], _gather_d2d_plan(2), 6, pool)
    cat, lse, lse4, lse16, y1, x1, h2 = outproj_fwd([o for o, _ in o_l] + [l for _, l in o_l], pool, x2, w_out_g, g_post_mix,
                                                    g_pre_ffn, mod6)
    lses = [lse[None], lse4, lse16]
    w_up_g, w_down_g = split_wait("gather_ffn_d2d_wait", gd_s, gd_r, ffn_land, _gather_d2d_plan(2), h2)
    w_down_f = w_down_g.reshape(D_FF, D_MODEL)
    gate, val, dy2, dout, loss_v, d_gt_f, d_g_post_ffn = ffn_fwd(h2, w_up_g, conv_w_f, conv_b, w_down_f, x1, tgt, g_post_ffn, mod6)

    dgc, dval, d_conv_w, d_conv_b, dw_down, dw_up = down_bwd(dy2, w_down_f, gate, val, conv_w_f, conv_b, h2)
    dx1, dy1, d_sh_f, d_sc_f, d_g_pre_ffn, d_gt_m, d_g_post_mix, dw_up = up_bwd(
        dgc, dval, conv_w_f, w_up_g, x1, dout, y1, g_pre_ffn, g_post_mix, mod6, h2, dw_up)
    rs_ffn = GradReduce("ffn", [dw_up, dw_down.reshape(N_CHIPS, D_FF // N_CHIPS, D_MODEL)], [256, 176], place)
    dy1 = rs_ffn.d2d_start(dy1)
    dpool, da1, da4, da16, dl1, dl4, dl16, dw_out = outproj_bwd(dy1, w_out_g, cat)
    dpool = rs_ffn.add_and_ici_start(dw_out, dpool)
    du, d_wbd, d_b_pool, d_scale = pool_bwd(dpool, mixed, wbd, b_pool2, scale2)
    dqkv = [attn_bwd(t, da, ls, dl, d) for t, da, ls, dl, d in zip(qkv, (da1[None], da4, da16), lses, (dl1[None], dl4, dl16), DILATIONS)]
    grad_x, d_sh_m, d_sc_m, d_g_pre_mix, dw_in = inproj_bwd(dqkv, du, x2, dx1, w_in_g, g_pre_mix, mod6, tc, tsa, tsb, h1)

    z1 = jnp.zeros((1, D_MODEL), F32)
    slab_a = jnp.concatenate(
        [d_sh_m, d_sc_m, d_gt_m, d_sh_f, d_sc_f, d_gt_f, d_g_pre_mix, d_g_post_mix, d_g_pre_ffn, d_g_post_ffn,
         jnp.concatenate([d_b_pool, d_scale, loss_v, jnp.zeros((1, 384), F32)], axis=1)] + [z1] * 5, axis=0)
    slab_b = jnp.concatenate([d_conv_w, d_conv_b, jnp.zeros((4, D_FF), F32)], axis=0)
    d_wpool = jnp.concatenate([d_wbd[gi * 64:(gi + 1) * 64, gi * 64:(gi + 1) * 64] for gi in range(4)], axis=0)
    dev = _dev_id(xi, yi, ci)
    small_src = [slab_a, slab_b, d_wpool]
    small_land = [lax.dynamic_update_slice(lax.empty((N_DEV,) + t.shape, F32), t[None], (dev, 0, 0)) for t in small_src]
    tok = jnp.zeros((8, 128), F32)
    gs_s, gs_r, gs_bufs, tok = split_start("small_ici_start", small_src + small_land, _small_ici_plan(3), 12, tok)
    rs_mix = GradReduce("mix", [dw_in, dw_out], [256, 256], place)
    tok = rs_mix.d2d_start(tok)
    tok = rs_ffn.final_and_share_start(tok, tok)
    gs_bufs = split_wait("small_ici_wait", gs_s, gs_r, gs_bufs, _small_ici_plan(3), tok)
    gt_s, gt_r, small_land, tok = split_start("small_d2d_start", gs_bufs[3:], _small_d2d_plan(3), 9, tok)
    tok = rs_mix.add_and_ici_start(tok, tok)
    slab_a_g, slab_b_g, wpool_g = split_wait("small_d2d_wait", gt_s, gt_r, small_land, _small_d2d_plan(3), tok)
    cw_cols = conv_w.shape[2]
    convw_g = lax.dynamic_slice(slab_b_g, (0, 0, chip * cw_cols), (N_DEV, 3, cw_cols))
    dmod_cols = lax.dynamic_slice(slab_a_g[:, :6, :].reshape(N_DEV, 6 * D_MODEL), (0, chip * cb_ada), (N_DEV, cb_ada))

    res = {}

    def big_adamw(name, w, g, m, v, tr):
        g_, d_, m_, v_ = adamw_rows(w[0], g, m[0], v[0], tr, "adamw_" + name)
        res[name] = (g_[None], d_[None], m_[None], v_[None])
        return v_

    g_ada, d_ada, m_ada, v_ada = adamw_ada(c_all.reshape(N_DEV, D_MODEL).T, dmod_cols, w_ada[0], m_w_ada[0], v_w_ada[0])
    res["w_ada"] = (g_ada[None], d_ada[None], m_ada[None], v_ada[None])
    g_w_up, g_w_down = rs_ffn.finish(v_ada)
    big_adamw("w_up", w_up, g_w_up, m_w_up, v_w_up, 256)
    last = big_adamw("w_down", w_down, g_w_down, m_w_down, v_w_down, 352)
    rs_mix.final_and_share_start(last, jnp.zeros((8, 128), F32))
    g_w_in, g_w_out = rs_mix.finish(last)
    big_adamw("w_in", w_in, g_w_in, m_w_in, v_w_in, 256)
    big_adamw("w_out", w_out, g_w_out, m_w_out, v_w_out, 256)
    small, loss_sum = adamw_small(slab_a_g, slab_b_g, convw_g, wpool_g, {
        "b_ada": (b_ada, m_b_ada, v_b_ada), "g_pre_mix": (g_pre_mix, m_g_pre_mix, v_g_pre_mix),
        "g_post_mix": (g_post_mix, m_g_post_mix, v_g_post_mix), "g_pre_ffn": (g_pre_ffn, m_g_pre_ffn, v_g_pre_ffn),
        "g_post_ffn": (g_post_ffn, m_g_post_ffn, v_g_post_ffn), "b_pool": (b_pool, m_b_pool, v_b_pool),
        "pool_scale": (pool_scale, m_pool_scale, v_pool_scale), "conv_b": (conv_b, m_conv_b, v_conv_b),
        "conv_w": (conv_w, m_conv_w, v_conv_w), "w_pool": (w_pool, m_w_pool, v_w_pool)})
    for name in ("b_ada", "g_pre_mix", "g_post_mix", "g_pre_ffn", "g_post_ffn", "pool_scale", "conv_b", "b_pool", "w_pool", "conv_w"):
        res[name] = tuple(small[name])

    loss = loss_sum[0, 0]
    order = ["w_ada", "b_ada", "g_pre_mix", "g_post_mix", "g_pre_ffn", "g_post_ffn", "w_in", "w_pool", "b_pool", "pool_scale",
             "w_out", "w_up", "conv_w", "conv_b", "w_down"]
    outs = [loss, grad_x[None]]
    for k in range(4):
        outs += [res[n][k] for n in order]
    return tuple(outs)
```

```python
import math

import jax
import jax.numpy as jnp
from jax import lax
from jax.experimental import pallas as pl
from jax.experimental.pallas import tpu as pltpu

F32 = jnp.float32
BF16 = jnp.bfloat16
MESH = pl.DeviceIdType.MESH

D_MODEL = 1024
HEAD_DIM = 64
POOL_W = 256
GROUP_W = 256
DILATIONS = (1, 4, 16)
ATT_BLOCK = 128
IN_W = 2560
D_FF = 2816
HALF_FF = 1408
ROT_DIM = 16
ROPE_THETA = 500000.0
NORM_EPS = 1e-6
N_CHIPS = 4
N_DEV = 8
NEG = -1e30

ADAM_LR = 0.001
ADAM_B1 = 0.9
ADAM_B2 = 0.999
ADAM_EPS = 1e-08
ADAM_WD = 0.01
ADAM_STEP = 10

VMEM_LIMIT = 56 * 1024 * 1024

NT = (((1,), (1,)), ((), ()))
TN = (((0,), (0,)), ((), ()))


def _params(n_grid=0, **kw):
    sem = ("arbitrary",) * n_grid if n_grid else None
    return pltpu.CompilerParams(dimension_semantics=sem, vmem_limit_bytes=VMEM_LIMIT, **kw)


def _full(shape):
    nd = len(shape)
    return pl.BlockSpec(tuple(shape), lambda *_: (0,) * nd, pipeline_mode=pl.Buffered(1))


def _rows(tm, ncol):
    return pl.BlockSpec((tm, ncol), lambda i: (i, 0))


def _zero(*refs):
    for ref in refs:
        ref[...] = jnp.zeros_like(ref)


def _colsum(v):
    return jnp.sum(v, axis=0, keepdims=True)


def _rope128(t, cs, sa, sb, sign):
    return t * cs + sign * (pltpu.roll(t, 8, 1) * sa + pltpu.roll(t, 120, 1) * sb)


FF_CHUNKS = tuple((ch, off, w) for ch in range(2) for off, w in ((0, 512), (512, 512), (1024, 384)))
GELU_C0 = math.sqrt(2.0 / math.pi)
GELU_C1 = GELU_C0 * 0.044715


def _gelu(z):
    z2 = z * z
    t = jnp.tanh(z * (GELU_C0 + GELU_C1 * z2))
    u = 0.5 * t + 0.5
    return z * u, u, t, z2


def _gelu_grad(z, u, t, z2):
    return u + (z * (GELU_C0 + (3.0 * GELU_C1) * z2)) * (0.5 - 0.5 * (t * t))


def _conv_taps(gate, halo, first):
    row = lax.broadcasted_iota(jnp.int32, gate.shape, 0)
    halo = jnp.where(first, 0.0, halo)
    nh = halo.shape[0]
    p1 = halo[nh - 1:nh, :]
    p2 = halo[nh - 2:nh - 1, :]
    g1 = jnp.where(row == 0, p1, pltpu.roll(gate, 1, 0))
    g2 = jnp.where(row == 0, p2, jnp.where(row == 1, p1, pltpu.roll(gate, 2, 0)))
    return g1, g2


def inproj_fwd(x, g, mod6, w_in_g, tc, tsa, tsb, tm=512):
    S = x.shape[0]

    def body(x_ref, g_ref, mod_ref, w_ref, tc_ref, tsa_ref, tsb_ref, h_ref, u_ref, q1_ref, q4_ref, q16_ref, scr):
        qkv_refs = (q1_ref, q4_ref, q16_ref)
        xv = x_ref[...]
        rstd = lax.rsqrt(jnp.mean(xv * xv, axis=-1, keepdims=True) + NORM_EPS)
        h = ((xv * rstd) * g_ref[...]) * (1.0 + mod_ref[1:2, :]) + mod_ref[0:1, :]
        hb = h.astype(BF16)
        h_ref[...] = hb
        cs, sa, sb = tc_ref[...], tsa_ref[...], tsb_ref[...]
        for j in range(N_CHIPS):
            res = jnp.dot(hb, w_ref[j], preferred_element_type=F32)
            for t in range(5):
                sp = 5 * j + t
                piece, half = sp // 2, sp % 2
                blk = res[:, t * 128:(t + 1) * 128]
                lanes = slice(half * 128, (half + 1) * 128)
                if piece == 0:
                    u_ref[:, lanes] = blk
                else:
                    kind, gi = (piece - 1) // 3, (piece - 1) % 3
                    if kind == 0:
                        blk = _rope128(blk, cs, sa, sb, 1.0) * (HEAD_DIM ** -0.5)
                    elif kind == 1:
                        blk = _rope128(blk, cs, sa, sb, 1.0)
                    d = DILATIONS[gi]
                    if d == 1:
                        q1_ref[kind, 0, :, lanes] = blk.astype(BF16)
                    else:
                        scr[...] = blk
                        for r in range(d):
                            qkv_refs[gi][kind, r, :, lanes] = scr[pl.ds(r, tm // d, stride=d), :].astype(BF16)

    cls = lambda d: pl.BlockSpec((3, d, tm // d, GROUP_W), lambda i: (0, 0, i, 0))
    return pl.pallas_call(
        body,
        name="inproj_fwd",
        grid=(S // tm,),
        in_specs=[_rows(tm, D_MODEL), _full((1, D_MODEL)), _full((6, D_MODEL)), _full(w_in_g.shape),
                  _rows(tm, 128), _rows(tm, 128), _rows(tm, 128)],
        out_specs=[_rows(tm, D_MODEL), _rows(tm, POOL_W)] + [cls(d) for d in DILATIONS],
        out_shape=[jax.ShapeDtypeStruct((S, D_MODEL), BF16), jax.ShapeDtypeStruct((S, POOL_W), F32)]
        + [jax.ShapeDtypeStruct((3, d, S // d, GROUP_W), BF16) for d in DILATIONS],
        scratch_shapes=[pltpu.VMEM((tm, 128), F32)],
        compiler_params=_params(1),
    )(x, g, mod6, w_in_g, tc, tsa, tsb)


def _attn_masks():
    row = lax.broadcasted_iota(jnp.int32, (2 * ATT_BLOCK, 2 * ATT_BLOCK), 0) % ATT_BLOCK
    col = lax.broadcasted_iota(jnp.int32, (2 * ATT_BLOCK, 2 * ATT_BLOCK), 1)
    band = (col >= row) & (col <= row + ATT_BLOCK)
    lane = lax.broadcasted_iota(jnp.int32, (ATT_BLOCK, 128), 1)
    return band, col, lane < HEAD_DIM


def _classes_per_step(d, nb):
    return min(d, max(1, 8 // nb))


def _stack_heads(t, lo):
    z = jnp.zeros_like(t)
    return jnp.concatenate([jnp.where(lo, t, z), jnp.where(lo, z, t)], axis=0)


def _unstack_heads(t2, lo):
    return jnp.where(lo, t2[:ATT_BLOCK], t2[ATT_BLOCK:])


def attn_fwd(qkv, d):
    L = qkv.shape[2]
    nb = L // ATT_BLOCK
    cpb = _classes_per_step(d, nb)

    def body(q_ref, k_ref, v_ref, o_ref, l_ref, kpad, vpad):
        for cls in range(cpb):
            kpad[cls, 0:ATT_BLOCK, :] = jnp.zeros((ATT_BLOCK, GROUP_W), BF16)
            vpad[cls, 0:ATT_BLOCK, :] = jnp.zeros((ATT_BLOCK, GROUP_W), BF16)
            kpad[cls, ATT_BLOCK:, :] = k_ref[cls]
            vpad[cls, ATT_BLOCK:, :] = v_ref[cls]
        band, col, lo = _attn_masks()

        def step(t, carry):
            cls, n = t // nb, t % nb
            r0 = pl.multiple_of(n * ATT_BLOCK, ATT_BLOCK)
            valid = band & ((col >= ATT_BLOCK) | (n > 0))
            qb = q_ref[cls, pl.ds(r0, ATT_BLOCK), :]
            kb = kpad[cls, pl.ds(r0, 2 * ATT_BLOCK), :]
            vb = vpad[cls, pl.ds(r0, 2 * ATT_BLOCK), :]
            for pair in range(2):
                lanes = slice(pair * 128, (pair + 1) * 128)
                qp, kp, vp = qb[:, lanes], kb[:, lanes], vb[:, lanes]
                s = lax.dot_general(_stack_heads(qp, lo), kp, NT, preferred_element_type=F32)
                s = jnp.where(valid, s, NEG)
                m = jnp.max(s, axis=1, keepdims=True)
                p = jnp.exp(s - m)
                den = jnp.sum(p, axis=1, keepdims=True)
                pv = jnp.dot(p.astype(BF16), vp, preferred_element_type=F32)
                o_ref[cls, pl.ds(r0, ATT_BLOCK), lanes] = _unstack_heads(pv / den, lo).astype(BF16)
                l_ref[cls, pl.ds(r0, ATT_BLOCK), lanes] = _unstack_heads(jnp.broadcast_to(m + jnp.log(den), pv.shape), lo)
            return carry

        lax.fori_loop(0, cpb * nb, step, 0, unroll=8)

    spec = lambda kind: pl.BlockSpec((None, cpb, L, GROUP_W), lambda r: (kind, r, 0, 0))
    return pl.pallas_call(
        body,
        name=f"attn_fwd_d{d}",
        grid=(d // cpb,),
        in_specs=[spec(0), spec(1), spec(2)],
        out_specs=[pl.BlockSpec((cpb, L, GROUP_W), lambda r: (r, 0, 0))] * 2,
        out_shape=[jax.ShapeDtypeStruct((d, L, GROUP_W), BF16), jax.ShapeDtypeStruct((d, L, GROUP_W), F32)],
        scratch_shapes=[pltpu.VMEM((cpb, L + ATT_BLOCK, GROUP_W), BF16)] * 2,
        compiler_params=_params(1),
    )(qkv, qkv, qkv)


def _pool_lane_windows(shape):
    lane = lax.broadcasted_iota(jnp.int32, shape, 1)
    return lane, jnp.where(lane < 64, 2, jnp.where(lane < 128, 4, jnp.where(lane < 192, 8, 16)))


def pool_fwd(u, wbd, b, scale):
    S = u.shape[0]

    def body(u_ref, w_ref, b_ref, s_ref, mixed_ref, out_ref):
        uv = u_ref[...]
        row = lax.broadcasted_iota(jnp.int32, uv.shape, 0)
        lane, win = _pool_lane_windows(uv.shape)

        def shift(a, k):
            return jnp.where(row >= k, pltpu.roll(a, k, 0), 0.0)

        s2 = uv + shift(uv, 1)
        s4 = s2 + shift(s2, 2)
        s8 = s4 + shift(s4, 4)
        s16 = s8 + shift(s8, 8)
        tsum = jnp.where(lane < 64, s2, jnp.where(lane < 128, s4, jnp.where(lane < 192, s8, s16)))
        cnt = jnp.minimum(row + 1, win).astype(F32)
        mb = (tsum / cnt - uv).astype(BF16)
        mixed_ref[...] = mb
        y = jnp.dot(mb, w_ref[...], preferred_element_type=F32) + b_ref[...]
        out_ref[...] = (y * s_ref[...]).astype(BF16)

    vm = pl.BlockSpec(memory_space=pltpu.VMEM)
    return pl.pallas_call(
        body,
        name="pool_fwd",
        in_specs=[vm] * 4,
        out_specs=[vm] * 2,
        out_shape=[jax.ShapeDtypeStruct((S, POOL_W), BF16)] * 2,
        compiler_params=_params(),
    )(u, wbd, b, scale)


def outproj_fwd(o_l, pool, x, w_out_g, g_post, g_pre, mod6, tm=512):
    S = x.shape[0]

    def body(o0, o1, o2, l0, l1, l2, pool_ref, x_ref, w_ref, gpost_ref, gpre_ref, mod_ref,
             cat_ref, lse_ref, lse4_ref, lse16_ref, y1_ref, x1_ref, h2_ref, so4, sl4, so16, sl16):
        for d, src, dst in ((4, o1, so4), (4, l1, sl4), (16, o2, so16), (16, l2, sl16)):
            for r in range(d):
                for h in range(2):
                    dst[h, pl.ds(r, tm // d, stride=d), :] = src[r, :, h * 128:(h + 1) * 128].astype(F32)
        nat = lambda ref: jnp.concatenate([ref[0], ref[1]], axis=1)
        a, b, c = l0[0], nat(sl4), nat(sl16)
        m = jnp.maximum(jnp.maximum(a, b), c)
        e0, e1, e2 = jnp.exp(a - m), jnp.exp(b - m), jnp.exp(c - m)
        z = e0 + e1 + e2
        lse = m + jnp.log(z)
        lse_ref[...] = lse
        for h in range(2):
            sl4[h] = lse[:, h * 128:(h + 1) * 128]
        for d, dst in ((4, lse4_ref), (16, lse16_ref)):
            for r in range(d):
                for h in range(2):
                    dst[r, :, h * 128:(h + 1) * 128] = sl4[h, pl.ds(r, tm // d, stride=d), :]
        attn = (e0 * o0[0].astype(F32) + e1 * nat(so4) + e2 * nat(so16)) / z
        cat = jnp.concatenate([pool_ref[...], attn.astype(BF16)], axis=1)
        cat_ref[...] = cat
        y1 = jnp.concatenate([jnp.dot(cat, w_ref[j], preferred_element_type=F32) for j in range(N_CHIPS)], axis=1)
        y1_ref[...] = y1.astype(BF16)
        rstd = lax.rsqrt(jnp.mean(y1 * y1, axis=-1, keepdims=True) + NORM_EPS)
        x1 = x_ref[...] + mod_ref[2:3, :] * ((y1 * rstd) * gpost_ref[...])
        x1_ref[...] = x1
        rstd2 = lax.rsqrt(jnp.mean(x1 * x1, axis=-1, keepdims=True) + NORM_EPS)
        h2 = ((x1 * rstd2) * gpre_ref[...]) * (1.0 + mod_ref[4:5, :]) + mod_ref[3:4, :]
        h2_ref[...] = h2.astype(BF16)

    t256 = _rows(tm, GROUP_W)
    cls = lambda d: pl.BlockSpec((d, tm // d, GROUP_W), lambda i: (0, i, 0))
    cls_shape = lambda d: jax.ShapeDtypeStruct((d, S // d, GROUP_W), F32)
    return pl.pallas_call(
        body,
        name="outproj_fwd",
        grid=(S // tm,),
        in_specs=[cls(d) for d in DILATIONS] * 2 + [t256, _rows(tm, D_MODEL), _full(w_out_g.shape), _full((1, D_MODEL)),
                                                    _full((1, D_MODEL)), _full((6, D_MODEL))],
        out_specs=[_rows(tm, 512), t256, cls(4), cls(16), _rows(tm, D_MODEL), _rows(tm, D_MODEL), _rows(tm, D_MODEL)],
        out_shape=[jax.ShapeDtypeStruct((S, 512), BF16), jax.ShapeDtypeStruct((S, GROUP_W), F32), cls_shape(4), cls_shape(16),
                   jax.ShapeDtypeStruct((S, D_MODEL), BF16), jax.ShapeDtypeStruct((S, D_MODEL), F32),
                   jax.ShapeDtypeStruct((S, D_MODEL), BF16)],
        scratch_shapes=[pltpu.VMEM((2, tm, 128), F32)] * 4,
        compiler_params=_params(1),
    )(*o_l, pool, x, w_out_g, g_post, g_pre, mod6)


def _halo_prev(tm, ncol):
    return pl.BlockSpec((16, ncol), lambda i: (jnp.maximum(i * (tm // 16) - 1, 0), 0))


def ffn_fwd(h2, w_up_g, conv_w, conv_b, w_down, x1, target, g_post, mod6, tm=512):
    S = x1.shape[0]

    def body(h_ref, wu_ref, cw_ref, cb_ref, wd_ref, x1_ref, tgt_ref, g_ref, mod_ref,
             gate_ref, val_ref, dy2_ref, dout_ref, loss_ref, dgt_ref, dg_ref, carry):
        first = pl.program_id(0) == 0

        @pl.when(first)
        def _():
            _zero(carry, loss_ref, dgt_ref, dg_ref)

        hb = h_ref[...]
        y2 = jnp.zeros((tm, D_MODEL), F32)
        for ch in range(2):
            cols = slice(ch * HALF_FF, (ch + 1) * HALF_FF)
            gt = jnp.dot(hb, wu_ref[ch], preferred_element_type=F32)
            gb = gt.astype(BF16)
            vb = jnp.dot(hb, wu_ref[2 + ch], preferred_element_type=F32).astype(BF16)
            gate_ref[:, cols] = gb
            val_ref[:, cols] = vb
            g1, g2 = _conv_taps(gt, carry[:, cols], first)
            carry[:, cols] = gt[tm - 8:, :]
            gc = g2 * cw_ref[0:1, cols] + g1 * cw_ref[1:2, cols] + gt * cw_ref[2:3, cols] + cb_ref[:, cols]
            ab = _gelu(gc.astype(BF16))[0] * vb
            y2 = y2 + jnp.dot(ab, wd_ref[cols, :], preferred_element_type=F32)
        rstd = lax.rsqrt(jnp.mean(y2 * y2, axis=-1, keepdims=True) + NORM_EPS)
        y2n = y2 * rstd
        gv = g_ref[...]
        gtf = mod_ref[5:6, :]
        r2 = y2n * gv
        diff = (x1_ref[...] + gtf * r2) - tgt_ref[...]
        loss_ref[...] += jnp.zeros((1, 128), F32) + 0.5 * jnp.sum(diff * diff) * (1.0 / D_MODEL)
        dout = diff * (1.0 / D_MODEL)
        dout_ref[...] = dout
        dgt_ref[...] += _colsum(dout * r2)
        dr2 = dout * gtf
        dg_ref[...] += _colsum(dr2 * y2n)
        dyn = dr2 * gv
        dy2 = rstd * (dyn - y2n * jnp.mean(dyn * y2n, axis=-1, keepdims=True))
        dy2_ref[...] = dy2.astype(BF16)

    vec = _full((1, D_MODEL))
    return pl.pallas_call(
        body,
        name="ffn_fwd",
        grid=(S // tm,),
        in_specs=[_rows(tm, D_MODEL), _full(w_up_g.shape), _full((3, D_FF)), _full((1, D_FF)), _full((D_FF, D_MODEL)),
                  _rows(tm, D_MODEL), _rows(tm, D_MODEL), vec, _full((6, D_MODEL))],
        out_specs=[_rows(tm, D_FF), _rows(tm, D_FF), _rows(tm, D_MODEL), _rows(tm, D_MODEL), _full((1, 128)), vec, vec],
        out_shape=[jax.ShapeDtypeStruct((S, D_FF), BF16)] * 2 + [jax.ShapeDtypeStruct((S, D_MODEL), BF16),
                                                                 jax.ShapeDtypeStruct((S, D_MODEL), F32),
                                                                 jax.ShapeDtypeStruct((1, 128), F32),
                                                                 jax.ShapeDtypeStruct((1, D_MODEL), F32),
                                                                 jax.ShapeDtypeStruct((1, D_MODEL), F32)],
        scratch_shapes=[pltpu.VMEM((8, D_FF), F32)],
        compiler_params=_params(1),
    )(h2, w_up_g, conv_w, conv_b, w_down, x1, target, g_post, mod6)


def down_bwd(dy2, w_down, gate, val, conv_w, conv_b, h2, tm=512):
    S = dy2.shape[0]

    def body(dy_ref, w_ref, gate_ref, halo_ref, val_ref, cw_ref, cb_ref, h_ref,
             dgc_ref, dval_ref, dcw_ref, dcb_ref, dwd_ref, dwu_ref):
        first = pl.program_id(1) == 0

        @pl.when(first)
        def _():
            dcw_ref[...] = jnp.zeros_like(dcw_ref)
            dcb_ref[...] = jnp.zeros_like(dcb_ref)
            dwd_ref[...] = jnp.zeros_like(dwd_ref)
            dwu_ref[...] = jnp.zeros_like(dwu_ref)

        dyb = dy_ref[...]
        hb = h_ref[...]
        pieces = [(off, w) for ch, off, w in FF_CHUNKS if ch == 0]

        def col(i):
            return slice(pieces[i][0], pieces[i][0] + pieces[i][1])

        def mm_da(i):
            return lax.dot_general(dyb, w_ref[col(i), :], NT, preferred_element_type=F32)

        def elementwise(i, da):
            cols = col(i)
            gt = gate_ref[:, cols].astype(F32)
            g1, g2 = _conv_taps(gt, halo_ref[:, cols].astype(F32), first)
            gc = g2 * cw_ref[0:1, cols] + g1 * cw_ref[1:2, cols] + gt * cw_ref[2:3, cols] + cb_ref[:, cols]
            zb, dab, vb = gc.astype(BF16), da.astype(BF16), val_ref[:, cols]
            ge, u, th, z2 = _gelu(zb)
            dgb = dab * vb * _gelu_grad(zb, u, th, z2)
            dgc_ref[:, cols] = dgb
            dgc = dgb.astype(F32)
            dvb = dab * ge
            dval_ref[:, cols] = dvb
            dcb_ref[:, cols] += _colsum(dgc)
            dcw_ref[0:1, cols] += _colsum(dgc * g2)
            dcw_ref[1:2, cols] += _colsum(dgc * g1)
            dcw_ref[2:3, cols] += _colsum(dgc * gt)
            return dvb, ge * vb

        def mm_dw(i, dvb_ab):
            dvb, ab = dvb_ab
            dwd_ref[col(i), :] += lax.dot_general(ab, dyb, TN, preferred_element_type=F32)
            dwu_ref[:, col(i)] += lax.dot_general(hb, dvb, TN, preferred_element_type=F32)

        n = len(pieces)
        da = mm_da(0)
        prev = None
        for i in range(n):
            nxt = mm_da(i + 1) if i + 1 < n else None
            if prev is not None:
                mm_dw(i - 1, prev)
            prev = elementwise(i, da)
            da = nxt
        mm_dw(n - 1, prev)

    one = pl.Buffered(1)
    tok = pl.BlockSpec((tm, D_MODEL), lambda c, i: (i, 0))
    ff = pl.BlockSpec((tm, HALF_FF), lambda c, i: (i, c))
    halo = pl.BlockSpec((16, HALF_FF), lambda c, i: (jnp.maximum(i * (tm // 16) - 1, 0), c))
    per_half = lambda rows: pl.BlockSpec((rows, HALF_FF), lambda c, i: (0, c), pipeline_mode=one)
    return pl.pallas_call(
        body,
        name="down_bwd",
        grid=(2, S // tm),
        in_specs=[tok, pl.BlockSpec((HALF_FF, D_MODEL), lambda c, i: (c, 0), pipeline_mode=one), ff, halo, ff,
                  per_half(3), per_half(1), tok],
        out_specs=[ff, ff, per_half(3), per_half(1), pl.BlockSpec((HALF_FF, D_MODEL), lambda c, i: (c, 0), pipeline_mode=one),
                   pl.BlockSpec((None, D_MODEL, HALF_FF), lambda c, i: (2 + c, 0, 0), pipeline_mode=one)],
        out_shape=[jax.ShapeDtypeStruct((S, D_FF), BF16), jax.ShapeDtypeStruct((S, D_FF), BF16),
                   jax.ShapeDtypeStruct((3, D_FF), F32), jax.ShapeDtypeStruct((1, D_FF), F32),
                   jax.ShapeDtypeStruct((D_FF, D_MODEL), F32), jax.ShapeDtypeStruct((N_CHIPS, D_MODEL, HALF_FF), F32)],
        compiler_params=_params(2),
    )(dy2, w_down, gate, gate, val, conv_w, conv_b, h2)


def up_bwd(dgc, dval, conv_w, w_up_g, x1, dout, y1, g_pre, g_post, mod6, h2, dw_up, tm=256):
    S = x1.shape[0]
    last_blk = S // 16 - 1

    def body(dgc_ref, nxt_ref, dval_ref, cw_ref, w_ref, x1_ref, dout_ref, y1_ref, gpre_ref, gpost_ref, mod_ref, h_ref, dwin_ref,
             dx1_ref, dy1_ref, dsh_ref, dsc_ref, dgpre_ref, dgt_ref, dgpost_ref, dwu_ref):
        last = pl.program_id(0) == pl.num_programs(0) - 1

        @pl.when(pl.program_id(0) == 0)
        def _():
            _zero(dwu_ref, dsh_ref, dsc_ref, dgpre_ref, dgt_ref, dgpost_ref)

        hb = h_ref[...]
        dh = jnp.zeros((tm, D_MODEL), F32)
        for ch in range(2):
            cols = slice(ch * HALF_FF, (ch + 1) * HALF_FF)
            dg = dgc_ref[:, cols].astype(F32)
            nx = jnp.where(last, 0.0, nxt_ref[:, cols].astype(F32))
            row = lax.broadcasted_iota(jnp.int32, dg.shape, 0)
            n0, n1 = nx[0:1, :], nx[1:2, :]
            u1 = jnp.where(row == tm - 1, n0, pltpu.roll(dg, tm - 1, 0))
            u2 = jnp.where(row == tm - 1, n1, jnp.where(row == tm - 2, n0, pltpu.roll(dg, tm - 2, 0)))
            dgate = (dg * cw_ref[2:3, cols] + u1 * cw_ref[1:2, cols] + u2 * cw_ref[0:1, cols]).astype(BF16)
            dwu_ref[ch] += lax.dot_general(hb, dgate, TN, preferred_element_type=F32)
            dh = dh + lax.dot_general(dgate, w_ref[ch], NT, preferred_element_type=F32)
            dh = dh + lax.dot_general(dval_ref[:, cols], w_ref[2 + ch], NT, preferred_element_type=F32)
        x1 = x1_ref[...]
        rstd = lax.rsqrt(jnp.mean(x1 * x1, axis=-1, keepdims=True) + NORM_EPS)
        n2 = x1 * rstd
        gpre = gpre_ref[...]
        one_sc = 1.0 + mod_ref[4:5, :]
        dsh_ref[...] += _colsum(dh)
        dsc_ref[...] += _colsum(dh * (n2 * gpre))
        dgpre_ref[...] += _colsum(dh * one_sc * n2)
        dn = dh * (gpre * one_sc)
        dx1 = dout_ref[...] + rstd * (dn - n2 * jnp.mean(dn * n2, axis=-1, keepdims=True))
        dx1_ref[...] = dx1
        y1 = y1_ref[...].astype(F32)
        rstd1 = lax.rsqrt(jnp.mean(y1 * y1, axis=-1, keepdims=True) + NORM_EPS)
        y1n = y1 * rstd1
        gpost = gpost_ref[...]
        gtm = mod_ref[2:3, :]
        dgt_ref[...] += _colsum(dx1 * (y1n * gpost))
        dr1 = dx1 * gtm
        dgpost_ref[...] += _colsum(dr1 * y1n)
        dyn = dr1 * gpost
        dy1 = rstd1 * (dyn - y1n * jnp.mean(dyn * y1n, axis=-1, keepdims=True))
        dy1_ref[...] = dy1.astype(BF16)

    vec = _full((1, D_MODEL))
    nxt = pl.BlockSpec((16, D_FF), lambda i: (jnp.minimum((i + 1) * (tm // 16), last_blk), 0))
    return pl.pallas_call(
        body,
        name="up_bwd",
        grid=(S // tm,),
        in_specs=[_rows(tm, D_FF), nxt, _rows(tm, D_FF), _full((3, D_FF)), _full(w_up_g.shape), _rows(tm, D_MODEL),
                  _rows(tm, D_MODEL), _rows(tm, D_MODEL), vec, vec, _full((6, D_MODEL)), _rows(tm, D_MODEL),
                  pl.BlockSpec(memory_space=pl.ANY)],
        out_specs=[_rows(tm, D_MODEL), _rows(tm, D_MODEL), vec, vec, vec, vec, vec,
                   pl.BlockSpec((2, D_MODEL, HALF_FF), lambda i: (0, 0, 0), pipeline_mode=pl.Buffered(1))],
        out_shape=[jax.ShapeDtypeStruct((S, D_MODEL), F32), jax.ShapeDtypeStruct((S, D_MODEL), BF16)]
        + [jax.ShapeDtypeStruct((1, D_MODEL), F32)] * 5 + [jax.ShapeDtypeStruct(dw_up.shape, F32)],
        input_output_aliases={12: 7},
        compiler_params=_params(1),
    )(dgc, dgc, dval, conv_w, w_up_g, x1, dout, y1, g_pre, g_post, mod6, h2, dw_up)


def outproj_bwd(dy1, w_out_g, cat, tm=512):
    S = dy1.shape[0]

    def body(dy_ref, w_ref, cat_ref, dpool_ref, dattn_ref, da4_ref, da16_ref, delta_ref, dl4_ref, dl16_ref, dw_ref, scr):
        @pl.when(pl.program_id(0) == 0)
        def _():
            dw_ref[...] = jnp.zeros_like(dw_ref)

        catb = cat_ref[...]
        dcat = jnp.zeros((tm, 512), F32)
        for j in range(N_CHIPS):
            dyj = dy_ref[:, j * 256:(j + 1) * 256]
            dcat = dcat + lax.dot_general(dyj, w_ref[j], NT, preferred_element_type=F32)
            dw_ref[j] += lax.dot_general(catb, dyj, TN, preferred_element_type=F32)
        dpool_ref[...] = dcat[:, :POOL_W]
        dattn = dcat[:, POOL_W:]
        dattn_ref[...] = dattn.astype(BF16)
        for h in range(2):
            scr[h] = dattn[:, h * 128:(h + 1) * 128]
        for d, dst in ((4, da4_ref), (16, da16_ref)):
            for r in range(d):
                for h in range(2):
                    dst[r, :, h * 128:(h + 1) * 128] = scr[h, pl.ds(r, tm // d, stride=d), :].astype(BF16)
        prod = dattn * catb[:, POOL_W:].astype(F32)
        r = lax.broadcasted_iota(jnp.int32, (GROUP_W, GROUP_W), 0) // HEAD_DIM
        c = lax.broadcasted_iota(jnp.int32, (GROUP_W, GROUP_W), 1) // HEAD_DIM
        ones_bd = jnp.where(r == c, 1.0, 0.0).astype(BF16)
        hi = prod.astype(BF16)
        lo = (prod - hi.astype(F32)).astype(BF16)
        delta = jnp.dot(hi, ones_bd, preferred_element_type=F32) + jnp.dot(lo, ones_bd, preferred_element_type=F32)
        delta_ref[...] = delta
        for h in range(2):
            scr[h] = delta[:, h * 128:(h + 1) * 128]
        for d, dst in ((4, dl4_ref), (16, dl16_ref)):
            for r in range(d):
                for h in range(2):
                    dst[r, :, h * 128:(h + 1) * 128] = scr[h, pl.ds(r, tm // d, stride=d), :]

    cls = lambda d: pl.BlockSpec((d, tm // d, GROUP_W), lambda i: (0, i, 0))
    cls_shape = lambda d, dt: jax.ShapeDtypeStruct((d, S // d, GROUP_W), dt)
    return pl.pallas_call(
        body,
        name="outproj_bwd",
        grid=(S // tm,),
        in_specs=[_rows(tm, D_MODEL), _full(w_out_g.shape), _rows(tm, 512)],
        out_specs=[_rows(tm, POOL_W), _rows(tm, GROUP_W), cls(4), cls(16), _rows(tm, GROUP_W), cls(4), cls(16),
                   _full(w_out_g.shape)],
        out_shape=[jax.ShapeDtypeStruct((S, POOL_W), F32), jax.ShapeDtypeStruct((S, GROUP_W), BF16), cls_shape(4, BF16),
                   cls_shape(16, BF16), jax.ShapeDtypeStruct((S, GROUP_W), F32), cls_shape(4, F32), cls_shape(16, F32),
                   jax.ShapeDtypeStruct(w_out_g.shape, F32)],
        scratch_shapes=[pltpu.VMEM((2, tm, 128), F32)],
        compiler_params=_params(1),
    )(dy1, w_out_g, cat)


def attn_bwd(qkv, dattn, lse, delta, d):
    L = qkv.shape[2]
    nb = L // ATT_BLOCK
    cpb = _classes_per_step(d, nb)

    def body(q_ref, k_ref, v_ref, do_ref, l_ref, dl_ref, out_ref, kpad, vpad, dkpad, dvpad):
        for cls in range(cpb):
            kpad[cls, 0:ATT_BLOCK, :] = jnp.zeros((ATT_BLOCK, GROUP_W), BF16)
            vpad[cls, 0:ATT_BLOCK, :] = jnp.zeros((ATT_BLOCK, GROUP_W), BF16)
            kpad[cls, ATT_BLOCK:, :] = k_ref[cls]
            vpad[cls, ATT_BLOCK:, :] = v_ref[cls]
        dkpad[...] = jnp.zeros_like(dkpad)
        dvpad[...] = jnp.zeros_like(dvpad)
        band, col, lo = _attn_masks()

        def step(t, carry):
            cls, n = t // nb, t % nb
            r0 = pl.multiple_of(n * ATT_BLOCK, ATT_BLOCK)
            valid = band & ((col >= ATT_BLOCK) | (n > 0))
            qb = q_ref[cls, pl.ds(r0, ATT_BLOCK), :]
            dob = do_ref[cls, pl.ds(r0, ATT_BLOCK), :]
            lb = l_ref[cls, pl.ds(r0, ATT_BLOCK), :]
            dlb = dl_ref[cls, pl.ds(r0, ATT_BLOCK), :]
            kb = kpad[cls, pl.ds(r0, 2 * ATT_BLOCK), :]
            vb = vpad[cls, pl.ds(r0, 2 * ATT_BLOCK), :]
            for pair in range(2):
                lanes = slice(pair * 128, (pair + 1) * 128)
                qp, dop, kp, vp = qb[:, lanes], dob[:, lanes], kb[:, lanes], vb[:, lanes]
                c0, c1 = pair * 128, pair * 128 + HEAD_DIM
                q2, do2 = _stack_heads(qp, lo), _stack_heads(dop, lo)
                lse2 = jnp.concatenate([lb[:, c0:c0 + 1], lb[:, c1:c1 + 1]], axis=0)
                dl2 = jnp.concatenate([dlb[:, c0:c0 + 1], dlb[:, c1:c1 + 1]], axis=0)
                s = lax.dot_general(q2, kp, NT, preferred_element_type=F32)
                s = jnp.where(valid, s, NEG)
                p = jnp.exp(s - lse2)
                dp = lax.dot_general(do2, vp, NT, preferred_element_type=F32)
                ds = (p * (dp - dl2)).astype(BF16)
                dq2 = jnp.dot(ds, kp, preferred_element_type=F32)
                out_ref[0, cls, pl.ds(r0, ATT_BLOCK), lanes] = _unstack_heads(dq2, lo)
                dkpad[cls, pl.ds(r0, 2 * ATT_BLOCK), lanes] += lax.dot_general(ds, q2, TN, preferred_element_type=F32)
                dvpad[cls, pl.ds(r0, 2 * ATT_BLOCK), lanes] += lax.dot_general(p.astype(BF16), do2, TN, preferred_element_type=F32)
            return carry

        lax.fori_loop(0, cpb * nb, step, 0, unroll=8)
        for cls in range(cpb):
            out_ref[1, cls] = dkpad[cls, ATT_BLOCK:, :]
            out_ref[2, cls] = dvpad[cls, ATT_BLOCK:, :]

    spec = lambda kind: pl.BlockSpec((None, cpb, L, GROUP_W), lambda r: (kind, r, 0, 0))
    per_cls = pl.BlockSpec((cpb, L, GROUP_W), lambda r: (r, 0, 0))
    return pl.pallas_call(
        body,
        name=f"attn_bwd_d{d}",
        grid=(d // cpb,),
        in_specs=[spec(0), spec(1), spec(2), per_cls, per_cls, per_cls],
        out_specs=pl.BlockSpec((3, cpb, L, GROUP_W), lambda r: (0, r, 0, 0)),
        out_shape=jax.ShapeDtypeStruct((3, d, L, GROUP_W), F32),
        scratch_shapes=[pltpu.VMEM((cpb, L + ATT_BLOCK, GROUP_W), BF16)] * 2 + [pltpu.VMEM((cpb, L + ATT_BLOCK, GROUP_W), F32)] * 2,
        compiler_params=_params(1),
    )(qkv, qkv, qkv, dattn, lse, delta)


def pool_bwd(dpool, mixed, wbd, b, scale):
    S = dpool.shape[0]

    def body(dp_ref, mx_ref, w_ref, b_ref, s_ref, du_ref, dw_ref, db_ref, ds_ref):
        dp = dp_ref[...]
        mb = mx_ref[...]
        wv = w_ref[...]
        ypre = jnp.dot(mb, wv, preferred_element_type=F32) + b_ref[...]
        ds_ref[...] = _colsum(dp * ypre)
        dpre = dp * s_ref[...]
        db_ref[...] = _colsum(dpre)
        dpb = dpre.astype(BF16)
        dw_ref[...] = lax.dot_general(mb, dpb, TN, preferred_element_type=F32)
        dmix = lax.dot_general(dpb, wv, NT, preferred_element_type=F32)
        row = lax.broadcasted_iota(jnp.int32, dmix.shape, 0)
        lane, win = _pool_lane_windows(dmix.shape)
        e = dmix / jnp.minimum(row + 1, win).astype(F32)

        def shift(a, k):
            return jnp.where(row < S - k, pltpu.roll(a, S - k, 0), 0.0)

        f2 = e + shift(e, 1)
        f4 = f2 + shift(f2, 2)
        f8 = f4 + shift(f4, 4)
        f16 = f8 + shift(f8, 8)
        du_ref[...] = jnp.where(lane < 64, f2, jnp.where(lane < 128, f4, jnp.where(lane < 192, f8, f16))) - dmix

    vm = pl.BlockSpec(memory_space=pltpu.VMEM)
    return pl.pallas_call(
        body,
        name="pool_bwd",
        in_specs=[vm] * 5,
        out_specs=[vm] * 4,
        out_shape=[jax.ShapeDtypeStruct((S, POOL_W), F32), jax.ShapeDtypeStruct((POOL_W, POOL_W), F32),
                   jax.ShapeDtypeStruct((1, POOL_W), F32), jax.ShapeDtypeStruct((1, POOL_W), F32)],
        compiler_params=_params(),
    )(dpool, mixed, wbd, b, scale)


def inproj_bwd(dqkv, du, x, dx1, w_in_g, g, mod6, tc, tsa, tsb, h1, tm=512):
    S = x.shape[0]

    def body(d0, d1, d2, du_ref, x_ref, dx1_ref, w_ref, g_ref, mod_ref, tc_ref, tsa_ref, tsb_ref, h_ref,
             gx_ref, dsh_ref, dsc_ref, dg_ref, dw_ref, s4, s16, dp_ref):
        @pl.when(pl.program_id(0) == 0)
        def _():
            _zero(dw_ref, dsh_ref, dsc_ref, dg_ref)

        cs, sa, sb = tc_ref[...], tsa_ref[...], tsb_ref[...]
        for d, src, dst in ((4, d1, s4), (16, d2, s16)):
            for kind in range(3):
                for r in range(d):
                    for h in range(2):
                        dst[kind, h, pl.ds(r, tm // d, stride=d), :] = src[kind, r, :, h * 128:(h + 1) * 128]
        for sp in range(20):
            piece, half = sp // 2, sp % 2
            lanes = slice(half * 128, (half + 1) * 128)
            if piece == 0:
                blk = du_ref[:, lanes]
            else:
                kind, gi = (piece - 1) // 3, (piece - 1) % 3
                blk = d0[kind, 0, :, lanes] if gi == 0 else (s4, s16)[gi - 1][kind, half]
                if kind == 0:
                    blk = _rope128(blk, cs, sa, sb, -1.0) * (HEAD_DIM ** -0.5)
                elif kind == 1:
                    blk = _rope128(blk, cs, sa, sb, -1.0)
            dp_ref[:, sp * 128:(sp + 1) * 128] = blk.astype(BF16)
        dh = jnp.zeros((tm, D_MODEL), F32)
        hbt = h_ref[...].T
        for j in range(N_CHIPS):
            dpj = dp_ref[:, j * 640:(j + 1) * 640]
            dh = dh + lax.dot_general(dpj, w_ref[j], NT, preferred_element_type=F32)
            dw_ref[j] += jnp.dot(hbt, dpj, preferred_element_type=F32)
        xv = x_ref[...]
        rstd = lax.rsqrt(jnp.mean(xv * xv, axis=-1, keepdims=True) + NORM_EPS)
        n1 = xv * rstd
        gv = g_ref[...]
        one_sc = 1.0 + mod_ref[1:2, :]
        dsh_ref[...] += _colsum(dh)
        dsc_ref[...] += _colsum(dh * (n1 * gv))
        dg_ref[...] += _colsum(dh * one_sc * n1)
        dn = dh * (gv * one_sc)
        gx_ref[...] = dx1_ref[...] + rstd * (dn - n1 * jnp.mean(dn * n1, axis=-1, keepdims=True))

    vec = _full((1, D_MODEL))
    dspec = lambda d: pl.BlockSpec((3, d, tm // d, GROUP_W), lambda i: (0, 0, i, 0))
    return pl.pallas_call(
        body,
        name="inproj_bwd",
        grid=(S // tm,),
        in_specs=[dspec(d) for d in DILATIONS] + [_rows(tm, POOL_W), _rows(tm, D_MODEL), _rows(tm, D_MODEL), _full(w_in_g.shape),
                                                  vec, _full((6, D_MODEL)), _rows(tm, 128), _rows(tm, 128), _rows(tm, 128),
                                                  _rows(tm, D_MODEL)],
        out_specs=[_rows(tm, D_MODEL), vec, vec, vec, _full(w_in_g.shape)],
        out_shape=[jax.ShapeDtypeStruct((S, D_MODEL), F32)] + [jax.ShapeDtypeStruct((1, D_MODEL), F32)] * 3
        + [jax.ShapeDtypeStruct(w_in_g.shape, F32)],
        scratch_shapes=[pltpu.VMEM((3, 2, tm, 128), F32)] * 2 + [pltpu.VMEM((tm, IN_W), BF16)],
        compiler_params=_params(1),
    )(*dqkv, du, x, dx1, w_in_g, g, mod6, tc, tsa, tsb, h1)


def _adamw(w, g, m, v):
    m = ADAM_B1 * m + (1.0 - ADAM_B1) * g
    v = ADAM_B2 * v + (1.0 - ADAM_B2) * (g * g)
    m_hat = m / (1.0 - ADAM_B1 ** ADAM_STEP)
    v_hat = v / (1.0 - ADAM_B2 ** ADAM_STEP)
    delta = -ADAM_LR * (m_hat / (jnp.sqrt(v_hat) + ADAM_EPS) + ADAM_WD * w)
    return delta, m, v


def adamw_rows(w, g, m, v, tr, name):
    R, C = w.shape

    def body(w_ref, g_ref, m_ref, v_ref, go_ref, d_ref, mo_ref, vo_ref):
        g = g_ref[...]
        go_ref[...] = g
        d_ref[...], mo_ref[...], vo_ref[...] = _adamw(w_ref[...], g, m_ref[...], v_ref[...])

    spec = pl.BlockSpec((tr, C), lambda i: (i, 0))
    return pl.pallas_call(
        body,
        name=name,
        grid=(R // tr,),
        in_specs=[spec] * 4,
        out_specs=[spec] * 4,
        out_shape=[jax.ShapeDtypeStruct((R, C), F32)] * 4,
        compiler_params=_params(1),
    )(w, g, m, v)


def adamw_ada(c_all_t, dmod_cols, w, m, v, tr=256):
    R, C = w.shape

    def body(ct_ref, dm_ref, w_ref, m_ref, v_ref, g_ref, d_ref, mo_ref, vo_ref):
        ct = ct_ref[...]
        act = ct * jax.nn.sigmoid(ct)
        dm = dm_ref[...]
        a_hi, d_hi = act.astype(BF16), dm.astype(BF16)
        a_lo, d_lo = (act - a_hi.astype(F32)).astype(BF16), (dm - d_hi.astype(F32)).astype(BF16)
        g = (jnp.dot(a_hi, d_hi, preferred_element_type=F32) + jnp.dot(a_lo, d_hi, preferred_element_type=F32)
             + jnp.dot(a_hi, d_lo, preferred_element_type=F32))
        g_ref[...] = g
        d_ref[...], mo_ref[...], vo_ref[...] = _adamw(w_ref[...], g, m_ref[...], v_ref[...])

    spec = pl.BlockSpec((tr, C), lambda i: (i, 0))
    return pl.pallas_call(
        body,
        name="adamw_ada",
        grid=(R // tr,),
        in_specs=[pl.BlockSpec((tr, N_DEV), lambda i: (i, 0)), _full((N_DEV, C)), spec, spec, spec],
        out_specs=[spec] * 4,
        out_shape=[jax.ShapeDtypeStruct((R, C), F32)] * 4,
        compiler_params=_params(1),
    )(c_all_t, dmod_cols, w, m, v)


def adamw_small(slab_a, slab_b, convw_g, wpool_g, params):
    names = ["b_ada", "g_pre_mix", "g_post_mix", "g_pre_ffn", "g_post_ffn", "b_pool", "pool_scale", "conv_b", "conv_w", "w_pool"]
    flat = []
    for n in names:
        flat += list(params[n])

    def body(a_ref, b_ref, cw_ref, wp_ref, *rest):
        ins, outs = rest[:30], rest[30:]

        def dev_sum(ref):
            t = ref[0]
            for dev in range(1, N_DEV):
                t = t + ref[dev]
            return t

        sa, sb_, scw, swp = dev_sum(a_ref), dev_sum(b_ref), dev_sum(cw_ref), dev_sum(wp_ref)
        grads = [
            jnp.concatenate([sa[k:k + 1, :] for k in range(6)], axis=1),
            sa[6:7, :], sa[7:8, :], sa[8:9, :], sa[9:10, :],
            sa[10:11, 0:256], sa[10:11, 256:512],
            sb_[3:4, :], scw, swp,
        ]
        for i, g in enumerate(grads):
            w_ref, m_ref, v_ref = ins[3 * i:3 * i + 3]
            if names[i] == "b_pool":
                parts = [((0, slice(grp, grp + 1)), g[:, grp * 64:(grp + 1) * 64]) for grp in range(4)]
            elif names[i] == "w_pool":
                parts = [((0, grp), g[grp * 64:(grp + 1) * 64, :]) for grp in range(4)]
            elif names[i] == "conv_w":
                parts = [((0,), g)]
            else:
                parts = [((Ellipsis,), g)]
            for at, gp in parts:
                d, mo, vo = _adamw(w_ref[at], gp, m_ref[at], v_ref[at])
                for k, val in enumerate((gp, d, mo, vo)):
                    outs[4 * i + k][at] = val
        outs[-1][...] = sa[10:11, 512:640]

    vm = pl.BlockSpec(memory_space=pltpu.VMEM)
    out_shape = []
    for n in names:
        out_shape += [jax.ShapeDtypeStruct(params[n][0].shape, F32)] * 4
    out_shape.append(jax.ShapeDtypeStruct((1, 128), F32))
    outs = pl.pallas_call(
        body,
        name="adamw_small",
        in_specs=[vm] * (4 + len(flat)),
        out_specs=[vm] * len(out_shape),
        out_shape=out_shape,
        compiler_params=_params(),
    )(slab_a, slab_b, convw_g, wpool_g, *flat)
    return {n: outs[4 * i:4 * i + 4] for i, n in enumerate(names)}, outs[-1]


def _place():
    return lax.axis_index("x"), lax.axis_index("y"), lax.axis_index("c")


def _other_chips(x, y):
    return [(1 - x, y), (x, 1 - y), (1 - x, 1 - y)]


def _chip_id(cx, cy):
    return 2 * cx + cy


HBM_SPEC = pl.BlockSpec(memory_space=pltpu.HBM)
SEM_SPEC = pl.BlockSpec(memory_space=pltpu.SEMAPHORE)
ANY_SPEC = pl.BlockSpec(memory_space=pl.ANY)
EFFECT = pltpu.SideEffectType.DATAFLOW_SIDE_EFFECTING


def _hbm(t):
    return pltpu.with_memory_space_constraint(t, pltpu.HBM)


def _hbm_shapes(ts):
    return [pltpu.HBM(t.shape, t.dtype) for t in ts]


def _half_rows(ref, lead, half, rh):
    return ref.at[lead, pl.ds(half * rh, rh), :]


def _flips():
    return [(fx, fy, fc) for fx in (0, 1) for fy in (0, 1) for fc in (0, 1)][1:]


def _flip(v, f):
    return v if f == 0 else 1 - v


def ada_mod(c3, w_ada, b_cols, conv_w):
    CB = w_ada.shape[1]

    def body(c_ref, w_hbm, b_ref, cw_ref, call_ref, mod_ref, cwall_ref, modall, send_sems, recv_sems, w_ref, w_sem):
        x, y, c = _place()
        me_dev = 4 * x + 2 * y + c
        me = _chip_id(x, y)
        w_load = pltpu.make_async_copy(w_hbm, w_ref, w_sem)
        w_load.start()
        call_ref[me_dev] = c_ref[0]
        cwall_ref[me] = cw_ref[...]
        sends = []
        for k, (cx, cy) in enumerate(_other_chips(x, y)):
            cp = pltpu.make_async_remote_copy(src_ref=cw_ref, dst_ref=cwall_ref.at[me], send_sem=send_sems.at[10 + k],
                                              recv_sem=recv_sems.at[10 + k], device_id=(cx, cy, c), device_id_type=MESH)
            cp.start()
            sends.append(cp)
        for k, (fx, fy, fc) in enumerate(_flips()):
            cp = pltpu.make_async_remote_copy(src_ref=c_ref.at[0], dst_ref=call_ref.at[me_dev], send_sem=send_sems.at[k],
                                              recv_sem=recv_sems.at[k],
                                              device_id=(_flip(x, fx), _flip(y, fy), _flip(c, fc)), device_id_type=MESH)
            cp.start()
            sends.append(cp)
        for k, (fx, fy, fc) in enumerate(_flips()):
            peer = 4 * _flip(x, fx) + 2 * _flip(y, fy) + _flip(c, fc)
            pltpu.make_async_remote_copy(src_ref=c_ref.at[0], dst_ref=call_ref.at[peer], send_sem=send_sems.at[k],
                                         recv_sem=recv_sems.at[k], device_id=(x, y, c), device_id_type=MESH).wait_recv()
        row = lax.broadcasted_iota(jnp.int32, (N_DEV, D_MODEL), 0)
        call = jnp.zeros((N_DEV, D_MODEL), F32)
        for dev in range(N_DEV):
            call = jnp.where(row == dev, call_ref[dev], call)
        act = call * jax.nn.sigmoid(call)
        w_load.wait()
        wv = w_ref[...]
        w_hi = wv.astype(BF16)
        w_lo = (wv - w_hi.astype(F32)).astype(BF16)
        a_hi = act.astype(BF16)
        a_lo = (act - a_hi.astype(F32)).astype(BF16)
        prod = (jnp.dot(a_hi, w_hi, preferred_element_type=F32) + jnp.dot(a_lo, w_hi, preferred_element_type=F32)
                + jnp.dot(a_hi, w_lo, preferred_element_type=F32))
        modall[me] = prod + b_ref[...]
        for k, (cx, cy) in enumerate(_other_chips(x, y)):
            cp = pltpu.make_async_remote_copy(src_ref=modall.at[me], dst_ref=modall.at[me], send_sem=send_sems.at[7 + k],
                                              recv_sem=recv_sems.at[7 + k], device_id=(cx, cy, c), device_id_type=MESH)
            cp.start()
            sends.append(cp)
        for k, (cx, cy) in enumerate(_other_chips(x, y)):
            blk = modall.at[_chip_id(cx, cy)]
            pltpu.make_async_remote_copy(src_ref=blk, dst_ref=blk, send_sem=send_sems.at[7 + k], recv_sem=recv_sems.at[7 + k],
                                         device_id=(x, y, c), device_id_type=MESH).wait_recv()
        for k, (cx, cy) in enumerate(_other_chips(x, y)):
            blk = cwall_ref.at[_chip_id(cx, cy)]
            pltpu.make_async_remote_copy(src_ref=blk, dst_ref=blk, send_sem=send_sems.at[10 + k], recv_sem=recv_sems.at[10 + k],
                                         device_id=(x, y, c), device_id_type=MESH).wait_recv()
        for cp in sends:
            cp.wait_send()
        mine = [modall[j, pl.ds(me_dev, 1), :] for j in range(N_CHIPS)]
        for r in range(6):
            pieces = []
            for h in range(2):
                pos = r * D_MODEL + h * 512
                pieces.append(mine[pos // CB][:, pos % CB:pos % CB + 512])
            mod_ref[r:r + 1, :] = jnp.concatenate(pieces, axis=1)

    vm = pl.BlockSpec(memory_space=pltpu.VMEM)
    return pl.pallas_call(
        body,
        name="ada_mod",
        in_specs=[vm, ANY_SPEC, vm, vm],
        out_specs=[vm] * 3,
        out_shape=[jax.ShapeDtypeStruct((N_DEV, 1, D_MODEL), F32), jax.ShapeDtypeStruct((6, D_MODEL), F32),
                   jax.ShapeDtypeStruct((N_CHIPS,) + conv_w.shape, F32)],
        scratch_shapes=[pltpu.VMEM((N_CHIPS, N_DEV, CB), F32), pltpu.SemaphoreType.DMA((13,)), pltpu.SemaphoreType.DMA((13,)),
                        pltpu.VMEM(w_ada.shape, F32), pltpu.SemaphoreType.DMA],
        compiler_params=pltpu.CompilerParams(has_side_effects=True, vmem_limit_bytes=VMEM_LIMIT),
    )(c3, w_ada, b_cols, conv_w)


def split_start(name, bufs, plan, n_sem, carry):
    nb = len(bufs)
    many = isinstance(carry, (list, tuple))
    alls = list(bufs) + (list(carry) if many else [carry])
    na = len(alls)

    def body(*refs):
        x, y, c = _place()
        ssem, rsem = refs[na], refs[na + 1]
        for i, (src, dst, dev) in enumerate(plan(refs[:nb], x, y, c)):
            pltpu.make_async_remote_copy(src_ref=src, dst_ref=dst, send_sem=ssem.at[i], recv_sem=rsem.at[i], device_id=dev,
                                         device_id_type=MESH).start()

    outs = pl.pallas_call(
        body,
        name=name,
        out_shape=[pltpu.SemaphoreType.DMA((n_sem,)), pltpu.SemaphoreType.DMA((n_sem,))] + _hbm_shapes(alls),
        in_specs=[HBM_SPEC] * na,
        out_specs=[SEM_SPEC, SEM_SPEC] + [HBM_SPEC] * na,
        input_output_aliases={i: 2 + i for i in range(na)},
        compiler_params=pltpu.CompilerParams(has_side_effects=EFFECT),
    )(*[_hbm(t) for t in alls])
    return outs[0], outs[1], list(outs[2:2 + nb]), (list(outs[2 + nb:]) if many else outs[-1])


def split_wait(name, ssem, rsem, bufs, plan, after):
    nb = len(bufs)

    def body(*refs):
        x, y, c = _place()
        s_ref, r_ref = refs[nb], refs[nb + 1]
        for i, (src, dst, dev) in enumerate(plan(refs[:nb], x, y, c)):
            cp = pltpu.make_async_remote_copy(src_ref=src, dst_ref=dst, send_sem=s_ref.at[i], recv_sem=r_ref.at[i], device_id=dev,
                                              device_id_type=MESH)
            cp.wait_send()
            cp.wait_recv()

    outs = pl.pallas_call(
        body,
        name=name,
        out_shape=_hbm_shapes(bufs),
        in_specs=[HBM_SPEC] * nb + [SEM_SPEC, SEM_SPEC, ANY_SPEC],
        out_specs=[HBM_SPEC] * nb,
        input_output_aliases={i: i for i in range(nb)},
        compiler_params=pltpu.CompilerParams(has_side_effects=EFFECT),
    )(*bufs, ssem, rsem, after)
    return list(outs)


def _gather_ici_plan(n):
    def plan(refs, x, y, c):
        out = []
        for w in range(n):
            rh = refs[w].shape[0] // 2
            for cx, cy in _other_chips(x, y):
                out.append((refs[w].at[pl.ds(c * rh, rh), :], _half_rows(refs[n + w], _chip_id(x, y), c, rh), (cx, cy, c)))
        return out

    return plan


def _gather_d2d_plan(n):
    def plan(refs, x, y, c):
        out = []
        for w in range(n):
            rh = refs[w].shape[1] // 2
            for cx, cy in _other_chips(x, y):
                blk = _half_rows(refs[w], _chip_id(cx, cy), c, rh)
                out.append((blk, blk, (x, y, 1 - c)))
        return out

    return plan


def _dev_id(x, y, c):
    return 4 * x + 2 * y + c


def _small_ici_plan(n):
    def plan(refs, x, y, c):
        out = []
        for w in range(n):
            dst = refs[n + w].at[_dev_id(x, y, c)]
            out.append((refs[w], dst, (x, y, 1 - c)))
            for cx, cy in _other_chips(x, y):
                out.append((refs[w], dst, (cx, cy, c)))
        return out

    return plan


def _small_d2d_plan(n):
    def plan(refs, x, y, c):
        out = []
        for w in range(n):
            for cx, cy in _other_chips(x, y):
                blk = refs[w].at[_dev_id(cx, cy, c)]
                out.append((blk, blk, (x, y, 1 - c)))
        return out

    return plan


def _rs_d2d_plan(n):
    def plan(refs, x, y, c):
        out = []
        for w in range(n):
            rh = refs[w].shape[1] // 2
            out.append((refs[w].at[:, pl.ds((1 - c) * rh, rh), :], refs[n + w], (x, y, 1 - c)))
        return out

    return plan


def _rs_ici_plan(n):
    def plan(refs, x, y, c):
        out = []
        for w in range(n):
            for k, (cx, cy) in enumerate(_other_chips(x, y)):
                out.append((refs[w].at[_chip_id(cx, cy)], refs[n + w].at[k], (cx, cy, c)))
        return out

    return plan


def _rs_share_plan(n):
    def plan(refs, x, y, c):
        out = []
        for w in range(n):
            rh = refs[w].shape[0] // 2
            rows = refs[w].at[pl.ds(c * rh, rh), :]
            out.append((rows, rows, (x, y, 1 - c)))
        return out

    return plan


def rs_add(grad, sibbuf, place, tr, name):
    _, R, C = grad.shape
    nt = (R // 2) // tr

    def body(p_ref, g_ref, s_ref, o_ref):
        o_ref[...] = (g_ref[...] + s_ref[...]).astype(BF16)

    return pl.pallas_call(
        body,
        name=name,
        grid_spec=pltpu.PrefetchScalarGridSpec(
            num_scalar_prefetch=1,
            grid=(N_CHIPS, nt),
            in_specs=[pl.BlockSpec((None, tr, C), lambda j, i, p: (j, p[0] * nt + i, 0)),
                      pl.BlockSpec((None, tr, C), lambda j, i, p: (j, i, 0))],
            out_specs=pl.BlockSpec((None, tr, C), lambda j, i, p: (j, i, 0)),
        ),
        out_shape=jax.ShapeDtypeStruct((N_CHIPS, R // 2, C), BF16),
        compiler_params=_params(2),
    )(place, grad, sibbuf)


def rs_final(grad, sibbuf, rbuf, place, tr, name):
    _, R, C = grad.shape
    nt = (R // 2) // tr

    def body(p_ref, g_ref, s_ref, r_ref, o_ref):
        o_ref[...] = (((g_ref[...] + s_ref[...]) + r_ref[0].astype(F32)) + r_ref[1].astype(F32)) + r_ref[2].astype(F32)

    return pl.pallas_call(
        body,
        name=name,
        grid_spec=pltpu.PrefetchScalarGridSpec(
            num_scalar_prefetch=1,
            grid=(nt,),
            in_specs=[pl.BlockSpec((None, tr, C), lambda i, p: (p[1], p[0] * nt + i, 0)),
                      pl.BlockSpec((None, tr, C), lambda i, p: (p[1], i, 0)),
                      pl.BlockSpec((3, tr, C), lambda i, p: (0, i, 0))],
            out_specs=pl.BlockSpec((tr, C), lambda i, p: (p[0] * nt + i, 0)),
        ),
        out_shape=jax.ShapeDtypeStruct((R, C), F32),
        compiler_params=_params(1),
    )(place, grad, sibbuf, rbuf)


class GradReduce:
    def __init__(self, tag, grads, rows, place):
        self.tag, self.grads, self.rows, self.place = tag, grads, rows, place
        self.n = len(grads)

    def d2d_start(self, carry):
        sib = [lax.empty((N_CHIPS, g.shape[1] // 2, g.shape[2]), F32) for g in self.grads]
        self.s1, self.r1, bufs, carry = split_start(f"rs_{self.tag}_d2d_start", self.grads + sib, _rs_d2d_plan(self.n), self.n, carry)
        self.bufs1 = bufs
        return carry

    def add_and_ici_start(self, after, carry):
        bufs = split_wait(f"rs_{self.tag}_d2d_wait", self.s1, self.r1, self.bufs1, _rs_d2d_plan(self.n), after)
        self.grads, self.sib = bufs[:self.n], bufs[self.n:]
        pb = [rs_add(g, s, self.place, tr, f"rs_{self.tag}_add{w}")
              for w, (g, s, tr) in enumerate(zip(self.grads, self.sib, self.rows))]
        rb = [lax.empty((3,) + p.shape[1:], BF16) for p in pb]
        self.s2, self.r2, self.bufs2, carry = split_start(f"rs_{self.tag}_ici_start", pb + rb, _rs_ici_plan(self.n), 3 * self.n, carry)
        return carry

    def final_and_share_start(self, after, carry):
        bufs = split_wait(f"rs_{self.tag}_ici_wait", self.s2, self.r2, self.bufs2, _rs_ici_plan(self.n), after)
        rb = bufs[self.n:]
        full = [rs_final(g, s, r, self.place, tr, f"rs_{self.tag}_final{w}")
                for w, (g, s, r, tr) in enumerate(zip(self.grads, self.sib, rb, self.rows))]
        self.s3, self.r3, self.bufs3, carry = split_start(f"rs_{self.tag}_share_start", full, _rs_share_plan(self.n), self.n, carry)
        return carry

    def finish(self, after):
        return split_wait(f"rs_{self.tag}_share_wait", self.s3, self.r3, self.bufs3, _rs_share_plan(self.n), after)


def _rope_tables(positions):
    inv_freq = ROPE_THETA ** (-jnp.arange(0, ROT_DIM, 2, dtype=F32) / ROT_DIM)
    ang = positions.astype(F32)[:, None] * inv_freq
    cos, sin = jnp.cos(ang), jnp.sin(ang)
    S = positions.shape[0]
    one, zero = jnp.ones((S, 48), F32), jnp.zeros((S, 48), F32)
    z8 = jnp.zeros((S, 8), F32)
    tc = jnp.concatenate([cos, cos, one], axis=1)
    tsa = jnp.concatenate([z8, sin, zero], axis=1)
    tsb = jnp.concatenate([-sin, z8, zero], axis=1)
    return tuple(jnp.tile(t, (1, 2)) for t in (tc, tsa, tsb))


def _block_diag(w_pool):
    wbd = jnp.zeros((POOL_W, POOL_W), F32)
    for gi in range(4):
        wbd = wbd.at[gi * 64:(gi + 1) * 64, gi * 64:(gi + 1) * 64].set(w_pool[gi])
    return wbd


def kernel(x, c, positions, w_ada, b_ada, g_pre_mix, g_post_mix, g_pre_ffn, g_post_ffn, w_in, w_pool, b_pool, pool_scale, w_out, w_up, conv_w, conv_b, w_down, loss_target, m_w_ada, m_b_ada, m_g_pre_mix, m_g_post_mix, m_g_pre_ffn, m_g_post_ffn, m_w_in, m_w_pool, m_b_pool, m_pool_scale, m_w_out, m_w_up, m_conv_w, m_conv_b, m_w_down, v_w_ada, v_b_ada, v_g_pre_mix, v_g_post_mix, v_g_pre_ffn, v_g_post_ffn, v_w_in, v_w_pool, v_b_pool, v_pool_scale, v_w_out, v_w_up, v_conv_w, v_conv_b, v_w_down):
    xi, yi, ci = lax.axis_index("x"), lax.axis_index("y"), lax.axis_index("c")
    chip = 2 * xi + yi
    place = jnp.stack([ci, chip]).astype(jnp.int32)
    x2, tgt = x[0], loss_target[0]
    S = x2.shape[0]

    def landing(s_):
        return lax.dynamic_update_slice(lax.empty((N_CHIPS,) + s_.shape, s_.dtype), s_[None], (chip, 0, 0))

    cb_ada = w_ada.shape[2]
    b_cols = lax.dynamic_slice(b_ada, (0, chip * cb_ada), (1, cb_ada))
    c_all, mod6, conv_w_g = ada_mod(c.reshape(1, 1, D_MODEL), w_ada[0], b_cols, conv_w[0])
    conv_w_f = jnp.transpose(conv_w_g, (1, 0, 2)).reshape(3, D_FF)
    mix_sh = [w_in[0].astype(BF16), w_out[0].astype(BF16)]
    ffn_sh = [w_up[0].astype(BF16), w_down[0].astype(BF16)]
    ga_s, ga_r, ga_bufs, mod6 = split_start("gather_mix_ici_start", mix_sh + [landing(t) for t in mix_sh], _gather_ici_plan(2), 6, mod6)
    gb_s, gb_r, gb_bufs, (mod6, tc, tsa, tsb) = split_start("gather_ffn_ici_start", ffn_sh + [landing(t) for t in ffn_sh],
                                                            _gather_ici_plan(2), 6, [mod6, *_rope_tables(positions[0])])
    wbd = _block_diag(w_pool[0]).astype(BF16)
    b_pool2, scale2 = b_pool.reshape(1, POOL_W), pool_scale
    ga_bufs = split_wait("gather_mix_ici_wait", ga_s, ga_r, ga_bufs, _gather_ici_plan(2), mod6)
    gc_s, gc_r, mix_land, mod6 = split_start("gather_mix_d2d_start", ga_bufs[2:], _gather_d2d_plan(2), 6, mod6)
    w_in_g, w_out_g = split_wait("gather_mix_d2d_wait", gc_s, gc_r, mix_land, _gather_d2d_plan(2), mod6)

    h1, u, *qkv = inproj_fwd(x2, g_pre_mix, mod6, w_in_g, tc, tsa, tsb)
    mixed, pool = pool_fwd(u, wbd, b_pool2, scale2)
    o_l = [attn_fwd(t, d) for t, d in zip(qkv, DILATIONS)]
    attn_done = sum(l[0, :8, :128] for _, l in o_l)
    gb_bufs = split_wait("gather_ffn_ici_wait", gb_s, gb_r, gb_bufs, _gather_ici_plan(2), attn_done)
    gd_s, gd_r, ffn_land, pool = split_start("gather_ffn_d2d_start", gb_bufs[2:], _gather_d2d_plan(2), 6, pool)
    cat, lse, lse4, lse16, y1, x1, h2 = outproj_fwd([o for o, _ in o_l] + [l for _, l in o_l], pool, x2, w_out_g, g_post_mix,
                                                    g_pre_ffn, mod6)
    lses = [lse[None], lse4, lse16]
    w_up_g, w_down_g = split_wait("gather_ffn_d2d_wait", gd_s, gd_r, ffn_land, _gather_d2d_plan(2), h2)
    w_down_f = w_down_g.reshape(D_FF, D_MODEL)
    gate, val, dy2, dout, loss_v, d_gt_f, d_g_post_ffn = ffn_fwd(h2, w_up_g, conv_w_f, conv_b, w_down_f, x1, tgt, g_post_ffn, mod6)

    dgc, dval, d_conv_w, d_conv_b, dw_down, dw_up = down_bwd(dy2, w_down_f, gate, val, conv_w_f, conv_b, h2)
    dx1, dy1, d_sh_f, d_sc_f, d_g_pre_ffn, d_gt_m, d_g_post_mix, dw_up = up_bwd(
        dgc, dval, conv_w_f, w_up_g, x1, dout, y1, g_pre_ffn, g_post_mix, mod6, h2, dw_up)
    rs_ffn = GradReduce("ffn", [dw_up, dw_down.reshape(N_CHIPS, D_FF // N_CHIPS, D_MODEL)], [256, 176], place)
    dy1 = rs_ffn.d2d_start(dy1)
    dpool, da1, da4, da16, dl1, dl4, dl16, dw_out = outproj_bwd(dy1, w_out_g, cat)
    dpool = rs_ffn.add_and_ici_start(dw_out, dpool)
    du, d_wbd, d_b_pool, d_scale = pool_bwd(dpool, mixed, wbd, b_pool2, scale2)
    dqkv = [attn_bwd(t, da, ls, dl, d) for t, da, ls, dl, d in zip(qkv, (da1[None], da4, da16), lses, (dl1[None], dl4, dl16), DILATIONS)]
    grad_x, d_sh_m, d_sc_m, d_g_pre_mix, dw_in = inproj_bwd(dqkv, du, x2, dx1, w_in_g, g_pre_mix, mod6, tc, tsa, tsb, h1)

    z1 = jnp.zeros((1, D_MODEL), F32)
    slab_a = jnp.concatenate(
        [d_sh_m, d_sc_m, d_gt_m, d_sh_f, d_sc_f, d_gt_f, d_g_pre_mix, d_g_post_mix, d_g_pre_ffn, d_g_post_ffn,
         jnp.concatenate([d_b_pool, d_scale, loss_v, jnp.zeros((1, 384), F32)], axis=1)] + [z1] * 5, axis=0)
    slab_b = jnp.concatenate([d_conv_w, d_conv_b, jnp.zeros((4, D_FF), F32)], axis=0)
    d_wpool = jnp.concatenate([d_wbd[gi * 64:(gi + 1) * 64, gi * 64:(gi + 1) * 64] for gi in range(4)], axis=0)
    dev = _dev_id(xi, yi, ci)
    small_src = [slab_a, slab_b, d_wpool]
    small_land = [lax.dynamic_update_slice(lax.empty((N_DEV,) + t.shape, F32), t[None], (dev, 0, 0)) for t in small_src]
    tok = jnp.zeros((8, 128), F32)
    gs_s, gs_r, gs_bufs, tok = split_start("small_ici_start", small_src + small_land, _small_ici_plan(3), 12, tok)
    rs_mix = GradReduce("mix", [dw_in, dw_out], [256, 256], place)
    tok = rs_mix.d2d_start(tok)
    tok = rs_ffn.final_and_share_start(tok, tok)
    gs_bufs = split_wait("small_ici_wait", gs_s, gs_r, gs_bufs, _small_ici_plan(3), tok)
    gt_s, gt_r, small_land, tok = split_start("small_d2d_start", gs_bufs[3:], _small_d2d_plan(3), 9, tok)
    tok = rs_mix.add_and_ici_start(tok, tok)
    slab_a_g, slab_b_g, wpool_g = split_wait("small_d2d_wait", gt_s, gt_r, small_land, _small_d2d_plan(3), tok)
    cw_cols = conv_w.shape[2]
    convw_g = lax.dynamic_slice(slab_b_g, (0, 0, chip * cw_cols), (N_DEV, 3, cw_cols))
    dmod_cols = lax.dynamic_slice(slab_a_g[:, :6, :].reshape(N_DEV, 6 * D_MODEL), (0, chip * cb_ada), (N_DEV, cb_ada))

    res = {}

    def big_adamw(name, w, g, m, v, tr):
        g_, d_, m_, v_ = adamw_rows(w[0], g, m[0], v[0], tr, "adamw_" + name)
        res[name] = (g_[None], d_[None], m_[None], v_[None])
        return v_

    g_ada, d_ada, m_ada, v_ada = adamw_ada(c_all.reshape(N_DEV, D_MODEL).T, dmod_cols, w_ada[0], m_w_ada[0], v_w_ada[0])
    res["w_ada"] = (g_ada[None], d_ada[None], m_ada[None], v_ada[None])
    g_w_up, g_w_down = rs_ffn.finish(v_ada)
    big_adamw("w_up", w_up, g_w_up, m_w_up, v_w_up, 256)
    last = big_adamw("w_down", w_down, g_w_down, m_w_down, v_w_down, 352)
    rs_mix.final_and_share_start(last, jnp.zeros((8, 128), F32))
    g_w_in, g_w_out = rs_mix.finish(last)
    big_adamw("w_in", w_in, g_w_in, m_w_in, v_w_in, 256)
    big_adamw("w_out", w_out, g_w_out, m_w_out, v_w_out, 256)
    small, loss_sum = adamw_small(slab_a_g, slab_b_g, convw_g, wpool_g, {
        "b_ada": (b_ada, m_b_ada, v_b_ada), "g_pre_mix": (g_pre_mix, m_g_pre_mix, v_g_pre_mix),
        "g_post_mix": (g_post_mix, m_g_post_mix, v_g_post_mix), "g_pre_ffn": (g_pre_ffn, m_g_pre_ffn, v_g_pre_ffn),
        "g_post_ffn": (g_post_ffn, m_g_post_ffn, v_g_post_ffn), "b_pool": (b_pool, m_b_pool, v_b_pool),
        "pool_scale": (pool_scale, m_pool_scale, v_pool_scale), "conv_b": (conv_b, m_conv_b, v_conv_b),
        "conv_w": (conv_w, m_conv_w, v_conv_w), "w_pool": (w_pool, m_w_pool, v_w_pool)})
    for name in ("b_ada", "g_pre_mix", "g_post_mix", "g_pre_ffn", "g_post_ffn", "pool_scale", "conv_b", "b_pool", "w_pool", "conv_w"):
        res[name] = tuple(small[name])

    loss = loss_sum[0, 0]
    order = ["w_ada", "b_ada", "g_pre_mix", "g_post_mix", "g_pre_ffn", "g_post_ffn", "w_in", "w_pool", "b_pool", "pool_scale",
             "w_out", "w_up", "conv_w", "conv_b", "w_down"]
    outs = [loss, grad_x[None]]
    for k in range(4):
        outs += [res[n][k] for n in order]
    return tuple(outs)
```

```python
import math

import jax
import jax.numpy as jnp
from jax import lax
from jax.experimental import pallas as pl
from jax.experimental.pallas import tpu as pltpu

F32 = jnp.float32
BF16 = jnp.bfloat16
MESH = pl.DeviceIdType.MESH

D_MODEL = 1024
HEAD_DIM = 64
POOL_W = 256
GROUP_W = 256
DILATIONS = (1, 4, 16)
ATT_BLOCK = 128
IN_W = 2560
D_FF = 2816
HALF_FF = 1408
ROT_DIM = 16
ROPE_THETA = 500000.0
NORM_EPS = 1e-6
N_CHIPS = 4
N_DEV = 8
NEG = -1e30

ADAM_LR = 0.001
ADAM_B1 = 0.9
ADAM_B2 = 0.999
ADAM_EPS = 1e-08
ADAM_WD = 0.01
ADAM_STEP = 10

VMEM_LIMIT = 56 * 1024 * 1024

NT = (((1,), (1,)), ((), ()))
TN = (((0,), (0,)), ((), ()))


def _params(n_grid=0, **kw):
    sem = ("arbitrary",) * n_grid if n_grid else None
    return pltpu.CompilerParams(dimension_semantics=sem, vmem_limit_bytes=VMEM_LIMIT, **kw)


def _full(shape):
    nd = len(shape)
    return pl.BlockSpec(tuple(shape), lambda *_: (0,) * nd, pipeline_mode=pl.Buffered(1))


def _rows(tm, ncol):
    return pl.BlockSpec((tm, ncol), lambda i: (i, 0))


def _zero(*refs):
    for ref in refs:
        ref[...] = jnp.zeros_like(ref)


def _colsum(v):
    return jnp.sum(v, axis=0, keepdims=True)


def _rope128(t, cs, sa, sb, sign):
    return t * cs + sign * (pltpu.roll(t, 8, 1) * sa + pltpu.roll(t, 120, 1) * sb)


FF_CHUNKS = tuple((ch, off, w) for ch in range(2) for off, w in ((0, 512), (512, 512), (1024, 384)))
GELU_C0 = math.sqrt(2.0 / math.pi)
GELU_C1 = GELU_C0 * 0.044715


def _gelu(z):
    z2 = z * z
    t = jnp.tanh(z * (GELU_C0 + GELU_C1 * z2))
    u = 0.5 * t + 0.5
    return z * u, u, t, z2


def _gelu_grad(z, u, t, z2):
    return u + (z * (GELU_C0 + (3.0 * GELU_C1) * z2)) * (0.5 - 0.5 * (t * t))


def _conv_taps(gate, halo, first):
    row = lax.broadcasted_iota(jnp.int32, gate.shape, 0)
    halo = jnp.where(first, 0.0, halo)
    nh = halo.shape[0]
    p1 = halo[nh - 1:nh, :]
    p2 = halo[nh - 2:nh - 1, :]
    g1 = jnp.where(row == 0, p1, pltpu.roll(gate, 1, 0))
    g2 = jnp.where(row == 0, p2, jnp.where(row == 1, p1, pltpu.roll(gate, 2, 0)))
    return g1, g2


def inproj_fwd(x, g, mod6, w_in_g, tc, tsa, tsb, tm=512):
    S = x.shape[0]

    def body(x_ref, g_ref, mod_ref, w_ref, tc_ref, tsa_ref, tsb_ref, h_ref, u_ref, q1_ref, q4_ref, q16_ref, scr):
        qkv_refs = (q1_ref, q4_ref, q16_ref)
        xv = x_ref[...]
        rstd = lax.rsqrt(jnp.mean(xv * xv, axis=-1, keepdims=True) + NORM_EPS)
        h = ((xv * rstd) * g_ref[...]) * (1.0 + mod_ref[1:2, :]) + mod_ref[0:1, :]
        hb = h.astype(BF16)
        h_ref[...] = hb
        cs, sa, sb = tc_ref[...], tsa_ref[...], tsb_ref[...]
        for j in range(N_CHIPS):
            res = jnp.dot(hb, w_ref[j], preferred_element_type=F32)
            for t in range(5):
                sp = 5 * j + t
                piece, half = sp // 2, sp % 2
                blk = res[:, t * 128:(t + 1) * 128]
                lanes = slice(half * 128, (half + 1) * 128)
                if piece == 0:
                    u_ref[:, lanes] = blk
                else:
                    kind, gi = (piece - 1) // 3, (piece - 1) % 3
                    if kind == 0:
                        blk = _rope128(blk, cs, sa, sb, 1.0) * (HEAD_DIM ** -0.5)
                    elif kind == 1:
                        blk = _rope128(blk, cs, sa, sb, 1.0)
                    d = DILATIONS[gi]
                    if d == 1:
                        q1_ref[kind, 0, :, lanes] = blk.astype(BF16)
                    else:
                        scr[...] = blk
                        for r in range(d):
                            qkv_refs[gi][kind, r, :, lanes] = scr[pl.ds(r, tm // d, stride=d), :].astype(BF16)

    cls = lambda d: pl.BlockSpec((3, d, tm // d, GROUP_W), lambda i: (0, 0, i, 0))
    return pl.pallas_call(
        body,
        name="inproj_fwd",
        grid=(S // tm,),
        in_specs=[_rows(tm, D_MODEL), _full((1, D_MODEL)), _full((6, D_MODEL)), _full(w_in_g.shape),
                  _rows(tm, 128), _rows(tm, 128), _rows(tm, 128)],
        out_specs=[_rows(tm, D_MODEL), _rows(tm, POOL_W)] + [cls(d) for d in DILATIONS],
        out_shape=[jax.ShapeDtypeStruct((S, D_MODEL), BF16), jax.ShapeDtypeStruct((S, POOL_W), F32)]
        + [jax.ShapeDtypeStruct((3, d, S // d, GROUP_W), BF16) for d in DILATIONS],
        scratch_shapes=[pltpu.VMEM((tm, 128), F32)],
        compiler_params=_params(1),
    )(x, g, mod6, w_in_g, tc, tsa, tsb)


def _attn_masks():
    row = lax.broadcasted_iota(jnp.int32, (2 * ATT_BLOCK, 2 * ATT_BLOCK), 0) % ATT_BLOCK
    col = lax.broadcasted_iota(jnp.int32, (2 * ATT_BLOCK, 2 * ATT_BLOCK), 1)
    band = (col >= row) & (col <= row + ATT_BLOCK)
    lane = lax.broadcasted_iota(jnp.int32, (ATT_BLOCK, 128), 1)
    return band, col, lane < HEAD_DIM


def _classes_per_step(d, nb):
    return min(d, max(1, 8 // nb))


def _stack_heads(t, lo):
    z = jnp.zeros_like(t)
    return jnp.concatenate([jnp.where(lo, t, z), jnp.where(lo, z, t)], axis=0)


def _unstack_heads(t2, lo):
    return jnp.where(lo, t2[:ATT_BLOCK], t2[ATT_BLOCK:])


def attn_fwd(qkv, d):
    L = qkv.shape[2]
    nb = L // ATT_BLOCK
    cpb = _classes_per_step(d, nb)

    def body(q_ref, k_ref, v_ref, o_ref, l_ref, kpad, vpad):
        for cls in range(cpb):
            kpad[cls, 0:ATT_BLOCK, :] = jnp.zeros((ATT_BLOCK, GROUP_W), BF16)
            vpad[cls, 0:ATT_BLOCK, :] = jnp.zeros((ATT_BLOCK, GROUP_W), BF16)
            kpad[cls, ATT_BLOCK:, :] = k_ref[cls]
            vpad[cls, ATT_BLOCK:, :] = v_ref[cls]
        band, col, lo = _attn_masks()

        def step(t, carry):
            cls, n = t // nb, t % nb
            r0 = pl.multiple_of(n * ATT_BLOCK, ATT_BLOCK)
            valid = band & ((col >= ATT_BLOCK) | (n > 0))
            qb = q_ref[cls, pl.ds(r0, ATT_BLOCK), :]
            kb = kpad[cls, pl.ds(r0, 2 * ATT_BLOCK), :]
            vb = vpad[cls, pl.ds(r0, 2 * ATT_BLOCK), :]
            for pair in range(2):
                lanes = slice(pair * 128, (pair + 1) * 128)
                qp, kp, vp = qb[:, lanes], kb[:, lanes], vb[:, lanes]
                s = lax.dot_general(_stack_heads(qp, lo), kp, NT, preferred_element_type=F32)
                s = jnp.where(valid, s, NEG)
                m = jnp.max(s, axis=1, keepdims=True)
                p = jnp.exp(s - m)
                den = jnp.sum(p, axis=1, keepdims=True)
                pv = jnp.dot(p.astype(BF16), vp, preferred_element_type=F32)
                o_ref[cls, pl.ds(r0, ATT_BLOCK), lanes] = _unstack_heads(pv / den, lo).astype(BF16)
                l_ref[cls, pl.ds(r0, ATT_BLOCK), lanes] = _unstack_heads(jnp.broadcast_to(m + jnp.log(den), pv.shape), lo)
            return carry

        lax.fori_loop(0, cpb * nb, step, 0, unroll=8)

    spec = lambda kind: pl.BlockSpec((None, cpb, L, GROUP_W), lambda r: (kind, r, 0, 0))
    return pl.pallas_call(
        body,
        name=f"attn_fwd_d{d}",
        grid=(d // cpb,),
        in_specs=[spec(0), spec(1), spec(2)],
        out_specs=[pl.BlockSpec((cpb, L, GROUP_W), lambda r: (r, 0, 0))] * 2,
        out_shape=[jax.ShapeDtypeStruct((d, L, GROUP_W), BF16), jax.ShapeDtypeStruct((d, L, GROUP_W), F32)],
        scratch_shapes=[pltpu.VMEM((cpb, L + ATT_BLOCK, GROUP_W), BF16)] * 2,
        compiler_params=_params(1),
    )(qkv, qkv, qkv)


def _pool_lane_windows(shape):
    lane = lax.broadcasted_iota(jnp.int32, shape, 1)
    return lane, jnp.where(lane < 64, 2, jnp.where(lane < 128, 4, jnp.where(lane < 192, 8, 16)))


def pool_fwd(u, wbd, b, scale):
    S = u.shape[0]

    def body(u_ref, w_ref, b_ref, s_ref, mixed_ref, out_ref):
        uv = u_ref[...]
        row = lax.broadcasted_iota(jnp.int32, uv.shape, 0)
        lane, win = _pool_lane_windows(uv.shape)

        def shift(a, k):
            return jnp.where(row >= k, pltpu.roll(a, k, 0), 0.0)

        s2 = uv + shift(uv, 1)
        s4 = s2 + shift(s2, 2)
        s8 = s4 + shift(s4, 4)
        s16 = s8 + shift(s8, 8)
        tsum = jnp.where(lane < 64, s2, jnp.where(lane < 128, s4, jnp.where(lane < 192, s8, s16)))
        cnt = jnp.minimum(row + 1, win).astype(F32)
        mb = (tsum / cnt - uv).astype(BF16)
        mixed_ref[...] = mb
        y = jnp.dot(mb, w_ref[...], preferred_element_type=F32) + b_ref[...]
        out_ref[...] = (y * s_ref[...]).astype(BF16)

    vm = pl.BlockSpec(memory_space=pltpu.VMEM)
    return pl.pallas_call(
        body,
        name="pool_fwd",
        in_specs=[vm] * 4,
        out_specs=[vm] * 2,
        out_shape=[jax.ShapeDtypeStruct((S, POOL_W), BF16)] * 2,
        compiler_params=_params(),
    )(u, wbd, b, scale)


def outproj_fwd(o_l, pool, x, w_out_g, g_post, g_pre, mod6, tm=512):
    S = x.shape[0]

    def body(o0, o1, o2, l0, l1, l2, pool_ref, x_ref, w_ref, gpost_ref, gpre_ref, mod_ref,
             cat_ref, lse_ref, lse4_ref, lse16_ref, y1_ref, x1_ref, h2_ref, so4, sl4, so16, sl16):
        for d, src, dst in ((4, o1, so4), (4, l1, sl4), (16, o2, so16), (16, l2, sl16)):
            for r in range(d):
                for h in range(2):
                    dst[h, pl.ds(r, tm // d, stride=d), :] = src[r, :, h * 128:(h + 1) * 128].astype(F32)
        nat = lambda ref: jnp.concatenate([ref[0], ref[1]], axis=1)
        a, b, c = l0[0], nat(sl4), nat(sl16)
        m = jnp.maximum(jnp.maximum(a, b), c)
        e0, e1, e2 = jnp.exp(a - m), jnp.exp(b - m), jnp.exp(c - m)
        z = e0 + e1 + e2
        lse = m + jnp.log(z)
        lse_ref[...] = lse
        for h in range(2):
            sl4[h] = lse[:, h * 128:(h + 1) * 128]
        for d, dst in ((4, lse4_ref), (16, lse16_ref)):
            for r in range(d):
                for h in range(2):
                    dst[r, :, h * 128:(h + 1) * 128] = sl4[h, pl.ds(r, tm // d, stride=d), :]
        attn = (e0 * o0[0].astype(F32) + e1 * nat(so4) + e2 * nat(so16)) / z
        cat = jnp.concatenate([pool_ref[...], attn.astype(BF16)], axis=1)
        cat_ref[...] = cat
        y1 = jnp.concatenate([jnp.dot(cat, w_ref[j], preferred_element_type=F32) for j in range(N_CHIPS)], axis=1)
        y1_ref[...] = y1.astype(BF16)
        rstd = lax.rsqrt(jnp.mean(y1 * y1, axis=-1, keepdims=True) + NORM_EPS)
        x1 = x_ref[...] + mod_ref[2:3, :] * ((y1 * rstd) * gpost_ref[...])
        x1_ref[...] = x1
        rstd2 = lax.rsqrt(jnp.mean(x1 * x1, axis=-1, keepdims=True) + NORM_EPS)
        h2 = ((x1 * rstd2) * gpre_ref[...]) * (1.0 + mod_ref[4:5, :]) + mod_ref[3:4, :]
        h2_ref[...] = h2.astype(BF16)

    t256 = _rows(tm, GROUP_W)
    cls = lambda d: pl.BlockSpec((d, tm // d, GROUP_W), lambda i: (0, i, 0))
    cls_shape = lambda d: jax.ShapeDtypeStruct((d, S // d, GROUP_W), F32)
    return pl.pallas_call(
        body,
        name="outproj_fwd",
        grid=(S // tm,),
        in_specs=[cls(d) for d in DILATIONS] * 2 + [t256, _rows(tm, D_MODEL), _full(w_out_g.shape), _full((1, D_MODEL)),
                                                    _full((1, D_MODEL)), _full((6, D_MODEL))],
        out_specs=[_rows(tm, 512), t256, cls(4), cls(16), _rows(tm, D_MODEL), _rows(tm, D_MODEL), _rows(tm, D_MODEL)],
        out_shape=[jax.ShapeDtypeStruct((S, 512), BF16), jax.ShapeDtypeStruct((S, GROUP_W), F32), cls_shape(4), cls_shape(16),
                   jax.ShapeDtypeStruct((S, D_MODEL), BF16), jax.ShapeDtypeStruct((S, D_MODEL), F32),
                   jax.ShapeDtypeStruct((S, D_MODEL), BF16)],
        scratch_shapes=[pltpu.VMEM((2, tm, 128), F32)] * 4,
        compiler_params=_params(1),
    )(*o_l, pool, x, w_out_g, g_post, g_pre, mod6)


def _halo_prev(tm, ncol):
    return pl.BlockSpec((16, ncol), lambda i: (jnp.maximum(i * (tm // 16) - 1, 0), 0))


def ffn_fwd(h2, w_up_g, conv_w, conv_b, w_down, x1, target, g_post, mod6, tm=512):
    S = x1.shape[0]

    def body(h_ref, wu_ref, cw_ref, cb_ref, wd_ref, x1_ref, tgt_ref, g_ref, mod_ref,
             gate_ref, val_ref, dy2_ref, dout_ref, loss_ref, dgt_ref, dg_ref, carry):
        first = pl.program_id(0) == 0

        @pl.when(first)
        def _():
            _zero(carry, loss_ref, dgt_ref, dg_ref)

        hb = h_ref[...]
        y2 = jnp.zeros((tm, D_MODEL), F32)
        for ch in range(2):
            cols = slice(ch * HALF_FF, (ch + 1) * HALF_FF)
            gt = jnp.dot(hb, wu_ref[ch], preferred_element_type=F32)
            gb = gt.astype(BF16)
            vb = jnp.dot(hb, wu_ref[2 + ch], preferred_element_type=F32).astype(BF16)
            gate_ref[:, cols] = gb
            val_ref[:, cols] = vb
            g1, g2 = _conv_taps(gt, carry[:, cols], first)
            carry[:, cols] = gt[tm - 8:, :]
            gc = g2 * cw_ref[0:1, cols] + g1 * cw_ref[1:2, cols] + gt * cw_ref[2:3, cols] + cb_ref[:, cols]
            ab = _gelu(gc.astype(BF16))[0] * vb
            y2 = y2 + jnp.dot(ab, wd_ref[cols, :], preferred_element_type=F32)
        rstd = lax.rsqrt(jnp.mean(y2 * y2, axis=-1, keepdims=True) + NORM_EPS)
        y2n = y2 * rstd
        gv = g_ref[...]
        gtf = mod_ref[5:6, :]
        r2 = y2n * gv
        diff = (x1_ref[...] + gtf * r2) - tgt_ref[...]
        loss_ref[...] += jnp.zeros((1, 128), F32) + 0.5 * jnp.sum(diff * diff) * (1.0 / D_MODEL)
        dout = diff * (1.0 / D_MODEL)
        dout_ref[...] = dout
        dgt_ref[...] += _colsum(dout * r2)
        dr2 = dout * gtf
        dg_ref[...] += _colsum(dr2 * y2n)
        dyn = dr2 * gv
        dy2 = rstd * (dyn - y2n * jnp.mean(dyn * y2n, axis=-1, keepdims=True))
        dy2_ref[...] = dy2.astype(BF16)

    vec = _full((1, D_MODEL))
    return pl.pallas_call(
        body,
        name="ffn_fwd",
        grid=(S // tm,),
        in_specs=[_rows(tm, D_MODEL), _full(w_up_g.shape), _full((3, D_FF)), _full((1, D_FF)), _full((D_FF, D_MODEL)),
                  _rows(tm, D_MODEL), _rows(tm, D_MODEL), vec, _full((6, D_MODEL))],
        out_specs=[_rows(tm, D_FF), _rows(tm, D_FF), _rows(tm, D_MODEL), _rows(tm, D_MODEL), _full((1, 128)), vec, vec],
        out_shape=[jax.ShapeDtypeStruct((S, D_FF), BF16)] * 2 + [jax.ShapeDtypeStruct((S, D_MODEL), BF16),
                                                                 jax.ShapeDtypeStruct((S, D_MODEL), F32),
                                                                 jax.ShapeDtypeStruct((1, 128), F32),
                                                                 jax.ShapeDtypeStruct((1, D_MODEL), F32),
                                                                 jax.ShapeDtypeStruct((1, D_MODEL), F32)],
        scratch_shapes=[pltpu.VMEM((8, D_FF), F32)],
        compiler_params=_params(1),
    )(h2, w_up_g, conv_w, conv_b, w_down, x1, target, g_post, mod6)


def down_bwd(dy2, w_down, gate, val, conv_w, conv_b, h2, tm=512):
    S = dy2.shape[0]

    def body(dy_ref, w_ref, gate_ref, halo_ref, val_ref, cw_ref, cb_ref, h_ref,
             dgc_ref, dval_ref, dcw_ref, dcb_ref, dwd_ref, dwu_ref):
        first = pl.program_id(1) == 0

        @pl.when(first)
        def _():
            dcw_ref[...] = jnp.zeros_like(dcw_ref)
            dcb_ref[...] = jnp.zeros_like(dcb_ref)
            dwd_ref[...] = jnp.zeros_like(dwd_ref)
            dwu_ref[...] = jnp.zeros_like(dwu_ref)

        dyb = dy_ref[...]
        hb = h_ref[...]
        pieces = [(off, w) for ch, off, w in FF_CHUNKS if ch == 0]

        def col(i):
            return slice(pieces[i][0], pieces[i][0] + pieces[i][1])

        def mm_da(i):
            return lax.dot_general(dyb, w_ref[col(i), :], NT, preferred_element_type=F32)

        def elementwise(i, da):
            cols = col(i)
            gt = gate_ref[:, cols].astype(F32)
            g1, g2 = _conv_taps(gt, halo_ref[:, cols].astype(F32), first)
            gc = g2 * cw_ref[0:1, cols] + g1 * cw_ref[1:2, cols] + gt * cw_ref[2:3, cols] + cb_ref[:, cols]
            zb, dab, vb = gc.astype(BF16), da.astype(BF16), val_ref[:, cols]
            ge, u, th, z2 = _gelu(zb)
            dgb = dab * vb * _gelu_grad(zb, u, th, z2)
            dgc_ref[:, cols] = dgb
            dgc = dgb.astype(F32)
            dvb = dab * ge
            dval_ref[:, cols] = dvb
            dcb_ref[:, cols] += _colsum(dgc)
            dcw_ref[0:1, cols] += _colsum(dgc * g2)
            dcw_ref[1:2, cols] += _colsum(dgc * g1)
            dcw_ref[2:3, cols] += _colsum(dgc * gt)
            return dvb, ge * vb

        def mm_dw(i, dvb_ab):
            dvb, ab = dvb_ab
            dwd_ref[col(i), :] += lax.dot_general(ab, dyb, TN, preferred_element_type=F32)
            dwu_ref[:, col(i)] += lax.dot_general(hb, dvb, TN, preferred_element_type=F32)

        n = len(pieces)
        da = mm_da(0)
        prev = None
        for i in range(n):
            nxt = mm_da(i + 1) if i + 1 < n else None
            if prev is not None:
                mm_dw(i - 1, prev)
            prev = elementwise(i, da)
            da = nxt
        mm_dw(n - 1, prev)

    one = pl.Buffered(1)
    tok = pl.BlockSpec((tm, D_MODEL), lambda c, i: (i, 0))
    ff = pl.BlockSpec((tm, HALF_FF), lambda c, i: (i, c))
    halo = pl.BlockSpec((16, HALF_FF), lambda c, i: (jnp.maximum(i * (tm // 16) - 1, 0), c))
    per_half = lambda rows: pl.BlockSpec((rows, HALF_FF), lambda c, i: (0, c), pipeline_mode=one)
    return pl.pallas_call(
        body,
        name="down_bwd",
        grid=(2, S // tm),
        in_specs=[tok, pl.BlockSpec((HALF_FF, D_MODEL), lambda c, i: (c, 0), pipeline_mode=one), ff, halo, ff,
                  per_half(3), per_half(1), tok],
        out_specs=[ff, ff, per_half(3), per_half(1), pl.BlockSpec((HALF_FF, D_MODEL), lambda c, i: (c, 0), pipeline_mode=one),
                   pl.BlockSpec((None, D_MODEL, HALF_FF), lambda c, i: (2 + c, 0, 0), pipeline_mode=one)],
        out_shape=[jax.ShapeDtypeStruct((S, D_FF), BF16), jax.ShapeDtypeStruct((S, D_FF), BF16),
                   jax.ShapeDtypeStruct((3, D_FF), F32), jax.ShapeDtypeStruct((1, D_FF), F32),
                   jax.ShapeDtypeStruct((D_FF, D_MODEL), F32), jax.ShapeDtypeStruct((N_CHIPS, D_MODEL, HALF_FF), F32)],
        compiler_params=_params(2),
    )(dy2, w_down, gate, gate, val, conv_w, conv_b, h2)


def up_bwd(dgc, dval, conv_w, w_up_g, x1, dout, y1, g_pre, g_post, mod6, h2, dw_up, tm=256):
    S = x1.shape[0]
    last_blk = S // 16 - 1

    def body(dgc_ref, nxt_ref, dval_ref, cw_ref, w_ref, x1_ref, dout_ref, y1_ref, gpre_ref, gpost_ref, mod_ref, h_ref, dwin_ref,
             dx1_ref, dy1_ref, dsh_ref, dsc_ref, dgpre_ref, dgt_ref, dgpost_ref, dwu_ref):
        last = pl.program_id(0) == pl.num_programs(0) - 1

        @pl.when(pl.program_id(0) == 0)
        def _():
            _zero(dwu_ref, dsh_ref, dsc_ref, dgpre_ref, dgt_ref, dgpost_ref)

        hb = h_ref[...]
        dh = jnp.zeros((tm, D_MODEL), F32)
        for ch in range(2):
            cols = slice(ch * HALF_FF, (ch + 1) * HALF_FF)
            dg = dgc_ref[:, cols].astype(F32)
            nx = jnp.where(last, 0.0, nxt_ref[:, cols].astype(F32))
            row = lax.broadcasted_iota(jnp.int32, dg.shape, 0)
            n0, n1 = nx[0:1, :], nx[1:2, :]
            u1 = jnp.where(row == tm - 1, n0, pltpu.roll(dg, tm - 1, 0))
            u2 = jnp.where(row == tm - 1, n1, jnp.where(row == tm - 2, n0, pltpu.roll(dg, tm - 2, 0)))
            dgate = (dg * cw_ref[2:3, cols] + u1 * cw_ref[1:2, cols] + u2 * cw_ref[0:1, cols]).astype(BF16)
            dwu_ref[ch] += lax.dot_general(hb, dgate, TN, preferred_element_type=F32)
            dh = dh + lax.dot_general(dgate, w_ref[ch], NT, preferred_element_type=F32)
            dh = dh + lax.dot_general(dval_ref[:, cols], w_ref[2 + ch], NT, preferred_element_type=F32)
        gpre = gpre_ref[...]
        one_sc = 1.0 + mod_ref[4:5, :]
        gpost = gpost_ref[...]
        gtm = mod_ref[2:3, :]
        for r0 in range(0, tm, tm // 2):
            rows = slice(r0, r0 + tm // 2)
            dhh = dh[rows, :]
            x1 = x1_ref[rows, :]
            rstd = lax.rsqrt(jnp.mean(x1 * x1, axis=-1, keepdims=True) + NORM_EPS)
            n2 = x1 * rstd
            dsh_ref[...] += _colsum(dhh)
            dsc_ref[...] += _colsum(dhh * (n2 * gpre))
            dgpre_ref[...] += _colsum(dhh * one_sc * n2)
            dn = dhh * (gpre * one_sc)
            dx1 = dout_ref[rows, :] + rstd * (dn - n2 * jnp.mean(dn * n2, axis=-1, keepdims=True))
            dx1_ref[rows, :] = dx1
            y1 = y1_ref[rows, :].astype(F32)
            rstd1 = lax.rsqrt(jnp.mean(y1 * y1, axis=-1, keepdims=True) + NORM_EPS)
            y1n = y1 * rstd1
            dgt_ref[...] += _colsum(dx1 * (y1n * gpost))
            dr1 = dx1 * gtm
            dgpost_ref[...] += _colsum(dr1 * y1n)
            dyn = dr1 * gpost
            dy1 = rstd1 * (dyn - y1n * jnp.mean(dyn * y1n, axis=-1, keepdims=True))
            dy1_ref[rows, :] = dy1.astype(BF16)

    vec = _full((1, D_MODEL))
    nxt = pl.BlockSpec((16, D_FF), lambda i: (jnp.minimum((i + 1) * (tm // 16), last_blk), 0))
    return pl.pallas_call(
        body,
        name="up_bwd",
        grid=(S // tm,),
        in_specs=[_rows(tm, D_FF), nxt, _rows(tm, D_FF), _full((3, D_FF)), _full(w_up_g.shape), _rows(tm, D_MODEL),
                  _rows(tm, D_MODEL), _rows(tm, D_MODEL), vec, vec, _full((6, D_MODEL)), _rows(tm, D_MODEL),
                  pl.BlockSpec(memory_space=pl.ANY)],
        out_specs=[_rows(tm, D_MODEL), _rows(tm, D_MODEL), vec, vec, vec, vec, vec,
                   pl.BlockSpec((2, D_MODEL, HALF_FF), lambda i: (0, 0, 0), pipeline_mode=pl.Buffered(1))],
        out_shape=[jax.ShapeDtypeStruct((S, D_MODEL), F32), jax.ShapeDtypeStruct((S, D_MODEL), BF16)]
        + [jax.ShapeDtypeStruct((1, D_MODEL), F32)] * 5 + [jax.ShapeDtypeStruct(dw_up.shape, F32)],
        input_output_aliases={12: 7},
        compiler_params=_params(1),
    )(dgc, dgc, dval, conv_w, w_up_g, x1, dout, y1, g_pre, g_post, mod6, h2, dw_up)


def outproj_bwd(dy1, w_out_g, cat, tm=512):
    S = dy1.shape[0]

    def body(dy_ref, w_ref, cat_ref, dpool_ref, dattn_ref, da4_ref, da16_ref, delta_ref, dl4_ref, dl16_ref, dw_ref, scr):
        @pl.when(pl.program_id(0) == 0)
        def _():
            dw_ref[...] = jnp.zeros_like(dw_ref)

        catb = cat_ref[...]
        dcat = jnp.zeros((tm, 512), F32)
        for j in range(N_CHIPS):
            dyj = dy_ref[:, j * 256:(j + 1) * 256]
            dcat = dcat + lax.dot_general(dyj, w_ref[j], NT, preferred_element_type=F32)
            dw_ref[j] += lax.dot_general(catb, dyj, TN, preferred_element_type=F32)
        dpool_ref[...] = dcat[:, :POOL_W]
        dattn = dcat[:, POOL_W:]
        dattn_ref[...] = dattn.astype(BF16)
        for h in range(2):
            scr[h] = dattn[:, h * 128:(h + 1) * 128]
        for d, dst in ((4, da4_ref), (16, da16_ref)):
            for r in range(d):
                for h in range(2):
                    dst[r, :, h * 128:(h + 1) * 128] = scr[h, pl.ds(r, tm // d, stride=d), :].astype(BF16)
        prod = dattn * catb[:, POOL_W:].astype(F32)
        r = lax.broadcasted_iota(jnp.int32, (GROUP_W, GROUP_W), 0) // HEAD_DIM
        c = lax.broadcasted_iota(jnp.int32, (GROUP_W, GROUP_W), 1) // HEAD_DIM
        ones_bd = jnp.where(r == c, 1.0, 0.0).astype(BF16)
        hi = prod.astype(BF16)
        lo = (prod - hi.astype(F32)).astype(BF16)
        delta = jnp.dot(hi, ones_bd, preferred_element_type=F32) + jnp.dot(lo, ones_bd, preferred_element_type=F32)
        delta_ref[...] = delta
        for h in range(2):
            scr[h] = delta[:, h * 128:(h + 1) * 128]
        for d, dst in ((4, dl4_ref), (16, dl16_ref)):
            for r in range(d):
                for h in range(2):
                    dst[r, :, h * 128:(h + 1) * 128] = scr[h, pl.ds(r, tm // d, stride=d), :]

    cls = lambda d: pl.BlockSpec((d, tm // d, GROUP_W), lambda i: (0, i, 0))
    cls_shape = lambda d, dt: jax.ShapeDtypeStruct((d, S // d, GROUP_W), dt)
    return pl.pallas_call(
        body,
        name="outproj_bwd",
        grid=(S // tm,),
        in_specs=[_rows(tm, D_MODEL), _full(w_out_g.shape), _rows(tm, 512)],
        out_specs=[_rows(tm, POOL_W), _rows(tm, GROUP_W), cls(4), cls(16), _rows(tm, GROUP_W), cls(4), cls(16),
                   _full(w_out_g.shape)],
        out_shape=[jax.ShapeDtypeStruct((S, POOL_W), F32), jax.ShapeDtypeStruct((S, GROUP_W), BF16), cls_shape(4, BF16),
                   cls_shape(16, BF16), jax.ShapeDtypeStruct((S, GROUP_W), F32), cls_shape(4, F32), cls_shape(16, F32),
                   jax.ShapeDtypeStruct(w_out_g.shape, F32)],
        scratch_shapes=[pltpu.VMEM((2, tm, 128), F32)],
        compiler_params=_params(1),
    )(dy1, w_out_g, cat)


def attn_bwd(qkv, dattn, lse, delta, d):
    L = qkv.shape[2]
    nb = L // ATT_BLOCK
    cpb = _classes_per_step(d, nb)

    def body(q_ref, k_ref, v_ref, do_ref, l_ref, dl_ref, out_ref, kpad, vpad, dkpad, dvpad):
        for cls in range(cpb):
            kpad[cls, 0:ATT_BLOCK, :] = jnp.zeros((ATT_BLOCK, GROUP_W), BF16)
            vpad[cls, 0:ATT_BLOCK, :] = jnp.zeros((ATT_BLOCK, GROUP_W), BF16)
            kpad[cls, ATT_BLOCK:, :] = k_ref[cls]
            vpad[cls, ATT_BLOCK:, :] = v_ref[cls]
        dkpad[...] = jnp.zeros_like(dkpad)
        dvpad[...] = jnp.zeros_like(dvpad)
        band, col, lo = _attn_masks()

        def step(t, carry):
            cls, n = t // nb, t % nb
            r0 = pl.multiple_of(n * ATT_BLOCK, ATT_BLOCK)
            valid = band & ((col >= ATT_BLOCK) | (n > 0))
            qb = q_ref[cls, pl.ds(r0, ATT_BLOCK), :]
            dob = do_ref[cls, pl.ds(r0, ATT_BLOCK), :]
            lb = l_ref[cls, pl.ds(r0, ATT_BLOCK), :]
            dlb = dl_ref[cls, pl.ds(r0, ATT_BLOCK), :]
            kb = kpad[cls, pl.ds(r0, 2 * ATT_BLOCK), :]
            vb = vpad[cls, pl.ds(r0, 2 * ATT_BLOCK), :]
            for pair in range(2):
                lanes = slice(pair * 128, (pair + 1) * 128)
                qp, dop, kp, vp = qb[:, lanes], dob[:, lanes], kb[:, lanes], vb[:, lanes]
                c0, c1 = pair * 128, pair * 128 + HEAD_DIM
                q2, do2 = _stack_heads(qp, lo), _stack_heads(dop, lo)
                lse2 = jnp.concatenate([lb[:, c0:c0 + 1], lb[:, c1:c1 + 1]], axis=0)
                dl2 = jnp.concatenate([dlb[:, c0:c0 + 1], dlb[:, c1:c1 + 1]], axis=0)
                s = lax.dot_general(q2, kp, NT, preferred_element_type=F32)
                s = jnp.where(valid, s, NEG)
                p = jnp.exp(s - lse2)
                dp = lax.dot_general(do2, vp, NT, preferred_element_type=F32)
                ds = (p * (dp - dl2)).astype(BF16)
                dq2 = jnp.dot(ds, kp, preferred_element_type=F32)
                out_ref[0, cls, pl.ds(r0, ATT_BLOCK), lanes] = _unstack_heads(dq2, lo)
                dkpad[cls, pl.ds(r0, 2 * ATT_BLOCK), lanes] += lax.dot_general(ds, q2, TN, preferred_element_type=F32)
                dvpad[cls, pl.ds(r0, 2 * ATT_BLOCK), lanes] += lax.dot_general(p.astype(BF16), do2, TN, preferred_element_type=F32)
            return carry

        lax.fori_loop(0, cpb * nb, step, 0, unroll=8)
        for cls in range(cpb):
            out_ref[1, cls] = dkpad[cls, ATT_BLOCK:, :]
            out_ref[2, cls] = dvpad[cls, ATT_BLOCK:, :]

    spec = lambda kind: pl.BlockSpec((None, cpb, L, GROUP_W), lambda r: (kind, r, 0, 0))
    per_cls = pl.BlockSpec((cpb, L, GROUP_W), lambda r: (r, 0, 0))
    return pl.pallas_call(
        body,
        name=f"attn_bwd_d{d}",
        grid=(d // cpb,),
        in_specs=[spec(0), spec(1), spec(2), per_cls, per_cls, per_cls],
        out_specs=pl.BlockSpec((3, cpb, L, GROUP_W), lambda r: (0, r, 0, 0)),
        out_shape=jax.ShapeDtypeStruct((3, d, L, GROUP_W), F32),
        scratch_shapes=[pltpu.VMEM((cpb, L + ATT_BLOCK, GROUP_W), BF16)] * 2 + [pltpu.VMEM((cpb, L + ATT_BLOCK, GROUP_W), F32)] * 2,
        compiler_params=_params(1),
    )(qkv, qkv, qkv, dattn, lse, delta)


def pool_bwd(dpool, mixed, wbd, b, scale):
    S = dpool.shape[0]

    def body(dp_ref, mx_ref, w_ref, b_ref, s_ref, du_ref, dw_ref, db_ref, ds_ref):
        dp = dp_ref[...]
        mb = mx_ref[...]
        wv = w_ref[...]
        ypre = jnp.dot(mb, wv, preferred_element_type=F32) + b_ref[...]
        ds_ref[...] = _colsum(dp * ypre)
        dpre = dp * s_ref[...]
        db_ref[...] = _colsum(dpre)
        dpb = dpre.astype(BF16)
        dw_ref[...] = lax.dot_general(mb, dpb, TN, preferred_element_type=F32)
        dmix = lax.dot_general(dpb, wv, NT, preferred_element_type=F32)
        row = lax.broadcasted_iota(jnp.int32, dmix.shape, 0)
        lane, win = _pool_lane_windows(dmix.shape)
        e = dmix / jnp.minimum(row + 1, win).astype(F32)

        def shift(a, k):
            return jnp.where(row < S - k, pltpu.roll(a, S - k, 0), 0.0)

        f2 = e + shift(e, 1)
        f4 = f2 + shift(f2, 2)
        f8 = f4 + shift(f4, 4)
        f16 = f8 + shift(f8, 8)
        du_ref[...] = jnp.where(lane < 64, f2, jnp.where(lane < 128, f4, jnp.where(lane < 192, f8, f16))) - dmix

    vm = pl.BlockSpec(memory_space=pltpu.VMEM)
    return pl.pallas_call(
        body,
        name="pool_bwd",
        in_specs=[vm] * 5,
        out_specs=[vm] * 4,
        out_shape=[jax.ShapeDtypeStruct((S, POOL_W), F32), jax.ShapeDtypeStruct((POOL_W, POOL_W), F32),
                   jax.ShapeDtypeStruct((1, POOL_W), F32), jax.ShapeDtypeStruct((1, POOL_W), F32)],
        compiler_params=_params(),
    )(dpool, mixed, wbd, b, scale)


def inproj_bwd(dqkv, du, x, dx1, w_in_g, g, mod6, tc, tsa, tsb, h1, tm=512):
    S = x.shape[0]

    def body(d0, d1, d2, du_ref, x_ref, dx1_ref, w_ref, g_ref, mod_ref, tc_ref, tsa_ref, tsb_ref, h_ref,
             gx_ref, dsh_ref, dsc_ref, dg_ref, dw_ref, s4, s16, dp_ref):
        @pl.when(pl.program_id(0) == 0)
        def _():
            _zero(dw_ref, dsh_ref, dsc_ref, dg_ref)

        cs, sa, sb = tc_ref[...], tsa_ref[...], tsb_ref[...]
        for d, src, dst in ((4, d1, s4), (16, d2, s16)):
            for kind in range(3):
                for r in range(d):
                    for h in range(2):
                        dst[kind, h, pl.ds(r, tm // d, stride=d), :] = src[kind, r, :, h * 128:(h + 1) * 128]
        for sp in range(20):
            piece, half = sp // 2, sp % 2
            lanes = slice(half * 128, (half + 1) * 128)
            if piece == 0:
                blk = du_ref[:, lanes]
            else:
                kind, gi = (piece - 1) // 3, (piece - 1) % 3
                blk = d0[kind, 0, :, lanes] if gi == 0 else (s4, s16)[gi - 1][kind, half]
                if kind == 0:
                    blk = _rope128(blk, cs, sa, sb, -1.0) * (HEAD_DIM ** -0.5)
                elif kind == 1:
                    blk = _rope128(blk, cs, sa, sb, -1.0)
            dp_ref[:, sp * 128:(sp + 1) * 128] = blk.astype(BF16)
        dh = jnp.zeros((tm, D_MODEL), F32)
        hbt = h_ref[...].T
        for j in range(N_CHIPS):
            dpj = dp_ref[:, j * 640:(j + 1) * 640]
            dh = dh + lax.dot_general(dpj, w_ref[j], NT, preferred_element_type=F32)
            dw_ref[j] += jnp.dot(hbt, dpj, preferred_element_type=F32)
        xv = x_ref[...]
        rstd = lax.rsqrt(jnp.mean(xv * xv, axis=-1, keepdims=True) + NORM_EPS)
        n1 = xv * rstd
        gv = g_ref[...]
        one_sc = 1.0 + mod_ref[1:2, :]
        dsh_ref[...] += _colsum(dh)
        dsc_ref[...] += _colsum(dh * (n1 * gv))
        dg_ref[...] += _colsum(dh * one_sc * n1)
        dn = dh * (gv * one_sc)
        gx_ref[...] = dx1_ref[...] + rstd * (dn - n1 * jnp.mean(dn * n1, axis=-1, keepdims=True))

    vec = _full((1, D_MODEL))
    dspec = lambda d: pl.BlockSpec((3, d, tm // d, GROUP_W), lambda i: (0, 0, i, 0))
    return pl.pallas_call(
        body,
        name="inproj_bwd",
        grid=(S // tm,),
        in_specs=[dspec(d) for d in DILATIONS] + [_rows(tm, POOL_W), _rows(tm, D_MODEL), _rows(tm, D_MODEL), _full(w_in_g.shape),
                                                  vec, _full((6, D_MODEL)), _rows(tm, 128), _rows(tm, 128), _rows(tm, 128),
                                                  _rows(tm, D_MODEL)],
        out_specs=[_rows(tm, D_MODEL), vec, vec, vec, _full(w_in_g.shape)],
        out_shape=[jax.ShapeDtypeStruct((S, D_MODEL), F32)] + [jax.ShapeDtypeStruct((1, D_MODEL), F32)] * 3
        + [jax.ShapeDtypeStruct(w_in_g.shape, F32)],
        scratch_shapes=[pltpu.VMEM((3, 2, tm, 128), F32)] * 2 + [pltpu.VMEM((tm, IN_W), BF16)],
        compiler_params=_params(1),
    )(*dqkv, du, x, dx1, w_in_g, g, mod6, tc, tsa, tsb, h1)


def _adamw(w, g, m, v):
    m = ADAM_B1 * m + (1.0 - ADAM_B1) * g
    v = ADAM_B2 * v + (1.0 - ADAM_B2) * (g * g)
    m_hat = m / (1.0 - ADAM_B1 ** ADAM_STEP)
    v_hat = v / (1.0 - ADAM_B2 ** ADAM_STEP)
    delta = -ADAM_LR * (m_hat / (jnp.sqrt(v_hat) + ADAM_EPS) + ADAM_WD * w)
    return delta, m, v


def adamw_rows(w, g, m, v, tr, name):
    R, C = w.shape

    def body(w_ref, g_ref, m_ref, v_ref, go_ref, d_ref, mo_ref, vo_ref):
        g = g_ref[...]
        go_ref[...] = g
        d_ref[...], mo_ref[...], vo_ref[...] = _adamw(w_ref[...], g, m_ref[...], v_ref[...])

    spec = pl.BlockSpec((tr, C), lambda i: (i, 0))
    return pl.pallas_call(
        body,
        name=name,
        grid=(R // tr,),
        in_specs=[spec] * 4,
        out_specs=[spec] * 4,
        out_shape=[jax.ShapeDtypeStruct((R, C), F32)] * 4,
        compiler_params=_params(1),
    )(w, g, m, v)


def adamw_ada(c_all_t, dmod_cols, w, m, v, tr=256):
    R, C = w.shape

    def body(ct_ref, dm_ref, w_ref, m_ref, v_ref, g_ref, d_ref, mo_ref, vo_ref):
        ct = ct_ref[...]
        act = ct * jax.nn.sigmoid(ct)
        dm = dm_ref[...]
        a_hi, d_hi = act.astype(BF16), dm.astype(BF16)
        a_lo, d_lo = (act - a_hi.astype(F32)).astype(BF16), (dm - d_hi.astype(F32)).astype(BF16)
        g = (jnp.dot(a_hi, d_hi, preferred_element_type=F32) + jnp.dot(a_lo, d_hi, preferred_element_type=F32)
             + jnp.dot(a_hi, d_lo, preferred_element_type=F32))
        g_ref[...] = g
        d_ref[...], mo_ref[...], vo_ref[...] = _adamw(w_ref[...], g, m_ref[...], v_ref[...])

    spec = pl.BlockSpec((tr, C), lambda i: (i, 0))
    return pl.pallas_call(
        body,
        name="adamw_ada",
        grid=(R // tr,),
        in_specs=[pl.BlockSpec((tr, N_DEV), lambda i: (i, 0)), _full((N_DEV, C)), spec, spec, spec],
        out_specs=[spec] * 4,
        out_shape=[jax.ShapeDtypeStruct((R, C), F32)] * 4,
        compiler_params=_params(1),
    )(c_all_t, dmod_cols, w, m, v)


def adamw_small(slab_a, slab_b, convw_g, wpool_g, params):
    names = ["b_ada", "g_pre_mix", "g_post_mix", "g_pre_ffn", "g_post_ffn", "b_pool", "pool_scale", "conv_b", "conv_w", "w_pool"]
    flat = []
    for n in names:
        flat += list(params[n])

    def body(a_ref, b_ref, cw_ref, wp_ref, *rest):
        ins, outs = rest[:30], rest[30:]

        def dev_sum(ref):
            t = ref[0]
            for dev in range(1, N_DEV):
                t = t + ref[dev]
            return t

        sa, sb_, scw, swp = dev_sum(a_ref), dev_sum(b_ref), dev_sum(cw_ref), dev_sum(wp_ref)
        grads = [
            jnp.concatenate([sa[k:k + 1, :] for k in range(6)], axis=1),
            sa[6:7, :], sa[7:8, :], sa[8:9, :], sa[9:10, :],
            sa[10:11, 0:256], sa[10:11, 256:512],
            sb_[3:4, :], scw, swp,
        ]
        for i, g in enumerate(grads):
            w_ref, m_ref, v_ref = ins[3 * i:3 * i + 3]
            if names[i] == "b_pool":
                parts = [((0, slice(grp, grp + 1)), g[:, grp * 64:(grp + 1) * 64]) for grp in range(4)]
            elif names[i] == "w_pool":
                parts = [((0, grp), g[grp * 64:(grp + 1) * 64, :]) for grp in range(4)]
            elif names[i] == "conv_w":
                parts = [((0,), g)]
            else:
                parts = [((Ellipsis,), g)]
            for at, gp in parts:
                d, mo, vo = _adamw(w_ref[at], gp, m_ref[at], v_ref[at])
                for k, val in enumerate((gp, d, mo, vo)):
                    outs[4 * i + k][at] = val
        outs[-1][...] = sa[10:11, 512:640]

    vm = pl.BlockSpec(memory_space=pltpu.VMEM)
    out_shape = []
    for n in names:
        out_shape += [jax.ShapeDtypeStruct(params[n][0].shape, F32)] * 4
    out_shape.append(jax.ShapeDtypeStruct((1, 128), F32))
    outs = pl.pallas_call(
        body,
        name="adamw_small",
        in_specs=[vm] * (4 + len(flat)),
        out_specs=[vm] * len(out_shape),
        out_shape=out_shape,
        compiler_params=_params(),
    )(slab_a, slab_b, convw_g, wpool_g, *flat)
    return {n: outs[4 * i:4 * i + 4] for i, n in enumerate(names)}, outs[-1]


def _place():
    return lax.axis_index("x"), lax.axis_index("y"), lax.axis_index("c")


def _other_chips(x, y):
    return [(1 - x, y), (x, 1 - y), (1 - x, 1 - y)]


def _chip_id(cx, cy):
    return 2 * cx + cy


HBM_SPEC = pl.BlockSpec(memory_space=pltpu.HBM)
SEM_SPEC = pl.BlockSpec(memory_space=pltpu.SEMAPHORE)
ANY_SPEC = pl.BlockSpec(memory_space=pl.ANY)
EFFECT = pltpu.SideEffectType.DATAFLOW_SIDE_EFFECTING


def _hbm(t):
    return pltpu.with_memory_space_constraint(t, pltpu.HBM)


def _hbm_shapes(ts):
    return [pltpu.HBM(t.shape, t.dtype) for t in ts]


def _half_rows(ref, lead, half, rh):
    return ref.at[lead, pl.ds(half * rh, rh), :]


def _flips():
    return [(fx, fy, fc) for fx in (0, 1) for fy in (0, 1) for fc in (0, 1)][1:]


def _flip(v, f):
    return v if f == 0 else 1 - v


def ada_mod(c3, w_ada, b_cols, conv_w):
    CB = w_ada.shape[1]

    def body(c_ref, w_hbm, b_ref, cw_ref, call_ref, mod_ref, cwall_ref, modall, send_sems, recv_sems, w_ref, w_sem):
        x, y, c = _place()
        me_dev = 4 * x + 2 * y + c
        me = _chip_id(x, y)
        w_load = pltpu.make_async_copy(w_hbm, w_ref, w_sem)
        w_load.start()
        call_ref[me_dev] = c_ref[0]
        cwall_ref[me] = cw_ref[...]
        sends = []
        for k, (cx, cy) in enumerate(_other_chips(x, y)):
            cp = pltpu.make_async_remote_copy(src_ref=cw_ref, dst_ref=cwall_ref.at[me], send_sem=send_sems.at[10 + k],
                                              recv_sem=recv_sems.at[10 + k], device_id=(cx, cy, c), device_id_type=MESH)
            cp.start()
            sends.append(cp)
        for k, (fx, fy, fc) in enumerate(_flips()):
            cp = pltpu.make_async_remote_copy(src_ref=c_ref.at[0], dst_ref=call_ref.at[me_dev], send_sem=send_sems.at[k],
                                              recv_sem=recv_sems.at[k],
                                              device_id=(_flip(x, fx), _flip(y, fy), _flip(c, fc)), device_id_type=MESH)
            cp.start()
            sends.append(cp)
        for k, (fx, fy, fc) in enumerate(_flips()):
            peer = 4 * _flip(x, fx) + 2 * _flip(y, fy) + _flip(c, fc)
            pltpu.make_async_remote_copy(src_ref=c_ref.at[0], dst_ref=call_ref.at[peer], send_sem=send_sems.at[k],
                                         recv_sem=recv_sems.at[k], device_id=(x, y, c), device_id_type=MESH).wait_recv()
        row = lax.broadcasted_iota(jnp.int32, (N_DEV, D_MODEL), 0)
        call = jnp.zeros((N_DEV, D_MODEL), F32)
        for dev in range(N_DEV):
            call = jnp.where(row == dev, call_ref[dev], call)
        act = call * jax.nn.sigmoid(call)
        w_load.wait()
        wv = w_ref[...]
        w_hi = wv.astype(BF16)
        w_lo = (wv - w_hi.astype(F32)).astype(BF16)
        a_hi = act.astype(BF16)
        a_lo = (act - a_hi.astype(F32)).astype(BF16)
        prod = (jnp.dot(a_hi, w_hi, preferred_element_type=F32) + jnp.dot(a_lo, w_hi, preferred_element_type=F32)
                + jnp.dot(a_hi, w_lo, preferred_element_type=F32))
        modall[me] = prod + b_ref[...]
        for k, (cx, cy) in enumerate(_other_chips(x, y)):
            cp = pltpu.make_async_remote_copy(src_ref=modall.at[me], dst_ref=modall.at[me], send_sem=send_sems.at[7 + k],
                                              recv_sem=recv_sems.at[7 + k], device_id=(cx, cy, c), device_id_type=MESH)
            cp.start()
            sends.append(cp)
        for k, (cx, cy) in enumerate(_other_chips(x, y)):
            blk = modall.at[_chip_id(cx, cy)]
            pltpu.make_async_remote_copy(src_ref=blk, dst_ref=blk, send_sem=send_sems.at[7 + k], recv_sem=recv_sems.at[7 + k],
                                         device_id=(x, y, c), device_id_type=MESH).wait_recv()
        for k, (cx, cy) in enumerate(_other_chips(x, y)):
            blk = cwall_ref.at[_chip_id(cx, cy)]
            pltpu.make_async_remote_copy(src_ref=blk, dst_ref=blk, send_sem=send_sems.at[10 + k], recv_sem=recv_sems.at[10 + k],
                                         device_id=(x, y, c), device_id_type=MESH).wait_recv()
        for cp in sends:
            cp.wait_send()
        mine = [modall[j, pl.ds(me_dev, 1), :] for j in range(N_CHIPS)]
        for r in range(6):
            pieces = []
            for h in range(2):
                pos = r * D_MODEL + h * 512
                pieces.append(mine[pos // CB][:, pos % CB:pos % CB + 512])
            mod_ref[r:r + 1, :] = jnp.concatenate(pieces, axis=1)

    vm = pl.BlockSpec(memory_space=pltpu.VMEM)
    return pl.pallas_call(
        body,
        name="ada_mod",
        in_specs=[vm, ANY_SPEC, vm, vm],
        out_specs=[vm] * 3,
        out_shape=[jax.ShapeDtypeStruct((N_DEV, 1, D_MODEL), F32), jax.ShapeDtypeStruct((6, D_MODEL), F32),
                   jax.ShapeDtypeStruct((N_CHIPS,) + conv_w.shape, F32)],
        scratch_shapes=[pltpu.VMEM((N_CHIPS, N_DEV, CB), F32), pltpu.SemaphoreType.DMA((13,)), pltpu.SemaphoreType.DMA((13,)),
                        pltpu.VMEM(w_ada.shape, F32), pltpu.SemaphoreType.DMA],
        compiler_params=pltpu.CompilerParams(has_side_effects=True, vmem_limit_bytes=VMEM_LIMIT),
    )(c3, w_ada, b_cols, conv_w)


def split_start(name, bufs, plan, n_sem, carry):
    nb = len(bufs)
    many = isinstance(carry, (list, tuple))
    alls = list(bufs) + (list(carry) if many else [carry])
    na = len(alls)

    def body(*refs):
        x, y, c = _place()
        ssem, rsem = refs[na], refs[na + 1]
        for i, (src, dst, dev) in enumerate(plan(refs[:nb], x, y, c)):
            pltpu.make_async_remote_copy(src_ref=src, dst_ref=dst, send_sem=ssem.at[i], recv_sem=rsem.at[i], device_id=dev,
                                         device_id_type=MESH).start()

    outs = pl.pallas_call(
        body,
        name=name,
        out_shape=[pltpu.SemaphoreType.DMA((n_sem,)), pltpu.SemaphoreType.DMA((n_sem,))] + _hbm_shapes(alls),
        in_specs=[HBM_SPEC] * na,
        out_specs=[SEM_SPEC, SEM_SPEC] + [HBM_SPEC] * na,
        input_output_aliases={i: 2 + i for i in range(na)},
        compiler_params=pltpu.CompilerParams(has_side_effects=EFFECT),
    )(*[_hbm(t) for t in alls])
    return outs[0], outs[1], list(outs[2:2 + nb]), (list(outs[2 + nb:]) if many else outs[-1])


def split_wait(name, ssem, rsem, bufs, plan, after):
    nb = len(bufs)

    def body(*refs):
        x, y, c = _place()
        s_ref, r_ref = refs[nb], refs[nb + 1]
        for i, (src, dst, dev) in enumerate(plan(refs[:nb], x, y, c)):
            cp = pltpu.make_async_remote_copy(src_ref=src, dst_ref=dst, send_sem=s_ref.at[i], recv_sem=r_ref.at[i], device_id=dev,
                                              device_id_type=MESH)
            cp.wait_send()
            cp.wait_recv()

    outs = pl.pallas_call(
        body,
        name=name,
        out_shape=_hbm_shapes(bufs),
        in_specs=[HBM_SPEC] * nb + [SEM_SPEC, SEM_SPEC, ANY_SPEC],
        out_specs=[HBM_SPEC] * nb,
        input_output_aliases={i: i for i in range(nb)},
        compiler_params=pltpu.CompilerParams(has_side_effects=EFFECT),
    )(*bufs, ssem, rsem, after)
    return list(outs)


def _gather_ici_plan(n):
    def plan(refs, x, y, c):
        out = []
        for w in range(n):
            rh = refs[w].shape[0] // 2
            for cx, cy in _other_chips(x, y):
                out.append((refs[w].at[pl.ds(c * rh, rh), :], _half_rows(refs[n + w], _chip_id(x, y), c, rh), (cx, cy, c)))
        return out

    return plan


def _gather_d2d_plan(n):
    def plan(refs, x, y, c):
        out = []
        for w in range(n):
            rh = refs[w].shape[1] // 2
            for cx, cy in _other_chips(x, y):
                blk = _half_rows(refs[w], _chip_id(cx, cy), c, rh)
                out.append((blk, blk, (x, y, 1 - c)))
        return out

    return plan


def _dev_id(x, y, c):
    return 4 * x + 2 * y + c


def _small_ici_plan(n):
    def plan(refs, x, y, c):
        out = []
        for w in range(n):
            dst = refs[n + w].at[_dev_id(x, y, c)]
            out.append((refs[w], dst, (x, y, 1 - c)))
            for cx, cy in _other_chips(x, y):
                out.append((refs[w], dst, (cx, cy, c)))
        return out

    return plan


def _small_d2d_plan(n):
    def plan(refs, x, y, c):
        out = []
        for w in range(n):
            for cx, cy in _other_chips(x, y):
                blk = refs[w].at[_dev_id(cx, cy, c)]
                out.append((blk, blk, (x, y, 1 - c)))
        return out

    return plan


def _rs_d2d_plan(n):
    def plan(refs, x, y, c):
        out = []
        for w in range(n):
            rh = refs[w].shape[1] // 2
            out.append((refs[w].at[:, pl.ds((1 - c) * rh, rh), :], refs[n + w], (x, y, 1 - c)))
        return out

    return plan


def _rs_ici_plan(n):
    def plan(refs, x, y, c):
        out = []
        for w in range(n):
            for k, (cx, cy) in enumerate(_other_chips(x, y)):
                out.append((refs[w].at[_chip_id(cx, cy)], refs[n + w].at[k], (cx, cy, c)))
        return out

    return plan


def _rs_share_plan(n):
    def plan(refs, x, y, c):
        out = []
        for w in range(n):
            rh = refs[w].shape[0] // 2
            rows = refs[w].at[pl.ds(c * rh, rh), :]
            out.append((rows, rows, (x, y, 1 - c)))
        return out

    return plan


def rs_add(grad, sibbuf, place, tr, name):
    _, R, C = grad.shape
    nt = (R // 2) // tr

    def body(p_ref, g_ref, s_ref, o_ref):
        o_ref[...] = (g_ref[...] + s_ref[...]).astype(BF16)

    return pl.pallas_call(
        body,
        name=name,
        grid_spec=pltpu.PrefetchScalarGridSpec(
            num_scalar_prefetch=1,
            grid=(N_CHIPS, nt),
            in_specs=[pl.BlockSpec((None, tr, C), lambda j, i, p: (j, p[0] * nt + i, 0)),
                      pl.BlockSpec((None, tr, C), lambda j, i, p: (j, i, 0))],
            out_specs=pl.BlockSpec((None, tr, C), lambda j, i, p: (j, i, 0)),
        ),
        out_shape=jax.ShapeDtypeStruct((N_CHIPS, R // 2, C), BF16),
        compiler_params=_params(2),
    )(place, grad, sibbuf)


def rs_final(grad, sibbuf, rbuf, place, tr, name):
    _, R, C = grad.shape
    nt = (R // 2) // tr

    def body(p_ref, g_ref, s_ref, r_ref, o_ref):
        o_ref[...] = (((g_ref[...] + s_ref[...]) + r_ref[0].astype(F32)) + r_ref[1].astype(F32)) + r_ref[2].astype(F32)

    return pl.pallas_call(
        body,
        name=name,
        grid_spec=pltpu.PrefetchScalarGridSpec(
            num_scalar_prefetch=1,
            grid=(nt,),
            in_specs=[pl.BlockSpec((None, tr, C), lambda i, p: (p[1], p[0] * nt + i, 0)),
                      pl.BlockSpec((None, tr, C), lambda i, p: (p[1], i, 0)),
                      pl.BlockSpec((3, tr, C), lambda i, p: (0, i, 0))],
            out_specs=pl.BlockSpec((tr, C), lambda i, p: (p[0] * nt + i, 0)),
        ),
        out_shape=jax.ShapeDtypeStruct((R, C), F32),
        compiler_params=_params(1),
    )(place, grad, sibbuf, rbuf)


class GradReduce:
    def __init__(self, tag, grads, rows, place):
        self.tag, self.grads, self.rows, self.place = tag, grads, rows, place
        self.n = len(grads)

    def d2d_start(self, carry):
        sib = [lax.empty((N_CHIPS, g.shape[1] // 2, g.shape[2]), F32) for g in self.grads]
        self.s1, self.r1, bufs, carry = split_start(f"rs_{self.tag}_d2d_start", self.grads + sib, _rs_d2d_plan(self.n), self.n, carry)
        self.bufs1 = bufs
        return carry

    def add_and_ici_start(self, after, carry):
        bufs = split_wait(f"rs_{self.tag}_d2d_wait", self.s1, self.r1, self.bufs1, _rs_d2d_plan(self.n), after)
        self.grads, self.sib = bufs[:self.n], bufs[self.n:]
        pb = [rs_add(g, s, self.place, tr, f"rs_{self.tag}_add{w}")
              for w, (g, s, tr) in enumerate(zip(self.grads, self.sib, self.rows))]
        rb = [lax.empty((3,) + p.shape[1:], BF16) for p in pb]
        self.s2, self.r2, self.bufs2, carry = split_start(f"rs_{self.tag}_ici_start", pb + rb, _rs_ici_plan(self.n), 3 * self.n, carry)
        return carry

    def final_and_share_start(self, after, carry):
        bufs = split_wait(f"rs_{self.tag}_ici_wait", self.s2, self.r2, self.bufs2, _rs_ici_plan(self.n), after)
        rb = bufs[self.n:]
        full = [rs_final(g, s, r, self.place, tr, f"rs_{self.tag}_final{w}")
                for w, (g, s, r, tr) in enumerate(zip(self.grads, self.sib, rb, self.rows))]
        self.s3, self.r3, self.bufs3, carry = split_start(f"rs_{self.tag}_share_start", full, _rs_share_plan(self.n), self.n, carry)
        return carry

    def finish(self, after):
        return split_wait(f"rs_{self.tag}_share_wait", self.s3, self.r3, self.bufs3, _rs_share_plan(self.n), after)


def _rope_tables(positions):
    inv_freq = ROPE_THETA ** (-jnp.arange(0, ROT_DIM, 2, dtype=F32) / ROT_DIM)
    ang = positions.astype(F32)[:, None] * inv_freq
    cos, sin = jnp.cos(ang), jnp.sin(ang)
    S = positions.shape[0]
    one, zero = jnp.ones((S, 48), F32), jnp.zeros((S, 48), F32)
    z8 = jnp.zeros((S, 8), F32)
    tc = jnp.concatenate([cos, cos, one], axis=1)
    tsa = jnp.concatenate([z8, sin, zero], axis=1)
    tsb = jnp.concatenate([-sin, z8, zero], axis=1)
    return tuple(jnp.tile(t, (1, 2)) for t in (tc, tsa, tsb))


def _block_diag(w_pool):
    wbd = jnp.zeros((POOL_W, POOL_W), F32)
    for gi in range(4):
        wbd = wbd.at[gi * 64:(gi + 1) * 64, gi * 64:(gi + 1) * 64].set(w_pool[gi])
    return wbd


def kernel(x, c, positions, w_ada, b_ada, g_pre_mix, g_post_mix, g_pre_ffn, g_post_ffn, w_in, w_pool, b_pool, pool_scale, w_out, w_up, conv_w, conv_b, w_down, loss_target, m_w_ada, m_b_ada, m_g_pre_mix, m_g_post_mix, m_g_pre_ffn, m_g_post_ffn, m_w_in, m_w_pool, m_b_pool, m_pool_scale, m_w_out, m_w_up, m_conv_w, m_conv_b, m_w_down, v_w_ada, v_b_ada, v_g_pre_mix, v_g_post_mix, v_g_pre_ffn, v_g_post_ffn, v_w_in, v_w_pool, v_b_pool, v_pool_scale, v_w_out, v_w_up, v_conv_w, v_conv_b, v_w_down):
    xi, yi, ci = lax.axis_index("x"), lax.axis_index("y"), lax.axis_index("c")
    chip = 2 * xi + yi
    place = jnp.stack([ci, chip]).astype(jnp.int32)
    x2, tgt = x[0], loss_target[0]
    S = x2.shape[0]

    def landing(s_):
        return lax.dynamic_update_slice(lax.empty((N_CHIPS,) + s_.shape, s_.dtype), s_[None], (chip, 0, 0))

    cb_ada = w_ada.shape[2]
    b_cols = lax.dynamic_slice(b_ada, (0, chip * cb_ada), (1, cb_ada))
    c_all, mod6, conv_w_g = ada_mod(c.reshape(1, 1, D_MODEL), w_ada[0], b_cols, conv_w[0])
    conv_w_f = jnp.transpose(conv_w_g, (1, 0, 2)).reshape(3, D_FF)
    mix_sh = [w_in[0].astype(BF16), w_out[0].astype(BF16)]
    ffn_sh = [w_up[0].astype(BF16), w_down[0].astype(BF16)]
    ga_s, ga_r, ga_bufs, mod6 = split_start("gather_mix_ici_start", mix_sh + [landing(t) for t in mix_sh], _gather_ici_plan(2), 6, mod6)
    gb_s, gb_r, gb_bufs, (mod6, tc, tsa, tsb) = split_start("gather_ffn_ici_start", ffn_sh + [landing(t) for t in ffn_sh],
                                                            _gather_ici_plan(2), 6, [mod6, *_rope_tables(positions[0])])
    wbd = _block_diag(w_pool[0]).astype(BF16)
    b_pool2, scale2 = b_pool.reshape(1, POOL_W), pool_scale
    ga_bufs = split_wait("gather_mix_ici_wait", ga_s, ga_r, ga_bufs, _gather_ici_plan(2), mod6)
    gc_s, gc_r, mix_land, mod6 = split_start("gather_mix_d2d_start", ga_bufs[2:], _gather_d2d_plan(2), 6, mod6)
    w_in_g, w_out_g = split_wait("gather_mix_d2d_wait", gc_s, gc_r, mix_land, _gather_d2d_plan(2), mod6)

    h1, u, *qkv = inproj_fwd(x2, g_pre_mix, mod6, w_in_g, tc, tsa, tsb)
    mixed, pool = pool_fwd(u, wbd, b_pool2, scale2)
    o_l = [attn_fwd(t, d) for t, d in zip(qkv, DILATIONS)]
    attn_done = sum(l[0, :8, :128] for _, l in o_l)
    gb_bufs = split_wait("gather_ffn_ici_wait", gb_s, gb_r, gb_bufs, _gather_ici_plan(2), attn_done)
    gd_s, gd_r, ffn_land, pool = split_start("gather_ffn_d2d_start", gb_bufs[2:], _gather_d2d_plan(2), 6, pool)
    cat, lse, lse4, lse16, y1, x1, h2 = outproj_fwd([o for o, _ in o_l] + [l for _, l in o_l], pool, x2, w_out_g, g_post_mix,
                                                    g_pre_ffn, mod6)
    lses = [lse[None], lse4, lse16]
    w_up_g, w_down_g = split_wait("gather_ffn_d2d_wait", gd_s, gd_r, ffn_land, _gather_d2d_plan(2), h2)
    w_down_f = w_down_g.reshape(D_FF, D_MODEL)
    gate, val, dy2, dout, loss_v, d_gt_f, d_g_post_ffn = ffn_fwd(h2, w_up_g, conv_w_f, conv_b, w_down_f, x1, tgt, g_post_ffn, mod6)

    dgc, dval, d_conv_w, d_conv_b, dw_down, dw_up = down_bwd(dy2, w_down_f, gate, val, conv_w_f, conv_b, h2)
    dx1, dy1, d_sh_f, d_sc_f, d_g_pre_ffn, d_gt_m, d_g_post_mix, dw_up = up_bwd(
        dgc, dval, conv_w_f, w_up_g, x1, dout, y1, g_pre_ffn, g_post_mix, mod6, h2, dw_up)
    rs_ffn = GradReduce("ffn", [dw_up, dw_down.reshape(N_CHIPS, D_FF // N_CHIPS, D_MODEL)], [256, 176], place)
    dy1 = rs_ffn.d2d_start(dy1)
    dpool, da1, da4, da16, dl1, dl4, dl16, dw_out = outproj_bwd(dy1, w_out_g, cat)
    dpool = rs_ffn.add_and_ici_start(dw_out, dpool)
    du, d_wbd, d_b_pool, d_scale = pool_bwd(dpool, mixed, wbd, b_pool2, scale2)
    dqkv = [attn_bwd(t, da, ls, dl, d) for t, da, ls, dl, d in zip(qkv, (da1[None], da4, da16), lses, (dl1[None], dl4, dl16), DILATIONS)]
    grad_x, d_sh_m, d_sc_m, d_g_pre_mix, dw_in = inproj_bwd(dqkv, du, x2, dx1, w_in_g, g_pre_mix, mod6, tc, tsa, tsb, h1)

    z1 = jnp.zeros((1, D_MODEL), F32)
    slab_a = jnp.concatenate(
        [d_sh_m, d_sc_m, d_gt_m, d_sh_f, d_sc_f, d_gt_f, d_g_pre_mix, d_g_post_mix, d_g_pre_ffn, d_g_post_ffn,
         jnp.concatenate([d_b_pool, d_scale, loss_v, jnp.zeros((1, 384), F32)], axis=1)] + [z1] * 5, axis=0)
    slab_b = jnp.concatenate([d_conv_w, d_conv_b, jnp.zeros((4, D_FF), F32)], axis=0)
    d_wpool = jnp.concatenate([d_wbd[gi * 64:(gi + 1) * 64, gi * 64:(gi + 1) * 64] for gi in range(4)], axis=0)
    dev = _dev_id(xi, yi, ci)
    small_src = [slab_a, slab_b, d_wpool]
    small_land = [lax.dynamic_update_slice(lax.empty((N_DEV,) + t.shape, F32), t[None], (dev, 0, 0)) for t in small_src]
    tok = jnp.zeros((8, 128), F32)
    gs_s, gs_r, gs_bufs, tok = split_start("small_ici_start", small_src + small_land, _small_ici_plan(3), 12, tok)
    rs_mix = GradReduce("mix", [dw_in, dw_out], [256, 256], place)
    tok = rs_mix.d2d_start(tok)
    tok = rs_ffn.final_and_share_start(tok, tok)
    gs_bufs = split_wait("small_ici_wait", gs_s, gs_r, gs_bufs, _small_ici_plan(3), tok)
    gt_s, gt_r, small_land, tok = split_start("small_d2d_start", gs_bufs[3:], _small_d2d_plan(3), 9, tok)
    tok = rs_mix.add_and_ici_start(tok, tok)
    slab_a_g, slab_b_g, wpool_g = split_wait("small_d2d_wait", gt_s, gt_r, small_land, _small_d2d_plan(3), tok)
    cw_cols = conv_w.shape[2]
    convw_g = lax.dynamic_slice(slab_b_g, (0, 0, chip * cw_cols), (N_DEV, 3, cw_cols))
    dmod_cols = lax.dynamic_slice(slab_a_g[:, :6, :].reshape(N_DEV, 6 * D_MODEL), (0, chip * cb_ada), (N_DEV, cb_ada))

    res = {}

    def big_adamw(name, w, g, m, v, tr):
        g_, d_, m_, v_ = adamw_rows(w[0], g, m[0], v[0], tr, "adamw_" + name)
        res[name] = (g_[None], d_[None], m_[None], v_[None])
        return v_

    g_ada, d_ada, m_ada, v_ada = adamw_ada(c_all.reshape(N_DEV, D_MODEL).T, dmod_cols, w_ada[0], m_w_ada[0], v_w_ada[0])
    res["w_ada"] = (g_ada[None], d_ada[None], m_ada[None], v_ada[None])
    g_w_up, g_w_down = rs_ffn.finish(v_ada)
    big_adamw("w_up", w_up, g_w_up, m_w_up, v_w_up, 256)
    last = big_adamw("w_down", w_down, g_w_down, m_w_down, v_w_down, 352)
    rs_mix.final_and_share_start(last, jnp.zeros((8, 128), F32))
    g_w_in, g_w_out = rs_mix.finish(last)
    big_adamw("w_in", w_in, g_w_in, m_w_in, v_w_in, 256)
    big_adamw("w_out", w_out, g_w_out, m_w_out, v_w_out, 256)
    small, loss_sum = adamw_small(slab_a_g, slab_b_g, convw_g, wpool_g, {
        "b_ada": (b_ada, m_b_ada, v_b_ada), "g_pre_mix": (g_pre_mix, m_g_pre_mix, v_g_pre_mix),
        "g_post_mix": (g_post_mix, m_g_post_mix, v_g_post_mix), "g_pre_ffn": (g_pre_ffn, m_g_pre_ffn, v_g_pre_ffn),
        "g_post_ffn": (g_post_ffn, m_g_post_ffn, v_g_post_ffn), "b_pool": (b_pool, m_b_pool, v_b_pool),
        "pool_scale": (pool_scale, m_pool_scale, v_pool_scale), "conv_b": (conv_b, m_conv_b, v_conv_b),
        "conv_w": (conv_w, m_conv_w, v_conv_w), "w_pool": (w_pool, m_w_pool, v_w_pool)})
    for name in ("b_ada", "g_pre_mix", "g_post_mix", "g_pre_ffn", "g_post_ffn", "pool_scale", "conv_b", "b_pool", "w_pool", "conv_w"):
        res[name] = tuple(small[name])

    loss = loss_sum[0, 0]
    order = ["w_ada", "b_ada", "g_pre_mix", "g_post_mix", "g_pre_ffn", "g_post_ffn", "w_in", "w_pool", "b_pool", "pool_scale",
             "w_out", "w_up", "conv_w", "conv_b", "w_down"]
    outs = [loss, grad_x[None]]
    for k in range(4):
        outs += [res[n][k] for n in order]
    return tuple(outs)
```
